```python
import jax, jax.numpy as jnp
from jax import lax
import numpy as np

D_MODEL = 1024
BATCH = 8
SEQ = 8192
DEPTH = 2

CHUNK = 64
Q_BLOCK = 128
FOX_HEADS = 8
FOX_HEAD_DIM = 64
FOX_WIDTH = FOX_HEADS * FOX_HEAD_DIM
CONV_WIDTH = 512
CONV_K = 3
EVEN_IN = 3 * FOX_WIDTH + FOX_HEADS + 3 * CONV_WIDTH
EVEN_MIX = FOX_WIDTH + CONV_WIDTH
GMLP_BLOCK = 128
GMLP_GROUPS = 8
GMLP_WIDTH = D_MODEL
GMLP_GROUP_DIM = GMLP_WIDTH // GMLP_GROUPS
FFN_HIDDEN = -(-8 * D_MODEL // (3 * 256)) * 256
ALPHA = (2.0 * DEPTH) ** 0.25
BETA = (8.0 * DEPTH) ** -0.25
N_EVEN = (DEPTH + 1) // 2
N_ODD = DEPTH // 2
LN_EPS = 1e-5

kernel_name = "fox_shortconv_gmlp_deepnorm_trunk"


def layer_norm(x, g, b):
    xf = x.astype(jnp.float32)
    mu = jnp.mean(xf, axis=-1, keepdims=True)
    var = jnp.mean(jnp.square(xf - mu), axis=-1, keepdims=True)
    return ((xf - mu) * lax.rsqrt(var + LN_EPS) * g + b).astype(x.dtype)


def forgetting_attention(q, k, v, log_f):
    bsz, s_len, h, dh = q.shape
    nb = s_len // Q_BLOCK
    c = jnp.cumsum(log_f, axis=1).transpose(0, 2, 1)
    kh = k.transpose(0, 2, 1, 3)
    vh = v.transpose(0, 2, 1, 3)
    qb = q.reshape(bsz, nb, Q_BLOCK, h, dh).transpose(1, 0, 3, 2, 4)
    cb = c.reshape(bsz, h, nb, Q_BLOCK).transpose(2, 0, 1, 3)
    pos = jnp.arange(s_len)
    posb = pos.reshape(nb, Q_BLOCK)
    scale = dh ** -0.5

    def block(args):
        q_blk, c_blk, p_blk = args
        s = jnp.einsum('bhqd,bhkd->bhqk', q_blk, kh,
                       preferred_element_type=jnp.float32) * scale
        s = s + c_blk[..., :, None] - c[..., None, :]
        s = jnp.where(p_blk[:, None] >= pos[None, :], s, -jnp.inf)
        p = jax.nn.softmax(s, axis=-1)
        return jnp.einsum('bhqk,bhkd->bhqd', p.astype(vh.dtype), vh)

    o = lax.map(block, (qb, cb, posb))
    return o.transpose(1, 0, 3, 2, 4).reshape(bsz, s_len, h * dh)


def short_conv_mixer(h, b_gate, c_gate, conv_w):
    s_len = h.shape[1]
    z = c_gate * h
    zp = jnp.pad(z, ((0, 0), (CONV_K - 1, 0), (0, 0)))
    y = conv_w[0] * zp[:, 0:s_len]
    for i in range(1, CONV_K):
        y = y + conv_w[i] * zp[:, i:i + s_len]
    return b_gate * y


def fox_conv_mixer(x, w_in, b_f, conv_w, w_out):
    bsz, s_len, _ = x.shape
    proj = x @ w_in
    cuts = np.cumsum([FOX_WIDTH, FOX_WIDTH, FOX_WIDTH, FOX_HEADS, CONV_WIDTH, CONV_WIDTH]).tolist()
    q, k, v, f_logit, b_gate, c_gate, h = jnp.split(proj, cuts, axis=-1)
    log_f = jax.nn.log_sigmoid((f_logit + b_f).astype(jnp.float32))
    heads = (bsz, s_len, FOX_HEADS, FOX_HEAD_DIM)
    attn = forgetting_attention(q.reshape(heads), k.reshape(heads), v.reshape(heads), log_f)
    conv = short_conv_mixer(h, b_gate, c_gate, conv_w)
    return jnp.concatenate([attn.astype(x.dtype), conv], axis=-1) @ w_out


def gmlp_mixer(x, w_in, v_ln_g, v_ln_b, w_s, b_s, w_out):
    bsz, s_len, _ = x.shape
    uv = jax.nn.gelu(x @ w_in, approximate=False)
    u, v = jnp.split(uv, 2, axis=-1)
    v = layer_norm(v, v_ln_g, v_ln_b)
    nc = s_len // GMLP_BLOCK
    vb = v.reshape(bsz, nc, GMLP_BLOCK, GMLP_GROUPS, GMLP_GROUP_DIM)
    chunk_id = jnp.arange(GMLP_BLOCK) // CHUNK
    mask = chunk_id[None, :] <= chunk_id[:, None]
    w = jnp.where(mask[None], w_s, jnp.zeros((), w_s.dtype))
    s = jnp.einsum('gij,bcjgd->bcigd', w, vb) + b_s.T[None, None, :, :, None]
    return (u * s.reshape(bsz, s_len, GMLP_WIDTH)) @ w_out


def swiglu(x, w_in, w_out):
    gate, up = jnp.split(x @ w_in, 2, axis=-1)
    return (jax.nn.silu(gate) * up) @ w_out


def _fwd_setup_inputs(seed: int = 0) -> dict:
    key = jax.random.key(seed)
    ks = jax.random.split(key, 20)
    nrm = jax.random.normal
    f32 = jnp.float32
    return {
        "x": nrm(ks[0], (BATCH, SEQ, D_MODEL), f32),
        "even_w_in": nrm(ks[1], (N_EVEN, D_MODEL, EVEN_IN), f32) * D_MODEL ** -0.5,
        "even_b_f": jax.random.uniform(ks[2], (N_EVEN, FOX_HEADS), f32, 1.0, 5.0),
        "even_conv_w": nrm(ks[3], (N_EVEN, CONV_K, CONV_WIDTH), f32) * CONV_K ** -0.5,
        "even_w_out": nrm(ks[4], (N_EVEN, EVEN_MIX, D_MODEL), f32) * (EVEN_MIX ** -0.5 * BETA),
        "odd_w_in": nrm(ks[5], (N_ODD, D_MODEL, 2 * GMLP_WIDTH), f32) * D_MODEL ** -0.5,
        "odd_v_ln_g": 1.0 + 0.1 * nrm(ks[6], (N_ODD, GMLP_WIDTH), f32),
        "odd_v_ln_b": 0.1 * nrm(ks[7], (N_ODD, GMLP_WIDTH), f32),
        "odd_w_s": nrm(ks[8], (N_ODD, GMLP_GROUPS, GMLP_BLOCK, GMLP_BLOCK), f32) * GMLP_BLOCK ** -0.5,
        "odd_b_s": 1.0 + 0.1 * nrm(ks[9], (N_ODD, GMLP_GROUPS, GMLP_BLOCK), f32),
        "odd_w_out": nrm(ks[10], (N_ODD, GMLP_WIDTH, D_MODEL), f32) * (GMLP_WIDTH ** -0.5 * BETA),
        "mix_ln_g": 1.0 + 0.1 * nrm(ks[11], (DEPTH, D_MODEL), f32),
        "mix_ln_b": 0.1 * nrm(ks[12], (DEPTH, D_MODEL), f32),
        "ffn_w_in": nrm(ks[13], (DEPTH, D_MODEL, 2 * FFN_HIDDEN), f32) * D_MODEL ** -0.5,
        "ffn_w_out": nrm(ks[14], (DEPTH, FFN_HIDDEN, D_MODEL), f32) * (FFN_HIDDEN ** -0.5 * BETA),
        "ffn_ln_g": 1.0 + 0.1 * nrm(ks[15], (DEPTH, D_MODEL), f32),
        "ffn_ln_b": 0.1 * nrm(ks[16], (DEPTH, D_MODEL), f32),
    }


def _fwd_reference(x, even_w_in, even_b_f, even_conv_w, even_w_out, odd_w_in, odd_v_ln_g,
              odd_v_ln_b, odd_w_s, odd_b_s, odd_w_out, mix_ln_g, mix_ln_b, ffn_w_in,
              ffn_w_out, ffn_ln_g, ffn_ln_b):
    for layer in range(DEPTH):
        i = layer // 2
        if layer % 2 == 0:
            m = fox_conv_mixer(x, even_w_in[i], even_b_f[i], even_conv_w[i], even_w_out[i])
        else:
            m = gmlp_mixer(x, odd_w_in[i], odd_v_ln_g[i], odd_v_ln_b[i], odd_w_s[i],
                           odd_b_s[i], odd_w_out[i])
        x = layer_norm(ALPHA * x + m, mix_ln_g[layer], mix_ln_b[layer])
        x = layer_norm(ALPHA * x + swiglu(x, ffn_w_in[layer], ffn_w_out[layer]),
                       ffn_ln_g[layer], ffn_ln_b[layer])
    return x


import jax as _jax
import jax.numpy as _jnp

TWIN_FORMAT = 'train_step'
FWD_PARAMS = ['x', 'even_w_in', 'even_b_f', 'even_conv_w', 'even_w_out', 'odd_w_in', 'odd_v_ln_g', 'odd_v_ln_b', 'odd_w_s', 'odd_b_s', 'odd_w_out', 'mix_ln_g', 'mix_ln_b', 'ffn_w_in', 'ffn_w_out', 'ffn_ln_g', 'ffn_ln_b']
TWIN_WEIGHTS = ['even_w_in', 'even_b_f', 'even_conv_w', 'even_w_out', 'odd_w_in', 'odd_v_ln_g', 'odd_v_ln_b', 'odd_w_s', 'odd_b_s', 'odd_w_out', 'mix_ln_g', 'mix_ln_b', 'ffn_w_in', 'ffn_w_out', 'ffn_ln_g', 'ffn_ln_b']
TWIN_DIFF_INPUT = 'x'
TWIN_INPUTS = ['x', 'even_w_in', 'even_b_f', 'even_conv_w', 'even_w_out', 'odd_w_in', 'odd_v_ln_g', 'odd_v_ln_b', 'odd_w_s', 'odd_b_s', 'odd_w_out', 'mix_ln_g', 'mix_ln_b', 'ffn_w_in', 'ffn_w_out', 'ffn_ln_g', 'ffn_ln_b', 'loss_target', 'm_even_w_in', 'm_even_b_f', 'm_even_conv_w', 'm_even_w_out', 'm_odd_w_in', 'm_odd_v_ln_g', 'm_odd_v_ln_b', 'm_odd_w_s', 'm_odd_b_s', 'm_odd_w_out', 'm_mix_ln_g', 'm_mix_ln_b', 'm_ffn_w_in', 'm_ffn_w_out', 'm_ffn_ln_g', 'm_ffn_ln_b', 'v_even_w_in', 'v_even_b_f', 'v_even_conv_w', 'v_even_w_out', 'v_odd_w_in', 'v_odd_v_ln_g', 'v_odd_v_ln_b', 'v_odd_w_s', 'v_odd_b_s', 'v_odd_w_out', 'v_mix_ln_g', 'v_mix_ln_b', 'v_ffn_w_in', 'v_ffn_w_out', 'v_ffn_ln_g', 'v_ffn_ln_b']
TWIN_OUTPUTS = ['loss', 'grad_x', 'grad_even_w_in', 'grad_even_b_f', 'grad_even_conv_w', 'grad_even_w_out', 'grad_odd_w_in', 'grad_odd_v_ln_g', 'grad_odd_v_ln_b', 'grad_odd_w_s', 'grad_odd_b_s', 'grad_odd_w_out', 'grad_mix_ln_g', 'grad_mix_ln_b', 'grad_ffn_w_in', 'grad_ffn_w_out', 'grad_ffn_ln_g', 'grad_ffn_ln_b', 'delta_even_w_in', 'delta_even_b_f', 'delta_even_conv_w', 'delta_even_w_out', 'delta_odd_w_in', 'delta_odd_v_ln_g', 'delta_odd_v_ln_b', 'delta_odd_w_s', 'delta_odd_b_s', 'delta_odd_w_out', 'delta_mix_ln_g', 'delta_mix_ln_b', 'delta_ffn_w_in', 'delta_ffn_w_out', 'delta_ffn_ln_g', 'delta_ffn_ln_b', 'new_m_even_w_in', 'new_m_even_b_f', 'new_m_even_conv_w', 'new_m_even_w_out', 'new_m_odd_w_in', 'new_m_odd_v_ln_g', 'new_m_odd_v_ln_b', 'new_m_odd_w_s', 'new_m_odd_b_s', 'new_m_odd_w_out', 'new_m_mix_ln_g', 'new_m_mix_ln_b', 'new_m_ffn_w_in', 'new_m_ffn_w_out', 'new_m_ffn_ln_g', 'new_m_ffn_ln_b', 'new_v_even_w_in', 'new_v_even_b_f', 'new_v_even_conv_w', 'new_v_even_w_out', 'new_v_odd_w_in', 'new_v_odd_v_ln_g', 'new_v_odd_v_ln_b', 'new_v_odd_w_s', 'new_v_odd_b_s', 'new_v_odd_w_out', 'new_v_mix_ln_g', 'new_v_mix_ln_b', 'new_v_ffn_w_in', 'new_v_ffn_w_out', 'new_v_ffn_ln_g', 'new_v_ffn_ln_b']
TWIN_LEAF_KINDS = {'loss': 'loss', 'grad_x': 'grad_x', 'grad_even_w_in': 'grad_w', 'grad_even_b_f': 'grad_w', 'grad_even_conv_w': 'grad_w', 'grad_even_w_out': 'grad_w', 'grad_odd_w_in': 'grad_w', 'grad_odd_v_ln_g': 'grad_w', 'grad_odd_v_ln_b': 'grad_w', 'grad_odd_w_s': 'grad_w', 'grad_odd_b_s': 'grad_w', 'grad_odd_w_out': 'grad_w', 'grad_mix_ln_g': 'grad_w', 'grad_mix_ln_b': 'grad_w', 'grad_ffn_w_in': 'grad_w', 'grad_ffn_w_out': 'grad_w', 'grad_ffn_ln_g': 'grad_w', 'grad_ffn_ln_b': 'grad_w', 'delta_even_w_in': 'delta_w', 'delta_even_b_f': 'delta_w', 'delta_even_conv_w': 'delta_w', 'delta_even_w_out': 'delta_w', 'delta_odd_w_in': 'delta_w', 'delta_odd_v_ln_g': 'delta_w', 'delta_odd_v_ln_b': 'delta_w', 'delta_odd_w_s': 'delta_w', 'delta_odd_b_s': 'delta_w', 'delta_odd_w_out': 'delta_w', 'delta_mix_ln_g': 'delta_w', 'delta_mix_ln_b': 'delta_w', 'delta_ffn_w_in': 'delta_w', 'delta_ffn_w_out': 'delta_w', 'delta_ffn_ln_g': 'delta_w', 'delta_ffn_ln_b': 'delta_w', 'new_m_even_w_in': 'new_m', 'new_m_even_b_f': 'new_m', 'new_m_even_conv_w': 'new_m', 'new_m_even_w_out': 'new_m', 'new_m_odd_w_in': 'new_m', 'new_m_odd_v_ln_g': 'new_m', 'new_m_odd_v_ln_b': 'new_m', 'new_m_odd_w_s': 'new_m', 'new_m_odd_b_s': 'new_m', 'new_m_odd_w_out': 'new_m', 'new_m_mix_ln_g': 'new_m', 'new_m_mix_ln_b': 'new_m', 'new_m_ffn_w_in': 'new_m', 'new_m_ffn_w_out': 'new_m', 'new_m_ffn_ln_g': 'new_m', 'new_m_ffn_ln_b': 'new_m', 'new_v_even_w_in': 'new_v', 'new_v_even_b_f': 'new_v', 'new_v_even_conv_w': 'new_v', 'new_v_even_w_out': 'new_v', 'new_v_odd_w_in': 'new_v', 'new_v_odd_v_ln_g': 'new_v', 'new_v_odd_v_ln_b': 'new_v', 'new_v_odd_w_s': 'new_v', 'new_v_odd_b_s': 'new_v', 'new_v_odd_w_out': 'new_v', 'new_v_mix_ln_g': 'new_v', 'new_v_mix_ln_b': 'new_v', 'new_v_ffn_w_in': 'new_v', 'new_v_ffn_w_out': 'new_v', 'new_v_ffn_ln_g': 'new_v', 'new_v_ffn_ln_b': 'new_v'}


def _forward(args):
    return _fwd_reference(*[args[k] for k in FWD_PARAMS])


def _output_shape():
    def fwd():
        inp = _fwd_setup_inputs(0)
        return _fwd_reference(*[inp[k] for k in FWD_PARAMS])
    out = _jax.eval_shape(fwd)
    return out.shape, out.dtype

N_MICROBATCH = 1
ADAM_LR = 0.001
ADAM_B1 = 0.9
ADAM_B2 = 0.999
ADAM_EPS = 1e-08
ADAM_WD = 0.01
ADAM_STEP = 10
PER_EXAMPLE_BATCH_AXIS = {'x': 0, 'loss_target': 0}
SHARED_INPUTS = []
_WEIGHT_DTYPES = {'even_w_in': _jnp.float32, 'even_b_f': _jnp.float32, 'even_conv_w': _jnp.float32, 'even_w_out': _jnp.float32, 'odd_w_in': _jnp.float32, 'odd_v_ln_g': _jnp.float32, 'odd_v_ln_b': _jnp.float32, 'odd_w_s': _jnp.float32, 'odd_b_s': _jnp.float32, 'odd_w_out': _jnp.float32, 'mix_ln_g': _jnp.float32, 'mix_ln_b': _jnp.float32, 'ffn_w_in': _jnp.float32, 'ffn_w_out': _jnp.float32, 'ffn_ln_g': _jnp.float32, 'ffn_ln_b': _jnp.float32}
MOMENT_SCALE = {'even_w_in': 6.623613e-02, 'even_b_f': 4.511725e-01, 'even_conv_w': 9.778967e-02, 'even_w_out': 1.383161e-01, 'odd_w_in': 9.631757e-02, 'odd_v_ln_g': 4.932423e-02, 'odd_v_ln_b': 5.067719e-02, 'odd_w_s': 5.062345e-02, 'odd_b_s': 6.453428e-02, 'odd_w_out': 1.138081e+00, 'mix_ln_g': 1.163952e+01, 'mix_ln_b': 6.273140e+00, 'ffn_w_in': 4.110315e-02, 'ffn_w_out': 1.520890e-01, 'ffn_ln_g': 4.818937e+01, 'ffn_ln_b': 1.032287e+01}


def _to_microbatches(a, axis):
    t = _jnp.moveaxis(a, axis, 0)
    t = t.reshape((N_MICROBATCH, t.shape[0] // N_MICROBATCH) + t.shape[1:])
    return _jnp.moveaxis(t, 1, axis + 1)


def setup_inputs(seed: int = 0) -> dict:
    inp = _fwd_setup_inputs(seed)
    key = _jax.random.fold_in(_jax.random.key(seed), 7919)
    shape, _ = _output_shape()
    out = dict(inp)
    out["loss_target"] = _jax.random.normal(_jax.random.fold_in(key, 0), shape, _jnp.float32)
    for i, name in enumerate(TWIN_WEIGHTS):
        w = inp[name].astype(_jnp.float32)
        if MOMENT_SCALE is None:
            s = _jnp.sqrt(_jnp.mean(_jnp.square(w)) + 1e-30)
        else:
            s = MOMENT_SCALE[name]
        km, kv = _jax.random.split(_jax.random.fold_in(key, i + 1))
        out[name] = w
        out["m_" + name] = s * _jax.random.normal(km, w.shape, _jnp.float32)
        out["v_" + name] = (s * s) * _jax.random.uniform(kv, w.shape, _jnp.float32, 0.5, 1.5)
    if N_MICROBATCH > 1:
        for name, axis in PER_EXAMPLE_BATCH_AXIS.items():
            out[name] = _to_microbatches(out[name], axis)
    return {'x': out['x'], 'even_w_in': out['even_w_in'], 'even_b_f': out['even_b_f'], 'even_conv_w': out['even_conv_w'], 'even_w_out': out['even_w_out'], 'odd_w_in': out['odd_w_in'], 'odd_v_ln_g': out['odd_v_ln_g'], 'odd_v_ln_b': out['odd_v_ln_b'], 'odd_w_s': out['odd_w_s'], 'odd_b_s': out['odd_b_s'], 'odd_w_out': out['odd_w_out'], 'mix_ln_g': out['mix_ln_g'], 'mix_ln_b': out['mix_ln_b'], 'ffn_w_in': out['ffn_w_in'], 'ffn_w_out': out['ffn_w_out'], 'ffn_ln_g': out['ffn_ln_g'], 'ffn_ln_b': out['ffn_ln_b'], 'loss_target': out['loss_target'], 'm_even_w_in': out['m_even_w_in'], 'm_even_b_f': out['m_even_b_f'], 'm_even_conv_w': out['m_even_conv_w'], 'm_even_w_out': out['m_even_w_out'], 'm_odd_w_in': out['m_odd_w_in'], 'm_odd_v_ln_g': out['m_odd_v_ln_g'], 'm_odd_v_ln_b': out['m_odd_v_ln_b'], 'm_odd_w_s': out['m_odd_w_s'], 'm_odd_b_s': out['m_odd_b_s'], 'm_odd_w_out': out['m_odd_w_out'], 'm_mix_ln_g': out['m_mix_ln_g'], 'm_mix_ln_b': out['m_mix_ln_b'], 'm_ffn_w_in': out['m_ffn_w_in'], 'm_ffn_w_out': out['m_ffn_w_out'], 'm_ffn_ln_g': out['m_ffn_ln_g'], 'm_ffn_ln_b': out['m_ffn_ln_b'], 'v_even_w_in': out['v_even_w_in'], 'v_even_b_f': out['v_even_b_f'], 'v_even_conv_w': out['v_even_conv_w'], 'v_even_w_out': out['v_even_w_out'], 'v_odd_w_in': out['v_odd_w_in'], 'v_odd_v_ln_g': out['v_odd_v_ln_g'], 'v_odd_v_ln_b': out['v_odd_v_ln_b'], 'v_odd_w_s': out['v_odd_w_s'], 'v_odd_b_s': out['v_odd_b_s'], 'v_odd_w_out': out['v_odd_w_out'], 'v_mix_ln_g': out['v_mix_ln_g'], 'v_mix_ln_b': out['v_mix_ln_b'], 'v_ffn_w_in': out['v_ffn_w_in'], 'v_ffn_w_out': out['v_ffn_w_out'], 'v_ffn_ln_g': out['v_ffn_ln_g'], 'v_ffn_ln_b': out['v_ffn_ln_b']}


def _loss(weights, diff, rest, loss_target):
    with _jax.named_scope("forward"):
        args = {**rest, TWIN_DIFF_INPUT: diff, **{k: w.astype(_WEIGHT_DTYPES[k]) for k, w in weights.items()}}
        y = _forward(args)
    with _jax.named_scope("loss_head"):
        err = _jnp.square(y.astype(_jnp.float32) - loss_target)
        return 0.5 * _jnp.sum(_jnp.mean(err, axis=-1)) if err.ndim else 0.5 * err


def _adamw(w, g, m, v):
    m = ADAM_B1 * m + (1.0 - ADAM_B1) * g
    v = ADAM_B2 * v + (1.0 - ADAM_B2) * _jnp.square(g)
    m_hat = m / (1.0 - ADAM_B1 ** ADAM_STEP)
    v_hat = v / (1.0 - ADAM_B2 ** ADAM_STEP)
    delta = -ADAM_LR * (m_hat / (_jnp.sqrt(v_hat) + ADAM_EPS) + ADAM_WD * w)
    return delta, m, v


def reference(x, even_w_in, even_b_f, even_conv_w, even_w_out, odd_w_in, odd_v_ln_g, odd_v_ln_b, odd_w_s, odd_b_s, odd_w_out, mix_ln_g, mix_ln_b, ffn_w_in, ffn_w_out, ffn_ln_g, ffn_ln_b, loss_target, m_even_w_in, m_even_b_f, m_even_conv_w, m_even_w_out, m_odd_w_in, m_odd_v_ln_g, m_odd_v_ln_b, m_odd_w_s, m_odd_b_s, m_odd_w_out, m_mix_ln_g, m_mix_ln_b, m_ffn_w_in, m_ffn_w_out, m_ffn_ln_g, m_ffn_ln_b, v_even_w_in, v_even_b_f, v_even_conv_w, v_even_w_out, v_odd_w_in, v_odd_v_ln_g, v_odd_v_ln_b, v_odd_w_s, v_odd_b_s, v_odd_w_out, v_mix_ln_g, v_mix_ln_b, v_ffn_w_in, v_ffn_w_out, v_ffn_ln_g, v_ffn_ln_b):
    given = dict(x=x, even_w_in=even_w_in, even_b_f=even_b_f, even_conv_w=even_conv_w, even_w_out=even_w_out, odd_w_in=odd_w_in, odd_v_ln_g=odd_v_ln_g, odd_v_ln_b=odd_v_ln_b, odd_w_s=odd_w_s, odd_b_s=odd_b_s, odd_w_out=odd_w_out, mix_ln_g=mix_ln_g, mix_ln_b=mix_ln_b, ffn_w_in=ffn_w_in, ffn_w_out=ffn_w_out, ffn_ln_g=ffn_ln_g, ffn_ln_b=ffn_ln_b, loss_target=loss_target, m_even_w_in=m_even_w_in, m_even_b_f=m_even_b_f, m_even_conv_w=m_even_conv_w, m_even_w_out=m_even_w_out, m_odd_w_in=m_odd_w_in, m_odd_v_ln_g=m_odd_v_ln_g, m_odd_v_ln_b=m_odd_v_ln_b, m_odd_w_s=m_odd_w_s, m_odd_b_s=m_odd_b_s, m_odd_w_out=m_odd_w_out, m_mix_ln_g=m_mix_ln_g, m_mix_ln_b=m_mix_ln_b, m_ffn_w_in=m_ffn_w_in, m_ffn_w_out=m_ffn_w_out, m_ffn_ln_g=m_ffn_ln_g, m_ffn_ln_b=m_ffn_ln_b, v_even_w_in=v_even_w_in, v_even_b_f=v_even_b_f, v_even_conv_w=v_even_conv_w, v_even_w_out=v_even_w_out, v_odd_w_in=v_odd_w_in, v_odd_v_ln_g=v_odd_v_ln_g, v_odd_v_ln_b=v_odd_v_ln_b, v_odd_w_s=v_odd_w_s, v_odd_b_s=v_odd_b_s, v_odd_w_out=v_odd_w_out, v_mix_ln_g=v_mix_ln_g, v_mix_ln_b=v_mix_ln_b, v_ffn_w_in=v_ffn_w_in, v_ffn_w_out=v_ffn_w_out, v_ffn_ln_g=v_ffn_ln_g, v_ffn_ln_b=v_ffn_ln_b)
    weights = {n: given[n] for n in TWIN_WEIGHTS}
    shared = {n: given[n] for n in SHARED_INPUTS}
    per_example = {n: given[n] for n in ['x']}
    grad_fn = _jax.value_and_grad(_loss, argnums=(0, 1))

    def one_microbatch(ex, loss_target):
        ex = dict(ex)
        diff = ex.pop(TWIN_DIFF_INPUT)
        return grad_fn(weights, diff, {**shared, **ex}, loss_target)

    if N_MICROBATCH == 1:
        loss, (grad_w, grad_x) = one_microbatch(per_example, given["loss_target"])
    else:
        def body(carry, xs):
            loss_sum, grad_sum = carry
            l_k, (gw_k, gx_k) = one_microbatch(xs[0], xs[1])
            with _jax.named_scope("update"):
                return (loss_sum + l_k, _jax.tree.map(_jnp.add, grad_sum, gw_k)), gx_k

        init = (_jnp.zeros((), _jnp.float32), _jax.tree.map(_jnp.zeros_like, weights))
        (loss, grad_w), grad_x = _jax.lax.scan(body, init, (per_example, given["loss_target"]))
    with _jax.named_scope("update"):
        delta_w, new_m, new_v = {}, {}, {}
        for n in TWIN_WEIGHTS:
            delta_w[n], new_m[n], new_v[n] = _adamw(weights[n], grad_w[n], given["m_" + n], given["v_" + n])
    return (loss, grad_x, *[grad_w[n] for n in TWIN_WEIGHTS], *[delta_w[n] for n in TWIN_WEIGHTS],
            *[new_m[n] for n in TWIN_WEIGHTS], *[new_v[n] for n in TWIN_WEIGHTS])
```

```python
import functools
import math

import jax
import jax.numpy as jnp
from jax import lax
from jax.experimental import pallas as pl
from jax.experimental.pallas import tpu as pltpu

F32 = jnp.float32
BF16 = jnp.bfloat16

D_MODEL = 1024
FOX_HEADS = 8
HEAD_DIM = 64
HEAD_PAIRS = FOX_HEADS // 2
FOX_WIDTH = FOX_HEADS * HEAD_DIM
CONV_WIDTH = 512
CONV_K = 3
QKV = 3 * FOX_WIDTH
BCH = 3 * CONV_WIDTH
EVEN_IN = QKV + FOX_HEADS + BCH
EVEN_IN_PAD = QKV + BCH + 128
GMLP_BLOCK = 128
GMLP_GROUPS = 8
CHUNK = 64
FFN_HIDDEN = 2816
HALF_HIDDEN = FFN_HIDDEN // 2
ALPHA = 4.0 ** 0.25
LN_EPS = 1e-5
ADAM_LR = 0.001
ADAM_B1 = 0.9
ADAM_B2 = 0.999
ADAM_EPS = 1e-08
ADAM_WD = 0.01
ADAM_STEP = 10
N_CHIPS = 4
N_DEV = 8
LANES = 128
SUBLANES = 8
ROW_TILE = 512
FFN_ROW_TILE = 256
ATT_BLOCK = 512
VMEM_LIMIT = 56 * 2 ** 20
NEG = -1e30
MESH = pl.DeviceIdType.MESH
HIGHEST = lax.Precision.HIGHEST
NT = (((1,), (1,)), ((), ()))
TN = (((0,), (0,)), ((), ()))


def _cp():
    return pltpu.CompilerParams(vmem_limit_bytes=VMEM_LIMIT)


def _resident(shape):
    zeros = (0,) * len(shape)
    return pl.BlockSpec(shape, lambda *_: zeros, pipeline_mode=pl.Buffered(1))


def _sds(shape, dtype):
    return jax.ShapeDtypeStruct(tuple(shape), dtype)


_MASKS = {
    "gather4": [(1, 0, 0), (0, 1, 0), (1, 1, 0)],
    "scatter4": [(1, 0, 0), (0, 1, 0), (1, 1, 0)],
    "swap2": [(0, 0, 1)],
    "gather8": [(0, 0, 1), (0, 1, 0), (0, 1, 1), (1, 0, 0), (1, 0, 1), (1, 1, 0), (1, 1, 1)],
}


def _exchange(arrs, mode, name):
    n = len(arrs)
    masks = _MASKS[mode]
    npeer = len(masks)
    lead = {"gather4": N_CHIPS, "gather8": N_DEV}.get(mode)
    out_shapes = [_sds(((lead,) if lead else ()) + a.shape, a.dtype) for a in arrs]

    def body(*refs):
        ins, outs = refs[:n], refs[n:2 * n]
        send_sems, recv_sems, loc_sems = refs[2 * n:]
        x, y, c = lax.axis_index("x"), lax.axis_index("y"), lax.axis_index("c")
        chip, dev = 2 * x + y, 4 * x + 2 * y + c
        sends, recvs, locs = [], [], []
        for k in range(n):
            if mode == "gather4":
                locs.append(pltpu.make_async_copy(ins[k], outs[k].at[chip], loc_sems.at[k]))
            elif mode == "scatter4":
                locs.append(pltpu.make_async_copy(ins[k].at[chip], outs[k].at[chip], loc_sems.at[k]))
            elif mode == "gather8":
                locs.append(pltpu.make_async_copy(ins[k], outs[k].at[dev], loc_sems.at[k]))
        for cp in locs:
            cp.start()
        for k in range(n):
            for j, (dx, dy, dc) in enumerate(masks):
                px = 1 - x if dx else x
                py = 1 - y if dy else y
                pc = 1 - c if dc else c
                pchip, pdev = 2 * px + py, 4 * px + 2 * py + pc
                if mode == "gather4":
                    src, dst, land = ins[k], outs[k].at[chip], outs[k].at[pchip]
                elif mode == "scatter4":
                    src, dst, land = ins[k].at[pchip], outs[k].at[chip], outs[k].at[pchip]
                elif mode == "swap2":
                    src, dst, land = ins[k], outs[k], outs[k]
                else:
                    src, dst, land = ins[k], outs[k].at[dev], outs[k].at[pdev]
                s = k * npeer + j
                kw = dict(send_sem=send_sems.at[s], recv_sem=recv_sems.at[s], device_id=(px, py, pc),
                          device_id_type=MESH)
                cp = pltpu.make_async_remote_copy(src_ref=src, dst_ref=dst, **kw)
                cp.start()
                sends.append(cp)
                recvs.append(pltpu.make_async_remote_copy(src_ref=src, dst_ref=land, **kw))
        for cp in recvs:
            cp.wait_recv()
        for cp in sends:
            cp.wait_send()
        for cp in locs:
            cp.wait()

    any_spec = pl.BlockSpec(memory_space=pl.ANY)
    outs = pl.pallas_call(
        body,
        out_shape=out_shapes,
        in_specs=[any_spec] * n,
        out_specs=[any_spec] * n,
        scratch_shapes=[pltpu.SemaphoreType.DMA((n * npeer,)), pltpu.SemaphoreType.DMA((n * npeer,)),
                        pltpu.SemaphoreType.DMA((max(n, 1),))],
        name=name,
    )(*arrs)
    return list(outs)


def _sigmoid(x):
    return 1.0 / (1.0 + jnp.exp(-x))


def _log_sigmoid(x):
    e = jnp.exp(-jnp.abs(x))
    log1p = jnp.where(e < 1e-2, e * (1.0 - e * (0.5 - e * (1.0 / 3.0))), jnp.log(1.0 + e))
    return jnp.minimum(x, 0.0) - log1p


def _gelu(a):
    return 0.5 * a * (1.0 + lax.erf(a * (2.0 ** -0.5)))


def _gelu_grad(a):
    cdf = 0.5 * (1.0 + lax.erf(a * (2.0 ** -0.5)))
    pdf = jnp.exp(-0.5 * a * a) * (1.0 / math.sqrt(2.0 * math.pi))
    return cdf + a * pdf


def _ln_fwd(z):
    mu = jnp.mean(z, axis=-1, keepdims=True)
    zc = z - mu
    var = jnp.mean(zc * zc, axis=-1, keepdims=True)
    rstd = lax.rsqrt(var + LN_EPS)
    return zc * rstd, rstd


def _ln_bwd(dy, xhat, rstd, g):
    dxh = dy * g
    m1 = jnp.mean(dxh, axis=-1, keepdims=True)
    m2 = jnp.mean(dxh * xhat, axis=-1, keepdims=True)
    dz = rstd * (dxh - m1 - xhat * m2)
    return dz, jnp.sum(dy * xhat, axis=0, keepdims=True), jnp.sum(dy, axis=0, keepdims=True)


def _shift_down(z, halo):
    r = lax.broadcasted_iota(jnp.int32, z.shape, 0)
    z1 = jnp.where(r == 0, halo[7:8, :], pltpu.roll(z, 1, 0))
    z2 = jnp.where(r == 0, halo[6:7, :], jnp.where(r == 1, halo[7:8, :], pltpu.roll(z, 2, 0)))
    return z1, z2


def _shift_up(z, halo):
    n = z.shape[0]
    r = lax.broadcasted_iota(jnp.int32, z.shape, 0)
    z1 = jnp.where(r == n - 1, halo[0:1, :], pltpu.roll(z, n - 1, 0))
    z2 = jnp.where(r == n - 1, halo[1:2, :], jnp.where(r == n - 2, halo[0:1, :], pltpu.roll(z, n - 2, 0)))
    return z1, z2


def _accumulate(ref, first, value):
    @pl.when(first)
    def _():
        ref[...] = value

    @pl.when(jnp.logical_not(first))
    def _():
        ref[...] += value


def _proj(x, w, splits, name):
    t, k = x.shape
    tm = min(ROW_TILE, t)

    def body(x_ref, w_ref, *outs):
        a = x_ref[...].astype(BF16)
        for (lo, hi, dt), o in zip(splits, outs):
            o[...] = jnp.dot(a, w_ref[:, lo:hi], preferred_element_type=F32).astype(dt)

    return pl.pallas_call(
        body, grid=(t // tm,),
        in_specs=[pl.BlockSpec((tm, k), lambda i: (i, 0)), _resident(w.shape)],
        out_specs=[pl.BlockSpec((tm, hi - lo), lambda i: (i, 0)) for lo, hi, _ in splits],
        out_shape=[_sds((t, hi - lo), dt) for lo, hi, dt in splits],
        compiler_params=_cp(), name=name)(x, w)


def _fgate_fwd(fl3, b_f):
    nc = fl3.shape[0]

    def body(f_ref, b_ref, c_ref):
        r = lax.broadcasted_iota(jnp.int32, (LANES, LANES), 0)
        cidx = lax.broadcasted_iota(jnp.int32, (LANES, LANES), 1)
        upper = (r <= cidx).astype(F32)

        def step(i, carry):
            lf = _log_sigmoid(f_ref[i] + b_ref[...])
            cc = jnp.dot(lf, upper, precision=HIGHEST, preferred_element_type=F32) + carry
            c_ref[i] = cc
            return cc[:, LANES - 1:LANES]

        lax.fori_loop(0, nc, step, jnp.zeros((FOX_HEADS, 1), F32))

    return pl.pallas_call(body, out_shape=_sds(fl3.shape, F32), name="fgate_fwd")(fl3, b_f)


def _fgate_bwd(dc3, fl3, b_f):
    nc = fl3.shape[0]

    def body(dc_ref, f_ref, b_ref, df_ref, db_ref):
        r = lax.broadcasted_iota(jnp.int32, (LANES, LANES), 0)
        cidx = lax.broadcasted_iota(jnp.int32, (LANES, LANES), 1)
        lower = (r >= cidx).astype(F32)

        def step(n, carry):
            suffix, db = carry
            i = nc - 1 - n
            dlf = jnp.dot(dc_ref[i], lower, precision=HIGHEST, preferred_element_type=F32) + suffix
            df = dlf * (1.0 - _sigmoid(f_ref[i] + b_ref[...]))
            df_ref[i] = df
            return dlf[:, 0:1], db + jnp.sum(df, axis=1, keepdims=True)

        zero = jnp.zeros((FOX_HEADS, 1), F32)
        _, db = lax.fori_loop(0, nc, step, (zero, zero))
        db_ref[...] = db

    return pl.pallas_call(body, out_shape=[_sds(fl3.shape, F32), _sds((FOX_HEADS, 1), F32)],
                          name="fgate_bwd")(dc3, fl3, b_f)


def _attn_fwd(qkv, c_col, c_row):
    t = qkv.shape[0]
    bq = min(ATT_BLOCK, t)
    nq = t // bq
    scale = HEAD_DIM ** -0.5

    def body(q_ref, k_ref, v_ref, cq_ref, ck_ref, o_ref, lse_ref, m_sc, l_sc, acc_sc):
        i, j = pl.program_id(1), pl.program_id(2)

        @pl.when(j == 0)
        def _():
            m_sc[...] = jnp.full(m_sc.shape, NEG, F32)
            l_sc[...] = jnp.zeros(l_sc.shape, F32)
            acc_sc[...] = jnp.zeros(acc_sc.shape, F32)

        @pl.when(j <= i)
        def _():
            row = i * bq + lax.broadcasted_iota(jnp.int32, (bq, bq), 0)
            col = j * bq + lax.broadcasted_iota(jnp.int32, (bq, bq), 1)
            causal = row >= col
            for h in range(2):
                hs = slice(h * HEAD_DIM, (h + 1) * HEAD_DIM)
                s = lax.dot_general(q_ref[:, hs], k_ref[:, hs], NT, preferred_element_type=F32) * scale
                s = s + cq_ref[0, :, h:h + 1] - ck_ref[0, h:h + 1, :]
                s = jnp.where(causal, s, NEG)
                m_prev = m_sc[h]
                m_new = jnp.maximum(m_prev, jnp.max(s, axis=-1, keepdims=True))
                a = jnp.exp(m_prev - m_new)
                p = jnp.exp(s - m_new)
                l_sc[h] = a * l_sc[h] + jnp.sum(p, axis=-1, keepdims=True)
                acc_sc[h] = a * acc_sc[h] + jnp.dot(p.astype(BF16), v_ref[:, hs], preferred_element_type=F32)
                m_sc[h] = m_new

        @pl.when(j == i)
        def _():
            for h in range(2):
                hs = slice(h * HEAD_DIM, (h + 1) * HEAD_DIM)
                o_ref[:, hs] = (acc_sc[h] / l_sc[h]).astype(BF16)
                lse_ref[0, :, h:h + 1] = m_sc[h] + jnp.log(l_sc[h])

    return pl.pallas_call(
        body, grid=(HEAD_PAIRS, nq, nq),
        in_specs=[pl.BlockSpec((bq, LANES), lambda hp, i, j: (i, hp)),
                  pl.BlockSpec((bq, LANES), lambda hp, i, j: (jnp.minimum(j, i), HEAD_PAIRS + hp)),
                  pl.BlockSpec((bq, LANES), lambda hp, i, j: (jnp.minimum(j, i), 2 * HEAD_PAIRS + hp)),
                  pl.BlockSpec((1, bq, 2), lambda hp, i, j: (hp, i, 0)),
                  pl.BlockSpec((1, 2, bq), lambda hp, i, j: (hp, 0, jnp.minimum(j, i)))],
        out_specs=[pl.BlockSpec((bq, LANES), lambda hp, i, j: (i, hp)),
                   pl.BlockSpec((1, bq, 2), lambda hp, i, j: (hp, i, 0))],
        out_shape=[_sds((t, FOX_WIDTH), BF16), _sds((HEAD_PAIRS, t, 2), F32)],
        scratch_shapes=[pltpu.VMEM((2, bq, 1), F32), pltpu.VMEM((2, bq, 1), F32), pltpu.VMEM((2, bq, HEAD_DIM), F32)],
        compiler_params=_cp(), name="attn_fwd")(qkv, qkv, qkv, c_col, c_row)


def _conv_fwd(bch, conv_w):
    t = bch.shape[0]
    tm = min(ROW_TILE, t)
    halo_blocks = tm // SUBLANES
    cw = CONV_WIDTH

    def body(cur_ref, prev_ref, w_ref, o_ref):
        i = pl.program_id(0)
        z = cur_ref[:, cw:2 * cw] * cur_ref[:, 2 * cw:]
        zp = jnp.where(i == 0, 0.0, prev_ref[:, cw:2 * cw] * prev_ref[:, 2 * cw:])
        z1, z2 = _shift_down(z, zp)
        y = w_ref[0:1, :] * z2 + w_ref[1:2, :] * z1 + w_ref[2:3, :] * z
        o_ref[...] = (cur_ref[:, :cw] * y).astype(BF16)

    return pl.pallas_call(
        body, grid=(t // tm,),
        in_specs=[pl.BlockSpec((tm, BCH), lambda i: (i, 0)),
                  pl.BlockSpec((SUBLANES, BCH), lambda i: (jnp.maximum(i * halo_blocks - 1, 0), 0)),
                  _resident(conv_w.shape)],
        out_specs=pl.BlockSpec((tm, cw), lambda i: (i, 0)),
        out_shape=_sds((t, cw), BF16), compiler_params=_cp(), name="conv_fwd")(bch, bch, conv_w)


def _mm_res_ln(pairs, res, g, b, name):
    t, d = res.shape
    tm = min(ROW_TILE, t)
    n = len(pairs)

    def body(*refs):
        a_refs, w_refs = refs[:n], refs[n:2 * n]
        res_ref, g_ref, b_ref, y_ref, xh_ref, rs_ref = refs[2 * n:]
        z = ALPHA * res_ref[...]
        for a_ref, w_ref in zip(a_refs, w_refs):
            z = z + jnp.dot(a_ref[...].astype(BF16), w_ref[...], preferred_element_type=F32)
        xhat, rstd = _ln_fwd(z)
        y_ref[...] = xhat * g_ref[...] + b_ref[...]
        xh_ref[...] = xhat
        rs_ref[...] = rstd

    row = lambda i: (i, 0)
    return pl.pallas_call(
        body, grid=(t // tm,),
        in_specs=[pl.BlockSpec((tm, a.shape[1]), row) for a, _ in pairs] + [_resident(w.shape) for _, w in pairs]
        + [pl.BlockSpec((tm, d), row), _resident(g.shape), _resident(b.shape)],
        out_specs=[pl.BlockSpec((tm, d), row), pl.BlockSpec((tm, d), row), pl.BlockSpec((tm, 1), row)],
        out_shape=[_sds((t, d), F32), _sds((t, d), F32), _sds((t, 1), F32)],
        compiler_params=_cp(), name=name)(*[a for a, _ in pairs], *[w for _, w in pairs], res, g, b)


def _ffn_in(x, wi, name):
    t, d = x.shape
    tm = min(FFN_ROW_TILE, t)
    hh = HALF_HIDDEN

    def body(x_ref, w_ref, gu_ref, h_ref):
        a = x_ref[...].astype(BF16)
        for c in range(2):
            gs, us = slice(c * hh, (c + 1) * hh), slice(FFN_HIDDEN + c * hh, FFN_HIDDEN + (c + 1) * hh)
            g = jnp.dot(a, w_ref[:, gs], preferred_element_type=F32)
            u = jnp.dot(a, w_ref[:, us], preferred_element_type=F32)
            gu_ref[:, gs] = g.astype(BF16)
            gu_ref[:, us] = u.astype(BF16)
            h_ref[:, gs] = (g * _sigmoid(g) * u).astype(BF16)

    row = lambda i: (i, 0)
    return pl.pallas_call(
        body, grid=(t // tm,),
        in_specs=[pl.BlockSpec((tm, d), row), _resident(wi.shape)],
        out_specs=[pl.BlockSpec((tm, 2 * FFN_HIDDEN), row), pl.BlockSpec((tm, FFN_HIDDEN), row)],
        out_shape=[_sds((t, 2 * FFN_HIDDEN), BF16), _sds((t, FFN_HIDDEN), BF16)],
        compiler_params=_cp(), name=name)(x, wi)


def _gmlp_fwd(x, w_in, vg, vb, wm, bs_col):
    t, d = x.shape
    tm = min(ROW_TILE, t)
    gb = GMLP_BLOCK

    def body(x_ref, w_ref, vg_ref, vb_ref, wm_ref, bs_ref, a_ref, o_ref):
        a = jnp.dot(x_ref[...].astype(BF16), w_ref[...], preferred_element_type=F32)
        a_ref[...] = a
        u = _gelu(a[:, :d])
        vhat, _ = _ln_fwd(_gelu(a[:, d:]))
        vln = (vhat * vg_ref[...] + vb_ref[...]).astype(BF16)
        for blk in range(tm // gb):
            rs = slice(blk * gb, (blk + 1) * gb)
            for gi in range(GMLP_GROUPS):
                cs = slice(gi * gb, (gi + 1) * gb)
                s = jnp.dot(wm_ref[gi], vln[rs, cs], preferred_element_type=F32) + bs_ref[:, gi:gi + 1]
                o_ref[rs, cs] = (u[rs, cs] * s).astype(BF16)

    row = lambda i: (i, 0)
    return pl.pallas_call(
        body, grid=(t // tm,),
        in_specs=[pl.BlockSpec((tm, d), row), _resident(w_in.shape), _resident(vg.shape), _resident(vb.shape),
                  _resident(wm.shape), _resident(bs_col.shape)],
        out_specs=[pl.BlockSpec((tm, 2 * d), row), pl.BlockSpec((tm, d), row)],
        out_shape=[_sds((t, 2 * d), F32), _sds((t, d), BF16)],
        compiler_params=_cp(), name="gmlp_fwd")(x, w_in, vg, vb, wm, bs_col)


def _loss_ln_bwd(xhat, rstd, g, b, target):
    t, d = xhat.shape
    tm = min(ROW_TILE, t)

    def body(xh_ref, rs_ref, g_ref, b_ref, t_ref, sq_ref, dz_ref, dg_ref, db_ref):
        first = pl.program_id(0) == 0
        xh = xh_ref[...]
        err = xh * g_ref[...] + b_ref[...] - t_ref[...]
        dz, dg, db = _ln_bwd(err * (1.0 / d), xh, rs_ref[...], g_ref[...])
        dz_ref[...] = dz
        _accumulate(sq_ref, first, jnp.sum(err * err, axis=0, keepdims=True))
        _accumulate(dg_ref, first, dg)
        _accumulate(db_ref, first, db)

    row = lambda i: (i, 0)
    vec = pl.BlockSpec((1, d), lambda i: (0, 0))
    return pl.pallas_call(
        body, grid=(t // tm,),
        in_specs=[pl.BlockSpec((tm, d), row), pl.BlockSpec((tm, 1), row), _resident(g.shape), _resident(b.shape),
                  pl.BlockSpec((tm, d), row)],
        out_specs=[vec, pl.BlockSpec((tm, d), row), vec, vec],
        out_shape=[_sds((1, d), F32), _sds((t, d), F32), _sds((1, d), F32), _sds((1, d), F32)],
        compiler_params=_cp(), name="loss_ln_bwd")(xhat, rstd, g, b, target)


def _mm_nt(pairs, ws, name, *, tm=ROW_TILE, res=None, ln=None, out_dtype=F32):
    t = pairs[0][0].shape[0]
    k = ws[0].shape[0]
    tm = min(tm, t)
    n, nw = len(pairs), len(ws)

    def body(*refs):
        a_refs, w_refs = refs[:n], refs[n:n + nw]
        rest = list(refs[n + nw:])
        dx = None
        for a_ref, (_, wi, lo, hi) in zip(a_refs, pairs):
            part = lax.dot_general(a_ref[...].astype(BF16), w_refs[wi][:, lo:hi], NT, preferred_element_type=F32)
            dx = part if dx is None else dx + part
        if res is not None:
            dx = dx + ALPHA * rest.pop(0)[...]
        if ln is None:
            rest[0][...] = dx.astype(out_dtype)
            return
        xh_ref, rs_ref, g_ref, dz_ref, dg_ref, db_ref = rest
        first = pl.program_id(0) == 0
        dz, dg, db = _ln_bwd(dx, xh_ref[...], rs_ref[...], g_ref[...])
        dz_ref[...] = dz
        _accumulate(dg_ref, first, dg)
        _accumulate(db_ref, first, db)

    row = lambda i: (i, 0)
    in_specs = [pl.BlockSpec((tm, a.shape[1]), row) for a, _, _, _ in pairs] + [_resident(w.shape) for w in ws]
    args = [a for a, _, _, _ in pairs] + list(ws)
    if res is not None:
        in_specs.append(pl.BlockSpec((tm, k), row))
        args.append(res)
    if ln is None:
        out_specs = pl.BlockSpec((tm, k), row)
        out_shape = _sds((t, k), out_dtype)
    else:
        xhat, rstd, g = ln
        in_specs += [pl.BlockSpec((tm, k), row), pl.BlockSpec((tm, 1), row), _resident(g.shape)]
        args += [xhat, rstd, g]
        vec = pl.BlockSpec((1, k), lambda i: (0, 0))
        out_specs = [pl.BlockSpec((tm, k), row), vec, vec]
        out_shape = [_sds((t, k), F32), _sds((1, k), F32), _sds((1, k), F32)]
    return pl.pallas_call(body, grid=(t // tm,), in_specs=in_specs, out_specs=out_specs, out_shape=out_shape,
                          compiler_params=_cp(), name=name)(*args)


def _mm_tn(a, b, name, *, tn, tk=None, stack_cols=False, out_dtype=BF16):
    t, k = a.shape
    n = b.shape[1]
    tk = k if tk is None else tk
    tt = min(ROW_TILE, t)
    nt = t // tt

    def body(a_ref, b_ref, o_ref, acc_ref):
        s = pl.program_id(2)
        part = lax.dot_general(a_ref[...].astype(BF16), b_ref[...].astype(BF16), TN, preferred_element_type=F32)
        _accumulate(acc_ref, s == 0, part)

        @pl.when(s == nt - 1)
        def _():
            o_ref[...] = acc_ref[...].astype(out_dtype).reshape(o_ref.shape)

    if stack_cols:
        assert tk == k
        out_spec = pl.BlockSpec((1, k, tn), lambda kk, j, s: (j, 0, 0))
        out_shape = _sds((n // tn, k, tn), out_dtype)
    else:
        out_spec = pl.BlockSpec((tk, tn), lambda kk, j, s: (kk, j))
        out_shape = _sds((k, n), out_dtype)
    return pl.pallas_call(
        body, grid=(k // tk, n // tn, nt),
        in_specs=[pl.BlockSpec((tt, tk), lambda kk, j, s: (s, kk)), pl.BlockSpec((tt, tn), lambda kk, j, s: (s, j))],
        out_specs=out_spec, out_shape=out_shape,
        scratch_shapes=[pltpu.VMEM((tk, tn), F32)],
        compiler_params=_cp(), name=name)(a, b)


def _ffn_bwd_hidden(dz, wo, gu, name):
    t, d = dz.shape
    tm = min(FFN_ROW_TILE, t)
    hh = HALF_HIDDEN

    def body(dz_ref, w_ref, gu_ref, o_ref):
        a = dz_ref[...].astype(BF16)
        for c in range(2):
            gs, us = slice(c * hh, (c + 1) * hh), slice(FFN_HIDDEN + c * hh, FFN_HIDDEN + (c + 1) * hh)
            dh = lax.dot_general(a, w_ref[gs, :], NT, preferred_element_type=F32)
            g = gu_ref[:, gs].astype(F32)
            u = gu_ref[:, us].astype(F32)
            sig = _sigmoid(g)
            o_ref[:, gs] = (dh * u * sig * (1.0 + g * (1.0 - sig))).astype(BF16)
            o_ref[:, us] = (dh * g * sig).astype(BF16)

    row = lambda i: (i, 0)
    return pl.pallas_call(
        body, grid=(t // tm,),
        in_specs=[pl.BlockSpec((tm, d), row), _resident(wo.shape), pl.BlockSpec((tm, 2 * FFN_HIDDEN), row)],
        out_specs=pl.BlockSpec((tm, 2 * FFN_HIDDEN), row),
        out_shape=_sds((t, 2 * FFN_HIDDEN), BF16), compiler_params=_cp(), name=name)(dz, wo, gu)


def _gmlp_bwd(dgated, a, vg, vb, wm, bs_col):
    t, d2 = a.shape
    d = d2 // 2
    tm = min(ROW_TILE, t)
    gb = GMLP_BLOCK

    def body(dg_ref, a_ref, vg_ref, vb_ref, wm_ref, bs_ref, da_ref, dws_ref, dbs_ref, dvg_ref, dvb_ref, dvln_sc):
        first = pl.program_id(0) == 0
        au, av = a_ref[:, :d], a_ref[:, d:]
        u = _gelu(au)
        vhat, rstd = _ln_fwd(_gelu(av))
        vln = (vhat * vg_ref[...] + vb_ref[...]).astype(BF16)
        dgate = dg_ref[...]

        @pl.when(first)
        def _():
            dws_ref[...] = jnp.zeros(dws_ref.shape, F32)
            dbs_ref[...] = jnp.zeros(dbs_ref.shape, F32)

        for blk in range(tm // gb):
            rs = slice(blk * gb, (blk + 1) * gb)
            for gi in range(GMLP_GROUPS):
                cs = slice(gi * gb, (gi + 1) * gb)
                vblk = vln[rs, cs]
                s = jnp.dot(wm_ref[gi], vblk, preferred_element_type=F32) + bs_ref[:, gi:gi + 1]
                dgb = dgate[rs, cs]
                da_ref[rs, cs] = (dgb * s * _gelu_grad(au[rs, cs])).astype(BF16)
                ds = dgb * u[rs, cs]
                dsb = ds.astype(BF16)
                dws_ref[gi] += lax.dot_general(dsb, vblk, NT, preferred_element_type=F32)
                dbs_ref[:, gi:gi + 1] += jnp.sum(ds, axis=1, keepdims=True)
                dvln_sc[rs, cs] = lax.dot_general(wm_ref[gi], dsb, TN, preferred_element_type=F32)
        dv, dvg, dvb = _ln_bwd(dvln_sc[...], vhat, rstd, vg_ref[...])
        da_ref[:, d:] = (dv * _gelu_grad(av)).astype(BF16)
        _accumulate(dvg_ref, first, dvg)
        _accumulate(dvb_ref, first, dvb)

    row = lambda i: (i, 0)
    vec = pl.BlockSpec((1, d), lambda i: (0, 0))
    return pl.pallas_call(
        body, grid=(t // tm,),
        in_specs=[pl.BlockSpec((tm, d), row), pl.BlockSpec((tm, d2), row), _resident(vg.shape), _resident(vb.shape),
                  _resident(wm.shape), _resident(bs_col.shape)],
        out_specs=[pl.BlockSpec((tm, d2), row), pl.BlockSpec(wm.shape, lambda i: (0, 0, 0)),
                   pl.BlockSpec(bs_col.shape, lambda i: (0, 0)), vec, vec],
        out_shape=[_sds((t, d2), BF16), _sds(wm.shape, F32), _sds(bs_col.shape, F32), _sds((1, d), F32), _sds((1, d), F32)],
        scratch_shapes=[pltpu.VMEM((tm, d), F32)],
        compiler_params=_cp(), name="gmlp_bwd")(dgated, a, vg, vb, wm, bs_col)


def _conv_bwd(bch, dmix, conv_w):
    t = bch.shape[0]
    tm = min(ROW_TILE, t)
    nb = t // tm
    halo_blocks = tm // SUBLANES
    cw = CONV_WIDTH

    def body(cur_ref, prev_ref, next_ref, dc_ref, dn_ref, w_ref, o_ref, dw_ref):
        i = pl.program_id(0)
        bgate, cgate, hval = cur_ref[:, :cw], cur_ref[:, cw:2 * cw], cur_ref[:, 2 * cw:]
        z = cgate * hval
        zp = jnp.where(i == 0, 0.0, prev_ref[:, cw:2 * cw] * prev_ref[:, 2 * cw:])
        z1, z2 = _shift_down(z, zp)
        w0, w1, w2 = w_ref[0:1, :], w_ref[1:2, :], w_ref[2:3, :]
        dconv = dc_ref[...]
        o_ref[:, :cw] = (dconv * (w0 * z2 + w1 * z1 + w2 * z)).astype(BF16)
        dy = dconv * bgate
        dyn = jnp.where(i == nb - 1, 0.0, dn_ref[...] * next_ref[:, :cw])
        dy1, dy2 = _shift_up(dy, dyn)
        dz = w2 * dy + w1 * dy1 + w0 * dy2
        o_ref[:, cw:2 * cw] = (dz * hval).astype(BF16)
        o_ref[:, 2 * cw:] = (dz * cgate).astype(BF16)

        @pl.when(i == 0)
        def _():
            dw_ref[...] = jnp.zeros(dw_ref.shape, F32)

        for tap, zs in enumerate((z2, z1, z)):
            dw_ref[tap:tap + 1, :] += jnp.sum(dy * zs, axis=0, keepdims=True)

    last_halo = t // SUBLANES - 1
    return pl.pallas_call(
        body, grid=(nb,),
        in_specs=[pl.BlockSpec((tm, BCH), lambda i: (i, 0)),
                  pl.BlockSpec((SUBLANES, BCH), lambda i: (jnp.maximum(i * halo_blocks - 1, 0), 0)),
                  pl.BlockSpec((SUBLANES, BCH), lambda i: (jnp.minimum((i + 1) * halo_blocks, last_halo), 0)),
                  pl.BlockSpec((tm, cw), lambda i: (i, 1)),
                  pl.BlockSpec((SUBLANES, cw), lambda i: (jnp.minimum((i + 1) * halo_blocks, last_halo), 1)),
                  _resident(conv_w.shape)],
        out_specs=[pl.BlockSpec((tm, BCH), lambda i: (i, 0)), pl.BlockSpec((SUBLANES, cw), lambda i: (0, 0))],
        out_shape=[_sds((t, BCH), BF16), _sds((SUBLANES, cw), F32)],
        compiler_params=_cp(), name="conv_bwd")(bch, bch, bch, dmix, dmix, conv_w)


def _attn_delta(o, dmix):
    t = o.shape[0]
    tm = min(ROW_TILE, t)

    def body(o_ref, do_ref, d_ref):
        for h in range(2):
            hs = slice(h * HEAD_DIM, (h + 1) * HEAD_DIM)
            d_ref[0, :, h:h + 1] = jnp.sum(o_ref[:, hs].astype(F32) * do_ref[:, hs], axis=-1, keepdims=True)

    return pl.pallas_call(
        body, grid=(HEAD_PAIRS, t // tm),
        in_specs=[pl.BlockSpec((tm, LANES), lambda hp, i: (i, hp)), pl.BlockSpec((tm, LANES), lambda hp, i: (i, hp))],
        out_specs=pl.BlockSpec((1, tm, 2), lambda hp, i: (hp, i, 0)),
        out_shape=_sds((HEAD_PAIRS, t, 2), F32), compiler_params=_cp(), name="attn_delta")(o, dmix)


def _attn_bwd(qkv, dmix, lse, delta, c_col, c_row):
    t = qkv.shape[0]
    bq = min(ATT_BLOCK, t)
    nq = t // bq
    scale = HEAD_DIM ** -0.5

    def body(q_ref, k_ref, v_ref, do_ref, lse_ref, dl_ref, cq_ref, ck_ref, dq_ref, dk_ref, dv_ref, dc_ref, dr_ref,
             dk_sc, dv_sc):
        j, i = pl.program_id(1), pl.program_id(2)

        @pl.when(jnp.logical_and(j == 0, i == 0))
        def _():
            dq_ref[...] = jnp.zeros(dq_ref.shape, F32)
            dr_ref[...] = jnp.zeros(dr_ref.shape, F32)

        @pl.when(i == j)
        def _():
            dk_sc[...] = jnp.zeros(dk_sc.shape, F32)
            dv_sc[...] = jnp.zeros(dv_sc.shape, F32)
            dc_ref[...] = jnp.zeros(dc_ref.shape, F32)

        @pl.when(i >= j)
        def _():
            row = i * bq + lax.broadcasted_iota(jnp.int32, (bq, bq), 0)
            col = j * bq + lax.broadcasted_iota(jnp.int32, (bq, bq), 1)
            causal = row >= col
            rows = pl.ds(pl.multiple_of(i * bq, bq), bq)
            for h in range(2):
                hs = slice(h * HEAD_DIM, (h + 1) * HEAD_DIM)
                q, k, v = q_ref[:, hs], k_ref[:, hs], v_ref[:, hs]
                do = do_ref[:, hs].astype(BF16)
                s = lax.dot_general(q, k, NT, preferred_element_type=F32) * scale
                s = s + cq_ref[0, :, h:h + 1] - ck_ref[0, h:h + 1, :]
                s = jnp.where(causal, s, NEG)
                p = jnp.exp(s - lse_ref[0, :, h:h + 1])
                dv_sc[h] += lax.dot_general(p.astype(BF16), do, TN, preferred_element_type=F32)
                dp = lax.dot_general(do, v, NT, preferred_element_type=F32)
                ds = p * (dp - dl_ref[0, :, h:h + 1])
                dc_ref[0, h:h + 1, :] -= jnp.sum(ds, axis=0, keepdims=True)
                dr_ref[0, rows, h:h + 1] += jnp.sum(ds, axis=1, keepdims=True)
                dsb = ds.astype(BF16)
                dk_sc[h] += lax.dot_general(dsb, q, TN, preferred_element_type=F32) * scale
                dq_ref[rows, hs] += jnp.dot(dsb, k, preferred_element_type=F32) * scale

        @pl.when(i == nq - 1)
        def _():
            for h in range(2):
                hs = slice(h * HEAD_DIM, (h + 1) * HEAD_DIM)
                dk_ref[:, hs] = dk_sc[h].astype(BF16)
                dv_ref[:, hs] = dv_sc[h].astype(BF16)

    qi = lambda hp, j, i: (jnp.maximum(i, j), hp)
    return pl.pallas_call(
        body, grid=(HEAD_PAIRS, nq, nq),
        in_specs=[pl.BlockSpec((bq, LANES), qi),
                  pl.BlockSpec((bq, LANES), lambda hp, j, i: (j, HEAD_PAIRS + hp)),
                  pl.BlockSpec((bq, LANES), lambda hp, j, i: (j, 2 * HEAD_PAIRS + hp)),
                  pl.BlockSpec((bq, LANES), qi),
                  pl.BlockSpec((1, bq, 2), lambda hp, j, i: (hp, jnp.maximum(i, j), 0)),
                  pl.BlockSpec((1, bq, 2), lambda hp, j, i: (hp, jnp.maximum(i, j), 0)),
                  pl.BlockSpec((1, bq, 2), lambda hp, j, i: (hp, jnp.maximum(i, j), 0)),
                  pl.BlockSpec((1, 2, bq), lambda hp, j, i: (hp, 0, j))],
        out_specs=[pl.BlockSpec((t, LANES), lambda hp, j, i: (0, hp)),
                   pl.BlockSpec((bq, LANES), lambda hp, j, i: (j, hp)),
                   pl.BlockSpec((bq, LANES), lambda hp, j, i: (j, hp)),
                   pl.BlockSpec((1, 2, bq), lambda hp, j, i: (hp, 0, j)),
                   pl.BlockSpec((1, t, 2), lambda hp, j, i: (hp, 0, 0))],
        out_shape=[_sds((t, FOX_WIDTH), F32), _sds((t, FOX_WIDTH), BF16), _sds((t, FOX_WIDTH), BF16),
                   _sds((HEAD_PAIRS, 2, t), F32), _sds((HEAD_PAIRS, t, 2), F32)],
        scratch_shapes=[pltpu.VMEM((2, bq, HEAD_DIM), F32), pltpu.VMEM((2, bq, HEAD_DIM), F32)],
        compiler_params=_cp(), name="attn_bwd")(qkv, qkv, qkv, dmix, lse, delta, c_col, c_row)


def _adamw(parts, w, m, v, name):
    nl, r, c = w.shape
    tr = r
    for cand in (256, 128, 64, 32, 16):
        if r > cand and r % cand == 0:
            tr = cand
            break
    npart = len(parts)
    bc1 = 1.0 - ADAM_B1 ** ADAM_STEP
    bc2 = 1.0 - ADAM_B2 ** ADAM_STEP

    def body(*refs):
        p_refs = refs[:npart]
        w_ref, m_ref, v_ref, g_ref, d_ref, nm_ref, nv_ref = refs[npart:]
        sums = []
        for p_ref in p_refs:
            acc = p_ref[0, 0].astype(F32)
            for s in range(1, p_ref.shape[0]):
                acc = acc + p_ref[s, 0].astype(F32)
            sums.append(acc)
        g = sums[0]
        for extra in sums[1:]:
            g = g + extra
        nm = ADAM_B1 * m_ref[0] + (1.0 - ADAM_B1) * g
        nv = ADAM_B2 * v_ref[0] + (1.0 - ADAM_B2) * (g * g)
        m_hat = nm / bc1
        v_hat = nv / bc2
        g_ref[0] = g
        d_ref[0] = -ADAM_LR * (m_hat / (jnp.sqrt(v_hat) + ADAM_EPS) + ADAM_WD * w_ref[0])
        nm_ref[0] = nm
        nv_ref[0] = nv

    blk = pl.BlockSpec((1, tr, c), lambda l, i: (l, i, 0))
    return pl.pallas_call(
        body, grid=(nl, r // tr),
        in_specs=[pl.BlockSpec((p.shape[0], 1, tr, c), lambda l, i: (0, l, i, 0)) for p in parts] + [blk, blk, blk],
        out_specs=[blk] * 4, out_shape=[_sds(w.shape, F32)] * 4,
        compiler_params=_cp(), name=name)(*parts, w, m, v)


def _to_rows(a):
    flat = a.reshape(-1)
    pad = (-flat.shape[0]) % LANES
    if pad:
        flat = jnp.concatenate([flat, jnp.zeros((pad,), flat.dtype)])
    return flat.reshape(-1, LANES)


def _col_layout(rows_8t):
    t = rows_8t.shape[1]
    return rows_8t.reshape(HEAD_PAIRS, 2, t).transpose(0, 2, 1)


def _by_owner_cols(dw):
    k, n = dw.shape
    return dw.reshape(k, N_CHIPS, n // N_CHIPS).transpose(1, 0, 2)[:, None]


def _ffn_fwd(xin, wi, wo, g, b, layer):
    gu, h = _ffn_in(xin, wi, f"ffn_in_{layer}")
    y, xhat, rstd = _mm_res_ln([(h, wo)], xin, g, b, f"ffn_out_ln_{layer}")
    return y, (xin, gu, h, xhat, rstd)


def _ffn_bwd(dz, saved, wi, wo, ln_below, layer):
    xin, gu, h, _, _ = saved
    dgu = _ffn_bwd_hidden(dz, wo, gu, f"ffn_bwd_hidden_{layer}")
    g_out = _mm_tn(h, dz, f"ffn_dw_out_{layer}", tn=D_MODEL, tk=HALF_HIDDEN)
    g_in = _mm_tn(xin, dgu, f"ffn_dw_in_{layer}", tn=HALF_HIDDEN, stack_cols=True)
    below = _mm_nt([(dgu, 0, 0, 2 * FFN_HIDDEN)], [wi], f"ffn_dx_{layer}", tm=FFN_ROW_TILE, res=dz, ln=ln_below)
    return below, g_in, g_out.reshape(N_CHIPS, FFN_HIDDEN // N_CHIPS, D_MODEL)


def kernel(x, even_w_in, even_b_f, even_conv_w, even_w_out, odd_w_in, odd_v_ln_g, odd_v_ln_b, odd_w_s, odd_b_s, odd_w_out, mix_ln_g, mix_ln_b, ffn_w_in, ffn_w_out, ffn_ln_g, ffn_ln_b, loss_target, m_even_w_in, m_even_b_f, m_even_conv_w, m_even_w_out, m_odd_w_in, m_odd_v_ln_g, m_odd_v_ln_b, m_odd_w_s, m_odd_b_s, m_odd_w_out, m_mix_ln_g, m_mix_ln_b, m_ffn_w_in, m_ffn_w_out, m_ffn_ln_g, m_ffn_ln_b, v_even_w_in, v_even_b_f, v_even_conv_w, v_even_w_out, v_odd_w_in, v_odd_v_ln_g, v_odd_v_ln_b, v_odd_w_s, v_odd_b_s, v_odd_w_out, v_mix_ln_g, v_mix_ln_b, v_ffn_w_in, v_ffn_w_out, v_ffn_ln_g, v_ffn_ln_b):
    t = x.shape[1]
    d = D_MODEL
    chip = 2 * lax.axis_index("x") + lax.axis_index("y")
    x2d = x[0]
    target = loss_target[0]

    small_shard = jnp.concatenate([odd_v_ln_g.reshape(2, LANES), odd_v_ln_b.reshape(2, LANES),
                                   even_conv_w.reshape(CONV_K, LANES), jnp.zeros((1, LANES), F32)], axis=0)
    g_ewi, g_ewo, g_owi, g_owo, g_fwi, g_fwo, g_small = _exchange(
        [even_w_in[0].astype(BF16), even_w_out[0].astype(BF16), odd_w_in[0].astype(BF16), odd_w_out[0].astype(BF16),
         ffn_w_in.astype(BF16), ffn_w_out.astype(BF16), small_shard], "gather4", "gather_weights")
    ewi = g_ewi.transpose(1, 0, 2).reshape(d, EVEN_IN)
    w_even_in = jnp.concatenate([ewi[:, :QKV], ewi[:, QKV + FOX_HEADS:], ewi[:, QKV:QKV + FOX_HEADS],
                                 jnp.zeros((d, LANES - FOX_HEADS), BF16)], axis=1)
    w_even_out = g_ewo.reshape(d, d)
    w_odd_in = g_owi.transpose(1, 0, 2).reshape(d, 2 * d)
    w_odd_out = g_owo.reshape(d, d)
    w_ffn_in = [g_fwi[:, l].transpose(1, 0, 2).reshape(d, 2 * FFN_HIDDEN) for l in range(2)]
    w_ffn_out = [g_fwo[:, l].reshape(FFN_HIDDEN, d) for l in range(2)]
    v_ln_g = g_small[:, 0:2].reshape(1, d)
    v_ln_b = g_small[:, 2:4].reshape(1, d)
    conv_w = g_small[:, 4:7].transpose(1, 0, 2).reshape(CONV_K, CONV_WIDTH)
    chunk_id = jnp.arange(GMLP_BLOCK) // CHUNK
    gmask = chunk_id[None, :] <= chunk_id[:, None]
    w_spatial = jnp.where(gmask[None], odd_w_s[0], 0.0).astype(BF16)
    bs_col = odd_b_s[0].T
    b_f_col = even_b_f.reshape(FOX_HEADS, 1)
    ln = lambda p, l: p[l:l + 1]

    qkv, bch, fl = _proj(x2d, w_even_in, [(0, QKV, BF16), (QKV, QKV + BCH, F32), (QKV + BCH, EVEN_IN_PAD, F32)], "even_proj")
    fl3 = fl[:, :FOX_HEADS].T.reshape(FOX_HEADS, t // LANES, LANES).transpose(1, 0, 2)
    c3 = _fgate_fwd(fl3, b_f_col)
    c_rows = c3.transpose(1, 0, 2).reshape(FOX_HEADS, t)
    c_row = c_rows.reshape(HEAD_PAIRS, 2, t)
    c_col = _col_layout(c_rows)
    attn, lse = _attn_fwd(qkv, c_col, c_row)
    conv = _conv_fwd(bch, conv_w)
    x1, xh1, rs1 = _mm_res_ln([(attn, w_even_out[:FOX_WIDTH]), (conv, w_even_out[FOX_WIDTH:])], x2d,
                              ln(mix_ln_g, 0), ln(mix_ln_b, 0), "even_out_ln")
    x2, ffn0 = _ffn_fwd(x1, w_ffn_in[0], w_ffn_out[0], ln(ffn_ln_g, 0), ln(ffn_ln_b, 0), 0)

    a_odd, gated = _gmlp_fwd(x2, w_odd_in, v_ln_g, v_ln_b, w_spatial, bs_col)
    x3, xh3, rs3 = _mm_res_ln([(gated, w_odd_out)], x2, ln(mix_ln_g, 1), ln(mix_ln_b, 1), "odd_out_ln")
    _, ffn1 = _ffn_fwd(x3, w_ffn_in[1], w_ffn_out[1], ln(ffn_ln_g, 1), ln(ffn_ln_b, 1), 1)

    sq, dz4, d_fg1, d_fb1 = _loss_ln_bwd(ffn1[3], ffn1[4], ln(ffn_ln_g, 1), ln(ffn_ln_b, 1), target)
    loss = lax.psum(0.5 / d * jnp.sum(sq), ("x", "y", "c"))
    (dz3, d_mg1, d_mb1), gi_f1, go_f1 = _ffn_bwd(dz4, ffn1, w_ffn_in[1], w_ffn_out[1], (xh3, rs3, ln(mix_ln_g, 1)), 1)

    dgated = _mm_nt([(dz3, 0, 0, d)], [w_odd_out], "odd_dgated")
    go_odd = _mm_tn(gated, dz3, "odd_dw_out", tn=d).reshape(N_CHIPS, 1, d // N_CHIPS, d)
    da_odd, dws, dbs_col, d_vg, d_vb = _gmlp_bwd(dgated, a_odd, v_ln_g, v_ln_b, w_spatial, bs_col)
    gi_odd = _by_owner_cols(_mm_tn(x2, da_odd, "odd_dw_in", tn=d))
    dz2, d_fg0, d_fb0 = _mm_nt([(da_odd, 0, 0, 2 * d)], [w_odd_in], "odd_dx", res=dz3,
                               ln=(ffn0[3], ffn0[4], ln(ffn_ln_g, 0)))
    (dz1, d_mg0, d_mb0), gi_f0, go_f0 = _ffn_bwd(dz2, ffn0, w_ffn_in[0], w_ffn_out[0], (xh1, rs1, ln(mix_ln_g, 0)), 0)

    dmix = _mm_nt([(dz1, 0, 0, d)], [w_even_out], "even_dmix")
    mix = jnp.concatenate([attn, conv], axis=1)
    go_even = _mm_tn(mix, dz1, "even_dw_out", tn=d).reshape(N_CHIPS, 1, d // N_CHIPS, d)
    dbch, dconv_w8 = _conv_bwd(bch, dmix, conv_w)
    delta = _attn_delta(attn, dmix)
    dq, dk, dv, dc_row, dr_col = _attn_bwd(qkv, dmix, lse, delta, c_col, c_row)
    dc_rows = dc_row.reshape(FOX_HEADS, t) + dr_col.transpose(0, 2, 1).reshape(FOX_HEADS, t)
    dc3 = dc_rows.reshape(FOX_HEADS, t // LANES, LANES).transpose(1, 0, 2)
    dfl3, d_bf = _fgate_bwd(dc3, fl3, b_f_col)
    dfl = jnp.concatenate([dfl3.transpose(1, 0, 2).reshape(FOX_HEADS, t).T.astype(BF16),
                           jnp.zeros((t, LANES - FOX_HEADS), BF16)], axis=1)
    grad_x = _mm_nt([(dq, 0, 0, FOX_WIDTH), (dk, 0, FOX_WIDTH, 2 * FOX_WIDTH), (dv, 0, 2 * FOX_WIDTH, QKV),
                     (dbch, 0, QKV, QKV + BCH), (dfl, 0, QKV + BCH, EVEN_IN_PAD)], [w_even_in], "even_dx", res=dz1)
    dqkv = jnp.concatenate([dq.astype(BF16), dk, dv], axis=1)
    dw_qkv = _mm_tn(x2d, dqkv, "even_dw_qkv", tn=QKV // 2, out_dtype=F32)
    dw_bch = _mm_tn(x2d, dbch, "even_dw_bch", tn=BCH // 2, out_dtype=F32)
    dw_f = _mm_tn(x2d, dfl, "even_dw_f", tn=LANES, out_dtype=F32)
    gi_even = _by_owner_cols(jnp.concatenate([dw_qkv, dw_f[:, :FOX_HEADS], dw_bch], axis=1).astype(BF16))

    big = [gi_even, go_even, gi_odd, go_odd, jnp.stack([gi_f0, gi_f1], axis=1), jnp.stack([go_f0, go_f1], axis=1)]
    mine = _exchange(big, "scatter4", "scatter_grads")
    theirs = _exchange(mine, "swap2", "swap_grads")
    big_w = [(even_w_in, m_even_w_in, v_even_w_in), (even_w_out, m_even_w_out, v_even_w_out),
             (odd_w_in, m_odd_w_in, v_odd_w_in), (odd_w_out, m_odd_w_out, v_odd_w_out),
             (ffn_w_in, m_ffn_w_in, v_ffn_w_in), (ffn_w_out, m_ffn_w_out, v_ffn_w_out)]
    big_names = ["even_w_in", "even_w_out", "odd_w_in", "odd_w_out", "ffn_w_in", "ffn_w_out"]
    res = {}
    for nm, own, sib, (w, m, v) in zip(big_names, mine, theirs, big_w):
        res[nm] = _adamw([own, sib], w, m, v, f"adamw_{nm}")

    dws_masked = jnp.where(gmask[None], dws, 0.0)
    rep_names = ["odd_w_s", "odd_b_s", "mix_ln_g", "mix_ln_b", "ffn_ln_g", "ffn_ln_b", "even_b_f"]
    rep_grads = [dws_masked, dbs_col.T, jnp.concatenate([d_mg0, d_mg1]), jnp.concatenate([d_mb0, d_mb1]),
                 jnp.concatenate([d_fg0, d_fg1]), jnp.concatenate([d_fb0, d_fb1]), d_bf.reshape(1, FOX_HEADS)]
    rep_w = [(odd_w_s, m_odd_w_s, v_odd_w_s), (odd_b_s, m_odd_b_s, v_odd_b_s), (mix_ln_g, m_mix_ln_g, v_mix_ln_g),
             (mix_ln_b, m_mix_ln_b, v_mix_ln_b), (ffn_ln_g, m_ffn_ln_g, v_ffn_ln_g), (ffn_ln_b, m_ffn_ln_b, v_ffn_ln_b),
             (even_b_f, m_even_b_f, v_even_b_f)]
    rep_rows = [_to_rows(gr) for gr in rep_grads]
    n_rep = sum(r.shape[0] for r in rep_rows)
    pad_rep = (-n_rep) % SUBLANES
    dconv_w = dconv_w8[:CONV_K].reshape(CONV_K, N_CHIPS, LANES).transpose(1, 0, 2).reshape(N_CHIPS * CONV_K, LANES)
    packed = jnp.concatenate(rep_rows + [jnp.zeros((pad_rep, LANES), F32), d_vg.reshape(SUBLANES, LANES),
                                         d_vb.reshape(SUBLANES, LANES), dconv_w, jnp.zeros((4, LANES), F32)], axis=0)
    (gathered,) = _exchange([packed], "gather8", "gather_small_grads")
    base = n_rep + pad_rep
    own_rows = jnp.concatenate([
        lax.dynamic_slice_in_dim(gathered, base + 2 * chip, 2, axis=1),
        lax.dynamic_slice_in_dim(gathered, base + SUBLANES + 2 * chip, 2, axis=1),
        lax.dynamic_slice_in_dim(gathered, base + 2 * SUBLANES + CONV_K * chip, CONV_K, axis=1),
        jnp.zeros((N_DEV, 1, LANES), F32)], axis=1)
    small_parts = jnp.concatenate([gathered[:, :base], own_rows], axis=1)[:, None]

    def pack_small(get):
        rows = [_to_rows(get(tw)) for tw in rep_w] + [jnp.zeros((pad_rep, LANES), F32)]
        rows += [get(sh).reshape(-1, LANES) for sh in ((odd_v_ln_g, m_odd_v_ln_g, v_odd_v_ln_g),
                                                       (odd_v_ln_b, m_odd_v_ln_b, v_odd_v_ln_b),
                                                       (even_conv_w, m_even_conv_w, v_even_conv_w))]
        return jnp.concatenate(rows + [jnp.zeros((1, LANES), F32)], axis=0)[None]

    small_out = _adamw([small_parts], pack_small(lambda tw: tw[0]), pack_small(lambda tw: tw[1]),
                       pack_small(lambda tw: tw[2]), "adamw_small")

    def unpack_small(rows3):
        rows = rows3[0]
        out, off = {}, 0
        for nm, (w, _, _), r in zip(rep_names, rep_w, rep_rows):
            out[nm] = rows[off:off + r.shape[0]].reshape(-1)[:w.size].reshape(w.shape)
            off += r.shape[0]
        off += pad_rep
        out["odd_v_ln_g"] = rows[off:off + 2].reshape(odd_v_ln_g.shape)
        out["odd_v_ln_b"] = rows[off + 2:off + 4].reshape(odd_v_ln_b.shape)
        out["even_conv_w"] = rows[off + 4:off + 4 + CONV_K].reshape(even_conv_w.shape)
        return out

    small = [unpack_small(o) for o in small_out]
    order = ["even_w_in", "even_b_f", "even_conv_w", "even_w_out", "odd_w_in", "odd_v_ln_g", "odd_v_ln_b", "odd_w_s",
             "odd_b_s", "odd_w_out", "mix_ln_g", "mix_ln_b", "ffn_w_in", "ffn_w_out", "ffn_ln_g", "ffn_ln_b"]
    outs = [loss, grad_x[None]]
    for kind in range(4):
        for nm in order:
            outs.append(res[nm][kind] if nm in res else small[kind][nm])
    return tuple(outs)
```

```python
import functools
import math

import jax
import jax.numpy as jnp
from jax import lax
from jax.experimental import pallas as pl
from jax.experimental.pallas import tpu as pltpu

F32 = jnp.float32
BF16 = jnp.bfloat16

D_MODEL = 1024
FOX_HEADS = 8
HEAD_DIM = 64
HEAD_PAIRS = FOX_HEADS // 2
FOX_WIDTH = FOX_HEADS * HEAD_DIM
CONV_WIDTH = 512
CONV_K = 3
QKV = 3 * FOX_WIDTH
BCH = 3 * CONV_WIDTH
EVEN_IN = QKV + FOX_HEADS + BCH
EVEN_IN_PAD = QKV + BCH + 128
GMLP_BLOCK = 128
GMLP_GROUPS = 8
CHUNK = 64
FFN_HIDDEN = 2816
HALF_HIDDEN = FFN_HIDDEN // 2
ALPHA = 4.0 ** 0.25
LN_EPS = 1e-5
ADAM_LR = 0.001
ADAM_B1 = 0.9
ADAM_B2 = 0.999
ADAM_EPS = 1e-08
ADAM_WD = 0.01
ADAM_STEP = 10
N_CHIPS = 4
N_DEV = 8
LANES = 128
SUBLANES = 8
ROW_TILE = 512
FFN_ROW_TILE = 256
ATT_BLOCK = 512
VMEM_LIMIT = 56 * 2 ** 20
NEG = -1e30
MESH = pl.DeviceIdType.MESH
HIGHEST = lax.Precision.HIGHEST
Q_C, Q_ONE, Q_LSE = 64, 67, 70
K_ONE, K_C, K_ONE2 = 64, 67, 70
V_ONE = 64
DO_DELTA = 65
NT = (((1,), (1,)), ((), ()))
TN = (((0,), (0,)), ((), ()))


def _cp():
    return pltpu.CompilerParams(vmem_limit_bytes=VMEM_LIMIT)


def _resident(shape):
    zeros = (0,) * len(shape)
    return pl.BlockSpec(shape, lambda *_: zeros, pipeline_mode=pl.Buffered(1))


def _sds(shape, dtype):
    return jax.ShapeDtypeStruct(tuple(shape), dtype)


_MASKS = {
    "gather4": [(1, 0, 0), (0, 1, 0), (1, 1, 0)],
    "scatter4": [(1, 0, 0), (0, 1, 0), (1, 1, 0)],
    "swap2": [(0, 0, 1)],
    "gather8": [(0, 0, 1), (0, 1, 0), (0, 1, 1), (1, 0, 0), (1, 0, 1), (1, 1, 0), (1, 1, 1)],
}


def _exchange(arrs, mode, name):
    n = len(arrs)
    masks = _MASKS[mode]
    npeer = len(masks)
    lead = {"gather4": N_CHIPS, "gather8": N_DEV}.get(mode)
    out_shapes = [_sds(((lead,) if lead else ()) + a.shape, a.dtype) for a in arrs]

    def body(*refs):
        ins, outs = refs[:n], refs[n:2 * n]
        send_sems, recv_sems, loc_sems = refs[2 * n:]
        x, y, c = lax.axis_index("x"), lax.axis_index("y"), lax.axis_index("c")
        chip, dev = 2 * x + y, 4 * x + 2 * y + c
        sends, recvs, locs = [], [], []
        for k in range(n):
            if mode == "gather4":
                locs.append(pltpu.make_async_copy(ins[k], outs[k].at[chip], loc_sems.at[k]))
            elif mode == "scatter4":
                locs.append(pltpu.make_async_copy(ins[k].at[chip], outs[k].at[chip], loc_sems.at[k]))
            elif mode == "gather8":
                locs.append(pltpu.make_async_copy(ins[k], outs[k].at[dev], loc_sems.at[k]))
        for cp in locs:
            cp.start()
        for k in range(n):
            for j, (dx, dy, dc) in enumerate(masks):
                px = 1 - x if dx else x
                py = 1 - y if dy else y
                pc = 1 - c if dc else c
                pchip, pdev = 2 * px + py, 4 * px + 2 * py + pc
                if mode == "gather4":
                    src, dst, land = ins[k], outs[k].at[chip], outs[k].at[pchip]
                elif mode == "scatter4":
                    src, dst, land = ins[k].at[pchip], outs[k].at[chip], outs[k].at[pchip]
                elif mode == "swap2":
                    src, dst, land = ins[k], outs[k], outs[k]
                else:
                    src, dst, land = ins[k], outs[k].at[dev], outs[k].at[pdev]
                s = k * npeer + j
                kw = dict(send_sem=send_sems.at[s], recv_sem=recv_sems.at[s], device_id=(px, py, pc),
                          device_id_type=MESH)
                cp = pltpu.make_async_remote_copy(src_ref=src, dst_ref=dst, **kw)
                cp.start()
                sends.append(cp)
                recvs.append(pltpu.make_async_remote_copy(src_ref=src, dst_ref=land, **kw))
        for cp in recvs:
            cp.wait_recv()
        for cp in sends:
            cp.wait_send()
        for cp in locs:
            cp.wait()

    any_spec = pl.BlockSpec(memory_space=pl.ANY)
    outs = pl.pallas_call(
        body,
        out_shape=out_shapes,
        in_specs=[any_spec] * n,
        out_specs=[any_spec] * n,
        scratch_shapes=[pltpu.SemaphoreType.DMA((n * npeer,)), pltpu.SemaphoreType.DMA((n * npeer,)),
                        pltpu.SemaphoreType.DMA((max(n, 1),))],
        name=name,
    )(*arrs)
    return list(outs)


def _sigmoid(x):
    return 1.0 / (1.0 + jnp.exp(-x))


def _log_sigmoid(x):
    e = jnp.exp(-jnp.abs(x))
    log1p = jnp.where(e < 1e-2, e * (1.0 - e * (0.5 - e * (1.0 / 3.0))), jnp.log(1.0 + e))
    return jnp.minimum(x, 0.0) - log1p


def _gelu(a):
    return 0.5 * a * (1.0 + lax.erf(a * (2.0 ** -0.5)))


def _gelu_grad(a):
    cdf = 0.5 * (1.0 + lax.erf(a * (2.0 ** -0.5)))
    pdf = jnp.exp(-0.5 * a * a) * (1.0 / math.sqrt(2.0 * math.pi))
    return cdf + a * pdf


def _ln_fwd(z):
    mu = jnp.mean(z, axis=-1, keepdims=True)
    zc = z - mu
    var = jnp.mean(zc * zc, axis=-1, keepdims=True)
    rstd = lax.rsqrt(var + LN_EPS)
    return zc * rstd, rstd


def _ln_bwd(dy, xhat, rstd, g):
    dxh = dy * g
    m1 = jnp.mean(dxh, axis=-1, keepdims=True)
    m2 = jnp.mean(dxh * xhat, axis=-1, keepdims=True)
    dz = rstd * (dxh - m1 - xhat * m2)
    return dz, jnp.sum(dy * xhat, axis=0, keepdims=True), jnp.sum(dy, axis=0, keepdims=True)


def _shift_down(z, halo):
    r = lax.broadcasted_iota(jnp.int32, z.shape, 0)
    z1 = jnp.where(r == 0, halo[7:8, :], pltpu.roll(z, 1, 0))
    z2 = jnp.where(r == 0, halo[6:7, :], jnp.where(r == 1, halo[7:8, :], pltpu.roll(z, 2, 0)))
    return z1, z2


def _shift_up(z, halo):
    n = z.shape[0]
    r = lax.broadcasted_iota(jnp.int32, z.shape, 0)
    z1 = jnp.where(r == n - 1, halo[0:1, :], pltpu.roll(z, n - 1, 0))
    z2 = jnp.where(r == n - 1, halo[1:2, :], jnp.where(r == n - 2, halo[0:1, :], pltpu.roll(z, n - 2, 0)))
    return z1, z2


def _accumulate(ref, first, value):
    @pl.when(first)
    def _():
        ref[...] = value

    @pl.when(jnp.logical_not(first))
    def _():
        ref[...] += value


def _proj(x, w, splits, name):
    t, k = x.shape
    tm = min(ROW_TILE, t)

    def body(x_ref, w_ref, *outs):
        a = x_ref[...].astype(BF16)
        for (lo, hi, dt), o in zip(splits, outs):
            o[...] = jnp.dot(a, w_ref[:, lo:hi], preferred_element_type=F32).astype(dt)

    return pl.pallas_call(
        body, grid=(t // tm,),
        in_specs=[pl.BlockSpec((tm, k), lambda i: (i, 0)), _resident(w.shape)],
        out_specs=[pl.BlockSpec((tm, hi - lo), lambda i: (i, 0)) for lo, hi, _ in splits],
        out_shape=[_sds((t, hi - lo), dt) for lo, hi, dt in splits],
        compiler_params=_cp(), name=name)(x, w)


def _fgate_fwd(fl3, b_f):
    nc = fl3.shape[0]

    def body(f_ref, b_ref, c_ref):
        r = lax.broadcasted_iota(jnp.int32, (LANES, LANES), 0)
        cidx = lax.broadcasted_iota(jnp.int32, (LANES, LANES), 1)
        upper = (r <= cidx).astype(F32)

        def step(i, carry):
            lf = _log_sigmoid(f_ref[i] + b_ref[...])
            cc = jnp.dot(lf, upper, precision=HIGHEST, preferred_element_type=F32) + carry
            c_ref[i] = cc
            return cc[:, LANES - 1:LANES]

        lax.fori_loop(0, nc, step, jnp.zeros((FOX_HEADS, 1), F32))

    return pl.pallas_call(body, out_shape=_sds(fl3.shape, F32), name="fgate_fwd")(fl3, b_f)


def _fgate_bwd(dc3, fl3, b_f):
    nc = fl3.shape[0]

    def body(dc_ref, f_ref, b_ref, df_ref, db_ref):
        r = lax.broadcasted_iota(jnp.int32, (LANES, LANES), 0)
        cidx = lax.broadcasted_iota(jnp.int32, (LANES, LANES), 1)
        lower = (r >= cidx).astype(F32)

        def step(n, carry):
            suffix, db = carry
            i = nc - 1 - n
            dlf = jnp.dot(dc_ref[i], lower, precision=HIGHEST, preferred_element_type=F32) + suffix
            df = dlf * (1.0 - _sigmoid(f_ref[i] + b_ref[...]))
            df_ref[i] = df
            return dlf[:, 0:1], db + jnp.sum(df, axis=1, keepdims=True)

        zero = jnp.zeros((FOX_HEADS, 1), F32)
        _, db = lax.fori_loop(0, nc, step, (zero, zero))
        db_ref[...] = db

    return pl.pallas_call(body, out_shape=[_sds(fl3.shape, F32), _sds((FOX_HEADS, 1), F32)],
                          name="fgate_bwd")(dc3, fl3, b_f)


def _split3(c):
    hi = c.astype(BF16).astype(F32)
    mid = (c - hi).astype(BF16).astype(F32)
    lo = (c - hi - mid).astype(BF16).astype(F32)
    return hi, mid, lo


def _lane_pieces(lane, start, pieces, sign):
    out = jnp.zeros(lane.shape, F32)
    for n, p in enumerate(pieces):
        out = jnp.where(lane == start + n, sign * p, out)
    return out


def _attn_pack(qkv, c_col):
    t = qkv.shape[0]
    tm = min(ROW_TILE, t)
    hd = HEAD_DIM

    def body(x_ref, c_ref, qp_ref, kp_ref, vp_ref, kt_ref, vt_ref):
        lane = lax.broadcasted_iota(jnp.int32, (tm, hd), 1) + hd
        for h in range(FOX_HEADS):
            pieces = _split3(c_ref[:, h:h + 1])
            ones = lambda a, b: jnp.where(jnp.logical_and(lane >= a, lane < b), 1.0, 0.0)
            q_extra = _lane_pieces(lane, Q_C, pieces, 1.0) + ones(Q_ONE, Q_ONE + 3)
            k_extra = _lane_pieces(lane, K_C, pieces, -1.0) + ones(K_ONE, K_ONE + 3) + ones(K_ONE2, K_ONE2 + 3)
            qp_ref[h, :, :hd] = (x_ref[:, h * hd:(h + 1) * hd].astype(F32) * (hd ** -0.5)).astype(BF16)
            qp_ref[h, :, hd:] = q_extra.astype(BF16)
            kp_ref[h, :, :hd] = x_ref[:, FOX_WIDTH + h * hd:FOX_WIDTH + (h + 1) * hd]
            kp_ref[h, :, hd:] = k_extra.astype(BF16)
            vp_ref[h, :, :hd] = x_ref[:, 2 * FOX_WIDTH + h * hd:2 * FOX_WIDTH + (h + 1) * hd]
            vp_ref[h, :, hd:] = ones(V_ONE, V_ONE + 4).astype(BF16)
            kt_ref[h] = kp_ref[h].astype(F32).T.astype(BF16)
            vt_ref[h] = vp_ref[h].astype(F32).T.astype(BF16)

    row3 = pl.BlockSpec((FOX_HEADS, tm, LANES), lambda i: (0, i, 0))
    col3 = pl.BlockSpec((FOX_HEADS, LANES, tm), lambda i: (0, 0, i))
    return pl.pallas_call(
        body, grid=(t // tm,),
        in_specs=[pl.BlockSpec((tm, QKV), lambda i: (i, 0)), pl.BlockSpec((tm, FOX_HEADS), lambda i: (i, 0))],
        out_specs=[row3, row3, row3, col3, col3],
        out_shape=[_sds((FOX_HEADS, t, LANES), BF16)] * 3 + [_sds((FOX_HEADS, LANES, t), BF16)] * 2,
        compiler_params=_cp(), name="attn_pack")(qkv, c_col)


def _triangle(nq, key_major):
    if key_major:
        pairs = [(i, j) for j in range(nq) for i in range(j, nq)]
    else:
        pairs = [(i, j) for i in range(nq) for j in range(i + 1)]
    return jnp.asarray([p[0] for p in pairs], jnp.int32), jnp.asarray([p[1] for p in pairs], jnp.int32)


def _attn_fwd(qp, kp, vt):
    t = qp.shape[1]
    bq = min(ATT_BLOCK, t)
    nq = t // bq
    i_tab, j_tab = _triangle(nq, key_major=False)

    def body(it_ref, jt_ref, q_ref, k_ref, vt_ref, o_ref, lse_ref, m_sc, acc_sc):
        s = pl.program_id(1)
        i, j = it_ref[s], jt_ref[s]

        @pl.when(j == 0)
        def _():
            m_sc[...] = jnp.full(m_sc.shape, NEG, F32)
            acc_sc[...] = jnp.zeros(acc_sc.shape, F32)

        def sweep(masked):
            for h in range(2):
                st = lax.dot_general(k_ref[h], q_ref[h], NT, preferred_element_type=F32)
                if masked:
                    key = lax.broadcasted_iota(jnp.int32, (bq, bq), 0)
                    qry = lax.broadcasted_iota(jnp.int32, (bq, bq), 1)
                    st = jnp.where(key <= qry, st, NEG)
                m_prev = m_sc[h]
                m_new = jnp.maximum(m_prev, jnp.max(st, axis=0, keepdims=True))
                pt = jnp.exp(st - m_new).astype(BF16)
                acc_sc[h] = jnp.exp(m_prev - m_new) * acc_sc[h] + jnp.dot(vt_ref[h], pt, preferred_element_type=F32)
                m_sc[h] = m_new

        @pl.when(j < i)
        def _():
            sweep(False)

        @pl.when(j == i)
        def _():
            sweep(True)
            for h in range(2):
                acc = acc_sc[h]
                denom = acc[V_ONE:V_ONE + 1, :]
                o_ref[:, h * HEAD_DIM:(h + 1) * HEAD_DIM] = (acc[:HEAD_DIM, :] / denom).T.astype(BF16)
                lse_ref[h] = m_sc[h] + jnp.log(denom)

    grid_spec = pltpu.PrefetchScalarGridSpec(
        num_scalar_prefetch=2, grid=(HEAD_PAIRS, i_tab.shape[0]),
        in_specs=[pl.BlockSpec((2, bq, LANES), lambda hp, s, it, jt: (hp, it[s], 0)),
                  pl.BlockSpec((2, bq, LANES), lambda hp, s, it, jt: (hp, jt[s], 0)),
                  pl.BlockSpec((2, LANES, bq), lambda hp, s, it, jt: (hp, 0, jt[s]))],
        out_specs=[pl.BlockSpec((bq, LANES), lambda hp, s, it, jt: (it[s], hp)),
                   pl.BlockSpec((2, 1, bq), lambda hp, s, it, jt: (hp, 0, it[s]))],
        scratch_shapes=[pltpu.VMEM((2, 1, bq), F32), pltpu.VMEM((2, LANES, bq), F32)])
    return pl.pallas_call(body, grid_spec=grid_spec,
                          out_shape=[_sds((t, FOX_WIDTH), BF16), _sds((FOX_HEADS, 1, t), F32)],
                          compiler_params=_cp(), name="attn_fwd")(i_tab, j_tab, qp, kp, vt)


def _conv_fwd(bch, conv_w):
    t = bch.shape[0]
    tm = min(ROW_TILE, t)
    halo_blocks = tm // SUBLANES
    cw = CONV_WIDTH

    def body(cur_ref, prev_ref, w_ref, o_ref):
        i = pl.program_id(0)
        z = cur_ref[:, cw:2 * cw] * cur_ref[:, 2 * cw:]
        zp = jnp.where(i == 0, 0.0, prev_ref[:, cw:2 * cw] * prev_ref[:, 2 * cw:])
        z1, z2 = _shift_down(z, zp)
        y = w_ref[0:1, :] * z2 + w_ref[1:2, :] * z1 + w_ref[2:3, :] * z
        o_ref[...] = (cur_ref[:, :cw] * y).astype(BF16)

    return pl.pallas_call(
        body, grid=(t // tm,),
        in_specs=[pl.BlockSpec((tm, BCH), lambda i: (i, 0)),
                  pl.BlockSpec((SUBLANES, BCH), lambda i: (jnp.maximum(i * halo_blocks - 1, 0), 0)),
                  _resident(conv_w.shape)],
        out_specs=pl.BlockSpec((tm, cw), lambda i: (i, 0)),
        out_shape=_sds((t, cw), BF16), compiler_params=_cp(), name="conv_fwd")(bch, bch, conv_w)


def _mm_res_ln(pairs, res, g, b, name):
    t, d = res.shape
    tm = min(ROW_TILE, t)
    n = len(pairs)

    def body(*refs):
        a_refs, w_refs = refs[:n], refs[n:2 * n]
        res_ref, g_ref, b_ref, y_ref, xh_ref, rs_ref = refs[2 * n:]
        z = ALPHA * res_ref[...]
        for a_ref, w_ref in zip(a_refs, w_refs):
            z = z + jnp.dot(a_ref[...].astype(BF16), w_ref[...], preferred_element_type=F32)
        xhat, rstd = _ln_fwd(z)
        y_ref[...] = xhat * g_ref[...] + b_ref[...]
        xh_ref[...] = xhat
        rs_ref[...] = rstd

    row = lambda i: (i, 0)
    return pl.pallas_call(
        body, grid=(t // tm,),
        in_specs=[pl.BlockSpec((tm, a.shape[1]), row) for a, _ in pairs] + [_resident(w.shape) for _, w in pairs]
        + [pl.BlockSpec((tm, d), row), _resident(g.shape), _resident(b.shape)],
        out_specs=[pl.BlockSpec((tm, d), row), pl.BlockSpec((tm, d), row), pl.BlockSpec((tm, 1), row)],
        out_shape=[_sds((t, d), F32), _sds((t, d), F32), _sds((t, 1), F32)],
        compiler_params=_cp(), name=name)(*[a for a, _ in pairs], *[w for _, w in pairs], res, g, b)


def _ffn_in(x, wi, name):
    t, d = x.shape
    tm = min(FFN_ROW_TILE, t)
    hh = HALF_HIDDEN

    def body(x_ref, w_ref, gu_ref, h_ref):
        a = x_ref[...].astype(BF16)
        for c in range(2):
            gs, us = slice(c * hh, (c + 1) * hh), slice(FFN_HIDDEN + c * hh, FFN_HIDDEN + (c + 1) * hh)
            g = jnp.dot(a, w_ref[:, gs], preferred_element_type=F32)
            u = jnp.dot(a, w_ref[:, us], preferred_element_type=F32)
            gu_ref[:, gs] = g.astype(BF16)
            gu_ref[:, us] = u.astype(BF16)
            h_ref[:, gs] = (g * _sigmoid(g) * u).astype(BF16)

    row = lambda i: (i, 0)
    return pl.pallas_call(
        body, grid=(t // tm,),
        in_specs=[pl.BlockSpec((tm, d), row), _resident(wi.shape)],
        out_specs=[pl.BlockSpec((tm, 2 * FFN_HIDDEN), row), pl.BlockSpec((tm, FFN_HIDDEN), row)],
        out_shape=[_sds((t, 2 * FFN_HIDDEN), BF16), _sds((t, FFN_HIDDEN), BF16)],
        compiler_params=_cp(), name=name)(x, wi)


def _gmlp_fwd(x, w_in, vg, vb, wm, bs_col):
    t, d = x.shape
    tm = min(ROW_TILE, t)
    gb = GMLP_BLOCK

    def body(x_ref, w_ref, vg_ref, vb_ref, wm_ref, bs_ref, a_ref, o_ref):
        a = jnp.dot(x_ref[...].astype(BF16), w_ref[...], preferred_element_type=F32)
        a_ref[...] = a
        u = _gelu(a[:, :d])
        vhat, _ = _ln_fwd(_gelu(a[:, d:]))
        vln = (vhat * vg_ref[...] + vb_ref[...]).astype(BF16)
        for blk in range(tm // gb):
            rs = slice(blk * gb, (blk + 1) * gb)
            for gi in range(GMLP_GROUPS):
                cs = slice(gi * gb, (gi + 1) * gb)
                s = jnp.dot(wm_ref[gi], vln[rs, cs], preferred_element_type=F32) + bs_ref[:, gi:gi + 1]
                o_ref[rs, cs] = (u[rs, cs] * s).astype(BF16)

    row = lambda i: (i, 0)
    return pl.pallas_call(
        body, grid=(t // tm,),
        in_specs=[pl.BlockSpec((tm, d), row), _resident(w_in.shape), _resident(vg.shape), _resident(vb.shape),
                  _resident(wm.shape), _resident(bs_col.shape)],
        out_specs=[pl.BlockSpec((tm, 2 * d), row), pl.BlockSpec((tm, d), row)],
        out_shape=[_sds((t, 2 * d), F32), _sds((t, d), BF16)],
        compiler_params=_cp(), name="gmlp_fwd")(x, w_in, vg, vb, wm, bs_col)


def _loss_ln_bwd(xhat, rstd, g, b, target):
    t, d = xhat.shape
    tm = min(ROW_TILE, t)

    def body(xh_ref, rs_ref, g_ref, b_ref, t_ref, sq_ref, dz_ref, dg_ref, db_ref):
        first = pl.program_id(0) == 0
        xh = xh_ref[...]
        err = xh * g_ref[...] + b_ref[...] - t_ref[...]
        dz, dg, db = _ln_bwd(err * (1.0 / d), xh, rs_ref[...], g_ref[...])
        dz_ref[...] = dz
        _accumulate(sq_ref, first, jnp.sum(err * err, axis=0, keepdims=True))
        _accumulate(dg_ref, first, dg)
        _accumulate(db_ref, first, db)

    row = lambda i: (i, 0)
    vec = pl.BlockSpec((1, d), lambda i: (0, 0))
    return pl.pallas_call(
        body, grid=(t // tm,),
        in_specs=[pl.BlockSpec((tm, d), row), pl.BlockSpec((tm, 1), row), _resident(g.shape), _resident(b.shape),
                  pl.BlockSpec((tm, d), row)],
        out_specs=[vec, pl.BlockSpec((tm, d), row), vec, vec],
        out_shape=[_sds((1, d), F32), _sds((t, d), F32), _sds((1, d), F32), _sds((1, d), F32)],
        compiler_params=_cp(), name="loss_ln_bwd")(xhat, rstd, g, b, target)


def _mm_nt(pairs, ws, name, *, tm=ROW_TILE, res=None, ln=None, out_dtype=F32):
    t = pairs[0][0].shape[0]
    k = ws[0].shape[0]
    tm = min(tm, t)
    n, nw = len(pairs), len(ws)

    def body(*refs):
        a_refs, w_refs = refs[:n], refs[n:n + nw]
        rest = list(refs[n + nw:])
        dx = None
        for a_ref, (_, wi, lo, hi) in zip(a_refs, pairs):
            part = lax.dot_general(a_ref[...].astype(BF16), w_refs[wi][:, lo:hi], NT, preferred_element_type=F32)
            dx = part if dx is None else dx + part
        if res is not None:
            dx = dx + ALPHA * rest.pop(0)[...]
        if ln is None:
            rest[0][...] = dx.astype(out_dtype)
            return
        xh_ref, rs_ref, g_ref, dz_ref, dg_ref, db_ref = rest
        first = pl.program_id(0) == 0
        dz, dg, db = _ln_bwd(dx, xh_ref[...], rs_ref[...], g_ref[...])
        dz_ref[...] = dz
        _accumulate(dg_ref, first, dg)
        _accumulate(db_ref, first, db)

    row = lambda i: (i, 0)
    in_specs = [pl.BlockSpec((tm, a.shape[1]), row) for a, _, _, _ in pairs] + [_resident(w.shape) for w in ws]
    args = [a for a, _, _, _ in pairs] + list(ws)
    if res is not None:
        in_specs.append(pl.BlockSpec((tm, k), row))
        args.append(res)
    if ln is None:
        out_specs = pl.BlockSpec((tm, k), row)
        out_shape = _sds((t, k), out_dtype)
    else:
        xhat, rstd, g = ln
        in_specs += [pl.BlockSpec((tm, k), row), pl.BlockSpec((tm, 1), row), _resident(g.shape)]
        args += [xhat, rstd, g]
        vec = pl.BlockSpec((1, k), lambda i: (0, 0))
        out_specs = [pl.BlockSpec((tm, k), row), vec, vec]
        out_shape = [_sds((t, k), F32), _sds((1, k), F32), _sds((1, k), F32)]
    return pl.pallas_call(body, grid=(t // tm,), in_specs=in_specs, out_specs=out_specs, out_shape=out_shape,
                          compiler_params=_cp(), name=name)(*args)


def _mm_tn(a, b, name, *, tn, tk=None, stack_cols=False, out_dtype=BF16):
    t, k = a.shape
    n = b.shape[1]
    tk = k if tk is None else tk
    tt = min(ROW_TILE, t)
    nt = t // tt

    def body(a_ref, b_ref, o_ref, acc_ref):
        s = pl.program_id(2)
        part = lax.dot_general(a_ref[...].astype(BF16), b_ref[...].astype(BF16), TN, preferred_element_type=F32)
        _accumulate(acc_ref, s == 0, part)

        @pl.when(s == nt - 1)
        def _():
            o_ref[...] = acc_ref[...].astype(out_dtype).reshape(o_ref.shape)

    if stack_cols:
        assert tk == k
        out_spec = pl.BlockSpec((1, k, tn), lambda kk, j, s: (j, 0, 0))
        out_shape = _sds((n // tn, k, tn), out_dtype)
    else:
        out_spec = pl.BlockSpec((tk, tn), lambda kk, j, s: (kk, j))
        out_shape = _sds((k, n), out_dtype)
    return pl.pallas_call(
        body, grid=(k // tk, n // tn, nt),
        in_specs=[pl.BlockSpec((tt, tk), lambda kk, j, s: (s, kk)), pl.BlockSpec((tt, tn), lambda kk, j, s: (s, j))],
        out_specs=out_spec, out_shape=out_shape,
        scratch_shapes=[pltpu.VMEM((tk, tn), F32)],
        compiler_params=_cp(), name=name)(a, b)


def _ffn_bwd_hidden(dz, wo, gu, name):
    t, d = dz.shape
    tm = min(FFN_ROW_TILE, t)
    hh = HALF_HIDDEN

    def body(dz_ref, w_ref, gu_ref, o_ref):
        a = dz_ref[...].astype(BF16)
        for c in range(2):
            gs, us = slice(c * hh, (c + 1) * hh), slice(FFN_HIDDEN + c * hh, FFN_HIDDEN + (c + 1) * hh)
            dh = lax.dot_general(a, w_ref[gs, :], NT, preferred_element_type=F32)
            g = gu_ref[:, gs].astype(F32)
            u = gu_ref[:, us].astype(F32)
            sig = _sigmoid(g)
            o_ref[:, gs] = (dh * u * sig * (1.0 + g * (1.0 - sig))).astype(BF16)
            o_ref[:, us] = (dh * g * sig).astype(BF16)

    row = lambda i: (i, 0)
    return pl.pallas_call(
        body, grid=(t // tm,),
        in_specs=[pl.BlockSpec((tm, d), row), _resident(wo.shape), pl.BlockSpec((tm, 2 * FFN_HIDDEN), row)],
        out_specs=pl.BlockSpec((tm, 2 * FFN_HIDDEN), row),
        out_shape=_sds((t, 2 * FFN_HIDDEN), BF16), compiler_params=_cp(), name=name)(dz, wo, gu)


def _gmlp_bwd(dgated, a, vg, vb, wm, bs_col):
    t, d2 = a.shape
    d = d2 // 2
    tm = min(ROW_TILE, t)
    gb = GMLP_BLOCK

    def body(dg_ref, a_ref, vg_ref, vb_ref, wm_ref, bs_ref, da_ref, dws_ref, dbs_ref, dvg_ref, dvb_ref, dvln_sc):
        first = pl.program_id(0) == 0
        au, av = a_ref[:, :d], a_ref[:, d:]
        u = _gelu(au)
        vhat, rstd = _ln_fwd(_gelu(av))
        vln = (vhat * vg_ref[...] + vb_ref[...]).astype(BF16)
        dgate = dg_ref[...]

        @pl.when(first)
        def _():
            dws_ref[...] = jnp.zeros(dws_ref.shape, F32)
            dbs_ref[...] = jnp.zeros(dbs_ref.shape, F32)

        for blk in range(tm // gb):
            rs = slice(blk * gb, (blk + 1) * gb)
            for gi in range(GMLP_GROUPS):
                cs = slice(gi * gb, (gi + 1) * gb)
                vblk = vln[rs, cs]
                s = jnp.dot(wm_ref[gi], vblk, preferred_element_type=F32) + bs_ref[:, gi:gi + 1]
                dgb = dgate[rs, cs]
                da_ref[rs, cs] = (dgb * s * _gelu_grad(au[rs, cs])).astype(BF16)
                ds = dgb * u[rs, cs]
                dsb = ds.astype(BF16)
                dws_ref[gi] += lax.dot_general(dsb, vblk, NT, preferred_element_type=F32)
                dbs_ref[:, gi:gi + 1] += jnp.sum(ds, axis=1, keepdims=True)
                dvln_sc[rs, cs] = lax.dot_general(wm_ref[gi], dsb, TN, preferred_element_type=F32)
        dv, dvg, dvb = _ln_bwd(dvln_sc[...], vhat, rstd, vg_ref[...])
        da_ref[:, d:] = (dv * _gelu_grad(av)).astype(BF16)
        _accumulate(dvg_ref, first, dvg)
        _accumulate(dvb_ref, first, dvb)

    row = lambda i: (i, 0)
    vec = pl.BlockSpec((1, d), lambda i: (0, 0))
    return pl.pallas_call(
        body, grid=(t // tm,),
        in_specs=[pl.BlockSpec((tm, d), row), pl.BlockSpec((tm, d2), row), _resident(vg.shape), _resident(vb.shape),
                  _resident(wm.shape), _resident(bs_col.shape)],
        out_specs=[pl.BlockSpec((tm, d2), row), pl.BlockSpec(wm.shape, lambda i: (0, 0, 0)),
                   pl.BlockSpec(bs_col.shape, lambda i: (0, 0)), vec, vec],
        out_shape=[_sds((t, d2), BF16), _sds(wm.shape, F32), _sds(bs_col.shape, F32), _sds((1, d), F32), _sds((1, d), F32)],
        scratch_shapes=[pltpu.VMEM((tm, d), F32)],
        compiler_params=_cp(), name="gmlp_bwd")(dgated, a, vg, vb, wm, bs_col)


def _conv_bwd(bch, dmix, conv_w):
    t = bch.shape[0]
    tm = min(ROW_TILE, t)
    nb = t // tm
    halo_blocks = tm // SUBLANES
    cw = CONV_WIDTH

    def body(cur_ref, prev_ref, next_ref, dc_ref, dn_ref, w_ref, o_ref, dw_ref):
        i = pl.program_id(0)
        bgate, cgate, hval = cur_ref[:, :cw], cur_ref[:, cw:2 * cw], cur_ref[:, 2 * cw:]
        z = cgate * hval
        zp = jnp.where(i == 0, 0.0, prev_ref[:, cw:2 * cw] * prev_ref[:, 2 * cw:])
        z1, z2 = _shift_down(z, zp)
        w0, w1, w2 = w_ref[0:1, :], w_ref[1:2, :], w_ref[2:3, :]
        dconv = dc_ref[...]
        o_ref[:, :cw] = (dconv * (w0 * z2 + w1 * z1 + w2 * z)).astype(BF16)
        dy = dconv * bgate
        dyn = jnp.where(i == nb - 1, 0.0, dn_ref[...] * next_ref[:, :cw])
        dy1, dy2 = _shift_up(dy, dyn)
        dz = w2 * dy + w1 * dy1 + w0 * dy2
        o_ref[:, cw:2 * cw] = (dz * hval).astype(BF16)
        o_ref[:, 2 * cw:] = (dz * cgate).astype(BF16)

        @pl.when(i == 0)
        def _():
            dw_ref[...] = jnp.zeros(dw_ref.shape, F32)

        for tap, zs in enumerate((z2, z1, z)):
            dw_ref[tap:tap + 1, :] += jnp.sum(dy * zs, axis=0, keepdims=True)

    last_halo = t // SUBLANES - 1
    return pl.pallas_call(
        body, grid=(nb,),
        in_specs=[pl.BlockSpec((tm, BCH), lambda i: (i, 0)),
                  pl.BlockSpec((SUBLANES, BCH), lambda i: (jnp.maximum(i * halo_blocks - 1, 0), 0)),
                  pl.BlockSpec((SUBLANES, BCH), lambda i: (jnp.minimum((i + 1) * halo_blocks, last_halo), 0)),
                  pl.BlockSpec((tm, cw), lambda i: (i, 1)),
                  pl.BlockSpec((SUBLANES, cw), lambda i: (jnp.minimum((i + 1) * halo_blocks, last_halo), 1)),
                  _resident(conv_w.shape)],
        out_specs=[pl.BlockSpec((tm, BCH), lambda i: (i, 0)), pl.BlockSpec((SUBLANES, cw), lambda i: (0, 0))],
        out_shape=[_sds((t, BCH), BF16), _sds((SUBLANES, cw), F32)],
        compiler_params=_cp(), name="conv_bwd")(bch, bch, bch, dmix, dmix, conv_w)


def _attn_bwd_prep(o, dmix, qp, lse_col):
    t = o.shape[0]
    tm = min(ROW_TILE, t)
    hd = HEAD_DIM

    def body(o_ref, do_ref, qp_ref, lse_ref, qb_ref, dob_ref):
        lane = lax.broadcasted_iota(jnp.int32, (tm, hd), 1) + hd
        for h in range(FOX_HEADS):
            do = do_ref[:, h * hd:(h + 1) * hd]
            delta = jnp.sum(o_ref[:, h * hd:(h + 1) * hd].astype(F32) * do, axis=-1, keepdims=True)
            dob_ref[h, :, :hd] = do.astype(BF16)
            dob_ref[h, :, hd:] = _lane_pieces(lane, DO_DELTA, _split3(delta), -1.0).astype(BF16)
            qb_ref[h, :, :hd] = qp_ref[h, :, :hd]
            qb_ref[h, :, hd:] = (qp_ref[h, :, hd:].astype(F32)
                                 + _lane_pieces(lane, Q_LSE, _split3(lse_ref[:, h:h + 1]), -1.0)).astype(BF16)

    row3 = pl.BlockSpec((FOX_HEADS, tm, LANES), lambda i: (0, i, 0))
    return pl.pallas_call(
        body, grid=(t // tm,),
        in_specs=[pl.BlockSpec((tm, FOX_WIDTH), lambda i: (i, 0)), pl.BlockSpec((tm, FOX_WIDTH), lambda i: (i, 0)), row3,
                  pl.BlockSpec((tm, FOX_HEADS), lambda i: (i, 0))],
        out_specs=[row3, row3], out_shape=[_sds((FOX_HEADS, t, LANES), BF16)] * 2,
        compiler_params=_cp(), name="attn_bwd_prep")(o, dmix, qp, lse_col)


def _attn_bwd(qb, kp, vp, dob, kt):
    t = qb.shape[1]
    bq = min(ATT_BLOCK, t)
    nq = t // bq
    i_tab, j_tab = _triangle(nq, key_major=True)

    def body(it_ref, jt_ref, q_ref, k_ref, v_ref, do_ref, kt_ref, dqt_ref, dk_ref, dv_ref, dk_sc, dv_sc):
        s = pl.program_id(1)
        i, j = it_ref[s], jt_ref[s]

        @pl.when(s == 0)
        def _():
            dqt_ref[...] = jnp.zeros(dqt_ref.shape, F32)

        @pl.when(i == j)
        def _():
            dk_sc[...] = jnp.zeros(dk_sc.shape, F32)
            dv_sc[...] = jnp.zeros(dv_sc.shape, F32)

        cols = pl.ds(pl.multiple_of(i * bq, bq), bq)

        def sweep(masked):
            for h in range(2):
                st = lax.dot_general(k_ref[h], q_ref[h], NT, preferred_element_type=F32)
                if masked:
                    key = lax.broadcasted_iota(jnp.int32, (bq, bq), 0)
                    qry = lax.broadcasted_iota(jnp.int32, (bq, bq), 1)
                    st = jnp.where(key <= qry, st, NEG)
                pt = jnp.exp(st)
                dst = pt * lax.dot_general(v_ref[h], do_ref[h], NT, preferred_element_type=F32)
                ptb, dstb = pt.astype(BF16), dst.astype(BF16)
                dv_sc[h] += jnp.dot(ptb, do_ref[h], preferred_element_type=F32)
                dk_sc[h] += jnp.dot(dstb, q_ref[h], preferred_element_type=F32)
                dqt_ref[h, :, cols] += jnp.dot(kt_ref[h], dstb, preferred_element_type=F32)

        @pl.when(i == j)
        def _():
            sweep(True)

        @pl.when(i > j)
        def _():
            sweep(False)

        @pl.when(i == nq - 1)
        def _():
            dk_ref[...] = dk_sc[...]
            dv_ref[...] = dv_sc[...].astype(BF16)

    qblk = pl.BlockSpec((2, bq, LANES), lambda hp, s, it, jt: (hp, it[s], 0))
    kblk = pl.BlockSpec((2, bq, LANES), lambda hp, s, it, jt: (hp, jt[s], 0))
    grid_spec = pltpu.PrefetchScalarGridSpec(
        num_scalar_prefetch=2, grid=(HEAD_PAIRS, i_tab.shape[0]),
        in_specs=[qblk, kblk, kblk, qblk, pl.BlockSpec((2, LANES, bq), lambda hp, s, it, jt: (hp, 0, jt[s]))],
        out_specs=[pl.BlockSpec((2, LANES, t), lambda hp, s, it, jt: (hp, 0, 0)), kblk, kblk],
        scratch_shapes=[pltpu.VMEM((2, bq, LANES), F32), pltpu.VMEM((2, bq, LANES), F32)])
    return pl.pallas_call(body, grid_spec=grid_spec,
                          out_shape=[_sds((FOX_HEADS, LANES, t), F32), _sds((FOX_HEADS, t, LANES), F32),
                                     _sds((FOX_HEADS, t, LANES), BF16)],
                          compiler_params=_cp(), name="attn_bwd")(i_tab, j_tab, qb, kp, vp, dob, kt)


def _attn_unpack(dqt, dkp, dvp):
    t = dkp.shape[1]
    tm = min(ROW_TILE, t)
    hd = HEAD_DIM

    def body(dqt_ref, dk_ref, dv_ref, o_ref, dc_ref):
        for h in range(FOX_HEADS):
            dq = dqt_ref[h].T
            o_ref[:, h * hd:(h + 1) * hd] = (dq[:, :hd] * (hd ** -0.5)).astype(BF16)
            o_ref[:, FOX_WIDTH + h * hd:FOX_WIDTH + (h + 1) * hd] = dk_ref[h, :, :hd].astype(BF16)
            o_ref[:, 2 * FOX_WIDTH + h * hd:2 * FOX_WIDTH + (h + 1) * hd] = dv_ref[h, :, :hd]
            dc_ref[:, h:h + 1] = dq[:, K_ONE:K_ONE + 1] - dk_ref[h, :, Q_ONE:Q_ONE + 1]

    row3 = pl.BlockSpec((FOX_HEADS, tm, LANES), lambda i: (0, i, 0))
    return pl.pallas_call(
        body, grid=(t // tm,),
        in_specs=[pl.BlockSpec((FOX_HEADS, LANES, tm), lambda i: (0, 0, i)), row3, row3],
        out_specs=[pl.BlockSpec((tm, QKV), lambda i: (i, 0)), pl.BlockSpec((tm, FOX_HEADS), lambda i: (i, 0))],
        out_shape=[_sds((t, QKV), BF16), _sds((t, FOX_HEADS), F32)],
        compiler_params=_cp(), name="attn_unpack")(dqt, dkp, dvp)


def _adamw(parts, w, m, v, name):
    nl, r, c = w.shape
    tr = r
    for cand in (256, 128, 64, 32, 16):
        if r > cand and r % cand == 0:
            tr = cand
            break
    npart = len(parts)
    bc1 = 1.0 - ADAM_B1 ** ADAM_STEP
    bc2 = 1.0 - ADAM_B2 ** ADAM_STEP

    def body(*refs):
        p_refs = refs[:npart]
        w_ref, m_ref, v_ref, g_ref, d_ref, nm_ref, nv_ref = refs[npart:]
        sums = []
        for p_ref in p_refs:
            acc = p_ref[0, 0].astype(F32)
            for s in range(1, p_ref.shape[0]):
                acc = acc + p_ref[s, 0].astype(F32)
            sums.append(acc)
        g = sums[0]
        for extra in sums[1:]:
            g = g + extra
        nm = ADAM_B1 * m_ref[0] + (1.0 - ADAM_B1) * g
        nv = ADAM_B2 * v_ref[0] + (1.0 - ADAM_B2) * (g * g)
        m_hat = nm / bc1
        v_hat = nv / bc2
        g_ref[0] = g
        d_ref[0] = -ADAM_LR * (m_hat / (jnp.sqrt(v_hat) + ADAM_EPS) + ADAM_WD * w_ref[0])
        nm_ref[0] = nm
        nv_ref[0] = nv

    blk = pl.BlockSpec((1, tr, c), lambda l, i: (l, i, 0))
    return pl.pallas_call(
        body, grid=(nl, r // tr),
        in_specs=[pl.BlockSpec((p.shape[0], 1, tr, c), lambda l, i: (0, l, i, 0)) for p in parts] + [blk, blk, blk],
        out_specs=[blk] * 4, out_shape=[_sds(w.shape, F32)] * 4,
        compiler_params=_cp(), name=name)(*parts, w, m, v)


def _to_rows(a):
    flat = a.reshape(-1)
    pad = (-flat.shape[0]) % LANES
    if pad:
        flat = jnp.concatenate([flat, jnp.zeros((pad,), flat.dtype)])
    return flat.reshape(-1, LANES)


def _by_owner_cols(dw):
    k, n = dw.shape
    return dw.reshape(k, N_CHIPS, n // N_CHIPS).transpose(1, 0, 2)[:, None]


def _ffn_fwd(xin, wi, wo, g, b, layer):
    gu, h = _ffn_in(xin, wi, f"ffn_in_{layer}")
    y, xhat, rstd = _mm_res_ln([(h, wo)], xin, g, b, f"ffn_out_ln_{layer}")
    return y, (xin, gu, h, xhat, rstd)


def _ffn_bwd(dz, saved, wi, wo, ln_below, layer):
    xin, gu, h, _, _ = saved
    dgu = _ffn_bwd_hidden(dz, wo, gu, f"ffn_bwd_hidden_{layer}")
    g_out = _mm_tn(h, dz, f"ffn_dw_out_{layer}", tn=D_MODEL, tk=HALF_HIDDEN)
    g_in = _mm_tn(xin, dgu, f"ffn_dw_in_{layer}", tn=HALF_HIDDEN, stack_cols=True)
    below = _mm_nt([(dgu, 0, 0, 2 * FFN_HIDDEN)], [wi], f"ffn_dx_{layer}", tm=FFN_ROW_TILE, res=dz, ln=ln_below)
    return below, g_in, g_out.reshape(N_CHIPS, FFN_HIDDEN // N_CHIPS, D_MODEL)


def kernel(x, even_w_in, even_b_f, even_conv_w, even_w_out, odd_w_in, odd_v_ln_g, odd_v_ln_b, odd_w_s, odd_b_s, odd_w_out, mix_ln_g, mix_ln_b, ffn_w_in, ffn_w_out, ffn_ln_g, ffn_ln_b, loss_target, m_even_w_in, m_even_b_f, m_even_conv_w, m_even_w_out, m_odd_w_in, m_odd_v_ln_g, m_odd_v_ln_b, m_odd_w_s, m_odd_b_s, m_odd_w_out, m_mix_ln_g, m_mix_ln_b, m_ffn_w_in, m_ffn_w_out, m_ffn_ln_g, m_ffn_ln_b, v_even_w_in, v_even_b_f, v_even_conv_w, v_even_w_out, v_odd_w_in, v_odd_v_ln_g, v_odd_v_ln_b, v_odd_w_s, v_odd_b_s, v_odd_w_out, v_mix_ln_g, v_mix_ln_b, v_ffn_w_in, v_ffn_w_out, v_ffn_ln_g, v_ffn_ln_b):
    t = x.shape[1]
    d = D_MODEL
    chip = 2 * lax.axis_index("x") + lax.axis_index("y")
    x2d = x[0]
    target = loss_target[0]

    small_shard = jnp.concatenate([odd_v_ln_g.reshape(2, LANES), odd_v_ln_b.reshape(2, LANES),
                                   even_conv_w.reshape(CONV_K, LANES), jnp.zeros((1, LANES), F32)], axis=0)
    g_ewi, g_ewo, g_owi, g_owo, g_fwi, g_fwo, g_small = _exchange(
        [even_w_in[0].astype(BF16), even_w_out[0].astype(BF16), odd_w_in[0].astype(BF16), odd_w_out[0].astype(BF16),
         ffn_w_in.astype(BF16), ffn_w_out.astype(BF16), small_shard], "gather4", "gather_weights")
    ewi = g_ewi.transpose(1, 0, 2).reshape(d, EVEN_IN)
    w_even_in = jnp.concatenate([ewi[:, :QKV], ewi[:, QKV + FOX_HEADS:], ewi[:, QKV:QKV + FOX_HEADS],
                                 jnp.zeros((d, LANES - FOX_HEADS), BF16)], axis=1)
    w_even_out = g_ewo.reshape(d, d)
    w_odd_in = g_owi.transpose(1, 0, 2).reshape(d, 2 * d)
    w_odd_out = g_owo.reshape(d, d)
    w_ffn_in = [g_fwi[:, l].transpose(1, 0, 2).reshape(d, 2 * FFN_HIDDEN) for l in range(2)]
    w_ffn_out = [g_fwo[:, l].reshape(FFN_HIDDEN, d) for l in range(2)]
    v_ln_g = g_small[:, 0:2].reshape(1, d)
    v_ln_b = g_small[:, 2:4].reshape(1, d)
    conv_w = g_small[:, 4:7].transpose(1, 0, 2).reshape(CONV_K, CONV_WIDTH)
    chunk_id = jnp.arange(GMLP_BLOCK) // CHUNK
    gmask = chunk_id[None, :] <= chunk_id[:, None]
    w_spatial = jnp.where(gmask[None], odd_w_s[0], 0.0).astype(BF16)
    bs_col = odd_b_s[0].T
    b_f_col = even_b_f.reshape(FOX_HEADS, 1)
    ln = lambda p, l: p[l:l + 1]

    qkv, bch, fl = _proj(x2d, w_even_in, [(0, QKV, BF16), (QKV, QKV + BCH, F32), (QKV + BCH, EVEN_IN_PAD, F32)], "even_proj")
    fl3 = fl[:, :FOX_HEADS].T.reshape(FOX_HEADS, t // LANES, LANES).transpose(1, 0, 2)
    c3 = _fgate_fwd(fl3, b_f_col)
    c_rows = c3.transpose(1, 0, 2).reshape(FOX_HEADS, t)
    qp, kp, vp, kt, vt = _attn_pack(qkv, c_rows.T)
    attn, lse = _attn_fwd(qp, kp, vt)
    conv = _conv_fwd(bch, conv_w)
    x1, xh1, rs1 = _mm_res_ln([(attn, w_even_out[:FOX_WIDTH]), (conv, w_even_out[FOX_WIDTH:])], x2d,
                              ln(mix_ln_g, 0), ln(mix_ln_b, 0), "even_out_ln")
    x2, ffn0 = _ffn_fwd(x1, w_ffn_in[0], w_ffn_out[0], ln(ffn_ln_g, 0), ln(ffn_ln_b, 0), 0)

    a_odd, gated = _gmlp_fwd(x2, w_odd_in, v_ln_g, v_ln_b, w_spatial, bs_col)
    x3, xh3, rs3 = _mm_res_ln([(gated, w_odd_out)], x2, ln(mix_ln_g, 1), ln(mix_ln_b, 1), "odd_out_ln")
    _, ffn1 = _ffn_fwd(x3, w_ffn_in[1], w_ffn_out[1], ln(ffn_ln_g, 1), ln(ffn_ln_b, 1), 1)

    sq, dz4, d_fg1, d_fb1 = _loss_ln_bwd(ffn1[3], ffn1[4], ln(ffn_ln_g, 1), ln(ffn_ln_b, 1), target)
    loss = lax.psum(0.5 / d * jnp.sum(sq), ("x", "y", "c"))
    (dz3, d_mg1, d_mb1), gi_f1, go_f1 = _ffn_bwd(dz4, ffn1, w_ffn_in[1], w_ffn_out[1], (xh3, rs3, ln(mix_ln_g, 1)), 1)

    dgated = _mm_nt([(dz3, 0, 0, d)], [w_odd_out], "odd_dgated")
    go_odd = _mm_tn(gated, dz3, "odd_dw_out", tn=d).reshape(N_CHIPS, 1, d // N_CHIPS, d)
    da_odd, dws, dbs_col, d_vg, d_vb = _gmlp_bwd(dgated, a_odd, v_ln_g, v_ln_b, w_spatial, bs_col)
    gi_odd = _by_owner_cols(_mm_tn(x2, da_odd, "odd_dw_in", tn=d))
    dz2, d_fg0, d_fb0 = _mm_nt([(da_odd, 0, 0, 2 * d)], [w_odd_in], "odd_dx", res=dz3,
                               ln=(ffn0[3], ffn0[4], ln(ffn_ln_g, 0)))
    (dz1, d_mg0, d_mb0), gi_f0, go_f0 = _ffn_bwd(dz2, ffn0, w_ffn_in[0], w_ffn_out[0], (xh1, rs1, ln(mix_ln_g, 0)), 0)

    dmix = _mm_nt([(dz1, 0, 0, d)], [w_even_out], "even_dmix")
    mix = jnp.concatenate([attn, conv], axis=1)
    go_even = _mm_tn(mix, dz1, "even_dw_out", tn=d).reshape(N_CHIPS, 1, d // N_CHIPS, d)
    dbch, dconv_w8 = _conv_bwd(bch, dmix, conv_w)
    qb, dob = _attn_bwd_prep(attn, dmix, qp, lse.reshape(FOX_HEADS, t).T)
    dqkv, dc_col = _attn_unpack(*_attn_bwd(qb, kp, vp, dob, kt))
    dc3 = dc_col.T.reshape(FOX_HEADS, t // LANES, LANES).transpose(1, 0, 2)
    dfl3, d_bf = _fgate_bwd(dc3, fl3, b_f_col)
    dfl = jnp.concatenate([dfl3.transpose(1, 0, 2).reshape(FOX_HEADS, t).T.astype(BF16),
                           jnp.zeros((t, LANES - FOX_HEADS), BF16)], axis=1)
    grad_x = _mm_nt([(dqkv, 0, 0, QKV), (dbch, 0, QKV, QKV + BCH), (dfl, 0, QKV + BCH, EVEN_IN_PAD)], [w_even_in],
                    "even_dx", res=dz1)
    dw_qkv = _mm_tn(x2d, dqkv, "even_dw_qkv", tn=QKV // 2, out_dtype=F32)
    dw_bch = _mm_tn(x2d, dbch, "even_dw_bch", tn=BCH // 2, out_dtype=F32)
    dw_f = _mm_tn(x2d, dfl, "even_dw_f", tn=LANES, out_dtype=F32)
    gi_even = _by_owner_cols(jnp.concatenate([dw_qkv, dw_f[:, :FOX_HEADS], dw_bch], axis=1).astype(BF16))

    big = [gi_even, go_even, gi_odd, go_odd, jnp.stack([gi_f0, gi_f1], axis=1), jnp.stack([go_f0, go_f1], axis=1)]
    mine = _exchange(big, "scatter4", "scatter_grads")
    theirs = _exchange(mine, "swap2", "swap_grads")
    big_w = [(even_w_in, m_even_w_in, v_even_w_in), (even_w_out, m_even_w_out, v_even_w_out),
             (odd_w_in, m_odd_w_in, v_odd_w_in), (odd_w_out, m_odd_w_out, v_odd_w_out),
             (ffn_w_in, m_ffn_w_in, v_ffn_w_in), (ffn_w_out, m_ffn_w_out, v_ffn_w_out)]
    big_names = ["even_w_in", "even_w_out", "odd_w_in", "odd_w_out", "ffn_w_in", "ffn_w_out"]
    res = {}
    for nm, own, sib, (w, m, v) in zip(big_names, mine, theirs, big_w):
        res[nm] = _adamw([own, sib], w, m, v, f"adamw_{nm}")

    dws_masked = jnp.where(gmask[None], dws, 0.0)
    rep_names = ["odd_w_s", "odd_b_s", "mix_ln_g", "mix_ln_b", "ffn_ln_g", "ffn_ln_b", "even_b_f"]
    rep_grads = [dws_masked, dbs_col.T, jnp.concatenate([d_mg0, d_mg1]), jnp.concatenate([d_mb0, d_mb1]),
                 jnp.concatenate([d_fg0, d_fg1]), jnp.concatenate([d_fb0, d_fb1]), d_bf.reshape(1, FOX_HEADS)]
    rep_w = [(odd_w_s, m_odd_w_s, v_odd_w_s), (odd_b_s, m_odd_b_s, v_odd_b_s), (mix_ln_g, m_mix_ln_g, v_mix_ln_g),
             (mix_ln_b, m_mix_ln_b, v_mix_ln_b), (ffn_ln_g, m_ffn_ln_g, v_ffn_ln_g), (ffn_ln_b, m_ffn_ln_b, v_ffn_ln_b),
             (even_b_f, m_even_b_f, v_even_b_f)]
    rep_rows = [_to_rows(gr) for gr in rep_grads]
    n_rep = sum(r.shape[0] for r in rep_rows)
    pad_rep = (-n_rep) % SUBLANES
    dconv_w = dconv_w8[:CONV_K].reshape(CONV_K, N_CHIPS, LANES).transpose(1, 0, 2).reshape(N_CHIPS * CONV_K, LANES)
    packed = jnp.concatenate(rep_rows + [jnp.zeros((pad_rep, LANES), F32), d_vg.reshape(SUBLANES, LANES),
                                         d_vb.reshape(SUBLANES, LANES), dconv_w, jnp.zeros((4, LANES), F32)], axis=0)
    (gathered,) = _exchange([packed], "gather8", "gather_small_grads")
    base = n_rep + pad_rep
    own_rows = jnp.concatenate([
        lax.dynamic_slice_in_dim(gathered, base + 2 * chip, 2, axis=1),
        lax.dynamic_slice_in_dim(gathered, base + SUBLANES + 2 * chip, 2, axis=1),
        lax.dynamic_slice_in_dim(gathered, base + 2 * SUBLANES + CONV_K * chip, CONV_K, axis=1),
        jnp.zeros((N_DEV, 1, LANES), F32)], axis=1)
    small_parts = jnp.concatenate([gathered[:, :base], own_rows], axis=1)[:, None]

    def pack_small(get):
        rows = [_to_rows(get(tw)) for tw in rep_w] + [jnp.zeros((pad_rep, LANES), F32)]
        rows += [get(sh).reshape(-1, LANES) for sh in ((odd_v_ln_g, m_odd_v_ln_g, v_odd_v_ln_g),
                                                       (odd_v_ln_b, m_odd_v_ln_b, v_odd_v_ln_b),
                                                       (even_conv_w, m_even_conv_w, v_even_conv_w))]
        return jnp.concatenate(rows + [jnp.zeros((1, LANES), F32)], axis=0)[None]

    small_out = _adamw([small_parts], pack_small(lambda tw: tw[0]), pack_small(lambda tw: tw[1]),
                       pack_small(lambda tw: tw[2]), "adamw_small")

    def unpack_small(rows3):
        rows = rows3[0]
        out, off = {}, 0
        for nm, (w, _, _), r in zip(rep_names, rep_w, rep_rows):
            out[nm] = rows[off:off + r.shape[0]].reshape(-1)[:w.size].reshape(w.shape)
            off += r.shape[0]
        off += pad_rep
        out["odd_v_ln_g"] = rows[off:off + 2].reshape(odd_v_ln_g.shape)
        out["odd_v_ln_b"] = rows[off + 2:off + 4].reshape(odd_v_ln_b.shape)
        out["even_conv_w"] = rows[off + 4:off + 4 + CONV_K].reshape(even_conv_w.shape)
        return out

    small = [unpack_small(o) for o in small_out]
    order = ["even_w_in", "even_b_f", "even_conv_w", "even_w_out", "odd_w_in", "odd_v_ln_g", "odd_v_ln_b", "odd_w_s",
             "odd_b_s", "odd_w_out", "mix_ln_g", "mix_ln_b", "ffn_w_in", "ffn_w_out", "ffn_ln_g", "ffn_ln_b"]
    outs = [loss, grad_x[None]]
    for kind in range(4):
        for nm in order:
            outs.append(res[nm][kind] if nm in res else small[kind][nm])
    return tuple(outs)
```

```python
import functools
import math

import jax
import jax.numpy as jnp
from jax import lax
from jax.experimental import pallas as pl
from jax.experimental.pallas import tpu as pltpu

F32 = jnp.float32
BF16 = jnp.bfloat16

D_MODEL = 1024
FOX_HEADS = 8
HEAD_DIM = 64
HEAD_PAIRS = FOX_HEADS // 2
FOX_WIDTH = FOX_HEADS * HEAD_DIM
CONV_WIDTH = 512
CONV_K = 3
QKV = 3 * FOX_WIDTH
BCH = 3 * CONV_WIDTH
EVEN_IN = QKV + FOX_HEADS + BCH
EVEN_IN_PAD = QKV + BCH + 128
GMLP_BLOCK = 128
GMLP_GROUPS = 8
CHUNK = 64
FFN_HIDDEN = 2816
HALF_HIDDEN = FFN_HIDDEN // 2
ALPHA = 4.0 ** 0.25
LN_EPS = 1e-5
ADAM_LR = 0.001
ADAM_B1 = 0.9
ADAM_B2 = 0.999
ADAM_EPS = 1e-08
ADAM_WD = 0.01
ADAM_STEP = 10
N_CHIPS = 4
N_DEV = 8
LANES = 128
SUBLANES = 8
ROW_TILE = 512
FFN_ROW_TILE = 256
ATT_BLOCK = 512
VMEM_LIMIT = 56 * 2 ** 20
NEG = -1e30
MESH = pl.DeviceIdType.MESH
HIGHEST = lax.Precision.HIGHEST
Q_C, Q_ONE, Q_LSE = 64, 67, 70
K_ONE, K_C, K_ONE2 = 64, 67, 70
V_ONE = 64
DO_DELTA = 65
NT = (((1,), (1,)), ((), ()))
TN = (((0,), (0,)), ((), ()))


def _cp():
    return pltpu.CompilerParams(vmem_limit_bytes=VMEM_LIMIT)


def _resident(shape):
    zeros = (0,) * len(shape)
    return pl.BlockSpec(shape, lambda *_: zeros, pipeline_mode=pl.Buffered(1))


def _sds(shape, dtype):
    return jax.ShapeDtypeStruct(tuple(shape), dtype)


_MASKS = {
    "gather4": [(1, 0, 0), (0, 1, 0), (1, 1, 0)],
    "scatter4": [(1, 0, 0), (0, 1, 0), (1, 1, 0)],
    "swap2": [(0, 0, 1)],
    "gather8": [(0, 0, 1), (0, 1, 0), (0, 1, 1), (1, 0, 0), (1, 0, 1), (1, 1, 0), (1, 1, 1)],
}


def _exchange(arrs, mode, name):
    n = len(arrs)
    masks = _MASKS[mode]
    npeer = len(masks)
    lead = {"gather4": N_CHIPS, "gather8": N_DEV}.get(mode)
    out_shapes = [_sds(((lead,) if lead else ()) + a.shape, a.dtype) for a in arrs]

    def body(*refs):
        ins, outs = refs[:n], refs[n:2 * n]
        send_sems, recv_sems, loc_sems = refs[2 * n:]
        x, y, c = lax.axis_index("x"), lax.axis_index("y"), lax.axis_index("c")
        chip, dev = 2 * x + y, 4 * x + 2 * y + c
        sends, recvs, locs = [], [], []
        for k in range(n):
            if mode == "gather4":
                locs.append(pltpu.make_async_copy(ins[k], outs[k].at[chip], loc_sems.at[k]))
            elif mode == "scatter4":
                locs.append(pltpu.make_async_copy(ins[k].at[chip], outs[k].at[chip], loc_sems.at[k]))
            elif mode == "gather8":
                locs.append(pltpu.make_async_copy(ins[k], outs[k].at[dev], loc_sems.at[k]))
        for cp in locs:
            cp.start()
        for k in range(n):
            for j, (dx, dy, dc) in enumerate(masks):
                px = 1 - x if dx else x
                py = 1 - y if dy else y
                pc = 1 - c if dc else c
                pchip, pdev = 2 * px + py, 4 * px + 2 * py + pc
                if mode == "gather4":
                    src, dst, land = ins[k], outs[k].at[chip], outs[k].at[pchip]
                elif mode == "scatter4":
                    src, dst, land = ins[k].at[pchip], outs[k].at[chip], outs[k].at[pchip]
                elif mode == "swap2":
                    src, dst, land = ins[k], outs[k], outs[k]
                else:
                    src, dst, land = ins[k], outs[k].at[dev], outs[k].at[pdev]
                s = k * npeer + j
                kw = dict(send_sem=send_sems.at[s], recv_sem=recv_sems.at[s], device_id=(px, py, pc),
                          device_id_type=MESH)
                cp = pltpu.make_async_remote_copy(src_ref=src, dst_ref=dst, **kw)
                cp.start()
                sends.append(cp)
                recvs.append(pltpu.make_async_remote_copy(src_ref=src, dst_ref=land, **kw))
        for cp in recvs:
            cp.wait_recv()
        for cp in sends:
            cp.wait_send()
        for cp in locs:
            cp.wait()

    any_spec = pl.BlockSpec(memory_space=pl.ANY)
    outs = pl.pallas_call(
        body,
        out_shape=out_shapes,
        in_specs=[any_spec] * n,
        out_specs=[any_spec] * n,
        scratch_shapes=[pltpu.SemaphoreType.DMA((n * npeer,)), pltpu.SemaphoreType.DMA((n * npeer,)),
                        pltpu.SemaphoreType.DMA((max(n, 1),))],
        name=name,
    )(*arrs)
    return list(outs)


_HBM_SPEC = pl.BlockSpec(memory_space=pltpu.HBM)
_SEM_SPEC = pl.BlockSpec(memory_space=pltpu.SEMAPHORE)
_ANY_SPEC = pl.BlockSpec(memory_space=pl.ANY)
_EFFECT = pltpu.SideEffectType.DATAFLOW_SIDE_EFFECTING
_CHIP_MASKS = [(1, 0), (0, 1), (1, 1)]


def _split_copies(mode, ins, lands, send_sems, recv_sems):
    x, y, c = lax.axis_index("x"), lax.axis_index("y"), lax.axis_index("c")
    chip = 2 * x + y
    out = []
    for k in range(len(ins)):
        for j, (dx, dy) in enumerate(_CHIP_MASKS):
            px = 1 - x if dx else x
            py = 1 - y if dy else y
            pchip = 2 * px + py
            src = ins[k] if mode == "gather4" else ins[k].at[pchip]
            s = k * len(_CHIP_MASKS) + j
            kw = dict(send_sem=send_sems.at[s], recv_sem=recv_sems.at[s], device_id=(px, py, c), device_id_type=MESH)
            out.append((pltpu.make_async_remote_copy(src_ref=src, dst_ref=lands[k].at[chip], **kw),
                        pltpu.make_async_remote_copy(src_ref=src, dst_ref=lands[k].at[pchip], **kw)))
    return out


def _split_start(arrs, mode, name, after=None):
    n = len(arrs)
    nsem = n * len(_CHIP_MASKS)
    land_shapes = [((N_CHIPS,) + a.shape) if mode == "gather4" else a.shape for a in arrs]

    def body(*refs):
        ins, lands = refs[:n], refs[n:2 * n]
        outs = refs[2 * n + (after is not None):]
        for start, _ in _split_copies(mode, ins, lands, outs[0], outs[1]):
            start.start()
        outs[-1][...] = jnp.zeros(outs[-1].shape, F32)

    srcs = [pltpu.with_memory_space_constraint(a, pltpu.HBM) for a in arrs]
    empties = [pltpu.with_memory_space_constraint(lax.empty(s, a.dtype), pltpu.HBM) for s, a in zip(land_shapes, arrs)]
    res = pl.pallas_call(
        body, name=name,
        out_shape=(pltpu.SemaphoreType.DMA((nsem,)), pltpu.SemaphoreType.DMA((nsem,)),
                   *[pltpu.HBM(a.shape, a.dtype) for a in arrs],
                   *[pltpu.HBM(s, a.dtype) for s, a in zip(land_shapes, arrs)],
                   _sds((SUBLANES, LANES), F32)),
        in_specs=[_HBM_SPEC] * (2 * n) + ([_ANY_SPEC] if after is not None else []),
        out_specs=(_SEM_SPEC, _SEM_SPEC, *[_HBM_SPEC] * (2 * n), pl.BlockSpec(memory_space=pltpu.VMEM)),
        input_output_aliases={k: 2 + k for k in range(2 * n)},
        compiler_params=pltpu.CompilerParams(has_side_effects=_EFFECT),
    )(*srcs, *empties, *([after] if after is not None else []))
    return dict(mode=mode, n=n, sems=res[:2], bufs=res[2:2 + 2 * n]), res[-1]


def _split_wait(handle, name, after):
    n, mode = handle["n"], handle["mode"]

    def body(*refs):
        ins, lands = refs[:n], refs[n:2 * n]
        send_sems, recv_sems = refs[2 * n], refs[2 * n + 1]
        for _, arrival in _split_copies(mode, ins, lands, send_sems, recv_sems):
            arrival.wait_send()
            arrival.wait_recv()

    bufs = handle["bufs"]
    res = pl.pallas_call(
        body, name=name,
        out_shape=tuple(pltpu.HBM(b.shape, b.dtype) for b in bufs),
        in_specs=[_HBM_SPEC] * (2 * n) + [_SEM_SPEC, _SEM_SPEC, _ANY_SPEC],
        out_specs=tuple([_HBM_SPEC] * (2 * n)),
        input_output_aliases={k: k for k in range(2 * n)},
        compiler_params=pltpu.CompilerParams(has_side_effects=_EFFECT),
    )(*bufs, *handle["sems"], after)
    return list(res[n:])


def _with_own(landed, own):
    chip = 2 * lax.axis_index("x") + lax.axis_index("y")
    return lax.dynamic_update_index_in_dim(landed, own, chip, 0)


def _sigmoid(x):
    return 1.0 / (1.0 + jnp.exp(-x))


def _log_sigmoid(x):
    e = jnp.exp(-jnp.abs(x))
    log1p = jnp.where(e < 1e-2, e * (1.0 - e * (0.5 - e * (1.0 / 3.0))), jnp.log(1.0 + e))
    return jnp.minimum(x, 0.0) - log1p


def _gelu(a):
    return 0.5 * a * (1.0 + lax.erf(a * (2.0 ** -0.5)))


def _gelu_grad(a):
    cdf = 0.5 * (1.0 + lax.erf(a * (2.0 ** -0.5)))
    pdf = jnp.exp(-0.5 * a * a) * (1.0 / math.sqrt(2.0 * math.pi))
    return cdf + a * pdf


def _ln_fwd(z):
    mu = jnp.mean(z, axis=-1, keepdims=True)
    zc = z - mu
    var = jnp.mean(zc * zc, axis=-1, keepdims=True)
    rstd = lax.rsqrt(var + LN_EPS)
    return zc * rstd, rstd


def _ln_bwd(dy, xhat, rstd, g):
    dxh = dy * g
    m1 = jnp.mean(dxh, axis=-1, keepdims=True)
    m2 = jnp.mean(dxh * xhat, axis=-1, keepdims=True)
    dz = rstd * (dxh - m1 - xhat * m2)
    return dz, jnp.sum(dy * xhat, axis=0, keepdims=True), jnp.sum(dy, axis=0, keepdims=True)


def _shift_down(z, halo):
    r = lax.broadcasted_iota(jnp.int32, z.shape, 0)
    z1 = jnp.where(r == 0, halo[7:8, :], pltpu.roll(z, 1, 0))
    z2 = jnp.where(r == 0, halo[6:7, :], jnp.where(r == 1, halo[7:8, :], pltpu.roll(z, 2, 0)))
    return z1, z2


def _shift_up(z, halo):
    n = z.shape[0]
    r = lax.broadcasted_iota(jnp.int32, z.shape, 0)
    z1 = jnp.where(r == n - 1, halo[0:1, :], pltpu.roll(z, n - 1, 0))
    z2 = jnp.where(r == n - 1, halo[1:2, :], jnp.where(r == n - 2, halo[0:1, :], pltpu.roll(z, n - 2, 0)))
    return z1, z2


def _accumulate(ref, first, value):
    @pl.when(first)
    def _():
        ref[...] = value

    @pl.when(jnp.logical_not(first))
    def _():
        ref[...] += value


def _proj(x, w, splits, name):
    t, k = x.shape
    tm = min(ROW_TILE, t)

    def body(x_ref, w_ref, *outs):
        a = x_ref[...].astype(BF16)
        for (lo, hi, dt), o in zip(splits, outs):
            o[...] = jnp.dot(a, w_ref[:, lo:hi], preferred_element_type=F32).astype(dt)

    return pl.pallas_call(
        body, grid=(t // tm,),
        in_specs=[pl.BlockSpec((tm, k), lambda i: (i, 0)), _resident(w.shape)],
        out_specs=[pl.BlockSpec((tm, hi - lo), lambda i: (i, 0)) for lo, hi, _ in splits],
        out_shape=[_sds((t, hi - lo), dt) for lo, hi, dt in splits],
        compiler_params=_cp(), name=name)(x, w)


def _fgate_fwd(fl3, b_f):
    nc = fl3.shape[0]

    def body(f_ref, b_ref, c_ref):
        r = lax.broadcasted_iota(jnp.int32, (LANES, LANES), 0)
        cidx = lax.broadcasted_iota(jnp.int32, (LANES, LANES), 1)
        upper = (r <= cidx).astype(F32)

        def step(i, carry):
            lf = _log_sigmoid(f_ref[i] + b_ref[...])
            cc = jnp.dot(lf, upper, precision=HIGHEST, preferred_element_type=F32) + carry
            c_ref[i] = cc
            return cc[:, LANES - 1:LANES]

        lax.fori_loop(0, nc, step, jnp.zeros((FOX_HEADS, 1), F32))

    return pl.pallas_call(body, out_shape=_sds(fl3.shape, F32), name="fgate_fwd")(fl3, b_f)


def _fgate_bwd(dc3, fl3, b_f):
    nc = fl3.shape[0]

    def body(dc_ref, f_ref, b_ref, df_ref, db_ref):
        r = lax.broadcasted_iota(jnp.int32, (LANES, LANES), 0)
        cidx = lax.broadcasted_iota(jnp.int32, (LANES, LANES), 1)
        lower = (r >= cidx).astype(F32)

        def step(n, carry):
            suffix, db = carry
            i = nc - 1 - n
            dlf = jnp.dot(dc_ref[i], lower, precision=HIGHEST, preferred_element_type=F32) + suffix
            df = dlf * (1.0 - _sigmoid(f_ref[i] + b_ref[...]))
            df_ref[i] = df
            return dlf[:, 0:1], db + jnp.sum(df, axis=1, keepdims=True)

        zero = jnp.zeros((FOX_HEADS, 1), F32)
        _, db = lax.fori_loop(0, nc, step, (zero, zero))
        db_ref[...] = db

    return pl.pallas_call(body, out_shape=[_sds(fl3.shape, F32), _sds((FOX_HEADS, 1), F32)],
                          name="fgate_bwd")(dc3, fl3, b_f)


def _split3(c):
    hi = c.astype(BF16).astype(F32)
    mid = (c - hi).astype(BF16).astype(F32)
    lo = (c - hi - mid).astype(BF16).astype(F32)
    return hi, mid, lo


def _lane_pieces(lane, start, pieces, sign):
    out = jnp.zeros(lane.shape, F32)
    for n, p in enumerate(pieces):
        out = jnp.where(lane == start + n, sign * p, out)
    return out


def _attn_pack(qkv, c_col):
    t = qkv.shape[0]
    tm = min(ROW_TILE, t)
    hd = HEAD_DIM

    def body(x_ref, c_ref, qp_ref, kp_ref, vp_ref, kt_ref, vt_ref):
        lane = lax.broadcasted_iota(jnp.int32, (tm, hd), 1) + hd
        for h in range(FOX_HEADS):
            pieces = _split3(c_ref[:, h:h + 1])
            ones = lambda a, b: jnp.where(jnp.logical_and(lane >= a, lane < b), 1.0, 0.0)
            q_extra = _lane_pieces(lane, Q_C, pieces, 1.0) + ones(Q_ONE, Q_ONE + 3)
            k_extra = _lane_pieces(lane, K_C, pieces, -1.0) + ones(K_ONE, K_ONE + 3) + ones(K_ONE2, K_ONE2 + 3)
            qp_ref[h, :, :hd] = (x_ref[:, h * hd:(h + 1) * hd].astype(F32) * (hd ** -0.5)).astype(BF16)
            qp_ref[h, :, hd:] = q_extra.astype(BF16)
            kp_ref[h, :, :hd] = x_ref[:, FOX_WIDTH + h * hd:FOX_WIDTH + (h + 1) * hd]
            kp_ref[h, :, hd:] = k_extra.astype(BF16)
            vp_ref[h, :, :hd] = x_ref[:, 2 * FOX_WIDTH + h * hd:2 * FOX_WIDTH + (h + 1) * hd]
            vp_ref[h, :, hd:] = ones(V_ONE, V_ONE + 4).astype(BF16)
            kt_ref[h] = kp_ref[h].astype(F32).T.astype(BF16)
            vt_ref[h] = vp_ref[h].astype(F32).T.astype(BF16)

    row3 = pl.BlockSpec((FOX_HEADS, tm, LANES), lambda i: (0, i, 0))
    col3 = pl.BlockSpec((FOX_HEADS, LANES, tm), lambda i: (0, 0, i))
    return pl.pallas_call(
        body, grid=(t // tm,),
        in_specs=[pl.BlockSpec((tm, QKV), lambda i: (i, 0)), pl.BlockSpec((tm, FOX_HEADS), lambda i: (i, 0))],
        out_specs=[row3, row3, row3, col3, col3],
        out_shape=[_sds((FOX_HEADS, t, LANES), BF16)] * 3 + [_sds((FOX_HEADS, LANES, t), BF16)] * 2,
        compiler_params=_cp(), name="attn_pack")(qkv, c_col)


def _triangle(nq, key_major):
    if key_major:
        pairs = [(i, j) for j in range(nq) for i in range(j, nq)]
    else:
        pairs = [(i, j) for i in range(nq) for j in range(i + 1)]
    return jnp.asarray([p[0] for p in pairs], jnp.int32), jnp.asarray([p[1] for p in pairs], jnp.int32)


def _attn_fwd(qp, kp, vt):
    t = qp.shape[1]
    bq = min(ATT_BLOCK, t)
    nq = t // bq
    i_tab, j_tab = _triangle(nq, key_major=False)

    def body(it_ref, jt_ref, q_ref, k_ref, vt_ref, o_ref, lse_ref, m_sc, acc_sc):
        s = pl.program_id(1)
        i, j = it_ref[s], jt_ref[s]

        @pl.when(j == 0)
        def _():
            m_sc[...] = jnp.full(m_sc.shape, NEG, F32)
            acc_sc[...] = jnp.zeros(acc_sc.shape, F32)

        def sweep(masked):
            for h in range(2):
                st = lax.dot_general(k_ref[h], q_ref[h], NT, preferred_element_type=F32)
                if masked:
                    key = lax.broadcasted_iota(jnp.int32, (bq, bq), 0)
                    qry = lax.broadcasted_iota(jnp.int32, (bq, bq), 1)
                    st = jnp.where(key <= qry, st, NEG)
                m_prev = m_sc[h]
                m_new = jnp.maximum(m_prev, jnp.max(st, axis=0, keepdims=True))
                pt = jnp.exp(st - m_new).astype(BF16)
                acc_sc[h] = jnp.exp(m_prev - m_new) * acc_sc[h] + jnp.dot(vt_ref[h], pt, preferred_element_type=F32)
                m_sc[h] = m_new

        @pl.when(j < i)
        def _():
            sweep(False)

        @pl.when(j == i)
        def _():
            sweep(True)
            for h in range(2):
                acc = acc_sc[h]
                denom = acc[V_ONE:V_ONE + 1, :]
                o_ref[:, h * HEAD_DIM:(h + 1) * HEAD_DIM] = (acc[:HEAD_DIM, :] / denom).T.astype(BF16)
                lse_ref[h] = m_sc[h] + jnp.log(denom)

    grid_spec = pltpu.PrefetchScalarGridSpec(
        num_scalar_prefetch=2, grid=(HEAD_PAIRS, i_tab.shape[0]),
        in_specs=[pl.BlockSpec((2, bq, LANES), lambda hp, s, it, jt: (hp, it[s], 0)),
                  pl.BlockSpec((2, bq, LANES), lambda hp, s, it, jt: (hp, jt[s], 0)),
                  pl.BlockSpec((2, LANES, bq), lambda hp, s, it, jt: (hp, 0, jt[s]))],
        out_specs=[pl.BlockSpec((bq, LANES), lambda hp, s, it, jt: (it[s], hp)),
                   pl.BlockSpec((2, 1, bq), lambda hp, s, it, jt: (hp, 0, it[s]))],
        scratch_shapes=[pltpu.VMEM((2, 1, bq), F32), pltpu.VMEM((2, LANES, bq), F32)])
    return pl.pallas_call(body, grid_spec=grid_spec,
                          out_shape=[_sds((t, FOX_WIDTH), BF16), _sds((FOX_HEADS, 1, t), F32)],
                          compiler_params=_cp(), name="attn_fwd")(i_tab, j_tab, qp, kp, vt)


def _conv_fwd(bch, conv_w):
    t = bch.shape[0]
    tm = min(ROW_TILE, t)
    halo_blocks = tm // SUBLANES
    cw = CONV_WIDTH

    def body(cur_ref, prev_ref, w_ref, o_ref):
        i = pl.program_id(0)
        z = cur_ref[:, cw:2 * cw] * cur_ref[:, 2 * cw:]
        zp = jnp.where(i == 0, 0.0, prev_ref[:, cw:2 * cw] * prev_ref[:, 2 * cw:])
        z1, z2 = _shift_down(z, zp)
        y = w_ref[0:1, :] * z2 + w_ref[1:2, :] * z1 + w_ref[2:3, :] * z
        o_ref[...] = (cur_ref[:, :cw] * y).astype(BF16)

    return pl.pallas_call(
        body, grid=(t // tm,),
        in_specs=[pl.BlockSpec((tm, BCH), lambda i: (i, 0)),
                  pl.BlockSpec((SUBLANES, BCH), lambda i: (jnp.maximum(i * halo_blocks - 1, 0), 0)),
                  _resident(conv_w.shape)],
        out_specs=pl.BlockSpec((tm, cw), lambda i: (i, 0)),
        out_shape=_sds((t, cw), BF16), compiler_params=_cp(), name="conv_fwd")(bch, bch, conv_w)


def _mm_res_ln(pairs, res, g, b, name):
    t, d = res.shape
    tm = min(ROW_TILE, t)
    n = len(pairs)

    def body(*refs):
        a_refs, w_refs = refs[:n], refs[n:2 * n]
        res_ref, g_ref, b_ref, y_ref, xh_ref, rs_ref = refs[2 * n:]
        z = ALPHA * res_ref[...]
        for a_ref, w_ref in zip(a_refs, w_refs):
            z = z + jnp.dot(a_ref[...].astype(BF16), w_ref[...], preferred_element_type=F32)
        xhat, rstd = _ln_fwd(z)
        y_ref[...] = xhat * g_ref[...] + b_ref[...]
        xh_ref[...] = xhat
        rs_ref[...] = rstd

    row = lambda i: (i, 0)
    return pl.pallas_call(
        body, grid=(t // tm,),
        in_specs=[pl.BlockSpec((tm, a.shape[1]), row) for a, _ in pairs] + [_resident(w.shape) for _, w in pairs]
        + [pl.BlockSpec((tm, d), row), _resident(g.shape), _resident(b.shape)],
        out_specs=[pl.BlockSpec((tm, d), row), pl.BlockSpec((tm, d), row), pl.BlockSpec((tm, 1), row)],
        out_shape=[_sds((t, d), F32), _sds((t, d), F32), _sds((t, 1), F32)],
        compiler_params=_cp(), name=name)(*[a for a, _ in pairs], *[w for _, w in pairs], res, g, b)


def _ffn_in(x, wi, name):
    t, d = x.shape
    tm = min(FFN_ROW_TILE, t)
    hh = HALF_HIDDEN

    def body(x_ref, w_ref, gu_ref, h_ref):
        a = x_ref[...].astype(BF16)
        for c in range(2):
            gs, us = slice(c * hh, (c + 1) * hh), slice(FFN_HIDDEN + c * hh, FFN_HIDDEN + (c + 1) * hh)
            g = jnp.dot(a, w_ref[:, gs], preferred_element_type=F32)
            u = jnp.dot(a, w_ref[:, us], preferred_element_type=F32)
            gu_ref[:, gs] = g.astype(BF16)
            gu_ref[:, us] = u.astype(BF16)
            h_ref[:, gs] = (g * _sigmoid(g) * u).astype(BF16)

    row = lambda i: (i, 0)
    return pl.pallas_call(
        body, grid=(t // tm,),
        in_specs=[pl.BlockSpec((tm, d), row), _resident(wi.shape)],
        out_specs=[pl.BlockSpec((tm, 2 * FFN_HIDDEN), row), pl.BlockSpec((tm, FFN_HIDDEN), row)],
        out_shape=[_sds((t, 2 * FFN_HIDDEN), BF16), _sds((t, FFN_HIDDEN), BF16)],
        compiler_params=_cp(), name=name)(x, wi)


def _gmlp_fwd(x, w_in, vg, vb, wm, bs_col):
    t, d = x.shape
    tm = min(ROW_TILE, t)
    gb = GMLP_BLOCK

    def body(x_ref, w_ref, vg_ref, vb_ref, wm_ref, bs_ref, a_ref, o_ref):
        a = jnp.dot(x_ref[...].astype(BF16), w_ref[...], preferred_element_type=F32)
        a_ref[...] = a
        u = _gelu(a[:, :d])
        vhat, _ = _ln_fwd(_gelu(a[:, d:]))
        vln = (vhat * vg_ref[...] + vb_ref[...]).astype(BF16)
        for blk in range(tm // gb):
            rs = slice(blk * gb, (blk + 1) * gb)
            for gi in range(GMLP_GROUPS):
                cs = slice(gi * gb, (gi + 1) * gb)
                s = jnp.dot(wm_ref[gi], vln[rs, cs], preferred_element_type=F32) + bs_ref[:, gi:gi + 1]
                o_ref[rs, cs] = (u[rs, cs] * s).astype(BF16)

    row = lambda i: (i, 0)
    return pl.pallas_call(
        body, grid=(t // tm,),
        in_specs=[pl.BlockSpec((tm, d), row), _resident(w_in.shape), _resident(vg.shape), _resident(vb.shape),
                  _resident(wm.shape), _resident(bs_col.shape)],
        out_specs=[pl.BlockSpec((tm, 2 * d), row), pl.BlockSpec((tm, d), row)],
        out_shape=[_sds((t, 2 * d), F32), _sds((t, d), BF16)],
        compiler_params=_cp(), name="gmlp_fwd")(x, w_in, vg, vb, wm, bs_col)


def _loss_ln_bwd(xhat, rstd, g, b, target):
    t, d = xhat.shape
    tm = min(ROW_TILE, t)

    def body(xh_ref, rs_ref, g_ref, b_ref, t_ref, sq_ref, dz_ref, dg_ref, db_ref):
        first = pl.program_id(0) == 0
        xh = xh_ref[...]
        err = xh * g_ref[...] + b_ref[...] - t_ref[...]
        dz, dg, db = _ln_bwd(err * (1.0 / d), xh, rs_ref[...], g_ref[...])
        dz_ref[...] = dz
        _accumulate(sq_ref, first, jnp.sum(err * err, axis=0, keepdims=True))
        _accumulate(dg_ref, first, dg)
        _accumulate(db_ref, first, db)

    row = lambda i: (i, 0)
    vec = pl.BlockSpec((1, d), lambda i: (0, 0))
    return pl.pallas_call(
        body, grid=(t // tm,),
        in_specs=[pl.BlockSpec((tm, d), row), pl.BlockSpec((tm, 1), row), _resident(g.shape), _resident(b.shape),
                  pl.BlockSpec((tm, d), row)],
        out_specs=[vec, pl.BlockSpec((tm, d), row), vec, vec],
        out_shape=[_sds((1, d), F32), _sds((t, d), F32), _sds((1, d), F32), _sds((1, d), F32)],
        compiler_params=_cp(), name="loss_ln_bwd")(xhat, rstd, g, b, target)


def _mm_nt(pairs, ws, name, *, tm=ROW_TILE, res=None, ln=None, out_dtype=F32, after=None):
    t = pairs[0][0].shape[0]
    k = ws[0].shape[0]
    tm = min(tm, t)
    n, nw = len(pairs), len(ws)

    def body(*refs):
        refs = refs[after is not None:]
        a_refs, w_refs = refs[:n], refs[n:n + nw]
        rest = list(refs[n + nw:])
        dx = None
        for a_ref, (_, wi, lo, hi) in zip(a_refs, pairs):
            part = lax.dot_general(a_ref[...].astype(BF16), w_refs[wi][:, lo:hi], NT, preferred_element_type=F32)
            dx = part if dx is None else dx + part
        if res is not None:
            dx = dx + ALPHA * rest.pop(0)[...]
        if ln is None:
            rest[0][...] = dx.astype(out_dtype)
            return
        xh_ref, rs_ref, g_ref, dz_ref, dg_ref, db_ref = rest
        first = pl.program_id(0) == 0
        dz, dg, db = _ln_bwd(dx, xh_ref[...], rs_ref[...], g_ref[...])
        dz_ref[...] = dz
        _accumulate(dg_ref, first, dg)
        _accumulate(db_ref, first, db)

    row = lambda i: (i, 0)
    in_specs = [pl.BlockSpec((tm, a.shape[1]), row) for a, _, _, _ in pairs] + [_resident(w.shape) for w in ws]
    args = [a for a, _, _, _ in pairs] + list(ws)
    if res is not None:
        in_specs.append(pl.BlockSpec((tm, k), row))
        args.append(res)
    if ln is None:
        out_specs = pl.BlockSpec((tm, k), row)
        out_shape = _sds((t, k), out_dtype)
    else:
        xhat, rstd, g = ln
        in_specs += [pl.BlockSpec((tm, k), row), pl.BlockSpec((tm, 1), row), _resident(g.shape)]
        args += [xhat, rstd, g]
        vec = pl.BlockSpec((1, k), lambda i: (0, 0))
        out_specs = [pl.BlockSpec((tm, k), row), vec, vec]
        out_shape = [_sds((t, k), F32), _sds((1, k), F32), _sds((1, k), F32)]
    if after is not None:
        in_specs.insert(0, _ANY_SPEC)
        args.insert(0, after)
    return pl.pallas_call(body, grid=(t // tm,), in_specs=in_specs, out_specs=out_specs, out_shape=out_shape,
                          compiler_params=_cp(), name=name)(*args)


def _mm_tn(a, b, name, *, tn, tk=None, stack_cols=False, out_dtype=BF16):
    t, k = a.shape
    n = b.shape[1]
    tk = k if tk is None else tk
    tt = min(ROW_TILE, t)
    nt = t // tt

    def body(a_ref, b_ref, o_ref, acc_ref):
        s = pl.program_id(2)
        part = lax.dot_general(a_ref[...].astype(BF16), b_ref[...].astype(BF16), TN, preferred_element_type=F32)
        _accumulate(acc_ref, s == 0, part)

        @pl.when(s == nt - 1)
        def _():
            o_ref[...] = acc_ref[...].astype(out_dtype).reshape(o_ref.shape)

    if stack_cols:
        assert tk == k
        out_spec = pl.BlockSpec((1, k, tn), lambda kk, j, s: (j, 0, 0))
        out_shape = _sds((n // tn, k, tn), out_dtype)
    else:
        out_spec = pl.BlockSpec((tk, tn), lambda kk, j, s: (kk, j))
        out_shape = _sds((k, n), out_dtype)
    return pl.pallas_call(
        body, grid=(k // tk, n // tn, nt),
        in_specs=[pl.BlockSpec((tt, tk), lambda kk, j, s: (s, kk)), pl.BlockSpec((tt, tn), lambda kk, j, s: (s, j))],
        out_specs=out_spec, out_shape=out_shape,
        scratch_shapes=[pltpu.VMEM((tk, tn), F32)],
        compiler_params=_cp(), name=name)(a, b)


def _ffn_bwd_hidden(dz, wo, gu, name):
    t, d = dz.shape
    tm = min(FFN_ROW_TILE, t)
    hh = HALF_HIDDEN

    def body(dz_ref, w_ref, gu_ref, o_ref):
        a = dz_ref[...].astype(BF16)
        for c in range(2):
            gs, us = slice(c * hh, (c + 1) * hh), slice(FFN_HIDDEN + c * hh, FFN_HIDDEN + (c + 1) * hh)
            dh = lax.dot_general(a, w_ref[gs, :], NT, preferred_element_type=F32)
            g = gu_ref[:, gs].astype(F32)
            u = gu_ref[:, us].astype(F32)
            sig = _sigmoid(g)
            o_ref[:, gs] = (dh * u * sig * (1.0 + g * (1.0 - sig))).astype(BF16)
            o_ref[:, us] = (dh * g * sig).astype(BF16)

    row = lambda i: (i, 0)
    return pl.pallas_call(
        body, grid=(t // tm,),
        in_specs=[pl.BlockSpec((tm, d), row), _resident(wo.shape), pl.BlockSpec((tm, 2 * FFN_HIDDEN), row)],
        out_specs=pl.BlockSpec((tm, 2 * FFN_HIDDEN), row),
        out_shape=_sds((t, 2 * FFN_HIDDEN), BF16), compiler_params=_cp(), name=name)(dz, wo, gu)


def _gmlp_bwd(dgated, a, vg, vb, wm, bs_col):
    t, d2 = a.shape
    d = d2 // 2
    tm = min(ROW_TILE, t)
    gb = GMLP_BLOCK

    def body(dg_ref, a_ref, vg_ref, vb_ref, wm_ref, bs_ref, da_ref, dws_ref, dbs_ref, dvg_ref, dvb_ref, dvln_sc):
        first = pl.program_id(0) == 0
        au, av = a_ref[:, :d], a_ref[:, d:]
        u = _gelu(au)
        vhat, rstd = _ln_fwd(_gelu(av))
        vln = (vhat * vg_ref[...] + vb_ref[...]).astype(BF16)
        dgate = dg_ref[...]

        @pl.when(first)
        def _():
            dws_ref[...] = jnp.zeros(dws_ref.shape, F32)
            dbs_ref[...] = jnp.zeros(dbs_ref.shape, F32)

        for blk in range(tm // gb):
            rs = slice(blk * gb, (blk + 1) * gb)
            for gi in range(GMLP_GROUPS):
                cs = slice(gi * gb, (gi + 1) * gb)
                vblk = vln[rs, cs]
                s = jnp.dot(wm_ref[gi], vblk, preferred_element_type=F32) + bs_ref[:, gi:gi + 1]
                dgb = dgate[rs, cs]
                da_ref[rs, cs] = (dgb * s * _gelu_grad(au[rs, cs])).astype(BF16)
                ds = dgb * u[rs, cs]
                dsb = ds.astype(BF16)
                dws_ref[gi] += lax.dot_general(dsb, vblk, NT, preferred_element_type=F32)
                dbs_ref[:, gi:gi + 1] += jnp.sum(ds, axis=1, keepdims=True)
                dvln_sc[rs, cs] = lax.dot_general(wm_ref[gi], dsb, TN, preferred_element_type=F32)
        dv, dvg, dvb = _ln_bwd(dvln_sc[...], vhat, rstd, vg_ref[...])
        da_ref[:, d:] = (dv * _gelu_grad(av)).astype(BF16)
        _accumulate(dvg_ref, first, dvg)
        _accumulate(dvb_ref, first, dvb)

    row = lambda i: (i, 0)
    vec = pl.BlockSpec((1, d), lambda i: (0, 0))
    return pl.pallas_call(
        body, grid=(t // tm,),
        in_specs=[pl.BlockSpec((tm, d), row), pl.BlockSpec((tm, d2), row), _resident(vg.shape), _resident(vb.shape),
                  _resident(wm.shape), _resident(bs_col.shape)],
        out_specs=[pl.BlockSpec((tm, d2), row), pl.BlockSpec(wm.shape, lambda i: (0, 0, 0)),
                   pl.BlockSpec(bs_col.shape, lambda i: (0, 0)), vec, vec],
        out_shape=[_sds((t, d2), BF16), _sds(wm.shape, F32), _sds(bs_col.shape, F32), _sds((1, d), F32), _sds((1, d), F32)],
        scratch_shapes=[pltpu.VMEM((tm, d), F32)],
        compiler_params=_cp(), name="gmlp_bwd")(dgated, a, vg, vb, wm, bs_col)


def _conv_bwd(bch, dmix, conv_w):
    t = bch.shape[0]
    tm = min(ROW_TILE, t)
    nb = t // tm
    halo_blocks = tm // SUBLANES
    cw = CONV_WIDTH

    def body(cur_ref, prev_ref, next_ref, dc_ref, dn_ref, w_ref, o_ref, dw_ref):
        i = pl.program_id(0)
        bgate, cgate, hval = cur_ref[:, :cw], cur_ref[:, cw:2 * cw], cur_ref[:, 2 * cw:]
        z = cgate * hval
        zp = jnp.where(i == 0, 0.0, prev_ref[:, cw:2 * cw] * prev_ref[:, 2 * cw:])
        z1, z2 = _shift_down(z, zp)
        w0, w1, w2 = w_ref[0:1, :], w_ref[1:2, :], w_ref[2:3, :]
        dconv = dc_ref[...]
        o_ref[:, :cw] = (dconv * (w0 * z2 + w1 * z1 + w2 * z)).astype(BF16)
        dy = dconv * bgate
        dyn = jnp.where(i == nb - 1, 0.0, dn_ref[...] * next_ref[:, :cw])
        dy1, dy2 = _shift_up(dy, dyn)
        dz = w2 * dy + w1 * dy1 + w0 * dy2
        o_ref[:, cw:2 * cw] = (dz * hval).astype(BF16)
        o_ref[:, 2 * cw:] = (dz * cgate).astype(BF16)

        @pl.when(i == 0)
        def _():
            dw_ref[...] = jnp.zeros(dw_ref.shape, F32)

        for tap, zs in enumerate((z2, z1, z)):
            dw_ref[tap:tap + 1, :] += jnp.sum(dy * zs, axis=0, keepdims=True)

    last_halo = t // SUBLANES - 1
    return pl.pallas_call(
        body, grid=(nb,),
        in_specs=[pl.BlockSpec((tm, BCH), lambda i: (i, 0)),
                  pl.BlockSpec((SUBLANES, BCH), lambda i: (jnp.maximum(i * halo_blocks - 1, 0), 0)),
                  pl.BlockSpec((SUBLANES, BCH), lambda i: (jnp.minimum((i + 1) * halo_blocks, last_halo), 0)),
                  pl.BlockSpec((tm, cw), lambda i: (i, 1)),
                  pl.BlockSpec((SUBLANES, cw), lambda i: (jnp.minimum((i + 1) * halo_blocks, last_halo), 1)),
                  _resident(conv_w.shape)],
        out_specs=[pl.BlockSpec((tm, BCH), lambda i: (i, 0)), pl.BlockSpec((SUBLANES, cw), lambda i: (0, 0))],
        out_shape=[_sds((t, BCH), BF16), _sds((SUBLANES, cw), F32)],
        compiler_params=_cp(), name="conv_bwd")(bch, bch, bch, dmix, dmix, conv_w)


def _attn_bwd_prep(o, dmix, qp, lse_col):
    t = o.shape[0]
    tm = min(ROW_TILE, t)
    hd = HEAD_DIM

    def body(o_ref, do_ref, qp_ref, lse_ref, qb_ref, dob_ref):
        lane = lax.broadcasted_iota(jnp.int32, (tm, hd), 1) + hd
        for h in range(FOX_HEADS):
            do = do_ref[:, h * hd:(h + 1) * hd]
            delta = jnp.sum(o_ref[:, h * hd:(h + 1) * hd].astype(F32) * do, axis=-1, keepdims=True)
            dob_ref[h, :, :hd] = do.astype(BF16)
            dob_ref[h, :, hd:] = _lane_pieces(lane, DO_DELTA, _split3(delta), -1.0).astype(BF16)
            qb_ref[h, :, :hd] = qp_ref[h, :, :hd]
            qb_ref[h, :, hd:] = (qp_ref[h, :, hd:].astype(F32)
                                 + _lane_pieces(lane, Q_LSE, _split3(lse_ref[:, h:h + 1]), -1.0)).astype(BF16)

    row3 = pl.BlockSpec((FOX_HEADS, tm, LANES), lambda i: (0, i, 0))
    return pl.pallas_call(
        body, grid=(t // tm,),
        in_specs=[pl.BlockSpec((tm, FOX_WIDTH), lambda i: (i, 0)), pl.BlockSpec((tm, FOX_WIDTH), lambda i: (i, 0)), row3,
                  pl.BlockSpec((tm, FOX_HEADS), lambda i: (i, 0))],
        out_specs=[row3, row3], out_shape=[_sds((FOX_HEADS, t, LANES), BF16)] * 2,
        compiler_params=_cp(), name="attn_bwd_prep")(o, dmix, qp, lse_col)


def _attn_bwd(qb, kp, vp, dob, kt):
    t = qb.shape[1]
    bq = min(ATT_BLOCK, t)
    nq = t // bq
    i_tab, j_tab = _triangle(nq, key_major=True)

    def body(it_ref, jt_ref, q_ref, k_ref, v_ref, do_ref, kt_ref, dqt_ref, dk_ref, dv_ref, dk_sc, dv_sc):
        s = pl.program_id(1)
        i, j = it_ref[s], jt_ref[s]

        @pl.when(s == 0)
        def _():
            dqt_ref[...] = jnp.zeros(dqt_ref.shape, F32)

        @pl.when(i == j)
        def _():
            dk_sc[...] = jnp.zeros(dk_sc.shape, F32)
            dv_sc[...] = jnp.zeros(dv_sc.shape, F32)

        cols = pl.ds(pl.multiple_of(i * bq, bq), bq)

        def sweep(masked):
            for h in range(2):
                st = lax.dot_general(k_ref[h], q_ref[h], NT, preferred_element_type=F32)
                if masked:
                    key = lax.broadcasted_iota(jnp.int32, (bq, bq), 0)
                    qry = lax.broadcasted_iota(jnp.int32, (bq, bq), 1)
                    st = jnp.where(key <= qry, st, NEG)
                pt = jnp.exp(st)
                dst = pt * lax.dot_general(v_ref[h], do_ref[h], NT, preferred_element_type=F32)
                ptb, dstb = pt.astype(BF16), dst.astype(BF16)
                dv_sc[h] += jnp.dot(ptb, do_ref[h], preferred_element_type=F32)
                dk_sc[h] += jnp.dot(dstb, q_ref[h], preferred_element_type=F32)
                dqt_ref[h, :, cols] += jnp.dot(kt_ref[h], dstb, preferred_element_type=F32)

        @pl.when(i == j)
        def _():
            sweep(True)

        @pl.when(i > j)
        def _():
            sweep(False)

        @pl.when(i == nq - 1)
        def _():
            dk_ref[...] = dk_sc[...]
            dv_ref[...] = dv_sc[...].astype(BF16)

    qblk = pl.BlockSpec((2, bq, LANES), lambda hp, s, it, jt: (hp, it[s], 0))
    kblk = pl.BlockSpec((2, bq, LANES), lambda hp, s, it, jt: (hp, jt[s], 0))
    grid_spec = pltpu.PrefetchScalarGridSpec(
        num_scalar_prefetch=2, grid=(HEAD_PAIRS, i_tab.shape[0]),
        in_specs=[qblk, kblk, kblk, qblk, pl.BlockSpec((2, LANES, bq), lambda hp, s, it, jt: (hp, 0, jt[s]))],
        out_specs=[pl.BlockSpec((2, LANES, t), lambda hp, s, it, jt: (hp, 0, 0)), kblk, kblk],
        scratch_shapes=[pltpu.VMEM((2, bq, LANES), F32), pltpu.VMEM((2, bq, LANES), F32)])
    return pl.pallas_call(body, grid_spec=grid_spec,
                          out_shape=[_sds((FOX_HEADS, LANES, t), F32), _sds((FOX_HEADS, t, LANES), F32),
                                     _sds((FOX_HEADS, t, LANES), BF16)],
                          compiler_params=_cp(), name="attn_bwd")(i_tab, j_tab, qb, kp, vp, dob, kt)


def _attn_unpack(dqt, dkp, dvp):
    t = dkp.shape[1]
    tm = min(ROW_TILE, t)
    hd = HEAD_DIM

    def body(dqt_ref, dk_ref, dv_ref, o_ref, dc_ref):
        for h in range(FOX_HEADS):
            dq = dqt_ref[h].T
            o_ref[:, h * hd:(h + 1) * hd] = (dq[:, :hd] * (hd ** -0.5)).astype(BF16)
            o_ref[:, FOX_WIDTH + h * hd:FOX_WIDTH + (h + 1) * hd] = dk_ref[h, :, :hd].astype(BF16)
            o_ref[:, 2 * FOX_WIDTH + h * hd:2 * FOX_WIDTH + (h + 1) * hd] = dv_ref[h, :, :hd]
            dc_ref[:, h:h + 1] = dq[:, K_ONE:K_ONE + 1] - dk_ref[h, :, Q_ONE:Q_ONE + 1]

    row3 = pl.BlockSpec((FOX_HEADS, tm, LANES), lambda i: (0, i, 0))
    return pl.pallas_call(
        body, grid=(t // tm,),
        in_specs=[pl.BlockSpec((FOX_HEADS, LANES, tm), lambda i: (0, 0, i)), row3, row3],
        out_specs=[pl.BlockSpec((tm, QKV), lambda i: (i, 0)), pl.BlockSpec((tm, FOX_HEADS), lambda i: (i, 0))],
        out_shape=[_sds((t, QKV), BF16), _sds((t, FOX_HEADS), F32)],
        compiler_params=_cp(), name="attn_unpack")(dqt, dkp, dvp)


def _adamw(parts, w, m, v, name):
    nl, r, c = w.shape
    tr = r
    for cand in (256, 128, 64, 32, 16):
        if r > cand and r % cand == 0:
            tr = cand
            break
    npart = len(parts)
    bc1 = 1.0 - ADAM_B1 ** ADAM_STEP
    bc2 = 1.0 - ADAM_B2 ** ADAM_STEP

    def body(*refs):
        p_refs = refs[:npart]
        w_ref, m_ref, v_ref, g_ref, d_ref, nm_ref, nv_ref = refs[npart:]
        sums = []
        for p_ref in p_refs:
            acc = p_ref[0, 0].astype(F32)
            for s in range(1, p_ref.shape[0]):
                acc = acc + p_ref[s, 0].astype(F32)
            sums.append(acc)
        g = sums[0]
        for extra in sums[1:]:
            g = g + extra
        nm = ADAM_B1 * m_ref[0] + (1.0 - ADAM_B1) * g
        nv = ADAM_B2 * v_ref[0] + (1.0 - ADAM_B2) * (g * g)
        m_hat = nm / bc1
        v_hat = nv / bc2
        g_ref[0] = g
        d_ref[0] = -ADAM_LR * (m_hat / (jnp.sqrt(v_hat) + ADAM_EPS) + ADAM_WD * w_ref[0])
        nm_ref[0] = nm
        nv_ref[0] = nv

    blk = pl.BlockSpec((1, tr, c), lambda l, i: (l, i, 0))
    return pl.pallas_call(
        body, grid=(nl, r // tr),
        in_specs=[pl.BlockSpec((p.shape[0], 1, tr, c), lambda l, i: (0, l, i, 0)) for p in parts] + [blk, blk, blk],
        out_specs=[blk] * 4, out_shape=[_sds(w.shape, F32)] * 4,
        compiler_params=_cp(), name=name)(*parts, w, m, v)


def _to_rows(a):
    flat = a.reshape(-1)
    pad = (-flat.shape[0]) % LANES
    if pad:
        flat = jnp.concatenate([flat, jnp.zeros((pad,), flat.dtype)])
    return flat.reshape(-1, LANES)


def _by_owner_cols(dw):
    k, n = dw.shape
    return dw.reshape(k, N_CHIPS, n // N_CHIPS).transpose(1, 0, 2)[:, None]


def _ffn_fwd(xin, wi, wo, g, b, layer):
    gu, h = _ffn_in(xin, wi, f"ffn_in_{layer}")
    y, xhat, rstd = _mm_res_ln([(h, wo)], xin, g, b, f"ffn_out_ln_{layer}")
    return y, (xin, gu, h, xhat, rstd)


def _ffn_bwd(dz, saved, wi, wo, ln_below, layer):
    xin, gu, h, _, _ = saved
    dgu = _ffn_bwd_hidden(dz, wo, gu, f"ffn_bwd_hidden_{layer}")
    g_out = _mm_tn(h, dz, f"ffn_dw_out_{layer}", tn=D_MODEL, tk=HALF_HIDDEN)
    g_in = _mm_tn(xin, dgu, f"ffn_dw_in_{layer}", tn=HALF_HIDDEN, stack_cols=True)
    below = _mm_nt([(dgu, 0, 0, 2 * FFN_HIDDEN)], [wi], f"ffn_dx_{layer}", tm=FFN_ROW_TILE, res=dz, ln=ln_below)
    return below, g_in, g_out.reshape(N_CHIPS, FFN_HIDDEN // N_CHIPS, D_MODEL)


def kernel(x, even_w_in, even_b_f, even_conv_w, even_w_out, odd_w_in, odd_v_ln_g, odd_v_ln_b, odd_w_s, odd_b_s, odd_w_out, mix_ln_g, mix_ln_b, ffn_w_in, ffn_w_out, ffn_ln_g, ffn_ln_b, loss_target, m_even_w_in, m_even_b_f, m_even_conv_w, m_even_w_out, m_odd_w_in, m_odd_v_ln_g, m_odd_v_ln_b, m_odd_w_s, m_odd_b_s, m_odd_w_out, m_mix_ln_g, m_mix_ln_b, m_ffn_w_in, m_ffn_w_out, m_ffn_ln_g, m_ffn_ln_b, v_even_w_in, v_even_b_f, v_even_conv_w, v_even_w_out, v_odd_w_in, v_odd_v_ln_g, v_odd_v_ln_b, v_odd_w_s, v_odd_b_s, v_odd_w_out, v_mix_ln_g, v_mix_ln_b, v_ffn_w_in, v_ffn_w_out, v_ffn_ln_g, v_ffn_ln_b):
    t = x.shape[1]
    d = D_MODEL
    chip = 2 * lax.axis_index("x") + lax.axis_index("y")
    x2d = x[0]
    target = loss_target[0]

    small_shard = jnp.concatenate([odd_v_ln_g.reshape(2, LANES), odd_v_ln_b.reshape(2, LANES),
                                   even_conv_w.reshape(CONV_K, LANES), jnp.zeros((1, LANES), F32)], axis=0)
    first = [even_w_in[0].astype(BF16), even_w_out[0].astype(BF16), small_shard]
    later = [odd_w_in[0].astype(BF16), odd_w_out[0].astype(BF16), ffn_w_in.astype(BF16), ffn_w_out.astype(BF16)]
    first_h, first_tok = _split_start(first, "gather4", "gather_first_start")
    later_h, later_tok = _split_start(later, "gather4", "gather_later_start", after=first_tok)
    g_ewi, g_ewo, g_small = [_with_own(g, own) for g, own in
                             zip(_split_wait(first_h, "gather_first_wait", later_tok), first)]
    ewi = g_ewi.transpose(1, 0, 2).reshape(d, EVEN_IN)
    w_even_in = jnp.concatenate([ewi[:, :QKV], ewi[:, QKV + FOX_HEADS:], ewi[:, QKV:QKV + FOX_HEADS],
                                 jnp.zeros((d, LANES - FOX_HEADS), BF16)], axis=1)
    w_even_out = g_ewo.reshape(d, d)
    v_ln_g = g_small[:, 0:2].reshape(1, d)
    v_ln_b = g_small[:, 2:4].reshape(1, d)
    conv_w = g_small[:, 4:7].transpose(1, 0, 2).reshape(CONV_K, CONV_WIDTH)
    chunk_id = jnp.arange(GMLP_BLOCK) // CHUNK
    gmask = chunk_id[None, :] <= chunk_id[:, None]
    w_spatial = jnp.where(gmask[None], odd_w_s[0], 0.0).astype(BF16)
    bs_col = odd_b_s[0].T
    b_f_col = even_b_f.reshape(FOX_HEADS, 1)
    ln = lambda p, l: p[l:l + 1]

    qkv, bch, fl = _proj(x2d, w_even_in, [(0, QKV, BF16), (QKV, QKV + BCH, F32), (QKV + BCH, EVEN_IN_PAD, F32)], "even_proj")
    fl3 = fl[:, :FOX_HEADS].T.reshape(FOX_HEADS, t // LANES, LANES).transpose(1, 0, 2)
    c3 = _fgate_fwd(fl3, b_f_col)
    c_rows = c3.transpose(1, 0, 2).reshape(FOX_HEADS, t)
    qp, kp, vp, kt, vt = _attn_pack(qkv, c_rows.T)
    attn, lse = _attn_fwd(qp, kp, vt)
    conv = _conv_fwd(bch, conv_w)
    x1, xh1, rs1 = _mm_res_ln([(attn, w_even_out[:FOX_WIDTH]), (conv, w_even_out[FOX_WIDTH:])], x2d,
                              ln(mix_ln_g, 0), ln(mix_ln_b, 0), "even_out_ln")
    g_owi, g_owo, g_fwi, g_fwo = [_with_own(g, own) for g, own in
                                  zip(_split_wait(later_h, "gather_later_wait", x1), later)]
    w_odd_in = g_owi.transpose(1, 0, 2).reshape(d, 2 * d)
    w_odd_out = g_owo.reshape(d, d)
    w_ffn_in = [g_fwi[:, l].transpose(1, 0, 2).reshape(d, 2 * FFN_HIDDEN) for l in range(2)]
    w_ffn_out = [g_fwo[:, l].reshape(FFN_HIDDEN, d) for l in range(2)]
    x2, ffn0 = _ffn_fwd(x1, w_ffn_in[0], w_ffn_out[0], ln(ffn_ln_g, 0), ln(ffn_ln_b, 0), 0)

    a_odd, gated = _gmlp_fwd(x2, w_odd_in, v_ln_g, v_ln_b, w_spatial, bs_col)
    x3, xh3, rs3 = _mm_res_ln([(gated, w_odd_out)], x2, ln(mix_ln_g, 1), ln(mix_ln_b, 1), "odd_out_ln")
    _, ffn1 = _ffn_fwd(x3, w_ffn_in[1], w_ffn_out[1], ln(ffn_ln_g, 1), ln(ffn_ln_b, 1), 1)

    sq, dz4, d_fg1, d_fb1 = _loss_ln_bwd(ffn1[3], ffn1[4], ln(ffn_ln_g, 1), ln(ffn_ln_b, 1), target)
    loss = lax.psum(0.5 / d * jnp.sum(sq), ("x", "y", "c"))
    (dz3, d_mg1, d_mb1), gi_f1, go_f1 = _ffn_bwd(dz4, ffn1, w_ffn_in[1], w_ffn_out[1], (xh3, rs3, ln(mix_ln_g, 1)), 1)

    dgated = _mm_nt([(dz3, 0, 0, d)], [w_odd_out], "odd_dgated")
    go_odd = _mm_tn(gated, dz3, "odd_dw_out", tn=d).reshape(N_CHIPS, 1, d // N_CHIPS, d)
    da_odd, dws, dbs_col, d_vg, d_vb = _gmlp_bwd(dgated, a_odd, v_ln_g, v_ln_b, w_spatial, bs_col)
    gi_odd = _by_owner_cols(_mm_tn(x2, da_odd, "odd_dw_in", tn=d))
    dz2, d_fg0, d_fb0 = _mm_nt([(da_odd, 0, 0, 2 * d)], [w_odd_in], "odd_dx", res=dz3,
                               ln=(ffn0[3], ffn0[4], ln(ffn_ln_g, 0)))
    (dz1, d_mg0, d_mb0), gi_f0, go_f0 = _ffn_bwd(dz2, ffn0, w_ffn_in[0], w_ffn_out[0], (xh1, rs1, ln(mix_ln_g, 0)), 0)

    sent_early = [gi_odd, go_odd, jnp.stack([gi_f0, gi_f1], axis=1), jnp.stack([go_f0, go_f1], axis=1)]
    early_h, early_tok = _split_start(sent_early, "scatter4", "scatter_early_start")
    dmix = _mm_nt([(dz1, 0, 0, d)], [w_even_out], "even_dmix", after=early_tok)
    mix =jnp.concatenate([attn, conv], axis=1)
    go_even = _mm_tn(mix, dz1, "even_dw_out", tn=d).reshape(N_CHIPS, 1, d // N_CHIPS, d)
    dbch, dconv_w8 = _conv_bwd(bch, dmix, conv_w)
    qb, dob = _attn_bwd_prep(attn, dmix, qp, lse.reshape(FOX_HEADS, t).T)
    dqkv, dc_col = _attn_unpack(*_attn_bwd(qb, kp, vp, dob, kt))
    dc3 = dc_col.T.reshape(FOX_HEADS, t // LANES, LANES).transpose(1, 0, 2)
    dfl3, d_bf = _fgate_bwd(dc3, fl3, b_f_col)
    dfl = jnp.concatenate([dfl3.transpose(1, 0, 2).reshape(FOX_HEADS, t).T.astype(BF16),
                           jnp.zeros((t, LANES - FOX_HEADS), BF16)], axis=1)
    grad_x = _mm_nt([(dqkv, 0, 0, QKV), (dbch, 0, QKV, QKV + BCH), (dfl, 0, QKV + BCH, EVEN_IN_PAD)], [w_even_in],
                    "even_dx", res=dz1)
    dw_qkv = _mm_tn(x2d, dqkv, "even_dw_qkv", tn=QKV // 2, out_dtype=F32)
    dw_bch = _mm_tn(x2d, dbch, "even_dw_bch", tn=BCH // 2, out_dtype=F32)
    dw_f = _mm_tn(x2d, dfl, "even_dw_f", tn=LANES, out_dtype=F32)
    gi_even = _by_owner_cols(jnp.concatenate([dw_qkv, dw_f[:, :FOX_HEADS], dw_bch], axis=1).astype(BF16))

    sent_late = [gi_even, go_even]
    late_h, late_tok = _split_start(sent_late, "scatter4", "scatter_late_start")
    landed = _split_wait(late_h, "scatter_late_wait", late_tok) + _split_wait(early_h, "scatter_early_wait", late_tok)
    chip_blk = lambda g: lax.dynamic_index_in_dim(g, chip, 0, keepdims=False)
    mine = [_with_own(r, chip_blk(g)) for r, g in zip(landed, sent_late + sent_early)]
    theirs = _exchange(mine, "swap2", "swap_grads")
    big_w = [(even_w_in, m_even_w_in, v_even_w_in), (even_w_out, m_even_w_out, v_even_w_out),
             (odd_w_in, m_odd_w_in, v_odd_w_in), (odd_w_out, m_odd_w_out, v_odd_w_out),
             (ffn_w_in, m_ffn_w_in, v_ffn_w_in), (ffn_w_out, m_ffn_w_out, v_ffn_w_out)]
    big_names = ["even_w_in", "even_w_out", "odd_w_in", "odd_w_out", "ffn_w_in", "ffn_w_out"]
    res = {}
    for nm, own, sib, (w, m, v) in zip(big_names, mine, theirs, big_w):
        res[nm] = _adamw([own, sib], w, m, v, f"adamw_{nm}")

    dws_masked = jnp.where(gmask[None], dws, 0.0)
    rep_names = ["odd_w_s", "odd_b_s", "mix_ln_g", "mix_ln_b", "ffn_ln_g", "ffn_ln_b", "even_b_f"]
    rep_grads = [dws_masked, dbs_col.T, jnp.concatenate([d_mg0, d_mg1]), jnp.concatenate([d_mb0, d_mb1]),
                 jnp.concatenate([d_fg0, d_fg1]), jnp.concatenate([d_fb0, d_fb1]), d_bf.reshape(1, FOX_HEADS)]
    rep_w = [(odd_w_s, m_odd_w_s, v_odd_w_s), (odd_b_s, m_odd_b_s, v_odd_b_s), (mix_ln_g, m_mix_ln_g, v_mix_ln_g),
             (mix_ln_b, m_mix_ln_b, v_mix_ln_b), (ffn_ln_g, m_ffn_ln_g, v_ffn_ln_g), (ffn_ln_b, m_ffn_ln_b, v_ffn_ln_b),
             (even_b_f, m_even_b_f, v_even_b_f)]
    rep_rows = [_to_rows(gr) for gr in rep_grads]
    n_rep = sum(r.shape[0] for r in rep_rows)
    pad_rep = (-n_rep) % SUBLANES
    dconv_w = dconv_w8[:CONV_K].reshape(CONV_K, N_CHIPS, LANES).transpose(1, 0, 2).reshape(N_CHIPS * CONV_K, LANES)
    packed = jnp.concatenate(rep_rows + [jnp.zeros((pad_rep, LANES), F32), d_vg.reshape(SUBLANES, LANES),
                                         d_vb.reshape(SUBLANES, LANES), dconv_w, jnp.zeros((4, LANES), F32)], axis=0)
    (gathered,) = _exchange([packed], "gather8", "gather_small_grads")
    base = n_rep + pad_rep
    own_rows = jnp.concatenate([
        lax.dynamic_slice_in_dim(gathered, base + 2 * chip, 2, axis=1),
        lax.dynamic_slice_in_dim(gathered, base + SUBLANES + 2 * chip, 2, axis=1),
        lax.dynamic_slice_in_dim(gathered, base + 2 * SUBLANES + CONV_K * chip, CONV_K, axis=1),
        jnp.zeros((N_DEV, 1, LANES), F32)], axis=1)
    small_parts = jnp.concatenate([gathered[:, :base], own_rows], axis=1)[:, None]

    def pack_small(get):
        rows = [_to_rows(get(tw)) for tw in rep_w] + [jnp.zeros((pad_rep, LANES), F32)]
        rows += [get(sh).reshape(-1, LANES) for sh in ((odd_v_ln_g, m_odd_v_ln_g, v_odd_v_ln_g),
                                                       (odd_v_ln_b, m_odd_v_ln_b, v_odd_v_ln_b),
                                                       (even_conv_w, m_even_conv_w, v_even_conv_w))]
        return jnp.concatenate(rows + [jnp.zeros((1, LANES), F32)], axis=0)[None]

    small_out = _adamw([small_parts], pack_small(lambda tw: tw[0]), pack_small(lambda tw: tw[1]),
                       pack_small(lambda tw: tw[2]), "adamw_small")

    def unpack_small(rows3):
        rows = rows3[0]
        out, off = {}, 0
        for nm, (w, _, _), r in zip(rep_names, rep_w, rep_rows):
            out[nm] = rows[off:off + r.shape[0]].reshape(-1)[:w.size].reshape(w.shape)
            off += r.shape[0]
        off += pad_rep
        out["odd_v_ln_g"] = rows[off:off + 2].reshape(odd_v_ln_g.shape)
        out["odd_v_ln_b"] = rows[off + 2:off + 4].reshape(odd_v_ln_b.shape)
        out["even_conv_w"] = rows[off + 4:off + 4 + CONV_K].reshape(even_conv_w.shape)
        return out

    small = [unpack_small(o) for o in small_out]
    order = ["even_w_in", "even_b_f", "even_conv_w", "even_w_out", "odd_w_in", "odd_v_ln_g", "odd_v_ln_b", "odd_w_s",
             "odd_b_s", "odd_w_out", "mix_ln_g", "mix_ln_b", "ffn_w_in", "ffn_w_out", "ffn_ln_g", "ffn_ln_b"]
    outs = [loss, grad_x[None]]
    for kind in range(4):
        for nm in order:
            outs.append(res[nm][kind] if nm in res else small[kind][nm])
    return tuple(outs)
```

```python
import functools
import math

import jax
import jax.numpy as jnp
from jax import lax
from jax.experimental import pallas as pl
from jax.experimental.pallas import tpu as pltpu

F32 = jnp.float32
BF16 = jnp.bfloat16

D_MODEL = 1024
FOX_HEADS = 8
HEAD_DIM = 64
HEAD_PAIRS = FOX_HEADS // 2
FOX_WIDTH = FOX_HEADS * HEAD_DIM
CONV_WIDTH = 512
CONV_K = 3
QKV = 3 * FOX_WIDTH
BCH = 3 * CONV_WIDTH
EVEN_IN = QKV + FOX_HEADS + BCH
EVEN_IN_PAD = QKV + BCH + 128
GMLP_BLOCK = 128
GMLP_GROUPS = 8
CHUNK = 64
FFN_HIDDEN = 2816
HALF_HIDDEN = FFN_HIDDEN // 2
ALPHA = 4.0 ** 0.25
LN_EPS = 1e-5
ADAM_LR = 0.001
ADAM_B1 = 0.9
ADAM_B2 = 0.999
ADAM_EPS = 1e-08
ADAM_WD = 0.01
ADAM_STEP = 10
N_CHIPS = 4
N_DEV = 8
LANES = 128
SUBLANES = 8
ROW_TILE = 512
FFN_ROW_TILE = 256
ATT_BLOCK = 512
ATT_FWD_HEADS = 4
ATT_BWD_HEADS = 2
VMEM_LIMIT = 56 * 2 ** 20
NEG = -1e30
MESH = pl.DeviceIdType.MESH
HIGHEST = lax.Precision.HIGHEST
Q_C, Q_ONE, Q_LSE = 64, 67, 70
K_ONE, K_C, K_ONE2 = 64, 67, 70
V_ONE = 64
DO_DELTA = 65
NT = (((1,), (1,)), ((), ()))
TN = (((0,), (0,)), ((), ()))


def _cp():
    return pltpu.CompilerParams(vmem_limit_bytes=VMEM_LIMIT)


def _resident(shape):
    zeros = (0,) * len(shape)
    return pl.BlockSpec(shape, lambda *_: zeros, pipeline_mode=pl.Buffered(1))


def _sds(shape, dtype):
    return jax.ShapeDtypeStruct(tuple(shape), dtype)


_MASKS = {
    "gather4": [(1, 0, 0), (0, 1, 0), (1, 1, 0)],
    "scatter4": [(1, 0, 0), (0, 1, 0), (1, 1, 0)],
    "swap2": [(0, 0, 1)],
    "gather8": [(0, 0, 1), (0, 1, 0), (0, 1, 1), (1, 0, 0), (1, 0, 1), (1, 1, 0), (1, 1, 1)],
}


def _exchange(arrs, mode, name):
    n = len(arrs)
    masks = _MASKS[mode]
    npeer = len(masks)
    lead = {"gather4": N_CHIPS, "gather8": N_DEV}.get(mode)
    out_shapes = [_sds(((lead,) if lead else ()) + a.shape, a.dtype) for a in arrs]

    def body(*refs):
        ins, outs = refs[:n], refs[n:2 * n]
        send_sems, recv_sems, loc_sems = refs[2 * n:]
        x, y, c = lax.axis_index("x"), lax.axis_index("y"), lax.axis_index("c")
        chip, dev = 2 * x + y, 4 * x + 2 * y + c
        sends, recvs, locs = [], [], []
        for k in range(n):
            if mode == "gather4":
                locs.append(pltpu.make_async_copy(ins[k], outs[k].at[chip], loc_sems.at[k]))
            elif mode == "scatter4":
                locs.append(pltpu.make_async_copy(ins[k].at[chip], outs[k].at[chip], loc_sems.at[k]))
            elif mode == "gather8":
                locs.append(pltpu.make_async_copy(ins[k], outs[k].at[dev], loc_sems.at[k]))
        for cp in locs:
            cp.start()
        for k in range(n):
            for j, (dx, dy, dc) in enumerate(masks):
                px = 1 - x if dx else x
                py = 1 - y if dy else y
                pc = 1 - c if dc else c
                pchip, pdev = 2 * px + py, 4 * px + 2 * py + pc
                if mode == "gather4":
                    src, dst, land = ins[k], outs[k].at[chip], outs[k].at[pchip]
                elif mode == "scatter4":
                    src, dst, land = ins[k].at[pchip], outs[k].at[chip], outs[k].at[pchip]
                elif mode == "swap2":
                    src, dst, land = ins[k], outs[k], outs[k]
                else:
                    src, dst, land = ins[k], outs[k].at[dev], outs[k].at[pdev]
                s = k * npeer + j
                kw = dict(send_sem=send_sems.at[s], recv_sem=recv_sems.at[s], device_id=(px, py, pc),
                          device_id_type=MESH)
                cp = pltpu.make_async_remote_copy(src_ref=src, dst_ref=dst, **kw)
                cp.start()
                sends.append(cp)
                recvs.append(pltpu.make_async_remote_copy(src_ref=src, dst_ref=land, **kw))
        for cp in recvs:
            cp.wait_recv()
        for cp in sends:
            cp.wait_send()
        for cp in locs:
            cp.wait()

    any_spec = pl.BlockSpec(memory_space=pl.ANY)
    outs = pl.pallas_call(
        body,
        out_shape=out_shapes,
        in_specs=[any_spec] * n,
        out_specs=[any_spec] * n,
        scratch_shapes=[pltpu.SemaphoreType.DMA((n * npeer,)), pltpu.SemaphoreType.DMA((n * npeer,)),
                        pltpu.SemaphoreType.DMA((max(n, 1),))],
        name=name,
    )(*arrs)
    return list(outs)


_HBM_SPEC = pl.BlockSpec(memory_space=pltpu.HBM)
_SEM_SPEC = pl.BlockSpec(memory_space=pltpu.SEMAPHORE)
_ANY_SPEC = pl.BlockSpec(memory_space=pl.ANY)
_EFFECT = pltpu.SideEffectType.DATAFLOW_SIDE_EFFECTING
_CHIP_MASKS = [(1, 0), (0, 1), (1, 1)]


def _split_copies(mode, ins, lands, send_sems, recv_sems):
    x, y, c = lax.axis_index("x"), lax.axis_index("y"), lax.axis_index("c")
    chip = 2 * x + y
    out = []
    for k in range(len(ins)):
        for j, (dx, dy) in enumerate(_CHIP_MASKS):
            px = 1 - x if dx else x
            py = 1 - y if dy else y
            pchip = 2 * px + py
            src = ins[k] if mode == "gather4" else ins[k].at[pchip]
            s = k * len(_CHIP_MASKS) + j
            kw = dict(send_sem=send_sems.at[s], recv_sem=recv_sems.at[s], device_id=(px, py, c), device_id_type=MESH)
            out.append((pltpu.make_async_remote_copy(src_ref=src, dst_ref=lands[k].at[chip], **kw),
                        pltpu.make_async_remote_copy(src_ref=src, dst_ref=lands[k].at[pchip], **kw)))
    return out


def _split_start(arrs, mode, name, after=None):
    n = len(arrs)
    nsem = n * len(_CHIP_MASKS)
    land_shapes = [((N_CHIPS,) + a.shape) if mode == "gather4" else a.shape for a in arrs]

    def body(*refs):
        ins, lands = refs[:n], refs[n:2 * n]
        outs = refs[2 * n + (after is not None):]
        for start, _ in _split_copies(mode, ins, lands, outs[0], outs[1]):
            start.start()
        outs[-1][...] = jnp.zeros(outs[-1].shape, F32)

    srcs = [pltpu.with_memory_space_constraint(a, pltpu.HBM) for a in arrs]
    empties = [pltpu.with_memory_space_constraint(lax.empty(s, a.dtype), pltpu.HBM) for s, a in zip(land_shapes, arrs)]
    res = pl.pallas_call(
        body, name=name,
        out_shape=(pltpu.SemaphoreType.DMA((nsem,)), pltpu.SemaphoreType.DMA((nsem,)),
                   *[pltpu.HBM(a.shape, a.dtype) for a in arrs],
                   *[pltpu.HBM(s, a.dtype) for s, a in zip(land_shapes, arrs)],
                   _sds((SUBLANES, LANES), F32)),
        in_specs=[_HBM_SPEC] * (2 * n) + ([_ANY_SPEC] if after is not None else []),
        out_specs=(_SEM_SPEC, _SEM_SPEC, *[_HBM_SPEC] * (2 * n), pl.BlockSpec(memory_space=pltpu.VMEM)),
        input_output_aliases={k: 2 + k for k in range(2 * n)},
        compiler_params=pltpu.CompilerParams(has_side_effects=_EFFECT),
    )(*srcs, *empties, *([after] if after is not None else []))
    return dict(mode=mode, n=n, sems=res[:2], bufs=res[2:2 + 2 * n]), res[-1]


def _split_wait(handle, name, after):
    n, mode = handle["n"], handle["mode"]

    def body(*refs):
        ins, lands = refs[:n], refs[n:2 * n]
        send_sems, recv_sems = refs[2 * n], refs[2 * n + 1]
        for _, arrival in _split_copies(mode, ins, lands, send_sems, recv_sems):
            arrival.wait_send()
            arrival.wait_recv()

    bufs = handle["bufs"]
    res = pl.pallas_call(
        body, name=name,
        out_shape=tuple(pltpu.HBM(b.shape, b.dtype) for b in bufs),
        in_specs=[_HBM_SPEC] * (2 * n) + [_SEM_SPEC, _SEM_SPEC, _ANY_SPEC],
        out_specs=tuple([_HBM_SPEC] * (2 * n)),
        input_output_aliases={k: k for k in range(2 * n)},
        compiler_params=pltpu.CompilerParams(has_side_effects=_EFFECT),
    )(*bufs, *handle["sems"], after)
    return list(res[n:])


def _with_own(landed, own):
    chip = 2 * lax.axis_index("x") + lax.axis_index("y")
    return lax.dynamic_update_index_in_dim(landed, own, chip, 0)


def _sigmoid(x):
    return 1.0 / (1.0 + jnp.exp(-x))


def _log_sigmoid(x):
    e = jnp.exp(-jnp.abs(x))
    log1p = jnp.where(e < 1e-2, e * (1.0 - e * (0.5 - e * (1.0 / 3.0))), jnp.log(1.0 + e))
    return jnp.minimum(x, 0.0) - log1p


def _gelu(a):
    return 0.5 * a * (1.0 + lax.erf(a * (2.0 ** -0.5)))


def _gelu_grad(a):
    cdf = 0.5 * (1.0 + lax.erf(a * (2.0 ** -0.5)))
    pdf = jnp.exp(-0.5 * a * a) * (1.0 / math.sqrt(2.0 * math.pi))
    return cdf + a * pdf


def _ln_fwd(z):
    mu = jnp.mean(z, axis=-1, keepdims=True)
    zc = z - mu
    var = jnp.mean(zc * zc, axis=-1, keepdims=True)
    rstd = lax.rsqrt(var + LN_EPS)
    return zc * rstd, rstd


def _ln_bwd(dy, xhat, rstd, g):
    dxh = dy * g
    m1 = jnp.mean(dxh, axis=-1, keepdims=True)
    m2 = jnp.mean(dxh * xhat, axis=-1, keepdims=True)
    dz = rstd * (dxh - m1 - xhat * m2)
    return dz, jnp.sum(dy * xhat, axis=0, keepdims=True), jnp.sum(dy, axis=0, keepdims=True)


def _shift_down(z, halo):
    r = lax.broadcasted_iota(jnp.int32, z.shape, 0)
    z1 = jnp.where(r == 0, halo[7:8, :], pltpu.roll(z, 1, 0))
    z2 = jnp.where(r == 0, halo[6:7, :], jnp.where(r == 1, halo[7:8, :], pltpu.roll(z, 2, 0)))
    return z1, z2


def _shift_up(z, halo):
    n = z.shape[0]
    r = lax.broadcasted_iota(jnp.int32, z.shape, 0)
    z1 = jnp.where(r == n - 1, halo[0:1, :], pltpu.roll(z, n - 1, 0))
    z2 = jnp.where(r == n - 1, halo[1:2, :], jnp.where(r == n - 2, halo[0:1, :], pltpu.roll(z, n - 2, 0)))
    return z1, z2


def _accumulate(ref, first, value):
    @pl.when(first)
    def _():
        ref[...] = value

    @pl.when(jnp.logical_not(first))
    def _():
        ref[...] += value


def _proj(x, w, splits, name):
    t, k = x.shape
    tm = min(ROW_TILE, t)

    def body(x_ref, w_ref, *outs):
        a = x_ref[...].astype(BF16)
        for (lo, hi, dt), o in zip(splits, outs):
            o[...] = jnp.dot(a, w_ref[:, lo:hi], preferred_element_type=F32).astype(dt)

    return pl.pallas_call(
        body, grid=(t // tm,),
        in_specs=[pl.BlockSpec((tm, k), lambda i: (i, 0)), _resident(w.shape)],
        out_specs=[pl.BlockSpec((tm, hi - lo), lambda i: (i, 0)) for lo, hi, _ in splits],
        out_shape=[_sds((t, hi - lo), dt) for lo, hi, dt in splits],
        compiler_params=_cp(), name=name)(x, w)


def _fgate_fwd(fl3, b_f):
    nc = fl3.shape[0]

    def body(f_ref, b_ref, c_ref):
        r = lax.broadcasted_iota(jnp.int32, (LANES, LANES), 0)
        cidx = lax.broadcasted_iota(jnp.int32, (LANES, LANES), 1)
        upper = (r <= cidx).astype(F32)

        def step(i, carry):
            lf = _log_sigmoid(f_ref[i] + b_ref[...])
            cc = jnp.dot(lf, upper, precision=HIGHEST, preferred_element_type=F32) + carry
            c_ref[i] = cc
            return cc[:, LANES - 1:LANES]

        lax.fori_loop(0, nc, step, jnp.zeros((FOX_HEADS, 1), F32))

    return pl.pallas_call(body, out_shape=_sds(fl3.shape, F32), name="fgate_fwd")(fl3, b_f)


def _fgate_bwd(dc3, fl3, b_f):
    nc = fl3.shape[0]

    def body(dc_ref, f_ref, b_ref, df_ref, db_ref):
        r = lax.broadcasted_iota(jnp.int32, (LANES, LANES), 0)
        cidx = lax.broadcasted_iota(jnp.int32, (LANES, LANES), 1)
        lower = (r >= cidx).astype(F32)

        def step(n, carry):
            suffix, db = carry
            i = nc - 1 - n
            dlf = jnp.dot(dc_ref[i], lower, precision=HIGHEST, preferred_element_type=F32) + suffix
            df = dlf * (1.0 - _sigmoid(f_ref[i] + b_ref[...]))
            df_ref[i] = df
            return dlf[:, 0:1], db + jnp.sum(df, axis=1, keepdims=True)

        zero = jnp.zeros((FOX_HEADS, 1), F32)
        _, db = lax.fori_loop(0, nc, step, (zero, zero))
        db_ref[...] = db

    return pl.pallas_call(body, out_shape=[_sds(fl3.shape, F32), _sds((FOX_HEADS, 1), F32)],
                          name="fgate_bwd")(dc3, fl3, b_f)


def _split3(c):
    hi = c.astype(BF16).astype(F32)
    mid = (c - hi).astype(BF16).astype(F32)
    lo = (c - hi - mid).astype(BF16).astype(F32)
    return hi, mid, lo


def _lane_pieces(lane, start, pieces, sign):
    out = jnp.zeros(lane.shape, F32)
    for n, p in enumerate(pieces):
        out = jnp.where(lane == start + n, sign * p, out)
    return out


def _attn_pack(qkv, c_col):
    t = qkv.shape[0]
    tm = min(ROW_TILE, t)
    hd = HEAD_DIM

    def body(x_ref, c_ref, qp_ref, kp_ref, vp_ref, kt_ref, vt_ref):
        lane = lax.broadcasted_iota(jnp.int32, (tm, hd), 1) + hd
        for h in range(FOX_HEADS):
            pieces = _split3(c_ref[:, h:h + 1])
            ones = lambda a, b: jnp.where(jnp.logical_and(lane >= a, lane < b), 1.0, 0.0)
            q_extra = _lane_pieces(lane, Q_C, pieces, 1.0) + ones(Q_ONE, Q_ONE + 3)
            k_extra = _lane_pieces(lane, K_C, pieces, -1.0) + ones(K_ONE, K_ONE + 3) + ones(K_ONE2, K_ONE2 + 3)
            qp_ref[h, :, :hd] = (x_ref[:, h * hd:(h + 1) * hd].astype(F32) * (hd ** -0.5)).astype(BF16)
            qp_ref[h, :, hd:] = q_extra.astype(BF16)
            kp_ref[h, :, :hd] = x_ref[:, FOX_WIDTH + h * hd:FOX_WIDTH + (h + 1) * hd]
            kp_ref[h, :, hd:] = k_extra.astype(BF16)
            vp_ref[h, :, :hd] = x_ref[:, 2 * FOX_WIDTH + h * hd:2 * FOX_WIDTH + (h + 1) * hd]
            vp_ref[h, :, hd:] = ones(V_ONE, V_ONE + 4).astype(BF16)
            kt_ref[h] = kp_ref[h].astype(F32).T.astype(BF16)
            vt_ref[h] = vp_ref[h].astype(F32).T.astype(BF16)

    row3 = pl.BlockSpec((FOX_HEADS, tm, LANES), lambda i: (0, i, 0))
    col3 = pl.BlockSpec((FOX_HEADS, LANES, tm), lambda i: (0, 0, i))
    return pl.pallas_call(
        body, grid=(t // tm,),
        in_specs=[pl.BlockSpec((tm, QKV), lambda i: (i, 0)), pl.BlockSpec((tm, FOX_HEADS), lambda i: (i, 0))],
        out_specs=[row3, row3, row3, col3, col3],
        out_shape=[_sds((FOX_HEADS, t, LANES), BF16)] * 3 + [_sds((FOX_HEADS, LANES, t), BF16)] * 2,
        compiler_params=_cp(), name="attn_pack")(qkv, c_col)


def _triangle(nq, key_major):
    if key_major:
        pairs = [(i, j) for j in range(nq) for i in range(j, nq)]
    else:
        pairs = [(i, j) for i in range(nq) for j in range(i + 1)]
    return jnp.asarray([p[0] for p in pairs], jnp.int32), jnp.asarray([p[1] for p in pairs], jnp.int32)


def _attn_fwd(qp, kp, vt):
    t = qp.shape[1]
    bq = min(ATT_BLOCK, t)
    nq = t // bq
    nh = ATT_FWD_HEADS
    i_tab, j_tab = _triangle(nq, key_major=False)

    def body(it_ref, jt_ref, q_ref, k_ref, vt_ref, o_ref, lse_ref, m_sc, acc_sc):
        s = pl.program_id(1)
        i, j = it_ref[s], jt_ref[s]

        @pl.when(j == 0)
        def _():
            m_sc[...] = jnp.full(m_sc.shape, NEG, F32)
            acc_sc[...] = jnp.zeros(acc_sc.shape, F32)

        def sweep(masked):
            scores = lambda h: lax.dot_general(k_ref[h], q_ref[h], NT, preferred_element_type=F32)

            def accumulate(h, pt, rescale):
                acc_sc[h] = rescale * acc_sc[h] + jnp.dot(vt_ref[h], pt, preferred_element_type=F32)

            ahead, behind = scores(0), None
            for h in range(nh):
                st = ahead
                if h + 1 < nh:
                    ahead = scores(h + 1)
                if behind is not None:
                    accumulate(*behind)
                if masked:
                    key = lax.broadcasted_iota(jnp.int32, (bq, bq), 0)
                    qry = lax.broadcasted_iota(jnp.int32, (bq, bq), 1)
                    st = jnp.where(key <= qry, st, NEG)
                m_prev = m_sc[h]
                m_new = jnp.maximum(m_prev, jnp.max(st, axis=0, keepdims=True))
                behind = (h, jnp.exp(st - m_new).astype(BF16), jnp.exp(m_prev - m_new))
                m_sc[h] = m_new
            accumulate(*behind)

        @pl.when(j < i)
        def _():
            sweep(False)

        @pl.when(j == i)
        def _():
            sweep(True)
            for h in range(nh):
                acc = acc_sc[h]
                denom = acc[V_ONE:V_ONE + 1, :]
                o_ref[:, h * HEAD_DIM:(h + 1) * HEAD_DIM] = (acc[:HEAD_DIM, :] / denom).T.astype(BF16)
                lse_ref[h] = m_sc[h] + jnp.log(denom)

    grid_spec = pltpu.PrefetchScalarGridSpec(
        num_scalar_prefetch=2, grid=(FOX_HEADS // nh, i_tab.shape[0]),
        in_specs=[pl.BlockSpec((nh, bq, LANES), lambda hp, s, it, jt: (hp, it[s], 0)),
                  pl.BlockSpec((nh, bq, LANES), lambda hp, s, it, jt: (hp, jt[s], 0)),
                  pl.BlockSpec((nh, LANES, bq), lambda hp, s, it, jt: (hp, 0, jt[s]))],
        out_specs=[pl.BlockSpec((bq, nh * HEAD_DIM), lambda hp, s, it, jt: (it[s], hp)),
                   pl.BlockSpec((nh, 1, bq), lambda hp, s, it, jt: (hp, 0, it[s]))],
        scratch_shapes=[pltpu.VMEM((nh, 1, bq), F32), pltpu.VMEM((nh, LANES, bq), F32)])
    return pl.pallas_call(body, grid_spec=grid_spec,
                          out_shape=[_sds((t, FOX_WIDTH), BF16), _sds((FOX_HEADS, 1, t), F32)],
                          compiler_params=_cp(), name="attn_fwd")(i_tab, j_tab, qp, kp, vt)


def _conv_fwd(bch, conv_w):
    t = bch.shape[0]
    tm = min(ROW_TILE, t)
    halo_blocks = tm // SUBLANES
    cw = CONV_WIDTH

    def body(cur_ref, prev_ref, w_ref, o_ref):
        i = pl.program_id(0)
        z = cur_ref[:, cw:2 * cw] * cur_ref[:, 2 * cw:]
        zp = jnp.where(i == 0, 0.0, prev_ref[:, cw:2 * cw] * prev_ref[:, 2 * cw:])
        z1, z2 = _shift_down(z, zp)
        y = w_ref[0:1, :] * z2 + w_ref[1:2, :] * z1 + w_ref[2:3, :] * z
        o_ref[...] = (cur_ref[:, :cw] * y).astype(BF16)

    return pl.pallas_call(
        body, grid=(t // tm,),
        in_specs=[pl.BlockSpec((tm, BCH), lambda i: (i, 0)),
                  pl.BlockSpec((SUBLANES, BCH), lambda i: (jnp.maximum(i * halo_blocks - 1, 0), 0)),
                  _resident(conv_w.shape)],
        out_specs=pl.BlockSpec((tm, cw), lambda i: (i, 0)),
        out_shape=_sds((t, cw), BF16), compiler_params=_cp(), name="conv_fwd")(bch, bch, conv_w)


def _mm_res_ln(pairs, res, g, b, name):
    t, d = res.shape
    tm = min(ROW_TILE, t)
    n = len(pairs)

    def body(*refs):
        a_refs, w_refs = refs[:n], refs[n:2 * n]
        res_ref, g_ref, b_ref, y_ref, xh_ref, rs_ref = refs[2 * n:]
        z = ALPHA * res_ref[...]
        for a_ref, w_ref in zip(a_refs, w_refs):
            z = z + jnp.dot(a_ref[...].astype(BF16), w_ref[...], preferred_element_type=F32)
        xhat, rstd = _ln_fwd(z)
        y_ref[...] = xhat * g_ref[...] + b_ref[...]
        xh_ref[...] = xhat
        rs_ref[...] = rstd

    row = lambda i: (i, 0)
    return pl.pallas_call(
        body, grid=(t // tm,),
        in_specs=[pl.BlockSpec((tm, a.shape[1]), row) for a, _ in pairs] + [_resident(w.shape) for _, w in pairs]
        + [pl.BlockSpec((tm, d), row), _resident(g.shape), _resident(b.shape)],
        out_specs=[pl.BlockSpec((tm, d), row), pl.BlockSpec((tm, d), row), pl.BlockSpec((tm, 1), row)],
        out_shape=[_sds((t, d), F32), _sds((t, d), F32), _sds((t, 1), F32)],
        compiler_params=_cp(), name=name)(*[a for a, _ in pairs], *[w for _, w in pairs], res, g, b)


def _ffn_in(x, wi, name):
    t, d = x.shape
    tm = min(FFN_ROW_TILE, t)
    hh = HALF_HIDDEN

    def body(x_ref, w_ref, gu_ref, h_ref):
        a = x_ref[...].astype(BF16)
        for c in range(2):
            gs, us = slice(c * hh, (c + 1) * hh), slice(FFN_HIDDEN + c * hh, FFN_HIDDEN + (c + 1) * hh)
            g = jnp.dot(a, w_ref[:, gs], preferred_element_type=F32)
            u = jnp.dot(a, w_ref[:, us], preferred_element_type=F32)
            gu_ref[:, gs] = g.astype(BF16)
            gu_ref[:, us] = u.astype(BF16)
            h_ref[:, gs] = (g * _sigmoid(g) * u).astype(BF16)

    row = lambda i: (i, 0)
    return pl.pallas_call(
        body, grid=(t // tm,),
        in_specs=[pl.BlockSpec((tm, d), row), _resident(wi.shape)],
        out_specs=[pl.BlockSpec((tm, 2 * FFN_HIDDEN), row), pl.BlockSpec((tm, FFN_HIDDEN), row)],
        out_shape=[_sds((t, 2 * FFN_HIDDEN), BF16), _sds((t, FFN_HIDDEN), BF16)],
        compiler_params=_cp(), name=name)(x, wi)


def _gmlp_fwd(x, w_in, vg, vb, wm, bs_col):
    t, d = x.shape
    tm = min(ROW_TILE, t)
    gb = GMLP_BLOCK

    def body(x_ref, w_ref, vg_ref, vb_ref, wm_ref, bs_ref, a_ref, o_ref):
        a = jnp.dot(x_ref[...].astype(BF16), w_ref[...], preferred_element_type=F32)
        a_ref[...] = a
        u = _gelu(a[:, :d])
        vhat, _ = _ln_fwd(_gelu(a[:, d:]))
        vln = (vhat * vg_ref[...] + vb_ref[...]).astype(BF16)
        for blk in range(tm // gb):
            rs = slice(blk * gb, (blk + 1) * gb)
            for gi in range(GMLP_GROUPS):
                cs = slice(gi * gb, (gi + 1) * gb)
                s = jnp.dot(wm_ref[gi], vln[rs, cs], preferred_element_type=F32) + bs_ref[:, gi:gi + 1]
                o_ref[rs, cs] = (u[rs, cs] * s).astype(BF16)

    row = lambda i: (i, 0)
    return pl.pallas_call(
        body, grid=(t // tm,),
        in_specs=[pl.BlockSpec((tm, d), row), _resident(w_in.shape), _resident(vg.shape), _resident(vb.shape),
                  _resident(wm.shape), _resident(bs_col.shape)],
        out_specs=[pl.BlockSpec((tm, 2 * d), row), pl.BlockSpec((tm, d), row)],
        out_shape=[_sds((t, 2 * d), F32), _sds((t, d), BF16)],
        compiler_params=_cp(), name="gmlp_fwd")(x, w_in, vg, vb, wm, bs_col)


def _loss_ln_bwd(xhat, rstd, g, b, target):
    t, d = xhat.shape
    tm = min(ROW_TILE, t)

    def body(xh_ref, rs_ref, g_ref, b_ref, t_ref, sq_ref, dz_ref, dg_ref, db_ref):
        first = pl.program_id(0) == 0
        xh = xh_ref[...]
        err = xh * g_ref[...] + b_ref[...] - t_ref[...]
        dz, dg, db = _ln_bwd(err * (1.0 / d), xh, rs_ref[...], g_ref[...])
        dz_ref[...] = dz
        _accumulate(sq_ref, first, jnp.sum(err * err, axis=0, keepdims=True))
        _accumulate(dg_ref, first, dg)
        _accumulate(db_ref, first, db)

    row = lambda i: (i, 0)
    vec = pl.BlockSpec((1, d), lambda i: (0, 0))
    return pl.pallas_call(
        body, grid=(t // tm,),
        in_specs=[pl.BlockSpec((tm, d), row), pl.BlockSpec((tm, 1), row), _resident(g.shape), _resident(b.shape),
                  pl.BlockSpec((tm, d), row)],
        out_specs=[vec, pl.BlockSpec((tm, d), row), vec, vec],
        out_shape=[_sds((1, d), F32), _sds((t, d), F32), _sds((1, d), F32), _sds((1, d), F32)],
        compiler_params=_cp(), name="loss_ln_bwd")(xhat, rstd, g, b, target)


def _mm_nt(pairs, ws, name, *, tm=ROW_TILE, res=None, ln=None, out_dtype=F32, after=None):
    t = pairs[0][0].shape[0]
    k = ws[0].shape[0]
    tm = min(tm, t)
    n, nw = len(pairs), len(ws)

    def body(*refs):
        refs = refs[after is not None:]
        a_refs, w_refs = refs[:n], refs[n:n + nw]
        rest = list(refs[n + nw:])
        dx = None
        for a_ref, (_, wi, lo, hi) in zip(a_refs, pairs):
            part = lax.dot_general(a_ref[...].astype(BF16), w_refs[wi][:, lo:hi], NT, preferred_element_type=F32)
            dx = part if dx is None else dx + part
        if res is not None:
            dx = dx + ALPHA * rest.pop(0)[...]
        if ln is None:
            rest[0][...] = dx.astype(out_dtype)
            return
        xh_ref, rs_ref, g_ref, dz_ref, dg_ref, db_ref = rest
        first = pl.program_id(0) == 0
        dz, dg, db = _ln_bwd(dx, xh_ref[...], rs_ref[...], g_ref[...])
        dz_ref[...] = dz
        _accumulate(dg_ref, first, dg)
        _accumulate(db_ref, first, db)

    row = lambda i: (i, 0)
    in_specs = [pl.BlockSpec((tm, a.shape[1]), row) for a, _, _, _ in pairs] + [_resident(w.shape) for w in ws]
    args = [a for a, _, _, _ in pairs] + list(ws)
    if res is not None:
        in_specs.append(pl.BlockSpec((tm, k), row))
        args.append(res)
    if ln is None:
        out_specs = pl.BlockSpec((tm, k), row)
        out_shape = _sds((t, k), out_dtype)
    else:
        xhat, rstd, g = ln
        in_specs += [pl.BlockSpec((tm, k), row), pl.BlockSpec((tm, 1), row), _resident(g.shape)]
        args += [xhat, rstd, g]
        vec = pl.BlockSpec((1, k), lambda i: (0, 0))
        out_specs = [pl.BlockSpec((tm, k), row), vec, vec]
        out_shape = [_sds((t, k), F32), _sds((1, k), F32), _sds((1, k), F32)]
    if after is not None:
        in_specs.insert(0, _ANY_SPEC)
        args.insert(0, after)
    return pl.pallas_call(body, grid=(t // tm,), in_specs=in_specs, out_specs=out_specs, out_shape=out_shape,
                          compiler_params=_cp(), name=name)(*args)


def _mm_tn(a, b, name, *, tn, tk=None, stack_cols=False, out_dtype=BF16):
    t, k = a.shape
    n = b.shape[1]
    tk = k if tk is None else tk
    tt = min(ROW_TILE, t)
    nt = t // tt

    def body(a_ref, b_ref, o_ref, acc_ref):
        s = pl.program_id(2)
        part = lax.dot_general(a_ref[...].astype(BF16), b_ref[...].astype(BF16), TN, preferred_element_type=F32)
        _accumulate(acc_ref, s == 0, part)

        @pl.when(s == nt - 1)
        def _():
            o_ref[...] = acc_ref[...].astype(out_dtype).reshape(o_ref.shape)

    if stack_cols:
        assert tk == k
        out_spec = pl.BlockSpec((1, k, tn), lambda kk, j, s: (j, 0, 0))
        out_shape = _sds((n // tn, k, tn), out_dtype)
    else:
        out_spec = pl.BlockSpec((tk, tn), lambda kk, j, s: (kk, j))
        out_shape = _sds((k, n), out_dtype)
    return pl.pallas_call(
        body, grid=(k // tk, n // tn, nt),
        in_specs=[pl.BlockSpec((tt, tk), lambda kk, j, s: (s, kk)), pl.BlockSpec((tt, tn), lambda kk, j, s: (s, j))],
        out_specs=out_spec, out_shape=out_shape,
        scratch_shapes=[pltpu.VMEM((tk, tn), F32)],
        compiler_params=_cp(), name=name)(a, b)


def _ffn_bwd_hidden(dz, wo, gu, name):
    t, d = dz.shape
    tm = min(FFN_ROW_TILE, t)
    hh = HALF_HIDDEN

    def body(dz_ref, w_ref, gu_ref, o_ref):
        a = dz_ref[...].astype(BF16)
        for c in range(2):
            gs, us = slice(c * hh, (c + 1) * hh), slice(FFN_HIDDEN + c * hh, FFN_HIDDEN + (c + 1) * hh)
            dh = lax.dot_general(a, w_ref[gs, :], NT, preferred_element_type=F32)
            g = gu_ref[:, gs].astype(F32)
            u = gu_ref[:, us].astype(F32)
            sig = _sigmoid(g)
            o_ref[:, gs] = (dh * u * sig * (1.0 + g * (1.0 - sig))).astype(BF16)
            o_ref[:, us] = (dh * g * sig).astype(BF16)

    row = lambda i: (i, 0)
    return pl.pallas_call(
        body, grid=(t // tm,),
        in_specs=[pl.BlockSpec((tm, d), row), _resident(wo.shape), pl.BlockSpec((tm, 2 * FFN_HIDDEN), row)],
        out_specs=pl.BlockSpec((tm, 2 * FFN_HIDDEN), row),
        out_shape=_sds((t, 2 * FFN_HIDDEN), BF16), compiler_params=_cp(), name=name)(dz, wo, gu)


def _gmlp_bwd(dgated, a, vg, vb, wm, bs_col):
    t, d2 = a.shape
    d = d2 // 2
    tm = min(ROW_TILE, t)
    gb = GMLP_BLOCK

    def body(dg_ref, a_ref, vg_ref, vb_ref, wm_ref, bs_ref, da_ref, dws_ref, dbs_ref, dvg_ref, dvb_ref, dvln_sc):
        first = pl.program_id(0) == 0
        au, av = a_ref[:, :d], a_ref[:, d:]
        u = _gelu(au)
        vhat, rstd = _ln_fwd(_gelu(av))
        vln = (vhat * vg_ref[...] + vb_ref[...]).astype(BF16)
        dgate = dg_ref[...]

        @pl.when(first)
        def _():
            dws_ref[...] = jnp.zeros(dws_ref.shape, F32)
            dbs_ref[...] = jnp.zeros(dbs_ref.shape, F32)

        for blk in range(tm // gb):
            rs = slice(blk * gb, (blk + 1) * gb)
            for gi in range(GMLP_GROUPS):
                cs = slice(gi * gb, (gi + 1) * gb)
                vblk = vln[rs, cs]
                s = jnp.dot(wm_ref[gi], vblk, preferred_element_type=F32) + bs_ref[:, gi:gi + 1]
                dgb = dgate[rs, cs]
                da_ref[rs, cs] = (dgb * s * _gelu_grad(au[rs, cs])).astype(BF16)
                ds = dgb * u[rs, cs]
                dsb = ds.astype(BF16)
                dws_ref[gi] += lax.dot_general(dsb, vblk, NT, preferred_element_type=F32)
                dbs_ref[:, gi:gi + 1] += jnp.sum(ds, axis=1, keepdims=True)
                dvln_sc[rs, cs] = lax.dot_general(wm_ref[gi], dsb, TN, preferred_element_type=F32)
        dv, dvg, dvb = _ln_bwd(dvln_sc[...], vhat, rstd, vg_ref[...])
        da_ref[:, d:] = (dv * _gelu_grad(av)).astype(BF16)
        _accumulate(dvg_ref, first, dvg)
        _accumulate(dvb_ref, first, dvb)

    row = lambda i: (i, 0)
    vec = pl.BlockSpec((1, d), lambda i: (0, 0))
    return pl.pallas_call(
        body, grid=(t // tm,),
        in_specs=[pl.BlockSpec((tm, d), row), pl.BlockSpec((tm, d2), row), _resident(vg.shape), _resident(vb.shape),
                  _resident(wm.shape), _resident(bs_col.shape)],
        out_specs=[pl.BlockSpec((tm, d2), row), pl.BlockSpec(wm.shape, lambda i: (0, 0, 0)),
                   pl.BlockSpec(bs_col.shape, lambda i: (0, 0)), vec, vec],
        out_shape=[_sds((t, d2), BF16), _sds(wm.shape, F32), _sds(bs_col.shape, F32), _sds((1, d), F32), _sds((1, d), F32)],
        scratch_shapes=[pltpu.VMEM((tm, d), F32)],
        compiler_params=_cp(), name="gmlp_bwd")(dgated, a, vg, vb, wm, bs_col)


def _conv_bwd(bch, dmix, conv_w):
    t = bch.shape[0]
    tm = min(ROW_TILE, t)
    nb = t // tm
    halo_blocks = tm // SUBLANES
    cw = CONV_WIDTH

    def body(cur_ref, prev_ref, next_ref, dc_ref, dn_ref, w_ref, o_ref, dw_ref):
        i = pl.program_id(0)
        bgate, cgate, hval = cur_ref[:, :cw], cur_ref[:, cw:2 * cw], cur_ref[:, 2 * cw:]
        z = cgate * hval
        zp = jnp.where(i == 0, 0.0, prev_ref[:, cw:2 * cw] * prev_ref[:, 2 * cw:])
        z1, z2 = _shift_down(z, zp)
        w0, w1, w2 = w_ref[0:1, :], w_ref[1:2, :], w_ref[2:3, :]
        dconv = dc_ref[...]
        o_ref[:, :cw] = (dconv * (w0 * z2 + w1 * z1 + w2 * z)).astype(BF16)
        dy = dconv * bgate
        dyn = jnp.where(i == nb - 1, 0.0, dn_ref[...] * next_ref[:, :cw])
        dy1, dy2 = _shift_up(dy, dyn)
        dz = w2 * dy + w1 * dy1 + w0 * dy2
        o_ref[:, cw:2 * cw] = (dz * hval).astype(BF16)
        o_ref[:, 2 * cw:] = (dz * cgate).astype(BF16)

        @pl.when(i == 0)
        def _():
            dw_ref[...] = jnp.zeros(dw_ref.shape, F32)

        for tap, zs in enumerate((z2, z1, z)):
            dw_ref[tap:tap + 1, :] += jnp.sum(dy * zs, axis=0, keepdims=True)

    last_halo = t // SUBLANES - 1
    return pl.pallas_call(
        body, grid=(nb,),
        in_specs=[pl.BlockSpec((tm, BCH), lambda i: (i, 0)),
                  pl.BlockSpec((SUBLANES, BCH), lambda i: (jnp.maximum(i * halo_blocks - 1, 0), 0)),
                  pl.BlockSpec((SUBLANES, BCH), lambda i: (jnp.minimum((i + 1) * halo_blocks, last_halo), 0)),
                  pl.BlockSpec((tm, cw), lambda i: (i, 1)),
                  pl.BlockSpec((SUBLANES, cw), lambda i: (jnp.minimum((i + 1) * halo_blocks, last_halo), 1)),
                  _resident(conv_w.shape)],
        out_specs=[pl.BlockSpec((tm, BCH), lambda i: (i, 0)), pl.BlockSpec((SUBLANES, cw), lambda i: (0, 0))],
        out_shape=[_sds((t, BCH), BF16), _sds((SUBLANES, cw), F32)],
        compiler_params=_cp(), name="conv_bwd")(bch, bch, bch, dmix, dmix, conv_w)


def _attn_bwd_prep(o, dmix, qp, lse_col):
    t = o.shape[0]
    tm = min(ROW_TILE, t)
    hd = HEAD_DIM

    def body(o_ref, do_ref, qp_ref, lse_ref, qb_ref, dob_ref):
        lane = lax.broadcasted_iota(jnp.int32, (tm, hd), 1) + hd
        for h in range(FOX_HEADS):
            do = do_ref[:, h * hd:(h + 1) * hd]
            delta = jnp.sum(o_ref[:, h * hd:(h + 1) * hd].astype(F32) * do, axis=-1, keepdims=True)
            dob_ref[h, :, :hd] = do.astype(BF16)
            dob_ref[h, :, hd:] = _lane_pieces(lane, DO_DELTA, _split3(delta), -1.0).astype(BF16)
            qb_ref[h, :, :hd] = qp_ref[h, :, :hd]
            qb_ref[h, :, hd:] = (qp_ref[h, :, hd:].astype(F32)
                                 + _lane_pieces(lane, Q_LSE, _split3(lse_ref[:, h:h + 1]), -1.0)).astype(BF16)

    row3 = pl.BlockSpec((FOX_HEADS, tm, LANES), lambda i: (0, i, 0))
    return pl.pallas_call(
        body, grid=(t // tm,),
        in_specs=[pl.BlockSpec((tm, FOX_WIDTH), lambda i: (i, 0)), pl.BlockSpec((tm, FOX_WIDTH), lambda i: (i, 0)), row3,
                  pl.BlockSpec((tm, FOX_HEADS), lambda i: (i, 0))],
        out_specs=[row3, row3], out_shape=[_sds((FOX_HEADS, t, LANES), BF16)] * 2,
        compiler_params=_cp(), name="attn_bwd_prep")(o, dmix, qp, lse_col)


def _attn_bwd(qb, kp, vp, dob, kt):
    t = qb.shape[1]
    bq = min(ATT_BLOCK, t)
    nq = t // bq
    i_tab, j_tab = _triangle(nq, key_major=True)

    def body(it_ref, jt_ref, q_ref, k_ref, v_ref, do_ref, kt_ref, dqt_ref, dk_ref, dv_ref, dk_sc, dv_sc):
        s = pl.program_id(1)
        i, j = it_ref[s], jt_ref[s]

        @pl.when(s == 0)
        def _():
            dqt_ref[...] = jnp.zeros(dqt_ref.shape, F32)

        @pl.when(i == j)
        def _():
            dk_sc[...] = jnp.zeros(dk_sc.shape, F32)
            dv_sc[...] = jnp.zeros(dv_sc.shape, F32)

        cols = pl.ds(pl.multiple_of(i * bq, bq), bq)

        def sweep(masked):
            def scores(h):
                return (lax.dot_general(k_ref[h], q_ref[h], NT, preferred_element_type=F32),
                        lax.dot_general(v_ref[h], do_ref[h], NT, preferred_element_type=F32))

            def accumulate(h, ptb, dstb):
                dv_sc[h] += jnp.dot(ptb, do_ref[h], preferred_element_type=F32)
                dk_sc[h] += jnp.dot(dstb, q_ref[h], preferred_element_type=F32)
                dqt_ref[h, :, cols] += jnp.dot(kt_ref[h], dstb, preferred_element_type=F32)

            ahead, behind = scores(0), None
            for h in range(ATT_BWD_HEADS):
                st, dpt = ahead
                if h + 1 < ATT_BWD_HEADS:
                    ahead = scores(h + 1)
                if behind is not None:
                    accumulate(*behind)
                if masked:
                    key = lax.broadcasted_iota(jnp.int32, (bq, bq), 0)
                    qry = lax.broadcasted_iota(jnp.int32, (bq, bq), 1)
                    st = jnp.where(key <= qry, st, NEG)
                pt = jnp.exp(st)
                behind = (h, pt.astype(BF16), (pt * dpt).astype(BF16))
            accumulate(*behind)

        @pl.when(i == j)
        def _():
            sweep(True)

        @pl.when(i > j)
        def _():
            sweep(False)

        @pl.when(i == nq - 1)
        def _():
            dk_ref[...] = dk_sc[...]
            dv_ref[...] = dv_sc[...].astype(BF16)

    qblk = pl.BlockSpec((2, bq, LANES), lambda hp, s, it, jt: (hp, it[s], 0))
    kblk = pl.BlockSpec((2, bq, LANES), lambda hp, s, it, jt: (hp, jt[s], 0))
    grid_spec = pltpu.PrefetchScalarGridSpec(
        num_scalar_prefetch=2, grid=(HEAD_PAIRS, i_tab.shape[0]),
        in_specs=[qblk, kblk, kblk, qblk, pl.BlockSpec((2, LANES, bq), lambda hp, s, it, jt: (hp, 0, jt[s]))],
        out_specs=[pl.BlockSpec((2, LANES, t), lambda hp, s, it, jt: (hp, 0, 0)), kblk, kblk],
        scratch_shapes=[pltpu.VMEM((2, bq, LANES), F32), pltpu.VMEM((2, bq, LANES), F32)])
    return pl.pallas_call(body, grid_spec=grid_spec,
                          out_shape=[_sds((FOX_HEADS, LANES, t), F32), _sds((FOX_HEADS, t, LANES), F32),
                                     _sds((FOX_HEADS, t, LANES), BF16)],
                          compiler_params=_cp(), name="attn_bwd")(i_tab, j_tab, qb, kp, vp, dob, kt)


def _attn_unpack(dqt, dkp, dvp):
    t = dkp.shape[1]
    tm = min(ROW_TILE, t)
    hd = HEAD_DIM

    def body(dqt_ref, dk_ref, dv_ref, o_ref, dc_ref):
        for h in range(FOX_HEADS):
            dq = dqt_ref[h].T
            o_ref[:, h * hd:(h + 1) * hd] = (dq[:, :hd] * (hd ** -0.5)).astype(BF16)
            o_ref[:, FOX_WIDTH + h * hd:FOX_WIDTH + (h + 1) * hd] = dk_ref[h, :, :hd].astype(BF16)
            o_ref[:, 2 * FOX_WIDTH + h * hd:2 * FOX_WIDTH + (h + 1) * hd] = dv_ref[h, :, :hd]
            dc_ref[:, h:h + 1] = dq[:, K_ONE:K_ONE + 1] - dk_ref[h, :, Q_ONE:Q_ONE + 1]

    row3 = pl.BlockSpec((FOX_HEADS, tm, LANES), lambda i: (0, i, 0))
    return pl.pallas_call(
        body, grid=(t // tm,),
        in_specs=[pl.BlockSpec((FOX_HEADS, LANES, tm), lambda i: (0, 0, i)), row3, row3],
        out_specs=[pl.BlockSpec((tm, QKV), lambda i: (i, 0)), pl.BlockSpec((tm, FOX_HEADS), lambda i: (i, 0))],
        out_shape=[_sds((t, QKV), BF16), _sds((t, FOX_HEADS), F32)],
        compiler_params=_cp(), name="attn_unpack")(dqt, dkp, dvp)


def _adamw(parts, w, m, v, name):
    nl, r, c = w.shape
    tr = r
    for cand in (256, 128, 64, 32, 16):
        if r > cand and r % cand == 0:
            tr = cand
            break
    npart = len(parts)
    bc1 = 1.0 - ADAM_B1 ** ADAM_STEP
    bc2 = 1.0 - ADAM_B2 ** ADAM_STEP

    def body(*refs):
        p_refs = refs[:npart]
        w_ref, m_ref, v_ref, g_ref, d_ref, nm_ref, nv_ref = refs[npart:]
        sums = []
        for p_ref in p_refs:
            acc = p_ref[0, 0].astype(F32)
            for s in range(1, p_ref.shape[0]):
                acc = acc + p_ref[s, 0].astype(F32)
            sums.append(acc)
        g = sums[0]
        for extra in sums[1:]:
            g = g + extra
        nm = ADAM_B1 * m_ref[0] + (1.0 - ADAM_B1) * g
        nv = ADAM_B2 * v_ref[0] + (1.0 - ADAM_B2) * (g * g)
        m_hat = nm / bc1
        v_hat = nv / bc2
        g_ref[0] = g
        d_ref[0] = -ADAM_LR * (m_hat / (jnp.sqrt(v_hat) + ADAM_EPS) + ADAM_WD * w_ref[0])
        nm_ref[0] = nm
        nv_ref[0] = nv

    blk = pl.BlockSpec((1, tr, c), lambda l, i: (l, i, 0))
    return pl.pallas_call(
        body, grid=(nl, r // tr),
        in_specs=[pl.BlockSpec((p.shape[0], 1, tr, c), lambda l, i: (0, l, i, 0)) for p in parts] + [blk, blk, blk],
        out_specs=[blk] * 4, out_shape=[_sds(w.shape, F32)] * 4,
        compiler_params=_cp(), name=name)(*parts, w, m, v)


def _to_rows(a):
    flat = a.reshape(-1)
    pad = (-flat.shape[0]) % LANES
    if pad:
        flat = jnp.concatenate([flat, jnp.zeros((pad,), flat.dtype)])
    return flat.reshape(-1, LANES)


def _by_owner_cols(dw):
    k, n = dw.shape
    return dw.reshape(k, N_CHIPS, n // N_CHIPS).transpose(1, 0, 2)[:, None]


def _ffn_fwd(xin, wi, wo, g, b, layer):
    gu, h = _ffn_in(xin, wi, f"ffn_in_{layer}")
    y, xhat, rstd = _mm_res_ln([(h, wo)], xin, g, b, f"ffn_out_ln_{layer}")
    return y, (xin, gu, h, xhat, rstd)


def _ffn_bwd(dz, saved, wi, wo, ln_below, layer):
    xin, gu, h, _, _ = saved
    dgu = _ffn_bwd_hidden(dz, wo, gu, f"ffn_bwd_hidden_{layer}")
    g_out = _mm_tn(h, dz, f"ffn_dw_out_{layer}", tn=D_MODEL, tk=HALF_HIDDEN)
    g_in = _mm_tn(xin, dgu, f"ffn_dw_in_{layer}", tn=HALF_HIDDEN, stack_cols=True)
    below = _mm_nt([(dgu, 0, 0, 2 * FFN_HIDDEN)], [wi], f"ffn_dx_{layer}", tm=FFN_ROW_TILE, res=dz, ln=ln_below)
    return below, g_in, g_out.reshape(N_CHIPS, FFN_HIDDEN // N_CHIPS, D_MODEL)


def kernel(x, even_w_in, even_b_f, even_conv_w, even_w_out, odd_w_in, odd_v_ln_g, odd_v_ln_b, odd_w_s, odd_b_s, odd_w_out, mix_ln_g, mix_ln_b, ffn_w_in, ffn_w_out, ffn_ln_g, ffn_ln_b, loss_target, m_even_w_in, m_even_b_f, m_even_conv_w, m_even_w_out, m_odd_w_in, m_odd_v_ln_g, m_odd_v_ln_b, m_odd_w_s, m_odd_b_s, m_odd_w_out, m_mix_ln_g, m_mix_ln_b, m_ffn_w_in, m_ffn_w_out, m_ffn_ln_g, m_ffn_ln_b, v_even_w_in, v_even_b_f, v_even_conv_w, v_even_w_out, v_odd_w_in, v_odd_v_ln_g, v_odd_v_ln_b, v_odd_w_s, v_odd_b_s, v_odd_w_out, v_mix_ln_g, v_mix_ln_b, v_ffn_w_in, v_ffn_w_out, v_ffn_ln_g, v_ffn_ln_b):
    t = x.shape[1]
    d = D_MODEL
    chip = 2 * lax.axis_index("x") + lax.axis_index("y")
    x2d = x[0]
    target = loss_target[0]

    small_shard = jnp.concatenate([odd_v_ln_g.reshape(2, LANES), odd_v_ln_b.reshape(2, LANES),
                                   even_conv_w.reshape(CONV_K, LANES), jnp.zeros((1, LANES), F32)], axis=0)
    first = [even_w_in[0].astype(BF16), even_w_out[0].astype(BF16), small_shard]
    later = [odd_w_in[0].astype(BF16), odd_w_out[0].astype(BF16), ffn_w_in.astype(BF16), ffn_w_out.astype(BF16)]
    first_h, first_tok = _split_start(first, "gather4", "gather_first_start")
    later_h, later_tok = _split_start(later, "gather4", "gather_later_start", after=first_tok)
    g_ewi, g_ewo, g_small = [_with_own(g, own) for g, own in
                             zip(_split_wait(first_h, "gather_first_wait", later_tok), first)]
    ewi = g_ewi.transpose(1, 0, 2).reshape(d, EVEN_IN)
    w_even_in = jnp.concatenate([ewi[:, :QKV], ewi[:, QKV + FOX_HEADS:], ewi[:, QKV:QKV + FOX_HEADS],
                                 jnp.zeros((d, LANES - FOX_HEADS), BF16)], axis=1)
    w_even_out = g_ewo.reshape(d, d)
    v_ln_g = g_small[:, 0:2].reshape(1, d)
    v_ln_b = g_small[:, 2:4].reshape(1, d)
    conv_w = g_small[:, 4:7].transpose(1, 0, 2).reshape(CONV_K, CONV_WIDTH)
    chunk_id = jnp.arange(GMLP_BLOCK) // CHUNK
    gmask = chunk_id[None, :] <= chunk_id[:, None]
    w_spatial = jnp.where(gmask[None], odd_w_s[0], 0.0).astype(BF16)
    bs_col = odd_b_s[0].T
    b_f_col = even_b_f.reshape(FOX_HEADS, 1)
    ln = lambda p, l: p[l:l + 1]

    qkv, bch, fl = _proj(x2d, w_even_in, [(0, QKV, BF16), (QKV, QKV + BCH, F32), (QKV + BCH, EVEN_IN_PAD, F32)], "even_proj")
    fl3 = fl[:, :FOX_HEADS].T.reshape(FOX_HEADS, t // LANES, LANES).transpose(1, 0, 2)
    c3 = _fgate_fwd(fl3, b_f_col)
    c_rows = c3.transpose(1, 0, 2).reshape(FOX_HEADS, t)
    qp, kp, vp, kt, vt = _attn_pack(qkv, c_rows.T)
    attn, lse = _attn_fwd(qp, kp, vt)
    conv = _conv_fwd(bch, conv_w)
    x1, xh1, rs1 = _mm_res_ln([(attn, w_even_out[:FOX_WIDTH]), (conv, w_even_out[FOX_WIDTH:])], x2d,
                              ln(mix_ln_g, 0), ln(mix_ln_b, 0), "even_out_ln")
    g_owi, g_owo, g_fwi, g_fwo = [_with_own(g, own) for g, own in
                                  zip(_split_wait(later_h, "gather_later_wait", x1), later)]
    w_odd_in = g_owi.transpose(1, 0, 2).reshape(d, 2 * d)
    w_odd_out = g_owo.reshape(d, d)
    w_ffn_in = [g_fwi[:, l].transpose(1, 0, 2).reshape(d, 2 * FFN_HIDDEN) for l in range(2)]
    w_ffn_out = [g_fwo[:, l].reshape(FFN_HIDDEN, d) for l in range(2)]
    x2, ffn0 = _ffn_fwd(x1, w_ffn_in[0], w_ffn_out[0], ln(ffn_ln_g, 0), ln(ffn_ln_b, 0), 0)

    a_odd, gated = _gmlp_fwd(x2, w_odd_in, v_ln_g, v_ln_b, w_spatial, bs_col)
    x3, xh3, rs3 = _mm_res_ln([(gated, w_odd_out)], x2, ln(mix_ln_g, 1), ln(mix_ln_b, 1), "odd_out_ln")
    _, ffn1 = _ffn_fwd(x3, w_ffn_in[1], w_ffn_out[1], ln(ffn_ln_g, 1), ln(ffn_ln_b, 1), 1)

    sq, dz4, d_fg1, d_fb1 = _loss_ln_bwd(ffn1[3], ffn1[4], ln(ffn_ln_g, 1), ln(ffn_ln_b, 1), target)
    loss = lax.psum(0.5 / d * jnp.sum(sq), ("x", "y", "c"))
    (dz3, d_mg1, d_mb1), gi_f1, go_f1 = _ffn_bwd(dz4, ffn1, w_ffn_in[1], w_ffn_out[1], (xh3, rs3, ln(mix_ln_g, 1)), 1)

    dgated = _mm_nt([(dz3, 0, 0, d)], [w_odd_out], "odd_dgated")
    go_odd = _mm_tn(gated, dz3, "odd_dw_out", tn=d).reshape(N_CHIPS, 1, d // N_CHIPS, d)
    da_odd, dws, dbs_col, d_vg, d_vb = _gmlp_bwd(dgated, a_odd, v_ln_g, v_ln_b, w_spatial, bs_col)
    gi_odd = _by_owner_cols(_mm_tn(x2, da_odd, "odd_dw_in", tn=d))
    dz2, d_fg0, d_fb0 = _mm_nt([(da_odd, 0, 0, 2 * d)], [w_odd_in], "odd_dx", res=dz3,
                               ln=(ffn0[3], ffn0[4], ln(ffn_ln_g, 0)))
    (dz1, d_mg0, d_mb0), gi_f0, go_f0 = _ffn_bwd(dz2, ffn0, w_ffn_in[0], w_ffn_out[0], (xh1, rs1, ln(mix_ln_g, 0)), 0)

    sent_early = [gi_odd, go_odd, jnp.stack([gi_f0, gi_f1], axis=1), jnp.stack([go_f0, go_f1], axis=1)]
    early_h, early_tok = _split_start(sent_early, "scatter4", "scatter_early_start")
    dmix = _mm_nt([(dz1, 0, 0, d)], [w_even_out], "even_dmix", after=early_tok)
    mix =jnp.concatenate([attn, conv], axis=1)
    go_even = _mm_tn(mix, dz1, "even_dw_out", tn=d).reshape(N_CHIPS, 1, d // N_CHIPS, d)
    dbch, dconv_w8 = _conv_bwd(bch, dmix, conv_w)
    qb, dob = _attn_bwd_prep(attn, dmix, qp, lse.reshape(FOX_HEADS, t).T)
    dqkv, dc_col = _attn_unpack(*_attn_bwd(qb, kp, vp, dob, kt))
    dc3 = dc_col.T.reshape(FOX_HEADS, t // LANES, LANES).transpose(1, 0, 2)
    dfl3, d_bf = _fgate_bwd(dc3, fl3, b_f_col)
    dfl = jnp.concatenate([dfl3.transpose(1, 0, 2).reshape(FOX_HEADS, t).T.astype(BF16),
                           jnp.zeros((t, LANES - FOX_HEADS), BF16)], axis=1)
    grad_x = _mm_nt([(dqkv, 0, 0, QKV), (dbch, 0, QKV, QKV + BCH), (dfl, 0, QKV + BCH, EVEN_IN_PAD)], [w_even_in],
                    "even_dx", res=dz1)
    dw_qkv = _mm_tn(x2d, dqkv, "even_dw_qkv", tn=QKV // 2, out_dtype=F32)
    dw_bch = _mm_tn(x2d, dbch, "even_dw_bch", tn=BCH // 2, out_dtype=F32)
    dw_f = _mm_tn(x2d, dfl, "even_dw_f", tn=LANES, out_dtype=F32)
    gi_even = _by_owner_cols(jnp.concatenate([dw_qkv, dw_f[:, :FOX_HEADS], dw_bch], axis=1).astype(BF16))

    sent_late = [gi_even, go_even]
    late_h, late_tok = _split_start(sent_late, "scatter4", "scatter_late_start")
    landed = _split_wait(late_h, "scatter_late_wait", late_tok) + _split_wait(early_h, "scatter_early_wait", late_tok)
    chip_blk = lambda g: lax.dynamic_index_in_dim(g, chip, 0, keepdims=False)
    mine = [_with_own(r, chip_blk(g)) for r, g in zip(landed, sent_late + sent_early)]
    theirs = _exchange(mine, "swap2", "swap_grads")
    big_w = [(even_w_in, m_even_w_in, v_even_w_in), (even_w_out, m_even_w_out, v_even_w_out),
             (odd_w_in, m_odd_w_in, v_odd_w_in), (odd_w_out, m_odd_w_out, v_odd_w_out),
             (ffn_w_in, m_ffn_w_in, v_ffn_w_in), (ffn_w_out, m_ffn_w_out, v_ffn_w_out)]
    big_names = ["even_w_in", "even_w_out", "odd_w_in", "odd_w_out", "ffn_w_in", "ffn_w_out"]
    res = {}
    for nm, own, sib, (w, m, v) in zip(big_names, mine, theirs, big_w):
        res[nm] = _adamw([own, sib], w, m, v, f"adamw_{nm}")

    dws_masked = jnp.where(gmask[None], dws, 0.0)
    rep_names = ["odd_w_s", "odd_b_s", "mix_ln_g", "mix_ln_b", "ffn_ln_g", "ffn_ln_b", "even_b_f"]
    rep_grads = [dws_masked, dbs_col.T, jnp.concatenate([d_mg0, d_mg1]), jnp.concatenate([d_mb0, d_mb1]),
                 jnp.concatenate([d_fg0, d_fg1]), jnp.concatenate([d_fb0, d_fb1]), d_bf.reshape(1, FOX_HEADS)]
    rep_w = [(odd_w_s, m_odd_w_s, v_odd_w_s), (odd_b_s, m_odd_b_s, v_odd_b_s), (mix_ln_g, m_mix_ln_g, v_mix_ln_g),
             (mix_ln_b, m_mix_ln_b, v_mix_ln_b), (ffn_ln_g, m_ffn_ln_g, v_ffn_ln_g), (ffn_ln_b, m_ffn_ln_b, v_ffn_ln_b),
             (even_b_f, m_even_b_f, v_even_b_f)]
    rep_rows = [_to_rows(gr) for gr in rep_grads]
    n_rep = sum(r.shape[0] for r in rep_rows)
    pad_rep = (-n_rep) % SUBLANES
    dconv_w = dconv_w8[:CONV_K].reshape(CONV_K, N_CHIPS, LANES).transpose(1, 0, 2).reshape(N_CHIPS * CONV_K, LANES)
    packed = jnp.concatenate(rep_rows + [jnp.zeros((pad_rep, LANES), F32), d_vg.reshape(SUBLANES, LANES),
                                         d_vb.reshape(SUBLANES, LANES), dconv_w, jnp.zeros((4, LANES), F32)], axis=0)
    (gathered,) = _exchange([packed], "gather8", "gather_small_grads")
    base = n_rep + pad_rep
    own_rows = jnp.concatenate([
        lax.dynamic_slice_in_dim(gathered, base + 2 * chip, 2, axis=1),
        lax.dynamic_slice_in_dim(gathered, base + SUBLANES + 2 * chip, 2, axis=1),
        lax.dynamic_slice_in_dim(gathered, base + 2 * SUBLANES + CONV_K * chip, CONV_K, axis=1),
        jnp.zeros((N_DEV, 1, LANES), F32)], axis=1)
    small_parts = jnp.concatenate([gathered[:, :base], own_rows], axis=1)[:, None]

    def pack_small(get):
        rows = [_to_rows(get(tw)) for tw in rep_w] + [jnp.zeros((pad_rep, LANES), F32)]
        rows += [get(sh).reshape(-1, LANES) for sh in ((odd_v_ln_g, m_odd_v_ln_g, v_odd_v_ln_g),
                                                       (odd_v_ln_b, m_odd_v_ln_b, v_odd_v_ln_b),
                                                       (even_conv_w, m_even_conv_w, v_even_conv_w))]
        return jnp.concatenate(rows + [jnp.zeros((1, LANES), F32)], axis=0)[None]

    small_out = _adamw([small_parts], pack_small(lambda tw: tw[0]), pack_small(lambda tw: tw[1]),
                       pack_small(lambda tw: tw[2]), "adamw_small")

    def unpack_small(rows3):
        rows = rows3[0]
        out, off = {}, 0
        for nm, (w, _, _), r in zip(rep_names, rep_w, rep_rows):
            out[nm] = rows[off:off + r.shape[0]].reshape(-1)[:w.size].reshape(w.shape)
            off += r.shape[0]
        off += pad_rep
        out["odd_v_ln_g"] = rows[off:off + 2].reshape(odd_v_ln_g.shape)
        out["odd_v_ln_b"] = rows[off + 2:off + 4].reshape(odd_v_ln_b.shape)
        out["even_conv_w"] = rows[off + 4:off + 4 + CONV_K].reshape(even_conv_w.shape)
        return out

    small = [unpack_small(o) for o in small_out]
    order = ["even_w_in", "even_b_f", "even_conv_w", "even_w_out", "odd_w_in", "odd_v_ln_g", "odd_v_ln_b", "odd_w_s",
             "odd_b_s", "odd_w_out", "mix_ln_g", "mix_ln_b", "ffn_w_in", "ffn_w_out", "ffn_ln_g", "ffn_ln_b"]
    outs = [loss, grad_x[None]]
    for kind in range(4):
        for nm in order:
            outs.append(res[nm][kind] if nm in res else small[kind][nm])
    return tuple(outs)
```

```python
import functools
import math

import jax
import jax.numpy as jnp
from jax import lax
from jax.experimental import pallas as pl
from jax.experimental.pallas import tpu as pltpu

F32 = jnp.float32
BF16 = jnp.bfloat16

D_MODEL = 1024
FOX_HEADS = 8
HEAD_DIM = 64
HEAD_PAIRS = FOX_HEADS // 2
FOX_WIDTH = FOX_HEADS * HEAD_DIM
CONV_WIDTH = 512
CONV_K = 3
QKV = 3 * FOX_WIDTH
BCH = 3 * CONV_WIDTH
EVEN_IN = QKV + FOX_HEADS + BCH
EVEN_IN_PAD = QKV + BCH + 128
GMLP_BLOCK = 128
GMLP_GROUPS = 8
CHUNK = 64
FFN_HIDDEN = 2816
HALF_HIDDEN = FFN_HIDDEN // 2
ALPHA = 4.0 ** 0.25
LN_EPS = 1e-5
ADAM_LR = 0.001
ADAM_B1 = 0.9
ADAM_B2 = 0.999
ADAM_EPS = 1e-08
ADAM_WD = 0.01
ADAM_STEP = 10
N_CHIPS = 4
N_DEV = 8
LANES = 128
SUBLANES = 8
ROW_TILE = 512
FFN_ROW_TILE = 256
REDUCE_TILE = 2048
ATT_BLOCK = 512
ATT_FWD_HEADS = 4
ATT_BWD_HEADS = 2
VMEM_LIMIT = 56 * 2 ** 20
NEG = -1e30
MESH = pl.DeviceIdType.MESH
HIGHEST = lax.Precision.HIGHEST
Q_C, Q_ONE, Q_LSE = 64, 67, 70
K_ONE, K_C, K_ONE2 = 64, 67, 70
V_ONE = 64
DO_DELTA = 65
NT = (((1,), (1,)), ((), ()))
TN = (((0,), (0,)), ((), ()))


def _cp():
    return pltpu.CompilerParams(vmem_limit_bytes=VMEM_LIMIT)


def _resident(shape):
    zeros = (0,) * len(shape)
    return pl.BlockSpec(shape, lambda *_: zeros, pipeline_mode=pl.Buffered(1))


def _sds(shape, dtype):
    return jax.ShapeDtypeStruct(tuple(shape), dtype)


_MASKS = {
    "gather4": [(1, 0, 0), (0, 1, 0), (1, 1, 0)],
    "scatter4": [(1, 0, 0), (0, 1, 0), (1, 1, 0)],
    "swap2": [(0, 0, 1)],
    "gather8": [(0, 0, 1), (0, 1, 0), (0, 1, 1), (1, 0, 0), (1, 0, 1), (1, 1, 0), (1, 1, 1)],
}


def _exchange(arrs, mode, name):
    n = len(arrs)
    masks = _MASKS[mode]
    npeer = len(masks)
    lead = {"gather4": N_CHIPS, "gather8": N_DEV}.get(mode)
    out_shapes = [_sds(((lead,) if lead else ()) + a.shape, a.dtype) for a in arrs]

    def body(*refs):
        ins, outs = refs[:n], refs[n:2 * n]
        send_sems, recv_sems, loc_sems = refs[2 * n:]
        x, y, c = lax.axis_index("x"), lax.axis_index("y"), lax.axis_index("c")
        chip, dev = 2 * x + y, 4 * x + 2 * y + c
        sends, recvs, locs = [], [], []
        for k in range(n):
            if mode == "gather4":
                locs.append(pltpu.make_async_copy(ins[k], outs[k].at[chip], loc_sems.at[k]))
            elif mode == "scatter4":
                locs.append(pltpu.make_async_copy(ins[k].at[chip], outs[k].at[chip], loc_sems.at[k]))
            elif mode == "gather8":
                locs.append(pltpu.make_async_copy(ins[k], outs[k].at[dev], loc_sems.at[k]))
        for cp in locs:
            cp.start()
        for k in range(n):
            for j, (dx, dy, dc) in enumerate(masks):
                px = 1 - x if dx else x
                py = 1 - y if dy else y
                pc = 1 - c if dc else c
                pchip, pdev = 2 * px + py, 4 * px + 2 * py + pc
                if mode == "gather4":
                    src, dst, land = ins[k], outs[k].at[chip], outs[k].at[pchip]
                elif mode == "scatter4":
                    src, dst, land = ins[k].at[pchip], outs[k].at[chip], outs[k].at[pchip]
                elif mode == "swap2":
                    src, dst, land = ins[k], outs[k], outs[k]
                else:
                    src, dst, land = ins[k], outs[k].at[dev], outs[k].at[pdev]
                s = k * npeer + j
                kw = dict(send_sem=send_sems.at[s], recv_sem=recv_sems.at[s], device_id=(px, py, pc),
                          device_id_type=MESH)
                cp = pltpu.make_async_remote_copy(src_ref=src, dst_ref=dst, **kw)
                cp.start()
                sends.append(cp)
                recvs.append(pltpu.make_async_remote_copy(src_ref=src, dst_ref=land, **kw))
        for cp in recvs:
            cp.wait_recv()
        for cp in sends:
            cp.wait_send()
        for cp in locs:
            cp.wait()

    any_spec = pl.BlockSpec(memory_space=pl.ANY)
    outs = pl.pallas_call(
        body,
        out_shape=out_shapes,
        in_specs=[any_spec] * n,
        out_specs=[any_spec] * n,
        scratch_shapes=[pltpu.SemaphoreType.DMA((n * npeer,)), pltpu.SemaphoreType.DMA((n * npeer,)),
                        pltpu.SemaphoreType.DMA((max(n, 1),))],
        name=name,
    )(*arrs)
    return list(outs)


_HBM_SPEC = pl.BlockSpec(memory_space=pltpu.HBM)
_SEM_SPEC = pl.BlockSpec(memory_space=pltpu.SEMAPHORE)
_ANY_SPEC = pl.BlockSpec(memory_space=pl.ANY)
_EFFECT = pltpu.SideEffectType.DATAFLOW_SIDE_EFFECTING


def _split_copies(mode, ins, lands, send_sems, recv_sems):
    x, y, c = lax.axis_index("x"), lax.axis_index("y"), lax.axis_index("c")
    chip, dev = 2 * x + y, 4 * x + 2 * y + c
    masks = _MASKS[mode]
    out = []
    for k in range(len(ins)):
        for j, (dx, dy, dc) in enumerate(masks):
            px = 1 - x if dx else x
            py = 1 - y if dy else y
            pc = 1 - c if dc else c
            pchip, pdev = 2 * px + py, 4 * px + 2 * py + pc
            if mode == "gather4":
                src, dst, land = ins[k], lands[k].at[chip], lands[k].at[pchip]
            elif mode == "scatter4":
                src, dst, land = ins[k].at[pchip], lands[k].at[chip], lands[k].at[pchip]
            elif mode == "swap2":
                src, dst, land = ins[k], lands[k], lands[k]
            else:
                src, dst, land = ins[k], lands[k].at[dev], lands[k].at[pdev]
            s = k * len(masks) + j
            kw = dict(send_sem=send_sems.at[s], recv_sem=recv_sems.at[s], device_id=(px, py, pc), device_id_type=MESH)
            out.append((pltpu.make_async_remote_copy(src_ref=src, dst_ref=dst, **kw),
                        pltpu.make_async_remote_copy(src_ref=src, dst_ref=land, **kw)))
    return out


def _split_start(arrs, mode, name, after=None):
    n = len(arrs)
    nsem = n * len(_MASKS[mode])
    lead = {"gather4": (N_CHIPS,), "gather8": (N_DEV,)}.get(mode, ())
    land_shapes = [lead + a.shape for a in arrs]

    def body(*refs):
        ins, lands = refs[:n], refs[n:2 * n]
        outs = refs[2 * n + (after is not None):]
        for start, _ in _split_copies(mode, ins, lands, outs[0], outs[1]):
            start.start()
        outs[-1][...] = jnp.zeros(outs[-1].shape, F32)

    srcs = [pltpu.with_memory_space_constraint(a, pltpu.HBM) for a in arrs]
    empties = [pltpu.with_memory_space_constraint(lax.empty(s, a.dtype), pltpu.HBM) for s, a in zip(land_shapes, arrs)]
    res = pl.pallas_call(
        body, name=name,
        out_shape=(pltpu.SemaphoreType.DMA((nsem,)), pltpu.SemaphoreType.DMA((nsem,)),
                   *[pltpu.HBM(a.shape, a.dtype) for a in arrs],
                   *[pltpu.HBM(s, a.dtype) for s, a in zip(land_shapes, arrs)],
                   _sds((SUBLANES, LANES), F32)),
        in_specs=[_HBM_SPEC] * (2 * n) + ([_ANY_SPEC] if after is not None else []),
        out_specs=(_SEM_SPEC, _SEM_SPEC, *[_HBM_SPEC] * (2 * n), pl.BlockSpec(memory_space=pltpu.VMEM)),
        input_output_aliases={k: 2 + k for k in range(2 * n)},
        compiler_params=pltpu.CompilerParams(has_side_effects=_EFFECT),
    )(*srcs, *empties, *([after] if after is not None else []))
    return dict(mode=mode, n=n, sems=res[:2], bufs=res[2:2 + 2 * n]), res[-1]


def _split_wait(handle, name, after):
    n, mode = handle["n"], handle["mode"]

    def body(*refs):
        ins, lands = refs[:n], refs[n:2 * n]
        send_sems, recv_sems = refs[2 * n], refs[2 * n + 1]
        for _, arrival in _split_copies(mode, ins, lands, send_sems, recv_sems):
            arrival.wait_send()
            arrival.wait_recv()

    bufs = handle["bufs"]
    res = pl.pallas_call(
        body, name=name,
        out_shape=tuple(pltpu.HBM(b.shape, b.dtype) for b in bufs),
        in_specs=[_HBM_SPEC] * (2 * n) + [_SEM_SPEC, _SEM_SPEC, _ANY_SPEC],
        out_specs=tuple([_HBM_SPEC] * (2 * n)),
        input_output_aliases={k: k for k in range(2 * n)},
        compiler_params=pltpu.CompilerParams(has_side_effects=_EFFECT),
    )(*bufs, *handle["sems"], after)
    return list(res[n:])


def _with_own(landed, own):
    chip = 2 * lax.axis_index("x") + lax.axis_index("y")
    return lax.dynamic_update_index_in_dim(landed, own, chip, 0)


def _sigmoid(x):
    return 1.0 / (1.0 + jnp.exp(-x))


def _log_sigmoid(x):
    e = jnp.exp(-jnp.abs(x))
    log1p = jnp.where(e < 1e-2, e * (1.0 - e * (0.5 - e * (1.0 / 3.0))), jnp.log(1.0 + e))
    return jnp.minimum(x, 0.0) - log1p


def _gelu(a):
    return 0.5 * a * (1.0 + lax.erf(a * (2.0 ** -0.5)))


def _gelu_grad(a):
    cdf = 0.5 * (1.0 + lax.erf(a * (2.0 ** -0.5)))
    pdf = jnp.exp(-0.5 * a * a) * (1.0 / math.sqrt(2.0 * math.pi))
    return cdf + a * pdf


def _ln_fwd(z):
    mu = jnp.mean(z, axis=-1, keepdims=True)
    zc = z - mu
    var = jnp.mean(zc * zc, axis=-1, keepdims=True)
    rstd = lax.rsqrt(var + LN_EPS)
    return zc * rstd, rstd


def _ln_bwd(dy, xhat, rstd, g):
    dxh = dy * g
    m1 = jnp.mean(dxh, axis=-1, keepdims=True)
    m2 = jnp.mean(dxh * xhat, axis=-1, keepdims=True)
    dz = rstd * (dxh - m1 - xhat * m2)
    return dz, jnp.sum(dy * xhat, axis=0, keepdims=True), jnp.sum(dy, axis=0, keepdims=True)


def _shift_down(z, halo):
    r = lax.broadcasted_iota(jnp.int32, z.shape, 0)
    z1 = jnp.where(r == 0, halo[7:8, :], pltpu.roll(z, 1, 0))
    z2 = jnp.where(r == 0, halo[6:7, :], jnp.where(r == 1, halo[7:8, :], pltpu.roll(z, 2, 0)))
    return z1, z2


def _shift_up(z, halo):
    n = z.shape[0]
    r = lax.broadcasted_iota(jnp.int32, z.shape, 0)
    z1 = jnp.where(r == n - 1, halo[0:1, :], pltpu.roll(z, n - 1, 0))
    z2 = jnp.where(r == n - 1, halo[1:2, :], jnp.where(r == n - 2, halo[0:1, :], pltpu.roll(z, n - 2, 0)))
    return z1, z2


def _accumulate(ref, first, value):
    @pl.when(first)
    def _():
        ref[...] = value

    @pl.when(jnp.logical_not(first))
    def _():
        ref[...] += value


def _proj(x, w, splits, name):
    t, k = x.shape
    tm = min(ROW_TILE, t)

    def body(x_ref, w_ref, *outs):
        a = x_ref[...].astype(BF16)
        for (lo, hi, dt), o in zip(splits, outs):
            o[...] = jnp.dot(a, w_ref[:, lo:hi], preferred_element_type=F32).astype(dt)

    return pl.pallas_call(
        body, grid=(t // tm,),
        in_specs=[pl.BlockSpec((tm, k), lambda i: (i, 0)), _resident(w.shape)],
        out_specs=[pl.BlockSpec((tm, hi - lo), lambda i: (i, 0)) for lo, hi, _ in splits],
        out_shape=[_sds((t, hi - lo), dt) for lo, hi, dt in splits],
        compiler_params=_cp(), name=name)(x, w)


def _fgate_fwd(fl3, b_f):
    nc = fl3.shape[0]

    def body(f_ref, b_ref, c_ref):
        r = lax.broadcasted_iota(jnp.int32, (LANES, LANES), 0)
        cidx = lax.broadcasted_iota(jnp.int32, (LANES, LANES), 1)
        upper = (r <= cidx).astype(F32)

        def step(i, carry):
            lf = _log_sigmoid(f_ref[i] + b_ref[...])
            cc = jnp.dot(lf, upper, precision=HIGHEST, preferred_element_type=F32) + carry
            c_ref[i] = cc
            return cc[:, LANES - 1:LANES]

        lax.fori_loop(0, nc, step, jnp.zeros((FOX_HEADS, 1), F32))

    return pl.pallas_call(body, out_shape=_sds(fl3.shape, F32), name="fgate_fwd")(fl3, b_f)


def _fgate_bwd(dc3, fl3, b_f):
    nc = fl3.shape[0]

    def body(dc_ref, f_ref, b_ref, df_ref, db_ref):
        r = lax.broadcasted_iota(jnp.int32, (LANES, LANES), 0)
        cidx = lax.broadcasted_iota(jnp.int32, (LANES, LANES), 1)
        lower = (r >= cidx).astype(F32)

        def step(n, carry):
            suffix, db = carry
            i = nc - 1 - n
            dlf = jnp.dot(dc_ref[i], lower, precision=HIGHEST, preferred_element_type=F32) + suffix
            df = dlf * (1.0 - _sigmoid(f_ref[i] + b_ref[...]))
            df_ref[i] = df
            return dlf[:, 0:1], db + jnp.sum(df, axis=1, keepdims=True)

        zero = jnp.zeros((FOX_HEADS, 1), F32)
        _, db = lax.fori_loop(0, nc, step, (zero, zero))
        db_ref[...] = db

    return pl.pallas_call(body, out_shape=[_sds(fl3.shape, F32), _sds((FOX_HEADS, 1), F32)],
                          name="fgate_bwd")(dc3, fl3, b_f)


def _split3(c):
    hi = c.astype(BF16).astype(F32)
    mid = (c - hi).astype(BF16).astype(F32)
    lo = (c - hi - mid).astype(BF16).astype(F32)
    return hi, mid, lo


def _lane_pieces(lane, start, pieces, sign):
    out = jnp.zeros(lane.shape, F32)
    for n, p in enumerate(pieces):
        out = jnp.where(lane == start + n, sign * p, out)
    return out


def _attn_pack(qkv, c_col):
    t = qkv.shape[0]
    tm = min(ROW_TILE, t)
    hd = HEAD_DIM

    def body(x_ref, c_ref, qp_ref, kp_ref, vp_ref, kt_ref, vt_ref):
        lane = lax.broadcasted_iota(jnp.int32, (tm, hd), 1) + hd
        for h in range(FOX_HEADS):
            pieces = _split3(c_ref[:, h:h + 1])
            ones = lambda a, b: jnp.where(jnp.logical_and(lane >= a, lane < b), 1.0, 0.0)
            q_extra = _lane_pieces(lane, Q_C, pieces, 1.0) + ones(Q_ONE, Q_ONE + 3)
            k_extra = _lane_pieces(lane, K_C, pieces, -1.0) + ones(K_ONE, K_ONE + 3) + ones(K_ONE2, K_ONE2 + 3)
            qp_ref[h, :, :hd] = (x_ref[:, h * hd:(h + 1) * hd].astype(F32) * (hd ** -0.5)).astype(BF16)
            qp_ref[h, :, hd:] = q_extra.astype(BF16)
            kp_ref[h, :, :hd] = x_ref[:, FOX_WIDTH + h * hd:FOX_WIDTH + (h + 1) * hd]
            kp_ref[h, :, hd:] = k_extra.astype(BF16)
            vp_ref[h, :, :hd] = x_ref[:, 2 * FOX_WIDTH + h * hd:2 * FOX_WIDTH + (h + 1) * hd]
            vp_ref[h, :, hd:] = ones(V_ONE, V_ONE + 4).astype(BF16)
            kt_ref[h] = kp_ref[h].astype(F32).T.astype(BF16)
            vt_ref[h] = vp_ref[h].astype(F32).T.astype(BF16)

    row3 = pl.BlockSpec((FOX_HEADS, tm, LANES), lambda i: (0, i, 0))
    col3 = pl.BlockSpec((FOX_HEADS, LANES, tm), lambda i: (0, 0, i))
    return pl.pallas_call(
        body, grid=(t // tm,),
        in_specs=[pl.BlockSpec((tm, QKV), lambda i: (i, 0)), pl.BlockSpec((tm, FOX_HEADS), lambda i: (i, 0))],
        out_specs=[row3, row3, row3, col3, col3],
        out_shape=[_sds((FOX_HEADS, t, LANES), BF16)] * 3 + [_sds((FOX_HEADS, LANES, t), BF16)] * 2,
        compiler_params=_cp(), name="attn_pack")(qkv, c_col)


def _triangle(nq, key_major):
    if key_major:
        pairs = [(i, j) for j in range(nq) for i in range(j, nq)]
    else:
        pairs = [(i, j) for i in range(nq) for j in range(i + 1)]
    return jnp.asarray([p[0] for p in pairs], jnp.int32), jnp.asarray([p[1] for p in pairs], jnp.int32)


def _attn_fwd(qp, kp, vt):
    t = qp.shape[1]
    bq = min(ATT_BLOCK, t)
    nq = t // bq
    nh = ATT_FWD_HEADS
    i_tab, j_tab = _triangle(nq, key_major=False)

    def body(it_ref, jt_ref, q_ref, k_ref, vt_ref, o_ref, lse_ref, m_sc, acc_sc):
        s = pl.program_id(1)
        i, j = it_ref[s], jt_ref[s]

        @pl.when(j == 0)
        def _():
            m_sc[...] = jnp.full(m_sc.shape, NEG, F32)
            acc_sc[...] = jnp.zeros(acc_sc.shape, F32)

        def sweep(masked):
            scores = lambda h: lax.dot_general(k_ref[h], q_ref[h], NT, preferred_element_type=F32)

            def accumulate(h, pt, rescale):
                acc_sc[h] = rescale * acc_sc[h] + jnp.dot(vt_ref[h], pt, preferred_element_type=F32)

            ahead, behind = scores(0), None
            for h in range(nh):
                st = ahead
                if h + 1 < nh:
                    ahead = scores(h + 1)
                if behind is not None:
                    accumulate(*behind)
                if masked:
                    key = lax.broadcasted_iota(jnp.int32, (bq, bq), 0)
                    qry = lax.broadcasted_iota(jnp.int32, (bq, bq), 1)
                    st = jnp.where(key <= qry, st, NEG)
                m_prev = m_sc[h]
                m_new = jnp.maximum(m_prev, jnp.max(st, axis=0, keepdims=True))
                behind = (h, jnp.exp(st - m_new).astype(BF16), jnp.exp(m_prev - m_new))
                m_sc[h] = m_new
            accumulate(*behind)

        @pl.when(j < i)
        def _():
            sweep(False)

        @pl.when(j == i)
        def _():
            sweep(True)
            for h in range(nh):
                acc = acc_sc[h]
                denom = acc[V_ONE:V_ONE + 1, :]
                o_ref[:, h * HEAD_DIM:(h + 1) * HEAD_DIM] = (acc[:HEAD_DIM, :] / denom).T.astype(BF16)
                lse_ref[h] = m_sc[h] + jnp.log(denom)

    grid_spec = pltpu.PrefetchScalarGridSpec(
        num_scalar_prefetch=2, grid=(FOX_HEADS // nh, i_tab.shape[0]),
        in_specs=[pl.BlockSpec((nh, bq, LANES), lambda hp, s, it, jt: (hp, it[s], 0)),
                  pl.BlockSpec((nh, bq, LANES), lambda hp, s, it, jt: (hp, jt[s], 0)),
                  pl.BlockSpec((nh, LANES, bq), lambda hp, s, it, jt: (hp, 0, jt[s]))],
        out_specs=[pl.BlockSpec((bq, nh * HEAD_DIM), lambda hp, s, it, jt: (it[s], hp)),
                   pl.BlockSpec((nh, 1, bq), lambda hp, s, it, jt: (hp, 0, it[s]))],
        scratch_shapes=[pltpu.VMEM((nh, 1, bq), F32), pltpu.VMEM((nh, LANES, bq), F32)])
    return pl.pallas_call(body, grid_spec=grid_spec,
                          out_shape=[_sds((t, FOX_WIDTH), BF16), _sds((FOX_HEADS, 1, t), F32)],
                          compiler_params=_cp(), name="attn_fwd")(i_tab, j_tab, qp, kp, vt)


def _conv_fwd(bch, conv_w):
    t = bch.shape[0]
    tm = min(ROW_TILE, t)
    halo_blocks = tm // SUBLANES
    cw = CONV_WIDTH

    def body(cur_ref, prev_ref, w_ref, o_ref):
        i = pl.program_id(0)
        z = cur_ref[:, cw:2 * cw] * cur_ref[:, 2 * cw:]
        zp = jnp.where(i == 0, 0.0, prev_ref[:, cw:2 * cw] * prev_ref[:, 2 * cw:])
        z1, z2 = _shift_down(z, zp)
        y = w_ref[0:1, :] * z2 + w_ref[1:2, :] * z1 + w_ref[2:3, :] * z
        o_ref[...] = (cur_ref[:, :cw] * y).astype(BF16)

    return pl.pallas_call(
        body, grid=(t // tm,),
        in_specs=[pl.BlockSpec((tm, BCH), lambda i: (i, 0)),
                  pl.BlockSpec((SUBLANES, BCH), lambda i: (jnp.maximum(i * halo_blocks - 1, 0), 0)),
                  _resident(conv_w.shape)],
        out_specs=pl.BlockSpec((tm, cw), lambda i: (i, 0)),
        out_shape=_sds((t, cw), BF16), compiler_params=_cp(), name="conv_fwd")(bch, bch, conv_w)


def _mm_res_ln(pairs, res, g, b, name):
    t, d = res.shape
    tm = min(ROW_TILE, t)
    n = len(pairs)

    def body(*refs):
        a_refs, w_refs = refs[:n], refs[n:2 * n]
        res_ref, g_ref, b_ref, y_ref, yb_ref, xh_ref, rs_ref = refs[2 * n:]
        z = ALPHA * res_ref[...]
        for a_ref, w_ref in zip(a_refs, w_refs):
            z = z + jnp.dot(a_ref[...].astype(BF16), w_ref[...], preferred_element_type=F32)
        xhat, rstd = _ln_fwd(z)
        y = xhat * g_ref[...] + b_ref[...]
        y_ref[...] = y
        yb_ref[...] = y.astype(BF16)
        xh_ref[...] = xhat
        rs_ref[...] = rstd

    row = lambda i: (i, 0)
    full = pl.BlockSpec((tm, d), row)
    return pl.pallas_call(
        body, grid=(t // tm,),
        in_specs=[pl.BlockSpec((tm, a.shape[1]), row) for a, _ in pairs] + [_resident(w.shape) for _, w in pairs]
        + [full, _resident(g.shape), _resident(b.shape)],
        out_specs=[full, full, full, pl.BlockSpec((tm, 1), row)],
        out_shape=[_sds((t, d), F32), _sds((t, d), BF16), _sds((t, d), F32), _sds((t, 1), F32)],
        compiler_params=_cp(), name=name)(*[a for a, _ in pairs], *[w for _, w in pairs], res, g, b)


def _ffn_in(x, wi, name):
    t, d = x.shape
    tm = min(FFN_ROW_TILE, t)
    hh = HALF_HIDDEN

    def body(x_ref, w_ref, gu_ref, h_ref):
        a = x_ref[...].astype(BF16)
        for c in range(2):
            gs, us = slice(c * hh, (c + 1) * hh), slice(FFN_HIDDEN + c * hh, FFN_HIDDEN + (c + 1) * hh)
            g = jnp.dot(a, w_ref[c], preferred_element_type=F32)
            u = jnp.dot(a, w_ref[2 + c], preferred_element_type=F32)
            gu_ref[:, gs] = g.astype(BF16)
            gu_ref[:, us] = u.astype(BF16)
            h_ref[:, gs] = (g * _sigmoid(g) * u).astype(BF16)

    row = lambda i: (i, 0)
    return pl.pallas_call(
        body, grid=(t // tm,),
        in_specs=[pl.BlockSpec((tm, d), row), _resident(wi.shape)],
        out_specs=[pl.BlockSpec((tm, 2 * FFN_HIDDEN), row), pl.BlockSpec((tm, FFN_HIDDEN), row)],
        out_shape=[_sds((t, 2 * FFN_HIDDEN), BF16), _sds((t, FFN_HIDDEN), BF16)],
        compiler_params=_cp(), name=name)(x, wi)


def _gmlp_fwd(x, w_in, vg, vb, wm, bs_col):
    t, d = x.shape
    tm = min(ROW_TILE, t)
    gb = GMLP_BLOCK

    def body(x_ref, w_ref, vg_ref, vb_ref, wm_ref, bs_ref, a_ref, o_ref):
        xb = x_ref[...].astype(BF16)
        nc = w_ref.shape[2]
        for j in range(w_ref.shape[0]):
            a_ref[:, j * nc:(j + 1) * nc] = jnp.dot(xb, w_ref[j], preferred_element_type=F32)
        u = _gelu(a_ref[:, :d])
        vhat, _ = _ln_fwd(_gelu(a_ref[:, d:]))
        vln = (vhat * vg_ref[...] + vb_ref[...]).astype(BF16)
        for blk in range(tm // gb):
            rs = slice(blk * gb, (blk + 1) * gb)
            for gi in range(GMLP_GROUPS):
                cs = slice(gi * gb, (gi + 1) * gb)
                s = jnp.dot(wm_ref[gi], vln[rs, cs], preferred_element_type=F32) + bs_ref[:, gi:gi + 1]
                o_ref[rs, cs] = (u[rs, cs] * s).astype(BF16)

    row = lambda i: (i, 0)
    return pl.pallas_call(
        body, grid=(t // tm,),
        in_specs=[pl.BlockSpec((tm, d), row), _resident(w_in.shape), _resident(vg.shape), _resident(vb.shape),
                  _resident(wm.shape), _resident(bs_col.shape)],
        out_specs=[pl.BlockSpec((tm, 2 * d), row), pl.BlockSpec((tm, d), row)],
        out_shape=[_sds((t, 2 * d), F32), _sds((t, d), BF16)],
        compiler_params=_cp(), name="gmlp_fwd")(x, w_in, vg, vb, wm, bs_col)


def _loss_ln_bwd(xhat, rstd, g, b, target):
    t, d = xhat.shape
    tm = min(ROW_TILE, t)

    def body(xh_ref, rs_ref, g_ref, b_ref, t_ref, sq_ref, dz_ref, dg_ref, db_ref):
        first = pl.program_id(0) == 0
        xh = xh_ref[...]
        err = xh * g_ref[...] + b_ref[...] - t_ref[...]
        dz, dg, db = _ln_bwd(err * (1.0 / d), xh, rs_ref[...], g_ref[...])
        dz_ref[...] = dz
        _accumulate(sq_ref, first, jnp.sum(err * err, axis=0, keepdims=True))
        _accumulate(dg_ref, first, dg)
        _accumulate(db_ref, first, db)

    row = lambda i: (i, 0)
    vec = pl.BlockSpec((1, d), lambda i: (0, 0))
    return pl.pallas_call(
        body, grid=(t // tm,),
        in_specs=[pl.BlockSpec((tm, d), row), pl.BlockSpec((tm, 1), row), _resident(g.shape), _resident(b.shape),
                  pl.BlockSpec((tm, d), row)],
        out_specs=[vec, pl.BlockSpec((tm, d), row), vec, vec],
        out_shape=[_sds((1, d), F32), _sds((t, d), F32), _sds((1, d), F32), _sds((1, d), F32)],
        compiler_params=_cp(), name="loss_ln_bwd")(xhat, rstd, g, b, target)


def _mm_nt(pairs, ws, name, *, tm=ROW_TILE, res=None, ln=None, out_dtype=F32, after=None):
    t = pairs[0][0].shape[0]
    k = ws[0].shape[-2]
    tm = min(tm, t)
    n, nw = len(pairs), len(ws)

    def body(*refs):
        refs = refs[after is not None:]
        a_refs, w_refs = refs[:n], refs[n:n + nw]
        rest = list(refs[n + nw:])
        dx = None
        for a_ref, (_, wi, lo, hi) in zip(a_refs, pairs):
            w_ref = w_refs[wi]
            if len(w_ref.shape) == 3:
                nc = w_ref.shape[2]
                parts = [lax.dot_general(a_ref[:, j * nc:(j + 1) * nc].astype(BF16), w_ref[j], NT,
                                         preferred_element_type=F32) for j in range(w_ref.shape[0])]
            else:
                parts = [lax.dot_general(a_ref[...].astype(BF16), w_ref[:, lo:hi], NT, preferred_element_type=F32)]
            for part in parts:
                dx = part if dx is None else dx + part
        if res is not None:
            dx = dx + ALPHA * rest.pop(0)[...]
        if ln is None:
            rest[0][...] = dx.astype(out_dtype)
            return
        xh_ref, rs_ref, g_ref, dz_ref, dg_ref, db_ref = rest
        first = pl.program_id(0) == 0
        dz, dg, db = _ln_bwd(dx, xh_ref[...], rs_ref[...], g_ref[...])
        dz_ref[...] = dz
        _accumulate(dg_ref, first, dg)
        _accumulate(db_ref, first, db)

    row = lambda i: (i, 0)
    in_specs = [pl.BlockSpec((tm, a.shape[1]), row) for a, _, _, _ in pairs] + [_resident(w.shape) for w in ws]
    args = [a for a, _, _, _ in pairs] + list(ws)
    if res is not None:
        in_specs.append(pl.BlockSpec((tm, k), row))
        args.append(res)
    if ln is None:
        out_specs = pl.BlockSpec((tm, k), row)
        out_shape = _sds((t, k), out_dtype)
    else:
        xhat, rstd, g = ln
        in_specs += [pl.BlockSpec((tm, k), row), pl.BlockSpec((tm, 1), row), _resident(g.shape)]
        args += [xhat, rstd, g]
        vec = pl.BlockSpec((1, k), lambda i: (0, 0))
        out_specs = [pl.BlockSpec((tm, k), row), vec, vec]
        out_shape = [_sds((t, k), F32), _sds((1, k), F32), _sds((1, k), F32)]
    if after is not None:
        in_specs.insert(0, _ANY_SPEC)
        args.insert(0, after)
    return pl.pallas_call(body, grid=(t // tm,), in_specs=in_specs, out_specs=out_specs, out_shape=out_shape,
                          compiler_params=_cp(), name=name)(*args)


def _mm_tn(a, b, name, *, tn, tk=None, tt=None, stack_cols=False, out_dtype=BF16, after=None):
    t, k = a.shape
    n = b.shape[1]
    tk = k if tk is None else tk
    tt = min(REDUCE_TILE if tt is None else tt, t)
    nt = t // tt

    def body(a_ref, b_ref, *rest):
        o_ref, acc_ref = rest[after is not None:]
        s = pl.program_id(2)
        part = lax.dot_general(a_ref[...].astype(BF16), b_ref[...].astype(BF16), TN, preferred_element_type=F32)
        _accumulate(acc_ref, s == 0, part)

        @pl.when(s == nt - 1)
        def _():
            o_ref[...] = acc_ref[...].astype(out_dtype).reshape(o_ref.shape)

    if stack_cols:
        assert tk == k
        out_spec = pl.BlockSpec((1, k, tn), lambda kk, j, s: (j, 0, 0))
        out_shape = _sds((n // tn, k, tn), out_dtype)
    else:
        out_spec = pl.BlockSpec((tk, tn), lambda kk, j, s: (kk, j))
        out_shape = _sds((k, n), out_dtype)
    return pl.pallas_call(
        body, grid=(k // tk, n // tn, nt),
        in_specs=[pl.BlockSpec((tt, tk), lambda kk, j, s: (s, kk)), pl.BlockSpec((tt, tn), lambda kk, j, s: (s, j))]
        + ([_ANY_SPEC] if after is not None else []),
        out_specs=out_spec, out_shape=out_shape,
        scratch_shapes=[pltpu.VMEM((tk, tn), F32)],
        compiler_params=_cp(), name=name)(a, b, *([after] if after is not None else []))


def _ffn_bwd_hidden(dz, wo, gu, name):
    t, d = dz.shape
    tm = min(FFN_ROW_TILE, t)
    hh = HALF_HIDDEN

    def body(dz_ref, w_ref, gu_ref, o_ref):
        a = dz_ref[...].astype(BF16)
        for c in range(2):
            gs, us = slice(c * hh, (c + 1) * hh), slice(FFN_HIDDEN + c * hh, FFN_HIDDEN + (c + 1) * hh)
            dh = lax.dot_general(a, w_ref[gs, :], NT, preferred_element_type=F32)
            g = gu_ref[:, gs].astype(F32)
            u = gu_ref[:, us].astype(F32)
            sig = _sigmoid(g)
            o_ref[:, gs] = (dh * u * sig * (1.0 + g * (1.0 - sig))).astype(BF16)
            o_ref[:, us] = (dh * g * sig).astype(BF16)

    row = lambda i: (i, 0)
    return pl.pallas_call(
        body, grid=(t // tm,),
        in_specs=[pl.BlockSpec((tm, d), row), _resident(wo.shape), pl.BlockSpec((tm, 2 * FFN_HIDDEN), row)],
        out_specs=pl.BlockSpec((tm, 2 * FFN_HIDDEN), row),
        out_shape=_sds((t, 2 * FFN_HIDDEN), BF16), compiler_params=_cp(), name=name)(dz, wo, gu)


def _gmlp_bwd(dgated, a, vg, vb, wm, bs_col):
    t, d2 = a.shape
    d = d2 // 2
    tm = min(ROW_TILE, t)
    gb = GMLP_BLOCK

    def body(dg_ref, a_ref, vg_ref, vb_ref, wm_ref, bs_ref, da_ref, dws_ref, dbs_ref, dvg_ref, dvb_ref, dvln_sc):
        first = pl.program_id(0) == 0
        au, av = a_ref[:, :d], a_ref[:, d:]
        u = _gelu(au)
        vhat, rstd = _ln_fwd(_gelu(av))
        vln = (vhat * vg_ref[...] + vb_ref[...]).astype(BF16)
        dgate = dg_ref[...]

        @pl.when(first)
        def _():
            dws_ref[...] = jnp.zeros(dws_ref.shape, F32)
            dbs_ref[...] = jnp.zeros(dbs_ref.shape, F32)

        for blk in range(tm // gb):
            rs = slice(blk * gb, (blk + 1) * gb)
            for gi in range(GMLP_GROUPS):
                cs = slice(gi * gb, (gi + 1) * gb)
                vblk = vln[rs, cs]
                s = jnp.dot(wm_ref[gi], vblk, preferred_element_type=F32) + bs_ref[:, gi:gi + 1]
                dgb = dgate[rs, cs]
                da_ref[rs, cs] = (dgb * s * _gelu_grad(au[rs, cs])).astype(BF16)
                ds = dgb * u[rs, cs]
                dsb = ds.astype(BF16)
                dws_ref[gi] += lax.dot_general(dsb, vblk, NT, preferred_element_type=F32)
                dbs_ref[:, gi:gi + 1] += jnp.sum(ds, axis=1, keepdims=True)
                dvln_sc[rs, cs] = lax.dot_general(wm_ref[gi], dsb, TN, preferred_element_type=F32)
        dv, dvg, dvb = _ln_bwd(dvln_sc[...], vhat, rstd, vg_ref[...])
        da_ref[:, d:] = (dv * _gelu_grad(av)).astype(BF16)
        _accumulate(dvg_ref, first, dvg)
        _accumulate(dvb_ref, first, dvb)

    row = lambda i: (i, 0)
    vec = pl.BlockSpec((1, d), lambda i: (0, 0))
    return pl.pallas_call(
        body, grid=(t // tm,),
        in_specs=[pl.BlockSpec((tm, d), row), pl.BlockSpec((tm, d2), row), _resident(vg.shape), _resident(vb.shape),
                  _resident(wm.shape), _resident(bs_col.shape)],
        out_specs=[pl.BlockSpec((tm, d2), row), pl.BlockSpec(wm.shape, lambda i: (0, 0, 0)),
                   pl.BlockSpec(bs_col.shape, lambda i: (0, 0)), vec, vec],
        out_shape=[_sds((t, d2), BF16), _sds(wm.shape, F32), _sds(bs_col.shape, F32), _sds((1, d), F32), _sds((1, d), F32)],
        scratch_shapes=[pltpu.VMEM((tm, d), F32)],
        compiler_params=_cp(), name="gmlp_bwd")(dgated, a, vg, vb, wm, bs_col)


def _conv_bwd(bch, dmix, conv_w):
    t = bch.shape[0]
    tm = min(ROW_TILE, t)
    nb = t // tm
    halo_blocks = tm // SUBLANES
    cw = CONV_WIDTH

    def body(cur_ref, prev_ref, next_ref, dc_ref, dn_ref, w_ref, o_ref, dw_ref):
        i = pl.program_id(0)
        bgate, cgate, hval = cur_ref[:, :cw], cur_ref[:, cw:2 * cw], cur_ref[:, 2 * cw:]
        z = cgate * hval
        zp = jnp.where(i == 0, 0.0, prev_ref[:, cw:2 * cw] * prev_ref[:, 2 * cw:])
        z1, z2 = _shift_down(z, zp)
        w0, w1, w2 = w_ref[0:1, :], w_ref[1:2, :], w_ref[2:3, :]
        dconv = dc_ref[...]
        o_ref[:, :cw] = (dconv * (w0 * z2 + w1 * z1 + w2 * z)).astype(BF16)
        dy = dconv * bgate
        dyn = jnp.where(i == nb - 1, 0.0, dn_ref[...] * next_ref[:, :cw])
        dy1, dy2 = _shift_up(dy, dyn)
        dz = w2 * dy + w1 * dy1 + w0 * dy2
        o_ref[:, cw:2 * cw] = (dz * hval).astype(BF16)
        o_ref[:, 2 * cw:] = (dz * cgate).astype(BF16)

        @pl.when(i == 0)
        def _():
            dw_ref[...] = jnp.zeros(dw_ref.shape, F32)

        for tap, zs in enumerate((z2, z1, z)):
            dw_ref[tap:tap + 1, :] += jnp.sum(dy * zs, axis=0, keepdims=True)

    last_halo = t // SUBLANES - 1
    return pl.pallas_call(
        body, grid=(nb,),
        in_specs=[pl.BlockSpec((tm, BCH), lambda i: (i, 0)),
                  pl.BlockSpec((SUBLANES, BCH), lambda i: (jnp.maximum(i * halo_blocks - 1, 0), 0)),
                  pl.BlockSpec((SUBLANES, BCH), lambda i: (jnp.minimum((i + 1) * halo_blocks, last_halo), 0)),
                  pl.BlockSpec((tm, cw), lambda i: (i, 1)),
                  pl.BlockSpec((SUBLANES, cw), lambda i: (jnp.minimum((i + 1) * halo_blocks, last_halo), 1)),
                  _resident(conv_w.shape)],
        out_specs=[pl.BlockSpec((tm, BCH), lambda i: (i, 0)), pl.BlockSpec((SUBLANES, cw), lambda i: (0, 0))],
        out_shape=[_sds((t, BCH), BF16), _sds((SUBLANES, cw), F32)],
        compiler_params=_cp(), name="conv_bwd")(bch, bch, bch, dmix, dmix, conv_w)


def _attn_bwd_prep(o, dmix, qp, lse_col):
    t = o.shape[0]
    tm = min(ROW_TILE, t)
    hd = HEAD_DIM

    def body(o_ref, do_ref, qp_ref, lse_ref, qb_ref, dob_ref):
        lane = lax.broadcasted_iota(jnp.int32, (tm, hd), 1) + hd
        for h in range(FOX_HEADS):
            do = do_ref[:, h * hd:(h + 1) * hd]
            delta = jnp.sum(o_ref[:, h * hd:(h + 1) * hd].astype(F32) * do, axis=-1, keepdims=True)
            dob_ref[h, :, :hd] = do.astype(BF16)
            dob_ref[h, :, hd:] = _lane_pieces(lane, DO_DELTA, _split3(delta), -1.0).astype(BF16)
            qb_ref[h, :, :hd] = qp_ref[h, :, :hd]
            qb_ref[h, :, hd:] = (qp_ref[h, :, hd:].astype(F32)
                                 + _lane_pieces(lane, Q_LSE, _split3(lse_ref[:, h:h + 1]), -1.0)).astype(BF16)

    row3 = pl.BlockSpec((FOX_HEADS, tm, LANES), lambda i: (0, i, 0))
    return pl.pallas_call(
        body, grid=(t // tm,),
        in_specs=[pl.BlockSpec((tm, FOX_WIDTH), lambda i: (i, 0)), pl.BlockSpec((tm, FOX_WIDTH), lambda i: (i, 0)), row3,
                  pl.BlockSpec((tm, FOX_HEADS), lambda i: (i, 0))],
        out_specs=[row3, row3], out_shape=[_sds((FOX_HEADS, t, LANES), BF16)] * 2,
        compiler_params=_cp(), name="attn_bwd_prep")(o, dmix, qp, lse_col)


def _attn_bwd(qb, kp, vp, dob, kt):
    t = qb.shape[1]
    bq = min(ATT_BLOCK, t)
    nq = t // bq
    i_tab, j_tab = _triangle(nq, key_major=True)

    def body(it_ref, jt_ref, q_ref, k_ref, v_ref, do_ref, kt_ref, dqt_ref, dk_ref, dv_ref, dk_sc, dv_sc):
        s = pl.program_id(1)
        i, j = it_ref[s], jt_ref[s]

        @pl.when(s == 0)
        def _():
            dqt_ref[...] = jnp.zeros(dqt_ref.shape, F32)

        @pl.when(i == j)
        def _():
            dk_sc[...] = jnp.zeros(dk_sc.shape, F32)
            dv_sc[...] = jnp.zeros(dv_sc.shape, F32)

        cols = pl.ds(pl.multiple_of(i * bq, bq), bq)

        def sweep(masked):
            def scores(h):
                return (lax.dot_general(k_ref[h], q_ref[h], NT, preferred_element_type=F32),
                        lax.dot_general(v_ref[h], do_ref[h], NT, preferred_element_type=F32))

            def accumulate(h, ptb, dstb):
                dv_sc[h] += jnp.dot(ptb, do_ref[h], preferred_element_type=F32)
                dk_sc[h] += jnp.dot(dstb, q_ref[h], preferred_element_type=F32)
                dqt_ref[h, :, cols] += jnp.dot(kt_ref[h], dstb, preferred_element_type=F32)

            ahead, behind = scores(0), None
            for h in range(ATT_BWD_HEADS):
                st, dpt = ahead
                if h + 1 < ATT_BWD_HEADS:
                    ahead = scores(h + 1)
                if behind is not None:
                    accumulate(*behind)
                if masked:
                    key = lax.broadcasted_iota(jnp.int32, (bq, bq), 0)
                    qry = lax.broadcasted_iota(jnp.int32, (bq, bq), 1)
                    st = jnp.where(key <= qry, st, NEG)
                pt = jnp.exp(st)
                behind = (h, pt.astype(BF16), (pt * dpt).astype(BF16))
            accumulate(*behind)

        @pl.when(i == j)
        def _():
            sweep(True)

        @pl.when(i > j)
        def _():
            sweep(False)

        @pl.when(i == nq - 1)
        def _():
            dk_ref[...] = dk_sc[...]
            dv_ref[...] = dv_sc[...].astype(BF16)

    qblk = pl.BlockSpec((2, bq, LANES), lambda hp, s, it, jt: (hp, it[s], 0))
    kblk = pl.BlockSpec((2, bq, LANES), lambda hp, s, it, jt: (hp, jt[s], 0))
    grid_spec = pltpu.PrefetchScalarGridSpec(
        num_scalar_prefetch=2, grid=(HEAD_PAIRS, i_tab.shape[0]),
        in_specs=[qblk, kblk, kblk, qblk, pl.BlockSpec((2, LANES, bq), lambda hp, s, it, jt: (hp, 0, jt[s]))],
        out_specs=[pl.BlockSpec((2, LANES, t), lambda hp, s, it, jt: (hp, 0, 0)), kblk, kblk],
        scratch_shapes=[pltpu.VMEM((2, bq, LANES), F32), pltpu.VMEM((2, bq, LANES), F32)])
    return pl.pallas_call(body, grid_spec=grid_spec,
                          out_shape=[_sds((FOX_HEADS, LANES, t), F32), _sds((FOX_HEADS, t, LANES), F32),
                                     _sds((FOX_HEADS, t, LANES), BF16)],
                          compiler_params=_cp(), name="attn_bwd")(i_tab, j_tab, qb, kp, vp, dob, kt)


def _attn_unpack(dqt, dkp, dvp):
    t = dkp.shape[1]
    tm = min(ROW_TILE, t)
    hd = HEAD_DIM

    def body(dqt_ref, dk_ref, dv_ref, o_ref, dc_ref):
        for h in range(FOX_HEADS):
            dq = dqt_ref[h].T
            o_ref[:, h * hd:(h + 1) * hd] = (dq[:, :hd] * (hd ** -0.5)).astype(BF16)
            o_ref[:, FOX_WIDTH + h * hd:FOX_WIDTH + (h + 1) * hd] = dk_ref[h, :, :hd].astype(BF16)
            o_ref[:, 2 * FOX_WIDTH + h * hd:2 * FOX_WIDTH + (h + 1) * hd] = dv_ref[h, :, :hd]
            dc_ref[:, h:h + 1] = dq[:, K_ONE:K_ONE + 1] - dk_ref[h, :, Q_ONE:Q_ONE + 1]

    row3 = pl.BlockSpec((FOX_HEADS, tm, LANES), lambda i: (0, i, 0))
    return pl.pallas_call(
        body, grid=(t // tm,),
        in_specs=[pl.BlockSpec((FOX_HEADS, LANES, tm), lambda i: (0, 0, i)), row3, row3],
        out_specs=[pl.BlockSpec((tm, QKV), lambda i: (i, 0)), pl.BlockSpec((tm, FOX_HEADS), lambda i: (i, 0))],
        out_shape=[_sds((t, QKV), BF16), _sds((t, FOX_HEADS), F32)],
        compiler_params=_cp(), name="attn_unpack")(dqt, dkp, dvp)


def _adamw(parts, w, m, v, name):
    nl, r, c = w.shape
    tr = r
    for cand in (256, 128, 64, 32, 16):
        if r > cand and r % cand == 0:
            tr = cand
            break
    npart = len(parts)
    bc1 = 1.0 - ADAM_B1 ** ADAM_STEP
    bc2 = 1.0 - ADAM_B2 ** ADAM_STEP

    def body(*refs):
        p_refs = refs[:npart]
        w_ref, m_ref, v_ref, g_ref, d_ref, nm_ref, nv_ref = refs[npart:]
        sums = []
        for p_ref in p_refs:
            acc = p_ref[0, 0].astype(F32)
            for s in range(1, p_ref.shape[0]):
                acc = acc + p_ref[s, 0].astype(F32)
            sums.append(acc)
        g = sums[0]
        for extra in sums[1:]:
            g = g + extra
        nm = ADAM_B1 * m_ref[0] + (1.0 - ADAM_B1) * g
        nv = ADAM_B2 * v_ref[0] + (1.0 - ADAM_B2) * (g * g)
        m_hat = nm / bc1
        v_hat = nv / bc2
        g_ref[0] = g
        d_ref[0] = -ADAM_LR * (m_hat / (jnp.sqrt(v_hat) + ADAM_EPS) + ADAM_WD * w_ref[0])
        nm_ref[0] = nm
        nv_ref[0] = nv

    blk = pl.BlockSpec((1, tr, c), lambda l, i: (l, i, 0))
    return pl.pallas_call(
        body, grid=(nl, r // tr),
        in_specs=[pl.BlockSpec((p.shape[0], 1, tr, c), lambda l, i: (0, l, i, 0)) for p in parts] + [blk, blk, blk],
        out_specs=[blk] * 4, out_shape=[_sds(w.shape, F32)] * 4,
        compiler_params=_cp(), name=name)(*parts, w, m, v)


def _to_rows(a):
    flat = a.reshape(-1)
    pad = (-flat.shape[0]) % LANES
    if pad:
        flat = jnp.concatenate([flat, jnp.zeros((pad,), flat.dtype)])
    return flat.reshape(-1, LANES)


def _by_owner_cols(dw):
    k, n = dw.shape
    return dw.reshape(k, N_CHIPS, n // N_CHIPS).transpose(1, 0, 2)[:, None]


def _ffn_fwd(xin, xin_b, wi, wo, g, b, layer):
    gu, h = _ffn_in(xin_b, wi, f"ffn_in_{layer}")
    y, y_b, xhat, rstd = _mm_res_ln([(h, wo)], xin, g, b, f"ffn_out_ln_{layer}")
    return (y, y_b), (xin_b, gu, h, xhat, rstd)


def _ffn_bwd(dz, saved, wi, wo, ln_below, layer):
    xin_b, gu, h, _, _ = saved
    dgu = _ffn_bwd_hidden(dz, wo, gu, f"ffn_bwd_hidden_{layer}")
    g_out = _mm_tn(h, dz, f"ffn_dw_out_{layer}", tn=D_MODEL, tk=HALF_HIDDEN, tt=REDUCE_TILE // 2)
    g_in = _mm_tn(xin_b, dgu, f"ffn_dw_in_{layer}", tn=HALF_HIDDEN, stack_cols=True)
    below = _mm_nt([(dgu, 0, 0, 0)], [wi], f"ffn_dx_{layer}", tm=FFN_ROW_TILE, res=dz, ln=ln_below)
    return below, g_in, g_out.reshape(N_CHIPS, FFN_HIDDEN // N_CHIPS, D_MODEL)


def kernel(x, even_w_in, even_b_f, even_conv_w, even_w_out, odd_w_in, odd_v_ln_g, odd_v_ln_b, odd_w_s, odd_b_s, odd_w_out, mix_ln_g, mix_ln_b, ffn_w_in, ffn_w_out, ffn_ln_g, ffn_ln_b, loss_target, m_even_w_in, m_even_b_f, m_even_conv_w, m_even_w_out, m_odd_w_in, m_odd_v_ln_g, m_odd_v_ln_b, m_odd_w_s, m_odd_b_s, m_odd_w_out, m_mix_ln_g, m_mix_ln_b, m_ffn_w_in, m_ffn_w_out, m_ffn_ln_g, m_ffn_ln_b, v_even_w_in, v_even_b_f, v_even_conv_w, v_even_w_out, v_odd_w_in, v_odd_v_ln_g, v_odd_v_ln_b, v_odd_w_s, v_odd_b_s, v_odd_w_out, v_mix_ln_g, v_mix_ln_b, v_ffn_w_in, v_ffn_w_out, v_ffn_ln_g, v_ffn_ln_b):
    t = x.shape[1]
    d = D_MODEL
    chip = 2 * lax.axis_index("x") + lax.axis_index("y")
    x2d = x[0]
    target = loss_target[0]

    small_shard = jnp.concatenate([odd_v_ln_g.reshape(2, LANES), odd_v_ln_b.reshape(2, LANES),
                                   even_conv_w.reshape(CONV_K, LANES), jnp.zeros((1, LANES), F32)], axis=0)
    first = [even_w_in[0].astype(BF16), even_w_out[0].astype(BF16), small_shard]
    later = [odd_w_in[0].astype(BF16), odd_w_out[0].astype(BF16), ffn_w_in[0].astype(BF16), ffn_w_in[1].astype(BF16),
             ffn_w_out[0].astype(BF16), ffn_w_out[1].astype(BF16)]
    first_h, first_tok = _split_start(first, "gather4", "gather_first_start")
    later_h, later_tok = _split_start(later, "gather4", "gather_later_start", after=first_tok)
    g_ewi, g_ewo, g_small = [_with_own(g, own) for g, own in
                             zip(_split_wait(first_h, "gather_first_wait", later_tok), first)]
    ewi = g_ewi.transpose(1, 0, 2).reshape(d, EVEN_IN)
    w_even_in = jnp.concatenate([ewi[:, :QKV], ewi[:, QKV + FOX_HEADS:], ewi[:, QKV:QKV + FOX_HEADS],
                                 jnp.zeros((d, LANES - FOX_HEADS), BF16)], axis=1)
    w_even_out = g_ewo.reshape(d, d)
    v_ln_g = g_small[:, 0:2].reshape(1, d)
    v_ln_b = g_small[:, 2:4].reshape(1, d)
    conv_w = g_small[:, 4:7].transpose(1, 0, 2).reshape(CONV_K, CONV_WIDTH)
    chunk_id = jnp.arange(GMLP_BLOCK) // CHUNK
    gmask = chunk_id[None, :] <= chunk_id[:, None]
    w_spatial = jnp.where(gmask[None], odd_w_s[0], 0.0).astype(BF16)
    bs_col = odd_b_s[0].T
    b_f_col = even_b_f.reshape(FOX_HEADS, 1)
    ln = lambda p, l: p[l:l + 1]

    qkv, bch, fl = _proj(x2d, w_even_in, [(0, QKV, BF16), (QKV, QKV + BCH, F32), (QKV + BCH, EVEN_IN_PAD, F32)], "even_proj")
    fl3 = fl[:, :FOX_HEADS].T.reshape(FOX_HEADS, t // LANES, LANES).transpose(1, 0, 2)
    c3 = _fgate_fwd(fl3, b_f_col)
    c_rows = c3.transpose(1, 0, 2).reshape(FOX_HEADS, t)
    qp, kp, vp, kt, vt = _attn_pack(qkv, c_rows.T)
    attn, lse = _attn_fwd(qp, kp, vt)
    conv = _conv_fwd(bch, conv_w)
    x1, x1_b, xh1, rs1 = _mm_res_ln([(attn, w_even_out[:FOX_WIDTH]), (conv, w_even_out[FOX_WIDTH:])], x2d,
                              ln(mix_ln_g, 0), ln(mix_ln_b, 0), "even_out_ln")
    w_odd_in, g_owo, w_fi0, w_fi1, g_fo0, g_fo1 = [_with_own(g, own) for g, own in
                                                   zip(_split_wait(later_h, "gather_later_wait", x1), later)]
    w_odd_out = g_owo.reshape(d, d)
    w_ffn_in = [w_fi0, w_fi1]
    w_ffn_out = [g_fo0.reshape(FFN_HIDDEN, d), g_fo1.reshape(FFN_HIDDEN, d)]
    (x2, x2_b), ffn0 = _ffn_fwd(x1, x1_b, w_ffn_in[0], w_ffn_out[0], ln(ffn_ln_g, 0), ln(ffn_ln_b, 0), 0)

    a_odd, gated = _gmlp_fwd(x2_b, w_odd_in, v_ln_g, v_ln_b, w_spatial, bs_col)
    x3, x3_b, xh3, rs3 = _mm_res_ln([(gated, w_odd_out)], x2, ln(mix_ln_g, 1), ln(mix_ln_b, 1), "odd_out_ln")
    _, ffn1 = _ffn_fwd(x3, x3_b, w_ffn_in[1], w_ffn_out[1], ln(ffn_ln_g, 1), ln(ffn_ln_b, 1), 1)

    sq, dz4, d_fg1, d_fb1 = _loss_ln_bwd(ffn1[3], ffn1[4], ln(ffn_ln_g, 1), ln(ffn_ln_b, 1), target)
    loss = lax.psum(0.5 / d * jnp.sum(sq), ("x", "y", "c"))
    (dz3, d_mg1, d_mb1), gi_f1, go_f1 = _ffn_bwd(dz4, ffn1, w_ffn_in[1], w_ffn_out[1], (xh3, rs3, ln(mix_ln_g, 1)), 1)

    dgated = _mm_nt([(dz3, 0, 0, d)], [w_odd_out], "odd_dgated")
    go_odd = _mm_tn(gated, dz3, "odd_dw_out", tn=d).reshape(N_CHIPS, 1, d // N_CHIPS, d)
    da_odd, dws, dbs_col, d_vg, d_vb = _gmlp_bwd(dgated, a_odd, v_ln_g, v_ln_b, w_spatial, bs_col)
    gi_odd = _mm_tn(x2_b, da_odd, "odd_dw_in", tn=d // 2, stack_cols=True)[:, None]
    dz2, d_fg0, d_fb0 = _mm_nt([(da_odd, 0, 0, 0)], [w_odd_in], "odd_dx", res=dz3,
                               ln=(ffn0[3], ffn0[4], ln(ffn_ln_g, 0)))
    (dz1, d_mg0, d_mb0), gi_f0, go_f0 = _ffn_bwd(dz2, ffn0, w_ffn_in[0], w_ffn_out[0], (xh1, rs1, ln(mix_ln_g, 0)), 0)

    sent_early = [gi_odd, go_odd, jnp.stack([gi_f0, gi_f1], axis=1), jnp.stack([go_f0, go_f1], axis=1)]
    early_h, early_tok = _split_start(sent_early, "scatter4", "scatter_early_start")
    dmix = _mm_nt([(dz1, 0, 0, d)], [w_even_out], "even_dmix", after=early_tok)
    mix =jnp.concatenate([attn, conv], axis=1)
    go_even = _mm_tn(mix, dz1, "even_dw_out", tn=d).reshape(N_CHIPS, 1, d // N_CHIPS, d)
    dbch, dconv_w8 = _conv_bwd(bch, dmix, conv_w)
    qb, dob = _attn_bwd_prep(attn, dmix, qp, lse.reshape(FOX_HEADS, t).T)
    dqkv, dc_col = _attn_unpack(*_attn_bwd(qb, kp, vp, dob, kt))
    dc3 = dc_col.T.reshape(FOX_HEADS, t // LANES, LANES).transpose(1, 0, 2)
    dfl3, d_bf = _fgate_bwd(dc3, fl3, b_f_col)
    dfl = jnp.concatenate([dfl3.transpose(1, 0, 2).reshape(FOX_HEADS, t).T.astype(BF16),
                           jnp.zeros((t, LANES - FOX_HEADS), BF16)], axis=1)

    dws_masked = jnp.where(gmask[None], dws, 0.0)
    rep_names = ["odd_w_s", "odd_b_s", "mix_ln_g", "mix_ln_b", "ffn_ln_g", "ffn_ln_b", "even_b_f"]
    rep_grads = [dws_masked, dbs_col.T, jnp.concatenate([d_mg0, d_mg1]), jnp.concatenate([d_mb0, d_mb1]),
                 jnp.concatenate([d_fg0, d_fg1]), jnp.concatenate([d_fb0, d_fb1]), d_bf.reshape(1, FOX_HEADS)]
    rep_w = [(odd_w_s, m_odd_w_s, v_odd_w_s), (odd_b_s, m_odd_b_s, v_odd_b_s), (mix_ln_g, m_mix_ln_g, v_mix_ln_g),
             (mix_ln_b, m_mix_ln_b, v_mix_ln_b), (ffn_ln_g, m_ffn_ln_g, v_ffn_ln_g), (ffn_ln_b, m_ffn_ln_b, v_ffn_ln_b),
             (even_b_f, m_even_b_f, v_even_b_f)]
    rep_rows = [_to_rows(gr) for gr in rep_grads]
    n_rep = sum(r.shape[0] for r in rep_rows)
    pad_rep = (-n_rep) % SUBLANES
    dconv_w = dconv_w8[:CONV_K].reshape(CONV_K, N_CHIPS, LANES).transpose(1, 0, 2).reshape(N_CHIPS * CONV_K, LANES)
    packed = jnp.concatenate(rep_rows + [jnp.zeros((pad_rep, LANES), F32), d_vg.reshape(SUBLANES, LANES),
                                         d_vb.reshape(SUBLANES, LANES), dconv_w, jnp.zeros((4, LANES), F32)], axis=0)
    small_h, small_tok = _split_start([packed], "gather8", "gather_small_start")

    chip_blk = lambda g: lax.dynamic_index_in_dim(g, chip, 0, keepdims=False)
    mine_early = [_with_own(r, chip_blk(g)) for r, g in
                  zip(_split_wait(early_h, "scatter_early_wait", small_tok), sent_early)]
    swap_h, swap_tok = _split_start(mine_early, "swap2", "swap_early_start")
    dw_qkv = _mm_tn(x2d, dqkv, "even_dw_qkv", tn=QKV // 2, out_dtype=F32, after=swap_tok)
    dw_bch = _mm_tn(x2d, dbch, "even_dw_bch", tn=BCH // 2, out_dtype=F32)
    dw_f = _mm_tn(x2d, dfl, "even_dw_f", tn=LANES, out_dtype=F32)
    gi_even = _by_owner_cols(jnp.concatenate([dw_qkv, dw_f[:, :FOX_HEADS], dw_bch], axis=1).astype(BF16))
    sent_late = [gi_even, go_even]
    late_h, late_tok = _split_start(sent_late, "scatter4", "scatter_late_start")
    grad_x = _mm_nt([(dqkv, 0, 0, QKV), (dbch, 0, QKV, QKV + BCH), (dfl, 0, QKV + BCH, EVEN_IN_PAD)], [w_even_in],
                    "even_dx", res=dz1, after=late_tok)
    mine_late = [_with_own(r, chip_blk(g)) for r, g in zip(_split_wait(late_h, "scatter_late_wait", grad_x), sent_late)]
    theirs_late = _exchange(mine_late, "swap2", "swap_late")
    theirs_early = _split_wait(swap_h, "swap_early_wait", theirs_late[0])
    (gathered,) = _split_wait(small_h, "gather_small_wait", theirs_early[0])
    gathered = lax.dynamic_update_index_in_dim(gathered, packed, 4 * lax.axis_index("x") + 2 * lax.axis_index("y")
                                               + lax.axis_index("c"), 0)
    mine, theirs = mine_late + mine_early, theirs_late + theirs_early
    big_w = [(even_w_in, m_even_w_in, v_even_w_in), (even_w_out, m_even_w_out, v_even_w_out),
             (odd_w_in, m_odd_w_in, v_odd_w_in), (odd_w_out, m_odd_w_out, v_odd_w_out),
             (ffn_w_in, m_ffn_w_in, v_ffn_w_in), (ffn_w_out, m_ffn_w_out, v_ffn_w_out)]
    big_names = ["even_w_in", "even_w_out", "odd_w_in", "odd_w_out", "ffn_w_in", "ffn_w_out"]
    res = {}
    for nm, own, sib, (w, m, v) in zip(big_names, mine, theirs, big_w):
        res[nm] = _adamw([own, sib], w, m, v, f"adamw_{nm}")

    base = n_rep + pad_rep
    own_rows = jnp.concatenate([
        lax.dynamic_slice_in_dim(gathered, base + 2 * chip, 2, axis=1),
        lax.dynamic_slice_in_dim(gathered, base + SUBLANES + 2 * chip, 2, axis=1),
        lax.dynamic_slice_in_dim(gathered, base + 2 * SUBLANES + CONV_K * chip, CONV_K, axis=1),
        jnp.zeros((N_DEV, 1, LANES), F32)], axis=1)
    small_parts = jnp.concatenate([gathered[:, :base], own_rows], axis=1)[:, None]

    def pack_small(get):
        rows = [_to_rows(get(tw)) for tw in rep_w] + [jnp.zeros((pad_rep, LANES), F32)]
        rows += [get(sh).reshape(-1, LANES) for sh in ((odd_v_ln_g, m_odd_v_ln_g, v_odd_v_ln_g),
                                                       (odd_v_ln_b, m_odd_v_ln_b, v_odd_v_ln_b),
                                                       (even_conv_w, m_even_conv_w, v_even_conv_w))]
        return jnp.concatenate(rows + [jnp.zeros((1, LANES), F32)], axis=0)[None]

    small_out = _adamw([small_parts], pack_small(lambda tw: tw[0]), pack_small(lambda tw: tw[1]),
                       pack_small(lambda tw: tw[2]), "adamw_small")

    def unpack_small(rows3):
        rows = rows3[0]
        out, off = {}, 0
        for nm, (w, _, _), r in zip(rep_names, rep_w, rep_rows):
            out[nm] = rows[off:off + r.shape[0]].reshape(-1)[:w.size].reshape(w.shape)
            off += r.shape[0]
        off += pad_rep
        out["odd_v_ln_g"] = rows[off:off + 2].reshape(odd_v_ln_g.shape)
        out["odd_v_ln_b"] = rows[off + 2:off + 4].reshape(odd_v_ln_b.shape)
        out["even_conv_w"] = rows[off + 4:off + 4 + CONV_K].reshape(even_conv_w.shape)
        return out

    small = [unpack_small(o) for o in small_out]
    order = ["even_w_in", "even_b_f", "even_conv_w", "even_w_out", "odd_w_in", "odd_v_ln_g", "odd_v_ln_b", "odd_w_s",
             "odd_b_s", "odd_w_out", "mix_ln_g", "mix_ln_b", "ffn_w_in", "ffn_w_out", "ffn_ln_g", "ffn_ln_b"]
    outs = [loss, grad_x[None]]
    for kind in range(4):
        for nm in order:
            outs.append(res[nm][kind] if nm in res else small[kind][nm])
    return tuple(outs)
```

```python
import functools
import math

import jax
import jax.numpy as jnp
from jax import lax
from jax.experimental import pallas as pl
from jax.experimental.pallas import tpu as pltpu

F32 = jnp.float32
BF16 = jnp.bfloat16

D_MODEL = 1024
FOX_HEADS = 8
HEAD_DIM = 64
HEAD_PAIRS = FOX_HEADS // 2
FOX_WIDTH = FOX_HEADS * HEAD_DIM
CONV_WIDTH = 512
CONV_K = 3
QKV = 3 * FOX_WIDTH
BCH = 3 * CONV_WIDTH
EVEN_IN = QKV + FOX_HEADS + BCH
EVEN_IN_PAD = QKV + BCH + 128
GMLP_BLOCK = 128
GMLP_GROUPS = 8
CHUNK = 64
FFN_HIDDEN = 2816
HALF_HIDDEN = FFN_HIDDEN // 2
ALPHA = 4.0 ** 0.25
LN_EPS = 1e-5
ADAM_LR = 0.001
ADAM_B1 = 0.9
ADAM_B2 = 0.999
ADAM_EPS = 1e-08
ADAM_WD = 0.01
ADAM_STEP = 10
N_CHIPS = 4
N_DEV = 8
LANES = 128
SUBLANES = 8
ROW_TILE = 512
FFN_ROW_TILE = 512
REDUCE_TILE = 2048
ATT_BLOCK = 512
ATT_FWD_HEADS = 4
ATT_BWD_HEADS = 4
VMEM_LIMIT = 56 * 2 ** 20
NEG = -1e30
MESH = pl.DeviceIdType.MESH
HIGHEST = lax.Precision.HIGHEST
Q_C, Q_ONE, Q_LSE = 64, 67, 70
K_ONE, K_C, K_ONE2 = 64, 67, 70
V_ONE = 64
DO_DELTA = 65
NT = (((1,), (1,)), ((), ()))
TN = (((0,), (0,)), ((), ()))


def _cp():
    return pltpu.CompilerParams(vmem_limit_bytes=VMEM_LIMIT)


def _resident(shape):
    zeros = (0,) * len(shape)
    return pl.BlockSpec(shape, lambda *_: zeros, pipeline_mode=pl.Buffered(1))


def _sds(shape, dtype):
    return jax.ShapeDtypeStruct(tuple(shape), dtype)


_MASKS = {
    "gather4": [(1, 0, 0), (0, 1, 0), (1, 1, 0)],
    "scatter4": [(1, 0, 0), (0, 1, 0), (1, 1, 0)],
    "swap2": [(0, 0, 1)],
    "gather8": [(0, 0, 1), (0, 1, 0), (0, 1, 1), (1, 0, 0), (1, 0, 1), (1, 1, 0), (1, 1, 1)],
}


def _exchange(arrs, mode, name):
    n = len(arrs)
    masks = _MASKS[mode]
    npeer = len(masks)
    lead = {"gather4": N_CHIPS, "gather8": N_DEV}.get(mode)
    out_shapes = [_sds(((lead,) if lead else ()) + a.shape, a.dtype) for a in arrs]

    def body(*refs):
        ins, outs = refs[:n], refs[n:2 * n]
        send_sems, recv_sems, loc_sems = refs[2 * n:]
        x, y, c = lax.axis_index("x"), lax.axis_index("y"), lax.axis_index("c")
        chip, dev = 2 * x + y, 4 * x + 2 * y + c
        sends, recvs, locs = [], [], []
        for k in range(n):
            if mode == "gather4":
                locs.append(pltpu.make_async_copy(ins[k], outs[k].at[chip], loc_sems.at[k]))
            elif mode == "scatter4":
                locs.append(pltpu.make_async_copy(ins[k].at[chip], outs[k].at[chip], loc_sems.at[k]))
            elif mode == "gather8":
                locs.append(pltpu.make_async_copy(ins[k], outs[k].at[dev], loc_sems.at[k]))
        for cp in locs:
            cp.start()
        for k in range(n):
            for j, (dx, dy, dc) in enumerate(masks):
                px = 1 - x if dx else x
                py = 1 - y if dy else y
                pc = 1 - c if dc else c
                pchip, pdev = 2 * px + py, 4 * px + 2 * py + pc
                if mode == "gather4":
                    src, dst, land = ins[k], outs[k].at[chip], outs[k].at[pchip]
                elif mode == "scatter4":
                    src, dst, land = ins[k].at[pchip], outs[k].at[chip], outs[k].at[pchip]
                elif mode == "swap2":
                    src, dst, land = ins[k], outs[k], outs[k]
                else:
                    src, dst, land = ins[k], outs[k].at[dev], outs[k].at[pdev]
                s = k * npeer + j
                kw = dict(send_sem=send_sems.at[s], recv_sem=recv_sems.at[s], device_id=(px, py, pc),
                          device_id_type=MESH)
                cp = pltpu.make_async_remote_copy(src_ref=src, dst_ref=dst, **kw)
                cp.start()
                sends.append(cp)
                recvs.append(pltpu.make_async_remote_copy(src_ref=src, dst_ref=land, **kw))
        for cp in recvs:
            cp.wait_recv()
        for cp in sends:
            cp.wait_send()
        for cp in locs:
            cp.wait()

    any_spec = pl.BlockSpec(memory_space=pl.ANY)
    outs = pl.pallas_call(
        body,
        out_shape=out_shapes,
        in_specs=[any_spec] * n,
        out_specs=[any_spec] * n,
        scratch_shapes=[pltpu.SemaphoreType.DMA((n * npeer,)), pltpu.SemaphoreType.DMA((n * npeer,)),
                        pltpu.SemaphoreType.DMA((max(n, 1),))],
        name=name,
    )(*arrs)
    return list(outs)


_HBM_SPEC = pl.BlockSpec(memory_space=pltpu.HBM)
_SEM_SPEC = pl.BlockSpec(memory_space=pltpu.SEMAPHORE)
_ANY_SPEC = pl.BlockSpec(memory_space=pl.ANY)
_EFFECT = pltpu.SideEffectType.DATAFLOW_SIDE_EFFECTING


def _split_copies(mode, ins, lands, send_sems, recv_sems):
    x, y, c = lax.axis_index("x"), lax.axis_index("y"), lax.axis_index("c")
    chip, dev = 2 * x + y, 4 * x + 2 * y + c
    masks = _MASKS[mode]
    out = []
    for k in range(len(ins)):
        for j, (dx, dy, dc) in enumerate(masks):
            px = 1 - x if dx else x
            py = 1 - y if dy else y
            pc = 1 - c if dc else c
            pchip, pdev = 2 * px + py, 4 * px + 2 * py + pc
            if mode == "gather4":
                src, dst, land = ins[k], lands[k].at[chip], lands[k].at[pchip]
            elif mode == "scatter4":
                src, dst, land = ins[k].at[pchip], lands[k].at[chip], lands[k].at[pchip]
            elif mode == "swap2":
                src, dst, land = ins[k], lands[k], lands[k]
            else:
                src, dst, land = ins[k], lands[k].at[dev], lands[k].at[pdev]
            s = k * len(masks) + j
            kw = dict(send_sem=send_sems.at[s], recv_sem=recv_sems.at[s], device_id=(px, py, pc), device_id_type=MESH)
            out.append((pltpu.make_async_remote_copy(src_ref=src, dst_ref=dst, **kw),
                        pltpu.make_async_remote_copy(src_ref=src, dst_ref=land, **kw)))
    return out


def _split_start(arrs, mode, name, after=None):
    n = len(arrs)
    nsem = n * len(_MASKS[mode])
    lead = {"gather4": (N_CHIPS,), "gather8": (N_DEV,)}.get(mode, ())
    land_shapes = [lead + a.shape for a in arrs]

    def body(*refs):
        ins, lands = refs[:n], refs[n:2 * n]
        outs = refs[2 * n + (after is not None):]
        for start, _ in _split_copies(mode, ins, lands, outs[0], outs[1]):
            start.start()
        outs[-1][...] = jnp.zeros(outs[-1].shape, F32)

    srcs = [pltpu.with_memory_space_constraint(a, pltpu.HBM) for a in arrs]
    empties = [pltpu.with_memory_space_constraint(lax.empty(s, a.dtype), pltpu.HBM) for s, a in zip(land_shapes, arrs)]
    res = pl.pallas_call(
        body, name=name,
        out_shape=(pltpu.SemaphoreType.DMA((nsem,)), pltpu.SemaphoreType.DMA((nsem,)),
                   *[pltpu.HBM(a.shape, a.dtype) for a in arrs],
                   *[pltpu.HBM(s, a.dtype) for s, a in zip(land_shapes, arrs)],
                   _sds((SUBLANES, LANES), F32)),
        in_specs=[_HBM_SPEC] * (2 * n) + ([_ANY_SPEC] if after is not None else []),
        out_specs=(_SEM_SPEC, _SEM_SPEC, *[_HBM_SPEC] * (2 * n), pl.BlockSpec(memory_space=pltpu.VMEM)),
        input_output_aliases={k: 2 + k for k in range(2 * n)},
        compiler_params=pltpu.CompilerParams(has_side_effects=_EFFECT),
    )(*srcs, *empties, *([after] if after is not None else []))
    return dict(mode=mode, n=n, sems=res[:2], bufs=res[2:2 + 2 * n]), res[-1]


def _split_wait(handle, name, after):
    n, mode = handle["n"], handle["mode"]

    def body(*refs):
        ins, lands = refs[:n], refs[n:2 * n]
        send_sems, recv_sems = refs[2 * n], refs[2 * n + 1]
        for _, arrival in _split_copies(mode, ins, lands, send_sems, recv_sems):
            arrival.wait_send()
            arrival.wait_recv()

    bufs = handle["bufs"]
    res = pl.pallas_call(
        body, name=name,
        out_shape=tuple(pltpu.HBM(b.shape, b.dtype) for b in bufs),
        in_specs=[_HBM_SPEC] * (2 * n) + [_SEM_SPEC, _SEM_SPEC, _ANY_SPEC],
        out_specs=tuple([_HBM_SPEC] * (2 * n)),
        input_output_aliases={k: k for k in range(2 * n)},
        compiler_params=pltpu.CompilerParams(has_side_effects=_EFFECT),
    )(*bufs, *handle["sems"], after)
    return list(res[n:])


def _with_own(landed, own):
    chip = 2 * lax.axis_index("x") + lax.axis_index("y")
    return lax.dynamic_update_index_in_dim(landed, own, chip, 0)


def _sigmoid(x):
    return 0.5 * jnp.tanh(0.5 * x) + 0.5


def _log_sigmoid(x):
    e = jnp.exp(-jnp.abs(x))
    log1p = jnp.where(e < 1e-2, e * (1.0 - e * (0.5 - e * (1.0 / 3.0))), jnp.log(1.0 + e))
    return jnp.minimum(x, 0.0) - log1p


def _gelu(a):
    return 0.5 * a * (1.0 + lax.erf(a * (2.0 ** -0.5)))


def _gelu_grad(a):
    cdf = 0.5 * (1.0 + lax.erf(a * (2.0 ** -0.5)))
    pdf = jnp.exp(-0.5 * a * a) * (1.0 / math.sqrt(2.0 * math.pi))
    return cdf + a * pdf


def _ln_fwd(z):
    mu = jnp.mean(z, axis=-1, keepdims=True)
    zc = z - mu
    var = jnp.mean(zc * zc, axis=-1, keepdims=True)
    rstd = lax.rsqrt(var + LN_EPS)
    return zc * rstd, rstd


def _ln_bwd(dy, xhat, rstd, g):
    dxh = dy * g
    m1 = jnp.mean(dxh, axis=-1, keepdims=True)
    m2 = jnp.mean(dxh * xhat, axis=-1, keepdims=True)
    dz = rstd * (dxh - m1 - xhat * m2)
    return dz, jnp.sum(dy * xhat, axis=0, keepdims=True), jnp.sum(dy, axis=0, keepdims=True)


def _shift_down(z, halo):
    r = lax.broadcasted_iota(jnp.int32, z.shape, 0)
    z1 = jnp.where(r == 0, halo[7:8, :], pltpu.roll(z, 1, 0))
    z2 = jnp.where(r == 0, halo[6:7, :], jnp.where(r == 1, halo[7:8, :], pltpu.roll(z, 2, 0)))
    return z1, z2


def _shift_up(z, halo):
    n = z.shape[0]
    r = lax.broadcasted_iota(jnp.int32, z.shape, 0)
    z1 = jnp.where(r == n - 1, halo[0:1, :], pltpu.roll(z, n - 1, 0))
    z2 = jnp.where(r == n - 1, halo[1:2, :], jnp.where(r == n - 2, halo[0:1, :], pltpu.roll(z, n - 2, 0)))
    return z1, z2


def _accumulate(ref, first, value):
    @pl.when(first)
    def _():
        ref[...] = value

    @pl.when(jnp.logical_not(first))
    def _():
        ref[...] += value


def _proj(x, w, splits, name):
    t, k = x.shape
    tm = min(ROW_TILE, t)

    def body(x_ref, w_ref, *outs):
        a = x_ref[...].astype(BF16)
        for (lo, hi, dt), o in zip(splits, outs):
            o[...] = jnp.dot(a, w_ref[:, lo:hi], preferred_element_type=F32).astype(dt)

    return pl.pallas_call(
        body, grid=(t // tm,),
        in_specs=[pl.BlockSpec((tm, k), lambda i: (i, 0)), _resident(w.shape)],
        out_specs=[pl.BlockSpec((tm, hi - lo), lambda i: (i, 0)) for lo, hi, _ in splits],
        out_shape=[_sds((t, hi - lo), dt) for lo, hi, dt in splits],
        compiler_params=_cp(), name=name)(x, w)


def _fgate_fwd(fl3, b_f):
    nc = fl3.shape[0]

    def body(f_ref, b_ref, c_ref):
        r = lax.broadcasted_iota(jnp.int32, (LANES, LANES), 0)
        cidx = lax.broadcasted_iota(jnp.int32, (LANES, LANES), 1)
        upper = (r <= cidx).astype(F32)

        def step(i, carry):
            lf = _log_sigmoid(f_ref[i] + b_ref[...])
            cc = jnp.dot(lf, upper, precision=HIGHEST, preferred_element_type=F32) + carry
            c_ref[i] = cc
            return cc[:, LANES - 1:LANES]

        lax.fori_loop(0, nc, step, jnp.zeros((FOX_HEADS, 1), F32))

    return pl.pallas_call(body, out_shape=_sds(fl3.shape, F32), name="fgate_fwd")(fl3, b_f)


def _fgate_bwd(dc3, fl3, b_f):
    nc = fl3.shape[0]

    def body(dc_ref, f_ref, b_ref, df_ref, db_ref):
        r = lax.broadcasted_iota(jnp.int32, (LANES, LANES), 0)
        cidx = lax.broadcasted_iota(jnp.int32, (LANES, LANES), 1)
        lower = (r >= cidx).astype(F32)

        def step(n, carry):
            suffix, db = carry
            i = nc - 1 - n
            dlf = jnp.dot(dc_ref[i], lower, precision=HIGHEST, preferred_element_type=F32) + suffix
            df = dlf * (1.0 - _sigmoid(f_ref[i] + b_ref[...]))
            df_ref[i] = df
            return dlf[:, 0:1], db + jnp.sum(df, axis=1, keepdims=True)

        zero = jnp.zeros((FOX_HEADS, 1), F32)
        _, db = lax.fori_loop(0, nc, step, (zero, zero))
        db_ref[...] = db

    return pl.pallas_call(body, out_shape=[_sds(fl3.shape, F32), _sds((FOX_HEADS, 1), F32)],
                          name="fgate_bwd")(dc3, fl3, b_f)


def _split3(c):
    hi = c.astype(BF16).astype(F32)
    mid = (c - hi).astype(BF16).astype(F32)
    lo = (c - hi - mid).astype(BF16).astype(F32)
    return hi, mid, lo


def _lane_pieces(lane, start, pieces, sign):
    out = jnp.zeros(lane.shape, F32)
    for n, p in enumerate(pieces):
        out = jnp.where(lane == start + n, sign * p, out)
    return out


def _attn_pack(qkv, c_col):
    t = qkv.shape[0]
    tm = min(ROW_TILE, t)
    hd = HEAD_DIM

    def body(x_ref, c_ref, qp_ref, kp_ref, vp_ref, kt_ref, vt_ref):
        lane = lax.broadcasted_iota(jnp.int32, (tm, hd), 1) + hd
        for h in range(FOX_HEADS):
            pieces = _split3(c_ref[:, h:h + 1])
            ones = lambda a, b: jnp.where(jnp.logical_and(lane >= a, lane < b), 1.0, 0.0)
            q_extra = _lane_pieces(lane, Q_C, pieces, 1.0) + ones(Q_ONE, Q_ONE + 3)
            k_extra = _lane_pieces(lane, K_C, pieces, -1.0) + ones(K_ONE, K_ONE + 3) + ones(K_ONE2, K_ONE2 + 3)
            qp_ref[h, :, :hd] = (x_ref[:, h * hd:(h + 1) * hd].astype(F32) * (hd ** -0.5)).astype(BF16)
            qp_ref[h, :, hd:] = q_extra.astype(BF16)
            kp_ref[h, :, :hd] = x_ref[:, FOX_WIDTH + h * hd:FOX_WIDTH + (h + 1) * hd]
            kp_ref[h, :, hd:] = k_extra.astype(BF16)
            vp_ref[h, :, :hd] = x_ref[:, 2 * FOX_WIDTH + h * hd:2 * FOX_WIDTH + (h + 1) * hd]
            vp_ref[h, :, hd:] = ones(V_ONE, V_ONE + 4).astype(BF16)
            kt_ref[h] = kp_ref[h].astype(F32).T.astype(BF16)
            vt_ref[h] = vp_ref[h].astype(F32).T.astype(BF16)

    row3 = pl.BlockSpec((FOX_HEADS, tm, LANES), lambda i: (0, i, 0))
    col3 = pl.BlockSpec((FOX_HEADS, LANES, tm), lambda i: (0, 0, i))
    return pl.pallas_call(
        body, grid=(t // tm,),
        in_specs=[pl.BlockSpec((tm, QKV), lambda i: (i, 0)), pl.BlockSpec((tm, FOX_HEADS), lambda i: (i, 0))],
        out_specs=[row3, row3, row3, col3, col3],
        out_shape=[_sds((FOX_HEADS, t, LANES), BF16)] * 3 + [_sds((FOX_HEADS, LANES, t), BF16)] * 2,
        compiler_params=_cp(), name="attn_pack")(qkv, c_col)


def _triangle(nq, key_major):
    if key_major:
        pairs = [(i, j) for j in range(nq) for i in range(j, nq)]
    else:
        pairs = [(i, j) for i in range(nq) for j in range(i + 1)]
    return jnp.asarray([p[0] for p in pairs], jnp.int32), jnp.asarray([p[1] for p in pairs], jnp.int32)


def _attn_fwd(qp, kp, vt):
    t = qp.shape[1]
    bq = min(ATT_BLOCK, t)
    nq = t // bq
    nh = ATT_FWD_HEADS
    i_tab, j_tab = _triangle(nq, key_major=False)

    def body(it_ref, jt_ref, q_ref, k_ref, vt_ref, o_ref, lse_ref, m_sc, acc_sc):
        s = pl.program_id(1)
        i, j = it_ref[s], jt_ref[s]

        @pl.when(j == 0)
        def _():
            m_sc[...] = jnp.full(m_sc.shape, NEG, F32)
            acc_sc[...] = jnp.zeros(acc_sc.shape, F32)

        def sweep(masked):
            scores = lambda h: lax.dot_general(k_ref[h], q_ref[h], NT, preferred_element_type=F32)

            def accumulate(h, pt, rescale):
                acc_sc[h] = rescale * acc_sc[h] + jnp.dot(vt_ref[h], pt, preferred_element_type=F32)

            ahead, behind = scores(0), None
            for h in range(nh):
                st = ahead
                if h + 1 < nh:
                    ahead = scores(h + 1)
                if behind is not None:
                    accumulate(*behind)
                if masked:
                    key = lax.broadcasted_iota(jnp.int32, (bq, bq), 0)
                    qry = lax.broadcasted_iota(jnp.int32, (bq, bq), 1)
                    st = jnp.where(key <= qry, st, NEG)
                m_prev = m_sc[h]
                m_new = jnp.maximum(m_prev, jnp.max(st, axis=0, keepdims=True))
                behind = (h, jnp.exp(st - m_new).astype(BF16), jnp.exp(m_prev - m_new))
                m_sc[h] = m_new
            accumulate(*behind)

        @pl.when(j < i)
        def _():
            sweep(False)

        @pl.when(j == i)
        def _():
            sweep(True)
            for h in range(nh):
                acc = acc_sc[h]
                denom = acc[V_ONE:V_ONE + 1, :]
                o_ref[:, h * HEAD_DIM:(h + 1) * HEAD_DIM] = (acc[:HEAD_DIM, :] / denom).T.astype(BF16)
                lse_ref[h] = m_sc[h] + jnp.log(denom)

    grid_spec = pltpu.PrefetchScalarGridSpec(
        num_scalar_prefetch=2, grid=(FOX_HEADS // nh, i_tab.shape[0]),
        in_specs=[pl.BlockSpec((nh, bq, LANES), lambda hp, s, it, jt: (hp, it[s], 0)),
                  pl.BlockSpec((nh, bq, LANES), lambda hp, s, it, jt: (hp, jt[s], 0)),
                  pl.BlockSpec((nh, LANES, bq), lambda hp, s, it, jt: (hp, 0, jt[s]))],
        out_specs=[pl.BlockSpec((bq, nh * HEAD_DIM), lambda hp, s, it, jt: (it[s], hp)),
                   pl.BlockSpec((nh, 1, bq), lambda hp, s, it, jt: (hp, 0, it[s]))],
        scratch_shapes=[pltpu.VMEM((nh, 1, bq), F32), pltpu.VMEM((nh, LANES, bq), F32)])
    return pl.pallas_call(body, grid_spec=grid_spec,
                          out_shape=[_sds((t, FOX_WIDTH), BF16), _sds((FOX_HEADS, 1, t), F32)],
                          compiler_params=_cp(), name="attn_fwd")(i_tab, j_tab, qp, kp, vt)


def _conv_fwd(bch, conv_w):
    t = bch.shape[0]
    tm = min(ROW_TILE, t)
    halo_blocks = tm // SUBLANES
    cw = CONV_WIDTH

    def body(cur_ref, prev_ref, w_ref, o_ref):
        i = pl.program_id(0)
        z = cur_ref[:, cw:2 * cw] * cur_ref[:, 2 * cw:]
        zp = jnp.where(i == 0, 0.0, prev_ref[:, cw:2 * cw] * prev_ref[:, 2 * cw:])
        z1, z2 = _shift_down(z, zp)
        y = w_ref[0:1, :] * z2 + w_ref[1:2, :] * z1 + w_ref[2:3, :] * z
        o_ref[...] = (cur_ref[:, :cw] * y).astype(BF16)

    return pl.pallas_call(
        body, grid=(t // tm,),
        in_specs=[pl.BlockSpec((tm, BCH), lambda i: (i, 0)),
                  pl.BlockSpec((SUBLANES, BCH), lambda i: (jnp.maximum(i * halo_blocks - 1, 0), 0)),
                  _resident(conv_w.shape)],
        out_specs=pl.BlockSpec((tm, cw), lambda i: (i, 0)),
        out_shape=_sds((t, cw), BF16), compiler_params=_cp(), name="conv_fwd")(bch, bch, conv_w)


def _mm_res_ln(pairs, res, g, b, name):
    t, d = res.shape
    tm = min(ROW_TILE, t)
    n = len(pairs)

    def body(*refs):
        a_refs, w_refs = refs[:n], refs[n:2 * n]
        res_ref, g_ref, b_ref, y_ref, yb_ref, xh_ref, rs_ref = refs[2 * n:]
        z = ALPHA * res_ref[...]
        for a_ref, w_ref in zip(a_refs, w_refs):
            z = z + jnp.dot(a_ref[...].astype(BF16), w_ref[...], preferred_element_type=F32)
        xhat, rstd = _ln_fwd(z)
        y = xhat * g_ref[...] + b_ref[...]
        y_ref[...] = y
        yb_ref[...] = y.astype(BF16)
        xh_ref[...] = xhat
        rs_ref[...] = rstd

    row = lambda i: (i, 0)
    full = pl.BlockSpec((tm, d), row)
    return pl.pallas_call(
        body, grid=(t // tm,),
        in_specs=[pl.BlockSpec((tm, a.shape[1]), row) for a, _ in pairs] + [_resident(w.shape) for _, w in pairs]
        + [full, _resident(g.shape), _resident(b.shape)],
        out_specs=[full, full, full, pl.BlockSpec((tm, 1), row)],
        out_shape=[_sds((t, d), F32), _sds((t, d), BF16), _sds((t, d), F32), _sds((t, 1), F32)],
        compiler_params=_cp(), name=name)(*[a for a, _ in pairs], *[w for _, w in pairs], res, g, b)


def _ffn_in(x, wi, name):
    t, d = x.shape
    tm = min(FFN_ROW_TILE, t)
    hh = HALF_HIDDEN

    def body(x_ref, w_ref, gu_ref, h_ref):
        a = x_ref[...].astype(BF16)
        for c in range(2):
            gs, us = slice(c * hh, (c + 1) * hh), slice(FFN_HIDDEN + c * hh, FFN_HIDDEN + (c + 1) * hh)
            g = jnp.dot(a, w_ref[c], preferred_element_type=F32)
            u = jnp.dot(a, w_ref[2 + c], preferred_element_type=F32)
            gu_ref[:, gs] = g.astype(BF16)
            gu_ref[:, us] = u.astype(BF16)
            h_ref[:, gs] = (g * _sigmoid(g) * u).astype(BF16)

    row = lambda i: (i, 0)
    return pl.pallas_call(
        body, grid=(t // tm,),
        in_specs=[pl.BlockSpec((tm, d), row), _resident(wi.shape)],
        out_specs=[pl.BlockSpec((tm, 2 * FFN_HIDDEN), row), pl.BlockSpec((tm, FFN_HIDDEN), row)],
        out_shape=[_sds((t, 2 * FFN_HIDDEN), BF16), _sds((t, FFN_HIDDEN), BF16)],
        compiler_params=_cp(), name=name)(x, wi)


def _gmlp_fwd(x, w_in, vg, vb, wm, bs_col):
    t, d = x.shape
    tm = min(ROW_TILE, t)
    gb = GMLP_BLOCK

    def body(x_ref, w_ref, vg_ref, vb_ref, wm_ref, bs_ref, a_ref, o_ref):
        xb = x_ref[...].astype(BF16)
        nc = w_ref.shape[2]
        for j in range(w_ref.shape[0]):
            a_ref[:, j * nc:(j + 1) * nc] = jnp.dot(xb, w_ref[j], preferred_element_type=F32)
        u = _gelu(a_ref[:, :d])
        vhat, _ = _ln_fwd(_gelu(a_ref[:, d:]))
        vln = (vhat * vg_ref[...] + vb_ref[...]).astype(BF16)
        for blk in range(tm // gb):
            rs = slice(blk * gb, (blk + 1) * gb)
            for gi in range(GMLP_GROUPS):
                cs = slice(gi * gb, (gi + 1) * gb)
                s = jnp.dot(wm_ref[gi], vln[rs, cs], preferred_element_type=F32) + bs_ref[:, gi:gi + 1]
                o_ref[rs, cs] = (u[rs, cs] * s).astype(BF16)

    row = lambda i: (i, 0)
    return pl.pallas_call(
        body, grid=(t // tm,),
        in_specs=[pl.BlockSpec((tm, d), row), _resident(w_in.shape), _resident(vg.shape), _resident(vb.shape),
                  _resident(wm.shape), _resident(bs_col.shape)],
        out_specs=[pl.BlockSpec((tm, 2 * d), row), pl.BlockSpec((tm, d), row)],
        out_shape=[_sds((t, 2 * d), F32), _sds((t, d), BF16)],
        compiler_params=_cp(), name="gmlp_fwd")(x, w_in, vg, vb, wm, bs_col)


def _loss_ln_bwd(xhat, rstd, g, b, target):
    t, d = xhat.shape
    tm = min(ROW_TILE, t)

    def body(xh_ref, rs_ref, g_ref, b_ref, t_ref, sq_ref, dz_ref, dg_ref, db_ref):
        first = pl.program_id(0) == 0
        xh = xh_ref[...]
        err = xh * g_ref[...] + b_ref[...] - t_ref[...]
        dz, dg, db = _ln_bwd(err * (1.0 / d), xh, rs_ref[...], g_ref[...])
        dz_ref[...] = dz
        _accumulate(sq_ref, first, jnp.sum(err * err, axis=0, keepdims=True))
        _accumulate(dg_ref, first, dg)
        _accumulate(db_ref, first, db)

    row = lambda i: (i, 0)
    vec = pl.BlockSpec((1, d), lambda i: (0, 0))
    return pl.pallas_call(
        body, grid=(t // tm,),
        in_specs=[pl.BlockSpec((tm, d), row), pl.BlockSpec((tm, 1), row), _resident(g.shape), _resident(b.shape),
                  pl.BlockSpec((tm, d), row)],
        out_specs=[vec, pl.BlockSpec((tm, d), row), vec, vec],
        out_shape=[_sds((1, d), F32), _sds((t, d), F32), _sds((1, d), F32), _sds((1, d), F32)],
        compiler_params=_cp(), name="loss_ln_bwd")(xhat, rstd, g, b, target)


def _mm_nt(pairs, ws, name, *, tm=ROW_TILE, res=None, ln=None, out_dtype=F32, after=None):
    t = pairs[0][0].shape[0]
    k = ws[0].shape[-2]
    tm = min(tm, t)
    n, nw = len(pairs), len(ws)

    def body(*refs):
        refs = refs[after is not None:]
        a_refs, w_refs = refs[:n], refs[n:n + nw]
        rest = list(refs[n + nw:])
        dx = None
        for a_ref, (_, wi, lo, hi) in zip(a_refs, pairs):
            w_ref = w_refs[wi]
            if len(w_ref.shape) == 3:
                nc = w_ref.shape[2]
                parts = [lax.dot_general(a_ref[:, j * nc:(j + 1) * nc].astype(BF16), w_ref[j], NT,
                                         preferred_element_type=F32) for j in range(w_ref.shape[0])]
            else:
                parts = [lax.dot_general(a_ref[...].astype(BF16), w_ref[:, lo:hi], NT, preferred_element_type=F32)]
            for part in parts:
                dx = part if dx is None else dx + part
        if res is not None:
            dx = dx + ALPHA * rest.pop(0)[...]
        if ln is None:
            rest[0][...] = dx.astype(out_dtype)
            return
        xh_ref, rs_ref, g_ref, dz_ref, dg_ref, db_ref = rest
        first = pl.program_id(0) == 0
        dz, dg, db = _ln_bwd(dx, xh_ref[...], rs_ref[...], g_ref[...])
        dz_ref[...] = dz
        _accumulate(dg_ref, first, dg)
        _accumulate(db_ref, first, db)

    row = lambda i: (i, 0)
    in_specs = [pl.BlockSpec((tm, a.shape[1]), row) for a, _, _, _ in pairs] + [_resident(w.shape) for w in ws]
    args = [a for a, _, _, _ in pairs] + list(ws)
    if res is not None:
        in_specs.append(pl.BlockSpec((tm, k), row))
        args.append(res)
    if ln is None:
        out_specs = pl.BlockSpec((tm, k), row)
        out_shape = _sds((t, k), out_dtype)
    else:
        xhat, rstd, g = ln
        in_specs += [pl.BlockSpec((tm, k), row), pl.BlockSpec((tm, 1), row), _resident(g.shape)]
        args += [xhat, rstd, g]
        vec = pl.BlockSpec((1, k), lambda i: (0, 0))
        out_specs = [pl.BlockSpec((tm, k), row), vec, vec]
        out_shape = [_sds((t, k), F32), _sds((1, k), F32), _sds((1, k), F32)]
    if after is not None:
        in_specs.insert(0, _ANY_SPEC)
        args.insert(0, after)
    return pl.pallas_call(body, grid=(t // tm,), in_specs=in_specs, out_specs=out_specs, out_shape=out_shape,
                          compiler_params=_cp(), name=name)(*args)


def _mm_tn(a, b, name, *, tn, tk=None, tt=None, stack_cols=False, out_dtype=BF16, after=None):
    t, k = a.shape
    n = b.shape[1]
    tk = k if tk is None else tk
    tt = min(REDUCE_TILE if tt is None else tt, t)
    nt = t // tt

    def body(a_ref, b_ref, *rest):
        o_ref, acc_ref = rest[after is not None:]
        s = pl.program_id(2)
        part = lax.dot_general(a_ref[...].astype(BF16), b_ref[...].astype(BF16), TN, preferred_element_type=F32)
        _accumulate(acc_ref, s == 0, part)

        @pl.when(s == nt - 1)
        def _():
            o_ref[...] = acc_ref[...].astype(out_dtype).reshape(o_ref.shape)

    if stack_cols:
        assert tk == k
        out_spec = pl.BlockSpec((1, k, tn), lambda kk, j, s: (j, 0, 0))
        out_shape = _sds((n // tn, k, tn), out_dtype)
    else:
        out_spec = pl.BlockSpec((tk, tn), lambda kk, j, s: (kk, j))
        out_shape = _sds((k, n), out_dtype)
    return pl.pallas_call(
        body, grid=(k // tk, n // tn, nt),
        in_specs=[pl.BlockSpec((tt, tk), lambda kk, j, s: (s, kk)), pl.BlockSpec((tt, tn), lambda kk, j, s: (s, j))]
        + ([_ANY_SPEC] if after is not None else []),
        out_specs=out_spec, out_shape=out_shape,
        scratch_shapes=[pltpu.VMEM((tk, tn), F32)],
        compiler_params=_cp(), name=name)(a, b, *([after] if after is not None else []))


def _ffn_bwd_hidden(dz, wo, gu, name):
    t, d = dz.shape
    tm = min(FFN_ROW_TILE, t)
    hh = HALF_HIDDEN

    def body(dz_ref, w_ref, gu_ref, o_ref):
        a = dz_ref[...].astype(BF16)
        for c in range(2):
            gs, us = slice(c * hh, (c + 1) * hh), slice(FFN_HIDDEN + c * hh, FFN_HIDDEN + (c + 1) * hh)
            dh = lax.dot_general(a, w_ref[gs, :], NT, preferred_element_type=F32)
            g = gu_ref[:, gs].astype(F32)
            u = gu_ref[:, us].astype(F32)
            sig = _sigmoid(g)
            o_ref[:, gs] = (dh * u * sig * (1.0 + g * (1.0 - sig))).astype(BF16)
            o_ref[:, us] = (dh * g * sig).astype(BF16)

    row = lambda i: (i, 0)
    return pl.pallas_call(
        body, grid=(t // tm,),
        in_specs=[pl.BlockSpec((tm, d), row), _resident(wo.shape), pl.BlockSpec((tm, 2 * FFN_HIDDEN), row)],
        out_specs=pl.BlockSpec((tm, 2 * FFN_HIDDEN), row),
        out_shape=_sds((t, 2 * FFN_HIDDEN), BF16), compiler_params=_cp(), name=name)(dz, wo, gu)


def _gmlp_bwd(dgated, a, vg, vb, wm, bs_col):
    t, d2 = a.shape
    d = d2 // 2
    tm = min(ROW_TILE, t)
    gb = GMLP_BLOCK

    def body(dg_ref, a_ref, vg_ref, vb_ref, wm_ref, bs_ref, da_ref, dws_ref, dbs_ref, dvg_ref, dvb_ref, dvln_sc):
        first = pl.program_id(0) == 0
        au, av = a_ref[:, :d], a_ref[:, d:]
        u = _gelu(au)
        vhat, rstd = _ln_fwd(_gelu(av))
        vln = (vhat * vg_ref[...] + vb_ref[...]).astype(BF16)
        dgate = dg_ref[...]

        @pl.when(first)
        def _():
            dws_ref[...] = jnp.zeros(dws_ref.shape, F32)
            dbs_ref[...] = jnp.zeros(dbs_ref.shape, F32)

        for blk in range(tm // gb):
            rs = slice(blk * gb, (blk + 1) * gb)
            for gi in range(GMLP_GROUPS):
                cs = slice(gi * gb, (gi + 1) * gb)
                vblk = vln[rs, cs]
                s = jnp.dot(wm_ref[gi], vblk, preferred_element_type=F32) + bs_ref[:, gi:gi + 1]
                dgb = dgate[rs, cs]
                da_ref[rs, cs] = (dgb * s * _gelu_grad(au[rs, cs])).astype(BF16)
                ds = dgb * u[rs, cs]
                dsb = ds.astype(BF16)
                dws_ref[gi] += lax.dot_general(dsb, vblk, NT, preferred_element_type=F32)
                dbs_ref[:, gi:gi + 1] += jnp.sum(ds, axis=1, keepdims=True)
                dvln_sc[rs, cs] = lax.dot_general(wm_ref[gi], dsb, TN, preferred_element_type=F32)
        dv, dvg, dvb = _ln_bwd(dvln_sc[...], vhat, rstd, vg_ref[...])
        da_ref[:, d:] = (dv * _gelu_grad(av)).astype(BF16)
        _accumulate(dvg_ref, first, dvg)
        _accumulate(dvb_ref, first, dvb)

    row = lambda i: (i, 0)
    vec = pl.BlockSpec((1, d), lambda i: (0, 0))
    return pl.pallas_call(
        body, grid=(t // tm,),
        in_specs=[pl.BlockSpec((tm, d), row), pl.BlockSpec((tm, d2), row), _resident(vg.shape), _resident(vb.shape),
                  _resident(wm.shape), _resident(bs_col.shape)],
        out_specs=[pl.BlockSpec((tm, d2), row), pl.BlockSpec(wm.shape, lambda i: (0, 0, 0)),
                   pl.BlockSpec(bs_col.shape, lambda i: (0, 0)), vec, vec],
        out_shape=[_sds((t, d2), BF16), _sds(wm.shape, F32), _sds(bs_col.shape, F32), _sds((1, d), F32), _sds((1, d), F32)],
        scratch_shapes=[pltpu.VMEM((tm, d), F32)],
        compiler_params=_cp(), name="gmlp_bwd")(dgated, a, vg, vb, wm, bs_col)


def _conv_bwd(bch, dmix, conv_w):
    t = bch.shape[0]
    tm = min(ROW_TILE, t)
    nb = t // tm
    halo_blocks = tm // SUBLANES
    cw = CONV_WIDTH

    def body(cur_ref, prev_ref, next_ref, dc_ref, dn_ref, w_ref, o_ref, dw_ref):
        i = pl.program_id(0)
        bgate, cgate, hval = cur_ref[:, :cw], cur_ref[:, cw:2 * cw], cur_ref[:, 2 * cw:]
        z = cgate * hval
        zp = jnp.where(i == 0, 0.0, prev_ref[:, cw:2 * cw] * prev_ref[:, 2 * cw:])
        z1, z2 = _shift_down(z, zp)
        w0, w1, w2 = w_ref[0:1, :], w_ref[1:2, :], w_ref[2:3, :]
        dconv = dc_ref[...]
        o_ref[:, :cw] = (dconv * (w0 * z2 + w1 * z1 + w2 * z)).astype(BF16)
        dy = dconv * bgate
        dyn = jnp.where(i == nb - 1, 0.0, dn_ref[...] * next_ref[:, :cw])
        dy1, dy2 = _shift_up(dy, dyn)
        dz = w2 * dy + w1 * dy1 + w0 * dy2
        o_ref[:, cw:2 * cw] = (dz * hval).astype(BF16)
        o_ref[:, 2 * cw:] = (dz * cgate).astype(BF16)

        @pl.when(i == 0)
        def _():
            dw_ref[...] = jnp.zeros(dw_ref.shape, F32)

        for tap, zs in enumerate((z2, z1, z)):
            dw_ref[tap:tap + 1, :] += jnp.sum(dy * zs, axis=0, keepdims=True)

    last_halo = t // SUBLANES - 1
    return pl.pallas_call(
        body, grid=(nb,),
        in_specs=[pl.BlockSpec((tm, BCH), lambda i: (i, 0)),
                  pl.BlockSpec((SUBLANES, BCH), lambda i: (jnp.maximum(i * halo_blocks - 1, 0), 0)),
                  pl.BlockSpec((SUBLANES, BCH), lambda i: (jnp.minimum((i + 1) * halo_blocks, last_halo), 0)),
                  pl.BlockSpec((tm, cw), lambda i: (i, 1)),
                  pl.BlockSpec((SUBLANES, cw), lambda i: (jnp.minimum((i + 1) * halo_blocks, last_halo), 1)),
                  _resident(conv_w.shape)],
        out_specs=[pl.BlockSpec((tm, BCH), lambda i: (i, 0)), pl.BlockSpec((SUBLANES, cw), lambda i: (0, 0))],
        out_shape=[_sds((t, BCH), BF16), _sds((SUBLANES, cw), F32)],
        compiler_params=_cp(), name="conv_bwd")(bch, bch, bch, dmix, dmix, conv_w)


def _attn_bwd_prep(o, dmix, qp, lse_col):
    t = o.shape[0]
    tm = min(ROW_TILE, t)
    hd = HEAD_DIM

    def body(o_ref, do_ref, qp_ref, lse_ref, qb_ref, dob_ref):
        lane = lax.broadcasted_iota(jnp.int32, (tm, hd), 1) + hd
        for h in range(FOX_HEADS):
            do = do_ref[:, h * hd:(h + 1) * hd]
            delta = jnp.sum(o_ref[:, h * hd:(h + 1) * hd].astype(F32) * do, axis=-1, keepdims=True)
            dob_ref[h, :, :hd] = do.astype(BF16)
            dob_ref[h, :, hd:] = _lane_pieces(lane, DO_DELTA, _split3(delta), -1.0).astype(BF16)
            qb_ref[h, :, :hd] = qp_ref[h, :, :hd]
            qb_ref[h, :, hd:] = (qp_ref[h, :, hd:].astype(F32)
                                 + _lane_pieces(lane, Q_LSE, _split3(lse_ref[:, h:h + 1]), -1.0)).astype(BF16)

    row3 = pl.BlockSpec((FOX_HEADS, tm, LANES), lambda i: (0, i, 0))
    return pl.pallas_call(
        body, grid=(t // tm,),
        in_specs=[pl.BlockSpec((tm, FOX_WIDTH), lambda i: (i, 0)), pl.BlockSpec((tm, FOX_WIDTH), lambda i: (i, 0)), row3,
                  pl.BlockSpec((tm, FOX_HEADS), lambda i: (i, 0))],
        out_specs=[row3, row3], out_shape=[_sds((FOX_HEADS, t, LANES), BF16)] * 2,
        compiler_params=_cp(), name="attn_bwd_prep")(o, dmix, qp, lse_col)


def _attn_bwd(qb, kp, vp, dob, kt):
    t = qb.shape[1]
    bq = min(ATT_BLOCK, t)
    nq = t // bq
    i_tab, j_tab = _triangle(nq, key_major=True)

    def body(it_ref, jt_ref, q_ref, k_ref, v_ref, do_ref, kt_ref, dqt_ref, dk_ref, dv_ref, dk_sc, dv_sc):
        s = pl.program_id(1)
        i, j = it_ref[s], jt_ref[s]

        @pl.when(s == 0)
        def _():
            dqt_ref[...] = jnp.zeros(dqt_ref.shape, F32)

        @pl.when(i == j)
        def _():
            dk_sc[...] = jnp.zeros(dk_sc.shape, F32)
            dv_sc[...] = jnp.zeros(dv_sc.shape, F32)

        cols = pl.ds(pl.multiple_of(i * bq, bq), bq)

        def sweep(masked):
            def scores(h):
                return (lax.dot_general(k_ref[h], q_ref[h], NT, preferred_element_type=F32),
                        lax.dot_general(v_ref[h], do_ref[h], NT, preferred_element_type=F32))

            def accumulate(h, ptb, dstb):
                dv_sc[h] += jnp.dot(ptb, do_ref[h], preferred_element_type=F32)
                dk_sc[h] += jnp.dot(dstb, q_ref[h], preferred_element_type=F32)
                dqt_ref[h, :, cols] += jnp.dot(kt_ref[h], dstb, preferred_element_type=F32)

            ahead, behind = scores(0), None
            for h in range(ATT_BWD_HEADS):
                st, dpt = ahead
                if h + 1 < ATT_BWD_HEADS:
                    ahead = scores(h + 1)
                if behind is not None:
                    accumulate(*behind)
                if masked:
                    key = lax.broadcasted_iota(jnp.int32, (bq, bq), 0)
                    qry = lax.broadcasted_iota(jnp.int32, (bq, bq), 1)
                    st = jnp.where(key <= qry, st, NEG)
                pt = jnp.exp(st)
                behind = (h, pt.astype(BF16), (pt * dpt).astype(BF16))
            accumulate(*behind)

        @pl.when(i == j)
        def _():
            sweep(True)

        @pl.when(i > j)
        def _():
            sweep(False)

        @pl.when(i == nq - 1)
        def _():
            dk_ref[...] = dk_sc[...]
            dv_ref[...] = dv_sc[...].astype(BF16)

    nh = ATT_BWD_HEADS
    qblk = pl.BlockSpec((nh, bq, LANES), lambda hp, s, it, jt: (hp, it[s], 0))
    kblk = pl.BlockSpec((nh, bq, LANES), lambda hp, s, it, jt: (hp, jt[s], 0))
    grid_spec = pltpu.PrefetchScalarGridSpec(
        num_scalar_prefetch=2, grid=(FOX_HEADS // nh, i_tab.shape[0]),
        in_specs=[qblk, kblk, kblk, qblk, pl.BlockSpec((nh, LANES, bq), lambda hp, s, it, jt: (hp, 0, jt[s]))],
        out_specs=[pl.BlockSpec((nh, LANES, t), lambda hp, s, it, jt: (hp, 0, 0), pipeline_mode=pl.Buffered(1)),
                   kblk, kblk],
        scratch_shapes=[pltpu.VMEM((nh, bq, LANES), F32), pltpu.VMEM((nh, bq, LANES), F32)])
    return pl.pallas_call(body, grid_spec=grid_spec,
                          out_shape=[_sds((FOX_HEADS, LANES, t), F32), _sds((FOX_HEADS, t, LANES), F32),
                                     _sds((FOX_HEADS, t, LANES), BF16)],
                          compiler_params=_cp(), name="attn_bwd")(i_tab, j_tab, qb, kp, vp, dob, kt)


def _attn_unpack(dqt, dkp, dvp):
    t = dkp.shape[1]
    tm = min(ROW_TILE, t)
    hd = HEAD_DIM

    def body(dqt_ref, dk_ref, dv_ref, o_ref, dc_ref):
        for h in range(FOX_HEADS):
            dq = dqt_ref[h].T
            o_ref[:, h * hd:(h + 1) * hd] = (dq[:, :hd] * (hd ** -0.5)).astype(BF16)
            o_ref[:, FOX_WIDTH + h * hd:FOX_WIDTH + (h + 1) * hd] = dk_ref[h, :, :hd].astype(BF16)
            o_ref[:, 2 * FOX_WIDTH + h * hd:2 * FOX_WIDTH + (h + 1) * hd] = dv_ref[h, :, :hd]
            dc_ref[:, h:h + 1] = dq[:, K_ONE:K_ONE + 1] - dk_ref[h, :, Q_ONE:Q_ONE + 1]

    row3 = pl.BlockSpec((FOX_HEADS, tm, LANES), lambda i: (0, i, 0))
    return pl.pallas_call(
        body, grid=(t // tm,),
        in_specs=[pl.BlockSpec((FOX_HEADS, LANES, tm), lambda i: (0, 0, i)), row3, row3],
        out_specs=[pl.BlockSpec((tm, QKV), lambda i: (i, 0)), pl.BlockSpec((tm, FOX_HEADS), lambda i: (i, 0))],
        out_shape=[_sds((t, QKV), BF16), _sds((t, FOX_HEADS), F32)],
        compiler_params=_cp(), name="attn_unpack")(dqt, dkp, dvp)


def _adamw(parts, w, m, v, name):
    nl, r, c = w.shape
    tr = r
    for cand in (256, 128, 64, 32, 16):
        if r > cand and r % cand == 0:
            tr = cand
            break
    npart = len(parts)
    bc1 = 1.0 - ADAM_B1 ** ADAM_STEP
    bc2 = 1.0 - ADAM_B2 ** ADAM_STEP

    def body(*refs):
        p_refs = refs[:npart]
        w_ref, m_ref, v_ref, g_ref, d_ref, nm_ref, nv_ref = refs[npart:]
        sums = []
        for p_ref in p_refs:
            acc = p_ref[0, 0].astype(F32)
            for s in range(1, p_ref.shape[0]):
                acc = acc + p_ref[s, 0].astype(F32)
            sums.append(acc)
        g = sums[0]
        for extra in sums[1:]:
            g = g + extra
        nm = ADAM_B1 * m_ref[0] + (1.0 - ADAM_B1) * g
        nv = ADAM_B2 * v_ref[0] + (1.0 - ADAM_B2) * (g * g)
        m_hat = nm / bc1
        v_hat = nv / bc2
        g_ref[0] = g
        d_ref[0] = -ADAM_LR * (m_hat / (jnp.sqrt(v_hat) + ADAM_EPS) + ADAM_WD * w_ref[0])
        nm_ref[0] = nm
        nv_ref[0] = nv

    blk = pl.BlockSpec((1, tr, c), lambda l, i: (l, i, 0))
    return pl.pallas_call(
        body, grid=(nl, r // tr),
        in_specs=[pl.BlockSpec((p.shape[0], 1, tr, c), lambda l, i: (0, l, i, 0)) for p in parts] + [blk, blk, blk],
        out_specs=[blk] * 4, out_shape=[_sds(w.shape, F32)] * 4,
        compiler_params=_cp(), name=name)(*parts, w, m, v)


def _to_rows(a):
    flat = a.reshape(-1)
    pad = (-flat.shape[0]) % LANES
    if pad:
        flat = jnp.concatenate([flat, jnp.zeros((pad,), flat.dtype)])
    return flat.reshape(-1, LANES)


def _by_owner_cols(dw):
    k, n = dw.shape
    return dw.reshape(k, N_CHIPS, n // N_CHIPS).transpose(1, 0, 2)[:, None]


def _ffn_fwd(xin, xin_b, wi, wo, g, b, layer):
    gu, h = _ffn_in(xin_b, wi, f"ffn_in_{layer}")
    y, y_b, xhat, rstd = _mm_res_ln([(h, wo)], xin, g, b, f"ffn_out_ln_{layer}")
    return (y, y_b), (xin_b, gu, h, xhat, rstd)


def _ffn_bwd(dz, saved, wi, wo, ln_below, layer):
    xin_b, gu, h, _, _ = saved
    dgu = _ffn_bwd_hidden(dz, wo, gu, f"ffn_bwd_hidden_{layer}")
    g_out = _mm_tn(h, dz, f"ffn_dw_out_{layer}", tn=D_MODEL, tk=HALF_HIDDEN, tt=REDUCE_TILE // 2)
    g_in = _mm_tn(xin_b, dgu, f"ffn_dw_in_{layer}", tn=HALF_HIDDEN, stack_cols=True)
    below = _mm_nt([(dgu, 0, 0, 0)], [wi], f"ffn_dx_{layer}", tm=FFN_ROW_TILE, res=dz, ln=ln_below)
    return below, g_in, g_out.reshape(N_CHIPS, FFN_HIDDEN // N_CHIPS, D_MODEL)


def kernel(x, even_w_in, even_b_f, even_conv_w, even_w_out, odd_w_in, odd_v_ln_g, odd_v_ln_b, odd_w_s, odd_b_s, odd_w_out, mix_ln_g, mix_ln_b, ffn_w_in, ffn_w_out, ffn_ln_g, ffn_ln_b, loss_target, m_even_w_in, m_even_b_f, m_even_conv_w, m_even_w_out, m_odd_w_in, m_odd_v_ln_g, m_odd_v_ln_b, m_odd_w_s, m_odd_b_s, m_odd_w_out, m_mix_ln_g, m_mix_ln_b, m_ffn_w_in, m_ffn_w_out, m_ffn_ln_g, m_ffn_ln_b, v_even_w_in, v_even_b_f, v_even_conv_w, v_even_w_out, v_odd_w_in, v_odd_v_ln_g, v_odd_v_ln_b, v_odd_w_s, v_odd_b_s, v_odd_w_out, v_mix_ln_g, v_mix_ln_b, v_ffn_w_in, v_ffn_w_out, v_ffn_ln_g, v_ffn_ln_b):
    t = x.shape[1]
    d = D_MODEL
    chip = 2 * lax.axis_index("x") + lax.axis_index("y")
    x2d = x[0]
    target = loss_target[0]

    small_shard = jnp.concatenate([odd_v_ln_g.reshape(2, LANES), odd_v_ln_b.reshape(2, LANES),
                                   even_conv_w.reshape(CONV_K, LANES), jnp.zeros((1, LANES), F32)], axis=0)
    first = [even_w_in[0].astype(BF16), even_w_out[0].astype(BF16), small_shard]
    later = [odd_w_in[0].astype(BF16), odd_w_out[0].astype(BF16), ffn_w_in[0].astype(BF16), ffn_w_in[1].astype(BF16),
             ffn_w_out[0].astype(BF16), ffn_w_out[1].astype(BF16)]
    first_h, first_tok = _split_start(first, "gather4", "gather_first_start")
    later_h, later_tok = _split_start(later, "gather4", "gather_later_start", after=first_tok)
    g_ewi, g_ewo, g_small = [_with_own(g, own) for g, own in
                             zip(_split_wait(first_h, "gather_first_wait", later_tok), first)]
    ewi = g_ewi.transpose(1, 0, 2).reshape(d, EVEN_IN)
    w_even_in = jnp.concatenate([ewi[:, :QKV], ewi[:, QKV + FOX_HEADS:], ewi[:, QKV:QKV + FOX_HEADS],
                                 jnp.zeros((d, LANES - FOX_HEADS), BF16)], axis=1)
    w_even_out = g_ewo.reshape(d, d)
    v_ln_g = g_small[:, 0:2].reshape(1, d)
    v_ln_b = g_small[:, 2:4].reshape(1, d)
    conv_w = g_small[:, 4:7].transpose(1, 0, 2).reshape(CONV_K, CONV_WIDTH)
    chunk_id = jnp.arange(GMLP_BLOCK) // CHUNK
    gmask = chunk_id[None, :] <= chunk_id[:, None]
    w_spatial = jnp.where(gmask[None], odd_w_s[0], 0.0).astype(BF16)
    bs_col = odd_b_s[0].T
    b_f_col = even_b_f.reshape(FOX_HEADS, 1)
    ln = lambda p, l: p[l:l + 1]

    qkv, bch, fl = _proj(x2d, w_even_in, [(0, QKV, BF16), (QKV, QKV + BCH, F32), (QKV + BCH, EVEN_IN_PAD, F32)], "even_proj")
    fl3 = fl[:, :FOX_HEADS].T.reshape(FOX_HEADS, t // LANES, LANES).transpose(1, 0, 2)
    c3 = _fgate_fwd(fl3, b_f_col)
    c_rows = c3.transpose(1, 0, 2).reshape(FOX_HEADS, t)
    qp, kp, vp, kt, vt = _attn_pack(qkv, c_rows.T)
    attn, lse = _attn_fwd(qp, kp, vt)
    conv = _conv_fwd(bch, conv_w)
    x1, x1_b, xh1, rs1 = _mm_res_ln([(attn, w_even_out[:FOX_WIDTH]), (conv, w_even_out[FOX_WIDTH:])], x2d,
                              ln(mix_ln_g, 0), ln(mix_ln_b, 0), "even_out_ln")
    w_odd_in, g_owo, w_fi0, w_fi1, g_fo0, g_fo1 = [_with_own(g, own) for g, own in
                                                   zip(_split_wait(later_h, "gather_later_wait", x1), later)]
    w_odd_out = g_owo.reshape(d, d)
    w_ffn_in = [w_fi0, w_fi1]
    w_ffn_out = [g_fo0.reshape(FFN_HIDDEN, d), g_fo1.reshape(FFN_HIDDEN, d)]
    (x2, x2_b), ffn0 = _ffn_fwd(x1, x1_b, w_ffn_in[0], w_ffn_out[0], ln(ffn_ln_g, 0), ln(ffn_ln_b, 0), 0)

    a_odd, gated = _gmlp_fwd(x2_b, w_odd_in, v_ln_g, v_ln_b, w_spatial, bs_col)
    x3, x3_b, xh3, rs3 = _mm_res_ln([(gated, w_odd_out)], x2, ln(mix_ln_g, 1), ln(mix_ln_b, 1), "odd_out_ln")
    _, ffn1 = _ffn_fwd(x3, x3_b, w_ffn_in[1], w_ffn_out[1], ln(ffn_ln_g, 1), ln(ffn_ln_b, 1), 1)

    sq, dz4, d_fg1, d_fb1 = _loss_ln_bwd(ffn1[3], ffn1[4], ln(ffn_ln_g, 1), ln(ffn_ln_b, 1), target)
    loss = lax.psum(0.5 / d * jnp.sum(sq), ("x", "y", "c"))
    (dz3, d_mg1, d_mb1), gi_f1, go_f1 = _ffn_bwd(dz4, ffn1, w_ffn_in[1], w_ffn_out[1], (xh3, rs3, ln(mix_ln_g, 1)), 1)

    dgated = _mm_nt([(dz3, 0, 0, d)], [w_odd_out], "odd_dgated")
    go_odd = _mm_tn(gated, dz3, "odd_dw_out", tn=d).reshape(N_CHIPS, 1, d // N_CHIPS, d)
    da_odd, dws, dbs_col, d_vg, d_vb = _gmlp_bwd(dgated, a_odd, v_ln_g, v_ln_b, w_spatial, bs_col)
    gi_odd = _mm_tn(x2_b, da_odd, "odd_dw_in", tn=d // 2, stack_cols=True)[:, None]
    dz2, d_fg0, d_fb0 = _mm_nt([(da_odd, 0, 0, 0)], [w_odd_in], "odd_dx", res=dz3,
                               ln=(ffn0[3], ffn0[4], ln(ffn_ln_g, 0)))
    (dz1, d_mg0, d_mb0), gi_f0, go_f0 = _ffn_bwd(dz2, ffn0, w_ffn_in[0], w_ffn_out[0], (xh1, rs1, ln(mix_ln_g, 0)), 0)

    sent_early = [gi_odd, go_odd, jnp.stack([gi_f0, gi_f1], axis=1), jnp.stack([go_f0, go_f1], axis=1)]
    early_h, early_tok = _split_start(sent_early, "scatter4", "scatter_early_start")
    dmix = _mm_nt([(dz1, 0, 0, d)], [w_even_out], "even_dmix", after=early_tok)
    mix =jnp.concatenate([attn, conv], axis=1)
    go_even = _mm_tn(mix, dz1, "even_dw_out", tn=d).reshape(N_CHIPS, 1, d // N_CHIPS, d)
    dbch, dconv_w8 = _conv_bwd(bch, dmix, conv_w)
    qb, dob = _attn_bwd_prep(attn, dmix, qp, lse.reshape(FOX_HEADS, t).T)
    dqkv, dc_col = _attn_unpack(*_attn_bwd(qb, kp, vp, dob, kt))
    dc3 = dc_col.T.reshape(FOX_HEADS, t // LANES, LANES).transpose(1, 0, 2)
    dfl3, d_bf = _fgate_bwd(dc3, fl3, b_f_col)
    dfl = jnp.concatenate([dfl3.transpose(1, 0, 2).reshape(FOX_HEADS, t).T.astype(BF16),
                           jnp.zeros((t, LANES - FOX_HEADS), BF16)], axis=1)

    dws_masked = jnp.where(gmask[None], dws, 0.0)
    rep_names = ["odd_w_s", "odd_b_s", "mix_ln_g", "mix_ln_b", "ffn_ln_g", "ffn_ln_b", "even_b_f"]
    rep_grads = [dws_masked, dbs_col.T, jnp.concatenate([d_mg0, d_mg1]), jnp.concatenate([d_mb0, d_mb1]),
                 jnp.concatenate([d_fg0, d_fg1]), jnp.concatenate([d_fb0, d_fb1]), d_bf.reshape(1, FOX_HEADS)]
    rep_w = [(odd_w_s, m_odd_w_s, v_odd_w_s), (odd_b_s, m_odd_b_s, v_odd_b_s), (mix_ln_g, m_mix_ln_g, v_mix_ln_g),
             (mix_ln_b, m_mix_ln_b, v_mix_ln_b), (ffn_ln_g, m_ffn_ln_g, v_ffn_ln_g), (ffn_ln_b, m_ffn_ln_b, v_ffn_ln_b),
             (even_b_f, m_even_b_f, v_even_b_f)]
    rep_rows = [_to_rows(gr) for gr in rep_grads]
    n_rep = sum(r.shape[0] for r in rep_rows)
    pad_rep = (-n_rep) % SUBLANES
    dconv_w = dconv_w8[:CONV_K].reshape(CONV_K, N_CHIPS, LANES).transpose(1, 0, 2).reshape(N_CHIPS * CONV_K, LANES)
    packed = jnp.concatenate(rep_rows + [jnp.zeros((pad_rep, LANES), F32), d_vg.reshape(SUBLANES, LANES),
                                         d_vb.reshape(SUBLANES, LANES), dconv_w, jnp.zeros((4, LANES), F32)], axis=0)
    small_h, small_tok = _split_start([packed], "gather8", "gather_small_start")

    chip_blk = lambda g: lax.dynamic_index_in_dim(g, chip, 0, keepdims=False)
    mine_early = [_with_own(r, chip_blk(g)) for r, g in
                  zip(_split_wait(early_h, "scatter_early_wait", small_tok), sent_early)]
    swap_h, swap_tok = _split_start(mine_early, "swap2", "swap_early_start")
    dw_qkv = _mm_tn(x2d, dqkv, "even_dw_qkv", tn=QKV // 2, out_dtype=F32, after=swap_tok)
    dw_bch = _mm_tn(x2d, dbch, "even_dw_bch", tn=BCH // 2, out_dtype=F32)
    dw_f = _mm_tn(x2d, dfl, "even_dw_f", tn=LANES, out_dtype=F32)
    gi_even = _by_owner_cols(jnp.concatenate([dw_qkv, dw_f[:, :FOX_HEADS], dw_bch], axis=1).astype(BF16))
    sent_late = [gi_even, go_even]
    late_h, late_tok = _split_start(sent_late, "scatter4", "scatter_late_start")
    grad_x = _mm_nt([(dqkv, 0, 0, QKV), (dbch, 0, QKV, QKV + BCH), (dfl, 0, QKV + BCH, EVEN_IN_PAD)], [w_even_in],
                    "even_dx", res=dz1, after=late_tok)
    mine_late = [_with_own(r, chip_blk(g)) for r, g in zip(_split_wait(late_h, "scatter_late_wait", grad_x), sent_late)]
    theirs_late = _exchange(mine_late, "swap2", "swap_late")
    theirs_early = _split_wait(swap_h, "swap_early_wait", theirs_late[0])
    (gathered,) = _split_wait(small_h, "gather_small_wait", theirs_early[0])
    gathered = lax.dynamic_update_index_in_dim(gathered, packed, 4 * lax.axis_index("x") + 2 * lax.axis_index("y")
                                               + lax.axis_index("c"), 0)
    mine, theirs = mine_late + mine_early, theirs_late + theirs_early
    big_w = [(even_w_in, m_even_w_in, v_even_w_in), (even_w_out, m_even_w_out, v_even_w_out),
             (odd_w_in, m_odd_w_in, v_odd_w_in), (odd_w_out, m_odd_w_out, v_odd_w_out),
             (ffn_w_in, m_ffn_w_in, v_ffn_w_in), (ffn_w_out, m_ffn_w_out, v_ffn_w_out)]
    big_names = ["even_w_in", "even_w_out", "odd_w_in", "odd_w_out", "ffn_w_in", "ffn_w_out"]
    res = {}
    for nm, own, sib, (w, m, v) in zip(big_names, mine, theirs, big_w):
        res[nm] = _adamw([own, sib], w, m, v, f"adamw_{nm}")

    base = n_rep + pad_rep
    own_rows = jnp.concatenate([
        lax.dynamic_slice_in_dim(gathered, base + 2 * chip, 2, axis=1),
        lax.dynamic_slice_in_dim(gathered, base + SUBLANES + 2 * chip, 2, axis=1),
        lax.dynamic_slice_in_dim(gathered, base + 2 * SUBLANES + CONV_K * chip, CONV_K, axis=1),
        jnp.zeros((N_DEV, 1, LANES), F32)], axis=1)
    small_parts = jnp.concatenate([gathered[:, :base], own_rows], axis=1)[:, None]

    def pack_small(get):
        rows = [_to_rows(get(tw)) for tw in rep_w] + [jnp.zeros((pad_rep, LANES), F32)]
        rows += [get(sh).reshape(-1, LANES) for sh in ((odd_v_ln_g, m_odd_v_ln_g, v_odd_v_ln_g),
                                                       (odd_v_ln_b, m_odd_v_ln_b, v_odd_v_ln_b),
                                                       (even_conv_w, m_even_conv_w, v_even_conv_w))]
        return jnp.concatenate(rows + [jnp.zeros((1, LANES), F32)], axis=0)[None]

    small_out = _adamw([small_parts], pack_small(lambda tw: tw[0]), pack_small(lambda tw: tw[1]),
                       pack_small(lambda tw: tw[2]), "adamw_small")

    def unpack_small(rows3):
        rows = rows3[0]
        out, off = {}, 0
        for nm, (w, _, _), r in zip(rep_names, rep_w, rep_rows):
            out[nm] = rows[off:off + r.shape[0]].reshape(-1)[:w.size].reshape(w.shape)
            off += r.shape[0]
        off += pad_rep
        out["odd_v_ln_g"] = rows[off:off + 2].reshape(odd_v_ln_g.shape)
        out["odd_v_ln_b"] = rows[off + 2:off + 4].reshape(odd_v_ln_b.shape)
        out["even_conv_w"] = rows[off + 4:off + 4 + CONV_K].reshape(even_conv_w.shape)
        return out

    small = [unpack_small(o) for o in small_out]
    order = ["even_w_in", "even_b_f", "even_conv_w", "even_w_out", "odd_w_in", "odd_v_ln_g", "odd_v_ln_b", "odd_w_s",
             "odd_b_s", "odd_w_out", "mix_ln_g", "mix_ln_b", "ffn_w_in", "ffn_w_out", "ffn_ln_g", "ffn_ln_b"]
    outs = [loss, grad_x[None]]
    for kind in range(4):
        for nm in order:
            outs.append(res[nm][kind] if nm in res else small[kind][nm])
    return tuple(outs)
```

```python
import functools
import math

import jax
import jax.numpy as jnp
from jax import lax
from jax.experimental import pallas as pl
from jax.experimental.pallas import tpu as pltpu

F32 = jnp.float32
BF16 = jnp.bfloat16

D_MODEL = 1024
FOX_HEADS = 8
HEAD_DIM = 64
HEAD_PAIRS = FOX_HEADS // 2
FOX_WIDTH = FOX_HEADS * HEAD_DIM
CONV_WIDTH = 512
CONV_K = 3
QKV = 3 * FOX_WIDTH
BCH = 3 * CONV_WIDTH
EVEN_IN = QKV + FOX_HEADS + BCH
EVEN_IN_PAD = QKV + BCH + 128
GMLP_BLOCK = 128
GMLP_GROUPS = 8
CHUNK = 64
FFN_HIDDEN = 2816
HALF_HIDDEN = FFN_HIDDEN // 2
ALPHA = 4.0 ** 0.25
LN_EPS = 1e-5
ADAM_LR = 0.001
ADAM_B1 = 0.9
ADAM_B2 = 0.999
ADAM_EPS = 1e-08
ADAM_WD = 0.01
ADAM_STEP = 10
N_CHIPS = 4
N_DEV = 8
LANES = 128
SUBLANES = 8
ROW_TILE = 512
FFN_ROW_TILE = 512
REDUCE_TILE = 2048
ATT_BLOCK = 512
ATT_FWD_HEADS = 4
ATT_BWD_HEADS = 4
VMEM_LIMIT = 56 * 2 ** 20
NEG = -1e30
MESH = pl.DeviceIdType.MESH
HIGHEST = lax.Precision.HIGHEST
Q_C, Q_ONE, Q_LSE = 64, 67, 70
K_ONE, K_C, K_ONE2 = 64, 67, 70
V_ONE = 64
DO_DELTA = 65
NT = (((1,), (1,)), ((), ()))
TN = (((0,), (0,)), ((), ()))


def _cp():
    return pltpu.CompilerParams(vmem_limit_bytes=VMEM_LIMIT)


def _resident(shape):
    zeros = (0,) * len(shape)
    return pl.BlockSpec(shape, lambda *_: zeros, pipeline_mode=pl.Buffered(1))


def _sds(shape, dtype):
    return jax.ShapeDtypeStruct(tuple(shape), dtype)


_MASKS = {
    "gather4": [(1, 0, 0), (0, 1, 0), (1, 1, 0)],
    "scatter4": [(1, 0, 0), (0, 1, 0), (1, 1, 0)],
    "swap2": [(0, 0, 1)],
    "gather8": [(0, 0, 1), (0, 1, 0), (0, 1, 1), (1, 0, 0), (1, 0, 1), (1, 1, 0), (1, 1, 1)],
}


def _exchange(arrs, mode, name):
    n = len(arrs)
    masks = _MASKS[mode]
    npeer = len(masks)
    lead = {"gather4": N_CHIPS, "gather8": N_DEV}.get(mode)
    out_shapes = [_sds(((lead,) if lead else ()) + a.shape, a.dtype) for a in arrs]

    def body(*refs):
        ins, outs = refs[:n], refs[n:2 * n]
        send_sems, recv_sems, loc_sems = refs[2 * n:]
        x, y, c = lax.axis_index("x"), lax.axis_index("y"), lax.axis_index("c")
        chip, dev = 2 * x + y, 4 * x + 2 * y + c
        sends, recvs, locs = [], [], []
        for k in range(n):
            if mode == "gather4":
                locs.append(pltpu.make_async_copy(ins[k], outs[k].at[chip], loc_sems.at[k]))
            elif mode == "scatter4":
                locs.append(pltpu.make_async_copy(ins[k].at[chip], outs[k].at[chip], loc_sems.at[k]))
            elif mode == "gather8":
                locs.append(pltpu.make_async_copy(ins[k], outs[k].at[dev], loc_sems.at[k]))
        for cp in locs:
            cp.start()
        for k in range(n):
            for j, (dx, dy, dc) in enumerate(masks):
                px = 1 - x if dx else x
                py = 1 - y if dy else y
                pc = 1 - c if dc else c
                pchip, pdev = 2 * px + py, 4 * px + 2 * py + pc
                if mode == "gather4":
                    src, dst, land = ins[k], outs[k].at[chip], outs[k].at[pchip]
                elif mode == "scatter4":
                    src, dst, land = ins[k].at[pchip], outs[k].at[chip], outs[k].at[pchip]
                elif mode == "swap2":
                    src, dst, land = ins[k], outs[k], outs[k]
                else:
                    src, dst, land = ins[k], outs[k].at[dev], outs[k].at[pdev]
                s = k * npeer + j
                kw = dict(send_sem=send_sems.at[s], recv_sem=recv_sems.at[s], device_id=(px, py, pc),
                          device_id_type=MESH)
                cp = pltpu.make_async_remote_copy(src_ref=src, dst_ref=dst, **kw)
                cp.start()
                sends.append(cp)
                recvs.append(pltpu.make_async_remote_copy(src_ref=src, dst_ref=land, **kw))
        for cp in recvs:
            cp.wait_recv()
        for cp in sends:
            cp.wait_send()
        for cp in locs:
            cp.wait()

    any_spec = pl.BlockSpec(memory_space=pl.ANY)
    outs = pl.pallas_call(
        body,
        out_shape=out_shapes,
        in_specs=[any_spec] * n,
        out_specs=[any_spec] * n,
        scratch_shapes=[pltpu.SemaphoreType.DMA((n * npeer,)), pltpu.SemaphoreType.DMA((n * npeer,)),
                        pltpu.SemaphoreType.DMA((max(n, 1),))],
        name=name,
    )(*arrs)
    return list(outs)


_HBM_SPEC = pl.BlockSpec(memory_space=pltpu.HBM)
_SEM_SPEC = pl.BlockSpec(memory_space=pltpu.SEMAPHORE)
_ANY_SPEC = pl.BlockSpec(memory_space=pl.ANY)
_EFFECT = pltpu.SideEffectType.DATAFLOW_SIDE_EFFECTING


def _split_copies(mode, ins, lands, send_sems, recv_sems):
    x, y, c = lax.axis_index("x"), lax.axis_index("y"), lax.axis_index("c")
    chip, dev = 2 * x + y, 4 * x + 2 * y + c
    masks = _MASKS[mode]
    out = []
    for k in range(len(ins)):
        for j, (dx, dy, dc) in enumerate(masks):
            px = 1 - x if dx else x
            py = 1 - y if dy else y
            pc = 1 - c if dc else c
            pchip, pdev = 2 * px + py, 4 * px + 2 * py + pc
            if mode == "gather4":
                src, dst, land = ins[k], lands[k].at[chip], lands[k].at[pchip]
            elif mode == "scatter4":
                src, dst, land = ins[k].at[pchip], lands[k].at[chip], lands[k].at[pchip]
            elif mode == "swap2":
                src, dst, land = ins[k], lands[k], lands[k]
            else:
                src, dst, land = ins[k], lands[k].at[dev], lands[k].at[pdev]
            s = k * len(masks) + j
            kw = dict(send_sem=send_sems.at[s], recv_sem=recv_sems.at[s], device_id=(px, py, pc), device_id_type=MESH)
            out.append((pltpu.make_async_remote_copy(src_ref=src, dst_ref=dst, **kw),
                        pltpu.make_async_remote_copy(src_ref=src, dst_ref=land, **kw)))
    return out


def _split_start(arrs, mode, name, after=None):
    n = len(arrs)
    nsem = n * len(_MASKS[mode])
    lead = {"gather4": (N_CHIPS,), "gather8": (N_DEV,)}.get(mode, ())
    land_shapes = [lead + a.shape for a in arrs]

    def body(*refs):
        ins, lands = refs[:n], refs[n:2 * n]
        outs = refs[2 * n + (after is not None):]
        for start, _ in _split_copies(mode, ins, lands, outs[0], outs[1]):
            start.start()
        outs[-1][...] = jnp.zeros(outs[-1].shape, F32)

    srcs = [pltpu.with_memory_space_constraint(a, pltpu.HBM) for a in arrs]
    empties = [pltpu.with_memory_space_constraint(lax.empty(s, a.dtype), pltpu.HBM) for s, a in zip(land_shapes, arrs)]
    res = pl.pallas_call(
        body, name=name,
        out_shape=(pltpu.SemaphoreType.DMA((nsem,)), pltpu.SemaphoreType.DMA((nsem,)),
                   *[pltpu.HBM(a.shape, a.dtype) for a in arrs],
                   *[pltpu.HBM(s, a.dtype) for s, a in zip(land_shapes, arrs)],
                   _sds((SUBLANES, LANES), F32)),
        in_specs=[_HBM_SPEC] * (2 * n) + ([_ANY_SPEC] if after is not None else []),
        out_specs=(_SEM_SPEC, _SEM_SPEC, *[_HBM_SPEC] * (2 * n), pl.BlockSpec(memory_space=pltpu.VMEM)),
        input_output_aliases={k: 2 + k for k in range(2 * n)},
        compiler_params=pltpu.CompilerParams(has_side_effects=_EFFECT),
    )(*srcs, *empties, *([after] if after is not None else []))
    return dict(mode=mode, n=n, sems=res[:2], bufs=res[2:2 + 2 * n]), res[-1]


def _split_wait(handle, name, after):
    n, mode = handle["n"], handle["mode"]

    def body(*refs):
        ins, lands = refs[:n], refs[n:2 * n]
        send_sems, recv_sems = refs[2 * n], refs[2 * n + 1]
        for _, arrival in _split_copies(mode, ins, lands, send_sems, recv_sems):
            arrival.wait_send()
            arrival.wait_recv()

    bufs = handle["bufs"]
    res = pl.pallas_call(
        body, name=name,
        out_shape=tuple(pltpu.HBM(b.shape, b.dtype) for b in bufs),
        in_specs=[_HBM_SPEC] * (2 * n) + [_SEM_SPEC, _SEM_SPEC, _ANY_SPEC],
        out_specs=tuple([_HBM_SPEC] * (2 * n)),
        input_output_aliases={k: k for k in range(2 * n)},
        compiler_params=pltpu.CompilerParams(has_side_effects=_EFFECT),
    )(*bufs, *handle["sems"], after)
    return list(res[n:])


def _with_own(landed, own):
    chip = 2 * lax.axis_index("x") + lax.axis_index("y")
    return lax.dynamic_update_index_in_dim(landed, own, chip, 0)


def _sigmoid(x):
    return 0.5 * jnp.tanh(0.5 * x) + 0.5


def _log_sigmoid(x):
    e = jnp.exp(-jnp.abs(x))
    log1p = jnp.where(e < 1e-2, e * (1.0 - e * (0.5 - e * (1.0 / 3.0))), jnp.log(1.0 + e))
    return jnp.minimum(x, 0.0) - log1p


def _gelu(a):
    return 0.5 * a * (1.0 + lax.erf(a * (2.0 ** -0.5)))


def _gelu_grad(a):
    cdf = 0.5 * (1.0 + lax.erf(a * (2.0 ** -0.5)))
    pdf = jnp.exp(-0.5 * a * a) * (1.0 / math.sqrt(2.0 * math.pi))
    return cdf + a * pdf


def _ln_fwd(z):
    mu = jnp.mean(z, axis=-1, keepdims=True)
    zc = z - mu
    var = jnp.mean(zc * zc, axis=-1, keepdims=True)
    rstd = lax.rsqrt(var + LN_EPS)
    return zc * rstd, rstd


def _ln_bwd(dy, xhat, rstd, g):
    dxh = dy * g
    m1 = jnp.mean(dxh, axis=-1, keepdims=True)
    m2 = jnp.mean(dxh * xhat, axis=-1, keepdims=True)
    dz = rstd * (dxh - m1 - xhat * m2)
    return dz, jnp.sum(dy * xhat, axis=0, keepdims=True), jnp.sum(dy, axis=0, keepdims=True)


def _shift_down(z, halo):
    r = lax.broadcasted_iota(jnp.int32, z.shape, 0)
    z1 = jnp.where(r == 0, halo[7:8, :], pltpu.roll(z, 1, 0))
    z2 = jnp.where(r == 0, halo[6:7, :], jnp.where(r == 1, halo[7:8, :], pltpu.roll(z, 2, 0)))
    return z1, z2


def _shift_up(z, halo):
    n = z.shape[0]
    r = lax.broadcasted_iota(jnp.int32, z.shape, 0)
    z1 = jnp.where(r == n - 1, halo[0:1, :], pltpu.roll(z, n - 1, 0))
    z2 = jnp.where(r == n - 1, halo[1:2, :], jnp.where(r == n - 2, halo[0:1, :], pltpu.roll(z, n - 2, 0)))
    return z1, z2


def _accumulate(ref, first, value):
    @pl.when(first)
    def _():
        ref[...] = value

    @pl.when(jnp.logical_not(first))
    def _():
        ref[...] += value


def _proj(x, w, splits, name):
    t, k = x.shape
    tm = min(ROW_TILE, t)

    def body(x_ref, w_ref, *outs):
        a = x_ref[...].astype(BF16)
        for (lo, hi, dt), o in zip(splits, outs):
            o[...] = jnp.dot(a, w_ref[:, lo:hi], preferred_element_type=F32).astype(dt)

    return pl.pallas_call(
        body, grid=(t // tm,),
        in_specs=[pl.BlockSpec((tm, k), lambda i: (i, 0)), _resident(w.shape)],
        out_specs=[pl.BlockSpec((tm, hi - lo), lambda i: (i, 0)) for lo, hi, _ in splits],
        out_shape=[_sds((t, hi - lo), dt) for lo, hi, dt in splits],
        compiler_params=_cp(), name=name)(x, w)


def _fgate_fwd(fl3, b_f):
    nc = fl3.shape[0]

    def body(f_ref, b_ref, c_ref):
        r = lax.broadcasted_iota(jnp.int32, (LANES, LANES), 0)
        cidx = lax.broadcasted_iota(jnp.int32, (LANES, LANES), 1)
        upper = (r <= cidx).astype(F32)

        def step(i, carry):
            lf = _log_sigmoid(f_ref[i] + b_ref[...])
            cc = jnp.dot(lf, upper, precision=HIGHEST, preferred_element_type=F32) + carry
            c_ref[i] = cc
            return cc[:, LANES - 1:LANES]

        lax.fori_loop(0, nc, step, jnp.zeros((FOX_HEADS, 1), F32))

    return pl.pallas_call(body, out_shape=_sds(fl3.shape, F32), name="fgate_fwd")(fl3, b_f)


def _fgate_bwd(dc3, fl3, b_f):
    nc = fl3.shape[0]

    def body(dc_ref, f_ref, b_ref, df_ref, db_ref):
        r = lax.broadcasted_iota(jnp.int32, (LANES, LANES), 0)
        cidx = lax.broadcasted_iota(jnp.int32, (LANES, LANES), 1)
        lower = (r >= cidx).astype(F32)

        def step(n, carry):
            suffix, db = carry
            i = nc - 1 - n
            dlf = jnp.dot(dc_ref[i], lower, precision=HIGHEST, preferred_element_type=F32) + suffix
            df = dlf * (1.0 - _sigmoid(f_ref[i] + b_ref[...]))
            df_ref[i] = df
            return dlf[:, 0:1], db + jnp.sum(df, axis=1, keepdims=True)

        zero = jnp.zeros((FOX_HEADS, 1), F32)
        _, db = lax.fori_loop(0, nc, step, (zero, zero))
        db_ref[...] = db

    return pl.pallas_call(body, out_shape=[_sds(fl3.shape, F32), _sds((FOX_HEADS, 1), F32)],
                          name="fgate_bwd")(dc3, fl3, b_f)


def _split3(c):
    hi = c.astype(BF16).astype(F32)
    mid = (c - hi).astype(BF16).astype(F32)
    lo = (c - hi - mid).astype(BF16).astype(F32)
    return hi, mid, lo


def _lane_pieces(lane, start, pieces, sign):
    out = jnp.zeros(lane.shape, F32)
    for n, p in enumerate(pieces):
        out = jnp.where(lane == start + n, sign * p, out)
    return out


def _attn_pack(qkv, c_col):
    t = qkv.shape[0]
    tm = min(ROW_TILE, t)
    hd = HEAD_DIM

    def body(x_ref, c_ref, qp_ref, kp_ref, vp_ref, kt_ref, vt_ref):
        lane = lax.broadcasted_iota(jnp.int32, (tm, hd), 1) + hd
        for h in range(FOX_HEADS):
            pieces = _split3(c_ref[:, h:h + 1])
            ones = lambda a, b: jnp.where(jnp.logical_and(lane >= a, lane < b), 1.0, 0.0)
            q_extra = _lane_pieces(lane, Q_C, pieces, 1.0) + ones(Q_ONE, Q_ONE + 3)
            k_extra = _lane_pieces(lane, K_C, pieces, -1.0) + ones(K_ONE, K_ONE + 3) + ones(K_ONE2, K_ONE2 + 3)
            qp_ref[h, :, :hd] = (x_ref[:, h * hd:(h + 1) * hd].astype(F32) * (hd ** -0.5)).astype(BF16)
            qp_ref[h, :, hd:] = q_extra.astype(BF16)
            kp_ref[h, :, :hd] = x_ref[:, FOX_WIDTH + h * hd:FOX_WIDTH + (h + 1) * hd]
            kp_ref[h, :, hd:] = k_extra.astype(BF16)
            vp_ref[h, :, :hd] = x_ref[:, 2 * FOX_WIDTH + h * hd:2 * FOX_WIDTH + (h + 1) * hd]
            vp_ref[h, :, hd:] = ones(V_ONE, V_ONE + 4).astype(BF16)
            kt_ref[h] = kp_ref[h].astype(F32).T.astype(BF16)
            vt_ref[h] = vp_ref[h].astype(F32).T.astype(BF16)

    row3 = pl.BlockSpec((FOX_HEADS, tm, LANES), lambda i: (0, i, 0))
    col3 = pl.BlockSpec((FOX_HEADS, LANES, tm), lambda i: (0, 0, i))
    return pl.pallas_call(
        body, grid=(t // tm,),
        in_specs=[pl.BlockSpec((tm, QKV), lambda i: (i, 0)), pl.BlockSpec((tm, FOX_HEADS), lambda i: (i, 0))],
        out_specs=[row3, row3, row3, col3, col3],
        out_shape=[_sds((FOX_HEADS, t, LANES), BF16)] * 3 + [_sds((FOX_HEADS, LANES, t), BF16)] * 2,
        compiler_params=_cp(), name="attn_pack")(qkv, c_col)


def _triangle(nq, key_major):
    if key_major:
        pairs = [(i, j) for j in range(nq) for i in range(j, nq)]
    else:
        pairs = [(i, j) for i in range(nq) for j in range(i + 1)]
    return jnp.asarray([p[0] for p in pairs], jnp.int32), jnp.asarray([p[1] for p in pairs], jnp.int32)


def _attn_fwd(qp, kp, vt):
    t = qp.shape[1]
    bq = min(ATT_BLOCK, t)
    nq = t // bq
    nh = ATT_FWD_HEADS
    i_tab, j_tab = _triangle(nq, key_major=False)

    def body(it_ref, jt_ref, q_ref, k_ref, vt_ref, o_ref, lse_ref, m_sc, acc_sc):
        s = pl.program_id(1)
        i, j = it_ref[s], jt_ref[s]

        @pl.when(j == 0)
        def _():
            m_sc[...] = jnp.full(m_sc.shape, NEG, F32)
            acc_sc[...] = jnp.zeros(acc_sc.shape, F32)

        def sweep(masked):
            scores = lambda h: lax.dot_general(k_ref[h], q_ref[h], NT, preferred_element_type=F32)

            def accumulate(h, pt, rescale):
                acc_sc[h] = rescale * acc_sc[h] + jnp.dot(vt_ref[h], pt, preferred_element_type=F32)

            ahead, behind = scores(0), None
            for h in range(nh):
                st = ahead
                if h + 1 < nh:
                    ahead = scores(h + 1)
                if behind is not None:
                    accumulate(*behind)
                if masked:
                    key = lax.broadcasted_iota(jnp.int32, (bq, bq), 0)
                    qry = lax.broadcasted_iota(jnp.int32, (bq, bq), 1)
                    st = jnp.where(key <= qry, st, NEG)
                m_prev = m_sc[h]
                m_new = jnp.maximum(m_prev, jnp.max(st, axis=0, keepdims=True))
                behind = (h, jnp.exp(st - m_new).astype(BF16), jnp.exp(m_prev - m_new))
                m_sc[h] = m_new
            accumulate(*behind)

        @pl.when(j < i)
        def _():
            sweep(False)

        @pl.when(j == i)
        def _():
            sweep(True)
            for h in range(nh):
                acc = acc_sc[h]
                denom = acc[V_ONE:V_ONE + 1, :]
                o_ref[:, h * HEAD_DIM:(h + 1) * HEAD_DIM] = (acc[:HEAD_DIM, :] / denom).T.astype(BF16)
                lse_ref[h] = m_sc[h] + jnp.log(denom)

    grid_spec = pltpu.PrefetchScalarGridSpec(
        num_scalar_prefetch=2, grid=(FOX_HEADS // nh, i_tab.shape[0]),
        in_specs=[pl.BlockSpec((nh, bq, LANES), lambda hp, s, it, jt: (hp, it[s], 0)),
                  pl.BlockSpec((nh, bq, LANES), lambda hp, s, it, jt: (hp, jt[s], 0)),
                  pl.BlockSpec((nh, LANES, bq), lambda hp, s, it, jt: (hp, 0, jt[s]))],
        out_specs=[pl.BlockSpec((bq, nh * HEAD_DIM), lambda hp, s, it, jt: (it[s], hp)),
                   pl.BlockSpec((nh, 1, bq), lambda hp, s, it, jt: (hp, 0, it[s]))],
        scratch_shapes=[pltpu.VMEM((nh, 1, bq), F32), pltpu.VMEM((nh, LANES, bq), F32)])
    return pl.pallas_call(body, grid_spec=grid_spec,
                          out_shape=[_sds((t, FOX_WIDTH), BF16), _sds((FOX_HEADS, 1, t), F32)],
                          compiler_params=_cp(), name="attn_fwd")(i_tab, j_tab, qp, kp, vt)


def _conv_fwd(bch, conv_w):
    t = bch.shape[0]
    tm = min(ROW_TILE, t)
    halo_blocks = tm // SUBLANES
    cw = CONV_WIDTH

    def body(cur_ref, prev_ref, w_ref, o_ref):
        i = pl.program_id(0)
        z = cur_ref[:, cw:2 * cw] * cur_ref[:, 2 * cw:]
        zp = jnp.where(i == 0, 0.0, prev_ref[:, cw:2 * cw] * prev_ref[:, 2 * cw:])
        z1, z2 = _shift_down(z, zp)
        y = w_ref[0:1, :] * z2 + w_ref[1:2, :] * z1 + w_ref[2:3, :] * z
        o_ref[...] = (cur_ref[:, :cw] * y).astype(BF16)

    return pl.pallas_call(
        body, grid=(t // tm,),
        in_specs=[pl.BlockSpec((tm, BCH), lambda i: (i, 0)),
                  pl.BlockSpec((SUBLANES, BCH), lambda i: (jnp.maximum(i * halo_blocks - 1, 0), 0)),
                  _resident(conv_w.shape)],
        out_specs=pl.BlockSpec((tm, cw), lambda i: (i, 0)),
        out_shape=_sds((t, cw), BF16), compiler_params=_cp(), name="conv_fwd")(bch, bch, conv_w)


def _mm_res_ln(pairs, res, g, b, name):
    t, d = res.shape
    tm = min(ROW_TILE, t)
    n = len(pairs)

    def body(*refs):
        a_refs, w_refs = refs[:n], refs[n:2 * n]
        res_ref, g_ref, b_ref, y_ref, yb_ref, xh_ref, rs_ref = refs[2 * n:]
        z = ALPHA * res_ref[...]
        for a_ref, w_ref in zip(a_refs, w_refs):
            z = z + jnp.dot(a_ref[...].astype(BF16), w_ref[...], preferred_element_type=F32)
        xhat, rstd = _ln_fwd(z)
        y = xhat * g_ref[...] + b_ref[...]
        y_ref[...] = y
        yb_ref[...] = y.astype(BF16)
        xh_ref[...] = xhat
        rs_ref[...] = rstd

    row = lambda i: (i, 0)
    full = pl.BlockSpec((tm, d), row)
    return pl.pallas_call(
        body, grid=(t // tm,),
        in_specs=[pl.BlockSpec((tm, a.shape[1]), row) for a, _ in pairs] + [_resident(w.shape) for _, w in pairs]
        + [full, _resident(g.shape), _resident(b.shape)],
        out_specs=[full, full, full, pl.BlockSpec((tm, 1), row)],
        out_shape=[_sds((t, d), F32), _sds((t, d), BF16), _sds((t, d), F32), _sds((t, 1), F32)],
        compiler_params=_cp(), name=name)(*[a for a, _ in pairs], *[w for _, w in pairs], res, g, b)


def _ffn_in(x, wi, name):
    t, d = x.shape
    tm = min(FFN_ROW_TILE, t)
    hh = HALF_HIDDEN

    def body(x_ref, w_ref, gu_ref, h_ref):
        a = x_ref[...].astype(BF16)
        for c in range(2):
            gs, us = slice(c * hh, (c + 1) * hh), slice(FFN_HIDDEN + c * hh, FFN_HIDDEN + (c + 1) * hh)
            g = jnp.dot(a, w_ref[c], preferred_element_type=F32)
            u = jnp.dot(a, w_ref[2 + c], preferred_element_type=F32)
            gu_ref[:, gs] = g.astype(BF16)
            gu_ref[:, us] = u.astype(BF16)
            h_ref[:, gs] = (g * _sigmoid(g) * u).astype(BF16)

    row = lambda i: (i, 0)
    return pl.pallas_call(
        body, grid=(t // tm,),
        in_specs=[pl.BlockSpec((tm, d), row), _resident(wi.shape)],
        out_specs=[pl.BlockSpec((tm, 2 * FFN_HIDDEN), row), pl.BlockSpec((tm, FFN_HIDDEN), row)],
        out_shape=[_sds((t, 2 * FFN_HIDDEN), BF16), _sds((t, FFN_HIDDEN), BF16)],
        compiler_params=_cp(), name=name)(x, wi)


def _gmlp_fwd(x, w_in, vg, vb, wm, bs_col):
    t, d = x.shape
    tm = min(ROW_TILE, t)
    gb = GMLP_BLOCK

    def body(x_ref, w_ref, vg_ref, vb_ref, wm_ref, bs_ref, a_ref, o_ref):
        xb = x_ref[...].astype(BF16)
        nc = w_ref.shape[2]
        for j in range(w_ref.shape[0]):
            a_ref[:, j * nc:(j + 1) * nc] = jnp.dot(xb, w_ref[j], preferred_element_type=F32)
        u = _gelu(a_ref[:, :d])
        vhat, _ = _ln_fwd(_gelu(a_ref[:, d:]))
        vln = (vhat * vg_ref[...] + vb_ref[...]).astype(BF16)
        for blk in range(tm // gb):
            rs = slice(blk * gb, (blk + 1) * gb)
            for gi in range(GMLP_GROUPS):
                cs = slice(gi * gb, (gi + 1) * gb)
                s = jnp.dot(wm_ref[gi], vln[rs, cs], preferred_element_type=F32) + bs_ref[:, gi:gi + 1]
                o_ref[rs, cs] = (u[rs, cs] * s).astype(BF16)

    row = lambda i: (i, 0)
    return pl.pallas_call(
        body, grid=(t // tm,),
        in_specs=[pl.BlockSpec((tm, d), row), _resident(w_in.shape), _resident(vg.shape), _resident(vb.shape),
                  _resident(wm.shape), _resident(bs_col.shape)],
        out_specs=[pl.BlockSpec((tm, 2 * d), row), pl.BlockSpec((tm, d), row)],
        out_shape=[_sds((t, 2 * d), F32), _sds((t, d), BF16)],
        compiler_params=_cp(), name="gmlp_fwd")(x, w_in, vg, vb, wm, bs_col)


def _loss_ln_bwd(xhat, rstd, g, b, target):
    t, d = xhat.shape
    tm = min(ROW_TILE, t)

    def body(xh_ref, rs_ref, g_ref, b_ref, t_ref, sq_ref, dz_ref, dg_ref, db_ref):
        first = pl.program_id(0) == 0
        xh = xh_ref[...]
        err = xh * g_ref[...] + b_ref[...] - t_ref[...]
        dz, dg, db = _ln_bwd(err * (1.0 / d), xh, rs_ref[...], g_ref[...])
        dz_ref[...] = dz
        _accumulate(sq_ref, first, jnp.sum(err * err, axis=0, keepdims=True))
        _accumulate(dg_ref, first, dg)
        _accumulate(db_ref, first, db)

    row = lambda i: (i, 0)
    vec = pl.BlockSpec((1, d), lambda i: (0, 0))
    return pl.pallas_call(
        body, grid=(t // tm,),
        in_specs=[pl.BlockSpec((tm, d), row), pl.BlockSpec((tm, 1), row), _resident(g.shape), _resident(b.shape),
                  pl.BlockSpec((tm, d), row)],
        out_specs=[vec, pl.BlockSpec((tm, d), row), vec, vec],
        out_shape=[_sds((1, d), F32), _sds((t, d), F32), _sds((1, d), F32), _sds((1, d), F32)],
        compiler_params=_cp(), name="loss_ln_bwd")(xhat, rstd, g, b, target)


def _mm_nt(pairs, ws, name, *, tm=ROW_TILE, res=None, ln=None, out_dtype=F32, after=None):
    t = pairs[0][0].shape[0]
    k = ws[0].shape[-2]
    tm = min(tm, t)
    n, nw = len(pairs), len(ws)

    def body(*refs):
        refs = refs[after is not None:]
        a_refs, w_refs = refs[:n], refs[n:n + nw]
        rest = list(refs[n + nw:])
        dx = None
        for a_ref, (_, wi, lo, hi) in zip(a_refs, pairs):
            w_ref = w_refs[wi]
            if len(w_ref.shape) == 3:
                nc = w_ref.shape[2]
                parts = [lax.dot_general(a_ref[:, j * nc:(j + 1) * nc].astype(BF16), w_ref[j], NT,
                                         preferred_element_type=F32) for j in range(w_ref.shape[0])]
            else:
                parts = [lax.dot_general(a_ref[...].astype(BF16), w_ref[:, lo:hi], NT, preferred_element_type=F32)]
            for part in parts:
                dx = part if dx is None else dx + part
        if res is not None:
            dx = dx + ALPHA * rest.pop(0)[...]
        if ln is None:
            rest[0][...] = dx.astype(out_dtype)
            return
        xh_ref, rs_ref, g_ref, dz_ref, dg_ref, db_ref = rest
        first = pl.program_id(0) == 0
        dz, dg, db = _ln_bwd(dx, xh_ref[...], rs_ref[...], g_ref[...])
        dz_ref[...] = dz
        _accumulate(dg_ref, first, dg)
        _accumulate(db_ref, first, db)

    row = lambda i: (i, 0)
    in_specs = [pl.BlockSpec((tm, a.shape[1]), row) for a, _, _, _ in pairs] + [_resident(w.shape) for w in ws]
    args = [a for a, _, _, _ in pairs] + list(ws)
    if res is not None:
        in_specs.append(pl.BlockSpec((tm, k), row))
        args.append(res)
    if ln is None:
        out_specs = pl.BlockSpec((tm, k), row)
        out_shape = _sds((t, k), out_dtype)
    else:
        xhat, rstd, g = ln
        in_specs += [pl.BlockSpec((tm, k), row), pl.BlockSpec((tm, 1), row), _resident(g.shape)]
        args += [xhat, rstd, g]
        vec = pl.BlockSpec((1, k), lambda i: (0, 0))
        out_specs = [pl.BlockSpec((tm, k), row), vec, vec]
        out_shape = [_sds((t, k), F32), _sds((1, k), F32), _sds((1, k), F32)]
    if after is not None:
        in_specs.insert(0, _ANY_SPEC)
        args.insert(0, after)
    return pl.pallas_call(body, grid=(t // tm,), in_specs=in_specs, out_specs=out_specs, out_shape=out_shape,
                          compiler_params=_cp(), name=name)(*args)


def _mm_tn(a, b, name, *, tn, tk=None, tt=None, stack_cols=False, out_dtype=BF16, after=None):
    t, k = a.shape
    n = b.shape[1]
    tk = k if tk is None else tk
    tt = min(REDUCE_TILE if tt is None else tt, t)
    nt = t // tt

    def body(a_ref, b_ref, *rest):
        o_ref, acc_ref = rest[after is not None:]
        s = pl.program_id(2)
        part = lax.dot_general(a_ref[...].astype(BF16), b_ref[...].astype(BF16), TN, preferred_element_type=F32)
        _accumulate(acc_ref, s == 0, part)

        @pl.when(s == nt - 1)
        def _():
            o_ref[...] = acc_ref[...].astype(out_dtype).reshape(o_ref.shape)

    if stack_cols:
        assert tk == k
        out_spec = pl.BlockSpec((1, k, tn), lambda kk, j, s: (j, 0, 0))
        out_shape = _sds((n // tn, k, tn), out_dtype)
    else:
        out_spec = pl.BlockSpec((tk, tn), lambda kk, j, s: (kk, j))
        out_shape = _sds((k, n), out_dtype)
    return pl.pallas_call(
        body, grid=(k // tk, n // tn, nt),
        in_specs=[pl.BlockSpec((tt, tk), lambda kk, j, s: (s, kk)), pl.BlockSpec((tt, tn), lambda kk, j, s: (s, j))]
        + ([_ANY_SPEC] if after is not None else []),
        out_specs=out_spec, out_shape=out_shape,
        scratch_shapes=[pltpu.VMEM((tk, tn), F32)],
        compiler_params=_cp(), name=name)(a, b, *([after] if after is not None else []))


def _ffn_bwd_hidden(dz, wo, gu, name):
    t, d = dz.shape
    tm = min(FFN_ROW_TILE, t)
    hh = HALF_HIDDEN

    def body(dz_ref, w_ref, gu_ref, o_ref):
        a = dz_ref[...].astype(BF16)
        for c in range(2):
            gs, us = slice(c * hh, (c + 1) * hh), slice(FFN_HIDDEN + c * hh, FFN_HIDDEN + (c + 1) * hh)
            dh = lax.dot_general(a, w_ref[gs, :], NT, preferred_element_type=F32)
            g = gu_ref[:, gs].astype(F32)
            u = gu_ref[:, us].astype(F32)
            sig = _sigmoid(g)
            o_ref[:, gs] = (dh * u * sig * (1.0 + g * (1.0 - sig))).astype(BF16)
            o_ref[:, us] = (dh * g * sig).astype(BF16)

    row = lambda i: (i, 0)
    return pl.pallas_call(
        body, grid=(t // tm,),
        in_specs=[pl.BlockSpec((tm, d), row), _resident(wo.shape), pl.BlockSpec((tm, 2 * FFN_HIDDEN), row)],
        out_specs=pl.BlockSpec((tm, 2 * FFN_HIDDEN), row),
        out_shape=_sds((t, 2 * FFN_HIDDEN), BF16), compiler_params=_cp(), name=name)(dz, wo, gu)


def _gmlp_bwd(dgated, a, vg, vb, wm, bs_col):
    t, d2 = a.shape
    d = d2 // 2
    tm = min(ROW_TILE, t)
    gb = GMLP_BLOCK

    def body(dg_ref, a_ref, vg_ref, vb_ref, wm_ref, bs_ref, da_ref, dws_ref, dbs_ref, dvg_ref, dvb_ref, dvln_sc):
        first = pl.program_id(0) == 0
        au, av = a_ref[:, :d], a_ref[:, d:]
        u = _gelu(au)
        vhat, rstd = _ln_fwd(_gelu(av))
        vln = (vhat * vg_ref[...] + vb_ref[...]).astype(BF16)
        dgate = dg_ref[...]

        @pl.when(first)
        def _():
            dws_ref[...] = jnp.zeros(dws_ref.shape, F32)
            dbs_ref[...] = jnp.zeros(dbs_ref.shape, F32)

        for blk in range(tm // gb):
            rs = slice(blk * gb, (blk + 1) * gb)
            for gi in range(GMLP_GROUPS):
                cs = slice(gi * gb, (gi + 1) * gb)
                vblk = vln[rs, cs]
                s = jnp.dot(wm_ref[gi], vblk, preferred_element_type=F32) + bs_ref[:, gi:gi + 1]
                dgb = dgate[rs, cs]
                da_ref[rs, cs] = (dgb * s * _gelu_grad(au[rs, cs])).astype(BF16)
                ds = dgb * u[rs, cs]
                dsb = ds.astype(BF16)
                dws_ref[gi] += lax.dot_general(dsb, vblk, NT, preferred_element_type=F32)
                dbs_ref[:, gi:gi + 1] += jnp.sum(ds, axis=1, keepdims=True)
                dvln_sc[rs, cs] = lax.dot_general(wm_ref[gi], dsb, TN, preferred_element_type=F32)
        dv, dvg, dvb = _ln_bwd(dvln_sc[...], vhat, rstd, vg_ref[...])
        da_ref[:, d:] = (dv * _gelu_grad(av)).astype(BF16)
        _accumulate(dvg_ref, first, dvg)
        _accumulate(dvb_ref, first, dvb)

    row = lambda i: (i, 0)
    vec = pl.BlockSpec((1, d), lambda i: (0, 0))
    return pl.pallas_call(
        body, grid=(t // tm,),
        in_specs=[pl.BlockSpec((tm, d), row), pl.BlockSpec((tm, d2), row), _resident(vg.shape), _resident(vb.shape),
                  _resident(wm.shape), _resident(bs_col.shape)],
        out_specs=[pl.BlockSpec((tm, d2), row), pl.BlockSpec(wm.shape, lambda i: (0, 0, 0)),
                   pl.BlockSpec(bs_col.shape, lambda i: (0, 0)), vec, vec],
        out_shape=[_sds((t, d2), BF16), _sds(wm.shape, F32), _sds(bs_col.shape, F32), _sds((1, d), F32), _sds((1, d), F32)],
        scratch_shapes=[pltpu.VMEM((tm, d), F32)],
        compiler_params=_cp(), name="gmlp_bwd")(dgated, a, vg, vb, wm, bs_col)


def _conv_bwd(bch, dmix, conv_w):
    t = bch.shape[0]
    tm = min(ROW_TILE, t)
    nb = t // tm
    halo_blocks = tm // SUBLANES
    cw = CONV_WIDTH

    def body(cur_ref, prev_ref, next_ref, dc_ref, dn_ref, w_ref, o_ref, dw_ref):
        i = pl.program_id(0)
        bgate, cgate, hval = cur_ref[:, :cw], cur_ref[:, cw:2 * cw], cur_ref[:, 2 * cw:]
        z = cgate * hval
        zp = jnp.where(i == 0, 0.0, prev_ref[:, cw:2 * cw] * prev_ref[:, 2 * cw:])
        z1, z2 = _shift_down(z, zp)
        w0, w1, w2 = w_ref[0:1, :], w_ref[1:2, :], w_ref[2:3, :]
        dconv = dc_ref[...]
        o_ref[:, :cw] = (dconv * (w0 * z2 + w1 * z1 + w2 * z)).astype(BF16)
        dy = dconv * bgate
        dyn = jnp.where(i == nb - 1, 0.0, dn_ref[...] * next_ref[:, :cw])
        dy1, dy2 = _shift_up(dy, dyn)
        dz = w2 * dy + w1 * dy1 + w0 * dy2
        o_ref[:, cw:2 * cw] = (dz * hval).astype(BF16)
        o_ref[:, 2 * cw:] = (dz * cgate).astype(BF16)

        @pl.when(i == 0)
        def _():
            dw_ref[...] = jnp.zeros(dw_ref.shape, F32)

        for tap, zs in enumerate((z2, z1, z)):
            dw_ref[tap:tap + 1, :] += jnp.sum(dy * zs, axis=0, keepdims=True)

    last_halo = t // SUBLANES - 1
    return pl.pallas_call(
        body, grid=(nb,),
        in_specs=[pl.BlockSpec((tm, BCH), lambda i: (i, 0)),
                  pl.BlockSpec((SUBLANES, BCH), lambda i: (jnp.maximum(i * halo_blocks - 1, 0), 0)),
                  pl.BlockSpec((SUBLANES, BCH), lambda i: (jnp.minimum((i + 1) * halo_blocks, last_halo), 0)),
                  pl.BlockSpec((tm, cw), lambda i: (i, 1)),
                  pl.BlockSpec((SUBLANES, cw), lambda i: (jnp.minimum((i + 1) * halo_blocks, last_halo), 1)),
                  _resident(conv_w.shape)],
        out_specs=[pl.BlockSpec((tm, BCH), lambda i: (i, 0)), pl.BlockSpec((SUBLANES, cw), lambda i: (0, 0))],
        out_shape=[_sds((t, BCH), BF16), _sds((SUBLANES, cw), F32)],
        compiler_params=_cp(), name="conv_bwd")(bch, bch, bch, dmix, dmix, conv_w)


def _attn_bwd_prep(o, dmix, qp, lse_col):
    t = o.shape[0]
    tm = min(ROW_TILE, t)
    hd = HEAD_DIM

    def body(o_ref, do_ref, qp_ref, lse_ref, qb_ref, dob_ref):
        lane = lax.broadcasted_iota(jnp.int32, (tm, hd), 1) + hd
        for h in range(FOX_HEADS):
            do = do_ref[:, h * hd:(h + 1) * hd]
            delta = jnp.sum(o_ref[:, h * hd:(h + 1) * hd].astype(F32) * do, axis=-1, keepdims=True)
            dob_ref[h, :, :hd] = do.astype(BF16)
            dob_ref[h, :, hd:] = _lane_pieces(lane, DO_DELTA, _split3(delta), -1.0).astype(BF16)
            qb_ref[h, :, :hd] = qp_ref[h, :, :hd]
            qb_ref[h, :, hd:] = (qp_ref[h, :, hd:].astype(F32)
                                 + _lane_pieces(lane, Q_LSE, _split3(lse_ref[:, h:h + 1]), -1.0)).astype(BF16)

    row3 = pl.BlockSpec((FOX_HEADS, tm, LANES), lambda i: (0, i, 0))
    return pl.pallas_call(
        body, grid=(t // tm,),
        in_specs=[pl.BlockSpec((tm, FOX_WIDTH), lambda i: (i, 0)), pl.BlockSpec((tm, FOX_WIDTH), lambda i: (i, 0)), row3,
                  pl.BlockSpec((tm, FOX_HEADS), lambda i: (i, 0))],
        out_specs=[row3, row3], out_shape=[_sds((FOX_HEADS, t, LANES), BF16)] * 2,
        compiler_params=_cp(), name="attn_bwd_prep")(o, dmix, qp, lse_col)


def _attn_bwd(qb, kp, vp, dob, kt):
    t = qb.shape[1]
    bq = min(ATT_BLOCK, t)
    nq = t // bq
    i_tab, j_tab = _triangle(nq, key_major=True)

    def body(it_ref, jt_ref, q_ref, k_ref, v_ref, do_ref, kt_ref, dqt_ref, dk_ref, dv_ref, dk_sc, dv_sc):
        s = pl.program_id(1)
        i, j = it_ref[s], jt_ref[s]

        @pl.when(s == 0)
        def _():
            dqt_ref[...] = jnp.zeros(dqt_ref.shape, F32)

        @pl.when(i == j)
        def _():
            dk_sc[...] = jnp.zeros(dk_sc.shape, F32)
            dv_sc[...] = jnp.zeros(dv_sc.shape, F32)

        cols = pl.ds(pl.multiple_of(i * bq, bq), bq)

        def sweep(masked):
            def scores(h):
                return (lax.dot_general(k_ref[h], q_ref[h], NT, preferred_element_type=F32),
                        lax.dot_general(v_ref[h], do_ref[h], NT, preferred_element_type=F32))

            def accumulate(h, ptb, dstb):
                dv_sc[h] += jnp.dot(ptb, do_ref[h], preferred_element_type=F32)
                dk_sc[h] += jnp.dot(dstb, q_ref[h], preferred_element_type=F32)
                dqt_ref[h, :, cols] += jnp.dot(kt_ref[h], dstb, preferred_element_type=F32)

            ahead, behind = scores(0), None
            for h in range(ATT_BWD_HEADS):
                st, dpt = ahead
                if h + 1 < ATT_BWD_HEADS:
                    ahead = scores(h + 1)
                if behind is not None:
                    accumulate(*behind)
                if masked:
                    key = lax.broadcasted_iota(jnp.int32, (bq, bq), 0)
                    qry = lax.broadcasted_iota(jnp.int32, (bq, bq), 1)
                    st = jnp.where(key <= qry, st, NEG)
                pt = jnp.exp(st)
                behind = (h, pt.astype(BF16), (pt * dpt).astype(BF16))
            accumulate(*behind)

        @pl.when(i == j)
        def _():
            sweep(True)

        @pl.when(i > j)
        def _():
            sweep(False)

        @pl.when(i == nq - 1)
        def _():
            dk_ref[...] = dk_sc[...]
            dv_ref[...] = dv_sc[...].astype(BF16)

    nh = ATT_BWD_HEADS
    qblk = pl.BlockSpec((nh, bq, LANES), lambda hp, s, it, jt: (hp, it[s], 0))
    kblk = pl.BlockSpec((nh, bq, LANES), lambda hp, s, it, jt: (hp, jt[s], 0))
    grid_spec = pltpu.PrefetchScalarGridSpec(
        num_scalar_prefetch=2, grid=(FOX_HEADS // nh, i_tab.shape[0]),
        in_specs=[qblk, kblk, kblk, qblk, pl.BlockSpec((nh, LANES, bq), lambda hp, s, it, jt: (hp, 0, jt[s]))],
        out_specs=[pl.BlockSpec((nh, LANES, t), lambda hp, s, it, jt: (hp, 0, 0), pipeline_mode=pl.Buffered(1)),
                   kblk, kblk],
        scratch_shapes=[pltpu.VMEM((nh, bq, LANES), F32), pltpu.VMEM((nh, bq, LANES), F32)])
    return pl.pallas_call(body, grid_spec=grid_spec,
                          out_shape=[_sds((FOX_HEADS, LANES, t), F32), _sds((FOX_HEADS, t, LANES), F32),
                                     _sds((FOX_HEADS, t, LANES), BF16)],
                          compiler_params=_cp(), name="attn_bwd")(i_tab, j_tab, qb, kp, vp, dob, kt)


def _attn_unpack(dqt, dkp, dvp):
    t = dkp.shape[1]
    tm = min(ROW_TILE, t)
    hd = HEAD_DIM

    def body(dqt_ref, dk_ref, dv_ref, o_ref, dc_ref):
        for h in range(FOX_HEADS):
            dq = dqt_ref[h].T
            o_ref[:, h * hd:(h + 1) * hd] = (dq[:, :hd] * (hd ** -0.5)).astype(BF16)
            o_ref[:, FOX_WIDTH + h * hd:FOX_WIDTH + (h + 1) * hd] = dk_ref[h, :, :hd].astype(BF16)
            o_ref[:, 2 * FOX_WIDTH + h * hd:2 * FOX_WIDTH + (h + 1) * hd] = dv_ref[h, :, :hd]
            dc_ref[:, h:h + 1] = dq[:, K_ONE:K_ONE + 1] - dk_ref[h, :, Q_ONE:Q_ONE + 1]

    row3 = pl.BlockSpec((FOX_HEADS, tm, LANES), lambda i: (0, i, 0))
    return pl.pallas_call(
        body, grid=(t // tm,),
        in_specs=[pl.BlockSpec((FOX_HEADS, LANES, tm), lambda i: (0, 0, i)), row3, row3],
        out_specs=[pl.BlockSpec((tm, QKV), lambda i: (i, 0)), pl.BlockSpec((tm, FOX_HEADS), lambda i: (i, 0))],
        out_shape=[_sds((t, QKV), BF16), _sds((t, FOX_HEADS), F32)],
        compiler_params=_cp(), name="attn_unpack")(dqt, dkp, dvp)


def _adamw(parts, w, m, v, name, layer=None, into=None):
    nl, r, c = w.shape
    tr = r
    for cand in (256, 128, 64, 32, 16):
        if r > cand and r % cand == 0:
            tr = cand
            break
    npart = len(parts)
    bc1 = 1.0 - ADAM_B1 ** ADAM_STEP
    bc2 = 1.0 - ADAM_B2 ** ADAM_STEP

    def body(*refs):
        p_refs = refs[:npart]
        w_ref, m_ref, v_ref = refs[npart:npart + 3]
        g_ref, d_ref, nm_ref, nv_ref = refs[-4:]
        sums = []
        for p_ref in p_refs:
            acc = p_ref[0, 0].astype(F32)
            for s in range(1, p_ref.shape[0]):
                acc = acc + p_ref[s, 0].astype(F32)
            sums.append(acc)
        g = sums[0]
        for extra in sums[1:]:
            g = g + extra
        nm = ADAM_B1 * m_ref[0] + (1.0 - ADAM_B1) * g
        nv = ADAM_B2 * v_ref[0] + (1.0 - ADAM_B2) * (g * g)
        m_hat = nm / bc1
        v_hat = nv / bc2
        g_ref[0] = g
        d_ref[0] = -ADAM_LR * (m_hat / (jnp.sqrt(v_hat) + ADAM_EPS) + ADAM_WD * w_ref[0])
        nm_ref[0] = nm
        nv_ref[0] = nv

    first = 0 if layer is None else layer
    blk = pl.BlockSpec((1, tr, c), lambda l, i: (first + l, i, 0))
    extra = [] if into is None else list(into)
    return pl.pallas_call(
        body, grid=(nl if layer is None else 1, r // tr),
        in_specs=[pl.BlockSpec((p.shape[0], 1, tr, c), lambda l, i: (0, l, i, 0)) for p in parts] + [blk, blk, blk]
        + [_ANY_SPEC] * len(extra),
        out_specs=[blk] * 4, out_shape=[_sds(w.shape, F32)] * 4,
        input_output_aliases={npart + 3 + k: k for k in range(len(extra))},
        compiler_params=_cp(), name=name)(*parts, w, m, v, *extra)


def _to_rows(a):
    flat = a.reshape(-1)
    pad = (-flat.shape[0]) % LANES
    if pad:
        flat = jnp.concatenate([flat, jnp.zeros((pad,), flat.dtype)])
    return flat.reshape(-1, LANES)


def _by_owner_cols(dw):
    k, n = dw.shape
    return dw.reshape(k, N_CHIPS, n // N_CHIPS).transpose(1, 0, 2)[:, None]


def _ffn_fwd(xin, xin_b, wi, wo, g, b, layer):
    gu, h = _ffn_in(xin_b, wi, f"ffn_in_{layer}")
    y, y_b, xhat, rstd = _mm_res_ln([(h, wo)], xin, g, b, f"ffn_out_ln_{layer}")
    return (y, y_b), (xin_b, gu, h, xhat, rstd)


def _ffn_bwd(dz, saved, wi, wo, ln_below, layer):
    xin_b, gu, h, _, _ = saved
    dgu = _ffn_bwd_hidden(dz, wo, gu, f"ffn_bwd_hidden_{layer}")
    g_out = _mm_tn(h, dz, f"ffn_dw_out_{layer}", tn=D_MODEL, tk=HALF_HIDDEN, tt=REDUCE_TILE // 2)
    g_in = _mm_tn(xin_b, dgu, f"ffn_dw_in_{layer}", tn=HALF_HIDDEN, stack_cols=True)
    below = _mm_nt([(dgu, 0, 0, 0)], [wi], f"ffn_dx_{layer}", tm=FFN_ROW_TILE, res=dz, ln=ln_below)
    return below, g_in, g_out.reshape(N_CHIPS, FFN_HIDDEN // N_CHIPS, D_MODEL)


def kernel(x, even_w_in, even_b_f, even_conv_w, even_w_out, odd_w_in, odd_v_ln_g, odd_v_ln_b, odd_w_s, odd_b_s, odd_w_out, mix_ln_g, mix_ln_b, ffn_w_in, ffn_w_out, ffn_ln_g, ffn_ln_b, loss_target, m_even_w_in, m_even_b_f, m_even_conv_w, m_even_w_out, m_odd_w_in, m_odd_v_ln_g, m_odd_v_ln_b, m_odd_w_s, m_odd_b_s, m_odd_w_out, m_mix_ln_g, m_mix_ln_b, m_ffn_w_in, m_ffn_w_out, m_ffn_ln_g, m_ffn_ln_b, v_even_w_in, v_even_b_f, v_even_conv_w, v_even_w_out, v_odd_w_in, v_odd_v_ln_g, v_odd_v_ln_b, v_odd_w_s, v_odd_b_s, v_odd_w_out, v_mix_ln_g, v_mix_ln_b, v_ffn_w_in, v_ffn_w_out, v_ffn_ln_g, v_ffn_ln_b):
    t = x.shape[1]
    d = D_MODEL
    chip = 2 * lax.axis_index("x") + lax.axis_index("y")
    x2d = x[0]
    target = loss_target[0]

    small_shard = jnp.concatenate([odd_v_ln_g.reshape(2, LANES), odd_v_ln_b.reshape(2, LANES),
                                   even_conv_w.reshape(CONV_K, LANES), jnp.zeros((1, LANES), F32)], axis=0)
    first = [even_w_in[0].astype(BF16), even_w_out[0].astype(BF16), small_shard]
    later = [odd_w_in[0].astype(BF16), odd_w_out[0].astype(BF16), ffn_w_in[0].astype(BF16), ffn_w_in[1].astype(BF16),
             ffn_w_out[0].astype(BF16), ffn_w_out[1].astype(BF16)]
    first_h, first_tok = _split_start(first, "gather4", "gather_first_start")
    later_h, later_tok = _split_start(later, "gather4", "gather_later_start", after=first_tok)
    g_ewi, g_ewo, g_small = [_with_own(g, own) for g, own in
                             zip(_split_wait(first_h, "gather_first_wait", later_tok), first)]
    ewi = g_ewi.transpose(1, 0, 2).reshape(d, EVEN_IN)
    w_even_in = jnp.concatenate([ewi[:, :QKV], ewi[:, QKV + FOX_HEADS:], ewi[:, QKV:QKV + FOX_HEADS],
                                 jnp.zeros((d, LANES - FOX_HEADS), BF16)], axis=1)
    w_even_out = g_ewo.reshape(d, d)
    v_ln_g = g_small[:, 0:2].reshape(1, d)
    v_ln_b = g_small[:, 2:4].reshape(1, d)
    conv_w = g_small[:, 4:7].transpose(1, 0, 2).reshape(CONV_K, CONV_WIDTH)
    chunk_id = jnp.arange(GMLP_BLOCK) // CHUNK
    gmask = chunk_id[None, :] <= chunk_id[:, None]
    w_spatial = jnp.where(gmask[None], odd_w_s[0], 0.0).astype(BF16)
    bs_col = odd_b_s[0].T
    b_f_col = even_b_f.reshape(FOX_HEADS, 1)
    ln = lambda p, l: p[l:l + 1]

    qkv, bch, fl = _proj(x2d, w_even_in, [(0, QKV, BF16), (QKV, QKV + BCH, F32), (QKV + BCH, EVEN_IN_PAD, F32)], "even_proj")
    fl3 = fl[:, :FOX_HEADS].T.reshape(FOX_HEADS, t // LANES, LANES).transpose(1, 0, 2)
    c3 = _fgate_fwd(fl3, b_f_col)
    c_rows = c3.transpose(1, 0, 2).reshape(FOX_HEADS, t)
    qp, kp, vp, kt, vt = _attn_pack(qkv, c_rows.T)
    attn, lse = _attn_fwd(qp, kp, vt)
    conv = _conv_fwd(bch, conv_w)
    x1, x1_b, xh1, rs1 = _mm_res_ln([(attn, w_even_out[:FOX_WIDTH]), (conv, w_even_out[FOX_WIDTH:])], x2d,
                              ln(mix_ln_g, 0), ln(mix_ln_b, 0), "even_out_ln")
    w_odd_in, g_owo, w_fi0, w_fi1, g_fo0, g_fo1 = [_with_own(g, own) for g, own in
                                                   zip(_split_wait(later_h, "gather_later_wait", x1), later)]
    w_odd_out = g_owo.reshape(d, d)
    w_ffn_in = [w_fi0, w_fi1]
    w_ffn_out = [g_fo0.reshape(FFN_HIDDEN, d), g_fo1.reshape(FFN_HIDDEN, d)]
    (x2, x2_b), ffn0 = _ffn_fwd(x1, x1_b, w_ffn_in[0], w_ffn_out[0], ln(ffn_ln_g, 0), ln(ffn_ln_b, 0), 0)

    a_odd, gated = _gmlp_fwd(x2_b, w_odd_in, v_ln_g, v_ln_b, w_spatial, bs_col)
    x3, x3_b, xh3, rs3 = _mm_res_ln([(gated, w_odd_out)], x2, ln(mix_ln_g, 1), ln(mix_ln_b, 1), "odd_out_ln")
    _, ffn1 = _ffn_fwd(x3, x3_b, w_ffn_in[1], w_ffn_out[1], ln(ffn_ln_g, 1), ln(ffn_ln_b, 1), 1)

    sq, dz4, d_fg1, d_fb1 = _loss_ln_bwd(ffn1[3], ffn1[4], ln(ffn_ln_g, 1), ln(ffn_ln_b, 1), target)
    loss = lax.psum(0.5 / d * jnp.sum(sq), ("x", "y", "c"))
    (dz3, d_mg1, d_mb1), gi_f1, go_f1 = _ffn_bwd(dz4, ffn1, w_ffn_in[1], w_ffn_out[1], (xh3, rs3, ln(mix_ln_g, 1)), 1)

    dgated = _mm_nt([(dz3, 0, 0, d)], [w_odd_out], "odd_dgated")
    go_odd = _mm_tn(gated, dz3, "odd_dw_out", tn=d).reshape(N_CHIPS, 1, d // N_CHIPS, d)
    da_odd, dws, dbs_col, d_vg, d_vb = _gmlp_bwd(dgated, a_odd, v_ln_g, v_ln_b, w_spatial, bs_col)
    gi_odd = _mm_tn(x2_b, da_odd, "odd_dw_in", tn=d // 2, stack_cols=True)[:, None]
    dz2, d_fg0, d_fb0 = _mm_nt([(da_odd, 0, 0, 0)], [w_odd_in], "odd_dx", res=dz3,
                               ln=(ffn0[3], ffn0[4], ln(ffn_ln_g, 0)))
    (dz1, d_mg0, d_mb0), gi_f0, go_f0 = _ffn_bwd(dz2, ffn0, w_ffn_in[0], w_ffn_out[0], (xh1, rs1, ln(mix_ln_g, 0)), 0)

    sent_early = [gi_odd, go_odd, gi_f0[:, None], gi_f1[:, None], go_f0[:, None], go_f1[:, None]]
    early_h, early_tok = _split_start(sent_early, "scatter4", "scatter_early_start")
    dmix = _mm_nt([(dz1, 0, 0, d)], [w_even_out], "even_dmix", after=early_tok)
    go_even = jnp.concatenate([_mm_tn(attn, dz1, "even_dw_out_attn", tn=d), _mm_tn(conv, dz1, "even_dw_out_conv", tn=d)],
                              axis=0).reshape(N_CHIPS, 1, d // N_CHIPS, d)
    dbch, dconv_w8 = _conv_bwd(bch, dmix, conv_w)
    qb, dob = _attn_bwd_prep(attn, dmix, qp, lse.reshape(FOX_HEADS, t).T)
    dqkv, dc_col = _attn_unpack(*_attn_bwd(qb, kp, vp, dob, kt))
    dc3 = dc_col.T.reshape(FOX_HEADS, t // LANES, LANES).transpose(1, 0, 2)
    dfl3, d_bf = _fgate_bwd(dc3, fl3, b_f_col)
    dfl = jnp.concatenate([dfl3.transpose(1, 0, 2).reshape(FOX_HEADS, t).T.astype(BF16),
                           jnp.zeros((t, LANES - FOX_HEADS), BF16)], axis=1)

    dws_masked = jnp.where(gmask[None], dws, 0.0)
    rep_names = ["odd_w_s", "odd_b_s", "mix_ln_g", "mix_ln_b", "ffn_ln_g", "ffn_ln_b", "even_b_f"]
    rep_grads = [dws_masked, dbs_col.T, jnp.concatenate([d_mg0, d_mg1]), jnp.concatenate([d_mb0, d_mb1]),
                 jnp.concatenate([d_fg0, d_fg1]), jnp.concatenate([d_fb0, d_fb1]), d_bf.reshape(1, FOX_HEADS)]
    rep_w = [(odd_w_s, m_odd_w_s, v_odd_w_s), (odd_b_s, m_odd_b_s, v_odd_b_s), (mix_ln_g, m_mix_ln_g, v_mix_ln_g),
             (mix_ln_b, m_mix_ln_b, v_mix_ln_b), (ffn_ln_g, m_ffn_ln_g, v_ffn_ln_g), (ffn_ln_b, m_ffn_ln_b, v_ffn_ln_b),
             (even_b_f, m_even_b_f, v_even_b_f)]
    rep_rows = [_to_rows(gr) for gr in rep_grads]
    n_rep = sum(r.shape[0] for r in rep_rows)
    pad_rep = (-n_rep) % SUBLANES
    dconv_w = dconv_w8[:CONV_K].reshape(CONV_K, N_CHIPS, LANES).transpose(1, 0, 2).reshape(N_CHIPS * CONV_K, LANES)
    packed = jnp.concatenate(rep_rows + [jnp.zeros((pad_rep, LANES), F32), d_vg.reshape(SUBLANES, LANES),
                                         d_vb.reshape(SUBLANES, LANES), dconv_w, jnp.zeros((4, LANES), F32)], axis=0)
    small_h, small_tok = _split_start([packed], "gather8", "gather_small_start")

    chip_blk = lambda g: lax.dynamic_index_in_dim(g, chip, 0, keepdims=False)
    mine_early = [_with_own(r, chip_blk(g)) for r, g in
                  zip(_split_wait(early_h, "scatter_early_wait", small_tok), sent_early)]
    swap_h, swap_tok = _split_start(mine_early, "swap2", "swap_early_start")
    dw_qkv = _mm_tn(x2d, dqkv, "even_dw_qkv", tn=QKV // 2, out_dtype=F32, after=swap_tok)
    dw_bch = _mm_tn(x2d, dbch, "even_dw_bch", tn=BCH // 2, out_dtype=F32)
    dw_f = _mm_tn(x2d, dfl, "even_dw_f", tn=LANES, out_dtype=F32)
    gi_even = _by_owner_cols(jnp.concatenate([dw_qkv, dw_f[:, :FOX_HEADS], dw_bch], axis=1).astype(BF16))
    sent_late = [gi_even, go_even]
    late_h, late_tok = _split_start(sent_late, "scatter4", "scatter_late_start")
    grad_x = _mm_nt([(dqkv, 0, 0, QKV), (dbch, 0, QKV, QKV + BCH), (dfl, 0, QKV + BCH, EVEN_IN_PAD)], [w_even_in],
                    "even_dx", res=dz1, after=late_tok)
    mine_late = [_with_own(r, chip_blk(g)) for r, g in zip(_split_wait(late_h, "scatter_late_wait", grad_x), sent_late)]
    theirs_late = _exchange(mine_late, "swap2", "swap_late")
    theirs_early = _split_wait(swap_h, "swap_early_wait", theirs_late[0])
    (gathered,) = _split_wait(small_h, "gather_small_wait", theirs_early[0])
    gathered = lax.dynamic_update_index_in_dim(gathered, packed, 4 * lax.axis_index("x") + 2 * lax.axis_index("y")
                                               + lax.axis_index("c"), 0)
    mine, theirs = mine_late + mine_early, theirs_late + theirs_early
    big_w = [(even_w_in, m_even_w_in, v_even_w_in), (even_w_out, m_even_w_out, v_even_w_out),
             (odd_w_in, m_odd_w_in, v_odd_w_in), (odd_w_out, m_odd_w_out, v_odd_w_out)]
    big_names = ["even_w_in", "even_w_out", "odd_w_in", "odd_w_out"]
    res = {}
    for nm, own, sib, (w, m, v) in zip(big_names, mine, theirs, big_w):
        res[nm] = _adamw([own, sib], w, m, v, f"adamw_{nm}")
    for nm, at, (w, m, v) in (("ffn_w_in", 4, (ffn_w_in, m_ffn_w_in, v_ffn_w_in)),
                              ("ffn_w_out", 6, (ffn_w_out, m_ffn_w_out, v_ffn_w_out))):
        upper = _adamw([mine[at + 1], theirs[at + 1]], w, m, v, f"adamw_{nm}_1", layer=1)
        res[nm] = _adamw([mine[at], theirs[at]], w, m, v, f"adamw_{nm}_0", layer=0, into=upper)

    base = n_rep + pad_rep
    own_rows = jnp.concatenate([
        lax.dynamic_slice_in_dim(gathered, base + 2 * chip, 2, axis=1),
        lax.dynamic_slice_in_dim(gathered, base + SUBLANES + 2 * chip, 2, axis=1),
        lax.dynamic_slice_in_dim(gathered, base + 2 * SUBLANES + CONV_K * chip, CONV_K, axis=1),
        jnp.zeros((N_DEV, 1, LANES), F32)], axis=1)
    small_parts = jnp.concatenate([gathered[:, :base], own_rows], axis=1)[:, None]

    def pack_small(get):
        rows = [_to_rows(get(tw)) for tw in rep_w] + [jnp.zeros((pad_rep, LANES), F32)]
        rows += [get(sh).reshape(-1, LANES) for sh in ((odd_v_ln_g, m_odd_v_ln_g, v_odd_v_ln_g),
                                                       (odd_v_ln_b, m_odd_v_ln_b, v_odd_v_ln_b),
                                                       (even_conv_w, m_even_conv_w, v_even_conv_w))]
        return jnp.concatenate(rows + [jnp.zeros((1, LANES), F32)], axis=0)[None]

    small_out = _adamw([small_parts], pack_small(lambda tw: tw[0]), pack_small(lambda tw: tw[1]),
                       pack_small(lambda tw: tw[2]), "adamw_small")

    def unpack_small(rows3):
        rows = rows3[0]
        out, off = {}, 0
        for nm, (w, _, _), r in zip(rep_names, rep_w, rep_rows):
            out[nm] = rows[off:off + r.shape[0]].reshape(-1)[:w.size].reshape(w.shape)
            off += r.shape[0]
        off += pad_rep
        out["odd_v_ln_g"] = rows[off:off + 2].reshape(odd_v_ln_g.shape)
        out["odd_v_ln_b"] = rows[off + 2:off + 4].reshape(odd_v_ln_b.shape)
        out["even_conv_w"] = rows[off + 4:off + 4 + CONV_K].reshape(even_conv_w.shape)
        return out

    small = [unpack_small(o) for o in small_out]
    order = ["even_w_in", "even_b_f", "even_conv_w", "even_w_out", "odd_w_in", "odd_v_ln_g", "odd_v_ln_b", "odd_w_s",
             "odd_b_s", "odd_w_out", "mix_ln_g", "mix_ln_b", "ffn_w_in", "ffn_w_out", "ffn_ln_g", "ffn_ln_b"]
    outs = [loss, grad_x[None]]
    for kind in range(4):
        for nm in order:
            outs.append(res[nm][kind] if nm in res else small[kind][nm])
    return tuple(outs)
```

```python
import functools
import math

import jax
import jax.numpy as jnp
from jax import lax
from jax.experimental import pallas as pl
from jax.experimental.pallas import tpu as pltpu

F32 = jnp.float32
BF16 = jnp.bfloat16

D_MODEL = 1024
FOX_HEADS = 8
HEAD_DIM = 64
HEAD_PAIRS = FOX_HEADS // 2
FOX_WIDTH = FOX_HEADS * HEAD_DIM
CONV_WIDTH = 512
CONV_K = 3
QKV = 3 * FOX_WIDTH
BCH = 3 * CONV_WIDTH
EVEN_IN = QKV + FOX_HEADS + BCH
EVEN_IN_PAD = QKV + BCH + 128
GMLP_BLOCK = 128
GMLP_GROUPS = 8
CHUNK = 64
FFN_HIDDEN = 2816
HALF_HIDDEN = FFN_HIDDEN // 2
ALPHA = 4.0 ** 0.25
LN_EPS = 1e-5
ADAM_LR = 0.001
ADAM_B1 = 0.9
ADAM_B2 = 0.999
ADAM_EPS = 1e-08
ADAM_WD = 0.01
ADAM_STEP = 10
N_CHIPS = 4
N_DEV = 8
LANES = 128
SUBLANES = 8
ROW_TILE = 512
FFN_ROW_TILE = 512
REDUCE_TILE = 2048
ATT_BLOCK = 512
ATT_FWD_HEADS = 4
ATT_BWD_HEADS = 4
VMEM_LIMIT = 56 * 2 ** 20
NEG = -1e30
MESH = pl.DeviceIdType.MESH
HIGHEST = lax.Precision.HIGHEST
Q_C, Q_ONE, Q_LSE = 64, 67, 70
K_ONE, K_C, K_ONE2 = 64, 67, 70
V_ONE = 64
DO_DELTA = 65
NT = (((1,), (1,)), ((), ()))
TN = (((0,), (0,)), ((), ()))


def _cp():
    return pltpu.CompilerParams(vmem_limit_bytes=VMEM_LIMIT)


def _resident(shape):
    zeros = (0,) * len(shape)
    return pl.BlockSpec(shape, lambda *_: zeros, pipeline_mode=pl.Buffered(1))


def _sds(shape, dtype):
    return jax.ShapeDtypeStruct(tuple(shape), dtype)


_MASKS = {
    "gather4": [(1, 0, 0), (0, 1, 0), (1, 1, 0)],
    "scatter4": [(1, 0, 0), (0, 1, 0), (1, 1, 0)],
    "swap2": [(0, 0, 1)],
    "gather8": [(0, 0, 1), (0, 1, 0), (0, 1, 1), (1, 0, 0), (1, 0, 1), (1, 1, 0), (1, 1, 1)],
}


def _exchange(arrs, mode, name):
    n = len(arrs)
    masks = _MASKS[mode]
    npeer = len(masks)
    lead = {"gather4": N_CHIPS, "gather8": N_DEV}.get(mode)
    out_shapes = [_sds(((lead,) if lead else ()) + a.shape, a.dtype) for a in arrs]

    def body(*refs):
        ins, outs = refs[:n], refs[n:2 * n]
        send_sems, recv_sems, loc_sems = refs[2 * n:]
        x, y, c = lax.axis_index("x"), lax.axis_index("y"), lax.axis_index("c")
        chip, dev = 2 * x + y, 4 * x + 2 * y + c
        sends, recvs, locs = [], [], []
        for k in range(n):
            if mode == "gather4":
                locs.append(pltpu.make_async_copy(ins[k], outs[k].at[chip], loc_sems.at[k]))
            elif mode == "scatter4":
                locs.append(pltpu.make_async_copy(ins[k].at[chip], outs[k].at[chip], loc_sems.at[k]))
            elif mode == "gather8":
                locs.append(pltpu.make_async_copy(ins[k], outs[k].at[dev], loc_sems.at[k]))
        for cp in locs:
            cp.start()
        for k in range(n):
            for j, (dx, dy, dc) in enumerate(masks):
                px = 1 - x if dx else x
                py = 1 - y if dy else y
                pc = 1 - c if dc else c
                pchip, pdev = 2 * px + py, 4 * px + 2 * py + pc
                if mode == "gather4":
                    src, dst, land = ins[k], outs[k].at[chip], outs[k].at[pchip]
                elif mode == "scatter4":
                    src, dst, land = ins[k].at[pchip], outs[k].at[chip], outs[k].at[pchip]
                elif mode == "swap2":
                    src, dst, land = ins[k], outs[k], outs[k]
                else:
                    src, dst, land = ins[k], outs[k].at[dev], outs[k].at[pdev]
                s = k * npeer + j
                kw = dict(send_sem=send_sems.at[s], recv_sem=recv_sems.at[s], device_id=(px, py, pc),
                          device_id_type=MESH)
                cp = pltpu.make_async_remote_copy(src_ref=src, dst_ref=dst, **kw)
                cp.start()
                sends.append(cp)
                recvs.append(pltpu.make_async_remote_copy(src_ref=src, dst_ref=land, **kw))
        for cp in recvs:
            cp.wait_recv()
        for cp in sends:
            cp.wait_send()
        for cp in locs:
            cp.wait()

    any_spec = pl.BlockSpec(memory_space=pl.ANY)
    outs = pl.pallas_call(
        body,
        out_shape=out_shapes,
        in_specs=[any_spec] * n,
        out_specs=[any_spec] * n,
        scratch_shapes=[pltpu.SemaphoreType.DMA((n * npeer,)), pltpu.SemaphoreType.DMA((n * npeer,)),
                        pltpu.SemaphoreType.DMA((max(n, 1),))],
        name=name,
    )(*arrs)
    return list(outs)


_HBM_SPEC = pl.BlockSpec(memory_space=pltpu.HBM)
_SEM_SPEC = pl.BlockSpec(memory_space=pltpu.SEMAPHORE)
_ANY_SPEC = pl.BlockSpec(memory_space=pl.ANY)
_EFFECT = pltpu.SideEffectType.DATAFLOW_SIDE_EFFECTING


def _split_copies(mode, ins, lands, send_sems, recv_sems):
    x, y, c = lax.axis_index("x"), lax.axis_index("y"), lax.axis_index("c")
    chip, dev = 2 * x + y, 4 * x + 2 * y + c
    masks = _MASKS[mode]
    out = []
    for k in range(len(ins)):
        for j, (dx, dy, dc) in enumerate(masks):
            px = 1 - x if dx else x
            py = 1 - y if dy else y
            pc = 1 - c if dc else c
            pchip, pdev = 2 * px + py, 4 * px + 2 * py + pc
            if mode == "gather4":
                src, dst, land = ins[k], lands[k].at[chip], lands[k].at[pchip]
            elif mode == "scatter4":
                src, dst, land = ins[k].at[pchip], lands[k].at[chip], lands[k].at[pchip]
            elif mode == "swap2":
                src, dst, land = ins[k], lands[k], lands[k]
            else:
                src, dst, land = ins[k], lands[k].at[dev], lands[k].at[pdev]
            s = k * len(masks) + j
            kw = dict(send_sem=send_sems.at[s], recv_sem=recv_sems.at[s], device_id=(px, py, pc), device_id_type=MESH)
            out.append((pltpu.make_async_remote_copy(src_ref=src, dst_ref=dst, **kw),
                        pltpu.make_async_remote_copy(src_ref=src, dst_ref=land, **kw)))
    return out


def _split_start(arrs, mode, name, after=None):
    n = len(arrs)
    nsem = n * len(_MASKS[mode])
    lead = {"gather4": (N_CHIPS,), "gather8": (N_DEV,)}.get(mode, ())
    land_shapes = [lead + a.shape for a in arrs]

    def body(*refs):
        ins, lands = refs[:n], refs[n:2 * n]
        outs = refs[2 * n + (after is not None):]
        for start, _ in _split_copies(mode, ins, lands, outs[0], outs[1]):
            start.start()
        outs[-1][...] = jnp.zeros(outs[-1].shape, F32)

    srcs = [pltpu.with_memory_space_constraint(a, pltpu.HBM) for a in arrs]
    empties = [pltpu.with_memory_space_constraint(lax.empty(s, a.dtype), pltpu.HBM) for s, a in zip(land_shapes, arrs)]
    res = pl.pallas_call(
        body, name=name,
        out_shape=(pltpu.SemaphoreType.DMA((nsem,)), pltpu.SemaphoreType.DMA((nsem,)),
                   *[pltpu.HBM(a.shape, a.dtype) for a in arrs],
                   *[pltpu.HBM(s, a.dtype) for s, a in zip(land_shapes, arrs)],
                   _sds((SUBLANES, LANES), F32)),
        in_specs=[_HBM_SPEC] * (2 * n) + ([_ANY_SPEC] if after is not None else []),
        out_specs=(_SEM_SPEC, _SEM_SPEC, *[_HBM_SPEC] * (2 * n), pl.BlockSpec(memory_space=pltpu.VMEM)),
        input_output_aliases={k: 2 + k for k in range(2 * n)},
        compiler_params=pltpu.CompilerParams(has_side_effects=_EFFECT),
    )(*srcs, *empties, *([after] if after is not None else []))
    return dict(mode=mode, n=n, sems=res[:2], bufs=res[2:2 + 2 * n]), res[-1]


def _split_wait(handle, name, after):
    n, mode = handle["n"], handle["mode"]

    def body(*refs):
        ins, lands = refs[:n], refs[n:2 * n]
        send_sems, recv_sems = refs[2 * n], refs[2 * n + 1]
        for _, arrival in _split_copies(mode, ins, lands, send_sems, recv_sems):
            arrival.wait_send()
            arrival.wait_recv()

    bufs = handle["bufs"]
    res = pl.pallas_call(
        body, name=name,
        out_shape=tuple(pltpu.HBM(b.shape, b.dtype) for b in bufs),
        in_specs=[_HBM_SPEC] * (2 * n) + [_SEM_SPEC, _SEM_SPEC, _ANY_SPEC],
        out_specs=tuple([_HBM_SPEC] * (2 * n)),
        input_output_aliases={k: k for k in range(2 * n)},
        compiler_params=pltpu.CompilerParams(has_side_effects=_EFFECT),
    )(*bufs, *handle["sems"], after)
    return list(res[n:])


def _with_own(landed, own):
    chip = 2 * lax.axis_index("x") + lax.axis_index("y")
    return lax.dynamic_update_index_in_dim(landed, own, chip, 0)


def _sigmoid(x):
    return 0.5 * jnp.tanh(0.5 * x) + 0.5


def _log_sigmoid(x):
    e = jnp.exp(-jnp.abs(x))
    log1p = jnp.where(e < 1e-2, e * (1.0 - e * (0.5 - e * (1.0 / 3.0))), jnp.log(1.0 + e))
    return jnp.minimum(x, 0.0) - log1p


def _gelu(a):
    return 0.5 * a * (1.0 + lax.erf(a * (2.0 ** -0.5)))


def _gelu_grad(a):
    cdf = 0.5 * (1.0 + lax.erf(a * (2.0 ** -0.5)))
    pdf = jnp.exp(-0.5 * a * a) * (1.0 / math.sqrt(2.0 * math.pi))
    return cdf + a * pdf


def _ln_fwd(z):
    mu = jnp.mean(z, axis=-1, keepdims=True)
    zc = z - mu
    var = jnp.mean(zc * zc, axis=-1, keepdims=True)
    rstd = lax.rsqrt(var + LN_EPS)
    return zc * rstd, rstd


def _ln_bwd(dy, xhat, rstd, g):
    dxh = dy * g
    m1 = jnp.mean(dxh, axis=-1, keepdims=True)
    m2 = jnp.mean(dxh * xhat, axis=-1, keepdims=True)
    dz = rstd * (dxh - m1 - xhat * m2)
    return dz, jnp.sum(dy * xhat, axis=0, keepdims=True), jnp.sum(dy, axis=0, keepdims=True)


def _shift_down(z, halo):
    r = lax.broadcasted_iota(jnp.int32, z.shape, 0)
    z1 = jnp.where(r == 0, halo[7:8, :], pltpu.roll(z, 1, 0))
    z2 = jnp.where(r == 0, halo[6:7, :], jnp.where(r == 1, halo[7:8, :], pltpu.roll(z, 2, 0)))
    return z1, z2


def _shift_up(z, halo):
    n = z.shape[0]
    r = lax.broadcasted_iota(jnp.int32, z.shape, 0)
    z1 = jnp.where(r == n - 1, halo[0:1, :], pltpu.roll(z, n - 1, 0))
    z2 = jnp.where(r == n - 1, halo[1:2, :], jnp.where(r == n - 2, halo[0:1, :], pltpu.roll(z, n - 2, 0)))
    return z1, z2


def _accumulate(ref, first, value):
    @pl.when(first)
    def _():
        ref[...] = value

    @pl.when(jnp.logical_not(first))
    def _():
        ref[...] += value


def _proj(x, w, splits, name):
    t, k = x.shape
    tm = min(ROW_TILE, t)

    def body(x_ref, w_ref, *outs):
        a = x_ref[...].astype(BF16)
        for (lo, hi, dt), o in zip(splits, outs):
            o[...] = jnp.dot(a, w_ref[:, lo:hi], preferred_element_type=F32).astype(dt)

    return pl.pallas_call(
        body, grid=(t // tm,),
        in_specs=[pl.BlockSpec((tm, k), lambda i: (i, 0)), _resident(w.shape)],
        out_specs=[pl.BlockSpec((tm, hi - lo), lambda i: (i, 0)) for lo, hi, _ in splits],
        out_shape=[_sds((t, hi - lo), dt) for lo, hi, dt in splits],
        compiler_params=_cp(), name=name)(x, w)


def _fgate_fwd(fl3, b_f):
    nc = fl3.shape[0]

    def body(f_ref, b_ref, c_ref):
        r = lax.broadcasted_iota(jnp.int32, (LANES, LANES), 0)
        cidx = lax.broadcasted_iota(jnp.int32, (LANES, LANES), 1)
        upper = (r <= cidx).astype(F32)

        def step(i, carry):
            lf = _log_sigmoid(f_ref[i] + b_ref[...])
            cc = jnp.dot(lf, upper, precision=HIGHEST, preferred_element_type=F32) + carry
            c_ref[i] = cc
            return cc[:, LANES - 1:LANES]

        lax.fori_loop(0, nc, step, jnp.zeros((FOX_HEADS, 1), F32))

    return pl.pallas_call(body, out_shape=_sds(fl3.shape, F32), name="fgate_fwd")(fl3, b_f)


def _fgate_bwd(dc3, fl3, b_f):
    nc = fl3.shape[0]

    def body(dc_ref, f_ref, b_ref, df_ref, db_ref):
        r = lax.broadcasted_iota(jnp.int32, (LANES, LANES), 0)
        cidx = lax.broadcasted_iota(jnp.int32, (LANES, LANES), 1)
        lower = (r >= cidx).astype(F32)

        def step(n, carry):
            suffix, db = carry
            i = nc - 1 - n
            dlf = jnp.dot(dc_ref[i], lower, precision=HIGHEST, preferred_element_type=F32) + suffix
            df = dlf * (1.0 - _sigmoid(f_ref[i] + b_ref[...]))
            df_ref[i] = df
            return dlf[:, 0:1], db + jnp.sum(df, axis=1, keepdims=True)

        zero = jnp.zeros((FOX_HEADS, 1), F32)
        _, db = lax.fori_loop(0, nc, step, (zero, zero))
        db_ref[...] = db

    return pl.pallas_call(body, out_shape=[_sds(fl3.shape, F32), _sds((FOX_HEADS, 1), F32)],
                          name="fgate_bwd")(dc3, fl3, b_f)


def _split3(c):
    hi = c.astype(BF16).astype(F32)
    mid = (c - hi).astype(BF16).astype(F32)
    lo = (c - hi - mid).astype(BF16).astype(F32)
    return hi, mid, lo


def _lane_pieces(lane, start, pieces, sign):
    out = jnp.zeros(lane.shape, F32)
    for n, p in enumerate(pieces):
        out = jnp.where(lane == start + n, sign * p, out)
    return out


def _attn_pack(qkv, c_col):
    t = qkv.shape[0]
    tm = min(ROW_TILE, t)
    hd = HEAD_DIM

    def body(x_ref, c_ref, qp_ref, kp_ref, vp_ref, kt_ref, vt_ref):
        lane = lax.broadcasted_iota(jnp.int32, (tm, hd), 1) + hd
        for h in range(FOX_HEADS):
            pieces = _split3(c_ref[:, h:h + 1])
            ones = lambda a, b: jnp.where(jnp.logical_and(lane >= a, lane < b), 1.0, 0.0)
            q_extra = _lane_pieces(lane, Q_C, pieces, 1.0) + ones(Q_ONE, Q_ONE + 3)
            k_extra = _lane_pieces(lane, K_C, pieces, -1.0) + ones(K_ONE, K_ONE + 3) + ones(K_ONE2, K_ONE2 + 3)
            qp_ref[h, :, :hd] = (x_ref[:, h * hd:(h + 1) * hd].astype(F32) * (hd ** -0.5)).astype(BF16)
            qp_ref[h, :, hd:] = q_extra.astype(BF16)
            kp_ref[h, :, :hd] = x_ref[:, FOX_WIDTH + h * hd:FOX_WIDTH + (h + 1) * hd]
            kp_ref[h, :, hd:] = k_extra.astype(BF16)
            vp_ref[h, :, :hd] = x_ref[:, 2 * FOX_WIDTH + h * hd:2 * FOX_WIDTH + (h + 1) * hd]
            vp_ref[h, :, hd:] = ones(V_ONE, V_ONE + 4).astype(BF16)
            kt_ref[h] = kp_ref[h].astype(F32).T.astype(BF16)
            vt_ref[h] = vp_ref[h].astype(F32).T.astype(BF16)

    row3 = pl.BlockSpec((FOX_HEADS, tm, LANES), lambda i: (0, i, 0))
    col3 = pl.BlockSpec((FOX_HEADS, LANES, tm), lambda i: (0, 0, i))
    return pl.pallas_call(
        body, grid=(t // tm,),
        in_specs=[pl.BlockSpec((tm, QKV), lambda i: (i, 0)), pl.BlockSpec((tm, FOX_HEADS), lambda i: (i, 0))],
        out_specs=[row3, row3, row3, col3, col3],
        out_shape=[_sds((FOX_HEADS, t, LANES), BF16)] * 3 + [_sds((FOX_HEADS, LANES, t), BF16)] * 2,
        compiler_params=_cp(), name="attn_pack")(qkv, c_col)


def _triangle(nq, key_major):
    if key_major:
        pairs = [(i, j) for j in range(nq) for i in range(j, nq)]
    else:
        pairs = [(i, j) for i in range(nq) for j in range(i + 1)]
    return jnp.asarray([p[0] for p in pairs], jnp.int32), jnp.asarray([p[1] for p in pairs], jnp.int32)


def _attn_fwd(qp, kp, vt):
    t = qp.shape[1]
    bq = min(ATT_BLOCK, t)
    nq = t // bq
    nh = ATT_FWD_HEADS
    i_tab, j_tab = _triangle(nq, key_major=False)

    def body(it_ref, jt_ref, q_ref, k_ref, vt_ref, o_ref, lse_ref, m_sc, acc_sc):
        s = pl.program_id(1)
        i, j = it_ref[s], jt_ref[s]

        @pl.when(j == 0)
        def _():
            m_sc[...] = jnp.full(m_sc.shape, NEG, F32)
            acc_sc[...] = jnp.zeros(acc_sc.shape, F32)

        def sweep(masked):
            scores = lambda h: lax.dot_general(k_ref[h], q_ref[h], NT, preferred_element_type=F32)

            def accumulate(h, pt, rescale):
                acc_sc[h] = rescale * acc_sc[h] + jnp.dot(vt_ref[h], pt, preferred_element_type=F32)

            ahead, behind = scores(0), None
            for h in range(nh):
                st = ahead
                if h + 1 < nh:
                    ahead = scores(h + 1)
                if behind is not None:
                    accumulate(*behind)
                if masked:
                    key = lax.broadcasted_iota(jnp.int32, (bq, bq), 0)
                    qry = lax.broadcasted_iota(jnp.int32, (bq, bq), 1)
                    st = jnp.where(key <= qry, st, NEG)
                m_prev = m_sc[h]
                m_new = jnp.maximum(m_prev, jnp.max(st, axis=0, keepdims=True))
                behind = (h, jnp.exp(st - m_new).astype(BF16), jnp.exp(m_prev - m_new))
                m_sc[h] = m_new
            accumulate(*behind)

        @pl.when(j < i)
        def _():
            sweep(False)

        @pl.when(j == i)
        def _():
            sweep(True)
            for h in range(nh):
                acc = acc_sc[h]
                denom = acc[V_ONE:V_ONE + 1, :]
                o_ref[:, h * HEAD_DIM:(h + 1) * HEAD_DIM] = (acc[:HEAD_DIM, :] / denom).T.astype(BF16)
                lse_ref[h] = m_sc[h] + jnp.log(denom)

    grid_spec = pltpu.PrefetchScalarGridSpec(
        num_scalar_prefetch=2, grid=(FOX_HEADS // nh, i_tab.shape[0]),
        in_specs=[pl.BlockSpec((nh, bq, LANES), lambda hp, s, it, jt: (hp, it[s], 0)),
                  pl.BlockSpec((nh, bq, LANES), lambda hp, s, it, jt: (hp, jt[s], 0)),
                  pl.BlockSpec((nh, LANES, bq), lambda hp, s, it, jt: (hp, 0, jt[s]))],
        out_specs=[pl.BlockSpec((bq, nh * HEAD_DIM), lambda hp, s, it, jt: (it[s], hp)),
                   pl.BlockSpec((nh, 1, bq), lambda hp, s, it, jt: (hp, 0, it[s]))],
        scratch_shapes=[pltpu.VMEM((nh, 1, bq), F32), pltpu.VMEM((nh, LANES, bq), F32)])
    return pl.pallas_call(body, grid_spec=grid_spec,
                          out_shape=[_sds((t, FOX_WIDTH), BF16), _sds((FOX_HEADS, 1, t), F32)],
                          compiler_params=_cp(), name="attn_fwd")(i_tab, j_tab, qp, kp, vt)


def _conv_fwd(bch, conv_w):
    t = bch.shape[0]
    tm = min(ROW_TILE, t)
    halo_blocks = tm // SUBLANES
    cw = CONV_WIDTH

    def body(cur_ref, prev_ref, w_ref, o_ref):
        i = pl.program_id(0)
        z = cur_ref[:, cw:2 * cw] * cur_ref[:, 2 * cw:]
        zp = jnp.where(i == 0, 0.0, prev_ref[:, cw:2 * cw] * prev_ref[:, 2 * cw:])
        z1, z2 = _shift_down(z, zp)
        y = w_ref[0:1, :] * z2 + w_ref[1:2, :] * z1 + w_ref[2:3, :] * z
        o_ref[...] = (cur_ref[:, :cw] * y).astype(BF16)

    return pl.pallas_call(
        body, grid=(t // tm,),
        in_specs=[pl.BlockSpec((tm, BCH), lambda i: (i, 0)),
                  pl.BlockSpec((SUBLANES, BCH), lambda i: (jnp.maximum(i * halo_blocks - 1, 0), 0)),
                  _resident(conv_w.shape)],
        out_specs=pl.BlockSpec((tm, cw), lambda i: (i, 0)),
        out_shape=_sds((t, cw), BF16), compiler_params=_cp(), name="conv_fwd")(bch, bch, conv_w)


def _mm_res_ln(pairs, res, g, b, name):
    from_ln = isinstance(res, tuple)
    res_args = list(res) if from_ln else [res]
    t, d = res_args[0].shape
    tm = min(ROW_TILE, t)
    n = len(pairs)

    def body(*refs):
        a_refs, w_refs = refs[:n], refs[n:2 * n]
        res_refs = refs[2 * n:2 * n + len(res_args)]
        g_ref, b_ref, yb_ref, xh_ref, rs_ref = refs[2 * n + len(res_args):]
        r = res_refs[0][...]
        if from_ln:
            r = r * res_refs[1][...] + res_refs[2][...]
        z = ALPHA * r
        for a_ref, w_ref in zip(a_refs, w_refs):
            z = z + jnp.dot(a_ref[...].astype(BF16), w_ref[...], preferred_element_type=F32)
        xhat, rstd = _ln_fwd(z)
        yb_ref[...] = (xhat * g_ref[...] + b_ref[...]).astype(BF16)
        xh_ref[...] = xhat
        rs_ref[...] = rstd

    row = lambda i: (i, 0)
    full = pl.BlockSpec((tm, d), row)
    return pl.pallas_call(
        body, grid=(t // tm,),
        in_specs=[pl.BlockSpec((tm, a.shape[1]), row) for a, _ in pairs] + [_resident(w.shape) for _, w in pairs]
        + [full] + [_resident(a.shape) for a in res_args[1:]] + [_resident(g.shape), _resident(b.shape)],
        out_specs=[full, full, pl.BlockSpec((tm, 1), row)],
        out_shape=[_sds((t, d), BF16), _sds((t, d), F32), _sds((t, 1), F32)],
        compiler_params=_cp(), name=name)(*[a for a, _ in pairs], *[w for _, w in pairs], *res_args, g, b)


def _ffn_in(x, wi, name):
    t, d = x.shape
    tm = min(FFN_ROW_TILE, t)
    hh = HALF_HIDDEN

    def body(x_ref, w_ref, gu_ref, h_ref):
        a = x_ref[...].astype(BF16)
        for c in range(2):
            gs, us = slice(c * hh, (c + 1) * hh), slice(FFN_HIDDEN + c * hh, FFN_HIDDEN + (c + 1) * hh)
            g = jnp.dot(a, w_ref[c], preferred_element_type=F32)
            u = jnp.dot(a, w_ref[2 + c], preferred_element_type=F32)
            sig = _sigmoid(g)
            silu = g * sig
            gu_ref[:, gs] = (u * sig * (1.0 + g * (1.0 - sig))).astype(BF16)
            gu_ref[:, us] = silu.astype(BF16)
            h_ref[:, gs] = (silu * u).astype(BF16)

    row = lambda i: (i, 0)
    return pl.pallas_call(
        body, grid=(t // tm,),
        in_specs=[pl.BlockSpec((tm, d), row), _resident(wi.shape)],
        out_specs=[pl.BlockSpec((tm, 2 * FFN_HIDDEN), row), pl.BlockSpec((tm, FFN_HIDDEN), row)],
        out_shape=[_sds((t, 2 * FFN_HIDDEN), BF16), _sds((t, FFN_HIDDEN), BF16)],
        compiler_params=_cp(), name=name)(x, wi)


def _gmlp_fwd(x, w_in, vg, vb, wm, bs_col):
    t, d = x.shape
    tm = min(ROW_TILE, t)
    gb = GMLP_BLOCK

    def body(x_ref, w_ref, vg_ref, vb_ref, wm_ref, bs_ref, a_ref, o_ref):
        xb = x_ref[...].astype(BF16)
        nc = w_ref.shape[2]
        for j in range(w_ref.shape[0]):
            a_ref[:, j * nc:(j + 1) * nc] = jnp.dot(xb, w_ref[j], preferred_element_type=F32)
        u = _gelu(a_ref[:, :d])
        vhat, _ = _ln_fwd(_gelu(a_ref[:, d:]))
        vln = (vhat * vg_ref[...] + vb_ref[...]).astype(BF16)
        for blk in range(tm // gb):
            rs = slice(blk * gb, (blk + 1) * gb)
            for gi in range(GMLP_GROUPS):
                cs = slice(gi * gb, (gi + 1) * gb)
                s = jnp.dot(wm_ref[gi], vln[rs, cs], preferred_element_type=F32) + bs_ref[:, gi:gi + 1]
                o_ref[rs, cs] = (u[rs, cs] * s).astype(BF16)

    row = lambda i: (i, 0)
    return pl.pallas_call(
        body, grid=(t // tm,),
        in_specs=[pl.BlockSpec((tm, d), row), _resident(w_in.shape), _resident(vg.shape), _resident(vb.shape),
                  _resident(wm.shape), _resident(bs_col.shape)],
        out_specs=[pl.BlockSpec((tm, 2 * d), row), pl.BlockSpec((tm, d), row)],
        out_shape=[_sds((t, 2 * d), F32), _sds((t, d), BF16)],
        compiler_params=_cp(), name="gmlp_fwd")(x, w_in, vg, vb, wm, bs_col)


def _loss_ln_bwd(xhat, rstd, g, b, target):
    t, d = xhat.shape
    tm = min(ROW_TILE, t)

    def body(xh_ref, rs_ref, g_ref, b_ref, t_ref, sq_ref, dz_ref, dg_ref, db_ref):
        first = pl.program_id(0) == 0
        xh = xh_ref[...]
        err = xh * g_ref[...] + b_ref[...] - t_ref[...]
        dz, dg, db = _ln_bwd(err * (1.0 / d), xh, rs_ref[...], g_ref[...])
        dz_ref[...] = dz
        _accumulate(sq_ref, first, jnp.sum(err * err, axis=0, keepdims=True))
        _accumulate(dg_ref, first, dg)
        _accumulate(db_ref, first, db)

    row = lambda i: (i, 0)
    vec = pl.BlockSpec((1, d), lambda i: (0, 0))
    return pl.pallas_call(
        body, grid=(t // tm,),
        in_specs=[pl.BlockSpec((tm, d), row), pl.BlockSpec((tm, 1), row), _resident(g.shape), _resident(b.shape),
                  pl.BlockSpec((tm, d), row)],
        out_specs=[vec, pl.BlockSpec((tm, d), row), vec, vec],
        out_shape=[_sds((1, d), F32), _sds((t, d), F32), _sds((1, d), F32), _sds((1, d), F32)],
        compiler_params=_cp(), name="loss_ln_bwd")(xhat, rstd, g, b, target)


def _mm_nt(pairs, ws, name, *, tm=ROW_TILE, res=None, ln=None, out_dtype=F32, after=None):
    t = pairs[0][0].shape[0]
    k = ws[0].shape[-2]
    tm = min(tm, t)
    n, nw = len(pairs), len(ws)

    def body(*refs):
        refs = refs[after is not None:]
        a_refs, w_refs = refs[:n], refs[n:n + nw]
        rest = list(refs[n + nw:])
        dx = None
        for a_ref, (_, wi, lo, hi) in zip(a_refs, pairs):
            w_ref = w_refs[wi]
            if len(w_ref.shape) == 3:
                nc = w_ref.shape[2]
                parts = [lax.dot_general(a_ref[:, j * nc:(j + 1) * nc].astype(BF16), w_ref[j], NT,
                                         preferred_element_type=F32) for j in range(w_ref.shape[0])]
            else:
                parts = [lax.dot_general(a_ref[...].astype(BF16), w_ref[:, lo:hi], NT, preferred_element_type=F32)]
            for part in parts:
                dx = part if dx is None else dx + part
        if res is not None:
            dx = dx + ALPHA * rest.pop(0)[...]
        if ln is None:
            rest[0][...] = dx.astype(out_dtype)
            return
        xh_ref, rs_ref, g_ref, dz_ref, dg_ref, db_ref = rest
        first = pl.program_id(0) == 0
        dz, dg, db = _ln_bwd(dx, xh_ref[...], rs_ref[...], g_ref[...])
        dz_ref[...] = dz
        _accumulate(dg_ref, first, dg)
        _accumulate(db_ref, first, db)

    row = lambda i: (i, 0)
    in_specs = [pl.BlockSpec((tm, a.shape[1]), row) for a, _, _, _ in pairs] + [_resident(w.shape) for w in ws]
    args = [a for a, _, _, _ in pairs] + list(ws)
    if res is not None:
        in_specs.append(pl.BlockSpec((tm, k), row))
        args.append(res)
    if ln is None:
        out_specs = pl.BlockSpec((tm, k), row)
        out_shape = _sds((t, k), out_dtype)
    else:
        xhat, rstd, g = ln
        in_specs += [pl.BlockSpec((tm, k), row), pl.BlockSpec((tm, 1), row), _resident(g.shape)]
        args += [xhat, rstd, g]
        vec = pl.BlockSpec((1, k), lambda i: (0, 0))
        out_specs = [pl.BlockSpec((tm, k), row), vec, vec]
        out_shape = [_sds((t, k), F32), _sds((1, k), F32), _sds((1, k), F32)]
    if after is not None:
        in_specs.insert(0, _ANY_SPEC)
        args.insert(0, after)
    return pl.pallas_call(body, grid=(t // tm,), in_specs=in_specs, out_specs=out_specs, out_shape=out_shape,
                          compiler_params=_cp(), name=name)(*args)


def _mm_tn(a, b, name, *, tn, tk=None, tt=None, stack_cols=False, out_dtype=BF16, after=None):
    t, k = a.shape
    n = b.shape[1]
    tk = k if tk is None else tk
    tt = min(REDUCE_TILE if tt is None else tt, t)
    nt = t // tt

    def body(a_ref, b_ref, *rest):
        o_ref, acc_ref = rest[after is not None:]
        s = pl.program_id(2)
        part = lax.dot_general(a_ref[...].astype(BF16), b_ref[...].astype(BF16), TN, preferred_element_type=F32)
        _accumulate(acc_ref, s == 0, part)

        @pl.when(s == nt - 1)
        def _():
            o_ref[...] = acc_ref[...].astype(out_dtype).reshape(o_ref.shape)

    if stack_cols:
        assert tk == k
        out_spec = pl.BlockSpec((1, k, tn), lambda kk, j, s: (j, 0, 0))
        out_shape = _sds((n // tn, k, tn), out_dtype)
    else:
        out_spec = pl.BlockSpec((tk, tn), lambda kk, j, s: (kk, j))
        out_shape = _sds((k, n), out_dtype)
    return pl.pallas_call(
        body, grid=(k // tk, n // tn, nt),
        in_specs=[pl.BlockSpec((tt, tk), lambda kk, j, s: (s, kk)), pl.BlockSpec((tt, tn), lambda kk, j, s: (s, j))]
        + ([_ANY_SPEC] if after is not None else []),
        out_specs=out_spec, out_shape=out_shape,
        scratch_shapes=[pltpu.VMEM((tk, tn), F32)],
        compiler_params=_cp(), name=name)(a, b, *([after] if after is not None else []))


def _ffn_bwd_hidden(dz, wo, gu, name):
    t, d = dz.shape
    tm = min(FFN_ROW_TILE, t)
    hh = HALF_HIDDEN

    def body(dz_ref, w_ref, gu_ref, o_ref):
        a = dz_ref[...].astype(BF16)
        for c in range(2):
            gs, us = slice(c * hh, (c + 1) * hh), slice(FFN_HIDDEN + c * hh, FFN_HIDDEN + (c + 1) * hh)
            dh = lax.dot_general(a, w_ref[gs, :], NT, preferred_element_type=F32)
            o_ref[:, gs] = (dh * gu_ref[:, gs].astype(F32)).astype(BF16)
            o_ref[:, us] = (dh * gu_ref[:, us].astype(F32)).astype(BF16)

    row = lambda i: (i, 0)
    return pl.pallas_call(
        body, grid=(t // tm,),
        in_specs=[pl.BlockSpec((tm, d), row), _resident(wo.shape), pl.BlockSpec((tm, 2 * FFN_HIDDEN), row)],
        out_specs=pl.BlockSpec((tm, 2 * FFN_HIDDEN), row),
        out_shape=_sds((t, 2 * FFN_HIDDEN), BF16), compiler_params=_cp(), name=name)(dz, wo, gu)


def _gmlp_bwd(dgated, a, vg, vb, wm, bs_col):
    t, d2 = a.shape
    d = d2 // 2
    tm = min(ROW_TILE, t)
    gb = GMLP_BLOCK

    def body(dg_ref, a_ref, vg_ref, vb_ref, wm_ref, bs_ref, da_ref, dws_ref, dbs_ref, dvg_ref, dvb_ref, dvln_sc):
        first = pl.program_id(0) == 0
        au, av = a_ref[:, :d], a_ref[:, d:]
        u = _gelu(au)
        vhat, rstd = _ln_fwd(_gelu(av))
        vln = (vhat * vg_ref[...] + vb_ref[...]).astype(BF16)
        dgate = dg_ref[...]

        @pl.when(first)
        def _():
            dws_ref[...] = jnp.zeros(dws_ref.shape, F32)
            dbs_ref[...] = jnp.zeros(dbs_ref.shape, F32)

        for blk in range(tm // gb):
            rs = slice(blk * gb, (blk + 1) * gb)
            for gi in range(GMLP_GROUPS):
                cs = slice(gi * gb, (gi + 1) * gb)
                vblk = vln[rs, cs]
                s = jnp.dot(wm_ref[gi], vblk, preferred_element_type=F32) + bs_ref[:, gi:gi + 1]
                dgb = dgate[rs, cs]
                da_ref[rs, cs] = (dgb * s * _gelu_grad(au[rs, cs])).astype(BF16)
                ds = dgb * u[rs, cs]
                dsb = ds.astype(BF16)
                dws_ref[gi] += lax.dot_general(dsb, vblk, NT, preferred_element_type=F32)
                dbs_ref[:, gi:gi + 1] += jnp.sum(ds, axis=1, keepdims=True)
                dvln_sc[rs, cs] = lax.dot_general(wm_ref[gi], dsb, TN, preferred_element_type=F32)
        dv, dvg, dvb = _ln_bwd(dvln_sc[...], vhat, rstd, vg_ref[...])
        da_ref[:, d:] = (dv * _gelu_grad(av)).astype(BF16)
        _accumulate(dvg_ref, first, dvg)
        _accumulate(dvb_ref, first, dvb)

    row = lambda i: (i, 0)
    vec = pl.BlockSpec((1, d), lambda i: (0, 0))
    return pl.pallas_call(
        body, grid=(t // tm,),
        in_specs=[pl.BlockSpec((tm, d), row), pl.BlockSpec((tm, d2), row), _resident(vg.shape), _resident(vb.shape),
                  _resident(wm.shape), _resident(bs_col.shape)],
        out_specs=[pl.BlockSpec((tm, d2), row), pl.BlockSpec(wm.shape, lambda i: (0, 0, 0)),
                   pl.BlockSpec(bs_col.shape, lambda i: (0, 0)), vec, vec],
        out_shape=[_sds((t, d2), BF16), _sds(wm.shape, F32), _sds(bs_col.shape, F32), _sds((1, d), F32), _sds((1, d), F32)],
        scratch_shapes=[pltpu.VMEM((tm, d), F32)],
        compiler_params=_cp(), name="gmlp_bwd")(dgated, a, vg, vb, wm, bs_col)


def _conv_bwd(bch, dmix, conv_w):
    t = bch.shape[0]
    tm = min(ROW_TILE, t)
    nb = t // tm
    halo_blocks = tm // SUBLANES
    cw = CONV_WIDTH

    def body(cur_ref, prev_ref, next_ref, dc_ref, dn_ref, w_ref, o_ref, dw_ref):
        i = pl.program_id(0)
        bgate, cgate, hval = cur_ref[:, :cw], cur_ref[:, cw:2 * cw], cur_ref[:, 2 * cw:]
        z = cgate * hval
        zp = jnp.where(i == 0, 0.0, prev_ref[:, cw:2 * cw] * prev_ref[:, 2 * cw:])
        z1, z2 = _shift_down(z, zp)
        w0, w1, w2 = w_ref[0:1, :], w_ref[1:2, :], w_ref[2:3, :]
        dconv = dc_ref[...]
        o_ref[:, :cw] = (dconv * (w0 * z2 + w1 * z1 + w2 * z)).astype(BF16)
        dy = dconv * bgate
        dyn = jnp.where(i == nb - 1, 0.0, dn_ref[...] * next_ref[:, :cw])
        dy1, dy2 = _shift_up(dy, dyn)
        dz = w2 * dy + w1 * dy1 + w0 * dy2
        o_ref[:, cw:2 * cw] = (dz * hval).astype(BF16)
        o_ref[:, 2 * cw:] = (dz * cgate).astype(BF16)

        @pl.when(i == 0)
        def _():
            dw_ref[...] = jnp.zeros(dw_ref.shape, F32)

        for tap, zs in enumerate((z2, z1, z)):
            dw_ref[tap:tap + 1, :] += jnp.sum(dy * zs, axis=0, keepdims=True)

    last_halo = t // SUBLANES - 1
    return pl.pallas_call(
        body, grid=(nb,),
        in_specs=[pl.BlockSpec((tm, BCH), lambda i: (i, 0)),
                  pl.BlockSpec((SUBLANES, BCH), lambda i: (jnp.maximum(i * halo_blocks - 1, 0), 0)),
                  pl.BlockSpec((SUBLANES, BCH), lambda i: (jnp.minimum((i + 1) * halo_blocks, last_halo), 0)),
                  pl.BlockSpec((tm, cw), lambda i: (i, 1)),
                  pl.BlockSpec((SUBLANES, cw), lambda i: (jnp.minimum((i + 1) * halo_blocks, last_halo), 1)),
                  _resident(conv_w.shape)],
        out_specs=[pl.BlockSpec((tm, BCH), lambda i: (i, 0)), pl.BlockSpec((SUBLANES, cw), lambda i: (0, 0))],
        out_shape=[_sds((t, BCH), BF16), _sds((SUBLANES, cw), F32)],
        compiler_params=_cp(), name="conv_bwd")(bch, bch, bch, dmix, dmix, conv_w)


def _attn_bwd_prep(o, dmix, qp, lse_col):
    t = o.shape[0]
    tm = min(ROW_TILE, t)
    hd = HEAD_DIM

    def body(o_ref, do_ref, qp_ref, lse_ref, qb_ref, dob_ref):
        lane = lax.broadcasted_iota(jnp.int32, (tm, hd), 1) + hd
        for h in range(FOX_HEADS):
            do = do_ref[:, h * hd:(h + 1) * hd]
            delta = jnp.sum(o_ref[:, h * hd:(h + 1) * hd].astype(F32) * do, axis=-1, keepdims=True)
            dob_ref[h, :, :hd] = do.astype(BF16)
            dob_ref[h, :, hd:] = _lane_pieces(lane, DO_DELTA, _split3(delta), -1.0).astype(BF16)
            qb_ref[h, :, :hd] = qp_ref[h, :, :hd]
            qb_ref[h, :, hd:] = (qp_ref[h, :, hd:].astype(F32)
                                 + _lane_pieces(lane, Q_LSE, _split3(lse_ref[:, h:h + 1]), -1.0)).astype(BF16)

    row3 = pl.BlockSpec((FOX_HEADS, tm, LANES), lambda i: (0, i, 0))
    return pl.pallas_call(
        body, grid=(t // tm,),
        in_specs=[pl.BlockSpec((tm, FOX_WIDTH), lambda i: (i, 0)), pl.BlockSpec((tm, FOX_WIDTH), lambda i: (i, 0)), row3,
                  pl.BlockSpec((tm, FOX_HEADS), lambda i: (i, 0))],
        out_specs=[row3, row3], out_shape=[_sds((FOX_HEADS, t, LANES), BF16)] * 2,
        compiler_params=_cp(), name="attn_bwd_prep")(o, dmix, qp, lse_col)


def _attn_bwd(qb, kp, vp, dob, kt):
    t = qb.shape[1]
    bq = min(ATT_BLOCK, t)
    nq = t // bq
    i_tab, j_tab = _triangle(nq, key_major=True)

    def body(it_ref, jt_ref, q_ref, k_ref, v_ref, do_ref, kt_ref, dqt_ref, dk_ref, dv_ref, dk_sc, dv_sc):
        s = pl.program_id(1)
        i, j = it_ref[s], jt_ref[s]

        @pl.when(s == 0)
        def _():
            dqt_ref[...] = jnp.zeros(dqt_ref.shape, F32)

        @pl.when(i == j)
        def _():
            dk_sc[...] = jnp.zeros(dk_sc.shape, F32)
            dv_sc[...] = jnp.zeros(dv_sc.shape, F32)

        cols = pl.ds(pl.multiple_of(i * bq, bq), bq)

        def sweep(masked):
            def scores(h):
                return (lax.dot_general(k_ref[h], q_ref[h], NT, preferred_element_type=F32),
                        lax.dot_general(v_ref[h], do_ref[h], NT, preferred_element_type=F32))

            def accumulate(h, ptb, dstb):
                dv_sc[h] += jnp.dot(ptb, do_ref[h], preferred_element_type=F32)
                dk_sc[h] += jnp.dot(dstb, q_ref[h], preferred_element_type=F32)
                dqt_ref[h, :, cols] += jnp.dot(kt_ref[h], dstb, preferred_element_type=F32)

            ahead, behind = scores(0), None
            for h in range(ATT_BWD_HEADS):
                st, dpt = ahead
                if h + 1 < ATT_BWD_HEADS:
                    ahead = scores(h + 1)
                if behind is not None:
                    accumulate(*behind)
                if masked:
                    key = lax.broadcasted_iota(jnp.int32, (bq, bq), 0)
                    qry = lax.broadcasted_iota(jnp.int32, (bq, bq), 1)
                    st = jnp.where(key <= qry, st, NEG)
                pt = jnp.exp(st)
                behind = (h, pt.astype(BF16), (pt * dpt).astype(BF16))
            accumulate(*behind)

        @pl.when(i == j)
        def _():
            sweep(True)

        @pl.when(i > j)
        def _():
            sweep(False)

        @pl.when(i == nq - 1)
        def _():
            dk_ref[...] = dk_sc[...]
            dv_ref[...] = dv_sc[...].astype(BF16)

    nh = ATT_BWD_HEADS
    qblk = pl.BlockSpec((nh, bq, LANES), lambda hp, s, it, jt: (hp, it[s], 0))
    kblk = pl.BlockSpec((nh, bq, LANES), lambda hp, s, it, jt: (hp, jt[s], 0))
    grid_spec = pltpu.PrefetchScalarGridSpec(
        num_scalar_prefetch=2, grid=(FOX_HEADS // nh, i_tab.shape[0]),
        in_specs=[qblk, kblk, kblk, qblk, pl.BlockSpec((nh, LANES, bq), lambda hp, s, it, jt: (hp, 0, jt[s]))],
        out_specs=[pl.BlockSpec((nh, LANES, t), lambda hp, s, it, jt: (hp, 0, 0), pipeline_mode=pl.Buffered(1)),
                   kblk, kblk],
        scratch_shapes=[pltpu.VMEM((nh, bq, LANES), F32), pltpu.VMEM((nh, bq, LANES), F32)])
    return pl.pallas_call(body, grid_spec=grid_spec,
                          out_shape=[_sds((FOX_HEADS, LANES, t), F32), _sds((FOX_HEADS, t, LANES), F32),
                                     _sds((FOX_HEADS, t, LANES), BF16)],
                          compiler_params=_cp(), name="attn_bwd")(i_tab, j_tab, qb, kp, vp, dob, kt)


def _attn_unpack(dqt, dkp, dvp):
    t = dkp.shape[1]
    tm = min(ROW_TILE, t)
    hd = HEAD_DIM

    def body(dqt_ref, dk_ref, dv_ref, o_ref, dc_ref):
        for h in range(FOX_HEADS):
            dq = dqt_ref[h].T
            o_ref[:, h * hd:(h + 1) * hd] = (dq[:, :hd] * (hd ** -0.5)).astype(BF16)
            o_ref[:, FOX_WIDTH + h * hd:FOX_WIDTH + (h + 1) * hd] = dk_ref[h, :, :hd].astype(BF16)
            o_ref[:, 2 * FOX_WIDTH + h * hd:2 * FOX_WIDTH + (h + 1) * hd] = dv_ref[h, :, :hd]
            dc_ref[:, h:h + 1] = dq[:, K_ONE:K_ONE + 1] - dk_ref[h, :, Q_ONE:Q_ONE + 1]

    row3 = pl.BlockSpec((FOX_HEADS, tm, LANES), lambda i: (0, i, 0))
    return pl.pallas_call(
        body, grid=(t // tm,),
        in_specs=[pl.BlockSpec((FOX_HEADS, LANES, tm), lambda i: (0, 0, i)), row3, row3],
        out_specs=[pl.BlockSpec((tm, QKV), lambda i: (i, 0)), pl.BlockSpec((tm, FOX_HEADS), lambda i: (i, 0))],
        out_shape=[_sds((t, QKV), BF16), _sds((t, FOX_HEADS), F32)],
        compiler_params=_cp(), name="attn_unpack")(dqt, dkp, dvp)


def _adamw(parts, w, m, v, name, layer=None, into=None):
    nl, r, c = w.shape
    tr = r
    for cand in (256, 128, 64, 32, 16):
        if r > cand and r % cand == 0:
            tr = cand
            break
    npart = len(parts)
    bc1 = 1.0 - ADAM_B1 ** ADAM_STEP
    bc2 = 1.0 - ADAM_B2 ** ADAM_STEP

    def body(*refs):
        p_refs = refs[:npart]
        w_ref, m_ref, v_ref = refs[npart:npart + 3]
        g_ref, d_ref, nm_ref, nv_ref = refs[-4:]
        sums = []
        for p_ref in p_refs:
            acc = p_ref[0, 0].astype(F32)
            for s in range(1, p_ref.shape[0]):
                acc = acc + p_ref[s, 0].astype(F32)
            sums.append(acc)
        g = sums[0]
        for extra in sums[1:]:
            g = g + extra
        nm = ADAM_B1 * m_ref[0] + (1.0 - ADAM_B1) * g
        nv = ADAM_B2 * v_ref[0] + (1.0 - ADAM_B2) * (g * g)
        m_hat = nm / bc1
        v_hat = nv / bc2
        g_ref[0] = g
        d_ref[0] = -ADAM_LR * (m_hat / (jnp.sqrt(v_hat) + ADAM_EPS) + ADAM_WD * w_ref[0])
        nm_ref[0] = nm
        nv_ref[0] = nv

    first = 0 if layer is None else layer
    blk = pl.BlockSpec((1, tr, c), lambda l, i: (first + l, i, 0))
    extra = [] if into is None else list(into)
    return pl.pallas_call(
        body, grid=(nl if layer is None else 1, r // tr),
        in_specs=[pl.BlockSpec((p.shape[0], 1, tr, c), lambda l, i: (0, l, i, 0)) for p in parts] + [blk, blk, blk]
        + [_ANY_SPEC] * len(extra),
        out_specs=[blk] * 4, out_shape=[_sds(w.shape, F32)] * 4,
        input_output_aliases={npart + 3 + k: k for k in range(len(extra))},
        compiler_params=_cp(), name=name)(*parts, w, m, v, *extra)


def _to_rows(a):
    flat = a.reshape(-1)
    pad = (-flat.shape[0]) % LANES
    if pad:
        flat = jnp.concatenate([flat, jnp.zeros((pad,), flat.dtype)])
    return flat.reshape(-1, LANES)


def _by_owner_cols(dw):
    k, n = dw.shape
    return dw.reshape(k, N_CHIPS, n // N_CHIPS).transpose(1, 0, 2)[:, None]


def _ffn_fwd(xin_ln, xin_b, wi, wo, g, b, layer):
    gu, h = _ffn_in(xin_b, wi, f"ffn_in_{layer}")
    y_b, xhat, rstd = _mm_res_ln([(h, wo)], xin_ln, g, b, f"ffn_out_ln_{layer}")
    return y_b, (xin_b, gu, h, xhat, rstd)


def _ffn_bwd(dz, saved, wi, wo, ln_below, layer):
    xin_b, gu, h, _, _ = saved
    dgu = _ffn_bwd_hidden(dz, wo, gu, f"ffn_bwd_hidden_{layer}")
    g_out = _mm_tn(h, dz, f"ffn_dw_out_{layer}", tn=D_MODEL, tk=HALF_HIDDEN, tt=REDUCE_TILE // 2)
    g_in = _mm_tn(xin_b, dgu, f"ffn_dw_in_{layer}", tn=HALF_HIDDEN, stack_cols=True)
    below = _mm_nt([(dgu, 0, 0, 0)], [wi], f"ffn_dx_{layer}", tm=FFN_ROW_TILE, res=dz, ln=ln_below)
    return below, g_in, g_out.reshape(N_CHIPS, FFN_HIDDEN // N_CHIPS, D_MODEL)


def kernel(x, even_w_in, even_b_f, even_conv_w, even_w_out, odd_w_in, odd_v_ln_g, odd_v_ln_b, odd_w_s, odd_b_s, odd_w_out, mix_ln_g, mix_ln_b, ffn_w_in, ffn_w_out, ffn_ln_g, ffn_ln_b, loss_target, m_even_w_in, m_even_b_f, m_even_conv_w, m_even_w_out, m_odd_w_in, m_odd_v_ln_g, m_odd_v_ln_b, m_odd_w_s, m_odd_b_s, m_odd_w_out, m_mix_ln_g, m_mix_ln_b, m_ffn_w_in, m_ffn_w_out, m_ffn_ln_g, m_ffn_ln_b, v_even_w_in, v_even_b_f, v_even_conv_w, v_even_w_out, v_odd_w_in, v_odd_v_ln_g, v_odd_v_ln_b, v_odd_w_s, v_odd_b_s, v_odd_w_out, v_mix_ln_g, v_mix_ln_b, v_ffn_w_in, v_ffn_w_out, v_ffn_ln_g, v_ffn_ln_b):
    t = x.shape[1]
    d = D_MODEL
    chip = 2 * lax.axis_index("x") + lax.axis_index("y")
    x2d = x[0]
    target = loss_target[0]

    small_shard = jnp.concatenate([odd_v_ln_g.reshape(2, LANES), odd_v_ln_b.reshape(2, LANES),
                                   even_conv_w.reshape(CONV_K, LANES), jnp.zeros((1, LANES), F32)], axis=0)
    first = [even_w_in[0].astype(BF16), even_w_out[0].astype(BF16), small_shard]
    later = [odd_w_in[0].astype(BF16), odd_w_out[0].astype(BF16), ffn_w_in[0].astype(BF16), ffn_w_in[1].astype(BF16),
             ffn_w_out[0].astype(BF16), ffn_w_out[1].astype(BF16)]
    first_h, first_tok = _split_start(first, "gather4", "gather_first_start")
    later_h, later_tok = _split_start(later, "gather4", "gather_later_start", after=first_tok)
    g_ewi, g_ewo, g_small = [_with_own(g, own) for g, own in
                             zip(_split_wait(first_h, "gather_first_wait", later_tok), first)]
    ewi = g_ewi.transpose(1, 0, 2).reshape(d, EVEN_IN)
    w_even_in = jnp.concatenate([ewi[:, :QKV], ewi[:, QKV + FOX_HEADS:], ewi[:, QKV:QKV + FOX_HEADS],
                                 jnp.zeros((d, LANES - FOX_HEADS), BF16)], axis=1)
    w_even_out = g_ewo.reshape(d, d)
    v_ln_g = g_small[:, 0:2].reshape(1, d)
    v_ln_b = g_small[:, 2:4].reshape(1, d)
    conv_w = g_small[:, 4:7].transpose(1, 0, 2).reshape(CONV_K, CONV_WIDTH)
    chunk_id = jnp.arange(GMLP_BLOCK) // CHUNK
    gmask = chunk_id[None, :] <= chunk_id[:, None]
    w_spatial = jnp.where(gmask[None], odd_w_s[0], 0.0).astype(BF16)
    bs_col = odd_b_s[0].T
    b_f_col = even_b_f.reshape(FOX_HEADS, 1)
    ln = lambda p, l: p[l:l + 1]

    qkv, bch, fl = _proj(x2d, w_even_in, [(0, QKV, BF16), (QKV, QKV + BCH, F32), (QKV + BCH, EVEN_IN_PAD, F32)], "even_proj")
    fl3 = fl[:, :FOX_HEADS].T.reshape(FOX_HEADS, t // LANES, LANES).transpose(1, 0, 2)
    c3 = _fgate_fwd(fl3, b_f_col)
    c_rows = c3.transpose(1, 0, 2).reshape(FOX_HEADS, t)
    qp, kp, vp, kt, vt = _attn_pack(qkv, c_rows.T)
    attn, lse = _attn_fwd(qp, kp, vt)
    conv = _conv_fwd(bch, conv_w)
    x1_b, xh1, rs1 = _mm_res_ln([(attn, w_even_out[:FOX_WIDTH]), (conv, w_even_out[FOX_WIDTH:])], x2d,
                                ln(mix_ln_g, 0), ln(mix_ln_b, 0), "even_out_ln")
    w_odd_in, g_owo, w_fi0, w_fi1, g_fo0, g_fo1 = [_with_own(g, own) for g, own in
                                                   zip(_split_wait(later_h, "gather_later_wait", x1_b), later)]
    w_odd_out = g_owo.reshape(d, d)
    w_ffn_in = [w_fi0, w_fi1]
    w_ffn_out = [g_fo0.reshape(FFN_HIDDEN, d), g_fo1.reshape(FFN_HIDDEN, d)]
    x2_b, ffn0 = _ffn_fwd((xh1, ln(mix_ln_g, 0), ln(mix_ln_b, 0)), x1_b, w_ffn_in[0], w_ffn_out[0],
                          ln(ffn_ln_g, 0), ln(ffn_ln_b, 0), 0)

    a_odd, gated = _gmlp_fwd(x2_b, w_odd_in, v_ln_g, v_ln_b, w_spatial, bs_col)
    x3_b, xh3, rs3 = _mm_res_ln([(gated, w_odd_out)], (ffn0[3], ln(ffn_ln_g, 0), ln(ffn_ln_b, 0)),
                                ln(mix_ln_g, 1), ln(mix_ln_b, 1), "odd_out_ln")
    _, ffn1 = _ffn_fwd((xh3, ln(mix_ln_g, 1), ln(mix_ln_b, 1)), x3_b, w_ffn_in[1], w_ffn_out[1],
                       ln(ffn_ln_g, 1), ln(ffn_ln_b, 1), 1)

    sq, dz4, d_fg1, d_fb1 = _loss_ln_bwd(ffn1[3], ffn1[4], ln(ffn_ln_g, 1), ln(ffn_ln_b, 1), target)
    loss = lax.psum(0.5 / d * jnp.sum(sq), ("x", "y", "c"))
    (dz3, d_mg1, d_mb1), gi_f1, go_f1 = _ffn_bwd(dz4, ffn1, w_ffn_in[1], w_ffn_out[1], (xh3, rs3, ln(mix_ln_g, 1)), 1)

    dgated = _mm_nt([(dz3, 0, 0, d)], [w_odd_out], "odd_dgated")
    go_odd = _mm_tn(gated, dz3, "odd_dw_out", tn=d).reshape(N_CHIPS, 1, d // N_CHIPS, d)
    da_odd, dws, dbs_col, d_vg, d_vb = _gmlp_bwd(dgated, a_odd, v_ln_g, v_ln_b, w_spatial, bs_col)
    gi_odd = _mm_tn(x2_b, da_odd, "odd_dw_in", tn=d // 2, stack_cols=True)[:, None]
    dz2, d_fg0, d_fb0 = _mm_nt([(da_odd, 0, 0, 0)], [w_odd_in], "odd_dx", res=dz3,
                               ln=(ffn0[3], ffn0[4], ln(ffn_ln_g, 0)))
    (dz1, d_mg0, d_mb0), gi_f0, go_f0 = _ffn_bwd(dz2, ffn0, w_ffn_in[0], w_ffn_out[0], (xh1, rs1, ln(mix_ln_g, 0)), 0)

    sent_early = [gi_odd, go_odd, gi_f0[:, None], gi_f1[:, None], go_f0[:, None], go_f1[:, None]]
    early_h, early_tok = _split_start(sent_early, "scatter4", "scatter_early_start")
    dmix = _mm_nt([(dz1, 0, 0, d)], [w_even_out], "even_dmix", after=early_tok)
    go_even = jnp.concatenate([_mm_tn(attn, dz1, "even_dw_out_attn", tn=d), _mm_tn(conv, dz1, "even_dw_out_conv", tn=d)],
                              axis=0).reshape(N_CHIPS, 1, d // N_CHIPS, d)
    dbch, dconv_w8 = _conv_bwd(bch, dmix, conv_w)
    qb, dob = _attn_bwd_prep(attn, dmix, qp, lse.reshape(FOX_HEADS, t).T)
    dqkv, dc_col = _attn_unpack(*_attn_bwd(qb, kp, vp, dob, kt))
    dc3 = dc_col.T.reshape(FOX_HEADS, t // LANES, LANES).transpose(1, 0, 2)
    dfl3, d_bf = _fgate_bwd(dc3, fl3, b_f_col)
    dfl = jnp.concatenate([dfl3.transpose(1, 0, 2).reshape(FOX_HEADS, t).T.astype(BF16),
                           jnp.zeros((t, LANES - FOX_HEADS), BF16)], axis=1)

    dws_masked = jnp.where(gmask[None], dws, 0.0)
    rep_names = ["odd_w_s", "odd_b_s", "mix_ln_g", "mix_ln_b", "ffn_ln_g", "ffn_ln_b", "even_b_f"]
    rep_grads = [dws_masked, dbs_col.T, jnp.concatenate([d_mg0, d_mg1]), jnp.concatenate([d_mb0, d_mb1]),
                 jnp.concatenate([d_fg0, d_fg1]), jnp.concatenate([d_fb0, d_fb1]), d_bf.reshape(1, FOX_HEADS)]
    rep_w = [(odd_w_s, m_odd_w_s, v_odd_w_s), (odd_b_s, m_odd_b_s, v_odd_b_s), (mix_ln_g, m_mix_ln_g, v_mix_ln_g),
             (mix_ln_b, m_mix_ln_b, v_mix_ln_b), (ffn_ln_g, m_ffn_ln_g, v_ffn_ln_g), (ffn_ln_b, m_ffn_ln_b, v_ffn_ln_b),
             (even_b_f, m_even_b_f, v_even_b_f)]
    rep_rows = [_to_rows(gr) for gr in rep_grads]
    n_rep = sum(r.shape[0] for r in rep_rows)
    pad_rep = (-n_rep) % SUBLANES
    dconv_w = dconv_w8[:CONV_K].reshape(CONV_K, N_CHIPS, LANES).transpose(1, 0, 2).reshape(N_CHIPS * CONV_K, LANES)
    packed = jnp.concatenate(rep_rows + [jnp.zeros((pad_rep, LANES), F32), d_vg.reshape(SUBLANES, LANES),
                                         d_vb.reshape(SUBLANES, LANES), dconv_w, jnp.zeros((4, LANES), F32)], axis=0)
    small_h, small_tok = _split_start([packed], "gather8", "gather_small_start")

    chip_blk = lambda g: lax.dynamic_index_in_dim(g, chip, 0, keepdims=False)
    mine_early = [_with_own(r, chip_blk(g)) for r, g in
                  zip(_split_wait(early_h, "scatter_early_wait", small_tok), sent_early)]
    swap_h, swap_tok = _split_start(mine_early, "swap2", "swap_early_start")
    dw_qkv = _mm_tn(x2d, dqkv, "even_dw_qkv", tn=QKV // 2, out_dtype=F32, after=swap_tok)
    dw_bch = _mm_tn(x2d, dbch, "even_dw_bch", tn=BCH // 2, out_dtype=F32)
    dw_f = _mm_tn(x2d, dfl, "even_dw_f", tn=LANES, out_dtype=F32)
    gi_even = _by_owner_cols(jnp.concatenate([dw_qkv, dw_f[:, :FOX_HEADS], dw_bch], axis=1).astype(BF16))
    sent_late = [gi_even, go_even]
    late_h, late_tok = _split_start(sent_late, "scatter4", "scatter_late_start")
    grad_x = _mm_nt([(dqkv, 0, 0, QKV), (dbch, 0, QKV, QKV + BCH), (dfl, 0, QKV + BCH, EVEN_IN_PAD)], [w_even_in],
                    "even_dx", res=dz1, after=late_tok)
    mine_late = [_with_own(r, chip_blk(g)) for r, g in zip(_split_wait(late_h, "scatter_late_wait", grad_x), sent_late)]
    theirs_late = _exchange(mine_late, "swap2", "swap_late")
    theirs_early = _split_wait(swap_h, "swap_early_wait", theirs_late[0])
    (gathered,) = _split_wait(small_h, "gather_small_wait", theirs_early[0])
    gathered = lax.dynamic_update_index_in_dim(gathered, packed, 4 * lax.axis_index("x") + 2 * lax.axis_index("y")
                                               + lax.axis_index("c"), 0)
    mine, theirs = mine_late + mine_early, theirs_late + theirs_early
    big_w = [(even_w_in, m_even_w_in, v_even_w_in), (even_w_out, m_even_w_out, v_even_w_out),
             (odd_w_in, m_odd_w_in, v_odd_w_in), (odd_w_out, m_odd_w_out, v_odd_w_out)]
    big_names = ["even_w_in", "even_w_out", "odd_w_in", "odd_w_out"]
    res = {}
    for nm, own, sib, (w, m, v) in zip(big_names, mine, theirs, big_w):
        res[nm] = _adamw([own, sib], w, m, v, f"adamw_{nm}")
    for nm, at, (w, m, v) in (("ffn_w_in", 4, (ffn_w_in, m_ffn_w_in, v_ffn_w_in)),
                              ("ffn_w_out", 6, (ffn_w_out, m_ffn_w_out, v_ffn_w_out))):
        upper = _adamw([mine[at + 1], theirs[at + 1]], w, m, v, f"adamw_{nm}_1", layer=1)
        res[nm] = _adamw([mine[at], theirs[at]], w, m, v, f"adamw_{nm}_0", layer=0, into=upper)

    base = n_rep + pad_rep
    own_rows = jnp.concatenate([
        lax.dynamic_slice_in_dim(gathered, base + 2 * chip, 2, axis=1),
        lax.dynamic_slice_in_dim(gathered, base + SUBLANES + 2 * chip, 2, axis=1),
        lax.dynamic_slice_in_dim(gathered, base + 2 * SUBLANES + CONV_K * chip, CONV_K, axis=1),
        jnp.zeros((N_DEV, 1, LANES), F32)], axis=1)
    small_parts = jnp.concatenate([gathered[:, :base], own_rows], axis=1)[:, None]

    def pack_small(get):
        rows = [_to_rows(get(tw)) for tw in rep_w] + [jnp.zeros((pad_rep, LANES), F32)]
        rows += [get(sh).reshape(-1, LANES) for sh in ((odd_v_ln_g, m_odd_v_ln_g, v_odd_v_ln_g),
                                                       (odd_v_ln_b, m_odd_v_ln_b, v_odd_v_ln_b),
                                                       (even_conv_w, m_even_conv_w, v_even_conv_w))]
        return jnp.concatenate(rows + [jnp.zeros((1, LANES), F32)], axis=0)[None]

    small_out = _adamw([small_parts], pack_small(lambda tw: tw[0]), pack_small(lambda tw: tw[1]),
                       pack_small(lambda tw: tw[2]), "adamw_small")

    def unpack_small(rows3):
        rows = rows3[0]
        out, off = {}, 0
        for nm, (w, _, _), r in zip(rep_names, rep_w, rep_rows):
            out[nm] = rows[off:off + r.shape[0]].reshape(-1)[:w.size].reshape(w.shape)
            off += r.shape[0]
        off += pad_rep
        out["odd_v_ln_g"] = rows[off:off + 2].reshape(odd_v_ln_g.shape)
        out["odd_v_ln_b"] = rows[off + 2:off + 4].reshape(odd_v_ln_b.shape)
        out["even_conv_w"] = rows[off + 4:off + 4 + CONV_K].reshape(even_conv_w.shape)
        return out

    small = [unpack_small(o) for o in small_out]
    order = ["even_w_in", "even_b_f", "even_conv_w", "even_w_out", "odd_w_in", "odd_v_ln_g", "odd_v_ln_b", "odd_w_s",
             "odd_b_s", "odd_w_out", "mix_ln_g", "mix_ln_b", "ffn_w_in", "ffn_w_out", "ffn_ln_g", "ffn_ln_b"]
    outs = [loss, grad_x[None]]
    for kind in range(4):
        for nm in order:
            outs.append(res[nm][kind] if nm in res else small[kind][nm])
    return tuple(outs)
```

```python
import functools
import math

import jax
import jax.numpy as jnp
from jax import lax
from jax.experimental import pallas as pl
from jax.experimental.pallas import tpu as pltpu

F32 = jnp.float32
BF16 = jnp.bfloat16

D_MODEL = 1024
FOX_HEADS = 8
HEAD_DIM = 64
HEAD_PAIRS = FOX_HEADS // 2
FOX_WIDTH = FOX_HEADS * HEAD_DIM
CONV_WIDTH = 512
CONV_K = 3
QKV = 3 * FOX_WIDTH
BCH = 3 * CONV_WIDTH
EVEN_IN = QKV + FOX_HEADS + BCH
EVEN_IN_PAD = QKV + BCH + 128
GMLP_BLOCK = 128
GMLP_GROUPS = 8
CHUNK = 64
FFN_HIDDEN = 2816
HALF_HIDDEN = FFN_HIDDEN // 2
ALPHA = 4.0 ** 0.25
LN_EPS = 1e-5
ADAM_LR = 0.001
ADAM_B1 = 0.9
ADAM_B2 = 0.999
ADAM_EPS = 1e-08
ADAM_WD = 0.01
ADAM_STEP = 10
N_CHIPS = 4
N_DEV = 8
LANES = 128
SUBLANES = 8
ROW_TILE = 512
FFN_ROW_TILE = 512
REDUCE_TILE = 2048
ATT_BLOCK = 512
ATT_FWD_HEADS = 8
ATT_BWD_HEADS = 4
VMEM_LIMIT = 56 * 2 ** 20
NEG = -1e30
MESH = pl.DeviceIdType.MESH
HIGHEST = lax.Precision.HIGHEST
Q_C, Q_ONE, Q_LSE = 64, 67, 70
K_ONE, K_C, K_ONE2 = 64, 67, 70
V_ONE = 64
DO_DELTA = 65
NT = (((1,), (1,)), ((), ()))
TN = (((0,), (0,)), ((), ()))


def _cp():
    return pltpu.CompilerParams(vmem_limit_bytes=VMEM_LIMIT)


def _resident(shape):
    zeros = (0,) * len(shape)
    return pl.BlockSpec(shape, lambda *_: zeros, pipeline_mode=pl.Buffered(1))


def _sds(shape, dtype):
    return jax.ShapeDtypeStruct(tuple(shape), dtype)


_MASKS = {
    "gather4": [(1, 0, 0), (0, 1, 0), (1, 1, 0)],
    "scatter4": [(1, 0, 0), (0, 1, 0), (1, 1, 0)],
    "swap2": [(0, 0, 1)],
    "gather8": [(0, 0, 1), (0, 1, 0), (0, 1, 1), (1, 0, 0), (1, 0, 1), (1, 1, 0), (1, 1, 1)],
}


def _exchange(arrs, mode, name):
    n = len(arrs)
    masks = _MASKS[mode]
    npeer = len(masks)
    lead = {"gather4": N_CHIPS, "gather8": N_DEV}.get(mode)
    out_shapes = [_sds(((lead,) if lead else ()) + a.shape, a.dtype) for a in arrs]

    def body(*refs):
        ins, outs = refs[:n], refs[n:2 * n]
        send_sems, recv_sems, loc_sems = refs[2 * n:]
        x, y, c = lax.axis_index("x"), lax.axis_index("y"), lax.axis_index("c")
        chip, dev = 2 * x + y, 4 * x + 2 * y + c
        sends, recvs, locs = [], [], []
        for k in range(n):
            if mode == "gather4":
                locs.append(pltpu.make_async_copy(ins[k], outs[k].at[chip], loc_sems.at[k]))
            elif mode == "scatter4":
                locs.append(pltpu.make_async_copy(ins[k].at[chip], outs[k].at[chip], loc_sems.at[k]))
            elif mode == "gather8":
                locs.append(pltpu.make_async_copy(ins[k], outs[k].at[dev], loc_sems.at[k]))
        for cp in locs:
            cp.start()
        for k in range(n):
            for j, (dx, dy, dc) in enumerate(masks):
                px = 1 - x if dx else x
                py = 1 - y if dy else y
                pc = 1 - c if dc else c
                pchip, pdev = 2 * px + py, 4 * px + 2 * py + pc
                if mode == "gather4":
                    src, dst, land = ins[k], outs[k].at[chip], outs[k].at[pchip]
                elif mode == "scatter4":
                    src, dst, land = ins[k].at[pchip], outs[k].at[chip], outs[k].at[pchip]
                elif mode == "swap2":
                    src, dst, land = ins[k], outs[k], outs[k]
                else:
                    src, dst, land = ins[k], outs[k].at[dev], outs[k].at[pdev]
                s = k * npeer + j
                kw = dict(send_sem=send_sems.at[s], recv_sem=recv_sems.at[s], device_id=(px, py, pc),
                          device_id_type=MESH)
                cp = pltpu.make_async_remote_copy(src_ref=src, dst_ref=dst, **kw)
                cp.start()
                sends.append(cp)
                recvs.append(pltpu.make_async_remote_copy(src_ref=src, dst_ref=land, **kw))
        for cp in recvs:
            cp.wait_recv()
        for cp in sends:
            cp.wait_send()
        for cp in locs:
            cp.wait()

    any_spec = pl.BlockSpec(memory_space=pl.ANY)
    outs = pl.pallas_call(
        body,
        out_shape=out_shapes,
        in_specs=[any_spec] * n,
        out_specs=[any_spec] * n,
        scratch_shapes=[pltpu.SemaphoreType.DMA((n * npeer,)), pltpu.SemaphoreType.DMA((n * npeer,)),
                        pltpu.SemaphoreType.DMA((max(n, 1),))],
        name=name,
    )(*arrs)
    return list(outs)


_HBM_SPEC = pl.BlockSpec(memory_space=pltpu.HBM)
_SEM_SPEC = pl.BlockSpec(memory_space=pltpu.SEMAPHORE)
_ANY_SPEC = pl.BlockSpec(memory_space=pl.ANY)
_EFFECT = pltpu.SideEffectType.DATAFLOW_SIDE_EFFECTING


def _split_copies(mode, ins, lands, send_sems, recv_sems):
    x, y, c = lax.axis_index("x"), lax.axis_index("y"), lax.axis_index("c")
    chip, dev = 2 * x + y, 4 * x + 2 * y + c
    masks = _MASKS[mode]
    out = []
    for k in range(len(ins)):
        for j, (dx, dy, dc) in enumerate(masks):
            px = 1 - x if dx else x
            py = 1 - y if dy else y
            pc = 1 - c if dc else c
            pchip, pdev = 2 * px + py, 4 * px + 2 * py + pc
            if mode == "gather4":
                src, dst, land = ins[k], lands[k].at[chip], lands[k].at[pchip]
            elif mode == "scatter4":
                src, dst, land = ins[k].at[pchip], lands[k].at[chip], lands[k].at[pchip]
            elif mode == "swap2":
                src, dst, land = ins[k], lands[k], lands[k]
            else:
                src, dst, land = ins[k], lands[k].at[dev], lands[k].at[pdev]
            s = k * len(masks) + j
            kw = dict(send_sem=send_sems.at[s], recv_sem=recv_sems.at[s], device_id=(px, py, pc), device_id_type=MESH)
            out.append((pltpu.make_async_remote_copy(src_ref=src, dst_ref=dst, **kw),
                        pltpu.make_async_remote_copy(src_ref=src, dst_ref=land, **kw)))
    return out


def _split_start(arrs, mode, name, after=None):
    n = len(arrs)
    nsem = n * len(_MASKS[mode])
    lead = {"gather4": (N_CHIPS,), "gather8": (N_DEV,)}.get(mode, ())
    land_shapes = [lead + a.shape for a in arrs]

    def body(*refs):
        ins, lands = refs[:n], refs[n:2 * n]
        outs = refs[2 * n + (after is not None):]
        for start, _ in _split_copies(mode, ins, lands, outs[0], outs[1]):
            start.start()
        outs[-1][...] = jnp.zeros(outs[-1].shape, F32)

    srcs = [pltpu.with_memory_space_constraint(a, pltpu.HBM) for a in arrs]
    empties = [pltpu.with_memory_space_constraint(lax.empty(s, a.dtype), pltpu.HBM) for s, a in zip(land_shapes, arrs)]
    res = pl.pallas_call(
        body, name=name,
        out_shape=(pltpu.SemaphoreType.DMA((nsem,)), pltpu.SemaphoreType.DMA((nsem,)),
                   *[pltpu.HBM(a.shape, a.dtype) for a in arrs],
                   *[pltpu.HBM(s, a.dtype) for s, a in zip(land_shapes, arrs)],
                   _sds((SUBLANES, LANES), F32)),
        in_specs=[_HBM_SPEC] * (2 * n) + ([_ANY_SPEC] if after is not None else []),
        out_specs=(_SEM_SPEC, _SEM_SPEC, *[_HBM_SPEC] * (2 * n), pl.BlockSpec(memory_space=pltpu.VMEM)),
        input_output_aliases={k: 2 + k for k in range(2 * n)},
        compiler_params=pltpu.CompilerParams(has_side_effects=_EFFECT),
    )(*srcs, *empties, *([after] if after is not None else []))
    return dict(mode=mode, n=n, sems=res[:2], bufs=res[2:2 + 2 * n]), res[-1]


def _split_wait(handle, name, after):
    n, mode = handle["n"], handle["mode"]

    def body(*refs):
        ins, lands = refs[:n], refs[n:2 * n]
        send_sems, recv_sems = refs[2 * n], refs[2 * n + 1]
        for _, arrival in _split_copies(mode, ins, lands, send_sems, recv_sems):
            arrival.wait_send()
            arrival.wait_recv()

    bufs = handle["bufs"]
    res = pl.pallas_call(
        body, name=name,
        out_shape=tuple(pltpu.HBM(b.shape, b.dtype) for b in bufs),
        in_specs=[_HBM_SPEC] * (2 * n) + [_SEM_SPEC, _SEM_SPEC, _ANY_SPEC],
        out_specs=tuple([_HBM_SPEC] * (2 * n)),
        input_output_aliases={k: k for k in range(2 * n)},
        compiler_params=pltpu.CompilerParams(has_side_effects=_EFFECT),
    )(*bufs, *handle["sems"], after)
    return list(res[n:])


def _with_own(landed, own):
    chip = 2 * lax.axis_index("x") + lax.axis_index("y")
    return lax.dynamic_update_index_in_dim(landed, own, chip, 0)


def _sigmoid(x):
    return 0.5 * jnp.tanh(0.5 * x) + 0.5


def _log_sigmoid(x):
    e = jnp.exp(-jnp.abs(x))
    log1p = jnp.where(e < 1e-2, e * (1.0 - e * (0.5 - e * (1.0 / 3.0))), jnp.log(1.0 + e))
    return jnp.minimum(x, 0.0) - log1p


def _gelu(a):
    return 0.5 * a * (1.0 + lax.erf(a * (2.0 ** -0.5)))


def _gelu_grad(a):
    cdf = 0.5 * (1.0 + lax.erf(a * (2.0 ** -0.5)))
    pdf = jnp.exp(-0.5 * a * a) * (1.0 / math.sqrt(2.0 * math.pi))
    return cdf + a * pdf


def _ln_fwd(z):
    mu = jnp.mean(z, axis=-1, keepdims=True)
    zc = z - mu
    var = jnp.mean(zc * zc, axis=-1, keepdims=True)
    rstd = lax.rsqrt(var + LN_EPS)
    return zc * rstd, rstd


def _ln_bwd(dy, xhat, rstd, g):
    dxh = dy * g
    m1 = jnp.mean(dxh, axis=-1, keepdims=True)
    m2 = jnp.mean(dxh * xhat, axis=-1, keepdims=True)
    dz = rstd * (dxh - m1 - xhat * m2)
    return dz, jnp.sum(dy * xhat, axis=0, keepdims=True), jnp.sum(dy, axis=0, keepdims=True)


def _shift_down(z, halo):
    r = lax.broadcasted_iota(jnp.int32, z.shape, 0)
    z1 = jnp.where(r == 0, halo[7:8, :], pltpu.roll(z, 1, 0))
    z2 = jnp.where(r == 0, halo[6:7, :], jnp.where(r == 1, halo[7:8, :], pltpu.roll(z, 2, 0)))
    return z1, z2


def _shift_up(z, halo):
    n = z.shape[0]
    r = lax.broadcasted_iota(jnp.int32, z.shape, 0)
    z1 = jnp.where(r == n - 1, halo[0:1, :], pltpu.roll(z, n - 1, 0))
    z2 = jnp.where(r == n - 1, halo[1:2, :], jnp.where(r == n - 2, halo[0:1, :], pltpu.roll(z, n - 2, 0)))
    return z1, z2


def _accumulate(ref, first, value):
    @pl.when(first)
    def _():
        ref[...] = value

    @pl.when(jnp.logical_not(first))
    def _():
        ref[...] += value


def _proj(x, w, splits, name):
    t, k = x.shape
    tm = min(ROW_TILE, t)

    def body(x_ref, w_ref, *outs):
        a = x_ref[...].astype(BF16)
        for (lo, hi, dt), o in zip(splits, outs):
            o[...] = jnp.dot(a, w_ref[:, lo:hi], preferred_element_type=F32).astype(dt)

    return pl.pallas_call(
        body, grid=(t // tm,),
        in_specs=[pl.BlockSpec((tm, k), lambda i: (i, 0)), _resident(w.shape)],
        out_specs=[pl.BlockSpec((tm, hi - lo), lambda i: (i, 0)) for lo, hi, _ in splits],
        out_shape=[_sds((t, hi - lo), dt) for lo, hi, dt in splits],
        compiler_params=_cp(), name=name)(x, w)


def _fgate_fwd(fl3, b_f):
    nc = fl3.shape[0]

    def body(f_ref, b_ref, c_ref):
        r = lax.broadcasted_iota(jnp.int32, (LANES, LANES), 0)
        cidx = lax.broadcasted_iota(jnp.int32, (LANES, LANES), 1)
        upper = (r <= cidx).astype(F32)

        def step(i, carry):
            lf = _log_sigmoid(f_ref[i] + b_ref[...])
            cc = jnp.dot(lf, upper, precision=HIGHEST, preferred_element_type=F32) + carry
            c_ref[i] = cc
            return cc[:, LANES - 1:LANES]

        lax.fori_loop(0, nc, step, jnp.zeros((FOX_HEADS, 1), F32))

    return pl.pallas_call(body, out_shape=_sds(fl3.shape, F32), name="fgate_fwd")(fl3, b_f)


def _fgate_bwd(dc3, fl3, b_f):
    nc = fl3.shape[0]

    def body(dc_ref, f_ref, b_ref, df_ref, db_ref):
        r = lax.broadcasted_iota(jnp.int32, (LANES, LANES), 0)
        cidx = lax.broadcasted_iota(jnp.int32, (LANES, LANES), 1)
        lower = (r >= cidx).astype(F32)

        def step(n, carry):
            suffix, db = carry
            i = nc - 1 - n
            dlf = jnp.dot(dc_ref[i], lower, precision=HIGHEST, preferred_element_type=F32) + suffix
            df = dlf * (1.0 - _sigmoid(f_ref[i] + b_ref[...]))
            df_ref[i] = df
            return dlf[:, 0:1], db + jnp.sum(df, axis=1, keepdims=True)

        zero = jnp.zeros((FOX_HEADS, 1), F32)
        _, db = lax.fori_loop(0, nc, step, (zero, zero))
        db_ref[...] = db

    return pl.pallas_call(body, out_shape=[_sds(fl3.shape, F32), _sds((FOX_HEADS, 1), F32)],
                          name="fgate_bwd")(dc3, fl3, b_f)


def _split3(c):
    hi = c.astype(BF16).astype(F32)
    mid = (c - hi).astype(BF16).astype(F32)
    lo = (c - hi - mid).astype(BF16).astype(F32)
    return hi, mid, lo


def _lane_pieces(lane, start, pieces, sign):
    out = jnp.zeros(lane.shape, F32)
    for n, p in enumerate(pieces):
        out = jnp.where(lane == start + n, sign * p, out)
    return out


def _attn_pack(qkv, c_col):
    t = qkv.shape[0]
    tm = min(ROW_TILE, t)
    hd = HEAD_DIM

    def body(x_ref, c_ref, qp_ref, kp_ref, vp_ref, kt_ref, vt_ref):
        lane = lax.broadcasted_iota(jnp.int32, (tm, hd), 1) + hd
        for h in range(FOX_HEADS):
            pieces = _split3(c_ref[:, h:h + 1])
            ones = lambda a, b: jnp.where(jnp.logical_and(lane >= a, lane < b), 1.0, 0.0)
            q_extra = _lane_pieces(lane, Q_C, pieces, 1.0) + ones(Q_ONE, Q_ONE + 3)
            k_extra = _lane_pieces(lane, K_C, pieces, -1.0) + ones(K_ONE, K_ONE + 3) + ones(K_ONE2, K_ONE2 + 3)
            qp_ref[h, :, :hd] = (x_ref[:, h * hd:(h + 1) * hd].astype(F32) * (hd ** -0.5)).astype(BF16)
            qp_ref[h, :, hd:] = q_extra.astype(BF16)
            kp_ref[h, :, :hd] = x_ref[:, FOX_WIDTH + h * hd:FOX_WIDTH + (h + 1) * hd]
            kp_ref[h, :, hd:] = k_extra.astype(BF16)
            vp_ref[h, :, :hd] = x_ref[:, 2 * FOX_WIDTH + h * hd:2 * FOX_WIDTH + (h + 1) * hd]
            vp_ref[h, :, hd:] = ones(V_ONE, V_ONE + 4).astype(BF16)
            kt_ref[h] = kp_ref[h].astype(F32).T.astype(BF16)
            vt_ref[h] = vp_ref[h].astype(F32).T.astype(BF16)

    row3 = pl.BlockSpec((FOX_HEADS, tm, LANES), lambda i: (0, i, 0))
    col3 = pl.BlockSpec((FOX_HEADS, LANES, tm), lambda i: (0, 0, i))
    return pl.pallas_call(
        body, grid=(t // tm,),
        in_specs=[pl.BlockSpec((tm, QKV), lambda i: (i, 0)), pl.BlockSpec((tm, FOX_HEADS), lambda i: (i, 0))],
        out_specs=[row3, row3, row3, col3, col3],
        out_shape=[_sds((FOX_HEADS, t, LANES), BF16)] * 3 + [_sds((FOX_HEADS, LANES, t), BF16)] * 2,
        compiler_params=_cp(), name="attn_pack")(qkv, c_col)


def _triangle(nq, key_major):
    if key_major:
        pairs = [(i, j) for j in range(nq) for i in range(j, nq)]
    else:
        pairs = [(i, j) for i in range(nq) for j in range(i + 1)]
    return jnp.asarray([p[0] for p in pairs], jnp.int32), jnp.asarray([p[1] for p in pairs], jnp.int32)


def _attn_fwd(qp, kp, vt):
    t = qp.shape[1]
    bq = min(ATT_BLOCK, t)
    nq = t // bq
    nh = ATT_FWD_HEADS
    i_tab, j_tab = _triangle(nq, key_major=False)

    def body(it_ref, jt_ref, q_ref, k_ref, vt_ref, o_ref, lse_ref, m_sc, acc_sc):
        s = pl.program_id(1)
        i, j = it_ref[s], jt_ref[s]

        @pl.when(j == 0)
        def _():
            m_sc[...] = jnp.full(m_sc.shape, NEG, F32)
            acc_sc[...] = jnp.zeros(acc_sc.shape, F32)

        def sweep(masked):
            scores = lambda h: lax.dot_general(k_ref[h], q_ref[h], NT, preferred_element_type=F32)

            def accumulate(h, pt, rescale):
                acc_sc[h] = rescale * acc_sc[h] + jnp.dot(vt_ref[h], pt, preferred_element_type=F32)

            ahead, behind = scores(0), None
            for h in range(nh):
                st = ahead
                if h + 1 < nh:
                    ahead = scores(h + 1)
                if behind is not None:
                    accumulate(*behind)
                if masked:
                    key = lax.broadcasted_iota(jnp.int32, (bq, bq), 0)
                    qry = lax.broadcasted_iota(jnp.int32, (bq, bq), 1)
                    st = jnp.where(key <= qry, st, NEG)
                m_prev = m_sc[h]
                m_new = jnp.maximum(m_prev, jnp.max(st, axis=0, keepdims=True))
                behind = (h, jnp.exp(st - m_new).astype(BF16), jnp.exp(m_prev - m_new))
                m_sc[h] = m_new
            accumulate(*behind)

        @pl.when(j < i)
        def _():
            sweep(False)

        @pl.when(j == i)
        def _():
            sweep(True)
            for h in range(nh):
                acc = acc_sc[h]
                denom = acc[V_ONE:V_ONE + 1, :]
                o_ref[:, h * HEAD_DIM:(h + 1) * HEAD_DIM] = (acc[:HEAD_DIM, :] / denom).T.astype(BF16)
                lse_ref[h] = m_sc[h] + jnp.log(denom)

    grid_spec = pltpu.PrefetchScalarGridSpec(
        num_scalar_prefetch=2, grid=(FOX_HEADS // nh, i_tab.shape[0]),
        in_specs=[pl.BlockSpec((nh, bq, LANES), lambda hp, s, it, jt: (hp, it[s], 0)),
                  pl.BlockSpec((nh, bq, LANES), lambda hp, s, it, jt: (hp, jt[s], 0)),
                  pl.BlockSpec((nh, LANES, bq), lambda hp, s, it, jt: (hp, 0, jt[s]))],
        out_specs=[pl.BlockSpec((bq, nh * HEAD_DIM), lambda hp, s, it, jt: (it[s], hp)),
                   pl.BlockSpec((nh, 1, bq), lambda hp, s, it, jt: (hp, 0, it[s]))],
        scratch_shapes=[pltpu.VMEM((nh, 1, bq), F32), pltpu.VMEM((nh, LANES, bq), F32)])
    return pl.pallas_call(body, grid_spec=grid_spec,
                          out_shape=[_sds((t, FOX_WIDTH), BF16), _sds((FOX_HEADS, 1, t), F32)],
                          compiler_params=_cp(), name="attn_fwd")(i_tab, j_tab, qp, kp, vt)


def _conv_fwd(bch, conv_w):
    t = bch.shape[0]
    tm = min(ROW_TILE, t)
    halo_blocks = tm // SUBLANES
    cw = CONV_WIDTH

    def body(cur_ref, prev_ref, w_ref, o_ref):
        i = pl.program_id(0)
        z = cur_ref[:, cw:2 * cw] * cur_ref[:, 2 * cw:]
        zp = jnp.where(i == 0, 0.0, prev_ref[:, cw:2 * cw] * prev_ref[:, 2 * cw:])
        z1, z2 = _shift_down(z, zp)
        y = w_ref[0:1, :] * z2 + w_ref[1:2, :] * z1 + w_ref[2:3, :] * z
        o_ref[...] = (cur_ref[:, :cw] * y).astype(BF16)

    return pl.pallas_call(
        body, grid=(t // tm,),
        in_specs=[pl.BlockSpec((tm, BCH), lambda i: (i, 0)),
                  pl.BlockSpec((SUBLANES, BCH), lambda i: (jnp.maximum(i * halo_blocks - 1, 0), 0)),
                  _resident(conv_w.shape)],
        out_specs=pl.BlockSpec((tm, cw), lambda i: (i, 0)),
        out_shape=_sds((t, cw), BF16), compiler_params=_cp(), name="conv_fwd")(bch, bch, conv_w)


def _mm_res_ln(pairs, res, g, b, name):
    from_ln = isinstance(res, tuple)
    res_args = list(res) if from_ln else [res]
    t, d = res_args[0].shape
    tm = min(ROW_TILE, t)
    n = len(pairs)

    def body(*refs):
        a_refs, w_refs = refs[:n], refs[n:2 * n]
        res_refs = refs[2 * n:2 * n + len(res_args)]
        g_ref, b_ref, yb_ref, xh_ref, rs_ref = refs[2 * n + len(res_args):]
        r = res_refs[0][...]
        if from_ln:
            r = r * res_refs[1][...] + res_refs[2][...]
        z = ALPHA * r
        for a_ref, w_ref in zip(a_refs, w_refs):
            z = z + jnp.dot(a_ref[...].astype(BF16), w_ref[...], preferred_element_type=F32)
        xhat, rstd = _ln_fwd(z)
        yb_ref[...] = (xhat * g_ref[...] + b_ref[...]).astype(BF16)
        xh_ref[...] = xhat
        rs_ref[...] = rstd

    row = lambda i: (i, 0)
    full = pl.BlockSpec((tm, d), row)
    return pl.pallas_call(
        body, grid=(t // tm,),
        in_specs=[pl.BlockSpec((tm, a.shape[1]), row) for a, _ in pairs] + [_resident(w.shape) for _, w in pairs]
        + [full] + [_resident(a.shape) for a in res_args[1:]] + [_resident(g.shape), _resident(b.shape)],
        out_specs=[full, full, pl.BlockSpec((tm, 1), row)],
        out_shape=[_sds((t, d), BF16), _sds((t, d), F32), _sds((t, 1), F32)],
        compiler_params=_cp(), name=name)(*[a for a, _ in pairs], *[w for _, w in pairs], *res_args, g, b)


def _ffn_in(x, wi, name):
    t, d = x.shape
    tm = min(FFN_ROW_TILE, t)
    hh = HALF_HIDDEN

    def body(x_ref, w_ref, gu_ref, h_ref):
        a = x_ref[...].astype(BF16)
        for c in range(2):
            gs, us = slice(c * hh, (c + 1) * hh), slice(FFN_HIDDEN + c * hh, FFN_HIDDEN + (c + 1) * hh)
            g = jnp.dot(a, w_ref[c], preferred_element_type=F32)
            u = jnp.dot(a, w_ref[2 + c], preferred_element_type=F32)
            sig = _sigmoid(g)
            silu = g * sig
            gu_ref[:, gs] = (u * sig * (1.0 + g * (1.0 - sig))).astype(BF16)
            gu_ref[:, us] = silu.astype(BF16)
            h_ref[:, gs] = (silu * u).astype(BF16)

    row = lambda i: (i, 0)
    return pl.pallas_call(
        body, grid=(t // tm,),
        in_specs=[pl.BlockSpec((tm, d), row), _resident(wi.shape)],
        out_specs=[pl.BlockSpec((tm, 2 * FFN_HIDDEN), row), pl.BlockSpec((tm, FFN_HIDDEN), row)],
        out_shape=[_sds((t, 2 * FFN_HIDDEN), BF16), _sds((t, FFN_HIDDEN), BF16)],
        compiler_params=_cp(), name=name)(x, wi)


def _gmlp_fwd(x, w_in, vg, vb, wm, bs_col):
    t, d = x.shape
    tm = min(ROW_TILE, t)
    gb = GMLP_BLOCK

    def body(x_ref, w_ref, vg_ref, vb_ref, wm_ref, bs_ref, a_ref, o_ref):
        xb = x_ref[...].astype(BF16)
        nc = w_ref.shape[2]
        for j in range(w_ref.shape[0]):
            a_ref[:, j * nc:(j + 1) * nc] = jnp.dot(xb, w_ref[j], preferred_element_type=F32)
        u = _gelu(a_ref[:, :d])
        vhat, _ = _ln_fwd(_gelu(a_ref[:, d:]))
        vln = (vhat * vg_ref[...] + vb_ref[...]).astype(BF16)
        for blk in range(tm // gb):
            rs = slice(blk * gb, (blk + 1) * gb)
            for gi in range(GMLP_GROUPS):
                cs = slice(gi * gb, (gi + 1) * gb)
                s = jnp.dot(wm_ref[gi], vln[rs, cs], preferred_element_type=F32) + bs_ref[:, gi:gi + 1]
                o_ref[rs, cs] = (u[rs, cs] * s).astype(BF16)

    row = lambda i: (i, 0)
    return pl.pallas_call(
        body, grid=(t // tm,),
        in_specs=[pl.BlockSpec((tm, d), row), _resident(w_in.shape), _resident(vg.shape), _resident(vb.shape),
                  _resident(wm.shape), _resident(bs_col.shape)],
        out_specs=[pl.BlockSpec((tm, 2 * d), row), pl.BlockSpec((tm, d), row)],
        out_shape=[_sds((t, 2 * d), F32), _sds((t, d), BF16)],
        compiler_params=_cp(), name="gmlp_fwd")(x, w_in, vg, vb, wm, bs_col)


def _loss_ln_bwd(xhat, rstd, g, b, target):
    t, d = xhat.shape
    tm = min(ROW_TILE, t)

    def body(xh_ref, rs_ref, g_ref, b_ref, t_ref, sq_ref, dz_ref, dg_ref, db_ref):
        first = pl.program_id(0) == 0
        xh = xh_ref[...]
        err = xh * g_ref[...] + b_ref[...] - t_ref[...]
        dz, dg, db = _ln_bwd(err * (1.0 / d), xh, rs_ref[...], g_ref[...])
        dz_ref[...] = dz
        _accumulate(sq_ref, first, jnp.sum(err * err, axis=0, keepdims=True))
        _accumulate(dg_ref, first, dg)
        _accumulate(db_ref, first, db)

    row = lambda i: (i, 0)
    vec = pl.BlockSpec((1, d), lambda i: (0, 0))
    return pl.pallas_call(
        body, grid=(t // tm,),
        in_specs=[pl.BlockSpec((tm, d), row), pl.BlockSpec((tm, 1), row), _resident(g.shape), _resident(b.shape),
                  pl.BlockSpec((tm, d), row)],
        out_specs=[vec, pl.BlockSpec((tm, d), row), vec, vec],
        out_shape=[_sds((1, d), F32), _sds((t, d), F32), _sds((1, d), F32), _sds((1, d), F32)],
        compiler_params=_cp(), name="loss_ln_bwd")(xhat, rstd, g, b, target)


def _mm_nt(pairs, ws, name, *, tm=ROW_TILE, res=None, ln=None, out_dtype=F32, after=None):
    t = pairs[0][0].shape[0]
    k = ws[0].shape[-2]
    tm = min(tm, t)
    n, nw = len(pairs), len(ws)

    def body(*refs):
        refs = refs[after is not None:]
        a_refs, w_refs = refs[:n], refs[n:n + nw]
        rest = list(refs[n + nw:])
        dx = None
        for a_ref, (_, wi, lo, hi) in zip(a_refs, pairs):
            w_ref = w_refs[wi]
            if len(w_ref.shape) == 3:
                nc = w_ref.shape[2]
                parts = [lax.dot_general(a_ref[:, j * nc:(j + 1) * nc].astype(BF16), w_ref[j], NT,
                                         preferred_element_type=F32) for j in range(w_ref.shape[0])]
            else:
                parts = [lax.dot_general(a_ref[...].astype(BF16), w_ref[:, lo:hi], NT, preferred_element_type=F32)]
            for part in parts:
                dx = part if dx is None else dx + part
        if res is not None:
            dx = dx + ALPHA * rest.pop(0)[...]
        if ln is None:
            rest[0][...] = dx.astype(out_dtype)
            return
        xh_ref, rs_ref, g_ref, dz_ref, dg_ref, db_ref = rest
        first = pl.program_id(0) == 0
        dz, dg, db = _ln_bwd(dx, xh_ref[...], rs_ref[...], g_ref[...])
        dz_ref[...] = dz
        _accumulate(dg_ref, first, dg)
        _accumulate(db_ref, first, db)

    row = lambda i: (i, 0)
    in_specs = [pl.BlockSpec((tm, a.shape[1]), row) for a, _, _, _ in pairs] + [_resident(w.shape) for w in ws]
    args = [a for a, _, _, _ in pairs] + list(ws)
    if res is not None:
        in_specs.append(pl.BlockSpec((tm, k), row))
        args.append(res)
    if ln is None:
        out_specs = pl.BlockSpec((tm, k), row)
        out_shape = _sds((t, k), out_dtype)
    else:
        xhat, rstd, g = ln
        in_specs += [pl.BlockSpec((tm, k), row), pl.BlockSpec((tm, 1), row), _resident(g.shape)]
        args += [xhat, rstd, g]
        vec = pl.BlockSpec((1, k), lambda i: (0, 0))
        out_specs = [pl.BlockSpec((tm, k), row), vec, vec]
        out_shape = [_sds((t, k), F32), _sds((1, k), F32), _sds((1, k), F32)]
    if after is not None:
        in_specs.insert(0, _ANY_SPEC)
        args.insert(0, after)
    return pl.pallas_call(body, grid=(t // tm,), in_specs=in_specs, out_specs=out_specs, out_shape=out_shape,
                          compiler_params=_cp(), name=name)(*args)


def _mm_tn(a, b, name, *, tn, tk=None, tt=None, stack_cols=False, out_dtype=BF16, after=None):
    t, k = a.shape
    n = b.shape[1]
    tk = k if tk is None else tk
    tt = min(REDUCE_TILE if tt is None else tt, t)
    nt = t // tt

    def body(a_ref, b_ref, *rest):
        o_ref, acc_ref = rest[after is not None:]
        s = pl.program_id(2)
        part = lax.dot_general(a_ref[...].astype(BF16), b_ref[...].astype(BF16), TN, preferred_element_type=F32)
        _accumulate(acc_ref, s == 0, part)

        @pl.when(s == nt - 1)
        def _():
            o_ref[...] = acc_ref[...].astype(out_dtype).reshape(o_ref.shape)

    if stack_cols:
        assert tk == k
        out_spec = pl.BlockSpec((1, k, tn), lambda kk, j, s: (j, 0, 0))
        out_shape = _sds((n // tn, k, tn), out_dtype)
    else:
        out_spec = pl.BlockSpec((tk, tn), lambda kk, j, s: (kk, j))
        out_shape = _sds((k, n), out_dtype)
    return pl.pallas_call(
        body, grid=(k // tk, n // tn, nt),
        in_specs=[pl.BlockSpec((tt, tk), lambda kk, j, s: (s, kk)), pl.BlockSpec((tt, tn), lambda kk, j, s: (s, j))]
        + ([_ANY_SPEC] if after is not None else []),
        out_specs=out_spec, out_shape=out_shape,
        scratch_shapes=[pltpu.VMEM((tk, tn), F32)],
        compiler_params=_cp(), name=name)(a, b, *([after] if after is not None else []))


def _ffn_bwd_hidden(dz, wo, gu, name):
    t, d = dz.shape
    tm = min(FFN_ROW_TILE, t)
    hh = HALF_HIDDEN

    def body(dz_ref, w_ref, gu_ref, o_ref):
        a = dz_ref[...].astype(BF16)
        for c in range(2):
            gs, us = slice(c * hh, (c + 1) * hh), slice(FFN_HIDDEN + c * hh, FFN_HIDDEN + (c + 1) * hh)
            dh = lax.dot_general(a, w_ref[gs, :], NT, preferred_element_type=F32)
            o_ref[:, gs] = (dh * gu_ref[:, gs].astype(F32)).astype(BF16)
            o_ref[:, us] = (dh * gu_ref[:, us].astype(F32)).astype(BF16)

    row = lambda i: (i, 0)
    return pl.pallas_call(
        body, grid=(t // tm,),
        in_specs=[pl.BlockSpec((tm, d), row), _resident(wo.shape), pl.BlockSpec((tm, 2 * FFN_HIDDEN), row)],
        out_specs=pl.BlockSpec((tm, 2 * FFN_HIDDEN), row),
        out_shape=_sds((t, 2 * FFN_HIDDEN), BF16), compiler_params=_cp(), name=name)(dz, wo, gu)


def _gmlp_bwd(dgated, a, vg, vb, wm, bs_col):
    t, d2 = a.shape
    d = d2 // 2
    tm = min(ROW_TILE, t)
    gb = GMLP_BLOCK

    def body(dg_ref, a_ref, vg_ref, vb_ref, wm_ref, bs_ref, da_ref, dws_ref, dbs_ref, dvg_ref, dvb_ref, dvln_sc):
        first = pl.program_id(0) == 0
        au, av = a_ref[:, :d], a_ref[:, d:]
        u = _gelu(au)
        vhat, rstd = _ln_fwd(_gelu(av))
        vln = (vhat * vg_ref[...] + vb_ref[...]).astype(BF16)
        dgate = dg_ref[...]

        @pl.when(first)
        def _():
            dws_ref[...] = jnp.zeros(dws_ref.shape, F32)
            dbs_ref[...] = jnp.zeros(dbs_ref.shape, F32)

        for blk in range(tm // gb):
            rs = slice(blk * gb, (blk + 1) * gb)
            for gi in range(GMLP_GROUPS):
                cs = slice(gi * gb, (gi + 1) * gb)
                vblk = vln[rs, cs]
                s = jnp.dot(wm_ref[gi], vblk, preferred_element_type=F32) + bs_ref[:, gi:gi + 1]
                dgb = dgate[rs, cs]
                da_ref[rs, cs] = (dgb * s * _gelu_grad(au[rs, cs])).astype(BF16)
                ds = dgb * u[rs, cs]
                dsb = ds.astype(BF16)
                dws_ref[gi] += lax.dot_general(dsb, vblk, NT, preferred_element_type=F32)
                dbs_ref[:, gi:gi + 1] += jnp.sum(ds, axis=1, keepdims=True)
                dvln_sc[rs, cs] = lax.dot_general(wm_ref[gi], dsb, TN, preferred_element_type=F32)
        dv, dvg, dvb = _ln_bwd(dvln_sc[...], vhat, rstd, vg_ref[...])
        da_ref[:, d:] = (dv * _gelu_grad(av)).astype(BF16)
        _accumulate(dvg_ref, first, dvg)
        _accumulate(dvb_ref, first, dvb)

    row = lambda i: (i, 0)
    vec = pl.BlockSpec((1, d), lambda i: (0, 0))
    return pl.pallas_call(
        body, grid=(t // tm,),
        in_specs=[pl.BlockSpec((tm, d), row), pl.BlockSpec((tm, d2), row), _resident(vg.shape), _resident(vb.shape),
                  _resident(wm.shape), _resident(bs_col.shape)],
        out_specs=[pl.BlockSpec((tm, d2), row), pl.BlockSpec(wm.shape, lambda i: (0, 0, 0)),
                   pl.BlockSpec(bs_col.shape, lambda i: (0, 0)), vec, vec],
        out_shape=[_sds((t, d2), BF16), _sds(wm.shape, F32), _sds(bs_col.shape, F32), _sds((1, d), F32), _sds((1, d), F32)],
        scratch_shapes=[pltpu.VMEM((tm, d), F32)],
        compiler_params=_cp(), name="gmlp_bwd")(dgated, a, vg, vb, wm, bs_col)


def _conv_bwd(bch, dmix, conv_w):
    t = bch.shape[0]
    tm = min(ROW_TILE, t)
    nb = t // tm
    halo_blocks = tm // SUBLANES
    cw = CONV_WIDTH

    def body(cur_ref, prev_ref, next_ref, dc_ref, dn_ref, w_ref, o_ref, dw_ref):
        i = pl.program_id(0)
        bgate, cgate, hval = cur_ref[:, :cw], cur_ref[:, cw:2 * cw], cur_ref[:, 2 * cw:]
        z = cgate * hval
        zp = jnp.where(i == 0, 0.0, prev_ref[:, cw:2 * cw] * prev_ref[:, 2 * cw:])
        z1, z2 = _shift_down(z, zp)
        w0, w1, w2 = w_ref[0:1, :], w_ref[1:2, :], w_ref[2:3, :]
        dconv = dc_ref[...]
        o_ref[:, :cw] = (dconv * (w0 * z2 + w1 * z1 + w2 * z)).astype(BF16)
        dy = dconv * bgate
        dyn = jnp.where(i == nb - 1, 0.0, dn_ref[...] * next_ref[:, :cw])
        dy1, dy2 = _shift_up(dy, dyn)
        dz = w2 * dy + w1 * dy1 + w0 * dy2
        o_ref[:, cw:2 * cw] = (dz * hval).astype(BF16)
        o_ref[:, 2 * cw:] = (dz * cgate).astype(BF16)

        @pl.when(i == 0)
        def _():
            dw_ref[...] = jnp.zeros(dw_ref.shape, F32)

        for tap, zs in enumerate((z2, z1, z)):
            dw_ref[tap:tap + 1, :] += jnp.sum(dy * zs, axis=0, keepdims=True)

    last_halo = t // SUBLANES - 1
    return pl.pallas_call(
        body, grid=(nb,),
        in_specs=[pl.BlockSpec((tm, BCH), lambda i: (i, 0)),
                  pl.BlockSpec((SUBLANES, BCH), lambda i: (jnp.maximum(i * halo_blocks - 1, 0), 0)),
                  pl.BlockSpec((SUBLANES, BCH), lambda i: (jnp.minimum((i + 1) * halo_blocks, last_halo), 0)),
                  pl.BlockSpec((tm, cw), lambda i: (i, 1)),
                  pl.BlockSpec((SUBLANES, cw), lambda i: (jnp.minimum((i + 1) * halo_blocks, last_halo), 1)),
                  _resident(conv_w.shape)],
        out_specs=[pl.BlockSpec((tm, BCH), lambda i: (i, 0)), pl.BlockSpec((SUBLANES, cw), lambda i: (0, 0))],
        out_shape=[_sds((t, BCH), BF16), _sds((SUBLANES, cw), F32)],
        compiler_params=_cp(), name="conv_bwd")(bch, bch, bch, dmix, dmix, conv_w)


def _attn_bwd_prep(o, dmix, qp, lse_col):
    t = o.shape[0]
    tm = min(ROW_TILE, t)
    hd = HEAD_DIM

    def body(o_ref, do_ref, qp_ref, lse_ref, qb_ref, dob_ref):
        lane = lax.broadcasted_iota(jnp.int32, (tm, hd), 1) + hd
        for h in range(FOX_HEADS):
            do = do_ref[:, h * hd:(h + 1) * hd]
            delta = jnp.sum(o_ref[:, h * hd:(h + 1) * hd].astype(F32) * do, axis=-1, keepdims=True)
            dob_ref[h, :, :hd] = do.astype(BF16)
            dob_ref[h, :, hd:] = _lane_pieces(lane, DO_DELTA, _split3(delta), -1.0).astype(BF16)
            qb_ref[h, :, :hd] = qp_ref[h, :, :hd]
            qb_ref[h, :, hd:] = (qp_ref[h, :, hd:].astype(F32)
                                 + _lane_pieces(lane, Q_LSE, _split3(lse_ref[:, h:h + 1]), -1.0)).astype(BF16)

    row3 = pl.BlockSpec((FOX_HEADS, tm, LANES), lambda i: (0, i, 0))
    return pl.pallas_call(
        body, grid=(t // tm,),
        in_specs=[pl.BlockSpec((tm, FOX_WIDTH), lambda i: (i, 0)), pl.BlockSpec((tm, FOX_WIDTH), lambda i: (i, 0)), row3,
                  pl.BlockSpec((tm, FOX_HEADS), lambda i: (i, 0))],
        out_specs=[row3, row3], out_shape=[_sds((FOX_HEADS, t, LANES), BF16)] * 2,
        compiler_params=_cp(), name="attn_bwd_prep")(o, dmix, qp, lse_col)


def _attn_bwd(qb, kp, vp, dob, kt):
    t = qb.shape[1]
    bq = min(ATT_BLOCK, t)
    nq = t // bq
    i_tab, j_tab = _triangle(nq, key_major=True)

    def body(it_ref, jt_ref, q_ref, k_ref, v_ref, do_ref, kt_ref, dqt_ref, dk_ref, dv_ref, dk_sc, dv_sc):
        s = pl.program_id(1)
        i, j = it_ref[s], jt_ref[s]

        @pl.when(s == 0)
        def _():
            dqt_ref[...] = jnp.zeros(dqt_ref.shape, F32)

        @pl.when(i == j)
        def _():
            dk_sc[...] = jnp.zeros(dk_sc.shape, F32)
            dv_sc[...] = jnp.zeros(dv_sc.shape, F32)

        cols = pl.ds(pl.multiple_of(i * bq, bq), bq)

        def sweep(masked):
            def scores(h):
                return (lax.dot_general(k_ref[h], q_ref[h], NT, preferred_element_type=F32),
                        lax.dot_general(v_ref[h], do_ref[h], NT, preferred_element_type=F32))

            def accumulate(h, ptb, dstb):
                dv_sc[h] += jnp.dot(ptb, do_ref[h], preferred_element_type=F32)
                dk_sc[h] += jnp.dot(dstb, q_ref[h], preferred_element_type=F32)
                dqt_ref[h, :, cols] += jnp.dot(kt_ref[h], dstb, preferred_element_type=F32)

            ahead, behind = scores(0), None
            for h in range(ATT_BWD_HEADS):
                st, dpt = ahead
                if h + 1 < ATT_BWD_HEADS:
                    ahead = scores(h + 1)
                if behind is not None:
                    accumulate(*behind)
                if masked:
                    key = lax.broadcasted_iota(jnp.int32, (bq, bq), 0)
                    qry = lax.broadcasted_iota(jnp.int32, (bq, bq), 1)
                    st = jnp.where(key <= qry, st, NEG)
                pt = jnp.exp(st)
                behind = (h, pt.astype(BF16), (pt * dpt).astype(BF16))
            accumulate(*behind)

        @pl.when(i == j)
        def _():
            sweep(True)

        @pl.when(i > j)
        def _():
            sweep(False)

        @pl.when(i == nq - 1)
        def _():
            dk_ref[...] = dk_sc[...]
            dv_ref[...] = dv_sc[...].astype(BF16)

    nh = ATT_BWD_HEADS
    qblk = pl.BlockSpec((nh, bq, LANES), lambda hp, s, it, jt: (hp, it[s], 0))
    kblk = pl.BlockSpec((nh, bq, LANES), lambda hp, s, it, jt: (hp, jt[s], 0))
    grid_spec = pltpu.PrefetchScalarGridSpec(
        num_scalar_prefetch=2, grid=(FOX_HEADS // nh, i_tab.shape[0]),
        in_specs=[qblk, kblk, kblk, qblk, pl.BlockSpec((nh, LANES, bq), lambda hp, s, it, jt: (hp, 0, jt[s]))],
        out_specs=[pl.BlockSpec((nh, LANES, t), lambda hp, s, it, jt: (hp, 0, 0), pipeline_mode=pl.Buffered(1)),
                   kblk, kblk],
        scratch_shapes=[pltpu.VMEM((nh, bq, LANES), F32), pltpu.VMEM((nh, bq, LANES), F32)])
    return pl.pallas_call(body, grid_spec=grid_spec,
                          out_shape=[_sds((FOX_HEADS, LANES, t), F32), _sds((FOX_HEADS, t, LANES), F32),
                                     _sds((FOX_HEADS, t, LANES), BF16)],
                          compiler_params=_cp(), name="attn_bwd")(i_tab, j_tab, qb, kp, vp, dob, kt)


def _attn_unpack(dqt, dkp, dvp):
    t = dkp.shape[1]
    tm = min(ROW_TILE, t)
    hd = HEAD_DIM

    def body(dqt_ref, dk_ref, dv_ref, o_ref, dc_ref):
        for h in range(FOX_HEADS):
            dq = dqt_ref[h].T
            o_ref[:, h * hd:(h + 1) * hd] = (dq[:, :hd] * (hd ** -0.5)).astype(BF16)
            o_ref[:, FOX_WIDTH + h * hd:FOX_WIDTH + (h + 1) * hd] = dk_ref[h, :, :hd].astype(BF16)
            o_ref[:, 2 * FOX_WIDTH + h * hd:2 * FOX_WIDTH + (h + 1) * hd] = dv_ref[h, :, :hd]
            dc_ref[:, h:h + 1] = dq[:, K_ONE:K_ONE + 1] - dk_ref[h, :, Q_ONE:Q_ONE + 1]

    row3 = pl.BlockSpec((FOX_HEADS, tm, LANES), lambda i: (0, i, 0))
    return pl.pallas_call(
        body, grid=(t // tm,),
        in_specs=[pl.BlockSpec((FOX_HEADS, LANES, tm), lambda i: (0, 0, i)), row3, row3],
        out_specs=[pl.BlockSpec((tm, QKV), lambda i: (i, 0)), pl.BlockSpec((tm, FOX_HEADS), lambda i: (i, 0))],
        out_shape=[_sds((t, QKV), BF16), _sds((t, FOX_HEADS), F32)],
        compiler_params=_cp(), name="attn_unpack")(dqt, dkp, dvp)


def _adamw(parts, w, m, v, name, layer=None, into=None):
    nl, r, c = w.shape
    tr = r
    for cand in (256, 128, 64, 32, 16):
        if r > cand and r % cand == 0:
            tr = cand
            break
    npart = len(parts)
    bc1 = 1.0 - ADAM_B1 ** ADAM_STEP
    bc2 = 1.0 - ADAM_B2 ** ADAM_STEP

    def body(*refs):
        p_refs = refs[:npart]
        w_ref, m_ref, v_ref = refs[npart:npart + 3]
        g_ref, d_ref, nm_ref, nv_ref = refs[-4:]
        sums = []
        for p_ref in p_refs:
            acc = p_ref[0, 0].astype(F32)
            for s in range(1, p_ref.shape[0]):
                acc = acc + p_ref[s, 0].astype(F32)
            sums.append(acc)
        g = sums[0]
        for extra in sums[1:]:
            g = g + extra
        nm = ADAM_B1 * m_ref[0] + (1.0 - ADAM_B1) * g
        nv = ADAM_B2 * v_ref[0] + (1.0 - ADAM_B2) * (g * g)
        m_hat = nm / bc1
        v_hat = nv / bc2
        g_ref[0] = g
        d_ref[0] = -ADAM_LR * (m_hat / (jnp.sqrt(v_hat) + ADAM_EPS) + ADAM_WD * w_ref[0])
        nm_ref[0] = nm
        nv_ref[0] = nv

    first = 0 if layer is None else layer
    blk = pl.BlockSpec((1, tr, c), lambda l, i: (first + l, i, 0))
    extra = [] if into is None else list(into)
    return pl.pallas_call(
        body, grid=(nl if layer is None else 1, r // tr),
        in_specs=[pl.BlockSpec((p.shape[0], 1, tr, c), lambda l, i: (0, l, i, 0)) for p in parts] + [blk, blk, blk]
        + [_ANY_SPEC] * len(extra),
        out_specs=[blk] * 4, out_shape=[_sds(w.shape, F32)] * 4,
        input_output_aliases={npart + 3 + k: k for k in range(len(extra))},
        compiler_params=_cp(), name=name)(*parts, w, m, v, *extra)


def _to_rows(a):
    flat = a.reshape(-1)
    pad = (-flat.shape[0]) % LANES
    if pad:
        flat = jnp.concatenate([flat, jnp.zeros((pad,), flat.dtype)])
    return flat.reshape(-1, LANES)


def _by_owner_cols(dw):
    k, n = dw.shape
    return dw.reshape(k, N_CHIPS, n // N_CHIPS).transpose(1, 0, 2)[:, None]


def _ffn_fwd(xin_ln, xin_b, wi, wo, g, b, layer):
    gu, h = _ffn_in(xin_b, wi, f"ffn_in_{layer}")
    y_b, xhat, rstd = _mm_res_ln([(h, wo)], xin_ln, g, b, f"ffn_out_ln_{layer}")
    return y_b, (xin_b, gu, h, xhat, rstd)


def _ffn_bwd(dz, saved, wi, wo, ln_below, layer):
    xin_b, gu, h, _, _ = saved
    dgu = _ffn_bwd_hidden(dz, wo, gu, f"ffn_bwd_hidden_{layer}")
    g_out = _mm_tn(h, dz, f"ffn_dw_out_{layer}", tn=D_MODEL, tk=HALF_HIDDEN, tt=REDUCE_TILE // 2)
    g_in = _mm_tn(xin_b, dgu, f"ffn_dw_in_{layer}", tn=HALF_HIDDEN, stack_cols=True)
    below = _mm_nt([(dgu, 0, 0, 0)], [wi], f"ffn_dx_{layer}", tm=FFN_ROW_TILE, res=dz, ln=ln_below)
    return below, g_in, g_out.reshape(N_CHIPS, FFN_HIDDEN // N_CHIPS, D_MODEL)


def kernel(x, even_w_in, even_b_f, even_conv_w, even_w_out, odd_w_in, odd_v_ln_g, odd_v_ln_b, odd_w_s, odd_b_s, odd_w_out, mix_ln_g, mix_ln_b, ffn_w_in, ffn_w_out, ffn_ln_g, ffn_ln_b, loss_target, m_even_w_in, m_even_b_f, m_even_conv_w, m_even_w_out, m_odd_w_in, m_odd_v_ln_g, m_odd_v_ln_b, m_odd_w_s, m_odd_b_s, m_odd_w_out, m_mix_ln_g, m_mix_ln_b, m_ffn_w_in, m_ffn_w_out, m_ffn_ln_g, m_ffn_ln_b, v_even_w_in, v_even_b_f, v_even_conv_w, v_even_w_out, v_odd_w_in, v_odd_v_ln_g, v_odd_v_ln_b, v_odd_w_s, v_odd_b_s, v_odd_w_out, v_mix_ln_g, v_mix_ln_b, v_ffn_w_in, v_ffn_w_out, v_ffn_ln_g, v_ffn_ln_b):
    t = x.shape[1]
    d = D_MODEL
    chip = 2 * lax.axis_index("x") + lax.axis_index("y")
    x2d = x[0]
    target = loss_target[0]

    small_shard = jnp.concatenate([odd_v_ln_g.reshape(2, LANES), odd_v_ln_b.reshape(2, LANES),
                                   even_conv_w.reshape(CONV_K, LANES), jnp.zeros((1, LANES), F32)], axis=0)
    first = [even_w_in[0].astype(BF16), even_w_out[0].astype(BF16), small_shard]
    later = [odd_w_in[0].astype(BF16), odd_w_out[0].astype(BF16), ffn_w_in[0].astype(BF16), ffn_w_in[1].astype(BF16),
             ffn_w_out[0].astype(BF16), ffn_w_out[1].astype(BF16)]
    first_h, first_tok = _split_start(first, "gather4", "gather_first_start")
    later_h, later_tok = _split_start(later, "gather4", "gather_later_start", after=first_tok)
    g_ewi, g_ewo, g_small = [_with_own(g, own) for g, own in
                             zip(_split_wait(first_h, "gather_first_wait", later_tok), first)]
    ewi = g_ewi.transpose(1, 0, 2).reshape(d, EVEN_IN)
    w_even_in = jnp.concatenate([ewi[:, :QKV], ewi[:, QKV + FOX_HEADS:], ewi[:, QKV:QKV + FOX_HEADS],
                                 jnp.zeros((d, LANES - FOX_HEADS), BF16)], axis=1)
    w_even_out = g_ewo.reshape(d, d)
    v_ln_g = g_small[:, 0:2].reshape(1, d)
    v_ln_b = g_small[:, 2:4].reshape(1, d)
    conv_w = g_small[:, 4:7].transpose(1, 0, 2).reshape(CONV_K, CONV_WIDTH)
    chunk_id = jnp.arange(GMLP_BLOCK) // CHUNK
    gmask = chunk_id[None, :] <= chunk_id[:, None]
    w_spatial = jnp.where(gmask[None], odd_w_s[0], 0.0).astype(BF16)
    bs_col = odd_b_s[0].T
    b_f_col = even_b_f.reshape(FOX_HEADS, 1)
    ln = lambda p, l: p[l:l + 1]

    qkv, bch, fl = _proj(x2d, w_even_in, [(0, QKV, BF16), (QKV, QKV + BCH, F32), (QKV + BCH, EVEN_IN_PAD, F32)], "even_proj")
    fl3 = fl[:, :FOX_HEADS].T.reshape(FOX_HEADS, t // LANES, LANES).transpose(1, 0, 2)
    c3 = _fgate_fwd(fl3, b_f_col)
    c_rows = c3.transpose(1, 0, 2).reshape(FOX_HEADS, t)
    qp, kp, vp, kt, vt = _attn_pack(qkv, c_rows.T)
    attn, lse = _attn_fwd(qp, kp, vt)
    conv = _conv_fwd(bch, conv_w)
    x1_b, xh1, rs1 = _mm_res_ln([(attn, w_even_out[:FOX_WIDTH]), (conv, w_even_out[FOX_WIDTH:])], x2d,
                                ln(mix_ln_g, 0), ln(mix_ln_b, 0), "even_out_ln")
    w_odd_in, g_owo, w_fi0, w_fi1, g_fo0, g_fo1 = [_with_own(g, own) for g, own in
                                                   zip(_split_wait(later_h, "gather_later_wait", x1_b), later)]
    w_odd_out = g_owo.reshape(d, d)
    w_ffn_in = [w_fi0, w_fi1]
    w_ffn_out = [g_fo0.reshape(FFN_HIDDEN, d), g_fo1.reshape(FFN_HIDDEN, d)]
    x2_b, ffn0 = _ffn_fwd((xh1, ln(mix_ln_g, 0), ln(mix_ln_b, 0)), x1_b, w_ffn_in[0], w_ffn_out[0],
                          ln(ffn_ln_g, 0), ln(ffn_ln_b, 0), 0)

    a_odd, gated = _gmlp_fwd(x2_b, w_odd_in, v_ln_g, v_ln_b, w_spatial, bs_col)
    x3_b, xh3, rs3 = _mm_res_ln([(gated, w_odd_out)], (ffn0[3], ln(ffn_ln_g, 0), ln(ffn_ln_b, 0)),
                                ln(mix_ln_g, 1), ln(mix_ln_b, 1), "odd_out_ln")
    _, ffn1 = _ffn_fwd((xh3, ln(mix_ln_g, 1), ln(mix_ln_b, 1)), x3_b, w_ffn_in[1], w_ffn_out[1],
                       ln(ffn_ln_g, 1), ln(ffn_ln_b, 1), 1)

    sq, dz4, d_fg1, d_fb1 = _loss_ln_bwd(ffn1[3], ffn1[4], ln(ffn_ln_g, 1), ln(ffn_ln_b, 1), target)
    loss = lax.psum(0.5 / d * jnp.sum(sq), ("x", "y", "c"))
    (dz3, d_mg1, d_mb1), gi_f1, go_f1 = _ffn_bwd(dz4, ffn1, w_ffn_in[1], w_ffn_out[1], (xh3, rs3, ln(mix_ln_g, 1)), 1)

    dgated = _mm_nt([(dz3, 0, 0, d)], [w_odd_out], "odd_dgated")
    go_odd = _mm_tn(gated, dz3, "odd_dw_out", tn=d).reshape(N_CHIPS, 1, d // N_CHIPS, d)
    da_odd, dws, dbs_col, d_vg, d_vb = _gmlp_bwd(dgated, a_odd, v_ln_g, v_ln_b, w_spatial, bs_col)
    gi_odd = _mm_tn(x2_b, da_odd, "odd_dw_in", tn=d // 2, stack_cols=True)[:, None]
    dz2, d_fg0, d_fb0 = _mm_nt([(da_odd, 0, 0, 0)], [w_odd_in], "odd_dx", res=dz3,
                               ln=(ffn0[3], ffn0[4], ln(ffn_ln_g, 0)))
    (dz1, d_mg0, d_mb0), gi_f0, go_f0 = _ffn_bwd(dz2, ffn0, w_ffn_in[0], w_ffn_out[0], (xh1, rs1, ln(mix_ln_g, 0)), 0)

    sent_early = [gi_odd, go_odd, gi_f0[:, None], gi_f1[:, None], go_f0[:, None], go_f1[:, None]]
    early_h, early_tok = _split_start(sent_early, "scatter4", "scatter_early_start")
    dmix = _mm_nt([(dz1, 0, 0, d)], [w_even_out], "even_dmix", after=early_tok)
    go_even = jnp.concatenate([_mm_tn(attn, dz1, "even_dw_out_attn", tn=d), _mm_tn(conv, dz1, "even_dw_out_conv", tn=d)],
                              axis=0).reshape(N_CHIPS, 1, d // N_CHIPS, d)
    dbch, dconv_w8 = _conv_bwd(bch, dmix, conv_w)
    qb, dob = _attn_bwd_prep(attn, dmix, qp, lse.reshape(FOX_HEADS, t).T)
    dqkv, dc_col = _attn_unpack(*_attn_bwd(qb, kp, vp, dob, kt))
    dc3 = dc_col.T.reshape(FOX_HEADS, t // LANES, LANES).transpose(1, 0, 2)
    dfl3, d_bf = _fgate_bwd(dc3, fl3, b_f_col)
    dfl = jnp.concatenate([dfl3.transpose(1, 0, 2).reshape(FOX_HEADS, t).T.astype(BF16),
                           jnp.zeros((t, LANES - FOX_HEADS), BF16)], axis=1)

    dws_masked = jnp.where(gmask[None], dws, 0.0)
    rep_names = ["odd_w_s", "odd_b_s", "mix_ln_g", "mix_ln_b", "ffn_ln_g", "ffn_ln_b", "even_b_f"]
    rep_grads = [dws_masked, dbs_col.T, jnp.concatenate([d_mg0, d_mg1]), jnp.concatenate([d_mb0, d_mb1]),
                 jnp.concatenate([d_fg0, d_fg1]), jnp.concatenate([d_fb0, d_fb1]), d_bf.reshape(1, FOX_HEADS)]
    rep_w = [(odd_w_s, m_odd_w_s, v_odd_w_s), (odd_b_s, m_odd_b_s, v_odd_b_s), (mix_ln_g, m_mix_ln_g, v_mix_ln_g),
             (mix_ln_b, m_mix_ln_b, v_mix_ln_b), (ffn_ln_g, m_ffn_ln_g, v_ffn_ln_g), (ffn_ln_b, m_ffn_ln_b, v_ffn_ln_b),
             (even_b_f, m_even_b_f, v_even_b_f)]
    rep_rows = [_to_rows(gr) for gr in rep_grads]
    n_rep = sum(r.shape[0] for r in rep_rows)
    pad_rep = (-n_rep) % SUBLANES
    dconv_w = dconv_w8[:CONV_K].reshape(CONV_K, N_CHIPS, LANES).transpose(1, 0, 2).reshape(N_CHIPS * CONV_K, LANES)
    packed = jnp.concatenate(rep_rows + [jnp.zeros((pad_rep, LANES), F32), d_vg.reshape(SUBLANES, LANES),
                                         d_vb.reshape(SUBLANES, LANES), dconv_w, jnp.zeros((4, LANES), F32)], axis=0)
    small_h, small_tok = _split_start([packed], "gather8", "gather_small_start")

    chip_blk = lambda g: lax.dynamic_index_in_dim(g, chip, 0, keepdims=False)
    mine_early = [_with_own(r, chip_blk(g)) for r, g in
                  zip(_split_wait(early_h, "scatter_early_wait", small_tok), sent_early)]
    swap_h, swap_tok = _split_start(mine_early, "swap2", "swap_early_start")
    dw_qkv = _mm_tn(x2d, dqkv, "even_dw_qkv", tn=QKV // 2, out_dtype=F32, after=swap_tok)
    dw_bch = _mm_tn(x2d, dbch, "even_dw_bch", tn=BCH // 2, out_dtype=F32)
    dw_f = _mm_tn(x2d, dfl, "even_dw_f", tn=LANES, out_dtype=F32)
    gi_even = _by_owner_cols(jnp.concatenate([dw_qkv, dw_f[:, :FOX_HEADS], dw_bch], axis=1).astype(BF16))
    sent_late = [gi_even, go_even]
    late_h, late_tok = _split_start(sent_late, "scatter4", "scatter_late_start")
    grad_x = _mm_nt([(dqkv, 0, 0, QKV), (dbch, 0, QKV, QKV + BCH), (dfl, 0, QKV + BCH, EVEN_IN_PAD)], [w_even_in],
                    "even_dx", res=dz1, after=late_tok)
    mine_late = [_with_own(r, chip_blk(g)) for r, g in zip(_split_wait(late_h, "scatter_late_wait", grad_x), sent_late)]
    theirs_late = _exchange(mine_late, "swap2", "swap_late")
    theirs_early = _split_wait(swap_h, "swap_early_wait", theirs_late[0])
    (gathered,) = _split_wait(small_h, "gather_small_wait", theirs_early[0])
    gathered = lax.dynamic_update_index_in_dim(gathered, packed, 4 * lax.axis_index("x") + 2 * lax.axis_index("y")
                                               + lax.axis_index("c"), 0)
    mine, theirs = mine_late + mine_early, theirs_late + theirs_early
    big_w = [(even_w_in, m_even_w_in, v_even_w_in), (even_w_out, m_even_w_out, v_even_w_out),
             (odd_w_in, m_odd_w_in, v_odd_w_in), (odd_w_out, m_odd_w_out, v_odd_w_out)]
    big_names = ["even_w_in", "even_w_out", "odd_w_in", "odd_w_out"]
    res = {}
    for nm, own, sib, (w, m, v) in zip(big_names, mine, theirs, big_w):
        res[nm] = _adamw([own, sib], w, m, v, f"adamw_{nm}")
    for nm, at, (w, m, v) in (("ffn_w_in", 4, (ffn_w_in, m_ffn_w_in, v_ffn_w_in)),
                              ("ffn_w_out", 6, (ffn_w_out, m_ffn_w_out, v_ffn_w_out))):
        upper = _adamw([mine[at + 1], theirs[at + 1]], w, m, v, f"adamw_{nm}_1", layer=1)
        res[nm] = _adamw([mine[at], theirs[at]], w, m, v, f"adamw_{nm}_0", layer=0, into=upper)

    base = n_rep + pad_rep
    own_rows = jnp.concatenate([
        lax.dynamic_slice_in_dim(gathered, base + 2 * chip, 2, axis=1),
        lax.dynamic_slice_in_dim(gathered, base + SUBLANES + 2 * chip, 2, axis=1),
        lax.dynamic_slice_in_dim(gathered, base + 2 * SUBLANES + CONV_K * chip, CONV_K, axis=1),
        jnp.zeros((N_DEV, 1, LANES), F32)], axis=1)
    small_parts = jnp.concatenate([gathered[:, :base], own_rows], axis=1)[:, None]

    def pack_small(get):
        rows = [_to_rows(get(tw)) for tw in rep_w] + [jnp.zeros((pad_rep, LANES), F32)]
        rows += [get(sh).reshape(-1, LANES) for sh in ((odd_v_ln_g, m_odd_v_ln_g, v_odd_v_ln_g),
                                                       (odd_v_ln_b, m_odd_v_ln_b, v_odd_v_ln_b),
                                                       (even_conv_w, m_even_conv_w, v_even_conv_w))]
        return jnp.concatenate(rows + [jnp.zeros((1, LANES), F32)], axis=0)[None]

    small_out = _adamw([small_parts], pack_small(lambda tw: tw[0]), pack_small(lambda tw: tw[1]),
                       pack_small(lambda tw: tw[2]), "adamw_small")

    def unpack_small(rows3):
        rows = rows3[0]
        out, off = {}, 0
        for nm, (w, _, _), r in zip(rep_names, rep_w, rep_rows):
            out[nm] = rows[off:off + r.shape[0]].reshape(-1)[:w.size].reshape(w.shape)
            off += r.shape[0]
        off += pad_rep
        out["odd_v_ln_g"] = rows[off:off + 2].reshape(odd_v_ln_g.shape)
        out["odd_v_ln_b"] = rows[off + 2:off + 4].reshape(odd_v_ln_b.shape)
        out["even_conv_w"] = rows[off + 4:off + 4 + CONV_K].reshape(even_conv_w.shape)
        return out

    small = [unpack_small(o) for o in small_out]
    order = ["even_w_in", "even_b_f", "even_conv_w", "even_w_out", "odd_w_in", "odd_v_ln_g", "odd_v_ln_b", "odd_w_s",
             "odd_b_s", "odd_w_out", "mix_ln_g", "mix_ln_b", "ffn_w_in", "ffn_w_out", "ffn_ln_g", "ffn_ln_b"]
    outs = [loss, grad_x[None]]
    for kind in range(4):
        for nm in order:
            outs.append(res[nm][kind] if nm in res else small[kind][nm])
    return tuple(outs)
```

```python
import functools
import math

import jax
import jax.numpy as jnp
from jax import lax
from jax.experimental import pallas as pl
from jax.experimental.pallas import tpu as pltpu

F32 = jnp.float32
BF16 = jnp.bfloat16

D_MODEL = 1024
FOX_HEADS = 8
HEAD_DIM = 64
HEAD_PAIRS = FOX_HEADS // 2
FOX_WIDTH = FOX_HEADS * HEAD_DIM
CONV_WIDTH = 512
CONV_K = 3
QKV = 3 * FOX_WIDTH
BCH = 3 * CONV_WIDTH
EVEN_IN = QKV + FOX_HEADS + BCH
EVEN_IN_PAD = QKV + BCH + 128
GMLP_BLOCK = 128
GMLP_GROUPS = 8
CHUNK = 64
FFN_HIDDEN = 2816
HALF_HIDDEN = FFN_HIDDEN // 2
ALPHA = 4.0 ** 0.25
LN_EPS = 1e-5
ADAM_LR = 0.001
ADAM_B1 = 0.9
ADAM_B2 = 0.999
ADAM_EPS = 1e-08
ADAM_WD = 0.01
ADAM_STEP = 10
N_CHIPS = 4
N_DEV = 8
LANES = 128
SUBLANES = 8
ROW_TILE = 512
FFN_ROW_TILE = 512
REDUCE_TILE = 2048
ATT_BLOCK = 512
ATT_FWD_HEADS = 8
ATT_BWD_HEADS = 4
VMEM_LIMIT = 56 * 2 ** 20
NEG = -1e30
MESH = pl.DeviceIdType.MESH
HIGHEST = lax.Precision.HIGHEST
Q_C, Q_ONE, Q_LSE = 64, 67, 70
K_ONE, K_C, K_ONE2 = 64, 67, 70
V_ONE = 64
DO_DELTA = 65
NT = (((1,), (1,)), ((), ()))
TN = (((0,), (0,)), ((), ()))


def _cp():
    return pltpu.CompilerParams(vmem_limit_bytes=VMEM_LIMIT)


def _resident(shape):
    zeros = (0,) * len(shape)
    return pl.BlockSpec(shape, lambda *_: zeros, pipeline_mode=pl.Buffered(1))


def _sds(shape, dtype):
    return jax.ShapeDtypeStruct(tuple(shape), dtype)


_MASKS = {
    "gather4": [(1, 0, 0), (0, 1, 0), (1, 1, 0)],
    "scatter4": [(1, 0, 0), (0, 1, 0), (1, 1, 0)],
    "swap2": [(0, 0, 1)],
    "gather8": [(0, 0, 1), (0, 1, 0), (0, 1, 1), (1, 0, 0), (1, 0, 1), (1, 1, 0), (1, 1, 1)],
}


def _exchange(arrs, mode, name):
    n = len(arrs)
    masks = _MASKS[mode]
    npeer = len(masks)
    lead = {"gather4": N_CHIPS, "gather8": N_DEV}.get(mode)
    out_shapes = [_sds(((lead,) if lead else ()) + a.shape, a.dtype) for a in arrs]

    def body(*refs):
        ins, outs = refs[:n], refs[n:2 * n]
        send_sems, recv_sems, loc_sems = refs[2 * n:]
        x, y, c = lax.axis_index("x"), lax.axis_index("y"), lax.axis_index("c")
        chip, dev = 2 * x + y, 4 * x + 2 * y + c
        sends, recvs, locs = [], [], []
        for k in range(n):
            if mode == "gather4":
                locs.append(pltpu.make_async_copy(ins[k], outs[k].at[chip], loc_sems.at[k]))
            elif mode == "scatter4":
                locs.append(pltpu.make_async_copy(ins[k].at[chip], outs[k].at[chip], loc_sems.at[k]))
            elif mode == "gather8":
                locs.append(pltpu.make_async_copy(ins[k], outs[k].at[dev], loc_sems.at[k]))
        for cp in locs:
            cp.start()
        for k in range(n):
            for j, (dx, dy, dc) in enumerate(masks):
                px = 1 - x if dx else x
                py = 1 - y if dy else y
                pc = 1 - c if dc else c
                pchip, pdev = 2 * px + py, 4 * px + 2 * py + pc
                if mode == "gather4":
                    src, dst, land = ins[k], outs[k].at[chip], outs[k].at[pchip]
                elif mode == "scatter4":
                    src, dst, land = ins[k].at[pchip], outs[k].at[chip], outs[k].at[pchip]
                elif mode == "swap2":
                    src, dst, land = ins[k], outs[k], outs[k]
                else:
                    src, dst, land = ins[k], outs[k].at[dev], outs[k].at[pdev]
                s = k * npeer + j
                kw = dict(send_sem=send_sems.at[s], recv_sem=recv_sems.at[s], device_id=(px, py, pc),
                          device_id_type=MESH)
                cp = pltpu.make_async_remote_copy(src_ref=src, dst_ref=dst, **kw)
                cp.start()
                sends.append(cp)
                recvs.append(pltpu.make_async_remote_copy(src_ref=src, dst_ref=land, **kw))
        for cp in recvs:
            cp.wait_recv()
        for cp in sends:
            cp.wait_send()
        for cp in locs:
            cp.wait()

    any_spec = pl.BlockSpec(memory_space=pl.ANY)
    outs = pl.pallas_call(
        body,
        out_shape=out_shapes,
        in_specs=[any_spec] * n,
        out_specs=[any_spec] * n,
        scratch_shapes=[pltpu.SemaphoreType.DMA((n * npeer,)), pltpu.SemaphoreType.DMA((n * npeer,)),
                        pltpu.SemaphoreType.DMA((max(n, 1),))],
        name=name,
    )(*arrs)
    return list(outs)


_HBM_SPEC = pl.BlockSpec(memory_space=pltpu.HBM)
_SEM_SPEC = pl.BlockSpec(memory_space=pltpu.SEMAPHORE)
_ANY_SPEC = pl.BlockSpec(memory_space=pl.ANY)
_EFFECT = pltpu.SideEffectType.DATAFLOW_SIDE_EFFECTING


def _split_copies(mode, ins, lands, send_sems, recv_sems):
    x, y, c = lax.axis_index("x"), lax.axis_index("y"), lax.axis_index("c")
    chip, dev = 2 * x + y, 4 * x + 2 * y + c
    masks = _MASKS[mode]
    out = []
    for k in range(len(ins)):
        for j, (dx, dy, dc) in enumerate(masks):
            px = 1 - x if dx else x
            py = 1 - y if dy else y
            pc = 1 - c if dc else c
            pchip, pdev = 2 * px + py, 4 * px + 2 * py + pc
            if mode == "gather4":
                src, dst, land = ins[k], lands[k].at[chip], lands[k].at[pchip]
            elif mode == "scatter4":
                src, dst, land = ins[k].at[pchip], lands[k].at[chip], lands[k].at[pchip]
            elif mode == "swap2":
                src, dst, land = ins[k], lands[k], lands[k]
            else:
                src, dst, land = ins[k], lands[k].at[dev], lands[k].at[pdev]
            s = k * len(masks) + j
            kw = dict(send_sem=send_sems.at[s], recv_sem=recv_sems.at[s], device_id=(px, py, pc), device_id_type=MESH)
            out.append((pltpu.make_async_remote_copy(src_ref=src, dst_ref=dst, **kw),
                        pltpu.make_async_remote_copy(src_ref=src, dst_ref=land, **kw)))
    return out


def _split_start(arrs, mode, name, after=None):
    n = len(arrs)
    nsem = n * len(_MASKS[mode])
    lead = {"gather4": (N_CHIPS,), "gather8": (N_DEV,)}.get(mode, ())
    land_shapes = [lead + a.shape for a in arrs]

    def body(*refs):
        ins, lands = refs[:n], refs[n:2 * n]
        outs = refs[2 * n + (after is not None):]
        for start, _ in _split_copies(mode, ins, lands, outs[0], outs[1]):
            start.start()
        outs[-1][...] = jnp.zeros(outs[-1].shape, F32)

    srcs = [pltpu.with_memory_space_constraint(a, pltpu.HBM) for a in arrs]
    empties = [pltpu.with_memory_space_constraint(lax.empty(s, a.dtype), pltpu.HBM) for s, a in zip(land_shapes, arrs)]
    res = pl.pallas_call(
        body, name=name,
        out_shape=(pltpu.SemaphoreType.DMA((nsem,)), pltpu.SemaphoreType.DMA((nsem,)),
                   *[pltpu.HBM(a.shape, a.dtype) for a in arrs],
                   *[pltpu.HBM(s, a.dtype) for s, a in zip(land_shapes, arrs)],
                   _sds((SUBLANES, LANES), F32)),
        in_specs=[_HBM_SPEC] * (2 * n) + ([_ANY_SPEC] if after is not None else []),
        out_specs=(_SEM_SPEC, _SEM_SPEC, *[_HBM_SPEC] * (2 * n), pl.BlockSpec(memory_space=pltpu.VMEM)),
        input_output_aliases={k: 2 + k for k in range(2 * n)},
        compiler_params=pltpu.CompilerParams(has_side_effects=_EFFECT),
    )(*srcs, *empties, *([after] if after is not None else []))
    return dict(mode=mode, n=n, sems=res[:2], bufs=res[2:2 + 2 * n]), res[-1]


def _split_wait(handle, name, after):
    n, mode = handle["n"], handle["mode"]

    def body(*refs):
        ins, lands = refs[:n], refs[n:2 * n]
        send_sems, recv_sems = refs[2 * n], refs[2 * n + 1]
        for _, arrival in _split_copies(mode, ins, lands, send_sems, recv_sems):
            arrival.wait_send()
            arrival.wait_recv()

    bufs = handle["bufs"]
    res = pl.pallas_call(
        body, name=name,
        out_shape=tuple(pltpu.HBM(b.shape, b.dtype) for b in bufs),
        in_specs=[_HBM_SPEC] * (2 * n) + [_SEM_SPEC, _SEM_SPEC, _ANY_SPEC],
        out_specs=tuple([_HBM_SPEC] * (2 * n)),
        input_output_aliases={k: k for k in range(2 * n)},
        compiler_params=pltpu.CompilerParams(has_side_effects=_EFFECT),
    )(*bufs, *handle["sems"], after)
    return list(res[n:])


def _with_own(landed, own):
    chip = 2 * lax.axis_index("x") + lax.axis_index("y")
    return lax.dynamic_update_index_in_dim(landed, own, chip, 0)


def _sigmoid(x):
    return 0.5 * jnp.tanh(0.5 * x) + 0.5


def _log_sigmoid(x):
    e = jnp.exp(-jnp.abs(x))
    log1p = jnp.where(e < 1e-2, e * (1.0 - e * (0.5 - e * (1.0 / 3.0))), jnp.log(1.0 + e))
    return jnp.minimum(x, 0.0) - log1p


def _ln_fwd(z):
    mu = jnp.mean(z, axis=-1, keepdims=True)
    zc = z - mu
    var = jnp.mean(zc * zc, axis=-1, keepdims=True)
    rstd = lax.rsqrt(var + LN_EPS)
    return zc * rstd, rstd


def _ln_bwd(dy, xhat, rstd, g):
    dxh = dy * g
    m1 = jnp.mean(dxh, axis=-1, keepdims=True)
    m2 = jnp.mean(dxh * xhat, axis=-1, keepdims=True)
    dz = rstd * (dxh - m1 - xhat * m2)
    return dz, jnp.sum(dy * xhat, axis=0, keepdims=True), jnp.sum(dy, axis=0, keepdims=True)


def _shift_down(z, halo):
    r = lax.broadcasted_iota(jnp.int32, z.shape, 0)
    z1 = jnp.where(r == 0, halo[7:8, :], pltpu.roll(z, 1, 0))
    z2 = jnp.where(r == 0, halo[6:7, :], jnp.where(r == 1, halo[7:8, :], pltpu.roll(z, 2, 0)))
    return z1, z2


def _shift_up(z, halo):
    n = z.shape[0]
    r = lax.broadcasted_iota(jnp.int32, z.shape, 0)
    z1 = jnp.where(r == n - 1, halo[0:1, :], pltpu.roll(z, n - 1, 0))
    z2 = jnp.where(r == n - 1, halo[1:2, :], jnp.where(r == n - 2, halo[0:1, :], pltpu.roll(z, n - 2, 0)))
    return z1, z2


def _accumulate(ref, first, value):
    @pl.when(first)
    def _():
        ref[...] = value

    @pl.when(jnp.logical_not(first))
    def _():
        ref[...] += value


def _proj(x, w, splits, name):
    t, k = x.shape
    tm = min(ROW_TILE, t)

    def body(x_ref, w_ref, *outs):
        a = x_ref[...].astype(BF16)
        for (lo, hi, dt), o in zip(splits, outs):
            o[...] = jnp.dot(a, w_ref[:, lo:hi], preferred_element_type=F32).astype(dt)

    return pl.pallas_call(
        body, grid=(t // tm,),
        in_specs=[pl.BlockSpec((tm, k), lambda i: (i, 0)), _resident(w.shape)],
        out_specs=[pl.BlockSpec((tm, hi - lo), lambda i: (i, 0)) for lo, hi, _ in splits],
        out_shape=[_sds((t, hi - lo), dt) for lo, hi, dt in splits],
        compiler_params=_cp(), name=name)(x, w)


def _fgate_fwd(fl3, b_f):
    nc = fl3.shape[0]

    def body(f_ref, b_ref, c_ref):
        r = lax.broadcasted_iota(jnp.int32, (LANES, LANES), 0)
        cidx = lax.broadcasted_iota(jnp.int32, (LANES, LANES), 1)
        upper = (r <= cidx).astype(F32)

        def step(i, carry):
            lf = _log_sigmoid(f_ref[i] + b_ref[...])
            cc = jnp.dot(lf, upper, precision=HIGHEST, preferred_element_type=F32) + carry
            c_ref[i] = cc
            return cc[:, LANES - 1:LANES]

        lax.fori_loop(0, nc, step, jnp.zeros((FOX_HEADS, 1), F32))

    return pl.pallas_call(body, out_shape=_sds(fl3.shape, F32), name="fgate_fwd")(fl3, b_f)


def _fgate_bwd(dc3, fl3, b_f):
    nc = fl3.shape[0]

    def body(dc_ref, f_ref, b_ref, df_ref, db_ref):
        r = lax.broadcasted_iota(jnp.int32, (LANES, LANES), 0)
        cidx = lax.broadcasted_iota(jnp.int32, (LANES, LANES), 1)
        lower = (r >= cidx).astype(F32)

        def step(n, carry):
            suffix, db = carry
            i = nc - 1 - n
            dlf = jnp.dot(dc_ref[i], lower, precision=HIGHEST, preferred_element_type=F32) + suffix
            df = dlf * (1.0 - _sigmoid(f_ref[i] + b_ref[...]))
            df_ref[i] = df
            return dlf[:, 0:1], db + jnp.sum(df, axis=1, keepdims=True)

        zero = jnp.zeros((FOX_HEADS, 1), F32)
        _, db = lax.fori_loop(0, nc, step, (zero, zero))
        db_ref[...] = db

    return pl.pallas_call(body, out_shape=[_sds(fl3.shape, F32), _sds((FOX_HEADS, 1), F32)],
                          name="fgate_bwd")(dc3, fl3, b_f)


def _split3(c):
    hi = c.astype(BF16).astype(F32)
    mid = (c - hi).astype(BF16).astype(F32)
    lo = (c - hi - mid).astype(BF16).astype(F32)
    return hi, mid, lo


def _lane_pieces(lane, start, pieces, sign):
    out = jnp.zeros(lane.shape, F32)
    for n, p in enumerate(pieces):
        out = jnp.where(lane == start + n, sign * p, out)
    return out


def _attn_pack(qkv, c_col):
    t = qkv.shape[0]
    tm = min(ROW_TILE, t)
    hd = HEAD_DIM

    def body(x_ref, c_ref, qp_ref, kp_ref, vp_ref, kt_ref, vt_ref):
        lane = lax.broadcasted_iota(jnp.int32, (tm, hd), 1) + hd
        for h in range(FOX_HEADS):
            pieces = _split3(c_ref[:, h:h + 1])
            ones = lambda a, b: jnp.where(jnp.logical_and(lane >= a, lane < b), 1.0, 0.0)
            q_extra = _lane_pieces(lane, Q_C, pieces, 1.0) + ones(Q_ONE, Q_ONE + 3)
            k_extra = _lane_pieces(lane, K_C, pieces, -1.0) + ones(K_ONE, K_ONE + 3) + ones(K_ONE2, K_ONE2 + 3)
            qp_ref[h, :, :hd] = (x_ref[:, h * hd:(h + 1) * hd].astype(F32) * (hd ** -0.5)).astype(BF16)
            qp_ref[h, :, hd:] = q_extra.astype(BF16)
            kp_ref[h, :, :hd] = x_ref[:, FOX_WIDTH + h * hd:FOX_WIDTH + (h + 1) * hd]
            kp_ref[h, :, hd:] = k_extra.astype(BF16)
            vp_ref[h, :, :hd] = x_ref[:, 2 * FOX_WIDTH + h * hd:2 * FOX_WIDTH + (h + 1) * hd]
            vp_ref[h, :, hd:] = ones(V_ONE, V_ONE + 4).astype(BF16)
            kt_ref[h] = kp_ref[h].astype(F32).T.astype(BF16)
            vt_ref[h] = vp_ref[h].astype(F32).T.astype(BF16)

    row3 = pl.BlockSpec((FOX_HEADS, tm, LANES), lambda i: (0, i, 0))
    col3 = pl.BlockSpec((FOX_HEADS, LANES, tm), lambda i: (0, 0, i))
    return pl.pallas_call(
        body, grid=(t // tm,),
        in_specs=[pl.BlockSpec((tm, QKV), lambda i: (i, 0)), pl.BlockSpec((tm, FOX_HEADS), lambda i: (i, 0))],
        out_specs=[row3, row3, row3, col3, col3],
        out_shape=[_sds((FOX_HEADS, t, LANES), BF16)] * 3 + [_sds((FOX_HEADS, LANES, t), BF16)] * 2,
        compiler_params=_cp(), name="attn_pack")(qkv, c_col)


def _triangle(nq, key_major):
    if key_major:
        pairs = [(i, j) for j in range(nq) for i in range(j, nq)]
    else:
        pairs = [(i, j) for i in range(nq) for j in range(i + 1)]
    return jnp.asarray([p[0] for p in pairs], jnp.int32), jnp.asarray([p[1] for p in pairs], jnp.int32)


def _attn_fwd(qp, kp, vt):
    t = qp.shape[1]
    bq = min(ATT_BLOCK, t)
    nq = t // bq
    nh = ATT_FWD_HEADS
    i_tab, j_tab = _triangle(nq, key_major=False)

    def body(it_ref, jt_ref, q_ref, k_ref, vt_ref, o_ref, lse_ref, m_sc, acc_sc):
        s = pl.program_id(1)
        i, j = it_ref[s], jt_ref[s]

        @pl.when(j == 0)
        def _():
            m_sc[...] = jnp.full(m_sc.shape, NEG, F32)
            acc_sc[...] = jnp.zeros(acc_sc.shape, F32)

        def sweep(masked):
            scores = lambda h: lax.dot_general(k_ref[h], q_ref[h], NT, preferred_element_type=F32)

            def accumulate(h, pt, rescale):
                acc_sc[h] = rescale * acc_sc[h] + jnp.dot(vt_ref[h], pt, preferred_element_type=F32)

            ahead, behind = scores(0), None
            for h in range(nh):
                st = ahead
                if h + 1 < nh:
                    ahead = scores(h + 1)
                if behind is not None:
                    accumulate(*behind)
                if masked:
                    key = lax.broadcasted_iota(jnp.int32, (bq, bq), 0)
                    qry = lax.broadcasted_iota(jnp.int32, (bq, bq), 1)
                    st = jnp.where(key <= qry, st, NEG)
                m_prev = m_sc[h]
                m_new = jnp.maximum(m_prev, jnp.max(st, axis=0, keepdims=True))
                behind = (h, jnp.exp(st - m_new).astype(BF16), jnp.exp(m_prev - m_new))
                m_sc[h] = m_new
            accumulate(*behind)

        @pl.when(j < i)
        def _():
            sweep(False)

        @pl.when(j == i)
        def _():
            sweep(True)
            for h in range(nh):
                acc = acc_sc[h]
                denom = acc[V_ONE:V_ONE + 1, :]
                o_ref[:, h * HEAD_DIM:(h + 1) * HEAD_DIM] = (acc[:HEAD_DIM, :] / denom).T.astype(BF16)
                lse_ref[h] = m_sc[h] + jnp.log(denom)

    grid_spec = pltpu.PrefetchScalarGridSpec(
        num_scalar_prefetch=2, grid=(FOX_HEADS // nh, i_tab.shape[0]),
        in_specs=[pl.BlockSpec((nh, bq, LANES), lambda hp, s, it, jt: (hp, it[s], 0)),
                  pl.BlockSpec((nh, bq, LANES), lambda hp, s, it, jt: (hp, jt[s], 0)),
                  pl.BlockSpec((nh, LANES, bq), lambda hp, s, it, jt: (hp, 0, jt[s]))],
        out_specs=[pl.BlockSpec((bq, nh * HEAD_DIM), lambda hp, s, it, jt: (it[s], hp)),
                   pl.BlockSpec((nh, 1, bq), lambda hp, s, it, jt: (hp, 0, it[s]))],
        scratch_shapes=[pltpu.VMEM((nh, 1, bq), F32), pltpu.VMEM((nh, LANES, bq), F32)])
    return pl.pallas_call(body, grid_spec=grid_spec,
                          out_shape=[_sds((t, FOX_WIDTH), BF16), _sds((FOX_HEADS, 1, t), F32)],
                          compiler_params=_cp(), name="attn_fwd")(i_tab, j_tab, qp, kp, vt)


def _conv_fwd(bch, conv_w):
    t = bch.shape[0]
    tm = min(ROW_TILE, t)
    halo_blocks = tm // SUBLANES
    cw = CONV_WIDTH

    def body(cur_ref, prev_ref, w_ref, o_ref):
        i = pl.program_id(0)
        z = cur_ref[:, cw:2 * cw] * cur_ref[:, 2 * cw:]
        zp = jnp.where(i == 0, 0.0, prev_ref[:, cw:2 * cw] * prev_ref[:, 2 * cw:])
        z1, z2 = _shift_down(z, zp)
        y = w_ref[0:1, :] * z2 + w_ref[1:2, :] * z1 + w_ref[2:3, :] * z
        o_ref[...] = (cur_ref[:, :cw] * y).astype(BF16)

    return pl.pallas_call(
        body, grid=(t // tm,),
        in_specs=[pl.BlockSpec((tm, BCH), lambda i: (i, 0)),
                  pl.BlockSpec((SUBLANES, BCH), lambda i: (jnp.maximum(i * halo_blocks - 1, 0), 0)),
                  _resident(conv_w.shape)],
        out_specs=pl.BlockSpec((tm, cw), lambda i: (i, 0)),
        out_shape=_sds((t, cw), BF16), compiler_params=_cp(), name="conv_fwd")(bch, bch, conv_w)


def _mm_res_ln(pairs, res, g, b, name):
    from_ln = isinstance(res, tuple)
    res_args = list(res) if from_ln else [res]
    t, d = res_args[0].shape
    tm = min(ROW_TILE, t)
    n = len(pairs)

    def body(*refs):
        a_refs, w_refs = refs[:n], refs[n:2 * n]
        res_refs = refs[2 * n:2 * n + len(res_args)]
        g_ref, b_ref, yb_ref, xh_ref, rs_ref = refs[2 * n + len(res_args):]
        r = res_refs[0][...]
        if from_ln:
            r = r * res_refs[1][...] + res_refs[2][...]
        z = ALPHA * r
        for a_ref, w_ref in zip(a_refs, w_refs):
            z = z + jnp.dot(a_ref[...].astype(BF16), w_ref[...], preferred_element_type=F32)
        xhat, rstd = _ln_fwd(z)
        yb_ref[...] = (xhat * g_ref[...] + b_ref[...]).astype(BF16)
        xh_ref[...] = xhat
        rs_ref[...] = rstd

    row = lambda i: (i, 0)
    full = pl.BlockSpec((tm, d), row)
    return pl.pallas_call(
        body, grid=(t // tm,),
        in_specs=[pl.BlockSpec((tm, a.shape[1]), row) for a, _ in pairs] + [_resident(w.shape) for _, w in pairs]
        + [full] + [_resident(a.shape) for a in res_args[1:]] + [_resident(g.shape), _resident(b.shape)],
        out_specs=[full, full, pl.BlockSpec((tm, 1), row)],
        out_shape=[_sds((t, d), BF16), _sds((t, d), F32), _sds((t, 1), F32)],
        compiler_params=_cp(), name=name)(*[a for a, _ in pairs], *[w for _, w in pairs], *res_args, g, b)


def _ffn_in(x, wi, name):
    t, d = x.shape
    tm = min(FFN_ROW_TILE, t)
    hh = HALF_HIDDEN

    def body(x_ref, w_ref, gu_ref, h_ref):
        a = x_ref[...].astype(BF16)
        for c in range(2):
            gs, us = slice(c * hh, (c + 1) * hh), slice(FFN_HIDDEN + c * hh, FFN_HIDDEN + (c + 1) * hh)
            g = jnp.dot(a, w_ref[c], preferred_element_type=F32)
            u = jnp.dot(a, w_ref[2 + c], preferred_element_type=F32)
            sig = _sigmoid(g)
            silu = g * sig
            gu_ref[:, gs] = (u * sig * (1.0 + g * (1.0 - sig))).astype(BF16)
            gu_ref[:, us] = silu.astype(BF16)
            h_ref[:, gs] = (silu * u).astype(BF16)

    row = lambda i: (i, 0)
    return pl.pallas_call(
        body, grid=(t // tm,),
        in_specs=[pl.BlockSpec((tm, d), row), _resident(wi.shape)],
        out_specs=[pl.BlockSpec((tm, 2 * FFN_HIDDEN), row), pl.BlockSpec((tm, FFN_HIDDEN), row)],
        out_shape=[_sds((t, 2 * FFN_HIDDEN), BF16), _sds((t, FFN_HIDDEN), BF16)],
        compiler_params=_cp(), name=name)(x, wi)


def _gmlp_fwd(x, w_in, vg, vb, wm, bs_col):
    t, d = x.shape
    tm = min(ROW_TILE, t)
    gb = GMLP_BLOCK

    def body(x_ref, w_ref, vg_ref, vb_ref, wm_ref, bs_ref, sv_ref, rs_ref, o_ref, a_sc):
        xb = x_ref[...].astype(BF16)
        nc = w_ref.shape[2]
        for j in range(w_ref.shape[0]):
            a_sc[:, j * nc:(j + 1) * nc] = jnp.dot(xb, w_ref[j], preferred_element_type=F32)
        halves = []
        for half in range(2):
            a = a_sc[:, half * d:(half + 1) * d]
            cdf = 0.5 * (1.0 + lax.erf(a * (2.0 ** -0.5)))
            halves.append(a * cdf)
            slope = cdf + a * (jnp.exp(-0.5 * a * a) * (1.0 / math.sqrt(2.0 * math.pi)))
            sv_ref[:, (2 * half + 1) * d:(2 * half + 2) * d] = slope.astype(BF16)
        u = halves[0]
        vhat, rstd = _ln_fwd(halves[1])
        sv_ref[:, :d] = u.astype(BF16)
        sv_ref[:, 2 * d:3 * d] = vhat.astype(BF16)
        rs_ref[...] = rstd
        vln = (vhat * vg_ref[...] + vb_ref[...]).astype(BF16)
        for blk in range(tm // gb):
            rs = slice(blk * gb, (blk + 1) * gb)
            for gi in range(GMLP_GROUPS):
                cs = slice(gi * gb, (gi + 1) * gb)
                s = jnp.dot(wm_ref[gi], vln[rs, cs], preferred_element_type=F32) + bs_ref[:, gi:gi + 1]
                o_ref[rs, cs] = (u[rs, cs] * s).astype(BF16)

    row = lambda i: (i, 0)
    return pl.pallas_call(
        body, grid=(t // tm,),
        in_specs=[pl.BlockSpec((tm, d), row), _resident(w_in.shape), _resident(vg.shape), _resident(vb.shape),
                  _resident(wm.shape), _resident(bs_col.shape)],
        out_specs=[pl.BlockSpec((tm, 4 * d), row), pl.BlockSpec((tm, 1), row), pl.BlockSpec((tm, d), row)],
        out_shape=[_sds((t, 4 * d), BF16), _sds((t, 1), F32), _sds((t, d), BF16)],
        scratch_shapes=[pltpu.VMEM((tm, 2 * d), F32)],
        compiler_params=_cp(), name="gmlp_fwd")(x, w_in, vg, vb, wm, bs_col)


def _loss_ln_bwd(xhat, rstd, g, b, target):
    t, d = xhat.shape
    tm = min(ROW_TILE, t)

    def body(xh_ref, rs_ref, g_ref, b_ref, t_ref, sq_ref, dz_ref, dg_ref, db_ref):
        first = pl.program_id(0) == 0
        xh = xh_ref[...]
        err = xh * g_ref[...] + b_ref[...] - t_ref[...]
        dz, dg, db = _ln_bwd(err * (1.0 / d), xh, rs_ref[...], g_ref[...])
        dz_ref[...] = dz
        _accumulate(sq_ref, first, jnp.sum(err * err, axis=0, keepdims=True))
        _accumulate(dg_ref, first, dg)
        _accumulate(db_ref, first, db)

    row = lambda i: (i, 0)
    vec = pl.BlockSpec((1, d), lambda i: (0, 0))
    return pl.pallas_call(
        body, grid=(t // tm,),
        in_specs=[pl.BlockSpec((tm, d), row), pl.BlockSpec((tm, 1), row), _resident(g.shape), _resident(b.shape),
                  pl.BlockSpec((tm, d), row)],
        out_specs=[vec, pl.BlockSpec((tm, d), row), vec, vec],
        out_shape=[_sds((1, d), F32), _sds((t, d), F32), _sds((1, d), F32), _sds((1, d), F32)],
        compiler_params=_cp(), name="loss_ln_bwd")(xhat, rstd, g, b, target)


def _mm_nt(pairs, ws, name, *, tm=ROW_TILE, res=None, ln=None, out_dtype=F32, after=None):
    t = pairs[0][0].shape[0]
    k = ws[0].shape[-2]
    tm = min(tm, t)
    n, nw = len(pairs), len(ws)

    def body(*refs):
        refs = refs[after is not None:]
        a_refs, w_refs = refs[:n], refs[n:n + nw]
        rest = list(refs[n + nw:])
        dx = None
        for a_ref, (_, wi, lo, hi) in zip(a_refs, pairs):
            w_ref = w_refs[wi]
            if len(w_ref.shape) == 3:
                nc = w_ref.shape[2]
                parts = [lax.dot_general(a_ref[:, j * nc:(j + 1) * nc].astype(BF16), w_ref[j], NT,
                                         preferred_element_type=F32) for j in range(w_ref.shape[0])]
            else:
                parts = [lax.dot_general(a_ref[...].astype(BF16), w_ref[:, lo:hi], NT, preferred_element_type=F32)]
            for part in parts:
                dx = part if dx is None else dx + part
        if res is not None:
            dx = dx + ALPHA * rest.pop(0)[...]
        if ln is None:
            rest[0][...] = dx.astype(out_dtype)
            return
        xh_ref, rs_ref, g_ref, dz_ref, dg_ref, db_ref = rest
        first = pl.program_id(0) == 0
        dz, dg, db = _ln_bwd(dx, xh_ref[...], rs_ref[...], g_ref[...])
        dz_ref[...] = dz
        _accumulate(dg_ref, first, dg)
        _accumulate(db_ref, first, db)

    row = lambda i: (i, 0)
    in_specs = [pl.BlockSpec((tm, a.shape[1]), row) for a, _, _, _ in pairs] + [_resident(w.shape) for w in ws]
    args = [a for a, _, _, _ in pairs] + list(ws)
    if res is not None:
        in_specs.append(pl.BlockSpec((tm, k), row))
        args.append(res)
    if ln is None:
        out_specs = pl.BlockSpec((tm, k), row)
        out_shape = _sds((t, k), out_dtype)
    else:
        xhat, rstd, g = ln
        in_specs += [pl.BlockSpec((tm, k), row), pl.BlockSpec((tm, 1), row), _resident(g.shape)]
        args += [xhat, rstd, g]
        vec = pl.BlockSpec((1, k), lambda i: (0, 0))
        out_specs = [pl.BlockSpec((tm, k), row), vec, vec]
        out_shape = [_sds((t, k), F32), _sds((1, k), F32), _sds((1, k), F32)]
    if after is not None:
        in_specs.insert(0, _ANY_SPEC)
        args.insert(0, after)
    return pl.pallas_call(body, grid=(t // tm,), in_specs=in_specs, out_specs=out_specs, out_shape=out_shape,
                          compiler_params=_cp(), name=name)(*args)


def _mm_tn(a, b, name, *, tn, tk=None, tt=None, stack_cols=False, out_dtype=BF16, after=None):
    t, k = a.shape
    n = b.shape[1]
    tk = k if tk is None else tk
    tt = min(REDUCE_TILE if tt is None else tt, t)
    nt = t // tt

    def body(a_ref, b_ref, *rest):
        o_ref, acc_ref = rest[after is not None:]
        s = pl.program_id(2)
        part = lax.dot_general(a_ref[...].astype(BF16), b_ref[...].astype(BF16), TN, preferred_element_type=F32)
        _accumulate(acc_ref, s == 0, part)

        @pl.when(s == nt - 1)
        def _():
            o_ref[...] = acc_ref[...].astype(out_dtype).reshape(o_ref.shape)

    if stack_cols:
        assert tk == k
        out_spec = pl.BlockSpec((1, k, tn), lambda kk, j, s: (j, 0, 0))
        out_shape = _sds((n // tn, k, tn), out_dtype)
    else:
        out_spec = pl.BlockSpec((tk, tn), lambda kk, j, s: (kk, j))
        out_shape = _sds((k, n), out_dtype)
    return pl.pallas_call(
        body, grid=(k // tk, n // tn, nt),
        in_specs=[pl.BlockSpec((tt, tk), lambda kk, j, s: (s, kk)), pl.BlockSpec((tt, tn), lambda kk, j, s: (s, j))]
        + ([_ANY_SPEC] if after is not None else []),
        out_specs=out_spec, out_shape=out_shape,
        scratch_shapes=[pltpu.VMEM((tk, tn), F32)],
        compiler_params=_cp(), name=name)(a, b, *([after] if after is not None else []))


def _ffn_bwd_hidden(dz, wo, gu, name):
    t, d = dz.shape
    tm = min(FFN_ROW_TILE, t)
    hh = HALF_HIDDEN

    def body(dz_ref, w_ref, gu_ref, o_ref):
        a = dz_ref[...].astype(BF16)
        for c in range(2):
            gs, us = slice(c * hh, (c + 1) * hh), slice(FFN_HIDDEN + c * hh, FFN_HIDDEN + (c + 1) * hh)
            dh = lax.dot_general(a, w_ref[gs, :], NT, preferred_element_type=F32)
            o_ref[:, gs] = (dh * gu_ref[:, gs].astype(F32)).astype(BF16)
            o_ref[:, us] = (dh * gu_ref[:, us].astype(F32)).astype(BF16)

    row = lambda i: (i, 0)
    return pl.pallas_call(
        body, grid=(t // tm,),
        in_specs=[pl.BlockSpec((tm, d), row), _resident(wo.shape), pl.BlockSpec((tm, 2 * FFN_HIDDEN), row)],
        out_specs=pl.BlockSpec((tm, 2 * FFN_HIDDEN), row),
        out_shape=_sds((t, 2 * FFN_HIDDEN), BF16), compiler_params=_cp(), name=name)(dz, wo, gu)


def _gmlp_bwd(dgated, saved, rstd_v, vg, vb, wm, bs_col):
    t, d = dgated.shape
    d2 = 2 * d
    tm = min(ROW_TILE, t)
    gb = GMLP_BLOCK

    def body(dg_ref, sv_ref, rs_ref, vg_ref, vb_ref, wm_ref, bs_ref, da_ref, dws_ref, dbs_ref, dvg_ref, dvb_ref, dvln_sc):
        first = pl.program_id(0) == 0
        u = sv_ref[:, :d].astype(F32)
        vhat = sv_ref[:, 2 * d:3 * d].astype(F32)
        rstd = rs_ref[...]
        vln = (vhat * vg_ref[...] + vb_ref[...]).astype(BF16)
        dgate = dg_ref[...]

        @pl.when(first)
        def _():
            dws_ref[...] = jnp.zeros(dws_ref.shape, F32)
            dbs_ref[...] = jnp.zeros(dbs_ref.shape, F32)

        for blk in range(tm // gb):
            rs = slice(blk * gb, (blk + 1) * gb)
            for gi in range(GMLP_GROUPS):
                cs = slice(gi * gb, (gi + 1) * gb)
                vblk = vln[rs, cs]
                s = jnp.dot(wm_ref[gi], vblk, preferred_element_type=F32) + bs_ref[:, gi:gi + 1]
                dgb = dgate[rs, cs]
                da_ref[rs, cs] = (dgb * s * sv_ref[rs, d + gi * gb:d + (gi + 1) * gb].astype(F32)).astype(BF16)
                ds = dgb * u[rs, cs]
                dsb = ds.astype(BF16)
                dws_ref[gi] += lax.dot_general(dsb, vblk, NT, preferred_element_type=F32)
                dbs_ref[:, gi:gi + 1] += jnp.sum(ds, axis=1, keepdims=True)
                dvln_sc[rs, cs] = lax.dot_general(wm_ref[gi], dsb, TN, preferred_element_type=F32)
        dv, dvg, dvb = _ln_bwd(dvln_sc[...], vhat, rstd, vg_ref[...])
        da_ref[:, d:] = (dv * sv_ref[:, 3 * d:].astype(F32)).astype(BF16)
        _accumulate(dvg_ref, first, dvg)
        _accumulate(dvb_ref, first, dvb)

    row = lambda i: (i, 0)
    vec = pl.BlockSpec((1, d), lambda i: (0, 0))
    return pl.pallas_call(
        body, grid=(t // tm,),
        in_specs=[pl.BlockSpec((tm, d), row), pl.BlockSpec((tm, 4 * d), row), pl.BlockSpec((tm, 1), row),
                  _resident(vg.shape), _resident(vb.shape), _resident(wm.shape), _resident(bs_col.shape)],
        out_specs=[pl.BlockSpec((tm, d2), row), pl.BlockSpec(wm.shape, lambda i: (0, 0, 0)),
                   pl.BlockSpec(bs_col.shape, lambda i: (0, 0)), vec, vec],
        out_shape=[_sds((t, d2), BF16), _sds(wm.shape, F32), _sds(bs_col.shape, F32), _sds((1, d), F32), _sds((1, d), F32)],
        scratch_shapes=[pltpu.VMEM((tm, d), F32)],
        compiler_params=_cp(), name="gmlp_bwd")(dgated, saved, rstd_v, vg, vb, wm, bs_col)


def _conv_bwd(bch, dmix, conv_w):
    t = bch.shape[0]
    tm = min(ROW_TILE, t)
    nb = t // tm
    halo_blocks = tm // SUBLANES
    cw = CONV_WIDTH

    def body(cur_ref, prev_ref, next_ref, dc_ref, dn_ref, w_ref, o_ref, dw_ref):
        i = pl.program_id(0)
        bgate, cgate, hval = cur_ref[:, :cw], cur_ref[:, cw:2 * cw], cur_ref[:, 2 * cw:]
        z = cgate * hval
        zp = jnp.where(i == 0, 0.0, prev_ref[:, cw:2 * cw] * prev_ref[:, 2 * cw:])
        z1, z2 = _shift_down(z, zp)
        w0, w1, w2 = w_ref[0:1, :], w_ref[1:2, :], w_ref[2:3, :]
        dconv = dc_ref[...]
        o_ref[:, :cw] = (dconv * (w0 * z2 + w1 * z1 + w2 * z)).astype(BF16)
        dy = dconv * bgate
        dyn = jnp.where(i == nb - 1, 0.0, dn_ref[...] * next_ref[:, :cw])
        dy1, dy2 = _shift_up(dy, dyn)
        dz = w2 * dy + w1 * dy1 + w0 * dy2
        o_ref[:, cw:2 * cw] = (dz * hval).astype(BF16)
        o_ref[:, 2 * cw:] = (dz * cgate).astype(BF16)

        @pl.when(i == 0)
        def _():
            dw_ref[...] = jnp.zeros(dw_ref.shape, F32)

        for tap, zs in enumerate((z2, z1, z)):
            dw_ref[tap:tap + 1, :] += jnp.sum(dy * zs, axis=0, keepdims=True)

    last_halo = t // SUBLANES - 1
    return pl.pallas_call(
        body, grid=(nb,),
        in_specs=[pl.BlockSpec((tm, BCH), lambda i: (i, 0)),
                  pl.BlockSpec((SUBLANES, BCH), lambda i: (jnp.maximum(i * halo_blocks - 1, 0), 0)),
                  pl.BlockSpec((SUBLANES, BCH), lambda i: (jnp.minimum((i + 1) * halo_blocks, last_halo), 0)),
                  pl.BlockSpec((tm, cw), lambda i: (i, 1)),
                  pl.BlockSpec((SUBLANES, cw), lambda i: (jnp.minimum((i + 1) * halo_blocks, last_halo), 1)),
                  _resident(conv_w.shape)],
        out_specs=[pl.BlockSpec((tm, BCH), lambda i: (i, 0)), pl.BlockSpec((SUBLANES, cw), lambda i: (0, 0))],
        out_shape=[_sds((t, BCH), BF16), _sds((SUBLANES, cw), F32)],
        compiler_params=_cp(), name="conv_bwd")(bch, bch, bch, dmix, dmix, conv_w)


def _attn_bwd_prep(o, dmix, qp, lse_col):
    t = o.shape[0]
    tm = min(ROW_TILE, t)
    hd = HEAD_DIM

    def body(o_ref, do_ref, qp_ref, lse_ref, qb_ref, dob_ref):
        lane = lax.broadcasted_iota(jnp.int32, (tm, hd), 1) + hd
        for h in range(FOX_HEADS):
            do = do_ref[:, h * hd:(h + 1) * hd]
            delta = jnp.sum(o_ref[:, h * hd:(h + 1) * hd].astype(F32) * do, axis=-1, keepdims=True)
            dob_ref[h, :, :hd] = do.astype(BF16)
            dob_ref[h, :, hd:] = _lane_pieces(lane, DO_DELTA, _split3(delta), -1.0).astype(BF16)
            qb_ref[h, :, :hd] = qp_ref[h, :, :hd]
            qb_ref[h, :, hd:] = (qp_ref[h, :, hd:].astype(F32)
                                 + _lane_pieces(lane, Q_LSE, _split3(lse_ref[:, h:h + 1]), -1.0)).astype(BF16)

    row3 = pl.BlockSpec((FOX_HEADS, tm, LANES), lambda i: (0, i, 0))
    return pl.pallas_call(
        body, grid=(t // tm,),
        in_specs=[pl.BlockSpec((tm, FOX_WIDTH), lambda i: (i, 0)), pl.BlockSpec((tm, FOX_WIDTH), lambda i: (i, 0)), row3,
                  pl.BlockSpec((tm, FOX_HEADS), lambda i: (i, 0))],
        out_specs=[row3, row3], out_shape=[_sds((FOX_HEADS, t, LANES), BF16)] * 2,
        compiler_params=_cp(), name="attn_bwd_prep")(o, dmix, qp, lse_col)


def _attn_bwd(qb, kp, vp, dob, kt):
    t = qb.shape[1]
    bq = min(ATT_BLOCK, t)
    nq = t // bq
    i_tab, j_tab = _triangle(nq, key_major=True)

    def body(it_ref, jt_ref, q_ref, k_ref, v_ref, do_ref, kt_ref, dqt_ref, dk_ref, dv_ref, dk_sc, dv_sc):
        s = pl.program_id(1)
        i, j = it_ref[s], jt_ref[s]

        @pl.when(s == 0)
        def _():
            dqt_ref[...] = jnp.zeros(dqt_ref.shape, F32)

        @pl.when(i == j)
        def _():
            dk_sc[...] = jnp.zeros(dk_sc.shape, F32)
            dv_sc[...] = jnp.zeros(dv_sc.shape, F32)

        cols = pl.ds(pl.multiple_of(i * bq, bq), bq)

        def sweep(masked):
            def scores(h):
                return (lax.dot_general(k_ref[h], q_ref[h], NT, preferred_element_type=F32),
                        lax.dot_general(v_ref[h], do_ref[h], NT, preferred_element_type=F32))

            def accumulate(h, ptb, dstb):
                dv_sc[h] += jnp.dot(ptb, do_ref[h], preferred_element_type=F32)
                dk_sc[h] += jnp.dot(dstb, q_ref[h], preferred_element_type=F32)
                dqt_ref[h, :, cols] += jnp.dot(kt_ref[h], dstb, preferred_element_type=F32)

            ahead, behind = scores(0), None
            for h in range(ATT_BWD_HEADS):
                st, dpt = ahead
                if h + 1 < ATT_BWD_HEADS:
                    ahead = scores(h + 1)
                if behind is not None:
                    accumulate(*behind)
                if masked:
                    key = lax.broadcasted_iota(jnp.int32, (bq, bq), 0)
                    qry = lax.broadcasted_iota(jnp.int32, (bq, bq), 1)
                    st = jnp.where(key <= qry, st, NEG)
                pt = jnp.exp(st)
                behind = (h, pt.astype(BF16), (pt * dpt).astype(BF16))
            accumulate(*behind)

        @pl.when(i == j)
        def _():
            sweep(True)

        @pl.when(i > j)
        def _():
            sweep(False)

        @pl.when(i == nq - 1)
        def _():
            dk_ref[...] = dk_sc[...]
            dv_ref[...] = dv_sc[...].astype(BF16)

    nh = ATT_BWD_HEADS
    qblk = pl.BlockSpec((nh, bq, LANES), lambda hp, s, it, jt: (hp, it[s], 0))
    kblk = pl.BlockSpec((nh, bq, LANES), lambda hp, s, it, jt: (hp, jt[s], 0))
    grid_spec = pltpu.PrefetchScalarGridSpec(
        num_scalar_prefetch=2, grid=(FOX_HEADS // nh, i_tab.shape[0]),
        in_specs=[qblk, kblk, kblk, qblk, pl.BlockSpec((nh, LANES, bq), lambda hp, s, it, jt: (hp, 0, jt[s]))],
        out_specs=[pl.BlockSpec((nh, LANES, t), lambda hp, s, it, jt: (hp, 0, 0), pipeline_mode=pl.Buffered(1)),
                   kblk, kblk],
        scratch_shapes=[pltpu.VMEM((nh, bq, LANES), F32), pltpu.VMEM((nh, bq, LANES), F32)])
    return pl.pallas_call(body, grid_spec=grid_spec,
                          out_shape=[_sds((FOX_HEADS, LANES, t), F32), _sds((FOX_HEADS, t, LANES), F32),
                                     _sds((FOX_HEADS, t, LANES), BF16)],
                          compiler_params=_cp(), name="attn_bwd")(i_tab, j_tab, qb, kp, vp, dob, kt)


def _attn_unpack(dqt, dkp, dvp):
    t = dkp.shape[1]
    tm = min(ROW_TILE, t)
    hd = HEAD_DIM

    def body(dqt_ref, dk_ref, dv_ref, o_ref, dc_ref):
        for h in range(FOX_HEADS):
            dq = dqt_ref[h].T
            o_ref[:, h * hd:(h + 1) * hd] = (dq[:, :hd] * (hd ** -0.5)).astype(BF16)
            o_ref[:, FOX_WIDTH + h * hd:FOX_WIDTH + (h + 1) * hd] = dk_ref[h, :, :hd].astype(BF16)
            o_ref[:, 2 * FOX_WIDTH + h * hd:2 * FOX_WIDTH + (h + 1) * hd] = dv_ref[h, :, :hd]
            dc_ref[:, h:h + 1] = dq[:, K_ONE:K_ONE + 1] - dk_ref[h, :, Q_ONE:Q_ONE + 1]

    row3 = pl.BlockSpec((FOX_HEADS, tm, LANES), lambda i: (0, i, 0))
    return pl.pallas_call(
        body, grid=(t // tm,),
        in_specs=[pl.BlockSpec((FOX_HEADS, LANES, tm), lambda i: (0, 0, i)), row3, row3],
        out_specs=[pl.BlockSpec((tm, QKV), lambda i: (i, 0)), pl.BlockSpec((tm, FOX_HEADS), lambda i: (i, 0))],
        out_shape=[_sds((t, QKV), BF16), _sds((t, FOX_HEADS), F32)],
        compiler_params=_cp(), name="attn_unpack")(dqt, dkp, dvp)


def _adamw(parts, w, m, v, name, layer=None, into=None):
    nl, r, c = w.shape
    tr = r
    for cand in (256, 128, 64, 32, 16):
        if r > cand and r % cand == 0:
            tr = cand
            break
    npart = len(parts)
    bc1 = 1.0 - ADAM_B1 ** ADAM_STEP
    bc2 = 1.0 - ADAM_B2 ** ADAM_STEP

    def body(*refs):
        p_refs = refs[:npart]
        w_ref, m_ref, v_ref = refs[npart:npart + 3]
        g_ref, d_ref, nm_ref, nv_ref = refs[-4:]
        sums = []
        for p_ref in p_refs:
            acc = p_ref[0, 0].astype(F32)
            for s in range(1, p_ref.shape[0]):
                acc = acc + p_ref[s, 0].astype(F32)
            sums.append(acc)
        g = sums[0]
        for extra in sums[1:]:
            g = g + extra
        nm = ADAM_B1 * m_ref[0] + (1.0 - ADAM_B1) * g
        nv = ADAM_B2 * v_ref[0] + (1.0 - ADAM_B2) * (g * g)
        m_hat = nm / bc1
        v_hat = nv / bc2
        g_ref[0] = g
        d_ref[0] = -ADAM_LR * (m_hat / (jnp.sqrt(v_hat) + ADAM_EPS) + ADAM_WD * w_ref[0])
        nm_ref[0] = nm
        nv_ref[0] = nv

    first = 0 if layer is None else layer
    blk = pl.BlockSpec((1, tr, c), lambda l, i: (first + l, i, 0))
    extra = [] if into is None else list(into)
    return pl.pallas_call(
        body, grid=(nl if layer is None else 1, r // tr),
        in_specs=[pl.BlockSpec((p.shape[0], 1, tr, c), lambda l, i: (0, l, i, 0)) for p in parts] + [blk, blk, blk]
        + [_ANY_SPEC] * len(extra),
        out_specs=[blk] * 4, out_shape=[_sds(w.shape, F32)] * 4,
        input_output_aliases={npart + 3 + k: k for k in range(len(extra))},
        compiler_params=_cp(), name=name)(*parts, w, m, v, *extra)


def _to_rows(a):
    flat = a.reshape(-1)
    pad = (-flat.shape[0]) % LANES
    if pad:
        flat = jnp.concatenate([flat, jnp.zeros((pad,), flat.dtype)])
    return flat.reshape(-1, LANES)


def _by_owner_cols(dw):
    k, n = dw.shape
    return dw.reshape(k, N_CHIPS, n // N_CHIPS).transpose(1, 0, 2)[:, None]


def _ffn_fwd(xin_ln, xin_b, wi, wo, g, b, layer):
    gu, h = _ffn_in(xin_b, wi, f"ffn_in_{layer}")
    y_b, xhat, rstd = _mm_res_ln([(h, wo)], xin_ln, g, b, f"ffn_out_ln_{layer}")
    return y_b, (xin_b, gu, h, xhat, rstd)


def _ffn_bwd(dz, saved, wi, wo, ln_below, layer):
    xin_b, gu, h, _, _ = saved
    dgu = _ffn_bwd_hidden(dz, wo, gu, f"ffn_bwd_hidden_{layer}")
    g_out = _mm_tn(h, dz, f"ffn_dw_out_{layer}", tn=D_MODEL, tk=HALF_HIDDEN, tt=REDUCE_TILE // 2)
    g_in = _mm_tn(xin_b, dgu, f"ffn_dw_in_{layer}", tn=HALF_HIDDEN, stack_cols=True)
    below = _mm_nt([(dgu, 0, 0, 0)], [wi], f"ffn_dx_{layer}", tm=FFN_ROW_TILE, res=dz, ln=ln_below)
    return below, g_in, g_out.reshape(N_CHIPS, FFN_HIDDEN // N_CHIPS, D_MODEL)


def kernel(x, even_w_in, even_b_f, even_conv_w, even_w_out, odd_w_in, odd_v_ln_g, odd_v_ln_b, odd_w_s, odd_b_s, odd_w_out, mix_ln_g, mix_ln_b, ffn_w_in, ffn_w_out, ffn_ln_g, ffn_ln_b, loss_target, m_even_w_in, m_even_b_f, m_even_conv_w, m_even_w_out, m_odd_w_in, m_odd_v_ln_g, m_odd_v_ln_b, m_odd_w_s, m_odd_b_s, m_odd_w_out, m_mix_ln_g, m_mix_ln_b, m_ffn_w_in, m_ffn_w_out, m_ffn_ln_g, m_ffn_ln_b, v_even_w_in, v_even_b_f, v_even_conv_w, v_even_w_out, v_odd_w_in, v_odd_v_ln_g, v_odd_v_ln_b, v_odd_w_s, v_odd_b_s, v_odd_w_out, v_mix_ln_g, v_mix_ln_b, v_ffn_w_in, v_ffn_w_out, v_ffn_ln_g, v_ffn_ln_b):
    t = x.shape[1]
    d = D_MODEL
    chip = 2 * lax.axis_index("x") + lax.axis_index("y")
    x2d = x[0]
    target = loss_target[0]

    small_shard = jnp.concatenate([odd_v_ln_g.reshape(2, LANES), odd_v_ln_b.reshape(2, LANES),
                                   even_conv_w.reshape(CONV_K, LANES), jnp.zeros((1, LANES), F32)], axis=0)
    first = [even_w_in[0].astype(BF16), even_w_out[0].astype(BF16), small_shard]
    later = [odd_w_in[0].astype(BF16), odd_w_out[0].astype(BF16), ffn_w_in[0].astype(BF16), ffn_w_in[1].astype(BF16),
             ffn_w_out[0].astype(BF16), ffn_w_out[1].astype(BF16)]
    first_h, first_tok = _split_start(first, "gather4", "gather_first_start")
    later_h, later_tok = _split_start(later, "gather4", "gather_later_start", after=first_tok)
    g_ewi, g_ewo, g_small = [_with_own(g, own) for g, own in
                             zip(_split_wait(first_h, "gather_first_wait", later_tok), first)]
    ewi = g_ewi.transpose(1, 0, 2).reshape(d, EVEN_IN)
    w_even_in = jnp.concatenate([ewi[:, :QKV], ewi[:, QKV + FOX_HEADS:], ewi[:, QKV:QKV + FOX_HEADS],
                                 jnp.zeros((d, LANES - FOX_HEADS), BF16)], axis=1)
    w_even_out = g_ewo.reshape(d, d)
    v_ln_g = g_small[:, 0:2].reshape(1, d)
    v_ln_b = g_small[:, 2:4].reshape(1, d)
    conv_w = g_small[:, 4:7].transpose(1, 0, 2).reshape(CONV_K, CONV_WIDTH)
    chunk_id = jnp.arange(GMLP_BLOCK) // CHUNK
    gmask = chunk_id[None, :] <= chunk_id[:, None]
    w_spatial = jnp.where(gmask[None], odd_w_s[0], 0.0).astype(BF16)
    bs_col = odd_b_s[0].T
    b_f_col = even_b_f.reshape(FOX_HEADS, 1)
    ln = lambda p, l: p[l:l + 1]

    qkv, bch, fl = _proj(x2d, w_even_in, [(0, QKV, BF16), (QKV, QKV + BCH, F32), (QKV + BCH, EVEN_IN_PAD, F32)], "even_proj")
    fl3 = fl[:, :FOX_HEADS].T.reshape(FOX_HEADS, t // LANES, LANES).transpose(1, 0, 2)
    c3 = _fgate_fwd(fl3, b_f_col)
    c_rows = c3.transpose(1, 0, 2).reshape(FOX_HEADS, t)
    qp, kp, vp, kt, vt = _attn_pack(qkv, c_rows.T)
    attn, lse = _attn_fwd(qp, kp, vt)
    conv = _conv_fwd(bch, conv_w)
    x1_b, xh1, rs1 = _mm_res_ln([(attn, w_even_out[:FOX_WIDTH]), (conv, w_even_out[FOX_WIDTH:])], x2d,
                                ln(mix_ln_g, 0), ln(mix_ln_b, 0), "even_out_ln")
    w_odd_in, g_owo, w_fi0, w_fi1, g_fo0, g_fo1 = [_with_own(g, own) for g, own in
                                                   zip(_split_wait(later_h, "gather_later_wait", x1_b), later)]
    w_odd_out = g_owo.reshape(d, d)
    w_ffn_in = [w_fi0, w_fi1]
    w_ffn_out = [g_fo0.reshape(FFN_HIDDEN, d), g_fo1.reshape(FFN_HIDDEN, d)]
    x2_b, ffn0 = _ffn_fwd((xh1, ln(mix_ln_g, 0), ln(mix_ln_b, 0)), x1_b, w_ffn_in[0], w_ffn_out[0],
                          ln(ffn_ln_g, 0), ln(ffn_ln_b, 0), 0)

    sv_odd, rs_odd, gated = _gmlp_fwd(x2_b, w_odd_in, v_ln_g, v_ln_b, w_spatial, bs_col)
    x3_b, xh3, rs3 = _mm_res_ln([(gated, w_odd_out)], (ffn0[3], ln(ffn_ln_g, 0), ln(ffn_ln_b, 0)),
                                ln(mix_ln_g, 1), ln(mix_ln_b, 1), "odd_out_ln")
    _, ffn1 = _ffn_fwd((xh3, ln(mix_ln_g, 1), ln(mix_ln_b, 1)), x3_b, w_ffn_in[1], w_ffn_out[1],
                       ln(ffn_ln_g, 1), ln(ffn_ln_b, 1), 1)

    sq, dz4, d_fg1, d_fb1 = _loss_ln_bwd(ffn1[3], ffn1[4], ln(ffn_ln_g, 1), ln(ffn_ln_b, 1), target)
    loss = lax.psum(0.5 / d * jnp.sum(sq), ("x", "y", "c"))
    (dz3, d_mg1, d_mb1), gi_f1, go_f1 = _ffn_bwd(dz4, ffn1, w_ffn_in[1], w_ffn_out[1], (xh3, rs3, ln(mix_ln_g, 1)), 1)

    dgated = _mm_nt([(dz3, 0, 0, d)], [w_odd_out], "odd_dgated")
    go_odd = _mm_tn(gated, dz3, "odd_dw_out", tn=d).reshape(N_CHIPS, 1, d // N_CHIPS, d)
    da_odd, dws, dbs_col, d_vg, d_vb = _gmlp_bwd(dgated, sv_odd, rs_odd, v_ln_g, v_ln_b, w_spatial, bs_col)
    gi_odd = _mm_tn(x2_b, da_odd, "odd_dw_in", tn=d // 2, stack_cols=True)[:, None]
    dz2, d_fg0, d_fb0 = _mm_nt([(da_odd, 0, 0, 0)], [w_odd_in], "odd_dx", res=dz3,
                               ln=(ffn0[3], ffn0[4], ln(ffn_ln_g, 0)))
    (dz1, d_mg0, d_mb0), gi_f0, go_f0 = _ffn_bwd(dz2, ffn0, w_ffn_in[0], w_ffn_out[0], (xh1, rs1, ln(mix_ln_g, 0)), 0)

    sent_early = [gi_odd, go_odd, gi_f0[:, None], gi_f1[:, None], go_f0[:, None], go_f1[:, None]]
    early_h, early_tok = _split_start(sent_early, "scatter4", "scatter_early_start")
    dmix = _mm_nt([(dz1, 0, 0, d)], [w_even_out], "even_dmix", after=early_tok)
    go_even = jnp.concatenate([_mm_tn(attn, dz1, "even_dw_out_attn", tn=d), _mm_tn(conv, dz1, "even_dw_out_conv", tn=d)],
                              axis=0).reshape(N_CHIPS, 1, d // N_CHIPS, d)
    dbch, dconv_w8 = _conv_bwd(bch, dmix, conv_w)
    qb, dob = _attn_bwd_prep(attn, dmix, qp, lse.reshape(FOX_HEADS, t).T)
    dqkv, dc_col = _attn_unpack(*_attn_bwd(qb, kp, vp, dob, kt))
    dc3 = dc_col.T.reshape(FOX_HEADS, t // LANES, LANES).transpose(1, 0, 2)
    dfl3, d_bf = _fgate_bwd(dc3, fl3, b_f_col)
    dfl = jnp.concatenate([dfl3.transpose(1, 0, 2).reshape(FOX_HEADS, t).T.astype(BF16),
                           jnp.zeros((t, LANES - FOX_HEADS), BF16)], axis=1)

    dws_masked = jnp.where(gmask[None], dws, 0.0)
    rep_names = ["odd_w_s", "odd_b_s", "mix_ln_g", "mix_ln_b", "ffn_ln_g", "ffn_ln_b", "even_b_f"]
    rep_grads = [dws_masked, dbs_col.T, jnp.concatenate([d_mg0, d_mg1]), jnp.concatenate([d_mb0, d_mb1]),
                 jnp.concatenate([d_fg0, d_fg1]), jnp.concatenate([d_fb0, d_fb1]), d_bf.reshape(1, FOX_HEADS)]
    rep_w = [(odd_w_s, m_odd_w_s, v_odd_w_s), (odd_b_s, m_odd_b_s, v_odd_b_s), (mix_ln_g, m_mix_ln_g, v_mix_ln_g),
             (mix_ln_b, m_mix_ln_b, v_mix_ln_b), (ffn_ln_g, m_ffn_ln_g, v_ffn_ln_g), (ffn_ln_b, m_ffn_ln_b, v_ffn_ln_b),
             (even_b_f, m_even_b_f, v_even_b_f)]
    rep_rows = [_to_rows(gr) for gr in rep_grads]
    n_rep = sum(r.shape[0] for r in rep_rows)
    pad_rep = (-n_rep) % SUBLANES
    dconv_w = dconv_w8[:CONV_K].reshape(CONV_K, N_CHIPS, LANES).transpose(1, 0, 2).reshape(N_CHIPS * CONV_K, LANES)
    packed = jnp.concatenate(rep_rows + [jnp.zeros((pad_rep, LANES), F32), d_vg.reshape(SUBLANES, LANES),
                                         d_vb.reshape(SUBLANES, LANES), dconv_w, jnp.zeros((4, LANES), F32)], axis=0)
    small_h, small_tok = _split_start([packed], "gather8", "gather_small_start")

    chip_blk = lambda g: lax.dynamic_index_in_dim(g, chip, 0, keepdims=False)
    mine_early = [_with_own(r, chip_blk(g)) for r, g in
                  zip(_split_wait(early_h, "scatter_early_wait", small_tok), sent_early)]
    swap_h, swap_tok = _split_start(mine_early, "swap2", "swap_early_start")
    dw_qkv = _mm_tn(x2d, dqkv, "even_dw_qkv", tn=QKV // 2, out_dtype=F32, after=swap_tok)
    dw_bch = _mm_tn(x2d, dbch, "even_dw_bch", tn=BCH // 2, out_dtype=F32)
    dw_f = _mm_tn(x2d, dfl, "even_dw_f", tn=LANES, out_dtype=F32)
    gi_even = _by_owner_cols(jnp.concatenate([dw_qkv, dw_f[:, :FOX_HEADS], dw_bch], axis=1).astype(BF16))
    sent_late = [gi_even, go_even]
    late_h, late_tok = _split_start(sent_late, "scatter4", "scatter_late_start")
    grad_x = _mm_nt([(dqkv, 0, 0, QKV), (dbch, 0, QKV, QKV + BCH), (dfl, 0, QKV + BCH, EVEN_IN_PAD)], [w_even_in],
                    "even_dx", res=dz1, after=late_tok)
    mine_late = [_with_own(r, chip_blk(g)) for r, g in zip(_split_wait(late_h, "scatter_late_wait", grad_x), sent_late)]
    theirs_late = _exchange(mine_late, "swap2", "swap_late")
    theirs_early = _split_wait(swap_h, "swap_early_wait", theirs_late[0])
    (gathered,) = _split_wait(small_h, "gather_small_wait", theirs_early[0])
    gathered = lax.dynamic_update_index_in_dim(gathered, packed, 4 * lax.axis_index("x") + 2 * lax.axis_index("y")
                                               + lax.axis_index("c"), 0)
    mine, theirs = mine_late + mine_early, theirs_late + theirs_early
    big_w = [(even_w_in, m_even_w_in, v_even_w_in), (even_w_out, m_even_w_out, v_even_w_out),
             (odd_w_in, m_odd_w_in, v_odd_w_in), (odd_w_out, m_odd_w_out, v_odd_w_out)]
    big_names = ["even_w_in", "even_w_out", "odd_w_in", "odd_w_out"]
    res = {}
    for nm, own, sib, (w, m, v) in zip(big_names, mine, theirs, big_w):
        res[nm] = _adamw([own, sib], w, m, v, f"adamw_{nm}")
    for nm, at, (w, m, v) in (("ffn_w_in", 4, (ffn_w_in, m_ffn_w_in, v_ffn_w_in)),
                              ("ffn_w_out", 6, (ffn_w_out, m_ffn_w_out, v_ffn_w_out))):
        upper = _adamw([mine[at + 1], theirs[at + 1]], w, m, v, f"adamw_{nm}_1", layer=1)
        res[nm] = _adamw([mine[at], theirs[at]], w, m, v, f"adamw_{nm}_0", layer=0, into=upper)

    base = n_rep + pad_rep
    own_rows = jnp.concatenate([
        lax.dynamic_slice_in_dim(gathered, base + 2 * chip, 2, axis=1),
        lax.dynamic_slice_in_dim(gathered, base + SUBLANES + 2 * chip, 2, axis=1),
        lax.dynamic_slice_in_dim(gathered, base + 2 * SUBLANES + CONV_K * chip, CONV_K, axis=1),
        jnp.zeros((N_DEV, 1, LANES), F32)], axis=1)
    small_parts = jnp.concatenate([gathered[:, :base], own_rows], axis=1)[:, None]

    def pack_small(get):
        rows = [_to_rows(get(tw)) for tw in rep_w] + [jnp.zeros((pad_rep, LANES), F32)]
        rows += [get(sh).reshape(-1, LANES) for sh in ((odd_v_ln_g, m_odd_v_ln_g, v_odd_v_ln_g),
                                                       (odd_v_ln_b, m_odd_v_ln_b, v_odd_v_ln_b),
                                                       (even_conv_w, m_even_conv_w, v_even_conv_w))]
        return jnp.concatenate(rows + [jnp.zeros((1, LANES), F32)], axis=0)[None]

    small_out = _adamw([small_parts], pack_small(lambda tw: tw[0]), pack_small(lambda tw: tw[1]),
                       pack_small(lambda tw: tw[2]), "adamw_small")

    def unpack_small(rows3):
        rows = rows3[0]
        out, off = {}, 0
        for nm, (w, _, _), r in zip(rep_names, rep_w, rep_rows):
            out[nm] = rows[off:off + r.shape[0]].reshape(-1)[:w.size].reshape(w.shape)
            off += r.shape[0]
        off += pad_rep
        out["odd_v_ln_g"] = rows[off:off + 2].reshape(odd_v_ln_g.shape)
        out["odd_v_ln_b"] = rows[off + 2:off + 4].reshape(odd_v_ln_b.shape)
        out["even_conv_w"] = rows[off + 4:off + 4 + CONV_K].reshape(even_conv_w.shape)
        return out

    small = [unpack_small(o) for o in small_out]
    order = ["even_w_in", "even_b_f", "even_conv_w", "even_w_out", "odd_w_in", "odd_v_ln_g", "odd_v_ln_b", "odd_w_s",
             "odd_b_s", "odd_w_out", "mix_ln_g", "mix_ln_b", "ffn_w_in", "ffn_w_out", "ffn_ln_g", "ffn_ln_b"]
    outs = [loss, grad_x[None]]
    for kind in range(4):
        for nm in order:
            outs.append(res[nm][kind] if nm in res else small[kind][nm])
    return tuple(outs)
```

```python
import functools
import math

import jax
import jax.numpy as jnp
from jax import lax
from jax.experimental import pallas as pl
from jax.experimental.pallas import tpu as pltpu

F32 = jnp.float32
BF16 = jnp.bfloat16

D_MODEL = 1024
FOX_HEADS = 8
HEAD_DIM = 64
HEAD_PAIRS = FOX_HEADS // 2
FOX_WIDTH = FOX_HEADS * HEAD_DIM
CONV_WIDTH = 512
CONV_K = 3
QKV = 3 * FOX_WIDTH
BCH = 3 * CONV_WIDTH
EVEN_IN = QKV + FOX_HEADS + BCH
EVEN_IN_PAD = QKV + BCH + 128
GMLP_BLOCK = 128
GMLP_GROUPS = 8
CHUNK = 64
FFN_HIDDEN = 2816
HALF_HIDDEN = FFN_HIDDEN // 2
ALPHA = 4.0 ** 0.25
LN_EPS = 1e-5
ADAM_LR = 0.001
ADAM_B1 = 0.9
ADAM_B2 = 0.999
ADAM_EPS = 1e-08
ADAM_WD = 0.01
ADAM_STEP = 10
N_CHIPS = 4
N_DEV = 8
LANES = 128
SUBLANES = 8
ROW_TILE = 512
FFN_ROW_TILE = 512
REDUCE_TILE = 2048
ATT_BLOCK = 512
ATT_FWD_HEADS = 8
ATT_BWD_HEADS = 4
ADAMW_BLOCK_BYTES = 2 ** 20
VMEM_LIMIT = 56 * 2 ** 20
NEG = -1e30
MESH = pl.DeviceIdType.MESH
HIGHEST = lax.Precision.HIGHEST
Q_C, Q_ONE, Q_LSE = 64, 67, 70
K_ONE, K_C, K_ONE2 = 64, 67, 70
V_ONE = 64
DO_DELTA = 65
NT = (((1,), (1,)), ((), ()))
TN = (((0,), (0,)), ((), ()))


def _cp():
    return pltpu.CompilerParams(vmem_limit_bytes=VMEM_LIMIT)


def _resident(shape):
    zeros = (0,) * len(shape)
    return pl.BlockSpec(shape, lambda *_: zeros, pipeline_mode=pl.Buffered(1))


def _sds(shape, dtype):
    return jax.ShapeDtypeStruct(tuple(shape), dtype)


_MASKS = {
    "gather4": [(1, 0, 0), (0, 1, 0), (1, 1, 0)],
    "scatter4": [(1, 0, 0), (0, 1, 0), (1, 1, 0)],
    "swap2": [(0, 0, 1)],
    "gather8": [(0, 0, 1), (0, 1, 0), (0, 1, 1), (1, 0, 0), (1, 0, 1), (1, 1, 0), (1, 1, 1)],
}


def _exchange(arrs, mode, name):
    n = len(arrs)
    masks = _MASKS[mode]
    npeer = len(masks)
    lead = {"gather4": N_CHIPS, "gather8": N_DEV}.get(mode)
    out_shapes = [_sds(((lead,) if lead else ()) + a.shape, a.dtype) for a in arrs]

    def body(*refs):
        ins, outs = refs[:n], refs[n:2 * n]
        send_sems, recv_sems, loc_sems = refs[2 * n:]
        x, y, c = lax.axis_index("x"), lax.axis_index("y"), lax.axis_index("c")
        chip, dev = 2 * x + y, 4 * x + 2 * y + c
        sends, recvs, locs = [], [], []
        for k in range(n):
            if mode == "gather4":
                locs.append(pltpu.make_async_copy(ins[k], outs[k].at[chip], loc_sems.at[k]))
            elif mode == "scatter4":
                locs.append(pltpu.make_async_copy(ins[k].at[chip], outs[k].at[chip], loc_sems.at[k]))
            elif mode == "gather8":
                locs.append(pltpu.make_async_copy(ins[k], outs[k].at[dev], loc_sems.at[k]))
        for cp in locs:
            cp.start()
        for k in range(n):
            for j, (dx, dy, dc) in enumerate(masks):
                px = 1 - x if dx else x
                py = 1 - y if dy else y
                pc = 1 - c if dc else c
                pchip, pdev = 2 * px + py, 4 * px + 2 * py + pc
                if mode == "gather4":
                    src, dst, land = ins[k], outs[k].at[chip], outs[k].at[pchip]
                elif mode == "scatter4":
                    src, dst, land = ins[k].at[pchip], outs[k].at[chip], outs[k].at[pchip]
                elif mode == "swap2":
                    src, dst, land = ins[k], outs[k], outs[k]
                else:
                    src, dst, land = ins[k], outs[k].at[dev], outs[k].at[pdev]
                s = k * npeer + j
                kw = dict(send_sem=send_sems.at[s], recv_sem=recv_sems.at[s], device_id=(px, py, pc),
                          device_id_type=MESH)
                cp = pltpu.make_async_remote_copy(src_ref=src, dst_ref=dst, **kw)
                cp.start()
                sends.append(cp)
                recvs.append(pltpu.make_async_remote_copy(src_ref=src, dst_ref=land, **kw))
        for cp in recvs:
            cp.wait_recv()
        for cp in sends:
            cp.wait_send()
        for cp in locs:
            cp.wait()

    any_spec = pl.BlockSpec(memory_space=pl.ANY)
    outs = pl.pallas_call(
        body,
        out_shape=out_shapes,
        in_specs=[any_spec] * n,
        out_specs=[any_spec] * n,
        scratch_shapes=[pltpu.SemaphoreType.DMA((n * npeer,)), pltpu.SemaphoreType.DMA((n * npeer,)),
                        pltpu.SemaphoreType.DMA((max(n, 1),))],
        name=name,
    )(*arrs)
    return list(outs)


_HBM_SPEC = pl.BlockSpec(memory_space=pltpu.HBM)
_SEM_SPEC = pl.BlockSpec(memory_space=pltpu.SEMAPHORE)
_ANY_SPEC = pl.BlockSpec(memory_space=pl.ANY)
_EFFECT = pltpu.SideEffectType.DATAFLOW_SIDE_EFFECTING


def _split_copies(mode, ins, lands, send_sems, recv_sems):
    x, y, c = lax.axis_index("x"), lax.axis_index("y"), lax.axis_index("c")
    chip, dev = 2 * x + y, 4 * x + 2 * y + c
    masks = _MASKS[mode]
    out = []
    for k in range(len(ins)):
        for j, (dx, dy, dc) in enumerate(masks):
            px = 1 - x if dx else x
            py = 1 - y if dy else y
            pc = 1 - c if dc else c
            pchip, pdev = 2 * px + py, 4 * px + 2 * py + pc
            if mode == "gather4":
                src, dst, land = ins[k], lands[k].at[chip], lands[k].at[pchip]
            elif mode == "scatter4":
                src, dst, land = ins[k].at[pchip], lands[k].at[chip], lands[k].at[pchip]
            elif mode == "swap2":
                src, dst, land = ins[k], lands[k], lands[k]
            else:
                src, dst, land = ins[k], lands[k].at[dev], lands[k].at[pdev]
            s = k * len(masks) + j
            kw = dict(send_sem=send_sems.at[s], recv_sem=recv_sems.at[s], device_id=(px, py, pc), device_id_type=MESH)
            out.append((pltpu.make_async_remote_copy(src_ref=src, dst_ref=dst, **kw),
                        pltpu.make_async_remote_copy(src_ref=src, dst_ref=land, **kw)))
    return out


def _split_start(arrs, mode, name, after=None):
    n = len(arrs)
    nsem = n * len(_MASKS[mode])
    lead = {"gather4": (N_CHIPS,), "gather8": (N_DEV,)}.get(mode, ())
    land_shapes = [lead + a.shape for a in arrs]

    def body(*refs):
        ins, lands = refs[:n], refs[n:2 * n]
        outs = refs[2 * n + (after is not None):]
        for start, _ in _split_copies(mode, ins, lands, outs[0], outs[1]):
            start.start()
        outs[-1][...] = jnp.zeros(outs[-1].shape, F32)

    srcs = [pltpu.with_memory_space_constraint(a, pltpu.HBM) for a in arrs]
    empties = [pltpu.with_memory_space_constraint(lax.empty(s, a.dtype), pltpu.HBM) for s, a in zip(land_shapes, arrs)]
    res = pl.pallas_call(
        body, name=name,
        out_shape=(pltpu.SemaphoreType.DMA((nsem,)), pltpu.SemaphoreType.DMA((nsem,)),
                   *[pltpu.HBM(a.shape, a.dtype) for a in arrs],
                   *[pltpu.HBM(s, a.dtype) for s, a in zip(land_shapes, arrs)],
                   _sds((SUBLANES, LANES), F32)),
        in_specs=[_HBM_SPEC] * (2 * n) + ([_ANY_SPEC] if after is not None else []),
        out_specs=(_SEM_SPEC, _SEM_SPEC, *[_HBM_SPEC] * (2 * n), pl.BlockSpec(memory_space=pltpu.VMEM)),
        input_output_aliases={k: 2 + k for k in range(2 * n)},
        compiler_params=pltpu.CompilerParams(has_side_effects=_EFFECT),
    )(*srcs, *empties, *([after] if after is not None else []))
    return dict(mode=mode, n=n, sems=res[:2], bufs=res[2:2 + 2 * n]), res[-1]


def _split_wait(handle, name, after):
    n, mode = handle["n"], handle["mode"]

    def body(*refs):
        ins, lands = refs[:n], refs[n:2 * n]
        send_sems, recv_sems = refs[2 * n], refs[2 * n + 1]
        for _, arrival in _split_copies(mode, ins, lands, send_sems, recv_sems):
            arrival.wait_send()
            arrival.wait_recv()

    bufs = handle["bufs"]
    res = pl.pallas_call(
        body, name=name,
        out_shape=tuple(pltpu.HBM(b.shape, b.dtype) for b in bufs),
        in_specs=[_HBM_SPEC] * (2 * n) + [_SEM_SPEC, _SEM_SPEC, _ANY_SPEC],
        out_specs=tuple([_HBM_SPEC] * (2 * n)),
        input_output_aliases={k: k for k in range(2 * n)},
        compiler_params=pltpu.CompilerParams(has_side_effects=_EFFECT),
    )(*bufs, *handle["sems"], after)
    return list(res[n:])


def _with_own(landed, own):
    chip = 2 * lax.axis_index("x") + lax.axis_index("y")
    return lax.dynamic_update_index_in_dim(landed, own, chip, 0)


def _sigmoid(x):
    return 0.5 * jnp.tanh(0.5 * x) + 0.5


def _log_sigmoid(x):
    e = jnp.exp(-jnp.abs(x))
    log1p = jnp.where(e < 1e-2, e * (1.0 - e * (0.5 - e * (1.0 / 3.0))), jnp.log(1.0 + e))
    return jnp.minimum(x, 0.0) - log1p


def _ln_fwd(z):
    mu = jnp.mean(z, axis=-1, keepdims=True)
    zc = z - mu
    var = jnp.mean(zc * zc, axis=-1, keepdims=True)
    rstd = lax.rsqrt(var + LN_EPS)
    return zc * rstd, rstd


def _ln_bwd(dy, xhat, rstd, g):
    dxh = dy * g
    m1 = jnp.mean(dxh, axis=-1, keepdims=True)
    m2 = jnp.mean(dxh * xhat, axis=-1, keepdims=True)
    dz = rstd * (dxh - m1 - xhat * m2)
    return dz, jnp.sum(dy * xhat, axis=0, keepdims=True), jnp.sum(dy, axis=0, keepdims=True)


def _shift_down(z, halo):
    r = lax.broadcasted_iota(jnp.int32, z.shape, 0)
    z1 = jnp.where(r == 0, halo[7:8, :], pltpu.roll(z, 1, 0))
    z2 = jnp.where(r == 0, halo[6:7, :], jnp.where(r == 1, halo[7:8, :], pltpu.roll(z, 2, 0)))
    return z1, z2


def _shift_up(z, halo):
    n = z.shape[0]
    r = lax.broadcasted_iota(jnp.int32, z.shape, 0)
    z1 = jnp.where(r == n - 1, halo[0:1, :], pltpu.roll(z, n - 1, 0))
    z2 = jnp.where(r == n - 1, halo[1:2, :], jnp.where(r == n - 2, halo[0:1, :], pltpu.roll(z, n - 2, 0)))
    return z1, z2


def _accumulate(ref, first, value):
    @pl.when(first)
    def _():
        ref[...] = value

    @pl.when(jnp.logical_not(first))
    def _():
        ref[...] += value


def _proj(x, w, splits, name):
    t, k = x.shape
    tm = min(ROW_TILE, t)

    def body(x_ref, w_ref, *outs):
        a = x_ref[...].astype(BF16)
        for (lo, hi, dt), o in zip(splits, outs):
            o[...] = jnp.dot(a, w_ref[:, lo:hi], preferred_element_type=F32).astype(dt)

    return pl.pallas_call(
        body, grid=(t // tm,),
        in_specs=[pl.BlockSpec((tm, k), lambda i: (i, 0)), _resident(w.shape)],
        out_specs=[pl.BlockSpec((tm, hi - lo), lambda i: (i, 0)) for lo, hi, _ in splits],
        out_shape=[_sds((t, hi - lo), dt) for lo, hi, dt in splits],
        compiler_params=_cp(), name=name)(x, w)


def _fgate_fwd(fl3, b_f):
    nc = fl3.shape[0]

    def body(f_ref, b_ref, c_ref):
        r = lax.broadcasted_iota(jnp.int32, (LANES, LANES), 0)
        cidx = lax.broadcasted_iota(jnp.int32, (LANES, LANES), 1)
        upper = (r <= cidx).astype(F32)

        def step(i, carry):
            lf = _log_sigmoid(f_ref[i] + b_ref[...])
            cc = jnp.dot(lf, upper, precision=HIGHEST, preferred_element_type=F32) + carry
            c_ref[i] = cc
            return cc[:, LANES - 1:LANES]

        lax.fori_loop(0, nc, step, jnp.zeros((FOX_HEADS, 1), F32))

    return pl.pallas_call(body, out_shape=_sds(fl3.shape, F32), name="fgate_fwd")(fl3, b_f)


def _fgate_bwd(dc3, fl3, b_f):
    nc = fl3.shape[0]

    def body(dc_ref, f_ref, b_ref, df_ref, db_ref):
        r = lax.broadcasted_iota(jnp.int32, (LANES, LANES), 0)
        cidx = lax.broadcasted_iota(jnp.int32, (LANES, LANES), 1)
        lower = (r >= cidx).astype(F32)

        def step(n, carry):
            suffix, db = carry
            i = nc - 1 - n
            dlf = jnp.dot(dc_ref[i], lower, precision=HIGHEST, preferred_element_type=F32) + suffix
            df = dlf * (1.0 - _sigmoid(f_ref[i] + b_ref[...]))
            df_ref[i] = df
            return dlf[:, 0:1], db + jnp.sum(df, axis=1, keepdims=True)

        zero = jnp.zeros((FOX_HEADS, 1), F32)
        _, db = lax.fori_loop(0, nc, step, (zero, zero))
        db_ref[...] = db

    return pl.pallas_call(body, out_shape=[_sds(fl3.shape, F32), _sds((FOX_HEADS, 1), F32)],
                          name="fgate_bwd")(dc3, fl3, b_f)


def _split3(c):
    hi = c.astype(BF16).astype(F32)
    mid = (c - hi).astype(BF16).astype(F32)
    lo = (c - hi - mid).astype(BF16).astype(F32)
    return hi, mid, lo


def _lane_pieces(lane, start, pieces, sign):
    out = jnp.zeros(lane.shape, F32)
    for n, p in enumerate(pieces):
        out = jnp.where(lane == start + n, sign * p, out)
    return out


def _attn_pack(qkv, c_col):
    t = qkv.shape[0]
    tm = min(ROW_TILE, t)
    hd = HEAD_DIM

    def body(x_ref, c_ref, qp_ref, kp_ref, vp_ref, kt_ref, vt_ref):
        lane = lax.broadcasted_iota(jnp.int32, (tm, hd), 1) + hd
        for h in range(FOX_HEADS):
            pieces = _split3(c_ref[:, h:h + 1])
            ones = lambda a, b: jnp.where(jnp.logical_and(lane >= a, lane < b), 1.0, 0.0)
            q_extra = _lane_pieces(lane, Q_C, pieces, 1.0) + ones(Q_ONE, Q_ONE + 3)
            k_extra = _lane_pieces(lane, K_C, pieces, -1.0) + ones(K_ONE, K_ONE + 3) + ones(K_ONE2, K_ONE2 + 3)
            qp_ref[h, :, :hd] = (x_ref[:, h * hd:(h + 1) * hd].astype(F32) * (hd ** -0.5)).astype(BF16)
            qp_ref[h, :, hd:] = q_extra.astype(BF16)
            kp_ref[h, :, :hd] = x_ref[:, FOX_WIDTH + h * hd:FOX_WIDTH + (h + 1) * hd]
            kp_ref[h, :, hd:] = k_extra.astype(BF16)
            vp_ref[h, :, :hd] = x_ref[:, 2 * FOX_WIDTH + h * hd:2 * FOX_WIDTH + (h + 1) * hd]
            vp_ref[h, :, hd:] = ones(V_ONE, V_ONE + 4).astype(BF16)
            kt_ref[h] = kp_ref[h].astype(F32).T.astype(BF16)
            vt_ref[h] = vp_ref[h].astype(F32).T.astype(BF16)

    row3 = pl.BlockSpec((FOX_HEADS, tm, LANES), lambda i: (0, i, 0))
    col3 = pl.BlockSpec((FOX_HEADS, LANES, tm), lambda i: (0, 0, i))
    return pl.pallas_call(
        body, grid=(t // tm,),
        in_specs=[pl.BlockSpec((tm, QKV), lambda i: (i, 0)), pl.BlockSpec((tm, FOX_HEADS), lambda i: (i, 0))],
        out_specs=[row3, row3, row3, col3, col3],
        out_shape=[_sds((FOX_HEADS, t, LANES), BF16)] * 3 + [_sds((FOX_HEADS, LANES, t), BF16)] * 2,
        compiler_params=_cp(), name="attn_pack")(qkv, c_col)


def _triangle(nq, key_major):
    if key_major:
        pairs = [(i, j) for j in range(nq) for i in range(j, nq)]
    else:
        pairs = [(i, j) for i in range(nq) for j in range(i + 1)]
    return jnp.asarray([p[0] for p in pairs], jnp.int32), jnp.asarray([p[1] for p in pairs], jnp.int32)


def _attn_fwd(qp, kp, vt):
    t = qp.shape[1]
    bq = min(ATT_BLOCK, t)
    nq = t // bq
    nh = ATT_FWD_HEADS
    i_tab, j_tab = _triangle(nq, key_major=False)

    def body(it_ref, jt_ref, q_ref, k_ref, vt_ref, o_ref, lse_ref, m_sc, acc_sc):
        s = pl.program_id(1)
        i, j = it_ref[s], jt_ref[s]

        @pl.when(j == 0)
        def _():
            m_sc[...] = jnp.full(m_sc.shape, NEG, F32)
            acc_sc[...] = jnp.zeros(acc_sc.shape, F32)

        def sweep(masked):
            scores = lambda h: lax.dot_general(k_ref[h], q_ref[h], NT, preferred_element_type=F32)

            def accumulate(h, pt, rescale):
                acc_sc[h] = rescale * acc_sc[h] + jnp.dot(vt_ref[h], pt, preferred_element_type=F32)

            ahead, behind = scores(0), None
            for h in range(nh):
                st = ahead
                if h + 1 < nh:
                    ahead = scores(h + 1)
                if behind is not None:
                    accumulate(*behind)
                if masked:
                    key = lax.broadcasted_iota(jnp.int32, (bq, bq), 0)
                    qry = lax.broadcasted_iota(jnp.int32, (bq, bq), 1)
                    st = jnp.where(key <= qry, st, NEG)
                m_prev = m_sc[h]
                m_new = jnp.maximum(m_prev, jnp.max(st, axis=0, keepdims=True))
                behind = (h, jnp.exp(st - m_new).astype(BF16), jnp.exp(m_prev - m_new))
                m_sc[h] = m_new
            accumulate(*behind)

        @pl.when(j < i)
        def _():
            sweep(False)

        @pl.when(j == i)
        def _():
            sweep(True)
            for h in range(nh):
                acc = acc_sc[h]
                denom = acc[V_ONE:V_ONE + 1, :]
                o_ref[:, h * HEAD_DIM:(h + 1) * HEAD_DIM] = (acc[:HEAD_DIM, :] / denom).T.astype(BF16)
                lse_ref[h] = m_sc[h] + jnp.log(denom)

    grid_spec = pltpu.PrefetchScalarGridSpec(
        num_scalar_prefetch=2, grid=(FOX_HEADS // nh, i_tab.shape[0]),
        in_specs=[pl.BlockSpec((nh, bq, LANES), lambda hp, s, it, jt: (hp, it[s], 0)),
                  pl.BlockSpec((nh, bq, LANES), lambda hp, s, it, jt: (hp, jt[s], 0)),
                  pl.BlockSpec((nh, LANES, bq), lambda hp, s, it, jt: (hp, 0, jt[s]))],
        out_specs=[pl.BlockSpec((bq, nh * HEAD_DIM), lambda hp, s, it, jt: (it[s], hp)),
                   pl.BlockSpec((nh, 1, bq), lambda hp, s, it, jt: (hp, 0, it[s]))],
        scratch_shapes=[pltpu.VMEM((nh, 1, bq), F32), pltpu.VMEM((nh, LANES, bq), F32)])
    return pl.pallas_call(body, grid_spec=grid_spec,
                          out_shape=[_sds((t, FOX_WIDTH), BF16), _sds((FOX_HEADS, 1, t), F32)],
                          compiler_params=_cp(), name="attn_fwd")(i_tab, j_tab, qp, kp, vt)


def _conv_fwd(bch, conv_w):
    t = bch.shape[0]
    tm = min(ROW_TILE, t)
    halo_blocks = tm // SUBLANES
    cw = CONV_WIDTH

    def body(cur_ref, prev_ref, w_ref, o_ref):
        i = pl.program_id(0)
        z = cur_ref[:, cw:2 * cw] * cur_ref[:, 2 * cw:]
        zp = jnp.where(i == 0, 0.0, prev_ref[:, cw:2 * cw] * prev_ref[:, 2 * cw:])
        z1, z2 = _shift_down(z, zp)
        y = w_ref[0:1, :] * z2 + w_ref[1:2, :] * z1 + w_ref[2:3, :] * z
        o_ref[...] = (cur_ref[:, :cw] * y).astype(BF16)

    return pl.pallas_call(
        body, grid=(t // tm,),
        in_specs=[pl.BlockSpec((tm, BCH), lambda i: (i, 0)),
                  pl.BlockSpec((SUBLANES, BCH), lambda i: (jnp.maximum(i * halo_blocks - 1, 0), 0)),
                  _resident(conv_w.shape)],
        out_specs=pl.BlockSpec((tm, cw), lambda i: (i, 0)),
        out_shape=_sds((t, cw), BF16), compiler_params=_cp(), name="conv_fwd")(bch, bch, conv_w)


def _mm_res_ln(pairs, res, g, b, name):
    from_ln = isinstance(res, tuple)
    res_args = list(res) if from_ln else [res]
    t, d = res_args[0].shape
    tm = min(ROW_TILE, t)
    n = len(pairs)

    def body(*refs):
        a_refs, w_refs = refs[:n], refs[n:2 * n]
        res_refs = refs[2 * n:2 * n + len(res_args)]
        g_ref, b_ref, yb_ref, xh_ref, rs_ref = refs[2 * n + len(res_args):]
        r = res_refs[0][...]
        if from_ln:
            r = r * res_refs[1][...] + res_refs[2][...]
        z = ALPHA * r
        for a_ref, w_ref in zip(a_refs, w_refs):
            z = z + jnp.dot(a_ref[...].astype(BF16), w_ref[...], preferred_element_type=F32)
        xhat, rstd = _ln_fwd(z)
        yb_ref[...] = (xhat * g_ref[...] + b_ref[...]).astype(BF16)
        xh_ref[...] = xhat
        rs_ref[...] = rstd

    row = lambda i: (i, 0)
    full = pl.BlockSpec((tm, d), row)
    return pl.pallas_call(
        body, grid=(t // tm,),
        in_specs=[pl.BlockSpec((tm, a.shape[1]), row) for a, _ in pairs] + [_resident(w.shape) for _, w in pairs]
        + [full] + [_resident(a.shape) for a in res_args[1:]] + [_resident(g.shape), _resident(b.shape)],
        out_specs=[full, full, pl.BlockSpec((tm, 1), row)],
        out_shape=[_sds((t, d), BF16), _sds((t, d), F32), _sds((t, 1), F32)],
        compiler_params=_cp(), name=name)(*[a for a, _ in pairs], *[w for _, w in pairs], *res_args, g, b)


def _ffn_in(x, wi, name):
    t, d = x.shape
    tm = min(FFN_ROW_TILE, t)
    hh = HALF_HIDDEN

    def body(x_ref, w_ref, gu_ref, h_ref):
        a = x_ref[...].astype(BF16)
        for c in range(2):
            gs, us = slice(c * hh, (c + 1) * hh), slice(FFN_HIDDEN + c * hh, FFN_HIDDEN + (c + 1) * hh)
            g = jnp.dot(a, w_ref[c], preferred_element_type=F32)
            u = jnp.dot(a, w_ref[2 + c], preferred_element_type=F32)
            sig = _sigmoid(g)
            silu = g * sig
            gu_ref[:, gs] = (u * sig * (1.0 + g * (1.0 - sig))).astype(BF16)
            gu_ref[:, us] = silu.astype(BF16)
            h_ref[:, gs] = (silu * u).astype(BF16)

    row = lambda i: (i, 0)
    return pl.pallas_call(
        body, grid=(t // tm,),
        in_specs=[pl.BlockSpec((tm, d), row), _resident(wi.shape)],
        out_specs=[pl.BlockSpec((tm, 2 * FFN_HIDDEN), row), pl.BlockSpec((tm, FFN_HIDDEN), row)],
        out_shape=[_sds((t, 2 * FFN_HIDDEN), BF16), _sds((t, FFN_HIDDEN), BF16)],
        compiler_params=_cp(), name=name)(x, wi)


def _gmlp_fwd(x, w_in, vg, vb, wm, bs_col):
    t, d = x.shape
    tm = min(ROW_TILE, t)
    gb = GMLP_BLOCK

    def body(x_ref, w_ref, vg_ref, vb_ref, wm_ref, bs_ref, sv_ref, rs_ref, o_ref, a_sc):
        xb = x_ref[...].astype(BF16)
        nc = w_ref.shape[2]
        for j in range(w_ref.shape[0]):
            a_sc[:, j * nc:(j + 1) * nc] = jnp.dot(xb, w_ref[j], preferred_element_type=F32)
        halves = []
        for half in range(2):
            a = a_sc[:, half * d:(half + 1) * d]
            cdf = 0.5 * (1.0 + lax.erf(a * (2.0 ** -0.5)))
            halves.append(a * cdf)
            slope = cdf + a * (jnp.exp(-0.5 * a * a) * (1.0 / math.sqrt(2.0 * math.pi)))
            sv_ref[:, (2 * half + 1) * d:(2 * half + 2) * d] = slope.astype(BF16)
        u = halves[0]
        vhat, rstd = _ln_fwd(halves[1])
        sv_ref[:, :d] = u.astype(BF16)
        sv_ref[:, 2 * d:3 * d] = vhat.astype(BF16)
        rs_ref[...] = rstd
        vln = (vhat * vg_ref[...] + vb_ref[...]).astype(BF16)
        for blk in range(tm // gb):
            rs = slice(blk * gb, (blk + 1) * gb)
            for gi in range(GMLP_GROUPS):
                cs = slice(gi * gb, (gi + 1) * gb)
                s = jnp.dot(wm_ref[gi], vln[rs, cs], preferred_element_type=F32) + bs_ref[:, gi:gi + 1]
                o_ref[rs, cs] = (u[rs, cs] * s).astype(BF16)

    row = lambda i: (i, 0)
    return pl.pallas_call(
        body, grid=(t // tm,),
        in_specs=[pl.BlockSpec((tm, d), row), _resident(w_in.shape), _resident(vg.shape), _resident(vb.shape),
                  _resident(wm.shape), _resident(bs_col.shape)],
        out_specs=[pl.BlockSpec((tm, 4 * d), row), pl.BlockSpec((tm, 1), row), pl.BlockSpec((tm, d), row)],
        out_shape=[_sds((t, 4 * d), BF16), _sds((t, 1), F32), _sds((t, d), BF16)],
        scratch_shapes=[pltpu.VMEM((tm, 2 * d), F32)],
        compiler_params=_cp(), name="gmlp_fwd")(x, w_in, vg, vb, wm, bs_col)


def _loss_ln_bwd(xhat, rstd, g, b, target):
    t, d = xhat.shape
    tm = min(ROW_TILE, t)

    def body(xh_ref, rs_ref, g_ref, b_ref, t_ref, sq_ref, dz_ref, dg_ref, db_ref):
        first = pl.program_id(0) == 0
        xh = xh_ref[...]
        err = xh * g_ref[...] + b_ref[...] - t_ref[...]
        dz, dg, db = _ln_bwd(err * (1.0 / d), xh, rs_ref[...], g_ref[...])
        dz_ref[...] = dz
        _accumulate(sq_ref, first, jnp.sum(err * err, axis=0, keepdims=True))
        _accumulate(dg_ref, first, dg)
        _accumulate(db_ref, first, db)

    row = lambda i: (i, 0)
    vec = pl.BlockSpec((1, d), lambda i: (0, 0))
    return pl.pallas_call(
        body, grid=(t // tm,),
        in_specs=[pl.BlockSpec((tm, d), row), pl.BlockSpec((tm, 1), row), _resident(g.shape), _resident(b.shape),
                  pl.BlockSpec((tm, d), row)],
        out_specs=[vec, pl.BlockSpec((tm, d), row), vec, vec],
        out_shape=[_sds((1, d), F32), _sds((t, d), F32), _sds((1, d), F32), _sds((1, d), F32)],
        compiler_params=_cp(), name="loss_ln_bwd")(xhat, rstd, g, b, target)


def _mm_nt(pairs, ws, name, *, tm=ROW_TILE, res=None, ln=None, out_dtype=F32, after=None):
    t = pairs[0][0].shape[0]
    k = ws[0].shape[-2]
    tm = min(tm, t)
    n, nw = len(pairs), len(ws)

    def body(*refs):
        refs = refs[after is not None:]
        a_refs, w_refs = refs[:n], refs[n:n + nw]
        rest = list(refs[n + nw:])
        dx = None
        for a_ref, (_, wi, lo, hi) in zip(a_refs, pairs):
            w_ref = w_refs[wi]
            if len(w_ref.shape) == 3:
                nc = w_ref.shape[2]
                parts = [lax.dot_general(a_ref[:, j * nc:(j + 1) * nc].astype(BF16), w_ref[j], NT,
                                         preferred_element_type=F32) for j in range(w_ref.shape[0])]
            else:
                parts = [lax.dot_general(a_ref[...].astype(BF16), w_ref[:, lo:hi], NT, preferred_element_type=F32)]
            for part in parts:
                dx = part if dx is None else dx + part
        if res is not None:
            dx = dx + ALPHA * rest.pop(0)[...]
        if ln is None:
            rest[0][...] = dx.astype(out_dtype)
            return
        xh_ref, rs_ref, g_ref, dz_ref, dg_ref, db_ref = rest
        first = pl.program_id(0) == 0
        dz, dg, db = _ln_bwd(dx, xh_ref[...], rs_ref[...], g_ref[...])
        dz_ref[...] = dz
        _accumulate(dg_ref, first, dg)
        _accumulate(db_ref, first, db)

    row = lambda i: (i, 0)
    in_specs = [pl.BlockSpec((tm, a.shape[1]), row) for a, _, _, _ in pairs] + [_resident(w.shape) for w in ws]
    args = [a for a, _, _, _ in pairs] + list(ws)
    if res is not None:
        in_specs.append(pl.BlockSpec((tm, k), row))
        args.append(res)
    if ln is None:
        out_specs = pl.BlockSpec((tm, k), row)
        out_shape = _sds((t, k), out_dtype)
    else:
        xhat, rstd, g = ln
        in_specs += [pl.BlockSpec((tm, k), row), pl.BlockSpec((tm, 1), row), _resident(g.shape)]
        args += [xhat, rstd, g]
        vec = pl.BlockSpec((1, k), lambda i: (0, 0))
        out_specs = [pl.BlockSpec((tm, k), row), vec, vec]
        out_shape = [_sds((t, k), F32), _sds((1, k), F32), _sds((1, k), F32)]
    if after is not None:
        in_specs.insert(0, _ANY_SPEC)
        args.insert(0, after)
    return pl.pallas_call(body, grid=(t // tm,), in_specs=in_specs, out_specs=out_specs, out_shape=out_shape,
                          compiler_params=_cp(), name=name)(*args)


def _mm_tn(a, b, name, *, tn, tk=None, tt=None, stack_cols=False, out_dtype=BF16, after=None):
    t, k = a.shape
    n = b.shape[1]
    tk = k if tk is None else tk
    tt = min(REDUCE_TILE if tt is None else tt, t)
    nt = t // tt

    def body(a_ref, b_ref, *rest):
        o_ref, acc_ref = rest[after is not None:]
        s = pl.program_id(2)
        part = lax.dot_general(a_ref[...].astype(BF16), b_ref[...].astype(BF16), TN, preferred_element_type=F32)
        _accumulate(acc_ref, s == 0, part)

        @pl.when(s == nt - 1)
        def _():
            o_ref[...] = acc_ref[...].astype(out_dtype).reshape(o_ref.shape)

    if stack_cols:
        assert tk == k
        out_spec = pl.BlockSpec((1, k, tn), lambda kk, j, s: (j, 0, 0))
        out_shape = _sds((n // tn, k, tn), out_dtype)
    else:
        out_spec = pl.BlockSpec((tk, tn), lambda kk, j, s: (kk, j))
        out_shape = _sds((k, n), out_dtype)
    return pl.pallas_call(
        body, grid=(k // tk, n // tn, nt),
        in_specs=[pl.BlockSpec((tt, tk), lambda kk, j, s: (s, kk)), pl.BlockSpec((tt, tn), lambda kk, j, s: (s, j))]
        + ([_ANY_SPEC] if after is not None else []),
        out_specs=out_spec, out_shape=out_shape,
        scratch_shapes=[pltpu.VMEM((tk, tn), F32)],
        compiler_params=_cp(), name=name)(a, b, *([after] if after is not None else []))


def _ffn_bwd_hidden(dz, wo, gu, name):
    t, d = dz.shape
    tm = min(FFN_ROW_TILE, t)
    hh = HALF_HIDDEN

    def body(dz_ref, w_ref, gu_ref, o_ref):
        a = dz_ref[...].astype(BF16)
        for c in range(2):
            gs, us = slice(c * hh, (c + 1) * hh), slice(FFN_HIDDEN + c * hh, FFN_HIDDEN + (c + 1) * hh)
            dh = lax.dot_general(a, w_ref[gs, :], NT, preferred_element_type=F32)
            o_ref[:, gs] = (dh * gu_ref[:, gs].astype(F32)).astype(BF16)
            o_ref[:, us] = (dh * gu_ref[:, us].astype(F32)).astype(BF16)

    row = lambda i: (i, 0)
    return pl.pallas_call(
        body, grid=(t // tm,),
        in_specs=[pl.BlockSpec((tm, d), row), _resident(wo.shape), pl.BlockSpec((tm, 2 * FFN_HIDDEN), row)],
        out_specs=pl.BlockSpec((tm, 2 * FFN_HIDDEN), row),
        out_shape=_sds((t, 2 * FFN_HIDDEN), BF16), compiler_params=_cp(), name=name)(dz, wo, gu)


def _gmlp_bwd(dgated, saved, rstd_v, vg, vb, wm, bs_col):
    t, d = dgated.shape
    d2 = 2 * d
    tm = min(ROW_TILE, t)
    gb = GMLP_BLOCK

    def body(dg_ref, sv_ref, rs_ref, vg_ref, vb_ref, wm_ref, bs_ref, da_ref, dws_ref, dbs_ref, dvg_ref, dvb_ref, dvln_sc):
        first = pl.program_id(0) == 0
        u = sv_ref[:, :d].astype(F32)
        vhat = sv_ref[:, 2 * d:3 * d].astype(F32)
        rstd = rs_ref[...]
        vln = (vhat * vg_ref[...] + vb_ref[...]).astype(BF16)
        dgate = dg_ref[...]

        @pl.when(first)
        def _():
            dws_ref[...] = jnp.zeros(dws_ref.shape, F32)
            dbs_ref[...] = jnp.zeros(dbs_ref.shape, F32)

        for blk in range(tm // gb):
            rs = slice(blk * gb, (blk + 1) * gb)
            for gi in range(GMLP_GROUPS):
                cs = slice(gi * gb, (gi + 1) * gb)
                vblk = vln[rs, cs]
                s = jnp.dot(wm_ref[gi], vblk, preferred_element_type=F32) + bs_ref[:, gi:gi + 1]
                dgb = dgate[rs, cs]
                da_ref[rs, cs] = (dgb * s * sv_ref[rs, d + gi * gb:d + (gi + 1) * gb].astype(F32)).astype(BF16)
                ds = dgb * u[rs, cs]
                dsb = ds.astype(BF16)
                dws_ref[gi] += lax.dot_general(dsb, vblk, NT, preferred_element_type=F32)
                dbs_ref[:, gi:gi + 1] += jnp.sum(ds, axis=1, keepdims=True)
                dvln_sc[rs, cs] = lax.dot_general(wm_ref[gi], dsb, TN, preferred_element_type=F32)
        dv, dvg, dvb = _ln_bwd(dvln_sc[...], vhat, rstd, vg_ref[...])
        da_ref[:, d:] = (dv * sv_ref[:, 3 * d:].astype(F32)).astype(BF16)
        _accumulate(dvg_ref, first, dvg)
        _accumulate(dvb_ref, first, dvb)

    row = lambda i: (i, 0)
    vec = pl.BlockSpec((1, d), lambda i: (0, 0))
    return pl.pallas_call(
        body, grid=(t // tm,),
        in_specs=[pl.BlockSpec((tm, d), row), pl.BlockSpec((tm, 4 * d), row), pl.BlockSpec((tm, 1), row),
                  _resident(vg.shape), _resident(vb.shape), _resident(wm.shape), _resident(bs_col.shape)],
        out_specs=[pl.BlockSpec((tm, d2), row), pl.BlockSpec(wm.shape, lambda i: (0, 0, 0)),
                   pl.BlockSpec(bs_col.shape, lambda i: (0, 0)), vec, vec],
        out_shape=[_sds((t, d2), BF16), _sds(wm.shape, F32), _sds(bs_col.shape, F32), _sds((1, d), F32), _sds((1, d), F32)],
        scratch_shapes=[pltpu.VMEM((tm, d), F32)],
        compiler_params=_cp(), name="gmlp_bwd")(dgated, saved, rstd_v, vg, vb, wm, bs_col)


def _conv_bwd(bch, dmix, conv_w):
    t = bch.shape[0]
    tm = min(ROW_TILE, t)
    nb = t // tm
    halo_blocks = tm // SUBLANES
    cw = CONV_WIDTH

    def body(cur_ref, prev_ref, next_ref, dc_ref, dn_ref, w_ref, o_ref, dw_ref):
        i = pl.program_id(0)
        bgate, cgate, hval = cur_ref[:, :cw], cur_ref[:, cw:2 * cw], cur_ref[:, 2 * cw:]
        z = cgate * hval
        zp = jnp.where(i == 0, 0.0, prev_ref[:, cw:2 * cw] * prev_ref[:, 2 * cw:])
        z1, z2 = _shift_down(z, zp)
        w0, w1, w2 = w_ref[0:1, :], w_ref[1:2, :], w_ref[2:3, :]
        dconv = dc_ref[...]
        o_ref[:, :cw] = (dconv * (w0 * z2 + w1 * z1 + w2 * z)).astype(BF16)
        dy = dconv * bgate
        dyn = jnp.where(i == nb - 1, 0.0, dn_ref[...] * next_ref[:, :cw])
        dy1, dy2 = _shift_up(dy, dyn)
        dz = w2 * dy + w1 * dy1 + w0 * dy2
        o_ref[:, cw:2 * cw] = (dz * hval).astype(BF16)
        o_ref[:, 2 * cw:] = (dz * cgate).astype(BF16)

        @pl.when(i == 0)
        def _():
            dw_ref[...] = jnp.zeros(dw_ref.shape, F32)

        for tap, zs in enumerate((z2, z1, z)):
            dw_ref[tap:tap + 1, :] += jnp.sum(dy * zs, axis=0, keepdims=True)

    last_halo = t // SUBLANES - 1
    return pl.pallas_call(
        body, grid=(nb,),
        in_specs=[pl.BlockSpec((tm, BCH), lambda i: (i, 0)),
                  pl.BlockSpec((SUBLANES, BCH), lambda i: (jnp.maximum(i * halo_blocks - 1, 0), 0)),
                  pl.BlockSpec((SUBLANES, BCH), lambda i: (jnp.minimum((i + 1) * halo_blocks, last_halo), 0)),
                  pl.BlockSpec((tm, cw), lambda i: (i, 1)),
                  pl.BlockSpec((SUBLANES, cw), lambda i: (jnp.minimum((i + 1) * halo_blocks, last_halo), 1)),
                  _resident(conv_w.shape)],
        out_specs=[pl.BlockSpec((tm, BCH), lambda i: (i, 0)), pl.BlockSpec((SUBLANES, cw), lambda i: (0, 0))],
        out_shape=[_sds((t, BCH), BF16), _sds((SUBLANES, cw), F32)],
        compiler_params=_cp(), name="conv_bwd")(bch, bch, bch, dmix, dmix, conv_w)


def _attn_bwd_prep(o, dmix, qp, lse_col):
    t = o.shape[0]
    tm = min(ROW_TILE, t)
    hd = HEAD_DIM

    def body(o_ref, do_ref, qp_ref, lse_ref, qb_ref, dob_ref):
        lane = lax.broadcasted_iota(jnp.int32, (tm, hd), 1) + hd
        for h in range(FOX_HEADS):
            do = do_ref[:, h * hd:(h + 1) * hd]
            delta = jnp.sum(o_ref[:, h * hd:(h + 1) * hd].astype(F32) * do, axis=-1, keepdims=True)
            dob_ref[h, :, :hd] = do.astype(BF16)
            dob_ref[h, :, hd:] = _lane_pieces(lane, DO_DELTA, _split3(delta), -1.0).astype(BF16)
            qb_ref[h, :, :hd] = qp_ref[h, :, :hd]
            qb_ref[h, :, hd:] = (qp_ref[h, :, hd:].astype(F32)
                                 + _lane_pieces(lane, Q_LSE, _split3(lse_ref[:, h:h + 1]), -1.0)).astype(BF16)

    row3 = pl.BlockSpec((FOX_HEADS, tm, LANES), lambda i: (0, i, 0))
    return pl.pallas_call(
        body, grid=(t // tm,),
        in_specs=[pl.BlockSpec((tm, FOX_WIDTH), lambda i: (i, 0)), pl.BlockSpec((tm, FOX_WIDTH), lambda i: (i, 0)), row3,
                  pl.BlockSpec((tm, FOX_HEADS), lambda i: (i, 0))],
        out_specs=[row3, row3], out_shape=[_sds((FOX_HEADS, t, LANES), BF16)] * 2,
        compiler_params=_cp(), name="attn_bwd_prep")(o, dmix, qp, lse_col)


def _attn_bwd(qb, kp, vp, dob, kt):
    t = qb.shape[1]
    bq = min(ATT_BLOCK, t)
    nq = t // bq
    i_tab, j_tab = _triangle(nq, key_major=True)

    def body(it_ref, jt_ref, q_ref, k_ref, v_ref, do_ref, kt_ref, dqt_ref, dk_ref, dv_ref, dk_sc, dv_sc):
        s = pl.program_id(1)
        i, j = it_ref[s], jt_ref[s]

        @pl.when(s == 0)
        def _():
            dqt_ref[...] = jnp.zeros(dqt_ref.shape, F32)

        @pl.when(i == j)
        def _():
            dk_sc[...] = jnp.zeros(dk_sc.shape, F32)
            dv_sc[...] = jnp.zeros(dv_sc.shape, F32)

        cols = pl.ds(pl.multiple_of(i * bq, bq), bq)

        def sweep(masked):
            def scores(h):
                return (lax.dot_general(k_ref[h], q_ref[h], NT, preferred_element_type=F32),
                        lax.dot_general(v_ref[h], do_ref[h], NT, preferred_element_type=F32))

            def accumulate(h, ptb, dstb):
                dv_sc[h] += jnp.dot(ptb, do_ref[h], preferred_element_type=F32)
                dk_sc[h] += jnp.dot(dstb, q_ref[h], preferred_element_type=F32)
                dqt_ref[h, :, cols] += jnp.dot(kt_ref[h], dstb, preferred_element_type=F32)

            ahead, behind = scores(0), None
            for h in range(ATT_BWD_HEADS):
                st, dpt = ahead
                if h + 1 < ATT_BWD_HEADS:
                    ahead = scores(h + 1)
                if behind is not None:
                    accumulate(*behind)
                if masked:
                    key = lax.broadcasted_iota(jnp.int32, (bq, bq), 0)
                    qry = lax.broadcasted_iota(jnp.int32, (bq, bq), 1)
                    st = jnp.where(key <= qry, st, NEG)
                pt = jnp.exp(st)
                behind = (h, pt.astype(BF16), (pt * dpt).astype(BF16))
            accumulate(*behind)

        @pl.when(i == j)
        def _():
            sweep(True)

        @pl.when(i > j)
        def _():
            sweep(False)

        @pl.when(i == nq - 1)
        def _():
            dk_ref[...] = dk_sc[...]
            dv_ref[...] = dv_sc[...].astype(BF16)

    nh = ATT_BWD_HEADS
    qblk = pl.BlockSpec((nh, bq, LANES), lambda hp, s, it, jt: (hp, it[s], 0))
    kblk = pl.BlockSpec((nh, bq, LANES), lambda hp, s, it, jt: (hp, jt[s], 0))
    grid_spec = pltpu.PrefetchScalarGridSpec(
        num_scalar_prefetch=2, grid=(FOX_HEADS // nh, i_tab.shape[0]),
        in_specs=[qblk, kblk, kblk, qblk, pl.BlockSpec((nh, LANES, bq), lambda hp, s, it, jt: (hp, 0, jt[s]))],
        out_specs=[pl.BlockSpec((nh, LANES, t), lambda hp, s, it, jt: (hp, 0, 0), pipeline_mode=pl.Buffered(1)),
                   kblk, kblk],
        scratch_shapes=[pltpu.VMEM((nh, bq, LANES), F32), pltpu.VMEM((nh, bq, LANES), F32)])
    return pl.pallas_call(body, grid_spec=grid_spec,
                          out_shape=[_sds((FOX_HEADS, LANES, t), F32), _sds((FOX_HEADS, t, LANES), F32),
                                     _sds((FOX_HEADS, t, LANES), BF16)],
                          compiler_params=_cp(), name="attn_bwd")(i_tab, j_tab, qb, kp, vp, dob, kt)


def _attn_unpack(dqt, dkp, dvp):
    t = dkp.shape[1]
    tm = min(ROW_TILE, t)
    hd = HEAD_DIM

    def body(dqt_ref, dk_ref, dv_ref, o_ref, dc_ref):
        for h in range(FOX_HEADS):
            dq = dqt_ref[h].T
            o_ref[:, h * hd:(h + 1) * hd] = (dq[:, :hd] * (hd ** -0.5)).astype(BF16)
            o_ref[:, FOX_WIDTH + h * hd:FOX_WIDTH + (h + 1) * hd] = dk_ref[h, :, :hd].astype(BF16)
            o_ref[:, 2 * FOX_WIDTH + h * hd:2 * FOX_WIDTH + (h + 1) * hd] = dv_ref[h, :, :hd]
            dc_ref[:, h:h + 1] = dq[:, K_ONE:K_ONE + 1] - dk_ref[h, :, Q_ONE:Q_ONE + 1]

    row3 = pl.BlockSpec((FOX_HEADS, tm, LANES), lambda i: (0, i, 0))
    return pl.pallas_call(
        body, grid=(t // tm,),
        in_specs=[pl.BlockSpec((FOX_HEADS, LANES, tm), lambda i: (0, 0, i)), row3, row3],
        out_specs=[pl.BlockSpec((tm, QKV), lambda i: (i, 0)), pl.BlockSpec((tm, FOX_HEADS), lambda i: (i, 0))],
        out_shape=[_sds((t, QKV), BF16), _sds((t, FOX_HEADS), F32)],
        compiler_params=_cp(), name="attn_unpack")(dqt, dkp, dvp)


def _adamw(parts, w, m, v, name, layer=None, into=None):
    nl, r, c = w.shape
    fits = [cand for cand in range(SUBLANES, r, SUBLANES) if r % cand == 0 and cand * c * 4 <= ADAMW_BLOCK_BYTES]
    tr = max(fits) if fits else r
    npart = len(parts)
    bc1 = 1.0 - ADAM_B1 ** ADAM_STEP
    bc2 = 1.0 - ADAM_B2 ** ADAM_STEP

    def body(*refs):
        p_refs = refs[:npart]
        w_ref, m_ref, v_ref = refs[npart:npart + 3]
        g_ref, d_ref, nm_ref, nv_ref = refs[-4:]
        sums = []
        for p_ref in p_refs:
            acc = p_ref[0, 0].astype(F32)
            for s in range(1, p_ref.shape[0]):
                acc = acc + p_ref[s, 0].astype(F32)
            sums.append(acc)
        g = sums[0]
        for extra in sums[1:]:
            g = g + extra
        nm = ADAM_B1 * m_ref[0] + (1.0 - ADAM_B1) * g
        nv = ADAM_B2 * v_ref[0] + (1.0 - ADAM_B2) * (g * g)
        m_hat = nm / bc1
        v_hat = nv / bc2
        g_ref[0] = g
        d_ref[0] = -ADAM_LR * (m_hat / (jnp.sqrt(v_hat) + ADAM_EPS) + ADAM_WD * w_ref[0])
        nm_ref[0] = nm
        nv_ref[0] = nv

    first = 0 if layer is None else layer
    blk = pl.BlockSpec((1, tr, c), lambda l, i: (first + l, i, 0))
    extra = [] if into is None else list(into)
    return pl.pallas_call(
        body, grid=(nl if layer is None else 1, r // tr),
        in_specs=[pl.BlockSpec((p.shape[0], 1, tr, c), lambda l, i: (0, l, i, 0)) for p in parts] + [blk, blk, blk]
        + [_ANY_SPEC] * len(extra),
        out_specs=[blk] * 4, out_shape=[_sds(w.shape, F32)] * 4,
        input_output_aliases={npart + 3 + k: k for k in range(len(extra))},
        compiler_params=_cp(), name=name)(*parts, w, m, v, *extra)


def _to_rows(a):
    flat = a.reshape(-1)
    pad = (-flat.shape[0]) % LANES
    if pad:
        flat = jnp.concatenate([flat, jnp.zeros((pad,), flat.dtype)])
    return flat.reshape(-1, LANES)


def _by_owner_cols(dw):
    k, n = dw.shape
    return dw.reshape(k, N_CHIPS, n // N_CHIPS).transpose(1, 0, 2)[:, None]


def _ffn_fwd(xin_ln, xin_b, wi, wo, g, b, layer):
    gu, h = _ffn_in(xin_b, wi, f"ffn_in_{layer}")
    y_b, xhat, rstd = _mm_res_ln([(h, wo)], xin_ln, g, b, f"ffn_out_ln_{layer}")
    return y_b, (xin_b, gu, h, xhat, rstd)


def _ffn_bwd(dz, saved, wi, wo, ln_below, layer):
    xin_b, gu, h, _, _ = saved
    dgu = _ffn_bwd_hidden(dz, wo, gu, f"ffn_bwd_hidden_{layer}")
    g_out = _mm_tn(h, dz, f"ffn_dw_out_{layer}", tn=D_MODEL, tk=HALF_HIDDEN, tt=REDUCE_TILE // 2)
    g_in = _mm_tn(xin_b, dgu, f"ffn_dw_in_{layer}", tn=HALF_HIDDEN, stack_cols=True)
    below = _mm_nt([(dgu, 0, 0, 0)], [wi], f"ffn_dx_{layer}", tm=FFN_ROW_TILE, res=dz, ln=ln_below)
    return below, g_in, g_out.reshape(N_CHIPS, FFN_HIDDEN // N_CHIPS, D_MODEL)


def kernel(x, even_w_in, even_b_f, even_conv_w, even_w_out, odd_w_in, odd_v_ln_g, odd_v_ln_b, odd_w_s, odd_b_s, odd_w_out, mix_ln_g, mix_ln_b, ffn_w_in, ffn_w_out, ffn_ln_g, ffn_ln_b, loss_target, m_even_w_in, m_even_b_f, m_even_conv_w, m_even_w_out, m_odd_w_in, m_odd_v_ln_g, m_odd_v_ln_b, m_odd_w_s, m_odd_b_s, m_odd_w_out, m_mix_ln_g, m_mix_ln_b, m_ffn_w_in, m_ffn_w_out, m_ffn_ln_g, m_ffn_ln_b, v_even_w_in, v_even_b_f, v_even_conv_w, v_even_w_out, v_odd_w_in, v_odd_v_ln_g, v_odd_v_ln_b, v_odd_w_s, v_odd_b_s, v_odd_w_out, v_mix_ln_g, v_mix_ln_b, v_ffn_w_in, v_ffn_w_out, v_ffn_ln_g, v_ffn_ln_b):
    t = x.shape[1]
    d = D_MODEL
    chip = 2 * lax.axis_index("x") + lax.axis_index("y")
    x2d = x[0]
    target = loss_target[0]

    small_shard = jnp.concatenate([odd_v_ln_g.reshape(2, LANES), odd_v_ln_b.reshape(2, LANES),
                                   even_conv_w.reshape(CONV_K, LANES), jnp.zeros((1, LANES), F32)], axis=0)
    first = [even_w_in[0].astype(BF16), even_w_out[0].astype(BF16), small_shard]
    later = [odd_w_in[0].astype(BF16), odd_w_out[0].astype(BF16), ffn_w_in[0].astype(BF16), ffn_w_in[1].astype(BF16),
             ffn_w_out[0].astype(BF16), ffn_w_out[1].astype(BF16)]
    first_h, first_tok = _split_start(first, "gather4", "gather_first_start")
    later_h, later_tok = _split_start(later, "gather4", "gather_later_start", after=first_tok)
    g_ewi, g_ewo, g_small = [_with_own(g, own) for g, own in
                             zip(_split_wait(first_h, "gather_first_wait", later_tok), first)]
    ewi = g_ewi.transpose(1, 0, 2).reshape(d, EVEN_IN)
    w_even_in = jnp.concatenate([ewi[:, :QKV], ewi[:, QKV + FOX_HEADS:], ewi[:, QKV:QKV + FOX_HEADS],
                                 jnp.zeros((d, LANES - FOX_HEADS), BF16)], axis=1)
    w_even_out = g_ewo.reshape(d, d)
    v_ln_g = g_small[:, 0:2].reshape(1, d)
    v_ln_b = g_small[:, 2:4].reshape(1, d)
    conv_w = g_small[:, 4:7].transpose(1, 0, 2).reshape(CONV_K, CONV_WIDTH)
    chunk_id = jnp.arange(GMLP_BLOCK) // CHUNK
    gmask = chunk_id[None, :] <= chunk_id[:, None]
    w_spatial = jnp.where(gmask[None], odd_w_s[0], 0.0).astype(BF16)
    bs_col = odd_b_s[0].T
    b_f_col = even_b_f.reshape(FOX_HEADS, 1)
    ln = lambda p, l: p[l:l + 1]

    qkv, bch, fl = _proj(x2d, w_even_in, [(0, QKV, BF16), (QKV, QKV + BCH, F32), (QKV + BCH, EVEN_IN_PAD, F32)], "even_proj")
    fl3 = fl[:, :FOX_HEADS].T.reshape(FOX_HEADS, t // LANES, LANES).transpose(1, 0, 2)
    c3 = _fgate_fwd(fl3, b_f_col)
    c_rows = c3.transpose(1, 0, 2).reshape(FOX_HEADS, t)
    qp, kp, vp, kt, vt = _attn_pack(qkv, c_rows.T)
    attn, lse = _attn_fwd(qp, kp, vt)
    conv = _conv_fwd(bch, conv_w)
    x1_b, xh1, rs1 = _mm_res_ln([(attn, w_even_out[:FOX_WIDTH]), (conv, w_even_out[FOX_WIDTH:])], x2d,
                                ln(mix_ln_g, 0), ln(mix_ln_b, 0), "even_out_ln")
    w_odd_in, g_owo, w_fi0, w_fi1, g_fo0, g_fo1 = [_with_own(g, own) for g, own in
                                                   zip(_split_wait(later_h, "gather_later_wait", x1_b), later)]
    w_odd_out = g_owo.reshape(d, d)
    w_ffn_in = [w_fi0, w_fi1]
    w_ffn_out = [g_fo0.reshape(FFN_HIDDEN, d), g_fo1.reshape(FFN_HIDDEN, d)]
    x2_b, ffn0 = _ffn_fwd((xh1, ln(mix_ln_g, 0), ln(mix_ln_b, 0)), x1_b, w_ffn_in[0], w_ffn_out[0],
                          ln(ffn_ln_g, 0), ln(ffn_ln_b, 0), 0)

    sv_odd, rs_odd, gated = _gmlp_fwd(x2_b, w_odd_in, v_ln_g, v_ln_b, w_spatial, bs_col)
    x3_b, xh3, rs3 = _mm_res_ln([(gated, w_odd_out)], (ffn0[3], ln(ffn_ln_g, 0), ln(ffn_ln_b, 0)),
                                ln(mix_ln_g, 1), ln(mix_ln_b, 1), "odd_out_ln")
    _, ffn1 = _ffn_fwd((xh3, ln(mix_ln_g, 1), ln(mix_ln_b, 1)), x3_b, w_ffn_in[1], w_ffn_out[1],
                       ln(ffn_ln_g, 1), ln(ffn_ln_b, 1), 1)

    sq, dz4, d_fg1, d_fb1 = _loss_ln_bwd(ffn1[3], ffn1[4], ln(ffn_ln_g, 1), ln(ffn_ln_b, 1), target)
    loss = lax.psum(0.5 / d * jnp.sum(sq), ("x", "y", "c"))
    (dz3, d_mg1, d_mb1), gi_f1, go_f1 = _ffn_bwd(dz4, ffn1, w_ffn_in[1], w_ffn_out[1], (xh3, rs3, ln(mix_ln_g, 1)), 1)

    dgated = _mm_nt([(dz3, 0, 0, d)], [w_odd_out], "odd_dgated")
    go_odd = _mm_tn(gated, dz3, "odd_dw_out", tn=d).reshape(N_CHIPS, 1, d // N_CHIPS, d)
    da_odd, dws, dbs_col, d_vg, d_vb = _gmlp_bwd(dgated, sv_odd, rs_odd, v_ln_g, v_ln_b, w_spatial, bs_col)
    gi_odd = _mm_tn(x2_b, da_odd, "odd_dw_in", tn=d // 2, stack_cols=True)[:, None]
    dz2, d_fg0, d_fb0 = _mm_nt([(da_odd, 0, 0, 0)], [w_odd_in], "odd_dx", res=dz3,
                               ln=(ffn0[3], ffn0[4], ln(ffn_ln_g, 0)))
    (dz1, d_mg0, d_mb0), gi_f0, go_f0 = _ffn_bwd(dz2, ffn0, w_ffn_in[0], w_ffn_out[0], (xh1, rs1, ln(mix_ln_g, 0)), 0)

    sent_early = [gi_odd, go_odd, gi_f0[:, None], gi_f1[:, None], go_f0[:, None], go_f1[:, None]]
    early_h, early_tok = _split_start(sent_early, "scatter4", "scatter_early_start")
    dmix = _mm_nt([(dz1, 0, 0, d)], [w_even_out], "even_dmix", after=early_tok)
    go_even = jnp.concatenate([_mm_tn(attn, dz1, "even_dw_out_attn", tn=d), _mm_tn(conv, dz1, "even_dw_out_conv", tn=d)],
                              axis=0).reshape(N_CHIPS, 1, d // N_CHIPS, d)
    dbch, dconv_w8 = _conv_bwd(bch, dmix, conv_w)
    qb, dob = _attn_bwd_prep(attn, dmix, qp, lse.reshape(FOX_HEADS, t).T)
    dqkv, dc_col = _attn_unpack(*_attn_bwd(qb, kp, vp, dob, kt))
    dc3 = dc_col.T.reshape(FOX_HEADS, t // LANES, LANES).transpose(1, 0, 2)
    dfl3, d_bf = _fgate_bwd(dc3, fl3, b_f_col)
    dfl = jnp.concatenate([dfl3.transpose(1, 0, 2).reshape(FOX_HEADS, t).T.astype(BF16),
                           jnp.zeros((t, LANES - FOX_HEADS), BF16)], axis=1)

    dws_masked = jnp.where(gmask[None], dws, 0.0)
    rep_names = ["odd_w_s", "odd_b_s", "mix_ln_g", "mix_ln_b", "ffn_ln_g", "ffn_ln_b", "even_b_f"]
    rep_grads = [dws_masked, dbs_col.T, jnp.concatenate([d_mg0, d_mg1]), jnp.concatenate([d_mb0, d_mb1]),
                 jnp.concatenate([d_fg0, d_fg1]), jnp.concatenate([d_fb0, d_fb1]), d_bf.reshape(1, FOX_HEADS)]
    rep_w = [(odd_w_s, m_odd_w_s, v_odd_w_s), (odd_b_s, m_odd_b_s, v_odd_b_s), (mix_ln_g, m_mix_ln_g, v_mix_ln_g),
             (mix_ln_b, m_mix_ln_b, v_mix_ln_b), (ffn_ln_g, m_ffn_ln_g, v_ffn_ln_g), (ffn_ln_b, m_ffn_ln_b, v_ffn_ln_b),
             (even_b_f, m_even_b_f, v_even_b_f)]
    rep_rows = [_to_rows(gr) for gr in rep_grads]
    n_rep = sum(r.shape[0] for r in rep_rows)
    pad_rep = (-n_rep) % SUBLANES
    dconv_w = dconv_w8[:CONV_K].reshape(CONV_K, N_CHIPS, LANES).transpose(1, 0, 2).reshape(N_CHIPS * CONV_K, LANES)
    packed = jnp.concatenate(rep_rows + [jnp.zeros((pad_rep, LANES), F32), d_vg.reshape(SUBLANES, LANES),
                                         d_vb.reshape(SUBLANES, LANES), dconv_w, jnp.zeros((4, LANES), F32)], axis=0)
    small_h, small_tok = _split_start([packed], "gather8", "gather_small_start")

    chip_blk = lambda g: lax.dynamic_index_in_dim(g, chip, 0, keepdims=False)
    mine_early = [_with_own(r, chip_blk(g)) for r, g in
                  zip(_split_wait(early_h, "scatter_early_wait", small_tok), sent_early)]
    swap_h, swap_tok = _split_start(mine_early, "swap2", "swap_early_start")
    dw_qkv = _mm_tn(dqkv, x2d, "even_dw_qkv", tn=d, tk=QKV // 2, after=swap_tok)
    dw_bch = _mm_tn(dbch, x2d, "even_dw_bch", tn=d, tk=BCH // 2)
    dw_f = _mm_tn(dfl, x2d, "even_dw_f", tn=d)
    gi_even = jnp.concatenate([dw_qkv, dw_f[:FOX_HEADS], dw_bch], axis=0).reshape(N_CHIPS, 1, -1, LANES)
    sent_late = [gi_even, go_even]
    late_h, late_tok = _split_start(sent_late, "scatter4", "scatter_late_start")
    grad_x = _mm_nt([(dqkv, 0, 0, QKV), (dbch, 0, QKV, QKV + BCH), (dfl, 0, QKV + BCH, EVEN_IN_PAD)], [w_even_in],
                    "even_dx", res=dz1, after=late_tok)
    mine_late = [_with_own(r, chip_blk(g)) for r, g in zip(_split_wait(late_h, "scatter_late_wait", grad_x), sent_late)]
    theirs_late = _exchange(mine_late, "swap2", "swap_late")
    theirs_early = _split_wait(swap_h, "swap_early_wait", theirs_late[0])
    (gathered,) = _split_wait(small_h, "gather_small_wait", theirs_early[0])
    gathered = lax.dynamic_update_index_in_dim(gathered, packed, 4 * lax.axis_index("x") + 2 * lax.axis_index("y")
                                               + lax.axis_index("c"), 0)
    mine, theirs = mine_late + mine_early, theirs_late + theirs_early
    big_w = [(even_w_in, m_even_w_in, v_even_w_in), (even_w_out, m_even_w_out, v_even_w_out),
             (odd_w_in, m_odd_w_in, v_odd_w_in), (odd_w_out, m_odd_w_out, v_odd_w_out)]
    big_names = ["even_w_in", "even_w_out", "odd_w_in", "odd_w_out"]
    res = {}
    for nm, own, sib, (w, m, v) in zip(big_names, mine, theirs, big_w):
        if nm == "even_w_in":
            rows = lambda a: jnp.swapaxes(a, 1, 2).reshape(1, -1, LANES)
            back = lambda a: jnp.swapaxes(a.reshape(1, EVEN_IN // N_CHIPS, d), 1, 2)
            res[nm] = [back(o) for o in _adamw([own, sib], rows(w), rows(m), rows(v), f"adamw_{nm}")]
            continue
        res[nm] = _adamw([own, sib], w, m, v, f"adamw_{nm}")
    for nm, at, (w, m, v) in (("ffn_w_in", 4, (ffn_w_in, m_ffn_w_in, v_ffn_w_in)),
                              ("ffn_w_out", 6, (ffn_w_out, m_ffn_w_out, v_ffn_w_out))):
        upper = _adamw([mine[at + 1], theirs[at + 1]], w, m, v, f"adamw_{nm}_1", layer=1)
        res[nm] = _adamw([mine[at], theirs[at]], w, m, v, f"adamw_{nm}_0", layer=0, into=upper)

    base = n_rep + pad_rep
    own_rows = jnp.concatenate([
        lax.dynamic_slice_in_dim(gathered, base + 2 * chip, 2, axis=1),
        lax.dynamic_slice_in_dim(gathered, base + SUBLANES + 2 * chip, 2, axis=1),
        lax.dynamic_slice_in_dim(gathered, base + 2 * SUBLANES + CONV_K * chip, CONV_K, axis=1),
        jnp.zeros((N_DEV, 1, LANES), F32)], axis=1)
    small_parts = jnp.concatenate([gathered[:, :base], own_rows], axis=1)[:, None]

    def pack_small(get):
        rows = [_to_rows(get(tw)) for tw in rep_w] + [jnp.zeros((pad_rep, LANES), F32)]
        rows += [get(sh).reshape(-1, LANES) for sh in ((odd_v_ln_g, m_odd_v_ln_g, v_odd_v_ln_g),
                                                       (odd_v_ln_b, m_odd_v_ln_b, v_odd_v_ln_b),
                                                       (even_conv_w, m_even_conv_w, v_even_conv_w))]
        return jnp.concatenate(rows + [jnp.zeros((1, LANES), F32)], axis=0)[None]

    small_out = _adamw([small_parts], pack_small(lambda tw: tw[0]), pack_small(lambda tw: tw[1]),
                       pack_small(lambda tw: tw[2]), "adamw_small")

    def unpack_small(rows3):
        rows = rows3[0]
        out, off = {}, 0
        for nm, (w, _, _), r in zip(rep_names, rep_w, rep_rows):
            out[nm] = rows[off:off + r.shape[0]].reshape(-1)[:w.size].reshape(w.shape)
            off += r.shape[0]
        off += pad_rep
        out["odd_v_ln_g"] = rows[off:off + 2].reshape(odd_v_ln_g.shape)
        out["odd_v_ln_b"] = rows[off + 2:off + 4].reshape(odd_v_ln_b.shape)
        out["even_conv_w"] = rows[off + 4:off + 4 + CONV_K].reshape(even_conv_w.shape)
        return out

    small = [unpack_small(o) for o in small_out]
    order = ["even_w_in", "even_b_f", "even_conv_w", "even_w_out", "odd_w_in", "odd_v_ln_g", "odd_v_ln_b", "odd_w_s",
             "odd_b_s", "odd_w_out", "mix_ln_g", "mix_ln_b", "ffn_w_in", "ffn_w_out", "ffn_ln_g", "ffn_ln_b"]
    outs = [loss, grad_x[None]]
    for kind in range(4):
        for nm in order:
            outs.append(res[nm][kind] if nm in res else small[kind][nm])
    return tuple(outs)
```

```python
import functools
import math

import jax
import jax.numpy as jnp
from jax import lax
from jax.experimental import pallas as pl
from jax.experimental.pallas import tpu as pltpu

F32 = jnp.float32
BF16 = jnp.bfloat16

D_MODEL = 1024
FOX_HEADS = 8
HEAD_DIM = 64
HEAD_PAIRS = FOX_HEADS // 2
FOX_WIDTH = FOX_HEADS * HEAD_DIM
CONV_WIDTH = 512
CONV_K = 3
QKV = 3 * FOX_WIDTH
BCH = 3 * CONV_WIDTH
EVEN_IN = QKV + FOX_HEADS + BCH
EVEN_IN_PAD = QKV + BCH + 128
GMLP_BLOCK = 128
GMLP_GROUPS = 8
CHUNK = 64
FFN_HIDDEN = 2816
HALF_HIDDEN = FFN_HIDDEN // 2
ALPHA = 4.0 ** 0.25
LN_EPS = 1e-5
ADAM_LR = 0.001
ADAM_B1 = 0.9
ADAM_B2 = 0.999
ADAM_EPS = 1e-08
ADAM_WD = 0.01
ADAM_STEP = 10
N_CHIPS = 4
N_DEV = 8
LANES = 128
SUBLANES = 8
ROW_TILE = 512
FFN_ROW_TILE = 512
REDUCE_TILE = 2048
ATT_BLOCK = 512
ATT_FWD_HEADS = 8
ATT_BWD_HEADS = 4
ADAMW_BLOCK_BYTES = 2 ** 20
VMEM_LIMIT = 56 * 2 ** 20
NEG = -1e30
MESH = pl.DeviceIdType.MESH
HIGHEST = lax.Precision.HIGHEST
Q_C, Q_ONE, Q_LSE = 64, 67, 70
K_ONE, K_C, K_ONE2 = 64, 67, 70
V_ONE = 64
DO_DELTA = 65
NT = (((1,), (1,)), ((), ()))
TN = (((0,), (0,)), ((), ()))


def _cp():
    return pltpu.CompilerParams(vmem_limit_bytes=VMEM_LIMIT)


def _resident(shape):
    zeros = (0,) * len(shape)
    return pl.BlockSpec(shape, lambda *_: zeros, pipeline_mode=pl.Buffered(1))


def _sds(shape, dtype):
    return jax.ShapeDtypeStruct(tuple(shape), dtype)


_MASKS = {
    "gather4": [(1, 0, 0), (0, 1, 0), (1, 1, 0)],
    "scatter4": [(1, 0, 0), (0, 1, 0), (1, 1, 0)],
    "swap2": [(0, 0, 1)],
    "gather8": [(0, 0, 1), (0, 1, 0), (0, 1, 1), (1, 0, 0), (1, 0, 1), (1, 1, 0), (1, 1, 1)],
}


def _exchange(arrs, mode, name):
    n = len(arrs)
    masks = _MASKS[mode]
    npeer = len(masks)
    lead = {"gather4": N_CHIPS, "gather8": N_DEV}.get(mode)
    out_shapes = [_sds(((lead,) if lead else ()) + a.shape, a.dtype) for a in arrs]

    def body(*refs):
        ins, outs = refs[:n], refs[n:2 * n]
        send_sems, recv_sems, loc_sems = refs[2 * n:]
        x, y, c = lax.axis_index("x"), lax.axis_index("y"), lax.axis_index("c")
        chip, dev = 2 * x + y, 4 * x + 2 * y + c
        sends, recvs, locs = [], [], []
        for k in range(n):
            if mode == "gather4":
                locs.append(pltpu.make_async_copy(ins[k], outs[k].at[chip], loc_sems.at[k]))
            elif mode == "scatter4":
                locs.append(pltpu.make_async_copy(ins[k].at[chip], outs[k].at[chip], loc_sems.at[k]))
            elif mode == "gather8":
                locs.append(pltpu.make_async_copy(ins[k], outs[k].at[dev], loc_sems.at[k]))
        for cp in locs:
            cp.start()
        for k in range(n):
            for j, (dx, dy, dc) in enumerate(masks):
                px = 1 - x if dx else x
                py = 1 - y if dy else y
                pc = 1 - c if dc else c
                pchip, pdev = 2 * px + py, 4 * px + 2 * py + pc
                if mode == "gather4":
                    src, dst, land = ins[k], outs[k].at[chip], outs[k].at[pchip]
                elif mode == "scatter4":
                    src, dst, land = ins[k].at[pchip], outs[k].at[chip], outs[k].at[pchip]
                elif mode == "swap2":
                    src, dst, land = ins[k], outs[k], outs[k]
                else:
                    src, dst, land = ins[k], outs[k].at[dev], outs[k].at[pdev]
                s = k * npeer + j
                kw = dict(send_sem=send_sems.at[s], recv_sem=recv_sems.at[s], device_id=(px, py, pc),
                          device_id_type=MESH)
                cp = pltpu.make_async_remote_copy(src_ref=src, dst_ref=dst, **kw)
                cp.start()
                sends.append(cp)
                recvs.append(pltpu.make_async_remote_copy(src_ref=src, dst_ref=land, **kw))
        for cp in recvs:
            cp.wait_recv()
        for cp in sends:
            cp.wait_send()
        for cp in locs:
            cp.wait()

    any_spec = pl.BlockSpec(memory_space=pl.ANY)
    outs = pl.pallas_call(
        body,
        out_shape=out_shapes,
        in_specs=[any_spec] * n,
        out_specs=[any_spec] * n,
        scratch_shapes=[pltpu.SemaphoreType.DMA((n * npeer,)), pltpu.SemaphoreType.DMA((n * npeer,)),
                        pltpu.SemaphoreType.DMA((max(n, 1),))],
        name=name,
    )(*arrs)
    return list(outs)


_HBM_SPEC = pl.BlockSpec(memory_space=pltpu.HBM)
_SEM_SPEC = pl.BlockSpec(memory_space=pltpu.SEMAPHORE)
_ANY_SPEC = pl.BlockSpec(memory_space=pl.ANY)
_EFFECT = pltpu.SideEffectType.DATAFLOW_SIDE_EFFECTING


def _split_copies(mode, ins, lands, send_sems, recv_sems):
    x, y, c = lax.axis_index("x"), lax.axis_index("y"), lax.axis_index("c")
    chip, dev = 2 * x + y, 4 * x + 2 * y + c
    masks = _MASKS[mode]
    out = []
    for k in range(len(ins)):
        for j, (dx, dy, dc) in enumerate(masks):
            px = 1 - x if dx else x
            py = 1 - y if dy else y
            pc = 1 - c if dc else c
            pchip, pdev = 2 * px + py, 4 * px + 2 * py + pc
            if mode == "gather4":
                src, dst, land = ins[k], lands[k].at[chip], lands[k].at[pchip]
            elif mode == "scatter4":
                src, dst, land = ins[k].at[pchip], lands[k].at[chip], lands[k].at[pchip]
            elif mode == "swap2":
                src, dst, land = ins[k], lands[k], lands[k]
            else:
                src, dst, land = ins[k], lands[k].at[dev], lands[k].at[pdev]
            s = k * len(masks) + j
            kw = dict(send_sem=send_sems.at[s], recv_sem=recv_sems.at[s], device_id=(px, py, pc), device_id_type=MESH)
            out.append((pltpu.make_async_remote_copy(src_ref=src, dst_ref=dst, **kw),
                        pltpu.make_async_remote_copy(src_ref=src, dst_ref=land, **kw)))
    return out


def _split_start(arrs, mode, name, after=None):
    n = len(arrs)
    nsem = n * len(_MASKS[mode])
    lead = {"gather4": (N_CHIPS,), "gather8": (N_DEV,)}.get(mode, ())
    land_shapes = [lead + a.shape for a in arrs]

    def body(*refs):
        ins, lands = refs[:n], refs[n:2 * n]
        outs = refs[2 * n + (after is not None):]
        for start, _ in _split_copies(mode, ins, lands, outs[0], outs[1]):
            start.start()
        outs[-1][...] = jnp.zeros(outs[-1].shape, F32)

    srcs = [pltpu.with_memory_space_constraint(a, pltpu.HBM) for a in arrs]
    empties = [pltpu.with_memory_space_constraint(lax.empty(s, a.dtype), pltpu.HBM) for s, a in zip(land_shapes, arrs)]
    res = pl.pallas_call(
        body, name=name,
        out_shape=(pltpu.SemaphoreType.DMA((nsem,)), pltpu.SemaphoreType.DMA((nsem,)),
                   *[pltpu.HBM(a.shape, a.dtype) for a in arrs],
                   *[pltpu.HBM(s, a.dtype) for s, a in zip(land_shapes, arrs)],
                   _sds((SUBLANES, LANES), F32)),
        in_specs=[_HBM_SPEC] * (2 * n) + ([_ANY_SPEC] if after is not None else []),
        out_specs=(_SEM_SPEC, _SEM_SPEC, *[_HBM_SPEC] * (2 * n), pl.BlockSpec(memory_space=pltpu.VMEM)),
        input_output_aliases={k: 2 + k for k in range(2 * n)},
        compiler_params=pltpu.CompilerParams(has_side_effects=_EFFECT),
    )(*srcs, *empties, *([after] if after is not None else []))
    return dict(mode=mode, n=n, sems=res[:2], bufs=res[2:2 + 2 * n]), res[-1]


def _split_wait(handle, name, after):
    n, mode = handle["n"], handle["mode"]

    def body(*refs):
        ins, lands = refs[:n], refs[n:2 * n]
        send_sems, recv_sems = refs[2 * n], refs[2 * n + 1]
        for _, arrival in _split_copies(mode, ins, lands, send_sems, recv_sems):
            arrival.wait_send()
            arrival.wait_recv()

    bufs = handle["bufs"]
    res = pl.pallas_call(
        body, name=name,
        out_shape=tuple(pltpu.HBM(b.shape, b.dtype) for b in bufs),
        in_specs=[_HBM_SPEC] * (2 * n) + [_SEM_SPEC, _SEM_SPEC, _ANY_SPEC],
        out_specs=tuple([_HBM_SPEC] * (2 * n)),
        input_output_aliases={k: k for k in range(2 * n)},
        compiler_params=pltpu.CompilerParams(has_side_effects=_EFFECT),
    )(*bufs, *handle["sems"], after)
    return list(res[n:])


def _with_own(landed, own):
    chip = 2 * lax.axis_index("x") + lax.axis_index("y")
    return lax.dynamic_update_index_in_dim(landed, own, chip, 0)


def _sigmoid(x):
    return 0.5 * jnp.tanh(0.5 * x) + 0.5


def _log_sigmoid(x):
    e = jnp.exp(-jnp.abs(x))
    log1p = jnp.where(e < 1e-2, e * (1.0 - e * (0.5 - e * (1.0 / 3.0))), jnp.log(1.0 + e))
    return jnp.minimum(x, 0.0) - log1p


def _ln_fwd(z):
    mu = jnp.mean(z, axis=-1, keepdims=True)
    zc = z - mu
    var = jnp.mean(zc * zc, axis=-1, keepdims=True)
    rstd = lax.rsqrt(var + LN_EPS)
    return zc * rstd, rstd


def _ln_bwd(dy, xhat, rstd, g):
    dxh = dy * g
    m1 = jnp.mean(dxh, axis=-1, keepdims=True)
    m2 = jnp.mean(dxh * xhat, axis=-1, keepdims=True)
    dz = rstd * (dxh - m1 - xhat * m2)
    return dz, jnp.sum(dy * xhat, axis=0, keepdims=True), jnp.sum(dy, axis=0, keepdims=True)


def _shift_down(z, halo):
    r = lax.broadcasted_iota(jnp.int32, z.shape, 0)
    z1 = jnp.where(r == 0, halo[7:8, :], pltpu.roll(z, 1, 0))
    z2 = jnp.where(r == 0, halo[6:7, :], jnp.where(r == 1, halo[7:8, :], pltpu.roll(z, 2, 0)))
    return z1, z2


def _shift_up(z, halo):
    n = z.shape[0]
    r = lax.broadcasted_iota(jnp.int32, z.shape, 0)
    z1 = jnp.where(r == n - 1, halo[0:1, :], pltpu.roll(z, n - 1, 0))
    z2 = jnp.where(r == n - 1, halo[1:2, :], jnp.where(r == n - 2, halo[0:1, :], pltpu.roll(z, n - 2, 0)))
    return z1, z2


def _accumulate(ref, first, value):
    @pl.when(first)
    def _():
        ref[...] = value

    @pl.when(jnp.logical_not(first))
    def _():
        ref[...] += value


def _proj(x, w, splits, name):
    t, k = x.shape
    tm = min(ROW_TILE, t)

    def body(x_ref, w_ref, *outs):
        a = x_ref[...].astype(BF16)
        for (lo, hi, dt), o in zip(splits, outs):
            o[...] = jnp.dot(a, w_ref[:, lo:hi], preferred_element_type=F32).astype(dt)

    return pl.pallas_call(
        body, grid=(t // tm,),
        in_specs=[pl.BlockSpec((tm, k), lambda i: (i, 0)), _resident(w.shape)],
        out_specs=[pl.BlockSpec((tm, hi - lo), lambda i: (i, 0)) for lo, hi, _ in splits],
        out_shape=[_sds((t, hi - lo), dt) for lo, hi, dt in splits],
        compiler_params=_cp(), name=name)(x, w)


def _fgate_fwd(fl3, b_f):
    nc = fl3.shape[0]

    def body(f_ref, b_ref, c_ref):
        r = lax.broadcasted_iota(jnp.int32, (LANES, LANES), 0)
        cidx = lax.broadcasted_iota(jnp.int32, (LANES, LANES), 1)
        upper = (r <= cidx).astype(F32)

        def step(i, carry):
            lf = _log_sigmoid(f_ref[i] + b_ref[...])
            cc = jnp.dot(lf, upper, precision=HIGHEST, preferred_element_type=F32) + carry
            c_ref[i] = cc
            return cc[:, LANES - 1:LANES]

        lax.fori_loop(0, nc, step, jnp.zeros((FOX_HEADS, 1), F32))

    return pl.pallas_call(body, out_shape=_sds(fl3.shape, F32), name="fgate_fwd")(fl3, b_f)


def _fgate_bwd(dc3, fl3, b_f):
    nc = fl3.shape[0]

    def body(dc_ref, f_ref, b_ref, df_ref, db_ref):
        r = lax.broadcasted_iota(jnp.int32, (LANES, LANES), 0)
        cidx = lax.broadcasted_iota(jnp.int32, (LANES, LANES), 1)
        lower = (r >= cidx).astype(F32)

        def step(n, carry):
            suffix, db = carry
            i = nc - 1 - n
            dlf = jnp.dot(dc_ref[i], lower, precision=HIGHEST, preferred_element_type=F32) + suffix
            df = dlf * (1.0 - _sigmoid(f_ref[i] + b_ref[...]))
            df_ref[i] = df
            return dlf[:, 0:1], db + jnp.sum(df, axis=1, keepdims=True)

        zero = jnp.zeros((FOX_HEADS, 1), F32)
        _, db = lax.fori_loop(0, nc, step, (zero, zero))
        db_ref[...] = db

    return pl.pallas_call(body, out_shape=[_sds(fl3.shape, F32), _sds((FOX_HEADS, 1), F32)],
                          name="fgate_bwd")(dc3, fl3, b_f)


def _split3(c):
    hi = c.astype(BF16).astype(F32)
    mid = (c - hi).astype(BF16).astype(F32)
    lo = (c - hi - mid).astype(BF16).astype(F32)
    return hi, mid, lo


def _lane_pieces(lane, start, pieces, sign):
    out = jnp.zeros(lane.shape, F32)
    for n, p in enumerate(pieces):
        out = jnp.where(lane == start + n, sign * p, out)
    return out


def _attn_pack(qkv, c_col):
    t = qkv.shape[0]
    tm = min(ROW_TILE, t)
    hd = HEAD_DIM

    def body(x_ref, c_ref, qp_ref, kp_ref, vp_ref, kt_ref, vt_ref):
        lane = lax.broadcasted_iota(jnp.int32, (tm, hd), 1) + hd
        for h in range(FOX_HEADS):
            pieces = _split3(c_ref[:, h:h + 1])
            ones = lambda a, b: jnp.where(jnp.logical_and(lane >= a, lane < b), 1.0, 0.0)
            q_extra = _lane_pieces(lane, Q_C, pieces, 1.0) + ones(Q_ONE, Q_ONE + 3)
            k_extra = _lane_pieces(lane, K_C, pieces, -1.0) + ones(K_ONE, K_ONE + 3) + ones(K_ONE2, K_ONE2 + 3)
            qp_ref[h, :, :hd] = (x_ref[:, h * hd:(h + 1) * hd].astype(F32) * (hd ** -0.5)).astype(BF16)
            qp_ref[h, :, hd:] = q_extra.astype(BF16)
            kp_ref[h, :, :hd] = x_ref[:, FOX_WIDTH + h * hd:FOX_WIDTH + (h + 1) * hd]
            kp_ref[h, :, hd:] = k_extra.astype(BF16)
            vp_ref[h, :, :hd] = x_ref[:, 2 * FOX_WIDTH + h * hd:2 * FOX_WIDTH + (h + 1) * hd]
            vp_ref[h, :, hd:] = ones(V_ONE, V_ONE + 4).astype(BF16)
            kt_ref[h] = kp_ref[h].astype(F32).T.astype(BF16)
            vt_ref[h] = vp_ref[h].astype(F32).T.astype(BF16)

    row3 = pl.BlockSpec((FOX_HEADS, tm, LANES), lambda i: (0, i, 0))
    col3 = pl.BlockSpec((FOX_HEADS, LANES, tm), lambda i: (0, 0, i))
    return pl.pallas_call(
        body, grid=(t // tm,),
        in_specs=[pl.BlockSpec((tm, QKV), lambda i: (i, 0)), pl.BlockSpec((tm, FOX_HEADS), lambda i: (i, 0))],
        out_specs=[row3, row3, row3, col3, col3],
        out_shape=[_sds((FOX_HEADS, t, LANES), BF16)] * 3 + [_sds((FOX_HEADS, LANES, t), BF16)] * 2,
        compiler_params=_cp(), name="attn_pack")(qkv, c_col)


def _triangle(nq, key_major):
    if key_major:
        pairs = [(i, j) for j in range(nq) for i in range(j, nq)]
    else:
        pairs = [(i, j) for i in range(nq) for j in range(i + 1)]
    return jnp.asarray([p[0] for p in pairs], jnp.int32), jnp.asarray([p[1] for p in pairs], jnp.int32)


def _attn_fwd(qp, kp, vt):
    t = qp.shape[1]
    bq = min(ATT_BLOCK, t)
    nq = t // bq
    nh = ATT_FWD_HEADS
    i_tab, j_tab = _triangle(nq, key_major=False)

    def body(it_ref, jt_ref, q_ref, k_ref, vt_ref, o_ref, lse_ref, m_sc, acc_sc):
        s = pl.program_id(1)
        i, j = it_ref[s], jt_ref[s]

        @pl.when(j == 0)
        def _():
            m_sc[...] = jnp.full(m_sc.shape, NEG, F32)
            acc_sc[...] = jnp.zeros(acc_sc.shape, F32)

        def sweep(masked):
            scores = lambda h: lax.dot_general(k_ref[h], q_ref[h], NT, preferred_element_type=F32)

            def accumulate(h, pt, rescale):
                acc_sc[h] = rescale * acc_sc[h] + jnp.dot(vt_ref[h], pt, preferred_element_type=F32)

            ahead, behind = scores(0), None
            for h in range(nh):
                st = ahead
                if h + 1 < nh:
                    ahead = scores(h + 1)
                if behind is not None:
                    accumulate(*behind)
                if masked:
                    key = lax.broadcasted_iota(jnp.int32, (bq, bq), 0)
                    qry = lax.broadcasted_iota(jnp.int32, (bq, bq), 1)
                    st = jnp.where(key <= qry, st, NEG)
                m_prev = m_sc[h]
                m_new = jnp.maximum(m_prev, jnp.max(st, axis=0, keepdims=True))
                behind = (h, jnp.exp(st - m_new).astype(BF16), jnp.exp(m_prev - m_new))
                m_sc[h] = m_new
            accumulate(*behind)

        @pl.when(j < i)
        def _():
            sweep(False)

        @pl.when(j == i)
        def _():
            sweep(True)
            for h in range(nh):
                acc = acc_sc[h]
                denom = acc[V_ONE:V_ONE + 1, :]
                o_ref[:, h * HEAD_DIM:(h + 1) * HEAD_DIM] = (acc[:HEAD_DIM, :] / denom).T.astype(BF16)
                lse_ref[h] = m_sc[h] + jnp.log(denom)

    grid_spec = pltpu.PrefetchScalarGridSpec(
        num_scalar_prefetch=2, grid=(FOX_HEADS // nh, i_tab.shape[0]),
        in_specs=[pl.BlockSpec((nh, bq, LANES), lambda hp, s, it, jt: (hp, it[s], 0)),
                  pl.BlockSpec((nh, bq, LANES), lambda hp, s, it, jt: (hp, jt[s], 0)),
                  pl.BlockSpec((nh, LANES, bq), lambda hp, s, it, jt: (hp, 0, jt[s]))],
        out_specs=[pl.BlockSpec((bq, nh * HEAD_DIM), lambda hp, s, it, jt: (it[s], hp)),
                   pl.BlockSpec((nh, 1, bq), lambda hp, s, it, jt: (hp, 0, it[s]))],
        scratch_shapes=[pltpu.VMEM((nh, 1, bq), F32), pltpu.VMEM((nh, LANES, bq), F32)])
    return pl.pallas_call(body, grid_spec=grid_spec,
                          out_shape=[_sds((t, FOX_WIDTH), BF16), _sds((FOX_HEADS, 1, t), F32)],
                          compiler_params=_cp(), name="attn_fwd")(i_tab, j_tab, qp, kp, vt)


def _conv_fwd(bch, conv_w):
    t = bch.shape[0]
    tm = min(ROW_TILE, t)
    halo_blocks = tm // SUBLANES
    cw = CONV_WIDTH

    def body(cur_ref, prev_ref, w_ref, o_ref):
        i = pl.program_id(0)
        z = cur_ref[:, cw:2 * cw] * cur_ref[:, 2 * cw:]
        zp = jnp.where(i == 0, 0.0, prev_ref[:, cw:2 * cw] * prev_ref[:, 2 * cw:])
        z1, z2 = _shift_down(z, zp)
        y = w_ref[0:1, :] * z2 + w_ref[1:2, :] * z1 + w_ref[2:3, :] * z
        o_ref[...] = (cur_ref[:, :cw] * y).astype(BF16)

    return pl.pallas_call(
        body, grid=(t // tm,),
        in_specs=[pl.BlockSpec((tm, BCH), lambda i: (i, 0)),
                  pl.BlockSpec((SUBLANES, BCH), lambda i: (jnp.maximum(i * halo_blocks - 1, 0), 0)),
                  _resident(conv_w.shape)],
        out_specs=pl.BlockSpec((tm, cw), lambda i: (i, 0)),
        out_shape=_sds((t, cw), BF16), compiler_params=_cp(), name="conv_fwd")(bch, bch, conv_w)


def _mm_res_ln(pairs, res, g, b, name):
    from_ln = isinstance(res, tuple)
    res_args = list(res) if from_ln else [res]
    t, d = res_args[0].shape
    tm = min(ROW_TILE, t)
    n = len(pairs)

    def body(*refs):
        a_refs, w_refs = refs[:n], refs[n:2 * n]
        res_refs = refs[2 * n:2 * n + len(res_args)]
        g_ref, b_ref, yb_ref, xh_ref, rs_ref = refs[2 * n + len(res_args):]
        r = res_refs[0][...]
        if from_ln:
            r = r * res_refs[1][...] + res_refs[2][...]
        z = ALPHA * r
        for a_ref, w_ref in zip(a_refs, w_refs):
            z = z + jnp.dot(a_ref[...].astype(BF16), w_ref[...], preferred_element_type=F32)
        xhat, rstd = _ln_fwd(z)
        yb_ref[...] = (xhat * g_ref[...] + b_ref[...]).astype(BF16)
        xh_ref[...] = xhat
        rs_ref[...] = rstd

    row = lambda i: (i, 0)
    full = pl.BlockSpec((tm, d), row)
    return pl.pallas_call(
        body, grid=(t // tm,),
        in_specs=[pl.BlockSpec((tm, a.shape[1]), row) for a, _ in pairs] + [_resident(w.shape) for _, w in pairs]
        + [full] + [_resident(a.shape) for a in res_args[1:]] + [_resident(g.shape), _resident(b.shape)],
        out_specs=[full, full, pl.BlockSpec((tm, 1), row)],
        out_shape=[_sds((t, d), BF16), _sds((t, d), F32), _sds((t, 1), F32)],
        compiler_params=_cp(), name=name)(*[a for a, _ in pairs], *[w for _, w in pairs], *res_args, g, b)


def _ffn_in(x, wi, name):
    t, d = x.shape
    tm = min(FFN_ROW_TILE, t)
    hh = HALF_HIDDEN

    def body(x_ref, w_ref, gu_ref, h_ref):
        a = x_ref[...].astype(BF16)
        for c in range(2):
            gs, us = slice(c * hh, (c + 1) * hh), slice(FFN_HIDDEN + c * hh, FFN_HIDDEN + (c + 1) * hh)
            g = jnp.dot(a, w_ref[c], preferred_element_type=F32)
            u = jnp.dot(a, w_ref[2 + c], preferred_element_type=F32)
            sig = _sigmoid(g)
            silu = g * sig
            gu_ref[:, gs] = (u * sig * (1.0 + g * (1.0 - sig))).astype(BF16)
            gu_ref[:, us] = silu.astype(BF16)
            h_ref[:, gs] = (silu * u).astype(BF16)

    row = lambda i: (i, 0)
    return pl.pallas_call(
        body, grid=(t // tm,),
        in_specs=[pl.BlockSpec((tm, d), row), _resident(wi.shape)],
        out_specs=[pl.BlockSpec((tm, 2 * FFN_HIDDEN), row), pl.BlockSpec((tm, FFN_HIDDEN), row)],
        out_shape=[_sds((t, 2 * FFN_HIDDEN), BF16), _sds((t, FFN_HIDDEN), BF16)],
        compiler_params=_cp(), name=name)(x, wi)


def _gmlp_fwd(x, w_in, vg, vb, wm, bs_col):
    t, d = x.shape
    tm = min(ROW_TILE, t)
    gb = GMLP_BLOCK

    def body(x_ref, w_ref, vg_ref, vb_ref, wm_ref, bs_ref, sv_ref, rs_ref, o_ref, a_sc):
        xb = x_ref[...].astype(BF16)
        nc = w_ref.shape[2]
        for j in range(w_ref.shape[0]):
            a_sc[:, j * nc:(j + 1) * nc] = jnp.dot(xb, w_ref[j], preferred_element_type=F32)
        halves = []
        for half in range(2):
            a = a_sc[:, half * d:(half + 1) * d]
            cdf = 0.5 * (1.0 + lax.erf(a * (2.0 ** -0.5)))
            halves.append(a * cdf)
            slope = cdf + a * (jnp.exp(-0.5 * a * a) * (1.0 / math.sqrt(2.0 * math.pi)))
            sv_ref[:, (2 * half + 1) * d:(2 * half + 2) * d] = slope.astype(BF16)
        u = halves[0]
        vhat, rstd = _ln_fwd(halves[1])
        sv_ref[:, :d] = u.astype(BF16)
        sv_ref[:, 2 * d:3 * d] = vhat.astype(BF16)
        rs_ref[...] = rstd
        vln = (vhat * vg_ref[...] + vb_ref[...]).astype(BF16)
        for blk in range(tm // gb):
            rs = slice(blk * gb, (blk + 1) * gb)
            for gi in range(GMLP_GROUPS):
                cs = slice(gi * gb, (gi + 1) * gb)
                s = jnp.dot(wm_ref[gi], vln[rs, cs], preferred_element_type=F32) + bs_ref[:, gi:gi + 1]
                o_ref[rs, cs] = (u[rs, cs] * s).astype(BF16)

    row = lambda i: (i, 0)
    return pl.pallas_call(
        body, grid=(t // tm,),
        in_specs=[pl.BlockSpec((tm, d), row), _resident(w_in.shape), _resident(vg.shape), _resident(vb.shape),
                  _resident(wm.shape), _resident(bs_col.shape)],
        out_specs=[pl.BlockSpec((tm, 4 * d), row), pl.BlockSpec((tm, 1), row), pl.BlockSpec((tm, d), row)],
        out_shape=[_sds((t, 4 * d), BF16), _sds((t, 1), F32), _sds((t, d), BF16)],
        scratch_shapes=[pltpu.VMEM((tm, 2 * d), F32)],
        compiler_params=_cp(), name="gmlp_fwd")(x, w_in, vg, vb, wm, bs_col)


def _loss_ln_bwd(xhat, rstd, g, b, target):
    t, d = xhat.shape
    tm = min(ROW_TILE, t)

    def body(xh_ref, rs_ref, g_ref, b_ref, t_ref, sq_ref, dz_ref, dg_ref, db_ref):
        first = pl.program_id(0) == 0
        xh = xh_ref[...]
        err = xh * g_ref[...] + b_ref[...] - t_ref[...]
        dz, dg, db = _ln_bwd(err * (1.0 / d), xh, rs_ref[...], g_ref[...])
        dz_ref[...] = dz
        _accumulate(sq_ref, first, jnp.sum(err * err, axis=0, keepdims=True))
        _accumulate(dg_ref, first, dg)
        _accumulate(db_ref, first, db)

    row = lambda i: (i, 0)
    vec = pl.BlockSpec((1, d), lambda i: (0, 0))
    return pl.pallas_call(
        body, grid=(t // tm,),
        in_specs=[pl.BlockSpec((tm, d), row), pl.BlockSpec((tm, 1), row), _resident(g.shape), _resident(b.shape),
                  pl.BlockSpec((tm, d), row)],
        out_specs=[vec, pl.BlockSpec((tm, d), row), vec, vec],
        out_shape=[_sds((1, d), F32), _sds((t, d), F32), _sds((1, d), F32), _sds((1, d), F32)],
        compiler_params=_cp(), name="loss_ln_bwd")(xhat, rstd, g, b, target)


def _mm_nt(pairs, ws, name, *, tm=ROW_TILE, res=None, ln=None, out_dtype=F32, after=None):
    t = pairs[0][0].shape[0]
    k = ws[0].shape[-2]
    tm = min(tm, t)
    n, nw = len(pairs), len(ws)

    def body(*refs):
        refs = refs[after is not None:]
        a_refs, w_refs = refs[:n], refs[n:n + nw]
        rest = list(refs[n + nw:])
        dx = None
        for a_ref, (_, wi, lo, hi) in zip(a_refs, pairs):
            w_ref = w_refs[wi]
            if len(w_ref.shape) == 3:
                nc = w_ref.shape[2]
                parts = [lax.dot_general(a_ref[:, j * nc:(j + 1) * nc].astype(BF16), w_ref[j], NT,
                                         preferred_element_type=F32) for j in range(w_ref.shape[0])]
            else:
                parts = [lax.dot_general(a_ref[...].astype(BF16), w_ref[:, lo:hi], NT, preferred_element_type=F32)]
            for part in parts:
                dx = part if dx is None else dx + part
        if res is not None:
            dx = dx + ALPHA * rest.pop(0)[...]
        if ln is None:
            rest[0][...] = dx.astype(out_dtype)
            return
        xh_ref, rs_ref, g_ref, dz_ref, dg_ref, db_ref = rest
        first = pl.program_id(0) == 0
        dz, dg, db = _ln_bwd(dx, xh_ref[...], rs_ref[...], g_ref[...])
        dz_ref[...] = dz
        _accumulate(dg_ref, first, dg)
        _accumulate(db_ref, first, db)

    row = lambda i: (i, 0)
    in_specs = [pl.BlockSpec((tm, a.shape[1]), row) for a, _, _, _ in pairs] + [_resident(w.shape) for w in ws]
    args = [a for a, _, _, _ in pairs] + list(ws)
    if res is not None:
        in_specs.append(pl.BlockSpec((tm, k), row))
        args.append(res)
    if ln is None:
        out_specs = pl.BlockSpec((tm, k), row)
        out_shape = _sds((t, k), out_dtype)
    else:
        xhat, rstd, g = ln
        in_specs += [pl.BlockSpec((tm, k), row), pl.BlockSpec((tm, 1), row), _resident(g.shape)]
        args += [xhat, rstd, g]
        vec = pl.BlockSpec((1, k), lambda i: (0, 0))
        out_specs = [pl.BlockSpec((tm, k), row), vec, vec]
        out_shape = [_sds((t, k), F32), _sds((1, k), F32), _sds((1, k), F32)]
    if after is not None:
        in_specs.insert(0, _ANY_SPEC)
        args.insert(0, after)
    return pl.pallas_call(body, grid=(t // tm,), in_specs=in_specs, out_specs=out_specs, out_shape=out_shape,
                          compiler_params=_cp(), name=name)(*args)


def _mm_tn(a, b, name, *, tn, tk=None, tt=None, stack_cols=False, out_dtype=BF16, after=None):
    t, k = a.shape
    n = b.shape[1]
    tk = k if tk is None else tk
    tt = min(REDUCE_TILE if tt is None else tt, t)
    nt = t // tt

    def body(a_ref, b_ref, *rest):
        o_ref, acc_ref = rest[after is not None:]
        s = pl.program_id(2)
        part = lax.dot_general(a_ref[...].astype(BF16), b_ref[...].astype(BF16), TN, preferred_element_type=F32)
        _accumulate(acc_ref, s == 0, part)

        @pl.when(s == nt - 1)
        def _():
            o_ref[...] = acc_ref[...].astype(out_dtype).reshape(o_ref.shape)

    if stack_cols:
        assert tk == k
        out_spec = pl.BlockSpec((1, k, tn), lambda kk, j, s: (j, 0, 0))
        out_shape = _sds((n // tn, k, tn), out_dtype)
    else:
        out_spec = pl.BlockSpec((tk, tn), lambda kk, j, s: (kk, j))
        out_shape = _sds((k, n), out_dtype)
    return pl.pallas_call(
        body, grid=(k // tk, n // tn, nt),
        in_specs=[pl.BlockSpec((tt, tk), lambda kk, j, s: (s, kk)), pl.BlockSpec((tt, tn), lambda kk, j, s: (s, j))]
        + ([_ANY_SPEC] if after is not None else []),
        out_specs=out_spec, out_shape=out_shape,
        scratch_shapes=[pltpu.VMEM((tk, tn), F32)],
        compiler_params=_cp(), name=name)(a, b, *([after] if after is not None else []))


def _ffn_bwd_hidden(dz, wo, gu, name):
    t, d = dz.shape
    tm = min(FFN_ROW_TILE, t)
    hh = HALF_HIDDEN

    def body(dz_ref, w_ref, gu_ref, o_ref):
        a = dz_ref[...].astype(BF16)
        for c in range(2):
            gs, us = slice(c * hh, (c + 1) * hh), slice(FFN_HIDDEN + c * hh, FFN_HIDDEN + (c + 1) * hh)
            dh = lax.dot_general(a, w_ref[gs, :], NT, preferred_element_type=F32)
            o_ref[:, gs] = (dh * gu_ref[:, gs].astype(F32)).astype(BF16)
            o_ref[:, us] = (dh * gu_ref[:, us].astype(F32)).astype(BF16)

    row = lambda i: (i, 0)
    return pl.pallas_call(
        body, grid=(t // tm,),
        in_specs=[pl.BlockSpec((tm, d), row), _resident(wo.shape), pl.BlockSpec((tm, 2 * FFN_HIDDEN), row)],
        out_specs=pl.BlockSpec((tm, 2 * FFN_HIDDEN), row),
        out_shape=_sds((t, 2 * FFN_HIDDEN), BF16), compiler_params=_cp(), name=name)(dz, wo, gu)


def _gmlp_bwd(dgated, saved, rstd_v, vg, vb, wm, bs_col):
    t, d = dgated.shape
    d2 = 2 * d
    tm = min(ROW_TILE, t)
    gb = GMLP_BLOCK

    def body(dg_ref, sv_ref, rs_ref, vg_ref, vb_ref, wm_ref, bs_ref, da_ref, dws_ref, dbs_ref, dvg_ref, dvb_ref, dvln_sc):
        first = pl.program_id(0) == 0
        u = sv_ref[:, :d].astype(F32)
        vhat = sv_ref[:, 2 * d:3 * d].astype(F32)
        rstd = rs_ref[...]
        vln = (vhat * vg_ref[...] + vb_ref[...]).astype(BF16)
        dgate = dg_ref[...]

        @pl.when(first)
        def _():
            dws_ref[...] = jnp.zeros(dws_ref.shape, F32)
            dbs_ref[...] = jnp.zeros(dbs_ref.shape, F32)

        for blk in range(tm // gb):
            rs = slice(blk * gb, (blk + 1) * gb)
            for gi in range(GMLP_GROUPS):
                cs = slice(gi * gb, (gi + 1) * gb)
                vblk = vln[rs, cs]
                s = jnp.dot(wm_ref[gi], vblk, preferred_element_type=F32) + bs_ref[:, gi:gi + 1]
                dgb = dgate[rs, cs]
                da_ref[rs, cs] = (dgb * s * sv_ref[rs, d + gi * gb:d + (gi + 1) * gb].astype(F32)).astype(BF16)
                ds = dgb * u[rs, cs]
                dsb = ds.astype(BF16)
                dws_ref[gi] += lax.dot_general(dsb, vblk, NT, preferred_element_type=F32)
                dbs_ref[:, gi:gi + 1] += jnp.sum(ds, axis=1, keepdims=True)
                dvln_sc[rs, cs] = lax.dot_general(wm_ref[gi], dsb, TN, preferred_element_type=F32)
        dv, dvg, dvb = _ln_bwd(dvln_sc[...], vhat, rstd, vg_ref[...])
        da_ref[:, d:] = (dv * sv_ref[:, 3 * d:].astype(F32)).astype(BF16)
        _accumulate(dvg_ref, first, dvg)
        _accumulate(dvb_ref, first, dvb)

    row = lambda i: (i, 0)
    vec = pl.BlockSpec((1, d), lambda i: (0, 0))
    return pl.pallas_call(
        body, grid=(t // tm,),
        in_specs=[pl.BlockSpec((tm, d), row), pl.BlockSpec((tm, 4 * d), row), pl.BlockSpec((tm, 1), row),
                  _resident(vg.shape), _resident(vb.shape), _resident(wm.shape), _resident(bs_col.shape)],
        out_specs=[pl.BlockSpec((tm, d2), row), pl.BlockSpec(wm.shape, lambda i: (0, 0, 0)),
                   pl.BlockSpec(bs_col.shape, lambda i: (0, 0)), vec, vec],
        out_shape=[_sds((t, d2), BF16), _sds(wm.shape, F32), _sds(bs_col.shape, F32), _sds((1, d), F32), _sds((1, d), F32)],
        scratch_shapes=[pltpu.VMEM((tm, d), F32)],
        compiler_params=_cp(), name="gmlp_bwd")(dgated, saved, rstd_v, vg, vb, wm, bs_col)


def _conv_bwd(bch, dmix, conv_w):
    t = bch.shape[0]
    tm = min(ROW_TILE, t)
    nb = t // tm
    halo_blocks = tm // SUBLANES
    cw = CONV_WIDTH

    def body(cur_ref, prev_ref, next_ref, dc_ref, dn_ref, w_ref, o_ref, dw_ref):
        i = pl.program_id(0)
        bgate, cgate, hval = cur_ref[:, :cw], cur_ref[:, cw:2 * cw], cur_ref[:, 2 * cw:]
        z = cgate * hval
        zp = jnp.where(i == 0, 0.0, prev_ref[:, cw:2 * cw] * prev_ref[:, 2 * cw:])
        z1, z2 = _shift_down(z, zp)
        w0, w1, w2 = w_ref[0:1, :], w_ref[1:2, :], w_ref[2:3, :]
        dconv = dc_ref[...]
        o_ref[:, :cw] = (dconv * (w0 * z2 + w1 * z1 + w2 * z)).astype(BF16)
        dy = dconv * bgate
        dyn = jnp.where(i == nb - 1, 0.0, dn_ref[...] * next_ref[:, :cw])
        dy1, dy2 = _shift_up(dy, dyn)
        dz = w2 * dy + w1 * dy1 + w0 * dy2
        o_ref[:, cw:2 * cw] = (dz * hval).astype(BF16)
        o_ref[:, 2 * cw:] = (dz * cgate).astype(BF16)

        @pl.when(i == 0)
        def _():
            dw_ref[...] = jnp.zeros(dw_ref.shape, F32)

        for tap, zs in enumerate((z2, z1, z)):
            dw_ref[tap:tap + 1, :] += jnp.sum(dy * zs, axis=0, keepdims=True)

    last_halo = t // SUBLANES - 1
    return pl.pallas_call(
        body, grid=(nb,),
        in_specs=[pl.BlockSpec((tm, BCH), lambda i: (i, 0)),
                  pl.BlockSpec((SUBLANES, BCH), lambda i: (jnp.maximum(i * halo_blocks - 1, 0), 0)),
                  pl.BlockSpec((SUBLANES, BCH), lambda i: (jnp.minimum((i + 1) * halo_blocks, last_halo), 0)),
                  pl.BlockSpec((tm, cw), lambda i: (i, 1)),
                  pl.BlockSpec((SUBLANES, cw), lambda i: (jnp.minimum((i + 1) * halo_blocks, last_halo), 1)),
                  _resident(conv_w.shape)],
        out_specs=[pl.BlockSpec((tm, BCH), lambda i: (i, 0)), pl.BlockSpec((SUBLANES, cw), lambda i: (0, 0))],
        out_shape=[_sds((t, BCH), BF16), _sds((SUBLANES, cw), F32)],
        compiler_params=_cp(), name="conv_bwd")(bch, bch, bch, dmix, dmix, conv_w)


def _attn_bwd_prep(o, dmix, qp, lse_col):
    t = o.shape[0]
    tm = min(ROW_TILE, t)
    hd = HEAD_DIM

    def body(o_ref, do_ref, qp_ref, lse_ref, qb_ref, dob_ref):
        lane = lax.broadcasted_iota(jnp.int32, (tm, hd), 1) + hd
        for h in range(FOX_HEADS):
            do = do_ref[:, h * hd:(h + 1) * hd]
            delta = jnp.sum(o_ref[:, h * hd:(h + 1) * hd].astype(F32) * do, axis=-1, keepdims=True)
            dob_ref[h, :, :hd] = do.astype(BF16)
            dob_ref[h, :, hd:] = _lane_pieces(lane, DO_DELTA, _split3(delta), -1.0).astype(BF16)
            qb_ref[h, :, :hd] = qp_ref[h, :, :hd]
            qb_ref[h, :, hd:] = (qp_ref[h, :, hd:].astype(F32)
                                 + _lane_pieces(lane, Q_LSE, _split3(lse_ref[:, h:h + 1]), -1.0)).astype(BF16)

    row3 = pl.BlockSpec((FOX_HEADS, tm, LANES), lambda i: (0, i, 0))
    return pl.pallas_call(
        body, grid=(t // tm,),
        in_specs=[pl.BlockSpec((tm, FOX_WIDTH), lambda i: (i, 0)), pl.BlockSpec((tm, FOX_WIDTH), lambda i: (i, 0)), row3,
                  pl.BlockSpec((tm, FOX_HEADS), lambda i: (i, 0))],
        out_specs=[row3, row3], out_shape=[_sds((FOX_HEADS, t, LANES), BF16)] * 2,
        compiler_params=_cp(), name="attn_bwd_prep")(o, dmix, qp, lse_col)


def _attn_bwd(qb, kp, vp, dob, kt):
    t = qb.shape[1]
    bq = min(ATT_BLOCK, t)
    nq = t // bq
    i_tab, j_tab = _triangle(nq, key_major=True)

    def body(it_ref, jt_ref, q_ref, k_ref, v_ref, do_ref, kt_ref, dqt_ref, dk_ref, dv_ref, dk_sc, dv_sc):
        s = pl.program_id(1)
        i, j = it_ref[s], jt_ref[s]

        @pl.when(s == 0)
        def _():
            dqt_ref[...] = jnp.zeros(dqt_ref.shape, F32)

        @pl.when(i == j)
        def _():
            dk_sc[...] = jnp.zeros(dk_sc.shape, F32)
            dv_sc[...] = jnp.zeros(dv_sc.shape, F32)

        cols = pl.ds(pl.multiple_of(i * bq, bq), bq)

        def sweep(masked):
            def scores(h):
                return (lax.dot_general(k_ref[h], q_ref[h], NT, preferred_element_type=F32),
                        lax.dot_general(v_ref[h], do_ref[h], NT, preferred_element_type=F32))

            def accumulate(h, ptb, dstb):
                dv_sc[h] += jnp.dot(ptb, do_ref[h], preferred_element_type=F32)
                dk_sc[h] += jnp.dot(dstb, q_ref[h], preferred_element_type=F32)
                dqt_ref[h, :, cols] += jnp.dot(kt_ref[h], dstb, preferred_element_type=F32)

            ahead, behind = scores(0), None
            for h in range(ATT_BWD_HEADS):
                st, dpt = ahead
                if h + 1 < ATT_BWD_HEADS:
                    ahead = scores(h + 1)
                if behind is not None:
                    accumulate(*behind)
                if masked:
                    key = lax.broadcasted_iota(jnp.int32, (bq, bq), 0)
                    qry = lax.broadcasted_iota(jnp.int32, (bq, bq), 1)
                    st = jnp.where(key <= qry, st, NEG)
                pt = jnp.exp(st)
                behind = (h, pt.astype(BF16), (pt * dpt).astype(BF16))
            accumulate(*behind)

        @pl.when(i == j)
        def _():
            sweep(True)

        @pl.when(i > j)
        def _():
            sweep(False)

        @pl.when(i == nq - 1)
        def _():
            dk_ref[...] = dk_sc[...]
            dv_ref[...] = dv_sc[...].astype(BF16)

    nh = ATT_BWD_HEADS
    qblk = pl.BlockSpec((nh, bq, LANES), lambda hp, s, it, jt: (hp, it[s], 0))
    kblk = pl.BlockSpec((nh, bq, LANES), lambda hp, s, it, jt: (hp, jt[s], 0))
    grid_spec = pltpu.PrefetchScalarGridSpec(
        num_scalar_prefetch=2, grid=(FOX_HEADS // nh, i_tab.shape[0]),
        in_specs=[qblk, kblk, kblk, qblk, pl.BlockSpec((nh, LANES, bq), lambda hp, s, it, jt: (hp, 0, jt[s]))],
        out_specs=[pl.BlockSpec((nh, LANES, t), lambda hp, s, it, jt: (hp, 0, 0), pipeline_mode=pl.Buffered(1)),
                   kblk, kblk],
        scratch_shapes=[pltpu.VMEM((nh, bq, LANES), F32), pltpu.VMEM((nh, bq, LANES), F32)])
    return pl.pallas_call(body, grid_spec=grid_spec,
                          out_shape=[_sds((FOX_HEADS, LANES, t), F32), _sds((FOX_HEADS, t, LANES), F32),
                                     _sds((FOX_HEADS, t, LANES), BF16)],
                          compiler_params=_cp(), name="attn_bwd")(i_tab, j_tab, qb, kp, vp, dob, kt)


def _attn_unpack(dqt, dkp, dvp):
    t = dkp.shape[1]
    tm = min(ROW_TILE, t)
    hd = HEAD_DIM

    def body(dqt_ref, dk_ref, dv_ref, o_ref, dc_ref):
        for h in range(FOX_HEADS):
            dq = dqt_ref[h].T
            o_ref[:, h * hd:(h + 1) * hd] = (dq[:, :hd] * (hd ** -0.5)).astype(BF16)
            o_ref[:, FOX_WIDTH + h * hd:FOX_WIDTH + (h + 1) * hd] = dk_ref[h, :, :hd].astype(BF16)
            o_ref[:, 2 * FOX_WIDTH + h * hd:2 * FOX_WIDTH + (h + 1) * hd] = dv_ref[h, :, :hd]
            dc_ref[:, h:h + 1] = dq[:, K_ONE:K_ONE + 1] - dk_ref[h, :, Q_ONE:Q_ONE + 1]

    row3 = pl.BlockSpec((FOX_HEADS, tm, LANES), lambda i: (0, i, 0))
    return pl.pallas_call(
        body, grid=(t // tm,),
        in_specs=[pl.BlockSpec((FOX_HEADS, LANES, tm), lambda i: (0, 0, i)), row3, row3],
        out_specs=[pl.BlockSpec((tm, QKV), lambda i: (i, 0)), pl.BlockSpec((tm, FOX_HEADS), lambda i: (i, 0))],
        out_shape=[_sds((t, QKV), BF16), _sds((t, FOX_HEADS), F32)],
        compiler_params=_cp(), name="attn_unpack")(dqt, dkp, dvp)


def _adamw(parts, w, m, v, name, layer=None, into=None):
    nl, r, c = w.shape
    fits = [cand for cand in [*range(SUBLANES, r, SUBLANES), r] if r % cand == 0 and cand * c * 4 <= ADAMW_BLOCK_BYTES]
    tr = max(fits) if fits else r
    npart = len(parts)
    bc1 = 1.0 - ADAM_B1 ** ADAM_STEP
    bc2 = 1.0 - ADAM_B2 ** ADAM_STEP

    def body(*refs):
        p_refs = refs[:npart]
        w_ref, m_ref, v_ref = refs[npart:npart + 3]
        g_ref, d_ref, nm_ref, nv_ref = refs[-4:]
        sums = []
        for p_ref in p_refs:
            acc = p_ref[0, 0].astype(F32)
            for s in range(1, p_ref.shape[0]):
                acc = acc + p_ref[s, 0].astype(F32)
            sums.append(acc)
        g = sums[0]
        for extra in sums[1:]:
            g = g + extra
        nm = ADAM_B1 * m_ref[0] + (1.0 - ADAM_B1) * g
        nv = ADAM_B2 * v_ref[0] + (1.0 - ADAM_B2) * (g * g)
        m_hat = nm / bc1
        v_hat = nv / bc2
        g_ref[0] = g
        d_ref[0] = -ADAM_LR * (m_hat / (jnp.sqrt(v_hat) + ADAM_EPS) + ADAM_WD * w_ref[0])
        nm_ref[0] = nm
        nv_ref[0] = nv

    first = 0 if layer is None else layer
    blk = pl.BlockSpec((1, tr, c), lambda l, i: (first + l, i, 0))
    extra = [] if into is None else list(into)
    return pl.pallas_call(
        body, grid=(nl if layer is None else 1, r // tr),
        in_specs=[pl.BlockSpec((p.shape[0], 1, tr, c), lambda l, i: (0, l, i, 0)) for p in parts] + [blk, blk, blk]
        + [_ANY_SPEC] * len(extra),
        out_specs=[blk] * 4, out_shape=[_sds(w.shape, F32)] * 4,
        input_output_aliases={npart + 3 + k: k for k in range(len(extra))},
        compiler_params=_cp(), name=name)(*parts, w, m, v, *extra)


def _to_rows(a):
    flat = a.reshape(-1)
    pad = (-flat.shape[0]) % LANES
    if pad:
        flat = jnp.concatenate([flat, jnp.zeros((pad,), flat.dtype)])
    return flat.reshape(-1, LANES)


def _by_owner_cols(dw):
    k, n = dw.shape
    return dw.reshape(k, N_CHIPS, n // N_CHIPS).transpose(1, 0, 2)[:, None]


def _ffn_fwd(xin_ln, xin_b, wi, wo, g, b, layer):
    gu, h = _ffn_in(xin_b, wi, f"ffn_in_{layer}")
    y_b, xhat, rstd = _mm_res_ln([(h, wo)], xin_ln, g, b, f"ffn_out_ln_{layer}")
    return y_b, (xin_b, gu, h, xhat, rstd)


def _ffn_bwd(dz, saved, wi, wo, ln_below, layer):
    xin_b, gu, h, _, _ = saved
    dgu = _ffn_bwd_hidden(dz, wo, gu, f"ffn_bwd_hidden_{layer}")
    g_out = _mm_tn(h, dz, f"ffn_dw_out_{layer}", tn=D_MODEL, tk=HALF_HIDDEN, tt=REDUCE_TILE // 2)
    g_in = _mm_tn(xin_b, dgu, f"ffn_dw_in_{layer}", tn=HALF_HIDDEN, stack_cols=True)
    below = _mm_nt([(dgu, 0, 0, 0)], [wi], f"ffn_dx_{layer}", tm=FFN_ROW_TILE, res=dz, ln=ln_below)
    return below, g_in, g_out.reshape(N_CHIPS, FFN_HIDDEN // N_CHIPS, D_MODEL)


def kernel(x, even_w_in, even_b_f, even_conv_w, even_w_out, odd_w_in, odd_v_ln_g, odd_v_ln_b, odd_w_s, odd_b_s, odd_w_out, mix_ln_g, mix_ln_b, ffn_w_in, ffn_w_out, ffn_ln_g, ffn_ln_b, loss_target, m_even_w_in, m_even_b_f, m_even_conv_w, m_even_w_out, m_odd_w_in, m_odd_v_ln_g, m_odd_v_ln_b, m_odd_w_s, m_odd_b_s, m_odd_w_out, m_mix_ln_g, m_mix_ln_b, m_ffn_w_in, m_ffn_w_out, m_ffn_ln_g, m_ffn_ln_b, v_even_w_in, v_even_b_f, v_even_conv_w, v_even_w_out, v_odd_w_in, v_odd_v_ln_g, v_odd_v_ln_b, v_odd_w_s, v_odd_b_s, v_odd_w_out, v_mix_ln_g, v_mix_ln_b, v_ffn_w_in, v_ffn_w_out, v_ffn_ln_g, v_ffn_ln_b):
    t = x.shape[1]
    d = D_MODEL
    chip = 2 * lax.axis_index("x") + lax.axis_index("y")
    x2d = x[0]
    target = loss_target[0]

    small_shard = jnp.concatenate([odd_v_ln_g.reshape(2, LANES), odd_v_ln_b.reshape(2, LANES),
                                   even_conv_w.reshape(CONV_K, LANES), jnp.zeros((1, LANES), F32)], axis=0)
    first = [even_w_in[0].astype(BF16), even_w_out[0].astype(BF16), small_shard]
    later = [odd_w_in[0].astype(BF16), odd_w_out[0].astype(BF16), ffn_w_in[0].astype(BF16), ffn_w_in[1].astype(BF16),
             ffn_w_out[0].astype(BF16), ffn_w_out[1].astype(BF16)]
    first_h, first_tok = _split_start(first, "gather4", "gather_first_start")
    later_h, later_tok = _split_start(later, "gather4", "gather_later_start", after=first_tok)
    g_ewi, g_ewo, g_small = [_with_own(g, own) for g, own in
                             zip(_split_wait(first_h, "gather_first_wait", later_tok), first)]
    ewi = g_ewi.transpose(1, 0, 2).reshape(d, EVEN_IN)
    w_even_in = jnp.concatenate([ewi[:, :QKV], ewi[:, QKV + FOX_HEADS:], ewi[:, QKV:QKV + FOX_HEADS],
                                 jnp.zeros((d, LANES - FOX_HEADS), BF16)], axis=1)
    w_even_out = g_ewo.reshape(d, d)
    v_ln_g = g_small[:, 0:2].reshape(1, d)
    v_ln_b = g_small[:, 2:4].reshape(1, d)
    conv_w = g_small[:, 4:7].transpose(1, 0, 2).reshape(CONV_K, CONV_WIDTH)
    chunk_id = jnp.arange(GMLP_BLOCK) // CHUNK
    gmask = chunk_id[None, :] <= chunk_id[:, None]
    w_spatial = jnp.where(gmask[None], odd_w_s[0], 0.0).astype(BF16)
    bs_col = odd_b_s[0].T
    b_f_col = even_b_f.reshape(FOX_HEADS, 1)
    ln = lambda p, l: p[l:l + 1]

    qkv, bch, fl = _proj(x2d, w_even_in, [(0, QKV, BF16), (QKV, QKV + BCH, F32), (QKV + BCH, EVEN_IN_PAD, F32)], "even_proj")
    fl3 = fl[:, :FOX_HEADS].T.reshape(FOX_HEADS, t // LANES, LANES).transpose(1, 0, 2)
    c3 = _fgate_fwd(fl3, b_f_col)
    c_rows = c3.transpose(1, 0, 2).reshape(FOX_HEADS, t)
    qp, kp, vp, kt, vt = _attn_pack(qkv, c_rows.T)
    attn, lse = _attn_fwd(qp, kp, vt)
    conv = _conv_fwd(bch, conv_w)
    x1_b, xh1, rs1 = _mm_res_ln([(attn, w_even_out[:FOX_WIDTH]), (conv, w_even_out[FOX_WIDTH:])], x2d,
                                ln(mix_ln_g, 0), ln(mix_ln_b, 0), "even_out_ln")
    w_odd_in, g_owo, w_fi0, w_fi1, g_fo0, g_fo1 = [_with_own(g, own) for g, own in
                                                   zip(_split_wait(later_h, "gather_later_wait", x1_b), later)]
    w_odd_out = g_owo.reshape(d, d)
    w_ffn_in = [w_fi0, w_fi1]
    w_ffn_out = [g_fo0.reshape(FFN_HIDDEN, d), g_fo1.reshape(FFN_HIDDEN, d)]
    x2_b, ffn0 = _ffn_fwd((xh1, ln(mix_ln_g, 0), ln(mix_ln_b, 0)), x1_b, w_ffn_in[0], w_ffn_out[0],
                          ln(ffn_ln_g, 0), ln(ffn_ln_b, 0), 0)

    sv_odd, rs_odd, gated = _gmlp_fwd(x2_b, w_odd_in, v_ln_g, v_ln_b, w_spatial, bs_col)
    x3_b, xh3, rs3 = _mm_res_ln([(gated, w_odd_out)], (ffn0[3], ln(ffn_ln_g, 0), ln(ffn_ln_b, 0)),
                                ln(mix_ln_g, 1), ln(mix_ln_b, 1), "odd_out_ln")
    _, ffn1 = _ffn_fwd((xh3, ln(mix_ln_g, 1), ln(mix_ln_b, 1)), x3_b, w_ffn_in[1], w_ffn_out[1],
                       ln(ffn_ln_g, 1), ln(ffn_ln_b, 1), 1)

    sq, dz4, d_fg1, d_fb1 = _loss_ln_bwd(ffn1[3], ffn1[4], ln(ffn_ln_g, 1), ln(ffn_ln_b, 1), target)
    loss = lax.psum(0.5 / d * jnp.sum(sq), ("x", "y", "c"))
    (dz3, d_mg1, d_mb1), gi_f1, go_f1 = _ffn_bwd(dz4, ffn1, w_ffn_in[1], w_ffn_out[1], (xh3, rs3, ln(mix_ln_g, 1)), 1)

    dgated = _mm_nt([(dz3, 0, 0, d)], [w_odd_out], "odd_dgated")
    go_odd = _mm_tn(gated, dz3, "odd_dw_out", tn=d).reshape(N_CHIPS, 1, d // N_CHIPS, d)
    da_odd, dws, dbs_col, d_vg, d_vb = _gmlp_bwd(dgated, sv_odd, rs_odd, v_ln_g, v_ln_b, w_spatial, bs_col)
    gi_odd = _mm_tn(x2_b, da_odd, "odd_dw_in", tn=d // 2, stack_cols=True)[:, None]
    dz2, d_fg0, d_fb0 = _mm_nt([(da_odd, 0, 0, 0)], [w_odd_in], "odd_dx", res=dz3,
                               ln=(ffn0[3], ffn0[4], ln(ffn_ln_g, 0)))
    (dz1, d_mg0, d_mb0), gi_f0, go_f0 = _ffn_bwd(dz2, ffn0, w_ffn_in[0], w_ffn_out[0], (xh1, rs1, ln(mix_ln_g, 0)), 0)

    sent_early = [gi_odd, go_odd, gi_f0[:, None], gi_f1[:, None], go_f0[:, None], go_f1[:, None]]
    early_h, early_tok = _split_start(sent_early, "scatter4", "scatter_early_start")
    dmix = _mm_nt([(dz1, 0, 0, d)], [w_even_out], "even_dmix", after=early_tok)
    go_even = jnp.concatenate([_mm_tn(attn, dz1, "even_dw_out_attn", tn=d), _mm_tn(conv, dz1, "even_dw_out_conv", tn=d)],
                              axis=0).reshape(N_CHIPS, 1, d // N_CHIPS, d)
    dbch, dconv_w8 = _conv_bwd(bch, dmix, conv_w)
    qb, dob = _attn_bwd_prep(attn, dmix, qp, lse.reshape(FOX_HEADS, t).T)
    dqkv, dc_col = _attn_unpack(*_attn_bwd(qb, kp, vp, dob, kt))
    dc3 = dc_col.T.reshape(FOX_HEADS, t // LANES, LANES).transpose(1, 0, 2)
    dfl3, d_bf = _fgate_bwd(dc3, fl3, b_f_col)
    dfl = jnp.concatenate([dfl3.transpose(1, 0, 2).reshape(FOX_HEADS, t).T.astype(BF16),
                           jnp.zeros((t, LANES - FOX_HEADS), BF16)], axis=1)

    dws_masked = jnp.where(gmask[None], dws, 0.0)
    rep_names = ["odd_w_s", "odd_b_s", "mix_ln_g", "mix_ln_b", "ffn_ln_g", "ffn_ln_b", "even_b_f"]
    rep_grads = [dws_masked, dbs_col.T, jnp.concatenate([d_mg0, d_mg1]), jnp.concatenate([d_mb0, d_mb1]),
                 jnp.concatenate([d_fg0, d_fg1]), jnp.concatenate([d_fb0, d_fb1]), d_bf.reshape(1, FOX_HEADS)]
    rep_w = [(odd_w_s, m_odd_w_s, v_odd_w_s), (odd_b_s, m_odd_b_s, v_odd_b_s), (mix_ln_g, m_mix_ln_g, v_mix_ln_g),
             (mix_ln_b, m_mix_ln_b, v_mix_ln_b), (ffn_ln_g, m_ffn_ln_g, v_ffn_ln_g), (ffn_ln_b, m_ffn_ln_b, v_ffn_ln_b),
             (even_b_f, m_even_b_f, v_even_b_f)]
    rep_rows = [_to_rows(gr) for gr in rep_grads]
    n_rep = sum(r.shape[0] for r in rep_rows)
    pad_rep = (-n_rep) % SUBLANES
    dconv_w = dconv_w8[:CONV_K].reshape(CONV_K, N_CHIPS, LANES).transpose(1, 0, 2).reshape(N_CHIPS * CONV_K, LANES)
    packed = jnp.concatenate(rep_rows + [jnp.zeros((pad_rep, LANES), F32), d_vg.reshape(SUBLANES, LANES),
                                         d_vb.reshape(SUBLANES, LANES), dconv_w, jnp.zeros((4, LANES), F32)], axis=0)
    small_h, small_tok = _split_start([packed], "gather8", "gather_small_start")

    chip_blk = lambda g: lax.dynamic_index_in_dim(g, chip, 0, keepdims=False)
    mine_early = [_with_own(r, chip_blk(g)) for r, g in
                  zip(_split_wait(early_h, "scatter_early_wait", small_tok), sent_early)]
    swap_h, swap_tok = _split_start(mine_early, "swap2", "swap_early_start")
    dw_qkv = _mm_tn(dqkv, x2d, "even_dw_qkv", tn=d, tk=QKV // 2, after=swap_tok)
    dw_bch = _mm_tn(dbch, x2d, "even_dw_bch", tn=d, tk=BCH // 2)
    dw_f = _mm_tn(dfl, x2d, "even_dw_f", tn=d)
    gi_even = jnp.concatenate([dw_qkv, dw_f[:FOX_HEADS], dw_bch], axis=0).reshape(N_CHIPS, 1, -1, LANES)
    sent_late = [gi_even, go_even]
    late_h, late_tok = _split_start(sent_late, "scatter4", "scatter_late_start")
    grad_x = _mm_nt([(dqkv, 0, 0, QKV), (dbch, 0, QKV, QKV + BCH), (dfl, 0, QKV + BCH, EVEN_IN_PAD)], [w_even_in],
                    "even_dx", res=dz1, after=late_tok)
    mine_late = [_with_own(r, chip_blk(g)) for r, g in zip(_split_wait(late_h, "scatter_late_wait", grad_x), sent_late)]
    theirs_late = _exchange(mine_late, "swap2", "swap_late")
    theirs_early = _split_wait(swap_h, "swap_early_wait", theirs_late[0])
    (gathered,) = _split_wait(small_h, "gather_small_wait", theirs_early[0])
    gathered = lax.dynamic_update_index_in_dim(gathered, packed, 4 * lax.axis_index("x") + 2 * lax.axis_index("y")
                                               + lax.axis_index("c"), 0)
    mine, theirs = mine_late + mine_early, theirs_late + theirs_early
    big_w = [(even_w_in, m_even_w_in, v_even_w_in), (even_w_out, m_even_w_out, v_even_w_out),
             (odd_w_in, m_odd_w_in, v_odd_w_in), (odd_w_out, m_odd_w_out, v_odd_w_out)]
    big_names = ["even_w_in", "even_w_out", "odd_w_in", "odd_w_out"]
    res = {}
    for nm, own, sib, (w, m, v) in zip(big_names, mine, theirs, big_w):
        if nm == "even_w_in":
            rows = lambda a: jnp.swapaxes(a, 1, 2).reshape(1, -1, LANES)
            back = lambda a: jnp.swapaxes(a.reshape(1, EVEN_IN // N_CHIPS, d), 1, 2)
            res[nm] = [back(o) for o in _adamw([own, sib], rows(w), rows(m), rows(v), f"adamw_{nm}")]
            continue
        res[nm] = _adamw([own, sib], w, m, v, f"adamw_{nm}")
    for nm, at, (w, m, v) in (("ffn_w_in", 4, (ffn_w_in, m_ffn_w_in, v_ffn_w_in)),
                              ("ffn_w_out", 6, (ffn_w_out, m_ffn_w_out, v_ffn_w_out))):
        upper = _adamw([mine[at + 1], theirs[at + 1]], w, m, v, f"adamw_{nm}_1", layer=1)
        res[nm] = _adamw([mine[at], theirs[at]], w, m, v, f"adamw_{nm}_0", layer=0, into=upper)

    base = n_rep + pad_rep
    own_rows = jnp.concatenate([
        lax.dynamic_slice_in_dim(gathered, base + 2 * chip, 2, axis=1),
        lax.dynamic_slice_in_dim(gathered, base + SUBLANES + 2 * chip, 2, axis=1),
        lax.dynamic_slice_in_dim(gathered, base + 2 * SUBLANES + CONV_K * chip, CONV_K, axis=1),
        jnp.zeros((N_DEV, 1, LANES), F32)], axis=1)
    small_parts = jnp.concatenate([gathered[:, :base], own_rows], axis=1)[:, None]

    def pack_small(get):
        rows = [_to_rows(get(tw)) for tw in rep_w] + [jnp.zeros((pad_rep, LANES), F32)]
        rows += [get(sh).reshape(-1, LANES) for sh in ((odd_v_ln_g, m_odd_v_ln_g, v_odd_v_ln_g),
                                                       (odd_v_ln_b, m_odd_v_ln_b, v_odd_v_ln_b),
                                                       (even_conv_w, m_even_conv_w, v_even_conv_w))]
        return jnp.concatenate(rows + [jnp.zeros((1, LANES), F32)], axis=0)[None]

    small_out = _adamw([small_parts], pack_small(lambda tw: tw[0]), pack_small(lambda tw: tw[1]),
                       pack_small(lambda tw: tw[2]), "adamw_small")

    def unpack_small(rows3):
        rows = rows3[0]
        out, off = {}, 0
        for nm, (w, _, _), r in zip(rep_names, rep_w, rep_rows):
            out[nm] = rows[off:off + r.shape[0]].reshape(-1)[:w.size].reshape(w.shape)
            off += r.shape[0]
        off += pad_rep
        out["odd_v_ln_g"] = rows[off:off + 2].reshape(odd_v_ln_g.shape)
        out["odd_v_ln_b"] = rows[off + 2:off + 4].reshape(odd_v_ln_b.shape)
        out["even_conv_w"] = rows[off + 4:off + 4 + CONV_K].reshape(even_conv_w.shape)
        return out

    small = [unpack_small(o) for o in small_out]
    order = ["even_w_in", "even_b_f", "even_conv_w", "even_w_out", "odd_w_in", "odd_v_ln_g", "odd_v_ln_b", "odd_w_s",
             "odd_b_s", "odd_w_out", "mix_ln_g", "mix_ln_b", "ffn_w_in", "ffn_w_out", "ffn_ln_g", "ffn_ln_b"]
    outs = [loss, grad_x[None]]
    for kind in range(4):
        for nm in order:
            outs.append(res[nm][kind] if nm in res else small[kind][nm])
    return tuple(outs)
```

```python
import functools
import math

import jax
import jax.numpy as jnp
from jax import lax
from jax.experimental import pallas as pl
from jax.experimental.pallas import tpu as pltpu

F32 = jnp.float32
BF16 = jnp.bfloat16

D_MODEL = 1024
FOX_HEADS = 8
HEAD_DIM = 64
HEAD_PAIRS = FOX_HEADS // 2
FOX_WIDTH = FOX_HEADS * HEAD_DIM
CONV_WIDTH = 512
CONV_K = 3
QKV = 3 * FOX_WIDTH
BCH = 3 * CONV_WIDTH
EVEN_IN = QKV + FOX_HEADS + BCH
EVEN_IN_PAD = QKV + BCH + 128
GMLP_BLOCK = 128
GMLP_GROUPS = 8
CHUNK = 64
FFN_HIDDEN = 2816
HALF_HIDDEN = FFN_HIDDEN // 2
ALPHA = 4.0 ** 0.25
LN_EPS = 1e-5
ADAM_LR = 0.001
ADAM_B1 = 0.9
ADAM_B2 = 0.999
ADAM_EPS = 1e-08
ADAM_WD = 0.01
ADAM_STEP = 10
N_CHIPS = 4
N_DEV = 8
LANES = 128
SUBLANES = 8
ROW_TILE = 512
FFN_ROW_TILE = 512
REDUCE_TILE = 2048
ATT_BLOCK = 512
ATT_FWD_HEADS = 8
ATT_BWD_HEADS = 4
ADAMW_BLOCK_BYTES = 2 ** 20
VMEM_LIMIT = 56 * 2 ** 20
NEG = -1e30
MESH = pl.DeviceIdType.MESH
HIGHEST = lax.Precision.HIGHEST
Q_C, Q_ONE, Q_LSE = 64, 67, 70
K_ONE, K_C, K_ONE2 = 64, 67, 70
V_ONE = 64
DO_DELTA = 65
NT = (((1,), (1,)), ((), ()))
TN = (((0,), (0,)), ((), ()))


def _cp():
    return pltpu.CompilerParams(vmem_limit_bytes=VMEM_LIMIT)


def _resident(shape):
    zeros = (0,) * len(shape)
    return pl.BlockSpec(shape, lambda *_: zeros, pipeline_mode=pl.Buffered(1))


def _sds(shape, dtype):
    return jax.ShapeDtypeStruct(tuple(shape), dtype)


_MASKS = {
    "gather4": [(1, 0, 0), (0, 1, 0), (1, 1, 0)],
    "scatter4": [(1, 0, 0), (0, 1, 0), (1, 1, 0)],
    "swap2": [(0, 0, 1)],
    "gather8": [(0, 0, 1), (0, 1, 0), (0, 1, 1), (1, 0, 0), (1, 0, 1), (1, 1, 0), (1, 1, 1)],
}


def _exchange(arrs, mode, name):
    n = len(arrs)
    masks = _MASKS[mode]
    npeer = len(masks)
    lead = {"gather4": N_CHIPS, "gather8": N_DEV}.get(mode)
    out_shapes = [_sds(((lead,) if lead else ()) + a.shape, a.dtype) for a in arrs]

    def body(*refs):
        ins, outs = refs[:n], refs[n:2 * n]
        send_sems, recv_sems, loc_sems = refs[2 * n:]
        x, y, c = lax.axis_index("x"), lax.axis_index("y"), lax.axis_index("c")
        chip, dev = 2 * x + y, 4 * x + 2 * y + c
        sends, recvs, locs = [], [], []
        for k in range(n):
            if mode == "gather4":
                locs.append(pltpu.make_async_copy(ins[k], outs[k].at[chip], loc_sems.at[k]))
            elif mode == "scatter4":
                locs.append(pltpu.make_async_copy(ins[k].at[chip], outs[k].at[chip], loc_sems.at[k]))
            elif mode == "gather8":
                locs.append(pltpu.make_async_copy(ins[k], outs[k].at[dev], loc_sems.at[k]))
        for cp in locs:
            cp.start()
        for k in range(n):
            for j, (dx, dy, dc) in enumerate(masks):
                px = 1 - x if dx else x
                py = 1 - y if dy else y
                pc = 1 - c if dc else c
                pchip, pdev = 2 * px + py, 4 * px + 2 * py + pc
                if mode == "gather4":
                    src, dst, land = ins[k], outs[k].at[chip], outs[k].at[pchip]
                elif mode == "scatter4":
                    src, dst, land = ins[k].at[pchip], outs[k].at[chip], outs[k].at[pchip]
                elif mode == "swap2":
                    src, dst, land = ins[k], outs[k], outs[k]
                else:
                    src, dst, land = ins[k], outs[k].at[dev], outs[k].at[pdev]
                s = k * npeer + j
                kw = dict(send_sem=send_sems.at[s], recv_sem=recv_sems.at[s], device_id=(px, py, pc),
                          device_id_type=MESH)
                cp = pltpu.make_async_remote_copy(src_ref=src, dst_ref=dst, **kw)
                cp.start()
                sends.append(cp)
                recvs.append(pltpu.make_async_remote_copy(src_ref=src, dst_ref=land, **kw))
        for cp in recvs:
            cp.wait_recv()
        for cp in sends:
            cp.wait_send()
        for cp in locs:
            cp.wait()

    any_spec = pl.BlockSpec(memory_space=pl.ANY)
    outs = pl.pallas_call(
        body,
        out_shape=out_shapes,
        in_specs=[any_spec] * n,
        out_specs=[any_spec] * n,
        scratch_shapes=[pltpu.SemaphoreType.DMA((n * npeer,)), pltpu.SemaphoreType.DMA((n * npeer,)),
                        pltpu.SemaphoreType.DMA((max(n, 1),))],
        name=name,
    )(*arrs)
    return list(outs)


_HBM_SPEC = pl.BlockSpec(memory_space=pltpu.HBM)
_SEM_SPEC = pl.BlockSpec(memory_space=pltpu.SEMAPHORE)
_ANY_SPEC = pl.BlockSpec(memory_space=pl.ANY)
_EFFECT = pltpu.SideEffectType.DATAFLOW_SIDE_EFFECTING


def _split_copies(mode, ins, lands, send_sems, recv_sems):
    x, y, c = lax.axis_index("x"), lax.axis_index("y"), lax.axis_index("c")
    chip, dev = 2 * x + y, 4 * x + 2 * y + c
    masks = _MASKS[mode]
    out = []
    for k in range(len(ins)):
        for j, (dx, dy, dc) in enumerate(masks):
            px = 1 - x if dx else x
            py = 1 - y if dy else y
            pc = 1 - c if dc else c
            pchip, pdev = 2 * px + py, 4 * px + 2 * py + pc
            if mode == "gather4":
                src, dst, land = ins[k], lands[k].at[chip], lands[k].at[pchip]
            elif mode == "scatter4":
                src, dst, land = ins[k].at[pchip], lands[k].at[chip], lands[k].at[pchip]
            elif mode == "swap2":
                src, dst, land = ins[k], lands[k], lands[k]
            else:
                src, dst, land = ins[k], lands[k].at[dev], lands[k].at[pdev]
            s = k * len(masks) + j
            kw = dict(send_sem=send_sems.at[s], recv_sem=recv_sems.at[s], device_id=(px, py, pc), device_id_type=MESH)
            out.append((pltpu.make_async_remote_copy(src_ref=src, dst_ref=dst, **kw),
                        pltpu.make_async_remote_copy(src_ref=src, dst_ref=land, **kw)))
    return out


def _split_start(arrs, mode, name, after=None):
    n = len(arrs)
    nsem = n * len(_MASKS[mode])
    lead = {"gather4": (N_CHIPS,), "gather8": (N_DEV,)}.get(mode, ())
    land_shapes = [lead + a.shape for a in arrs]

    def body(*refs):
        ins, lands = refs[:n], refs[n:2 * n]
        outs = refs[2 * n + (after is not None):]
        for start, _ in _split_copies(mode, ins, lands, outs[0], outs[1]):
            start.start()
        outs[-1][...] = jnp.zeros(outs[-1].shape, F32)

    srcs = [pltpu.with_memory_space_constraint(a, pltpu.HBM) for a in arrs]
    empties = [pltpu.with_memory_space_constraint(lax.empty(s, a.dtype), pltpu.HBM) for s, a in zip(land_shapes, arrs)]
    res = pl.pallas_call(
        body, name=name,
        out_shape=(pltpu.SemaphoreType.DMA((nsem,)), pltpu.SemaphoreType.DMA((nsem,)),
                   *[pltpu.HBM(a.shape, a.dtype) for a in arrs],
                   *[pltpu.HBM(s, a.dtype) for s, a in zip(land_shapes, arrs)],
                   _sds((SUBLANES, LANES), F32)),
        in_specs=[_HBM_SPEC] * (2 * n) + ([_ANY_SPEC] if after is not None else []),
        out_specs=(_SEM_SPEC, _SEM_SPEC, *[_HBM_SPEC] * (2 * n), pl.BlockSpec(memory_space=pltpu.VMEM)),
        input_output_aliases={k: 2 + k for k in range(2 * n)},
        compiler_params=pltpu.CompilerParams(has_side_effects=_EFFECT),
    )(*srcs, *empties, *([after] if after is not None else []))
    return dict(mode=mode, n=n, sems=res[:2], bufs=res[2:2 + 2 * n]), res[-1]


def _split_wait(handle, name, after):
    n, mode = handle["n"], handle["mode"]

    def body(*refs):
        ins, lands = refs[:n], refs[n:2 * n]
        send_sems, recv_sems = refs[2 * n], refs[2 * n + 1]
        for _, arrival in _split_copies(mode, ins, lands, send_sems, recv_sems):
            arrival.wait_send()
            arrival.wait_recv()

    bufs = handle["bufs"]
    res = pl.pallas_call(
        body, name=name,
        out_shape=tuple(pltpu.HBM(b.shape, b.dtype) for b in bufs),
        in_specs=[_HBM_SPEC] * (2 * n) + [_SEM_SPEC, _SEM_SPEC, _ANY_SPEC],
        out_specs=tuple([_HBM_SPEC] * (2 * n)),
        input_output_aliases={k: k for k in range(2 * n)},
        compiler_params=pltpu.CompilerParams(has_side_effects=_EFFECT),
    )(*bufs, *handle["sems"], after)
    return list(res[:n]), list(res[n:])


def _with_own(landed, own):
    chip = 2 * lax.axis_index("x") + lax.axis_index("y")
    return lax.dynamic_update_index_in_dim(landed, own, chip, 0)


def _gathered(handle, name, after):
    sent, landed = _split_wait(handle, name, after)
    return [_with_own(g, own) for g, own in zip(landed, sent)]


def _scattered(handle, name, after):
    chip = 2 * lax.axis_index("x") + lax.axis_index("y")
    sent, landed = _split_wait(handle, name, after)
    return [_with_own(r, lax.dynamic_index_in_dim(g, chip, 0, keepdims=False)) for r, g in zip(landed, sent)]


def _sigmoid(x):
    return 0.5 * jnp.tanh(0.5 * x) + 0.5


def _log_sigmoid(x):
    e = jnp.exp(-jnp.abs(x))
    log1p = jnp.where(e < 1e-2, e * (1.0 - e * (0.5 - e * (1.0 / 3.0))), jnp.log(1.0 + e))
    return jnp.minimum(x, 0.0) - log1p


def _ln_fwd(z):
    mu = jnp.mean(z, axis=-1, keepdims=True)
    zc = z - mu
    var = jnp.mean(zc * zc, axis=-1, keepdims=True)
    rstd = lax.rsqrt(var + LN_EPS)
    return zc * rstd, rstd


def _ln_bwd(dy, xhat, rstd, g):
    dxh = dy * g
    m1 = jnp.mean(dxh, axis=-1, keepdims=True)
    m2 = jnp.mean(dxh * xhat, axis=-1, keepdims=True)
    dz = rstd * (dxh - m1 - xhat * m2)
    return dz, jnp.sum(dy * xhat, axis=0, keepdims=True), jnp.sum(dy, axis=0, keepdims=True)


def _shift_down(z, halo):
    r = lax.broadcasted_iota(jnp.int32, z.shape, 0)
    z1 = jnp.where(r == 0, halo[7:8, :], pltpu.roll(z, 1, 0))
    z2 = jnp.where(r == 0, halo[6:7, :], jnp.where(r == 1, halo[7:8, :], pltpu.roll(z, 2, 0)))
    return z1, z2


def _shift_up(z, halo):
    n = z.shape[0]
    r = lax.broadcasted_iota(jnp.int32, z.shape, 0)
    z1 = jnp.where(r == n - 1, halo[0:1, :], pltpu.roll(z, n - 1, 0))
    z2 = jnp.where(r == n - 1, halo[1:2, :], jnp.where(r == n - 2, halo[0:1, :], pltpu.roll(z, n - 2, 0)))
    return z1, z2


def _accumulate(ref, first, value):
    @pl.when(first)
    def _():
        ref[...] = value

    @pl.when(jnp.logical_not(first))
    def _():
        ref[...] += value


def _proj(x, w, splits, name):
    t, k = x.shape
    tm = min(ROW_TILE, t)

    def body(x_ref, w_ref, *outs):
        a = x_ref[...].astype(BF16)
        for (lo, hi, dt), o in zip(splits, outs):
            o[...] = jnp.dot(a, w_ref[:, lo:hi], preferred_element_type=F32).astype(dt)

    return pl.pallas_call(
        body, grid=(t // tm,),
        in_specs=[pl.BlockSpec((tm, k), lambda i: (i, 0)), _resident(w.shape)],
        out_specs=[pl.BlockSpec((tm, hi - lo), lambda i: (i, 0)) for lo, hi, _ in splits],
        out_shape=[_sds((t, hi - lo), dt) for lo, hi, dt in splits],
        compiler_params=_cp(), name=name)(x, w)


def _fgate_fwd(fl3, b_f):
    nc = fl3.shape[0]

    def body(f_ref, b_ref, c_ref):
        r = lax.broadcasted_iota(jnp.int32, (LANES, LANES), 0)
        cidx = lax.broadcasted_iota(jnp.int32, (LANES, LANES), 1)
        upper = (r <= cidx).astype(F32)

        def step(i, carry):
            lf = _log_sigmoid(f_ref[i] + b_ref[...])
            cc = jnp.dot(lf, upper, precision=HIGHEST, preferred_element_type=F32) + carry
            c_ref[i] = cc
            return cc[:, LANES - 1:LANES]

        lax.fori_loop(0, nc, step, jnp.zeros((FOX_HEADS, 1), F32))

    return pl.pallas_call(body, out_shape=_sds(fl3.shape, F32), name="fgate_fwd")(fl3, b_f)


def _fgate_bwd(dc3, fl3, b_f):
    nc = fl3.shape[0]

    def body(dc_ref, f_ref, b_ref, df_ref, db_ref):
        r = lax.broadcasted_iota(jnp.int32, (LANES, LANES), 0)
        cidx = lax.broadcasted_iota(jnp.int32, (LANES, LANES), 1)
        lower = (r >= cidx).astype(F32)

        def step(n, carry):
            suffix, db = carry
            i = nc - 1 - n
            dlf = jnp.dot(dc_ref[i], lower, precision=HIGHEST, preferred_element_type=F32) + suffix
            df = dlf * (1.0 - _sigmoid(f_ref[i] + b_ref[...]))
            df_ref[i] = df
            return dlf[:, 0:1], db + jnp.sum(df, axis=1, keepdims=True)

        zero = jnp.zeros((FOX_HEADS, 1), F32)
        _, db = lax.fori_loop(0, nc, step, (zero, zero))
        db_ref[...] = db

    return pl.pallas_call(body, out_shape=[_sds(fl3.shape, F32), _sds((FOX_HEADS, 1), F32)],
                          name="fgate_bwd")(dc3, fl3, b_f)


def _split3(c):
    hi = c.astype(BF16).astype(F32)
    mid = (c - hi).astype(BF16).astype(F32)
    lo = (c - hi - mid).astype(BF16).astype(F32)
    return hi, mid, lo


def _lane_pieces(lane, start, pieces, sign):
    out = jnp.zeros(lane.shape, F32)
    for n, p in enumerate(pieces):
        out = jnp.where(lane == start + n, sign * p, out)
    return out


def _attn_pack(qkv, c_col):
    t = qkv.shape[0]
    tm = min(ROW_TILE, t)
    hd = HEAD_DIM

    def body(x_ref, c_ref, qp_ref, kp_ref, vp_ref, kt_ref, vt_ref):
        lane = lax.broadcasted_iota(jnp.int32, (tm, hd), 1) + hd
        for h in range(FOX_HEADS):
            pieces = _split3(c_ref[:, h:h + 1])
            ones = lambda a, b: jnp.where(jnp.logical_and(lane >= a, lane < b), 1.0, 0.0)
            q_extra = _lane_pieces(lane, Q_C, pieces, 1.0) + ones(Q_ONE, Q_ONE + 3)
            k_extra = _lane_pieces(lane, K_C, pieces, -1.0) + ones(K_ONE, K_ONE + 3) + ones(K_ONE2, K_ONE2 + 3)
            qp_ref[h, :, :hd] = (x_ref[:, h * hd:(h + 1) * hd].astype(F32) * (hd ** -0.5)).astype(BF16)
            qp_ref[h, :, hd:] = q_extra.astype(BF16)
            kp_ref[h, :, :hd] = x_ref[:, FOX_WIDTH + h * hd:FOX_WIDTH + (h + 1) * hd]
            kp_ref[h, :, hd:] = k_extra.astype(BF16)
            vp_ref[h, :, :hd] = x_ref[:, 2 * FOX_WIDTH + h * hd:2 * FOX_WIDTH + (h + 1) * hd]
            vp_ref[h, :, hd:] = ones(V_ONE, V_ONE + 4).astype(BF16)
            kt_ref[h] = kp_ref[h].astype(F32).T.astype(BF16)
            vt_ref[h] = vp_ref[h].astype(F32).T.astype(BF16)

    row3 = pl.BlockSpec((FOX_HEADS, tm, LANES), lambda i: (0, i, 0))
    col3 = pl.BlockSpec((FOX_HEADS, LANES, tm), lambda i: (0, 0, i))
    return pl.pallas_call(
        body, grid=(t // tm,),
        in_specs=[pl.BlockSpec((tm, QKV), lambda i: (i, 0)), pl.BlockSpec((tm, FOX_HEADS), lambda i: (i, 0))],
        out_specs=[row3, row3, row3, col3, col3],
        out_shape=[_sds((FOX_HEADS, t, LANES), BF16)] * 3 + [_sds((FOX_HEADS, LANES, t), BF16)] * 2,
        compiler_params=_cp(), name="attn_pack")(qkv, c_col)


def _triangle(nq, key_major):
    if key_major:
        pairs = [(i, j) for j in range(nq) for i in range(j, nq)]
    else:
        pairs = [(i, j) for i in range(nq) for j in range(i + 1)]
    return jnp.asarray([p[0] for p in pairs], jnp.int32), jnp.asarray([p[1] for p in pairs], jnp.int32)


def _attn_fwd(qp, kp, vt):
    t = qp.shape[1]
    bq = min(ATT_BLOCK, t)
    nq = t // bq
    nh = ATT_FWD_HEADS
    i_tab, j_tab = _triangle(nq, key_major=False)

    def body(it_ref, jt_ref, q_ref, k_ref, vt_ref, o_ref, lse_ref, m_sc, acc_sc):
        s = pl.program_id(1)
        i, j = it_ref[s], jt_ref[s]

        @pl.when(j == 0)
        def _():
            m_sc[...] = jnp.full(m_sc.shape, NEG, F32)
            acc_sc[...] = jnp.zeros(acc_sc.shape, F32)

        def sweep(masked):
            scores = lambda h: lax.dot_general(k_ref[h], q_ref[h], NT, preferred_element_type=F32)

            def accumulate(h, pt, rescale):
                acc_sc[h] = rescale * acc_sc[h] + jnp.dot(vt_ref[h], pt, preferred_element_type=F32)

            ahead, behind = scores(0), None
            for h in range(nh):
                st = ahead
                if h + 1 < nh:
                    ahead = scores(h + 1)
                if behind is not None:
                    accumulate(*behind)
                if masked:
                    key = lax.broadcasted_iota(jnp.int32, (bq, bq), 0)
                    qry = lax.broadcasted_iota(jnp.int32, (bq, bq), 1)
                    st = jnp.where(key <= qry, st, NEG)
                m_prev = m_sc[h]
                m_new = jnp.maximum(m_prev, jnp.max(st, axis=0, keepdims=True))
                behind = (h, jnp.exp(st - m_new).astype(BF16), jnp.exp(m_prev - m_new))
                m_sc[h] = m_new
            accumulate(*behind)

        @pl.when(j < i)
        def _():
            sweep(False)

        @pl.when(j == i)
        def _():
            sweep(True)
            for h in range(nh):
                acc = acc_sc[h]
                denom = acc[V_ONE:V_ONE + 1, :]
                o_ref[:, h * HEAD_DIM:(h + 1) * HEAD_DIM] = (acc[:HEAD_DIM, :] / denom).T.astype(BF16)
                lse_ref[h] = m_sc[h] + jnp.log(denom)

    grid_spec = pltpu.PrefetchScalarGridSpec(
        num_scalar_prefetch=2, grid=(FOX_HEADS // nh, i_tab.shape[0]),
        in_specs=[pl.BlockSpec((nh, bq, LANES), lambda hp, s, it, jt: (hp, it[s], 0)),
                  pl.BlockSpec((nh, bq, LANES), lambda hp, s, it, jt: (hp, jt[s], 0)),
                  pl.BlockSpec((nh, LANES, bq), lambda hp, s, it, jt: (hp, 0, jt[s]))],
        out_specs=[pl.BlockSpec((bq, nh * HEAD_DIM), lambda hp, s, it, jt: (it[s], hp)),
                   pl.BlockSpec((nh, 1, bq), lambda hp, s, it, jt: (hp, 0, it[s]))],
        scratch_shapes=[pltpu.VMEM((nh, 1, bq), F32), pltpu.VMEM((nh, LANES, bq), F32)])
    return pl.pallas_call(body, grid_spec=grid_spec,
                          out_shape=[_sds((t, FOX_WIDTH), BF16), _sds((FOX_HEADS, 1, t), F32)],
                          compiler_params=_cp(), name="attn_fwd")(i_tab, j_tab, qp, kp, vt)


def _conv_fwd(bch, conv_w):
    t = bch.shape[0]
    tm = min(ROW_TILE, t)
    halo_blocks = tm // SUBLANES
    cw = CONV_WIDTH

    def body(cur_ref, prev_ref, w_ref, o_ref):
        i = pl.program_id(0)
        z = cur_ref[:, cw:2 * cw] * cur_ref[:, 2 * cw:]
        zp = jnp.where(i == 0, 0.0, prev_ref[:, cw:2 * cw] * prev_ref[:, 2 * cw:])
        z1, z2 = _shift_down(z, zp)
        y = w_ref[0:1, :] * z2 + w_ref[1:2, :] * z1 + w_ref[2:3, :] * z
        o_ref[...] = (cur_ref[:, :cw] * y).astype(BF16)

    return pl.pallas_call(
        body, grid=(t // tm,),
        in_specs=[pl.BlockSpec((tm, BCH), lambda i: (i, 0)),
                  pl.BlockSpec((SUBLANES, BCH), lambda i: (jnp.maximum(i * halo_blocks - 1, 0), 0)),
                  _resident(conv_w.shape)],
        out_specs=pl.BlockSpec((tm, cw), lambda i: (i, 0)),
        out_shape=_sds((t, cw), BF16), compiler_params=_cp(), name="conv_fwd")(bch, bch, conv_w)


def _mm_res_ln(pairs, res, g, b, name):
    from_ln = isinstance(res, tuple)
    res_args = list(res) if from_ln else [res]
    t, d = res_args[0].shape
    tm = min(ROW_TILE, t)
    n = len(pairs)

    def body(*refs):
        a_refs, w_refs = refs[:n], refs[n:2 * n]
        res_refs = refs[2 * n:2 * n + len(res_args)]
        g_ref, b_ref, yb_ref, xh_ref, rs_ref = refs[2 * n + len(res_args):]
        r = res_refs[0][...]
        if from_ln:
            r = r * res_refs[1][...] + res_refs[2][...]
        z = ALPHA * r
        for a_ref, w_ref in zip(a_refs, w_refs):
            z = z + jnp.dot(a_ref[...].astype(BF16), w_ref[...], preferred_element_type=F32)
        xhat, rstd = _ln_fwd(z)
        yb_ref[...] = (xhat * g_ref[...] + b_ref[...]).astype(BF16)
        xh_ref[...] = xhat
        rs_ref[...] = rstd

    row = lambda i: (i, 0)
    full = pl.BlockSpec((tm, d), row)
    return pl.pallas_call(
        body, grid=(t // tm,),
        in_specs=[pl.BlockSpec((tm, a.shape[1]), row) for a, _ in pairs] + [_resident(w.shape) for _, w in pairs]
        + [full] + [_resident(a.shape) for a in res_args[1:]] + [_resident(g.shape), _resident(b.shape)],
        out_specs=[full, full, pl.BlockSpec((tm, 1), row)],
        out_shape=[_sds((t, d), BF16), _sds((t, d), F32), _sds((t, 1), F32)],
        compiler_params=_cp(), name=name)(*[a for a, _ in pairs], *[w for _, w in pairs], *res_args, g, b)


def _ffn_in(x, wi, name):
    t, d = x.shape
    tm = min(FFN_ROW_TILE, t)
    hh = HALF_HIDDEN

    def body(x_ref, w_ref, gu_ref, h_ref):
        a = x_ref[...].astype(BF16)
        for c in range(2):
            gs, us = slice(c * hh, (c + 1) * hh), slice(FFN_HIDDEN + c * hh, FFN_HIDDEN + (c + 1) * hh)
            g = jnp.dot(a, w_ref[c], preferred_element_type=F32)
            u = jnp.dot(a, w_ref[2 + c], preferred_element_type=F32)
            sig = _sigmoid(g)
            silu = g * sig
            gu_ref[:, gs] = (u * sig * (1.0 + g * (1.0 - sig))).astype(BF16)
            gu_ref[:, us] = silu.astype(BF16)
            h_ref[:, gs] = (silu * u).astype(BF16)

    row = lambda i: (i, 0)
    return pl.pallas_call(
        body, grid=(t // tm,),
        in_specs=[pl.BlockSpec((tm, d), row), _resident(wi.shape)],
        out_specs=[pl.BlockSpec((tm, 2 * FFN_HIDDEN), row), pl.BlockSpec((tm, FFN_HIDDEN), row)],
        out_shape=[_sds((t, 2 * FFN_HIDDEN), BF16), _sds((t, FFN_HIDDEN), BF16)],
        compiler_params=_cp(), name=name)(x, wi)


def _gmlp_fwd(x, w_in, vg, vb, wm, bs_col):
    t, d = x.shape
    tm = min(ROW_TILE, t)
    gb = GMLP_BLOCK

    def body(x_ref, w_ref, vg_ref, vb_ref, wm_ref, bs_ref, sv_ref, rs_ref, o_ref, a_sc):
        xb = x_ref[...].astype(BF16)
        nc = w_ref.shape[2]
        for j in range(w_ref.shape[0]):
            a_sc[:, j * nc:(j + 1) * nc] = jnp.dot(xb, w_ref[j], preferred_element_type=F32)
        halves = []
        for half in range(2):
            a = a_sc[:, half * d:(half + 1) * d]
            cdf = 0.5 * (1.0 + lax.erf(a * (2.0 ** -0.5)))
            halves.append(a * cdf)
            slope = cdf + a * (jnp.exp(-0.5 * a * a) * (1.0 / math.sqrt(2.0 * math.pi)))
            sv_ref[:, (2 * half + 1) * d:(2 * half + 2) * d] = slope.astype(BF16)
        u = halves[0]
        vhat, rstd = _ln_fwd(halves[1])
        sv_ref[:, :d] = u.astype(BF16)
        sv_ref[:, 2 * d:3 * d] = vhat.astype(BF16)
        rs_ref[...] = rstd
        vln = (vhat * vg_ref[...] + vb_ref[...]).astype(BF16)
        for blk in range(tm // gb):
            rs = slice(blk * gb, (blk + 1) * gb)
            for gi in range(GMLP_GROUPS):
                cs = slice(gi * gb, (gi + 1) * gb)
                s = jnp.dot(wm_ref[gi], vln[rs, cs], preferred_element_type=F32) + bs_ref[:, gi:gi + 1]
                o_ref[rs, cs] = (u[rs, cs] * s).astype(BF16)

    row = lambda i: (i, 0)
    return pl.pallas_call(
        body, grid=(t // tm,),
        in_specs=[pl.BlockSpec((tm, d), row), _resident(w_in.shape), _resident(vg.shape), _resident(vb.shape),
                  _resident(wm.shape), _resident(bs_col.shape)],
        out_specs=[pl.BlockSpec((tm, 4 * d), row), pl.BlockSpec((tm, 1), row), pl.BlockSpec((tm, d), row)],
        out_shape=[_sds((t, 4 * d), BF16), _sds((t, 1), F32), _sds((t, d), BF16)],
        scratch_shapes=[pltpu.VMEM((tm, 2 * d), F32)],
        compiler_params=_cp(), name="gmlp_fwd")(x, w_in, vg, vb, wm, bs_col)


def _loss_ln_bwd(xhat, rstd, g, b, target):
    t, d = xhat.shape
    tm = min(ROW_TILE, t)

    def body(xh_ref, rs_ref, g_ref, b_ref, t_ref, sq_ref, dz_ref, dg_ref, db_ref):
        first = pl.program_id(0) == 0
        xh = xh_ref[...]
        err = xh * g_ref[...] + b_ref[...] - t_ref[...]
        dz, dg, db = _ln_bwd(err * (1.0 / d), xh, rs_ref[...], g_ref[...])
        dz_ref[...] = dz
        _accumulate(sq_ref, first, jnp.sum(err * err, axis=0, keepdims=True))
        _accumulate(dg_ref, first, dg)
        _accumulate(db_ref, first, db)

    row = lambda i: (i, 0)
    vec = pl.BlockSpec((1, d), lambda i: (0, 0))
    return pl.pallas_call(
        body, grid=(t // tm,),
        in_specs=[pl.BlockSpec((tm, d), row), pl.BlockSpec((tm, 1), row), _resident(g.shape), _resident(b.shape),
                  pl.BlockSpec((tm, d), row)],
        out_specs=[vec, pl.BlockSpec((tm, d), row), vec, vec],
        out_shape=[_sds((1, d), F32), _sds((t, d), F32), _sds((1, d), F32), _sds((1, d), F32)],
        compiler_params=_cp(), name="loss_ln_bwd")(xhat, rstd, g, b, target)


def _mm_nt(pairs, ws, name, *, tm=ROW_TILE, res=None, ln=None, out_dtype=F32, after=None):
    t = pairs[0][0].shape[0]
    k = ws[0].shape[-2]
    tm = min(tm, t)
    n, nw = len(pairs), len(ws)

    def body(*refs):
        refs = refs[after is not None:]
        a_refs, w_refs = refs[:n], refs[n:n + nw]
        rest = list(refs[n + nw:])
        dx = None
        for a_ref, (_, wi, lo, hi) in zip(a_refs, pairs):
            w_ref = w_refs[wi]
            if len(w_ref.shape) == 3:
                nc = w_ref.shape[2]
                parts = [lax.dot_general(a_ref[:, j * nc:(j + 1) * nc].astype(BF16), w_ref[j], NT,
                                         preferred_element_type=F32) for j in range(w_ref.shape[0])]
            else:
                parts = [lax.dot_general(a_ref[...].astype(BF16), w_ref[:, lo:hi], NT, preferred_element_type=F32)]
            for part in parts:
                dx = part if dx is None else dx + part
        if res is not None:
            dx = dx + ALPHA * rest.pop(0)[...]
        if ln is None:
            rest[0][...] = dx.astype(out_dtype)
            return
        xh_ref, rs_ref, g_ref, dz_ref, dg_ref, db_ref = rest
        first = pl.program_id(0) == 0
        dz, dg, db = _ln_bwd(dx, xh_ref[...], rs_ref[...], g_ref[...])
        dz_ref[...] = dz
        _accumulate(dg_ref, first, dg)
        _accumulate(db_ref, first, db)

    row = lambda i: (i, 0)
    in_specs = [pl.BlockSpec((tm, a.shape[1]), row) for a, _, _, _ in pairs] + [_resident(w.shape) for w in ws]
    args = [a for a, _, _, _ in pairs] + list(ws)
    if res is not None:
        in_specs.append(pl.BlockSpec((tm, k), row))
        args.append(res)
    if ln is None:
        out_specs = pl.BlockSpec((tm, k), row)
        out_shape = _sds((t, k), out_dtype)
    else:
        xhat, rstd, g = ln
        in_specs += [pl.BlockSpec((tm, k), row), pl.BlockSpec((tm, 1), row), _resident(g.shape)]
        args += [xhat, rstd, g]
        vec = pl.BlockSpec((1, k), lambda i: (0, 0))
        out_specs = [pl.BlockSpec((tm, k), row), vec, vec]
        out_shape = [_sds((t, k), F32), _sds((1, k), F32), _sds((1, k), F32)]
    if after is not None:
        in_specs.insert(0, _ANY_SPEC)
        args.insert(0, after)
    return pl.pallas_call(body, grid=(t // tm,), in_specs=in_specs, out_specs=out_specs, out_shape=out_shape,
                          compiler_params=_cp(), name=name)(*args)


def _mm_tn(a, b, name, *, tn, tk=None, tt=None, stack_cols=False, out_dtype=BF16, after=None):
    t, k = a.shape
    n = b.shape[1]
    tk = k if tk is None else tk
    tt = min(REDUCE_TILE if tt is None else tt, t)
    nt = t // tt

    def body(a_ref, b_ref, *rest):
        o_ref, acc_ref = rest[after is not None:]
        s = pl.program_id(2)
        part = lax.dot_general(a_ref[...].astype(BF16), b_ref[...].astype(BF16), TN, preferred_element_type=F32)
        _accumulate(acc_ref, s == 0, part)

        @pl.when(s == nt - 1)
        def _():
            o_ref[...] = acc_ref[...].astype(out_dtype).reshape(o_ref.shape)

    if stack_cols:
        assert tk == k
        out_spec = pl.BlockSpec((1, k, tn), lambda kk, j, s: (j, 0, 0))
        out_shape = _sds((n // tn, k, tn), out_dtype)
    else:
        out_spec = pl.BlockSpec((tk, tn), lambda kk, j, s: (kk, j))
        out_shape = _sds((k, n), out_dtype)
    return pl.pallas_call(
        body, grid=(k // tk, n // tn, nt),
        in_specs=[pl.BlockSpec((tt, tk), lambda kk, j, s: (s, kk)), pl.BlockSpec((tt, tn), lambda kk, j, s: (s, j))]
        + ([_ANY_SPEC] if after is not None else []),
        out_specs=out_spec, out_shape=out_shape,
        scratch_shapes=[pltpu.VMEM((tk, tn), F32)],
        compiler_params=_cp(), name=name)(a, b, *([after] if after is not None else []))


def _ffn_bwd_hidden(dz, wo, gu, name):
    t, d = dz.shape
    tm = min(FFN_ROW_TILE, t)
    hh = HALF_HIDDEN

    def body(dz_ref, w_ref, gu_ref, o_ref):
        a = dz_ref[...].astype(BF16)
        for c in range(2):
            gs, us = slice(c * hh, (c + 1) * hh), slice(FFN_HIDDEN + c * hh, FFN_HIDDEN + (c + 1) * hh)
            dh = lax.dot_general(a, w_ref[gs, :], NT, preferred_element_type=F32)
            o_ref[:, gs] = (dh * gu_ref[:, gs].astype(F32)).astype(BF16)
            o_ref[:, us] = (dh * gu_ref[:, us].astype(F32)).astype(BF16)

    row = lambda i: (i, 0)
    return pl.pallas_call(
        body, grid=(t // tm,),
        in_specs=[pl.BlockSpec((tm, d), row), _resident(wo.shape), pl.BlockSpec((tm, 2 * FFN_HIDDEN), row)],
        out_specs=pl.BlockSpec((tm, 2 * FFN_HIDDEN), row),
        out_shape=_sds((t, 2 * FFN_HIDDEN), BF16), compiler_params=_cp(), name=name)(dz, wo, gu)


def _gmlp_bwd(dgated, saved, rstd_v, vg, vb, wm, bs_col):
    t, d = dgated.shape
    d2 = 2 * d
    tm = min(ROW_TILE, t)
    gb = GMLP_BLOCK

    def body(dg_ref, sv_ref, rs_ref, vg_ref, vb_ref, wm_ref, bs_ref, da_ref, dws_ref, dbs_ref, dvg_ref, dvb_ref, dvln_sc):
        first = pl.program_id(0) == 0
        u = sv_ref[:, :d].astype(F32)
        vhat = sv_ref[:, 2 * d:3 * d].astype(F32)
        rstd = rs_ref[...]
        vln = (vhat * vg_ref[...] + vb_ref[...]).astype(BF16)
        dgate = dg_ref[...]

        @pl.when(first)
        def _():
            dws_ref[...] = jnp.zeros(dws_ref.shape, F32)
            dbs_ref[...] = jnp.zeros(dbs_ref.shape, F32)

        for blk in range(tm // gb):
            rs = slice(blk * gb, (blk + 1) * gb)
            for gi in range(GMLP_GROUPS):
                cs = slice(gi * gb, (gi + 1) * gb)
                vblk = vln[rs, cs]
                s = jnp.dot(wm_ref[gi], vblk, preferred_element_type=F32) + bs_ref[:, gi:gi + 1]
                dgb = dgate[rs, cs]
                da_ref[rs, cs] = (dgb * s * sv_ref[rs, d + gi * gb:d + (gi + 1) * gb].astype(F32)).astype(BF16)
                ds = dgb * u[rs, cs]
                dsb = ds.astype(BF16)
                dws_ref[gi] += lax.dot_general(dsb, vblk, NT, preferred_element_type=F32)
                dbs_ref[:, gi:gi + 1] += jnp.sum(ds, axis=1, keepdims=True)
                dvln_sc[rs, cs] = lax.dot_general(wm_ref[gi], dsb, TN, preferred_element_type=F32)
        dv, dvg, dvb = _ln_bwd(dvln_sc[...], vhat, rstd, vg_ref[...])
        da_ref[:, d:] = (dv * sv_ref[:, 3 * d:].astype(F32)).astype(BF16)
        _accumulate(dvg_ref, first, dvg)
        _accumulate(dvb_ref, first, dvb)

    row = lambda i: (i, 0)
    vec = pl.BlockSpec((1, d), lambda i: (0, 0))
    return pl.pallas_call(
        body, grid=(t // tm,),
        in_specs=[pl.BlockSpec((tm, d), row), pl.BlockSpec((tm, 4 * d), row), pl.BlockSpec((tm, 1), row),
                  _resident(vg.shape), _resident(vb.shape), _resident(wm.shape), _resident(bs_col.shape)],
        out_specs=[pl.BlockSpec((tm, d2), row), pl.BlockSpec(wm.shape, lambda i: (0, 0, 0)),
                   pl.BlockSpec(bs_col.shape, lambda i: (0, 0)), vec, vec],
        out_shape=[_sds((t, d2), BF16), _sds(wm.shape, F32), _sds(bs_col.shape, F32), _sds((1, d), F32), _sds((1, d), F32)],
        scratch_shapes=[pltpu.VMEM((tm, d), F32)],
        compiler_params=_cp(), name="gmlp_bwd")(dgated, saved, rstd_v, vg, vb, wm, bs_col)


def _conv_bwd(bch, dmix, conv_w):
    t = bch.shape[0]
    tm = min(ROW_TILE, t)
    nb = t // tm
    halo_blocks = tm // SUBLANES
    cw = CONV_WIDTH

    def body(cur_ref, prev_ref, next_ref, dc_ref, dn_ref, w_ref, o_ref, dw_ref):
        i = pl.program_id(0)
        bgate, cgate, hval = cur_ref[:, :cw], cur_ref[:, cw:2 * cw], cur_ref[:, 2 * cw:]
        z = cgate * hval
        zp = jnp.where(i == 0, 0.0, prev_ref[:, cw:2 * cw] * prev_ref[:, 2 * cw:])
        z1, z2 = _shift_down(z, zp)
        w0, w1, w2 = w_ref[0:1, :], w_ref[1:2, :], w_ref[2:3, :]
        dconv = dc_ref[...]
        o_ref[:, :cw] = (dconv * (w0 * z2 + w1 * z1 + w2 * z)).astype(BF16)
        dy = dconv * bgate
        dyn = jnp.where(i == nb - 1, 0.0, dn_ref[...] * next_ref[:, :cw])
        dy1, dy2 = _shift_up(dy, dyn)
        dz = w2 * dy + w1 * dy1 + w0 * dy2
        o_ref[:, cw:2 * cw] = (dz * hval).astype(BF16)
        o_ref[:, 2 * cw:] = (dz * cgate).astype(BF16)

        @pl.when(i == 0)
        def _():
            dw_ref[...] = jnp.zeros(dw_ref.shape, F32)

        for tap, zs in enumerate((z2, z1, z)):
            dw_ref[tap:tap + 1, :] += jnp.sum(dy * zs, axis=0, keepdims=True)

    last_halo = t // SUBLANES - 1
    return pl.pallas_call(
        body, grid=(nb,),
        in_specs=[pl.BlockSpec((tm, BCH), lambda i: (i, 0)),
                  pl.BlockSpec((SUBLANES, BCH), lambda i: (jnp.maximum(i * halo_blocks - 1, 0), 0)),
                  pl.BlockSpec((SUBLANES, BCH), lambda i: (jnp.minimum((i + 1) * halo_blocks, last_halo), 0)),
                  pl.BlockSpec((tm, cw), lambda i: (i, 1)),
                  pl.BlockSpec((SUBLANES, cw), lambda i: (jnp.minimum((i + 1) * halo_blocks, last_halo), 1)),
                  _resident(conv_w.shape)],
        out_specs=[pl.BlockSpec((tm, BCH), lambda i: (i, 0)), pl.BlockSpec((SUBLANES, cw), lambda i: (0, 0))],
        out_shape=[_sds((t, BCH), BF16), _sds((SUBLANES, cw), F32)],
        compiler_params=_cp(), name="conv_bwd")(bch, bch, bch, dmix, dmix, conv_w)


def _attn_bwd_prep(o, dmix, qp, lse_col):
    t = o.shape[0]
    tm = min(ROW_TILE, t)
    hd = HEAD_DIM

    def body(o_ref, do_ref, qp_ref, lse_ref, qb_ref, dob_ref):
        lane = lax.broadcasted_iota(jnp.int32, (tm, hd), 1) + hd
        for h in range(FOX_HEADS):
            do = do_ref[:, h * hd:(h + 1) * hd]
            delta = jnp.sum(o_ref[:, h * hd:(h + 1) * hd].astype(F32) * do, axis=-1, keepdims=True)
            dob_ref[h, :, :hd] = do.astype(BF16)
            dob_ref[h, :, hd:] = _lane_pieces(lane, DO_DELTA, _split3(delta), -1.0).astype(BF16)
            qb_ref[h, :, :hd] = qp_ref[h, :, :hd]
            qb_ref[h, :, hd:] = (qp_ref[h, :, hd:].astype(F32)
                                 + _lane_pieces(lane, Q_LSE, _split3(lse_ref[:, h:h + 1]), -1.0)).astype(BF16)

    row3 = pl.BlockSpec((FOX_HEADS, tm, LANES), lambda i: (0, i, 0))
    return pl.pallas_call(
        body, grid=(t // tm,),
        in_specs=[pl.BlockSpec((tm, FOX_WIDTH), lambda i: (i, 0)), pl.BlockSpec((tm, FOX_WIDTH), lambda i: (i, 0)), row3,
                  pl.BlockSpec((tm, FOX_HEADS), lambda i: (i, 0))],
        out_specs=[row3, row3], out_shape=[_sds((FOX_HEADS, t, LANES), BF16)] * 2,
        compiler_params=_cp(), name="attn_bwd_prep")(o, dmix, qp, lse_col)


def _attn_bwd(qb, kp, vp, dob, kt):
    t = qb.shape[1]
    bq = min(ATT_BLOCK, t)
    nq = t // bq
    i_tab, j_tab = _triangle(nq, key_major=True)

    def body(it_ref, jt_ref, q_ref, k_ref, v_ref, do_ref, kt_ref, dqt_ref, dk_ref, dv_ref, dk_sc, dv_sc):
        s = pl.program_id(1)
        i, j = it_ref[s], jt_ref[s]

        @pl.when(s == 0)
        def _():
            dqt_ref[...] = jnp.zeros(dqt_ref.shape, F32)

        @pl.when(i == j)
        def _():
            dk_sc[...] = jnp.zeros(dk_sc.shape, F32)
            dv_sc[...] = jnp.zeros(dv_sc.shape, F32)

        cols = pl.ds(pl.multiple_of(i * bq, bq), bq)

        def sweep(masked):
            def scores(h):
                return (lax.dot_general(k_ref[h], q_ref[h], NT, preferred_element_type=F32),
                        lax.dot_general(v_ref[h], do_ref[h], NT, preferred_element_type=F32))

            def accumulate(h, ptb, dstb):
                dv_sc[h] += jnp.dot(ptb, do_ref[h], preferred_element_type=F32)
                dk_sc[h] += jnp.dot(dstb, q_ref[h], preferred_element_type=F32)
                dqt_ref[h, :, cols] += jnp.dot(kt_ref[h], dstb, preferred_element_type=F32)

            ahead, behind = scores(0), None
            for h in range(ATT_BWD_HEADS):
                st, dpt = ahead
                if h + 1 < ATT_BWD_HEADS:
                    ahead = scores(h + 1)
                if behind is not None:
                    accumulate(*behind)
                if masked:
                    key = lax.broadcasted_iota(jnp.int32, (bq, bq), 0)
                    qry = lax.broadcasted_iota(jnp.int32, (bq, bq), 1)
                    st = jnp.where(key <= qry, st, NEG)
                pt = jnp.exp(st)
                behind = (h, pt.astype(BF16), (pt * dpt).astype(BF16))
            accumulate(*behind)

        @pl.when(i == j)
        def _():
            sweep(True)

        @pl.when(i > j)
        def _():
            sweep(False)

        @pl.when(i == nq - 1)
        def _():
            dk_ref[...] = dk_sc[...]
            dv_ref[...] = dv_sc[...].astype(BF16)

    nh = ATT_BWD_HEADS
    qblk = pl.BlockSpec((nh, bq, LANES), lambda hp, s, it, jt: (hp, it[s], 0))
    kblk = pl.BlockSpec((nh, bq, LANES), lambda hp, s, it, jt: (hp, jt[s], 0))
    grid_spec = pltpu.PrefetchScalarGridSpec(
        num_scalar_prefetch=2, grid=(FOX_HEADS // nh, i_tab.shape[0]),
        in_specs=[qblk, kblk, kblk, qblk, pl.BlockSpec((nh, LANES, bq), lambda hp, s, it, jt: (hp, 0, jt[s]))],
        out_specs=[pl.BlockSpec((nh, LANES, t), lambda hp, s, it, jt: (hp, 0, 0), pipeline_mode=pl.Buffered(1)),
                   kblk, kblk],
        scratch_shapes=[pltpu.VMEM((nh, bq, LANES), F32), pltpu.VMEM((nh, bq, LANES), F32)])
    return pl.pallas_call(body, grid_spec=grid_spec,
                          out_shape=[_sds((FOX_HEADS, LANES, t), F32), _sds((FOX_HEADS, t, LANES), F32),
                                     _sds((FOX_HEADS, t, LANES), BF16)],
                          compiler_params=_cp(), name="attn_bwd")(i_tab, j_tab, qb, kp, vp, dob, kt)


def _attn_unpack(dqt, dkp, dvp):
    t = dkp.shape[1]
    tm = min(ROW_TILE, t)
    hd = HEAD_DIM

    def body(dqt_ref, dk_ref, dv_ref, o_ref, dc_ref):
        for h in range(FOX_HEADS):
            dq = dqt_ref[h].T
            o_ref[:, h * hd:(h + 1) * hd] = (dq[:, :hd] * (hd ** -0.5)).astype(BF16)
            o_ref[:, FOX_WIDTH + h * hd:FOX_WIDTH + (h + 1) * hd] = dk_ref[h, :, :hd].astype(BF16)
            o_ref[:, 2 * FOX_WIDTH + h * hd:2 * FOX_WIDTH + (h + 1) * hd] = dv_ref[h, :, :hd]
            dc_ref[:, h:h + 1] = dq[:, K_ONE:K_ONE + 1] - dk_ref[h, :, Q_ONE:Q_ONE + 1]

    row3 = pl.BlockSpec((FOX_HEADS, tm, LANES), lambda i: (0, i, 0))
    return pl.pallas_call(
        body, grid=(t // tm,),
        in_specs=[pl.BlockSpec((FOX_HEADS, LANES, tm), lambda i: (0, 0, i)), row3, row3],
        out_specs=[pl.BlockSpec((tm, QKV), lambda i: (i, 0)), pl.BlockSpec((tm, FOX_HEADS), lambda i: (i, 0))],
        out_shape=[_sds((t, QKV), BF16), _sds((t, FOX_HEADS), F32)],
        compiler_params=_cp(), name="attn_unpack")(dqt, dkp, dvp)


def _adamw(parts, w, m, v, name, layer=None, into=None):
    nl, r, c = w.shape
    fits = [cand for cand in [*range(SUBLANES, r, SUBLANES), r] if r % cand == 0 and cand * c * 4 <= ADAMW_BLOCK_BYTES]
    tr = max(fits) if fits else r
    npart = len(parts)
    bc1 = 1.0 - ADAM_B1 ** ADAM_STEP
    bc2 = 1.0 - ADAM_B2 ** ADAM_STEP

    def body(*refs):
        p_refs = refs[:npart]
        w_ref, m_ref, v_ref = refs[npart:npart + 3]
        g_ref, d_ref, nm_ref, nv_ref = refs[-4:]
        sums = []
        for p_ref in p_refs:
            acc = p_ref[0, 0].astype(F32)
            for s in range(1, p_ref.shape[0]):
                acc = acc + p_ref[s, 0].astype(F32)
            sums.append(acc)
        g = sums[0]
        for extra in sums[1:]:
            g = g + extra
        nm = ADAM_B1 * m_ref[0] + (1.0 - ADAM_B1) * g
        nv = ADAM_B2 * v_ref[0] + (1.0 - ADAM_B2) * (g * g)
        m_hat = nm / bc1
        v_hat = nv / bc2
        g_ref[0] = g
        d_ref[0] = -ADAM_LR * (m_hat / (jnp.sqrt(v_hat) + ADAM_EPS) + ADAM_WD * w_ref[0])
        nm_ref[0] = nm
        nv_ref[0] = nv

    first = 0 if layer is None else layer
    blk = pl.BlockSpec((1, tr, c), lambda l, i: (first + l, i, 0))
    extra = [] if into is None else list(into)
    return pl.pallas_call(
        body, grid=(nl if layer is None else 1, r // tr),
        in_specs=[pl.BlockSpec((p.shape[0], 1, tr, c), lambda l, i: (0, l, i, 0)) for p in parts] + [blk, blk, blk]
        + [_ANY_SPEC] * len(extra),
        out_specs=[blk] * 4, out_shape=[_sds(w.shape, F32)] * 4,
        input_output_aliases={npart + 3 + k: k for k in range(len(extra))},
        compiler_params=_cp(), name=name)(*parts, w, m, v, *extra)


def _to_rows(a):
    flat = a.reshape(-1)
    pad = (-flat.shape[0]) % LANES
    if pad:
        flat = jnp.concatenate([flat, jnp.zeros((pad,), flat.dtype)])
    return flat.reshape(-1, LANES)


def _by_owner_cols(dw):
    k, n = dw.shape
    return dw.reshape(k, N_CHIPS, n // N_CHIPS).transpose(1, 0, 2)[:, None]


def _ffn_fwd(xin_ln, xin_b, wi, wo, g, b, layer):
    gu, h = _ffn_in(xin_b, wi, f"ffn_in_{layer}")
    y_b, xhat, rstd = _mm_res_ln([(h, wo)], xin_ln, g, b, f"ffn_out_ln_{layer}")
    return y_b, (xin_b, gu, h, xhat, rstd)


def _ffn_bwd(dz, saved, wi, wo, ln_below, layer):
    xin_b, gu, h, _, _ = saved
    dgu = _ffn_bwd_hidden(dz, wo, gu, f"ffn_bwd_hidden_{layer}")
    g_out = _mm_tn(h, dz, f"ffn_dw_out_{layer}", tn=D_MODEL, tk=HALF_HIDDEN, tt=REDUCE_TILE // 2)
    g_in = _mm_tn(xin_b, dgu, f"ffn_dw_in_{layer}", tn=HALF_HIDDEN, stack_cols=True)
    below = _mm_nt([(dgu, 0, 0, 0)], [wi], f"ffn_dx_{layer}", tm=FFN_ROW_TILE, res=dz, ln=ln_below)
    return below, g_in, g_out.reshape(N_CHIPS, FFN_HIDDEN // N_CHIPS, D_MODEL)


def kernel(x, even_w_in, even_b_f, even_conv_w, even_w_out, odd_w_in, odd_v_ln_g, odd_v_ln_b, odd_w_s, odd_b_s, odd_w_out, mix_ln_g, mix_ln_b, ffn_w_in, ffn_w_out, ffn_ln_g, ffn_ln_b, loss_target, m_even_w_in, m_even_b_f, m_even_conv_w, m_even_w_out, m_odd_w_in, m_odd_v_ln_g, m_odd_v_ln_b, m_odd_w_s, m_odd_b_s, m_odd_w_out, m_mix_ln_g, m_mix_ln_b, m_ffn_w_in, m_ffn_w_out, m_ffn_ln_g, m_ffn_ln_b, v_even_w_in, v_even_b_f, v_even_conv_w, v_even_w_out, v_odd_w_in, v_odd_v_ln_g, v_odd_v_ln_b, v_odd_w_s, v_odd_b_s, v_odd_w_out, v_mix_ln_g, v_mix_ln_b, v_ffn_w_in, v_ffn_w_out, v_ffn_ln_g, v_ffn_ln_b):
    t = x.shape[1]
    d = D_MODEL
    chip = 2 * lax.axis_index("x") + lax.axis_index("y")
    x2d = x[0]
    target = loss_target[0]

    small_shard = jnp.concatenate([odd_v_ln_g.reshape(2, LANES), odd_v_ln_b.reshape(2, LANES),
                                   even_conv_w.reshape(CONV_K, LANES), jnp.zeros((1, LANES), F32)], axis=0)
    first = [even_w_in[0].astype(BF16), even_w_out[0].astype(BF16), small_shard]
    later = [odd_w_in[0].astype(BF16), odd_w_out[0].astype(BF16), ffn_w_in[0].astype(BF16), ffn_w_in[1].astype(BF16),
             ffn_w_out[0].astype(BF16), ffn_w_out[1].astype(BF16)]
    first_h, first_tok = _split_start(first, "gather4", "gather_first_start")
    later_h, later_tok = _split_start(later, "gather4", "gather_later_start", after=first_tok)
    g_ewi, g_ewo, g_small = _gathered(first_h, "gather_first_wait", later_tok)
    ewi = g_ewi.transpose(1, 0, 2).reshape(d, EVEN_IN)
    w_even_in = jnp.concatenate([ewi[:, :QKV], ewi[:, QKV + FOX_HEADS:], ewi[:, QKV:QKV + FOX_HEADS],
                                 jnp.zeros((d, LANES - FOX_HEADS), BF16)], axis=1)
    w_even_out = g_ewo.reshape(d, d)
    v_ln_g = g_small[:, 0:2].reshape(1, d)
    v_ln_b = g_small[:, 2:4].reshape(1, d)
    conv_w = g_small[:, 4:7].transpose(1, 0, 2).reshape(CONV_K, CONV_WIDTH)
    chunk_id = jnp.arange(GMLP_BLOCK) // CHUNK
    gmask = chunk_id[None, :] <= chunk_id[:, None]
    w_spatial = jnp.where(gmask[None], odd_w_s[0], 0.0).astype(BF16)
    bs_col = odd_b_s[0].T
    b_f_col = even_b_f.reshape(FOX_HEADS, 1)
    ln = lambda p, l: p[l:l + 1]

    qkv, bch, fl = _proj(x2d, w_even_in, [(0, QKV, BF16), (QKV, QKV + BCH, F32), (QKV + BCH, EVEN_IN_PAD, F32)], "even_proj")
    fl3 = fl[:, :FOX_HEADS].T.reshape(FOX_HEADS, t // LANES, LANES).transpose(1, 0, 2)
    c3 = _fgate_fwd(fl3, b_f_col)
    c_rows = c3.transpose(1, 0, 2).reshape(FOX_HEADS, t)
    qp, kp, vp, kt, vt = _attn_pack(qkv, c_rows.T)
    attn, lse = _attn_fwd(qp, kp, vt)
    conv = _conv_fwd(bch, conv_w)
    x1_b, xh1, rs1 = _mm_res_ln([(attn, w_even_out[:FOX_WIDTH]), (conv, w_even_out[FOX_WIDTH:])], x2d,
                                ln(mix_ln_g, 0), ln(mix_ln_b, 0), "even_out_ln")
    w_odd_in, g_owo, w_fi0, w_fi1, g_fo0, g_fo1 = _gathered(later_h, "gather_later_wait", x1_b)
    w_odd_out = g_owo.reshape(d, d)
    w_ffn_in = [w_fi0, w_fi1]
    w_ffn_out = [g_fo0.reshape(FFN_HIDDEN, d), g_fo1.reshape(FFN_HIDDEN, d)]
    x2_b, ffn0 = _ffn_fwd((xh1, ln(mix_ln_g, 0), ln(mix_ln_b, 0)), x1_b, w_ffn_in[0], w_ffn_out[0],
                          ln(ffn_ln_g, 0), ln(ffn_ln_b, 0), 0)

    sv_odd, rs_odd, gated = _gmlp_fwd(x2_b, w_odd_in, v_ln_g, v_ln_b, w_spatial, bs_col)
    x3_b, xh3, rs3 = _mm_res_ln([(gated, w_odd_out)], (ffn0[3], ln(ffn_ln_g, 0), ln(ffn_ln_b, 0)),
                                ln(mix_ln_g, 1), ln(mix_ln_b, 1), "odd_out_ln")
    _, ffn1 = _ffn_fwd((xh3, ln(mix_ln_g, 1), ln(mix_ln_b, 1)), x3_b, w_ffn_in[1], w_ffn_out[1],
                       ln(ffn_ln_g, 1), ln(ffn_ln_b, 1), 1)

    sq, dz4, d_fg1, d_fb1 = _loss_ln_bwd(ffn1[3], ffn1[4], ln(ffn_ln_g, 1), ln(ffn_ln_b, 1), target)
    loss = lax.psum(0.5 / d * jnp.sum(sq), ("x", "y", "c"))
    (dz3, d_mg1, d_mb1), gi_f1, go_f1 = _ffn_bwd(dz4, ffn1, w_ffn_in[1], w_ffn_out[1], (xh3, rs3, ln(mix_ln_g, 1)), 1)

    dgated = _mm_nt([(dz3, 0, 0, d)], [w_odd_out], "odd_dgated")
    go_odd = _mm_tn(gated, dz3, "odd_dw_out", tn=d).reshape(N_CHIPS, 1, d // N_CHIPS, d)
    da_odd, dws, dbs_col, d_vg, d_vb = _gmlp_bwd(dgated, sv_odd, rs_odd, v_ln_g, v_ln_b, w_spatial, bs_col)
    gi_odd = _mm_tn(x2_b, da_odd, "odd_dw_in", tn=d // 2, stack_cols=True)[:, None]
    dz2, d_fg0, d_fb0 = _mm_nt([(da_odd, 0, 0, 0)], [w_odd_in], "odd_dx", res=dz3,
                               ln=(ffn0[3], ffn0[4], ln(ffn_ln_g, 0)))
    (dz1, d_mg0, d_mb0), gi_f0, go_f0 = _ffn_bwd(dz2, ffn0, w_ffn_in[0], w_ffn_out[0], (xh1, rs1, ln(mix_ln_g, 0)), 0)

    sent_early = [gi_odd, go_odd, gi_f0[:, None], gi_f1[:, None], go_f0[:, None], go_f1[:, None]]
    early_h, early_tok = _split_start(sent_early, "scatter4", "scatter_early_start")
    dmix = _mm_nt([(dz1, 0, 0, d)], [w_even_out], "even_dmix", after=early_tok)
    go_even = jnp.concatenate([_mm_tn(attn, dz1, "even_dw_out_attn", tn=d), _mm_tn(conv, dz1, "even_dw_out_conv", tn=d)],
                              axis=0).reshape(N_CHIPS, 1, d // N_CHIPS, d)
    dbch, dconv_w8 = _conv_bwd(bch, dmix, conv_w)
    qb, dob = _attn_bwd_prep(attn, dmix, qp, lse.reshape(FOX_HEADS, t).T)
    dqkv, dc_col = _attn_unpack(*_attn_bwd(qb, kp, vp, dob, kt))
    dc3 = dc_col.T.reshape(FOX_HEADS, t // LANES, LANES).transpose(1, 0, 2)
    dfl3, d_bf = _fgate_bwd(dc3, fl3, b_f_col)
    dfl = jnp.concatenate([dfl3.transpose(1, 0, 2).reshape(FOX_HEADS, t).T.astype(BF16),
                           jnp.zeros((t, LANES - FOX_HEADS), BF16)], axis=1)

    dws_masked = jnp.where(gmask[None], dws, 0.0)
    rep_names = ["odd_w_s", "odd_b_s", "mix_ln_g", "mix_ln_b", "ffn_ln_g", "ffn_ln_b", "even_b_f"]
    rep_grads = [dws_masked, dbs_col.T, jnp.concatenate([d_mg0, d_mg1]), jnp.concatenate([d_mb0, d_mb1]),
                 jnp.concatenate([d_fg0, d_fg1]), jnp.concatenate([d_fb0, d_fb1]), d_bf.reshape(1, FOX_HEADS)]
    rep_w = [(odd_w_s, m_odd_w_s, v_odd_w_s), (odd_b_s, m_odd_b_s, v_odd_b_s), (mix_ln_g, m_mix_ln_g, v_mix_ln_g),
             (mix_ln_b, m_mix_ln_b, v_mix_ln_b), (ffn_ln_g, m_ffn_ln_g, v_ffn_ln_g), (ffn_ln_b, m_ffn_ln_b, v_ffn_ln_b),
             (even_b_f, m_even_b_f, v_even_b_f)]
    rep_rows = [_to_rows(gr) for gr in rep_grads]
    n_rep = sum(r.shape[0] for r in rep_rows)
    pad_rep = (-n_rep) % SUBLANES
    dconv_w = dconv_w8[:CONV_K].reshape(CONV_K, N_CHIPS, LANES).transpose(1, 0, 2).reshape(N_CHIPS * CONV_K, LANES)
    packed = jnp.concatenate(rep_rows + [jnp.zeros((pad_rep, LANES), F32), d_vg.reshape(SUBLANES, LANES),
                                         d_vb.reshape(SUBLANES, LANES), dconv_w, jnp.zeros((4, LANES), F32)], axis=0)
    small_h, small_tok = _split_start([packed], "gather8", "gather_small_start")

    swap_h, swap_tok = _split_start(_scattered(early_h, "scatter_early_wait", small_tok), "swap2", "swap_early_start")
    dw_qkv = _mm_tn(dqkv, x2d, "even_dw_qkv", tn=d, tk=QKV // 2, after=swap_tok)
    dw_bch = _mm_tn(dbch, x2d, "even_dw_bch", tn=d, tk=BCH // 2)
    dw_f = _mm_tn(dfl, x2d, "even_dw_f", tn=d)
    gi_even = jnp.concatenate([dw_qkv, dw_f[:FOX_HEADS], dw_bch], axis=0).reshape(N_CHIPS, 1, -1, LANES)
    sent_late = [gi_even, go_even]
    late_h, late_tok = _split_start(sent_late, "scatter4", "scatter_late_start")
    grad_x = _mm_nt([(dqkv, 0, 0, QKV), (dbch, 0, QKV, QKV + BCH), (dfl, 0, QKV + BCH, EVEN_IN_PAD)], [w_even_in],
                    "even_dx", res=dz1, after=late_tok)
    mine_late = _scattered(late_h, "scatter_late_wait", grad_x)
    theirs_late = _exchange(mine_late, "swap2", "swap_late")
    mine_early, theirs_early = _split_wait(swap_h, "swap_early_wait", theirs_late[0])
    (packed,), (gathered,) = _split_wait(small_h, "gather_small_wait", theirs_early[0])
    gathered = lax.dynamic_update_index_in_dim(gathered, packed, 4 * lax.axis_index("x") + 2 * lax.axis_index("y")
                                               + lax.axis_index("c"), 0)
    mine, theirs = mine_late + mine_early, theirs_late + theirs_early
    big_w = [(even_w_in, m_even_w_in, v_even_w_in), (even_w_out, m_even_w_out, v_even_w_out),
             (odd_w_in, m_odd_w_in, v_odd_w_in), (odd_w_out, m_odd_w_out, v_odd_w_out)]
    big_names = ["even_w_in", "even_w_out", "odd_w_in", "odd_w_out"]
    res = {}
    for nm, own, sib, (w, m, v) in zip(big_names, mine, theirs, big_w):
        if nm == "even_w_in":
            rows = lambda a: jnp.swapaxes(a, 1, 2).reshape(1, -1, LANES)
            back = lambda a: jnp.swapaxes(a.reshape(1, EVEN_IN // N_CHIPS, d), 1, 2)
            res[nm] = [back(o) for o in _adamw([own, sib], rows(w), rows(m), rows(v), f"adamw_{nm}")]
            continue
        res[nm] = _adamw([own, sib], w, m, v, f"adamw_{nm}")
    for nm, at, (w, m, v) in (("ffn_w_in", 4, (ffn_w_in, m_ffn_w_in, v_ffn_w_in)),
                              ("ffn_w_out", 6, (ffn_w_out, m_ffn_w_out, v_ffn_w_out))):
        upper = _adamw([mine[at + 1], theirs[at + 1]], w, m, v, f"adamw_{nm}_1", layer=1)
        res[nm] = _adamw([mine[at], theirs[at]], w, m, v, f"adamw_{nm}_0", layer=0, into=upper)

    base = n_rep + pad_rep
    own_rows = jnp.concatenate([
        lax.dynamic_slice_in_dim(gathered, base + 2 * chip, 2, axis=1),
        lax.dynamic_slice_in_dim(gathered, base + SUBLANES + 2 * chip, 2, axis=1),
        lax.dynamic_slice_in_dim(gathered, base + 2 * SUBLANES + CONV_K * chip, CONV_K, axis=1),
        jnp.zeros((N_DEV, 1, LANES), F32)], axis=1)
    small_parts = jnp.concatenate([gathered[:, :base], own_rows], axis=1)[:, None]

    def pack_small(get):
        rows = [_to_rows(get(tw)) for tw in rep_w] + [jnp.zeros((pad_rep, LANES), F32)]
        rows += [get(sh).reshape(-1, LANES) for sh in ((odd_v_ln_g, m_odd_v_ln_g, v_odd_v_ln_g),
                                                       (odd_v_ln_b, m_odd_v_ln_b, v_odd_v_ln_b),
                                                       (even_conv_w, m_even_conv_w, v_even_conv_w))]
        return jnp.concatenate(rows + [jnp.zeros((1, LANES), F32)], axis=0)[None]

    small_out = _adamw([small_parts], pack_small(lambda tw: tw[0]), pack_small(lambda tw: tw[1]),
                       pack_small(lambda tw: tw[2]), "adamw_small")

    def unpack_small(rows3):
        rows = rows3[0]
        out, off = {}, 0
        for nm, (w, _, _), r in zip(rep_names, rep_w, rep_rows):
            out[nm] = rows[off:off + r.shape[0]].reshape(-1)[:w.size].reshape(w.shape)
            off += r.shape[0]
        off += pad_rep
        out["odd_v_ln_g"] = rows[off:off + 2].reshape(odd_v_ln_g.shape)
        out["odd_v_ln_b"] = rows[off + 2:off + 4].reshape(odd_v_ln_b.shape)
        out["even_conv_w"] = rows[off + 4:off + 4 + CONV_K].reshape(even_conv_w.shape)
        return out

    small = [unpack_small(o) for o in small_out]
    order = ["even_w_in", "even_b_f", "even_conv_w", "even_w_out", "odd_w_in", "odd_v_ln_g", "odd_v_ln_b", "odd_w_s",
             "odd_b_s", "odd_w_out", "mix_ln_g", "mix_ln_b", "ffn_w_in", "ffn_w_out", "ffn_ln_g", "ffn_ln_b"]
    outs = [loss, grad_x[None]]
    for kind in range(4):
        for nm in order:
            outs.append(res[nm][kind] if nm in res else small[kind][nm])
    return tuple(outs)
```

```python
import functools
import math

import jax
import jax.numpy as jnp
from jax import lax
from jax.experimental import pallas as pl
from jax.experimental.pallas import tpu as pltpu

F32 = jnp.float32
BF16 = jnp.bfloat16

D_MODEL = 1024
FOX_HEADS = 8
HEAD_DIM = 64
HEAD_PAIRS = FOX_HEADS // 2
FOX_WIDTH = FOX_HEADS * HEAD_DIM
CONV_WIDTH = 512
CONV_K = 3
QKV = 3 * FOX_WIDTH
BCH = 3 * CONV_WIDTH
EVEN_IN = QKV + FOX_HEADS + BCH
EVEN_IN_PAD = QKV + BCH + 128
GMLP_BLOCK = 128
GMLP_GROUPS = 8
CHUNK = 64
FFN_HIDDEN = 2816
HALF_HIDDEN = FFN_HIDDEN // 2
ALPHA = 4.0 ** 0.25
LN_EPS = 1e-5
ADAM_LR = 0.001
ADAM_B1 = 0.9
ADAM_B2 = 0.999
ADAM_EPS = 1e-08
ADAM_WD = 0.01
ADAM_STEP = 10
N_CHIPS = 4
N_DEV = 8
LANES = 128
SUBLANES = 8
ROW_TILE = 512
FFN_ROW_TILE = 512
REDUCE_TILE = 2048
ATT_BLOCK = 512
ATT_FWD_HEADS = 8
ATT_BWD_HEADS = 4
ADAMW_BLOCK_BYTES = 2 ** 20
VMEM_LIMIT = 56 * 2 ** 20
NEG = -1e30
MESH = pl.DeviceIdType.MESH
HIGHEST = lax.Precision.HIGHEST
Q_C, Q_ONE, Q_LSE = 64, 67, 70
K_ONE, K_C, K_ONE2 = 64, 67, 70
V_ONE = 64
DO_DELTA = 65
NT = (((1,), (1,)), ((), ()))
TN = (((0,), (0,)), ((), ()))


def _cp():
    return pltpu.CompilerParams(vmem_limit_bytes=VMEM_LIMIT)


def _resident(shape):
    zeros = (0,) * len(shape)
    return pl.BlockSpec(shape, lambda *_: zeros, pipeline_mode=pl.Buffered(1))


def _sds(shape, dtype):
    return jax.ShapeDtypeStruct(tuple(shape), dtype)


_MASKS = {
    "gather4": [(1, 0, 0), (0, 1, 0), (1, 1, 0)],
    "scatter4": [(1, 0, 0), (0, 1, 0), (1, 1, 0)],
    "swap2": [(0, 0, 1)],
    "gather8": [(0, 0, 1), (0, 1, 0), (0, 1, 1), (1, 0, 0), (1, 0, 1), (1, 1, 0), (1, 1, 1)],
}


def _exchange(arrs, mode, name):
    n = len(arrs)
    masks = _MASKS[mode]
    npeer = len(masks)
    lead = {"gather4": N_CHIPS, "gather8": N_DEV}.get(mode)
    out_shapes = [_sds(((lead,) if lead else ()) + a.shape, a.dtype) for a in arrs]

    def body(*refs):
        ins, outs = refs[:n], refs[n:2 * n]
        send_sems, recv_sems, loc_sems = refs[2 * n:]
        x, y, c = lax.axis_index("x"), lax.axis_index("y"), lax.axis_index("c")
        chip, dev = 2 * x + y, 4 * x + 2 * y + c
        sends, recvs, locs = [], [], []
        for k in range(n):
            if mode == "gather4":
                locs.append(pltpu.make_async_copy(ins[k], outs[k].at[chip], loc_sems.at[k]))
            elif mode == "scatter4":
                locs.append(pltpu.make_async_copy(ins[k].at[chip], outs[k].at[chip], loc_sems.at[k]))
            elif mode == "gather8":
                locs.append(pltpu.make_async_copy(ins[k], outs[k].at[dev], loc_sems.at[k]))
        for cp in locs:
            cp.start()
        for k in range(n):
            for j, (dx, dy, dc) in enumerate(masks):
                px = 1 - x if dx else x
                py = 1 - y if dy else y
                pc = 1 - c if dc else c
                pchip, pdev = 2 * px + py, 4 * px + 2 * py + pc
                if mode == "gather4":
                    src, dst, land = ins[k], outs[k].at[chip], outs[k].at[pchip]
                elif mode == "scatter4":
                    src, dst, land = ins[k].at[pchip], outs[k].at[chip], outs[k].at[pchip]
                elif mode == "swap2":
                    src, dst, land = ins[k], outs[k], outs[k]
                else:
                    src, dst, land = ins[k], outs[k].at[dev], outs[k].at[pdev]
                s = k * npeer + j
                kw = dict(send_sem=send_sems.at[s], recv_sem=recv_sems.at[s], device_id=(px, py, pc),
                          device_id_type=MESH)
                cp = pltpu.make_async_remote_copy(src_ref=src, dst_ref=dst, **kw)
                cp.start()
                sends.append(cp)
                recvs.append(pltpu.make_async_remote_copy(src_ref=src, dst_ref=land, **kw))
        for cp in recvs:
            cp.wait_recv()
        for cp in sends:
            cp.wait_send()
        for cp in locs:
            cp.wait()

    any_spec = pl.BlockSpec(memory_space=pl.ANY)
    outs = pl.pallas_call(
        body,
        out_shape=out_shapes,
        in_specs=[any_spec] * n,
        out_specs=[any_spec] * n,
        scratch_shapes=[pltpu.SemaphoreType.DMA((n * npeer,)), pltpu.SemaphoreType.DMA((n * npeer,)),
                        pltpu.SemaphoreType.DMA((max(n, 1),))],
        name=name,
    )(*arrs)
    return list(outs)


_HBM_SPEC = pl.BlockSpec(memory_space=pltpu.HBM)
_SEM_SPEC = pl.BlockSpec(memory_space=pltpu.SEMAPHORE)
_ANY_SPEC = pl.BlockSpec(memory_space=pl.ANY)
_EFFECT = pltpu.SideEffectType.DATAFLOW_SIDE_EFFECTING


def _split_copies(mode, ins, lands, send_sems, recv_sems):
    x, y, c = lax.axis_index("x"), lax.axis_index("y"), lax.axis_index("c")
    chip, dev = 2 * x + y, 4 * x + 2 * y + c
    masks = _MASKS[mode]
    out = []
    for k in range(len(ins)):
        for j, (dx, dy, dc) in enumerate(masks):
            px = 1 - x if dx else x
            py = 1 - y if dy else y
            pc = 1 - c if dc else c
            pchip, pdev = 2 * px + py, 4 * px + 2 * py + pc
            if mode == "gather4":
                src, dst, land = ins[k], lands[k].at[chip], lands[k].at[pchip]
            elif mode == "scatter4":
                src, dst, land = ins[k].at[pchip], lands[k].at[chip], lands[k].at[pchip]
            elif mode == "swap2":
                src, dst, land = ins[k], lands[k], lands[k]
            else:
                src, dst, land = ins[k], lands[k].at[dev], lands[k].at[pdev]
            s = k * len(masks) + j
            kw = dict(send_sem=send_sems.at[s], recv_sem=recv_sems.at[s], device_id=(px, py, pc), device_id_type=MESH)
            out.append((pltpu.make_async_remote_copy(src_ref=src, dst_ref=dst, **kw),
                        pltpu.make_async_remote_copy(src_ref=src, dst_ref=land, **kw)))
    return out


def _split_start(arrs, mode, name, after=None):
    n = len(arrs)
    nsem = n * len(_MASKS[mode])
    lead = {"gather4": (N_CHIPS,), "gather8": (N_DEV,)}.get(mode, ())
    land_shapes = [lead + a.shape for a in arrs]

    def body(*refs):
        ins, lands = refs[:n], refs[n:2 * n]
        outs = refs[2 * n + (after is not None):]
        for start, _ in _split_copies(mode, ins, lands, outs[0], outs[1]):
            start.start()
        outs[-1][...] = jnp.zeros(outs[-1].shape, F32)

    srcs = [pltpu.with_memory_space_constraint(a, pltpu.HBM) for a in arrs]
    empties = [pltpu.with_memory_space_constraint(lax.empty(s, a.dtype), pltpu.HBM) for s, a in zip(land_shapes, arrs)]
    res = pl.pallas_call(
        body, name=name,
        out_shape=(pltpu.SemaphoreType.DMA((nsem,)), pltpu.SemaphoreType.DMA((nsem,)),
                   *[pltpu.HBM(a.shape, a.dtype) for a in arrs],
                   *[pltpu.HBM(s, a.dtype) for s, a in zip(land_shapes, arrs)],
                   _sds((SUBLANES, LANES), F32)),
        in_specs=[_HBM_SPEC] * (2 * n) + ([_ANY_SPEC] if after is not None else []),
        out_specs=(_SEM_SPEC, _SEM_SPEC, *[_HBM_SPEC] * (2 * n), pl.BlockSpec(memory_space=pltpu.VMEM)),
        input_output_aliases={k: 2 + k for k in range(2 * n)},
        compiler_params=pltpu.CompilerParams(has_side_effects=_EFFECT),
    )(*srcs, *empties, *([after] if after is not None else []))
    return dict(mode=mode, n=n, sems=res[:2], bufs=res[2:2 + 2 * n]), res[-1]


def _split_wait(handle, name, after):
    n, mode = handle["n"], handle["mode"]

    def body(*refs):
        ins, lands = refs[:n], refs[n:2 * n]
        send_sems, recv_sems = refs[2 * n], refs[2 * n + 1]
        for _, arrival in _split_copies(mode, ins, lands, send_sems, recv_sems):
            arrival.wait_send()
            arrival.wait_recv()

    bufs = handle["bufs"]
    res = pl.pallas_call(
        body, name=name,
        out_shape=tuple(pltpu.HBM(b.shape, b.dtype) for b in bufs),
        in_specs=[_HBM_SPEC] * (2 * n) + [_SEM_SPEC, _SEM_SPEC, _ANY_SPEC],
        out_specs=tuple([_HBM_SPEC] * (2 * n)),
        input_output_aliases={k: k for k in range(2 * n)},
        compiler_params=pltpu.CompilerParams(has_side_effects=_EFFECT),
    )(*bufs, *handle["sems"], after)
    return list(res[:n]), list(res[n:])


def _with_own(landed, own):
    chip = 2 * lax.axis_index("x") + lax.axis_index("y")
    return lax.dynamic_update_index_in_dim(landed, own, chip, 0)


def _gathered(handle, name, after):
    sent, landed = _split_wait(handle, name, after)
    return [_with_own(g, own) for g, own in zip(landed, sent)]


def _scattered(handle, name, after):
    chip = 2 * lax.axis_index("x") + lax.axis_index("y")
    sent, landed = _split_wait(handle, name, after)
    return [_with_own(r, lax.dynamic_index_in_dim(g, chip, 0, keepdims=False)) for r, g in zip(landed, sent)]


def _sigmoid(x):
    return 0.5 * jnp.tanh(0.5 * x) + 0.5


def _log_sigmoid(x):
    e = jnp.exp(-jnp.abs(x))
    log1p = jnp.where(e < 1e-2, e * (1.0 - e * (0.5 - e * (1.0 / 3.0))), jnp.log(1.0 + e))
    return jnp.minimum(x, 0.0) - log1p


def _ln_fwd(z):
    mu = jnp.mean(z, axis=-1, keepdims=True)
    zc = z - mu
    var = jnp.mean(zc * zc, axis=-1, keepdims=True)
    rstd = lax.rsqrt(var + LN_EPS)
    return zc * rstd, rstd


def _ln_bwd(dy, xhat, rstd, g):
    dxh = dy * g
    m1 = jnp.mean(dxh, axis=-1, keepdims=True)
    m2 = jnp.mean(dxh * xhat, axis=-1, keepdims=True)
    dz = rstd * (dxh - m1 - xhat * m2)
    return dz, jnp.sum(dy * xhat, axis=0, keepdims=True), jnp.sum(dy, axis=0, keepdims=True)


def _shift_down(z, halo):
    r = lax.broadcasted_iota(jnp.int32, z.shape, 0)
    z1 = jnp.where(r == 0, halo[7:8, :], pltpu.roll(z, 1, 0))
    z2 = jnp.where(r == 0, halo[6:7, :], jnp.where(r == 1, halo[7:8, :], pltpu.roll(z, 2, 0)))
    return z1, z2


def _shift_up(z, halo):
    n = z.shape[0]
    r = lax.broadcasted_iota(jnp.int32, z.shape, 0)
    z1 = jnp.where(r == n - 1, halo[0:1, :], pltpu.roll(z, n - 1, 0))
    z2 = jnp.where(r == n - 1, halo[1:2, :], jnp.where(r == n - 2, halo[0:1, :], pltpu.roll(z, n - 2, 0)))
    return z1, z2


def _accumulate(ref, first, value):
    @pl.when(first)
    def _():
        ref[...] = value

    @pl.when(jnp.logical_not(first))
    def _():
        ref[...] += value


def _proj(x, w, splits, name):
    t, k = x.shape
    tm = min(ROW_TILE, t)

    def body(x_ref, w_ref, *outs):
        a = x_ref[...].astype(BF16)
        for (lo, hi, dt), o in zip(splits, outs):
            o[...] = jnp.dot(a, w_ref[:, lo:hi], preferred_element_type=F32).astype(dt)

    return pl.pallas_call(
        body, grid=(t // tm,),
        in_specs=[pl.BlockSpec((tm, k), lambda i: (i, 0)), _resident(w.shape)],
        out_specs=[pl.BlockSpec((tm, hi - lo), lambda i: (i, 0)) for lo, hi, _ in splits],
        out_shape=[_sds((t, hi - lo), dt) for lo, hi, dt in splits],
        compiler_params=_cp(), name=name)(x, w)


def _fgate_fwd(fl3, b_f):
    nc = fl3.shape[0]

    def body(f_ref, b_ref, c_ref):
        r = lax.broadcasted_iota(jnp.int32, (LANES, LANES), 0)
        cidx = lax.broadcasted_iota(jnp.int32, (LANES, LANES), 1)
        upper = (r <= cidx).astype(F32)

        def step(i, carry):
            lf = _log_sigmoid(f_ref[i] + b_ref[...])
            cc = jnp.dot(lf, upper, precision=HIGHEST, preferred_element_type=F32) + carry
            c_ref[i] = cc
            return cc[:, LANES - 1:LANES]

        lax.fori_loop(0, nc, step, jnp.zeros((FOX_HEADS, 1), F32))

    return pl.pallas_call(body, out_shape=_sds(fl3.shape, F32), name="fgate_fwd")(fl3, b_f)


def _fgate_bwd(dc3, fl3, b_f):
    nc = fl3.shape[0]

    def body(dc_ref, f_ref, b_ref, df_ref, db_ref):
        r = lax.broadcasted_iota(jnp.int32, (LANES, LANES), 0)
        cidx = lax.broadcasted_iota(jnp.int32, (LANES, LANES), 1)
        lower = (r >= cidx).astype(F32)

        def step(n, carry):
            suffix, db = carry
            i = nc - 1 - n
            dlf = jnp.dot(dc_ref[i], lower, precision=HIGHEST, preferred_element_type=F32) + suffix
            df = dlf * (1.0 - _sigmoid(f_ref[i] + b_ref[...]))
            df_ref[i] = df
            return dlf[:, 0:1], db + jnp.sum(df, axis=1, keepdims=True)

        zero = jnp.zeros((FOX_HEADS, 1), F32)
        _, db = lax.fori_loop(0, nc, step, (zero, zero))
        db_ref[...] = db

    return pl.pallas_call(body, out_shape=[_sds(fl3.shape, F32), _sds((FOX_HEADS, 1), F32)],
                          name="fgate_bwd")(dc3, fl3, b_f)


def _split3(c):
    hi = c.astype(BF16).astype(F32)
    mid = (c - hi).astype(BF16).astype(F32)
    lo = (c - hi - mid).astype(BF16).astype(F32)
    return hi, mid, lo


def _lane_pieces(lane, start, pieces, sign):
    out = jnp.zeros(lane.shape, F32)
    for n, p in enumerate(pieces):
        out = jnp.where(lane == start + n, sign * p, out)
    return out


def _attn_pack(qkv, c_col):
    t = qkv.shape[0]
    tm = min(ROW_TILE, t)
    hd = HEAD_DIM

    def body(x_ref, c_ref, qp_ref, kp_ref, vp_ref, kt_ref, vt_ref):
        lane = lax.broadcasted_iota(jnp.int32, (tm, hd), 1) + hd
        for h in range(FOX_HEADS):
            pieces = _split3(c_ref[:, h:h + 1])
            ones = lambda a, b: jnp.where(jnp.logical_and(lane >= a, lane < b), 1.0, 0.0)
            q_extra = _lane_pieces(lane, Q_C, pieces, 1.0) + ones(Q_ONE, Q_ONE + 3)
            k_extra = _lane_pieces(lane, K_C, pieces, -1.0) + ones(K_ONE, K_ONE + 3) + ones(K_ONE2, K_ONE2 + 3)
            qp_ref[h, :, :hd] = (x_ref[:, h * hd:(h + 1) * hd].astype(F32) * (hd ** -0.5)).astype(BF16)
            qp_ref[h, :, hd:] = q_extra.astype(BF16)
            kp_ref[h, :, :hd] = x_ref[:, FOX_WIDTH + h * hd:FOX_WIDTH + (h + 1) * hd]
            kp_ref[h, :, hd:] = k_extra.astype(BF16)
            vp_ref[h, :, :hd] = x_ref[:, 2 * FOX_WIDTH + h * hd:2 * FOX_WIDTH + (h + 1) * hd]
            vp_ref[h, :, hd:] = ones(V_ONE, V_ONE + 4).astype(BF16)
            kt_ref[h] = kp_ref[h].astype(F32).T.astype(BF16)
            vt_ref[h] = vp_ref[h].astype(F32).T.astype(BF16)

    row3 = pl.BlockSpec((FOX_HEADS, tm, LANES), lambda i: (0, i, 0))
    col3 = pl.BlockSpec((FOX_HEADS, LANES, tm), lambda i: (0, 0, i))
    return pl.pallas_call(
        body, grid=(t // tm,),
        in_specs=[pl.BlockSpec((tm, QKV), lambda i: (i, 0)), pl.BlockSpec((tm, FOX_HEADS), lambda i: (i, 0))],
        out_specs=[row3, row3, row3, col3, col3],
        out_shape=[_sds((FOX_HEADS, t, LANES), BF16)] * 3 + [_sds((FOX_HEADS, LANES, t), BF16)] * 2,
        compiler_params=_cp(), name="attn_pack")(qkv, c_col)


def _triangle(nq, key_major):
    if key_major:
        pairs = [(i, j) for j in range(nq) for i in range(j, nq)]
    else:
        pairs = [(i, j) for i in range(nq) for j in range(i + 1)]
    return jnp.asarray([p[0] for p in pairs], jnp.int32), jnp.asarray([p[1] for p in pairs], jnp.int32)


def _attn_fwd(qp, kp, vt):
    t = qp.shape[1]
    bq = min(ATT_BLOCK, t)
    nq = t // bq
    nh = ATT_FWD_HEADS
    i_tab, j_tab = _triangle(nq, key_major=False)

    def body(it_ref, jt_ref, q_ref, k_ref, vt_ref, o_ref, lse_ref, m_sc, acc_sc):
        s = pl.program_id(1)
        i, j = it_ref[s], jt_ref[s]

        @pl.when(j == 0)
        def _():
            m_sc[...] = jnp.full(m_sc.shape, NEG, F32)
            acc_sc[...] = jnp.zeros(acc_sc.shape, F32)

        def sweep(masked):
            scores = lambda h: lax.dot_general(k_ref[h], q_ref[h], NT, preferred_element_type=F32)

            def accumulate(h, pt, rescale):
                acc_sc[h] = rescale * acc_sc[h] + jnp.dot(vt_ref[h], pt, preferred_element_type=F32)

            ahead, behind = scores(0), None
            for h in range(nh):
                st = ahead
                if h + 1 < nh:
                    ahead = scores(h + 1)
                if behind is not None:
                    accumulate(*behind)
                if masked:
                    key = lax.broadcasted_iota(jnp.int32, (bq, bq), 0)
                    qry = lax.broadcasted_iota(jnp.int32, (bq, bq), 1)
                    st = jnp.where(key <= qry, st, NEG)
                m_prev = m_sc[h]
                m_new = jnp.maximum(m_prev, jnp.max(st, axis=0, keepdims=True))
                behind = (h, jnp.exp(st - m_new).astype(BF16), jnp.exp(m_prev - m_new))
                m_sc[h] = m_new
            accumulate(*behind)

        @pl.when(j < i)
        def _():
            sweep(False)

        @pl.when(j == i)
        def _():
            sweep(True)
            for h in range(nh):
                acc = acc_sc[h]
                denom = acc[V_ONE:V_ONE + 1, :]
                o_ref[:, h * HEAD_DIM:(h + 1) * HEAD_DIM] = (acc[:HEAD_DIM, :] / denom).T.astype(BF16)
                lse_ref[h] = m_sc[h] + jnp.log(denom)

    grid_spec = pltpu.PrefetchScalarGridSpec(
        num_scalar_prefetch=2, grid=(FOX_HEADS // nh, i_tab.shape[0]),
        in_specs=[pl.BlockSpec((nh, bq, LANES), lambda hp, s, it, jt: (hp, it[s], 0)),
                  pl.BlockSpec((nh, bq, LANES), lambda hp, s, it, jt: (hp, jt[s], 0)),
                  pl.BlockSpec((nh, LANES, bq), lambda hp, s, it, jt: (hp, 0, jt[s]))],
        out_specs=[pl.BlockSpec((bq, nh * HEAD_DIM), lambda hp, s, it, jt: (it[s], hp)),
                   pl.BlockSpec((nh, 1, bq), lambda hp, s, it, jt: (hp, 0, it[s]))],
        scratch_shapes=[pltpu.VMEM((nh, 1, bq), F32), pltpu.VMEM((nh, LANES, bq), F32)])
    return pl.pallas_call(body, grid_spec=grid_spec,
                          out_shape=[_sds((t, FOX_WIDTH), BF16), _sds((FOX_HEADS, 1, t), F32)],
                          compiler_params=_cp(), name="attn_fwd")(i_tab, j_tab, qp, kp, vt)


def _conv_fwd(bch, conv_w):
    t = bch.shape[0]
    tm = min(ROW_TILE, t)
    halo_blocks = tm // SUBLANES
    cw = CONV_WIDTH

    def body(cur_ref, prev_ref, w_ref, o_ref):
        i = pl.program_id(0)
        z = cur_ref[:, cw:2 * cw] * cur_ref[:, 2 * cw:]
        zp = jnp.where(i == 0, 0.0, prev_ref[:, cw:2 * cw] * prev_ref[:, 2 * cw:])
        z1, z2 = _shift_down(z, zp)
        y = w_ref[0:1, :] * z2 + w_ref[1:2, :] * z1 + w_ref[2:3, :] * z
        o_ref[...] = (cur_ref[:, :cw] * y).astype(BF16)

    return pl.pallas_call(
        body, grid=(t // tm,),
        in_specs=[pl.BlockSpec((tm, BCH), lambda i: (i, 0)),
                  pl.BlockSpec((SUBLANES, BCH), lambda i: (jnp.maximum(i * halo_blocks - 1, 0), 0)),
                  _resident(conv_w.shape)],
        out_specs=pl.BlockSpec((tm, cw), lambda i: (i, 0)),
        out_shape=_sds((t, cw), BF16), compiler_params=_cp(), name="conv_fwd")(bch, bch, conv_w)


def _mm_res_ln(pairs, res, g, b, name):
    from_ln = isinstance(res, tuple)
    res_args = list(res) if from_ln else [res]
    t, d = res_args[0].shape
    tm = min(ROW_TILE, t)
    n = len(pairs)

    def body(*refs):
        a_refs, w_refs = refs[:n], refs[n:2 * n]
        res_refs = refs[2 * n:2 * n + len(res_args)]
        g_ref, b_ref, yb_ref, xh_ref, rs_ref = refs[2 * n + len(res_args):]
        r = res_refs[0][...]
        if from_ln:
            r = r * res_refs[1][...] + res_refs[2][...]
        z = ALPHA * r
        for a_ref, w_ref in zip(a_refs, w_refs):
            z = z + jnp.dot(a_ref[...].astype(BF16), w_ref[...], preferred_element_type=F32)
        xhat, rstd = _ln_fwd(z)
        yb_ref[...] = (xhat * g_ref[...] + b_ref[...]).astype(BF16)
        xh_ref[...] = xhat
        rs_ref[...] = rstd

    row = lambda i: (i, 0)
    full = pl.BlockSpec((tm, d), row)
    return pl.pallas_call(
        body, grid=(t // tm,),
        in_specs=[pl.BlockSpec((tm, a.shape[1]), row) for a, _ in pairs] + [_resident(w.shape) for _, w in pairs]
        + [full] + [_resident(a.shape) for a in res_args[1:]] + [_resident(g.shape), _resident(b.shape)],
        out_specs=[full, full, pl.BlockSpec((tm, 1), row)],
        out_shape=[_sds((t, d), BF16), _sds((t, d), F32), _sds((t, 1), F32)],
        compiler_params=_cp(), name=name)(*[a for a, _ in pairs], *[w for _, w in pairs], *res_args, g, b)


def _ffn_in(x, wi, name):
    t, d = x.shape
    tm = min(FFN_ROW_TILE, t)
    hh = HALF_HIDDEN

    def body(x_ref, w_ref, gu_ref, h_ref):
        a = x_ref[...].astype(BF16)
        for c in range(2):
            gs, us = slice(c * hh, (c + 1) * hh), slice(FFN_HIDDEN + c * hh, FFN_HIDDEN + (c + 1) * hh)
            g = jnp.dot(a, w_ref[c], preferred_element_type=F32)
            u = jnp.dot(a, w_ref[2 + c], preferred_element_type=F32)
            sig = _sigmoid(g)
            silu = g * sig
            gu_ref[:, gs] = (u * sig * (1.0 + g * (1.0 - sig))).astype(BF16)
            gu_ref[:, us] = silu.astype(BF16)
            h_ref[:, gs] = (silu * u).astype(BF16)

    row = lambda i: (i, 0)
    return pl.pallas_call(
        body, grid=(t // tm,),
        in_specs=[pl.BlockSpec((tm, d), row), _resident(wi.shape)],
        out_specs=[pl.BlockSpec((tm, 2 * FFN_HIDDEN), row), pl.BlockSpec((tm, FFN_HIDDEN), row)],
        out_shape=[_sds((t, 2 * FFN_HIDDEN), BF16), _sds((t, FFN_HIDDEN), BF16)],
        compiler_params=_cp(), name=name)(x, wi)


def _gmlp_fwd(x, w_in, vg, vb, wm, bs_col):
    t, d = x.shape
    tm = min(ROW_TILE, t)
    gb = GMLP_BLOCK

    def body(x_ref, w_ref, vg_ref, vb_ref, wm_ref, bs_ref, sv_ref, rs_ref, o_ref, a_sc):
        xb = x_ref[...].astype(BF16)
        nc = w_ref.shape[2]
        for j in range(w_ref.shape[0]):
            a_sc[:, j * nc:(j + 1) * nc] = jnp.dot(xb, w_ref[j], preferred_element_type=F32)
        halves = []
        for half in range(2):
            a = a_sc[:, half * d:(half + 1) * d]
            cdf = 0.5 * (1.0 + lax.erf(a * (2.0 ** -0.5)))
            halves.append(a * cdf)
            slope = cdf + a * (jnp.exp(-0.5 * a * a) * (1.0 / math.sqrt(2.0 * math.pi)))
            sv_ref[:, (2 * half + 1) * d:(2 * half + 2) * d] = slope.astype(BF16)
        u = halves[0]
        vhat, rstd = _ln_fwd(halves[1])
        sv_ref[:, :d] = u.astype(BF16)
        sv_ref[:, 2 * d:3 * d] = vhat.astype(BF16)
        rs_ref[...] = rstd
        vln = (vhat * vg_ref[...] + vb_ref[...]).astype(BF16)
        for blk in range(tm // gb):
            rs = slice(blk * gb, (blk + 1) * gb)
            for gi in range(GMLP_GROUPS):
                cs = slice(gi * gb, (gi + 1) * gb)
                s = jnp.dot(wm_ref[gi], vln[rs, cs], preferred_element_type=F32) + bs_ref[:, gi:gi + 1]
                o_ref[rs, cs] = (u[rs, cs] * s).astype(BF16)

    row = lambda i: (i, 0)
    return pl.pallas_call(
        body, grid=(t // tm,),
        in_specs=[pl.BlockSpec((tm, d), row), _resident(w_in.shape), _resident(vg.shape), _resident(vb.shape),
                  _resident(wm.shape), _resident(bs_col.shape)],
        out_specs=[pl.BlockSpec((tm, 4 * d), row), pl.BlockSpec((tm, 1), row), pl.BlockSpec((tm, d), row)],
        out_shape=[_sds((t, 4 * d), BF16), _sds((t, 1), F32), _sds((t, d), BF16)],
        scratch_shapes=[pltpu.VMEM((tm, 2 * d), F32)],
        compiler_params=_cp(), name="gmlp_fwd")(x, w_in, vg, vb, wm, bs_col)


def _loss_ln_bwd(xhat, rstd, g, b, target):
    t, d = xhat.shape
    tm = min(ROW_TILE, t)

    def body(xh_ref, rs_ref, g_ref, b_ref, t_ref, sq_ref, dz_ref, dg_ref, db_ref):
        first = pl.program_id(0) == 0
        xh = xh_ref[...]
        err = xh * g_ref[...] + b_ref[...] - t_ref[...]
        dz, dg, db = _ln_bwd(err * (1.0 / d), xh, rs_ref[...], g_ref[...])
        dz_ref[...] = dz
        _accumulate(sq_ref, first, jnp.sum(err * err, axis=0, keepdims=True))
        _accumulate(dg_ref, first, dg)
        _accumulate(db_ref, first, db)

    row = lambda i: (i, 0)
    vec = pl.BlockSpec((1, d), lambda i: (0, 0))
    return pl.pallas_call(
        body, grid=(t // tm,),
        in_specs=[pl.BlockSpec((tm, d), row), pl.BlockSpec((tm, 1), row), _resident(g.shape), _resident(b.shape),
                  pl.BlockSpec((tm, d), row)],
        out_specs=[vec, pl.BlockSpec((tm, d), row), vec, vec],
        out_shape=[_sds((1, d), F32), _sds((t, d), F32), _sds((1, d), F32), _sds((1, d), F32)],
        compiler_params=_cp(), name="loss_ln_bwd")(xhat, rstd, g, b, target)


def _mm_nt(pairs, ws, name, *, tm=ROW_TILE, res=None, ln=None, out_dtype=F32, after=None):
    t = pairs[0][0].shape[0]
    k = ws[0].shape[-2]
    tm = min(tm, t)
    n, nw = len(pairs), len(ws)

    def body(*refs):
        refs = refs[after is not None:]
        a_refs, w_refs = refs[:n], refs[n:n + nw]
        rest = list(refs[n + nw:])
        dx = None
        for a_ref, (_, wi, lo, hi) in zip(a_refs, pairs):
            w_ref = w_refs[wi]
            if len(w_ref.shape) == 3:
                nc = w_ref.shape[2]
                parts = [lax.dot_general(a_ref[:, j * nc:(j + 1) * nc].astype(BF16), w_ref[j], NT,
                                         preferred_element_type=F32) for j in range(w_ref.shape[0])]
            else:
                parts = [lax.dot_general(a_ref[...].astype(BF16), w_ref[:, lo:hi], NT, preferred_element_type=F32)]
            for part in parts:
                dx = part if dx is None else dx + part
        if res is not None:
            dx = dx + ALPHA * rest.pop(0)[...]
        if ln is None:
            rest[0][...] = dx.astype(out_dtype)
            return
        xh_ref, rs_ref, g_ref, dz_ref, dg_ref, db_ref = rest
        first = pl.program_id(0) == 0
        dz, dg, db = _ln_bwd(dx, xh_ref[...], rs_ref[...], g_ref[...])
        dz_ref[...] = dz
        _accumulate(dg_ref, first, dg)
        _accumulate(db_ref, first, db)

    row = lambda i: (i, 0)
    in_specs = [pl.BlockSpec((tm, a.shape[1]), row) for a, _, _, _ in pairs] + [_resident(w.shape) for w in ws]
    args = [a for a, _, _, _ in pairs] + list(ws)
    if res is not None:
        in_specs.append(pl.BlockSpec((tm, k), row))
        args.append(res)
    if ln is None:
        out_specs = pl.BlockSpec((tm, k), row)
        out_shape = _sds((t, k), out_dtype)
    else:
        xhat, rstd, g = ln
        in_specs += [pl.BlockSpec((tm, k), row), pl.BlockSpec((tm, 1), row), _resident(g.shape)]
        args += [xhat, rstd, g]
        vec = pl.BlockSpec((1, k), lambda i: (0, 0))
        out_specs = [pl.BlockSpec((tm, k), row), vec, vec]
        out_shape = [_sds((t, k), F32), _sds((1, k), F32), _sds((1, k), F32)]
    if after is not None:
        in_specs.insert(0, _ANY_SPEC)
        args.insert(0, after)
    return pl.pallas_call(body, grid=(t // tm,), in_specs=in_specs, out_specs=out_specs, out_shape=out_shape,
                          compiler_params=_cp(), name=name)(*args)


def _mm_tn(a, b, name, *, tn, tk=None, tt=None, stack_cols=False, out_dtype=BF16, after=None):
    t, k = a.shape
    n = b.shape[1]
    tk = k if tk is None else tk
    tt = min(REDUCE_TILE if tt is None else tt, t)
    nt = t // tt

    def body(a_ref, b_ref, *rest):
        o_ref, acc_ref = rest[after is not None:]
        s = pl.program_id(2)
        part = lax.dot_general(a_ref[...].astype(BF16), b_ref[...].astype(BF16), TN, preferred_element_type=F32)
        _accumulate(acc_ref, s == 0, part)

        @pl.when(s == nt - 1)
        def _():
            o_ref[...] = acc_ref[...].astype(out_dtype).reshape(o_ref.shape)

    if stack_cols:
        assert tk == k
        out_spec = pl.BlockSpec((1, k, tn), lambda kk, j, s: (j, 0, 0))
        out_shape = _sds((n // tn, k, tn), out_dtype)
    else:
        out_spec = pl.BlockSpec((tk, tn), lambda kk, j, s: (kk, j))
        out_shape = _sds((k, n), out_dtype)
    return pl.pallas_call(
        body, grid=(k // tk, n // tn, nt),
        in_specs=[pl.BlockSpec((tt, tk), lambda kk, j, s: (s, kk)), pl.BlockSpec((tt, tn), lambda kk, j, s: (s, j))]
        + ([_ANY_SPEC] if after is not None else []),
        out_specs=out_spec, out_shape=out_shape,
        scratch_shapes=[pltpu.VMEM((tk, tn), F32)],
        compiler_params=_cp(), name=name)(a, b, *([after] if after is not None else []))


def _ffn_bwd_hidden(dz, wo, gu, name):
    t, d = dz.shape
    tm = min(FFN_ROW_TILE, t)
    hh = HALF_HIDDEN

    def body(dz_ref, w_ref, gu_ref, o_ref):
        a = dz_ref[...].astype(BF16)
        for c in range(2):
            gs, us = slice(c * hh, (c + 1) * hh), slice(FFN_HIDDEN + c * hh, FFN_HIDDEN + (c + 1) * hh)
            dh = lax.dot_general(a, w_ref[gs, :], NT, preferred_element_type=F32)
            o_ref[:, gs] = (dh * gu_ref[:, gs].astype(F32)).astype(BF16)
            o_ref[:, us] = (dh * gu_ref[:, us].astype(F32)).astype(BF16)

    row = lambda i: (i, 0)
    return pl.pallas_call(
        body, grid=(t // tm,),
        in_specs=[pl.BlockSpec((tm, d), row), _resident(wo.shape), pl.BlockSpec((tm, 2 * FFN_HIDDEN), row)],
        out_specs=pl.BlockSpec((tm, 2 * FFN_HIDDEN), row),
        out_shape=_sds((t, 2 * FFN_HIDDEN), BF16), compiler_params=_cp(), name=name)(dz, wo, gu)


def _gmlp_bwd(dgated, saved, rstd_v, vg, vb, wm, bs_col):
    t, d = dgated.shape
    d2 = 2 * d
    tm = min(ROW_TILE, t)
    gb = GMLP_BLOCK

    def body(dg_ref, sv_ref, rs_ref, vg_ref, vb_ref, wm_ref, bs_ref, da_ref, dws_ref, dbs_ref, dvg_ref, dvb_ref, dvln_sc):
        first = pl.program_id(0) == 0
        u = sv_ref[:, :d].astype(F32)
        vhat = sv_ref[:, 2 * d:3 * d].astype(F32)
        rstd = rs_ref[...]
        vln = (vhat * vg_ref[...] + vb_ref[...]).astype(BF16)
        dgate = dg_ref[...]

        @pl.when(first)
        def _():
            dws_ref[...] = jnp.zeros(dws_ref.shape, F32)
            dbs_ref[...] = jnp.zeros(dbs_ref.shape, F32)

        for blk in range(tm // gb):
            rs = slice(blk * gb, (blk + 1) * gb)
            for gi in range(GMLP_GROUPS):
                cs = slice(gi * gb, (gi + 1) * gb)
                vblk = vln[rs, cs]
                s = jnp.dot(wm_ref[gi], vblk, preferred_element_type=F32) + bs_ref[:, gi:gi + 1]
                dgb = dgate[rs, cs]
                da_ref[rs, cs] = (dgb * s * sv_ref[rs, d + gi * gb:d + (gi + 1) * gb].astype(F32)).astype(BF16)
                ds = dgb * u[rs, cs]
                dsb = ds.astype(BF16)
                dws_ref[gi] += lax.dot_general(dsb, vblk, NT, preferred_element_type=F32)
                dbs_ref[:, gi:gi + 1] += jnp.sum(ds, axis=1, keepdims=True)
                dvln_sc[rs, cs] = lax.dot_general(wm_ref[gi], dsb, TN, preferred_element_type=F32)
        dv, dvg, dvb = _ln_bwd(dvln_sc[...], vhat, rstd, vg_ref[...])
        da_ref[:, d:] = (dv * sv_ref[:, 3 * d:].astype(F32)).astype(BF16)
        _accumulate(dvg_ref, first, dvg)
        _accumulate(dvb_ref, first, dvb)

    row = lambda i: (i, 0)
    vec = pl.BlockSpec((1, d), lambda i: (0, 0))
    return pl.pallas_call(
        body, grid=(t // tm,),
        in_specs=[pl.BlockSpec((tm, d), row), pl.BlockSpec((tm, 4 * d), row), pl.BlockSpec((tm, 1), row),
                  _resident(vg.shape), _resident(vb.shape), _resident(wm.shape), _resident(bs_col.shape)],
        out_specs=[pl.BlockSpec((tm, d2), row), pl.BlockSpec(wm.shape, lambda i: (0, 0, 0)),
                   pl.BlockSpec(bs_col.shape, lambda i: (0, 0)), vec, vec],
        out_shape=[_sds((t, d2), BF16), _sds(wm.shape, F32), _sds(bs_col.shape, F32), _sds((1, d), F32), _sds((1, d), F32)],
        scratch_shapes=[pltpu.VMEM((tm, d), F32)],
        compiler_params=_cp(), name="gmlp_bwd")(dgated, saved, rstd_v, vg, vb, wm, bs_col)


def _conv_bwd(bch, dmix, conv_w):
    t = bch.shape[0]
    tm = min(ROW_TILE, t)
    nb = t // tm
    halo_blocks = tm // SUBLANES
    cw = CONV_WIDTH

    def body(cur_ref, prev_ref, next_ref, dc_ref, dn_ref, w_ref, o_ref, dw_ref):
        i = pl.program_id(0)
        bgate, cgate, hval = cur_ref[:, :cw], cur_ref[:, cw:2 * cw], cur_ref[:, 2 * cw:]
        z = cgate * hval
        zp = jnp.where(i == 0, 0.0, prev_ref[:, cw:2 * cw] * prev_ref[:, 2 * cw:])
        z1, z2 = _shift_down(z, zp)
        w0, w1, w2 = w_ref[0:1, :], w_ref[1:2, :], w_ref[2:3, :]
        dconv = dc_ref[...]
        o_ref[:, :cw] = (dconv * (w0 * z2 + w1 * z1 + w2 * z)).astype(BF16)
        dy = dconv * bgate
        dyn = jnp.where(i == nb - 1, 0.0, dn_ref[...] * next_ref[:, :cw])
        dy1, dy2 = _shift_up(dy, dyn)
        dz = w2 * dy + w1 * dy1 + w0 * dy2
        o_ref[:, cw:2 * cw] = (dz * hval).astype(BF16)
        o_ref[:, 2 * cw:] = (dz * cgate).astype(BF16)

        @pl.when(i == 0)
        def _():
            dw_ref[...] = jnp.zeros(dw_ref.shape, F32)

        for tap, zs in enumerate((z2, z1, z)):
            dw_ref[tap:tap + 1, :] += jnp.sum(dy * zs, axis=0, keepdims=True)

    last_halo = t // SUBLANES - 1
    return pl.pallas_call(
        body, grid=(nb,),
        in_specs=[pl.BlockSpec((tm, BCH), lambda i: (i, 0)),
                  pl.BlockSpec((SUBLANES, BCH), lambda i: (jnp.maximum(i * halo_blocks - 1, 0), 0)),
                  pl.BlockSpec((SUBLANES, BCH), lambda i: (jnp.minimum((i + 1) * halo_blocks, last_halo), 0)),
                  pl.BlockSpec((tm, cw), lambda i: (i, 1)),
                  pl.BlockSpec((SUBLANES, cw), lambda i: (jnp.minimum((i + 1) * halo_blocks, last_halo), 1)),
                  _resident(conv_w.shape)],
        out_specs=[pl.BlockSpec((tm, BCH), lambda i: (i, 0)), pl.BlockSpec((SUBLANES, cw), lambda i: (0, 0))],
        out_shape=[_sds((t, BCH), BF16), _sds((SUBLANES, cw), F32)],
        compiler_params=_cp(), name="conv_bwd")(bch, bch, bch, dmix, dmix, conv_w)


def _attn_bwd_prep(o, dmix, qp, lse_col):
    t = o.shape[0]
    tm = min(ROW_TILE, t)
    hd = HEAD_DIM

    def body(o_ref, do_ref, qp_ref, lse_ref, qb_ref, dob_ref):
        lane = lax.broadcasted_iota(jnp.int32, (tm, hd), 1) + hd
        for h in range(FOX_HEADS):
            do = do_ref[:, h * hd:(h + 1) * hd]
            delta = jnp.sum(o_ref[:, h * hd:(h + 1) * hd].astype(F32) * do, axis=-1, keepdims=True)
            dob_ref[h, :, :hd] = do.astype(BF16)
            dob_ref[h, :, hd:] = _lane_pieces(lane, DO_DELTA, _split3(delta), -1.0).astype(BF16)
            qb_ref[h, :, :hd] = qp_ref[h, :, :hd]
            qb_ref[h, :, hd:] = (qp_ref[h, :, hd:].astype(F32)
                                 + _lane_pieces(lane, Q_LSE, _split3(lse_ref[:, h:h + 1]), -1.0)).astype(BF16)

    row3 = pl.BlockSpec((FOX_HEADS, tm, LANES), lambda i: (0, i, 0))
    return pl.pallas_call(
        body, grid=(t // tm,),
        in_specs=[pl.BlockSpec((tm, FOX_WIDTH), lambda i: (i, 0)), pl.BlockSpec((tm, FOX_WIDTH), lambda i: (i, 0)), row3,
                  pl.BlockSpec((tm, FOX_HEADS), lambda i: (i, 0))],
        out_specs=[row3, row3], out_shape=[_sds((FOX_HEADS, t, LANES), BF16)] * 2,
        compiler_params=_cp(), name="attn_bwd_prep")(o, dmix, qp, lse_col)


def _attn_bwd(qb, kp, vp, dob, kt):
    t = qb.shape[1]
    bq = min(ATT_BLOCK, t)
    nq = t // bq
    i_tab, j_tab = _triangle(nq, key_major=True)

    def body(it_ref, jt_ref, q_ref, k_ref, v_ref, do_ref, kt_ref, dqt_ref, dk_ref, dv_ref, dk_sc, dv_sc):
        s = pl.program_id(1)
        i, j = it_ref[s], jt_ref[s]

        @pl.when(s == 0)
        def _():
            dqt_ref[...] = jnp.zeros(dqt_ref.shape, F32)

        @pl.when(i == j)
        def _():
            dk_sc[...] = jnp.zeros(dk_sc.shape, F32)
            dv_sc[...] = jnp.zeros(dv_sc.shape, F32)

        cols = pl.ds(pl.multiple_of(i * bq, bq), bq)

        def sweep(masked):
            def scores(h):
                return (lax.dot_general(k_ref[h], q_ref[h], NT, preferred_element_type=F32),
                        lax.dot_general(v_ref[h], do_ref[h], NT, preferred_element_type=F32))

            def accumulate(h, ptb, dstb):
                dv_sc[h] += jnp.dot(ptb, do_ref[h], preferred_element_type=F32)
                dk_sc[h] += jnp.dot(dstb, q_ref[h], preferred_element_type=F32)
                dqt_ref[h, :, cols] += jnp.dot(kt_ref[h], dstb, preferred_element_type=F32)

            ahead, behind = scores(0), None
            for h in range(ATT_BWD_HEADS):
                st, dpt = ahead
                if h + 1 < ATT_BWD_HEADS:
                    ahead = scores(h + 1)
                if behind is not None:
                    accumulate(*behind)
                if masked:
                    key = lax.broadcasted_iota(jnp.int32, (bq, bq), 0)
                    qry = lax.broadcasted_iota(jnp.int32, (bq, bq), 1)
                    st = jnp.where(key <= qry, st, NEG)
                pt = jnp.exp(st)
                behind = (h, pt.astype(BF16), (pt * dpt).astype(BF16))
            accumulate(*behind)

        @pl.when(i == j)
        def _():
            sweep(True)

        @pl.when(i > j)
        def _():
            sweep(False)

        @pl.when(i == nq - 1)
        def _():
            dk_ref[...] = dk_sc[...]
            dv_ref[...] = dv_sc[...].astype(BF16)

    nh = ATT_BWD_HEADS
    qblk = pl.BlockSpec((nh, bq, LANES), lambda hp, s, it, jt: (hp, it[s], 0))
    kblk = pl.BlockSpec((nh, bq, LANES), lambda hp, s, it, jt: (hp, jt[s], 0))
    grid_spec = pltpu.PrefetchScalarGridSpec(
        num_scalar_prefetch=2, grid=(FOX_HEADS // nh, i_tab.shape[0]),
        in_specs=[qblk, kblk, kblk, qblk, pl.BlockSpec((nh, LANES, bq), lambda hp, s, it, jt: (hp, 0, jt[s]))],
        out_specs=[pl.BlockSpec((nh, LANES, t), lambda hp, s, it, jt: (hp, 0, 0), pipeline_mode=pl.Buffered(1)),
                   kblk, kblk],
        scratch_shapes=[pltpu.VMEM((nh, bq, LANES), F32), pltpu.VMEM((nh, bq, LANES), F32)])
    return pl.pallas_call(body, grid_spec=grid_spec,
                          out_shape=[_sds((FOX_HEADS, LANES, t), F32), _sds((FOX_HEADS, t, LANES), F32),
                                     _sds((FOX_HEADS, t, LANES), BF16)],
                          compiler_params=_cp(), name="attn_bwd")(i_tab, j_tab, qb, kp, vp, dob, kt)


def _attn_unpack(dqt, dkp, dvp):
    t = dkp.shape[1]
    tm = min(ROW_TILE, t)
    hd = HEAD_DIM

    def body(dqt_ref, dk_ref, dv_ref, o_ref, dc_ref):
        for h in range(FOX_HEADS):
            dq = dqt_ref[h].T
            o_ref[:, h * hd:(h + 1) * hd] = (dq[:, :hd] * (hd ** -0.5)).astype(BF16)
            o_ref[:, FOX_WIDTH + h * hd:FOX_WIDTH + (h + 1) * hd] = dk_ref[h, :, :hd].astype(BF16)
            o_ref[:, 2 * FOX_WIDTH + h * hd:2 * FOX_WIDTH + (h + 1) * hd] = dv_ref[h, :, :hd]
            dc_ref[:, h:h + 1] = dq[:, K_ONE:K_ONE + 1] - dk_ref[h, :, Q_ONE:Q_ONE + 1]

    row3 = pl.BlockSpec((FOX_HEADS, tm, LANES), lambda i: (0, i, 0))
    return pl.pallas_call(
        body, grid=(t // tm,),
        in_specs=[pl.BlockSpec((FOX_HEADS, LANES, tm), lambda i: (0, 0, i)), row3, row3],
        out_specs=[pl.BlockSpec((tm, QKV), lambda i: (i, 0)), pl.BlockSpec((tm, FOX_HEADS), lambda i: (i, 0))],
        out_shape=[_sds((t, QKV), BF16), _sds((t, FOX_HEADS), F32)],
        compiler_params=_cp(), name="attn_unpack")(dqt, dkp, dvp)


def _adamw(parts, w, m, v, name, layer=None, into=None):
    nl, r, c = w.shape
    fits = [cand for cand in [*range(SUBLANES, r, SUBLANES), r] if r % cand == 0 and cand * c * 4 <= ADAMW_BLOCK_BYTES]
    tr = max(fits) if fits else r
    npart = len(parts)
    bc1 = 1.0 - ADAM_B1 ** ADAM_STEP
    bc2 = 1.0 - ADAM_B2 ** ADAM_STEP

    def body(*refs):
        p_refs = refs[:npart]
        w_ref, m_ref, v_ref = refs[npart:npart + 3]
        g_ref, d_ref, nm_ref, nv_ref = refs[-4:]
        sums = []
        for p_ref in p_refs:
            acc = p_ref[0, 0].astype(F32)
            for s in range(1, p_ref.shape[0]):
                acc = acc + p_ref[s, 0].astype(F32)
            sums.append(acc)
        g = sums[0]
        for extra in sums[1:]:
            g = g + extra
        nm = ADAM_B1 * m_ref[0] + (1.0 - ADAM_B1) * g
        nv = ADAM_B2 * v_ref[0] + (1.0 - ADAM_B2) * (g * g)
        m_hat = nm / bc1
        v_hat = nv / bc2
        g_ref[0] = g
        d_ref[0] = -ADAM_LR * (m_hat / (jnp.sqrt(v_hat) + ADAM_EPS) + ADAM_WD * w_ref[0])
        nm_ref[0] = nm
        nv_ref[0] = nv

    first = 0 if layer is None else layer
    blk = pl.BlockSpec((1, tr, c), lambda l, i: (first + l, i, 0))
    extra = [] if into is None else list(into)
    return pl.pallas_call(
        body, grid=(nl if layer is None else 1, r // tr),
        in_specs=[pl.BlockSpec((p.shape[0], 1, tr, c), lambda l, i: (0, l, i, 0)) for p in parts] + [blk, blk, blk]
        + [_ANY_SPEC] * len(extra),
        out_specs=[blk] * 4, out_shape=[_sds(w.shape, F32)] * 4,
        input_output_aliases={npart + 3 + k: k for k in range(len(extra))},
        compiler_params=_cp(), name=name)(*parts, w, m, v, *extra)


def _to_rows(a):
    flat = a.reshape(-1)
    pad = (-flat.shape[0]) % LANES
    if pad:
        flat = jnp.concatenate([flat, jnp.zeros((pad,), flat.dtype)])
    return flat.reshape(-1, LANES)


def _by_owner_cols(dw):
    k, n = dw.shape
    return dw.reshape(k, N_CHIPS, n // N_CHIPS).transpose(1, 0, 2)[:, None]


def _ffn_fwd(xin_ln, xin_b, wi, wo, g, b, layer):
    gu, h = _ffn_in(xin_b, wi, f"ffn_in_{layer}")
    y_b, xhat, rstd = _mm_res_ln([(h, wo)], xin_ln, g, b, f"ffn_out_ln_{layer}")
    return y_b, (xin_b, gu, h, xhat, rstd)


def _ffn_bwd(dz, saved, wi, wo, ln_below, layer):
    xin_b, gu, h, _, _ = saved
    dgu = _ffn_bwd_hidden(dz, wo, gu, f"ffn_bwd_hidden_{layer}")
    g_out = _mm_tn(h, dz, f"ffn_dw_out_{layer}", tn=D_MODEL, tk=HALF_HIDDEN, tt=REDUCE_TILE // 2)
    g_in = _mm_tn(xin_b, dgu, f"ffn_dw_in_{layer}", tn=HALF_HIDDEN, stack_cols=True)
    below = _mm_nt([(dgu, 0, 0, 0)], [wi], f"ffn_dx_{layer}", tm=FFN_ROW_TILE, res=dz, ln=ln_below)
    return below, g_in, g_out.reshape(N_CHIPS, FFN_HIDDEN // N_CHIPS, D_MODEL)


def kernel(x, even_w_in, even_b_f, even_conv_w, even_w_out, odd_w_in, odd_v_ln_g, odd_v_ln_b, odd_w_s, odd_b_s, odd_w_out, mix_ln_g, mix_ln_b, ffn_w_in, ffn_w_out, ffn_ln_g, ffn_ln_b, loss_target, m_even_w_in, m_even_b_f, m_even_conv_w, m_even_w_out, m_odd_w_in, m_odd_v_ln_g, m_odd_v_ln_b, m_odd_w_s, m_odd_b_s, m_odd_w_out, m_mix_ln_g, m_mix_ln_b, m_ffn_w_in, m_ffn_w_out, m_ffn_ln_g, m_ffn_ln_b, v_even_w_in, v_even_b_f, v_even_conv_w, v_even_w_out, v_odd_w_in, v_odd_v_ln_g, v_odd_v_ln_b, v_odd_w_s, v_odd_b_s, v_odd_w_out, v_mix_ln_g, v_mix_ln_b, v_ffn_w_in, v_ffn_w_out, v_ffn_ln_g, v_ffn_ln_b):
    t = x.shape[1]
    d = D_MODEL
    chip = 2 * lax.axis_index("x") + lax.axis_index("y")
    x2d = x[0]
    target = loss_target[0]

    small_shard = jnp.concatenate([odd_v_ln_g.reshape(2, LANES), odd_v_ln_b.reshape(2, LANES),
                                   even_conv_w.reshape(CONV_K, LANES), jnp.zeros((1, LANES), F32)], axis=0)
    first = [even_w_in[0].astype(BF16)]
    second = [even_w_out[0].astype(BF16), small_shard]
    later = [odd_w_in[0].astype(BF16), odd_w_out[0].astype(BF16), ffn_w_in[0].astype(BF16), ffn_w_in[1].astype(BF16),
             ffn_w_out[0].astype(BF16), ffn_w_out[1].astype(BF16)]
    first_h, first_tok = _split_start(first, "gather4", "gather_first_start")
    second_h, second_tok = _split_start(second, "gather4", "gather_second_start", after=first_tok)
    later_h, later_tok = _split_start(later, "gather4", "gather_later_start", after=second_tok)
    (g_ewi,) = _gathered(first_h, "gather_first_wait", later_tok)
    ewi = g_ewi.transpose(1, 0, 2).reshape(d, EVEN_IN)
    w_even_in = jnp.concatenate([ewi[:, :QKV], ewi[:, QKV + FOX_HEADS:], ewi[:, QKV:QKV + FOX_HEADS],
                                 jnp.zeros((d, LANES - FOX_HEADS), BF16)], axis=1)
    chunk_id = jnp.arange(GMLP_BLOCK) // CHUNK
    gmask = chunk_id[None, :] <= chunk_id[:, None]
    w_spatial = jnp.where(gmask[None], odd_w_s[0], 0.0).astype(BF16)
    bs_col = odd_b_s[0].T
    b_f_col = even_b_f.reshape(FOX_HEADS, 1)
    ln = lambda p, l: p[l:l + 1]

    qkv, bch, fl = _proj(x2d, w_even_in, [(0, QKV, BF16), (QKV, QKV + BCH, F32), (QKV + BCH, EVEN_IN_PAD, F32)], "even_proj")
    fl3 = fl[:, :FOX_HEADS].T.reshape(FOX_HEADS, t // LANES, LANES).transpose(1, 0, 2)
    c3 = _fgate_fwd(fl3, b_f_col)
    c_rows = c3.transpose(1, 0, 2).reshape(FOX_HEADS, t)
    qp, kp, vp, kt, vt = _attn_pack(qkv, c_rows.T)
    attn, lse = _attn_fwd(qp, kp, vt)
    g_ewo, g_small = _gathered(second_h, "gather_second_wait", attn)
    w_even_out = g_ewo.reshape(d, d)
    v_ln_g = g_small[:, 0:2].reshape(1, d)
    v_ln_b = g_small[:, 2:4].reshape(1, d)
    conv_w = g_small[:, 4:7].transpose(1, 0, 2).reshape(CONV_K, CONV_WIDTH)
    conv = _conv_fwd(bch, conv_w)
    x1_b, xh1, rs1 = _mm_res_ln([(attn, w_even_out[:FOX_WIDTH]), (conv, w_even_out[FOX_WIDTH:])], x2d,
                                ln(mix_ln_g, 0), ln(mix_ln_b, 0), "even_out_ln")
    w_odd_in, g_owo, w_fi0, w_fi1, g_fo0, g_fo1 = _gathered(later_h, "gather_later_wait", x1_b)
    w_odd_out = g_owo.reshape(d, d)
    w_ffn_in = [w_fi0, w_fi1]
    w_ffn_out = [g_fo0.reshape(FFN_HIDDEN, d), g_fo1.reshape(FFN_HIDDEN, d)]
    x2_b, ffn0 = _ffn_fwd((xh1, ln(mix_ln_g, 0), ln(mix_ln_b, 0)), x1_b, w_ffn_in[0], w_ffn_out[0],
                          ln(ffn_ln_g, 0), ln(ffn_ln_b, 0), 0)

    sv_odd, rs_odd, gated = _gmlp_fwd(x2_b, w_odd_in, v_ln_g, v_ln_b, w_spatial, bs_col)
    x3_b, xh3, rs3 = _mm_res_ln([(gated, w_odd_out)], (ffn0[3], ln(ffn_ln_g, 0), ln(ffn_ln_b, 0)),
                                ln(mix_ln_g, 1), ln(mix_ln_b, 1), "odd_out_ln")
    _, ffn1 = _ffn_fwd((xh3, ln(mix_ln_g, 1), ln(mix_ln_b, 1)), x3_b, w_ffn_in[1], w_ffn_out[1],
                       ln(ffn_ln_g, 1), ln(ffn_ln_b, 1), 1)

    sq, dz4, d_fg1, d_fb1 = _loss_ln_bwd(ffn1[3], ffn1[4], ln(ffn_ln_g, 1), ln(ffn_ln_b, 1), target)
    loss = lax.psum(0.5 / d * jnp.sum(sq), ("x", "y", "c"))
    (dz3, d_mg1, d_mb1), gi_f1, go_f1 = _ffn_bwd(dz4, ffn1, w_ffn_in[1], w_ffn_out[1], (xh3, rs3, ln(mix_ln_g, 1)), 1)

    dgated = _mm_nt([(dz3, 0, 0, d)], [w_odd_out], "odd_dgated")
    go_odd = _mm_tn(gated, dz3, "odd_dw_out", tn=d).reshape(N_CHIPS, 1, d // N_CHIPS, d)
    da_odd, dws, dbs_col, d_vg, d_vb = _gmlp_bwd(dgated, sv_odd, rs_odd, v_ln_g, v_ln_b, w_spatial, bs_col)
    gi_odd = _mm_tn(x2_b, da_odd, "odd_dw_in", tn=d // 2, stack_cols=True)[:, None]
    dz2, d_fg0, d_fb0 = _mm_nt([(da_odd, 0, 0, 0)], [w_odd_in], "odd_dx", res=dz3,
                               ln=(ffn0[3], ffn0[4], ln(ffn_ln_g, 0)))
    (dz1, d_mg0, d_mb0), gi_f0, go_f0 = _ffn_bwd(dz2, ffn0, w_ffn_in[0], w_ffn_out[0], (xh1, rs1, ln(mix_ln_g, 0)), 0)

    sent_early = [gi_odd, go_odd, gi_f0[:, None], gi_f1[:, None], go_f0[:, None], go_f1[:, None]]
    early_h, early_tok = _split_start(sent_early, "scatter4", "scatter_early_start")
    dmix = _mm_nt([(dz1, 0, 0, d)], [w_even_out], "even_dmix", after=early_tok)
    go_even = jnp.concatenate([_mm_tn(attn, dz1, "even_dw_out_attn", tn=d), _mm_tn(conv, dz1, "even_dw_out_conv", tn=d)],
                              axis=0).reshape(N_CHIPS, 1, d // N_CHIPS, d)
    dbch, dconv_w8 = _conv_bwd(bch, dmix, conv_w)
    qb, dob = _attn_bwd_prep(attn, dmix, qp, lse.reshape(FOX_HEADS, t).T)
    dqkv, dc_col = _attn_unpack(*_attn_bwd(qb, kp, vp, dob, kt))
    dc3 = dc_col.T.reshape(FOX_HEADS, t // LANES, LANES).transpose(1, 0, 2)
    dfl3, d_bf = _fgate_bwd(dc3, fl3, b_f_col)
    dfl = jnp.concatenate([dfl3.transpose(1, 0, 2).reshape(FOX_HEADS, t).T.astype(BF16),
                           jnp.zeros((t, LANES - FOX_HEADS), BF16)], axis=1)

    dws_masked = jnp.where(gmask[None], dws, 0.0)
    rep_names = ["odd_w_s", "odd_b_s", "mix_ln_g", "mix_ln_b", "ffn_ln_g", "ffn_ln_b", "even_b_f"]
    rep_grads = [dws_masked, dbs_col.T, jnp.concatenate([d_mg0, d_mg1]), jnp.concatenate([d_mb0, d_mb1]),
                 jnp.concatenate([d_fg0, d_fg1]), jnp.concatenate([d_fb0, d_fb1]), d_bf.reshape(1, FOX_HEADS)]
    rep_w = [(odd_w_s, m_odd_w_s, v_odd_w_s), (odd_b_s, m_odd_b_s, v_odd_b_s), (mix_ln_g, m_mix_ln_g, v_mix_ln_g),
             (mix_ln_b, m_mix_ln_b, v_mix_ln_b), (ffn_ln_g, m_ffn_ln_g, v_ffn_ln_g), (ffn_ln_b, m_ffn_ln_b, v_ffn_ln_b),
             (even_b_f, m_even_b_f, v_even_b_f)]
    rep_rows = [_to_rows(gr) for gr in rep_grads]
    n_rep = sum(r.shape[0] for r in rep_rows)
    pad_rep = (-n_rep) % SUBLANES
    dconv_w = dconv_w8[:CONV_K].reshape(CONV_K, N_CHIPS, LANES).transpose(1, 0, 2).reshape(N_CHIPS * CONV_K, LANES)
    packed = jnp.concatenate(rep_rows + [jnp.zeros((pad_rep, LANES), F32), d_vg.reshape(SUBLANES, LANES),
                                         d_vb.reshape(SUBLANES, LANES), dconv_w, jnp.zeros((4, LANES), F32)], axis=0)
    small_h, small_tok = _split_start([packed], "gather8", "gather_small_start")

    swap_h, swap_tok = _split_start(_scattered(early_h, "scatter_early_wait", small_tok), "swap2", "swap_early_start")
    dw_qkv = _mm_tn(dqkv, x2d, "even_dw_qkv", tn=d, tk=QKV // 2, after=swap_tok)
    dw_bch = _mm_tn(dbch, x2d, "even_dw_bch", tn=d, tk=BCH // 2)
    dw_f = _mm_tn(dfl, x2d, "even_dw_f", tn=d)
    gi_even = jnp.concatenate([dw_qkv, dw_f[:FOX_HEADS], dw_bch], axis=0).reshape(N_CHIPS, 1, -1, LANES)
    sent_late = [gi_even, go_even]
    late_h, late_tok = _split_start(sent_late, "scatter4", "scatter_late_start")
    grad_x = _mm_nt([(dqkv, 0, 0, QKV), (dbch, 0, QKV, QKV + BCH), (dfl, 0, QKV + BCH, EVEN_IN_PAD)], [w_even_in],
                    "even_dx", res=dz1, after=late_tok)
    mine, theirs = _split_wait(swap_h, "swap_early_wait", grad_x)
    res = {}
    res["odd_w_in"] = _adamw([mine[0], theirs[0]], odd_w_in, m_odd_w_in, v_odd_w_in, "adamw_odd_w_in")
    res["odd_w_out"] = _adamw([mine[1], theirs[1]], odd_w_out, m_odd_w_out, v_odd_w_out, "adamw_odd_w_out")
    for nm, at, (w, m, v) in (("ffn_w_in", 2, (ffn_w_in, m_ffn_w_in, v_ffn_w_in)),
                              ("ffn_w_out", 4, (ffn_w_out, m_ffn_w_out, v_ffn_w_out))):
        upper = _adamw([mine[at + 1], theirs[at + 1]], w, m, v, f"adamw_{nm}_1", layer=1)
        res[nm] = _adamw([mine[at], theirs[at]], w, m, v, f"adamw_{nm}_0", layer=0, into=upper)
    mine_late = _scattered(late_h, "scatter_late_wait", res["ffn_w_out"][0])
    theirs_late = _exchange(mine_late, "swap2", "swap_late")
    rows = lambda a: jnp.swapaxes(a, 1, 2).reshape(1, -1, LANES)
    back = lambda a: jnp.swapaxes(a.reshape(1, EVEN_IN // N_CHIPS, d), 1, 2)
    res["even_w_in"] = [back(o) for o in _adamw([mine_late[0], theirs_late[0]], rows(even_w_in), rows(m_even_w_in),
                                                rows(v_even_w_in), "adamw_even_w_in")]
    res["even_w_out"] = _adamw([mine_late[1], theirs_late[1]], even_w_out, m_even_w_out, v_even_w_out,
                               "adamw_even_w_out")
    (packed,), (gathered,) = _split_wait(small_h, "gather_small_wait", theirs_late[0])
    gathered = lax.dynamic_update_index_in_dim(gathered, packed, 4 * lax.axis_index("x") + 2 * lax.axis_index("y")
                                               + lax.axis_index("c"), 0)

    base = n_rep + pad_rep
    own_rows = jnp.concatenate([
        lax.dynamic_slice_in_dim(gathered, base + 2 * chip, 2, axis=1),
        lax.dynamic_slice_in_dim(gathered, base + SUBLANES + 2 * chip, 2, axis=1),
        lax.dynamic_slice_in_dim(gathered, base + 2 * SUBLANES + CONV_K * chip, CONV_K, axis=1),
        jnp.zeros((N_DEV, 1, LANES), F32)], axis=1)
    small_parts = jnp.concatenate([gathered[:, :base], own_rows], axis=1)[:, None]

    def pack_small(get):
        rows = [_to_rows(get(tw)) for tw in rep_w] + [jnp.zeros((pad_rep, LANES), F32)]
        rows += [get(sh).reshape(-1, LANES) for sh in ((odd_v_ln_g, m_odd_v_ln_g, v_odd_v_ln_g),
                                                       (odd_v_ln_b, m_odd_v_ln_b, v_odd_v_ln_b),
                                                       (even_conv_w, m_even_conv_w, v_even_conv_w))]
        return jnp.concatenate(rows + [jnp.zeros((1, LANES), F32)], axis=0)[None]

    small_out = _adamw([small_parts], pack_small(lambda tw: tw[0]), pack_small(lambda tw: tw[1]),
                       pack_small(lambda tw: tw[2]), "adamw_small")

    def unpack_small(rows3):
        rows = rows3[0]
        out, off = {}, 0
        for nm, (w, _, _), r in zip(rep_names, rep_w, rep_rows):
            out[nm] = rows[off:off + r.shape[0]].reshape(-1)[:w.size].reshape(w.shape)
            off += r.shape[0]
        off += pad_rep
        out["odd_v_ln_g"] = rows[off:off + 2].reshape(odd_v_ln_g.shape)
        out["odd_v_ln_b"] = rows[off + 2:off + 4].reshape(odd_v_ln_b.shape)
        out["even_conv_w"] = rows[off + 4:off + 4 + CONV_K].reshape(even_conv_w.shape)
        return out

    small = [unpack_small(o) for o in small_out]
    order = ["even_w_in", "even_b_f", "even_conv_w", "even_w_out", "odd_w_in", "odd_v_ln_g", "odd_v_ln_b", "odd_w_s",
             "odd_b_s", "odd_w_out", "mix_ln_g", "mix_ln_b", "ffn_w_in", "ffn_w_out", "ffn_ln_g", "ffn_ln_b"]
    outs = [loss, grad_x[None]]
    for kind in range(4):
        for nm in order:
            outs.append(res[nm][kind] if nm in res else small[kind][nm])
    return tuple(outs)
```

```python
import functools
import math

import jax
import jax.numpy as jnp
from jax import lax
from jax.experimental import pallas as pl
from jax.experimental.pallas import tpu as pltpu

F32 = jnp.float32
BF16 = jnp.bfloat16

D_MODEL = 1024
FOX_HEADS = 8
HEAD_DIM = 64
HEAD_PAIRS = FOX_HEADS // 2
FOX_WIDTH = FOX_HEADS * HEAD_DIM
CONV_WIDTH = 512
CONV_K = 3
QKV = 3 * FOX_WIDTH
BCH = 3 * CONV_WIDTH
EVEN_IN = QKV + FOX_HEADS + BCH
EVEN_IN_PAD = QKV + BCH + 128
GMLP_BLOCK = 128
GMLP_GROUPS = 8
CHUNK = 64
FFN_HIDDEN = 2816
HALF_HIDDEN = FFN_HIDDEN // 2
ALPHA = 4.0 ** 0.25
LN_EPS = 1e-5
ADAM_LR = 0.001
ADAM_B1 = 0.9
ADAM_B2 = 0.999
ADAM_EPS = 1e-08
ADAM_WD = 0.01
ADAM_STEP = 10
N_CHIPS = 4
N_DEV = 8
LANES = 128
SUBLANES = 8
ROW_TILE = 512
FFN_ROW_TILE = 512
REDUCE_TILE = 2048
ATT_BLOCK = 512
ATT_FWD_HEADS = 8
ATT_BWD_HEADS = 4
ADAMW_BLOCK_BYTES = 2 ** 20
VMEM_LIMIT = 56 * 2 ** 20
NEG = -1e30
MESH = pl.DeviceIdType.MESH
HIGHEST = lax.Precision.HIGHEST
Q_C, Q_ONE, Q_LSE = 64, 67, 70
K_ONE, K_C, K_ONE2 = 64, 67, 70
V_ONE = 64
DO_DELTA = 65
NT = (((1,), (1,)), ((), ()))
TN = (((0,), (0,)), ((), ()))


def _cp():
    return pltpu.CompilerParams(vmem_limit_bytes=VMEM_LIMIT)


def _resident(shape):
    zeros = (0,) * len(shape)
    return pl.BlockSpec(shape, lambda *_: zeros, pipeline_mode=pl.Buffered(1))


def _sds(shape, dtype):
    return jax.ShapeDtypeStruct(tuple(shape), dtype)


_MASKS = {
    "gather4": [(1, 0, 0), (0, 1, 0), (1, 1, 0)],
    "scatter4": [(1, 0, 0), (0, 1, 0), (1, 1, 0)],
    "swap2": [(0, 0, 1)],
    "gather8": [(0, 0, 1), (0, 1, 0), (0, 1, 1), (1, 0, 0), (1, 0, 1), (1, 1, 0), (1, 1, 1)],
}


def _exchange(arrs, mode, name):
    n = len(arrs)
    masks = _MASKS[mode]
    npeer = len(masks)
    lead = {"gather4": N_CHIPS, "gather8": N_DEV}.get(mode)
    out_shapes = [_sds(((lead,) if lead else ()) + a.shape, a.dtype) for a in arrs]

    def body(*refs):
        ins, outs = refs[:n], refs[n:2 * n]
        send_sems, recv_sems, loc_sems = refs[2 * n:]
        x, y, c = lax.axis_index("x"), lax.axis_index("y"), lax.axis_index("c")
        chip, dev = 2 * x + y, 4 * x + 2 * y + c
        sends, recvs, locs = [], [], []
        for k in range(n):
            if mode == "gather4":
                locs.append(pltpu.make_async_copy(ins[k], outs[k].at[chip], loc_sems.at[k]))
            elif mode == "scatter4":
                locs.append(pltpu.make_async_copy(ins[k].at[chip], outs[k].at[chip], loc_sems.at[k]))
            elif mode == "gather8":
                locs.append(pltpu.make_async_copy(ins[k], outs[k].at[dev], loc_sems.at[k]))
        for cp in locs:
            cp.start()
        for k in range(n):
            for j, (dx, dy, dc) in enumerate(masks):
                px = 1 - x if dx else x
                py = 1 - y if dy else y
                pc = 1 - c if dc else c
                pchip, pdev = 2 * px + py, 4 * px + 2 * py + pc
                if mode == "gather4":
                    src, dst, land = ins[k], outs[k].at[chip], outs[k].at[pchip]
                elif mode == "scatter4":
                    src, dst, land = ins[k].at[pchip], outs[k].at[chip], outs[k].at[pchip]
                elif mode == "swap2":
                    src, dst, land = ins[k], outs[k], outs[k]
                else:
                    src, dst, land = ins[k], outs[k].at[dev], outs[k].at[pdev]
                s = k * npeer + j
                kw = dict(send_sem=send_sems.at[s], recv_sem=recv_sems.at[s], device_id=(px, py, pc),
                          device_id_type=MESH)
                cp = pltpu.make_async_remote_copy(src_ref=src, dst_ref=dst, **kw)
                cp.start()
                sends.append(cp)
                recvs.append(pltpu.make_async_remote_copy(src_ref=src, dst_ref=land, **kw))
        for cp in recvs:
            cp.wait_recv()
        for cp in sends:
            cp.wait_send()
        for cp in locs:
            cp.wait()

    any_spec = pl.BlockSpec(memory_space=pl.ANY)
    outs = pl.pallas_call(
        body,
        out_shape=out_shapes,
        in_specs=[any_spec] * n,
        out_specs=[any_spec] * n,
        scratch_shapes=[pltpu.SemaphoreType.DMA((n * npeer,)), pltpu.SemaphoreType.DMA((n * npeer,)),
                        pltpu.SemaphoreType.DMA((max(n, 1),))],
        name=name,
    )(*arrs)
    return list(outs)


_HBM_SPEC = pl.BlockSpec(memory_space=pltpu.HBM)
_SEM_SPEC = pl.BlockSpec(memory_space=pltpu.SEMAPHORE)
_ANY_SPEC = pl.BlockSpec(memory_space=pl.ANY)
_EFFECT = pltpu.SideEffectType.DATAFLOW_SIDE_EFFECTING


def _split_copies(mode, ins, lands, send_sems, recv_sems):
    x, y, c = lax.axis_index("x"), lax.axis_index("y"), lax.axis_index("c")
    chip, dev = 2 * x + y, 4 * x + 2 * y + c
    masks = _MASKS[mode]
    out = []
    for k in range(len(ins)):
        for j, (dx, dy, dc) in enumerate(masks):
            px = 1 - x if dx else x
            py = 1 - y if dy else y
            pc = 1 - c if dc else c
            pchip, pdev = 2 * px + py, 4 * px + 2 * py + pc
            if mode == "gather4":
                src, dst, land = ins[k], lands[k].at[chip], lands[k].at[pchip]
            elif mode == "scatter4":
                src, dst, land = ins[k].at[pchip], lands[k].at[chip], lands[k].at[pchip]
            elif mode == "swap2":
                src, dst, land = ins[k], lands[k], lands[k]
            else:
                src, dst, land = ins[k], lands[k].at[dev], lands[k].at[pdev]
            s = k * len(masks) + j
            kw = dict(send_sem=send_sems.at[s], recv_sem=recv_sems.at[s], device_id=(px, py, pc), device_id_type=MESH)
            out.append((pltpu.make_async_remote_copy(src_ref=src, dst_ref=dst, **kw),
                        pltpu.make_async_remote_copy(src_ref=src, dst_ref=land, **kw)))
    return out


def _split_start(arrs, mode, name, after=None):
    n = len(arrs)
    nsem = n * len(_MASKS[mode])
    lead = {"gather4": (N_CHIPS,), "gather8": (N_DEV,)}.get(mode, ())
    land_shapes = [lead + a.shape for a in arrs]

    def body(*refs):
        ins, lands = refs[:n], refs[n:2 * n]
        outs = refs[2 * n + (after is not None):]
        for start, _ in _split_copies(mode, ins, lands, outs[0], outs[1]):
            start.start()
        outs[-1][...] = jnp.zeros(outs[-1].shape, F32)

    srcs = [pltpu.with_memory_space_constraint(a, pltpu.HBM) for a in arrs]
    empties = [pltpu.with_memory_space_constraint(lax.empty(s, a.dtype), pltpu.HBM) for s, a in zip(land_shapes, arrs)]
    res = pl.pallas_call(
        body, name=name,
        out_shape=(pltpu.SemaphoreType.DMA((nsem,)), pltpu.SemaphoreType.DMA((nsem,)),
                   *[pltpu.HBM(a.shape, a.dtype) for a in arrs],
                   *[pltpu.HBM(s, a.dtype) for s, a in zip(land_shapes, arrs)],
                   _sds((SUBLANES, LANES), F32)),
        in_specs=[_HBM_SPEC] * (2 * n) + ([_ANY_SPEC] if after is not None else []),
        out_specs=(_SEM_SPEC, _SEM_SPEC, *[_HBM_SPEC] * (2 * n), pl.BlockSpec(memory_space=pltpu.VMEM)),
        input_output_aliases={k: 2 + k for k in range(2 * n)},
        compiler_params=pltpu.CompilerParams(has_side_effects=_EFFECT),
    )(*srcs, *empties, *([after] if after is not None else []))
    return dict(mode=mode, n=n, sems=res[:2], bufs=res[2:2 + 2 * n]), res[-1]


def _split_wait(handle, name, after):
    n, mode = handle["n"], handle["mode"]

    def body(*refs):
        ins, lands = refs[:n], refs[n:2 * n]
        send_sems, recv_sems = refs[2 * n], refs[2 * n + 1]
        for _, arrival in _split_copies(mode, ins, lands, send_sems, recv_sems):
            arrival.wait_send()
            arrival.wait_recv()

    bufs = handle["bufs"]
    res = pl.pallas_call(
        body, name=name,
        out_shape=tuple(pltpu.HBM(b.shape, b.dtype) for b in bufs),
        in_specs=[_HBM_SPEC] * (2 * n) + [_SEM_SPEC, _SEM_SPEC, _ANY_SPEC],
        out_specs=tuple([_HBM_SPEC] * (2 * n)),
        input_output_aliases={k: k for k in range(2 * n)},
        compiler_params=pltpu.CompilerParams(has_side_effects=_EFFECT),
    )(*bufs, *handle["sems"], after)
    return list(res[:n]), list(res[n:])


def _with_own(landed, own):
    chip = 2 * lax.axis_index("x") + lax.axis_index("y")
    return lax.dynamic_update_index_in_dim(landed, own, chip, 0)


def _gathered(handle, name, after):
    sent, landed = _split_wait(handle, name, after)
    return [_with_own(g, own) for g, own in zip(landed, sent)]


def _scattered(handle, name, after):
    chip = 2 * lax.axis_index("x") + lax.axis_index("y")
    sent, landed = _split_wait(handle, name, after)
    return [_with_own(r, lax.dynamic_index_in_dim(g, chip, 0, keepdims=False)) for r, g in zip(landed, sent)]


def _sigmoid(x):
    return 0.5 * jnp.tanh(0.5 * x) + 0.5


def _log_sigmoid(x):
    e = jnp.exp(-jnp.abs(x))
    log1p = jnp.where(e < 1e-2, e * (1.0 - e * (0.5 - e * (1.0 / 3.0))), jnp.log(1.0 + e))
    return jnp.minimum(x, 0.0) - log1p


def _ln_fwd(z):
    mu = jnp.mean(z, axis=-1, keepdims=True)
    zc = z - mu
    var = jnp.mean(zc * zc, axis=-1, keepdims=True)
    rstd = lax.rsqrt(var + LN_EPS)
    return zc * rstd, rstd


def _ln_bwd(dy, xhat, rstd, g):
    dxh = dy * g
    m1 = jnp.mean(dxh, axis=-1, keepdims=True)
    m2 = jnp.mean(dxh * xhat, axis=-1, keepdims=True)
    dz = rstd * (dxh - m1 - xhat * m2)
    return dz, jnp.sum(dy * xhat, axis=0, keepdims=True), jnp.sum(dy, axis=0, keepdims=True)


def _shift_down(z, halo):
    r = lax.broadcasted_iota(jnp.int32, z.shape, 0)
    z1 = jnp.where(r == 0, halo[7:8, :], pltpu.roll(z, 1, 0))
    z2 = jnp.where(r == 0, halo[6:7, :], jnp.where(r == 1, halo[7:8, :], pltpu.roll(z, 2, 0)))
    return z1, z2


def _shift_up(z, halo):
    n = z.shape[0]
    r = lax.broadcasted_iota(jnp.int32, z.shape, 0)
    z1 = jnp.where(r == n - 1, halo[0:1, :], pltpu.roll(z, n - 1, 0))
    z2 = jnp.where(r == n - 1, halo[1:2, :], jnp.where(r == n - 2, halo[0:1, :], pltpu.roll(z, n - 2, 0)))
    return z1, z2


def _accumulate(ref, first, value):
    @pl.when(first)
    def _():
        ref[...] = value

    @pl.when(jnp.logical_not(first))
    def _():
        ref[...] += value


def _proj(x, w, splits, name):
    t, k = x.shape
    tm = min(ROW_TILE, t)

    def body(x_ref, w_ref, *outs):
        a = x_ref[...].astype(BF16)
        for (lo, hi, dt), o in zip(splits, outs):
            o[...] = jnp.dot(a, w_ref[:, lo:hi], preferred_element_type=F32).astype(dt)

    return pl.pallas_call(
        body, grid=(t // tm,),
        in_specs=[pl.BlockSpec((tm, k), lambda i: (i, 0)), _resident(w.shape)],
        out_specs=[pl.BlockSpec((tm, hi - lo), lambda i: (i, 0)) for lo, hi, _ in splits],
        out_shape=[_sds((t, hi - lo), dt) for lo, hi, dt in splits],
        compiler_params=_cp(), name=name)(x, w)


def _fgate_fwd(fl3, b_f):
    nc = fl3.shape[0]

    def body(f_ref, b_ref, c_ref):
        r = lax.broadcasted_iota(jnp.int32, (LANES, LANES), 0)
        cidx = lax.broadcasted_iota(jnp.int32, (LANES, LANES), 1)
        upper = (r <= cidx).astype(F32)

        def step(i, carry):
            lf = _log_sigmoid(f_ref[i] + b_ref[...])
            cc = jnp.dot(lf, upper, precision=HIGHEST, preferred_element_type=F32) + carry
            c_ref[i] = cc
            return cc[:, LANES - 1:LANES]

        lax.fori_loop(0, nc, step, jnp.zeros((FOX_HEADS, 1), F32))

    return pl.pallas_call(body, out_shape=_sds(fl3.shape, F32), name="fgate_fwd")(fl3, b_f)


def _fgate_bwd(dc3, fl3, b_f):
    nc = fl3.shape[0]

    def body(dc_ref, f_ref, b_ref, df_ref, db_ref):
        r = lax.broadcasted_iota(jnp.int32, (LANES, LANES), 0)
        cidx = lax.broadcasted_iota(jnp.int32, (LANES, LANES), 1)
        lower = (r >= cidx).astype(F32)

        def step(n, carry):
            suffix, db = carry
            i = nc - 1 - n
            dlf = jnp.dot(dc_ref[i], lower, precision=HIGHEST, preferred_element_type=F32) + suffix
            df = dlf * (1.0 - _sigmoid(f_ref[i] + b_ref[...]))
            df_ref[i] = df
            return dlf[:, 0:1], db + jnp.sum(df, axis=1, keepdims=True)

        zero = jnp.zeros((FOX_HEADS, 1), F32)
        _, db = lax.fori_loop(0, nc, step, (zero, zero))
        db_ref[...] = db

    return pl.pallas_call(body, out_shape=[_sds(fl3.shape, F32), _sds((FOX_HEADS, 1), F32)],
                          name="fgate_bwd")(dc3, fl3, b_f)


def _split3(c):
    hi = c.astype(BF16).astype(F32)
    mid = (c - hi).astype(BF16).astype(F32)
    lo = (c - hi - mid).astype(BF16).astype(F32)
    return hi, mid, lo


PIECE_ONE = 3 * FOX_HEADS


def _piece_rows(values):
    hi, mid, lo = _split3(values)
    lane = lax.broadcasted_iota(jnp.int32, values.shape, 1)
    row = hi + pltpu.roll(mid, FOX_HEADS, 1) + pltpu.roll(lo, 2 * FOX_HEADS, 1) + jnp.where(lane == PIECE_ONE, 1.0, 0.0)
    return row.astype(BF16)


def _piece_selector(start, sign, ones=()):
    sel = [[0.0] * FOX_WIDTH for _ in range(LANES)]
    for h in range(FOX_HEADS):
        for n in range(3):
            sel[n * FOX_HEADS + h][h * HEAD_DIM + start - HEAD_DIM + n] = sign
        for lane in ones:
            sel[PIECE_ONE][h * HEAD_DIM + lane - HEAD_DIM] = 1.0
    return jnp.asarray(sel, BF16)


def _attn_pack(qkv, c_pad):
    t = qkv.shape[0]
    tm = min(ROW_TILE, t)
    hd = HEAD_DIM
    sel_q = _piece_selector(Q_C, 1.0, range(Q_ONE, Q_ONE + 3))
    sel_k = _piece_selector(K_C, -1.0, [*range(K_ONE, K_ONE + 3), *range(K_ONE2, K_ONE2 + 3)])
    sel_v = _piece_selector(HEAD_DIM, 0.0, range(V_ONE, V_ONE + 4))

    def body(x_ref, c_ref, sq_ref, sk_ref, sv_ref, qp_ref, kp_ref, vp_ref, kt_ref, vt_ref):
        pieces = _piece_rows(c_ref[...])
        q_extra = jnp.dot(pieces, sq_ref[...], preferred_element_type=F32).astype(BF16)
        k_extra = jnp.dot(pieces, sk_ref[...], preferred_element_type=F32).astype(BF16)
        v_extra = jnp.dot(pieces, sv_ref[...], preferred_element_type=F32).astype(BF16)
        for h in range(FOX_HEADS):
            hs = slice(h * hd, (h + 1) * hd)
            qp_ref[h, :, :hd] = (x_ref[:, hs].astype(F32) * (hd ** -0.5)).astype(BF16)
            qp_ref[h, :, hd:] = q_extra[:, hs]
            kp_ref[h, :, :hd] = x_ref[:, FOX_WIDTH + h * hd:FOX_WIDTH + (h + 1) * hd]
            kp_ref[h, :, hd:] = k_extra[:, hs]
            vp_ref[h, :, :hd] = x_ref[:, 2 * FOX_WIDTH + h * hd:2 * FOX_WIDTH + (h + 1) * hd]
            vp_ref[h, :, hd:] = v_extra[:, hs]
            kt_ref[h] = kp_ref[h].T
            vt_ref[h] = vp_ref[h].T

    row3 = pl.BlockSpec((FOX_HEADS, tm, LANES), lambda i: (0, i, 0))
    col3 = pl.BlockSpec((FOX_HEADS, LANES, tm), lambda i: (0, 0, i))
    sel = _resident(sel_q.shape)
    return pl.pallas_call(
        body, grid=(t // tm,),
        in_specs=[pl.BlockSpec((tm, QKV), lambda i: (i, 0)), pl.BlockSpec((tm, LANES), lambda i: (i, 0)), sel, sel, sel],
        out_specs=[row3, row3, row3, col3, col3],
        out_shape=[_sds((FOX_HEADS, t, LANES), BF16)] * 3 + [_sds((FOX_HEADS, LANES, t), BF16)] * 2,
        compiler_params=_cp(), name="attn_pack")(qkv, c_pad, sel_q, sel_k, sel_v)


def _triangle(nq, key_major):
    if key_major:
        pairs = [(i, j) for j in range(nq) for i in range(j, nq)]
    else:
        pairs = [(i, j) for i in range(nq) for j in range(i + 1)]
    return jnp.asarray([p[0] for p in pairs], jnp.int32), jnp.asarray([p[1] for p in pairs], jnp.int32)


def _attn_fwd(qp, kp, vt):
    t = qp.shape[1]
    bq = min(ATT_BLOCK, t)
    nq = t // bq
    nh = ATT_FWD_HEADS
    i_tab, j_tab = _triangle(nq, key_major=False)

    def body(it_ref, jt_ref, q_ref, k_ref, vt_ref, o_ref, lse_ref, m_sc, acc_sc):
        s = pl.program_id(1)
        i, j = it_ref[s], jt_ref[s]

        @pl.when(j == 0)
        def _():
            m_sc[...] = jnp.full(m_sc.shape, NEG, F32)
            acc_sc[...] = jnp.zeros(acc_sc.shape, F32)

        def sweep(masked):
            scores = lambda h: lax.dot_general(k_ref[h], q_ref[h], NT, preferred_element_type=F32)

            def accumulate(h, pt, rescale):
                acc_sc[h] = rescale * acc_sc[h] + jnp.dot(vt_ref[h], pt, preferred_element_type=F32)

            ahead, behind = scores(0), None
            for h in range(nh):
                st = ahead
                if h + 1 < nh:
                    ahead = scores(h + 1)
                if behind is not None:
                    accumulate(*behind)
                if masked:
                    key = lax.broadcasted_iota(jnp.int32, (bq, bq), 0)
                    qry = lax.broadcasted_iota(jnp.int32, (bq, bq), 1)
                    st = jnp.where(key <= qry, st, NEG)
                m_prev = m_sc[h]
                m_new = jnp.maximum(m_prev, jnp.max(st, axis=0, keepdims=True))
                behind = (h, jnp.exp(st - m_new).astype(BF16), jnp.exp(m_prev - m_new))
                m_sc[h] = m_new
            accumulate(*behind)

        @pl.when(j < i)
        def _():
            sweep(False)

        @pl.when(j == i)
        def _():
            sweep(True)
            for h in range(nh):
                acc = acc_sc[h]
                denom = acc[V_ONE:V_ONE + 1, :]
                o_ref[:, h * HEAD_DIM:(h + 1) * HEAD_DIM] = (acc[:HEAD_DIM, :] / denom).T.astype(BF16)
                lse_ref[h] = m_sc[h] + jnp.log(denom)

    grid_spec = pltpu.PrefetchScalarGridSpec(
        num_scalar_prefetch=2, grid=(FOX_HEADS // nh, i_tab.shape[0]),
        in_specs=[pl.BlockSpec((nh, bq, LANES), lambda hp, s, it, jt: (hp, it[s], 0)),
                  pl.BlockSpec((nh, bq, LANES), lambda hp, s, it, jt: (hp, jt[s], 0)),
                  pl.BlockSpec((nh, LANES, bq), lambda hp, s, it, jt: (hp, 0, jt[s]))],
        out_specs=[pl.BlockSpec((bq, nh * HEAD_DIM), lambda hp, s, it, jt: (it[s], hp)),
                   pl.BlockSpec((nh, 1, bq), lambda hp, s, it, jt: (hp, 0, it[s]))],
        scratch_shapes=[pltpu.VMEM((nh, 1, bq), F32), pltpu.VMEM((nh, LANES, bq), F32)])
    return pl.pallas_call(body, grid_spec=grid_spec,
                          out_shape=[_sds((t, FOX_WIDTH), BF16), _sds((FOX_HEADS, 1, t), F32)],
                          compiler_params=_cp(), name="attn_fwd")(i_tab, j_tab, qp, kp, vt)


def _conv_fwd(bch, conv_w):
    t = bch.shape[0]
    tm = min(ROW_TILE, t)
    halo_blocks = tm // SUBLANES
    cw = CONV_WIDTH

    def body(cur_ref, prev_ref, w_ref, o_ref):
        i = pl.program_id(0)
        z = cur_ref[:, cw:2 * cw] * cur_ref[:, 2 * cw:]
        zp = jnp.where(i == 0, 0.0, prev_ref[:, cw:2 * cw] * prev_ref[:, 2 * cw:])
        z1, z2 = _shift_down(z, zp)
        y = w_ref[0:1, :] * z2 + w_ref[1:2, :] * z1 + w_ref[2:3, :] * z
        o_ref[...] = (cur_ref[:, :cw] * y).astype(BF16)

    return pl.pallas_call(
        body, grid=(t // tm,),
        in_specs=[pl.BlockSpec((tm, BCH), lambda i: (i, 0)),
                  pl.BlockSpec((SUBLANES, BCH), lambda i: (jnp.maximum(i * halo_blocks - 1, 0), 0)),
                  _resident(conv_w.shape)],
        out_specs=pl.BlockSpec((tm, cw), lambda i: (i, 0)),
        out_shape=_sds((t, cw), BF16), compiler_params=_cp(), name="conv_fwd")(bch, bch, conv_w)


def _mm_res_ln(pairs, res, g, b, name):
    from_ln = isinstance(res, tuple)
    res_args = list(res) if from_ln else [res]
    t, d = res_args[0].shape
    tm = min(ROW_TILE, t)
    n = len(pairs)

    def body(*refs):
        a_refs, w_refs = refs[:n], refs[n:2 * n]
        res_refs = refs[2 * n:2 * n + len(res_args)]
        g_ref, b_ref, yb_ref, xh_ref, rs_ref = refs[2 * n + len(res_args):]
        r = res_refs[0][...]
        if from_ln:
            r = r * res_refs[1][...] + res_refs[2][...]
        z = ALPHA * r
        for a_ref, w_ref in zip(a_refs, w_refs):
            z = z + jnp.dot(a_ref[...].astype(BF16), w_ref[...], preferred_element_type=F32)
        xhat, rstd = _ln_fwd(z)
        yb_ref[...] = (xhat * g_ref[...] + b_ref[...]).astype(BF16)
        xh_ref[...] = xhat
        rs_ref[...] = rstd

    row = lambda i: (i, 0)
    full = pl.BlockSpec((tm, d), row)
    return pl.pallas_call(
        body, grid=(t // tm,),
        in_specs=[pl.BlockSpec((tm, a.shape[1]), row) for a, _ in pairs] + [_resident(w.shape) for _, w in pairs]
        + [full] + [_resident(a.shape) for a in res_args[1:]] + [_resident(g.shape), _resident(b.shape)],
        out_specs=[full, full, pl.BlockSpec((tm, 1), row)],
        out_shape=[_sds((t, d), BF16), _sds((t, d), F32), _sds((t, 1), F32)],
        compiler_params=_cp(), name=name)(*[a for a, _ in pairs], *[w for _, w in pairs], *res_args, g, b)


def _ffn_in(x, wi, name):
    t, d = x.shape
    tm = min(FFN_ROW_TILE, t)
    hh = HALF_HIDDEN

    def body(x_ref, w_ref, gu_ref, h_ref):
        a = x_ref[...].astype(BF16)
        for c in range(2):
            gs, us = slice(c * hh, (c + 1) * hh), slice(FFN_HIDDEN + c * hh, FFN_HIDDEN + (c + 1) * hh)
            g = jnp.dot(a, w_ref[c], preferred_element_type=F32)
            u = jnp.dot(a, w_ref[2 + c], preferred_element_type=F32)
            sig = _sigmoid(g)
            silu = g * sig
            gu_ref[:, gs] = (u * sig * (1.0 + g * (1.0 - sig))).astype(BF16)
            gu_ref[:, us] = silu.astype(BF16)
            h_ref[:, gs] = (silu * u).astype(BF16)

    row = lambda i: (i, 0)
    return pl.pallas_call(
        body, grid=(t // tm,),
        in_specs=[pl.BlockSpec((tm, d), row), _resident(wi.shape)],
        out_specs=[pl.BlockSpec((tm, 2 * FFN_HIDDEN), row), pl.BlockSpec((tm, FFN_HIDDEN), row)],
        out_shape=[_sds((t, 2 * FFN_HIDDEN), BF16), _sds((t, FFN_HIDDEN), BF16)],
        compiler_params=_cp(), name=name)(x, wi)


def _gmlp_fwd(x, w_in, vg, vb, wm, bs_col):
    t, d = x.shape
    tm = min(ROW_TILE, t)
    gb = GMLP_BLOCK

    def body(x_ref, w_ref, vg_ref, vb_ref, wm_ref, bs_ref, sv_ref, rs_ref, o_ref, a_sc):
        xb = x_ref[...].astype(BF16)
        nc = w_ref.shape[2]
        for j in range(w_ref.shape[0]):
            a_sc[:, j * nc:(j + 1) * nc] = jnp.dot(xb, w_ref[j], preferred_element_type=F32)
        halves = []
        for half in range(2):
            a = a_sc[:, half * d:(half + 1) * d]
            cdf = 0.5 * (1.0 + lax.erf(a * (2.0 ** -0.5)))
            halves.append(a * cdf)
            slope = cdf + a * (jnp.exp(-0.5 * a * a) * (1.0 / math.sqrt(2.0 * math.pi)))
            sv_ref[:, (2 * half + 1) * d:(2 * half + 2) * d] = slope.astype(BF16)
        u = halves[0]
        vhat, rstd = _ln_fwd(halves[1])
        sv_ref[:, :d] = u.astype(BF16)
        sv_ref[:, 2 * d:3 * d] = vhat.astype(BF16)
        rs_ref[...] = rstd
        vln = (vhat * vg_ref[...] + vb_ref[...]).astype(BF16)
        for blk in range(tm // gb):
            rs = slice(blk * gb, (blk + 1) * gb)
            for gi in range(GMLP_GROUPS):
                cs = slice(gi * gb, (gi + 1) * gb)
                s = jnp.dot(wm_ref[gi], vln[rs, cs], preferred_element_type=F32) + bs_ref[:, gi:gi + 1]
                o_ref[rs, cs] = (u[rs, cs] * s).astype(BF16)

    row = lambda i: (i, 0)
    return pl.pallas_call(
        body, grid=(t // tm,),
        in_specs=[pl.BlockSpec((tm, d), row), _resident(w_in.shape), _resident(vg.shape), _resident(vb.shape),
                  _resident(wm.shape), _resident(bs_col.shape)],
        out_specs=[pl.BlockSpec((tm, 4 * d), row), pl.BlockSpec((tm, 1), row), pl.BlockSpec((tm, d), row)],
        out_shape=[_sds((t, 4 * d), BF16), _sds((t, 1), F32), _sds((t, d), BF16)],
        scratch_shapes=[pltpu.VMEM((tm, 2 * d), F32)],
        compiler_params=_cp(), name="gmlp_fwd")(x, w_in, vg, vb, wm, bs_col)


def _loss_ln_bwd(xhat, rstd, g, b, target):
    t, d = xhat.shape
    tm = min(ROW_TILE, t)

    def body(xh_ref, rs_ref, g_ref, b_ref, t_ref, sq_ref, dz_ref, dg_ref, db_ref):
        first = pl.program_id(0) == 0
        xh = xh_ref[...]
        err = xh * g_ref[...] + b_ref[...] - t_ref[...]
        dz, dg, db = _ln_bwd(err * (1.0 / d), xh, rs_ref[...], g_ref[...])
        dz_ref[...] = dz
        _accumulate(sq_ref, first, jnp.sum(err * err, axis=0, keepdims=True))
        _accumulate(dg_ref, first, dg)
        _accumulate(db_ref, first, db)

    row = lambda i: (i, 0)
    vec = pl.BlockSpec((1, d), lambda i: (0, 0))
    return pl.pallas_call(
        body, grid=(t // tm,),
        in_specs=[pl.BlockSpec((tm, d), row), pl.BlockSpec((tm, 1), row), _resident(g.shape), _resident(b.shape),
                  pl.BlockSpec((tm, d), row)],
        out_specs=[vec, pl.BlockSpec((tm, d), row), vec, vec],
        out_shape=[_sds((1, d), F32), _sds((t, d), F32), _sds((1, d), F32), _sds((1, d), F32)],
        compiler_params=_cp(), name="loss_ln_bwd")(xhat, rstd, g, b, target)


def _mm_nt(pairs, ws, name, *, tm=ROW_TILE, res=None, ln=None, out_dtype=F32, after=None):
    t = pairs[0][0].shape[0]
    k = ws[0].shape[-2]
    tm = min(tm, t)
    n, nw = len(pairs), len(ws)

    def body(*refs):
        refs = refs[after is not None:]
        a_refs, w_refs = refs[:n], refs[n:n + nw]
        rest = list(refs[n + nw:])
        dx = None
        for a_ref, (_, wi, lo, hi) in zip(a_refs, pairs):
            w_ref = w_refs[wi]
            if len(w_ref.shape) == 3:
                nc = w_ref.shape[2]
                parts = [lax.dot_general(a_ref[:, j * nc:(j + 1) * nc].astype(BF16), w_ref[j], NT,
                                         preferred_element_type=F32) for j in range(w_ref.shape[0])]
            else:
                parts = [lax.dot_general(a_ref[...].astype(BF16), w_ref[:, lo:hi], NT, preferred_element_type=F32)]
            for part in parts:
                dx = part if dx is None else dx + part
        if res is not None:
            dx = dx + ALPHA * rest.pop(0)[...]
        if ln is None:
            rest[0][...] = dx.astype(out_dtype)
            return
        xh_ref, rs_ref, g_ref, dz_ref, dg_ref, db_ref = rest
        first = pl.program_id(0) == 0
        dz, dg, db = _ln_bwd(dx, xh_ref[...], rs_ref[...], g_ref[...])
        dz_ref[...] = dz
        _accumulate(dg_ref, first, dg)
        _accumulate(db_ref, first, db)

    row = lambda i: (i, 0)
    in_specs = [pl.BlockSpec((tm, a.shape[1]), row) for a, _, _, _ in pairs] + [_resident(w.shape) for w in ws]
    args = [a for a, _, _, _ in pairs] + list(ws)
    if res is not None:
        in_specs.append(pl.BlockSpec((tm, k), row))
        args.append(res)
    if ln is None:
        out_specs = pl.BlockSpec((tm, k), row)
        out_shape = _sds((t, k), out_dtype)
    else:
        xhat, rstd, g = ln
        in_specs += [pl.BlockSpec((tm, k), row), pl.BlockSpec((tm, 1), row), _resident(g.shape)]
        args += [xhat, rstd, g]
        vec = pl.BlockSpec((1, k), lambda i: (0, 0))
        out_specs = [pl.BlockSpec((tm, k), row), vec, vec]
        out_shape = [_sds((t, k), F32), _sds((1, k), F32), _sds((1, k), F32)]
    if after is not None:
        in_specs.insert(0, _ANY_SPEC)
        args.insert(0, after)
    return pl.pallas_call(body, grid=(t // tm,), in_specs=in_specs, out_specs=out_specs, out_shape=out_shape,
                          compiler_params=_cp(), name=name)(*args)


def _mm_tn(a, b, name, *, tn, tk=None, tt=None, stack_cols=False, out_dtype=BF16, after=None):
    t, k = a.shape
    n = b.shape[1]
    tk = k if tk is None else tk
    tt = min(REDUCE_TILE if tt is None else tt, t)
    nt = t // tt

    def body(a_ref, b_ref, *rest):
        o_ref, acc_ref = rest[after is not None:]
        s = pl.program_id(2)
        part = lax.dot_general(a_ref[...].astype(BF16), b_ref[...].astype(BF16), TN, preferred_element_type=F32)
        _accumulate(acc_ref, s == 0, part)

        @pl.when(s == nt - 1)
        def _():
            o_ref[...] = acc_ref[...].astype(out_dtype).reshape(o_ref.shape)

    if stack_cols:
        assert tk == k
        out_spec = pl.BlockSpec((1, k, tn), lambda kk, j, s: (j, 0, 0))
        out_shape = _sds((n // tn, k, tn), out_dtype)
    else:
        out_spec = pl.BlockSpec((tk, tn), lambda kk, j, s: (kk, j))
        out_shape = _sds((k, n), out_dtype)
    return pl.pallas_call(
        body, grid=(k // tk, n // tn, nt),
        in_specs=[pl.BlockSpec((tt, tk), lambda kk, j, s: (s, kk)), pl.BlockSpec((tt, tn), lambda kk, j, s: (s, j))]
        + ([_ANY_SPEC] if after is not None else []),
        out_specs=out_spec, out_shape=out_shape,
        scratch_shapes=[pltpu.VMEM((tk, tn), F32)],
        compiler_params=_cp(), name=name)(a, b, *([after] if after is not None else []))


def _ffn_bwd_hidden(dz, wo, gu, name):
    t, d = dz.shape
    tm = min(FFN_ROW_TILE, t)
    hh = HALF_HIDDEN

    def body(dz_ref, w_ref, gu_ref, o_ref):
        a = dz_ref[...].astype(BF16)
        for c in range(2):
            gs, us = slice(c * hh, (c + 1) * hh), slice(FFN_HIDDEN + c * hh, FFN_HIDDEN + (c + 1) * hh)
            dh = lax.dot_general(a, w_ref[gs, :], NT, preferred_element_type=F32)
            o_ref[:, gs] = (dh * gu_ref[:, gs].astype(F32)).astype(BF16)
            o_ref[:, us] = (dh * gu_ref[:, us].astype(F32)).astype(BF16)

    row = lambda i: (i, 0)
    return pl.pallas_call(
        body, grid=(t // tm,),
        in_specs=[pl.BlockSpec((tm, d), row), _resident(wo.shape), pl.BlockSpec((tm, 2 * FFN_HIDDEN), row)],
        out_specs=pl.BlockSpec((tm, 2 * FFN_HIDDEN), row),
        out_shape=_sds((t, 2 * FFN_HIDDEN), BF16), compiler_params=_cp(), name=name)(dz, wo, gu)


def _gmlp_bwd(dgated, saved, rstd_v, vg, vb, wm, bs_col):
    t, d = dgated.shape
    d2 = 2 * d
    tm = min(ROW_TILE, t)
    gb = GMLP_BLOCK

    def body(dg_ref, sv_ref, rs_ref, vg_ref, vb_ref, wm_ref, bs_ref, da_ref, dws_ref, dbs_ref, dvg_ref, dvb_ref, dvln_sc):
        first = pl.program_id(0) == 0
        u = sv_ref[:, :d].astype(F32)
        vhat = sv_ref[:, 2 * d:3 * d].astype(F32)
        rstd = rs_ref[...]
        vln = (vhat * vg_ref[...] + vb_ref[...]).astype(BF16)
        dgate = dg_ref[...]

        @pl.when(first)
        def _():
            dws_ref[...] = jnp.zeros(dws_ref.shape, F32)
            dbs_ref[...] = jnp.zeros(dbs_ref.shape, F32)

        for blk in range(tm // gb):
            rs = slice(blk * gb, (blk + 1) * gb)
            for gi in range(GMLP_GROUPS):
                cs = slice(gi * gb, (gi + 1) * gb)
                vblk = vln[rs, cs]
                s = jnp.dot(wm_ref[gi], vblk, preferred_element_type=F32) + bs_ref[:, gi:gi + 1]
                dgb = dgate[rs, cs]
                da_ref[rs, cs] = (dgb * s * sv_ref[rs, d + gi * gb:d + (gi + 1) * gb].astype(F32)).astype(BF16)
                ds = dgb * u[rs, cs]
                dsb = ds.astype(BF16)
                dws_ref[gi] += lax.dot_general(dsb, vblk, NT, preferred_element_type=F32)
                dbs_ref[:, gi:gi + 1] += jnp.sum(ds, axis=1, keepdims=True)
                dvln_sc[rs, cs] = lax.dot_general(wm_ref[gi], dsb, TN, preferred_element_type=F32)
        dv, dvg, dvb = _ln_bwd(dvln_sc[...], vhat, rstd, vg_ref[...])
        da_ref[:, d:] = (dv * sv_ref[:, 3 * d:].astype(F32)).astype(BF16)
        _accumulate(dvg_ref, first, dvg)
        _accumulate(dvb_ref, first, dvb)

    row = lambda i: (i, 0)
    vec = pl.BlockSpec((1, d), lambda i: (0, 0))
    return pl.pallas_call(
        body, grid=(t // tm,),
        in_specs=[pl.BlockSpec((tm, d), row), pl.BlockSpec((tm, 4 * d), row), pl.BlockSpec((tm, 1), row),
                  _resident(vg.shape), _resident(vb.shape), _resident(wm.shape), _resident(bs_col.shape)],
        out_specs=[pl.BlockSpec((tm, d2), row), pl.BlockSpec(wm.shape, lambda i: (0, 0, 0)),
                   pl.BlockSpec(bs_col.shape, lambda i: (0, 0)), vec, vec],
        out_shape=[_sds((t, d2), BF16), _sds(wm.shape, F32), _sds(bs_col.shape, F32), _sds((1, d), F32), _sds((1, d), F32)],
        scratch_shapes=[pltpu.VMEM((tm, d), F32)],
        compiler_params=_cp(), name="gmlp_bwd")(dgated, saved, rstd_v, vg, vb, wm, bs_col)


def _conv_bwd(bch, dmix, conv_w):
    t = bch.shape[0]
    tm = min(ROW_TILE, t)
    nb = t // tm
    halo_blocks = tm // SUBLANES
    cw = CONV_WIDTH

    def body(cur_ref, prev_ref, next_ref, dc_ref, dn_ref, w_ref, o_ref, dw_ref):
        i = pl.program_id(0)
        bgate, cgate, hval = cur_ref[:, :cw], cur_ref[:, cw:2 * cw], cur_ref[:, 2 * cw:]
        z = cgate * hval
        zp = jnp.where(i == 0, 0.0, prev_ref[:, cw:2 * cw] * prev_ref[:, 2 * cw:])
        z1, z2 = _shift_down(z, zp)
        w0, w1, w2 = w_ref[0:1, :], w_ref[1:2, :], w_ref[2:3, :]
        dconv = dc_ref[...]
        o_ref[:, :cw] = (dconv * (w0 * z2 + w1 * z1 + w2 * z)).astype(BF16)
        dy = dconv * bgate
        dyn = jnp.where(i == nb - 1, 0.0, dn_ref[...] * next_ref[:, :cw])
        dy1, dy2 = _shift_up(dy, dyn)
        dz = w2 * dy + w1 * dy1 + w0 * dy2
        o_ref[:, cw:2 * cw] = (dz * hval).astype(BF16)
        o_ref[:, 2 * cw:] = (dz * cgate).astype(BF16)

        @pl.when(i == 0)
        def _():
            dw_ref[...] = jnp.zeros(dw_ref.shape, F32)

        for tap, zs in enumerate((z2, z1, z)):
            dw_ref[tap:tap + 1, :] += jnp.sum(dy * zs, axis=0, keepdims=True)

    last_halo = t // SUBLANES - 1
    return pl.pallas_call(
        body, grid=(nb,),
        in_specs=[pl.BlockSpec((tm, BCH), lambda i: (i, 0)),
                  pl.BlockSpec((SUBLANES, BCH), lambda i: (jnp.maximum(i * halo_blocks - 1, 0), 0)),
                  pl.BlockSpec((SUBLANES, BCH), lambda i: (jnp.minimum((i + 1) * halo_blocks, last_halo), 0)),
                  pl.BlockSpec((tm, cw), lambda i: (i, 1)),
                  pl.BlockSpec((SUBLANES, cw), lambda i: (jnp.minimum((i + 1) * halo_blocks, last_halo), 1)),
                  _resident(conv_w.shape)],
        out_specs=[pl.BlockSpec((tm, BCH), lambda i: (i, 0)), pl.BlockSpec((SUBLANES, cw), lambda i: (0, 0))],
        out_shape=[_sds((t, BCH), BF16), _sds((SUBLANES, cw), F32)],
        compiler_params=_cp(), name="conv_bwd")(bch, bch, bch, dmix, dmix, conv_w)


def _attn_bwd_prep(o, dmix, qp, lse_pad):
    t = o.shape[0]
    tm = min(ROW_TILE, t)
    hd = HEAD_DIM
    sel_lse = _piece_selector(Q_LSE, -1.0)
    sel_delta = _piece_selector(DO_DELTA, -1.0)
    head_of = jnp.asarray([[1.0 if col == row // hd else 0.0 for col in range(LANES)] for row in range(FOX_WIDTH)], F32)

    def body(o_ref, do_ref, qp_ref, lse_ref, sl_ref, sd_ref, seg_ref, qb_ref, dob_ref):
        do = do_ref[...]
        delta = jnp.dot(o_ref[...].astype(F32) * do, seg_ref[...], precision=HIGHEST, preferred_element_type=F32)
        lse_extra = jnp.dot(_piece_rows(lse_ref[...]), sl_ref[...], preferred_element_type=F32)
        do_extra = jnp.dot(_piece_rows(delta), sd_ref[...], preferred_element_type=F32).astype(BF16)
        for h in range(FOX_HEADS):
            hs = slice(h * hd, (h + 1) * hd)
            dob_ref[h, :, :hd] = do[:, hs].astype(BF16)
            dob_ref[h, :, hd:] = do_extra[:, hs]
            qb_ref[h, :, :hd] = qp_ref[h, :, :hd]
            qb_ref[h, :, hd:] = (qp_ref[h, :, hd:].astype(F32) + lse_extra[:, hs]).astype(BF16)

    row3 = pl.BlockSpec((FOX_HEADS, tm, LANES), lambda i: (0, i, 0))
    return pl.pallas_call(
        body, grid=(t // tm,),
        in_specs=[pl.BlockSpec((tm, FOX_WIDTH), lambda i: (i, 0)), pl.BlockSpec((tm, FOX_WIDTH), lambda i: (i, 0)), row3,
                  pl.BlockSpec((tm, LANES), lambda i: (i, 0)), _resident(sel_lse.shape), _resident(sel_delta.shape),
                  _resident(head_of.shape)],
        out_specs=[row3, row3], out_shape=[_sds((FOX_HEADS, t, LANES), BF16)] * 2,
        compiler_params=_cp(), name="attn_bwd_prep")(o, dmix, qp, lse_pad, sel_lse, sel_delta, head_of)


def _attn_bwd(qb, kp, vp, dob, kt):
    t = qb.shape[1]
    bq = min(ATT_BLOCK, t)
    nq = t // bq
    i_tab, j_tab = _triangle(nq, key_major=True)

    def body(it_ref, jt_ref, q_ref, k_ref, v_ref, do_ref, kt_ref, dqt_ref, dk_ref, dv_ref, dk_sc, dv_sc):
        s = pl.program_id(1)
        i, j = it_ref[s], jt_ref[s]

        @pl.when(s == 0)
        def _():
            dqt_ref[...] = jnp.zeros(dqt_ref.shape, F32)

        @pl.when(i == j)
        def _():
            dk_sc[...] = jnp.zeros(dk_sc.shape, F32)
            dv_sc[...] = jnp.zeros(dv_sc.shape, F32)

        cols = pl.ds(pl.multiple_of(i * bq, bq), bq)

        def sweep(masked):
            def scores(h):
                return (lax.dot_general(k_ref[h], q_ref[h], NT, preferred_element_type=F32),
                        lax.dot_general(v_ref[h], do_ref[h], NT, preferred_element_type=F32))

            def accumulate(h, ptb, dstb):
                dv_sc[h] += jnp.dot(ptb, do_ref[h], preferred_element_type=F32)
                dk_sc[h] += jnp.dot(dstb, q_ref[h], preferred_element_type=F32)
                dqt_ref[h, :, cols] += jnp.dot(kt_ref[h], dstb, preferred_element_type=F32)

            ahead, behind = scores(0), None
            for h in range(ATT_BWD_HEADS):
                st, dpt = ahead
                if h + 1 < ATT_BWD_HEADS:
                    ahead = scores(h + 1)
                if behind is not None:
                    accumulate(*behind)
                if masked:
                    key = lax.broadcasted_iota(jnp.int32, (bq, bq), 0)
                    qry = lax.broadcasted_iota(jnp.int32, (bq, bq), 1)
                    st = jnp.where(key <= qry, st, NEG)
                pt = jnp.exp(st)
                behind = (h, pt.astype(BF16), (pt * dpt).astype(BF16))
            accumulate(*behind)

        @pl.when(i == j)
        def _():
            sweep(True)

        @pl.when(i > j)
        def _():
            sweep(False)

        @pl.when(i == nq - 1)
        def _():
            dk_ref[...] = dk_sc[...]
            dv_ref[...] = dv_sc[...].astype(BF16)

    nh = ATT_BWD_HEADS
    qblk = pl.BlockSpec((nh, bq, LANES), lambda hp, s, it, jt: (hp, it[s], 0))
    kblk = pl.BlockSpec((nh, bq, LANES), lambda hp, s, it, jt: (hp, jt[s], 0))
    grid_spec = pltpu.PrefetchScalarGridSpec(
        num_scalar_prefetch=2, grid=(FOX_HEADS // nh, i_tab.shape[0]),
        in_specs=[qblk, kblk, kblk, qblk, pl.BlockSpec((nh, LANES, bq), lambda hp, s, it, jt: (hp, 0, jt[s]))],
        out_specs=[pl.BlockSpec((nh, LANES, t), lambda hp, s, it, jt: (hp, 0, 0), pipeline_mode=pl.Buffered(1)),
                   kblk, kblk],
        scratch_shapes=[pltpu.VMEM((nh, bq, LANES), F32), pltpu.VMEM((nh, bq, LANES), F32)])
    return pl.pallas_call(body, grid_spec=grid_spec,
                          out_shape=[_sds((FOX_HEADS, LANES, t), F32), _sds((FOX_HEADS, t, LANES), F32),
                                     _sds((FOX_HEADS, t, LANES), BF16)],
                          compiler_params=_cp(), name="attn_bwd")(i_tab, j_tab, qb, kp, vp, dob, kt)


def _attn_unpack(dqt, dkp, dvp):
    t = dkp.shape[1]
    tm = min(ROW_TILE, t)
    hd = HEAD_DIM

    def body(dqt_ref, dk_ref, dv_ref, o_ref, dc_ref):
        for h in range(FOX_HEADS):
            dq = dqt_ref[h].T
            o_ref[:, h * hd:(h + 1) * hd] = (dq[:, :hd] * (hd ** -0.5)).astype(BF16)
            o_ref[:, FOX_WIDTH + h * hd:FOX_WIDTH + (h + 1) * hd] = dk_ref[h, :, :hd].astype(BF16)
            o_ref[:, 2 * FOX_WIDTH + h * hd:2 * FOX_WIDTH + (h + 1) * hd] = dv_ref[h, :, :hd]
            dc_ref[:, h:h + 1] = dq[:, K_ONE:K_ONE + 1] - dk_ref[h, :, Q_ONE:Q_ONE + 1]

    row3 = pl.BlockSpec((FOX_HEADS, tm, LANES), lambda i: (0, i, 0))
    return pl.pallas_call(
        body, grid=(t // tm,),
        in_specs=[pl.BlockSpec((FOX_HEADS, LANES, tm), lambda i: (0, 0, i)), row3, row3],
        out_specs=[pl.BlockSpec((tm, QKV), lambda i: (i, 0)), pl.BlockSpec((tm, FOX_HEADS), lambda i: (i, 0))],
        out_shape=[_sds((t, QKV), BF16), _sds((t, FOX_HEADS), F32)],
        compiler_params=_cp(), name="attn_unpack")(dqt, dkp, dvp)


def _adamw(parts, w, m, v, name, layer=None, into=None):
    nl, r, c = w.shape
    fits = [cand for cand in [*range(SUBLANES, r, SUBLANES), r] if r % cand == 0 and cand * c * 4 <= ADAMW_BLOCK_BYTES]
    tr = max(fits) if fits else r
    npart = len(parts)
    bc1 = 1.0 - ADAM_B1 ** ADAM_STEP
    bc2 = 1.0 - ADAM_B2 ** ADAM_STEP

    def body(*refs):
        p_refs = refs[:npart]
        w_ref, m_ref, v_ref = refs[npart:npart + 3]
        g_ref, d_ref, nm_ref, nv_ref = refs[-4:]
        sums = []
        for p_ref in p_refs:
            acc = p_ref[0, 0].astype(F32)
            for s in range(1, p_ref.shape[0]):
                acc = acc + p_ref[s, 0].astype(F32)
            sums.append(acc)
        g = sums[0]
        for extra in sums[1:]:
            g = g + extra
        nm = ADAM_B1 * m_ref[0] + (1.0 - ADAM_B1) * g
        nv = ADAM_B2 * v_ref[0] + (1.0 - ADAM_B2) * (g * g)
        m_hat = nm / bc1
        v_hat = nv / bc2
        g_ref[0] = g
        d_ref[0] = -ADAM_LR * (m_hat / (jnp.sqrt(v_hat) + ADAM_EPS) + ADAM_WD * w_ref[0])
        nm_ref[0] = nm
        nv_ref[0] = nv

    first = 0 if layer is None else layer
    blk = pl.BlockSpec((1, tr, c), lambda l, i: (first + l, i, 0))
    extra = [] if into is None else list(into)
    return pl.pallas_call(
        body, grid=(nl if layer is None else 1, r // tr),
        in_specs=[pl.BlockSpec((p.shape[0], 1, tr, c), lambda l, i: (0, l, i, 0)) for p in parts] + [blk, blk, blk]
        + [_ANY_SPEC] * len(extra),
        out_specs=[blk] * 4, out_shape=[_sds(w.shape, F32)] * 4,
        input_output_aliases={npart + 3 + k: k for k in range(len(extra))},
        compiler_params=_cp(), name=name)(*parts, w, m, v, *extra)


def _to_rows(a):
    flat = a.reshape(-1)
    pad = (-flat.shape[0]) % LANES
    if pad:
        flat = jnp.concatenate([flat, jnp.zeros((pad,), flat.dtype)])
    return flat.reshape(-1, LANES)


def _by_owner_cols(dw):
    k, n = dw.shape
    return dw.reshape(k, N_CHIPS, n // N_CHIPS).transpose(1, 0, 2)[:, None]


def _ffn_fwd(xin_ln, xin_b, wi, wo, g, b, layer):
    gu, h = _ffn_in(xin_b, wi, f"ffn_in_{layer}")
    y_b, xhat, rstd = _mm_res_ln([(h, wo)], xin_ln, g, b, f"ffn_out_ln_{layer}")
    return y_b, (xin_b, gu, h, xhat, rstd)


def _ffn_bwd(dz, saved, wi, wo, ln_below, layer):
    xin_b, gu, h, _, _ = saved
    dgu = _ffn_bwd_hidden(dz, wo, gu, f"ffn_bwd_hidden_{layer}")
    g_out = _mm_tn(h, dz, f"ffn_dw_out_{layer}", tn=D_MODEL, tk=HALF_HIDDEN, tt=REDUCE_TILE // 2)
    g_in = _mm_tn(xin_b, dgu, f"ffn_dw_in_{layer}", tn=HALF_HIDDEN, stack_cols=True)
    below = _mm_nt([(dgu, 0, 0, 0)], [wi], f"ffn_dx_{layer}", tm=FFN_ROW_TILE, res=dz, ln=ln_below)
    return below, g_in, g_out.reshape(N_CHIPS, FFN_HIDDEN // N_CHIPS, D_MODEL)


def kernel(x, even_w_in, even_b_f, even_conv_w, even_w_out, odd_w_in, odd_v_ln_g, odd_v_ln_b, odd_w_s, odd_b_s, odd_w_out, mix_ln_g, mix_ln_b, ffn_w_in, ffn_w_out, ffn_ln_g, ffn_ln_b, loss_target, m_even_w_in, m_even_b_f, m_even_conv_w, m_even_w_out, m_odd_w_in, m_odd_v_ln_g, m_odd_v_ln_b, m_odd_w_s, m_odd_b_s, m_odd_w_out, m_mix_ln_g, m_mix_ln_b, m_ffn_w_in, m_ffn_w_out, m_ffn_ln_g, m_ffn_ln_b, v_even_w_in, v_even_b_f, v_even_conv_w, v_even_w_out, v_odd_w_in, v_odd_v_ln_g, v_odd_v_ln_b, v_odd_w_s, v_odd_b_s, v_odd_w_out, v_mix_ln_g, v_mix_ln_b, v_ffn_w_in, v_ffn_w_out, v_ffn_ln_g, v_ffn_ln_b):
    t = x.shape[1]
    d = D_MODEL
    chip = 2 * lax.axis_index("x") + lax.axis_index("y")
    x2d = x[0]
    target = loss_target[0]

    small_shard = jnp.concatenate([odd_v_ln_g.reshape(2, LANES), odd_v_ln_b.reshape(2, LANES),
                                   even_conv_w.reshape(CONV_K, LANES), jnp.zeros((1, LANES), F32)], axis=0)
    first = [even_w_in[0].astype(BF16)]
    second = [even_w_out[0].astype(BF16), small_shard]
    later = [odd_w_in[0].astype(BF16), odd_w_out[0].astype(BF16), ffn_w_in[0].astype(BF16), ffn_w_in[1].astype(BF16),
             ffn_w_out[0].astype(BF16), ffn_w_out[1].astype(BF16)]
    first_h, first_tok = _split_start(first, "gather4", "gather_first_start")
    second_h, second_tok = _split_start(second, "gather4", "gather_second_start", after=first_tok)
    later_h, later_tok = _split_start(later, "gather4", "gather_later_start", after=second_tok)
    (g_ewi,) = _gathered(first_h, "gather_first_wait", later_tok)
    ewi = g_ewi.transpose(1, 0, 2).reshape(d, EVEN_IN)
    w_even_in = jnp.concatenate([ewi[:, :QKV], ewi[:, QKV + FOX_HEADS:], ewi[:, QKV:QKV + FOX_HEADS],
                                 jnp.zeros((d, LANES - FOX_HEADS), BF16)], axis=1)
    chunk_id = jnp.arange(GMLP_BLOCK) // CHUNK
    gmask = chunk_id[None, :] <= chunk_id[:, None]
    w_spatial = jnp.where(gmask[None], odd_w_s[0], 0.0).astype(BF16)
    bs_col = odd_b_s[0].T
    b_f_col = even_b_f.reshape(FOX_HEADS, 1)
    ln = lambda p, l: p[l:l + 1]

    qkv, bch, fl = _proj(x2d, w_even_in, [(0, QKV, BF16), (QKV, QKV + BCH, F32), (QKV + BCH, EVEN_IN_PAD, F32)], "even_proj")
    fl3 = fl[:, :FOX_HEADS].T.reshape(FOX_HEADS, t // LANES, LANES).transpose(1, 0, 2)
    c3 = _fgate_fwd(fl3, b_f_col)
    c_rows = c3.transpose(1, 0, 2).reshape(FOX_HEADS, t)
    head_lanes = lambda rows: jnp.pad(rows.T, ((0, 0), (0, LANES - FOX_HEADS)))
    qp, kp, vp, kt, vt = _attn_pack(qkv, head_lanes(c_rows))
    attn, lse = _attn_fwd(qp, kp, vt)
    g_ewo, g_small = _gathered(second_h, "gather_second_wait", attn)
    w_even_out = g_ewo.reshape(d, d)
    v_ln_g = g_small[:, 0:2].reshape(1, d)
    v_ln_b = g_small[:, 2:4].reshape(1, d)
    conv_w = g_small[:, 4:7].transpose(1, 0, 2).reshape(CONV_K, CONV_WIDTH)
    conv = _conv_fwd(bch, conv_w)
    x1_b, xh1, rs1 = _mm_res_ln([(attn, w_even_out[:FOX_WIDTH]), (conv, w_even_out[FOX_WIDTH:])], x2d,
                                ln(mix_ln_g, 0), ln(mix_ln_b, 0), "even_out_ln")
    w_odd_in, g_owo, w_fi0, w_fi1, g_fo0, g_fo1 = _gathered(later_h, "gather_later_wait", x1_b)
    w_odd_out = g_owo.reshape(d, d)
    w_ffn_in = [w_fi0, w_fi1]
    w_ffn_out = [g_fo0.reshape(FFN_HIDDEN, d), g_fo1.reshape(FFN_HIDDEN, d)]
    x2_b, ffn0 = _ffn_fwd((xh1, ln(mix_ln_g, 0), ln(mix_ln_b, 0)), x1_b, w_ffn_in[0], w_ffn_out[0],
                          ln(ffn_ln_g, 0), ln(ffn_ln_b, 0), 0)

    sv_odd, rs_odd, gated = _gmlp_fwd(x2_b, w_odd_in, v_ln_g, v_ln_b, w_spatial, bs_col)
    x3_b, xh3, rs3 = _mm_res_ln([(gated, w_odd_out)], (ffn0[3], ln(ffn_ln_g, 0), ln(ffn_ln_b, 0)),
                                ln(mix_ln_g, 1), ln(mix_ln_b, 1), "odd_out_ln")
    _, ffn1 = _ffn_fwd((xh3, ln(mix_ln_g, 1), ln(mix_ln_b, 1)), x3_b, w_ffn_in[1], w_ffn_out[1],
                       ln(ffn_ln_g, 1), ln(ffn_ln_b, 1), 1)

    sq, dz4, d_fg1, d_fb1 = _loss_ln_bwd(ffn1[3], ffn1[4], ln(ffn_ln_g, 1), ln(ffn_ln_b, 1), target)
    loss = lax.psum(0.5 / d * jnp.sum(sq), ("x", "y", "c"))
    (dz3, d_mg1, d_mb1), gi_f1, go_f1 = _ffn_bwd(dz4, ffn1, w_ffn_in[1], w_ffn_out[1], (xh3, rs3, ln(mix_ln_g, 1)), 1)

    dgated = _mm_nt([(dz3, 0, 0, d)], [w_odd_out], "odd_dgated")
    go_odd = _mm_tn(gated, dz3, "odd_dw_out", tn=d).reshape(N_CHIPS, 1, d // N_CHIPS, d)
    da_odd, dws, dbs_col, d_vg, d_vb = _gmlp_bwd(dgated, sv_odd, rs_odd, v_ln_g, v_ln_b, w_spatial, bs_col)
    gi_odd = _mm_tn(x2_b, da_odd, "odd_dw_in", tn=d // 2, stack_cols=True)[:, None]
    dz2, d_fg0, d_fb0 = _mm_nt([(da_odd, 0, 0, 0)], [w_odd_in], "odd_dx", res=dz3,
                               ln=(ffn0[3], ffn0[4], ln(ffn_ln_g, 0)))
    (dz1, d_mg0, d_mb0), gi_f0, go_f0 = _ffn_bwd(dz2, ffn0, w_ffn_in[0], w_ffn_out[0], (xh1, rs1, ln(mix_ln_g, 0)), 0)

    sent_early = [gi_odd, go_odd, gi_f0[:, None], gi_f1[:, None], go_f0[:, None], go_f1[:, None]]
    early_h, early_tok = _split_start(sent_early, "scatter4", "scatter_early_start")
    dmix = _mm_nt([(dz1, 0, 0, d)], [w_even_out], "even_dmix", after=early_tok)
    go_even = jnp.concatenate([_mm_tn(attn, dz1, "even_dw_out_attn", tn=d), _mm_tn(conv, dz1, "even_dw_out_conv", tn=d)],
                              axis=0).reshape(N_CHIPS, 1, d // N_CHIPS, d)
    dbch, dconv_w8 = _conv_bwd(bch, dmix, conv_w)
    qb, dob = _attn_bwd_prep(attn, dmix, qp, head_lanes(lse.reshape(FOX_HEADS, t)))
    dqkv, dc_col = _attn_unpack(*_attn_bwd(qb, kp, vp, dob, kt))
    dc3 = dc_col.T.reshape(FOX_HEADS, t // LANES, LANES).transpose(1, 0, 2)
    dfl3, d_bf = _fgate_bwd(dc3, fl3, b_f_col)
    dfl = jnp.concatenate([dfl3.transpose(1, 0, 2).reshape(FOX_HEADS, t).T.astype(BF16),
                           jnp.zeros((t, LANES - FOX_HEADS), BF16)], axis=1)

    dws_masked = jnp.where(gmask[None], dws, 0.0)
    rep_names = ["odd_w_s", "odd_b_s", "mix_ln_g", "mix_ln_b", "ffn_ln_g", "ffn_ln_b", "even_b_f"]
    rep_grads = [dws_masked, dbs_col.T, jnp.concatenate([d_mg0, d_mg1]), jnp.concatenate([d_mb0, d_mb1]),
                 jnp.concatenate([d_fg0, d_fg1]), jnp.concatenate([d_fb0, d_fb1]), d_bf.reshape(1, FOX_HEADS)]
    rep_w = [(odd_w_s, m_odd_w_s, v_odd_w_s), (odd_b_s, m_odd_b_s, v_odd_b_s), (mix_ln_g, m_mix_ln_g, v_mix_ln_g),
             (mix_ln_b, m_mix_ln_b, v_mix_ln_b), (ffn_ln_g, m_ffn_ln_g, v_ffn_ln_g), (ffn_ln_b, m_ffn_ln_b, v_ffn_ln_b),
             (even_b_f, m_even_b_f, v_even_b_f)]
    rep_rows = [_to_rows(gr) for gr in rep_grads]
    n_rep = sum(r.shape[0] for r in rep_rows)
    pad_rep = (-n_rep) % SUBLANES
    dconv_w = dconv_w8[:CONV_K].reshape(CONV_K, N_CHIPS, LANES).transpose(1, 0, 2).reshape(N_CHIPS * CONV_K, LANES)
    packed = jnp.concatenate(rep_rows + [jnp.zeros((pad_rep, LANES), F32), d_vg.reshape(SUBLANES, LANES),
                                         d_vb.reshape(SUBLANES, LANES), dconv_w, jnp.zeros((4, LANES), F32)], axis=0)
    small_h, small_tok = _split_start([packed], "gather8", "gather_small_start")

    swap_h, swap_tok = _split_start(_scattered(early_h, "scatter_early_wait", small_tok), "swap2", "swap_early_start")
    dw_qkv = _mm_tn(dqkv, x2d, "even_dw_qkv", tn=d, tk=QKV // 2, after=swap_tok)
    dw_bch = _mm_tn(dbch, x2d, "even_dw_bch", tn=d, tk=BCH // 2)
    dw_f = _mm_tn(dfl, x2d, "even_dw_f", tn=d)
    gi_even = jnp.concatenate([dw_qkv, dw_f[:FOX_HEADS], dw_bch], axis=0).reshape(N_CHIPS, 1, -1, LANES)
    sent_late = [gi_even, go_even]
    late_h, late_tok = _split_start(sent_late, "scatter4", "scatter_late_start")
    grad_x = _mm_nt([(dqkv, 0, 0, QKV), (dbch, 0, QKV, QKV + BCH), (dfl, 0, QKV + BCH, EVEN_IN_PAD)], [w_even_in],
                    "even_dx", res=dz1, after=late_tok)
    mine, theirs = _split_wait(swap_h, "swap_early_wait", grad_x)
    res = {}
    res["odd_w_in"] = _adamw([mine[0], theirs[0]], odd_w_in, m_odd_w_in, v_odd_w_in, "adamw_odd_w_in")
    res["odd_w_out"] = _adamw([mine[1], theirs[1]], odd_w_out, m_odd_w_out, v_odd_w_out, "adamw_odd_w_out")
    for nm, at, (w, m, v) in (("ffn_w_in", 2, (ffn_w_in, m_ffn_w_in, v_ffn_w_in)),
                              ("ffn_w_out", 4, (ffn_w_out, m_ffn_w_out, v_ffn_w_out))):
        upper = _adamw([mine[at + 1], theirs[at + 1]], w, m, v, f"adamw_{nm}_1", layer=1)
        res[nm] = _adamw([mine[at], theirs[at]], w, m, v, f"adamw_{nm}_0", layer=0, into=upper)
    mine_late = _scattered(late_h, "scatter_late_wait", res["ffn_w_out"][0])
    theirs_late = _exchange(mine_late, "swap2", "swap_late")
    rows = lambda a: jnp.swapaxes(a, 1, 2).reshape(1, -1, LANES)
    back = lambda a: jnp.swapaxes(a.reshape(1, EVEN_IN // N_CHIPS, d), 1, 2)
    res["even_w_in"] = [back(o) for o in _adamw([mine_late[0], theirs_late[0]], rows(even_w_in), rows(m_even_w_in),
                                                rows(v_even_w_in), "adamw_even_w_in")]
    res["even_w_out"] = _adamw([mine_late[1], theirs_late[1]], even_w_out, m_even_w_out, v_even_w_out,
                               "adamw_even_w_out")
    (packed,), (gathered,) = _split_wait(small_h, "gather_small_wait", theirs_late[0])
    gathered = lax.dynamic_update_index_in_dim(gathered, packed, 4 * lax.axis_index("x") + 2 * lax.axis_index("y")
                                               + lax.axis_index("c"), 0)

    base = n_rep + pad_rep
    own_rows = jnp.concatenate([
        lax.dynamic_slice_in_dim(gathered, base + 2 * chip, 2, axis=1),
        lax.dynamic_slice_in_dim(gathered, base + SUBLANES + 2 * chip, 2, axis=1),
        lax.dynamic_slice_in_dim(gathered, base + 2 * SUBLANES + CONV_K * chip, CONV_K, axis=1),
        jnp.zeros((N_DEV, 1, LANES), F32)], axis=1)
    small_parts = jnp.concatenate([gathered[:, :base], own_rows], axis=1)[:, None]

    def pack_small(get):
        rows = [_to_rows(get(tw)) for tw in rep_w] + [jnp.zeros((pad_rep, LANES), F32)]
        rows += [get(sh).reshape(-1, LANES) for sh in ((odd_v_ln_g, m_odd_v_ln_g, v_odd_v_ln_g),
                                                       (odd_v_ln_b, m_odd_v_ln_b, v_odd_v_ln_b),
                                                       (even_conv_w, m_even_conv_w, v_even_conv_w))]
        return jnp.concatenate(rows + [jnp.zeros((1, LANES), F32)], axis=0)[None]

    small_out = _adamw([small_parts], pack_small(lambda tw: tw[0]), pack_small(lambda tw: tw[1]),
                       pack_small(lambda tw: tw[2]), "adamw_small")

    def unpack_small(rows3):
        rows = rows3[0]
        out, off = {}, 0
        for nm, (w, _, _), r in zip(rep_names, rep_w, rep_rows):
            out[nm] = rows[off:off + r.shape[0]].reshape(-1)[:w.size].reshape(w.shape)
            off += r.shape[0]
        off += pad_rep
        out["odd_v_ln_g"] = rows[off:off + 2].reshape(odd_v_ln_g.shape)
        out["odd_v_ln_b"] = rows[off + 2:off + 4].reshape(odd_v_ln_b.shape)
        out["even_conv_w"] = rows[off + 4:off + 4 + CONV_K].reshape(even_conv_w.shape)
        return out

    small = [unpack_small(o) for o in small_out]
    order = ["even_w_in", "even_b_f", "even_conv_w", "even_w_out", "odd_w_in", "odd_v_ln_g", "odd_v_ln_b", "odd_w_s",
             "odd_b_s", "odd_w_out", "mix_ln_g", "mix_ln_b", "ffn_w_in", "ffn_w_out", "ffn_ln_g", "ffn_ln_b"]
    outs = [loss, grad_x[None]]
    for kind in range(4):
        for nm in order:
            outs.append(res[nm][kind] if nm in res else small[kind][nm])
    return tuple(outs)
```

```python
import functools
import math

import jax
import jax.numpy as jnp
from jax import lax
from jax.experimental import pallas as pl
from jax.experimental.pallas import tpu as pltpu

F32 = jnp.float32
BF16 = jnp.bfloat16

D_MODEL = 1024
FOX_HEADS = 8
HEAD_DIM = 64
HEAD_PAIRS = FOX_HEADS // 2
FOX_WIDTH = FOX_HEADS * HEAD_DIM
CONV_WIDTH = 512
CONV_K = 3
QKV = 3 * FOX_WIDTH
BCH = 3 * CONV_WIDTH
EVEN_IN = QKV + FOX_HEADS + BCH
EVEN_IN_PAD = QKV + BCH + 128
GMLP_BLOCK = 128
GMLP_GROUPS = 8
CHUNK = 64
FFN_HIDDEN = 2816
HALF_HIDDEN = FFN_HIDDEN // 2
ALPHA = 4.0 ** 0.25
LN_EPS = 1e-5
ADAM_LR = 0.001
ADAM_B1 = 0.9
ADAM_B2 = 0.999
ADAM_EPS = 1e-08
ADAM_WD = 0.01
ADAM_STEP = 10
N_CHIPS = 4
N_DEV = 8
LANES = 128
SUBLANES = 8
ROW_TILE = 512
FFN_ROW_TILE = 512
FFN_FUSED_ROW_TILE = 256
REDUCE_TILE = 2048
ATT_BLOCK = 512
ATT_FWD_HEADS = 8
ATT_BWD_HEADS = 4
ADAMW_BLOCK_BYTES = 2 ** 20
VMEM_LIMIT = 56 * 2 ** 20
NEG = -1e30
MESH = pl.DeviceIdType.MESH
HIGHEST = lax.Precision.HIGHEST
Q_C, Q_ONE, Q_LSE = 64, 67, 70
K_ONE, K_C, K_ONE2 = 64, 67, 70
V_ONE = 64
DO_DELTA = 65
NT = (((1,), (1,)), ((), ()))
TN = (((0,), (0,)), ((), ()))


def _cp():
    return pltpu.CompilerParams(vmem_limit_bytes=VMEM_LIMIT)


def _resident(shape):
    zeros = (0,) * len(shape)
    return pl.BlockSpec(shape, lambda *_: zeros, pipeline_mode=pl.Buffered(1))


def _sds(shape, dtype):
    return jax.ShapeDtypeStruct(tuple(shape), dtype)


_MASKS = {
    "gather4": [(1, 0, 0), (0, 1, 0), (1, 1, 0)],
    "scatter4": [(1, 0, 0), (0, 1, 0), (1, 1, 0)],
    "swap2": [(0, 0, 1)],
    "gather8": [(0, 0, 1), (0, 1, 0), (0, 1, 1), (1, 0, 0), (1, 0, 1), (1, 1, 0), (1, 1, 1)],
}


def _exchange(arrs, mode, name):
    n = len(arrs)
    masks = _MASKS[mode]
    npeer = len(masks)
    lead = {"gather4": N_CHIPS, "gather8": N_DEV}.get(mode)
    out_shapes = [_sds(((lead,) if lead else ()) + a.shape, a.dtype) for a in arrs]

    def body(*refs):
        ins, outs = refs[:n], refs[n:2 * n]
        send_sems, recv_sems, loc_sems = refs[2 * n:]
        x, y, c = lax.axis_index("x"), lax.axis_index("y"), lax.axis_index("c")
        chip, dev = 2 * x + y, 4 * x + 2 * y + c
        sends, recvs, locs = [], [], []
        for k in range(n):
            if mode == "gather4":
                locs.append(pltpu.make_async_copy(ins[k], outs[k].at[chip], loc_sems.at[k]))
            elif mode == "scatter4":
                locs.append(pltpu.make_async_copy(ins[k].at[chip], outs[k].at[chip], loc_sems.at[k]))
            elif mode == "gather8":
                locs.append(pltpu.make_async_copy(ins[k], outs[k].at[dev], loc_sems.at[k]))
        for cp in locs:
            cp.start()
        for k in range(n):
            for j, (dx, dy, dc) in enumerate(masks):
                px = 1 - x if dx else x
                py = 1 - y if dy else y
                pc = 1 - c if dc else c
                pchip, pdev = 2 * px + py, 4 * px + 2 * py + pc
                if mode == "gather4":
                    src, dst, land = ins[k], outs[k].at[chip], outs[k].at[pchip]
                elif mode == "scatter4":
                    src, dst, land = ins[k].at[pchip], outs[k].at[chip], outs[k].at[pchip]
                elif mode == "swap2":
                    src, dst, land = ins[k], outs[k], outs[k]
                else:
                    src, dst, land = ins[k], outs[k].at[dev], outs[k].at[pdev]
                s = k * npeer + j
                kw = dict(send_sem=send_sems.at[s], recv_sem=recv_sems.at[s], device_id=(px, py, pc),
                          device_id_type=MESH)
                cp = pltpu.make_async_remote_copy(src_ref=src, dst_ref=dst, **kw)
                cp.start()
                sends.append(cp)
                recvs.append(pltpu.make_async_remote_copy(src_ref=src, dst_ref=land, **kw))
        for cp in recvs:
            cp.wait_recv()
        for cp in sends:
            cp.wait_send()
        for cp in locs:
            cp.wait()

    any_spec = pl.BlockSpec(memory_space=pl.ANY)
    outs = pl.pallas_call(
        body,
        out_shape=out_shapes,
        in_specs=[any_spec] * n,
        out_specs=[any_spec] * n,
        scratch_shapes=[pltpu.SemaphoreType.DMA((n * npeer,)), pltpu.SemaphoreType.DMA((n * npeer,)),
                        pltpu.SemaphoreType.DMA((max(n, 1),))],
        name=name,
    )(*arrs)
    return list(outs)


_HBM_SPEC = pl.BlockSpec(memory_space=pltpu.HBM)
_SEM_SPEC = pl.BlockSpec(memory_space=pltpu.SEMAPHORE)
_ANY_SPEC = pl.BlockSpec(memory_space=pl.ANY)
_EFFECT = pltpu.SideEffectType.DATAFLOW_SIDE_EFFECTING


def _split_copies(mode, ins, lands, send_sems, recv_sems):
    x, y, c = lax.axis_index("x"), lax.axis_index("y"), lax.axis_index("c")
    chip, dev = 2 * x + y, 4 * x + 2 * y + c
    masks = _MASKS[mode]
    out = []
    for k in range(len(ins)):
        for j, (dx, dy, dc) in enumerate(masks):
            px = 1 - x if dx else x
            py = 1 - y if dy else y
            pc = 1 - c if dc else c
            pchip, pdev = 2 * px + py, 4 * px + 2 * py + pc
            if mode == "gather4":
                src, dst, land = ins[k], lands[k].at[chip], lands[k].at[pchip]
            elif mode == "scatter4":
                src, dst, land = ins[k].at[pchip], lands[k].at[chip], lands[k].at[pchip]
            elif mode == "swap2":
                src, dst, land = ins[k], lands[k], lands[k]
            else:
                src, dst, land = ins[k], lands[k].at[dev], lands[k].at[pdev]
            s = k * len(masks) + j
            kw = dict(send_sem=send_sems.at[s], recv_sem=recv_sems.at[s], device_id=(px, py, pc), device_id_type=MESH)
            out.append((pltpu.make_async_remote_copy(src_ref=src, dst_ref=dst, **kw),
                        pltpu.make_async_remote_copy(src_ref=src, dst_ref=land, **kw)))
    return out


def _split_start(arrs, mode, name, after=None):
    n = len(arrs)
    nsem = n * len(_MASKS[mode])
    lead = {"gather4": (N_CHIPS,), "gather8": (N_DEV,)}.get(mode, ())
    land_shapes = [lead + a.shape for a in arrs]

    def body(*refs):
        ins, lands = refs[:n], refs[n:2 * n]
        outs = refs[2 * n + (after is not None):]
        for start, _ in _split_copies(mode, ins, lands, outs[0], outs[1]):
            start.start()
        outs[-1][...] = jnp.zeros(outs[-1].shape, F32)

    srcs = [pltpu.with_memory_space_constraint(a, pltpu.HBM) for a in arrs]
    empties = [pltpu.with_memory_space_constraint(lax.empty(s, a.dtype), pltpu.HBM) for s, a in zip(land_shapes, arrs)]
    res = pl.pallas_call(
        body, name=name,
        out_shape=(pltpu.SemaphoreType.DMA((nsem,)), pltpu.SemaphoreType.DMA((nsem,)),
                   *[pltpu.HBM(a.shape, a.dtype) for a in arrs],
                   *[pltpu.HBM(s, a.dtype) for s, a in zip(land_shapes, arrs)],
                   _sds((SUBLANES, LANES), F32)),
        in_specs=[_HBM_SPEC] * (2 * n) + ([_ANY_SPEC] if after is not None else []),
        out_specs=(_SEM_SPEC, _SEM_SPEC, *[_HBM_SPEC] * (2 * n), pl.BlockSpec(memory_space=pltpu.VMEM)),
        input_output_aliases={k: 2 + k for k in range(2 * n)},
        compiler_params=pltpu.CompilerParams(has_side_effects=_EFFECT),
    )(*srcs, *empties, *([after] if after is not None else []))
    return dict(mode=mode, n=n, sems=res[:2], bufs=res[2:2 + 2 * n]), res[-1]


def _split_wait(handle, name, after):
    n, mode = handle["n"], handle["mode"]

    def body(*refs):
        ins, lands = refs[:n], refs[n:2 * n]
        send_sems, recv_sems = refs[2 * n], refs[2 * n + 1]
        for _, arrival in _split_copies(mode, ins, lands, send_sems, recv_sems):
            arrival.wait_send()
            arrival.wait_recv()

    bufs = handle["bufs"]
    res = pl.pallas_call(
        body, name=name,
        out_shape=tuple(pltpu.HBM(b.shape, b.dtype) for b in bufs),
        in_specs=[_HBM_SPEC] * (2 * n) + [_SEM_SPEC, _SEM_SPEC, _ANY_SPEC],
        out_specs=tuple([_HBM_SPEC] * (2 * n)),
        input_output_aliases={k: k for k in range(2 * n)},
        compiler_params=pltpu.CompilerParams(has_side_effects=_EFFECT),
    )(*bufs, *handle["sems"], after)
    return list(res[:n]), list(res[n:])


def _with_own(landed, own):
    chip = 2 * lax.axis_index("x") + lax.axis_index("y")
    return lax.dynamic_update_index_in_dim(landed, own, chip, 0)


def _gathered(handle, name, after):
    sent, landed = _split_wait(handle, name, after)
    return [_with_own(g, own) for g, own in zip(landed, sent)]


def _scattered(handle, name, after):
    chip = 2 * lax.axis_index("x") + lax.axis_index("y")
    sent, landed = _split_wait(handle, name, after)
    return [_with_own(r, lax.dynamic_index_in_dim(g, chip, 0, keepdims=False)) for r, g in zip(landed, sent)]


def _sigmoid(x):
    return 0.5 * jnp.tanh(0.5 * x) + 0.5


def _log_sigmoid(x):
    e = jnp.exp(-jnp.abs(x))
    log1p = jnp.where(e < 1e-2, e * (1.0 - e * (0.5 - e * (1.0 / 3.0))), jnp.log(1.0 + e))
    return jnp.minimum(x, 0.0) - log1p


def _ln_fwd(z):
    mu = jnp.mean(z, axis=-1, keepdims=True)
    zc = z - mu
    var = jnp.mean(zc * zc, axis=-1, keepdims=True)
    rstd = lax.rsqrt(var + LN_EPS)
    return zc * rstd, rstd


def _ln_bwd(dy, xhat, rstd, g):
    dxh = dy * g
    m1 = jnp.mean(dxh, axis=-1, keepdims=True)
    m2 = jnp.mean(dxh * xhat, axis=-1, keepdims=True)
    dz = rstd * (dxh - m1 - xhat * m2)
    return dz, jnp.sum(dy * xhat, axis=0, keepdims=True), jnp.sum(dy, axis=0, keepdims=True)


def _shift_down(z, halo):
    r = lax.broadcasted_iota(jnp.int32, z.shape, 0)
    z1 = jnp.where(r == 0, halo[7:8, :], pltpu.roll(z, 1, 0))
    z2 = jnp.where(r == 0, halo[6:7, :], jnp.where(r == 1, halo[7:8, :], pltpu.roll(z, 2, 0)))
    return z1, z2


def _shift_up(z, halo):
    n = z.shape[0]
    r = lax.broadcasted_iota(jnp.int32, z.shape, 0)
    z1 = jnp.where(r == n - 1, halo[0:1, :], pltpu.roll(z, n - 1, 0))
    z2 = jnp.where(r == n - 1, halo[1:2, :], jnp.where(r == n - 2, halo[0:1, :], pltpu.roll(z, n - 2, 0)))
    return z1, z2


def _accumulate(ref, first, value):
    @pl.when(first)
    def _():
        ref[...] = value

    @pl.when(jnp.logical_not(first))
    def _():
        ref[...] += value


def _proj(x, w, splits, name):
    t, k = x.shape
    tm = min(ROW_TILE, t)

    def body(x_ref, w_ref, *outs):
        a = x_ref[...].astype(BF16)
        for (lo, hi, dt), o in zip(splits, outs):
            o[...] = jnp.dot(a, w_ref[:, lo:hi], preferred_element_type=F32).astype(dt)

    return pl.pallas_call(
        body, grid=(t // tm,),
        in_specs=[pl.BlockSpec((tm, k), lambda i: (i, 0)), _resident(w.shape)],
        out_specs=[pl.BlockSpec((tm, hi - lo), lambda i: (i, 0)) for lo, hi, _ in splits],
        out_shape=[_sds((t, hi - lo), dt) for lo, hi, dt in splits],
        compiler_params=_cp(), name=name)(x, w)


def _fgate_fwd(fl3, b_f):
    nc = fl3.shape[0]

    def body(f_ref, b_ref, c_ref):
        r = lax.broadcasted_iota(jnp.int32, (LANES, LANES), 0)
        cidx = lax.broadcasted_iota(jnp.int32, (LANES, LANES), 1)
        upper = (r <= cidx).astype(F32)

        def step(i, carry):
            lf = _log_sigmoid(f_ref[i] + b_ref[...])
            cc = jnp.dot(lf, upper, precision=HIGHEST, preferred_element_type=F32) + carry
            c_ref[i] = cc
            return cc[:, LANES - 1:LANES]

        lax.fori_loop(0, nc, step, jnp.zeros((FOX_HEADS, 1), F32))

    return pl.pallas_call(body, out_shape=_sds(fl3.shape, F32), name="fgate_fwd")(fl3, b_f)


def _fgate_bwd(dc3, fl3, b_f):
    nc = fl3.shape[0]

    def body(dc_ref, f_ref, b_ref, df_ref, db_ref):
        r = lax.broadcasted_iota(jnp.int32, (LANES, LANES), 0)
        cidx = lax.broadcasted_iota(jnp.int32, (LANES, LANES), 1)
        lower = (r >= cidx).astype(F32)

        def step(n, carry):
            suffix, db = carry
            i = nc - 1 - n
            dlf = jnp.dot(dc_ref[i], lower, precision=HIGHEST, preferred_element_type=F32) + suffix
            df = dlf * (1.0 - _sigmoid(f_ref[i] + b_ref[...]))
            df_ref[i] = df
            return dlf[:, 0:1], db + jnp.sum(df, axis=1, keepdims=True)

        zero = jnp.zeros((FOX_HEADS, 1), F32)
        _, db = lax.fori_loop(0, nc, step, (zero, zero))
        db_ref[...] = db

    return pl.pallas_call(body, out_shape=[_sds(fl3.shape, F32), _sds((FOX_HEADS, 1), F32)],
                          name="fgate_bwd")(dc3, fl3, b_f)


def _split3(c):
    hi = c.astype(BF16).astype(F32)
    mid = (c - hi).astype(BF16).astype(F32)
    lo = (c - hi - mid).astype(BF16).astype(F32)
    return hi, mid, lo


PIECE_ONE = 3 * FOX_HEADS


def _piece_rows(values):
    hi, mid, lo = _split3(values)
    lane = lax.broadcasted_iota(jnp.int32, values.shape, 1)
    row = hi + pltpu.roll(mid, FOX_HEADS, 1) + pltpu.roll(lo, 2 * FOX_HEADS, 1) + jnp.where(lane == PIECE_ONE, 1.0, 0.0)
    return row.astype(BF16)


def _piece_selector(start, sign, ones=()):
    sel = [[0.0] * FOX_WIDTH for _ in range(LANES)]
    for h in range(FOX_HEADS):
        for n in range(3):
            sel[n * FOX_HEADS + h][h * HEAD_DIM + start - HEAD_DIM + n] = sign
        for lane in ones:
            sel[PIECE_ONE][h * HEAD_DIM + lane - HEAD_DIM] = 1.0
    return jnp.asarray(sel, BF16)


def _attn_pack(qkv, c_pad):
    t = qkv.shape[0]
    tm = min(ROW_TILE, t)
    hd = HEAD_DIM
    sel_q = _piece_selector(Q_C, 1.0, range(Q_ONE, Q_ONE + 3))
    sel_k = _piece_selector(K_C, -1.0, [*range(K_ONE, K_ONE + 3), *range(K_ONE2, K_ONE2 + 3)])
    sel_v = _piece_selector(HEAD_DIM, 0.0, range(V_ONE, V_ONE + 4))

    def body(x_ref, c_ref, sq_ref, sk_ref, sv_ref, qp_ref, kp_ref, vp_ref, kt_ref, vt_ref):
        pieces = _piece_rows(c_ref[...])
        q_extra = jnp.dot(pieces, sq_ref[...], preferred_element_type=F32).astype(BF16)
        k_extra = jnp.dot(pieces, sk_ref[...], preferred_element_type=F32).astype(BF16)
        v_extra = jnp.dot(pieces, sv_ref[...], preferred_element_type=F32).astype(BF16)
        for h in range(FOX_HEADS):
            hs = slice(h * hd, (h + 1) * hd)
            qp_ref[h, :, :hd] = (x_ref[:, hs].astype(F32) * (hd ** -0.5)).astype(BF16)
            qp_ref[h, :, hd:] = q_extra[:, hs]
            kp_ref[h, :, :hd] = x_ref[:, FOX_WIDTH + h * hd:FOX_WIDTH + (h + 1) * hd]
            kp_ref[h, :, hd:] = k_extra[:, hs]
            vp_ref[h, :, :hd] = x_ref[:, 2 * FOX_WIDTH + h * hd:2 * FOX_WIDTH + (h + 1) * hd]
            vp_ref[h, :, hd:] = v_extra[:, hs]
            kt_ref[h] = kp_ref[h].T
            vt_ref[h] = vp_ref[h].T

    row3 = pl.BlockSpec((FOX_HEADS, tm, LANES), lambda i: (0, i, 0))
    col3 = pl.BlockSpec((FOX_HEADS, LANES, tm), lambda i: (0, 0, i))
    sel = _resident(sel_q.shape)
    return pl.pallas_call(
        body, grid=(t // tm,),
        in_specs=[pl.BlockSpec((tm, QKV), lambda i: (i, 0)), pl.BlockSpec((tm, LANES), lambda i: (i, 0)), sel, sel, sel],
        out_specs=[row3, row3, row3, col3, col3],
        out_shape=[_sds((FOX_HEADS, t, LANES), BF16)] * 3 + [_sds((FOX_HEADS, LANES, t), BF16)] * 2,
        compiler_params=_cp(), name="attn_pack")(qkv, c_pad, sel_q, sel_k, sel_v)


def _triangle(nq, key_major):
    if key_major:
        pairs = [(i, j) for j in range(nq) for i in range(j, nq)]
    else:
        pairs = [(i, j) for i in range(nq) for j in range(i + 1)]
    return jnp.asarray([p[0] for p in pairs], jnp.int32), jnp.asarray([p[1] for p in pairs], jnp.int32)


def _attn_fwd(qp, kp, vt):
    t = qp.shape[1]
    bq = min(ATT_BLOCK, t)
    nq = t // bq
    nh = ATT_FWD_HEADS
    i_tab, j_tab = _triangle(nq, key_major=False)

    def body(it_ref, jt_ref, q_ref, k_ref, vt_ref, o_ref, lse_ref, m_sc, acc_sc):
        s = pl.program_id(1)
        i, j = it_ref[s], jt_ref[s]

        @pl.when(j == 0)
        def _():
            m_sc[...] = jnp.full(m_sc.shape, NEG, F32)
            acc_sc[...] = jnp.zeros(acc_sc.shape, F32)

        def sweep(masked):
            scores = lambda h: lax.dot_general(k_ref[h], q_ref[h], NT, preferred_element_type=F32)

            def accumulate(h, pt, rescale):
                acc_sc[h] = rescale * acc_sc[h] + jnp.dot(vt_ref[h], pt, preferred_element_type=F32)

            ahead, behind = scores(0), None
            for h in range(nh):
                st = ahead
                if h + 1 < nh:
                    ahead = scores(h + 1)
                if behind is not None:
                    accumulate(*behind)
                if masked:
                    key = lax.broadcasted_iota(jnp.int32, (bq, bq), 0)
                    qry = lax.broadcasted_iota(jnp.int32, (bq, bq), 1)
                    st = jnp.where(key <= qry, st, NEG)
                m_prev = m_sc[h]
                m_new = jnp.maximum(m_prev, jnp.max(st, axis=0, keepdims=True))
                behind = (h, jnp.exp(st - m_new).astype(BF16), jnp.exp(m_prev - m_new))
                m_sc[h] = m_new
            accumulate(*behind)

        @pl.when(j < i)
        def _():
            sweep(False)

        @pl.when(j == i)
        def _():
            sweep(True)
            for h in range(nh):
                acc = acc_sc[h]
                denom = acc[V_ONE:V_ONE + 1, :]
                o_ref[:, h * HEAD_DIM:(h + 1) * HEAD_DIM] = (acc[:HEAD_DIM, :] / denom).T.astype(BF16)
                lse_ref[h] = m_sc[h] + jnp.log(denom)

    grid_spec = pltpu.PrefetchScalarGridSpec(
        num_scalar_prefetch=2, grid=(FOX_HEADS // nh, i_tab.shape[0]),
        in_specs=[pl.BlockSpec((nh, bq, LANES), lambda hp, s, it, jt: (hp, it[s], 0)),
                  pl.BlockSpec((nh, bq, LANES), lambda hp, s, it, jt: (hp, jt[s], 0)),
                  pl.BlockSpec((nh, LANES, bq), lambda hp, s, it, jt: (hp, 0, jt[s]))],
        out_specs=[pl.BlockSpec((bq, nh * HEAD_DIM), lambda hp, s, it, jt: (it[s], hp)),
                   pl.BlockSpec((nh, 1, bq), lambda hp, s, it, jt: (hp, 0, it[s]))],
        scratch_shapes=[pltpu.VMEM((nh, 1, bq), F32), pltpu.VMEM((nh, LANES, bq), F32)])
    return pl.pallas_call(body, grid_spec=grid_spec,
                          out_shape=[_sds((t, FOX_WIDTH), BF16), _sds((FOX_HEADS, 1, t), F32)],
                          compiler_params=_cp(), name="attn_fwd")(i_tab, j_tab, qp, kp, vt)


def _conv_fwd(bch, conv_w):
    t = bch.shape[0]
    tm = min(ROW_TILE, t)
    halo_blocks = tm // SUBLANES
    cw = CONV_WIDTH

    def body(cur_ref, prev_ref, w_ref, o_ref):
        i = pl.program_id(0)
        z = cur_ref[:, cw:2 * cw] * cur_ref[:, 2 * cw:]
        zp = jnp.where(i == 0, 0.0, prev_ref[:, cw:2 * cw] * prev_ref[:, 2 * cw:])
        z1, z2 = _shift_down(z, zp)
        y = w_ref[0:1, :] * z2 + w_ref[1:2, :] * z1 + w_ref[2:3, :] * z
        o_ref[...] = (cur_ref[:, :cw] * y).astype(BF16)

    return pl.pallas_call(
        body, grid=(t // tm,),
        in_specs=[pl.BlockSpec((tm, BCH), lambda i: (i, 0)),
                  pl.BlockSpec((SUBLANES, BCH), lambda i: (jnp.maximum(i * halo_blocks - 1, 0), 0)),
                  _resident(conv_w.shape)],
        out_specs=pl.BlockSpec((tm, cw), lambda i: (i, 0)),
        out_shape=_sds((t, cw), BF16), compiler_params=_cp(), name="conv_fwd")(bch, bch, conv_w)


def _mm_res_ln(pairs, res, g, b, name):
    from_ln = isinstance(res, tuple)
    res_args = list(res) if from_ln else [res]
    t, d = res_args[0].shape
    tm = min(ROW_TILE, t)
    n = len(pairs)

    def body(*refs):
        a_refs, w_refs = refs[:n], refs[n:2 * n]
        res_refs = refs[2 * n:2 * n + len(res_args)]
        g_ref, b_ref, yb_ref, xh_ref, rs_ref = refs[2 * n + len(res_args):]
        r = res_refs[0][...]
        if from_ln:
            r = r * res_refs[1][...] + res_refs[2][...]
        z = ALPHA * r
        for a_ref, w_ref in zip(a_refs, w_refs):
            z = z + jnp.dot(a_ref[...].astype(BF16), w_ref[...], preferred_element_type=F32)
        xhat, rstd = _ln_fwd(z)
        yb_ref[...] = (xhat * g_ref[...] + b_ref[...]).astype(BF16)
        xh_ref[...] = xhat
        rs_ref[...] = rstd

    row = lambda i: (i, 0)
    full = pl.BlockSpec((tm, d), row)
    return pl.pallas_call(
        body, grid=(t // tm,),
        in_specs=[pl.BlockSpec((tm, a.shape[1]), row) for a, _ in pairs] + [_resident(w.shape) for _, w in pairs]
        + [full] + [_resident(a.shape) for a in res_args[1:]] + [_resident(g.shape), _resident(b.shape)],
        out_specs=[full, full, pl.BlockSpec((tm, 1), row)],
        out_shape=[_sds((t, d), BF16), _sds((t, d), F32), _sds((t, 1), F32)],
        compiler_params=_cp(), name=name)(*[a for a, _ in pairs], *[w for _, w in pairs], *res_args, g, b)


def _ffn_in(x, wi, name):
    t, d = x.shape
    tm = min(FFN_ROW_TILE, t)
    hh = HALF_HIDDEN

    def body(x_ref, w_ref, gu_ref, h_ref):
        a = x_ref[...].astype(BF16)
        for c in range(2):
            gs, us = slice(c * hh, (c + 1) * hh), slice(FFN_HIDDEN + c * hh, FFN_HIDDEN + (c + 1) * hh)
            g = jnp.dot(a, w_ref[c], preferred_element_type=F32)
            u = jnp.dot(a, w_ref[2 + c], preferred_element_type=F32)
            sig = _sigmoid(g)
            silu = g * sig
            gu_ref[:, gs] = (u * sig * (1.0 + g * (1.0 - sig))).astype(BF16)
            gu_ref[:, us] = silu.astype(BF16)
            h_ref[:, gs] = (silu * u).astype(BF16)

    row = lambda i: (i, 0)
    return pl.pallas_call(
        body, grid=(t // tm,),
        in_specs=[pl.BlockSpec((tm, d), row), _resident(wi.shape)],
        out_specs=[pl.BlockSpec((tm, 2 * FFN_HIDDEN), row), pl.BlockSpec((tm, FFN_HIDDEN), row)],
        out_shape=[_sds((t, 2 * FFN_HIDDEN), BF16), _sds((t, FFN_HIDDEN), BF16)],
        compiler_params=_cp(), name=name)(x, wi)


def _gmlp_fwd(x, w_in, vg, vb, wm, bs_col):
    t, d = x.shape
    tm = min(ROW_TILE, t)
    gb = GMLP_BLOCK

    def body(x_ref, w_ref, vg_ref, vb_ref, wm_ref, bs_ref, sv_ref, rs_ref, o_ref, a_sc):
        xb = x_ref[...].astype(BF16)
        nc = w_ref.shape[2]
        for j in range(w_ref.shape[0]):
            a_sc[:, j * nc:(j + 1) * nc] = jnp.dot(xb, w_ref[j], preferred_element_type=F32)
        halves = []
        for half in range(2):
            a = a_sc[:, half * d:(half + 1) * d]
            cdf = 0.5 * (1.0 + lax.erf(a * (2.0 ** -0.5)))
            halves.append(a * cdf)
            slope = cdf + a * (jnp.exp(-0.5 * a * a) * (1.0 / math.sqrt(2.0 * math.pi)))
            sv_ref[:, (2 * half + 1) * d:(2 * half + 2) * d] = slope.astype(BF16)
        u = halves[0]
        vhat, rstd = _ln_fwd(halves[1])
        sv_ref[:, :d] = u.astype(BF16)
        sv_ref[:, 2 * d:3 * d] = vhat.astype(BF16)
        rs_ref[...] = rstd
        vln = (vhat * vg_ref[...] + vb_ref[...]).astype(BF16)
        for blk in range(tm // gb):
            rs = slice(blk * gb, (blk + 1) * gb)
            for gi in range(GMLP_GROUPS):
                cs = slice(gi * gb, (gi + 1) * gb)
                s = jnp.dot(wm_ref[gi], vln[rs, cs], preferred_element_type=F32) + bs_ref[:, gi:gi + 1]
                o_ref[rs, cs] = (u[rs, cs] * s).astype(BF16)

    row = lambda i: (i, 0)
    return pl.pallas_call(
        body, grid=(t // tm,),
        in_specs=[pl.BlockSpec((tm, d), row), _resident(w_in.shape), _resident(vg.shape), _resident(vb.shape),
                  _resident(wm.shape), _resident(bs_col.shape)],
        out_specs=[pl.BlockSpec((tm, 4 * d), row), pl.BlockSpec((tm, 1), row), pl.BlockSpec((tm, d), row)],
        out_shape=[_sds((t, 4 * d), BF16), _sds((t, 1), F32), _sds((t, d), BF16)],
        scratch_shapes=[pltpu.VMEM((tm, 2 * d), F32)],
        compiler_params=_cp(), name="gmlp_fwd")(x, w_in, vg, vb, wm, bs_col)


def _loss_ln_bwd(xhat, rstd, g, b, target):
    t, d = xhat.shape
    tm = min(ROW_TILE, t)

    def body(xh_ref, rs_ref, g_ref, b_ref, t_ref, sq_ref, dz_ref, dg_ref, db_ref):
        first = pl.program_id(0) == 0
        xh = xh_ref[...]
        err = xh * g_ref[...] + b_ref[...] - t_ref[...]
        dz, dg, db = _ln_bwd(err * (1.0 / d), xh, rs_ref[...], g_ref[...])
        dz_ref[...] = dz
        _accumulate(sq_ref, first, jnp.sum(err * err, axis=0, keepdims=True))
        _accumulate(dg_ref, first, dg)
        _accumulate(db_ref, first, db)

    row = lambda i: (i, 0)
    vec = pl.BlockSpec((1, d), lambda i: (0, 0))
    return pl.pallas_call(
        body, grid=(t // tm,),
        in_specs=[pl.BlockSpec((tm, d), row), pl.BlockSpec((tm, 1), row), _resident(g.shape), _resident(b.shape),
                  pl.BlockSpec((tm, d), row)],
        out_specs=[vec, pl.BlockSpec((tm, d), row), vec, vec],
        out_shape=[_sds((1, d), F32), _sds((t, d), F32), _sds((1, d), F32), _sds((1, d), F32)],
        compiler_params=_cp(), name="loss_ln_bwd")(xhat, rstd, g, b, target)


def _mm_nt(pairs, ws, name, *, tm=ROW_TILE, res=None, ln=None, out_dtype=F32, after=None):
    t = pairs[0][0].shape[0]
    k = ws[0].shape[-2]
    tm = min(tm, t)
    n, nw = len(pairs), len(ws)

    def body(*refs):
        refs = refs[after is not None:]
        a_refs, w_refs = refs[:n], refs[n:n + nw]
        rest = list(refs[n + nw:])
        dx = None
        for a_ref, (_, wi, lo, hi) in zip(a_refs, pairs):
            w_ref = w_refs[wi]
            if len(w_ref.shape) == 3:
                nc = w_ref.shape[2]
                parts = [lax.dot_general(a_ref[:, j * nc:(j + 1) * nc].astype(BF16), w_ref[j], NT,
                                         preferred_element_type=F32) for j in range(w_ref.shape[0])]
            else:
                parts = [lax.dot_general(a_ref[...].astype(BF16), w_ref[:, lo:hi], NT, preferred_element_type=F32)]
            for part in parts:
                dx = part if dx is None else dx + part
        if res is not None:
            dx = dx + ALPHA * rest.pop(0)[...]
        if ln is None:
            rest[0][...] = dx.astype(out_dtype)
            return
        xh_ref, rs_ref, g_ref, dz_ref, dg_ref, db_ref = rest
        first = pl.program_id(0) == 0
        dz, dg, db = _ln_bwd(dx, xh_ref[...], rs_ref[...], g_ref[...])
        dz_ref[...] = dz
        _accumulate(dg_ref, first, dg)
        _accumulate(db_ref, first, db)

    row = lambda i: (i, 0)
    in_specs = [pl.BlockSpec((tm, a.shape[1]), row) for a, _, _, _ in pairs] + [_resident(w.shape) for w in ws]
    args = [a for a, _, _, _ in pairs] + list(ws)
    if res is not None:
        in_specs.append(pl.BlockSpec((tm, k), row))
        args.append(res)
    if ln is None:
        out_specs = pl.BlockSpec((tm, k), row)
        out_shape = _sds((t, k), out_dtype)
    else:
        xhat, rstd, g = ln
        in_specs += [pl.BlockSpec((tm, k), row), pl.BlockSpec((tm, 1), row), _resident(g.shape)]
        args += [xhat, rstd, g]
        vec = pl.BlockSpec((1, k), lambda i: (0, 0))
        out_specs = [pl.BlockSpec((tm, k), row), vec, vec]
        out_shape = [_sds((t, k), F32), _sds((1, k), F32), _sds((1, k), F32)]
    if after is not None:
        in_specs.insert(0, _ANY_SPEC)
        args.insert(0, after)
    return pl.pallas_call(body, grid=(t // tm,), in_specs=in_specs, out_specs=out_specs, out_shape=out_shape,
                          compiler_params=_cp(), name=name)(*args)


def _mm_tn(a, b, name, *, tn, tk=None, tt=None, stack_cols=False, out_dtype=BF16, after=None):
    t, k = a.shape
    n = b.shape[1]
    tk = k if tk is None else tk
    tt = min(REDUCE_TILE if tt is None else tt, t)
    nt = t // tt

    def body(a_ref, b_ref, *rest):
        o_ref, acc_ref = rest[after is not None:]
        s = pl.program_id(2)
        part = lax.dot_general(a_ref[...].astype(BF16), b_ref[...].astype(BF16), TN, preferred_element_type=F32)
        _accumulate(acc_ref, s == 0, part)

        @pl.when(s == nt - 1)
        def _():
            o_ref[...] = acc_ref[...].astype(out_dtype).reshape(o_ref.shape)

    if stack_cols:
        assert tk == k
        out_spec = pl.BlockSpec((1, k, tn), lambda kk, j, s: (j, 0, 0))
        out_shape = _sds((n // tn, k, tn), out_dtype)
    else:
        out_spec = pl.BlockSpec((tk, tn), lambda kk, j, s: (kk, j))
        out_shape = _sds((k, n), out_dtype)
    return pl.pallas_call(
        body, grid=(k // tk, n // tn, nt),
        in_specs=[pl.BlockSpec((tt, tk), lambda kk, j, s: (s, kk)), pl.BlockSpec((tt, tn), lambda kk, j, s: (s, j))]
        + ([_ANY_SPEC] if after is not None else []),
        out_specs=out_spec, out_shape=out_shape,
        scratch_shapes=[pltpu.VMEM((tk, tn), F32)],
        compiler_params=_cp(), name=name)(a, b, *([after] if after is not None else []))


def _ffn_bwd_rows(dz, wo, gu, wi, ln_below, name):
    t, d = dz.shape
    tm = min(FFN_FUSED_ROW_TILE, t)
    hh = HALF_HIDDEN
    xhat, rstd, g = ln_below

    def body(dz_ref, wo_ref, gu_ref, wi_ref, xh_ref, rs_ref, g_ref, dgu_ref, dzb_ref, dg_ref, db_ref):
        first = pl.program_id(0) == 0
        a = dz_ref[...].astype(BF16)
        for c in range(2):
            gs, us = slice(c * hh, (c + 1) * hh), slice(FFN_HIDDEN + c * hh, FFN_HIDDEN + (c + 1) * hh)
            dh = lax.dot_general(a, wo_ref[gs, :], NT, preferred_element_type=F32)
            dgu_ref[:, gs] = (dh * gu_ref[:, gs].astype(F32)).astype(BF16)
            dgu_ref[:, us] = (dh * gu_ref[:, us].astype(F32)).astype(BF16)
        dx = ALPHA * dz_ref[...]
        for j in range(wi_ref.shape[0]):
            dx = dx + lax.dot_general(dgu_ref[:, j * hh:(j + 1) * hh], wi_ref[j], NT, preferred_element_type=F32)
        dzb, dg, db = _ln_bwd(dx, xh_ref[...], rs_ref[...], g_ref[...])
        dzb_ref[...] = dzb
        _accumulate(dg_ref, first, dg)
        _accumulate(db_ref, first, db)

    row = lambda i: (i, 0)
    wide, full = pl.BlockSpec((tm, 2 * FFN_HIDDEN), row), pl.BlockSpec((tm, d), row)
    vec = pl.BlockSpec((1, d), lambda i: (0, 0))
    return pl.pallas_call(
        body, grid=(t // tm,),
        in_specs=[full, _resident(wo.shape), wide, _resident(wi.shape), full, pl.BlockSpec((tm, 1), row),
                  _resident(g.shape)],
        out_specs=[wide, full, vec, vec],
        out_shape=[_sds((t, 2 * FFN_HIDDEN), BF16), _sds((t, d), F32), _sds((1, d), F32), _sds((1, d), F32)],
        compiler_params=_cp(), name=name)(dz, wo, gu, wi, xhat, rstd, g)


def _gmlp_bwd(dgated, saved, rstd_v, vg, vb, wm, bs_col):
    t, d = dgated.shape
    d2 = 2 * d
    tm = min(ROW_TILE, t)
    gb = GMLP_BLOCK

    def body(dg_ref, sv_ref, rs_ref, vg_ref, vb_ref, wm_ref, bs_ref, da_ref, dws_ref, dbs_ref, dvg_ref, dvb_ref, dvln_sc):
        first = pl.program_id(0) == 0
        u = sv_ref[:, :d].astype(F32)
        vhat = sv_ref[:, 2 * d:3 * d].astype(F32)
        rstd = rs_ref[...]
        vln = (vhat * vg_ref[...] + vb_ref[...]).astype(BF16)
        dgate = dg_ref[...]

        @pl.when(first)
        def _():
            dws_ref[...] = jnp.zeros(dws_ref.shape, F32)
            dbs_ref[...] = jnp.zeros(dbs_ref.shape, F32)

        for blk in range(tm // gb):
            rs = slice(blk * gb, (blk + 1) * gb)
            for gi in range(GMLP_GROUPS):
                cs = slice(gi * gb, (gi + 1) * gb)
                vblk = vln[rs, cs]
                s = jnp.dot(wm_ref[gi], vblk, preferred_element_type=F32) + bs_ref[:, gi:gi + 1]
                dgb = dgate[rs, cs]
                da_ref[rs, cs] = (dgb * s * sv_ref[rs, d + gi * gb:d + (gi + 1) * gb].astype(F32)).astype(BF16)
                ds = dgb * u[rs, cs]
                dsb = ds.astype(BF16)
                dws_ref[gi] += lax.dot_general(dsb, vblk, NT, preferred_element_type=F32)
                dbs_ref[:, gi:gi + 1] += jnp.sum(ds, axis=1, keepdims=True)
                dvln_sc[rs, cs] = lax.dot_general(wm_ref[gi], dsb, TN, preferred_element_type=F32)
        dv, dvg, dvb = _ln_bwd(dvln_sc[...], vhat, rstd, vg_ref[...])
        da_ref[:, d:] = (dv * sv_ref[:, 3 * d:].astype(F32)).astype(BF16)
        _accumulate(dvg_ref, first, dvg)
        _accumulate(dvb_ref, first, dvb)

    row = lambda i: (i, 0)
    vec = pl.BlockSpec((1, d), lambda i: (0, 0))
    return pl.pallas_call(
        body, grid=(t // tm,),
        in_specs=[pl.BlockSpec((tm, d), row), pl.BlockSpec((tm, 4 * d), row), pl.BlockSpec((tm, 1), row),
                  _resident(vg.shape), _resident(vb.shape), _resident(wm.shape), _resident(bs_col.shape)],
        out_specs=[pl.BlockSpec((tm, d2), row), pl.BlockSpec(wm.shape, lambda i: (0, 0, 0)),
                   pl.BlockSpec(bs_col.shape, lambda i: (0, 0)), vec, vec],
        out_shape=[_sds((t, d2), BF16), _sds(wm.shape, F32), _sds(bs_col.shape, F32), _sds((1, d), F32), _sds((1, d), F32)],
        scratch_shapes=[pltpu.VMEM((tm, d), F32)],
        compiler_params=_cp(), name="gmlp_bwd")(dgated, saved, rstd_v, vg, vb, wm, bs_col)


def _conv_bwd(bch, dmix, conv_w):
    t = bch.shape[0]
    tm = min(ROW_TILE, t)
    nb = t // tm
    halo_blocks = tm // SUBLANES
    cw = CONV_WIDTH

    def body(cur_ref, prev_ref, next_ref, dc_ref, dn_ref, w_ref, o_ref, dw_ref):
        i = pl.program_id(0)
        bgate, cgate, hval = cur_ref[:, :cw], cur_ref[:, cw:2 * cw], cur_ref[:, 2 * cw:]
        z = cgate * hval
        zp = jnp.where(i == 0, 0.0, prev_ref[:, cw:2 * cw] * prev_ref[:, 2 * cw:])
        z1, z2 = _shift_down(z, zp)
        w0, w1, w2 = w_ref[0:1, :], w_ref[1:2, :], w_ref[2:3, :]
        dconv = dc_ref[...]
        o_ref[:, :cw] = (dconv * (w0 * z2 + w1 * z1 + w2 * z)).astype(BF16)
        dy = dconv * bgate
        dyn = jnp.where(i == nb - 1, 0.0, dn_ref[...] * next_ref[:, :cw])
        dy1, dy2 = _shift_up(dy, dyn)
        dz = w2 * dy + w1 * dy1 + w0 * dy2
        o_ref[:, cw:2 * cw] = (dz * hval).astype(BF16)
        o_ref[:, 2 * cw:] = (dz * cgate).astype(BF16)

        @pl.when(i == 0)
        def _():
            dw_ref[...] = jnp.zeros(dw_ref.shape, F32)

        for tap, zs in enumerate((z2, z1, z)):
            dw_ref[tap:tap + 1, :] += jnp.sum(dy * zs, axis=0, keepdims=True)

    last_halo = t // SUBLANES - 1
    return pl.pallas_call(
        body, grid=(nb,),
        in_specs=[pl.BlockSpec((tm, BCH), lambda i: (i, 0)),
                  pl.BlockSpec((SUBLANES, BCH), lambda i: (jnp.maximum(i * halo_blocks - 1, 0), 0)),
                  pl.BlockSpec((SUBLANES, BCH), lambda i: (jnp.minimum((i + 1) * halo_blocks, last_halo), 0)),
                  pl.BlockSpec((tm, cw), lambda i: (i, 1)),
                  pl.BlockSpec((SUBLANES, cw), lambda i: (jnp.minimum((i + 1) * halo_blocks, last_halo), 1)),
                  _resident(conv_w.shape)],
        out_specs=[pl.BlockSpec((tm, BCH), lambda i: (i, 0)), pl.BlockSpec((SUBLANES, cw), lambda i: (0, 0))],
        out_shape=[_sds((t, BCH), BF16), _sds((SUBLANES, cw), F32)],
        compiler_params=_cp(), name="conv_bwd")(bch, bch, bch, dmix, dmix, conv_w)


def _attn_bwd_prep(o, dmix, qp, lse_pad):
    t = o.shape[0]
    tm = min(ROW_TILE, t)
    hd = HEAD_DIM
    sel_lse = _piece_selector(Q_LSE, -1.0)
    sel_delta = _piece_selector(DO_DELTA, -1.0)
    head_of = jnp.asarray([[1.0 if col == row // hd else 0.0 for col in range(LANES)] for row in range(FOX_WIDTH)], F32)

    def body(o_ref, do_ref, qp_ref, lse_ref, sl_ref, sd_ref, seg_ref, qb_ref, dob_ref):
        do = do_ref[...]
        delta = jnp.dot(o_ref[...].astype(F32) * do, seg_ref[...], precision=HIGHEST, preferred_element_type=F32)
        lse_extra = jnp.dot(_piece_rows(lse_ref[...]), sl_ref[...], preferred_element_type=F32)
        do_extra = jnp.dot(_piece_rows(delta), sd_ref[...], preferred_element_type=F32).astype(BF16)
        for h in range(FOX_HEADS):
            hs = slice(h * hd, (h + 1) * hd)
            dob_ref[h, :, :hd] = do[:, hs].astype(BF16)
            dob_ref[h, :, hd:] = do_extra[:, hs]
            qb_ref[h, :, :hd] = qp_ref[h, :, :hd]
            qb_ref[h, :, hd:] = (qp_ref[h, :, hd:].astype(F32) + lse_extra[:, hs]).astype(BF16)

    row3 = pl.BlockSpec((FOX_HEADS, tm, LANES), lambda i: (0, i, 0))
    return pl.pallas_call(
        body, grid=(t // tm,),
        in_specs=[pl.BlockSpec((tm, FOX_WIDTH), lambda i: (i, 0)), pl.BlockSpec((tm, FOX_WIDTH), lambda i: (i, 0)), row3,
                  pl.BlockSpec((tm, LANES), lambda i: (i, 0)), _resident(sel_lse.shape), _resident(sel_delta.shape),
                  _resident(head_of.shape)],
        out_specs=[row3, row3], out_shape=[_sds((FOX_HEADS, t, LANES), BF16)] * 2,
        compiler_params=_cp(), name="attn_bwd_prep")(o, dmix, qp, lse_pad, sel_lse, sel_delta, head_of)


def _attn_bwd(qb, kp, vp, dob, kt):
    t = qb.shape[1]
    bq = min(ATT_BLOCK, t)
    nq = t // bq
    i_tab, j_tab = _triangle(nq, key_major=True)

    def body(it_ref, jt_ref, q_ref, k_ref, v_ref, do_ref, kt_ref, dqt_ref, dk_ref, dv_ref, dk_sc, dv_sc):
        s = pl.program_id(1)
        i, j = it_ref[s], jt_ref[s]

        @pl.when(s == 0)
        def _():
            dqt_ref[...] = jnp.zeros(dqt_ref.shape, F32)

        @pl.when(i == j)
        def _():
            dk_sc[...] = jnp.zeros(dk_sc.shape, F32)
            dv_sc[...] = jnp.zeros(dv_sc.shape, F32)

        cols = pl.ds(pl.multiple_of(i * bq, bq), bq)

        def sweep(masked):
            def scores(h):
                return (lax.dot_general(k_ref[h], q_ref[h], NT, preferred_element_type=F32),
                        lax.dot_general(v_ref[h], do_ref[h], NT, preferred_element_type=F32))

            def accumulate(h, ptb, dstb):
                dv_sc[h] += jnp.dot(ptb, do_ref[h], preferred_element_type=F32)
                dk_sc[h] += jnp.dot(dstb, q_ref[h], preferred_element_type=F32)
                dqt_ref[h, :, cols] += jnp.dot(kt_ref[h], dstb, preferred_element_type=F32)

            ahead, behind = scores(0), None
            for h in range(ATT_BWD_HEADS):
                st, dpt = ahead
                if h + 1 < ATT_BWD_HEADS:
                    ahead = scores(h + 1)
                if behind is not None:
                    accumulate(*behind)
                if masked:
                    key = lax.broadcasted_iota(jnp.int32, (bq, bq), 0)
                    qry = lax.broadcasted_iota(jnp.int32, (bq, bq), 1)
                    st = jnp.where(key <= qry, st, NEG)
                pt = jnp.exp(st)
                behind = (h, pt.astype(BF16), (pt * dpt).astype(BF16))
            accumulate(*behind)

        @pl.when(i == j)
        def _():
            sweep(True)

        @pl.when(i > j)
        def _():
            sweep(False)

        @pl.when(i == nq - 1)
        def _():
            dk_ref[...] = dk_sc[...]
            dv_ref[...] = dv_sc[...].astype(BF16)

    nh = ATT_BWD_HEADS
    qblk = pl.BlockSpec((nh, bq, LANES), lambda hp, s, it, jt: (hp, it[s], 0))
    kblk = pl.BlockSpec((nh, bq, LANES), lambda hp, s, it, jt: (hp, jt[s], 0))
    grid_spec = pltpu.PrefetchScalarGridSpec(
        num_scalar_prefetch=2, grid=(FOX_HEADS // nh, i_tab.shape[0]),
        in_specs=[qblk, kblk, kblk, qblk, pl.BlockSpec((nh, LANES, bq), lambda hp, s, it, jt: (hp, 0, jt[s]))],
        out_specs=[pl.BlockSpec((nh, LANES, t), lambda hp, s, it, jt: (hp, 0, 0), pipeline_mode=pl.Buffered(1)),
                   kblk, kblk],
        scratch_shapes=[pltpu.VMEM((nh, bq, LANES), F32), pltpu.VMEM((nh, bq, LANES), F32)])
    return pl.pallas_call(body, grid_spec=grid_spec,
                          out_shape=[_sds((FOX_HEADS, LANES, t), F32), _sds((FOX_HEADS, t, LANES), F32),
                                     _sds((FOX_HEADS, t, LANES), BF16)],
                          compiler_params=_cp(), name="attn_bwd")(i_tab, j_tab, qb, kp, vp, dob, kt)


def _attn_unpack(dqt, dkp, dvp):
    t = dkp.shape[1]
    tm = min(ROW_TILE, t)
    hd = HEAD_DIM

    def body(dqt_ref, dk_ref, dv_ref, o_ref, dc_ref):
        for h in range(FOX_HEADS):
            dq = dqt_ref[h].T
            o_ref[:, h * hd:(h + 1) * hd] = (dq[:, :hd] * (hd ** -0.5)).astype(BF16)
            o_ref[:, FOX_WIDTH + h * hd:FOX_WIDTH + (h + 1) * hd] = dk_ref[h, :, :hd].astype(BF16)
            o_ref[:, 2 * FOX_WIDTH + h * hd:2 * FOX_WIDTH + (h + 1) * hd] = dv_ref[h, :, :hd]
            dc_ref[:, h:h + 1] = dq[:, K_ONE:K_ONE + 1] - dk_ref[h, :, Q_ONE:Q_ONE + 1]

    row3 = pl.BlockSpec((FOX_HEADS, tm, LANES), lambda i: (0, i, 0))
    return pl.pallas_call(
        body, grid=(t // tm,),
        in_specs=[pl.BlockSpec((FOX_HEADS, LANES, tm), lambda i: (0, 0, i)), row3, row3],
        out_specs=[pl.BlockSpec((tm, QKV), lambda i: (i, 0)), pl.BlockSpec((tm, FOX_HEADS), lambda i: (i, 0))],
        out_shape=[_sds((t, QKV), BF16), _sds((t, FOX_HEADS), F32)],
        compiler_params=_cp(), name="attn_unpack")(dqt, dkp, dvp)


def _adamw(parts, w, m, v, name, layer=None, into=None):
    nl, r, c = w.shape
    fits = [cand for cand in [*range(SUBLANES, r, SUBLANES), r] if r % cand == 0 and cand * c * 4 <= ADAMW_BLOCK_BYTES]
    tr = max(fits) if fits else r
    npart = len(parts)
    bc1 = 1.0 - ADAM_B1 ** ADAM_STEP
    bc2 = 1.0 - ADAM_B2 ** ADAM_STEP

    def body(*refs):
        p_refs = refs[:npart]
        w_ref, m_ref, v_ref = refs[npart:npart + 3]
        g_ref, d_ref, nm_ref, nv_ref = refs[-4:]
        sums = []
        for p_ref in p_refs:
            acc = p_ref[0, 0].astype(F32)
            for s in range(1, p_ref.shape[0]):
                acc = acc + p_ref[s, 0].astype(F32)
            sums.append(acc)
        g = sums[0]
        for extra in sums[1:]:
            g = g + extra
        nm = ADAM_B1 * m_ref[0] + (1.0 - ADAM_B1) * g
        nv = ADAM_B2 * v_ref[0] + (1.0 - ADAM_B2) * (g * g)
        m_hat = nm / bc1
        v_hat = nv / bc2
        g_ref[0] = g
        d_ref[0] = -ADAM_LR * (m_hat / (jnp.sqrt(v_hat) + ADAM_EPS) + ADAM_WD * w_ref[0])
        nm_ref[0] = nm
        nv_ref[0] = nv

    first = 0 if layer is None else layer
    blk = pl.BlockSpec((1, tr, c), lambda l, i: (first + l, i, 0))
    extra = [] if into is None else list(into)
    return pl.pallas_call(
        body, grid=(nl if layer is None else 1, r // tr),
        in_specs=[pl.BlockSpec((p.shape[0], 1, tr, c), lambda l, i: (0, l, i, 0)) for p in parts] + [blk, blk, blk]
        + [_ANY_SPEC] * len(extra),
        out_specs=[blk] * 4, out_shape=[_sds(w.shape, F32)] * 4,
        input_output_aliases={npart + 3 + k: k for k in range(len(extra))},
        compiler_params=_cp(), name=name)(*parts, w, m, v, *extra)


def _to_rows(a):
    flat = a.reshape(-1)
    pad = (-flat.shape[0]) % LANES
    if pad:
        flat = jnp.concatenate([flat, jnp.zeros((pad,), flat.dtype)])
    return flat.reshape(-1, LANES)


def _by_owner_cols(dw):
    k, n = dw.shape
    return dw.reshape(k, N_CHIPS, n // N_CHIPS).transpose(1, 0, 2)[:, None]


def _ffn_fwd(xin_ln, xin_b, wi, wo, g, b, layer):
    gu, h = _ffn_in(xin_b, wi, f"ffn_in_{layer}")
    y_b, xhat, rstd = _mm_res_ln([(h, wo)], xin_ln, g, b, f"ffn_out_ln_{layer}")
    return y_b, (xin_b, gu, h, xhat, rstd)


def _ffn_bwd(dz, saved, wi, wo, ln_below, layer):
    xin_b, gu, h, _, _ = saved
    dgu, *below = _ffn_bwd_rows(dz, wo, gu, wi, ln_below, f"ffn_bwd_rows_{layer}")
    g_out = _mm_tn(h, dz, f"ffn_dw_out_{layer}", tn=D_MODEL, tk=HALF_HIDDEN, tt=REDUCE_TILE // 2)
    g_in = _mm_tn(xin_b, dgu, f"ffn_dw_in_{layer}", tn=HALF_HIDDEN, stack_cols=True)
    return below, g_in, g_out.reshape(N_CHIPS, FFN_HIDDEN // N_CHIPS, D_MODEL)


def kernel(x, even_w_in, even_b_f, even_conv_w, even_w_out, odd_w_in, odd_v_ln_g, odd_v_ln_b, odd_w_s, odd_b_s, odd_w_out, mix_ln_g, mix_ln_b, ffn_w_in, ffn_w_out, ffn_ln_g, ffn_ln_b, loss_target, m_even_w_in, m_even_b_f, m_even_conv_w, m_even_w_out, m_odd_w_in, m_odd_v_ln_g, m_odd_v_ln_b, m_odd_w_s, m_odd_b_s, m_odd_w_out, m_mix_ln_g, m_mix_ln_b, m_ffn_w_in, m_ffn_w_out, m_ffn_ln_g, m_ffn_ln_b, v_even_w_in, v_even_b_f, v_even_conv_w, v_even_w_out, v_odd_w_in, v_odd_v_ln_g, v_odd_v_ln_b, v_odd_w_s, v_odd_b_s, v_odd_w_out, v_mix_ln_g, v_mix_ln_b, v_ffn_w_in, v_ffn_w_out, v_ffn_ln_g, v_ffn_ln_b):
    t = x.shape[1]
    d = D_MODEL
    chip = 2 * lax.axis_index("x") + lax.axis_index("y")
    x2d = x[0]
    target = loss_target[0]

    small_shard = jnp.concatenate([odd_v_ln_g.reshape(2, LANES), odd_v_ln_b.reshape(2, LANES),
                                   even_conv_w.reshape(CONV_K, LANES), jnp.zeros((1, LANES), F32)], axis=0)
    first = [even_w_in[0].astype(BF16)]
    second = [even_w_out[0].astype(BF16), small_shard]
    later = [odd_w_in[0].astype(BF16), odd_w_out[0].astype(BF16), ffn_w_in[0].astype(BF16), ffn_w_in[1].astype(BF16),
             ffn_w_out[0].astype(BF16), ffn_w_out[1].astype(BF16)]
    first_h, first_tok = _split_start(first, "gather4", "gather_first_start")
    second_h, second_tok = _split_start(second, "gather4", "gather_second_start", after=first_tok)
    later_h, later_tok = _split_start(later, "gather4", "gather_later_start", after=second_tok)
    (g_ewi,) = _gathered(first_h, "gather_first_wait", later_tok)
    ewi = g_ewi.transpose(1, 0, 2).reshape(d, EVEN_IN)
    w_even_in = jnp.concatenate([ewi[:, :QKV], ewi[:, QKV + FOX_HEADS:], ewi[:, QKV:QKV + FOX_HEADS],
                                 jnp.zeros((d, LANES - FOX_HEADS), BF16)], axis=1)
    chunk_id = jnp.arange(GMLP_BLOCK) // CHUNK
    gmask = chunk_id[None, :] <= chunk_id[:, None]
    w_spatial = jnp.where(gmask[None], odd_w_s[0], 0.0).astype(BF16)
    bs_col = odd_b_s[0].T
    b_f_col = even_b_f.reshape(FOX_HEADS, 1)
    ln = lambda p, l: p[l:l + 1]

    qkv, bch, fl = _proj(x2d, w_even_in, [(0, QKV, BF16), (QKV, QKV + BCH, F32), (QKV + BCH, EVEN_IN_PAD, F32)], "even_proj")
    fl3 = fl[:, :FOX_HEADS].T.reshape(FOX_HEADS, t // LANES, LANES).transpose(1, 0, 2)
    c3 = _fgate_fwd(fl3, b_f_col)
    c_rows = c3.transpose(1, 0, 2).reshape(FOX_HEADS, t)
    head_lanes = lambda rows: jnp.pad(rows.T, ((0, 0), (0, LANES - FOX_HEADS)))
    qp, kp, vp, kt, vt = _attn_pack(qkv, head_lanes(c_rows))
    attn, lse = _attn_fwd(qp, kp, vt)
    g_ewo, g_small = _gathered(second_h, "gather_second_wait", attn)
    w_even_out = g_ewo.reshape(d, d)
    v_ln_g = g_small[:, 0:2].reshape(1, d)
    v_ln_b = g_small[:, 2:4].reshape(1, d)
    conv_w = g_small[:, 4:7].transpose(1, 0, 2).reshape(CONV_K, CONV_WIDTH)
    conv = _conv_fwd(bch, conv_w)
    x1_b, xh1, rs1 = _mm_res_ln([(attn, w_even_out[:FOX_WIDTH]), (conv, w_even_out[FOX_WIDTH:])], x2d,
                                ln(mix_ln_g, 0), ln(mix_ln_b, 0), "even_out_ln")
    w_odd_in, g_owo, w_fi0, w_fi1, g_fo0, g_fo1 = _gathered(later_h, "gather_later_wait", x1_b)
    w_odd_out = g_owo.reshape(d, d)
    w_ffn_in = [w_fi0, w_fi1]
    w_ffn_out = [g_fo0.reshape(FFN_HIDDEN, d), g_fo1.reshape(FFN_HIDDEN, d)]
    x2_b, ffn0 = _ffn_fwd((xh1, ln(mix_ln_g, 0), ln(mix_ln_b, 0)), x1_b, w_ffn_in[0], w_ffn_out[0],
                          ln(ffn_ln_g, 0), ln(ffn_ln_b, 0), 0)

    sv_odd, rs_odd, gated = _gmlp_fwd(x2_b, w_odd_in, v_ln_g, v_ln_b, w_spatial, bs_col)
    x3_b, xh3, rs3 = _mm_res_ln([(gated, w_odd_out)], (ffn0[3], ln(ffn_ln_g, 0), ln(ffn_ln_b, 0)),
                                ln(mix_ln_g, 1), ln(mix_ln_b, 1), "odd_out_ln")
    _, ffn1 = _ffn_fwd((xh3, ln(mix_ln_g, 1), ln(mix_ln_b, 1)), x3_b, w_ffn_in[1], w_ffn_out[1],
                       ln(ffn_ln_g, 1), ln(ffn_ln_b, 1), 1)

    sq, dz4, d_fg1, d_fb1 = _loss_ln_bwd(ffn1[3], ffn1[4], ln(ffn_ln_g, 1), ln(ffn_ln_b, 1), target)
    loss = lax.psum(0.5 / d * jnp.sum(sq), ("x", "y", "c"))
    (dz3, d_mg1, d_mb1), gi_f1, go_f1 = _ffn_bwd(dz4, ffn1, w_ffn_in[1], w_ffn_out[1], (xh3, rs3, ln(mix_ln_g, 1)), 1)

    dgated = _mm_nt([(dz3, 0, 0, d)], [w_odd_out], "odd_dgated")
    go_odd = _mm_tn(gated, dz3, "odd_dw_out", tn=d).reshape(N_CHIPS, 1, d // N_CHIPS, d)
    da_odd, dws, dbs_col, d_vg, d_vb = _gmlp_bwd(dgated, sv_odd, rs_odd, v_ln_g, v_ln_b, w_spatial, bs_col)
    gi_odd = _mm_tn(x2_b, da_odd, "odd_dw_in", tn=d // 2, stack_cols=True)[:, None]
    dz2, d_fg0, d_fb0 = _mm_nt([(da_odd, 0, 0, 0)], [w_odd_in], "odd_dx", res=dz3,
                               ln=(ffn0[3], ffn0[4], ln(ffn_ln_g, 0)))
    (dz1, d_mg0, d_mb0), gi_f0, go_f0 = _ffn_bwd(dz2, ffn0, w_ffn_in[0], w_ffn_out[0], (xh1, rs1, ln(mix_ln_g, 0)), 0)

    sent_early = [gi_odd, go_odd, gi_f0[:, None], gi_f1[:, None], go_f0[:, None], go_f1[:, None]]
    early_h, early_tok = _split_start(sent_early, "scatter4", "scatter_early_start")
    dmix = _mm_nt([(dz1, 0, 0, d)], [w_even_out], "even_dmix", after=early_tok)
    go_even = jnp.concatenate([_mm_tn(attn, dz1, "even_dw_out_attn", tn=d), _mm_tn(conv, dz1, "even_dw_out_conv", tn=d)],
                              axis=0).reshape(N_CHIPS, 1, d // N_CHIPS, d)
    dbch, dconv_w8 = _conv_bwd(bch, dmix, conv_w)
    qb, dob = _attn_bwd_prep(attn, dmix, qp, head_lanes(lse.reshape(FOX_HEADS, t)))
    dqkv, dc_col = _attn_unpack(*_attn_bwd(qb, kp, vp, dob, kt))
    dc3 = dc_col.T.reshape(FOX_HEADS, t // LANES, LANES).transpose(1, 0, 2)
    dfl3, d_bf = _fgate_bwd(dc3, fl3, b_f_col)
    dfl = jnp.concatenate([dfl3.transpose(1, 0, 2).reshape(FOX_HEADS, t).T.astype(BF16),
                           jnp.zeros((t, LANES - FOX_HEADS), BF16)], axis=1)

    dws_masked = jnp.where(gmask[None], dws, 0.0)
    rep_names = ["odd_w_s", "odd_b_s", "mix_ln_g", "mix_ln_b", "ffn_ln_g", "ffn_ln_b", "even_b_f"]
    rep_grads = [dws_masked, dbs_col.T, jnp.concatenate([d_mg0, d_mg1]), jnp.concatenate([d_mb0, d_mb1]),
                 jnp.concatenate([d_fg0, d_fg1]), jnp.concatenate([d_fb0, d_fb1]), d_bf.reshape(1, FOX_HEADS)]
    rep_w = [(odd_w_s, m_odd_w_s, v_odd_w_s), (odd_b_s, m_odd_b_s, v_odd_b_s), (mix_ln_g, m_mix_ln_g, v_mix_ln_g),
             (mix_ln_b, m_mix_ln_b, v_mix_ln_b), (ffn_ln_g, m_ffn_ln_g, v_ffn_ln_g), (ffn_ln_b, m_ffn_ln_b, v_ffn_ln_b),
             (even_b_f, m_even_b_f, v_even_b_f)]
    rep_rows = [_to_rows(gr) for gr in rep_grads]
    n_rep = sum(r.shape[0] for r in rep_rows)
    pad_rep = (-n_rep) % SUBLANES
    dconv_w = dconv_w8[:CONV_K].reshape(CONV_K, N_CHIPS, LANES).transpose(1, 0, 2).reshape(N_CHIPS * CONV_K, LANES)
    packed = jnp.concatenate(rep_rows + [jnp.zeros((pad_rep, LANES), F32), d_vg.reshape(SUBLANES, LANES),
                                         d_vb.reshape(SUBLANES, LANES), dconv_w, jnp.zeros((4, LANES), F32)], axis=0)
    small_h, small_tok = _split_start([packed], "gather8", "gather_small_start")

    swap_h, swap_tok = _split_start(_scattered(early_h, "scatter_early_wait", small_tok), "swap2", "swap_early_start")
    dw_qkv = _mm_tn(dqkv, x2d, "even_dw_qkv", tn=d, tk=QKV // 2, after=swap_tok)
    dw_bch = _mm_tn(dbch, x2d, "even_dw_bch", tn=d, tk=BCH // 2)
    dw_f = _mm_tn(dfl, x2d, "even_dw_f", tn=d)
    gi_even = jnp.concatenate([dw_qkv, dw_f[:FOX_HEADS], dw_bch], axis=0).reshape(N_CHIPS, 1, -1, LANES)
    sent_late = [gi_even, go_even]
    late_h, late_tok = _split_start(sent_late, "scatter4", "scatter_late_start")
    grad_x = _mm_nt([(dqkv, 0, 0, QKV), (dbch, 0, QKV, QKV + BCH), (dfl, 0, QKV + BCH, EVEN_IN_PAD)], [w_even_in],
                    "even_dx", res=dz1, after=late_tok)
    mine, theirs = _split_wait(swap_h, "swap_early_wait", grad_x)
    res = {}
    res["odd_w_in"] = _adamw([mine[0], theirs[0]], odd_w_in, m_odd_w_in, v_odd_w_in, "adamw_odd_w_in")
    res["odd_w_out"] = _adamw([mine[1], theirs[1]], odd_w_out, m_odd_w_out, v_odd_w_out, "adamw_odd_w_out")
    for nm, at, (w, m, v) in (("ffn_w_in", 2, (ffn_w_in, m_ffn_w_in, v_ffn_w_in)),
                              ("ffn_w_out", 4, (ffn_w_out, m_ffn_w_out, v_ffn_w_out))):
        upper = _adamw([mine[at + 1], theirs[at + 1]], w, m, v, f"adamw_{nm}_1", layer=1)
        res[nm] = _adamw([mine[at], theirs[at]], w, m, v, f"adamw_{nm}_0", layer=0, into=upper)
    mine_late = _scattered(late_h, "scatter_late_wait", res["ffn_w_out"][0])
    theirs_late = _exchange(mine_late, "swap2", "swap_late")
    rows = lambda a: jnp.swapaxes(a, 1, 2).reshape(1, -1, LANES)
    back = lambda a: jnp.swapaxes(a.reshape(1, EVEN_IN // N_CHIPS, d), 1, 2)
    res["even_w_in"] = [back(o) for o in _adamw([mine_late[0], theirs_late[0]], rows(even_w_in), rows(m_even_w_in),
                                                rows(v_even_w_in), "adamw_even_w_in")]
    res["even_w_out"] = _adamw([mine_late[1], theirs_late[1]], even_w_out, m_even_w_out, v_even_w_out,
                               "adamw_even_w_out")
    (packed,), (gathered,) = _split_wait(small_h, "gather_small_wait", theirs_late[0])
    gathered = lax.dynamic_update_index_in_dim(gathered, packed, 4 * lax.axis_index("x") + 2 * lax.axis_index("y")
                                               + lax.axis_index("c"), 0)

    base = n_rep + pad_rep
    own_rows = jnp.concatenate([
        lax.dynamic_slice_in_dim(gathered, base + 2 * chip, 2, axis=1),
        lax.dynamic_slice_in_dim(gathered, base + SUBLANES + 2 * chip, 2, axis=1),
        lax.dynamic_slice_in_dim(gathered, base + 2 * SUBLANES + CONV_K * chip, CONV_K, axis=1),
        jnp.zeros((N_DEV, 1, LANES), F32)], axis=1)
    small_parts = jnp.concatenate([gathered[:, :base], own_rows], axis=1)[:, None]

    def pack_small(get):
        rows = [_to_rows(get(tw)) for tw in rep_w] + [jnp.zeros((pad_rep, LANES), F32)]
        rows += [get(sh).reshape(-1, LANES) for sh in ((odd_v_ln_g, m_odd_v_ln_g, v_odd_v_ln_g),
                                                       (odd_v_ln_b, m_odd_v_ln_b, v_odd_v_ln_b),
                                                       (even_conv_w, m_even_conv_w, v_even_conv_w))]
        return jnp.concatenate(rows + [jnp.zeros((1, LANES), F32)], axis=0)[None]

    small_out = _adamw([small_parts], pack_small(lambda tw: tw[0]), pack_small(lambda tw: tw[1]),
                       pack_small(lambda tw: tw[2]), "adamw_small")

    def unpack_small(rows3):
        rows = rows3[0]
        out, off = {}, 0
        for nm, (w, _, _), r in zip(rep_names, rep_w, rep_rows):
            out[nm] = rows[off:off + r.shape[0]].reshape(-1)[:w.size].reshape(w.shape)
            off += r.shape[0]
        off += pad_rep
        out["odd_v_ln_g"] = rows[off:off + 2].reshape(odd_v_ln_g.shape)
        out["odd_v_ln_b"] = rows[off + 2:off + 4].reshape(odd_v_ln_b.shape)
        out["even_conv_w"] = rows[off + 4:off + 4 + CONV_K].reshape(even_conv_w.shape)
        return out

    small = [unpack_small(o) for o in small_out]
    order = ["even_w_in", "even_b_f", "even_conv_w", "even_w_out", "odd_w_in", "odd_v_ln_g", "odd_v_ln_b", "odd_w_s",
             "odd_b_s", "odd_w_out", "mix_ln_g", "mix_ln_b", "ffn_w_in", "ffn_w_out", "ffn_ln_g", "ffn_ln_b"]
    outs = [loss, grad_x[None]]
    for kind in range(4):
        for nm in order:
            outs.append(res[nm][kind] if nm in res else small[kind][nm])
    return tuple(outs)
```

```python
import functools
import math

import jax
import jax.numpy as jnp
from jax import lax
from jax.experimental import pallas as pl
from jax.experimental.pallas import tpu as pltpu

F32 = jnp.float32
BF16 = jnp.bfloat16

D_MODEL = 1024
FOX_HEADS = 8
HEAD_DIM = 64
HEAD_PAIRS = FOX_HEADS // 2
FOX_WIDTH = FOX_HEADS * HEAD_DIM
CONV_WIDTH = 512
CONV_K = 3
QKV = 3 * FOX_WIDTH
BCH = 3 * CONV_WIDTH
EVEN_IN = QKV + FOX_HEADS + BCH
EVEN_IN_PAD = QKV + BCH + 128
GMLP_BLOCK = 128
GMLP_GROUPS = 8
CHUNK = 64
FFN_HIDDEN = 2816
HALF_HIDDEN = FFN_HIDDEN // 2
ALPHA = 4.0 ** 0.25
LN_EPS = 1e-5
ADAM_LR = 0.001
ADAM_B1 = 0.9
ADAM_B2 = 0.999
ADAM_EPS = 1e-08
ADAM_WD = 0.01
ADAM_STEP = 10
N_CHIPS = 4
N_DEV = 8
LANES = 128
SUBLANES = 8
ROW_TILE = 512
FFN_FUSED_ROW_TILE = 256
REDUCE_TILE = 2048
ATT_BLOCK = 512
ATT_FWD_HEADS = 8
ATT_BWD_HEADS = 4
ADAMW_BLOCK_BYTES = 2 ** 20
VMEM_LIMIT = 56 * 2 ** 20
NEG = -1e30
MESH = pl.DeviceIdType.MESH
HIGHEST = lax.Precision.HIGHEST
Q_C, Q_ONE, Q_LSE = 64, 67, 70
K_ONE, K_C, K_ONE2 = 64, 67, 70
V_ONE = 64
DO_DELTA = 65
NT = (((1,), (1,)), ((), ()))
TN = (((0,), (0,)), ((), ()))


def _cp():
    return pltpu.CompilerParams(vmem_limit_bytes=VMEM_LIMIT)


def _resident(shape):
    zeros = (0,) * len(shape)
    return pl.BlockSpec(shape, lambda *_: zeros, pipeline_mode=pl.Buffered(1))


def _sds(shape, dtype):
    return jax.ShapeDtypeStruct(tuple(shape), dtype)


_MASKS = {
    "gather4": [(1, 0, 0), (0, 1, 0), (1, 1, 0)],
    "scatter4": [(1, 0, 0), (0, 1, 0), (1, 1, 0)],
    "swap2": [(0, 0, 1)],
    "gather8": [(0, 0, 1), (0, 1, 0), (0, 1, 1), (1, 0, 0), (1, 0, 1), (1, 1, 0), (1, 1, 1)],
}


def _exchange(arrs, mode, name):
    n = len(arrs)
    masks = _MASKS[mode]
    npeer = len(masks)
    lead = {"gather4": N_CHIPS, "gather8": N_DEV}.get(mode)
    out_shapes = [_sds(((lead,) if lead else ()) + a.shape, a.dtype) for a in arrs]

    def body(*refs):
        ins, outs = refs[:n], refs[n:2 * n]
        send_sems, recv_sems, loc_sems = refs[2 * n:]
        x, y, c = lax.axis_index("x"), lax.axis_index("y"), lax.axis_index("c")
        chip, dev = 2 * x + y, 4 * x + 2 * y + c
        sends, recvs, locs = [], [], []
        for k in range(n):
            if mode == "gather4":
                locs.append(pltpu.make_async_copy(ins[k], outs[k].at[chip], loc_sems.at[k]))
            elif mode == "scatter4":
                locs.append(pltpu.make_async_copy(ins[k].at[chip], outs[k].at[chip], loc_sems.at[k]))
            elif mode == "gather8":
                locs.append(pltpu.make_async_copy(ins[k], outs[k].at[dev], loc_sems.at[k]))
        for cp in locs:
            cp.start()
        for k in range(n):
            for j, (dx, dy, dc) in enumerate(masks):
                px = 1 - x if dx else x
                py = 1 - y if dy else y
                pc = 1 - c if dc else c
                pchip, pdev = 2 * px + py, 4 * px + 2 * py + pc
                if mode == "gather4":
                    src, dst, land = ins[k], outs[k].at[chip], outs[k].at[pchip]
                elif mode == "scatter4":
                    src, dst, land = ins[k].at[pchip], outs[k].at[chip], outs[k].at[pchip]
                elif mode == "swap2":
                    src, dst, land = ins[k], outs[k], outs[k]
                else:
                    src, dst, land = ins[k], outs[k].at[dev], outs[k].at[pdev]
                s = k * npeer + j
                kw = dict(send_sem=send_sems.at[s], recv_sem=recv_sems.at[s], device_id=(px, py, pc),
                          device_id_type=MESH)
                cp = pltpu.make_async_remote_copy(src_ref=src, dst_ref=dst, **kw)
                cp.start()
                sends.append(cp)
                recvs.append(pltpu.make_async_remote_copy(src_ref=src, dst_ref=land, **kw))
        for cp in recvs:
            cp.wait_recv()
        for cp in sends:
            cp.wait_send()
        for cp in locs:
            cp.wait()

    any_spec = pl.BlockSpec(memory_space=pl.ANY)
    outs = pl.pallas_call(
        body,
        out_shape=out_shapes,
        in_specs=[any_spec] * n,
        out_specs=[any_spec] * n,
        scratch_shapes=[pltpu.SemaphoreType.DMA((n * npeer,)), pltpu.SemaphoreType.DMA((n * npeer,)),
                        pltpu.SemaphoreType.DMA((max(n, 1),))],
        name=name,
    )(*arrs)
    return list(outs)


_HBM_SPEC = pl.BlockSpec(memory_space=pltpu.HBM)
_SEM_SPEC = pl.BlockSpec(memory_space=pltpu.SEMAPHORE)
_ANY_SPEC = pl.BlockSpec(memory_space=pl.ANY)
_EFFECT = pltpu.SideEffectType.DATAFLOW_SIDE_EFFECTING


def _split_copies(mode, ins, lands, send_sems, recv_sems):
    x, y, c = lax.axis_index("x"), lax.axis_index("y"), lax.axis_index("c")
    chip, dev = 2 * x + y, 4 * x + 2 * y + c
    masks = _MASKS[mode]
    out = []
    for k in range(len(ins)):
        for j, (dx, dy, dc) in enumerate(masks):
            px = 1 - x if dx else x
            py = 1 - y if dy else y
            pc = 1 - c if dc else c
            pchip, pdev = 2 * px + py, 4 * px + 2 * py + pc
            if mode == "gather4":
                src, dst, land = ins[k], lands[k].at[chip], lands[k].at[pchip]
            elif mode == "scatter4":
                src, dst, land = ins[k].at[pchip], lands[k].at[chip], lands[k].at[pchip]
            elif mode == "swap2":
                src, dst, land = ins[k], lands[k], lands[k]
            else:
                src, dst, land = ins[k], lands[k].at[dev], lands[k].at[pdev]
            s = k * len(masks) + j
            kw = dict(send_sem=send_sems.at[s], recv_sem=recv_sems.at[s], device_id=(px, py, pc), device_id_type=MESH)
            out.append((pltpu.make_async_remote_copy(src_ref=src, dst_ref=dst, **kw),
                        pltpu.make_async_remote_copy(src_ref=src, dst_ref=land, **kw)))
    return out


def _split_start(arrs, mode, name, after=None):
    n = len(arrs)
    nsem = n * len(_MASKS[mode])
    lead = {"gather4": (N_CHIPS,), "gather8": (N_DEV,)}.get(mode, ())
    land_shapes = [lead + a.shape for a in arrs]

    def body(*refs):
        ins, lands = refs[:n], refs[n:2 * n]
        outs = refs[2 * n + (after is not None):]
        for start, _ in _split_copies(mode, ins, lands, outs[0], outs[1]):
            start.start()
        outs[-1][...] = jnp.zeros(outs[-1].shape, F32)

    srcs = [pltpu.with_memory_space_constraint(a, pltpu.HBM) for a in arrs]
    empties = [pltpu.with_memory_space_constraint(lax.empty(s, a.dtype), pltpu.HBM) for s, a in zip(land_shapes, arrs)]
    res = pl.pallas_call(
        body, name=name,
        out_shape=(pltpu.SemaphoreType.DMA((nsem,)), pltpu.SemaphoreType.DMA((nsem,)),
                   *[pltpu.HBM(a.shape, a.dtype) for a in arrs],
                   *[pltpu.HBM(s, a.dtype) for s, a in zip(land_shapes, arrs)],
                   _sds((SUBLANES, LANES), F32)),
        in_specs=[_HBM_SPEC] * (2 * n) + ([_ANY_SPEC] if after is not None else []),
        out_specs=(_SEM_SPEC, _SEM_SPEC, *[_HBM_SPEC] * (2 * n), pl.BlockSpec(memory_space=pltpu.VMEM)),
        input_output_aliases={k: 2 + k for k in range(2 * n)},
        compiler_params=pltpu.CompilerParams(has_side_effects=_EFFECT),
    )(*srcs, *empties, *([after] if after is not None else []))
    return dict(mode=mode, n=n, sems=res[:2], bufs=res[2:2 + 2 * n]), res[-1]


def _split_wait(handle, name, after):
    n, mode = handle["n"], handle["mode"]

    def body(*refs):
        ins, lands = refs[:n], refs[n:2 * n]
        send_sems, recv_sems = refs[2 * n], refs[2 * n + 1]
        for _, arrival in _split_copies(mode, ins, lands, send_sems, recv_sems):
            arrival.wait_send()
            arrival.wait_recv()

    bufs = handle["bufs"]
    res = pl.pallas_call(
        body, name=name,
        out_shape=tuple(pltpu.HBM(b.shape, b.dtype) for b in bufs),
        in_specs=[_HBM_SPEC] * (2 * n) + [_SEM_SPEC, _SEM_SPEC, _ANY_SPEC],
        out_specs=tuple([_HBM_SPEC] * (2 * n)),
        input_output_aliases={k: k for k in range(2 * n)},
        compiler_params=pltpu.CompilerParams(has_side_effects=_EFFECT),
    )(*bufs, *handle["sems"], after)
    return list(res[:n]), list(res[n:])


def _with_own(landed, own):
    chip = 2 * lax.axis_index("x") + lax.axis_index("y")
    return lax.dynamic_update_index_in_dim(landed, own, chip, 0)


def _gathered(handle, name, after):
    sent, landed = _split_wait(handle, name, after)
    return [_with_own(g, own) for g, own in zip(landed, sent)]


def _scattered(handle, name, after):
    chip = 2 * lax.axis_index("x") + lax.axis_index("y")
    sent, landed = _split_wait(handle, name, after)
    return [_with_own(r, lax.dynamic_index_in_dim(g, chip, 0, keepdims=False)) for r, g in zip(landed, sent)]


def _sigmoid(x):
    return 0.5 * jnp.tanh(0.5 * x) + 0.5


def _log_sigmoid(x):
    e = jnp.exp(-jnp.abs(x))
    log1p = jnp.where(e < 1e-2, e * (1.0 - e * (0.5 - e * (1.0 / 3.0))), jnp.log(1.0 + e))
    return jnp.minimum(x, 0.0) - log1p


def _ln_fwd(z):
    mu = jnp.mean(z, axis=-1, keepdims=True)
    zc = z - mu
    var = jnp.mean(zc * zc, axis=-1, keepdims=True)
    rstd = lax.rsqrt(var + LN_EPS)
    return zc * rstd, rstd


def _ln_bwd(dy, xhat, rstd, g):
    dxh = dy * g
    m1 = jnp.mean(dxh, axis=-1, keepdims=True)
    m2 = jnp.mean(dxh * xhat, axis=-1, keepdims=True)
    dz = rstd * (dxh - m1 - xhat * m2)
    return dz, jnp.sum(dy * xhat, axis=0, keepdims=True), jnp.sum(dy, axis=0, keepdims=True)


def _shift_down(z, halo):
    r = lax.broadcasted_iota(jnp.int32, z.shape, 0)
    z1 = jnp.where(r == 0, halo[7:8, :], pltpu.roll(z, 1, 0))
    z2 = jnp.where(r == 0, halo[6:7, :], jnp.where(r == 1, halo[7:8, :], pltpu.roll(z, 2, 0)))
    return z1, z2


def _shift_up(z, halo):
    n = z.shape[0]
    r = lax.broadcasted_iota(jnp.int32, z.shape, 0)
    z1 = jnp.where(r == n - 1, halo[0:1, :], pltpu.roll(z, n - 1, 0))
    z2 = jnp.where(r == n - 1, halo[1:2, :], jnp.where(r == n - 2, halo[0:1, :], pltpu.roll(z, n - 2, 0)))
    return z1, z2


def _accumulate(ref, first, value):
    @pl.when(first)
    def _():
        ref[...] = value

    @pl.when(jnp.logical_not(first))
    def _():
        ref[...] += value


def _proj(x, w, splits, name):
    t, k = x.shape
    tm = min(ROW_TILE, t)

    def body(x_ref, w_ref, *outs):
        a = x_ref[...].astype(BF16)
        for (lo, hi, dt), o in zip(splits, outs):
            o[...] = jnp.dot(a, w_ref[:, lo:hi], preferred_element_type=F32).astype(dt)

    return pl.pallas_call(
        body, grid=(t // tm,),
        in_specs=[pl.BlockSpec((tm, k), lambda i: (i, 0)), _resident(w.shape)],
        out_specs=[pl.BlockSpec((tm, hi - lo), lambda i: (i, 0)) for lo, hi, _ in splits],
        out_shape=[_sds((t, hi - lo), dt) for lo, hi, dt in splits],
        compiler_params=_cp(), name=name)(x, w)


def _fgate_fwd(fl3, b_f):
    nc = fl3.shape[0]

    def body(f_ref, b_ref, c_ref):
        r = lax.broadcasted_iota(jnp.int32, (LANES, LANES), 0)
        cidx = lax.broadcasted_iota(jnp.int32, (LANES, LANES), 1)
        upper = (r <= cidx).astype(F32)

        def step(i, carry):
            lf = _log_sigmoid(f_ref[i] + b_ref[...])
            cc = jnp.dot(lf, upper, precision=HIGHEST, preferred_element_type=F32) + carry
            c_ref[i] = cc
            return cc[:, LANES - 1:LANES]

        lax.fori_loop(0, nc, step, jnp.zeros((FOX_HEADS, 1), F32))

    return pl.pallas_call(body, out_shape=_sds(fl3.shape, F32), name="fgate_fwd")(fl3, b_f)


def _fgate_bwd(dc3, fl3, b_f):
    nc = fl3.shape[0]

    def body(dc_ref, f_ref, b_ref, df_ref, db_ref):
        r = lax.broadcasted_iota(jnp.int32, (LANES, LANES), 0)
        cidx = lax.broadcasted_iota(jnp.int32, (LANES, LANES), 1)
        lower = (r >= cidx).astype(F32)

        def step(n, carry):
            suffix, db = carry
            i = nc - 1 - n
            dlf = jnp.dot(dc_ref[i], lower, precision=HIGHEST, preferred_element_type=F32) + suffix
            df = dlf * (1.0 - _sigmoid(f_ref[i] + b_ref[...]))
            df_ref[i] = df
            return dlf[:, 0:1], db + jnp.sum(df, axis=1, keepdims=True)

        zero = jnp.zeros((FOX_HEADS, 1), F32)
        _, db = lax.fori_loop(0, nc, step, (zero, zero))
        db_ref[...] = db

    return pl.pallas_call(body, out_shape=[_sds(fl3.shape, F32), _sds((FOX_HEADS, 1), F32)],
                          name="fgate_bwd")(dc3, fl3, b_f)


def _split3(c):
    hi = c.astype(BF16).astype(F32)
    mid = (c - hi).astype(BF16).astype(F32)
    lo = (c - hi - mid).astype(BF16).astype(F32)
    return hi, mid, lo


PIECE_ONE = 3 * FOX_HEADS


def _piece_rows(values):
    hi, mid, lo = _split3(values)
    lane = lax.broadcasted_iota(jnp.int32, values.shape, 1)
    row = hi + pltpu.roll(mid, FOX_HEADS, 1) + pltpu.roll(lo, 2 * FOX_HEADS, 1) + jnp.where(lane == PIECE_ONE, 1.0, 0.0)
    return row.astype(BF16)


def _piece_selector(start, sign, ones=()):
    sel = [[0.0] * FOX_WIDTH for _ in range(LANES)]
    for h in range(FOX_HEADS):
        for n in range(3):
            sel[n * FOX_HEADS + h][h * HEAD_DIM + start - HEAD_DIM + n] = sign
        for lane in ones:
            sel[PIECE_ONE][h * HEAD_DIM + lane - HEAD_DIM] = 1.0
    return jnp.asarray(sel, BF16)


def _attn_pack(qkv, c_pad):
    t = qkv.shape[0]
    tm = min(ROW_TILE, t)
    hd = HEAD_DIM
    sel_q = _piece_selector(Q_C, 1.0, range(Q_ONE, Q_ONE + 3))
    sel_k = _piece_selector(K_C, -1.0, [*range(K_ONE, K_ONE + 3), *range(K_ONE2, K_ONE2 + 3)])
    sel_v = _piece_selector(HEAD_DIM, 0.0, range(V_ONE, V_ONE + 4))

    def body(x_ref, c_ref, sq_ref, sk_ref, sv_ref, qp_ref, kp_ref, vp_ref, kt_ref, vt_ref):
        pieces = _piece_rows(c_ref[...])
        q_extra = jnp.dot(pieces, sq_ref[...], preferred_element_type=F32).astype(BF16)
        k_extra = jnp.dot(pieces, sk_ref[...], preferred_element_type=F32).astype(BF16)
        v_extra = jnp.dot(pieces, sv_ref[...], preferred_element_type=F32).astype(BF16)
        for h in range(FOX_HEADS):
            hs = slice(h * hd, (h + 1) * hd)
            qp_ref[h, :, :hd] = (x_ref[:, hs].astype(F32) * (hd ** -0.5)).astype(BF16)
            qp_ref[h, :, hd:] = q_extra[:, hs]
            kp_ref[h, :, :hd] = x_ref[:, FOX_WIDTH + h * hd:FOX_WIDTH + (h + 1) * hd]
            kp_ref[h, :, hd:] = k_extra[:, hs]
            vp_ref[h, :, :hd] = x_ref[:, 2 * FOX_WIDTH + h * hd:2 * FOX_WIDTH + (h + 1) * hd]
            vp_ref[h, :, hd:] = v_extra[:, hs]
            kt_ref[h] = kp_ref[h].T
            vt_ref[h] = vp_ref[h].T

    row3 = pl.BlockSpec((FOX_HEADS, tm, LANES), lambda i: (0, i, 0))
    col3 = pl.BlockSpec((FOX_HEADS, LANES, tm), lambda i: (0, 0, i))
    sel = _resident(sel_q.shape)
    return pl.pallas_call(
        body, grid=(t // tm,),
        in_specs=[pl.BlockSpec((tm, QKV), lambda i: (i, 0)), pl.BlockSpec((tm, LANES), lambda i: (i, 0)), sel, sel, sel],
        out_specs=[row3, row3, row3, col3, col3],
        out_shape=[_sds((FOX_HEADS, t, LANES), BF16)] * 3 + [_sds((FOX_HEADS, LANES, t), BF16)] * 2,
        compiler_params=_cp(), name="attn_pack")(qkv, c_pad, sel_q, sel_k, sel_v)


def _triangle(nq, key_major):
    if key_major:
        pairs = [(i, j) for j in range(nq) for i in range(j, nq)]
    else:
        pairs = [(i, j) for i in range(nq) for j in range(i + 1)]
    return jnp.asarray([p[0] for p in pairs], jnp.int32), jnp.asarray([p[1] for p in pairs], jnp.int32)


def _attn_fwd(qp, kp, vt):
    t = qp.shape[1]
    bq = min(ATT_BLOCK, t)
    nq = t // bq
    nh = ATT_FWD_HEADS
    i_tab, j_tab = _triangle(nq, key_major=False)

    def body(it_ref, jt_ref, q_ref, k_ref, vt_ref, o_ref, lse_ref, m_sc, acc_sc):
        s = pl.program_id(1)
        i, j = it_ref[s], jt_ref[s]

        @pl.when(j == 0)
        def _():
            m_sc[...] = jnp.full(m_sc.shape, NEG, F32)
            acc_sc[...] = jnp.zeros(acc_sc.shape, F32)

        def sweep(masked):
            scores = lambda h: lax.dot_general(k_ref[h], q_ref[h], NT, preferred_element_type=F32)

            def accumulate(h, pt, rescale):
                acc_sc[h] = rescale * acc_sc[h] + jnp.dot(vt_ref[h], pt, preferred_element_type=F32)

            ahead, behind = scores(0), None
            for h in range(nh):
                st = ahead
                if h + 1 < nh:
                    ahead = scores(h + 1)
                if behind is not None:
                    accumulate(*behind)
                if masked:
                    key = lax.broadcasted_iota(jnp.int32, (bq, bq), 0)
                    qry = lax.broadcasted_iota(jnp.int32, (bq, bq), 1)
                    st = jnp.where(key <= qry, st, NEG)
                m_prev = m_sc[h]
                m_new = jnp.maximum(m_prev, jnp.max(st, axis=0, keepdims=True))
                behind = (h, jnp.exp(st - m_new).astype(BF16), jnp.exp(m_prev - m_new))
                m_sc[h] = m_new
            accumulate(*behind)

        @pl.when(j < i)
        def _():
            sweep(False)

        @pl.when(j == i)
        def _():
            sweep(True)
            for h in range(nh):
                acc = acc_sc[h]
                denom = acc[V_ONE:V_ONE + 1, :]
                o_ref[:, h * HEAD_DIM:(h + 1) * HEAD_DIM] = (acc[:HEAD_DIM, :] / denom).T.astype(BF16)
                lse_ref[h] = m_sc[h] + jnp.log(denom)

    grid_spec = pltpu.PrefetchScalarGridSpec(
        num_scalar_prefetch=2, grid=(FOX_HEADS // nh, i_tab.shape[0]),
        in_specs=[pl.BlockSpec((nh, bq, LANES), lambda hp, s, it, jt: (hp, it[s], 0)),
                  pl.BlockSpec((nh, bq, LANES), lambda hp, s, it, jt: (hp, jt[s], 0)),
                  pl.BlockSpec((nh, LANES, bq), lambda hp, s, it, jt: (hp, 0, jt[s]))],
        out_specs=[pl.BlockSpec((bq, nh * HEAD_DIM), lambda hp, s, it, jt: (it[s], hp)),
                   pl.BlockSpec((nh, 1, bq), lambda hp, s, it, jt: (hp, 0, it[s]))],
        scratch_shapes=[pltpu.VMEM((nh, 1, bq), F32), pltpu.VMEM((nh, LANES, bq), F32)])
    return pl.pallas_call(body, grid_spec=grid_spec,
                          out_shape=[_sds((t, FOX_WIDTH), BF16), _sds((FOX_HEADS, 1, t), F32)],
                          compiler_params=_cp(), name="attn_fwd")(i_tab, j_tab, qp, kp, vt)


def _conv_fwd(bch, conv_w):
    t = bch.shape[0]
    tm = min(ROW_TILE, t)
    halo_blocks = tm // SUBLANES
    cw = CONV_WIDTH

    def body(cur_ref, prev_ref, w_ref, o_ref):
        i = pl.program_id(0)
        z = cur_ref[:, cw:2 * cw] * cur_ref[:, 2 * cw:]
        zp = jnp.where(i == 0, 0.0, prev_ref[:, cw:2 * cw] * prev_ref[:, 2 * cw:])
        z1, z2 = _shift_down(z, zp)
        y = w_ref[0:1, :] * z2 + w_ref[1:2, :] * z1 + w_ref[2:3, :] * z
        o_ref[...] = (cur_ref[:, :cw] * y).astype(BF16)

    return pl.pallas_call(
        body, grid=(t // tm,),
        in_specs=[pl.BlockSpec((tm, BCH), lambda i: (i, 0)),
                  pl.BlockSpec((SUBLANES, BCH), lambda i: (jnp.maximum(i * halo_blocks - 1, 0), 0)),
                  _resident(conv_w.shape)],
        out_specs=pl.BlockSpec((tm, cw), lambda i: (i, 0)),
        out_shape=_sds((t, cw), BF16), compiler_params=_cp(), name="conv_fwd")(bch, bch, conv_w)


def _mm_res_ln(pairs, res, g, b, name):
    from_ln = isinstance(res, tuple)
    res_args = list(res) if from_ln else [res]
    t, d = res_args[0].shape
    tm = min(ROW_TILE, t)
    n = len(pairs)

    def body(*refs):
        a_refs, w_refs = refs[:n], refs[n:2 * n]
        res_refs = refs[2 * n:2 * n + len(res_args)]
        g_ref, b_ref, yb_ref, xh_ref, rs_ref = refs[2 * n + len(res_args):]
        r = res_refs[0][...]
        if from_ln:
            r = r * res_refs[1][...] + res_refs[2][...]
        z = ALPHA * r
        for a_ref, w_ref in zip(a_refs, w_refs):
            z = z + jnp.dot(a_ref[...].astype(BF16), w_ref[...], preferred_element_type=F32)
        xhat, rstd = _ln_fwd(z)
        yb_ref[...] = (xhat * g_ref[...] + b_ref[...]).astype(BF16)
        xh_ref[...] = xhat
        rs_ref[...] = rstd

    row = lambda i: (i, 0)
    full = pl.BlockSpec((tm, d), row)
    return pl.pallas_call(
        body, grid=(t // tm,),
        in_specs=[pl.BlockSpec((tm, a.shape[1]), row) for a, _ in pairs] + [_resident(w.shape) for _, w in pairs]
        + [full] + [_resident(a.shape) for a in res_args[1:]] + [_resident(g.shape), _resident(b.shape)],
        out_specs=[full, full, pl.BlockSpec((tm, 1), row)],
        out_shape=[_sds((t, d), BF16), _sds((t, d), F32), _sds((t, 1), F32)],
        compiler_params=_cp(), name=name)(*[a for a, _ in pairs], *[w for _, w in pairs], *res_args, g, b)


def _gmlp_fwd(x, w_in, vg, vb, wm, bs_col):
    t, d = x.shape
    tm = min(ROW_TILE, t)
    gb = GMLP_BLOCK

    def body(x_ref, w_ref, vg_ref, vb_ref, wm_ref, bs_ref, sv_ref, rs_ref, o_ref, a_sc):
        xb = x_ref[...].astype(BF16)
        nc = w_ref.shape[2]
        for j in range(w_ref.shape[0]):
            a_sc[:, j * nc:(j + 1) * nc] = jnp.dot(xb, w_ref[j], preferred_element_type=F32)
        halves = []
        for half in range(2):
            a = a_sc[:, half * d:(half + 1) * d]
            cdf = 0.5 * (1.0 + lax.erf(a * (2.0 ** -0.5)))
            halves.append(a * cdf)
            slope = cdf + a * (jnp.exp(-0.5 * a * a) * (1.0 / math.sqrt(2.0 * math.pi)))
            sv_ref[:, (2 * half + 1) * d:(2 * half + 2) * d] = slope.astype(BF16)
        u = halves[0]
        vhat, rstd = _ln_fwd(halves[1])
        sv_ref[:, :d] = u.astype(BF16)
        sv_ref[:, 2 * d:3 * d] = vhat.astype(BF16)
        rs_ref[...] = rstd
        vln = (vhat * vg_ref[...] + vb_ref[...]).astype(BF16)
        for blk in range(tm // gb):
            rs = slice(blk * gb, (blk + 1) * gb)
            for gi in range(GMLP_GROUPS):
                cs = slice(gi * gb, (gi + 1) * gb)
                s = jnp.dot(wm_ref[gi], vln[rs, cs], preferred_element_type=F32) + bs_ref[:, gi:gi + 1]
                o_ref[rs, cs] = (u[rs, cs] * s).astype(BF16)

    row = lambda i: (i, 0)
    return pl.pallas_call(
        body, grid=(t // tm,),
        in_specs=[pl.BlockSpec((tm, d), row), _resident(w_in.shape), _resident(vg.shape), _resident(vb.shape),
                  _resident(wm.shape), _resident(bs_col.shape)],
        out_specs=[pl.BlockSpec((tm, 4 * d), row), pl.BlockSpec((tm, 1), row), pl.BlockSpec((tm, d), row)],
        out_shape=[_sds((t, 4 * d), BF16), _sds((t, 1), F32), _sds((t, d), BF16)],
        scratch_shapes=[pltpu.VMEM((tm, 2 * d), F32)],
        compiler_params=_cp(), name="gmlp_fwd")(x, w_in, vg, vb, wm, bs_col)


def _loss_ln_bwd(xhat, rstd, g, b, target):
    t, d = xhat.shape
    tm = min(ROW_TILE, t)

    def body(xh_ref, rs_ref, g_ref, b_ref, t_ref, sq_ref, dz_ref, dg_ref, db_ref):
        first = pl.program_id(0) == 0
        xh = xh_ref[...]
        err = xh * g_ref[...] + b_ref[...] - t_ref[...]
        dz, dg, db = _ln_bwd(err * (1.0 / d), xh, rs_ref[...], g_ref[...])
        dz_ref[...] = dz
        _accumulate(sq_ref, first, jnp.sum(err * err, axis=0, keepdims=True))
        _accumulate(dg_ref, first, dg)
        _accumulate(db_ref, first, db)

    row = lambda i: (i, 0)
    vec = pl.BlockSpec((1, d), lambda i: (0, 0))
    return pl.pallas_call(
        body, grid=(t // tm,),
        in_specs=[pl.BlockSpec((tm, d), row), pl.BlockSpec((tm, 1), row), _resident(g.shape), _resident(b.shape),
                  pl.BlockSpec((tm, d), row)],
        out_specs=[vec, pl.BlockSpec((tm, d), row), vec, vec],
        out_shape=[_sds((1, d), F32), _sds((t, d), F32), _sds((1, d), F32), _sds((1, d), F32)],
        compiler_params=_cp(), name="loss_ln_bwd")(xhat, rstd, g, b, target)


def _mm_nt(pairs, ws, name, *, tm=ROW_TILE, res=None, ln=None, out_dtype=F32, after=None):
    t = pairs[0][0].shape[0]
    k = ws[0].shape[-2]
    tm = min(tm, t)
    n, nw = len(pairs), len(ws)

    def body(*refs):
        refs = refs[after is not None:]
        a_refs, w_refs = refs[:n], refs[n:n + nw]
        rest = list(refs[n + nw:])
        dx = None
        for a_ref, (_, wi, lo, hi) in zip(a_refs, pairs):
            w_ref = w_refs[wi]
            if len(w_ref.shape) == 3:
                nc = w_ref.shape[2]
                parts = [lax.dot_general(a_ref[:, j * nc:(j + 1) * nc].astype(BF16), w_ref[j], NT,
                                         preferred_element_type=F32) for j in range(w_ref.shape[0])]
            else:
                parts = [lax.dot_general(a_ref[...].astype(BF16), w_ref[:, lo:hi], NT, preferred_element_type=F32)]
            for part in parts:
                dx = part if dx is None else dx + part
        if res is not None:
            dx = dx + ALPHA * rest.pop(0)[...]
        if ln is None:
            rest[0][...] = dx.astype(out_dtype)
            return
        xh_ref, rs_ref, g_ref, dz_ref, dg_ref, db_ref = rest
        first = pl.program_id(0) == 0
        dz, dg, db = _ln_bwd(dx, xh_ref[...], rs_ref[...], g_ref[...])
        dz_ref[...] = dz
        _accumulate(dg_ref, first, dg)
        _accumulate(db_ref, first, db)

    row = lambda i: (i, 0)
    in_specs = [pl.BlockSpec((tm, a.shape[1]), row) for a, _, _, _ in pairs] + [_resident(w.shape) for w in ws]
    args = [a for a, _, _, _ in pairs] + list(ws)
    if res is not None:
        in_specs.append(pl.BlockSpec((tm, k), row))
        args.append(res)
    if ln is None:
        out_specs = pl.BlockSpec((tm, k), row)
        out_shape = _sds((t, k), out_dtype)
    else:
        xhat, rstd, g = ln
        in_specs += [pl.BlockSpec((tm, k), row), pl.BlockSpec((tm, 1), row), _resident(g.shape)]
        args += [xhat, rstd, g]
        vec = pl.BlockSpec((1, k), lambda i: (0, 0))
        out_specs = [pl.BlockSpec((tm, k), row), vec, vec]
        out_shape = [_sds((t, k), F32), _sds((1, k), F32), _sds((1, k), F32)]
    if after is not None:
        in_specs.insert(0, _ANY_SPEC)
        args.insert(0, after)
    return pl.pallas_call(body, grid=(t // tm,), in_specs=in_specs, out_specs=out_specs, out_shape=out_shape,
                          compiler_params=_cp(), name=name)(*args)


def _mm_tn(a, b, name, *, tn, tk=None, tt=None, stack_cols=False, out_dtype=BF16, after=None):
    t, k = a.shape
    n = b.shape[1]
    tk = k if tk is None else tk
    tt = min(REDUCE_TILE if tt is None else tt, t)
    nt = t // tt

    def body(a_ref, b_ref, *rest):
        o_ref, acc_ref = rest[after is not None:]
        s = pl.program_id(2)
        part = lax.dot_general(a_ref[...].astype(BF16), b_ref[...].astype(BF16), TN, preferred_element_type=F32)
        _accumulate(acc_ref, s == 0, part)

        @pl.when(s == nt - 1)
        def _():
            o_ref[...] = acc_ref[...].astype(out_dtype).reshape(o_ref.shape)

    if stack_cols:
        assert tk == k
        out_spec = pl.BlockSpec((1, k, tn), lambda kk, j, s: (j, 0, 0))
        out_shape = _sds((n // tn, k, tn), out_dtype)
    else:
        out_spec = pl.BlockSpec((tk, tn), lambda kk, j, s: (kk, j))
        out_shape = _sds((k, n), out_dtype)
    return pl.pallas_call(
        body, grid=(k // tk, n // tn, nt),
        in_specs=[pl.BlockSpec((tt, tk), lambda kk, j, s: (s, kk)), pl.BlockSpec((tt, tn), lambda kk, j, s: (s, j))]
        + ([_ANY_SPEC] if after is not None else []),
        out_specs=out_spec, out_shape=out_shape,
        scratch_shapes=[pltpu.VMEM((tk, tn), F32)],
        compiler_params=_cp(), name=name)(a, b, *([after] if after is not None else []))


def _ffn_bwd_rows(dz, wo, gu, wi, ln_below, name):
    t, d = dz.shape
    tm = min(FFN_FUSED_ROW_TILE, t)
    hh = HALF_HIDDEN
    xhat, rstd, g = ln_below

    def body(dz_ref, wo_ref, gu_ref, wi_ref, xh_ref, rs_ref, g_ref, dgu_ref, dzb_ref, dg_ref, db_ref):
        first = pl.program_id(0) == 0
        a = dz_ref[...].astype(BF16)
        for c in range(2):
            gs, us = slice(c * hh, (c + 1) * hh), slice(FFN_HIDDEN + c * hh, FFN_HIDDEN + (c + 1) * hh)
            dh = lax.dot_general(a, wo_ref[gs, :], NT, preferred_element_type=F32)
            dgu_ref[:, gs] = (dh * gu_ref[:, gs].astype(F32)).astype(BF16)
            dgu_ref[:, us] = (dh * gu_ref[:, us].astype(F32)).astype(BF16)
        dx = ALPHA * dz_ref[...]
        for j in range(wi_ref.shape[0]):
            dx = dx + lax.dot_general(dgu_ref[:, j * hh:(j + 1) * hh], wi_ref[j], NT, preferred_element_type=F32)
        dzb, dg, db = _ln_bwd(dx, xh_ref[...], rs_ref[...], g_ref[...])
        dzb_ref[...] = dzb
        _accumulate(dg_ref, first, dg)
        _accumulate(db_ref, first, db)

    row = lambda i: (i, 0)
    wide, full = pl.BlockSpec((tm, 2 * FFN_HIDDEN), row), pl.BlockSpec((tm, d), row)
    vec = pl.BlockSpec((1, d), lambda i: (0, 0))
    return pl.pallas_call(
        body, grid=(t // tm,),
        in_specs=[full, _resident(wo.shape), wide, _resident(wi.shape), full, pl.BlockSpec((tm, 1), row),
                  _resident(g.shape)],
        out_specs=[wide, full, vec, vec],
        out_shape=[_sds((t, 2 * FFN_HIDDEN), BF16), _sds((t, d), F32), _sds((1, d), F32), _sds((1, d), F32)],
        compiler_params=_cp(), name=name)(dz, wo, gu, wi, xhat, rstd, g)


def _gmlp_bwd(dgated, saved, rstd_v, vg, vb, wm, bs_col):
    t, d = dgated.shape
    d2 = 2 * d
    tm = min(ROW_TILE, t)
    gb = GMLP_BLOCK

    def body(dg_ref, sv_ref, rs_ref, vg_ref, vb_ref, wm_ref, bs_ref, da_ref, dws_ref, dbs_ref, dvg_ref, dvb_ref, dvln_sc):
        first = pl.program_id(0) == 0
        u = sv_ref[:, :d].astype(F32)
        vhat = sv_ref[:, 2 * d:3 * d].astype(F32)
        rstd = rs_ref[...]
        vln = (vhat * vg_ref[...] + vb_ref[...]).astype(BF16)
        dgate = dg_ref[...]

        @pl.when(first)
        def _():
            dws_ref[...] = jnp.zeros(dws_ref.shape, F32)
            dbs_ref[...] = jnp.zeros(dbs_ref.shape, F32)

        for blk in range(tm // gb):
            rs = slice(blk * gb, (blk + 1) * gb)
            for gi in range(GMLP_GROUPS):
                cs = slice(gi * gb, (gi + 1) * gb)
                vblk = vln[rs, cs]
                s = jnp.dot(wm_ref[gi], vblk, preferred_element_type=F32) + bs_ref[:, gi:gi + 1]
                dgb = dgate[rs, cs]
                da_ref[rs, cs] = (dgb * s * sv_ref[rs, d + gi * gb:d + (gi + 1) * gb].astype(F32)).astype(BF16)
                ds = dgb * u[rs, cs]
                dsb = ds.astype(BF16)
                dws_ref[gi] += lax.dot_general(dsb, vblk, NT, preferred_element_type=F32)
                dbs_ref[:, gi:gi + 1] += jnp.sum(ds, axis=1, keepdims=True)
                dvln_sc[rs, cs] = lax.dot_general(wm_ref[gi], dsb, TN, preferred_element_type=F32)
        dv, dvg, dvb = _ln_bwd(dvln_sc[...], vhat, rstd, vg_ref[...])
        da_ref[:, d:] = (dv * sv_ref[:, 3 * d:].astype(F32)).astype(BF16)
        _accumulate(dvg_ref, first, dvg)
        _accumulate(dvb_ref, first, dvb)

    row = lambda i: (i, 0)
    vec = pl.BlockSpec((1, d), lambda i: (0, 0))
    return pl.pallas_call(
        body, grid=(t // tm,),
        in_specs=[pl.BlockSpec((tm, d), row), pl.BlockSpec((tm, 4 * d), row), pl.BlockSpec((tm, 1), row),
                  _resident(vg.shape), _resident(vb.shape), _resident(wm.shape), _resident(bs_col.shape)],
        out_specs=[pl.BlockSpec((tm, d2), row), pl.BlockSpec(wm.shape, lambda i: (0, 0, 0)),
                   pl.BlockSpec(bs_col.shape, lambda i: (0, 0)), vec, vec],
        out_shape=[_sds((t, d2), BF16), _sds(wm.shape, F32), _sds(bs_col.shape, F32), _sds((1, d), F32), _sds((1, d), F32)],
        scratch_shapes=[pltpu.VMEM((tm, d), F32)],
        compiler_params=_cp(), name="gmlp_bwd")(dgated, saved, rstd_v, vg, vb, wm, bs_col)


def _conv_bwd(bch, dmix, conv_w):
    t = bch.shape[0]
    tm = min(ROW_TILE, t)
    nb = t // tm
    halo_blocks = tm // SUBLANES
    cw = CONV_WIDTH

    def body(cur_ref, prev_ref, next_ref, dc_ref, dn_ref, w_ref, o_ref, dw_ref):
        i = pl.program_id(0)
        bgate, cgate, hval = cur_ref[:, :cw], cur_ref[:, cw:2 * cw], cur_ref[:, 2 * cw:]
        z = cgate * hval
        zp = jnp.where(i == 0, 0.0, prev_ref[:, cw:2 * cw] * prev_ref[:, 2 * cw:])
        z1, z2 = _shift_down(z, zp)
        w0, w1, w2 = w_ref[0:1, :], w_ref[1:2, :], w_ref[2:3, :]
        dconv = dc_ref[...]
        o_ref[:, :cw] = (dconv * (w0 * z2 + w1 * z1 + w2 * z)).astype(BF16)
        dy = dconv * bgate
        dyn = jnp.where(i == nb - 1, 0.0, dn_ref[...] * next_ref[:, :cw])
        dy1, dy2 = _shift_up(dy, dyn)
        dz = w2 * dy + w1 * dy1 + w0 * dy2
        o_ref[:, cw:2 * cw] = (dz * hval).astype(BF16)
        o_ref[:, 2 * cw:] = (dz * cgate).astype(BF16)

        @pl.when(i == 0)
        def _():
            dw_ref[...] = jnp.zeros(dw_ref.shape, F32)

        for tap, zs in enumerate((z2, z1, z)):
            dw_ref[tap:tap + 1, :] += jnp.sum(dy * zs, axis=0, keepdims=True)

    last_halo = t // SUBLANES - 1
    return pl.pallas_call(
        body, grid=(nb,),
        in_specs=[pl.BlockSpec((tm, BCH), lambda i: (i, 0)),
                  pl.BlockSpec((SUBLANES, BCH), lambda i: (jnp.maximum(i * halo_blocks - 1, 0), 0)),
                  pl.BlockSpec((SUBLANES, BCH), lambda i: (jnp.minimum((i + 1) * halo_blocks, last_halo), 0)),
                  pl.BlockSpec((tm, cw), lambda i: (i, 1)),
                  pl.BlockSpec((SUBLANES, cw), lambda i: (jnp.minimum((i + 1) * halo_blocks, last_halo), 1)),
                  _resident(conv_w.shape)],
        out_specs=[pl.BlockSpec((tm, BCH), lambda i: (i, 0)), pl.BlockSpec((SUBLANES, cw), lambda i: (0, 0))],
        out_shape=[_sds((t, BCH), BF16), _sds((SUBLANES, cw), F32)],
        compiler_params=_cp(), name="conv_bwd")(bch, bch, bch, dmix, dmix, conv_w)


def _attn_bwd_prep(o, dmix, qp, lse_pad):
    t = o.shape[0]
    tm = min(ROW_TILE, t)
    hd = HEAD_DIM
    sel_lse = _piece_selector(Q_LSE, -1.0)
    sel_delta = _piece_selector(DO_DELTA, -1.0)
    head_of = jnp.asarray([[1.0 if col == row // hd else 0.0 for col in range(LANES)] for row in range(FOX_WIDTH)], F32)

    def body(o_ref, do_ref, qp_ref, lse_ref, sl_ref, sd_ref, seg_ref, qb_ref, dob_ref):
        do = do_ref[...]
        delta = jnp.dot(o_ref[...].astype(F32) * do, seg_ref[...], precision=HIGHEST, preferred_element_type=F32)
        lse_extra = jnp.dot(_piece_rows(lse_ref[...]), sl_ref[...], preferred_element_type=F32)
        do_extra = jnp.dot(_piece_rows(delta), sd_ref[...], preferred_element_type=F32).astype(BF16)
        for h in range(FOX_HEADS):
            hs = slice(h * hd, (h + 1) * hd)
            dob_ref[h, :, :hd] = do[:, hs].astype(BF16)
            dob_ref[h, :, hd:] = do_extra[:, hs]
            qb_ref[h, :, :hd] = qp_ref[h, :, :hd]
            qb_ref[h, :, hd:] = (qp_ref[h, :, hd:].astype(F32) + lse_extra[:, hs]).astype(BF16)

    row3 = pl.BlockSpec((FOX_HEADS, tm, LANES), lambda i: (0, i, 0))
    return pl.pallas_call(
        body, grid=(t // tm,),
        in_specs=[pl.BlockSpec((tm, FOX_WIDTH), lambda i: (i, 0)), pl.BlockSpec((tm, FOX_WIDTH), lambda i: (i, 0)), row3,
                  pl.BlockSpec((tm, LANES), lambda i: (i, 0)), _resident(sel_lse.shape), _resident(sel_delta.shape),
                  _resident(head_of.shape)],
        out_specs=[row3, row3], out_shape=[_sds((FOX_HEADS, t, LANES), BF16)] * 2,
        compiler_params=_cp(), name="attn_bwd_prep")(o, dmix, qp, lse_pad, sel_lse, sel_delta, head_of)


def _attn_bwd(qb, kp, vp, dob, kt):
    t = qb.shape[1]
    bq = min(ATT_BLOCK, t)
    nq = t // bq
    i_tab, j_tab = _triangle(nq, key_major=True)

    def body(it_ref, jt_ref, q_ref, k_ref, v_ref, do_ref, kt_ref, dqt_ref, dk_ref, dv_ref, dk_sc, dv_sc):
        s = pl.program_id(1)
        i, j = it_ref[s], jt_ref[s]

        @pl.when(s == 0)
        def _():
            dqt_ref[...] = jnp.zeros(dqt_ref.shape, F32)

        @pl.when(i == j)
        def _():
            dk_sc[...] = jnp.zeros(dk_sc.shape, F32)
            dv_sc[...] = jnp.zeros(dv_sc.shape, F32)

        cols = pl.ds(pl.multiple_of(i * bq, bq), bq)

        def sweep(masked):
            def scores(h):
                return (lax.dot_general(k_ref[h], q_ref[h], NT, preferred_element_type=F32),
                        lax.dot_general(v_ref[h], do_ref[h], NT, preferred_element_type=F32))

            def accumulate(h, ptb, dstb):
                dv_sc[h] += jnp.dot(ptb, do_ref[h], preferred_element_type=F32)
                dk_sc[h] += jnp.dot(dstb, q_ref[h], preferred_element_type=F32)
                dqt_ref[h, :, cols] += jnp.dot(kt_ref[h], dstb, preferred_element_type=F32)

            ahead, behind = scores(0), None
            for h in range(ATT_BWD_HEADS):
                st, dpt = ahead
                if h + 1 < ATT_BWD_HEADS:
                    ahead = scores(h + 1)
                if behind is not None:
                    accumulate(*behind)
                if masked:
                    key = lax.broadcasted_iota(jnp.int32, (bq, bq), 0)
                    qry = lax.broadcasted_iota(jnp.int32, (bq, bq), 1)
                    st = jnp.where(key <= qry, st, NEG)
                pt = jnp.exp(st)
                behind = (h, pt.astype(BF16), (pt * dpt).astype(BF16))
            accumulate(*behind)

        @pl.when(i == j)
        def _():
            sweep(True)

        @pl.when(i > j)
        def _():
            sweep(False)

        @pl.when(i == nq - 1)
        def _():
            dk_ref[...] = dk_sc[...]
            dv_ref[...] = dv_sc[...].astype(BF16)

    nh = ATT_BWD_HEADS
    qblk = pl.BlockSpec((nh, bq, LANES), lambda hp, s, it, jt: (hp, it[s], 0))
    kblk = pl.BlockSpec((nh, bq, LANES), lambda hp, s, it, jt: (hp, jt[s], 0))
    grid_spec = pltpu.PrefetchScalarGridSpec(
        num_scalar_prefetch=2, grid=(FOX_HEADS // nh, i_tab.shape[0]),
        in_specs=[qblk, kblk, kblk, qblk, pl.BlockSpec((nh, LANES, bq), lambda hp, s, it, jt: (hp, 0, jt[s]))],
        out_specs=[pl.BlockSpec((nh, LANES, t), lambda hp, s, it, jt: (hp, 0, 0), pipeline_mode=pl.Buffered(1)),
                   kblk, kblk],
        scratch_shapes=[pltpu.VMEM((nh, bq, LANES), F32), pltpu.VMEM((nh, bq, LANES), F32)])
    return pl.pallas_call(body, grid_spec=grid_spec,
                          out_shape=[_sds((FOX_HEADS, LANES, t), F32), _sds((FOX_HEADS, t, LANES), F32),
                                     _sds((FOX_HEADS, t, LANES), BF16)],
                          compiler_params=_cp(), name="attn_bwd")(i_tab, j_tab, qb, kp, vp, dob, kt)


def _attn_unpack(dqt, dkp, dvp):
    t = dkp.shape[1]
    tm = min(ROW_TILE, t)
    hd = HEAD_DIM

    def body(dqt_ref, dk_ref, dv_ref, o_ref, dc_ref):
        for h in range(FOX_HEADS):
            dq = dqt_ref[h].T
            o_ref[:, h * hd:(h + 1) * hd] = (dq[:, :hd] * (hd ** -0.5)).astype(BF16)
            o_ref[:, FOX_WIDTH + h * hd:FOX_WIDTH + (h + 1) * hd] = dk_ref[h, :, :hd].astype(BF16)
            o_ref[:, 2 * FOX_WIDTH + h * hd:2 * FOX_WIDTH + (h + 1) * hd] = dv_ref[h, :, :hd]
            dc_ref[:, h:h + 1] = dq[:, K_ONE:K_ONE + 1] - dk_ref[h, :, Q_ONE:Q_ONE + 1]

    row3 = pl.BlockSpec((FOX_HEADS, tm, LANES), lambda i: (0, i, 0))
    return pl.pallas_call(
        body, grid=(t // tm,),
        in_specs=[pl.BlockSpec((FOX_HEADS, LANES, tm), lambda i: (0, 0, i)), row3, row3],
        out_specs=[pl.BlockSpec((tm, QKV), lambda i: (i, 0)), pl.BlockSpec((tm, FOX_HEADS), lambda i: (i, 0))],
        out_shape=[_sds((t, QKV), BF16), _sds((t, FOX_HEADS), F32)],
        compiler_params=_cp(), name="attn_unpack")(dqt, dkp, dvp)


def _adamw(parts, w, m, v, name, layer=None, into=None):
    nl, r, c = w.shape
    fits = [cand for cand in [*range(SUBLANES, r, SUBLANES), r] if r % cand == 0 and cand * c * 4 <= ADAMW_BLOCK_BYTES]
    tr = max(fits) if fits else r
    npart = len(parts)
    bc1 = 1.0 - ADAM_B1 ** ADAM_STEP
    bc2 = 1.0 - ADAM_B2 ** ADAM_STEP

    def body(*refs):
        p_refs = refs[:npart]
        w_ref, m_ref, v_ref = refs[npart:npart + 3]
        g_ref, d_ref, nm_ref, nv_ref = refs[-4:]
        sums = []
        for p_ref in p_refs:
            acc = p_ref[0, 0].astype(F32)
            for s in range(1, p_ref.shape[0]):
                acc = acc + p_ref[s, 0].astype(F32)
            sums.append(acc)
        g = sums[0]
        for extra in sums[1:]:
            g = g + extra
        nm = ADAM_B1 * m_ref[0] + (1.0 - ADAM_B1) * g
        nv = ADAM_B2 * v_ref[0] + (1.0 - ADAM_B2) * (g * g)
        m_hat = nm / bc1
        v_hat = nv / bc2
        g_ref[0] = g
        d_ref[0] = -ADAM_LR * (m_hat / (jnp.sqrt(v_hat) + ADAM_EPS) + ADAM_WD * w_ref[0])
        nm_ref[0] = nm
        nv_ref[0] = nv

    first = 0 if layer is None else layer
    blk = pl.BlockSpec((1, tr, c), lambda l, i: (first + l, i, 0))
    extra = [] if into is None else list(into)
    return pl.pallas_call(
        body, grid=(nl if layer is None else 1, r // tr),
        in_specs=[pl.BlockSpec((p.shape[0], 1, tr, c), lambda l, i: (0, l, i, 0)) for p in parts] + [blk, blk, blk]
        + [_ANY_SPEC] * len(extra),
        out_specs=[blk] * 4, out_shape=[_sds(w.shape, F32)] * 4,
        input_output_aliases={npart + 3 + k: k for k in range(len(extra))},
        compiler_params=_cp(), name=name)(*parts, w, m, v, *extra)


def _to_rows(a):
    flat = a.reshape(-1)
    pad = (-flat.shape[0]) % LANES
    if pad:
        flat = jnp.concatenate([flat, jnp.zeros((pad,), flat.dtype)])
    return flat.reshape(-1, LANES)


def _by_owner_cols(dw):
    k, n = dw.shape
    return dw.reshape(k, N_CHIPS, n // N_CHIPS).transpose(1, 0, 2)[:, None]


def _ffn_fwd(xin_ln, xin_b, wi, wo, g, b, layer):
    t, d = xin_b.shape
    tm = min(FFN_FUSED_ROW_TILE, t)
    hh = HALF_HIDDEN
    rxh, rg, rb = xin_ln

    def body(x_ref, wi_ref, wo_ref, rxh_ref, rg_ref, rb_ref, g_ref, b_ref, gu_ref, h_ref, yb_ref, xh_ref, rs_ref):
        a = x_ref[...]
        for c in range(2):
            gs, us = slice(c * hh, (c + 1) * hh), slice(FFN_HIDDEN + c * hh, FFN_HIDDEN + (c + 1) * hh)
            gate = jnp.dot(a, wi_ref[c], preferred_element_type=F32)
            up = jnp.dot(a, wi_ref[2 + c], preferred_element_type=F32)
            sig = _sigmoid(gate)
            silu = gate * sig
            gu_ref[:, gs] = (up * sig * (1.0 + gate * (1.0 - sig))).astype(BF16)
            gu_ref[:, us] = silu.astype(BF16)
            h_ref[:, gs] = (silu * up).astype(BF16)
        z = ALPHA * (rxh_ref[...] * rg_ref[...] + rb_ref[...]) + jnp.dot(h_ref[...], wo_ref[...], preferred_element_type=F32)
        xhat, rstd = _ln_fwd(z)
        yb_ref[...] = (xhat * g_ref[...] + b_ref[...]).astype(BF16)
        xh_ref[...] = xhat
        rs_ref[...] = rstd

    row = lambda i: (i, 0)
    full = pl.BlockSpec((tm, d), row)
    vec = _resident(g.shape)
    gu, h, y_b, xhat, rstd = pl.pallas_call(
        body, grid=(t // tm,),
        in_specs=[full, _resident(wi.shape), _resident(wo.shape), full, vec, vec, vec, vec],
        out_specs=[pl.BlockSpec((tm, 2 * FFN_HIDDEN), row), pl.BlockSpec((tm, FFN_HIDDEN), row), full, full,
                   pl.BlockSpec((tm, 1), row)],
        out_shape=[_sds((t, 2 * FFN_HIDDEN), BF16), _sds((t, FFN_HIDDEN), BF16), _sds((t, d), BF16), _sds((t, d), F32),
                   _sds((t, 1), F32)],
        compiler_params=_cp(), name=f"ffn_fwd_rows_{layer}")(xin_b, wi, wo, rxh, rg, rb, g, b)
    return y_b, (xin_b, gu, h, xhat, rstd)


def _ffn_bwd(dz, saved, wi, wo, ln_below, layer):
    xin_b, gu, h, _, _ = saved
    dgu, *below = _ffn_bwd_rows(dz, wo, gu, wi, ln_below, f"ffn_bwd_rows_{layer}")
    g_out = _mm_tn(h, dz, f"ffn_dw_out_{layer}", tn=D_MODEL, tk=HALF_HIDDEN, tt=REDUCE_TILE // 2)
    g_in = _mm_tn(xin_b, dgu, f"ffn_dw_in_{layer}", tn=HALF_HIDDEN, stack_cols=True)
    return below, g_in, g_out.reshape(N_CHIPS, FFN_HIDDEN // N_CHIPS, D_MODEL)


def kernel(x, even_w_in, even_b_f, even_conv_w, even_w_out, odd_w_in, odd_v_ln_g, odd_v_ln_b, odd_w_s, odd_b_s, odd_w_out, mix_ln_g, mix_ln_b, ffn_w_in, ffn_w_out, ffn_ln_g, ffn_ln_b, loss_target, m_even_w_in, m_even_b_f, m_even_conv_w, m_even_w_out, m_odd_w_in, m_odd_v_ln_g, m_odd_v_ln_b, m_odd_w_s, m_odd_b_s, m_odd_w_out, m_mix_ln_g, m_mix_ln_b, m_ffn_w_in, m_ffn_w_out, m_ffn_ln_g, m_ffn_ln_b, v_even_w_in, v_even_b_f, v_even_conv_w, v_even_w_out, v_odd_w_in, v_odd_v_ln_g, v_odd_v_ln_b, v_odd_w_s, v_odd_b_s, v_odd_w_out, v_mix_ln_g, v_mix_ln_b, v_ffn_w_in, v_ffn_w_out, v_ffn_ln_g, v_ffn_ln_b):
    t = x.shape[1]
    d = D_MODEL
    chip = 2 * lax.axis_index("x") + lax.axis_index("y")
    x2d = x[0]
    target = loss_target[0]

    small_shard = jnp.concatenate([odd_v_ln_g.reshape(2, LANES), odd_v_ln_b.reshape(2, LANES),
                                   even_conv_w.reshape(CONV_K, LANES), jnp.zeros((1, LANES), F32)], axis=0)
    first = [even_w_in[0].astype(BF16)]
    second = [even_w_out[0].astype(BF16), small_shard]
    later = [odd_w_in[0].astype(BF16), odd_w_out[0].astype(BF16), ffn_w_in[0].astype(BF16), ffn_w_in[1].astype(BF16),
             ffn_w_out[0].astype(BF16), ffn_w_out[1].astype(BF16)]
    first_h, first_tok = _split_start(first, "gather4", "gather_first_start")
    second_h, second_tok = _split_start(second, "gather4", "gather_second_start", after=first_tok)
    later_h, later_tok = _split_start(later, "gather4", "gather_later_start", after=second_tok)
    (g_ewi,) = _gathered(first_h, "gather_first_wait", later_tok)
    ewi = g_ewi.transpose(1, 0, 2).reshape(d, EVEN_IN)
    w_even_in = jnp.concatenate([ewi[:, :QKV], ewi[:, QKV + FOX_HEADS:], ewi[:, QKV:QKV + FOX_HEADS],
                                 jnp.zeros((d, LANES - FOX_HEADS), BF16)], axis=1)
    chunk_id = jnp.arange(GMLP_BLOCK) // CHUNK
    gmask = chunk_id[None, :] <= chunk_id[:, None]
    w_spatial = jnp.where(gmask[None], odd_w_s[0], 0.0).astype(BF16)
    bs_col = odd_b_s[0].T
    b_f_col = even_b_f.reshape(FOX_HEADS, 1)
    ln = lambda p, l: p[l:l + 1]

    qkv, bch, fl = _proj(x2d, w_even_in, [(0, QKV, BF16), (QKV, QKV + BCH, F32), (QKV + BCH, EVEN_IN_PAD, F32)], "even_proj")
    fl3 = fl[:, :FOX_HEADS].T.reshape(FOX_HEADS, t // LANES, LANES).transpose(1, 0, 2)
    c3 = _fgate_fwd(fl3, b_f_col)
    c_rows = c3.transpose(1, 0, 2).reshape(FOX_HEADS, t)
    head_lanes = lambda rows: jnp.pad(rows.T, ((0, 0), (0, LANES - FOX_HEADS)))
    qp, kp, vp, kt, vt = _attn_pack(qkv, head_lanes(c_rows))
    attn, lse = _attn_fwd(qp, kp, vt)
    g_ewo, g_small = _gathered(second_h, "gather_second_wait", attn)
    w_even_out = g_ewo.reshape(d, d)
    v_ln_g = g_small[:, 0:2].reshape(1, d)
    v_ln_b = g_small[:, 2:4].reshape(1, d)
    conv_w = g_small[:, 4:7].transpose(1, 0, 2).reshape(CONV_K, CONV_WIDTH)
    conv = _conv_fwd(bch, conv_w)
    x1_b, xh1, rs1 = _mm_res_ln([(attn, w_even_out[:FOX_WIDTH]), (conv, w_even_out[FOX_WIDTH:])], x2d,
                                ln(mix_ln_g, 0), ln(mix_ln_b, 0), "even_out_ln")
    w_odd_in, g_owo, w_fi0, w_fi1, g_fo0, g_fo1 = _gathered(later_h, "gather_later_wait", x1_b)
    w_odd_out = g_owo.reshape(d, d)
    w_ffn_in = [w_fi0, w_fi1]
    w_ffn_out = [g_fo0.reshape(FFN_HIDDEN, d), g_fo1.reshape(FFN_HIDDEN, d)]
    x2_b, ffn0 = _ffn_fwd((xh1, ln(mix_ln_g, 0), ln(mix_ln_b, 0)), x1_b, w_ffn_in[0], w_ffn_out[0],
                          ln(ffn_ln_g, 0), ln(ffn_ln_b, 0), 0)

    sv_odd, rs_odd, gated = _gmlp_fwd(x2_b, w_odd_in, v_ln_g, v_ln_b, w_spatial, bs_col)
    x3_b, xh3, rs3 = _mm_res_ln([(gated, w_odd_out)], (ffn0[3], ln(ffn_ln_g, 0), ln(ffn_ln_b, 0)),
                                ln(mix_ln_g, 1), ln(mix_ln_b, 1), "odd_out_ln")
    _, ffn1 = _ffn_fwd((xh3, ln(mix_ln_g, 1), ln(mix_ln_b, 1)), x3_b, w_ffn_in[1], w_ffn_out[1],
                       ln(ffn_ln_g, 1), ln(ffn_ln_b, 1), 1)

    sq, dz4, d_fg1, d_fb1 = _loss_ln_bwd(ffn1[3], ffn1[4], ln(ffn_ln_g, 1), ln(ffn_ln_b, 1), target)
    loss = lax.psum(0.5 / d * jnp.sum(sq), ("x", "y", "c"))
    (dz3, d_mg1, d_mb1), gi_f1, go_f1 = _ffn_bwd(dz4, ffn1, w_ffn_in[1], w_ffn_out[1], (xh3, rs3, ln(mix_ln_g, 1)), 1)

    dgated = _mm_nt([(dz3, 0, 0, d)], [w_odd_out], "odd_dgated")
    go_odd = _mm_tn(gated, dz3, "odd_dw_out", tn=d).reshape(N_CHIPS, 1, d // N_CHIPS, d)
    da_odd, dws, dbs_col, d_vg, d_vb = _gmlp_bwd(dgated, sv_odd, rs_odd, v_ln_g, v_ln_b, w_spatial, bs_col)
    gi_odd = _mm_tn(x2_b, da_odd, "odd_dw_in", tn=d // 2, stack_cols=True)[:, None]
    dz2, d_fg0, d_fb0 = _mm_nt([(da_odd, 0, 0, 0)], [w_odd_in], "odd_dx", res=dz3,
                               ln=(ffn0[3], ffn0[4], ln(ffn_ln_g, 0)))
    (dz1, d_mg0, d_mb0), gi_f0, go_f0 = _ffn_bwd(dz2, ffn0, w_ffn_in[0], w_ffn_out[0], (xh1, rs1, ln(mix_ln_g, 0)), 0)

    sent_early = [gi_odd, go_odd, gi_f0[:, None], gi_f1[:, None], go_f0[:, None], go_f1[:, None]]
    early_h, early_tok = _split_start(sent_early, "scatter4", "scatter_early_start")
    dmix = _mm_nt([(dz1, 0, 0, d)], [w_even_out], "even_dmix", after=early_tok)
    go_even = jnp.concatenate([_mm_tn(attn, dz1, "even_dw_out_attn", tn=d), _mm_tn(conv, dz1, "even_dw_out_conv", tn=d)],
                              axis=0).reshape(N_CHIPS, 1, d // N_CHIPS, d)
    dbch, dconv_w8 = _conv_bwd(bch, dmix, conv_w)
    qb, dob = _attn_bwd_prep(attn, dmix, qp, head_lanes(lse.reshape(FOX_HEADS, t)))
    dqkv, dc_col = _attn_unpack(*_attn_bwd(qb, kp, vp, dob, kt))
    dc3 = dc_col.T.reshape(FOX_HEADS, t // LANES, LANES).transpose(1, 0, 2)
    dfl3, d_bf = _fgate_bwd(dc3, fl3, b_f_col)
    dfl = jnp.concatenate([dfl3.transpose(1, 0, 2).reshape(FOX_HEADS, t).T.astype(BF16),
                           jnp.zeros((t, LANES - FOX_HEADS), BF16)], axis=1)

    dws_masked = jnp.where(gmask[None], dws, 0.0)
    rep_names = ["odd_w_s", "odd_b_s", "mix_ln_g", "mix_ln_b", "ffn_ln_g", "ffn_ln_b", "even_b_f"]
    rep_grads = [dws_masked, dbs_col.T, jnp.concatenate([d_mg0, d_mg1]), jnp.concatenate([d_mb0, d_mb1]),
                 jnp.concatenate([d_fg0, d_fg1]), jnp.concatenate([d_fb0, d_fb1]), d_bf.reshape(1, FOX_HEADS)]
    rep_w = [(odd_w_s, m_odd_w_s, v_odd_w_s), (odd_b_s, m_odd_b_s, v_odd_b_s), (mix_ln_g, m_mix_ln_g, v_mix_ln_g),
             (mix_ln_b, m_mix_ln_b, v_mix_ln_b), (ffn_ln_g, m_ffn_ln_g, v_ffn_ln_g), (ffn_ln_b, m_ffn_ln_b, v_ffn_ln_b),
             (even_b_f, m_even_b_f, v_even_b_f)]
    rep_rows = [_to_rows(gr) for gr in rep_grads]
    n_rep = sum(r.shape[0] for r in rep_rows)
    pad_rep = (-n_rep) % SUBLANES
    dconv_w = dconv_w8[:CONV_K].reshape(CONV_K, N_CHIPS, LANES).transpose(1, 0, 2).reshape(N_CHIPS * CONV_K, LANES)
    packed = jnp.concatenate(rep_rows + [jnp.zeros((pad_rep, LANES), F32), d_vg.reshape(SUBLANES, LANES),
                                         d_vb.reshape(SUBLANES, LANES), dconv_w, jnp.zeros((4, LANES), F32)], axis=0)
    small_h, small_tok = _split_start([packed], "gather8", "gather_small_start")

    swap_h, swap_tok = _split_start(_scattered(early_h, "scatter_early_wait", small_tok), "swap2", "swap_early_start")
    dw_qkv = _mm_tn(dqkv, x2d, "even_dw_qkv", tn=d, tk=QKV // 2, after=swap_tok)
    dw_bch = _mm_tn(dbch, x2d, "even_dw_bch", tn=d, tk=BCH // 2)
    dw_f = _mm_tn(dfl, x2d, "even_dw_f", tn=d)
    gi_even = jnp.concatenate([dw_qkv, dw_f[:FOX_HEADS], dw_bch], axis=0).reshape(N_CHIPS, 1, -1, LANES)
    sent_late = [gi_even, go_even]
    late_h, late_tok = _split_start(sent_late, "scatter4", "scatter_late_start")
    grad_x = _mm_nt([(dqkv, 0, 0, QKV), (dbch, 0, QKV, QKV + BCH), (dfl, 0, QKV + BCH, EVEN_IN_PAD)], [w_even_in],
                    "even_dx", res=dz1, after=late_tok)
    mine, theirs = _split_wait(swap_h, "swap_early_wait", grad_x)
    res = {}
    res["odd_w_in"] = _adamw([mine[0], theirs[0]], odd_w_in, m_odd_w_in, v_odd_w_in, "adamw_odd_w_in")
    res["odd_w_out"] = _adamw([mine[1], theirs[1]], odd_w_out, m_odd_w_out, v_odd_w_out, "adamw_odd_w_out")
    for nm, at, (w, m, v) in (("ffn_w_in", 2, (ffn_w_in, m_ffn_w_in, v_ffn_w_in)),
                              ("ffn_w_out", 4, (ffn_w_out, m_ffn_w_out, v_ffn_w_out))):
        upper = _adamw([mine[at + 1], theirs[at + 1]], w, m, v, f"adamw_{nm}_1", layer=1)
        res[nm] = _adamw([mine[at], theirs[at]], w, m, v, f"adamw_{nm}_0", layer=0, into=upper)
    mine_late = _scattered(late_h, "scatter_late_wait", res["ffn_w_out"][0])
    theirs_late = _exchange(mine_late, "swap2", "swap_late")
    rows = lambda a: jnp.swapaxes(a, 1, 2).reshape(1, -1, LANES)
    back = lambda a: jnp.swapaxes(a.reshape(1, EVEN_IN // N_CHIPS, d), 1, 2)
    res["even_w_in"] = [back(o) for o in _adamw([mine_late[0], theirs_late[0]], rows(even_w_in), rows(m_even_w_in),
                                                rows(v_even_w_in), "adamw_even_w_in")]
    res["even_w_out"] = _adamw([mine_late[1], theirs_late[1]], even_w_out, m_even_w_out, v_even_w_out,
                               "adamw_even_w_out")
    (packed,), (gathered,) = _split_wait(small_h, "gather_small_wait", theirs_late[0])
    gathered = lax.dynamic_update_index_in_dim(gathered, packed, 4 * lax.axis_index("x") + 2 * lax.axis_index("y")
                                               + lax.axis_index("c"), 0)

    base = n_rep + pad_rep
    own_rows = jnp.concatenate([
        lax.dynamic_slice_in_dim(gathered, base + 2 * chip, 2, axis=1),
        lax.dynamic_slice_in_dim(gathered, base + SUBLANES + 2 * chip, 2, axis=1),
        lax.dynamic_slice_in_dim(gathered, base + 2 * SUBLANES + CONV_K * chip, CONV_K, axis=1),
        jnp.zeros((N_DEV, 1, LANES), F32)], axis=1)
    small_parts = jnp.concatenate([gathered[:, :base], own_rows], axis=1)[:, None]

    def pack_small(get):
        rows = [_to_rows(get(tw)) for tw in rep_w] + [jnp.zeros((pad_rep, LANES), F32)]
        rows += [get(sh).reshape(-1, LANES) for sh in ((odd_v_ln_g, m_odd_v_ln_g, v_odd_v_ln_g),
                                                       (odd_v_ln_b, m_odd_v_ln_b, v_odd_v_ln_b),
                                                       (even_conv_w, m_even_conv_w, v_even_conv_w))]
        return jnp.concatenate(rows + [jnp.zeros((1, LANES), F32)], axis=0)[None]

    small_out = _adamw([small_parts], pack_small(lambda tw: tw[0]), pack_small(lambda tw: tw[1]),
                       pack_small(lambda tw: tw[2]), "adamw_small")

    def unpack_small(rows3):
        rows = rows3[0]
        out, off = {}, 0
        for nm, (w, _, _), r in zip(rep_names, rep_w, rep_rows):
            out[nm] = rows[off:off + r.shape[0]].reshape(-1)[:w.size].reshape(w.shape)
            off += r.shape[0]
        off += pad_rep
        out["odd_v_ln_g"] = rows[off:off + 2].reshape(odd_v_ln_g.shape)
        out["odd_v_ln_b"] = rows[off + 2:off + 4].reshape(odd_v_ln_b.shape)
        out["even_conv_w"] = rows[off + 4:off + 4 + CONV_K].reshape(even_conv_w.shape)
        return out

    small = [unpack_small(o) for o in small_out]
    order = ["even_w_in", "even_b_f", "even_conv_w", "even_w_out", "odd_w_in", "odd_v_ln_g", "odd_v_ln_b", "odd_w_s",
             "odd_b_s", "odd_w_out", "mix_ln_g", "mix_ln_b", "ffn_w_in", "ffn_w_out", "ffn_ln_g", "ffn_ln_b"]
    outs = [loss, grad_x[None]]
    for kind in range(4):
        for nm in order:
            outs.append(res[nm][kind] if nm in res else small[kind][nm])
    return tuple(outs)
```

```python
import functools
import math

import jax
import jax.numpy as jnp
from jax import lax
from jax.experimental import pallas as pl
from jax.experimental.pallas import tpu as pltpu

F32 = jnp.float32
BF16 = jnp.bfloat16

D_MODEL = 1024
FOX_HEADS = 8
HEAD_DIM = 64
HEAD_PAIRS = FOX_HEADS // 2
FOX_WIDTH = FOX_HEADS * HEAD_DIM
CONV_WIDTH = 512
CONV_K = 3
QKV = 3 * FOX_WIDTH
BCH = 3 * CONV_WIDTH
EVEN_IN = QKV + FOX_HEADS + BCH
EVEN_IN_PAD = QKV + BCH + 128
GMLP_BLOCK = 128
GMLP_GROUPS = 8
CHUNK = 64
FFN_HIDDEN = 2816
HALF_HIDDEN = FFN_HIDDEN // 2
ALPHA = 4.0 ** 0.25
LN_EPS = 1e-5
ADAM_LR = 0.001
ADAM_B1 = 0.9
ADAM_B2 = 0.999
ADAM_EPS = 1e-08
ADAM_WD = 0.01
ADAM_STEP = 10
N_CHIPS = 4
N_DEV = 8
LANES = 128
SUBLANES = 8
ROW_TILE = 512
FFN_FUSED_ROW_TILE = 256
REDUCE_TILE = 2048
ATT_BLOCK = 512
ATT_FWD_HEADS = 8
ATT_BWD_HEADS = 4
ADAMW_BLOCK_BYTES = 2 ** 20
VMEM_LIMIT = 56 * 2 ** 20
NEG = -1e30
MESH = pl.DeviceIdType.MESH
HIGHEST = lax.Precision.HIGHEST
Q_C, Q_ONE, Q_LSE = 64, 67, 70
K_ONE, K_C, K_ONE2 = 64, 67, 70
V_ONE = 64
DO_DELTA = 65
NT = (((1,), (1,)), ((), ()))
TN = (((0,), (0,)), ((), ()))


def _cp():
    return pltpu.CompilerParams(vmem_limit_bytes=VMEM_LIMIT)


def _resident(shape):
    zeros = (0,) * len(shape)
    return pl.BlockSpec(shape, lambda *_: zeros, pipeline_mode=pl.Buffered(1))


def _sds(shape, dtype):
    return jax.ShapeDtypeStruct(tuple(shape), dtype)


_MASKS = {
    "gather4": [(1, 0, 0), (0, 1, 0), (1, 1, 0)],
    "scatter4": [(1, 0, 0), (0, 1, 0), (1, 1, 0)],
    "swap2": [(0, 0, 1)],
    "gather8": [(0, 0, 1), (0, 1, 0), (0, 1, 1), (1, 0, 0), (1, 0, 1), (1, 1, 0), (1, 1, 1)],
}


def _exchange(arrs, mode, name):
    n = len(arrs)
    masks = _MASKS[mode]
    npeer = len(masks)
    lead = {"gather4": N_CHIPS, "gather8": N_DEV}.get(mode)
    out_shapes = [_sds(((lead,) if lead else ()) + a.shape, a.dtype) for a in arrs]

    def body(*refs):
        ins, outs = refs[:n], refs[n:2 * n]
        send_sems, recv_sems, loc_sems = refs[2 * n:]
        x, y, c = lax.axis_index("x"), lax.axis_index("y"), lax.axis_index("c")
        chip, dev = 2 * x + y, 4 * x + 2 * y + c
        sends, recvs, locs = [], [], []
        for k in range(n):
            if mode == "gather4":
                locs.append(pltpu.make_async_copy(ins[k], outs[k].at[chip], loc_sems.at[k]))
            elif mode == "scatter4":
                locs.append(pltpu.make_async_copy(ins[k].at[chip], outs[k].at[chip], loc_sems.at[k]))
            elif mode == "gather8":
                locs.append(pltpu.make_async_copy(ins[k], outs[k].at[dev], loc_sems.at[k]))
        for cp in locs:
            cp.start()
        for k in range(n):
            for j, (dx, dy, dc) in enumerate(masks):
                px = 1 - x if dx else x
                py = 1 - y if dy else y
                pc = 1 - c if dc else c
                pchip, pdev = 2 * px + py, 4 * px + 2 * py + pc
                if mode == "gather4":
                    src, dst, land = ins[k], outs[k].at[chip], outs[k].at[pchip]
                elif mode == "scatter4":
                    src, dst, land = ins[k].at[pchip], outs[k].at[chip], outs[k].at[pchip]
                elif mode == "swap2":
                    src, dst, land = ins[k], outs[k], outs[k]
                else:
                    src, dst, land = ins[k], outs[k].at[dev], outs[k].at[pdev]
                s = k * npeer + j
                kw = dict(send_sem=send_sems.at[s], recv_sem=recv_sems.at[s], device_id=(px, py, pc),
                          device_id_type=MESH)
                cp = pltpu.make_async_remote_copy(src_ref=src, dst_ref=dst, **kw)
                cp.start()
                sends.append(cp)
                recvs.append(pltpu.make_async_remote_copy(src_ref=src, dst_ref=land, **kw))
        for cp in recvs:
            cp.wait_recv()
        for cp in sends:
            cp.wait_send()
        for cp in locs:
            cp.wait()

    any_spec = pl.BlockSpec(memory_space=pl.ANY)
    outs = pl.pallas_call(
        body,
        out_shape=out_shapes,
        in_specs=[any_spec] * n,
        out_specs=[any_spec] * n,
        scratch_shapes=[pltpu.SemaphoreType.DMA((n * npeer,)), pltpu.SemaphoreType.DMA((n * npeer,)),
                        pltpu.SemaphoreType.DMA((max(n, 1),))],
        name=name,
    )(*arrs)
    return list(outs)


_HBM_SPEC = pl.BlockSpec(memory_space=pltpu.HBM)
_SEM_SPEC = pl.BlockSpec(memory_space=pltpu.SEMAPHORE)
_ANY_SPEC = pl.BlockSpec(memory_space=pl.ANY)
_EFFECT = pltpu.SideEffectType.DATAFLOW_SIDE_EFFECTING


def _split_copies(mode, ins, lands, send_sems, recv_sems):
    x, y, c = lax.axis_index("x"), lax.axis_index("y"), lax.axis_index("c")
    chip, dev = 2 * x + y, 4 * x + 2 * y + c
    masks = _MASKS[mode]
    out = []
    for k in range(len(ins)):
        for j, (dx, dy, dc) in enumerate(masks):
            px = 1 - x if dx else x
            py = 1 - y if dy else y
            pc = 1 - c if dc else c
            pchip, pdev = 2 * px + py, 4 * px + 2 * py + pc
            if mode == "gather4":
                src, dst, land = ins[k], lands[k].at[chip], lands[k].at[pchip]
            elif mode == "scatter4":
                src, dst, land = ins[k].at[pchip], lands[k].at[chip], lands[k].at[pchip]
            elif mode == "swap2":
                src, dst, land = ins[k], lands[k], lands[k]
            else:
                src, dst, land = ins[k], lands[k].at[dev], lands[k].at[pdev]
            s = k * len(masks) + j
            kw = dict(send_sem=send_sems.at[s], recv_sem=recv_sems.at[s], device_id=(px, py, pc), device_id_type=MESH)
            out.append((pltpu.make_async_remote_copy(src_ref=src, dst_ref=dst, **kw),
                        pltpu.make_async_remote_copy(src_ref=src, dst_ref=land, **kw)))
    return out


def _split_start(arrs, mode, name, after=None):
    n = len(arrs)
    nsem = n * len(_MASKS[mode])
    lead = {"gather4": (N_CHIPS,), "gather8": (N_DEV,)}.get(mode, ())
    land_shapes = [lead + a.shape for a in arrs]

    def body(*refs):
        ins, lands = refs[:n], refs[n:2 * n]
        outs = refs[2 * n + (after is not None):]
        for start, _ in _split_copies(mode, ins, lands, outs[0], outs[1]):
            start.start()
        outs[-1][...] = jnp.zeros(outs[-1].shape, F32)

    srcs = [pltpu.with_memory_space_constraint(a, pltpu.HBM) for a in arrs]
    empties = [pltpu.with_memory_space_constraint(lax.empty(s, a.dtype), pltpu.HBM) for s, a in zip(land_shapes, arrs)]
    res = pl.pallas_call(
        body, name=name,
        out_shape=(pltpu.SemaphoreType.DMA((nsem,)), pltpu.SemaphoreType.DMA((nsem,)),
                   *[pltpu.HBM(a.shape, a.dtype) for a in arrs],
                   *[pltpu.HBM(s, a.dtype) for s, a in zip(land_shapes, arrs)],
                   _sds((SUBLANES, LANES), F32)),
        in_specs=[_HBM_SPEC] * (2 * n) + ([_ANY_SPEC] if after is not None else []),
        out_specs=(_SEM_SPEC, _SEM_SPEC, *[_HBM_SPEC] * (2 * n), pl.BlockSpec(memory_space=pltpu.VMEM)),
        input_output_aliases={k: 2 + k for k in range(2 * n)},
        compiler_params=pltpu.CompilerParams(has_side_effects=_EFFECT),
    )(*srcs, *empties, *([after] if after is not None else []))
    return dict(mode=mode, n=n, sems=res[:2], bufs=res[2:2 + 2 * n]), res[-1]


def _split_wait(handle, name, after):
    n, mode = handle["n"], handle["mode"]

    def body(*refs):
        ins, lands = refs[:n], refs[n:2 * n]
        send_sems, recv_sems = refs[2 * n], refs[2 * n + 1]
        for _, arrival in _split_copies(mode, ins, lands, send_sems, recv_sems):
            arrival.wait_send()
            arrival.wait_recv()

    bufs = handle["bufs"]
    res = pl.pallas_call(
        body, name=name,
        out_shape=tuple(pltpu.HBM(b.shape, b.dtype) for b in bufs),
        in_specs=[_HBM_SPEC] * (2 * n) + [_SEM_SPEC, _SEM_SPEC, _ANY_SPEC],
        out_specs=tuple([_HBM_SPEC] * (2 * n)),
        input_output_aliases={k: k for k in range(2 * n)},
        compiler_params=pltpu.CompilerParams(has_side_effects=_EFFECT),
    )(*bufs, *handle["sems"], after)
    return list(res[:n]), list(res[n:])


def _with_own(landed, own):
    chip = 2 * lax.axis_index("x") + lax.axis_index("y")
    return lax.dynamic_update_index_in_dim(landed, own, chip, 0)


def _gathered(handle, name, after):
    sent, landed = _split_wait(handle, name, after)
    return [_with_own(g, own) for g, own in zip(landed, sent)]


def _scattered(handle, name, after):
    chip = 2 * lax.axis_index("x") + lax.axis_index("y")
    sent, landed = _split_wait(handle, name, after)
    return [_with_own(r, lax.dynamic_index_in_dim(g, chip, 0, keepdims=False)) for r, g in zip(landed, sent)]


def _sigmoid(x):
    return 0.5 * jnp.tanh(0.5 * x) + 0.5


def _log_sigmoid(x):
    e = jnp.exp(-jnp.abs(x))
    log1p = jnp.where(e < 1e-2, e * (1.0 - e * (0.5 - e * (1.0 / 3.0))), jnp.log(1.0 + e))
    return jnp.minimum(x, 0.0) - log1p


def _ln_fwd(z):
    mu = jnp.mean(z, axis=-1, keepdims=True)
    zc = z - mu
    var = jnp.mean(zc * zc, axis=-1, keepdims=True)
    rstd = lax.rsqrt(var + LN_EPS)
    return zc * rstd, rstd


def _ln_bwd(dy, xhat, rstd, g):
    dxh = dy * g
    m1 = jnp.mean(dxh, axis=-1, keepdims=True)
    m2 = jnp.mean(dxh * xhat, axis=-1, keepdims=True)
    dz = rstd * (dxh - m1 - xhat * m2)
    return dz, jnp.sum(dy * xhat, axis=0, keepdims=True), jnp.sum(dy, axis=0, keepdims=True)


def _shift_down(z, halo):
    r = lax.broadcasted_iota(jnp.int32, z.shape, 0)
    z1 = jnp.where(r == 0, halo[7:8, :], pltpu.roll(z, 1, 0))
    z2 = jnp.where(r == 0, halo[6:7, :], jnp.where(r == 1, halo[7:8, :], pltpu.roll(z, 2, 0)))
    return z1, z2


def _shift_up(z, halo):
    n = z.shape[0]
    r = lax.broadcasted_iota(jnp.int32, z.shape, 0)
    z1 = jnp.where(r == n - 1, halo[0:1, :], pltpu.roll(z, n - 1, 0))
    z2 = jnp.where(r == n - 1, halo[1:2, :], jnp.where(r == n - 2, halo[0:1, :], pltpu.roll(z, n - 2, 0)))
    return z1, z2


def _accumulate(ref, first, value):
    @pl.when(first)
    def _():
        ref[...] = value

    @pl.when(jnp.logical_not(first))
    def _():
        ref[...] += value


def _proj(x, w, splits, name):
    t, k = x.shape
    tm = min(ROW_TILE, t)

    def body(x_ref, w_ref, *outs):
        a = x_ref[...].astype(BF16)
        for (lo, hi, dt), o in zip(splits, outs):
            o[...] = jnp.dot(a, w_ref[:, lo:hi], preferred_element_type=F32).astype(dt)

    return pl.pallas_call(
        body, grid=(t // tm,),
        in_specs=[pl.BlockSpec((tm, k), lambda i: (i, 0)), _resident(w.shape)],
        out_specs=[pl.BlockSpec((tm, hi - lo), lambda i: (i, 0)) for lo, hi, _ in splits],
        out_shape=[_sds((t, hi - lo), dt) for lo, hi, dt in splits],
        compiler_params=_cp(), name=name)(x, w)


def _fgate_fwd(fl3, b_f):
    nc = fl3.shape[0]

    def body(f_ref, b_ref, c_ref):
        r = lax.broadcasted_iota(jnp.int32, (LANES, LANES), 0)
        cidx = lax.broadcasted_iota(jnp.int32, (LANES, LANES), 1)
        upper = (r <= cidx).astype(F32)

        def step(i, carry):
            lf = _log_sigmoid(f_ref[i] + b_ref[...])
            cc = jnp.dot(lf, upper, precision=HIGHEST, preferred_element_type=F32) + carry
            c_ref[i] = cc
            return cc[:, LANES - 1:LANES]

        lax.fori_loop(0, nc, step, jnp.zeros((FOX_HEADS, 1), F32))

    return pl.pallas_call(body, out_shape=_sds(fl3.shape, F32), name="fgate_fwd")(fl3, b_f)


def _fgate_bwd(dc3, fl3, b_f):
    nc = fl3.shape[0]

    def body(dc_ref, f_ref, b_ref, df_ref, db_ref):
        r = lax.broadcasted_iota(jnp.int32, (LANES, LANES), 0)
        cidx = lax.broadcasted_iota(jnp.int32, (LANES, LANES), 1)
        lower = (r >= cidx).astype(F32)

        def step(n, carry):
            suffix, db = carry
            i = nc - 1 - n
            dlf = jnp.dot(dc_ref[i], lower, precision=HIGHEST, preferred_element_type=F32) + suffix
            df = dlf * (1.0 - _sigmoid(f_ref[i] + b_ref[...]))
            df_ref[i] = df
            return dlf[:, 0:1], db + jnp.sum(df, axis=1, keepdims=True)

        zero = jnp.zeros((FOX_HEADS, 1), F32)
        _, db = lax.fori_loop(0, nc, step, (zero, zero))
        db_ref[...] = db

    return pl.pallas_call(body, out_shape=[_sds(fl3.shape, F32), _sds((FOX_HEADS, 1), F32)],
                          name="fgate_bwd")(dc3, fl3, b_f)


def _split3(c):
    hi = c.astype(BF16).astype(F32)
    mid = (c - hi).astype(BF16).astype(F32)
    lo = (c - hi - mid).astype(BF16).astype(F32)
    return hi, mid, lo


PIECE_ONE = 3 * FOX_HEADS


def _piece_rows(values):
    hi, mid, lo = _split3(values)
    lane = lax.broadcasted_iota(jnp.int32, values.shape, 1)
    row = hi + pltpu.roll(mid, FOX_HEADS, 1) + pltpu.roll(lo, 2 * FOX_HEADS, 1) + jnp.where(lane == PIECE_ONE, 1.0, 0.0)
    return row.astype(BF16)


def _piece_selector(start, sign, ones=()):
    sel = [[0.0] * FOX_WIDTH for _ in range(LANES)]
    for h in range(FOX_HEADS):
        for n in range(3):
            sel[n * FOX_HEADS + h][h * HEAD_DIM + start - HEAD_DIM + n] = sign
        for lane in ones:
            sel[PIECE_ONE][h * HEAD_DIM + lane - HEAD_DIM] = 1.0
    return jnp.asarray(sel, BF16)


def _attn_pack(qkv, c_pad):
    t = qkv.shape[0]
    tm = min(ROW_TILE, t)
    hd = HEAD_DIM
    sel_q = _piece_selector(Q_C, 1.0, range(Q_ONE, Q_ONE + 3))
    sel_k = _piece_selector(K_C, -1.0, [*range(K_ONE, K_ONE + 3), *range(K_ONE2, K_ONE2 + 3)])
    sel_v = _piece_selector(HEAD_DIM, 0.0, range(V_ONE, V_ONE + 4))

    def body(x_ref, c_ref, sq_ref, sk_ref, sv_ref, qp_ref, kp_ref, vp_ref, kt_ref, vt_ref):
        pieces = _piece_rows(c_ref[...])
        q_extra = jnp.dot(pieces, sq_ref[...], preferred_element_type=F32).astype(BF16)
        k_extra = jnp.dot(pieces, sk_ref[...], preferred_element_type=F32).astype(BF16)
        v_extra = jnp.dot(pieces, sv_ref[...], preferred_element_type=F32).astype(BF16)
        for h in range(FOX_HEADS):
            hs = slice(h * hd, (h + 1) * hd)
            qp_ref[h, :, :hd] = (x_ref[:, hs].astype(F32) * (hd ** -0.5)).astype(BF16)
            qp_ref[h, :, hd:] = q_extra[:, hs]
            kp_ref[h, :, :hd] = x_ref[:, FOX_WIDTH + h * hd:FOX_WIDTH + (h + 1) * hd]
            kp_ref[h, :, hd:] = k_extra[:, hs]
            vp_ref[h, :, :hd] = x_ref[:, 2 * FOX_WIDTH + h * hd:2 * FOX_WIDTH + (h + 1) * hd]
            vp_ref[h, :, hd:] = v_extra[:, hs]
            kt_ref[h] = kp_ref[h].T
            vt_ref[h] = vp_ref[h].T

    row3 = pl.BlockSpec((FOX_HEADS, tm, LANES), lambda i: (0, i, 0))
    col3 = pl.BlockSpec((FOX_HEADS, LANES, tm), lambda i: (0, 0, i))
    sel = _resident(sel_q.shape)
    return pl.pallas_call(
        body, grid=(t // tm,),
        in_specs=[pl.BlockSpec((tm, QKV), lambda i: (i, 0)), pl.BlockSpec((tm, LANES), lambda i: (i, 0)), sel, sel, sel],
        out_specs=[row3, row3, row3, col3, col3],
        out_shape=[_sds((FOX_HEADS, t, LANES), BF16)] * 3 + [_sds((FOX_HEADS, LANES, t), BF16)] * 2,
        compiler_params=_cp(), name="attn_pack")(qkv, c_pad, sel_q, sel_k, sel_v)


def _triangle(nq, key_major):
    if key_major:
        pairs = [(i, j) for j in range(nq) for i in range(j, nq)]
    else:
        pairs = [(i, j) for i in range(nq) for j in range(i + 1)]
    return jnp.asarray([p[0] for p in pairs], jnp.int32), jnp.asarray([p[1] for p in pairs], jnp.int32)


def _attn_fwd(qp, kp, vt):
    t = qp.shape[1]
    bq = min(ATT_BLOCK, t)
    nq = t // bq
    nh = ATT_FWD_HEADS
    i_tab, j_tab = _triangle(nq, key_major=False)

    def body(it_ref, jt_ref, q_ref, k_ref, vt_ref, o_ref, lse_ref, m_sc, acc_sc):
        s = pl.program_id(1)
        i, j = it_ref[s], jt_ref[s]

        @pl.when(j == 0)
        def _():
            m_sc[...] = jnp.full(m_sc.shape, NEG, F32)
            acc_sc[...] = jnp.zeros(acc_sc.shape, F32)

        def sweep(masked):
            scores = lambda h: lax.dot_general(k_ref[h], q_ref[h], NT, preferred_element_type=F32)

            def accumulate(h, pt, rescale):
                acc_sc[h] = rescale * acc_sc[h] + jnp.dot(vt_ref[h], pt, preferred_element_type=F32)

            ahead, behind = scores(0), None
            for h in range(nh):
                st = ahead
                if h + 1 < nh:
                    ahead = scores(h + 1)
                if behind is not None:
                    accumulate(*behind)
                if masked:
                    key = lax.broadcasted_iota(jnp.int32, (bq, bq), 0)
                    qry = lax.broadcasted_iota(jnp.int32, (bq, bq), 1)
                    st = jnp.where(key <= qry, st, NEG)
                m_prev = m_sc[h]
                m_new = jnp.maximum(m_prev, jnp.max(st, axis=0, keepdims=True))
                behind = (h, jnp.exp(st - m_new).astype(BF16), jnp.exp(m_prev - m_new))
                m_sc[h] = m_new
            accumulate(*behind)

        @pl.when(j < i)
        def _():
            sweep(False)

        @pl.when(j == i)
        def _():
            sweep(True)
            for h in range(nh):
                acc = acc_sc[h]
                denom = acc[V_ONE:V_ONE + 1, :]
                o_ref[:, h * HEAD_DIM:(h + 1) * HEAD_DIM] = (acc[:HEAD_DIM, :] / denom).T.astype(BF16)
                lse_ref[h] = m_sc[h] + jnp.log(denom)

    grid_spec = pltpu.PrefetchScalarGridSpec(
        num_scalar_prefetch=2, grid=(FOX_HEADS // nh, i_tab.shape[0]),
        in_specs=[pl.BlockSpec((nh, bq, LANES), lambda hp, s, it, jt: (hp, it[s], 0)),
                  pl.BlockSpec((nh, bq, LANES), lambda hp, s, it, jt: (hp, jt[s], 0)),
                  pl.BlockSpec((nh, LANES, bq), lambda hp, s, it, jt: (hp, 0, jt[s]))],
        out_specs=[pl.BlockSpec((bq, nh * HEAD_DIM), lambda hp, s, it, jt: (it[s], hp)),
                   pl.BlockSpec((nh, 1, bq), lambda hp, s, it, jt: (hp, 0, it[s]))],
        scratch_shapes=[pltpu.VMEM((nh, 1, bq), F32), pltpu.VMEM((nh, LANES, bq), F32)])
    return pl.pallas_call(body, grid_spec=grid_spec,
                          out_shape=[_sds((t, FOX_WIDTH), BF16), _sds((FOX_HEADS, 1, t), F32)],
                          compiler_params=_cp(), name="attn_fwd")(i_tab, j_tab, qp, kp, vt)


def _conv_fwd(bch, conv_w):
    t = bch.shape[0]
    tm = min(ROW_TILE, t)
    halo_blocks = tm // SUBLANES
    cw = CONV_WIDTH

    def body(cur_ref, prev_ref, w_ref, o_ref):
        i = pl.program_id(0)
        z = cur_ref[:, cw:2 * cw] * cur_ref[:, 2 * cw:]
        zp = jnp.where(i == 0, 0.0, prev_ref[:, cw:2 * cw] * prev_ref[:, 2 * cw:])
        z1, z2 = _shift_down(z, zp)
        y = w_ref[0:1, :] * z2 + w_ref[1:2, :] * z1 + w_ref[2:3, :] * z
        o_ref[...] = (cur_ref[:, :cw] * y).astype(BF16)

    return pl.pallas_call(
        body, grid=(t // tm,),
        in_specs=[pl.BlockSpec((tm, BCH), lambda i: (i, 0)),
                  pl.BlockSpec((SUBLANES, BCH), lambda i: (jnp.maximum(i * halo_blocks - 1, 0), 0)),
                  _resident(conv_w.shape)],
        out_specs=pl.BlockSpec((tm, cw), lambda i: (i, 0)),
        out_shape=_sds((t, cw), BF16), compiler_params=_cp(), name="conv_fwd")(bch, bch, conv_w)


def _mm_res_ln(pairs, res, g, b, name):
    from_ln = isinstance(res, tuple)
    res_args = list(res) if from_ln else [res]
    t, d = res_args[0].shape
    tm = min(ROW_TILE, t)
    n = len(pairs)

    def body(*refs):
        a_refs, w_refs = refs[:n], refs[n:2 * n]
        res_refs = refs[2 * n:2 * n + len(res_args)]
        g_ref, b_ref, yb_ref, xh_ref, rs_ref = refs[2 * n + len(res_args):]
        r = res_refs[0][...]
        if from_ln:
            r = r * res_refs[1][...] + res_refs[2][...]
        z = ALPHA * r
        for a_ref, w_ref in zip(a_refs, w_refs):
            z = z + jnp.dot(a_ref[...].astype(BF16), w_ref[...], preferred_element_type=F32)
        xhat, rstd = _ln_fwd(z)
        yb_ref[...] = (xhat * g_ref[...] + b_ref[...]).astype(BF16)
        xh_ref[...] = xhat
        rs_ref[...] = rstd

    row = lambda i: (i, 0)
    full = pl.BlockSpec((tm, d), row)
    return pl.pallas_call(
        body, grid=(t // tm,),
        in_specs=[pl.BlockSpec((tm, a.shape[1]), row) for a, _ in pairs] + [_resident(w.shape) for _, w in pairs]
        + [full] + [_resident(a.shape) for a in res_args[1:]] + [_resident(g.shape), _resident(b.shape)],
        out_specs=[full, full, pl.BlockSpec((tm, 1), row)],
        out_shape=[_sds((t, d), BF16), _sds((t, d), F32), _sds((t, 1), F32)],
        compiler_params=_cp(), name=name)(*[a for a, _ in pairs], *[w for _, w in pairs], *res_args, g, b)


def _gmlp_fwd(x, w_in, vg, vb, wm, bs_col, w_out, res_ln, g, b):
    t, d = x.shape
    tm = min(ROW_TILE, t)
    gb = GMLP_BLOCK
    rxh, rg, rb = res_ln

    def body(x_ref, w_ref, vg_ref, vb_ref, wm_ref, bs_ref, wo_ref, rxh_ref, rg_ref, rb_ref, g_ref, b_ref,
             sv_ref, rs_ref, o_ref, yb_ref, xh_ref, rsy_ref, a_sc):
        xb = x_ref[...].astype(BF16)
        nc = w_ref.shape[2]
        for j in range(w_ref.shape[0]):
            a_sc[:, j * nc:(j + 1) * nc] = jnp.dot(xb, w_ref[j], preferred_element_type=F32)
        halves = []
        for half in range(2):
            a = a_sc[:, half * d:(half + 1) * d]
            cdf = 0.5 * (1.0 + lax.erf(a * (2.0 ** -0.5)))
            halves.append(a * cdf)
            slope = cdf + a * (jnp.exp(-0.5 * a * a) * (1.0 / math.sqrt(2.0 * math.pi)))
            sv_ref[:, (2 * half + 1) * d:(2 * half + 2) * d] = slope.astype(BF16)
        u = halves[0]
        vhat, rstd = _ln_fwd(halves[1])
        sv_ref[:, :d] = u.astype(BF16)
        sv_ref[:, 2 * d:3 * d] = vhat.astype(BF16)
        rs_ref[...] = rstd
        vln = (vhat * vg_ref[...] + vb_ref[...]).astype(BF16)
        for blk in range(tm // gb):
            rs = slice(blk * gb, (blk + 1) * gb)
            for gi in range(GMLP_GROUPS):
                cs = slice(gi * gb, (gi + 1) * gb)
                s = jnp.dot(wm_ref[gi], vln[rs, cs], preferred_element_type=F32) + bs_ref[:, gi:gi + 1]
                o_ref[rs, cs] = (u[rs, cs] * s).astype(BF16)
        z = ALPHA * (rxh_ref[...] * rg_ref[...] + rb_ref[...]) + jnp.dot(o_ref[...], wo_ref[...], preferred_element_type=F32)
        xhat, rstd_y = _ln_fwd(z)
        yb_ref[...] = (xhat * g_ref[...] + b_ref[...]).astype(BF16)
        xh_ref[...] = xhat
        rsy_ref[...] = rstd_y

    row = lambda i: (i, 0)
    full, col, vec = pl.BlockSpec((tm, d), row), pl.BlockSpec((tm, 1), row), _resident(g.shape)
    return pl.pallas_call(
        body, grid=(t // tm,),
        in_specs=[full, _resident(w_in.shape), _resident(vg.shape), _resident(vb.shape),
                  _resident(wm.shape), _resident(bs_col.shape), _resident(w_out.shape), full, vec, vec, vec, vec],
        out_specs=[pl.BlockSpec((tm, 4 * d), row), col, full, full, full, col],
        out_shape=[_sds((t, 4 * d), BF16), _sds((t, 1), F32), _sds((t, d), BF16), _sds((t, d), BF16), _sds((t, d), F32),
                   _sds((t, 1), F32)],
        scratch_shapes=[pltpu.VMEM((tm, 2 * d), F32)],
        compiler_params=_cp(), name="gmlp_fwd")(x, w_in, vg, vb, wm, bs_col, w_out, rxh, rg, rb, g, b)


def _loss_ln_bwd(xhat, rstd, g, b, target):
    t, d = xhat.shape
    tm = min(ROW_TILE, t)

    def body(xh_ref, rs_ref, g_ref, b_ref, t_ref, sq_ref, dz_ref, dg_ref, db_ref):
        first = pl.program_id(0) == 0
        xh = xh_ref[...]
        err = xh * g_ref[...] + b_ref[...] - t_ref[...]
        dz, dg, db = _ln_bwd(err * (1.0 / d), xh, rs_ref[...], g_ref[...])
        dz_ref[...] = dz
        _accumulate(sq_ref, first, jnp.sum(err * err, axis=0, keepdims=True))
        _accumulate(dg_ref, first, dg)
        _accumulate(db_ref, first, db)

    row = lambda i: (i, 0)
    vec = pl.BlockSpec((1, d), lambda i: (0, 0))
    return pl.pallas_call(
        body, grid=(t // tm,),
        in_specs=[pl.BlockSpec((tm, d), row), pl.BlockSpec((tm, 1), row), _resident(g.shape), _resident(b.shape),
                  pl.BlockSpec((tm, d), row)],
        out_specs=[vec, pl.BlockSpec((tm, d), row), vec, vec],
        out_shape=[_sds((1, d), F32), _sds((t, d), F32), _sds((1, d), F32), _sds((1, d), F32)],
        compiler_params=_cp(), name="loss_ln_bwd")(xhat, rstd, g, b, target)


def _mm_nt(pairs, ws, name, *, tm=ROW_TILE, res=None, ln=None, out_dtype=F32, after=None):
    t = pairs[0][0].shape[0]
    k = ws[0].shape[-2]
    tm = min(tm, t)
    n, nw = len(pairs), len(ws)

    def body(*refs):
        refs = refs[after is not None:]
        a_refs, w_refs = refs[:n], refs[n:n + nw]
        rest = list(refs[n + nw:])
        dx = None
        for a_ref, (_, wi, lo, hi) in zip(a_refs, pairs):
            w_ref = w_refs[wi]
            if len(w_ref.shape) == 3:
                nc = w_ref.shape[2]
                parts = [lax.dot_general(a_ref[:, j * nc:(j + 1) * nc].astype(BF16), w_ref[j], NT,
                                         preferred_element_type=F32) for j in range(w_ref.shape[0])]
            else:
                parts = [lax.dot_general(a_ref[...].astype(BF16), w_ref[:, lo:hi], NT, preferred_element_type=F32)]
            for part in parts:
                dx = part if dx is None else dx + part
        if res is not None:
            dx = dx + ALPHA * rest.pop(0)[...]
        if ln is None:
            rest[0][...] = dx.astype(out_dtype)
            return
        xh_ref, rs_ref, g_ref, dz_ref, dg_ref, db_ref = rest
        first = pl.program_id(0) == 0
        dz, dg, db = _ln_bwd(dx, xh_ref[...], rs_ref[...], g_ref[...])
        dz_ref[...] = dz
        _accumulate(dg_ref, first, dg)
        _accumulate(db_ref, first, db)

    row = lambda i: (i, 0)
    in_specs = [pl.BlockSpec((tm, a.shape[1]), row) for a, _, _, _ in pairs] + [_resident(w.shape) for w in ws]
    args = [a for a, _, _, _ in pairs] + list(ws)
    if res is not None:
        in_specs.append(pl.BlockSpec((tm, k), row))
        args.append(res)
    if ln is None:
        out_specs = pl.BlockSpec((tm, k), row)
        out_shape = _sds((t, k), out_dtype)
    else:
        xhat, rstd, g = ln
        in_specs += [pl.BlockSpec((tm, k), row), pl.BlockSpec((tm, 1), row), _resident(g.shape)]
        args += [xhat, rstd, g]
        vec = pl.BlockSpec((1, k), lambda i: (0, 0))
        out_specs = [pl.BlockSpec((tm, k), row), vec, vec]
        out_shape = [_sds((t, k), F32), _sds((1, k), F32), _sds((1, k), F32)]
    if after is not None:
        in_specs.insert(0, _ANY_SPEC)
        args.insert(0, after)
    return pl.pallas_call(body, grid=(t // tm,), in_specs=in_specs, out_specs=out_specs, out_shape=out_shape,
                          compiler_params=_cp(), name=name)(*args)


def _mm_tn(a, b, name, *, tn, tk=None, tt=None, stack_cols=False, out_dtype=BF16, after=None):
    t, k = a.shape
    n = b.shape[1]
    tk = k if tk is None else tk
    tt = min(REDUCE_TILE if tt is None else tt, t)
    nt = t // tt

    def body(a_ref, b_ref, *rest):
        o_ref, acc_ref = rest[after is not None:]
        s = pl.program_id(2)
        part = lax.dot_general(a_ref[...].astype(BF16), b_ref[...].astype(BF16), TN, preferred_element_type=F32)
        _accumulate(acc_ref, s == 0, part)

        @pl.when(s == nt - 1)
        def _():
            o_ref[...] = acc_ref[...].astype(out_dtype).reshape(o_ref.shape)

    if stack_cols:
        assert tk == k
        out_spec = pl.BlockSpec((1, k, tn), lambda kk, j, s: (j, 0, 0))
        out_shape = _sds((n // tn, k, tn), out_dtype)
    else:
        out_spec = pl.BlockSpec((tk, tn), lambda kk, j, s: (kk, j))
        out_shape = _sds((k, n), out_dtype)
    return pl.pallas_call(
        body, grid=(k // tk, n // tn, nt),
        in_specs=[pl.BlockSpec((tt, tk), lambda kk, j, s: (s, kk)), pl.BlockSpec((tt, tn), lambda kk, j, s: (s, j))]
        + ([_ANY_SPEC] if after is not None else []),
        out_specs=out_spec, out_shape=out_shape,
        scratch_shapes=[pltpu.VMEM((tk, tn), F32)],
        compiler_params=_cp(), name=name)(a, b, *([after] if after is not None else []))


def _ffn_bwd_rows(dz, wo, gu, wi, ln_below, name):
    t, d = dz.shape
    tm = min(FFN_FUSED_ROW_TILE, t)
    hh = HALF_HIDDEN
    xhat, rstd, g = ln_below

    def body(dz_ref, wo_ref, gu_ref, wi_ref, xh_ref, rs_ref, g_ref, dgu_ref, dzb_ref, dg_ref, db_ref):
        first = pl.program_id(0) == 0
        a = dz_ref[...].astype(BF16)
        for c in range(2):
            gs, us = slice(c * hh, (c + 1) * hh), slice(FFN_HIDDEN + c * hh, FFN_HIDDEN + (c + 1) * hh)
            dh = lax.dot_general(a, wo_ref[gs, :], NT, preferred_element_type=F32)
            dgu_ref[:, gs] = (dh * gu_ref[:, gs].astype(F32)).astype(BF16)
            dgu_ref[:, us] = (dh * gu_ref[:, us].astype(F32)).astype(BF16)
        dx = ALPHA * dz_ref[...]
        for j in range(wi_ref.shape[0]):
            dx = dx + lax.dot_general(dgu_ref[:, j * hh:(j + 1) * hh], wi_ref[j], NT, preferred_element_type=F32)
        dzb, dg, db = _ln_bwd(dx, xh_ref[...], rs_ref[...], g_ref[...])
        dzb_ref[...] = dzb
        _accumulate(dg_ref, first, dg)
        _accumulate(db_ref, first, db)

    row = lambda i: (i, 0)
    wide, full = pl.BlockSpec((tm, 2 * FFN_HIDDEN), row), pl.BlockSpec((tm, d), row)
    vec = pl.BlockSpec((1, d), lambda i: (0, 0))
    return pl.pallas_call(
        body, grid=(t // tm,),
        in_specs=[full, _resident(wo.shape), wide, _resident(wi.shape), full, pl.BlockSpec((tm, 1), row),
                  _resident(g.shape)],
        out_specs=[wide, full, vec, vec],
        out_shape=[_sds((t, 2 * FFN_HIDDEN), BF16), _sds((t, d), F32), _sds((1, d), F32), _sds((1, d), F32)],
        compiler_params=_cp(), name=name)(dz, wo, gu, wi, xhat, rstd, g)


def _gmlp_bwd(dz, w_out, saved, rstd_v, vg, vb, wm, bs_col, w_in, ln_below):
    t, d = dz.shape
    d2 = 2 * d
    tm = min(ROW_TILE, t)
    gb = GMLP_BLOCK
    xhat_below, rstd_below, g_below = ln_below

    def body(dz_ref, wo_ref, sv_ref, rs_ref, vg_ref, vb_ref, wm_ref, bs_ref, wi_ref, xh_ref, rsb_ref, gb_ref,
             da_ref, dws_ref, dbs_ref, dvg_ref, dvb_ref, dzb_ref, dg_ref, db_ref, dvln_sc):
        first = pl.program_id(0) == 0
        u = sv_ref[:, :d].astype(F32)
        vhat = sv_ref[:, 2 * d:3 * d].astype(F32)
        rstd = rs_ref[...]
        vln = (vhat * vg_ref[...] + vb_ref[...]).astype(BF16)
        dgate = lax.dot_general(dz_ref[...].astype(BF16), wo_ref[...], NT, preferred_element_type=F32)

        @pl.when(first)
        def _():
            dws_ref[...] = jnp.zeros(dws_ref.shape, F32)
            dbs_ref[...] = jnp.zeros(dbs_ref.shape, F32)

        for blk in range(tm // gb):
            rs = slice(blk * gb, (blk + 1) * gb)
            for gi in range(GMLP_GROUPS):
                cs = slice(gi * gb, (gi + 1) * gb)
                vblk = vln[rs, cs]
                s = jnp.dot(wm_ref[gi], vblk, preferred_element_type=F32) + bs_ref[:, gi:gi + 1]
                dgb = dgate[rs, cs]
                da_ref[rs, cs] = (dgb * s * sv_ref[rs, d + gi * gb:d + (gi + 1) * gb].astype(F32)).astype(BF16)
                ds = dgb * u[rs, cs]
                dsb = ds.astype(BF16)
                dws_ref[gi] += lax.dot_general(dsb, vblk, NT, preferred_element_type=F32)
                dbs_ref[:, gi:gi + 1] += jnp.sum(ds, axis=1, keepdims=True)
                dvln_sc[rs, cs] = lax.dot_general(wm_ref[gi], dsb, TN, preferred_element_type=F32)
        dv, dvg, dvb = _ln_bwd(dvln_sc[...], vhat, rstd, vg_ref[...])
        da_ref[:, d:] = (dv * sv_ref[:, 3 * d:].astype(F32)).astype(BF16)
        _accumulate(dvg_ref, first, dvg)
        _accumulate(dvb_ref, first, dvb)
        dx = ALPHA * dz_ref[...]
        nc = wi_ref.shape[2]
        for j in range(wi_ref.shape[0]):
            dx = dx + lax.dot_general(da_ref[:, j * nc:(j + 1) * nc], wi_ref[j], NT, preferred_element_type=F32)
        dzb, dg, db = _ln_bwd(dx, xh_ref[...], rsb_ref[...], gb_ref[...])
        dzb_ref[...] = dzb
        _accumulate(dg_ref, first, dg)
        _accumulate(db_ref, first, db)

    row = lambda i: (i, 0)
    full, col = pl.BlockSpec((tm, d), row), pl.BlockSpec((tm, 1), row)
    vec = pl.BlockSpec((1, d), lambda i: (0, 0))
    return pl.pallas_call(
        body, grid=(t // tm,),
        in_specs=[full, _resident(w_out.shape), pl.BlockSpec((tm, 4 * d), row), col,
                  _resident(vg.shape), _resident(vb.shape), _resident(wm.shape), _resident(bs_col.shape),
                  _resident(w_in.shape), full, col, _resident(g_below.shape)],
        out_specs=[pl.BlockSpec((tm, d2), row), pl.BlockSpec(wm.shape, lambda i: (0, 0, 0)),
                   pl.BlockSpec(bs_col.shape, lambda i: (0, 0)), vec, vec, full, vec, vec],
        out_shape=[_sds((t, d2), BF16), _sds(wm.shape, F32), _sds(bs_col.shape, F32), _sds((1, d), F32), _sds((1, d), F32),
                   _sds((t, d), F32), _sds((1, d), F32), _sds((1, d), F32)],
        scratch_shapes=[pltpu.VMEM((tm, d), F32)],
        compiler_params=_cp(), name="gmlp_bwd")(dz, w_out, saved, rstd_v, vg, vb, wm, bs_col, w_in, xhat_below,
                                                rstd_below, g_below)


def _conv_bwd(bch, dmix, conv_w):
    t = bch.shape[0]
    tm = min(ROW_TILE, t)
    nb = t // tm
    halo_blocks = tm // SUBLANES
    cw = CONV_WIDTH

    def body(cur_ref, prev_ref, next_ref, dc_ref, dn_ref, w_ref, o_ref, dw_ref):
        i = pl.program_id(0)
        bgate, cgate, hval = cur_ref[:, :cw], cur_ref[:, cw:2 * cw], cur_ref[:, 2 * cw:]
        z = cgate * hval
        zp = jnp.where(i == 0, 0.0, prev_ref[:, cw:2 * cw] * prev_ref[:, 2 * cw:])
        z1, z2 = _shift_down(z, zp)
        w0, w1, w2 = w_ref[0:1, :], w_ref[1:2, :], w_ref[2:3, :]
        dconv = dc_ref[...]
        o_ref[:, :cw] = (dconv * (w0 * z2 + w1 * z1 + w2 * z)).astype(BF16)
        dy = dconv * bgate
        dyn = jnp.where(i == nb - 1, 0.0, dn_ref[...] * next_ref[:, :cw])
        dy1, dy2 = _shift_up(dy, dyn)
        dz = w2 * dy + w1 * dy1 + w0 * dy2
        o_ref[:, cw:2 * cw] = (dz * hval).astype(BF16)
        o_ref[:, 2 * cw:] = (dz * cgate).astype(BF16)

        @pl.when(i == 0)
        def _():
            dw_ref[...] = jnp.zeros(dw_ref.shape, F32)

        for tap, zs in enumerate((z2, z1, z)):
            dw_ref[tap:tap + 1, :] += jnp.sum(dy * zs, axis=0, keepdims=True)

    last_halo = t // SUBLANES - 1
    return pl.pallas_call(
        body, grid=(nb,),
        in_specs=[pl.BlockSpec((tm, BCH), lambda i: (i, 0)),
                  pl.BlockSpec((SUBLANES, BCH), lambda i: (jnp.maximum(i * halo_blocks - 1, 0), 0)),
                  pl.BlockSpec((SUBLANES, BCH), lambda i: (jnp.minimum((i + 1) * halo_blocks, last_halo), 0)),
                  pl.BlockSpec((tm, cw), lambda i: (i, 1)),
                  pl.BlockSpec((SUBLANES, cw), lambda i: (jnp.minimum((i + 1) * halo_blocks, last_halo), 1)),
                  _resident(conv_w.shape)],
        out_specs=[pl.BlockSpec((tm, BCH), lambda i: (i, 0)), pl.BlockSpec((SUBLANES, cw), lambda i: (0, 0))],
        out_shape=[_sds((t, BCH), BF16), _sds((SUBLANES, cw), F32)],
        compiler_params=_cp(), name="conv_bwd")(bch, bch, bch, dmix, dmix, conv_w)


def _attn_bwd_prep(o, dmix, qp, lse_pad):
    t = o.shape[0]
    tm = min(ROW_TILE, t)
    hd = HEAD_DIM
    sel_lse = _piece_selector(Q_LSE, -1.0)
    sel_delta = _piece_selector(DO_DELTA, -1.0)
    head_of = jnp.asarray([[1.0 if col == row // hd else 0.0 for col in range(LANES)] for row in range(FOX_WIDTH)], F32)

    def body(o_ref, do_ref, qp_ref, lse_ref, sl_ref, sd_ref, seg_ref, qb_ref, dob_ref):
        do = do_ref[...]
        delta = jnp.dot(o_ref[...].astype(F32) * do, seg_ref[...], precision=HIGHEST, preferred_element_type=F32)
        lse_extra = jnp.dot(_piece_rows(lse_ref[...]), sl_ref[...], preferred_element_type=F32)
        do_extra = jnp.dot(_piece_rows(delta), sd_ref[...], preferred_element_type=F32).astype(BF16)
        for h in range(FOX_HEADS):
            hs = slice(h * hd, (h + 1) * hd)
            dob_ref[h, :, :hd] = do[:, hs].astype(BF16)
            dob_ref[h, :, hd:] = do_extra[:, hs]
            qb_ref[h, :, :hd] = qp_ref[h, :, :hd]
            qb_ref[h, :, hd:] = (qp_ref[h, :, hd:].astype(F32) + lse_extra[:, hs]).astype(BF16)

    row3 = pl.BlockSpec((FOX_HEADS, tm, LANES), lambda i: (0, i, 0))
    return pl.pallas_call(
        body, grid=(t // tm,),
        in_specs=[pl.BlockSpec((tm, FOX_WIDTH), lambda i: (i, 0)), pl.BlockSpec((tm, FOX_WIDTH), lambda i: (i, 0)), row3,
                  pl.BlockSpec((tm, LANES), lambda i: (i, 0)), _resident(sel_lse.shape), _resident(sel_delta.shape),
                  _resident(head_of.shape)],
        out_specs=[row3, row3], out_shape=[_sds((FOX_HEADS, t, LANES), BF16)] * 2,
        compiler_params=_cp(), name="attn_bwd_prep")(o, dmix, qp, lse_pad, sel_lse, sel_delta, head_of)


def _attn_bwd(qb, kp, vp, dob, kt):
    t = qb.shape[1]
    bq = min(ATT_BLOCK, t)
    nq = t // bq
    i_tab, j_tab = _triangle(nq, key_major=True)

    def body(it_ref, jt_ref, q_ref, k_ref, v_ref, do_ref, kt_ref, dqt_ref, dk_ref, dv_ref, dk_sc, dv_sc):
        s = pl.program_id(1)
        i, j = it_ref[s], jt_ref[s]

        @pl.when(s == 0)
        def _():
            dqt_ref[...] = jnp.zeros(dqt_ref.shape, F32)

        @pl.when(i == j)
        def _():
            dk_sc[...] = jnp.zeros(dk_sc.shape, F32)
            dv_sc[...] = jnp.zeros(dv_sc.shape, F32)

        cols = pl.ds(pl.multiple_of(i * bq, bq), bq)

        def sweep(masked):
            def scores(h):
                return (lax.dot_general(k_ref[h], q_ref[h], NT, preferred_element_type=F32),
                        lax.dot_general(v_ref[h], do_ref[h], NT, preferred_element_type=F32))

            def accumulate(h, ptb, dstb):
                dv_sc[h] += jnp.dot(ptb, do_ref[h], preferred_element_type=F32)
                dk_sc[h] += jnp.dot(dstb, q_ref[h], preferred_element_type=F32)
                dqt_ref[h, :, cols] += jnp.dot(kt_ref[h], dstb, preferred_element_type=F32)

            ahead, behind = scores(0), None
            for h in range(ATT_BWD_HEADS):
                st, dpt = ahead
                if h + 1 < ATT_BWD_HEADS:
                    ahead = scores(h + 1)
                if behind is not None:
                    accumulate(*behind)
                if masked:
                    key = lax.broadcasted_iota(jnp.int32, (bq, bq), 0)
                    qry = lax.broadcasted_iota(jnp.int32, (bq, bq), 1)
                    st = jnp.where(key <= qry, st, NEG)
                pt = jnp.exp(st)
                behind = (h, pt.astype(BF16), (pt * dpt).astype(BF16))
            accumulate(*behind)

        @pl.when(i == j)
        def _():
            sweep(True)

        @pl.when(i > j)
        def _():
            sweep(False)

        @pl.when(i == nq - 1)
        def _():
            dk_ref[...] = dk_sc[...]
            dv_ref[...] = dv_sc[...].astype(BF16)

    nh = ATT_BWD_HEADS
    qblk = pl.BlockSpec((nh, bq, LANES), lambda hp, s, it, jt: (hp, it[s], 0))
    kblk = pl.BlockSpec((nh, bq, LANES), lambda hp, s, it, jt: (hp, jt[s], 0))
    grid_spec = pltpu.PrefetchScalarGridSpec(
        num_scalar_prefetch=2, grid=(FOX_HEADS // nh, i_tab.shape[0]),
        in_specs=[qblk, kblk, kblk, qblk, pl.BlockSpec((nh, LANES, bq), lambda hp, s, it, jt: (hp, 0, jt[s]))],
        out_specs=[pl.BlockSpec((nh, LANES, t), lambda hp, s, it, jt: (hp, 0, 0), pipeline_mode=pl.Buffered(1)),
                   kblk, kblk],
        scratch_shapes=[pltpu.VMEM((nh, bq, LANES), F32), pltpu.VMEM((nh, bq, LANES), F32)])
    return pl.pallas_call(body, grid_spec=grid_spec,
                          out_shape=[_sds((FOX_HEADS, LANES, t), F32), _sds((FOX_HEADS, t, LANES), F32),
                                     _sds((FOX_HEADS, t, LANES), BF16)],
                          compiler_params=_cp(), name="attn_bwd")(i_tab, j_tab, qb, kp, vp, dob, kt)


def _attn_unpack(dqt, dkp, dvp):
    t = dkp.shape[1]
    tm = min(ROW_TILE, t)
    hd = HEAD_DIM

    def body(dqt_ref, dk_ref, dv_ref, o_ref, dc_ref):
        for h in range(FOX_HEADS):
            dq = dqt_ref[h].T
            o_ref[:, h * hd:(h + 1) * hd] = (dq[:, :hd] * (hd ** -0.5)).astype(BF16)
            o_ref[:, FOX_WIDTH + h * hd:FOX_WIDTH + (h + 1) * hd] = dk_ref[h, :, :hd].astype(BF16)
            o_ref[:, 2 * FOX_WIDTH + h * hd:2 * FOX_WIDTH + (h + 1) * hd] = dv_ref[h, :, :hd]
            dc_ref[:, h:h + 1] = dq[:, K_ONE:K_ONE + 1] - dk_ref[h, :, Q_ONE:Q_ONE + 1]

    row3 = pl.BlockSpec((FOX_HEADS, tm, LANES), lambda i: (0, i, 0))
    return pl.pallas_call(
        body, grid=(t // tm,),
        in_specs=[pl.BlockSpec((FOX_HEADS, LANES, tm), lambda i: (0, 0, i)), row3, row3],
        out_specs=[pl.BlockSpec((tm, QKV), lambda i: (i, 0)), pl.BlockSpec((tm, FOX_HEADS), lambda i: (i, 0))],
        out_shape=[_sds((t, QKV), BF16), _sds((t, FOX_HEADS), F32)],
        compiler_params=_cp(), name="attn_unpack")(dqt, dkp, dvp)


def _adamw(parts, w, m, v, name, layer=None, into=None):
    nl, r, c = w.shape
    fits = [cand for cand in [*range(SUBLANES, r, SUBLANES), r] if r % cand == 0 and cand * c * 4 <= ADAMW_BLOCK_BYTES]
    tr = max(fits) if fits else r
    npart = len(parts)
    bc1 = 1.0 - ADAM_B1 ** ADAM_STEP
    bc2 = 1.0 - ADAM_B2 ** ADAM_STEP

    def body(*refs):
        p_refs = refs[:npart]
        w_ref, m_ref, v_ref = refs[npart:npart + 3]
        g_ref, d_ref, nm_ref, nv_ref = refs[-4:]
        sums = []
        for p_ref in p_refs:
            acc = p_ref[0, 0].astype(F32)
            for s in range(1, p_ref.shape[0]):
                acc = acc + p_ref[s, 0].astype(F32)
            sums.append(acc)
        g = sums[0]
        for extra in sums[1:]:
            g = g + extra
        nm = ADAM_B1 * m_ref[0] + (1.0 - ADAM_B1) * g
        nv = ADAM_B2 * v_ref[0] + (1.0 - ADAM_B2) * (g * g)
        m_hat = nm / bc1
        v_hat = nv / bc2
        g_ref[0] = g
        d_ref[0] = -ADAM_LR * (m_hat / (jnp.sqrt(v_hat) + ADAM_EPS) + ADAM_WD * w_ref[0])
        nm_ref[0] = nm
        nv_ref[0] = nv

    first = 0 if layer is None else layer
    blk = pl.BlockSpec((1, tr, c), lambda l, i: (first + l, i, 0))
    extra = [] if into is None else list(into)
    return pl.pallas_call(
        body, grid=(nl if layer is None else 1, r // tr),
        in_specs=[pl.BlockSpec((p.shape[0], 1, tr, c), lambda l, i: (0, l, i, 0)) for p in parts] + [blk, blk, blk]
        + [_ANY_SPEC] * len(extra),
        out_specs=[blk] * 4, out_shape=[_sds(w.shape, F32)] * 4,
        input_output_aliases={npart + 3 + k: k for k in range(len(extra))},
        compiler_params=_cp(), name=name)(*parts, w, m, v, *extra)


def _to_rows(a):
    flat = a.reshape(-1)
    pad = (-flat.shape[0]) % LANES
    if pad:
        flat = jnp.concatenate([flat, jnp.zeros((pad,), flat.dtype)])
    return flat.reshape(-1, LANES)


def _by_owner_cols(dw):
    k, n = dw.shape
    return dw.reshape(k, N_CHIPS, n // N_CHIPS).transpose(1, 0, 2)[:, None]


def _ffn_fwd(xin_ln, xin_b, wi, wo, g, b, layer):
    t, d = xin_b.shape
    tm = min(FFN_FUSED_ROW_TILE, t)
    hh = HALF_HIDDEN
    rxh, rg, rb = xin_ln

    def body(x_ref, wi_ref, wo_ref, rxh_ref, rg_ref, rb_ref, g_ref, b_ref, gu_ref, h_ref, yb_ref, xh_ref, rs_ref):
        a = x_ref[...]
        for c in range(2):
            gs, us = slice(c * hh, (c + 1) * hh), slice(FFN_HIDDEN + c * hh, FFN_HIDDEN + (c + 1) * hh)
            gate = jnp.dot(a, wi_ref[c], preferred_element_type=F32)
            up = jnp.dot(a, wi_ref[2 + c], preferred_element_type=F32)
            sig = _sigmoid(gate)
            silu = gate * sig
            gu_ref[:, gs] = (up * sig * (1.0 + gate * (1.0 - sig))).astype(BF16)
            gu_ref[:, us] = silu.astype(BF16)
            h_ref[:, gs] = (silu * up).astype(BF16)
        z = ALPHA * (rxh_ref[...] * rg_ref[...] + rb_ref[...]) + jnp.dot(h_ref[...], wo_ref[...], preferred_element_type=F32)
        xhat, rstd = _ln_fwd(z)
        yb_ref[...] = (xhat * g_ref[...] + b_ref[...]).astype(BF16)
        xh_ref[...] = xhat
        rs_ref[...] = rstd

    row = lambda i: (i, 0)
    full = pl.BlockSpec((tm, d), row)
    vec = _resident(g.shape)
    gu, h, y_b, xhat, rstd = pl.pallas_call(
        body, grid=(t // tm,),
        in_specs=[full, _resident(wi.shape), _resident(wo.shape), full, vec, vec, vec, vec],
        out_specs=[pl.BlockSpec((tm, 2 * FFN_HIDDEN), row), pl.BlockSpec((tm, FFN_HIDDEN), row), full, full,
                   pl.BlockSpec((tm, 1), row)],
        out_shape=[_sds((t, 2 * FFN_HIDDEN), BF16), _sds((t, FFN_HIDDEN), BF16), _sds((t, d), BF16), _sds((t, d), F32),
                   _sds((t, 1), F32)],
        compiler_params=_cp(), name=f"ffn_fwd_rows_{layer}")(xin_b, wi, wo, rxh, rg, rb, g, b)
    return y_b, (xin_b, gu, h, xhat, rstd)


def _ffn_bwd(dz, saved, wi, wo, ln_below, layer):
    xin_b, gu, h, _, _ = saved
    dgu, *below = _ffn_bwd_rows(dz, wo, gu, wi, ln_below, f"ffn_bwd_rows_{layer}")
    g_out = _mm_tn(h, dz, f"ffn_dw_out_{layer}", tn=D_MODEL, tk=HALF_HIDDEN, tt=REDUCE_TILE // 2)
    g_in = _mm_tn(xin_b, dgu, f"ffn_dw_in_{layer}", tn=HALF_HIDDEN, stack_cols=True)
    return below, g_in, g_out.reshape(N_CHIPS, FFN_HIDDEN // N_CHIPS, D_MODEL)


def kernel(x, even_w_in, even_b_f, even_conv_w, even_w_out, odd_w_in, odd_v_ln_g, odd_v_ln_b, odd_w_s, odd_b_s, odd_w_out, mix_ln_g, mix_ln_b, ffn_w_in, ffn_w_out, ffn_ln_g, ffn_ln_b, loss_target, m_even_w_in, m_even_b_f, m_even_conv_w, m_even_w_out, m_odd_w_in, m_odd_v_ln_g, m_odd_v_ln_b, m_odd_w_s, m_odd_b_s, m_odd_w_out, m_mix_ln_g, m_mix_ln_b, m_ffn_w_in, m_ffn_w_out, m_ffn_ln_g, m_ffn_ln_b, v_even_w_in, v_even_b_f, v_even_conv_w, v_even_w_out, v_odd_w_in, v_odd_v_ln_g, v_odd_v_ln_b, v_odd_w_s, v_odd_b_s, v_odd_w_out, v_mix_ln_g, v_mix_ln_b, v_ffn_w_in, v_ffn_w_out, v_ffn_ln_g, v_ffn_ln_b):
    t = x.shape[1]
    d = D_MODEL
    chip = 2 * lax.axis_index("x") + lax.axis_index("y")
    x2d = x[0]
    target = loss_target[0]

    small_shard = jnp.concatenate([odd_v_ln_g.reshape(2, LANES), odd_v_ln_b.reshape(2, LANES),
                                   even_conv_w.reshape(CONV_K, LANES), jnp.zeros((1, LANES), F32)], axis=0)
    first = [even_w_in[0].astype(BF16)]
    second = [even_w_out[0].astype(BF16), small_shard]
    later = [odd_w_in[0].astype(BF16), odd_w_out[0].astype(BF16), ffn_w_in[0].astype(BF16), ffn_w_in[1].astype(BF16),
             ffn_w_out[0].astype(BF16), ffn_w_out[1].astype(BF16)]
    first_h, first_tok = _split_start(first, "gather4", "gather_first_start")
    second_h, second_tok = _split_start(second, "gather4", "gather_second_start", after=first_tok)
    later_h, later_tok = _split_start(later, "gather4", "gather_later_start", after=second_tok)
    (g_ewi,) = _gathered(first_h, "gather_first_wait", later_tok)
    ewi = g_ewi.transpose(1, 0, 2).reshape(d, EVEN_IN)
    w_even_in = jnp.concatenate([ewi[:, :QKV], ewi[:, QKV + FOX_HEADS:], ewi[:, QKV:QKV + FOX_HEADS],
                                 jnp.zeros((d, LANES - FOX_HEADS), BF16)], axis=1)
    chunk_id = jnp.arange(GMLP_BLOCK) // CHUNK
    gmask = chunk_id[None, :] <= chunk_id[:, None]
    w_spatial = jnp.where(gmask[None], odd_w_s[0], 0.0).astype(BF16)
    bs_col = odd_b_s[0].T
    b_f_col = even_b_f.reshape(FOX_HEADS, 1)
    ln = lambda p, l: p[l:l + 1]

    qkv, bch, fl = _proj(x2d, w_even_in, [(0, QKV, BF16), (QKV, QKV + BCH, F32), (QKV + BCH, EVEN_IN_PAD, F32)], "even_proj")
    fl3 = fl[:, :FOX_HEADS].T.reshape(FOX_HEADS, t // LANES, LANES).transpose(1, 0, 2)
    c3 = _fgate_fwd(fl3, b_f_col)
    c_rows = c3.transpose(1, 0, 2).reshape(FOX_HEADS, t)
    head_lanes = lambda rows: jnp.pad(rows.T, ((0, 0), (0, LANES - FOX_HEADS)))
    qp, kp, vp, kt, vt = _attn_pack(qkv, head_lanes(c_rows))
    attn, lse = _attn_fwd(qp, kp, vt)
    g_ewo, g_small = _gathered(second_h, "gather_second_wait", attn)
    w_even_out = g_ewo.reshape(d, d)
    v_ln_g = g_small[:, 0:2].reshape(1, d)
    v_ln_b = g_small[:, 2:4].reshape(1, d)
    conv_w = g_small[:, 4:7].transpose(1, 0, 2).reshape(CONV_K, CONV_WIDTH)
    conv = _conv_fwd(bch, conv_w)
    x1_b, xh1, rs1 = _mm_res_ln([(attn, w_even_out[:FOX_WIDTH]), (conv, w_even_out[FOX_WIDTH:])], x2d,
                                ln(mix_ln_g, 0), ln(mix_ln_b, 0), "even_out_ln")
    w_odd_in, g_owo, w_fi0, w_fi1, g_fo0, g_fo1 = _gathered(later_h, "gather_later_wait", x1_b)
    w_odd_out = g_owo.reshape(d, d)
    w_ffn_in = [w_fi0, w_fi1]
    w_ffn_out = [g_fo0.reshape(FFN_HIDDEN, d), g_fo1.reshape(FFN_HIDDEN, d)]
    x2_b, ffn0 = _ffn_fwd((xh1, ln(mix_ln_g, 0), ln(mix_ln_b, 0)), x1_b, w_ffn_in[0], w_ffn_out[0],
                          ln(ffn_ln_g, 0), ln(ffn_ln_b, 0), 0)

    sv_odd, rs_odd, gated, x3_b, xh3, rs3 = _gmlp_fwd(
        x2_b, w_odd_in, v_ln_g, v_ln_b, w_spatial, bs_col, w_odd_out, (ffn0[3], ln(ffn_ln_g, 0), ln(ffn_ln_b, 0)),
        ln(mix_ln_g, 1), ln(mix_ln_b, 1))
    _, ffn1 = _ffn_fwd((xh3, ln(mix_ln_g, 1), ln(mix_ln_b, 1)), x3_b, w_ffn_in[1], w_ffn_out[1],
                       ln(ffn_ln_g, 1), ln(ffn_ln_b, 1), 1)

    sq, dz4, d_fg1, d_fb1 = _loss_ln_bwd(ffn1[3], ffn1[4], ln(ffn_ln_g, 1), ln(ffn_ln_b, 1), target)
    loss = lax.psum(0.5 / d * jnp.sum(sq), ("x", "y", "c"))
    (dz3, d_mg1, d_mb1), gi_f1, go_f1 = _ffn_bwd(dz4, ffn1, w_ffn_in[1], w_ffn_out[1], (xh3, rs3, ln(mix_ln_g, 1)), 1)

    go_odd = _mm_tn(gated, dz3, "odd_dw_out", tn=d).reshape(N_CHIPS, 1, d // N_CHIPS, d)
    da_odd, dws, dbs_col, d_vg, d_vb, dz2, d_fg0, d_fb0 = _gmlp_bwd(
        dz3, w_odd_out, sv_odd, rs_odd, v_ln_g, v_ln_b, w_spatial, bs_col, w_odd_in,
        (ffn0[3], ffn0[4], ln(ffn_ln_g, 0)))
    gi_odd = _mm_tn(x2_b, da_odd, "odd_dw_in", tn=d // 2, stack_cols=True)[:, None]
    (dz1, d_mg0, d_mb0), gi_f0, go_f0 = _ffn_bwd(dz2, ffn0, w_ffn_in[0], w_ffn_out[0], (xh1, rs1, ln(mix_ln_g, 0)), 0)

    sent_early = [gi_odd, go_odd, gi_f0[:, None], gi_f1[:, None], go_f0[:, None], go_f1[:, None]]
    early_h, early_tok = _split_start(sent_early, "scatter4", "scatter_early_start")
    dmix = _mm_nt([(dz1, 0, 0, d)], [w_even_out], "even_dmix", after=early_tok)
    go_even = jnp.concatenate([_mm_tn(attn, dz1, "even_dw_out_attn", tn=d), _mm_tn(conv, dz1, "even_dw_out_conv", tn=d)],
                              axis=0).reshape(N_CHIPS, 1, d // N_CHIPS, d)
    dbch, dconv_w8 = _conv_bwd(bch, dmix, conv_w)
    qb, dob = _attn_bwd_prep(attn, dmix, qp, head_lanes(lse.reshape(FOX_HEADS, t)))
    dqkv, dc_col = _attn_unpack(*_attn_bwd(qb, kp, vp, dob, kt))
    dc3 = dc_col.T.reshape(FOX_HEADS, t // LANES, LANES).transpose(1, 0, 2)
    dfl3, d_bf = _fgate_bwd(dc3, fl3, b_f_col)
    dfl = jnp.concatenate([dfl3.transpose(1, 0, 2).reshape(FOX_HEADS, t).T.astype(BF16),
                           jnp.zeros((t, LANES - FOX_HEADS), BF16)], axis=1)

    dws_masked = jnp.where(gmask[None], dws, 0.0)
    rep_names = ["odd_w_s", "odd_b_s", "mix_ln_g", "mix_ln_b", "ffn_ln_g", "ffn_ln_b", "even_b_f"]
    rep_grads = [dws_masked, dbs_col.T, jnp.concatenate([d_mg0, d_mg1]), jnp.concatenate([d_mb0, d_mb1]),
                 jnp.concatenate([d_fg0, d_fg1]), jnp.concatenate([d_fb0, d_fb1]), d_bf.reshape(1, FOX_HEADS)]
    rep_w = [(odd_w_s, m_odd_w_s, v_odd_w_s), (odd_b_s, m_odd_b_s, v_odd_b_s), (mix_ln_g, m_mix_ln_g, v_mix_ln_g),
             (mix_ln_b, m_mix_ln_b, v_mix_ln_b), (ffn_ln_g, m_ffn_ln_g, v_ffn_ln_g), (ffn_ln_b, m_ffn_ln_b, v_ffn_ln_b),
             (even_b_f, m_even_b_f, v_even_b_f)]
    rep_rows = [_to_rows(gr) for gr in rep_grads]
    n_rep = sum(r.shape[0] for r in rep_rows)
    pad_rep = (-n_rep) % SUBLANES
    dconv_w = dconv_w8[:CONV_K].reshape(CONV_K, N_CHIPS, LANES).transpose(1, 0, 2).reshape(N_CHIPS * CONV_K, LANES)
    packed = jnp.concatenate(rep_rows + [jnp.zeros((pad_rep, LANES), F32), d_vg.reshape(SUBLANES, LANES),
                                         d_vb.reshape(SUBLANES, LANES), dconv_w, jnp.zeros((4, LANES), F32)], axis=0)
    small_h, small_tok = _split_start([packed], "gather8", "gather_small_start")

    swap_h, swap_tok = _split_start(_scattered(early_h, "scatter_early_wait", small_tok), "swap2", "swap_early_start")
    dw_qkv = _mm_tn(dqkv, x2d, "even_dw_qkv", tn=d, tk=QKV // 2, after=swap_tok)
    dw_bch = _mm_tn(dbch, x2d, "even_dw_bch", tn=d, tk=BCH // 2)
    dw_f = _mm_tn(dfl, x2d, "even_dw_f", tn=d)
    gi_even = jnp.concatenate([dw_qkv, dw_f[:FOX_HEADS], dw_bch], axis=0).reshape(N_CHIPS, 1, -1, LANES)
    sent_late = [gi_even, go_even]
    late_h, late_tok = _split_start(sent_late, "scatter4", "scatter_late_start")
    grad_x = _mm_nt([(dqkv, 0, 0, QKV), (dbch, 0, QKV, QKV + BCH), (dfl, 0, QKV + BCH, EVEN_IN_PAD)], [w_even_in],
                    "even_dx", res=dz1, after=late_tok)
    mine, theirs = _split_wait(swap_h, "swap_early_wait", grad_x)
    res = {}
    res["odd_w_in"] = _adamw([mine[0], theirs[0]], odd_w_in, m_odd_w_in, v_odd_w_in, "adamw_odd_w_in")
    res["odd_w_out"] = _adamw([mine[1], theirs[1]], odd_w_out, m_odd_w_out, v_odd_w_out, "adamw_odd_w_out")
    for nm, at, (w, m, v) in (("ffn_w_in", 2, (ffn_w_in, m_ffn_w_in, v_ffn_w_in)),
                              ("ffn_w_out", 4, (ffn_w_out, m_ffn_w_out, v_ffn_w_out))):
        upper = _adamw([mine[at + 1], theirs[at + 1]], w, m, v, f"adamw_{nm}_1", layer=1)
        res[nm] = _adamw([mine[at], theirs[at]], w, m, v, f"adamw_{nm}_0", layer=0, into=upper)
    mine_late = _scattered(late_h, "scatter_late_wait", res["ffn_w_out"][0])
    theirs_late = _exchange(mine_late, "swap2", "swap_late")
    rows = lambda a: jnp.swapaxes(a, 1, 2).reshape(1, -1, LANES)
    back = lambda a: jnp.swapaxes(a.reshape(1, EVEN_IN // N_CHIPS, d), 1, 2)
    res["even_w_in"] = [back(o) for o in _adamw([mine_late[0], theirs_late[0]], rows(even_w_in), rows(m_even_w_in),
                                                rows(v_even_w_in), "adamw_even_w_in")]
    res["even_w_out"] = _adamw([mine_late[1], theirs_late[1]], even_w_out, m_even_w_out, v_even_w_out,
                               "adamw_even_w_out")
    (packed,), (gathered,) = _split_wait(small_h, "gather_small_wait", theirs_late[0])
    gathered = lax.dynamic_update_index_in_dim(gathered, packed, 4 * lax.axis_index("x") + 2 * lax.axis_index("y")
                                               + lax.axis_index("c"), 0)

    base = n_rep + pad_rep
    own_rows = jnp.concatenate([
        lax.dynamic_slice_in_dim(gathered, base + 2 * chip, 2, axis=1),
        lax.dynamic_slice_in_dim(gathered, base + SUBLANES + 2 * chip, 2, axis=1),
        lax.dynamic_slice_in_dim(gathered, base + 2 * SUBLANES + CONV_K * chip, CONV_K, axis=1),
        jnp.zeros((N_DEV, 1, LANES), F32)], axis=1)
    small_parts = jnp.concatenate([gathered[:, :base], own_rows], axis=1)[:, None]

    def pack_small(get):
        rows = [_to_rows(get(tw)) for tw in rep_w] + [jnp.zeros((pad_rep, LANES), F32)]
        rows += [get(sh).reshape(-1, LANES) for sh in ((odd_v_ln_g, m_odd_v_ln_g, v_odd_v_ln_g),
                                                       (odd_v_ln_b, m_odd_v_ln_b, v_odd_v_ln_b),
                                                       (even_conv_w, m_even_conv_w, v_even_conv_w))]
        return jnp.concatenate(rows + [jnp.zeros((1, LANES), F32)], axis=0)[None]

    small_out = _adamw([small_parts], pack_small(lambda tw: tw[0]), pack_small(lambda tw: tw[1]),
                       pack_small(lambda tw: tw[2]), "adamw_small")

    def unpack_small(rows3):
        rows = rows3[0]
        out, off = {}, 0
        for nm, (w, _, _), r in zip(rep_names, rep_w, rep_rows):
            out[nm] = rows[off:off + r.shape[0]].reshape(-1)[:w.size].reshape(w.shape)
            off += r.shape[0]
        off += pad_rep
        out["odd_v_ln_g"] = rows[off:off + 2].reshape(odd_v_ln_g.shape)
        out["odd_v_ln_b"] = rows[off + 2:off + 4].reshape(odd_v_ln_b.shape)
        out["even_conv_w"] = rows[off + 4:off + 4 + CONV_K].reshape(even_conv_w.shape)
        return out

    small = [unpack_small(o) for o in small_out]
    order = ["even_w_in", "even_b_f", "even_conv_w", "even_w_out", "odd_w_in", "odd_v_ln_g", "odd_v_ln_b", "odd_w_s",
             "odd_b_s", "odd_w_out", "mix_ln_g", "mix_ln_b", "ffn_w_in", "ffn_w_out", "ffn_ln_g", "ffn_ln_b"]
    outs = [loss, grad_x[None]]
    for kind in range(4):
        for nm in order:
            outs.append(res[nm][kind] if nm in res else small[kind][nm])
    return tuple(outs)
```

```python
import functools
import math

import jax
import jax.numpy as jnp
from jax import lax
from jax.experimental import pallas as pl
from jax.experimental.pallas import tpu as pltpu

F32 = jnp.float32
BF16 = jnp.bfloat16

D_MODEL = 1024
FOX_HEADS = 8
HEAD_DIM = 64
HEAD_PAIRS = FOX_HEADS // 2
FOX_WIDTH = FOX_HEADS * HEAD_DIM
CONV_WIDTH = 512
CONV_K = 3
QKV = 3 * FOX_WIDTH
BCH = 3 * CONV_WIDTH
EVEN_IN = QKV + FOX_HEADS + BCH
EVEN_IN_PAD = QKV + BCH + 128
GMLP_BLOCK = 128
GMLP_GROUPS = 8
CHUNK = 64
FFN_HIDDEN = 2816
HALF_HIDDEN = FFN_HIDDEN // 2
ALPHA = 4.0 ** 0.25
LN_EPS = 1e-5
ADAM_LR = 0.001
ADAM_B1 = 0.9
ADAM_B2 = 0.999
ADAM_EPS = 1e-08
ADAM_WD = 0.01
ADAM_STEP = 10
N_CHIPS = 4
N_DEV = 8
LANES = 128
SUBLANES = 8
ROW_TILE = 512
FFN_FUSED_ROW_TILE = 256
REDUCE_TILE = 2048
ATT_BLOCK = 512
ATT_FWD_HEADS = 8
ATT_BWD_HEADS = 4
ADAMW_BLOCK_BYTES = 2 ** 20
VMEM_LIMIT = 56 * 2 ** 20
NEG = -1e30
MESH = pl.DeviceIdType.MESH
HIGHEST = lax.Precision.HIGHEST
Q_C, Q_ONE, Q_LSE = 64, 67, 70
K_ONE, K_C, K_ONE2 = 64, 67, 70
V_ONE = 64
DO_DELTA = 65
NT = (((1,), (1,)), ((), ()))
TN = (((0,), (0,)), ((), ()))


def _cp():
    return pltpu.CompilerParams(vmem_limit_bytes=VMEM_LIMIT)


def _resident(shape):
    zeros = (0,) * len(shape)
    return pl.BlockSpec(shape, lambda *_: zeros, pipeline_mode=pl.Buffered(1))


def _sds(shape, dtype):
    return jax.ShapeDtypeStruct(tuple(shape), dtype)


_MASKS = {
    "gather4": [(1, 0, 0), (0, 1, 0), (1, 1, 0)],
    "scatter4": [(1, 0, 0), (0, 1, 0), (1, 1, 0)],
    "swap2": [(0, 0, 1)],
    "gather8": [(0, 0, 1), (0, 1, 0), (0, 1, 1), (1, 0, 0), (1, 0, 1), (1, 1, 0), (1, 1, 1)],
}


def _exchange(arrs, mode, name):
    n = len(arrs)
    masks = _MASKS[mode]
    npeer = len(masks)
    lead = {"gather4": N_CHIPS, "gather8": N_DEV}.get(mode)
    out_shapes = [_sds(((lead,) if lead else ()) + a.shape, a.dtype) for a in arrs]

    def body(*refs):
        ins, outs = refs[:n], refs[n:2 * n]
        send_sems, recv_sems, loc_sems = refs[2 * n:]
        x, y, c = lax.axis_index("x"), lax.axis_index("y"), lax.axis_index("c")
        chip, dev = 2 * x + y, 4 * x + 2 * y + c
        sends, recvs, locs = [], [], []
        for k in range(n):
            if mode == "gather4":
                locs.append(pltpu.make_async_copy(ins[k], outs[k].at[chip], loc_sems.at[k]))
            elif mode == "scatter4":
                locs.append(pltpu.make_async_copy(ins[k].at[chip], outs[k].at[chip], loc_sems.at[k]))
            elif mode == "gather8":
                locs.append(pltpu.make_async_copy(ins[k], outs[k].at[dev], loc_sems.at[k]))
        for cp in locs:
            cp.start()
        for k in range(n):
            for j, (dx, dy, dc) in enumerate(masks):
                px = 1 - x if dx else x
                py = 1 - y if dy else y
                pc = 1 - c if dc else c
                pchip, pdev = 2 * px + py, 4 * px + 2 * py + pc
                if mode == "gather4":
                    src, dst, land = ins[k], outs[k].at[chip], outs[k].at[pchip]
                elif mode == "scatter4":
                    src, dst, land = ins[k].at[pchip], outs[k].at[chip], outs[k].at[pchip]
                elif mode == "swap2":
                    src, dst, land = ins[k], outs[k], outs[k]
                else:
                    src, dst, land = ins[k], outs[k].at[dev], outs[k].at[pdev]
                s = k * npeer + j
                kw = dict(send_sem=send_sems.at[s], recv_sem=recv_sems.at[s], device_id=(px, py, pc),
                          device_id_type=MESH)
                cp = pltpu.make_async_remote_copy(src_ref=src, dst_ref=dst, **kw)
                cp.start()
                sends.append(cp)
                recvs.append(pltpu.make_async_remote_copy(src_ref=src, dst_ref=land, **kw))
        for cp in recvs:
            cp.wait_recv()
        for cp in sends:
            cp.wait_send()
        for cp in locs:
            cp.wait()

    any_spec = pl.BlockSpec(memory_space=pl.ANY)
    outs = pl.pallas_call(
        body,
        out_shape=out_shapes,
        in_specs=[any_spec] * n,
        out_specs=[any_spec] * n,
        scratch_shapes=[pltpu.SemaphoreType.DMA((n * npeer,)), pltpu.SemaphoreType.DMA((n * npeer,)),
                        pltpu.SemaphoreType.DMA((max(n, 1),))],
        name=name,
    )(*arrs)
    return list(outs)


_HBM_SPEC = pl.BlockSpec(memory_space=pltpu.HBM)
_SEM_SPEC = pl.BlockSpec(memory_space=pltpu.SEMAPHORE)
_ANY_SPEC = pl.BlockSpec(memory_space=pl.ANY)
_EFFECT = pltpu.SideEffectType.DATAFLOW_SIDE_EFFECTING


def _split_copies(mode, ins, lands, send_sems, recv_sems):
    x, y, c = lax.axis_index("x"), lax.axis_index("y"), lax.axis_index("c")
    chip, dev = 2 * x + y, 4 * x + 2 * y + c
    masks = _MASKS[mode]
    out = []
    for k in range(len(ins)):
        for j, (dx, dy, dc) in enumerate(masks):
            px = 1 - x if dx else x
            py = 1 - y if dy else y
            pc = 1 - c if dc else c
            pchip, pdev = 2 * px + py, 4 * px + 2 * py + pc
            if mode == "gather4":
                src, dst, land = ins[k], lands[k].at[chip], lands[k].at[pchip]
            elif mode == "scatter4":
                src, dst, land = ins[k].at[pchip], lands[k].at[chip], lands[k].at[pchip]
            elif mode == "swap2":
                src, dst, land = ins[k], lands[k], lands[k]
            else:
                src, dst, land = ins[k], lands[k].at[dev], lands[k].at[pdev]
            s = k * len(masks) + j
            kw = dict(send_sem=send_sems.at[s], recv_sem=recv_sems.at[s], device_id=(px, py, pc), device_id_type=MESH)
            out.append((pltpu.make_async_remote_copy(src_ref=src, dst_ref=dst, **kw),
                        pltpu.make_async_remote_copy(src_ref=src, dst_ref=land, **kw)))
    return out


def _split_start(arrs, mode, name, after=None):
    n = len(arrs)
    nsem = n * len(_MASKS[mode])
    lead = {"gather4": (N_CHIPS,), "gather8": (N_DEV,)}.get(mode, ())
    land_shapes = [lead + a.shape for a in arrs]

    def body(*refs):
        ins, lands = refs[:n], refs[n:2 * n]
        outs = refs[2 * n + (after is not None):]
        for start, _ in _split_copies(mode, ins, lands, outs[0], outs[1]):
            start.start()
        outs[-1][...] = jnp.zeros(outs[-1].shape, F32)

    srcs = [pltpu.with_memory_space_constraint(a, pltpu.HBM) for a in arrs]
    empties = [pltpu.with_memory_space_constraint(lax.empty(s, a.dtype), pltpu.HBM) for s, a in zip(land_shapes, arrs)]
    res = pl.pallas_call(
        body, name=name,
        out_shape=(pltpu.SemaphoreType.DMA((nsem,)), pltpu.SemaphoreType.DMA((nsem,)),
                   *[pltpu.HBM(a.shape, a.dtype) for a in arrs],
                   *[pltpu.HBM(s, a.dtype) for s, a in zip(land_shapes, arrs)],
                   _sds((SUBLANES, LANES), F32)),
        in_specs=[_HBM_SPEC] * (2 * n) + ([_ANY_SPEC] if after is not None else []),
        out_specs=(_SEM_SPEC, _SEM_SPEC, *[_HBM_SPEC] * (2 * n), pl.BlockSpec(memory_space=pltpu.VMEM)),
        input_output_aliases={k: 2 + k for k in range(2 * n)},
        compiler_params=pltpu.CompilerParams(has_side_effects=_EFFECT),
    )(*srcs, *empties, *([after] if after is not None else []))
    return dict(mode=mode, n=n, sems=res[:2], bufs=res[2:2 + 2 * n]), res[-1]


def _split_wait(handle, name, after):
    n, mode = handle["n"], handle["mode"]

    def body(*refs):
        ins, lands = refs[:n], refs[n:2 * n]
        send_sems, recv_sems = refs[2 * n], refs[2 * n + 1]
        for _, arrival in _split_copies(mode, ins, lands, send_sems, recv_sems):
            arrival.wait_send()
            arrival.wait_recv()

    bufs = handle["bufs"]
    res = pl.pallas_call(
        body, name=name,
        out_shape=tuple(pltpu.HBM(b.shape, b.dtype) for b in bufs),
        in_specs=[_HBM_SPEC] * (2 * n) + [_SEM_SPEC, _SEM_SPEC, _ANY_SPEC],
        out_specs=tuple([_HBM_SPEC] * (2 * n)),
        input_output_aliases={k: k for k in range(2 * n)},
        compiler_params=pltpu.CompilerParams(has_side_effects=_EFFECT),
    )(*bufs, *handle["sems"], after)
    return list(res[:n]), list(res[n:])


def _with_own(landed, own):
    chip = 2 * lax.axis_index("x") + lax.axis_index("y")
    return lax.dynamic_update_index_in_dim(landed, own, chip, 0)


def _gathered(handle, name, after):
    sent, landed = _split_wait(handle, name, after)
    return [_with_own(g, own) for g, own in zip(landed, sent)]


def _scattered(handle, name, after):
    chip = 2 * lax.axis_index("x") + lax.axis_index("y")
    sent, landed = _split_wait(handle, name, after)
    return [_with_own(r, lax.dynamic_index_in_dim(g, chip, 0, keepdims=False)) for r, g in zip(landed, sent)]


def _sigmoid(x):
    return 0.5 * jnp.tanh(0.5 * x) + 0.5


def _log_sigmoid(x):
    e = jnp.exp(-jnp.abs(x))
    log1p = jnp.where(e < 1e-2, e * (1.0 - e * (0.5 - e * (1.0 / 3.0))), jnp.log(1.0 + e))
    return jnp.minimum(x, 0.0) - log1p


def _ln_fwd(z):
    mu = jnp.mean(z, axis=-1, keepdims=True)
    zc = z - mu
    var = jnp.mean(zc * zc, axis=-1, keepdims=True)
    rstd = lax.rsqrt(var + LN_EPS)
    return zc * rstd, rstd


def _ln_bwd(dy, xhat, rstd, g):
    dxh = dy * g
    m1 = jnp.mean(dxh, axis=-1, keepdims=True)
    m2 = jnp.mean(dxh * xhat, axis=-1, keepdims=True)
    dz = rstd * (dxh - m1 - xhat * m2)
    return dz, jnp.sum(dy * xhat, axis=0, keepdims=True), jnp.sum(dy, axis=0, keepdims=True)


def _shift_down(z, halo):
    r = lax.broadcasted_iota(jnp.int32, z.shape, 0)
    z1 = jnp.where(r == 0, halo[7:8, :], pltpu.roll(z, 1, 0))
    z2 = jnp.where(r == 0, halo[6:7, :], jnp.where(r == 1, halo[7:8, :], pltpu.roll(z, 2, 0)))
    return z1, z2


def _shift_up(z, halo):
    n = z.shape[0]
    r = lax.broadcasted_iota(jnp.int32, z.shape, 0)
    z1 = jnp.where(r == n - 1, halo[0:1, :], pltpu.roll(z, n - 1, 0))
    z2 = jnp.where(r == n - 1, halo[1:2, :], jnp.where(r == n - 2, halo[0:1, :], pltpu.roll(z, n - 2, 0)))
    return z1, z2


def _accumulate(ref, first, value):
    @pl.when(first)
    def _():
        ref[...] = value

    @pl.when(jnp.logical_not(first))
    def _():
        ref[...] += value


def _proj(x, wt, splits, name):
    t, k = x.shape
    tm = min(ROW_TILE, t)
    w = wt

    def body(x_ref, w_ref, *outs):
        a = x_ref[...].astype(BF16)
        for (lo, hi, dt), o in zip(splits, outs):
            o[...] = lax.dot_general(a, w_ref[lo:hi, :], NT, preferred_element_type=F32).astype(dt)

    return pl.pallas_call(
        body, grid=(t // tm,),
        in_specs=[pl.BlockSpec((tm, k), lambda i: (i, 0)), _resident(w.shape)],
        out_specs=[pl.BlockSpec((tm, hi - lo), lambda i: (i, 0)) for lo, hi, _ in splits],
        out_shape=[_sds((t, hi - lo), dt) for lo, hi, dt in splits],
        compiler_params=_cp(), name=name)(x, w)


def _fgate_fwd(fl3, b_f):
    nc = fl3.shape[0]

    def body(f_ref, b_ref, c_ref):
        r = lax.broadcasted_iota(jnp.int32, (LANES, LANES), 0)
        cidx = lax.broadcasted_iota(jnp.int32, (LANES, LANES), 1)
        upper = (r <= cidx).astype(F32)

        def step(i, carry):
            lf = _log_sigmoid(f_ref[i] + b_ref[...])
            cc = jnp.dot(lf, upper, precision=HIGHEST, preferred_element_type=F32) + carry
            c_ref[i] = cc
            return cc[:, LANES - 1:LANES]

        lax.fori_loop(0, nc, step, jnp.zeros((FOX_HEADS, 1), F32))

    return pl.pallas_call(body, out_shape=_sds(fl3.shape, F32), name="fgate_fwd")(fl3, b_f)


def _fgate_bwd(dc3, fl3, b_f):
    nc = fl3.shape[0]

    def body(dc_ref, f_ref, b_ref, df_ref, db_ref):
        r = lax.broadcasted_iota(jnp.int32, (LANES, LANES), 0)
        cidx = lax.broadcasted_iota(jnp.int32, (LANES, LANES), 1)
        lower = (r >= cidx).astype(F32)

        def step(n, carry):
            suffix, db = carry
            i = nc - 1 - n
            dlf = jnp.dot(dc_ref[i], lower, precision=HIGHEST, preferred_element_type=F32) + suffix
            df = dlf * (1.0 - _sigmoid(f_ref[i] + b_ref[...]))
            df_ref[i] = df
            return dlf[:, 0:1], db + jnp.sum(df, axis=1, keepdims=True)

        zero = jnp.zeros((FOX_HEADS, 1), F32)
        _, db = lax.fori_loop(0, nc, step, (zero, zero))
        db_ref[...] = db

    return pl.pallas_call(body, out_shape=[_sds(fl3.shape, F32), _sds((FOX_HEADS, 1), F32)],
                          name="fgate_bwd")(dc3, fl3, b_f)


def _split3(c):
    hi = c.astype(BF16).astype(F32)
    mid = (c - hi).astype(BF16).astype(F32)
    lo = (c - hi - mid).astype(BF16).astype(F32)
    return hi, mid, lo


PIECE_ONE = 3 * FOX_HEADS


def _piece_rows(values):
    hi, mid, lo = _split3(values)
    lane = lax.broadcasted_iota(jnp.int32, values.shape, 1)
    row = hi + pltpu.roll(mid, FOX_HEADS, 1) + pltpu.roll(lo, 2 * FOX_HEADS, 1) + jnp.where(lane == PIECE_ONE, 1.0, 0.0)
    return row.astype(BF16)


def _piece_selector(start, sign, ones=()):
    sel = [[0.0] * FOX_WIDTH for _ in range(LANES)]
    for h in range(FOX_HEADS):
        for n in range(3):
            sel[n * FOX_HEADS + h][h * HEAD_DIM + start - HEAD_DIM + n] = sign
        for lane in ones:
            sel[PIECE_ONE][h * HEAD_DIM + lane - HEAD_DIM] = 1.0
    return jnp.asarray(sel, BF16)


def _attn_pack(qkv, c_pad):
    t = qkv.shape[0]
    tm = min(ROW_TILE, t)
    hd = HEAD_DIM
    sel_q = _piece_selector(Q_C, 1.0, range(Q_ONE, Q_ONE + 3))
    sel_k = _piece_selector(K_C, -1.0, [*range(K_ONE, K_ONE + 3), *range(K_ONE2, K_ONE2 + 3)])
    sel_v = _piece_selector(HEAD_DIM, 0.0, range(V_ONE, V_ONE + 4))

    def body(x_ref, c_ref, sq_ref, sk_ref, sv_ref, qp_ref, kp_ref, vp_ref, kt_ref, vt_ref):
        pieces = _piece_rows(c_ref[...])
        q_extra = jnp.dot(pieces, sq_ref[...], preferred_element_type=F32).astype(BF16)
        k_extra = jnp.dot(pieces, sk_ref[...], preferred_element_type=F32).astype(BF16)
        v_extra = jnp.dot(pieces, sv_ref[...], preferred_element_type=F32).astype(BF16)
        for h in range(FOX_HEADS):
            hs = slice(h * hd, (h + 1) * hd)
            qp_ref[h, :, :hd] = (x_ref[:, hs].astype(F32) * (hd ** -0.5)).astype(BF16)
            qp_ref[h, :, hd:] = q_extra[:, hs]
            kp_ref[h, :, :hd] = x_ref[:, FOX_WIDTH + h * hd:FOX_WIDTH + (h + 1) * hd]
            kp_ref[h, :, hd:] = k_extra[:, hs]
            vp_ref[h, :, :hd] = x_ref[:, 2 * FOX_WIDTH + h * hd:2 * FOX_WIDTH + (h + 1) * hd]
            vp_ref[h, :, hd:] = v_extra[:, hs]
            kt_ref[h] = kp_ref[h].T
            vt_ref[h] = vp_ref[h].T

    row3 = pl.BlockSpec((FOX_HEADS, tm, LANES), lambda i: (0, i, 0))
    col3 = pl.BlockSpec((FOX_HEADS, LANES, tm), lambda i: (0, 0, i))
    sel = _resident(sel_q.shape)
    return pl.pallas_call(
        body, grid=(t // tm,),
        in_specs=[pl.BlockSpec((tm, QKV), lambda i: (i, 0)), pl.BlockSpec((tm, LANES), lambda i: (i, 0)), sel, sel, sel],
        out_specs=[row3, row3, row3, col3, col3],
        out_shape=[_sds((FOX_HEADS, t, LANES), BF16)] * 3 + [_sds((FOX_HEADS, LANES, t), BF16)] * 2,
        compiler_params=_cp(), name="attn_pack")(qkv, c_pad, sel_q, sel_k, sel_v)


def _triangle(nq, key_major):
    if key_major:
        pairs = [(i, j) for j in range(nq) for i in range(j, nq)]
    else:
        pairs = [(i, j) for i in range(nq) for j in range(i + 1)]
    return jnp.asarray([p[0] for p in pairs], jnp.int32), jnp.asarray([p[1] for p in pairs], jnp.int32)


def _attn_fwd(qp, kp, vt):
    t = qp.shape[1]
    bq = min(ATT_BLOCK, t)
    nq = t // bq
    nh = ATT_FWD_HEADS
    i_tab, j_tab = _triangle(nq, key_major=False)

    def body(it_ref, jt_ref, q_ref, k_ref, vt_ref, o_ref, lse_ref, m_sc, acc_sc):
        s = pl.program_id(1)
        i, j = it_ref[s], jt_ref[s]

        @pl.when(j == 0)
        def _():
            m_sc[...] = jnp.full(m_sc.shape, NEG, F32)
            acc_sc[...] = jnp.zeros(acc_sc.shape, F32)

        def sweep(masked):
            scores = lambda h: lax.dot_general(k_ref[h], q_ref[h], NT, preferred_element_type=F32)

            def accumulate(h, pt, rescale):
                acc_sc[h] = rescale * acc_sc[h] + jnp.dot(vt_ref[h], pt, preferred_element_type=F32)

            ahead, behind = scores(0), None
            for h in range(nh):
                st = ahead
                if h + 1 < nh:
                    ahead = scores(h + 1)
                if behind is not None:
                    accumulate(*behind)
                if masked:
                    key = lax.broadcasted_iota(jnp.int32, (bq, bq), 0)
                    qry = lax.broadcasted_iota(jnp.int32, (bq, bq), 1)
                    st = jnp.where(key <= qry, st, NEG)
                m_prev = m_sc[h]
                m_new = jnp.maximum(m_prev, jnp.max(st, axis=0, keepdims=True))
                behind = (h, jnp.exp(st - m_new).astype(BF16), jnp.exp(m_prev - m_new))
                m_sc[h] = m_new
            accumulate(*behind)

        @pl.when(j < i)
        def _():
            sweep(False)

        @pl.when(j == i)
        def _():
            sweep(True)
            for h in range(nh):
                acc = acc_sc[h]
                denom = acc[V_ONE:V_ONE + 1, :]
                o_ref[:, h * HEAD_DIM:(h + 1) * HEAD_DIM] = (acc[:HEAD_DIM, :] / denom).T.astype(BF16)
                lse_ref[h] = m_sc[h] + jnp.log(denom)

    grid_spec = pltpu.PrefetchScalarGridSpec(
        num_scalar_prefetch=2, grid=(FOX_HEADS // nh, i_tab.shape[0]),
        in_specs=[pl.BlockSpec((nh, bq, LANES), lambda hp, s, it, jt: (hp, it[s], 0)),
                  pl.BlockSpec((nh, bq, LANES), lambda hp, s, it, jt: (hp, jt[s], 0)),
                  pl.BlockSpec((nh, LANES, bq), lambda hp, s, it, jt: (hp, 0, jt[s]))],
        out_specs=[pl.BlockSpec((bq, nh * HEAD_DIM), lambda hp, s, it, jt: (it[s], hp)),
                   pl.BlockSpec((nh, 1, bq), lambda hp, s, it, jt: (hp, 0, it[s]))],
        scratch_shapes=[pltpu.VMEM((nh, 1, bq), F32), pltpu.VMEM((nh, LANES, bq), F32)])
    return pl.pallas_call(body, grid_spec=grid_spec,
                          out_shape=[_sds((t, FOX_WIDTH), BF16), _sds((FOX_HEADS, 1, t), F32)],
                          compiler_params=_cp(), name="attn_fwd")(i_tab, j_tab, qp, kp, vt)


def _conv_fwd(bch, conv_w):
    t = bch.shape[0]
    tm = min(ROW_TILE, t)
    halo_blocks = tm // SUBLANES
    cw = CONV_WIDTH

    def body(cur_ref, prev_ref, w_ref, o_ref):
        i = pl.program_id(0)
        z = cur_ref[:, cw:2 * cw] * cur_ref[:, 2 * cw:]
        zp = jnp.where(i == 0, 0.0, prev_ref[:, cw:2 * cw] * prev_ref[:, 2 * cw:])
        z1, z2 = _shift_down(z, zp)
        y = w_ref[0:1, :] * z2 + w_ref[1:2, :] * z1 + w_ref[2:3, :] * z
        o_ref[...] = (cur_ref[:, :cw] * y).astype(BF16)

    return pl.pallas_call(
        body, grid=(t // tm,),
        in_specs=[pl.BlockSpec((tm, BCH), lambda i: (i, 0)),
                  pl.BlockSpec((SUBLANES, BCH), lambda i: (jnp.maximum(i * halo_blocks - 1, 0), 0)),
                  _resident(conv_w.shape)],
        out_specs=pl.BlockSpec((tm, cw), lambda i: (i, 0)),
        out_shape=_sds((t, cw), BF16), compiler_params=_cp(), name="conv_fwd")(bch, bch, conv_w)


def _mm_res_ln(pairs, res, g, b, name):
    from_ln = isinstance(res, tuple)
    res_args = list(res) if from_ln else [res]
    t, d = res_args[0].shape
    tm = min(ROW_TILE, t)
    n = len(pairs)

    def body(*refs):
        a_refs, w_refs = refs[:n], refs[n:2 * n]
        res_refs = refs[2 * n:2 * n + len(res_args)]
        g_ref, b_ref, yb_ref, xh_ref, rs_ref = refs[2 * n + len(res_args):]
        r = res_refs[0][...]
        if from_ln:
            r = r * res_refs[1][...] + res_refs[2][...]
        z = ALPHA * r
        for a_ref, w_ref in zip(a_refs, w_refs):
            z = z + jnp.dot(a_ref[...].astype(BF16), w_ref[...], preferred_element_type=F32)
        xhat, rstd = _ln_fwd(z)
        yb_ref[...] = (xhat * g_ref[...] + b_ref[...]).astype(BF16)
        xh_ref[...] = xhat
        rs_ref[...] = rstd

    row = lambda i: (i, 0)
    full = pl.BlockSpec((tm, d), row)
    return pl.pallas_call(
        body, grid=(t // tm,),
        in_specs=[pl.BlockSpec((tm, a.shape[1]), row) for a, _ in pairs] + [_resident(w.shape) for _, w in pairs]
        + [full] + [_resident(a.shape) for a in res_args[1:]] + [_resident(g.shape), _resident(b.shape)],
        out_specs=[full, full, pl.BlockSpec((tm, 1), row)],
        out_shape=[_sds((t, d), BF16), _sds((t, d), F32), _sds((t, 1), F32)],
        compiler_params=_cp(), name=name)(*[a for a, _ in pairs], *[w for _, w in pairs], *res_args, g, b)


def _gmlp_fwd(x, w_in, vg, vb, wm, bs_col, w_out, res_ln, g, b):
    t, d = x.shape
    tm = min(ROW_TILE, t)
    gb = GMLP_BLOCK
    rxh, rg, rb = res_ln

    def body(x_ref, w_ref, vg_ref, vb_ref, wm_ref, bs_ref, wo_ref, rxh_ref, rg_ref, rb_ref, g_ref, b_ref,
             sv_ref, rs_ref, o_ref, yb_ref, xh_ref, rsy_ref, a_sc):
        xb = x_ref[...].astype(BF16)
        nc = w_ref.shape[2]
        for j in range(w_ref.shape[0]):
            a_sc[:, j * nc:(j + 1) * nc] = jnp.dot(xb, w_ref[j], preferred_element_type=F32)
        halves = []
        for half in range(2):
            a = a_sc[:, half * d:(half + 1) * d]
            cdf = 0.5 * (1.0 + lax.erf(a * (2.0 ** -0.5)))
            halves.append(a * cdf)
            slope = cdf + a * (jnp.exp(-0.5 * a * a) * (1.0 / math.sqrt(2.0 * math.pi)))
            sv_ref[:, (2 * half + 1) * d:(2 * half + 2) * d] = slope.astype(BF16)
        u = halves[0]
        vhat, rstd = _ln_fwd(halves[1])
        sv_ref[:, :d] = u.astype(BF16)
        sv_ref[:, 2 * d:3 * d] = vhat.astype(BF16)
        rs_ref[...] = rstd
        vln = (vhat * vg_ref[...] + vb_ref[...]).astype(BF16)
        for blk in range(tm // gb):
            rs = slice(blk * gb, (blk + 1) * gb)
            for gi in range(GMLP_GROUPS):
                cs = slice(gi * gb, (gi + 1) * gb)
                s = jnp.dot(wm_ref[gi], vln[rs, cs], preferred_element_type=F32) + bs_ref[:, gi:gi + 1]
                o_ref[rs, cs] = (u[rs, cs] * s).astype(BF16)
        z = ALPHA * (rxh_ref[...] * rg_ref[...] + rb_ref[...]) + jnp.dot(o_ref[...], wo_ref[...], preferred_element_type=F32)
        xhat, rstd_y = _ln_fwd(z)
        yb_ref[...] = (xhat * g_ref[...] + b_ref[...]).astype(BF16)
        xh_ref[...] = xhat
        rsy_ref[...] = rstd_y

    row = lambda i: (i, 0)
    full, col, vec = pl.BlockSpec((tm, d), row), pl.BlockSpec((tm, 1), row), _resident(g.shape)
    return pl.pallas_call(
        body, grid=(t // tm,),
        in_specs=[full, _resident(w_in.shape), _resident(vg.shape), _resident(vb.shape),
                  _resident(wm.shape), _resident(bs_col.shape), _resident(w_out.shape), full, vec, vec, vec, vec],
        out_specs=[pl.BlockSpec((tm, 4 * d), row), col, full, full, full, col],
        out_shape=[_sds((t, 4 * d), BF16), _sds((t, 1), F32), _sds((t, d), BF16), _sds((t, d), BF16), _sds((t, d), F32),
                   _sds((t, 1), F32)],
        scratch_shapes=[pltpu.VMEM((tm, 2 * d), F32)],
        compiler_params=_cp(), name="gmlp_fwd")(x, w_in, vg, vb, wm, bs_col, w_out, rxh, rg, rb, g, b)


def _loss_ln_bwd(xhat, rstd, g, b, target):
    t, d = xhat.shape
    tm = min(ROW_TILE, t)

    def body(xh_ref, rs_ref, g_ref, b_ref, t_ref, sq_ref, dz_ref, dg_ref, db_ref):
        first = pl.program_id(0) == 0
        xh = xh_ref[...]
        err = xh * g_ref[...] + b_ref[...] - t_ref[...]
        dz, dg, db = _ln_bwd(err * (1.0 / d), xh, rs_ref[...], g_ref[...])
        dz_ref[...] = dz
        _accumulate(sq_ref, first, jnp.sum(err * err, axis=0, keepdims=True))
        _accumulate(dg_ref, first, dg)
        _accumulate(db_ref, first, db)

    row = lambda i: (i, 0)
    vec = pl.BlockSpec((1, d), lambda i: (0, 0))
    return pl.pallas_call(
        body, grid=(t // tm,),
        in_specs=[pl.BlockSpec((tm, d), row), pl.BlockSpec((tm, 1), row), _resident(g.shape), _resident(b.shape),
                  pl.BlockSpec((tm, d), row)],
        out_specs=[vec, pl.BlockSpec((tm, d), row), vec, vec],
        out_shape=[_sds((1, d), F32), _sds((t, d), F32), _sds((1, d), F32), _sds((1, d), F32)],
        compiler_params=_cp(), name="loss_ln_bwd")(xhat, rstd, g, b, target)


def _mm_back(pairs, w, name, *, w_rows, res=None, after=None):
    t = pairs[0][0].shape[0]
    k = w.shape[1] if w_rows else w.shape[0]
    tm = min(ROW_TILE, t)
    n = len(pairs)

    def body(*refs):
        refs = refs[after is not None:]
        a_refs, w_ref = refs[:n], refs[n]
        dx = ALPHA * refs[n + 1][...] if res is not None else None
        for a_ref, (_, lo, hi) in zip(a_refs, pairs):
            a = a_ref[...].astype(BF16)
            if w_rows:
                part = jnp.dot(a, w_ref[lo:hi, :], preferred_element_type=F32)
            else:
                part = lax.dot_general(a, w_ref[:, lo:hi], NT, preferred_element_type=F32)
            dx = part if dx is None else dx + part
        refs[-1][...] = dx

    row = lambda i: (i, 0)
    in_specs = [pl.BlockSpec((tm, a.shape[1]), row) for a, _, _ in pairs] + [_resident(w.shape)]
    args = [a for a, _, _ in pairs] + [w]
    if res is not None:
        in_specs.append(pl.BlockSpec((tm, k), row))
        args.append(res)
    if after is not None:
        in_specs.insert(0, _ANY_SPEC)
        args.insert(0, after)
    return pl.pallas_call(body, grid=(t // tm,), in_specs=in_specs, out_specs=pl.BlockSpec((tm, k), row),
                          out_shape=_sds((t, k), F32), compiler_params=_cp(), name=name)(*args)


def _mm_tn(a, b, name, *, tn, tk=None, tt=None, stack_cols=False, out_dtype=BF16, after=None):
    t, k = a.shape
    n = b.shape[1]
    tk = k if tk is None else tk
    tt = min(REDUCE_TILE if tt is None else tt, t)
    nt = t // tt

    def body(a_ref, b_ref, *rest):
        o_ref, acc_ref = rest[after is not None:]
        s = pl.program_id(2)
        part = lax.dot_general(a_ref[...].astype(BF16), b_ref[...].astype(BF16), TN, preferred_element_type=F32)
        _accumulate(acc_ref, s == 0, part)

        @pl.when(s == nt - 1)
        def _():
            o_ref[...] = acc_ref[...].astype(out_dtype).reshape(o_ref.shape)

    if stack_cols:
        assert tk == k
        out_spec = pl.BlockSpec((1, k, tn), lambda kk, j, s: (j, 0, 0))
        out_shape = _sds((n // tn, k, tn), out_dtype)
    else:
        out_spec = pl.BlockSpec((tk, tn), lambda kk, j, s: (kk, j))
        out_shape = _sds((k, n), out_dtype)
    return pl.pallas_call(
        body, grid=(k // tk, n // tn, nt),
        in_specs=[pl.BlockSpec((tt, tk), lambda kk, j, s: (s, kk)), pl.BlockSpec((tt, tn), lambda kk, j, s: (s, j))]
        + ([_ANY_SPEC] if after is not None else []),
        out_specs=out_spec, out_shape=out_shape,
        scratch_shapes=[pltpu.VMEM((tk, tn), F32)],
        compiler_params=_cp(), name=name)(a, b, *([after] if after is not None else []))


def _ffn_bwd_rows(dz, wo, gu, wi, ln_below, name):
    t, d = dz.shape
    tm = min(FFN_FUSED_ROW_TILE, t)
    hh = HALF_HIDDEN
    xhat, rstd, g = ln_below

    def body(dz_ref, wo_ref, gu_ref, wi_ref, xh_ref, rs_ref, g_ref, dgu_ref, dzb_ref, dg_ref, db_ref):
        first = pl.program_id(0) == 0
        a = dz_ref[...].astype(BF16)
        for c in range(2):
            gs, us = slice(c * hh, (c + 1) * hh), slice(FFN_HIDDEN + c * hh, FFN_HIDDEN + (c + 1) * hh)
            dh = lax.dot_general(a, wo_ref[gs, :], NT, preferred_element_type=F32)
            dgu_ref[:, gs] = (dh * gu_ref[:, gs].astype(F32)).astype(BF16)
            dgu_ref[:, us] = (dh * gu_ref[:, us].astype(F32)).astype(BF16)
        dx = ALPHA * dz_ref[...]
        for j in range(wi_ref.shape[0]):
            dx = dx + lax.dot_general(dgu_ref[:, j * hh:(j + 1) * hh], wi_ref[j], NT, preferred_element_type=F32)
        dzb, dg, db = _ln_bwd(dx, xh_ref[...], rs_ref[...], g_ref[...])
        dzb_ref[...] = dzb
        _accumulate(dg_ref, first, dg)
        _accumulate(db_ref, first, db)

    row = lambda i: (i, 0)
    wide, full = pl.BlockSpec((tm, 2 * FFN_HIDDEN), row), pl.BlockSpec((tm, d), row)
    vec = pl.BlockSpec((1, d), lambda i: (0, 0))
    return pl.pallas_call(
        body, grid=(t // tm,),
        in_specs=[full, _resident(wo.shape), wide, _resident(wi.shape), full, pl.BlockSpec((tm, 1), row),
                  _resident(g.shape)],
        out_specs=[wide, full, vec, vec],
        out_shape=[_sds((t, 2 * FFN_HIDDEN), BF16), _sds((t, d), F32), _sds((1, d), F32), _sds((1, d), F32)],
        compiler_params=_cp(), name=name)(dz, wo, gu, wi, xhat, rstd, g)


def _gmlp_bwd(dz, w_out, saved, rstd_v, vg, vb, wm, bs_col, w_in, ln_below):
    t, d = dz.shape
    d2 = 2 * d
    tm = min(ROW_TILE, t)
    gb = GMLP_BLOCK
    xhat_below, rstd_below, g_below = ln_below

    def body(dz_ref, wo_ref, sv_ref, rs_ref, vg_ref, vb_ref, wm_ref, bs_ref, wi_ref, xh_ref, rsb_ref, gb_ref,
             da_ref, dws_ref, dbs_ref, dvg_ref, dvb_ref, dzb_ref, dg_ref, db_ref, dvln_sc):
        first = pl.program_id(0) == 0
        u = sv_ref[:, :d].astype(F32)
        vhat = sv_ref[:, 2 * d:3 * d].astype(F32)
        rstd = rs_ref[...]
        vln = (vhat * vg_ref[...] + vb_ref[...]).astype(BF16)
        dgate = lax.dot_general(dz_ref[...].astype(BF16), wo_ref[...], NT, preferred_element_type=F32)

        @pl.when(first)
        def _():
            dws_ref[...] = jnp.zeros(dws_ref.shape, F32)
            dbs_ref[...] = jnp.zeros(dbs_ref.shape, F32)

        for blk in range(tm // gb):
            rs = slice(blk * gb, (blk + 1) * gb)
            for gi in range(GMLP_GROUPS):
                cs = slice(gi * gb, (gi + 1) * gb)
                vblk = vln[rs, cs]
                s = jnp.dot(wm_ref[gi], vblk, preferred_element_type=F32) + bs_ref[:, gi:gi + 1]
                dgb = dgate[rs, cs]
                da_ref[rs, cs] = (dgb * s * sv_ref[rs, d + gi * gb:d + (gi + 1) * gb].astype(F32)).astype(BF16)
                ds = dgb * u[rs, cs]
                dsb = ds.astype(BF16)
                dws_ref[gi] += lax.dot_general(dsb, vblk, NT, preferred_element_type=F32)
                dbs_ref[:, gi:gi + 1] += jnp.sum(ds, axis=1, keepdims=True)
                dvln_sc[rs, cs] = lax.dot_general(wm_ref[gi], dsb, TN, preferred_element_type=F32)
        dv, dvg, dvb = _ln_bwd(dvln_sc[...], vhat, rstd, vg_ref[...])
        da_ref[:, d:] = (dv * sv_ref[:, 3 * d:].astype(F32)).astype(BF16)
        _accumulate(dvg_ref, first, dvg)
        _accumulate(dvb_ref, first, dvb)
        dx = ALPHA * dz_ref[...]
        nc = wi_ref.shape[2]
        for j in range(wi_ref.shape[0]):
            dx = dx + lax.dot_general(da_ref[:, j * nc:(j + 1) * nc], wi_ref[j], NT, preferred_element_type=F32)
        dzb, dg, db = _ln_bwd(dx, xh_ref[...], rsb_ref[...], gb_ref[...])
        dzb_ref[...] = dzb
        _accumulate(dg_ref, first, dg)
        _accumulate(db_ref, first, db)

    row = lambda i: (i, 0)
    full, col = pl.BlockSpec((tm, d), row), pl.BlockSpec((tm, 1), row)
    vec = pl.BlockSpec((1, d), lambda i: (0, 0))
    return pl.pallas_call(
        body, grid=(t // tm,),
        in_specs=[full, _resident(w_out.shape), pl.BlockSpec((tm, 4 * d), row), col,
                  _resident(vg.shape), _resident(vb.shape), _resident(wm.shape), _resident(bs_col.shape),
                  _resident(w_in.shape), full, col, _resident(g_below.shape)],
        out_specs=[pl.BlockSpec((tm, d2), row), pl.BlockSpec(wm.shape, lambda i: (0, 0, 0)),
                   pl.BlockSpec(bs_col.shape, lambda i: (0, 0)), vec, vec, full, vec, vec],
        out_shape=[_sds((t, d2), BF16), _sds(wm.shape, F32), _sds(bs_col.shape, F32), _sds((1, d), F32), _sds((1, d), F32),
                   _sds((t, d), F32), _sds((1, d), F32), _sds((1, d), F32)],
        scratch_shapes=[pltpu.VMEM((tm, d), F32)],
        compiler_params=_cp(), name="gmlp_bwd")(dz, w_out, saved, rstd_v, vg, vb, wm, bs_col, w_in, xhat_below,
                                                rstd_below, g_below)


def _conv_bwd(bch, dmix, conv_w):
    t = bch.shape[0]
    tm = min(ROW_TILE, t)
    nb = t // tm
    halo_blocks = tm // SUBLANES
    cw = CONV_WIDTH

    def body(cur_ref, prev_ref, next_ref, dc_ref, dn_ref, w_ref, o_ref, dw_ref):
        i = pl.program_id(0)
        bgate, cgate, hval = cur_ref[:, :cw], cur_ref[:, cw:2 * cw], cur_ref[:, 2 * cw:]
        z = cgate * hval
        zp = jnp.where(i == 0, 0.0, prev_ref[:, cw:2 * cw] * prev_ref[:, 2 * cw:])
        z1, z2 = _shift_down(z, zp)
        w0, w1, w2 = w_ref[0:1, :], w_ref[1:2, :], w_ref[2:3, :]
        dconv = dc_ref[...]
        o_ref[:, :cw] = (dconv * (w0 * z2 + w1 * z1 + w2 * z)).astype(BF16)
        dy = dconv * bgate
        dyn = jnp.where(i == nb - 1, 0.0, dn_ref[...] * next_ref[:, :cw])
        dy1, dy2 = _shift_up(dy, dyn)
        dz = w2 * dy + w1 * dy1 + w0 * dy2
        o_ref[:, cw:2 * cw] = (dz * hval).astype(BF16)
        o_ref[:, 2 * cw:] = (dz * cgate).astype(BF16)

        @pl.when(i == 0)
        def _():
            dw_ref[...] = jnp.zeros(dw_ref.shape, F32)

        for tap, zs in enumerate((z2, z1, z)):
            dw_ref[tap:tap + 1, :] += jnp.sum(dy * zs, axis=0, keepdims=True)

    last_halo = t // SUBLANES - 1
    return pl.pallas_call(
        body, grid=(nb,),
        in_specs=[pl.BlockSpec((tm, BCH), lambda i: (i, 0)),
                  pl.BlockSpec((SUBLANES, BCH), lambda i: (jnp.maximum(i * halo_blocks - 1, 0), 0)),
                  pl.BlockSpec((SUBLANES, BCH), lambda i: (jnp.minimum((i + 1) * halo_blocks, last_halo), 0)),
                  pl.BlockSpec((tm, cw), lambda i: (i, 1)),
                  pl.BlockSpec((SUBLANES, cw), lambda i: (jnp.minimum((i + 1) * halo_blocks, last_halo), 1)),
                  _resident(conv_w.shape)],
        out_specs=[pl.BlockSpec((tm, BCH), lambda i: (i, 0)), pl.BlockSpec((SUBLANES, cw), lambda i: (0, 0))],
        out_shape=[_sds((t, BCH), BF16), _sds((SUBLANES, cw), F32)],
        compiler_params=_cp(), name="conv_bwd")(bch, bch, bch, dmix, dmix, conv_w)


def _attn_bwd_prep(o, dmix, qp, lse_pad):
    t = o.shape[0]
    tm = min(ROW_TILE, t)
    hd = HEAD_DIM
    sel_lse = _piece_selector(Q_LSE, -1.0)
    sel_delta = _piece_selector(DO_DELTA, -1.0)
    head_of = jnp.asarray([[1.0 if col == row // hd else 0.0 for col in range(LANES)] for row in range(FOX_WIDTH)], F32)

    def body(o_ref, do_ref, qp_ref, lse_ref, sl_ref, sd_ref, seg_ref, qb_ref, dob_ref):
        do = do_ref[...]
        delta = jnp.dot(o_ref[...].astype(F32) * do, seg_ref[...], precision=HIGHEST, preferred_element_type=F32)
        lse_extra = jnp.dot(_piece_rows(lse_ref[...]), sl_ref[...], preferred_element_type=F32)
        do_extra = jnp.dot(_piece_rows(delta), sd_ref[...], preferred_element_type=F32).astype(BF16)
        for h in range(FOX_HEADS):
            hs = slice(h * hd, (h + 1) * hd)
            dob_ref[h, :, :hd] = do[:, hs].astype(BF16)
            dob_ref[h, :, hd:] = do_extra[:, hs]
            qb_ref[h, :, :hd] = qp_ref[h, :, :hd]
            qb_ref[h, :, hd:] = (qp_ref[h, :, hd:].astype(F32) + lse_extra[:, hs]).astype(BF16)

    row3 = pl.BlockSpec((FOX_HEADS, tm, LANES), lambda i: (0, i, 0))
    return pl.pallas_call(
        body, grid=(t // tm,),
        in_specs=[pl.BlockSpec((tm, FOX_WIDTH), lambda i: (i, 0)), pl.BlockSpec((tm, FOX_WIDTH), lambda i: (i, 0)), row3,
                  pl.BlockSpec((tm, LANES), lambda i: (i, 0)), _resident(sel_lse.shape), _resident(sel_delta.shape),
                  _resident(head_of.shape)],
        out_specs=[row3, row3], out_shape=[_sds((FOX_HEADS, t, LANES), BF16)] * 2,
        compiler_params=_cp(), name="attn_bwd_prep")(o, dmix, qp, lse_pad, sel_lse, sel_delta, head_of)


def _attn_bwd(qb, kp, vp, dob, kt):
    t = qb.shape[1]
    bq = min(ATT_BLOCK, t)
    nq = t // bq
    i_tab, j_tab = _triangle(nq, key_major=True)

    def body(it_ref, jt_ref, q_ref, k_ref, v_ref, do_ref, kt_ref, dqt_ref, dk_ref, dv_ref, dk_sc, dv_sc):
        s = pl.program_id(1)
        i, j = it_ref[s], jt_ref[s]

        @pl.when(s == 0)
        def _():
            dqt_ref[...] = jnp.zeros(dqt_ref.shape, F32)

        @pl.when(i == j)
        def _():
            dk_sc[...] = jnp.zeros(dk_sc.shape, F32)
            dv_sc[...] = jnp.zeros(dv_sc.shape, F32)

        cols = pl.ds(pl.multiple_of(i * bq, bq), bq)

        def sweep(masked):
            def scores(h):
                return (lax.dot_general(k_ref[h], q_ref[h], NT, preferred_element_type=F32),
                        lax.dot_general(v_ref[h], do_ref[h], NT, preferred_element_type=F32))

            def accumulate(h, ptb, dstb):
                dv_sc[h] += jnp.dot(ptb, do_ref[h], preferred_element_type=F32)
                dk_sc[h] += jnp.dot(dstb, q_ref[h], preferred_element_type=F32)
                dqt_ref[h, :, cols] += jnp.dot(kt_ref[h], dstb, preferred_element_type=F32)

            ahead, behind = scores(0), None
            for h in range(ATT_BWD_HEADS):
                st, dpt = ahead
                if h + 1 < ATT_BWD_HEADS:
                    ahead = scores(h + 1)
                if behind is not None:
                    accumulate(*behind)
                if masked:
                    key = lax.broadcasted_iota(jnp.int32, (bq, bq), 0)
                    qry = lax.broadcasted_iota(jnp.int32, (bq, bq), 1)
                    st = jnp.where(key <= qry, st, NEG)
                pt = jnp.exp(st)
                behind = (h, pt.astype(BF16), (pt * dpt).astype(BF16))
            accumulate(*behind)

        @pl.when(i == j)
        def _():
            sweep(True)

        @pl.when(i > j)
        def _():
            sweep(False)

        @pl.when(i == nq - 1)
        def _():
            dk_ref[...] = dk_sc[...]
            dv_ref[...] = dv_sc[...].astype(BF16)

    nh = ATT_BWD_HEADS
    qblk = pl.BlockSpec((nh, bq, LANES), lambda hp, s, it, jt: (hp, it[s], 0))
    kblk = pl.BlockSpec((nh, bq, LANES), lambda hp, s, it, jt: (hp, jt[s], 0))
    grid_spec = pltpu.PrefetchScalarGridSpec(
        num_scalar_prefetch=2, grid=(FOX_HEADS // nh, i_tab.shape[0]),
        in_specs=[qblk, kblk, kblk, qblk, pl.BlockSpec((nh, LANES, bq), lambda hp, s, it, jt: (hp, 0, jt[s]))],
        out_specs=[pl.BlockSpec((nh, LANES, t), lambda hp, s, it, jt: (hp, 0, 0), pipeline_mode=pl.Buffered(1)),
                   kblk, kblk],
        scratch_shapes=[pltpu.VMEM((nh, bq, LANES), F32), pltpu.VMEM((nh, bq, LANES), F32)])
    return pl.pallas_call(body, grid_spec=grid_spec,
                          out_shape=[_sds((FOX_HEADS, LANES, t), F32), _sds((FOX_HEADS, t, LANES), F32),
                                     _sds((FOX_HEADS, t, LANES), BF16)],
                          compiler_params=_cp(), name="attn_bwd")(i_tab, j_tab, qb, kp, vp, dob, kt)


def _attn_unpack(dqt, dkp, dvp):
    t = dkp.shape[1]
    tm = min(ROW_TILE, t)
    hd = HEAD_DIM

    def body(dqt_ref, dk_ref, dv_ref, o_ref, dc_ref):
        for h in range(FOX_HEADS):
            dq = dqt_ref[h].T
            o_ref[:, h * hd:(h + 1) * hd] = (dq[:, :hd] * (hd ** -0.5)).astype(BF16)
            o_ref[:, FOX_WIDTH + h * hd:FOX_WIDTH + (h + 1) * hd] = dk_ref[h, :, :hd].astype(BF16)
            o_ref[:, 2 * FOX_WIDTH + h * hd:2 * FOX_WIDTH + (h + 1) * hd] = dv_ref[h, :, :hd]
            dc_ref[:, h:h + 1] = dq[:, K_ONE:K_ONE + 1] - dk_ref[h, :, Q_ONE:Q_ONE + 1]

    row3 = pl.BlockSpec((FOX_HEADS, tm, LANES), lambda i: (0, i, 0))
    return pl.pallas_call(
        body, grid=(t // tm,),
        in_specs=[pl.BlockSpec((FOX_HEADS, LANES, tm), lambda i: (0, 0, i)), row3, row3],
        out_specs=[pl.BlockSpec((tm, QKV), lambda i: (i, 0)), pl.BlockSpec((tm, FOX_HEADS), lambda i: (i, 0))],
        out_shape=[_sds((t, QKV), BF16), _sds((t, FOX_HEADS), F32)],
        compiler_params=_cp(), name="attn_unpack")(dqt, dkp, dvp)


def _adamw(parts, w, m, v, name, layer=None, into=None):
    nl, r, c = w.shape
    fits = [cand for cand in [*range(SUBLANES, r, SUBLANES), r] if r % cand == 0 and cand * c * 4 <= ADAMW_BLOCK_BYTES]
    tr = max(fits) if fits else r
    npart = len(parts)
    bc1 = 1.0 - ADAM_B1 ** ADAM_STEP
    bc2 = 1.0 - ADAM_B2 ** ADAM_STEP

    def body(*refs):
        p_refs = refs[:npart]
        w_ref, m_ref, v_ref = refs[npart:npart + 3]
        g_ref, d_ref, nm_ref, nv_ref = refs[-4:]
        sums = []
        for p_ref in p_refs:
            acc = p_ref[0, 0].astype(F32)
            for s in range(1, p_ref.shape[0]):
                acc = acc + p_ref[s, 0].astype(F32)
            sums.append(acc)
        g = sums[0]
        for extra in sums[1:]:
            g = g + extra
        nm = ADAM_B1 * m_ref[0] + (1.0 - ADAM_B1) * g
        nv = ADAM_B2 * v_ref[0] + (1.0 - ADAM_B2) * (g * g)
        m_hat = nm / bc1
        v_hat = nv / bc2
        g_ref[0] = g
        d_ref[0] = -ADAM_LR * (m_hat / (jnp.sqrt(v_hat) + ADAM_EPS) + ADAM_WD * w_ref[0])
        nm_ref[0] = nm
        nv_ref[0] = nv

    first = 0 if layer is None else layer
    blk = pl.BlockSpec((1, tr, c), lambda l, i: (first + l, i, 0))
    extra = [] if into is None else list(into)
    return pl.pallas_call(
        body, grid=(nl if layer is None else 1, r // tr),
        in_specs=[pl.BlockSpec((p.shape[0], 1, tr, c), lambda l, i: (0, l, i, 0)) for p in parts] + [blk, blk, blk]
        + [_ANY_SPEC] * len(extra),
        out_specs=[blk] * 4, out_shape=[_sds(w.shape, F32)] * 4,
        input_output_aliases={npart + 3 + k: k for k in range(len(extra))},
        compiler_params=_cp(), name=name)(*parts, w, m, v, *extra)


def _to_rows(a):
    flat = a.reshape(-1)
    pad = (-flat.shape[0]) % LANES
    if pad:
        flat = jnp.concatenate([flat, jnp.zeros((pad,), flat.dtype)])
    return flat.reshape(-1, LANES)


def _by_owner_cols(dw):
    k, n = dw.shape
    return dw.reshape(k, N_CHIPS, n // N_CHIPS).transpose(1, 0, 2)[:, None]


def _ffn_fwd(xin_ln, xin_b, wi, wo, g, b, layer):
    t, d = xin_b.shape
    tm = min(FFN_FUSED_ROW_TILE, t)
    hh = HALF_HIDDEN
    rxh, rg, rb = xin_ln

    def body(x_ref, wi_ref, wo_ref, rxh_ref, rg_ref, rb_ref, g_ref, b_ref, gu_ref, h_ref, yb_ref, xh_ref, rs_ref):
        a = x_ref[...]
        for c in range(2):
            gs, us = slice(c * hh, (c + 1) * hh), slice(FFN_HIDDEN + c * hh, FFN_HIDDEN + (c + 1) * hh)
            gate = jnp.dot(a, wi_ref[c], preferred_element_type=F32)
            up = jnp.dot(a, wi_ref[2 + c], preferred_element_type=F32)
            sig = _sigmoid(gate)
            silu = gate * sig
            gu_ref[:, gs] = (up * sig * (1.0 + gate * (1.0 - sig))).astype(BF16)
            gu_ref[:, us] = silu.astype(BF16)
            h_ref[:, gs] = (silu * up).astype(BF16)
        z = ALPHA * (rxh_ref[...] * rg_ref[...] + rb_ref[...]) + jnp.dot(h_ref[...], wo_ref[...], preferred_element_type=F32)
        xhat, rstd = _ln_fwd(z)
        yb_ref[...] = (xhat * g_ref[...] + b_ref[...]).astype(BF16)
        xh_ref[...] = xhat
        rs_ref[...] = rstd

    row = lambda i: (i, 0)
    full = pl.BlockSpec((tm, d), row)
    vec = _resident(g.shape)
    gu, h, y_b, xhat, rstd = pl.pallas_call(
        body, grid=(t // tm,),
        in_specs=[full, _resident(wi.shape), _resident(wo.shape), full, vec, vec, vec, vec],
        out_specs=[pl.BlockSpec((tm, 2 * FFN_HIDDEN), row), pl.BlockSpec((tm, FFN_HIDDEN), row), full, full,
                   pl.BlockSpec((tm, 1), row)],
        out_shape=[_sds((t, 2 * FFN_HIDDEN), BF16), _sds((t, FFN_HIDDEN), BF16), _sds((t, d), BF16), _sds((t, d), F32),
                   _sds((t, 1), F32)],
        compiler_params=_cp(), name=f"ffn_fwd_rows_{layer}")(xin_b, wi, wo, rxh, rg, rb, g, b)
    return y_b, (xin_b, gu, h, xhat, rstd)


def _ffn_bwd(dz, saved, wi, wo, ln_below, layer):
    xin_b, gu, h, _, _ = saved
    dgu, *below = _ffn_bwd_rows(dz, wo, gu, wi, ln_below, f"ffn_bwd_rows_{layer}")
    g_out = _mm_tn(h, dz, f"ffn_dw_out_{layer}", tn=D_MODEL, tk=HALF_HIDDEN, tt=REDUCE_TILE // 2)
    g_in = _mm_tn(xin_b, dgu, f"ffn_dw_in_{layer}", tn=HALF_HIDDEN, stack_cols=True)
    return below, g_in, g_out.reshape(N_CHIPS, FFN_HIDDEN // N_CHIPS, D_MODEL)


def kernel(x, even_w_in, even_b_f, even_conv_w, even_w_out, odd_w_in, odd_v_ln_g, odd_v_ln_b, odd_w_s, odd_b_s, odd_w_out, mix_ln_g, mix_ln_b, ffn_w_in, ffn_w_out, ffn_ln_g, ffn_ln_b, loss_target, m_even_w_in, m_even_b_f, m_even_conv_w, m_even_w_out, m_odd_w_in, m_odd_v_ln_g, m_odd_v_ln_b, m_odd_w_s, m_odd_b_s, m_odd_w_out, m_mix_ln_g, m_mix_ln_b, m_ffn_w_in, m_ffn_w_out, m_ffn_ln_g, m_ffn_ln_b, v_even_w_in, v_even_b_f, v_even_conv_w, v_even_w_out, v_odd_w_in, v_odd_v_ln_g, v_odd_v_ln_b, v_odd_w_s, v_odd_b_s, v_odd_w_out, v_mix_ln_g, v_mix_ln_b, v_ffn_w_in, v_ffn_w_out, v_ffn_ln_g, v_ffn_ln_b):
    t = x.shape[1]
    d = D_MODEL
    chip = 2 * lax.axis_index("x") + lax.axis_index("y")
    x2d = x[0]
    target = loss_target[0]

    small_shard = jnp.concatenate([odd_v_ln_g.reshape(2, LANES), odd_v_ln_b.reshape(2, LANES),
                                   even_conv_w.reshape(CONV_K, LANES), jnp.zeros((1, LANES), F32)], axis=0)
    first = [jnp.swapaxes(even_w_in[0], 0, 1).astype(BF16)]
    second = [even_w_out[0].astype(BF16), small_shard]
    later = [odd_w_in[0].astype(BF16), odd_w_out[0].astype(BF16), ffn_w_in[0].astype(BF16), ffn_w_in[1].astype(BF16),
             ffn_w_out[0].astype(BF16), ffn_w_out[1].astype(BF16)]
    first_h, first_tok = _split_start(first, "gather4", "gather_first_start")
    second_h, second_tok = _split_start(second, "gather4", "gather_second_start", after=first_tok)
    later_h, later_tok = _split_start(later, "gather4", "gather_later_start", after=second_tok)
    (g_ewi,) = _gathered(first_h, "gather_first_wait", later_tok)
    ewi = g_ewi.reshape(EVEN_IN, d)
    w_even_in = jnp.concatenate([ewi[:QKV], ewi[QKV + FOX_HEADS:],
                                 jnp.pad(ewi[QKV:QKV + FOX_HEADS], ((0, LANES - FOX_HEADS), (0, 0)))], axis=0)
    chunk_id = jnp.arange(GMLP_BLOCK) // CHUNK
    gmask = chunk_id[None, :] <= chunk_id[:, None]
    w_spatial = jnp.where(gmask[None], odd_w_s[0], 0.0).astype(BF16)
    bs_col = odd_b_s[0].T
    b_f_col = even_b_f.reshape(FOX_HEADS, 1)
    ln = lambda p, l: p[l:l + 1]

    qkv, bch, fl = _proj(x2d, w_even_in, [(0, QKV, BF16), (QKV, QKV + BCH, F32), (QKV + BCH, EVEN_IN_PAD, F32)], "even_proj")
    fl3 = fl[:, :FOX_HEADS].T.reshape(FOX_HEADS, t // LANES, LANES).transpose(1, 0, 2)
    c3 = _fgate_fwd(fl3, b_f_col)
    c_rows = c3.transpose(1, 0, 2).reshape(FOX_HEADS, t)
    head_lanes = lambda rows: jnp.pad(rows.T, ((0, 0), (0, LANES - FOX_HEADS)))
    qp, kp, vp, kt, vt = _attn_pack(qkv, head_lanes(c_rows))
    attn, lse = _attn_fwd(qp, kp, vt)
    g_ewo, g_small = _gathered(second_h, "gather_second_wait", attn)
    w_even_out = g_ewo.reshape(d, d)
    v_ln_g = g_small[:, 0:2].reshape(1, d)
    v_ln_b = g_small[:, 2:4].reshape(1, d)
    conv_w = g_small[:, 4:7].transpose(1, 0, 2).reshape(CONV_K, CONV_WIDTH)
    conv = _conv_fwd(bch, conv_w)
    x1_b, xh1, rs1 = _mm_res_ln([(attn, w_even_out[:FOX_WIDTH]), (conv, w_even_out[FOX_WIDTH:])], x2d,
                                ln(mix_ln_g, 0), ln(mix_ln_b, 0), "even_out_ln")
    w_odd_in, g_owo, w_fi0, w_fi1, g_fo0, g_fo1 = _gathered(later_h, "gather_later_wait", x1_b)
    w_odd_out = g_owo.reshape(d, d)
    w_ffn_in = [w_fi0, w_fi1]
    w_ffn_out = [g_fo0.reshape(FFN_HIDDEN, d), g_fo1.reshape(FFN_HIDDEN, d)]
    x2_b, ffn0 = _ffn_fwd((xh1, ln(mix_ln_g, 0), ln(mix_ln_b, 0)), x1_b, w_ffn_in[0], w_ffn_out[0],
                          ln(ffn_ln_g, 0), ln(ffn_ln_b, 0), 0)

    sv_odd, rs_odd, gated, x3_b, xh3, rs3 = _gmlp_fwd(
        x2_b, w_odd_in, v_ln_g, v_ln_b, w_spatial, bs_col, w_odd_out, (ffn0[3], ln(ffn_ln_g, 0), ln(ffn_ln_b, 0)),
        ln(mix_ln_g, 1), ln(mix_ln_b, 1))
    _, ffn1 = _ffn_fwd((xh3, ln(mix_ln_g, 1), ln(mix_ln_b, 1)), x3_b, w_ffn_in[1], w_ffn_out[1],
                       ln(ffn_ln_g, 1), ln(ffn_ln_b, 1), 1)

    sq, dz4, d_fg1, d_fb1 = _loss_ln_bwd(ffn1[3], ffn1[4], ln(ffn_ln_g, 1), ln(ffn_ln_b, 1), target)
    loss = lax.psum(0.5 / d * jnp.sum(sq), ("x", "y", "c"))
    (dz3, d_mg1, d_mb1), gi_f1, go_f1 = _ffn_bwd(dz4, ffn1, w_ffn_in[1], w_ffn_out[1], (xh3, rs3, ln(mix_ln_g, 1)), 1)

    go_odd = _mm_tn(gated, dz3, "odd_dw_out", tn=d).reshape(N_CHIPS, 1, d // N_CHIPS, d)
    da_odd, dws, dbs_col, d_vg, d_vb, dz2, d_fg0, d_fb0 = _gmlp_bwd(
        dz3, w_odd_out, sv_odd, rs_odd, v_ln_g, v_ln_b, w_spatial, bs_col, w_odd_in,
        (ffn0[3], ffn0[4], ln(ffn_ln_g, 0)))
    gi_odd = _mm_tn(x2_b, da_odd, "odd_dw_in", tn=d // 2, stack_cols=True)[:, None]
    (dz1, d_mg0, d_mb0), gi_f0, go_f0 = _ffn_bwd(dz2, ffn0, w_ffn_in[0], w_ffn_out[0], (xh1, rs1, ln(mix_ln_g, 0)), 0)

    sent_early = [gi_odd, go_odd, gi_f0[:, None], gi_f1[:, None], go_f0[:, None], go_f1[:, None]]
    early_h, early_tok = _split_start(sent_early, "scatter4", "scatter_early_start")
    dmix = _mm_back([(dz1, 0, d)], w_even_out, "even_dmix", w_rows=False, after=early_tok)
    go_even = jnp.concatenate([_mm_tn(attn, dz1, "even_dw_out_attn", tn=d), _mm_tn(conv, dz1, "even_dw_out_conv", tn=d)],
                              axis=0).reshape(N_CHIPS, 1, d // N_CHIPS, d)
    dbch, dconv_w8 = _conv_bwd(bch, dmix, conv_w)
    qb, dob = _attn_bwd_prep(attn, dmix, qp, head_lanes(lse.reshape(FOX_HEADS, t)))
    dqkv, dc_col = _attn_unpack(*_attn_bwd(qb, kp, vp, dob, kt))
    dc3 = dc_col.T.reshape(FOX_HEADS, t // LANES, LANES).transpose(1, 0, 2)
    dfl3, d_bf = _fgate_bwd(dc3, fl3, b_f_col)
    dfl = jnp.concatenate([dfl3.transpose(1, 0, 2).reshape(FOX_HEADS, t).T.astype(BF16),
                           jnp.zeros((t, LANES - FOX_HEADS), BF16)], axis=1)

    dws_masked = jnp.where(gmask[None], dws, 0.0)
    rep_names = ["odd_w_s", "odd_b_s", "mix_ln_g", "mix_ln_b", "ffn_ln_g", "ffn_ln_b", "even_b_f"]
    rep_grads = [dws_masked, dbs_col.T, jnp.concatenate([d_mg0, d_mg1]), jnp.concatenate([d_mb0, d_mb1]),
                 jnp.concatenate([d_fg0, d_fg1]), jnp.concatenate([d_fb0, d_fb1]), d_bf.reshape(1, FOX_HEADS)]
    rep_w = [(odd_w_s, m_odd_w_s, v_odd_w_s), (odd_b_s, m_odd_b_s, v_odd_b_s), (mix_ln_g, m_mix_ln_g, v_mix_ln_g),
             (mix_ln_b, m_mix_ln_b, v_mix_ln_b), (ffn_ln_g, m_ffn_ln_g, v_ffn_ln_g), (ffn_ln_b, m_ffn_ln_b, v_ffn_ln_b),
             (even_b_f, m_even_b_f, v_even_b_f)]
    rep_rows = [_to_rows(gr) for gr in rep_grads]
    n_rep = sum(r.shape[0] for r in rep_rows)
    pad_rep = (-n_rep) % SUBLANES
    dconv_w = dconv_w8[:CONV_K].reshape(CONV_K, N_CHIPS, LANES).transpose(1, 0, 2).reshape(N_CHIPS * CONV_K, LANES)
    packed = jnp.concatenate(rep_rows + [jnp.zeros((pad_rep, LANES), F32), d_vg.reshape(SUBLANES, LANES),
                                         d_vb.reshape(SUBLANES, LANES), dconv_w, jnp.zeros((4, LANES), F32)], axis=0)
    small_h, small_tok = _split_start([packed], "gather8", "gather_small_start")

    swap_h, swap_tok = _split_start(_scattered(early_h, "scatter_early_wait", small_tok), "swap2", "swap_early_start")
    dw_qkv = _mm_tn(dqkv, x2d, "even_dw_qkv", tn=d, tk=QKV // 2, after=swap_tok)
    dw_bch = _mm_tn(dbch, x2d, "even_dw_bch", tn=d, tk=BCH // 2)
    dw_f = _mm_tn(dfl, x2d, "even_dw_f", tn=d)
    gi_even = jnp.concatenate([dw_qkv, dw_f[:FOX_HEADS], dw_bch], axis=0).reshape(N_CHIPS, 1, -1, LANES)
    sent_late = [gi_even, go_even]
    late_h, late_tok = _split_start(sent_late, "scatter4", "scatter_late_start")
    grad_x = _mm_back([(dqkv, 0, QKV), (dbch, QKV, QKV + BCH), (dfl, QKV + BCH, EVEN_IN_PAD)], w_even_in,
                      "even_dx", w_rows=True, res=dz1, after=late_tok)
    mine, theirs = _split_wait(swap_h, "swap_early_wait", grad_x)
    res = {}
    res["odd_w_in"] = _adamw([mine[0], theirs[0]], odd_w_in, m_odd_w_in, v_odd_w_in, "adamw_odd_w_in")
    res["odd_w_out"] = _adamw([mine[1], theirs[1]], odd_w_out, m_odd_w_out, v_odd_w_out, "adamw_odd_w_out")
    for nm, at, (w, m, v) in (("ffn_w_in", 2, (ffn_w_in, m_ffn_w_in, v_ffn_w_in)),
                              ("ffn_w_out", 4, (ffn_w_out, m_ffn_w_out, v_ffn_w_out))):
        upper = _adamw([mine[at + 1], theirs[at + 1]], w, m, v, f"adamw_{nm}_1", layer=1)
        res[nm] = _adamw([mine[at], theirs[at]], w, m, v, f"adamw_{nm}_0", layer=0, into=upper)
    mine_late = _scattered(late_h, "scatter_late_wait", res["ffn_w_out"][0])
    theirs_late = _exchange(mine_late, "swap2", "swap_late")
    rows = lambda a: jnp.swapaxes(a, 1, 2).reshape(1, -1, LANES)
    back = lambda a: jnp.swapaxes(a.reshape(1, EVEN_IN // N_CHIPS, d), 1, 2)
    res["even_w_in"] = [back(o) for o in _adamw([mine_late[0], theirs_late[0]], rows(even_w_in), rows(m_even_w_in),
                                                rows(v_even_w_in), "adamw_even_w_in")]
    res["even_w_out"] = _adamw([mine_late[1], theirs_late[1]], even_w_out, m_even_w_out, v_even_w_out,
                               "adamw_even_w_out")
    (packed,), (gathered,) = _split_wait(small_h, "gather_small_wait", theirs_late[0])
    gathered = lax.dynamic_update_index_in_dim(gathered, packed, 4 * lax.axis_index("x") + 2 * lax.axis_index("y")
                                               + lax.axis_index("c"), 0)

    base = n_rep + pad_rep
    own_rows = jnp.concatenate([
        lax.dynamic_slice_in_dim(gathered, base + 2 * chip, 2, axis=1),
        lax.dynamic_slice_in_dim(gathered, base + SUBLANES + 2 * chip, 2, axis=1),
        lax.dynamic_slice_in_dim(gathered, base + 2 * SUBLANES + CONV_K * chip, CONV_K, axis=1),
        jnp.zeros((N_DEV, 1, LANES), F32)], axis=1)
    small_parts = jnp.concatenate([gathered[:, :base], own_rows], axis=1)[:, None]

    def pack_small(get):
        rows = [_to_rows(get(tw)) for tw in rep_w] + [jnp.zeros((pad_rep, LANES), F32)]
        rows += [get(sh).reshape(-1, LANES) for sh in ((odd_v_ln_g, m_odd_v_ln_g, v_odd_v_ln_g),
                                                       (odd_v_ln_b, m_odd_v_ln_b, v_odd_v_ln_b),
                                                       (even_conv_w, m_even_conv_w, v_even_conv_w))]
        return jnp.concatenate(rows + [jnp.zeros((1, LANES), F32)], axis=0)[None]

    small_out = _adamw([small_parts], pack_small(lambda tw: tw[0]), pack_small(lambda tw: tw[1]),
                       pack_small(lambda tw: tw[2]), "adamw_small")

    def unpack_small(rows3):
        rows = rows3[0]
        out, off = {}, 0
        for nm, (w, _, _), r in zip(rep_names, rep_w, rep_rows):
            out[nm] = rows[off:off + r.shape[0]].reshape(-1)[:w.size].reshape(w.shape)
            off += r.shape[0]
        off += pad_rep
        out["odd_v_ln_g"] = rows[off:off + 2].reshape(odd_v_ln_g.shape)
        out["odd_v_ln_b"] = rows[off + 2:off + 4].reshape(odd_v_ln_b.shape)
        out["even_conv_w"] = rows[off + 4:off + 4 + CONV_K].reshape(even_conv_w.shape)
        return out

    small = [unpack_small(o) for o in small_out]
    order = ["even_w_in", "even_b_f", "even_conv_w", "even_w_out", "odd_w_in", "odd_v_ln_g", "odd_v_ln_b", "odd_w_s",
             "odd_b_s", "odd_w_out", "mix_ln_g", "mix_ln_b", "ffn_w_in", "ffn_w_out", "ffn_ln_g", "ffn_ln_b"]
    outs = [loss, grad_x[None]]
    for kind in range(4):
        for nm in order:
            outs.append(res[nm][kind] if nm in res else small[kind][nm])
    return tuple(outs)
```

```python
import functools
import math

import jax
import jax.numpy as jnp
from jax import lax
from jax.experimental import pallas as pl
from jax.experimental.pallas import tpu as pltpu

F32 = jnp.float32
BF16 = jnp.bfloat16

D_MODEL = 1024
FOX_HEADS = 8
HEAD_DIM = 64
HEAD_PAIRS = FOX_HEADS // 2
FOX_WIDTH = FOX_HEADS * HEAD_DIM
CONV_WIDTH = 512
CONV_K = 3
QKV = 3 * FOX_WIDTH
BCH = 3 * CONV_WIDTH
EVEN_IN = QKV + FOX_HEADS + BCH
EVEN_IN_PAD = QKV + BCH + 128
GMLP_BLOCK = 128
GMLP_GROUPS = 8
CHUNK = 64
FFN_HIDDEN = 2816
HALF_HIDDEN = FFN_HIDDEN // 2
ALPHA = 4.0 ** 0.25
LN_EPS = 1e-5
ADAM_LR = 0.001
ADAM_B1 = 0.9
ADAM_B2 = 0.999
ADAM_EPS = 1e-08
ADAM_WD = 0.01
ADAM_STEP = 10
N_CHIPS = 4
N_DEV = 8
LANES = 128
SUBLANES = 8
ROW_TILE = 512
FFN_FUSED_ROW_TILE = 256
REDUCE_TILE = 2048
ATT_BLOCK = 512
ATT_FWD_HEADS = 8
ATT_BWD_HEADS = 4
ADAMW_BLOCK_BYTES = 2 ** 20
VMEM_LIMIT = 56 * 2 ** 20
NEG = -1e30
MESH = pl.DeviceIdType.MESH
HIGHEST = lax.Precision.HIGHEST
Q_C, Q_ONE, Q_LSE = 64, 67, 70
K_ONE, K_C, K_ONE2 = 64, 67, 70
V_ONE = 64
DO_DELTA = 65
NT = (((1,), (1,)), ((), ()))
TN = (((0,), (0,)), ((), ()))


def _cp():
    return pltpu.CompilerParams(vmem_limit_bytes=VMEM_LIMIT)


def _resident(shape):
    zeros = (0,) * len(shape)
    return pl.BlockSpec(shape, lambda *_: zeros, pipeline_mode=pl.Buffered(1))


def _sds(shape, dtype):
    return jax.ShapeDtypeStruct(tuple(shape), dtype)


_MASKS = {
    "gather4": [(1, 0, 0), (0, 1, 0), (1, 1, 0)],
    "scatter4": [(1, 0, 0), (0, 1, 0), (1, 1, 0)],
    "swap2": [(0, 0, 1)],
    "gather8": [(0, 0, 1), (0, 1, 0), (0, 1, 1), (1, 0, 0), (1, 0, 1), (1, 1, 0), (1, 1, 1)],
}


def _exchange(arrs, mode, name):
    n = len(arrs)
    masks = _MASKS[mode]
    npeer = len(masks)
    lead = {"gather4": N_CHIPS, "gather8": N_DEV}.get(mode)
    out_shapes = [_sds(((lead,) if lead else ()) + a.shape, a.dtype) for a in arrs]

    def body(*refs):
        ins, outs = refs[:n], refs[n:2 * n]
        send_sems, recv_sems, loc_sems = refs[2 * n:]
        x, y, c = lax.axis_index("x"), lax.axis_index("y"), lax.axis_index("c")
        chip, dev = 2 * x + y, 4 * x + 2 * y + c
        sends, recvs, locs = [], [], []
        for k in range(n):
            if mode == "gather4":
                locs.append(pltpu.make_async_copy(ins[k], outs[k].at[chip], loc_sems.at[k]))
            elif mode == "scatter4":
                locs.append(pltpu.make_async_copy(ins[k].at[chip], outs[k].at[chip], loc_sems.at[k]))
            elif mode == "gather8":
                locs.append(pltpu.make_async_copy(ins[k], outs[k].at[dev], loc_sems.at[k]))
        for cp in locs:
            cp.start()
        for k in range(n):
            for j, (dx, dy, dc) in enumerate(masks):
                px = 1 - x if dx else x
                py = 1 - y if dy else y
                pc = 1 - c if dc else c
                pchip, pdev = 2 * px + py, 4 * px + 2 * py + pc
                if mode == "gather4":
                    src, dst, land = ins[k], outs[k].at[chip], outs[k].at[pchip]
                elif mode == "scatter4":
                    src, dst, land = ins[k].at[pchip], outs[k].at[chip], outs[k].at[pchip]
                elif mode == "swap2":
                    src, dst, land = ins[k], outs[k], outs[k]
                else:
                    src, dst, land = ins[k], outs[k].at[dev], outs[k].at[pdev]
                s = k * npeer + j
                kw = dict(send_sem=send_sems.at[s], recv_sem=recv_sems.at[s], device_id=(px, py, pc),
                          device_id_type=MESH)
                cp = pltpu.make_async_remote_copy(src_ref=src, dst_ref=dst, **kw)
                cp.start()
                sends.append(cp)
                recvs.append(pltpu.make_async_remote_copy(src_ref=src, dst_ref=land, **kw))
        for cp in recvs:
            cp.wait_recv()
        for cp in sends:
            cp.wait_send()
        for cp in locs:
            cp.wait()

    any_spec = pl.BlockSpec(memory_space=pl.ANY)
    outs = pl.pallas_call(
        body,
        out_shape=out_shapes,
        in_specs=[any_spec] * n,
        out_specs=[any_spec] * n,
        scratch_shapes=[pltpu.SemaphoreType.DMA((n * npeer,)), pltpu.SemaphoreType.DMA((n * npeer,)),
                        pltpu.SemaphoreType.DMA((max(n, 1),))],
        name=name,
    )(*arrs)
    return list(outs)


_HBM_SPEC = pl.BlockSpec(memory_space=pltpu.HBM)
_SEM_SPEC = pl.BlockSpec(memory_space=pltpu.SEMAPHORE)
_ANY_SPEC = pl.BlockSpec(memory_space=pl.ANY)
_EFFECT = pltpu.SideEffectType.DATAFLOW_SIDE_EFFECTING


def _split_copies(mode, ins, lands, send_sems, recv_sems):
    x, y, c = lax.axis_index("x"), lax.axis_index("y"), lax.axis_index("c")
    chip, dev = 2 * x + y, 4 * x + 2 * y + c
    masks = _MASKS[mode]
    out = []
    for k in range(len(ins)):
        for j, (dx, dy, dc) in enumerate(masks):
            px = 1 - x if dx else x
            py = 1 - y if dy else y
            pc = 1 - c if dc else c
            pchip, pdev = 2 * px + py, 4 * px + 2 * py + pc
            if mode == "gather4":
                src, dst, land = ins[k], lands[k].at[chip], lands[k].at[pchip]
            elif mode == "scatter4":
                src, dst, land = ins[k].at[pchip], lands[k].at[chip], lands[k].at[pchip]
            elif mode == "swap2":
                src, dst, land = ins[k], lands[k], lands[k]
            else:
                src, dst, land = ins[k], lands[k].at[dev], lands[k].at[pdev]
            s = k * len(masks) + j
            kw = dict(send_sem=send_sems.at[s], recv_sem=recv_sems.at[s], device_id=(px, py, pc), device_id_type=MESH)
            out.append((pltpu.make_async_remote_copy(src_ref=src, dst_ref=dst, **kw),
                        pltpu.make_async_remote_copy(src_ref=src, dst_ref=land, **kw)))
    return out


def _split_start(arrs, mode, name, after=None):
    n = len(arrs)
    nsem = n * len(_MASKS[mode])
    lead = {"gather4": (N_CHIPS,), "gather8": (N_DEV,)}.get(mode, ())
    land_shapes = [lead + a.shape for a in arrs]

    def body(*refs):
        ins, lands = refs[:n], refs[n:2 * n]
        outs = refs[2 * n + (after is not None):]
        for start, _ in _split_copies(mode, ins, lands, outs[0], outs[1]):
            start.start()
        outs[-1][...] = jnp.zeros(outs[-1].shape, F32)

    srcs = [pltpu.with_memory_space_constraint(a, pltpu.HBM) for a in arrs]
    empties = [pltpu.with_memory_space_constraint(lax.empty(s, a.dtype), pltpu.HBM) for s, a in zip(land_shapes, arrs)]
    res = pl.pallas_call(
        body, name=name,
        out_shape=(pltpu.SemaphoreType.DMA((nsem,)), pltpu.SemaphoreType.DMA((nsem,)),
                   *[pltpu.HBM(a.shape, a.dtype) for a in arrs],
                   *[pltpu.HBM(s, a.dtype) for s, a in zip(land_shapes, arrs)],
                   _sds((SUBLANES, LANES), F32)),
        in_specs=[_HBM_SPEC] * (2 * n) + ([_ANY_SPEC] if after is not None else []),
        out_specs=(_SEM_SPEC, _SEM_SPEC, *[_HBM_SPEC] * (2 * n), pl.BlockSpec(memory_space=pltpu.VMEM)),
        input_output_aliases={k: 2 + k for k in range(2 * n)},
        compiler_params=pltpu.CompilerParams(has_side_effects=_EFFECT),
    )(*srcs, *empties, *([after] if after is not None else []))
    return dict(mode=mode, n=n, sems=res[:2], bufs=res[2:2 + 2 * n]), res[-1]


def _split_wait(handle, name, after):
    n, mode = handle["n"], handle["mode"]

    def body(*refs):
        ins, lands = refs[:n], refs[n:2 * n]
        send_sems, recv_sems = refs[2 * n], refs[2 * n + 1]
        for _, arrival in _split_copies(mode, ins, lands, send_sems, recv_sems):
            arrival.wait_send()
            arrival.wait_recv()

    bufs = handle["bufs"]
    res = pl.pallas_call(
        body, name=name,
        out_shape=tuple(pltpu.HBM(b.shape, b.dtype) for b in bufs),
        in_specs=[_HBM_SPEC] * (2 * n) + [_SEM_SPEC, _SEM_SPEC, _ANY_SPEC],
        out_specs=tuple([_HBM_SPEC] * (2 * n)),
        input_output_aliases={k: k for k in range(2 * n)},
        compiler_params=pltpu.CompilerParams(has_side_effects=_EFFECT),
    )(*bufs, *handle["sems"], after)
    return list(res[:n]), list(res[n:])


def _with_own(landed, own):
    chip = 2 * lax.axis_index("x") + lax.axis_index("y")
    return lax.dynamic_update_index_in_dim(landed, own, chip, 0)


def _gathered(handle, name, after):
    sent, landed = _split_wait(handle, name, after)
    return [_with_own(g, own) for g, own in zip(landed, sent)]


def _scattered(handle, name, after):
    chip = 2 * lax.axis_index("x") + lax.axis_index("y")
    sent, landed = _split_wait(handle, name, after)
    return [_with_own(r, lax.dynamic_index_in_dim(g, chip, 0, keepdims=False)) for r, g in zip(landed, sent)]


def _sigmoid(x):
    return 0.5 * jnp.tanh(0.5 * x) + 0.5


def _log_sigmoid(x):
    e = jnp.exp(-jnp.abs(x))
    log1p = jnp.where(e < 1e-2, e * (1.0 - e * (0.5 - e * (1.0 / 3.0))), jnp.log(1.0 + e))
    return jnp.minimum(x, 0.0) - log1p


def _ln_fwd(z):
    mu = jnp.mean(z, axis=-1, keepdims=True)
    zc = z - mu
    var = jnp.mean(zc * zc, axis=-1, keepdims=True)
    rstd = lax.rsqrt(var + LN_EPS)
    return zc * rstd, rstd


def _ln_bwd(dy, xhat, rstd, g):
    dxh = dy * g
    m1 = jnp.mean(dxh, axis=-1, keepdims=True)
    m2 = jnp.mean(dxh * xhat, axis=-1, keepdims=True)
    dz = rstd * (dxh - m1 - xhat * m2)
    return dz, jnp.sum(dy * xhat, axis=0, keepdims=True), jnp.sum(dy, axis=0, keepdims=True)


def _shift_down(z, halo):
    r = lax.broadcasted_iota(jnp.int32, z.shape, 0)
    z1 = jnp.where(r == 0, halo[7:8, :], pltpu.roll(z, 1, 0))
    z2 = jnp.where(r == 0, halo[6:7, :], jnp.where(r == 1, halo[7:8, :], pltpu.roll(z, 2, 0)))
    return z1, z2


def _shift_up(z, halo):
    n = z.shape[0]
    r = lax.broadcasted_iota(jnp.int32, z.shape, 0)
    z1 = jnp.where(r == n - 1, halo[0:1, :], pltpu.roll(z, n - 1, 0))
    z2 = jnp.where(r == n - 1, halo[1:2, :], jnp.where(r == n - 2, halo[0:1, :], pltpu.roll(z, n - 2, 0)))
    return z1, z2


def _accumulate(ref, first, value):
    @pl.when(first)
    def _():
        ref[...] = value

    @pl.when(jnp.logical_not(first))
    def _():
        ref[...] += value


def _proj(x, wt, splits, name):
    t, k = x.shape
    tm = min(ROW_TILE, t)
    w = wt

    def body(x_ref, w_ref, *outs):
        a = x_ref[...].astype(BF16)
        for (lo, hi, dt), o in zip(splits, outs):
            o[...] = lax.dot_general(a, w_ref[lo:hi, :], NT, preferred_element_type=F32).astype(dt)

    return pl.pallas_call(
        body, grid=(t // tm,),
        in_specs=[pl.BlockSpec((tm, k), lambda i: (i, 0)), _resident(w.shape)],
        out_specs=[pl.BlockSpec((tm, hi - lo), lambda i: (i, 0)) for lo, hi, _ in splits],
        out_shape=[_sds((t, hi - lo), dt) for lo, hi, dt in splits],
        compiler_params=_cp(), name=name)(x, w)


def _fgate_fwd(fl3, b_f):
    nc = fl3.shape[0]

    def body(f_ref, b_ref, c_ref):
        r = lax.broadcasted_iota(jnp.int32, (LANES, LANES), 0)
        cidx = lax.broadcasted_iota(jnp.int32, (LANES, LANES), 1)
        upper = (r <= cidx).astype(F32)

        def step(i, carry):
            lf = _log_sigmoid(f_ref[i] + b_ref[...])
            cc = jnp.dot(lf, upper, precision=HIGHEST, preferred_element_type=F32) + carry
            c_ref[i] = cc
            return cc[:, LANES - 1:LANES]

        lax.fori_loop(0, nc, step, jnp.zeros((FOX_HEADS, 1), F32))

    return pl.pallas_call(body, out_shape=_sds(fl3.shape, F32), name="fgate_fwd")(fl3, b_f)


def _fgate_bwd(dc3, fl3, b_f):
    nc = fl3.shape[0]

    def body(dc_ref, f_ref, b_ref, df_ref, db_ref):
        r = lax.broadcasted_iota(jnp.int32, (LANES, LANES), 0)
        cidx = lax.broadcasted_iota(jnp.int32, (LANES, LANES), 1)
        lower = (r >= cidx).astype(F32)

        def step(n, carry):
            suffix, db = carry
            i = nc - 1 - n
            dlf = jnp.dot(dc_ref[i], lower, precision=HIGHEST, preferred_element_type=F32) + suffix
            df = dlf * (1.0 - _sigmoid(f_ref[i] + b_ref[...]))
            df_ref[i] = df
            return dlf[:, 0:1], db + jnp.sum(df, axis=1, keepdims=True)

        zero = jnp.zeros((FOX_HEADS, 1), F32)
        _, db = lax.fori_loop(0, nc, step, (zero, zero))
        db_ref[...] = db

    return pl.pallas_call(body, out_shape=[_sds(fl3.shape, F32), _sds((FOX_HEADS, 1), F32)],
                          name="fgate_bwd")(dc3, fl3, b_f)


def _split3(c):
    hi = c.astype(BF16).astype(F32)
    mid = (c - hi).astype(BF16).astype(F32)
    lo = (c - hi - mid).astype(BF16).astype(F32)
    return hi, mid, lo


PIECE_ONE = 3 * FOX_HEADS


def _piece_rows(values):
    hi, mid, lo = _split3(values)
    lane = lax.broadcasted_iota(jnp.int32, values.shape, 1)
    row = hi + pltpu.roll(mid, FOX_HEADS, 1) + pltpu.roll(lo, 2 * FOX_HEADS, 1) + jnp.where(lane == PIECE_ONE, 1.0, 0.0)
    return row.astype(BF16)


def _piece_selector(start, sign, ones=()):
    sel = [[0.0] * FOX_WIDTH for _ in range(LANES)]
    for h in range(FOX_HEADS):
        for n in range(3):
            sel[n * FOX_HEADS + h][h * HEAD_DIM + start - HEAD_DIM + n] = sign
        for lane in ones:
            sel[PIECE_ONE][h * HEAD_DIM + lane - HEAD_DIM] = 1.0
    return jnp.asarray(sel, BF16)


def _attn_pack(qkv, c_pad):
    t = qkv.shape[0]
    tm = min(ROW_TILE, t)
    hd = HEAD_DIM
    sel_q = _piece_selector(Q_C, 1.0, range(Q_ONE, Q_ONE + 3))
    sel_k = _piece_selector(K_C, -1.0, [*range(K_ONE, K_ONE + 3), *range(K_ONE2, K_ONE2 + 3)])
    sel_v = _piece_selector(HEAD_DIM, 0.0, range(V_ONE, V_ONE + 4))

    def body(x_ref, c_ref, sq_ref, sk_ref, sv_ref, qp_ref, kp_ref, vp_ref, kt_ref, vt_ref):
        pieces = _piece_rows(c_ref[...])
        q_extra = jnp.dot(pieces, sq_ref[...], preferred_element_type=F32).astype(BF16)
        k_extra = jnp.dot(pieces, sk_ref[...], preferred_element_type=F32).astype(BF16)
        v_extra = jnp.dot(pieces, sv_ref[...], preferred_element_type=F32).astype(BF16)
        for h in range(FOX_HEADS):
            hs = slice(h * hd, (h + 1) * hd)
            qp_ref[h, :, :hd] = (x_ref[:, hs].astype(F32) * (hd ** -0.5)).astype(BF16)
            qp_ref[h, :, hd:] = q_extra[:, hs]
            kp_ref[h, :, :hd] = x_ref[:, FOX_WIDTH + h * hd:FOX_WIDTH + (h + 1) * hd]
            kp_ref[h, :, hd:] = k_extra[:, hs]
            vp_ref[h, :, :hd] = x_ref[:, 2 * FOX_WIDTH + h * hd:2 * FOX_WIDTH + (h + 1) * hd]
            vp_ref[h, :, hd:] = v_extra[:, hs]
            kt_ref[h] = kp_ref[h].T
            vt_ref[h] = vp_ref[h].T

    row3 = pl.BlockSpec((FOX_HEADS, tm, LANES), lambda i: (0, i, 0))
    col3 = pl.BlockSpec((FOX_HEADS, LANES, tm), lambda i: (0, 0, i))
    sel = _resident(sel_q.shape)
    return pl.pallas_call(
        body, grid=(t // tm,),
        in_specs=[pl.BlockSpec((tm, QKV), lambda i: (i, 0)), pl.BlockSpec((tm, LANES), lambda i: (i, 0)), sel, sel, sel],
        out_specs=[row3, row3, row3, col3, col3],
        out_shape=[_sds((FOX_HEADS, t, LANES), BF16)] * 3 + [_sds((FOX_HEADS, LANES, t), BF16)] * 2,
        compiler_params=_cp(), name="attn_pack")(qkv, c_pad, sel_q, sel_k, sel_v)


def _triangle(nq, key_major):
    if key_major:
        pairs = [(i, j) for j in range(nq) for i in range(j, nq)]
    else:
        pairs = [(i, j) for i in range(nq) for j in range(i + 1)]
    return jnp.asarray([p[0] for p in pairs], jnp.int32), jnp.asarray([p[1] for p in pairs], jnp.int32)


def _attn_fwd(qp, kp, vt):
    t = qp.shape[1]
    bq = min(ATT_BLOCK, t)
    nq = t // bq
    nh = ATT_FWD_HEADS
    i_tab, j_tab = _triangle(nq, key_major=False)

    def body(it_ref, jt_ref, q_ref, k_ref, vt_ref, o_ref, lse_ref, m_sc, acc_sc):
        s = pl.program_id(1)
        i, j = it_ref[s], jt_ref[s]

        @pl.when(j == 0)
        def _():
            m_sc[...] = jnp.full(m_sc.shape, NEG, F32)
            acc_sc[...] = jnp.zeros(acc_sc.shape, F32)

        def sweep(masked):
            scores = lambda h: lax.dot_general(k_ref[h], q_ref[h], NT, preferred_element_type=F32)

            def accumulate(h, pt, rescale):
                acc_sc[h] = rescale * acc_sc[h] + jnp.dot(vt_ref[h], pt, preferred_element_type=F32)

            ahead, behind = scores(0), None
            for h in range(nh):
                st = ahead
                if h + 1 < nh:
                    ahead = scores(h + 1)
                if behind is not None:
                    accumulate(*behind)
                if masked:
                    key = lax.broadcasted_iota(jnp.int32, (bq, bq), 0)
                    qry = lax.broadcasted_iota(jnp.int32, (bq, bq), 1)
                    st = jnp.where(key <= qry, st, NEG)
                m_prev = m_sc[h]
                m_new = jnp.maximum(m_prev, jnp.max(st, axis=0, keepdims=True))
                behind = (h, jnp.exp(st - m_new).astype(BF16), jnp.exp(m_prev - m_new))
                m_sc[h] = m_new
            accumulate(*behind)

        @pl.when(j < i)
        def _():
            sweep(False)

        @pl.when(j == i)
        def _():
            sweep(True)
            for h in range(nh):
                acc = acc_sc[h]
                denom = acc[V_ONE:V_ONE + 1, :]
                o_ref[:, h * HEAD_DIM:(h + 1) * HEAD_DIM] = (acc[:HEAD_DIM, :] / denom).T.astype(BF16)
                lse_ref[h] = m_sc[h] + jnp.log(denom)

    grid_spec = pltpu.PrefetchScalarGridSpec(
        num_scalar_prefetch=2, grid=(FOX_HEADS // nh, i_tab.shape[0]),
        in_specs=[pl.BlockSpec((nh, bq, LANES), lambda hp, s, it, jt: (hp, it[s], 0)),
                  pl.BlockSpec((nh, bq, LANES), lambda hp, s, it, jt: (hp, jt[s], 0)),
                  pl.BlockSpec((nh, LANES, bq), lambda hp, s, it, jt: (hp, 0, jt[s]))],
        out_specs=[pl.BlockSpec((bq, nh * HEAD_DIM), lambda hp, s, it, jt: (it[s], hp)),
                   pl.BlockSpec((nh, 1, bq), lambda hp, s, it, jt: (hp, 0, it[s]))],
        scratch_shapes=[pltpu.VMEM((nh, 1, bq), F32), pltpu.VMEM((nh, LANES, bq), F32)])
    return pl.pallas_call(body, grid_spec=grid_spec,
                          out_shape=[_sds((t, FOX_WIDTH), BF16), _sds((FOX_HEADS, 1, t), F32)],
                          compiler_params=_cp(), name="attn_fwd")(i_tab, j_tab, qp, kp, vt)


def _conv_fwd(bch, conv_w):
    t = bch.shape[0]
    tm = min(ROW_TILE, t)
    halo_blocks = tm // SUBLANES
    cw = CONV_WIDTH

    def body(cur_ref, prev_ref, w_ref, o_ref):
        i = pl.program_id(0)
        z = cur_ref[:, cw:2 * cw] * cur_ref[:, 2 * cw:]
        zp = jnp.where(i == 0, 0.0, prev_ref[:, cw:2 * cw] * prev_ref[:, 2 * cw:])
        z1, z2 = _shift_down(z, zp)
        y = w_ref[0:1, :] * z2 + w_ref[1:2, :] * z1 + w_ref[2:3, :] * z
        o_ref[...] = (cur_ref[:, :cw] * y).astype(BF16)

    return pl.pallas_call(
        body, grid=(t // tm,),
        in_specs=[pl.BlockSpec((tm, BCH), lambda i: (i, 0)),
                  pl.BlockSpec((SUBLANES, BCH), lambda i: (jnp.maximum(i * halo_blocks - 1, 0), 0)),
                  _resident(conv_w.shape)],
        out_specs=pl.BlockSpec((tm, cw), lambda i: (i, 0)),
        out_shape=_sds((t, cw), BF16), compiler_params=_cp(), name="conv_fwd")(bch, bch, conv_w)


def _mm_res_ln(pairs, res, g, b, name):
    from_ln = isinstance(res, tuple)
    res_args = list(res) if from_ln else [res]
    t, d = res_args[0].shape
    tm = min(ROW_TILE, t)
    n = len(pairs)

    def body(*refs):
        a_refs, w_refs = refs[:n], refs[n:2 * n]
        res_refs = refs[2 * n:2 * n + len(res_args)]
        g_ref, b_ref, yb_ref, xh_ref, rs_ref = refs[2 * n + len(res_args):]
        r = res_refs[0][...]
        if from_ln:
            r = r * res_refs[1][...] + res_refs[2][...]
        z = ALPHA * r
        for a_ref, w_ref in zip(a_refs, w_refs):
            z = z + jnp.dot(a_ref[...].astype(BF16), w_ref[...], preferred_element_type=F32)
        xhat, rstd = _ln_fwd(z)
        yb_ref[...] = (xhat * g_ref[...] + b_ref[...]).astype(BF16)
        xh_ref[...] = xhat
        rs_ref[...] = rstd

    row = lambda i: (i, 0)
    full = pl.BlockSpec((tm, d), row)
    return pl.pallas_call(
        body, grid=(t // tm,),
        in_specs=[pl.BlockSpec((tm, a.shape[1]), row) for a, _ in pairs] + [_resident(w.shape) for _, w in pairs]
        + [full] + [_resident(a.shape) for a in res_args[1:]] + [_resident(g.shape), _resident(b.shape)],
        out_specs=[full, full, pl.BlockSpec((tm, 1), row)],
        out_shape=[_sds((t, d), BF16), _sds((t, d), F32), _sds((t, 1), F32)],
        compiler_params=_cp(), name=name)(*[a for a, _ in pairs], *[w for _, w in pairs], *res_args, g, b)


def _gmlp_fwd(x, w_in, vg, vb, wm, bs_col, w_out, res_ln, g, b):
    t, d = x.shape
    tm = min(ROW_TILE, t)
    gb = GMLP_BLOCK
    rxh, rg, rb = res_ln

    def body(x_ref, w_ref, vg_ref, vb_ref, wm_ref, bs_ref, wo_ref, rxh_ref, rg_ref, rb_ref, g_ref, b_ref,
             sv_ref, rs_ref, o_ref, yb_ref, xh_ref, rsy_ref, a_sc):
        xb = x_ref[...].astype(BF16)
        nc = w_ref.shape[2]
        for j in range(w_ref.shape[0]):
            a_sc[:, j * nc:(j + 1) * nc] = jnp.dot(xb, w_ref[j], preferred_element_type=F32)
        halves = []
        for half in range(2):
            a = a_sc[:, half * d:(half + 1) * d]
            cdf = 0.5 * (1.0 + lax.erf(a * (2.0 ** -0.5)))
            halves.append(a * cdf)
            slope = cdf + a * (jnp.exp(-0.5 * a * a) * (1.0 / math.sqrt(2.0 * math.pi)))
            sv_ref[:, (2 * half + 1) * d:(2 * half + 2) * d] = slope.astype(BF16)
        u = halves[0]
        vhat, rstd = _ln_fwd(halves[1])
        sv_ref[:, :d] = u.astype(BF16)
        sv_ref[:, 2 * d:3 * d] = vhat.astype(BF16)
        rs_ref[...] = rstd
        vln = (vhat * vg_ref[...] + vb_ref[...]).astype(BF16)
        for blk in range(tm // gb):
            rs = slice(blk * gb, (blk + 1) * gb)
            for gi in range(GMLP_GROUPS):
                cs = slice(gi * gb, (gi + 1) * gb)
                s = jnp.dot(wm_ref[gi], vln[rs, cs], preferred_element_type=F32) + bs_ref[:, gi:gi + 1]
                o_ref[rs, cs] = (u[rs, cs] * s).astype(BF16)
        z = ALPHA * (rxh_ref[...] * rg_ref[...] + rb_ref[...]) + jnp.dot(o_ref[...], wo_ref[...], preferred_element_type=F32)
        xhat, rstd_y = _ln_fwd(z)
        yb_ref[...] = (xhat * g_ref[...] + b_ref[...]).astype(BF16)
        xh_ref[...] = xhat
        rsy_ref[...] = rstd_y

    row = lambda i: (i, 0)
    full, col, vec = pl.BlockSpec((tm, d), row), pl.BlockSpec((tm, 1), row), _resident(g.shape)
    return pl.pallas_call(
        body, grid=(t // tm,),
        in_specs=[full, _resident(w_in.shape), _resident(vg.shape), _resident(vb.shape),
                  _resident(wm.shape), _resident(bs_col.shape), _resident(w_out.shape), full, vec, vec, vec, vec],
        out_specs=[pl.BlockSpec((tm, 4 * d), row), col, full, full, full, col],
        out_shape=[_sds((t, 4 * d), BF16), _sds((t, 1), F32), _sds((t, d), BF16), _sds((t, d), BF16), _sds((t, d), F32),
                   _sds((t, 1), F32)],
        scratch_shapes=[pltpu.VMEM((tm, 2 * d), F32)],
        compiler_params=_cp(), name="gmlp_fwd")(x, w_in, vg, vb, wm, bs_col, w_out, rxh, rg, rb, g, b)


def _loss_ln_bwd(xhat, rstd, g, b, target):
    t, d = xhat.shape
    tm = min(ROW_TILE, t)

    def body(xh_ref, rs_ref, g_ref, b_ref, t_ref, sq_ref, dz_ref, dg_ref, db_ref):
        first = pl.program_id(0) == 0
        xh = xh_ref[...]
        err = xh * g_ref[...] + b_ref[...] - t_ref[...]
        dz, dg, db = _ln_bwd(err * (1.0 / d), xh, rs_ref[...], g_ref[...])
        dz_ref[...] = dz
        _accumulate(sq_ref, first, jnp.sum(err * err, axis=0, keepdims=True))
        _accumulate(dg_ref, first, dg)
        _accumulate(db_ref, first, db)

    row = lambda i: (i, 0)
    vec = pl.BlockSpec((1, d), lambda i: (0, 0))
    return pl.pallas_call(
        body, grid=(t // tm,),
        in_specs=[pl.BlockSpec((tm, d), row), pl.BlockSpec((tm, 1), row), _resident(g.shape), _resident(b.shape),
                  pl.BlockSpec((tm, d), row)],
        out_specs=[vec, pl.BlockSpec((tm, d), row), vec, vec],
        out_shape=[_sds((1, d), F32), _sds((t, d), F32), _sds((1, d), F32), _sds((1, d), F32)],
        compiler_params=_cp(), name="loss_ln_bwd")(xhat, rstd, g, b, target)


def _mm_back(pairs, wt, res, after, name):
    t = pairs[0][0].shape[0]
    k = wt.shape[1]
    tm = min(ROW_TILE, t)
    n = len(pairs)

    def body(after_ref, *refs):
        a_refs, w_ref, res_ref, o_ref = refs[:n], refs[n], refs[n + 1], refs[n + 2]
        dx = ALPHA * res_ref[...]
        for a_ref, (_, lo, hi) in zip(a_refs, pairs):
            dx = dx + jnp.dot(a_ref[...].astype(BF16), w_ref[lo:hi, :], preferred_element_type=F32)
        o_ref[...] = dx

    row = lambda i: (i, 0)
    return pl.pallas_call(
        body, grid=(t // tm,),
        in_specs=[_ANY_SPEC] + [pl.BlockSpec((tm, a.shape[1]), row) for a, _, _ in pairs]
        + [_resident(wt.shape), pl.BlockSpec((tm, k), row)],
        out_specs=pl.BlockSpec((tm, k), row), out_shape=_sds((t, k), F32),
        compiler_params=_cp(), name=name)(after, *[a for a, _, _ in pairs], wt, res)


def _mm_tn(a, b, name, *, tn, tk=None, tt=None, stack_cols=False, out_dtype=BF16, after=None):
    t, k = a.shape
    n = b.shape[1]
    tk = k if tk is None else tk
    tt = min(REDUCE_TILE if tt is None else tt, t)
    nt = t // tt

    def body(a_ref, b_ref, *rest):
        o_ref, acc_ref = rest[after is not None:]
        s = pl.program_id(2)
        part = lax.dot_general(a_ref[...].astype(BF16), b_ref[...].astype(BF16), TN, preferred_element_type=F32)
        _accumulate(acc_ref, s == 0, part)

        @pl.when(s == nt - 1)
        def _():
            o_ref[...] = acc_ref[...].astype(out_dtype).reshape(o_ref.shape)

    if stack_cols:
        assert tk == k
        out_spec = pl.BlockSpec((1, k, tn), lambda kk, j, s: (j, 0, 0))
        out_shape = _sds((n // tn, k, tn), out_dtype)
    else:
        out_spec = pl.BlockSpec((tk, tn), lambda kk, j, s: (kk, j))
        out_shape = _sds((k, n), out_dtype)
    return pl.pallas_call(
        body, grid=(k // tk, n // tn, nt),
        in_specs=[pl.BlockSpec((tt, tk), lambda kk, j, s: (s, kk)), pl.BlockSpec((tt, tn), lambda kk, j, s: (s, j))]
        + ([_ANY_SPEC] if after is not None else []),
        out_specs=out_spec, out_shape=out_shape,
        scratch_shapes=[pltpu.VMEM((tk, tn), F32)],
        compiler_params=_cp(), name=name)(a, b, *([after] if after is not None else []))


def _ffn_bwd_rows(dz, wo, gu, wi, ln_below, name):
    t, d = dz.shape
    tm = min(FFN_FUSED_ROW_TILE, t)
    hh = HALF_HIDDEN
    xhat, rstd, g = ln_below

    def body(dz_ref, wo_ref, gu_ref, wi_ref, xh_ref, rs_ref, g_ref, dgu_ref, dzb_ref, dg_ref, db_ref):
        first = pl.program_id(0) == 0
        a = dz_ref[...].astype(BF16)
        for c in range(2):
            gs, us = slice(c * hh, (c + 1) * hh), slice(FFN_HIDDEN + c * hh, FFN_HIDDEN + (c + 1) * hh)
            dh = lax.dot_general(a, wo_ref[gs, :], NT, preferred_element_type=F32)
            dgu_ref[:, gs] = (dh * gu_ref[:, gs].astype(F32)).astype(BF16)
            dgu_ref[:, us] = (dh * gu_ref[:, us].astype(F32)).astype(BF16)
        dx = ALPHA * dz_ref[...]
        for j in range(wi_ref.shape[0]):
            dx = dx + lax.dot_general(dgu_ref[:, j * hh:(j + 1) * hh], wi_ref[j], NT, preferred_element_type=F32)
        dzb, dg, db = _ln_bwd(dx, xh_ref[...], rs_ref[...], g_ref[...])
        dzb_ref[...] = dzb
        _accumulate(dg_ref, first, dg)
        _accumulate(db_ref, first, db)

    row = lambda i: (i, 0)
    wide, full = pl.BlockSpec((tm, 2 * FFN_HIDDEN), row), pl.BlockSpec((tm, d), row)
    vec = pl.BlockSpec((1, d), lambda i: (0, 0))
    return pl.pallas_call(
        body, grid=(t // tm,),
        in_specs=[full, _resident(wo.shape), wide, _resident(wi.shape), full, pl.BlockSpec((tm, 1), row),
                  _resident(g.shape)],
        out_specs=[wide, full, vec, vec],
        out_shape=[_sds((t, 2 * FFN_HIDDEN), BF16), _sds((t, d), F32), _sds((1, d), F32), _sds((1, d), F32)],
        compiler_params=_cp(), name=name)(dz, wo, gu, wi, xhat, rstd, g)


def _gmlp_bwd(dz, w_out, saved, rstd_v, vg, vb, wm, bs_col, w_in, ln_below):
    t, d = dz.shape
    d2 = 2 * d
    tm = min(ROW_TILE, t)
    gb = GMLP_BLOCK
    xhat_below, rstd_below, g_below = ln_below

    def body(dz_ref, wo_ref, sv_ref, rs_ref, vg_ref, vb_ref, wm_ref, bs_ref, wi_ref, xh_ref, rsb_ref, gb_ref,
             da_ref, dws_ref, dbs_ref, dvg_ref, dvb_ref, dzb_ref, dg_ref, db_ref, dvln_sc):
        first = pl.program_id(0) == 0
        u = sv_ref[:, :d].astype(F32)
        vhat = sv_ref[:, 2 * d:3 * d].astype(F32)
        rstd = rs_ref[...]
        vln = (vhat * vg_ref[...] + vb_ref[...]).astype(BF16)
        dgate = lax.dot_general(dz_ref[...].astype(BF16), wo_ref[...], NT, preferred_element_type=F32)

        @pl.when(first)
        def _():
            dws_ref[...] = jnp.zeros(dws_ref.shape, F32)
            dbs_ref[...] = jnp.zeros(dbs_ref.shape, F32)

        for blk in range(tm // gb):
            rs = slice(blk * gb, (blk + 1) * gb)
            for gi in range(GMLP_GROUPS):
                cs = slice(gi * gb, (gi + 1) * gb)
                vblk = vln[rs, cs]
                s = jnp.dot(wm_ref[gi], vblk, preferred_element_type=F32) + bs_ref[:, gi:gi + 1]
                dgb = dgate[rs, cs]
                da_ref[rs, cs] = (dgb * s * sv_ref[rs, d + gi * gb:d + (gi + 1) * gb].astype(F32)).astype(BF16)
                ds = dgb * u[rs, cs]
                dsb = ds.astype(BF16)
                dws_ref[gi] += lax.dot_general(dsb, vblk, NT, preferred_element_type=F32)
                dbs_ref[:, gi:gi + 1] += jnp.sum(ds, axis=1, keepdims=True)
                dvln_sc[rs, cs] = lax.dot_general(wm_ref[gi], dsb, TN, preferred_element_type=F32)
        dv, dvg, dvb = _ln_bwd(dvln_sc[...], vhat, rstd, vg_ref[...])
        da_ref[:, d:] = (dv * sv_ref[:, 3 * d:].astype(F32)).astype(BF16)
        _accumulate(dvg_ref, first, dvg)
        _accumulate(dvb_ref, first, dvb)
        dx = ALPHA * dz_ref[...]
        nc = wi_ref.shape[2]
        for j in range(wi_ref.shape[0]):
            dx = dx + lax.dot_general(da_ref[:, j * nc:(j + 1) * nc], wi_ref[j], NT, preferred_element_type=F32)
        dzb, dg, db = _ln_bwd(dx, xh_ref[...], rsb_ref[...], gb_ref[...])
        dzb_ref[...] = dzb
        _accumulate(dg_ref, first, dg)
        _accumulate(db_ref, first, db)

    row = lambda i: (i, 0)
    full, col = pl.BlockSpec((tm, d), row), pl.BlockSpec((tm, 1), row)
    vec = pl.BlockSpec((1, d), lambda i: (0, 0))
    return pl.pallas_call(
        body, grid=(t // tm,),
        in_specs=[full, _resident(w_out.shape), pl.BlockSpec((tm, 4 * d), row), col,
                  _resident(vg.shape), _resident(vb.shape), _resident(wm.shape), _resident(bs_col.shape),
                  _resident(w_in.shape), full, col, _resident(g_below.shape)],
        out_specs=[pl.BlockSpec((tm, d2), row), pl.BlockSpec(wm.shape, lambda i: (0, 0, 0)),
                   pl.BlockSpec(bs_col.shape, lambda i: (0, 0)), vec, vec, full, vec, vec],
        out_shape=[_sds((t, d2), BF16), _sds(wm.shape, F32), _sds(bs_col.shape, F32), _sds((1, d), F32), _sds((1, d), F32),
                   _sds((t, d), F32), _sds((1, d), F32), _sds((1, d), F32)],
        scratch_shapes=[pltpu.VMEM((tm, d), F32)],
        compiler_params=_cp(), name="gmlp_bwd")(dz, w_out, saved, rstd_v, vg, vb, wm, bs_col, w_in, xhat_below,
                                                rstd_below, g_below)


def _conv_bwd(bch, dconv, conv_w):
    t = bch.shape[0]
    tm = min(ROW_TILE, t)
    nb = t // tm
    halo_blocks = tm // SUBLANES
    cw = CONV_WIDTH

    def body(cur_ref, prev_ref, next_ref, dc_ref, dn_ref, w_ref, o_ref, dw_ref):
        i = pl.program_id(0)
        bgate, cgate, hval = cur_ref[:, :cw], cur_ref[:, cw:2 * cw], cur_ref[:, 2 * cw:]
        z = cgate * hval
        zp = jnp.where(i == 0, 0.0, prev_ref[:, cw:2 * cw] * prev_ref[:, 2 * cw:])
        z1, z2 = _shift_down(z, zp)
        w0, w1, w2 = w_ref[0:1, :], w_ref[1:2, :], w_ref[2:3, :]
        dconv = dc_ref[...]
        o_ref[:, :cw] = (dconv * (w0 * z2 + w1 * z1 + w2 * z)).astype(BF16)
        dy = dconv * bgate
        dyn = jnp.where(i == nb - 1, 0.0, dn_ref[...] * next_ref[:, :cw])
        dy1, dy2 = _shift_up(dy, dyn)
        dz = w2 * dy + w1 * dy1 + w0 * dy2
        o_ref[:, cw:2 * cw] = (dz * hval).astype(BF16)
        o_ref[:, 2 * cw:] = (dz * cgate).astype(BF16)

        @pl.when(i == 0)
        def _():
            dw_ref[...] = jnp.zeros(dw_ref.shape, F32)

        for tap, zs in enumerate((z2, z1, z)):
            dw_ref[tap:tap + 1, :] += jnp.sum(dy * zs, axis=0, keepdims=True)

    last_halo = t // SUBLANES - 1
    return pl.pallas_call(
        body, grid=(nb,),
        in_specs=[pl.BlockSpec((tm, BCH), lambda i: (i, 0)),
                  pl.BlockSpec((SUBLANES, BCH), lambda i: (jnp.maximum(i * halo_blocks - 1, 0), 0)),
                  pl.BlockSpec((SUBLANES, BCH), lambda i: (jnp.minimum((i + 1) * halo_blocks, last_halo), 0)),
                  pl.BlockSpec((tm, cw), lambda i: (i, 0)),
                  pl.BlockSpec((SUBLANES, cw), lambda i: (jnp.minimum((i + 1) * halo_blocks, last_halo), 0)),
                  _resident(conv_w.shape)],
        out_specs=[pl.BlockSpec((tm, BCH), lambda i: (i, 0)), pl.BlockSpec((SUBLANES, cw), lambda i: (0, 0))],
        out_shape=[_sds((t, BCH), BF16), _sds((SUBLANES, cw), F32)],
        compiler_params=_cp(), name="conv_bwd")(bch, bch, bch, dconv, dconv, conv_w)


def _attn_bwd_prep(dz, w_out, o, qp, lse_pad, after):
    t = o.shape[0]
    tm = min(ROW_TILE, t)
    hd = HEAD_DIM
    sel_lse = _piece_selector(Q_LSE, -1.0)
    sel_delta = _piece_selector(DO_DELTA, -1.0)
    head_of = jnp.asarray([[1.0 if col == row // hd else 0.0 for col in range(LANES)] for row in range(FOX_WIDTH)], F32)

    def body(after_ref, dz_ref, wo_ref, o_ref, qp_ref, lse_ref, sl_ref, sd_ref, seg_ref, qb_ref, dob_ref, dconv_ref):
        dzb = dz_ref[...].astype(BF16)
        do = lax.dot_general(dzb, wo_ref[:FOX_WIDTH, :], NT, preferred_element_type=F32)
        dconv_ref[...] = lax.dot_general(dzb, wo_ref[FOX_WIDTH:, :], NT, preferred_element_type=F32)
        delta = jnp.dot(o_ref[...].astype(F32) * do, seg_ref[...], precision=HIGHEST, preferred_element_type=F32)
        lse_extra = jnp.dot(_piece_rows(lse_ref[...]), sl_ref[...], preferred_element_type=F32)
        do_extra = jnp.dot(_piece_rows(delta), sd_ref[...], preferred_element_type=F32).astype(BF16)
        for h in range(FOX_HEADS):
            hs = slice(h * hd, (h + 1) * hd)
            dob_ref[h, :, :hd] = do[:, hs].astype(BF16)
            dob_ref[h, :, hd:] = do_extra[:, hs]
            qb_ref[h, :, :hd] = qp_ref[h, :, :hd]
            qb_ref[h, :, hd:] = (qp_ref[h, :, hd:].astype(F32) + lse_extra[:, hs]).astype(BF16)

    row = lambda i: (i, 0)
    row3 = pl.BlockSpec((FOX_HEADS, tm, LANES), lambda i: (0, i, 0))
    half = pl.BlockSpec((tm, FOX_WIDTH), row)
    return pl.pallas_call(
        body, grid=(t // tm,),
        in_specs=[_ANY_SPEC, pl.BlockSpec((tm, dz.shape[1]), row), _resident(w_out.shape), half, row3,
                  pl.BlockSpec((tm, LANES), row), _resident(sel_lse.shape), _resident(sel_delta.shape),
                  _resident(head_of.shape)],
        out_specs=[row3, row3, half],
        out_shape=[_sds((FOX_HEADS, t, LANES), BF16)] * 2 + [_sds((t, FOX_WIDTH), F32)],
        compiler_params=_cp(), name="attn_bwd_prep")(after, dz, w_out, o, qp, lse_pad, sel_lse, sel_delta, head_of)


def _attn_bwd(qb, kp, vp, dob, kt):
    t = qb.shape[1]
    bq = min(ATT_BLOCK, t)
    nq = t // bq
    i_tab, j_tab = _triangle(nq, key_major=True)

    def body(it_ref, jt_ref, q_ref, k_ref, v_ref, do_ref, kt_ref, dqt_ref, dk_ref, dv_ref, dk_sc, dv_sc):
        s = pl.program_id(1)
        i, j = it_ref[s], jt_ref[s]

        @pl.when(s == 0)
        def _():
            dqt_ref[...] = jnp.zeros(dqt_ref.shape, F32)

        @pl.when(i == j)
        def _():
            dk_sc[...] = jnp.zeros(dk_sc.shape, F32)
            dv_sc[...] = jnp.zeros(dv_sc.shape, F32)

        cols = pl.ds(pl.multiple_of(i * bq, bq), bq)

        def sweep(masked):
            def scores(h):
                return (lax.dot_general(k_ref[h], q_ref[h], NT, preferred_element_type=F32),
                        lax.dot_general(v_ref[h], do_ref[h], NT, preferred_element_type=F32))

            def accumulate(h, ptb, dstb):
                dv_sc[h] += jnp.dot(ptb, do_ref[h], preferred_element_type=F32)
                dk_sc[h] += jnp.dot(dstb, q_ref[h], preferred_element_type=F32)
                dqt_ref[h, :, cols] += jnp.dot(kt_ref[h], dstb, preferred_element_type=F32)

            ahead, behind = scores(0), None
            for h in range(ATT_BWD_HEADS):
                st, dpt = ahead
                if h + 1 < ATT_BWD_HEADS:
                    ahead = scores(h + 1)
                if behind is not None:
                    accumulate(*behind)
                if masked:
                    key = lax.broadcasted_iota(jnp.int32, (bq, bq), 0)
                    qry = lax.broadcasted_iota(jnp.int32, (bq, bq), 1)
                    st = jnp.where(key <= qry, st, NEG)
                pt = jnp.exp(st)
                behind = (h, pt.astype(BF16), (pt * dpt).astype(BF16))
            accumulate(*behind)

        @pl.when(i == j)
        def _():
            sweep(True)

        @pl.when(i > j)
        def _():
            sweep(False)

        @pl.when(i == nq - 1)
        def _():
            dk_ref[...] = dk_sc[...]
            dv_ref[...] = dv_sc[...].astype(BF16)

    nh = ATT_BWD_HEADS
    qblk = pl.BlockSpec((nh, bq, LANES), lambda hp, s, it, jt: (hp, it[s], 0))
    kblk = pl.BlockSpec((nh, bq, LANES), lambda hp, s, it, jt: (hp, jt[s], 0))
    grid_spec = pltpu.PrefetchScalarGridSpec(
        num_scalar_prefetch=2, grid=(FOX_HEADS // nh, i_tab.shape[0]),
        in_specs=[qblk, kblk, kblk, qblk, pl.BlockSpec((nh, LANES, bq), lambda hp, s, it, jt: (hp, 0, jt[s]))],
        out_specs=[pl.BlockSpec((nh, LANES, t), lambda hp, s, it, jt: (hp, 0, 0), pipeline_mode=pl.Buffered(1)),
                   kblk, kblk],
        scratch_shapes=[pltpu.VMEM((nh, bq, LANES), F32), pltpu.VMEM((nh, bq, LANES), F32)])
    return pl.pallas_call(body, grid_spec=grid_spec,
                          out_shape=[_sds((FOX_HEADS, LANES, t), F32), _sds((FOX_HEADS, t, LANES), F32),
                                     _sds((FOX_HEADS, t, LANES), BF16)],
                          compiler_params=_cp(), name="attn_bwd")(i_tab, j_tab, qb, kp, vp, dob, kt)


def _attn_unpack(dqt, dkp, dvp):
    t = dkp.shape[1]
    tm = min(ROW_TILE, t)
    hd = HEAD_DIM

    def body(dqt_ref, dk_ref, dv_ref, o_ref, dc_ref):
        for h in range(FOX_HEADS):
            dq = dqt_ref[h].T
            o_ref[:, h * hd:(h + 1) * hd] = (dq[:, :hd] * (hd ** -0.5)).astype(BF16)
            o_ref[:, FOX_WIDTH + h * hd:FOX_WIDTH + (h + 1) * hd] = dk_ref[h, :, :hd].astype(BF16)
            o_ref[:, 2 * FOX_WIDTH + h * hd:2 * FOX_WIDTH + (h + 1) * hd] = dv_ref[h, :, :hd]
            dc_ref[:, h:h + 1] = dq[:, K_ONE:K_ONE + 1] - dk_ref[h, :, Q_ONE:Q_ONE + 1]

    row3 = pl.BlockSpec((FOX_HEADS, tm, LANES), lambda i: (0, i, 0))
    return pl.pallas_call(
        body, grid=(t // tm,),
        in_specs=[pl.BlockSpec((FOX_HEADS, LANES, tm), lambda i: (0, 0, i)), row3, row3],
        out_specs=[pl.BlockSpec((tm, QKV), lambda i: (i, 0)), pl.BlockSpec((tm, FOX_HEADS), lambda i: (i, 0))],
        out_shape=[_sds((t, QKV), BF16), _sds((t, FOX_HEADS), F32)],
        compiler_params=_cp(), name="attn_unpack")(dqt, dkp, dvp)


def _adamw(parts, w, m, v, name, layer=None, into=None):
    nl, r, c = w.shape
    fits = [cand for cand in [*range(SUBLANES, r, SUBLANES), r] if r % cand == 0 and cand * c * 4 <= ADAMW_BLOCK_BYTES]
    tr = max(fits) if fits else r
    npart = len(parts)
    bc1 = 1.0 - ADAM_B1 ** ADAM_STEP
    bc2 = 1.0 - ADAM_B2 ** ADAM_STEP

    def body(*refs):
        p_refs = refs[:npart]
        w_ref, m_ref, v_ref = refs[npart:npart + 3]
        g_ref, d_ref, nm_ref, nv_ref = refs[-4:]
        sums = []
        for p_ref in p_refs:
            acc = p_ref[0, 0].astype(F32)
            for s in range(1, p_ref.shape[0]):
                acc = acc + p_ref[s, 0].astype(F32)
            sums.append(acc)
        g = sums[0]
        for extra in sums[1:]:
            g = g + extra
        nm = ADAM_B1 * m_ref[0] + (1.0 - ADAM_B1) * g
        nv = ADAM_B2 * v_ref[0] + (1.0 - ADAM_B2) * (g * g)
        m_hat = nm / bc1
        v_hat = nv / bc2
        g_ref[0] = g
        d_ref[0] = -ADAM_LR * (m_hat / (jnp.sqrt(v_hat) + ADAM_EPS) + ADAM_WD * w_ref[0])
        nm_ref[0] = nm
        nv_ref[0] = nv

    first = 0 if layer is None else layer
    blk = pl.BlockSpec((1, tr, c), lambda l, i: (first + l, i, 0))
    extra = [] if into is None else list(into)
    return pl.pallas_call(
        body, grid=(nl if layer is None else 1, r // tr),
        in_specs=[pl.BlockSpec((p.shape[0], 1, tr, c), lambda l, i: (0, l, i, 0)) for p in parts] + [blk, blk, blk]
        + [_ANY_SPEC] * len(extra),
        out_specs=[blk] * 4, out_shape=[_sds(w.shape, F32)] * 4,
        input_output_aliases={npart + 3 + k: k for k in range(len(extra))},
        compiler_params=_cp(), name=name)(*parts, w, m, v, *extra)


def _to_rows(a):
    flat = a.reshape(-1)
    pad = (-flat.shape[0]) % LANES
    if pad:
        flat = jnp.concatenate([flat, jnp.zeros((pad,), flat.dtype)])
    return flat.reshape(-1, LANES)


def _by_owner_cols(dw):
    k, n = dw.shape
    return dw.reshape(k, N_CHIPS, n // N_CHIPS).transpose(1, 0, 2)[:, None]


def _ffn_fwd(xin_ln, xin_b, wi, wo, g, b, layer):
    t, d = xin_b.shape
    tm = min(FFN_FUSED_ROW_TILE, t)
    hh = HALF_HIDDEN
    rxh, rg, rb = xin_ln

    def body(x_ref, wi_ref, wo_ref, rxh_ref, rg_ref, rb_ref, g_ref, b_ref, gu_ref, h_ref, yb_ref, xh_ref, rs_ref):
        a = x_ref[...]
        for c in range(2):
            gs, us = slice(c * hh, (c + 1) * hh), slice(FFN_HIDDEN + c * hh, FFN_HIDDEN + (c + 1) * hh)
            gate = jnp.dot(a, wi_ref[c], preferred_element_type=F32)
            up = jnp.dot(a, wi_ref[2 + c], preferred_element_type=F32)
            sig = _sigmoid(gate)
            silu = gate * sig
            gu_ref[:, gs] = (up * sig * (1.0 + gate * (1.0 - sig))).astype(BF16)
            gu_ref[:, us] = silu.astype(BF16)
            h_ref[:, gs] = (silu * up).astype(BF16)
        z = ALPHA * (rxh_ref[...] * rg_ref[...] + rb_ref[...]) + jnp.dot(h_ref[...], wo_ref[...], preferred_element_type=F32)
        xhat, rstd = _ln_fwd(z)
        yb_ref[...] = (xhat * g_ref[...] + b_ref[...]).astype(BF16)
        xh_ref[...] = xhat
        rs_ref[...] = rstd

    row = lambda i: (i, 0)
    full = pl.BlockSpec((tm, d), row)
    vec = _resident(g.shape)
    gu, h, y_b, xhat, rstd = pl.pallas_call(
        body, grid=(t // tm,),
        in_specs=[full, _resident(wi.shape), _resident(wo.shape), full, vec, vec, vec, vec],
        out_specs=[pl.BlockSpec((tm, 2 * FFN_HIDDEN), row), pl.BlockSpec((tm, FFN_HIDDEN), row), full, full,
                   pl.BlockSpec((tm, 1), row)],
        out_shape=[_sds((t, 2 * FFN_HIDDEN), BF16), _sds((t, FFN_HIDDEN), BF16), _sds((t, d), BF16), _sds((t, d), F32),
                   _sds((t, 1), F32)],
        compiler_params=_cp(), name=f"ffn_fwd_rows_{layer}")(xin_b, wi, wo, rxh, rg, rb, g, b)
    return y_b, (xin_b, gu, h, xhat, rstd)


def _ffn_bwd(dz, saved, wi, wo, ln_below, layer):
    xin_b, gu, h, _, _ = saved
    dgu, *below = _ffn_bwd_rows(dz, wo, gu, wi, ln_below, f"ffn_bwd_rows_{layer}")
    g_out = _mm_tn(h, dz, f"ffn_dw_out_{layer}", tn=D_MODEL, tk=HALF_HIDDEN, tt=REDUCE_TILE // 2)
    g_in = _mm_tn(xin_b, dgu, f"ffn_dw_in_{layer}", tn=HALF_HIDDEN, stack_cols=True)
    return below, g_in, g_out.reshape(N_CHIPS, FFN_HIDDEN // N_CHIPS, D_MODEL)


def kernel(x, even_w_in, even_b_f, even_conv_w, even_w_out, odd_w_in, odd_v_ln_g, odd_v_ln_b, odd_w_s, odd_b_s, odd_w_out, mix_ln_g, mix_ln_b, ffn_w_in, ffn_w_out, ffn_ln_g, ffn_ln_b, loss_target, m_even_w_in, m_even_b_f, m_even_conv_w, m_even_w_out, m_odd_w_in, m_odd_v_ln_g, m_odd_v_ln_b, m_odd_w_s, m_odd_b_s, m_odd_w_out, m_mix_ln_g, m_mix_ln_b, m_ffn_w_in, m_ffn_w_out, m_ffn_ln_g, m_ffn_ln_b, v_even_w_in, v_even_b_f, v_even_conv_w, v_even_w_out, v_odd_w_in, v_odd_v_ln_g, v_odd_v_ln_b, v_odd_w_s, v_odd_b_s, v_odd_w_out, v_mix_ln_g, v_mix_ln_b, v_ffn_w_in, v_ffn_w_out, v_ffn_ln_g, v_ffn_ln_b):
    t = x.shape[1]
    d = D_MODEL
    chip = 2 * lax.axis_index("x") + lax.axis_index("y")
    x2d = x[0]
    target = loss_target[0]

    small_shard = jnp.concatenate([odd_v_ln_g.reshape(2, LANES), odd_v_ln_b.reshape(2, LANES),
                                   even_conv_w.reshape(CONV_K, LANES), jnp.zeros((1, LANES), F32)], axis=0)
    first = [jnp.swapaxes(even_w_in[0], 0, 1).astype(BF16)]
    second = [even_w_out[0].astype(BF16), small_shard]
    later = [odd_w_in[0].astype(BF16), odd_w_out[0].astype(BF16), ffn_w_in[0].astype(BF16), ffn_w_in[1].astype(BF16),
             ffn_w_out[0].astype(BF16), ffn_w_out[1].astype(BF16)]
    first_h, first_tok = _split_start(first, "gather4", "gather_first_start")
    second_h, second_tok = _split_start(second, "gather4", "gather_second_start", after=first_tok)
    later_h, later_tok = _split_start(later, "gather4", "gather_later_start", after=second_tok)
    (g_ewi,) = _gathered(first_h, "gather_first_wait", later_tok)
    ewi = g_ewi.reshape(EVEN_IN, d)
    w_even_in = jnp.concatenate([ewi[:QKV], ewi[QKV + FOX_HEADS:],
                                 jnp.pad(ewi[QKV:QKV + FOX_HEADS], ((0, LANES - FOX_HEADS), (0, 0)))], axis=0)
    chunk_id = jnp.arange(GMLP_BLOCK) // CHUNK
    gmask = chunk_id[None, :] <= chunk_id[:, None]
    w_spatial = jnp.where(gmask[None], odd_w_s[0], 0.0).astype(BF16)
    bs_col = odd_b_s[0].T
    b_f_col = even_b_f.reshape(FOX_HEADS, 1)
    ln = lambda p, l: p[l:l + 1]

    qkv, bch, fl = _proj(x2d, w_even_in, [(0, QKV, BF16), (QKV, QKV + BCH, F32), (QKV + BCH, EVEN_IN_PAD, F32)], "even_proj")
    fl3 = fl[:, :FOX_HEADS].T.reshape(FOX_HEADS, t // LANES, LANES).transpose(1, 0, 2)
    c3 = _fgate_fwd(fl3, b_f_col)
    c_rows = c3.transpose(1, 0, 2).reshape(FOX_HEADS, t)
    head_lanes = lambda rows: jnp.pad(rows.T, ((0, 0), (0, LANES - FOX_HEADS)))
    qp, kp, vp, kt, vt = _attn_pack(qkv, head_lanes(c_rows))
    attn, lse = _attn_fwd(qp, kp, vt)
    g_ewo, g_small = _gathered(second_h, "gather_second_wait", attn)
    w_even_out = g_ewo.reshape(d, d)
    v_ln_g = g_small[:, 0:2].reshape(1, d)
    v_ln_b = g_small[:, 2:4].reshape(1, d)
    conv_w = g_small[:, 4:7].transpose(1, 0, 2).reshape(CONV_K, CONV_WIDTH)
    conv = _conv_fwd(bch, conv_w)
    x1_b, xh1, rs1 = _mm_res_ln([(attn, w_even_out[:FOX_WIDTH]), (conv, w_even_out[FOX_WIDTH:])], x2d,
                                ln(mix_ln_g, 0), ln(mix_ln_b, 0), "even_out_ln")
    w_odd_in, g_owo, w_fi0, w_fi1, g_fo0, g_fo1 = _gathered(later_h, "gather_later_wait", x1_b)
    w_odd_out = g_owo.reshape(d, d)
    w_ffn_in = [w_fi0, w_fi1]
    w_ffn_out = [g_fo0.reshape(FFN_HIDDEN, d), g_fo1.reshape(FFN_HIDDEN, d)]
    x2_b, ffn0 = _ffn_fwd((xh1, ln(mix_ln_g, 0), ln(mix_ln_b, 0)), x1_b, w_ffn_in[0], w_ffn_out[0],
                          ln(ffn_ln_g, 0), ln(ffn_ln_b, 0), 0)

    sv_odd, rs_odd, gated, x3_b, xh3, rs3 = _gmlp_fwd(
        x2_b, w_odd_in, v_ln_g, v_ln_b, w_spatial, bs_col, w_odd_out, (ffn0[3], ln(ffn_ln_g, 0), ln(ffn_ln_b, 0)),
        ln(mix_ln_g, 1), ln(mix_ln_b, 1))
    _, ffn1 = _ffn_fwd((xh3, ln(mix_ln_g, 1), ln(mix_ln_b, 1)), x3_b, w_ffn_in[1], w_ffn_out[1],
                       ln(ffn_ln_g, 1), ln(ffn_ln_b, 1), 1)

    sq, dz4, d_fg1, d_fb1 = _loss_ln_bwd(ffn1[3], ffn1[4], ln(ffn_ln_g, 1), ln(ffn_ln_b, 1), target)
    loss = lax.psum(0.5 / d * jnp.sum(sq), ("x", "y", "c"))
    (dz3, d_mg1, d_mb1), gi_f1, go_f1 = _ffn_bwd(dz4, ffn1, w_ffn_in[1], w_ffn_out[1], (xh3, rs3, ln(mix_ln_g, 1)), 1)

    go_odd = _mm_tn(gated, dz3, "odd_dw_out", tn=d).reshape(N_CHIPS, 1, d // N_CHIPS, d)
    da_odd, dws, dbs_col, d_vg, d_vb, dz2, d_fg0, d_fb0 = _gmlp_bwd(
        dz3, w_odd_out, sv_odd, rs_odd, v_ln_g, v_ln_b, w_spatial, bs_col, w_odd_in,
        (ffn0[3], ffn0[4], ln(ffn_ln_g, 0)))
    gi_odd = _mm_tn(x2_b, da_odd, "odd_dw_in", tn=d // 2, stack_cols=True)[:, None]
    (dz1, d_mg0, d_mb0), gi_f0, go_f0 = _ffn_bwd(dz2, ffn0, w_ffn_in[0], w_ffn_out[0], (xh1, rs1, ln(mix_ln_g, 0)), 0)

    sent_early = [gi_odd, go_odd, gi_f0[:, None], gi_f1[:, None], go_f0[:, None], go_f1[:, None]]
    early_h, early_tok = _split_start(sent_early, "scatter4", "scatter_early_start")
    qb, dob, dconv = _attn_bwd_prep(dz1, w_even_out, attn, qp, head_lanes(lse.reshape(FOX_HEADS, t)), early_tok)
    go_even = jnp.concatenate([_mm_tn(attn, dz1, "even_dw_out_attn", tn=d), _mm_tn(conv, dz1, "even_dw_out_conv", tn=d)],
                              axis=0).reshape(N_CHIPS, 1, d // N_CHIPS, d)
    dbch, dconv_w8 = _conv_bwd(bch, dconv, conv_w)
    dqkv, dc_col = _attn_unpack(*_attn_bwd(qb, kp, vp, dob, kt))
    dc3 = dc_col.T.reshape(FOX_HEADS, t // LANES, LANES).transpose(1, 0, 2)
    dfl3, d_bf = _fgate_bwd(dc3, fl3, b_f_col)
    dfl = jnp.concatenate([dfl3.transpose(1, 0, 2).reshape(FOX_HEADS, t).T.astype(BF16),
                           jnp.zeros((t, LANES - FOX_HEADS), BF16)], axis=1)

    dws_masked = jnp.where(gmask[None], dws, 0.0)
    rep_names = ["odd_w_s", "odd_b_s", "mix_ln_g", "mix_ln_b", "ffn_ln_g", "ffn_ln_b", "even_b_f"]
    rep_grads = [dws_masked, dbs_col.T, jnp.concatenate([d_mg0, d_mg1]), jnp.concatenate([d_mb0, d_mb1]),
                 jnp.concatenate([d_fg0, d_fg1]), jnp.concatenate([d_fb0, d_fb1]), d_bf.reshape(1, FOX_HEADS)]
    rep_w = [(odd_w_s, m_odd_w_s, v_odd_w_s), (odd_b_s, m_odd_b_s, v_odd_b_s), (mix_ln_g, m_mix_ln_g, v_mix_ln_g),
             (mix_ln_b, m_mix_ln_b, v_mix_ln_b), (ffn_ln_g, m_ffn_ln_g, v_ffn_ln_g), (ffn_ln_b, m_ffn_ln_b, v_ffn_ln_b),
             (even_b_f, m_even_b_f, v_even_b_f)]
    rep_rows = [_to_rows(gr) for gr in rep_grads]
    n_rep = sum(r.shape[0] for r in rep_rows)
    pad_rep = (-n_rep) % SUBLANES
    dconv_w = dconv_w8[:CONV_K].reshape(CONV_K, N_CHIPS, LANES).transpose(1, 0, 2).reshape(N_CHIPS * CONV_K, LANES)
    packed = jnp.concatenate(rep_rows + [jnp.zeros((pad_rep, LANES), F32), d_vg.reshape(SUBLANES, LANES),
                                         d_vb.reshape(SUBLANES, LANES), dconv_w, jnp.zeros((4, LANES), F32)], axis=0)
    small_h, small_tok = _split_start([packed], "gather8", "gather_small_start")

    swap_h, swap_tok = _split_start(_scattered(early_h, "scatter_early_wait", small_tok), "swap2", "swap_early_start")
    dw_qkv = _mm_tn(dqkv, x2d, "even_dw_qkv", tn=d, tk=QKV // 2, after=swap_tok)
    dw_bch = _mm_tn(dbch, x2d, "even_dw_bch", tn=d, tk=BCH // 2)
    dw_f = _mm_tn(dfl, x2d, "even_dw_f", tn=d)
    gi_even = jnp.concatenate([dw_qkv, dw_f[:FOX_HEADS], dw_bch], axis=0).reshape(N_CHIPS, 1, -1, LANES)
    sent_late = [gi_even, go_even]
    late_h, late_tok = _split_start(sent_late, "scatter4", "scatter_late_start")
    grad_x = _mm_back([(dqkv, 0, QKV), (dbch, QKV, QKV + BCH), (dfl, QKV + BCH, EVEN_IN_PAD)], w_even_in, dz1,
                      late_tok, "even_dx")
    mine, theirs = _split_wait(swap_h, "swap_early_wait", grad_x)
    res = {}
    res["odd_w_in"] = _adamw([mine[0], theirs[0]], odd_w_in, m_odd_w_in, v_odd_w_in, "adamw_odd_w_in")
    res["odd_w_out"] = _adamw([mine[1], theirs[1]], odd_w_out, m_odd_w_out, v_odd_w_out, "adamw_odd_w_out")
    for nm, at, (w, m, v) in (("ffn_w_in", 2, (ffn_w_in, m_ffn_w_in, v_ffn_w_in)),
                              ("ffn_w_out", 4, (ffn_w_out, m_ffn_w_out, v_ffn_w_out))):
        upper = _adamw([mine[at + 1], theirs[at + 1]], w, m, v, f"adamw_{nm}_1", layer=1)
        res[nm] = _adamw([mine[at], theirs[at]], w, m, v, f"adamw_{nm}_0", layer=0, into=upper)
    mine_late = _scattered(late_h, "scatter_late_wait", res["ffn_w_out"][0])
    theirs_late = _exchange(mine_late, "swap2", "swap_late")
    rows = lambda a: jnp.swapaxes(a, 1, 2).reshape(1, -1, LANES)
    back = lambda a: jnp.swapaxes(a.reshape(1, EVEN_IN // N_CHIPS, d), 1, 2)
    res["even_w_in"] = [back(o) for o in _adamw([mine_late[0], theirs_late[0]], rows(even_w_in), rows(m_even_w_in),
                                                rows(v_even_w_in), "adamw_even_w_in")]
    res["even_w_out"] = _adamw([mine_late[1], theirs_late[1]], even_w_out, m_even_w_out, v_even_w_out,
                               "adamw_even_w_out")
    (packed,), (gathered,) = _split_wait(small_h, "gather_small_wait", theirs_late[0])
    gathered = lax.dynamic_update_index_in_dim(gathered, packed, 4 * lax.axis_index("x") + 2 * lax.axis_index("y")
                                               + lax.axis_index("c"), 0)

    base = n_rep + pad_rep
    own_rows = jnp.concatenate([
        lax.dynamic_slice_in_dim(gathered, base + 2 * chip, 2, axis=1),
        lax.dynamic_slice_in_dim(gathered, base + SUBLANES + 2 * chip, 2, axis=1),
        lax.dynamic_slice_in_dim(gathered, base + 2 * SUBLANES + CONV_K * chip, CONV_K, axis=1),
        jnp.zeros((N_DEV, 1, LANES), F32)], axis=1)
    small_parts = jnp.concatenate([gathered[:, :base], own_rows], axis=1)[:, None]

    def pack_small(get):
        rows = [_to_rows(get(tw)) for tw in rep_w] + [jnp.zeros((pad_rep, LANES), F32)]
        rows += [get(sh).reshape(-1, LANES) for sh in ((odd_v_ln_g, m_odd_v_ln_g, v_odd_v_ln_g),
                                                       (odd_v_ln_b, m_odd_v_ln_b, v_odd_v_ln_b),
                                                       (even_conv_w, m_even_conv_w, v_even_conv_w))]
        return jnp.concatenate(rows + [jnp.zeros((1, LANES), F32)], axis=0)[None]

    small_out = _adamw([small_parts], pack_small(lambda tw: tw[0]), pack_small(lambda tw: tw[1]),
                       pack_small(lambda tw: tw[2]), "adamw_small")

    def unpack_small(rows3):
        rows = rows3[0]
        out, off = {}, 0
        for nm, (w, _, _), r in zip(rep_names, rep_w, rep_rows):
            out[nm] = rows[off:off + r.shape[0]].reshape(-1)[:w.size].reshape(w.shape)
            off += r.shape[0]
        off += pad_rep
        out["odd_v_ln_g"] = rows[off:off + 2].reshape(odd_v_ln_g.shape)
        out["odd_v_ln_b"] = rows[off + 2:off + 4].reshape(odd_v_ln_b.shape)
        out["even_conv_w"] = rows[off + 4:off + 4 + CONV_K].reshape(even_conv_w.shape)
        return out

    small = [unpack_small(o) for o in small_out]
    order = ["even_w_in", "even_b_f", "even_conv_w", "even_w_out", "odd_w_in", "odd_v_ln_g", "odd_v_ln_b", "odd_w_s",
             "odd_b_s", "odd_w_out", "mix_ln_g", "mix_ln_b", "ffn_w_in", "ffn_w_out", "ffn_ln_g", "ffn_ln_b"]
    outs = [loss, grad_x[None]]
    for kind in range(4):
        for nm in order:
            outs.append(res[nm][kind] if nm in res else small[kind][nm])
    return tuple(outs)
```

```python
import functools
import math

import jax
import jax.numpy as jnp
from jax import lax
from jax.experimental import pallas as pl
from jax.experimental.pallas import tpu as pltpu

F32 = jnp.float32
BF16 = jnp.bfloat16

D_MODEL = 1024
FOX_HEADS = 8
HEAD_DIM = 64
HEAD_PAIRS = FOX_HEADS // 2
FOX_WIDTH = FOX_HEADS * HEAD_DIM
CONV_WIDTH = 512
CONV_K = 3
QKV = 3 * FOX_WIDTH
BCH = 3 * CONV_WIDTH
EVEN_IN = QKV + FOX_HEADS + BCH
EVEN_IN_PAD = QKV + BCH + 128
GMLP_BLOCK = 128
GMLP_GROUPS = 8
CHUNK = 64
FFN_HIDDEN = 2816
HALF_HIDDEN = FFN_HIDDEN // 2
ALPHA = 4.0 ** 0.25
LN_EPS = 1e-5
ADAM_LR = 0.001
ADAM_B1 = 0.9
ADAM_B2 = 0.999
ADAM_EPS = 1e-08
ADAM_WD = 0.01
ADAM_STEP = 10
N_CHIPS = 4
N_DEV = 8
LANES = 128
SUBLANES = 8
ROW_TILE = 512
FFN_FUSED_ROW_TILE = 256
REDUCE_TILE = 2048
ATT_BLOCK = 512
ATT_FWD_HEADS = 8
ATT_BWD_HEADS = 4
ADAMW_BLOCK_BYTES = 2 ** 20
VMEM_LIMIT = 56 * 2 ** 20
NEG = -1e30
MESH = pl.DeviceIdType.MESH
HIGHEST = lax.Precision.HIGHEST
Q_C, Q_ONE, Q_LSE = 64, 67, 70
K_ONE, K_C, K_ONE2 = 64, 67, 70
V_ONE = 64
DO_DELTA = 65
NT = (((1,), (1,)), ((), ()))
TN = (((0,), (0,)), ((), ()))


def _cp():
    return pltpu.CompilerParams(vmem_limit_bytes=VMEM_LIMIT)


def _resident(shape):
    zeros = (0,) * len(shape)
    return pl.BlockSpec(shape, lambda *_: zeros, pipeline_mode=pl.Buffered(1))


def _sds(shape, dtype):
    return jax.ShapeDtypeStruct(tuple(shape), dtype)


_MASKS = {
    "gather4": [(1, 0, 0), (0, 1, 0), (1, 1, 0)],
    "scatter4": [(1, 0, 0), (0, 1, 0), (1, 1, 0)],
    "swap2": [(0, 0, 1)],
    "gather8": [(0, 0, 1), (0, 1, 0), (0, 1, 1), (1, 0, 0), (1, 0, 1), (1, 1, 0), (1, 1, 1)],
}


def _exchange(arrs, mode, name):
    n = len(arrs)
    masks = _MASKS[mode]
    npeer = len(masks)
    lead = {"gather4": N_CHIPS, "gather8": N_DEV}.get(mode)
    out_shapes = [_sds(((lead,) if lead else ()) + a.shape, a.dtype) for a in arrs]

    def body(*refs):
        ins, outs = refs[:n], refs[n:2 * n]
        send_sems, recv_sems, loc_sems = refs[2 * n:]
        x, y, c = lax.axis_index("x"), lax.axis_index("y"), lax.axis_index("c")
        chip, dev = 2 * x + y, 4 * x + 2 * y + c
        sends, recvs, locs = [], [], []
        for k in range(n):
            if mode == "gather4":
                locs.append(pltpu.make_async_copy(ins[k], outs[k].at[chip], loc_sems.at[k]))
            elif mode == "scatter4":
                locs.append(pltpu.make_async_copy(ins[k].at[chip], outs[k].at[chip], loc_sems.at[k]))
            elif mode == "gather8":
                locs.append(pltpu.make_async_copy(ins[k], outs[k].at[dev], loc_sems.at[k]))
        for cp in locs:
            cp.start()
        for k in range(n):
            for j, (dx, dy, dc) in enumerate(masks):
                px = 1 - x if dx else x
                py = 1 - y if dy else y
                pc = 1 - c if dc else c
                pchip, pdev = 2 * px + py, 4 * px + 2 * py + pc
                if mode == "gather4":
                    src, dst, land = ins[k], outs[k].at[chip], outs[k].at[pchip]
                elif mode == "scatter4":
                    src, dst, land = ins[k].at[pchip], outs[k].at[chip], outs[k].at[pchip]
                elif mode == "swap2":
                    src, dst, land = ins[k], outs[k], outs[k]
                else:
                    src, dst, land = ins[k], outs[k].at[dev], outs[k].at[pdev]
                s = k * npeer + j
                kw = dict(send_sem=send_sems.at[s], recv_sem=recv_sems.at[s], device_id=(px, py, pc),
                          device_id_type=MESH)
                cp = pltpu.make_async_remote_copy(src_ref=src, dst_ref=dst, **kw)
                cp.start()
                sends.append(cp)
                recvs.append(pltpu.make_async_remote_copy(src_ref=src, dst_ref=land, **kw))
        for cp in recvs:
            cp.wait_recv()
        for cp in sends:
            cp.wait_send()
        for cp in locs:
            cp.wait()

    any_spec = pl.BlockSpec(memory_space=pl.ANY)
    outs = pl.pallas_call(
        body,
        out_shape=out_shapes,
        in_specs=[any_spec] * n,
        out_specs=[any_spec] * n,
        scratch_shapes=[pltpu.SemaphoreType.DMA((n * npeer,)), pltpu.SemaphoreType.DMA((n * npeer,)),
                        pltpu.SemaphoreType.DMA((max(n, 1),))],
        name=name,
    )(*arrs)
    return list(outs)


_HBM_SPEC = pl.BlockSpec(memory_space=pltpu.HBM)
_SEM_SPEC = pl.BlockSpec(memory_space=pltpu.SEMAPHORE)
_ANY_SPEC = pl.BlockSpec(memory_space=pl.ANY)
_EFFECT = pltpu.SideEffectType.DATAFLOW_SIDE_EFFECTING


def _split_copies(mode, ins, lands, send_sems, recv_sems):
    x, y, c = lax.axis_index("x"), lax.axis_index("y"), lax.axis_index("c")
    chip, dev = 2 * x + y, 4 * x + 2 * y + c
    masks = _MASKS[mode]
    out = []
    for k in range(len(ins)):
        for j, (dx, dy, dc) in enumerate(masks):
            px = 1 - x if dx else x
            py = 1 - y if dy else y
            pc = 1 - c if dc else c
            pchip, pdev = 2 * px + py, 4 * px + 2 * py + pc
            if mode == "gather4":
                src, dst, land = ins[k], lands[k].at[chip], lands[k].at[pchip]
            elif mode == "scatter4":
                src, dst, land = ins[k].at[pchip], lands[k].at[chip], lands[k].at[pchip]
            elif mode == "swap2":
                src, dst, land = ins[k], lands[k], lands[k]
            else:
                src, dst, land = ins[k], lands[k].at[dev], lands[k].at[pdev]
            s = k * len(masks) + j
            kw = dict(send_sem=send_sems.at[s], recv_sem=recv_sems.at[s], device_id=(px, py, pc), device_id_type=MESH)
            out.append((pltpu.make_async_remote_copy(src_ref=src, dst_ref=dst, **kw),
                        pltpu.make_async_remote_copy(src_ref=src, dst_ref=land, **kw)))
    return out


def _split_start(arrs, mode, name, after=None):
    n = len(arrs)
    nsem = n * len(_MASKS[mode])
    lead = {"gather4": (N_CHIPS,), "gather8": (N_DEV,)}.get(mode, ())
    land_shapes = [lead + a.shape for a in arrs]

    def body(*refs):
        ins, lands = refs[:n], refs[n:2 * n]
        outs = refs[2 * n + (after is not None):]
        for start, _ in _split_copies(mode, ins, lands, outs[0], outs[1]):
            start.start()
        outs[-1][...] = jnp.zeros(outs[-1].shape, F32)

    srcs = [pltpu.with_memory_space_constraint(a, pltpu.HBM) for a in arrs]
    empties = [pltpu.with_memory_space_constraint(lax.empty(s, a.dtype), pltpu.HBM) for s, a in zip(land_shapes, arrs)]
    res = pl.pallas_call(
        body, name=name,
        out_shape=(pltpu.SemaphoreType.DMA((nsem,)), pltpu.SemaphoreType.DMA((nsem,)),
                   *[pltpu.HBM(a.shape, a.dtype) for a in arrs],
                   *[pltpu.HBM(s, a.dtype) for s, a in zip(land_shapes, arrs)],
                   _sds((SUBLANES, LANES), F32)),
        in_specs=[_HBM_SPEC] * (2 * n) + ([_ANY_SPEC] if after is not None else []),
        out_specs=(_SEM_SPEC, _SEM_SPEC, *[_HBM_SPEC] * (2 * n), pl.BlockSpec(memory_space=pltpu.VMEM)),
        input_output_aliases={k: 2 + k for k in range(2 * n)},
        compiler_params=pltpu.CompilerParams(has_side_effects=_EFFECT),
    )(*srcs, *empties, *([after] if after is not None else []))
    return dict(mode=mode, n=n, sems=res[:2], bufs=res[2:2 + 2 * n]), res[-1]


def _split_wait(handle, name, after):
    n, mode = handle["n"], handle["mode"]

    def body(*refs):
        ins, lands = refs[:n], refs[n:2 * n]
        send_sems, recv_sems = refs[2 * n], refs[2 * n + 1]
        for _, arrival in _split_copies(mode, ins, lands, send_sems, recv_sems):
            arrival.wait_send()
            arrival.wait_recv()

    bufs = handle["bufs"]
    res = pl.pallas_call(
        body, name=name,
        out_shape=tuple(pltpu.HBM(b.shape, b.dtype) for b in bufs),
        in_specs=[_HBM_SPEC] * (2 * n) + [_SEM_SPEC, _SEM_SPEC, _ANY_SPEC],
        out_specs=tuple([_HBM_SPEC] * (2 * n)),
        input_output_aliases={k: k for k in range(2 * n)},
        compiler_params=pltpu.CompilerParams(has_side_effects=_EFFECT),
    )(*bufs, *handle["sems"], after)
    return list(res[:n]), list(res[n:])


def _with_own(landed, own):
    chip = 2 * lax.axis_index("x") + lax.axis_index("y")
    return lax.dynamic_update_index_in_dim(landed, own, chip, 0)


def _gathered(handle, name, after):
    sent, landed = _split_wait(handle, name, after)
    return [_with_own(g, own) for g, own in zip(landed, sent)]


def _scattered(handle, name, after):
    chip = 2 * lax.axis_index("x") + lax.axis_index("y")
    sent, landed = _split_wait(handle, name, after)
    return [_with_own(r, lax.dynamic_index_in_dim(g, chip, 0, keepdims=False)) for r, g in zip(landed, sent)]


def _sigmoid(x):
    return 0.5 * jnp.tanh(0.5 * x) + 0.5


def _log_sigmoid(x):
    e = jnp.exp(-jnp.abs(x))
    log1p = jnp.where(e < 1e-2, e * (1.0 - e * (0.5 - e * (1.0 / 3.0))), jnp.log(1.0 + e))
    return jnp.minimum(x, 0.0) - log1p


def _ln_fwd(z):
    mu = jnp.mean(z, axis=-1, keepdims=True)
    zc = z - mu
    var = jnp.mean(zc * zc, axis=-1, keepdims=True)
    rstd = lax.rsqrt(var + LN_EPS)
    return zc * rstd, rstd


def _ln_bwd(dy, xhat, rstd, g):
    dxh = dy * g
    m1 = jnp.mean(dxh, axis=-1, keepdims=True)
    m2 = jnp.mean(dxh * xhat, axis=-1, keepdims=True)
    dz = rstd * (dxh - m1 - xhat * m2)
    return dz, jnp.sum(dy * xhat, axis=0, keepdims=True), jnp.sum(dy, axis=0, keepdims=True)


def _shift_down(z, halo):
    r = lax.broadcasted_iota(jnp.int32, z.shape, 0)
    z1 = jnp.where(r == 0, halo[7:8, :], pltpu.roll(z, 1, 0))
    z2 = jnp.where(r == 0, halo[6:7, :], jnp.where(r == 1, halo[7:8, :], pltpu.roll(z, 2, 0)))
    return z1, z2


def _shift_up(z, halo):
    n = z.shape[0]
    r = lax.broadcasted_iota(jnp.int32, z.shape, 0)
    z1 = jnp.where(r == n - 1, halo[0:1, :], pltpu.roll(z, n - 1, 0))
    z2 = jnp.where(r == n - 1, halo[1:2, :], jnp.where(r == n - 2, halo[0:1, :], pltpu.roll(z, n - 2, 0)))
    return z1, z2


def _accumulate(ref, first, value):
    @pl.when(first)
    def _():
        ref[...] = value

    @pl.when(jnp.logical_not(first))
    def _():
        ref[...] += value


def _proj(x, wt, splits, name):
    t, k = x.shape
    tm = min(ROW_TILE, t)
    w = wt

    def body(x_ref, w_ref, *outs):
        a = x_ref[...].astype(BF16)
        for (lo, hi, dt), o in zip(splits, outs):
            o[...] = lax.dot_general(a, w_ref[lo:hi, :], NT, preferred_element_type=F32).astype(dt)

    return pl.pallas_call(
        body, grid=(t // tm,),
        in_specs=[pl.BlockSpec((tm, k), lambda i: (i, 0)), _resident(w.shape)],
        out_specs=[pl.BlockSpec((tm, hi - lo), lambda i: (i, 0)) for lo, hi, _ in splits],
        out_shape=[_sds((t, hi - lo), dt) for lo, hi, dt in splits],
        compiler_params=_cp(), name=name)(x, w)


def _fgate_fwd(fl3, b_f):
    nc = fl3.shape[0]

    def body(f_ref, b_ref, c_ref):
        r = lax.broadcasted_iota(jnp.int32, (LANES, LANES), 0)
        cidx = lax.broadcasted_iota(jnp.int32, (LANES, LANES), 1)
        upper = (r <= cidx).astype(F32)

        def step(i, carry):
            lf = _log_sigmoid(f_ref[i] + b_ref[...])
            cc = jnp.dot(lf, upper, precision=HIGHEST, preferred_element_type=F32) + carry
            c_ref[i] = cc
            return cc[:, LANES - 1:LANES]

        lax.fori_loop(0, nc, step, jnp.zeros((FOX_HEADS, 1), F32))

    return pl.pallas_call(body, out_shape=_sds(fl3.shape, F32), name="fgate_fwd")(fl3, b_f)


def _fgate_bwd(dc3, fl3, b_f):
    nc = fl3.shape[0]

    def body(dc_ref, f_ref, b_ref, df_ref, db_ref):
        r = lax.broadcasted_iota(jnp.int32, (LANES, LANES), 0)
        cidx = lax.broadcasted_iota(jnp.int32, (LANES, LANES), 1)
        lower = (r >= cidx).astype(F32)

        def step(n, carry):
            suffix, db = carry
            i = nc - 1 - n
            dlf = jnp.dot(dc_ref[i], lower, precision=HIGHEST, preferred_element_type=F32) + suffix
            df = dlf * (1.0 - _sigmoid(f_ref[i] + b_ref[...]))
            df_ref[i] = df
            return dlf[:, 0:1], db + jnp.sum(df, axis=1, keepdims=True)

        zero = jnp.zeros((FOX_HEADS, 1), F32)
        _, db = lax.fori_loop(0, nc, step, (zero, zero))
        db_ref[...] = db

    return pl.pallas_call(body, out_shape=[_sds(fl3.shape, F32), _sds((FOX_HEADS, 1), F32)],
                          name="fgate_bwd")(dc3, fl3, b_f)


def _split3(c):
    hi = c.astype(BF16).astype(F32)
    mid = (c - hi).astype(BF16).astype(F32)
    lo = (c - hi - mid).astype(BF16).astype(F32)
    return hi, mid, lo


PIECE_ONE = 3 * FOX_HEADS


def _piece_rows(values):
    hi, mid, lo = _split3(values)
    lane = lax.broadcasted_iota(jnp.int32, values.shape, 1)
    row = hi + pltpu.roll(mid, FOX_HEADS, 1) + pltpu.roll(lo, 2 * FOX_HEADS, 1) + jnp.where(lane == PIECE_ONE, 1.0, 0.0)
    return row.astype(BF16)


def _piece_selector(start, sign, ones=()):
    sel = [[0.0] * FOX_WIDTH for _ in range(LANES)]
    for h in range(FOX_HEADS):
        for n in range(3):
            sel[n * FOX_HEADS + h][h * HEAD_DIM + start - HEAD_DIM + n] = sign
        for lane in ones:
            sel[PIECE_ONE][h * HEAD_DIM + lane - HEAD_DIM] = 1.0
    return jnp.asarray(sel, BF16)


def _attn_pack(qkv, c_pad):
    t = qkv.shape[0]
    tm = min(ROW_TILE, t)
    hd = HEAD_DIM
    sel_q = _piece_selector(Q_C, 1.0, range(Q_ONE, Q_ONE + 3))
    sel_k = _piece_selector(K_C, -1.0, [*range(K_ONE, K_ONE + 3), *range(K_ONE2, K_ONE2 + 3)])
    sel_v = _piece_selector(HEAD_DIM, 0.0, range(V_ONE, V_ONE + 4))

    def body(x_ref, c_ref, sq_ref, sk_ref, sv_ref, qp_ref, kp_ref, vp_ref, kt_ref, vt_ref):
        pieces = _piece_rows(c_ref[...])
        q_extra = jnp.dot(pieces, sq_ref[...], preferred_element_type=F32).astype(BF16)
        k_extra = jnp.dot(pieces, sk_ref[...], preferred_element_type=F32).astype(BF16)
        v_extra = jnp.dot(pieces, sv_ref[...], preferred_element_type=F32).astype(BF16)
        for h in range(FOX_HEADS):
            hs = slice(h * hd, (h + 1) * hd)
            qp_ref[h, :, :hd] = (x_ref[:, hs].astype(F32) * (hd ** -0.5)).astype(BF16)
            qp_ref[h, :, hd:] = q_extra[:, hs]
            kp_ref[h, :, :hd] = x_ref[:, FOX_WIDTH + h * hd:FOX_WIDTH + (h + 1) * hd]
            kp_ref[h, :, hd:] = k_extra[:, hs]
            vp_ref[h, :, :hd] = x_ref[:, 2 * FOX_WIDTH + h * hd:2 * FOX_WIDTH + (h + 1) * hd]
            vp_ref[h, :, hd:] = v_extra[:, hs]
            kt_ref[h] = kp_ref[h].T
            vt_ref[h] = vp_ref[h].T

    row3 = pl.BlockSpec((FOX_HEADS, tm, LANES), lambda i: (0, i, 0))
    col3 = pl.BlockSpec((FOX_HEADS, LANES, tm), lambda i: (0, 0, i))
    sel = _resident(sel_q.shape)
    return pl.pallas_call(
        body, grid=(t // tm,),
        in_specs=[pl.BlockSpec((tm, QKV), lambda i: (i, 0)), pl.BlockSpec((tm, LANES), lambda i: (i, 0)), sel, sel, sel],
        out_specs=[row3, row3, row3, col3, col3],
        out_shape=[_sds((FOX_HEADS, t, LANES), BF16)] * 3 + [_sds((FOX_HEADS, LANES, t), BF16)] * 2,
        compiler_params=_cp(), name="attn_pack")(qkv, c_pad, sel_q, sel_k, sel_v)


def _triangle(nq, key_major):
    if key_major:
        pairs = [(i, j) for j in range(nq) for i in range(j, nq)]
    else:
        pairs = [(i, j) for i in range(nq) for j in range(i + 1)]
    return jnp.asarray([p[0] for p in pairs], jnp.int32), jnp.asarray([p[1] for p in pairs], jnp.int32)


def _attn_fwd(qp, kp, vt):
    t = qp.shape[1]
    bq = min(ATT_BLOCK, t)
    nq = t // bq
    nh = ATT_FWD_HEADS
    i_tab, j_tab = _triangle(nq, key_major=False)

    def body(it_ref, jt_ref, q_ref, k_ref, vt_ref, o_ref, lse_ref, m_sc, acc_sc):
        s = pl.program_id(1)
        i, j = it_ref[s], jt_ref[s]

        @pl.when(j == 0)
        def _():
            m_sc[...] = jnp.full(m_sc.shape, NEG, F32)
            acc_sc[...] = jnp.zeros(acc_sc.shape, F32)

        def sweep(masked):
            scores = lambda h: lax.dot_general(k_ref[h], q_ref[h], NT, preferred_element_type=F32)

            def accumulate(h, pt, rescale):
                acc_sc[h] = rescale * acc_sc[h] + jnp.dot(vt_ref[h], pt, preferred_element_type=F32)

            ahead, behind = scores(0), None
            for h in range(nh):
                st = ahead
                if h + 1 < nh:
                    ahead = scores(h + 1)
                if behind is not None:
                    accumulate(*behind)
                if masked:
                    key = lax.broadcasted_iota(jnp.int32, (bq, bq), 0)
                    qry = lax.broadcasted_iota(jnp.int32, (bq, bq), 1)
                    st = jnp.where(key <= qry, st, NEG)
                m_prev = m_sc[h]
                m_new = jnp.maximum(m_prev, jnp.max(st, axis=0, keepdims=True))
                behind = (h, jnp.exp(st - m_new).astype(BF16), jnp.exp(m_prev - m_new))
                m_sc[h] = m_new
            accumulate(*behind)

        @pl.when(j < i)
        def _():
            sweep(False)

        @pl.when(j == i)
        def _():
            sweep(True)
            for h in range(nh):
                acc = acc_sc[h]
                denom = acc[V_ONE:V_ONE + 1, :]
                o_ref[:, h * HEAD_DIM:(h + 1) * HEAD_DIM] = (acc[:HEAD_DIM, :] / denom).T.astype(BF16)
                lse_ref[h] = m_sc[h] + jnp.log(denom)

    grid_spec = pltpu.PrefetchScalarGridSpec(
        num_scalar_prefetch=2, grid=(FOX_HEADS // nh, i_tab.shape[0]),
        in_specs=[pl.BlockSpec((nh, bq, LANES), lambda hp, s, it, jt: (hp, it[s], 0)),
                  pl.BlockSpec((nh, bq, LANES), lambda hp, s, it, jt: (hp, jt[s], 0)),
                  pl.BlockSpec((nh, LANES, bq), lambda hp, s, it, jt: (hp, 0, jt[s]))],
        out_specs=[pl.BlockSpec((bq, nh * HEAD_DIM), lambda hp, s, it, jt: (it[s], hp)),
                   pl.BlockSpec((nh, 1, bq), lambda hp, s, it, jt: (hp, 0, it[s]))],
        scratch_shapes=[pltpu.VMEM((nh, 1, bq), F32), pltpu.VMEM((nh, LANES, bq), F32)])
    return pl.pallas_call(body, grid_spec=grid_spec,
                          out_shape=[_sds((t, FOX_WIDTH), BF16), _sds((FOX_HEADS, 1, t), F32)],
                          compiler_params=_cp(), name="attn_fwd")(i_tab, j_tab, qp, kp, vt)


def _conv_fwd(bch, conv_w):
    t = bch.shape[0]
    tm = min(ROW_TILE, t)
    halo_blocks = tm // SUBLANES
    cw = CONV_WIDTH

    def body(cur_ref, prev_ref, w_ref, o_ref):
        i = pl.program_id(0)
        z = cur_ref[:, cw:2 * cw] * cur_ref[:, 2 * cw:]
        zp = jnp.where(i == 0, 0.0, prev_ref[:, cw:2 * cw] * prev_ref[:, 2 * cw:])
        z1, z2 = _shift_down(z, zp)
        y = w_ref[0:1, :] * z2 + w_ref[1:2, :] * z1 + w_ref[2:3, :] * z
        o_ref[...] = (cur_ref[:, :cw] * y).astype(BF16)

    return pl.pallas_call(
        body, grid=(t // tm,),
        in_specs=[pl.BlockSpec((tm, BCH), lambda i: (i, 0)),
                  pl.BlockSpec((SUBLANES, BCH), lambda i: (jnp.maximum(i * halo_blocks - 1, 0), 0)),
                  _resident(conv_w.shape)],
        out_specs=pl.BlockSpec((tm, cw), lambda i: (i, 0)),
        out_shape=_sds((t, cw), BF16), compiler_params=_cp(), name="conv_fwd")(bch, bch, conv_w)


def _mm_res_ln(pairs, res, g, b, name):
    from_ln = isinstance(res, tuple)
    res_args = list(res) if from_ln else [res]
    t, d = res_args[0].shape
    tm = min(ROW_TILE, t)
    n = len(pairs)

    def body(*refs):
        a_refs, w_refs = refs[:n], refs[n:2 * n]
        res_refs = refs[2 * n:2 * n + len(res_args)]
        g_ref, b_ref, yb_ref, xh_ref, rs_ref = refs[2 * n + len(res_args):]
        r = res_refs[0][...]
        if from_ln:
            r = r * res_refs[1][...] + res_refs[2][...]
        z = ALPHA * r
        for a_ref, w_ref in zip(a_refs, w_refs):
            z = z + jnp.dot(a_ref[...].astype(BF16), w_ref[...], preferred_element_type=F32)
        xhat, rstd = _ln_fwd(z)
        yb_ref[...] = (xhat * g_ref[...] + b_ref[...]).astype(BF16)
        xh_ref[...] = xhat
        rs_ref[...] = rstd

    row = lambda i: (i, 0)
    full = pl.BlockSpec((tm, d), row)
    return pl.pallas_call(
        body, grid=(t // tm,),
        in_specs=[pl.BlockSpec((tm, a.shape[1]), row) for a, _ in pairs] + [_resident(w.shape) for _, w in pairs]
        + [full] + [_resident(a.shape) for a in res_args[1:]] + [_resident(g.shape), _resident(b.shape)],
        out_specs=[full, full, pl.BlockSpec((tm, 1), row)],
        out_shape=[_sds((t, d), BF16), _sds((t, d), F32), _sds((t, 1), F32)],
        compiler_params=_cp(), name=name)(*[a for a, _ in pairs], *[w for _, w in pairs], *res_args, g, b)


def _gmlp_fwd(x, w_in, vg, vb, wm, bs_col, w_out, res_ln, g, b):
    t, d = x.shape
    tm = min(ROW_TILE, t)
    gb = GMLP_BLOCK
    rxh, rg, rb = res_ln

    def body(x_ref, w_ref, vg_ref, vb_ref, wm_ref, bs_ref, wo_ref, rxh_ref, rg_ref, rb_ref, g_ref, b_ref,
             sv_ref, rs_ref, o_ref, yb_ref, xh_ref, rsy_ref, a_sc):
        xb = x_ref[...].astype(BF16)
        nc = w_ref.shape[2]
        for j in range(w_ref.shape[0]):
            a_sc[:, j * nc:(j + 1) * nc] = jnp.dot(xb, w_ref[j], preferred_element_type=F32)
        halves = []
        for half in range(2):
            a = a_sc[:, half * d:(half + 1) * d]
            cdf = 0.5 * (1.0 + lax.erf(a * (2.0 ** -0.5)))
            halves.append(a * cdf)
            slope = cdf + a * (jnp.exp(-0.5 * a * a) * (1.0 / math.sqrt(2.0 * math.pi)))
            sv_ref[:, (2 * half + 1) * d:(2 * half + 2) * d] = slope.astype(BF16)
        u = halves[0]
        vhat, rstd = _ln_fwd(halves[1])
        sv_ref[:, :d] = u.astype(BF16)
        sv_ref[:, 2 * d:3 * d] = vhat.astype(BF16)
        rs_ref[...] = rstd
        vln = (vhat * vg_ref[...] + vb_ref[...]).astype(BF16)
        for blk in range(tm // gb):
            rs = slice(blk * gb, (blk + 1) * gb)
            for gi in range(GMLP_GROUPS):
                cs = slice(gi * gb, (gi + 1) * gb)
                s = jnp.dot(wm_ref[gi], vln[rs, cs], preferred_element_type=F32) + bs_ref[:, gi:gi + 1]
                o_ref[rs, cs] = (u[rs, cs] * s).astype(BF16)
        z = ALPHA * (rxh_ref[...] * rg_ref[...] + rb_ref[...]) + jnp.dot(o_ref[...], wo_ref[...], preferred_element_type=F32)
        xhat, rstd_y = _ln_fwd(z)
        yb_ref[...] = (xhat * g_ref[...] + b_ref[...]).astype(BF16)
        xh_ref[...] = xhat
        rsy_ref[...] = rstd_y

    row = lambda i: (i, 0)
    full, col, vec = pl.BlockSpec((tm, d), row), pl.BlockSpec((tm, 1), row), _resident(g.shape)
    return pl.pallas_call(
        body, grid=(t // tm,),
        in_specs=[full, _resident(w_in.shape), _resident(vg.shape), _resident(vb.shape),
                  _resident(wm.shape), _resident(bs_col.shape), _resident(w_out.shape), full, vec, vec, vec, vec],
        out_specs=[pl.BlockSpec((tm, 4 * d), row), col, full, full, full, col],
        out_shape=[_sds((t, 4 * d), BF16), _sds((t, 1), F32), _sds((t, d), BF16), _sds((t, d), BF16), _sds((t, d), F32),
                   _sds((t, 1), F32)],
        scratch_shapes=[pltpu.VMEM((tm, 2 * d), F32)],
        compiler_params=_cp(), name="gmlp_fwd")(x, w_in, vg, vb, wm, bs_col, w_out, rxh, rg, rb, g, b)


def _mm_back(pairs, wt, res, after, name):
    t = pairs[0][0].shape[0]
    k = wt.shape[1]
    tm = min(ROW_TILE, t)
    n = len(pairs)

    def body(after_ref, *refs):
        a_refs, w_ref, res_ref, o_ref = refs[:n], refs[n], refs[n + 1], refs[n + 2]
        dx = ALPHA * res_ref[...]
        for a_ref, (_, lo, hi) in zip(a_refs, pairs):
            dx = dx + jnp.dot(a_ref[...].astype(BF16), w_ref[lo:hi, :], preferred_element_type=F32)
        o_ref[...] = dx

    row = lambda i: (i, 0)
    return pl.pallas_call(
        body, grid=(t // tm,),
        in_specs=[_ANY_SPEC] + [pl.BlockSpec((tm, a.shape[1]), row) for a, _, _ in pairs]
        + [_resident(wt.shape), pl.BlockSpec((tm, k), row)],
        out_specs=pl.BlockSpec((tm, k), row), out_shape=_sds((t, k), F32),
        compiler_params=_cp(), name=name)(after, *[a for a, _, _ in pairs], wt, res)


def _mm_tn(a, b, name, *, tn, tk=None, tt=None, stack_cols=False, out_dtype=BF16, after=None):
    t, k = a.shape
    n = b.shape[1]
    tk = k if tk is None else tk
    tt = min(REDUCE_TILE if tt is None else tt, t)
    nt = t // tt

    def body(a_ref, b_ref, *rest):
        o_ref, acc_ref = rest[after is not None:]
        s = pl.program_id(2)
        part = lax.dot_general(a_ref[...].astype(BF16), b_ref[...].astype(BF16), TN, preferred_element_type=F32)
        _accumulate(acc_ref, s == 0, part)

        @pl.when(s == nt - 1)
        def _():
            o_ref[...] = acc_ref[...].astype(out_dtype).reshape(o_ref.shape)

    if stack_cols:
        assert tk == k
        out_spec = pl.BlockSpec((1, k, tn), lambda kk, j, s: (j, 0, 0))
        out_shape = _sds((n // tn, k, tn), out_dtype)
    else:
        out_spec = pl.BlockSpec((tk, tn), lambda kk, j, s: (kk, j))
        out_shape = _sds((k, n), out_dtype)
    return pl.pallas_call(
        body, grid=(k // tk, n // tn, nt),
        in_specs=[pl.BlockSpec((tt, tk), lambda kk, j, s: (s, kk)), pl.BlockSpec((tt, tn), lambda kk, j, s: (s, j))]
        + ([_ANY_SPEC] if after is not None else []),
        out_specs=out_spec, out_shape=out_shape,
        scratch_shapes=[pltpu.VMEM((tk, tn), F32)],
        compiler_params=_cp(), name=name)(a, b, *([after] if after is not None else []))


def _ffn_bwd_rows(dz, wo, gu, wi, ln_below, name):
    t, d = dz.shape
    tm = min(FFN_FUSED_ROW_TILE, t)
    hh = HALF_HIDDEN
    xhat, rstd, g = ln_below

    def body(dz_ref, wo_ref, gu_ref, wi_ref, xh_ref, rs_ref, g_ref, dgu_ref, dzb_ref, dg_ref, db_ref):
        first = pl.program_id(0) == 0
        a = dz_ref[...].astype(BF16)
        for c in range(2):
            gs, us = slice(c * hh, (c + 1) * hh), slice(FFN_HIDDEN + c * hh, FFN_HIDDEN + (c + 1) * hh)
            dh = lax.dot_general(a, wo_ref[gs, :], NT, preferred_element_type=F32)
            dgu_ref[:, gs] = (dh * gu_ref[:, gs].astype(F32)).astype(BF16)
            dgu_ref[:, us] = (dh * gu_ref[:, us].astype(F32)).astype(BF16)
        dx = ALPHA * dz_ref[...]
        for j in range(wi_ref.shape[0]):
            dx = dx + lax.dot_general(dgu_ref[:, j * hh:(j + 1) * hh], wi_ref[j], NT, preferred_element_type=F32)
        dzb, dg, db = _ln_bwd(dx, xh_ref[...], rs_ref[...], g_ref[...])
        dzb_ref[...] = dzb
        _accumulate(dg_ref, first, dg)
        _accumulate(db_ref, first, db)

    row = lambda i: (i, 0)
    wide, full = pl.BlockSpec((tm, 2 * FFN_HIDDEN), row), pl.BlockSpec((tm, d), row)
    vec = pl.BlockSpec((1, d), lambda i: (0, 0))
    return pl.pallas_call(
        body, grid=(t // tm,),
        in_specs=[full, _resident(wo.shape), wide, _resident(wi.shape), full, pl.BlockSpec((tm, 1), row),
                  _resident(g.shape)],
        out_specs=[wide, full, vec, vec],
        out_shape=[_sds((t, 2 * FFN_HIDDEN), BF16), _sds((t, d), F32), _sds((1, d), F32), _sds((1, d), F32)],
        compiler_params=_cp(), name=name)(dz, wo, gu, wi, xhat, rstd, g)


def _gmlp_bwd(dz, w_out, saved, rstd_v, vg, vb, wm, bs_col, w_in, ln_below):
    t, d = dz.shape
    d2 = 2 * d
    tm = min(ROW_TILE, t)
    gb = GMLP_BLOCK
    xhat_below, rstd_below, g_below = ln_below

    def body(dz_ref, wo_ref, sv_ref, rs_ref, vg_ref, vb_ref, wm_ref, bs_ref, wi_ref, xh_ref, rsb_ref, gb_ref,
             da_ref, dws_ref, dbs_ref, dvg_ref, dvb_ref, dzb_ref, dg_ref, db_ref, dvln_sc):
        first = pl.program_id(0) == 0
        u = sv_ref[:, :d].astype(F32)
        vhat = sv_ref[:, 2 * d:3 * d].astype(F32)
        rstd = rs_ref[...]
        vln = (vhat * vg_ref[...] + vb_ref[...]).astype(BF16)
        dgate = lax.dot_general(dz_ref[...].astype(BF16), wo_ref[...], NT, preferred_element_type=F32)

        @pl.when(first)
        def _():
            dws_ref[...] = jnp.zeros(dws_ref.shape, F32)
            dbs_ref[...] = jnp.zeros(dbs_ref.shape, F32)

        for blk in range(tm // gb):
            rs = slice(blk * gb, (blk + 1) * gb)
            for gi in range(GMLP_GROUPS):
                cs = slice(gi * gb, (gi + 1) * gb)
                vblk = vln[rs, cs]
                s = jnp.dot(wm_ref[gi], vblk, preferred_element_type=F32) + bs_ref[:, gi:gi + 1]
                dgb = dgate[rs, cs]
                da_ref[rs, cs] = (dgb * s * sv_ref[rs, d + gi * gb:d + (gi + 1) * gb].astype(F32)).astype(BF16)
                ds = dgb * u[rs, cs]
                dsb = ds.astype(BF16)
                dws_ref[gi] += lax.dot_general(dsb, vblk, NT, preferred_element_type=F32)
                dbs_ref[:, gi:gi + 1] += jnp.sum(ds, axis=1, keepdims=True)
                dvln_sc[rs, cs] = lax.dot_general(wm_ref[gi], dsb, TN, preferred_element_type=F32)
        dv, dvg, dvb = _ln_bwd(dvln_sc[...], vhat, rstd, vg_ref[...])
        da_ref[:, d:] = (dv * sv_ref[:, 3 * d:].astype(F32)).astype(BF16)
        _accumulate(dvg_ref, first, dvg)
        _accumulate(dvb_ref, first, dvb)
        dx = ALPHA * dz_ref[...]
        nc = wi_ref.shape[2]
        for j in range(wi_ref.shape[0]):
            dx = dx + lax.dot_general(da_ref[:, j * nc:(j + 1) * nc], wi_ref[j], NT, preferred_element_type=F32)
        dzb, dg, db = _ln_bwd(dx, xh_ref[...], rsb_ref[...], gb_ref[...])
        dzb_ref[...] = dzb
        _accumulate(dg_ref, first, dg)
        _accumulate(db_ref, first, db)

    row = lambda i: (i, 0)
    full, col = pl.BlockSpec((tm, d), row), pl.BlockSpec((tm, 1), row)
    vec = pl.BlockSpec((1, d), lambda i: (0, 0))
    return pl.pallas_call(
        body, grid=(t // tm,),
        in_specs=[full, _resident(w_out.shape), pl.BlockSpec((tm, 4 * d), row), col,
                  _resident(vg.shape), _resident(vb.shape), _resident(wm.shape), _resident(bs_col.shape),
                  _resident(w_in.shape), full, col, _resident(g_below.shape)],
        out_specs=[pl.BlockSpec((tm, d2), row), pl.BlockSpec(wm.shape, lambda i: (0, 0, 0)),
                   pl.BlockSpec(bs_col.shape, lambda i: (0, 0)), vec, vec, full, vec, vec],
        out_shape=[_sds((t, d2), BF16), _sds(wm.shape, F32), _sds(bs_col.shape, F32), _sds((1, d), F32), _sds((1, d), F32),
                   _sds((t, d), F32), _sds((1, d), F32), _sds((1, d), F32)],
        scratch_shapes=[pltpu.VMEM((tm, d), F32)],
        compiler_params=_cp(), name="gmlp_bwd")(dz, w_out, saved, rstd_v, vg, vb, wm, bs_col, w_in, xhat_below,
                                                rstd_below, g_below)


def _conv_bwd(bch, dconv, conv_w):
    t = bch.shape[0]
    tm = min(ROW_TILE, t)
    nb = t // tm
    halo_blocks = tm // SUBLANES
    cw = CONV_WIDTH

    def body(cur_ref, prev_ref, next_ref, dc_ref, dn_ref, w_ref, o_ref, dw_ref):
        i = pl.program_id(0)
        bgate, cgate, hval = cur_ref[:, :cw], cur_ref[:, cw:2 * cw], cur_ref[:, 2 * cw:]
        z = cgate * hval
        zp = jnp.where(i == 0, 0.0, prev_ref[:, cw:2 * cw] * prev_ref[:, 2 * cw:])
        z1, z2 = _shift_down(z, zp)
        w0, w1, w2 = w_ref[0:1, :], w_ref[1:2, :], w_ref[2:3, :]
        dconv = dc_ref[...]
        o_ref[:, :cw] = (dconv * (w0 * z2 + w1 * z1 + w2 * z)).astype(BF16)
        dy = dconv * bgate
        dyn = jnp.where(i == nb - 1, 0.0, dn_ref[...] * next_ref[:, :cw])
        dy1, dy2 = _shift_up(dy, dyn)
        dz = w2 * dy + w1 * dy1 + w0 * dy2
        o_ref[:, cw:2 * cw] = (dz * hval).astype(BF16)
        o_ref[:, 2 * cw:] = (dz * cgate).astype(BF16)

        @pl.when(i == 0)
        def _():
            dw_ref[...] = jnp.zeros(dw_ref.shape, F32)

        for tap, zs in enumerate((z2, z1, z)):
            dw_ref[tap:tap + 1, :] += jnp.sum(dy * zs, axis=0, keepdims=True)

    last_halo = t // SUBLANES - 1
    return pl.pallas_call(
        body, grid=(nb,),
        in_specs=[pl.BlockSpec((tm, BCH), lambda i: (i, 0)),
                  pl.BlockSpec((SUBLANES, BCH), lambda i: (jnp.maximum(i * halo_blocks - 1, 0), 0)),
                  pl.BlockSpec((SUBLANES, BCH), lambda i: (jnp.minimum((i + 1) * halo_blocks, last_halo), 0)),
                  pl.BlockSpec((tm, cw), lambda i: (i, 0)),
                  pl.BlockSpec((SUBLANES, cw), lambda i: (jnp.minimum((i + 1) * halo_blocks, last_halo), 0)),
                  _resident(conv_w.shape)],
        out_specs=[pl.BlockSpec((tm, BCH), lambda i: (i, 0)), pl.BlockSpec((SUBLANES, cw), lambda i: (0, 0))],
        out_shape=[_sds((t, BCH), BF16), _sds((SUBLANES, cw), F32)],
        compiler_params=_cp(), name="conv_bwd")(bch, bch, bch, dconv, dconv, conv_w)


def _attn_bwd_prep(dz, w_out, o, qp, lse_pad, after):
    t = o.shape[0]
    tm = min(ROW_TILE, t)
    hd = HEAD_DIM
    sel_lse = _piece_selector(Q_LSE, -1.0)
    sel_delta = _piece_selector(DO_DELTA, -1.0)
    head_of = jnp.asarray([[1.0 if col == row // hd else 0.0 for col in range(LANES)] for row in range(FOX_WIDTH)], F32)

    def body(after_ref, dz_ref, wo_ref, o_ref, qp_ref, lse_ref, sl_ref, sd_ref, seg_ref, qb_ref, dob_ref, dconv_ref):
        dzb = dz_ref[...].astype(BF16)
        do = lax.dot_general(dzb, wo_ref[:FOX_WIDTH, :], NT, preferred_element_type=F32)
        dconv_ref[...] = lax.dot_general(dzb, wo_ref[FOX_WIDTH:, :], NT, preferred_element_type=F32)
        delta = jnp.dot(o_ref[...].astype(F32) * do, seg_ref[...], precision=HIGHEST, preferred_element_type=F32)
        lse_extra = jnp.dot(_piece_rows(lse_ref[...]), sl_ref[...], preferred_element_type=F32)
        do_extra = jnp.dot(_piece_rows(delta), sd_ref[...], preferred_element_type=F32).astype(BF16)
        for h in range(FOX_HEADS):
            hs = slice(h * hd, (h + 1) * hd)
            dob_ref[h, :, :hd] = do[:, hs].astype(BF16)
            dob_ref[h, :, hd:] = do_extra[:, hs]
            qb_ref[h, :, :hd] = qp_ref[h, :, :hd]
            qb_ref[h, :, hd:] = (qp_ref[h, :, hd:].astype(F32) + lse_extra[:, hs]).astype(BF16)

    row = lambda i: (i, 0)
    row3 = pl.BlockSpec((FOX_HEADS, tm, LANES), lambda i: (0, i, 0))
    half = pl.BlockSpec((tm, FOX_WIDTH), row)
    return pl.pallas_call(
        body, grid=(t // tm,),
        in_specs=[_ANY_SPEC, pl.BlockSpec((tm, dz.shape[1]), row), _resident(w_out.shape), half, row3,
                  pl.BlockSpec((tm, LANES), row), _resident(sel_lse.shape), _resident(sel_delta.shape),
                  _resident(head_of.shape)],
        out_specs=[row3, row3, half],
        out_shape=[_sds((FOX_HEADS, t, LANES), BF16)] * 2 + [_sds((t, FOX_WIDTH), F32)],
        compiler_params=_cp(), name="attn_bwd_prep")(after, dz, w_out, o, qp, lse_pad, sel_lse, sel_delta, head_of)


def _attn_bwd(qb, kp, vp, dob, kt):
    t = qb.shape[1]
    bq = min(ATT_BLOCK, t)
    nq = t // bq
    i_tab, j_tab = _triangle(nq, key_major=True)

    def body(it_ref, jt_ref, q_ref, k_ref, v_ref, do_ref, kt_ref, dqt_ref, dk_ref, dv_ref, dk_sc, dv_sc):
        s = pl.program_id(1)
        i, j = it_ref[s], jt_ref[s]

        @pl.when(s == 0)
        def _():
            dqt_ref[...] = jnp.zeros(dqt_ref.shape, F32)

        @pl.when(i == j)
        def _():
            dk_sc[...] = jnp.zeros(dk_sc.shape, F32)
            dv_sc[...] = jnp.zeros(dv_sc.shape, F32)

        cols = pl.ds(pl.multiple_of(i * bq, bq), bq)

        def sweep(masked):
            def scores(h):
                return (lax.dot_general(k_ref[h], q_ref[h], NT, preferred_element_type=F32),
                        lax.dot_general(v_ref[h], do_ref[h], NT, preferred_element_type=F32))

            def accumulate(h, ptb, dstb):
                dv_sc[h] += jnp.dot(ptb, do_ref[h], preferred_element_type=F32)
                dk_sc[h] += jnp.dot(dstb, q_ref[h], preferred_element_type=F32)
                dqt_ref[h, :, cols] += jnp.dot(kt_ref[h], dstb, preferred_element_type=F32)

            ahead, behind = scores(0), None
            for h in range(ATT_BWD_HEADS):
                st, dpt = ahead
                if h + 1 < ATT_BWD_HEADS:
                    ahead = scores(h + 1)
                if behind is not None:
                    accumulate(*behind)
                if masked:
                    key = lax.broadcasted_iota(jnp.int32, (bq, bq), 0)
                    qry = lax.broadcasted_iota(jnp.int32, (bq, bq), 1)
                    st = jnp.where(key <= qry, st, NEG)
                pt = jnp.exp(st)
                behind = (h, pt.astype(BF16), (pt * dpt).astype(BF16))
            accumulate(*behind)

        @pl.when(i == j)
        def _():
            sweep(True)

        @pl.when(i > j)
        def _():
            sweep(False)

        @pl.when(i == nq - 1)
        def _():
            dk_ref[...] = dk_sc[...]
            dv_ref[...] = dv_sc[...].astype(BF16)

    nh = ATT_BWD_HEADS
    qblk = pl.BlockSpec((nh, bq, LANES), lambda hp, s, it, jt: (hp, it[s], 0))
    kblk = pl.BlockSpec((nh, bq, LANES), lambda hp, s, it, jt: (hp, jt[s], 0))
    grid_spec = pltpu.PrefetchScalarGridSpec(
        num_scalar_prefetch=2, grid=(FOX_HEADS // nh, i_tab.shape[0]),
        in_specs=[qblk, kblk, kblk, qblk, pl.BlockSpec((nh, LANES, bq), lambda hp, s, it, jt: (hp, 0, jt[s]))],
        out_specs=[pl.BlockSpec((nh, LANES, t), lambda hp, s, it, jt: (hp, 0, 0), pipeline_mode=pl.Buffered(1)),
                   kblk, kblk],
        scratch_shapes=[pltpu.VMEM((nh, bq, LANES), F32), pltpu.VMEM((nh, bq, LANES), F32)])
    return pl.pallas_call(body, grid_spec=grid_spec,
                          out_shape=[_sds((FOX_HEADS, LANES, t), F32), _sds((FOX_HEADS, t, LANES), F32),
                                     _sds((FOX_HEADS, t, LANES), BF16)],
                          compiler_params=_cp(), name="attn_bwd")(i_tab, j_tab, qb, kp, vp, dob, kt)


def _attn_unpack(dqt, dkp, dvp):
    t = dkp.shape[1]
    tm = min(ROW_TILE, t)
    hd = HEAD_DIM

    def body(dqt_ref, dk_ref, dv_ref, o_ref, dc_ref):
        for h in range(FOX_HEADS):
            dq = dqt_ref[h].T
            o_ref[:, h * hd:(h + 1) * hd] = (dq[:, :hd] * (hd ** -0.5)).astype(BF16)
            o_ref[:, FOX_WIDTH + h * hd:FOX_WIDTH + (h + 1) * hd] = dk_ref[h, :, :hd].astype(BF16)
            o_ref[:, 2 * FOX_WIDTH + h * hd:2 * FOX_WIDTH + (h + 1) * hd] = dv_ref[h, :, :hd]
            dc_ref[:, h:h + 1] = dq[:, K_ONE:K_ONE + 1] - dk_ref[h, :, Q_ONE:Q_ONE + 1]

    row3 = pl.BlockSpec((FOX_HEADS, tm, LANES), lambda i: (0, i, 0))
    return pl.pallas_call(
        body, grid=(t // tm,),
        in_specs=[pl.BlockSpec((FOX_HEADS, LANES, tm), lambda i: (0, 0, i)), row3, row3],
        out_specs=[pl.BlockSpec((tm, QKV), lambda i: (i, 0)), pl.BlockSpec((tm, FOX_HEADS), lambda i: (i, 0))],
        out_shape=[_sds((t, QKV), BF16), _sds((t, FOX_HEADS), F32)],
        compiler_params=_cp(), name="attn_unpack")(dqt, dkp, dvp)


def _adamw(parts, w, m, v, name, layer=None, into=None):
    nl, r, c = w.shape
    fits = [cand for cand in [*range(SUBLANES, r, SUBLANES), r] if r % cand == 0 and cand * c * 4 <= ADAMW_BLOCK_BYTES]
    tr = max(fits) if fits else r
    npart = len(parts)
    bc1 = 1.0 - ADAM_B1 ** ADAM_STEP
    bc2 = 1.0 - ADAM_B2 ** ADAM_STEP

    def body(*refs):
        p_refs = refs[:npart]
        w_ref, m_ref, v_ref = refs[npart:npart + 3]
        g_ref, d_ref, nm_ref, nv_ref = refs[-4:]
        sums = []
        for p_ref in p_refs:
            acc = p_ref[0, 0].astype(F32)
            for s in range(1, p_ref.shape[0]):
                acc = acc + p_ref[s, 0].astype(F32)
            sums.append(acc)
        g = sums[0]
        for extra in sums[1:]:
            g = g + extra
        nm = ADAM_B1 * m_ref[0] + (1.0 - ADAM_B1) * g
        nv = ADAM_B2 * v_ref[0] + (1.0 - ADAM_B2) * (g * g)
        m_hat = nm / bc1
        v_hat = nv / bc2
        g_ref[0] = g
        d_ref[0] = -ADAM_LR * (m_hat / (jnp.sqrt(v_hat) + ADAM_EPS) + ADAM_WD * w_ref[0])
        nm_ref[0] = nm
        nv_ref[0] = nv

    first = 0 if layer is None else layer
    blk = pl.BlockSpec((1, tr, c), lambda l, i: (first + l, i, 0))
    extra = [] if into is None else list(into)
    return pl.pallas_call(
        body, grid=(nl if layer is None else 1, r // tr),
        in_specs=[pl.BlockSpec((p.shape[0], 1, tr, c), lambda l, i: (0, l, i, 0)) for p in parts] + [blk, blk, blk]
        + [_ANY_SPEC] * len(extra),
        out_specs=[blk] * 4, out_shape=[_sds(w.shape, F32)] * 4,
        input_output_aliases={npart + 3 + k: k for k in range(len(extra))},
        compiler_params=_cp(), name=name)(*parts, w, m, v, *extra)


def _to_rows(a):
    flat = a.reshape(-1)
    pad = (-flat.shape[0]) % LANES
    if pad:
        flat = jnp.concatenate([flat, jnp.zeros((pad,), flat.dtype)])
    return flat.reshape(-1, LANES)


def _by_owner_cols(dw):
    k, n = dw.shape
    return dw.reshape(k, N_CHIPS, n // N_CHIPS).transpose(1, 0, 2)[:, None]


def _ffn_fwd(xin_ln, xin_b, wi, wo, g, b, layer, target=None):
    t, d = xin_b.shape
    tm = min(FFN_FUSED_ROW_TILE, t)
    hh = HALF_HIDDEN
    rxh, rg, rb = xin_ln

    def body(x_ref, wi_ref, wo_ref, rxh_ref, rg_ref, rb_ref, g_ref, b_ref, *rest):
        gu_ref, h_ref = rest[target is not None:][:2]
        a = x_ref[...]
        for c in range(2):
            gs, us = slice(c * hh, (c + 1) * hh), slice(FFN_HIDDEN + c * hh, FFN_HIDDEN + (c + 1) * hh)
            gate = jnp.dot(a, wi_ref[c], preferred_element_type=F32)
            up = jnp.dot(a, wi_ref[2 + c], preferred_element_type=F32)
            sig = _sigmoid(gate)
            silu = gate * sig
            gu_ref[:, gs] = (up * sig * (1.0 + gate * (1.0 - sig))).astype(BF16)
            gu_ref[:, us] = silu.astype(BF16)
            h_ref[:, gs] = (silu * up).astype(BF16)
        z = ALPHA * (rxh_ref[...] * rg_ref[...] + rb_ref[...]) + jnp.dot(h_ref[...], wo_ref[...], preferred_element_type=F32)
        xhat, rstd = _ln_fwd(z)
        if target is None:
            yb_ref, xh_ref, rs_ref = rest[2:]
            yb_ref[...] = (xhat * g_ref[...] + b_ref[...]).astype(BF16)
            xh_ref[...] = xhat
            rs_ref[...] = rstd
            return
        sq_ref, dz_ref, dg_ref, db_ref = rest[3:]
        first = pl.program_id(0) == 0
        err = xhat * g_ref[...] + b_ref[...] - rest[0][...]
        dz, dg, db = _ln_bwd(err * (1.0 / d), xhat, rstd, g_ref[...])
        dz_ref[...] = dz
        _accumulate(sq_ref, first, jnp.sum(err * err, axis=0, keepdims=True))
        _accumulate(dg_ref, first, dg)
        _accumulate(db_ref, first, db)

    row = lambda i: (i, 0)
    full = pl.BlockSpec((tm, d), row)
    vec = _resident(g.shape)
    acc = pl.BlockSpec((1, d), lambda i: (0, 0))
    in_specs = [full, _resident(wi.shape), _resident(wo.shape), full, vec, vec, vec, vec]
    out_specs = [pl.BlockSpec((tm, 2 * FFN_HIDDEN), row), pl.BlockSpec((tm, FFN_HIDDEN), row)]
    out_shape = [_sds((t, 2 * FFN_HIDDEN), BF16), _sds((t, FFN_HIDDEN), BF16)]
    args = [xin_b, wi, wo, rxh, rg, rb, g, b]
    if target is None:
        out_specs += [full, full, pl.BlockSpec((tm, 1), row)]
        out_shape += [_sds((t, d), BF16), _sds((t, d), F32), _sds((t, 1), F32)]
    else:
        in_specs.append(full)
        args.append(target)
        out_specs += [acc, full, acc, acc]
        out_shape += [_sds((1, d), F32), _sds((t, d), F32), _sds((1, d), F32), _sds((1, d), F32)]
    gu, h, *tail = pl.pallas_call(body, grid=(t // tm,), in_specs=in_specs, out_specs=out_specs, out_shape=out_shape,
                                  compiler_params=_cp(), name=f"ffn_fwd_rows_{layer}")(*args)
    if target is None:
        y_b, xhat, rstd = tail
        return y_b, (xin_b, gu, h, xhat, rstd)
    return tail, (xin_b, gu, h)


def _ffn_bwd(dz, saved, wi, wo, ln_below, layer):
    xin_b, gu, h = saved[:3]
    dgu, *below = _ffn_bwd_rows(dz, wo, gu, wi, ln_below, f"ffn_bwd_rows_{layer}")
    g_out = _mm_tn(h, dz, f"ffn_dw_out_{layer}", tn=D_MODEL, tk=HALF_HIDDEN, tt=REDUCE_TILE // 2)
    g_in = _mm_tn(xin_b, dgu, f"ffn_dw_in_{layer}", tn=HALF_HIDDEN, stack_cols=True)
    return below, g_in, g_out.reshape(N_CHIPS, FFN_HIDDEN // N_CHIPS, D_MODEL)


def kernel(x, even_w_in, even_b_f, even_conv_w, even_w_out, odd_w_in, odd_v_ln_g, odd_v_ln_b, odd_w_s, odd_b_s, odd_w_out, mix_ln_g, mix_ln_b, ffn_w_in, ffn_w_out, ffn_ln_g, ffn_ln_b, loss_target, m_even_w_in, m_even_b_f, m_even_conv_w, m_even_w_out, m_odd_w_in, m_odd_v_ln_g, m_odd_v_ln_b, m_odd_w_s, m_odd_b_s, m_odd_w_out, m_mix_ln_g, m_mix_ln_b, m_ffn_w_in, m_ffn_w_out, m_ffn_ln_g, m_ffn_ln_b, v_even_w_in, v_even_b_f, v_even_conv_w, v_even_w_out, v_odd_w_in, v_odd_v_ln_g, v_odd_v_ln_b, v_odd_w_s, v_odd_b_s, v_odd_w_out, v_mix_ln_g, v_mix_ln_b, v_ffn_w_in, v_ffn_w_out, v_ffn_ln_g, v_ffn_ln_b):
    t = x.shape[1]
    d = D_MODEL
    chip = 2 * lax.axis_index("x") + lax.axis_index("y")
    x2d = x[0]
    target = loss_target[0]

    small_shard = jnp.concatenate([odd_v_ln_g.reshape(2, LANES), odd_v_ln_b.reshape(2, LANES),
                                   even_conv_w.reshape(CONV_K, LANES), jnp.zeros((1, LANES), F32)], axis=0)
    first = [jnp.swapaxes(even_w_in[0], 0, 1).astype(BF16)]
    second = [even_w_out[0].astype(BF16), small_shard]
    later = [odd_w_in[0].astype(BF16), odd_w_out[0].astype(BF16), ffn_w_in[0].astype(BF16), ffn_w_in[1].astype(BF16),
             ffn_w_out[0].astype(BF16), ffn_w_out[1].astype(BF16)]
    first_h, first_tok = _split_start(first, "gather4", "gather_first_start")
    second_h, second_tok = _split_start(second, "gather4", "gather_second_start", after=first_tok)
    later_h, later_tok = _split_start(later, "gather4", "gather_later_start", after=second_tok)
    (g_ewi,) = _gathered(first_h, "gather_first_wait", later_tok)
    ewi = g_ewi.reshape(EVEN_IN, d)
    w_even_in = jnp.concatenate([ewi[:QKV], ewi[QKV + FOX_HEADS:],
                                 jnp.pad(ewi[QKV:QKV + FOX_HEADS], ((0, LANES - FOX_HEADS), (0, 0)))], axis=0)
    chunk_id = jnp.arange(GMLP_BLOCK) // CHUNK
    gmask = chunk_id[None, :] <= chunk_id[:, None]
    w_spatial = jnp.where(gmask[None], odd_w_s[0], 0.0).astype(BF16)
    bs_col = odd_b_s[0].T
    b_f_col = even_b_f.reshape(FOX_HEADS, 1)
    ln = lambda p, l: p[l:l + 1]

    qkv, bch, fl = _proj(x2d, w_even_in, [(0, QKV, BF16), (QKV, QKV + BCH, F32), (QKV + BCH, EVEN_IN_PAD, F32)], "even_proj")
    fl3 = fl[:, :FOX_HEADS].T.reshape(FOX_HEADS, t // LANES, LANES).transpose(1, 0, 2)
    c3 = _fgate_fwd(fl3, b_f_col)
    c_rows = c3.transpose(1, 0, 2).reshape(FOX_HEADS, t)
    head_lanes = lambda rows: jnp.pad(rows.T, ((0, 0), (0, LANES - FOX_HEADS)))
    qp, kp, vp, kt, vt = _attn_pack(qkv, head_lanes(c_rows))
    attn, lse = _attn_fwd(qp, kp, vt)
    g_ewo, g_small = _gathered(second_h, "gather_second_wait", attn)
    w_even_out = g_ewo.reshape(d, d)
    v_ln_g = g_small[:, 0:2].reshape(1, d)
    v_ln_b = g_small[:, 2:4].reshape(1, d)
    conv_w = g_small[:, 4:7].transpose(1, 0, 2).reshape(CONV_K, CONV_WIDTH)
    conv = _conv_fwd(bch, conv_w)
    x1_b, xh1, rs1 = _mm_res_ln([(attn, w_even_out[:FOX_WIDTH]), (conv, w_even_out[FOX_WIDTH:])], x2d,
                                ln(mix_ln_g, 0), ln(mix_ln_b, 0), "even_out_ln")
    w_odd_in, g_owo, w_fi0, w_fi1, g_fo0, g_fo1 = _gathered(later_h, "gather_later_wait", x1_b)
    w_odd_out = g_owo.reshape(d, d)
    w_ffn_in = [w_fi0, w_fi1]
    w_ffn_out = [g_fo0.reshape(FFN_HIDDEN, d), g_fo1.reshape(FFN_HIDDEN, d)]
    x2_b, ffn0 = _ffn_fwd((xh1, ln(mix_ln_g, 0), ln(mix_ln_b, 0)), x1_b, w_ffn_in[0], w_ffn_out[0],
                          ln(ffn_ln_g, 0), ln(ffn_ln_b, 0), 0)

    sv_odd, rs_odd, gated, x3_b, xh3, rs3 = _gmlp_fwd(
        x2_b, w_odd_in, v_ln_g, v_ln_b, w_spatial, bs_col, w_odd_out, (ffn0[3], ln(ffn_ln_g, 0), ln(ffn_ln_b, 0)),
        ln(mix_ln_g, 1), ln(mix_ln_b, 1))
    (sq, dz4, d_fg1, d_fb1), ffn1 = _ffn_fwd((xh3, ln(mix_ln_g, 1), ln(mix_ln_b, 1)), x3_b, w_ffn_in[1], w_ffn_out[1],
                                             ln(ffn_ln_g, 1), ln(ffn_ln_b, 1), 1, target=target)

    loss = lax.psum(0.5 / d * jnp.sum(sq), ("x", "y", "c"))
    (dz3, d_mg1, d_mb1), gi_f1, go_f1 = _ffn_bwd(dz4, ffn1, w_ffn_in[1], w_ffn_out[1], (xh3, rs3, ln(mix_ln_g, 1)), 1)

    go_odd = _mm_tn(gated, dz3, "odd_dw_out", tn=d).reshape(N_CHIPS, 1, d // N_CHIPS, d)
    da_odd, dws, dbs_col, d_vg, d_vb, dz2, d_fg0, d_fb0 = _gmlp_bwd(
        dz3, w_odd_out, sv_odd, rs_odd, v_ln_g, v_ln_b, w_spatial, bs_col, w_odd_in,
        (ffn0[3], ffn0[4], ln(ffn_ln_g, 0)))
    gi_odd = _mm_tn(x2_b, da_odd, "odd_dw_in", tn=d // 2, stack_cols=True)[:, None]
    (dz1, d_mg0, d_mb0), gi_f0, go_f0 = _ffn_bwd(dz2, ffn0, w_ffn_in[0], w_ffn_out[0], (xh1, rs1, ln(mix_ln_g, 0)), 0)

    sent_early = [gi_odd, go_odd, gi_f0[:, None], gi_f1[:, None], go_f0[:, None], go_f1[:, None]]
    early_h, early_tok = _split_start(sent_early, "scatter4", "scatter_early_start")
    qb, dob, dconv = _attn_bwd_prep(dz1, w_even_out, attn, qp, head_lanes(lse.reshape(FOX_HEADS, t)), early_tok)
    go_even = jnp.concatenate([_mm_tn(attn, dz1, "even_dw_out_attn", tn=d), _mm_tn(conv, dz1, "even_dw_out_conv", tn=d)],
                              axis=0).reshape(N_CHIPS, 1, d // N_CHIPS, d)
    dbch, dconv_w8 = _conv_bwd(bch, dconv, conv_w)
    dqkv, dc_col = _attn_unpack(*_attn_bwd(qb, kp, vp, dob, kt))
    dc3 = dc_col.T.reshape(FOX_HEADS, t // LANES, LANES).transpose(1, 0, 2)
    dfl3, d_bf = _fgate_bwd(dc3, fl3, b_f_col)
    dfl = jnp.concatenate([dfl3.transpose(1, 0, 2).reshape(FOX_HEADS, t).T.astype(BF16),
                           jnp.zeros((t, LANES - FOX_HEADS), BF16)], axis=1)

    dws_masked = jnp.where(gmask[None], dws, 0.0)
    rep_names = ["odd_w_s", "odd_b_s", "mix_ln_g", "mix_ln_b", "ffn_ln_g", "ffn_ln_b", "even_b_f"]
    rep_grads = [dws_masked, dbs_col.T, jnp.concatenate([d_mg0, d_mg1]), jnp.concatenate([d_mb0, d_mb1]),
                 jnp.concatenate([d_fg0, d_fg1]), jnp.concatenate([d_fb0, d_fb1]), d_bf.reshape(1, FOX_HEADS)]
    rep_w = [(odd_w_s, m_odd_w_s, v_odd_w_s), (odd_b_s, m_odd_b_s, v_odd_b_s), (mix_ln_g, m_mix_ln_g, v_mix_ln_g),
             (mix_ln_b, m_mix_ln_b, v_mix_ln_b), (ffn_ln_g, m_ffn_ln_g, v_ffn_ln_g), (ffn_ln_b, m_ffn_ln_b, v_ffn_ln_b),
             (even_b_f, m_even_b_f, v_even_b_f)]
    rep_rows = [_to_rows(gr) for gr in rep_grads]
    n_rep = sum(r.shape[0] for r in rep_rows)
    pad_rep = (-n_rep) % SUBLANES
    dconv_w = dconv_w8[:CONV_K].reshape(CONV_K, N_CHIPS, LANES).transpose(1, 0, 2).reshape(N_CHIPS * CONV_K, LANES)
    packed = jnp.concatenate(rep_rows + [jnp.zeros((pad_rep, LANES), F32), d_vg.reshape(SUBLANES, LANES),
                                         d_vb.reshape(SUBLANES, LANES), dconv_w, jnp.zeros((4, LANES), F32)], axis=0)
    small_h, small_tok = _split_start([packed], "gather8", "gather_small_start")

    swap_h, swap_tok = _split_start(_scattered(early_h, "scatter_early_wait", small_tok), "swap2", "swap_early_start")
    dw_qkv = _mm_tn(dqkv, x2d, "even_dw_qkv", tn=d, tk=QKV // 2, after=swap_tok)
    dw_bch = _mm_tn(dbch, x2d, "even_dw_bch", tn=d, tk=BCH // 2)
    dw_f = _mm_tn(dfl, x2d, "even_dw_f", tn=d)
    gi_even = jnp.concatenate([dw_qkv, dw_f[:FOX_HEADS], dw_bch], axis=0).reshape(N_CHIPS, 1, -1, LANES)
    sent_late = [gi_even, go_even]
    late_h, late_tok = _split_start(sent_late, "scatter4", "scatter_late_start")
    grad_x = _mm_back([(dqkv, 0, QKV), (dbch, QKV, QKV + BCH), (dfl, QKV + BCH, EVEN_IN_PAD)], w_even_in, dz1,
                      late_tok, "even_dx")
    mine, theirs = _split_wait(swap_h, "swap_early_wait", grad_x)
    res = {}
    res["odd_w_in"] = _adamw([mine[0], theirs[0]], odd_w_in, m_odd_w_in, v_odd_w_in, "adamw_odd_w_in")
    res["odd_w_out"] = _adamw([mine[1], theirs[1]], odd_w_out, m_odd_w_out, v_odd_w_out, "adamw_odd_w_out")
    for nm, at, (w, m, v) in (("ffn_w_in", 2, (ffn_w_in, m_ffn_w_in, v_ffn_w_in)),
                              ("ffn_w_out", 4, (ffn_w_out, m_ffn_w_out, v_ffn_w_out))):
        upper = _adamw([mine[at + 1], theirs[at + 1]], w, m, v, f"adamw_{nm}_1", layer=1)
        res[nm] = _adamw([mine[at], theirs[at]], w, m, v, f"adamw_{nm}_0", layer=0, into=upper)
    mine_late = _scattered(late_h, "scatter_late_wait", res["ffn_w_out"][0])
    theirs_late = _exchange(mine_late, "swap2", "swap_late")
    rows = lambda a: jnp.swapaxes(a, 1, 2).reshape(1, -1, LANES)
    back = lambda a: jnp.swapaxes(a.reshape(1, EVEN_IN // N_CHIPS, d), 1, 2)
    res["even_w_in"] = [back(o) for o in _adamw([mine_late[0], theirs_late[0]], rows(even_w_in), rows(m_even_w_in),
                                                rows(v_even_w_in), "adamw_even_w_in")]
    res["even_w_out"] = _adamw([mine_late[1], theirs_late[1]], even_w_out, m_even_w_out, v_even_w_out,
                               "adamw_even_w_out")
    (packed,), (gathered,) = _split_wait(small_h, "gather_small_wait", theirs_late[0])
    gathered = lax.dynamic_update_index_in_dim(gathered, packed, 4 * lax.axis_index("x") + 2 * lax.axis_index("y")
                                               + lax.axis_index("c"), 0)

    base = n_rep + pad_rep
    own_rows = jnp.concatenate([
        lax.dynamic_slice_in_dim(gathered, base + 2 * chip, 2, axis=1),
        lax.dynamic_slice_in_dim(gathered, base + SUBLANES + 2 * chip, 2, axis=1),
        lax.dynamic_slice_in_dim(gathered, base + 2 * SUBLANES + CONV_K * chip, CONV_K, axis=1),
        jnp.zeros((N_DEV, 1, LANES), F32)], axis=1)
    small_parts = jnp.concatenate([gathered[:, :base], own_rows], axis=1)[:, None]

    def pack_small(get):
        rows = [_to_rows(get(tw)) for tw in rep_w] + [jnp.zeros((pad_rep, LANES), F32)]
        rows += [get(sh).reshape(-1, LANES) for sh in ((odd_v_ln_g, m_odd_v_ln_g, v_odd_v_ln_g),
                                                       (odd_v_ln_b, m_odd_v_ln_b, v_odd_v_ln_b),
                                                       (even_conv_w, m_even_conv_w, v_even_conv_w))]
        return jnp.concatenate(rows + [jnp.zeros((1, LANES), F32)], axis=0)[None]

    small_out = _adamw([small_parts], pack_small(lambda tw: tw[0]), pack_small(lambda tw: tw[1]),
                       pack_small(lambda tw: tw[2]), "adamw_small")

    def unpack_small(rows3):
        rows = rows3[0]
        out, off = {}, 0
        for nm, (w, _, _), r in zip(rep_names, rep_w, rep_rows):
            out[nm] = rows[off:off + r.shape[0]].reshape(-1)[:w.size].reshape(w.shape)
            off += r.shape[0]
        off += pad_rep
        out["odd_v_ln_g"] = rows[off:off + 2].reshape(odd_v_ln_g.shape)
        out["odd_v_ln_b"] = rows[off + 2:off + 4].reshape(odd_v_ln_b.shape)
        out["even_conv_w"] = rows[off + 4:off + 4 + CONV_K].reshape(even_conv_w.shape)
        return out

    small = [unpack_small(o) for o in small_out]
    order = ["even_w_in", "even_b_f", "even_conv_w", "even_w_out", "odd_w_in", "odd_v_ln_g", "odd_v_ln_b", "odd_w_s",
             "odd_b_s", "odd_w_out", "mix_ln_g", "mix_ln_b", "ffn_w_in", "ffn_w_out", "ffn_ln_g", "ffn_ln_b"]
    outs = [loss, grad_x[None]]
    for kind in range(4):
        for nm in order:
            outs.append(res[nm][kind] if nm in res else small[kind][nm])
    return tuple(outs)
```

```python
import functools
import math

import jax
import jax.numpy as jnp
from jax import lax
from jax.experimental import pallas as pl
from jax.experimental.pallas import tpu as pltpu

F32 = jnp.float32
BF16 = jnp.bfloat16

D_MODEL = 1024
FOX_HEADS = 8
HEAD_DIM = 64
HEAD_PAIRS = FOX_HEADS // 2
FOX_WIDTH = FOX_HEADS * HEAD_DIM
CONV_WIDTH = 512
CONV_K = 3
QKV = 3 * FOX_WIDTH
BCH = 3 * CONV_WIDTH
EVEN_IN = QKV + FOX_HEADS + BCH
EVEN_IN_PAD = QKV + BCH + 128
GMLP_BLOCK = 128
GMLP_GROUPS = 8
CHUNK = 64
FFN_HIDDEN = 2816
HALF_HIDDEN = FFN_HIDDEN // 2
ALPHA = 4.0 ** 0.25
LN_EPS = 1e-5
ADAM_LR = 0.001
ADAM_B1 = 0.9
ADAM_B2 = 0.999
ADAM_EPS = 1e-08
ADAM_WD = 0.01
ADAM_STEP = 10
N_CHIPS = 4
N_DEV = 8
LANES = 128
SUBLANES = 8
ROW_TILE = 512
FFN_FUSED_ROW_TILE = 256
REDUCE_TILE = 2048
ATT_BLOCK = 512
ATT_FWD_HEADS = 8
ATT_BWD_HEADS = 4
ADAMW_BLOCK_BYTES = 2 ** 20
VMEM_LIMIT = 56 * 2 ** 20
NEG = -1e30
MESH = pl.DeviceIdType.MESH
HIGHEST = lax.Precision.HIGHEST
Q_C, Q_ONE, Q_LSE = 64, 67, 70
K_ONE, K_C, K_ONE2 = 64, 67, 70
V_ONE = 64
DO_DELTA = 65
NT = (((1,), (1,)), ((), ()))
TN = (((0,), (0,)), ((), ()))


def _cp():
    return pltpu.CompilerParams(vmem_limit_bytes=VMEM_LIMIT)


def _resident(shape):
    zeros = (0,) * len(shape)
    return pl.BlockSpec(shape, lambda *_: zeros, pipeline_mode=pl.Buffered(1))


def _sds(shape, dtype):
    return jax.ShapeDtypeStruct(tuple(shape), dtype)


_MASKS = {
    "gather4": [(1, 0, 0), (0, 1, 0), (1, 1, 0)],
    "scatter4": [(1, 0, 0), (0, 1, 0), (1, 1, 0)],
    "swap2": [(0, 0, 1)],
    "gather8": [(0, 0, 1), (0, 1, 0), (0, 1, 1), (1, 0, 0), (1, 0, 1), (1, 1, 0), (1, 1, 1)],
}


def _exchange(arrs, mode, name):
    n = len(arrs)
    masks = _MASKS[mode]
    npeer = len(masks)
    lead = {"gather4": N_CHIPS, "gather8": N_DEV}.get(mode)
    out_shapes = [_sds(((lead,) if lead else ()) + a.shape, a.dtype) for a in arrs]

    def body(*refs):
        ins, outs = refs[:n], refs[n:2 * n]
        send_sems, recv_sems, loc_sems = refs[2 * n:]
        x, y, c = lax.axis_index("x"), lax.axis_index("y"), lax.axis_index("c")
        chip, dev = 2 * x + y, 4 * x + 2 * y + c
        sends, recvs, locs = [], [], []
        for k in range(n):
            if mode == "gather4":
                locs.append(pltpu.make_async_copy(ins[k], outs[k].at[chip], loc_sems.at[k]))
            elif mode == "scatter4":
                locs.append(pltpu.make_async_copy(ins[k].at[chip], outs[k].at[chip], loc_sems.at[k]))
            elif mode == "gather8":
                locs.append(pltpu.make_async_copy(ins[k], outs[k].at[dev], loc_sems.at[k]))
        for cp in locs:
            cp.start()
        for k in range(n):
            for j, (dx, dy, dc) in enumerate(masks):
                px = 1 - x if dx else x
                py = 1 - y if dy else y
                pc = 1 - c if dc else c
                pchip, pdev = 2 * px + py, 4 * px + 2 * py + pc
                if mode == "gather4":
                    src, dst, land = ins[k], outs[k].at[chip], outs[k].at[pchip]
                elif mode == "scatter4":
                    src, dst, land = ins[k].at[pchip], outs[k].at[chip], outs[k].at[pchip]
                elif mode == "swap2":
                    src, dst, land = ins[k], outs[k], outs[k]
                else:
                    src, dst, land = ins[k], outs[k].at[dev], outs[k].at[pdev]
                s = k * npeer + j
                kw = dict(send_sem=send_sems.at[s], recv_sem=recv_sems.at[s], device_id=(px, py, pc),
                          device_id_type=MESH)
                cp = pltpu.make_async_remote_copy(src_ref=src, dst_ref=dst, **kw)
                cp.start()
                sends.append(cp)
                recvs.append(pltpu.make_async_remote_copy(src_ref=src, dst_ref=land, **kw))
        for cp in recvs:
            cp.wait_recv()
        for cp in sends:
            cp.wait_send()
        for cp in locs:
            cp.wait()

    any_spec = pl.BlockSpec(memory_space=pl.ANY)
    outs = pl.pallas_call(
        body,
        out_shape=out_shapes,
        in_specs=[any_spec] * n,
        out_specs=[any_spec] * n,
        scratch_shapes=[pltpu.SemaphoreType.DMA((n * npeer,)), pltpu.SemaphoreType.DMA((n * npeer,)),
                        pltpu.SemaphoreType.DMA((max(n, 1),))],
        name=name,
    )(*arrs)
    return list(outs)


_HBM_SPEC = pl.BlockSpec(memory_space=pltpu.HBM)
_SEM_SPEC = pl.BlockSpec(memory_space=pltpu.SEMAPHORE)
_ANY_SPEC = pl.BlockSpec(memory_space=pl.ANY)
_EFFECT = pltpu.SideEffectType.DATAFLOW_SIDE_EFFECTING


def _split_copies(mode, ins, lands, send_sems, recv_sems):
    x, y, c = lax.axis_index("x"), lax.axis_index("y"), lax.axis_index("c")
    chip, dev = 2 * x + y, 4 * x + 2 * y + c
    masks = _MASKS[mode]
    out = []
    for k in range(len(ins)):
        for j, (dx, dy, dc) in enumerate(masks):
            px = 1 - x if dx else x
            py = 1 - y if dy else y
            pc = 1 - c if dc else c
            pchip, pdev = 2 * px + py, 4 * px + 2 * py + pc
            if mode == "gather4":
                src, dst, land = ins[k], lands[k].at[chip], lands[k].at[pchip]
            elif mode == "scatter4":
                src, dst, land = ins[k].at[pchip], lands[k].at[chip], lands[k].at[pchip]
            elif mode == "swap2":
                src, dst, land = ins[k], lands[k], lands[k]
            else:
                src, dst, land = ins[k], lands[k].at[dev], lands[k].at[pdev]
            s = k * len(masks) + j
            kw = dict(send_sem=send_sems.at[s], recv_sem=recv_sems.at[s], device_id=(px, py, pc), device_id_type=MESH)
            out.append((pltpu.make_async_remote_copy(src_ref=src, dst_ref=dst, **kw),
                        pltpu.make_async_remote_copy(src_ref=src, dst_ref=land, **kw)))
    return out


def _split_start(arrs, mode, name, after=None):
    n = len(arrs)
    nsem = n * len(_MASKS[mode])
    lead = {"gather4": (N_CHIPS,), "gather8": (N_DEV,)}.get(mode, ())
    land_shapes = [lead + a.shape for a in arrs]

    def body(*refs):
        ins, lands = refs[:n], refs[n:2 * n]
        outs = refs[2 * n + (after is not None):]
        for start, _ in _split_copies(mode, ins, lands, outs[0], outs[1]):
            start.start()
        outs[-1][...] = jnp.zeros(outs[-1].shape, F32)

    srcs = [pltpu.with_memory_space_constraint(a, pltpu.HBM) for a in arrs]
    empties = [pltpu.with_memory_space_constraint(lax.empty(s, a.dtype), pltpu.HBM) for s, a in zip(land_shapes, arrs)]
    res = pl.pallas_call(
        body, name=name,
        out_shape=(pltpu.SemaphoreType.DMA((nsem,)), pltpu.SemaphoreType.DMA((nsem,)),
                   *[pltpu.HBM(a.shape, a.dtype) for a in arrs],
                   *[pltpu.HBM(s, a.dtype) for s, a in zip(land_shapes, arrs)],
                   _sds((SUBLANES, LANES), F32)),
        in_specs=[_HBM_SPEC] * (2 * n) + ([_ANY_SPEC] if after is not None else []),
        out_specs=(_SEM_SPEC, _SEM_SPEC, *[_HBM_SPEC] * (2 * n), pl.BlockSpec(memory_space=pltpu.VMEM)),
        input_output_aliases={k: 2 + k for k in range(2 * n)},
        compiler_params=pltpu.CompilerParams(has_side_effects=_EFFECT),
    )(*srcs, *empties, *([after] if after is not None else []))
    return dict(mode=mode, n=n, sems=res[:2], bufs=res[2:2 + 2 * n]), res[-1]


def _split_wait(handle, name, after):
    n, mode = handle["n"], handle["mode"]

    def body(*refs):
        ins, lands = refs[:n], refs[n:2 * n]
        send_sems, recv_sems = refs[2 * n], refs[2 * n + 1]
        for _, arrival in _split_copies(mode, ins, lands, send_sems, recv_sems):
            arrival.wait_send()
            arrival.wait_recv()

    bufs = handle["bufs"]
    res = pl.pallas_call(
        body, name=name,
        out_shape=tuple(pltpu.HBM(b.shape, b.dtype) for b in bufs),
        in_specs=[_HBM_SPEC] * (2 * n) + [_SEM_SPEC, _SEM_SPEC, _ANY_SPEC],
        out_specs=tuple([_HBM_SPEC] * (2 * n)),
        input_output_aliases={k: k for k in range(2 * n)},
        compiler_params=pltpu.CompilerParams(has_side_effects=_EFFECT),
    )(*bufs, *handle["sems"], after)
    return list(res[:n]), list(res[n:])


def _with_own(landed, own):
    chip = 2 * lax.axis_index("x") + lax.axis_index("y")
    return lax.dynamic_update_index_in_dim(landed, own, chip, 0)


def _gathered(handle, name, after):
    sent, landed = _split_wait(handle, name, after)
    return [_with_own(g, own) for g, own in zip(landed, sent)]


def _scattered(handle, name, after):
    chip = 2 * lax.axis_index("x") + lax.axis_index("y")
    sent, landed = _split_wait(handle, name, after)
    return [_with_own(r, lax.dynamic_index_in_dim(g, chip, 0, keepdims=False)) for r, g in zip(landed, sent)]


def _sigmoid(x):
    return 0.5 * jnp.tanh(0.5 * x) + 0.5


def _log_sigmoid(x):
    e = jnp.exp(-jnp.abs(x))
    log1p = jnp.where(e < 1e-2, e * (1.0 - e * (0.5 - e * (1.0 / 3.0))), jnp.log(1.0 + e))
    return jnp.minimum(x, 0.0) - log1p


def _ln_fwd(z):
    mu = jnp.mean(z, axis=-1, keepdims=True)
    zc = z - mu
    var = jnp.mean(zc * zc, axis=-1, keepdims=True)
    rstd = lax.rsqrt(var + LN_EPS)
    return zc * rstd, rstd


def _ln_bwd(dy, xhat, rstd, g):
    dxh = dy * g
    m1 = jnp.mean(dxh, axis=-1, keepdims=True)
    m2 = jnp.mean(dxh * xhat, axis=-1, keepdims=True)
    dz = rstd * (dxh - m1 - xhat * m2)
    return dz, jnp.sum(dy * xhat, axis=0, keepdims=True), jnp.sum(dy, axis=0, keepdims=True)


def _shift_down(z, halo):
    r = lax.broadcasted_iota(jnp.int32, z.shape, 0)
    z1 = jnp.where(r == 0, halo[7:8, :], pltpu.roll(z, 1, 0))
    z2 = jnp.where(r == 0, halo[6:7, :], jnp.where(r == 1, halo[7:8, :], pltpu.roll(z, 2, 0)))
    return z1, z2


def _shift_up(z, halo):
    n = z.shape[0]
    r = lax.broadcasted_iota(jnp.int32, z.shape, 0)
    z1 = jnp.where(r == n - 1, halo[0:1, :], pltpu.roll(z, n - 1, 0))
    z2 = jnp.where(r == n - 1, halo[1:2, :], jnp.where(r == n - 2, halo[0:1, :], pltpu.roll(z, n - 2, 0)))
    return z1, z2


def _accumulate(ref, first, value):
    @pl.when(first)
    def _():
        ref[...] = value

    @pl.when(jnp.logical_not(first))
    def _():
        ref[...] += value


def _proj(x, wt, splits, name):
    t, k = x.shape
    tm = min(ROW_TILE, t)
    w = wt

    def body(x_ref, w_ref, *outs):
        a = x_ref[...].astype(BF16)
        for (lo, hi, dt), o in zip(splits, outs):
            o[...] = lax.dot_general(a, w_ref[lo:hi, :], NT, preferred_element_type=F32).astype(dt)

    return pl.pallas_call(
        body, grid=(t // tm,),
        in_specs=[pl.BlockSpec((tm, k), lambda i: (i, 0)), _resident(w.shape)],
        out_specs=[pl.BlockSpec((tm, hi - lo), lambda i: (i, 0)) for lo, hi, _ in splits],
        out_shape=[_sds((t, hi - lo), dt) for lo, hi, dt in splits],
        compiler_params=_cp(), name=name)(x, w)


def _fgate_fwd(fl3, b_f):
    nc = fl3.shape[0]

    def body(f_ref, b_ref, c_ref):
        r = lax.broadcasted_iota(jnp.int32, (LANES, LANES), 0)
        cidx = lax.broadcasted_iota(jnp.int32, (LANES, LANES), 1)
        upper = (r <= cidx).astype(F32)

        def step(i, carry):
            lf = _log_sigmoid(f_ref[i] + b_ref[...])
            cc = jnp.dot(lf, upper, precision=HIGHEST, preferred_element_type=F32) + carry
            c_ref[i] = cc
            return cc[:, LANES - 1:LANES]

        lax.fori_loop(0, nc, step, jnp.zeros((FOX_HEADS, 1), F32))

    return pl.pallas_call(body, out_shape=_sds(fl3.shape, F32), name="fgate_fwd")(fl3, b_f)


def _fgate_bwd(dc3, fl3, b_f):
    nc = fl3.shape[0]

    def body(dc_ref, f_ref, b_ref, df_ref, db_ref):
        r = lax.broadcasted_iota(jnp.int32, (LANES, LANES), 0)
        cidx = lax.broadcasted_iota(jnp.int32, (LANES, LANES), 1)
        lower = (r >= cidx).astype(F32)

        def step(n, carry):
            suffix, db = carry
            i = nc - 1 - n
            dlf = jnp.dot(dc_ref[i], lower, precision=HIGHEST, preferred_element_type=F32) + suffix
            df = dlf * (1.0 - _sigmoid(f_ref[i] + b_ref[...]))
            df_ref[i] = df
            return dlf[:, 0:1], db + jnp.sum(df, axis=1, keepdims=True)

        zero = jnp.zeros((FOX_HEADS, 1), F32)
        _, db = lax.fori_loop(0, nc, step, (zero, zero))
        db_ref[...] = db

    return pl.pallas_call(body, out_shape=[_sds(fl3.shape, F32), _sds((FOX_HEADS, 1), F32)],
                          name="fgate_bwd")(dc3, fl3, b_f)


def _split3(c):
    hi = c.astype(BF16).astype(F32)
    mid = (c - hi).astype(BF16).astype(F32)
    lo = (c - hi - mid).astype(BF16).astype(F32)
    return hi, mid, lo


PIECE_ONE = 3 * FOX_HEADS


def _piece_rows(values):
    hi, mid, lo = _split3(values)
    lane = lax.broadcasted_iota(jnp.int32, values.shape, 1)
    row = hi + pltpu.roll(mid, FOX_HEADS, 1) + pltpu.roll(lo, 2 * FOX_HEADS, 1) + jnp.where(lane == PIECE_ONE, 1.0, 0.0)
    return row.astype(BF16)


def _piece_selector(start, sign, ones=()):
    sel = [[0.0] * FOX_WIDTH for _ in range(LANES)]
    for h in range(FOX_HEADS):
        for n in range(3):
            sel[n * FOX_HEADS + h][h * HEAD_DIM + start - HEAD_DIM + n] = sign
        for lane in ones:
            sel[PIECE_ONE][h * HEAD_DIM + lane - HEAD_DIM] = 1.0
    return jnp.asarray(sel, BF16)


def _attn_pack(qkv, c_pad):
    t = qkv.shape[0]
    tm = min(ROW_TILE, t)
    hd = HEAD_DIM
    sel_q = _piece_selector(Q_C, 1.0, range(Q_ONE, Q_ONE + 3))
    sel_k = _piece_selector(K_C, -1.0, [*range(K_ONE, K_ONE + 3), *range(K_ONE2, K_ONE2 + 3)])
    sel_v = _piece_selector(HEAD_DIM, 0.0, range(V_ONE, V_ONE + 4))

    def body(x_ref, c_ref, sq_ref, sk_ref, sv_ref, qp_ref, kp_ref, vp_ref, kt_ref, vt_ref):
        pieces = _piece_rows(c_ref[...])
        q_extra = jnp.dot(pieces, sq_ref[...], preferred_element_type=F32).astype(BF16)
        k_extra = jnp.dot(pieces, sk_ref[...], preferred_element_type=F32).astype(BF16)
        v_extra = jnp.dot(pieces, sv_ref[...], preferred_element_type=F32).astype(BF16)
        for h in range(FOX_HEADS):
            hs = slice(h * hd, (h + 1) * hd)
            qp_ref[h, :, :hd] = (x_ref[:, hs].astype(F32) * (hd ** -0.5)).astype(BF16)
            qp_ref[h, :, hd:] = q_extra[:, hs]
            kp_ref[h, :, :hd] = x_ref[:, FOX_WIDTH + h * hd:FOX_WIDTH + (h + 1) * hd]
            kp_ref[h, :, hd:] = k_extra[:, hs]
            vp_ref[h, :, :hd] = x_ref[:, 2 * FOX_WIDTH + h * hd:2 * FOX_WIDTH + (h + 1) * hd]
            vp_ref[h, :, hd:] = v_extra[:, hs]
            kt_ref[h] = kp_ref[h].T
            vt_ref[h] = vp_ref[h].T

    row3 = pl.BlockSpec((FOX_HEADS, tm, LANES), lambda i: (0, i, 0))
    col3 = pl.BlockSpec((FOX_HEADS, LANES, tm), lambda i: (0, 0, i))
    sel = _resident(sel_q.shape)
    return pl.pallas_call(
        body, grid=(t // tm,),
        in_specs=[pl.BlockSpec((tm, QKV), lambda i: (i, 0)), pl.BlockSpec((tm, LANES), lambda i: (i, 0)), sel, sel, sel],
        out_specs=[row3, row3, row3, col3, col3],
        out_shape=[_sds((FOX_HEADS, t, LANES), BF16)] * 3 + [_sds((FOX_HEADS, LANES, t), BF16)] * 2,
        compiler_params=_cp(), name="attn_pack")(qkv, c_pad, sel_q, sel_k, sel_v)


def _triangle(nq, key_major):
    if key_major:
        pairs = [(i, j) for j in range(nq) for i in range(j, nq)]
    else:
        pairs = [(i, j) for i in range(nq) for j in range(i + 1)]
    return jnp.asarray([p[0] for p in pairs], jnp.int32), jnp.asarray([p[1] for p in pairs], jnp.int32)


def _attn_fwd(qp, kp, vt):
    t = qp.shape[1]
    bq = min(ATT_BLOCK, t)
    nq = t // bq
    nh = ATT_FWD_HEADS
    i_tab, j_tab = _triangle(nq, key_major=False)

    def body(it_ref, jt_ref, q_ref, k_ref, vt_ref, o_ref, lse_ref, m_sc, acc_sc):
        s = pl.program_id(1)
        i, j = it_ref[s], jt_ref[s]

        @pl.when(j == 0)
        def _():
            m_sc[...] = jnp.full(m_sc.shape, NEG, F32)
            acc_sc[...] = jnp.zeros(acc_sc.shape, F32)

        def sweep(masked):
            scores = lambda h: lax.dot_general(k_ref[h], q_ref[h], NT, preferred_element_type=F32)

            def accumulate(h, pt, rescale):
                acc_sc[h] = rescale * acc_sc[h] + jnp.dot(vt_ref[h], pt, preferred_element_type=F32)

            ahead, behind = scores(0), None
            for h in range(nh):
                st = ahead
                if h + 1 < nh:
                    ahead = scores(h + 1)
                if behind is not None:
                    accumulate(*behind)
                if masked:
                    key = lax.broadcasted_iota(jnp.int32, (bq, bq), 0)
                    qry = lax.broadcasted_iota(jnp.int32, (bq, bq), 1)
                    st = jnp.where(key <= qry, st, NEG)
                m_prev = m_sc[h]
                m_new = jnp.maximum(m_prev, jnp.max(st, axis=0, keepdims=True))
                behind = (h, jnp.exp(st - m_new).astype(BF16), jnp.exp(m_prev - m_new))
                m_sc[h] = m_new
            accumulate(*behind)

        @pl.when(j < i)
        def _():
            sweep(False)

        @pl.when(j == i)
        def _():
            sweep(True)
            for h in range(nh):
                acc = acc_sc[h]
                denom = acc[V_ONE:V_ONE + 1, :]
                o_ref[:, h * HEAD_DIM:(h + 1) * HEAD_DIM] = (acc[:HEAD_DIM, :] / denom).T.astype(BF16)
                lse_ref[h] = m_sc[h] + jnp.log(denom)

    grid_spec = pltpu.PrefetchScalarGridSpec(
        num_scalar_prefetch=2, grid=(FOX_HEADS // nh, i_tab.shape[0]),
        in_specs=[pl.BlockSpec((nh, bq, LANES), lambda hp, s, it, jt: (hp, it[s], 0)),
                  pl.BlockSpec((nh, bq, LANES), lambda hp, s, it, jt: (hp, jt[s], 0)),
                  pl.BlockSpec((nh, LANES, bq), lambda hp, s, it, jt: (hp, 0, jt[s]))],
        out_specs=[pl.BlockSpec((bq, nh * HEAD_DIM), lambda hp, s, it, jt: (it[s], hp)),
                   pl.BlockSpec((nh, 1, bq), lambda hp, s, it, jt: (hp, 0, it[s]))],
        scratch_shapes=[pltpu.VMEM((nh, 1, bq), F32), pltpu.VMEM((nh, LANES, bq), F32)])
    return pl.pallas_call(body, grid_spec=grid_spec,
                          out_shape=[_sds((t, FOX_WIDTH), BF16), _sds((FOX_HEADS, 1, t), F32)],
                          compiler_params=_cp(), name="attn_fwd")(i_tab, j_tab, qp, kp, vt)


def _even_out(attn, bch, conv_w, w_out, x, g, b):
    t, d = x.shape
    tm = min(ROW_TILE, t)
    halo_blocks = tm // SUBLANES
    cw = CONV_WIDTH

    def body(a_ref, cur_ref, prev_ref, cw_ref, wo_ref, x_ref, g_ref, b_ref, conv_ref, yb_ref, xh_ref, rs_ref):
        i = pl.program_id(0)
        z = cur_ref[:, cw:2 * cw] * cur_ref[:, 2 * cw:]
        zp = jnp.where(i == 0, 0.0, prev_ref[:, cw:2 * cw] * prev_ref[:, 2 * cw:])
        z1, z2 = _shift_down(z, zp)
        conv = (cur_ref[:, :cw] * (cw_ref[0:1, :] * z2 + cw_ref[1:2, :] * z1 + cw_ref[2:3, :] * z)).astype(BF16)
        conv_ref[...] = conv
        pre = (ALPHA * x_ref[...] + jnp.dot(a_ref[...], wo_ref[:FOX_WIDTH, :], preferred_element_type=F32)
               + jnp.dot(conv, wo_ref[FOX_WIDTH:, :], preferred_element_type=F32))
        xhat, rstd = _ln_fwd(pre)
        yb_ref[...] = (xhat * g_ref[...] + b_ref[...]).astype(BF16)
        xh_ref[...] = xhat
        rs_ref[...] = rstd

    row = lambda i: (i, 0)
    full, half = pl.BlockSpec((tm, d), row), pl.BlockSpec((tm, cw), row)
    return pl.pallas_call(
        body, grid=(t // tm,),
        in_specs=[half, pl.BlockSpec((tm, BCH), row),
                  pl.BlockSpec((SUBLANES, BCH), lambda i: (jnp.maximum(i * halo_blocks - 1, 0), 0)),
                  _resident(conv_w.shape), _resident(w_out.shape), full, _resident(g.shape), _resident(b.shape)],
        out_specs=[half, full, full, pl.BlockSpec((tm, 1), row)],
        out_shape=[_sds((t, cw), BF16), _sds((t, d), BF16), _sds((t, d), F32), _sds((t, 1), F32)],
        compiler_params=_cp(), name="even_out")(attn, bch, bch, conv_w, w_out, x, g, b)


def _gmlp_fwd(x, w_in, vg, vb, wm, bs_col, w_out, res_ln, g, b):
    t, d = x.shape
    tm = min(ROW_TILE, t)
    gb = GMLP_BLOCK
    rxh, rg, rb = res_ln

    def body(x_ref, w_ref, vg_ref, vb_ref, wm_ref, bs_ref, wo_ref, rxh_ref, rg_ref, rb_ref, g_ref, b_ref,
             sv_ref, rs_ref, o_ref, yb_ref, xh_ref, rsy_ref, a_sc):
        xb = x_ref[...].astype(BF16)
        nc = w_ref.shape[2]
        for j in range(w_ref.shape[0]):
            a_sc[:, j * nc:(j + 1) * nc] = jnp.dot(xb, w_ref[j], preferred_element_type=F32)
        halves = []
        for half in range(2):
            a = a_sc[:, half * d:(half + 1) * d]
            cdf = 0.5 * (1.0 + lax.erf(a * (2.0 ** -0.5)))
            halves.append(a * cdf)
            slope = cdf + a * (jnp.exp(-0.5 * a * a) * (1.0 / math.sqrt(2.0 * math.pi)))
            sv_ref[:, (2 * half + 1) * d:(2 * half + 2) * d] = slope.astype(BF16)
        u = halves[0]
        vhat, rstd = _ln_fwd(halves[1])
        sv_ref[:, :d] = u.astype(BF16)
        sv_ref[:, 2 * d:3 * d] = vhat.astype(BF16)
        rs_ref[...] = rstd
        vln = (vhat * vg_ref[...] + vb_ref[...]).astype(BF16)
        for blk in range(tm // gb):
            rs = slice(blk * gb, (blk + 1) * gb)
            for gi in range(GMLP_GROUPS):
                cs = slice(gi * gb, (gi + 1) * gb)
                s = jnp.dot(wm_ref[gi], vln[rs, cs], preferred_element_type=F32) + bs_ref[:, gi:gi + 1]
                o_ref[rs, cs] = (u[rs, cs] * s).astype(BF16)
        z = ALPHA * (rxh_ref[...] * rg_ref[...] + rb_ref[...]) + jnp.dot(o_ref[...], wo_ref[...], preferred_element_type=F32)
        xhat, rstd_y = _ln_fwd(z)
        yb_ref[...] = (xhat * g_ref[...] + b_ref[...]).astype(BF16)
        xh_ref[...] = xhat
        rsy_ref[...] = rstd_y

    row = lambda i: (i, 0)
    full, col, vec = pl.BlockSpec((tm, d), row), pl.BlockSpec((tm, 1), row), _resident(g.shape)
    return pl.pallas_call(
        body, grid=(t // tm,),
        in_specs=[full, _resident(w_in.shape), _resident(vg.shape), _resident(vb.shape),
                  _resident(wm.shape), _resident(bs_col.shape), _resident(w_out.shape), full, vec, vec, vec, vec],
        out_specs=[pl.BlockSpec((tm, 4 * d), row), col, full, full, full, col],
        out_shape=[_sds((t, 4 * d), BF16), _sds((t, 1), F32), _sds((t, d), BF16), _sds((t, d), BF16), _sds((t, d), F32),
                   _sds((t, 1), F32)],
        scratch_shapes=[pltpu.VMEM((tm, 2 * d), F32)],
        compiler_params=_cp(), name="gmlp_fwd")(x, w_in, vg, vb, wm, bs_col, w_out, rxh, rg, rb, g, b)


def _mm_back(pairs, wt, res, after, name):
    t = pairs[0][0].shape[0]
    k = wt.shape[1]
    tm = min(ROW_TILE, t)
    n = len(pairs)

    def body(after_ref, *refs):
        a_refs, w_ref, res_ref, o_ref = refs[:n], refs[n], refs[n + 1], refs[n + 2]
        dx = ALPHA * res_ref[...]
        for a_ref, (_, lo, hi) in zip(a_refs, pairs):
            dx = dx + jnp.dot(a_ref[...].astype(BF16), w_ref[lo:hi, :], preferred_element_type=F32)
        o_ref[...] = dx

    row = lambda i: (i, 0)
    return pl.pallas_call(
        body, grid=(t // tm,),
        in_specs=[_ANY_SPEC] + [pl.BlockSpec((tm, a.shape[1]), row) for a, _, _ in pairs]
        + [_resident(wt.shape), pl.BlockSpec((tm, k), row)],
        out_specs=pl.BlockSpec((tm, k), row), out_shape=_sds((t, k), F32),
        compiler_params=_cp(), name=name)(after, *[a for a, _, _ in pairs], wt, res)


def _mm_tn(a, b, name, *, tn, tk=None, tt=None, stack_cols=False, out_dtype=BF16, after=None):
    t, k = a.shape
    n = b.shape[1]
    tk = k if tk is None else tk
    tt = min(REDUCE_TILE if tt is None else tt, t)
    nt = t // tt

    def body(a_ref, b_ref, *rest):
        o_ref, acc_ref = rest[after is not None:]
        s = pl.program_id(2)
        part = lax.dot_general(a_ref[...].astype(BF16), b_ref[...].astype(BF16), TN, preferred_element_type=F32)
        _accumulate(acc_ref, s == 0, part)

        @pl.when(s == nt - 1)
        def _():
            o_ref[...] = acc_ref[...].astype(out_dtype).reshape(o_ref.shape)

    if stack_cols:
        assert tk == k
        out_spec = pl.BlockSpec((1, k, tn), lambda kk, j, s: (j, 0, 0))
        out_shape = _sds((n // tn, k, tn), out_dtype)
    else:
        out_spec = pl.BlockSpec((tk, tn), lambda kk, j, s: (kk, j))
        out_shape = _sds((k, n), out_dtype)
    return pl.pallas_call(
        body, grid=(k // tk, n // tn, nt),
        in_specs=[pl.BlockSpec((tt, tk), lambda kk, j, s: (s, kk)), pl.BlockSpec((tt, tn), lambda kk, j, s: (s, j))]
        + ([_ANY_SPEC] if after is not None else []),
        out_specs=out_spec, out_shape=out_shape,
        scratch_shapes=[pltpu.VMEM((tk, tn), F32)],
        compiler_params=_cp(), name=name)(a, b, *([after] if after is not None else []))


def _ffn_bwd_rows(dz, wo, gu, wi, ln_below, name):
    t, d = dz.shape
    tm = min(FFN_FUSED_ROW_TILE, t)
    hh = HALF_HIDDEN
    xhat, rstd, g = ln_below

    def body(dz_ref, wo_ref, gu_ref, wi_ref, xh_ref, rs_ref, g_ref, dgu_ref, dzb_ref, dg_ref, db_ref):
        first = pl.program_id(0) == 0
        a = dz_ref[...].astype(BF16)
        for c in range(2):
            gs, us = slice(c * hh, (c + 1) * hh), slice(FFN_HIDDEN + c * hh, FFN_HIDDEN + (c + 1) * hh)
            dh = lax.dot_general(a, wo_ref[gs, :], NT, preferred_element_type=F32)
            dgu_ref[:, gs] = (dh * gu_ref[:, gs].astype(F32)).astype(BF16)
            dgu_ref[:, us] = (dh * gu_ref[:, us].astype(F32)).astype(BF16)
        dx = ALPHA * dz_ref[...]
        for j in range(wi_ref.shape[0]):
            dx = dx + lax.dot_general(dgu_ref[:, j * hh:(j + 1) * hh], wi_ref[j], NT, preferred_element_type=F32)
        dzb, dg, db = _ln_bwd(dx, xh_ref[...], rs_ref[...], g_ref[...])
        dzb_ref[...] = dzb
        _accumulate(dg_ref, first, dg)
        _accumulate(db_ref, first, db)

    row = lambda i: (i, 0)
    wide, full = pl.BlockSpec((tm, 2 * FFN_HIDDEN), row), pl.BlockSpec((tm, d), row)
    vec = pl.BlockSpec((1, d), lambda i: (0, 0))
    return pl.pallas_call(
        body, grid=(t // tm,),
        in_specs=[full, _resident(wo.shape), wide, _resident(wi.shape), full, pl.BlockSpec((tm, 1), row),
                  _resident(g.shape)],
        out_specs=[wide, full, vec, vec],
        out_shape=[_sds((t, 2 * FFN_HIDDEN), BF16), _sds((t, d), F32), _sds((1, d), F32), _sds((1, d), F32)],
        compiler_params=_cp(), name=name)(dz, wo, gu, wi, xhat, rstd, g)


def _gmlp_bwd(dz, w_out, saved, rstd_v, vg, vb, wm, bs_col, w_in, ln_below):
    t, d = dz.shape
    d2 = 2 * d
    tm = min(ROW_TILE, t)
    gb = GMLP_BLOCK
    xhat_below, rstd_below, g_below = ln_below

    def body(dz_ref, wo_ref, sv_ref, rs_ref, vg_ref, vb_ref, wm_ref, bs_ref, wi_ref, xh_ref, rsb_ref, gb_ref,
             da_ref, dws_ref, dbs_ref, dvg_ref, dvb_ref, dzb_ref, dg_ref, db_ref, dvln_sc):
        first = pl.program_id(0) == 0
        u = sv_ref[:, :d].astype(F32)
        vhat = sv_ref[:, 2 * d:3 * d].astype(F32)
        rstd = rs_ref[...]
        vln = (vhat * vg_ref[...] + vb_ref[...]).astype(BF16)
        dgate = lax.dot_general(dz_ref[...].astype(BF16), wo_ref[...], NT, preferred_element_type=F32)

        @pl.when(first)
        def _():
            dws_ref[...] = jnp.zeros(dws_ref.shape, F32)
            dbs_ref[...] = jnp.zeros(dbs_ref.shape, F32)

        for blk in range(tm // gb):
            rs = slice(blk * gb, (blk + 1) * gb)
            for gi in range(GMLP_GROUPS):
                cs = slice(gi * gb, (gi + 1) * gb)
                vblk = vln[rs, cs]
                s = jnp.dot(wm_ref[gi], vblk, preferred_element_type=F32) + bs_ref[:, gi:gi + 1]
                dgb = dgate[rs, cs]
                da_ref[rs, cs] = (dgb * s * sv_ref[rs, d + gi * gb:d + (gi + 1) * gb].astype(F32)).astype(BF16)
                ds = dgb * u[rs, cs]
                dsb = ds.astype(BF16)
                dws_ref[gi] += lax.dot_general(dsb, vblk, NT, preferred_element_type=F32)
                dbs_ref[:, gi:gi + 1] += jnp.sum(ds, axis=1, keepdims=True)
                dvln_sc[rs, cs] = lax.dot_general(wm_ref[gi], dsb, TN, preferred_element_type=F32)
        dv, dvg, dvb = _ln_bwd(dvln_sc[...], vhat, rstd, vg_ref[...])
        da_ref[:, d:] = (dv * sv_ref[:, 3 * d:].astype(F32)).astype(BF16)
        _accumulate(dvg_ref, first, dvg)
        _accumulate(dvb_ref, first, dvb)
        dx = ALPHA * dz_ref[...]
        nc = wi_ref.shape[2]
        for j in range(wi_ref.shape[0]):
            dx = dx + lax.dot_general(da_ref[:, j * nc:(j + 1) * nc], wi_ref[j], NT, preferred_element_type=F32)
        dzb, dg, db = _ln_bwd(dx, xh_ref[...], rsb_ref[...], gb_ref[...])
        dzb_ref[...] = dzb
        _accumulate(dg_ref, first, dg)
        _accumulate(db_ref, first, db)

    row = lambda i: (i, 0)
    full, col = pl.BlockSpec((tm, d), row), pl.BlockSpec((tm, 1), row)
    vec = pl.BlockSpec((1, d), lambda i: (0, 0))
    return pl.pallas_call(
        body, grid=(t // tm,),
        in_specs=[full, _resident(w_out.shape), pl.BlockSpec((tm, 4 * d), row), col,
                  _resident(vg.shape), _resident(vb.shape), _resident(wm.shape), _resident(bs_col.shape),
                  _resident(w_in.shape), full, col, _resident(g_below.shape)],
        out_specs=[pl.BlockSpec((tm, d2), row), pl.BlockSpec(wm.shape, lambda i: (0, 0, 0)),
                   pl.BlockSpec(bs_col.shape, lambda i: (0, 0)), vec, vec, full, vec, vec],
        out_shape=[_sds((t, d2), BF16), _sds(wm.shape, F32), _sds(bs_col.shape, F32), _sds((1, d), F32), _sds((1, d), F32),
                   _sds((t, d), F32), _sds((1, d), F32), _sds((1, d), F32)],
        scratch_shapes=[pltpu.VMEM((tm, d), F32)],
        compiler_params=_cp(), name="gmlp_bwd")(dz, w_out, saved, rstd_v, vg, vb, wm, bs_col, w_in, xhat_below,
                                                rstd_below, g_below)


def _conv_bwd(bch, dconv, conv_w):
    t = bch.shape[0]
    tm = min(ROW_TILE, t)
    nb = t // tm
    halo_blocks = tm // SUBLANES
    cw = CONV_WIDTH

    def body(cur_ref, prev_ref, next_ref, dc_ref, dn_ref, w_ref, o_ref, dw_ref):
        i = pl.program_id(0)
        bgate, cgate, hval = cur_ref[:, :cw], cur_ref[:, cw:2 * cw], cur_ref[:, 2 * cw:]
        z = cgate * hval
        zp = jnp.where(i == 0, 0.0, prev_ref[:, cw:2 * cw] * prev_ref[:, 2 * cw:])
        z1, z2 = _shift_down(z, zp)
        w0, w1, w2 = w_ref[0:1, :], w_ref[1:2, :], w_ref[2:3, :]
        dconv = dc_ref[...]
        o_ref[:, :cw] = (dconv * (w0 * z2 + w1 * z1 + w2 * z)).astype(BF16)
        dy = dconv * bgate
        dyn = jnp.where(i == nb - 1, 0.0, dn_ref[...] * next_ref[:, :cw])
        dy1, dy2 = _shift_up(dy, dyn)
        dz = w2 * dy + w1 * dy1 + w0 * dy2
        o_ref[:, cw:2 * cw] = (dz * hval).astype(BF16)
        o_ref[:, 2 * cw:] = (dz * cgate).astype(BF16)

        @pl.when(i == 0)
        def _():
            dw_ref[...] = jnp.zeros(dw_ref.shape, F32)

        for tap, zs in enumerate((z2, z1, z)):
            dw_ref[tap:tap + 1, :] += jnp.sum(dy * zs, axis=0, keepdims=True)

    last_halo = t // SUBLANES - 1
    return pl.pallas_call(
        body, grid=(nb,),
        in_specs=[pl.BlockSpec((tm, BCH), lambda i: (i, 0)),
                  pl.BlockSpec((SUBLANES, BCH), lambda i: (jnp.maximum(i * halo_blocks - 1, 0), 0)),
                  pl.BlockSpec((SUBLANES, BCH), lambda i: (jnp.minimum((i + 1) * halo_blocks, last_halo), 0)),
                  pl.BlockSpec((tm, cw), lambda i: (i, 0)),
                  pl.BlockSpec((SUBLANES, cw), lambda i: (jnp.minimum((i + 1) * halo_blocks, last_halo), 0)),
                  _resident(conv_w.shape)],
        out_specs=[pl.BlockSpec((tm, BCH), lambda i: (i, 0)), pl.BlockSpec((SUBLANES, cw), lambda i: (0, 0))],
        out_shape=[_sds((t, BCH), BF16), _sds((SUBLANES, cw), F32)],
        compiler_params=_cp(), name="conv_bwd")(bch, bch, bch, dconv, dconv, conv_w)


def _attn_bwd_prep(dz, w_out, o, qp, lse_pad, after):
    t = o.shape[0]
    tm = min(ROW_TILE, t)
    hd = HEAD_DIM
    sel_lse = _piece_selector(Q_LSE, -1.0)
    sel_delta = _piece_selector(DO_DELTA, -1.0)
    head_of = jnp.asarray([[1.0 if col == row // hd else 0.0 for col in range(LANES)] for row in range(FOX_WIDTH)], F32)

    def body(after_ref, dz_ref, wo_ref, o_ref, qp_ref, lse_ref, sl_ref, sd_ref, seg_ref, qb_ref, dob_ref, dconv_ref):
        dzb = dz_ref[...].astype(BF16)
        do = lax.dot_general(dzb, wo_ref[:FOX_WIDTH, :], NT, preferred_element_type=F32)
        dconv_ref[...] = lax.dot_general(dzb, wo_ref[FOX_WIDTH:, :], NT, preferred_element_type=F32)
        delta = jnp.dot(o_ref[...].astype(F32) * do, seg_ref[...], precision=HIGHEST, preferred_element_type=F32)
        lse_extra = jnp.dot(_piece_rows(lse_ref[...]), sl_ref[...], preferred_element_type=F32)
        do_extra = jnp.dot(_piece_rows(delta), sd_ref[...], preferred_element_type=F32).astype(BF16)
        for h in range(FOX_HEADS):
            hs = slice(h * hd, (h + 1) * hd)
            dob_ref[h, :, :hd] = do[:, hs].astype(BF16)
            dob_ref[h, :, hd:] = do_extra[:, hs]
            qb_ref[h, :, :hd] = qp_ref[h, :, :hd]
            qb_ref[h, :, hd:] = (qp_ref[h, :, hd:].astype(F32) + lse_extra[:, hs]).astype(BF16)

    row = lambda i: (i, 0)
    row3 = pl.BlockSpec((FOX_HEADS, tm, LANES), lambda i: (0, i, 0))
    half = pl.BlockSpec((tm, FOX_WIDTH), row)
    return pl.pallas_call(
        body, grid=(t // tm,),
        in_specs=[_ANY_SPEC, pl.BlockSpec((tm, dz.shape[1]), row), _resident(w_out.shape), half, row3,
                  pl.BlockSpec((tm, LANES), row), _resident(sel_lse.shape), _resident(sel_delta.shape),
                  _resident(head_of.shape)],
        out_specs=[row3, row3, half],
        out_shape=[_sds((FOX_HEADS, t, LANES), BF16)] * 2 + [_sds((t, FOX_WIDTH), F32)],
        compiler_params=_cp(), name="attn_bwd_prep")(after, dz, w_out, o, qp, lse_pad, sel_lse, sel_delta, head_of)


def _attn_bwd(qb, kp, vp, dob, kt):
    t = qb.shape[1]
    bq = min(ATT_BLOCK, t)
    nq = t // bq
    i_tab, j_tab = _triangle(nq, key_major=True)

    def body(it_ref, jt_ref, q_ref, k_ref, v_ref, do_ref, kt_ref, dqt_ref, dk_ref, dv_ref, dk_sc, dv_sc):
        s = pl.program_id(1)
        i, j = it_ref[s], jt_ref[s]

        @pl.when(s == 0)
        def _():
            dqt_ref[...] = jnp.zeros(dqt_ref.shape, F32)

        @pl.when(i == j)
        def _():
            dk_sc[...] = jnp.zeros(dk_sc.shape, F32)
            dv_sc[...] = jnp.zeros(dv_sc.shape, F32)

        cols = pl.ds(pl.multiple_of(i * bq, bq), bq)

        def sweep(masked):
            def scores(h):
                return (lax.dot_general(k_ref[h], q_ref[h], NT, preferred_element_type=F32),
                        lax.dot_general(v_ref[h], do_ref[h], NT, preferred_element_type=F32))

            def accumulate(h, ptb, dstb):
                dv_sc[h] += jnp.dot(ptb, do_ref[h], preferred_element_type=F32)
                dk_sc[h] += jnp.dot(dstb, q_ref[h], preferred_element_type=F32)
                dqt_ref[h, :, cols] += jnp.dot(kt_ref[h], dstb, preferred_element_type=F32)

            ahead, behind = scores(0), None
            for h in range(ATT_BWD_HEADS):
                st, dpt = ahead
                if h + 1 < ATT_BWD_HEADS:
                    ahead = scores(h + 1)
                if behind is not None:
                    accumulate(*behind)
                if masked:
                    key = lax.broadcasted_iota(jnp.int32, (bq, bq), 0)
                    qry = lax.broadcasted_iota(jnp.int32, (bq, bq), 1)
                    st = jnp.where(key <= qry, st, NEG)
                pt = jnp.exp(st)
                behind = (h, pt.astype(BF16), (pt * dpt).astype(BF16))
            accumulate(*behind)

        @pl.when(i == j)
        def _():
            sweep(True)

        @pl.when(i > j)
        def _():
            sweep(False)

        @pl.when(i == nq - 1)
        def _():
            dk_ref[...] = dk_sc[...]
            dv_ref[...] = dv_sc[...].astype(BF16)

    nh = ATT_BWD_HEADS
    qblk = pl.BlockSpec((nh, bq, LANES), lambda hp, s, it, jt: (hp, it[s], 0))
    kblk = pl.BlockSpec((nh, bq, LANES), lambda hp, s, it, jt: (hp, jt[s], 0))
    grid_spec = pltpu.PrefetchScalarGridSpec(
        num_scalar_prefetch=2, grid=(FOX_HEADS // nh, i_tab.shape[0]),
        in_specs=[qblk, kblk, kblk, qblk, pl.BlockSpec((nh, LANES, bq), lambda hp, s, it, jt: (hp, 0, jt[s]))],
        out_specs=[pl.BlockSpec((nh, LANES, t), lambda hp, s, it, jt: (hp, 0, 0), pipeline_mode=pl.Buffered(1)),
                   kblk, kblk],
        scratch_shapes=[pltpu.VMEM((nh, bq, LANES), F32), pltpu.VMEM((nh, bq, LANES), F32)])
    return pl.pallas_call(body, grid_spec=grid_spec,
                          out_shape=[_sds((FOX_HEADS, LANES, t), F32), _sds((FOX_HEADS, t, LANES), F32),
                                     _sds((FOX_HEADS, t, LANES), BF16)],
                          compiler_params=_cp(), name="attn_bwd")(i_tab, j_tab, qb, kp, vp, dob, kt)


def _attn_unpack(dqt, dkp, dvp):
    t = dkp.shape[1]
    tm = min(ROW_TILE, t)
    hd = HEAD_DIM

    def body(dqt_ref, dk_ref, dv_ref, o_ref, dc_ref):
        for h in range(FOX_HEADS):
            dq = dqt_ref[h].T
            o_ref[:, h * hd:(h + 1) * hd] = (dq[:, :hd] * (hd ** -0.5)).astype(BF16)
            o_ref[:, FOX_WIDTH + h * hd:FOX_WIDTH + (h + 1) * hd] = dk_ref[h, :, :hd].astype(BF16)
            o_ref[:, 2 * FOX_WIDTH + h * hd:2 * FOX_WIDTH + (h + 1) * hd] = dv_ref[h, :, :hd]
            dc_ref[:, h:h + 1] = dq[:, K_ONE:K_ONE + 1] - dk_ref[h, :, Q_ONE:Q_ONE + 1]

    row3 = pl.BlockSpec((FOX_HEADS, tm, LANES), lambda i: (0, i, 0))
    return pl.pallas_call(
        body, grid=(t // tm,),
        in_specs=[pl.BlockSpec((FOX_HEADS, LANES, tm), lambda i: (0, 0, i)), row3, row3],
        out_specs=[pl.BlockSpec((tm, QKV), lambda i: (i, 0)), pl.BlockSpec((tm, FOX_HEADS), lambda i: (i, 0))],
        out_shape=[_sds((t, QKV), BF16), _sds((t, FOX_HEADS), F32)],
        compiler_params=_cp(), name="attn_unpack")(dqt, dkp, dvp)


def _adamw(parts, w, m, v, name, layer=None, into=None):
    nl, r, c = w.shape
    fits = [cand for cand in [*range(SUBLANES, r, SUBLANES), r] if r % cand == 0 and cand * c * 4 <= ADAMW_BLOCK_BYTES]
    tr = max(fits) if fits else r
    npart = len(parts)
    bc1 = 1.0 - ADAM_B1 ** ADAM_STEP
    bc2 = 1.0 - ADAM_B2 ** ADAM_STEP

    def body(*refs):
        p_refs = refs[:npart]
        w_ref, m_ref, v_ref = refs[npart:npart + 3]
        g_ref, d_ref, nm_ref, nv_ref = refs[-4:]
        sums = []
        for p_ref in p_refs:
            acc = p_ref[0, 0].astype(F32)
            for s in range(1, p_ref.shape[0]):
                acc = acc + p_ref[s, 0].astype(F32)
            sums.append(acc)
        g = sums[0]
        for extra in sums[1:]:
            g = g + extra
        nm = ADAM_B1 * m_ref[0] + (1.0 - ADAM_B1) * g
        nv = ADAM_B2 * v_ref[0] + (1.0 - ADAM_B2) * (g * g)
        m_hat = nm / bc1
        v_hat = nv / bc2
        g_ref[0] = g
        d_ref[0] = -ADAM_LR * (m_hat / (jnp.sqrt(v_hat) + ADAM_EPS) + ADAM_WD * w_ref[0])
        nm_ref[0] = nm
        nv_ref[0] = nv

    first = 0 if layer is None else layer
    blk = pl.BlockSpec((1, tr, c), lambda l, i: (first + l, i, 0))
    extra = [] if into is None else list(into)
    return pl.pallas_call(
        body, grid=(nl if layer is None else 1, r // tr),
        in_specs=[pl.BlockSpec((p.shape[0], 1, tr, c), lambda l, i: (0, l, i, 0)) for p in parts] + [blk, blk, blk]
        + [_ANY_SPEC] * len(extra),
        out_specs=[blk] * 4, out_shape=[_sds(w.shape, F32)] * 4,
        input_output_aliases={npart + 3 + k: k for k in range(len(extra))},
        compiler_params=_cp(), name=name)(*parts, w, m, v, *extra)


def _to_rows(a):
    flat = a.reshape(-1)
    pad = (-flat.shape[0]) % LANES
    if pad:
        flat = jnp.concatenate([flat, jnp.zeros((pad,), flat.dtype)])
    return flat.reshape(-1, LANES)


def _by_owner_cols(dw):
    k, n = dw.shape
    return dw.reshape(k, N_CHIPS, n // N_CHIPS).transpose(1, 0, 2)[:, None]


def _ffn_fwd(xin_ln, xin_b, wi, wo, g, b, layer, target=None):
    t, d = xin_b.shape
    tm = min(FFN_FUSED_ROW_TILE, t)
    hh = HALF_HIDDEN
    rxh, rg, rb = xin_ln

    def body(x_ref, wi_ref, wo_ref, rxh_ref, rg_ref, rb_ref, g_ref, b_ref, *rest):
        gu_ref, h_ref = rest[target is not None:][:2]
        a = x_ref[...]
        for c in range(2):
            gs, us = slice(c * hh, (c + 1) * hh), slice(FFN_HIDDEN + c * hh, FFN_HIDDEN + (c + 1) * hh)
            gate = jnp.dot(a, wi_ref[c], preferred_element_type=F32)
            up = jnp.dot(a, wi_ref[2 + c], preferred_element_type=F32)
            sig = _sigmoid(gate)
            silu = gate * sig
            gu_ref[:, gs] = (up * sig * (1.0 + gate * (1.0 - sig))).astype(BF16)
            gu_ref[:, us] = silu.astype(BF16)
            h_ref[:, gs] = (silu * up).astype(BF16)
        z = ALPHA * (rxh_ref[...] * rg_ref[...] + rb_ref[...]) + jnp.dot(h_ref[...], wo_ref[...], preferred_element_type=F32)
        xhat, rstd = _ln_fwd(z)
        if target is None:
            yb_ref, xh_ref, rs_ref = rest[2:]
            yb_ref[...] = (xhat * g_ref[...] + b_ref[...]).astype(BF16)
            xh_ref[...] = xhat
            rs_ref[...] = rstd
            return
        sq_ref, dz_ref, dg_ref, db_ref = rest[3:]
        first = pl.program_id(0) == 0
        err = xhat * g_ref[...] + b_ref[...] - rest[0][...]
        dz, dg, db = _ln_bwd(err * (1.0 / d), xhat, rstd, g_ref[...])
        dz_ref[...] = dz
        _accumulate(sq_ref, first, jnp.sum(err * err, axis=0, keepdims=True))
        _accumulate(dg_ref, first, dg)
        _accumulate(db_ref, first, db)

    row = lambda i: (i, 0)
    full = pl.BlockSpec((tm, d), row)
    vec = _resident(g.shape)
    acc = pl.BlockSpec((1, d), lambda i: (0, 0))
    in_specs = [full, _resident(wi.shape), _resident(wo.shape), full, vec, vec, vec, vec]
    out_specs = [pl.BlockSpec((tm, 2 * FFN_HIDDEN), row), pl.BlockSpec((tm, FFN_HIDDEN), row)]
    out_shape = [_sds((t, 2 * FFN_HIDDEN), BF16), _sds((t, FFN_HIDDEN), BF16)]
    args = [xin_b, wi, wo, rxh, rg, rb, g, b]
    if target is None:
        out_specs += [full, full, pl.BlockSpec((tm, 1), row)]
        out_shape += [_sds((t, d), BF16), _sds((t, d), F32), _sds((t, 1), F32)]
    else:
        in_specs.append(full)
        args.append(target)
        out_specs += [acc, full, acc, acc]
        out_shape += [_sds((1, d), F32), _sds((t, d), F32), _sds((1, d), F32), _sds((1, d), F32)]
    gu, h, *tail = pl.pallas_call(body, grid=(t // tm,), in_specs=in_specs, out_specs=out_specs, out_shape=out_shape,
                                  compiler_params=_cp(), name=f"ffn_fwd_rows_{layer}")(*args)
    if target is None:
        y_b, xhat, rstd = tail
        return y_b, (xin_b, gu, h, xhat, rstd)
    return tail, (xin_b, gu, h)


def _ffn_bwd(dz, saved, wi, wo, ln_below, layer):
    xin_b, gu, h = saved[:3]
    dgu, *below = _ffn_bwd_rows(dz, wo, gu, wi, ln_below, f"ffn_bwd_rows_{layer}")
    g_out = _mm_tn(h, dz, f"ffn_dw_out_{layer}", tn=D_MODEL, tk=HALF_HIDDEN)
    g_in = _mm_tn(xin_b, dgu, f"ffn_dw_in_{layer}", tn=HALF_HIDDEN, stack_cols=True)
    return below, g_in, g_out.reshape(N_CHIPS, FFN_HIDDEN // N_CHIPS, D_MODEL)


def kernel(x, even_w_in, even_b_f, even_conv_w, even_w_out, odd_w_in, odd_v_ln_g, odd_v_ln_b, odd_w_s, odd_b_s, odd_w_out, mix_ln_g, mix_ln_b, ffn_w_in, ffn_w_out, ffn_ln_g, ffn_ln_b, loss_target, m_even_w_in, m_even_b_f, m_even_conv_w, m_even_w_out, m_odd_w_in, m_odd_v_ln_g, m_odd_v_ln_b, m_odd_w_s, m_odd_b_s, m_odd_w_out, m_mix_ln_g, m_mix_ln_b, m_ffn_w_in, m_ffn_w_out, m_ffn_ln_g, m_ffn_ln_b, v_even_w_in, v_even_b_f, v_even_conv_w, v_even_w_out, v_odd_w_in, v_odd_v_ln_g, v_odd_v_ln_b, v_odd_w_s, v_odd_b_s, v_odd_w_out, v_mix_ln_g, v_mix_ln_b, v_ffn_w_in, v_ffn_w_out, v_ffn_ln_g, v_ffn_ln_b):
    t = x.shape[1]
    d = D_MODEL
    chip = 2 * lax.axis_index("x") + lax.axis_index("y")
    x2d = x[0]
    target = loss_target[0]

    small_shard = jnp.concatenate([odd_v_ln_g.reshape(2, LANES), odd_v_ln_b.reshape(2, LANES),
                                   even_conv_w.reshape(CONV_K, LANES), jnp.zeros((1, LANES), F32)], axis=0)
    first = [jnp.swapaxes(even_w_in[0], 0, 1).astype(BF16)]
    second = [even_w_out[0].astype(BF16), small_shard]
    later = [odd_w_in[0].astype(BF16), odd_w_out[0].astype(BF16), ffn_w_in[0].astype(BF16), ffn_w_in[1].astype(BF16),
             ffn_w_out[0].astype(BF16), ffn_w_out[1].astype(BF16)]
    first_h, first_tok = _split_start(first, "gather4", "gather_first_start")
    second_h, second_tok = _split_start(second, "gather4", "gather_second_start", after=first_tok)
    later_h, later_tok = _split_start(later, "gather4", "gather_later_start", after=second_tok)
    (g_ewi,) = _gathered(first_h, "gather_first_wait", later_tok)
    ewi = g_ewi.reshape(EVEN_IN, d)
    w_even_in = jnp.concatenate([ewi[:QKV], ewi[QKV + FOX_HEADS:],
                                 jnp.pad(ewi[QKV:QKV + FOX_HEADS], ((0, LANES - FOX_HEADS), (0, 0)))], axis=0)
    chunk_id = jnp.arange(GMLP_BLOCK) // CHUNK
    gmask = chunk_id[None, :] <= chunk_id[:, None]
    w_spatial = jnp.where(gmask[None], odd_w_s[0], 0.0).astype(BF16)
    bs_col = odd_b_s[0].T
    b_f_col = even_b_f.reshape(FOX_HEADS, 1)
    ln = lambda p, l: p[l:l + 1]

    qkv, bch, fl = _proj(x2d, w_even_in, [(0, QKV, BF16), (QKV, QKV + BCH, F32), (QKV + BCH, EVEN_IN_PAD, F32)], "even_proj")
    fl3 = fl[:, :FOX_HEADS].T.reshape(FOX_HEADS, t // LANES, LANES).transpose(1, 0, 2)
    c3 = _fgate_fwd(fl3, b_f_col)
    c_rows = c3.transpose(1, 0, 2).reshape(FOX_HEADS, t)
    head_lanes = lambda rows: jnp.pad(rows.T, ((0, 0), (0, LANES - FOX_HEADS)))
    qp, kp, vp, kt, vt = _attn_pack(qkv, head_lanes(c_rows))
    attn, lse = _attn_fwd(qp, kp, vt)
    g_ewo, g_small = _gathered(second_h, "gather_second_wait", attn)
    w_even_out = g_ewo.reshape(d, d)
    v_ln_g = g_small[:, 0:2].reshape(1, d)
    v_ln_b = g_small[:, 2:4].reshape(1, d)
    conv_w = g_small[:, 4:7].transpose(1, 0, 2).reshape(CONV_K, CONV_WIDTH)
    conv, x1_b, xh1, rs1 = _even_out(attn, bch, conv_w, w_even_out, x2d, ln(mix_ln_g, 0), ln(mix_ln_b, 0))
    w_odd_in, g_owo, w_fi0, w_fi1, g_fo0, g_fo1 = _gathered(later_h, "gather_later_wait", x1_b)
    w_odd_out = g_owo.reshape(d, d)
    w_ffn_in = [w_fi0, w_fi1]
    w_ffn_out = [g_fo0.reshape(FFN_HIDDEN, d), g_fo1.reshape(FFN_HIDDEN, d)]
    x2_b, ffn0 = _ffn_fwd((xh1, ln(mix_ln_g, 0), ln(mix_ln_b, 0)), x1_b, w_ffn_in[0], w_ffn_out[0],
                          ln(ffn_ln_g, 0), ln(ffn_ln_b, 0), 0)

    sv_odd, rs_odd, gated, x3_b, xh3, rs3 = _gmlp_fwd(
        x2_b, w_odd_in, v_ln_g, v_ln_b, w_spatial, bs_col, w_odd_out, (ffn0[3], ln(ffn_ln_g, 0), ln(ffn_ln_b, 0)),
        ln(mix_ln_g, 1), ln(mix_ln_b, 1))
    (sq, dz4, d_fg1, d_fb1), ffn1 = _ffn_fwd((xh3, ln(mix_ln_g, 1), ln(mix_ln_b, 1)), x3_b, w_ffn_in[1], w_ffn_out[1],
                                             ln(ffn_ln_g, 1), ln(ffn_ln_b, 1), 1, target=target)

    loss = lax.psum(0.5 / d * jnp.sum(sq), ("x", "y", "c"))
    (dz3, d_mg1, d_mb1), gi_f1, go_f1 = _ffn_bwd(dz4, ffn1, w_ffn_in[1], w_ffn_out[1], (xh3, rs3, ln(mix_ln_g, 1)), 1)

    go_odd = _mm_tn(gated, dz3, "odd_dw_out", tn=d).reshape(N_CHIPS, 1, d // N_CHIPS, d)
    da_odd, dws, dbs_col, d_vg, d_vb, dz2, d_fg0, d_fb0 = _gmlp_bwd(
        dz3, w_odd_out, sv_odd, rs_odd, v_ln_g, v_ln_b, w_spatial, bs_col, w_odd_in,
        (ffn0[3], ffn0[4], ln(ffn_ln_g, 0)))
    gi_odd = _mm_tn(x2_b, da_odd, "odd_dw_in", tn=d // 2, stack_cols=True)[:, None]
    (dz1, d_mg0, d_mb0), gi_f0, go_f0 = _ffn_bwd(dz2, ffn0, w_ffn_in[0], w_ffn_out[0], (xh1, rs1, ln(mix_ln_g, 0)), 0)

    sent_early = [gi_odd, go_odd, gi_f0[:, None], gi_f1[:, None], go_f0[:, None], go_f1[:, None]]
    early_h, early_tok = _split_start(sent_early, "scatter4", "scatter_early_start")
    qb, dob, dconv = _attn_bwd_prep(dz1, w_even_out, attn, qp, head_lanes(lse.reshape(FOX_HEADS, t)), early_tok)
    go_even = jnp.concatenate([_mm_tn(attn, dz1, "even_dw_out_attn", tn=d), _mm_tn(conv, dz1, "even_dw_out_conv", tn=d)],
                              axis=0).reshape(N_CHIPS, 1, d // N_CHIPS, d)
    dbch, dconv_w8 = _conv_bwd(bch, dconv, conv_w)
    dqkv, dc_col = _attn_unpack(*_attn_bwd(qb, kp, vp, dob, kt))
    dc3 = dc_col.T.reshape(FOX_HEADS, t // LANES, LANES).transpose(1, 0, 2)
    dfl3, d_bf = _fgate_bwd(dc3, fl3, b_f_col)
    dfl = jnp.concatenate([dfl3.transpose(1, 0, 2).reshape(FOX_HEADS, t).T.astype(BF16),
                           jnp.zeros((t, LANES - FOX_HEADS), BF16)], axis=1)

    dws_masked = jnp.where(gmask[None], dws, 0.0)
    rep_names = ["odd_w_s", "odd_b_s", "mix_ln_g", "mix_ln_b", "ffn_ln_g", "ffn_ln_b", "even_b_f"]
    rep_grads = [dws_masked, dbs_col.T, jnp.concatenate([d_mg0, d_mg1]), jnp.concatenate([d_mb0, d_mb1]),
                 jnp.concatenate([d_fg0, d_fg1]), jnp.concatenate([d_fb0, d_fb1]), d_bf.reshape(1, FOX_HEADS)]
    rep_w = [(odd_w_s, m_odd_w_s, v_odd_w_s), (odd_b_s, m_odd_b_s, v_odd_b_s), (mix_ln_g, m_mix_ln_g, v_mix_ln_g),
             (mix_ln_b, m_mix_ln_b, v_mix_ln_b), (ffn_ln_g, m_ffn_ln_g, v_ffn_ln_g), (ffn_ln_b, m_ffn_ln_b, v_ffn_ln_b),
             (even_b_f, m_even_b_f, v_even_b_f)]
    rep_rows = [_to_rows(gr) for gr in rep_grads]
    n_rep = sum(r.shape[0] for r in rep_rows)
    pad_rep = (-n_rep) % SUBLANES
    dconv_w = dconv_w8[:CONV_K].reshape(CONV_K, N_CHIPS, LANES).transpose(1, 0, 2).reshape(N_CHIPS * CONV_K, LANES)
    packed = jnp.concatenate(rep_rows + [jnp.zeros((pad_rep, LANES), F32), d_vg.reshape(SUBLANES, LANES),
                                         d_vb.reshape(SUBLANES, LANES), dconv_w, jnp.zeros((4, LANES), F32)], axis=0)
    small_h, small_tok = _split_start([packed], "gather8", "gather_small_start")

    swap_h, swap_tok = _split_start(_scattered(early_h, "scatter_early_wait", small_tok), "swap2", "swap_early_start")
    dw_qkv = _mm_tn(dqkv, x2d, "even_dw_qkv", tn=d, tk=QKV // 2, after=swap_tok)
    dw_bch = _mm_tn(dbch, x2d, "even_dw_bch", tn=d, tk=BCH // 2)
    dw_f = _mm_tn(dfl, x2d, "even_dw_f", tn=d)
    gi_even = jnp.concatenate([dw_qkv, dw_f[:FOX_HEADS], dw_bch], axis=0).reshape(N_CHIPS, 1, -1, LANES)
    sent_late = [gi_even, go_even]
    late_h, late_tok = _split_start(sent_late, "scatter4", "scatter_late_start")
    grad_x = _mm_back([(dqkv, 0, QKV), (dbch, QKV, QKV + BCH), (dfl, QKV + BCH, EVEN_IN_PAD)], w_even_in, dz1,
                      late_tok, "even_dx")
    mine, theirs = _split_wait(swap_h, "swap_early_wait", grad_x)
    res = {}
    res["odd_w_in"] = _adamw([mine[0], theirs[0]], odd_w_in, m_odd_w_in, v_odd_w_in, "adamw_odd_w_in")
    res["odd_w_out"] = _adamw([mine[1], theirs[1]], odd_w_out, m_odd_w_out, v_odd_w_out, "adamw_odd_w_out")
    for nm, at, (w, m, v) in (("ffn_w_in", 2, (ffn_w_in, m_ffn_w_in, v_ffn_w_in)),
                              ("ffn_w_out", 4, (ffn_w_out, m_ffn_w_out, v_ffn_w_out))):
        upper = _adamw([mine[at + 1], theirs[at + 1]], w, m, v, f"adamw_{nm}_1", layer=1)
        res[nm] = _adamw([mine[at], theirs[at]], w, m, v, f"adamw_{nm}_0", layer=0, into=upper)
    mine_late = _scattered(late_h, "scatter_late_wait", res["ffn_w_out"][0])
    theirs_late = _exchange(mine_late, "swap2", "swap_late")
    rows = lambda a: jnp.swapaxes(a, 1, 2).reshape(1, -1, LANES)
    back = lambda a: jnp.swapaxes(a.reshape(1, EVEN_IN // N_CHIPS, d), 1, 2)
    res["even_w_in"] = [back(o) for o in _adamw([mine_late[0], theirs_late[0]], rows(even_w_in), rows(m_even_w_in),
                                                rows(v_even_w_in), "adamw_even_w_in")]
    res["even_w_out"] = _adamw([mine_late[1], theirs_late[1]], even_w_out, m_even_w_out, v_even_w_out,
                               "adamw_even_w_out")
    (packed,), (gathered,) = _split_wait(small_h, "gather_small_wait", theirs_late[0])
    gathered = lax.dynamic_update_index_in_dim(gathered, packed, 4 * lax.axis_index("x") + 2 * lax.axis_index("y")
                                               + lax.axis_index("c"), 0)

    base = n_rep + pad_rep
    own_rows = jnp.concatenate([
        lax.dynamic_slice_in_dim(gathered, base + 2 * chip, 2, axis=1),
        lax.dynamic_slice_in_dim(gathered, base + SUBLANES + 2 * chip, 2, axis=1),
        lax.dynamic_slice_in_dim(gathered, base + 2 * SUBLANES + CONV_K * chip, CONV_K, axis=1),
        jnp.zeros((N_DEV, 1, LANES), F32)], axis=1)
    small_parts = jnp.concatenate([gathered[:, :base], own_rows], axis=1)[:, None]

    def pack_small(get):
        rows = [_to_rows(get(tw)) for tw in rep_w] + [jnp.zeros((pad_rep, LANES), F32)]
        rows += [get(sh).reshape(-1, LANES) for sh in ((odd_v_ln_g, m_odd_v_ln_g, v_odd_v_ln_g),
                                                       (odd_v_ln_b, m_odd_v_ln_b, v_odd_v_ln_b),
                                                       (even_conv_w, m_even_conv_w, v_even_conv_w))]
        return jnp.concatenate(rows + [jnp.zeros((1, LANES), F32)], axis=0)[None]

    small_out = _adamw([small_parts], pack_small(lambda tw: tw[0]), pack_small(lambda tw: tw[1]),
                       pack_small(lambda tw: tw[2]), "adamw_small")

    def unpack_small(rows3):
        rows = rows3[0]
        out, off = {}, 0
        for nm, (w, _, _), r in zip(rep_names, rep_w, rep_rows):
            out[nm] = rows[off:off + r.shape[0]].reshape(-1)[:w.size].reshape(w.shape)
            off += r.shape[0]
        off += pad_rep
        out["odd_v_ln_g"] = rows[off:off + 2].reshape(odd_v_ln_g.shape)
        out["odd_v_ln_b"] = rows[off + 2:off + 4].reshape(odd_v_ln_b.shape)
        out["even_conv_w"] = rows[off + 4:off + 4 + CONV_K].reshape(even_conv_w.shape)
        return out

    small = [unpack_small(o) for o in small_out]
    order = ["even_w_in", "even_b_f", "even_conv_w", "even_w_out", "odd_w_in", "odd_v_ln_g", "odd_v_ln_b", "odd_w_s",
             "odd_b_s", "odd_w_out", "mix_ln_g", "mix_ln_b", "ffn_w_in", "ffn_w_out", "ffn_ln_g", "ffn_ln_b"]
    outs = [loss, grad_x[None]]
    for kind in range(4):
        for nm in order:
            outs.append(res[nm][kind] if nm in res else small[kind][nm])
    return tuple(outs)
```

```python
import math

import jax
import jax.numpy as jnp
from jax import lax
from jax.experimental import pallas as pl
from jax.experimental.pallas import tpu as pltpu

F32 = jnp.float32
BF16 = jnp.bfloat16

D_MODEL = 1024
FOX_HEADS = 8
HEAD_DIM = 64
FOX_WIDTH = FOX_HEADS * HEAD_DIM
CONV_WIDTH = 512
CONV_K = 3
QKV = 3 * FOX_WIDTH
BCH = 3 * CONV_WIDTH
EVEN_IN = QKV + FOX_HEADS + BCH
EVEN_IN_PAD = QKV + BCH + 128
GMLP_BLOCK = 128
GMLP_GROUPS = 8
CHUNK = 64
FFN_HIDDEN = 2816
HALF_HIDDEN = FFN_HIDDEN // 2
ALPHA = 4.0 ** 0.25
LN_EPS = 1e-5
ADAM_LR = 0.001
ADAM_B1 = 0.9
ADAM_B2 = 0.999
ADAM_EPS = 1e-08
ADAM_WD = 0.01
ADAM_STEP = 10
N_CHIPS = 4
N_DEV = 8
LANES = 128
SUBLANES = 8
ROW_TILE = 512
FFN_FUSED_ROW_TILE = 256
REDUCE_TILE = 2048
ATT_BLOCK = 512
ATT_FWD_HEADS = 8
ATT_BWD_HEADS = 8
ADAMW_BLOCK_BYTES = 2 ** 20
VMEM_LIMIT = 56 * 2 ** 20
ATT_BWD_VMEM_LIMIT = 60 * 2 ** 20
NEG = -1e30
MESH = pl.DeviceIdType.MESH
HIGHEST = lax.Precision.HIGHEST
Q_C, Q_ONE, Q_LSE = 64, 67, 70
K_ONE, K_C, K_ONE2 = 64, 67, 70
V_ONE = 64
DO_DELTA = 65
NT = (((1,), (1,)), ((), ()))
TN = (((0,), (0,)), ((), ()))


def _cp(limit=VMEM_LIMIT):
    return pltpu.CompilerParams(vmem_limit_bytes=limit)


def _resident(shape):
    zeros = (0,) * len(shape)
    return pl.BlockSpec(shape, lambda *_: zeros, pipeline_mode=pl.Buffered(1))


def _sds(shape, dtype):
    return jax.ShapeDtypeStruct(tuple(shape), dtype)


_MASKS = {
    "gather4": [(1, 0, 0), (0, 1, 0), (1, 1, 0)],
    "scatter4": [(1, 0, 0), (0, 1, 0), (1, 1, 0)],
    "swap2": [(0, 0, 1)],
    "gather8": [(0, 0, 1), (0, 1, 0), (0, 1, 1), (1, 0, 0), (1, 0, 1), (1, 1, 0), (1, 1, 1)],
}


def _exchange(arrs, mode, name):
    n = len(arrs)
    masks = _MASKS[mode]
    npeer = len(masks)
    lead = {"gather4": N_CHIPS, "gather8": N_DEV}.get(mode)
    out_shapes = [_sds(((lead,) if lead else ()) + a.shape, a.dtype) for a in arrs]

    def body(*refs):
        ins, outs = refs[:n], refs[n:2 * n]
        send_sems, recv_sems, loc_sems = refs[2 * n:]
        x, y, c = lax.axis_index("x"), lax.axis_index("y"), lax.axis_index("c")
        chip, dev = 2 * x + y, 4 * x + 2 * y + c
        sends, recvs, locs = [], [], []
        for k in range(n):
            if mode == "gather4":
                locs.append(pltpu.make_async_copy(ins[k], outs[k].at[chip], loc_sems.at[k]))
            elif mode == "scatter4":
                locs.append(pltpu.make_async_copy(ins[k].at[chip], outs[k].at[chip], loc_sems.at[k]))
            elif mode == "gather8":
                locs.append(pltpu.make_async_copy(ins[k], outs[k].at[dev], loc_sems.at[k]))
        for cp in locs:
            cp.start()
        for k in range(n):
            for j, (dx, dy, dc) in enumerate(masks):
                px = 1 - x if dx else x
                py = 1 - y if dy else y
                pc = 1 - c if dc else c
                pchip, pdev = 2 * px + py, 4 * px + 2 * py + pc
                if mode == "gather4":
                    src, dst, land = ins[k], outs[k].at[chip], outs[k].at[pchip]
                elif mode == "scatter4":
                    src, dst, land = ins[k].at[pchip], outs[k].at[chip], outs[k].at[pchip]
                elif mode == "swap2":
                    src, dst, land = ins[k], outs[k], outs[k]
                else:
                    src, dst, land = ins[k], outs[k].at[dev], outs[k].at[pdev]
                s = k * npeer + j
                kw = dict(send_sem=send_sems.at[s], recv_sem=recv_sems.at[s], device_id=(px, py, pc),
                          device_id_type=MESH)
                cp = pltpu.make_async_remote_copy(src_ref=src, dst_ref=dst, **kw)
                cp.start()
                sends.append(cp)
                recvs.append(pltpu.make_async_remote_copy(src_ref=src, dst_ref=land, **kw))
        for cp in recvs:
            cp.wait_recv()
        for cp in sends:
            cp.wait_send()
        for cp in locs:
            cp.wait()

    any_spec = pl.BlockSpec(memory_space=pl.ANY)
    outs = pl.pallas_call(
        body,
        out_shape=out_shapes,
        in_specs=[any_spec] * n,
        out_specs=[any_spec] * n,
        scratch_shapes=[pltpu.SemaphoreType.DMA((n * npeer,)), pltpu.SemaphoreType.DMA((n * npeer,)),
                        pltpu.SemaphoreType.DMA((max(n, 1),))],
        name=name,
    )(*arrs)
    return list(outs)


_HBM_SPEC = pl.BlockSpec(memory_space=pltpu.HBM)
_SEM_SPEC = pl.BlockSpec(memory_space=pltpu.SEMAPHORE)
_ANY_SPEC = pl.BlockSpec(memory_space=pl.ANY)
_EFFECT = pltpu.SideEffectType.DATAFLOW_SIDE_EFFECTING


def _split_copies(mode, ins, lands, send_sems, recv_sems):
    x, y, c = lax.axis_index("x"), lax.axis_index("y"), lax.axis_index("c")
    chip, dev = 2 * x + y, 4 * x + 2 * y + c
    masks = _MASKS[mode]
    out = []
    for k in range(len(ins)):
        for j, (dx, dy, dc) in enumerate(masks):
            px = 1 - x if dx else x
            py = 1 - y if dy else y
            pc = 1 - c if dc else c
            pchip, pdev = 2 * px + py, 4 * px + 2 * py + pc
            if mode == "gather4":
                src, dst, land = ins[k], lands[k].at[chip], lands[k].at[pchip]
            elif mode == "scatter4":
                src, dst, land = ins[k].at[pchip], lands[k].at[chip], lands[k].at[pchip]
            elif mode == "swap2":
                src, dst, land = ins[k], lands[k], lands[k]
            else:
                src, dst, land = ins[k], lands[k].at[dev], lands[k].at[pdev]
            s = k * len(masks) + j
            kw = dict(send_sem=send_sems.at[s], recv_sem=recv_sems.at[s], device_id=(px, py, pc), device_id_type=MESH)
            out.append((pltpu.make_async_remote_copy(src_ref=src, dst_ref=dst, **kw),
                        pltpu.make_async_remote_copy(src_ref=src, dst_ref=land, **kw)))
    return out


def _split_start(arrs, mode, name, after=None):
    n = len(arrs)
    nsem = n * len(_MASKS[mode])
    lead = {"gather4": (N_CHIPS,), "gather8": (N_DEV,)}.get(mode, ())
    land_shapes = [lead + a.shape for a in arrs]

    def body(*refs):
        ins, lands = refs[:n], refs[n:2 * n]
        outs = refs[2 * n + (after is not None):]
        for start, _ in _split_copies(mode, ins, lands, outs[0], outs[1]):
            start.start()
        outs[-1][...] = jnp.zeros(outs[-1].shape, F32)

    srcs = [pltpu.with_memory_space_constraint(a, pltpu.HBM) for a in arrs]
    empties = [pltpu.with_memory_space_constraint(lax.empty(s, a.dtype), pltpu.HBM) for s, a in zip(land_shapes, arrs)]
    res = pl.pallas_call(
        body, name=name,
        out_shape=(pltpu.SemaphoreType.DMA((nsem,)), pltpu.SemaphoreType.DMA((nsem,)),
                   *[pltpu.HBM(a.shape, a.dtype) for a in arrs],
                   *[pltpu.HBM(s, a.dtype) for s, a in zip(land_shapes, arrs)],
                   _sds((SUBLANES, LANES), F32)),
        in_specs=[_HBM_SPEC] * (2 * n) + ([_ANY_SPEC] if after is not None else []),
        out_specs=(_SEM_SPEC, _SEM_SPEC, *[_HBM_SPEC] * (2 * n), pl.BlockSpec(memory_space=pltpu.VMEM)),
        input_output_aliases={k: 2 + k for k in range(2 * n)},
        compiler_params=pltpu.CompilerParams(has_side_effects=_EFFECT),
    )(*srcs, *empties, *([after] if after is not None else []))
    return dict(mode=mode, n=n, sems=res[:2], bufs=res[2:2 + 2 * n]), res[-1]


def _split_wait(handle, name, after):
    n, mode = handle["n"], handle["mode"]

    def body(*refs):
        ins, lands = refs[:n], refs[n:2 * n]
        send_sems, recv_sems = refs[2 * n], refs[2 * n + 1]
        for _, arrival in _split_copies(mode, ins, lands, send_sems, recv_sems):
            arrival.wait_send()
            arrival.wait_recv()

    bufs = handle["bufs"]
    res = pl.pallas_call(
        body, name=name,
        out_shape=tuple(pltpu.HBM(b.shape, b.dtype) for b in bufs),
        in_specs=[_HBM_SPEC] * (2 * n) + [_SEM_SPEC, _SEM_SPEC, _ANY_SPEC],
        out_specs=tuple([_HBM_SPEC] * (2 * n)),
        input_output_aliases={k: k for k in range(2 * n)},
        compiler_params=pltpu.CompilerParams(has_side_effects=_EFFECT),
    )(*bufs, *handle["sems"], after)
    return list(res[:n]), list(res[n:])


def _with_own(landed, own):
    chip = 2 * lax.axis_index("x") + lax.axis_index("y")
    return lax.dynamic_update_index_in_dim(landed, own, chip, 0)


def _gathered(handle, name, after):
    sent, landed = _split_wait(handle, name, after)
    return [_with_own(g, own) for g, own in zip(landed, sent)]


def _scattered(handle, name, after):
    chip = 2 * lax.axis_index("x") + lax.axis_index("y")
    sent, landed = _split_wait(handle, name, after)
    return [_with_own(r, lax.dynamic_index_in_dim(g, chip, 0, keepdims=False)) for r, g in zip(landed, sent)]


def _sigmoid(x):
    return 0.5 * jnp.tanh(0.5 * x) + 0.5


def _log_sigmoid(x):
    e = jnp.exp(-jnp.abs(x))
    log1p = jnp.where(e < 1e-2, e * (1.0 - e * (0.5 - e * (1.0 / 3.0))), jnp.log(1.0 + e))
    return jnp.minimum(x, 0.0) - log1p


def _ln_fwd(z):
    mu = jnp.mean(z, axis=-1, keepdims=True)
    zc = z - mu
    var = jnp.mean(zc * zc, axis=-1, keepdims=True)
    rstd = lax.rsqrt(var + LN_EPS)
    return zc * rstd, rstd


def _ln_bwd(dy, xhat, rstd, g):
    dxh = dy * g
    m1 = jnp.mean(dxh, axis=-1, keepdims=True)
    m2 = jnp.mean(dxh * xhat, axis=-1, keepdims=True)
    dz = rstd * (dxh - m1 - xhat * m2)
    return dz, jnp.sum(dy * xhat, axis=0, keepdims=True), jnp.sum(dy, axis=0, keepdims=True)


def _shift_down(z, halo):
    r = lax.broadcasted_iota(jnp.int32, z.shape, 0)
    z1 = jnp.where(r == 0, halo[7:8, :], pltpu.roll(z, 1, 0))
    z2 = jnp.where(r == 0, halo[6:7, :], jnp.where(r == 1, halo[7:8, :], pltpu.roll(z, 2, 0)))
    return z1, z2


def _shift_up(z, halo):
    n = z.shape[0]
    r = lax.broadcasted_iota(jnp.int32, z.shape, 0)
    z1 = jnp.where(r == n - 1, halo[0:1, :], pltpu.roll(z, n - 1, 0))
    z2 = jnp.where(r == n - 1, halo[1:2, :], jnp.where(r == n - 2, halo[0:1, :], pltpu.roll(z, n - 2, 0)))
    return z1, z2


def _accumulate(ref, first, value):
    @pl.when(first)
    def _():
        ref[...] = value

    @pl.when(jnp.logical_not(first))
    def _():
        ref[...] += value


def _proj(x, wt, splits, name):
    t, k = x.shape
    tm = min(ROW_TILE, t)
    w = wt

    def body(x_ref, w_ref, *outs):
        a = x_ref[...].astype(BF16)
        for (lo, hi, dt), o in zip(splits, outs):
            o[...] = lax.dot_general(a, w_ref[lo:hi, :], NT, preferred_element_type=F32).astype(dt)

    return pl.pallas_call(
        body, grid=(t // tm,),
        in_specs=[pl.BlockSpec((tm, k), lambda i: (i, 0)), _resident(w.shape)],
        out_specs=[pl.BlockSpec((tm, hi - lo), lambda i: (i, 0)) for lo, hi, _ in splits],
        out_shape=[_sds((t, hi - lo), dt) for lo, hi, dt in splits],
        compiler_params=_cp(), name=name)(x, w)


def _fgate_fwd(fl3, b_f):
    nc = fl3.shape[0]

    def body(f_ref, b_ref, c_ref):
        r = lax.broadcasted_iota(jnp.int32, (LANES, LANES), 0)
        cidx = lax.broadcasted_iota(jnp.int32, (LANES, LANES), 1)
        upper = (r <= cidx).astype(F32)

        def step(i, carry):
            lf = _log_sigmoid(f_ref[i] + b_ref[...])
            cc = jnp.dot(lf, upper, precision=HIGHEST, preferred_element_type=F32) + carry
            c_ref[i] = cc
            return cc[:, LANES - 1:LANES]

        lax.fori_loop(0, nc, step, jnp.zeros((FOX_HEADS, 1), F32))

    return pl.pallas_call(body, out_shape=_sds(fl3.shape, F32), name="fgate_fwd")(fl3, b_f)


def _fgate_bwd(dc3, fl3, b_f):
    nc = fl3.shape[0]

    def body(dc_ref, f_ref, b_ref, df_ref, db_ref):
        r = lax.broadcasted_iota(jnp.int32, (LANES, LANES), 0)
        cidx = lax.broadcasted_iota(jnp.int32, (LANES, LANES), 1)
        lower = (r >= cidx).astype(F32)

        def step(n, carry):
            suffix, db = carry
            i = nc - 1 - n
            dlf = jnp.dot(dc_ref[i], lower, precision=HIGHEST, preferred_element_type=F32) + suffix
            df = dlf * (1.0 - _sigmoid(f_ref[i] + b_ref[...]))
            df_ref[i] = df
            return dlf[:, 0:1], db + jnp.sum(df, axis=1, keepdims=True)

        zero = jnp.zeros((FOX_HEADS, 1), F32)
        _, db = lax.fori_loop(0, nc, step, (zero, zero))
        db_ref[...] = db

    return pl.pallas_call(body, out_shape=[_sds(fl3.shape, F32), _sds((FOX_HEADS, 1), F32)],
                          name="fgate_bwd")(dc3, fl3, b_f)


def _split3(c):
    hi = c.astype(BF16).astype(F32)
    mid = (c - hi).astype(BF16).astype(F32)
    lo = (c - hi - mid).astype(BF16).astype(F32)
    return hi, mid, lo


PIECE_ONE = 3 * FOX_HEADS


def _piece_rows(values):
    hi, mid, lo = _split3(values)
    lane = lax.broadcasted_iota(jnp.int32, values.shape, 1)
    row = hi + pltpu.roll(mid, FOX_HEADS, 1) + pltpu.roll(lo, 2 * FOX_HEADS, 1) + jnp.where(lane == PIECE_ONE, 1.0, 0.0)
    return row.astype(BF16)


def _piece_selector(start, sign, ones=()):
    sel = [[0.0] * FOX_WIDTH for _ in range(LANES)]
    for h in range(FOX_HEADS):
        for n in range(3):
            sel[n * FOX_HEADS + h][h * HEAD_DIM + start - HEAD_DIM + n] = sign
        for lane in ones:
            sel[PIECE_ONE][h * HEAD_DIM + lane - HEAD_DIM] = 1.0
    return jnp.asarray(sel, BF16)


def _attn_pack(qkv, c_pad):
    t = qkv.shape[0]
    tm = min(ROW_TILE, t)
    hd = HEAD_DIM
    sel_q = _piece_selector(Q_C, 1.0, range(Q_ONE, Q_ONE + 3))
    sel_k = _piece_selector(K_C, -1.0, [*range(K_ONE, K_ONE + 3), *range(K_ONE2, K_ONE2 + 3)])
    sel_v = _piece_selector(HEAD_DIM, 0.0, range(V_ONE, V_ONE + 4))

    def body(x_ref, c_ref, sq_ref, sk_ref, sv_ref, qp_ref, kp_ref, vp_ref, kt_ref, vt_ref):
        pieces = _piece_rows(c_ref[...])
        q_extra = jnp.dot(pieces, sq_ref[...], preferred_element_type=F32).astype(BF16)
        k_extra = jnp.dot(pieces, sk_ref[...], preferred_element_type=F32).astype(BF16)
        v_extra = jnp.dot(pieces, sv_ref[...], preferred_element_type=F32).astype(BF16)
        for h in range(FOX_HEADS):
            hs = slice(h * hd, (h + 1) * hd)
            qp_ref[h, :, :hd] = (x_ref[:, hs].astype(F32) * (hd ** -0.5)).astype(BF16)
            qp_ref[h, :, hd:] = q_extra[:, hs]
            kp_ref[h, :, :hd] = x_ref[:, FOX_WIDTH + h * hd:FOX_WIDTH + (h + 1) * hd]
            kp_ref[h, :, hd:] = k_extra[:, hs]
            vp_ref[h, :, :hd] = x_ref[:, 2 * FOX_WIDTH + h * hd:2 * FOX_WIDTH + (h + 1) * hd]
            vp_ref[h, :, hd:] = v_extra[:, hs]
            kt_ref[h] = kp_ref[h].T
            vt_ref[h] = vp_ref[h].T

    row3 = pl.BlockSpec((FOX_HEADS, tm, LANES), lambda i: (0, i, 0))
    col3 = pl.BlockSpec((FOX_HEADS, LANES, tm), lambda i: (0, 0, i))
    sel = _resident(sel_q.shape)
    return pl.pallas_call(
        body, grid=(t // tm,),
        in_specs=[pl.BlockSpec((tm, QKV), lambda i: (i, 0)), pl.BlockSpec((tm, LANES), lambda i: (i, 0)), sel, sel, sel],
        out_specs=[row3, row3, row3, col3, col3],
        out_shape=[_sds((FOX_HEADS, t, LANES), BF16)] * 3 + [_sds((FOX_HEADS, LANES, t), BF16)] * 2,
        compiler_params=_cp(), name="attn_pack")(qkv, c_pad, sel_q, sel_k, sel_v)


def _triangle(nq, key_major):
    if key_major:
        pairs = [(i, j) for j in range(nq) for i in range(j, nq)]
    else:
        pairs = [(i, j) for i in range(nq) for j in range(i + 1)]
    return jnp.asarray([p[0] for p in pairs], jnp.int32), jnp.asarray([p[1] for p in pairs], jnp.int32)


def _attn_fwd(qp, kp, vt):
    t = qp.shape[1]
    bq = min(ATT_BLOCK, t)
    nq = t // bq
    nh = ATT_FWD_HEADS
    i_tab, j_tab = _triangle(nq, key_major=False)

    def body(it_ref, jt_ref, q_ref, k_ref, vt_ref, o_ref, lse_ref, m_sc, acc_sc):
        s = pl.program_id(1)
        i, j = it_ref[s], jt_ref[s]

        @pl.when(j == 0)
        def _():
            m_sc[...] = jnp.full(m_sc.shape, NEG, F32)
            acc_sc[...] = jnp.zeros(acc_sc.shape, F32)

        def sweep(masked):
            scores = lambda h: lax.dot_general(k_ref[h], q_ref[h], NT, preferred_element_type=F32)

            def accumulate(h, pt, rescale):
                acc_sc[h] = rescale * acc_sc[h] + jnp.dot(vt_ref[h], pt, preferred_element_type=F32)

            ahead, behind = scores(0), None
            for h in range(nh):
                st = ahead
                if h + 1 < nh:
                    ahead = scores(h + 1)
                if behind is not None:
                    accumulate(*behind)
                if masked:
                    key = lax.broadcasted_iota(jnp.int32, (bq, bq), 0)
                    qry = lax.broadcasted_iota(jnp.int32, (bq, bq), 1)
                    st = jnp.where(key <= qry, st, NEG)
                m_prev = m_sc[h]
                m_new = jnp.maximum(m_prev, jnp.max(st, axis=0, keepdims=True))
                behind = (h, jnp.exp(st - m_new).astype(BF16), jnp.exp(m_prev - m_new))
                m_sc[h] = m_new
            accumulate(*behind)

        @pl.when(j < i)
        def _():
            sweep(False)

        @pl.when(j == i)
        def _():
            sweep(True)
            for h in range(nh):
                acc = acc_sc[h]
                denom = acc[V_ONE:V_ONE + 1, :]
                o_ref[:, h * HEAD_DIM:(h + 1) * HEAD_DIM] = (acc[:HEAD_DIM, :] / denom).T.astype(BF16)
                lse_ref[h] = m_sc[h] + jnp.log(denom)

    grid_spec = pltpu.PrefetchScalarGridSpec(
        num_scalar_prefetch=2, grid=(FOX_HEADS // nh, i_tab.shape[0]),
        in_specs=[pl.BlockSpec((nh, bq, LANES), lambda hp, s, it, jt: (hp, it[s], 0)),
                  pl.BlockSpec((nh, bq, LANES), lambda hp, s, it, jt: (hp, jt[s], 0)),
                  pl.BlockSpec((nh, LANES, bq), lambda hp, s, it, jt: (hp, 0, jt[s]))],
        out_specs=[pl.BlockSpec((bq, nh * HEAD_DIM), lambda hp, s, it, jt: (it[s], hp)),
                   pl.BlockSpec((nh, 1, bq), lambda hp, s, it, jt: (hp, 0, it[s]))],
        scratch_shapes=[pltpu.VMEM((nh, 1, bq), F32), pltpu.VMEM((nh, LANES, bq), F32)])
    return pl.pallas_call(body, grid_spec=grid_spec,
                          out_shape=[_sds((t, FOX_WIDTH), BF16), _sds((FOX_HEADS, 1, t), F32)],
                          compiler_params=_cp(), name="attn_fwd")(i_tab, j_tab, qp, kp, vt)


def _even_out(attn, bch, conv_w, w_out, x, g, b):
    t, d = x.shape
    tm = min(ROW_TILE, t)
    halo_blocks = tm // SUBLANES
    cw = CONV_WIDTH

    def body(a_ref, cur_ref, prev_ref, cw_ref, wo_ref, x_ref, g_ref, b_ref, conv_ref, yb_ref, xh_ref, rs_ref):
        i = pl.program_id(0)
        z = cur_ref[:, cw:2 * cw] * cur_ref[:, 2 * cw:]
        zp = jnp.where(i == 0, 0.0, prev_ref[:, cw:2 * cw] * prev_ref[:, 2 * cw:])
        z1, z2 = _shift_down(z, zp)
        conv = (cur_ref[:, :cw] * (cw_ref[0:1, :] * z2 + cw_ref[1:2, :] * z1 + cw_ref[2:3, :] * z)).astype(BF16)
        conv_ref[...] = conv
        pre = (ALPHA * x_ref[...] + jnp.dot(a_ref[...], wo_ref[:FOX_WIDTH, :], preferred_element_type=F32)
               + jnp.dot(conv, wo_ref[FOX_WIDTH:, :], preferred_element_type=F32))
        xhat, rstd = _ln_fwd(pre)
        yb_ref[...] = (xhat * g_ref[...] + b_ref[...]).astype(BF16)
        xh_ref[...] = xhat
        rs_ref[...] = rstd

    row = lambda i: (i, 0)
    full, half = pl.BlockSpec((tm, d), row), pl.BlockSpec((tm, cw), row)
    return pl.pallas_call(
        body, grid=(t // tm,),
        in_specs=[half, pl.BlockSpec((tm, BCH), row),
                  pl.BlockSpec((SUBLANES, BCH), lambda i: (jnp.maximum(i * halo_blocks - 1, 0), 0)),
                  _resident(conv_w.shape), _resident(w_out.shape), full, _resident(g.shape), _resident(b.shape)],
        out_specs=[half, full, full, pl.BlockSpec((tm, 1), row)],
        out_shape=[_sds((t, cw), BF16), _sds((t, d), BF16), _sds((t, d), F32), _sds((t, 1), F32)],
        compiler_params=_cp(), name="even_out")(attn, bch, bch, conv_w, w_out, x, g, b)


def _gmlp_fwd(x, w_in, vg, vb, wm, bs_col, w_out, res_ln, g, b):
    t, d = x.shape
    tm = min(ROW_TILE, t)
    gb = GMLP_BLOCK
    rxh, rg, rb = res_ln

    def body(x_ref, w_ref, vg_ref, vb_ref, wm_ref, bs_ref, wo_ref, rxh_ref, rg_ref, rb_ref, g_ref, b_ref,
             sv_ref, rs_ref, o_ref, yb_ref, xh_ref, rsy_ref, a_sc):
        xb = x_ref[...].astype(BF16)
        nc = w_ref.shape[2]
        for j in range(w_ref.shape[0]):
            a_sc[:, j * nc:(j + 1) * nc] = jnp.dot(xb, w_ref[j], preferred_element_type=F32)
        halves = []
        for half in range(2):
            a = a_sc[:, half * d:(half + 1) * d]
            cdf = 0.5 * (1.0 + lax.erf(a * (2.0 ** -0.5)))
            halves.append(a * cdf)
            slope = cdf + a * (jnp.exp(-0.5 * a * a) * (1.0 / math.sqrt(2.0 * math.pi)))
            sv_ref[:, (2 * half + 1) * d:(2 * half + 2) * d] = slope.astype(BF16)
        u = halves[0]
        vhat, rstd = _ln_fwd(halves[1])
        sv_ref[:, :d] = u.astype(BF16)
        sv_ref[:, 2 * d:3 * d] = vhat.astype(BF16)
        rs_ref[...] = rstd
        vln = (vhat * vg_ref[...] + vb_ref[...]).astype(BF16)
        for blk in range(tm // gb):
            rs = slice(blk * gb, (blk + 1) * gb)
            for gi in range(GMLP_GROUPS):
                cs = slice(gi * gb, (gi + 1) * gb)
                s = jnp.dot(wm_ref[gi], vln[rs, cs], preferred_element_type=F32) + bs_ref[:, gi:gi + 1]
                o_ref[rs, cs] = (u[rs, cs] * s).astype(BF16)
        z = ALPHA * (rxh_ref[...] * rg_ref[...] + rb_ref[...]) + jnp.dot(o_ref[...], wo_ref[...], preferred_element_type=F32)
        xhat, rstd_y = _ln_fwd(z)
        yb_ref[...] = (xhat * g_ref[...] + b_ref[...]).astype(BF16)
        xh_ref[...] = xhat
        rsy_ref[...] = rstd_y

    row = lambda i: (i, 0)
    full, col, vec = pl.BlockSpec((tm, d), row), pl.BlockSpec((tm, 1), row), _resident(g.shape)
    return pl.pallas_call(
        body, grid=(t // tm,),
        in_specs=[full, _resident(w_in.shape), _resident(vg.shape), _resident(vb.shape),
                  _resident(wm.shape), _resident(bs_col.shape), _resident(w_out.shape), full, vec, vec, vec, vec],
        out_specs=[pl.BlockSpec((tm, 4 * d), row), col, full, full, full, col],
        out_shape=[_sds((t, 4 * d), BF16), _sds((t, 1), F32), _sds((t, d), BF16), _sds((t, d), BF16), _sds((t, d), F32),
                   _sds((t, 1), F32)],
        scratch_shapes=[pltpu.VMEM((tm, 2 * d), F32)],
        compiler_params=_cp(), name="gmlp_fwd")(x, w_in, vg, vb, wm, bs_col, w_out, rxh, rg, rb, g, b)


def _mm_back(pairs, wt, res, after, name):
    t = pairs[0][0].shape[0]
    k = wt.shape[1]
    tm = min(ROW_TILE, t)
    n = len(pairs)

    def body(after_ref, *refs):
        a_refs, w_ref, res_ref, o_ref = refs[:n], refs[n], refs[n + 1], refs[n + 2]
        dx = ALPHA * res_ref[...]
        for a_ref, (_, lo, hi) in zip(a_refs, pairs):
            dx = dx + jnp.dot(a_ref[...].astype(BF16), w_ref[lo:hi, :], preferred_element_type=F32)
        o_ref[...] = dx

    row = lambda i: (i, 0)
    return pl.pallas_call(
        body, grid=(t // tm,),
        in_specs=[_ANY_SPEC] + [pl.BlockSpec((tm, a.shape[1]), row) for a, _, _ in pairs]
        + [_resident(wt.shape), pl.BlockSpec((tm, k), row)],
        out_specs=pl.BlockSpec((tm, k), row), out_shape=_sds((t, k), F32),
        compiler_params=_cp(), name=name)(after, *[a for a, _, _ in pairs], wt, res)


def _mm_tn(a, b, name, *, tn, tk=None, tt=None, stack_cols=False, out_dtype=BF16, after=None):
    t, k = a.shape
    n = b.shape[1]
    tk = k if tk is None else tk
    tt = min(REDUCE_TILE if tt is None else tt, t)
    nt = t // tt

    def body(a_ref, b_ref, *rest):
        o_ref, acc_ref = rest[after is not None:]
        s = pl.program_id(2)
        part = lax.dot_general(a_ref[...].astype(BF16), b_ref[...].astype(BF16), TN, preferred_element_type=F32)
        _accumulate(acc_ref, s == 0, part)

        @pl.when(s == nt - 1)
        def _():
            o_ref[...] = acc_ref[...].astype(out_dtype).reshape(o_ref.shape)

    if stack_cols:
        assert tk == k
        out_spec = pl.BlockSpec((1, k, tn), lambda kk, j, s: (j, 0, 0))
        out_shape = _sds((n // tn, k, tn), out_dtype)
    else:
        out_spec = pl.BlockSpec((tk, tn), lambda kk, j, s: (kk, j))
        out_shape = _sds((k, n), out_dtype)
    return pl.pallas_call(
        body, grid=(k // tk, n // tn, nt),
        in_specs=[pl.BlockSpec((tt, tk), lambda kk, j, s: (s, kk)), pl.BlockSpec((tt, tn), lambda kk, j, s: (s, j))]
        + ([_ANY_SPEC] if after is not None else []),
        out_specs=out_spec, out_shape=out_shape,
        scratch_shapes=[pltpu.VMEM((tk, tn), F32)],
        compiler_params=_cp(), name=name)(a, b, *([after] if after is not None else []))


def _ffn_bwd_rows(dz, wo, gu, wi, ln_below, name):
    t, d = dz.shape
    tm = min(FFN_FUSED_ROW_TILE, t)
    hh = HALF_HIDDEN
    xhat, rstd, g = ln_below

    def body(dz_ref, wo_ref, gu_ref, wi_ref, xh_ref, rs_ref, g_ref, dgu_ref, dzb_ref, dg_ref, db_ref):
        first = pl.program_id(0) == 0
        a = dz_ref[...].astype(BF16)
        for c in range(2):
            gs, us = slice(c * hh, (c + 1) * hh), slice(FFN_HIDDEN + c * hh, FFN_HIDDEN + (c + 1) * hh)
            dh = lax.dot_general(a, wo_ref[gs, :], NT, preferred_element_type=F32)
            dgu_ref[:, gs] = (dh * gu_ref[:, gs].astype(F32)).astype(BF16)
            dgu_ref[:, us] = (dh * gu_ref[:, us].astype(F32)).astype(BF16)
        dx = ALPHA * dz_ref[...]
        for j in range(wi_ref.shape[0]):
            dx = dx + lax.dot_general(dgu_ref[:, j * hh:(j + 1) * hh], wi_ref[j], NT, preferred_element_type=F32)
        dzb, dg, db = _ln_bwd(dx, xh_ref[...], rs_ref[...], g_ref[...])
        dzb_ref[...] = dzb
        _accumulate(dg_ref, first, dg)
        _accumulate(db_ref, first, db)

    row = lambda i: (i, 0)
    wide, full = pl.BlockSpec((tm, 2 * FFN_HIDDEN), row), pl.BlockSpec((tm, d), row)
    vec = pl.BlockSpec((1, d), lambda i: (0, 0))
    return pl.pallas_call(
        body, grid=(t // tm,),
        in_specs=[full, _resident(wo.shape), wide, _resident(wi.shape), full, pl.BlockSpec((tm, 1), row),
                  _resident(g.shape)],
        out_specs=[wide, full, vec, vec],
        out_shape=[_sds((t, 2 * FFN_HIDDEN), BF16), _sds((t, d), F32), _sds((1, d), F32), _sds((1, d), F32)],
        compiler_params=_cp(), name=name)(dz, wo, gu, wi, xhat, rstd, g)


def _gmlp_bwd(dz, w_out, saved, rstd_v, vg, vb, wm, bs_col, w_in, ln_below):
    t, d = dz.shape
    d2 = 2 * d
    tm = min(ROW_TILE, t)
    gb = GMLP_BLOCK
    xhat_below, rstd_below, g_below = ln_below

    def body(dz_ref, wo_ref, sv_ref, rs_ref, vg_ref, vb_ref, wm_ref, bs_ref, wi_ref, xh_ref, rsb_ref, gb_ref,
             da_ref, dws_ref, dbs_ref, dvg_ref, dvb_ref, dzb_ref, dg_ref, db_ref, dvln_sc):
        first = pl.program_id(0) == 0
        u = sv_ref[:, :d].astype(F32)
        vhat = sv_ref[:, 2 * d:3 * d].astype(F32)
        rstd = rs_ref[...]
        vln = (vhat * vg_ref[...] + vb_ref[...]).astype(BF16)
        dgate = lax.dot_general(dz_ref[...].astype(BF16), wo_ref[...], NT, preferred_element_type=F32)

        @pl.when(first)
        def _():
            dws_ref[...] = jnp.zeros(dws_ref.shape, F32)
            dbs_ref[...] = jnp.zeros(dbs_ref.shape, F32)

        for blk in range(tm // gb):
            rs = slice(blk * gb, (blk + 1) * gb)
            for gi in range(GMLP_GROUPS):
                cs = slice(gi * gb, (gi + 1) * gb)
                vblk = vln[rs, cs]
                s = jnp.dot(wm_ref[gi], vblk, preferred_element_type=F32) + bs_ref[:, gi:gi + 1]
                dgb = dgate[rs, cs]
                da_ref[rs, cs] = (dgb * s * sv_ref[rs, d + gi * gb:d + (gi + 1) * gb].astype(F32)).astype(BF16)
                ds = dgb * u[rs, cs]
                dsb = ds.astype(BF16)
                dws_ref[gi] += lax.dot_general(dsb, vblk, NT, preferred_element_type=F32)
                dbs_ref[:, gi:gi + 1] += jnp.sum(ds, axis=1, keepdims=True)
                dvln_sc[rs, cs] = lax.dot_general(wm_ref[gi], dsb, TN, preferred_element_type=F32)
        dv, dvg, dvb = _ln_bwd(dvln_sc[...], vhat, rstd, vg_ref[...])
        da_ref[:, d:] = (dv * sv_ref[:, 3 * d:].astype(F32)).astype(BF16)
        _accumulate(dvg_ref, first, dvg)
        _accumulate(dvb_ref, first, dvb)
        dx = ALPHA * dz_ref[...]
        nc = wi_ref.shape[2]
        for j in range(wi_ref.shape[0]):
            dx = dx + lax.dot_general(da_ref[:, j * nc:(j + 1) * nc], wi_ref[j], NT, preferred_element_type=F32)
        dzb, dg, db = _ln_bwd(dx, xh_ref[...], rsb_ref[...], gb_ref[...])
        dzb_ref[...] = dzb
        _accumulate(dg_ref, first, dg)
        _accumulate(db_ref, first, db)

    row = lambda i: (i, 0)
    full, col = pl.BlockSpec((tm, d), row), pl.BlockSpec((tm, 1), row)
    vec = pl.BlockSpec((1, d), lambda i: (0, 0))
    return pl.pallas_call(
        body, grid=(t // tm,),
        in_specs=[full, _resident(w_out.shape), pl.BlockSpec((tm, 4 * d), row), col,
                  _resident(vg.shape), _resident(vb.shape), _resident(wm.shape), _resident(bs_col.shape),
                  _resident(w_in.shape), full, col, _resident(g_below.shape)],
        out_specs=[pl.BlockSpec((tm, d2), row), pl.BlockSpec(wm.shape, lambda i: (0, 0, 0)),
                   pl.BlockSpec(bs_col.shape, lambda i: (0, 0)), vec, vec, full, vec, vec],
        out_shape=[_sds((t, d2), BF16), _sds(wm.shape, F32), _sds(bs_col.shape, F32), _sds((1, d), F32), _sds((1, d), F32),
                   _sds((t, d), F32), _sds((1, d), F32), _sds((1, d), F32)],
        scratch_shapes=[pltpu.VMEM((tm, d), F32)],
        compiler_params=_cp(), name="gmlp_bwd")(dz, w_out, saved, rstd_v, vg, vb, wm, bs_col, w_in, xhat_below,
                                                rstd_below, g_below)


def _conv_bwd(bch, dconv, conv_w):
    t = bch.shape[0]
    tm = min(ROW_TILE, t)
    nb = t // tm
    halo_blocks = tm // SUBLANES
    cw = CONV_WIDTH

    def body(cur_ref, prev_ref, next_ref, dc_ref, dn_ref, w_ref, o_ref, dw_ref):
        i = pl.program_id(0)
        bgate, cgate, hval = cur_ref[:, :cw], cur_ref[:, cw:2 * cw], cur_ref[:, 2 * cw:]
        z = cgate * hval
        zp = jnp.where(i == 0, 0.0, prev_ref[:, cw:2 * cw] * prev_ref[:, 2 * cw:])
        z1, z2 = _shift_down(z, zp)
        w0, w1, w2 = w_ref[0:1, :], w_ref[1:2, :], w_ref[2:3, :]
        dconv = dc_ref[...]
        o_ref[:, :cw] = (dconv * (w0 * z2 + w1 * z1 + w2 * z)).astype(BF16)
        dy = dconv * bgate
        dyn = jnp.where(i == nb - 1, 0.0, dn_ref[...] * next_ref[:, :cw])
        dy1, dy2 = _shift_up(dy, dyn)
        dz = w2 * dy + w1 * dy1 + w0 * dy2
        o_ref[:, cw:2 * cw] = (dz * hval).astype(BF16)
        o_ref[:, 2 * cw:] = (dz * cgate).astype(BF16)

        @pl.when(i == 0)
        def _():
            dw_ref[...] = jnp.zeros(dw_ref.shape, F32)

        for tap, zs in enumerate((z2, z1, z)):
            dw_ref[tap:tap + 1, :] += jnp.sum(dy * zs, axis=0, keepdims=True)

    last_halo = t // SUBLANES - 1
    return pl.pallas_call(
        body, grid=(nb,),
        in_specs=[pl.BlockSpec((tm, BCH), lambda i: (i, 0)),
                  pl.BlockSpec((SUBLANES, BCH), lambda i: (jnp.maximum(i * halo_blocks - 1, 0), 0)),
                  pl.BlockSpec((SUBLANES, BCH), lambda i: (jnp.minimum((i + 1) * halo_blocks, last_halo), 0)),
                  pl.BlockSpec((tm, cw), lambda i: (i, 0)),
                  pl.BlockSpec((SUBLANES, cw), lambda i: (jnp.minimum((i + 1) * halo_blocks, last_halo), 0)),
                  _resident(conv_w.shape)],
        out_specs=[pl.BlockSpec((tm, BCH), lambda i: (i, 0)), pl.BlockSpec((SUBLANES, cw), lambda i: (0, 0))],
        out_shape=[_sds((t, BCH), BF16), _sds((SUBLANES, cw), F32)],
        compiler_params=_cp(), name="conv_bwd")(bch, bch, bch, dconv, dconv, conv_w)


def _attn_bwd_prep(dz, w_out, o, qp, lse_pad, after):
    t = o.shape[0]
    tm = min(ROW_TILE, t)
    hd = HEAD_DIM
    sel_lse = _piece_selector(Q_LSE, -1.0)
    sel_delta = _piece_selector(DO_DELTA, -1.0)
    head_of = jnp.asarray([[1.0 if col == row // hd else 0.0 for col in range(LANES)] for row in range(FOX_WIDTH)], F32)

    def body(after_ref, dz_ref, wo_ref, o_ref, qp_ref, lse_ref, sl_ref, sd_ref, seg_ref, qb_ref, dob_ref, dconv_ref):
        dzb = dz_ref[...].astype(BF16)
        do = lax.dot_general(dzb, wo_ref[:FOX_WIDTH, :], NT, preferred_element_type=F32)
        dconv_ref[...] = lax.dot_general(dzb, wo_ref[FOX_WIDTH:, :], NT, preferred_element_type=F32)
        delta = jnp.dot(o_ref[...].astype(F32) * do, seg_ref[...], precision=HIGHEST, preferred_element_type=F32)
        lse_extra = jnp.dot(_piece_rows(lse_ref[...]), sl_ref[...], preferred_element_type=F32)
        do_extra = jnp.dot(_piece_rows(delta), sd_ref[...], preferred_element_type=F32).astype(BF16)
        for h in range(FOX_HEADS):
            hs = slice(h * hd, (h + 1) * hd)
            dob_ref[h, :, :hd] = do[:, hs].astype(BF16)
            dob_ref[h, :, hd:] = do_extra[:, hs]
            qb_ref[h, :, :hd] = qp_ref[h, :, :hd]
            qb_ref[h, :, hd:] = (qp_ref[h, :, hd:].astype(F32) + lse_extra[:, hs]).astype(BF16)

    row = lambda i: (i, 0)
    row3 = pl.BlockSpec((FOX_HEADS, tm, LANES), lambda i: (0, i, 0))
    half = pl.BlockSpec((tm, FOX_WIDTH), row)
    return pl.pallas_call(
        body, grid=(t // tm,),
        in_specs=[_ANY_SPEC, pl.BlockSpec((tm, dz.shape[1]), row), _resident(w_out.shape), half, row3,
                  pl.BlockSpec((tm, LANES), row), _resident(sel_lse.shape), _resident(sel_delta.shape),
                  _resident(head_of.shape)],
        out_specs=[row3, row3, half],
        out_shape=[_sds((FOX_HEADS, t, LANES), BF16)] * 2 + [_sds((t, FOX_WIDTH), F32)],
        compiler_params=_cp(), name="attn_bwd_prep")(after, dz, w_out, o, qp, lse_pad, sel_lse, sel_delta, head_of)


def _attn_bwd(qb, kp, vp, dob, kt):
    t = qb.shape[1]
    bq = min(ATT_BLOCK, t)
    nq = t // bq
    i_tab, j_tab = _triangle(nq, key_major=True)

    def body(it_ref, jt_ref, q_ref, k_ref, v_ref, do_ref, kt_ref, dqt_ref, dk_ref, dv_ref, dk_sc, dv_sc):
        s = pl.program_id(1)
        i, j = it_ref[s], jt_ref[s]

        @pl.when(s == 0)
        def _():
            dqt_ref[...] = jnp.zeros(dqt_ref.shape, F32)

        @pl.when(i == j)
        def _():
            dk_sc[...] = jnp.zeros(dk_sc.shape, F32)
            dv_sc[...] = jnp.zeros(dv_sc.shape, F32)

        cols = pl.ds(pl.multiple_of(i * bq, bq), bq)

        def sweep(masked):
            def scores(h):
                return (lax.dot_general(k_ref[h], q_ref[h], NT, preferred_element_type=F32),
                        lax.dot_general(v_ref[h], do_ref[h], NT, preferred_element_type=F32))

            def accumulate(h, ptb, dstb):
                dv_sc[h] += jnp.dot(ptb, do_ref[h], preferred_element_type=F32)
                dk_sc[h] += jnp.dot(dstb, q_ref[h], preferred_element_type=F32)
                dqt_ref[h, :, cols] += jnp.dot(kt_ref[h], dstb, preferred_element_type=F32)

            ahead, behind = scores(0), None
            for h in range(ATT_BWD_HEADS):
                st, dpt = ahead
                if h + 1 < ATT_BWD_HEADS:
                    ahead = scores(h + 1)
                if behind is not None:
                    accumulate(*behind)
                if masked:
                    key = lax.broadcasted_iota(jnp.int32, (bq, bq), 0)
                    qry = lax.broadcasted_iota(jnp.int32, (bq, bq), 1)
                    st = jnp.where(key <= qry, st, NEG)
                pt = jnp.exp(st)
                behind = (h, pt.astype(BF16), (pt * dpt).astype(BF16))
            accumulate(*behind)

        @pl.when(i == j)
        def _():
            sweep(True)

        @pl.when(i > j)
        def _():
            sweep(False)

        @pl.when(i == nq - 1)
        def _():
            dk_ref[...] = dk_sc[...]
            dv_ref[...] = dv_sc[...].astype(BF16)

    nh = ATT_BWD_HEADS
    qblk = pl.BlockSpec((nh, bq, LANES), lambda hp, s, it, jt: (hp, it[s], 0))
    kblk = pl.BlockSpec((nh, bq, LANES), lambda hp, s, it, jt: (hp, jt[s], 0))
    grid_spec = pltpu.PrefetchScalarGridSpec(
        num_scalar_prefetch=2, grid=(FOX_HEADS // nh, i_tab.shape[0]),
        in_specs=[qblk, kblk, kblk, qblk, pl.BlockSpec((nh, LANES, bq), lambda hp, s, it, jt: (hp, 0, jt[s]))],
        out_specs=[pl.BlockSpec((nh, LANES, t), lambda hp, s, it, jt: (hp, 0, 0), pipeline_mode=pl.Buffered(1)),
                   kblk, kblk],
        scratch_shapes=[pltpu.VMEM((nh, bq, LANES), F32), pltpu.VMEM((nh, bq, LANES), F32)])
    return pl.pallas_call(body, grid_spec=grid_spec,
                          out_shape=[_sds((FOX_HEADS, LANES, t), F32), _sds((FOX_HEADS, t, LANES), F32),
                                     _sds((FOX_HEADS, t, LANES), BF16)],
                          compiler_params=_cp(ATT_BWD_VMEM_LIMIT), name="attn_bwd")(i_tab, j_tab, qb, kp, vp, dob, kt)


def _attn_unpack(dqt, dkp, dvp):
    t = dkp.shape[1]
    tm = min(ROW_TILE, t)
    hd = HEAD_DIM

    def body(dqt_ref, dk_ref, dv_ref, o_ref, dc_ref):
        for h in range(FOX_HEADS):
            dq = dqt_ref[h].T
            o_ref[:, h * hd:(h + 1) * hd] = (dq[:, :hd] * (hd ** -0.5)).astype(BF16)
            o_ref[:, FOX_WIDTH + h * hd:FOX_WIDTH + (h + 1) * hd] = dk_ref[h, :, :hd].astype(BF16)
            o_ref[:, 2 * FOX_WIDTH + h * hd:2 * FOX_WIDTH + (h + 1) * hd] = dv_ref[h, :, :hd]
            dc_ref[:, h:h + 1] = dq[:, K_ONE:K_ONE + 1] - dk_ref[h, :, Q_ONE:Q_ONE + 1]

    row3 = pl.BlockSpec((FOX_HEADS, tm, LANES), lambda i: (0, i, 0))
    return pl.pallas_call(
        body, grid=(t // tm,),
        in_specs=[pl.BlockSpec((FOX_HEADS, LANES, tm), lambda i: (0, 0, i)), row3, row3],
        out_specs=[pl.BlockSpec((tm, QKV), lambda i: (i, 0)), pl.BlockSpec((tm, FOX_HEADS), lambda i: (i, 0))],
        out_shape=[_sds((t, QKV), BF16), _sds((t, FOX_HEADS), F32)],
        compiler_params=_cp(), name="attn_unpack")(dqt, dkp, dvp)


def _adamw(parts, w, m, v, name, layer=None, into=None):
    nl, r, c = w.shape
    fits = [cand for cand in [*range(SUBLANES, r, SUBLANES), r] if r % cand == 0 and cand * c * 4 <= ADAMW_BLOCK_BYTES]
    tr = max(fits) if fits else r
    npart = len(parts)
    bc1 = 1.0 - ADAM_B1 ** ADAM_STEP
    bc2 = 1.0 - ADAM_B2 ** ADAM_STEP

    def body(*refs):
        p_refs = refs[:npart]
        w_ref, m_ref, v_ref = refs[npart:npart + 3]
        g_ref, d_ref, nm_ref, nv_ref = refs[-4:]
        sums = []
        for p_ref in p_refs:
            acc = p_ref[0, 0].astype(F32)
            for s in range(1, p_ref.shape[0]):
                acc = acc + p_ref[s, 0].astype(F32)
            sums.append(acc)
        g = sums[0]
        for extra in sums[1:]:
            g = g + extra
        nm = ADAM_B1 * m_ref[0] + (1.0 - ADAM_B1) * g
        nv = ADAM_B2 * v_ref[0] + (1.0 - ADAM_B2) * (g * g)
        m_hat = nm / bc1
        v_hat = nv / bc2
        g_ref[0] = g
        d_ref[0] = -ADAM_LR * (m_hat / (jnp.sqrt(v_hat) + ADAM_EPS) + ADAM_WD * w_ref[0])
        nm_ref[0] = nm
        nv_ref[0] = nv

    first = 0 if layer is None else layer
    blk = pl.BlockSpec((1, tr, c), lambda l, i: (first + l, i, 0))
    extra = [] if into is None else list(into)
    return pl.pallas_call(
        body, grid=(nl if layer is None else 1, r // tr),
        in_specs=[pl.BlockSpec((p.shape[0], 1, tr, c), lambda l, i: (0, l, i, 0)) for p in parts] + [blk, blk, blk]
        + [_ANY_SPEC] * len(extra),
        out_specs=[blk] * 4, out_shape=[_sds(w.shape, F32)] * 4,
        input_output_aliases={npart + 3 + k: k for k in range(len(extra))},
        compiler_params=_cp(), name=name)(*parts, w, m, v, *extra)


def _to_rows(a):
    flat = a.reshape(-1)
    pad = (-flat.shape[0]) % LANES
    if pad:
        flat = jnp.concatenate([flat, jnp.zeros((pad,), flat.dtype)])
    return flat.reshape(-1, LANES)


def _ffn_fwd(xin_ln, xin_b, wi, wo, g, b, layer, target=None):
    t, d = xin_b.shape
    tm = min(FFN_FUSED_ROW_TILE, t)
    hh = HALF_HIDDEN
    rxh, rg, rb = xin_ln

    def body(x_ref, wi_ref, wo_ref, rxh_ref, rg_ref, rb_ref, g_ref, b_ref, *rest):
        gu_ref, h_ref = rest[target is not None:][:2]
        a = x_ref[...]
        for c in range(2):
            gs, us = slice(c * hh, (c + 1) * hh), slice(FFN_HIDDEN + c * hh, FFN_HIDDEN + (c + 1) * hh)
            gate = jnp.dot(a, wi_ref[c], preferred_element_type=F32)
            up = jnp.dot(a, wi_ref[2 + c], preferred_element_type=F32)
            sig = _sigmoid(gate)
            silu = gate * sig
            gu_ref[:, gs] = (up * sig * (1.0 + gate * (1.0 - sig))).astype(BF16)
            gu_ref[:, us] = silu.astype(BF16)
            h_ref[:, gs] = (silu * up).astype(BF16)
        z = ALPHA * (rxh_ref[...] * rg_ref[...] + rb_ref[...]) + jnp.dot(h_ref[...], wo_ref[...], preferred_element_type=F32)
        xhat, rstd = _ln_fwd(z)
        if target is None:
            yb_ref, xh_ref, rs_ref = rest[2:]
            yb_ref[...] = (xhat * g_ref[...] + b_ref[...]).astype(BF16)
            xh_ref[...] = xhat
            rs_ref[...] = rstd
            return
        sq_ref, dz_ref, dg_ref, db_ref = rest[3:]
        first = pl.program_id(0) == 0
        err = xhat * g_ref[...] + b_ref[...] - rest[0][...]
        dz, dg, db = _ln_bwd(err * (1.0 / d), xhat, rstd, g_ref[...])
        dz_ref[...] = dz
        _accumulate(sq_ref, first, jnp.sum(err * err, axis=0, keepdims=True))
        _accumulate(dg_ref, first, dg)
        _accumulate(db_ref, first, db)

    row = lambda i: (i, 0)
    full = pl.BlockSpec((tm, d), row)
    vec = _resident(g.shape)
    acc = pl.BlockSpec((1, d), lambda i: (0, 0))
    in_specs = [full, _resident(wi.shape), _resident(wo.shape), full, vec, vec, vec, vec]
    out_specs = [pl.BlockSpec((tm, 2 * FFN_HIDDEN), row), pl.BlockSpec((tm, FFN_HIDDEN), row)]
    out_shape = [_sds((t, 2 * FFN_HIDDEN), BF16), _sds((t, FFN_HIDDEN), BF16)]
    args = [xin_b, wi, wo, rxh, rg, rb, g, b]
    if target is None:
        out_specs += [full, full, pl.BlockSpec((tm, 1), row)]
        out_shape += [_sds((t, d), BF16), _sds((t, d), F32), _sds((t, 1), F32)]
    else:
        in_specs.append(full)
        args.append(target)
        out_specs += [acc, full, acc, acc]
        out_shape += [_sds((1, d), F32), _sds((t, d), F32), _sds((1, d), F32), _sds((1, d), F32)]
    gu, h, *tail = pl.pallas_call(body, grid=(t // tm,), in_specs=in_specs, out_specs=out_specs, out_shape=out_shape,
                                  compiler_params=_cp(), name=f"ffn_fwd_rows_{layer}")(*args)
    if target is None:
        y_b, xhat, rstd = tail
        return y_b, (xin_b, gu, h, xhat, rstd)
    return tail, (xin_b, gu, h)


def _ffn_bwd(dz, saved, wi, wo, ln_below, layer):
    xin_b, gu, h = saved[:3]
    dgu, *below = _ffn_bwd_rows(dz, wo, gu, wi, ln_below, f"ffn_bwd_rows_{layer}")
    g_out = _mm_tn(h, dz, f"ffn_dw_out_{layer}", tn=D_MODEL, tk=HALF_HIDDEN)
    g_in = _mm_tn(xin_b, dgu, f"ffn_dw_in_{layer}", tn=HALF_HIDDEN, stack_cols=True)
    return below, g_in, g_out.reshape(N_CHIPS, FFN_HIDDEN // N_CHIPS, D_MODEL)


def kernel(x, even_w_in, even_b_f, even_conv_w, even_w_out, odd_w_in, odd_v_ln_g, odd_v_ln_b, odd_w_s, odd_b_s, odd_w_out, mix_ln_g, mix_ln_b, ffn_w_in, ffn_w_out, ffn_ln_g, ffn_ln_b, loss_target, m_even_w_in, m_even_b_f, m_even_conv_w, m_even_w_out, m_odd_w_in, m_odd_v_ln_g, m_odd_v_ln_b, m_odd_w_s, m_odd_b_s, m_odd_w_out, m_mix_ln_g, m_mix_ln_b, m_ffn_w_in, m_ffn_w_out, m_ffn_ln_g, m_ffn_ln_b, v_even_w_in, v_even_b_f, v_even_conv_w, v_even_w_out, v_odd_w_in, v_odd_v_ln_g, v_odd_v_ln_b, v_odd_w_s, v_odd_b_s, v_odd_w_out, v_mix_ln_g, v_mix_ln_b, v_ffn_w_in, v_ffn_w_out, v_ffn_ln_g, v_ffn_ln_b):
    t = x.shape[1]
    d = D_MODEL
    chip = 2 * lax.axis_index("x") + lax.axis_index("y")
    x2d = x[0]
    target = loss_target[0]

    small_shard = jnp.concatenate([odd_v_ln_g.reshape(2, LANES), odd_v_ln_b.reshape(2, LANES),
                                   even_conv_w.reshape(CONV_K, LANES), jnp.zeros((1, LANES), F32)], axis=0)
    first = [jnp.swapaxes(even_w_in[0], 0, 1).astype(BF16)]
    second = [even_w_out[0].astype(BF16), small_shard]
    later = [odd_w_in[0].astype(BF16), odd_w_out[0].astype(BF16), ffn_w_in[0].astype(BF16), ffn_w_in[1].astype(BF16),
             ffn_w_out[0].astype(BF16), ffn_w_out[1].astype(BF16)]
    first_h, first_tok = _split_start(first, "gather4", "gather_first_start")
    second_h, second_tok = _split_start(second, "gather4", "gather_second_start", after=first_tok)
    later_h, later_tok = _split_start(later, "gather4", "gather_later_start", after=second_tok)
    (g_ewi,) = _gathered(first_h, "gather_first_wait", later_tok)
    ewi = g_ewi.reshape(EVEN_IN, d)
    w_even_in = jnp.concatenate([ewi[:QKV], ewi[QKV + FOX_HEADS:],
                                 jnp.pad(ewi[QKV:QKV + FOX_HEADS], ((0, LANES - FOX_HEADS), (0, 0)))], axis=0)
    chunk_id = jnp.arange(GMLP_BLOCK) // CHUNK
    gmask = chunk_id[None, :] <= chunk_id[:, None]
    w_spatial = jnp.where(gmask[None], odd_w_s[0], 0.0).astype(BF16)
    bs_col = odd_b_s[0].T
    b_f_col = even_b_f.reshape(FOX_HEADS, 1)
    ln = lambda p, l: p[l:l + 1]

    qkv, bch, fl = _proj(x2d, w_even_in, [(0, QKV, BF16), (QKV, QKV + BCH, F32), (QKV + BCH, EVEN_IN_PAD, F32)], "even_proj")
    fl3 = fl[:, :FOX_HEADS].T.reshape(FOX_HEADS, t // LANES, LANES).transpose(1, 0, 2)
    c3 = _fgate_fwd(fl3, b_f_col)
    c_rows = c3.transpose(1, 0, 2).reshape(FOX_HEADS, t)
    head_lanes = lambda rows: jnp.pad(rows.T, ((0, 0), (0, LANES - FOX_HEADS)))
    qp, kp, vp, kt, vt = _attn_pack(qkv, head_lanes(c_rows))
    attn, lse = _attn_fwd(qp, kp, vt)
    g_ewo, g_small = _gathered(second_h, "gather_second_wait", attn)
    w_even_out = g_ewo.reshape(d, d)
    v_ln_g = g_small[:, 0:2].reshape(1, d)
    v_ln_b = g_small[:, 2:4].reshape(1, d)
    conv_w = g_small[:, 4:7].transpose(1, 0, 2).reshape(CONV_K, CONV_WIDTH)
    conv, x1_b, xh1, rs1 = _even_out(attn, bch, conv_w, w_even_out, x2d, ln(mix_ln_g, 0), ln(mix_ln_b, 0))
    w_odd_in, g_owo, w_fi0, w_fi1, g_fo0, g_fo1 = _gathered(later_h, "gather_later_wait", x1_b)
    w_odd_out = g_owo.reshape(d, d)
    w_ffn_in = [w_fi0, w_fi1]
    w_ffn_out = [g_fo0.reshape(FFN_HIDDEN, d), g_fo1.reshape(FFN_HIDDEN, d)]
    x2_b, ffn0 = _ffn_fwd((xh1, ln(mix_ln_g, 0), ln(mix_ln_b, 0)), x1_b, w_ffn_in[0], w_ffn_out[0],
                          ln(ffn_ln_g, 0), ln(ffn_ln_b, 0), 0)

    sv_odd, rs_odd, gated, x3_b, xh3, rs3 = _gmlp_fwd(
        x2_b, w_odd_in, v_ln_g, v_ln_b, w_spatial, bs_col, w_odd_out, (ffn0[3], ln(ffn_ln_g, 0), ln(ffn_ln_b, 0)),
        ln(mix_ln_g, 1), ln(mix_ln_b, 1))
    (sq, dz4, d_fg1, d_fb1), ffn1 = _ffn_fwd((xh3, ln(mix_ln_g, 1), ln(mix_ln_b, 1)), x3_b, w_ffn_in[1], w_ffn_out[1],
                                             ln(ffn_ln_g, 1), ln(ffn_ln_b, 1), 1, target=target)

    loss = lax.psum(0.5 / d * jnp.sum(sq), ("x", "y", "c"))
    (dz3, d_mg1, d_mb1), gi_f1, go_f1 = _ffn_bwd(dz4, ffn1, w_ffn_in[1], w_ffn_out[1], (xh3, rs3, ln(mix_ln_g, 1)), 1)

    go_odd = _mm_tn(gated, dz3, "odd_dw_out", tn=d).reshape(N_CHIPS, 1, d // N_CHIPS, d)
    da_odd, dws, dbs_col, d_vg, d_vb, dz2, d_fg0, d_fb0 = _gmlp_bwd(
        dz3, w_odd_out, sv_odd, rs_odd, v_ln_g, v_ln_b, w_spatial, bs_col, w_odd_in,
        (ffn0[3], ffn0[4], ln(ffn_ln_g, 0)))
    gi_odd = _mm_tn(x2_b, da_odd, "odd_dw_in", tn=d // 2, stack_cols=True)[:, None]
    (dz1, d_mg0, d_mb0), gi_f0, go_f0 = _ffn_bwd(dz2, ffn0, w_ffn_in[0], w_ffn_out[0], (xh1, rs1, ln(mix_ln_g, 0)), 0)

    sent_early = [gi_odd, go_odd, gi_f0[:, None], gi_f1[:, None], go_f0[:, None], go_f1[:, None]]
    early_h, early_tok = _split_start(sent_early, "scatter4", "scatter_early_start")
    qb, dob, dconv = _attn_bwd_prep(dz1, w_even_out, attn, qp, head_lanes(lse.reshape(FOX_HEADS, t)), early_tok)
    go_even = jnp.concatenate([_mm_tn(attn, dz1, "even_dw_out_attn", tn=d), _mm_tn(conv, dz1, "even_dw_out_conv", tn=d)],
                              axis=0).reshape(N_CHIPS, 1, d // N_CHIPS, d)
    dbch, dconv_w8 = _conv_bwd(bch, dconv, conv_w)
    dqkv, dc_col = _attn_unpack(*_attn_bwd(qb, kp, vp, dob, kt))
    dc3 = dc_col.T.reshape(FOX_HEADS, t // LANES, LANES).transpose(1, 0, 2)
    dfl3, d_bf = _fgate_bwd(dc3, fl3, b_f_col)
    dfl = jnp.concatenate([dfl3.transpose(1, 0, 2).reshape(FOX_HEADS, t).T.astype(BF16),
                           jnp.zeros((t, LANES - FOX_HEADS), BF16)], axis=1)

    dws_masked = jnp.where(gmask[None], dws, 0.0)
    rep_names = ["odd_w_s", "odd_b_s", "mix_ln_g", "mix_ln_b", "ffn_ln_g", "ffn_ln_b", "even_b_f"]
    rep_grads = [dws_masked, dbs_col.T, jnp.concatenate([d_mg0, d_mg1]), jnp.concatenate([d_mb0, d_mb1]),
                 jnp.concatenate([d_fg0, d_fg1]), jnp.concatenate([d_fb0, d_fb1]), d_bf.reshape(1, FOX_HEADS)]
    rep_w = [(odd_w_s, m_odd_w_s, v_odd_w_s), (odd_b_s, m_odd_b_s, v_odd_b_s), (mix_ln_g, m_mix_ln_g, v_mix_ln_g),
             (mix_ln_b, m_mix_ln_b, v_mix_ln_b), (ffn_ln_g, m_ffn_ln_g, v_ffn_ln_g), (ffn_ln_b, m_ffn_ln_b, v_ffn_ln_b),
             (even_b_f, m_even_b_f, v_even_b_f)]
    rep_rows = [_to_rows(gr) for gr in rep_grads]
    n_rep = sum(r.shape[0] for r in rep_rows)
    pad_rep = (-n_rep) % SUBLANES
    dconv_w = dconv_w8[:CONV_K].reshape(CONV_K, N_CHIPS, LANES).transpose(1, 0, 2).reshape(N_CHIPS * CONV_K, LANES)
    packed = jnp.concatenate(rep_rows + [jnp.zeros((pad_rep, LANES), F32), d_vg.reshape(SUBLANES, LANES),
                                         d_vb.reshape(SUBLANES, LANES), dconv_w, jnp.zeros((4, LANES), F32)], axis=0)
    small_h, small_tok = _split_start([packed], "gather8", "gather_small_start")

    swap_h, swap_tok = _split_start(_scattered(early_h, "scatter_early_wait", small_tok), "swap2", "swap_early_start")
    dw_qkv = _mm_tn(dqkv, x2d, "even_dw_qkv", tn=d, tk=QKV // 2, after=swap_tok)
    dw_bch = _mm_tn(dbch, x2d, "even_dw_bch", tn=d, tk=BCH // 2)
    dw_f = _mm_tn(dfl, x2d, "even_dw_f", tn=d)
    gi_even = jnp.concatenate([dw_qkv, dw_f[:FOX_HEADS], dw_bch], axis=0).reshape(N_CHIPS, 1, -1, LANES)
    sent_late = [gi_even, go_even]
    late_h, late_tok = _split_start(sent_late, "scatter4", "scatter_late_start")
    grad_x = _mm_back([(dqkv, 0, QKV), (dbch, QKV, QKV + BCH), (dfl, QKV + BCH, EVEN_IN_PAD)], w_even_in, dz1,
                      late_tok, "even_dx")
    mine, theirs = _split_wait(swap_h, "swap_early_wait", grad_x)
    res = {}
    res["odd_w_in"] = _adamw([mine[0], theirs[0]], odd_w_in, m_odd_w_in, v_odd_w_in, "adamw_odd_w_in")
    res["odd_w_out"] = _adamw([mine[1], theirs[1]], odd_w_out, m_odd_w_out, v_odd_w_out, "adamw_odd_w_out")
    for nm, at, (w, m, v) in (("ffn_w_in", 2, (ffn_w_in, m_ffn_w_in, v_ffn_w_in)),
                              ("ffn_w_out", 4, (ffn_w_out, m_ffn_w_out, v_ffn_w_out))):
        upper = _adamw([mine[at + 1], theirs[at + 1]], w, m, v, f"adamw_{nm}_1", layer=1)
        res[nm] = _adamw([mine[at], theirs[at]], w, m, v, f"adamw_{nm}_0", layer=0, into=upper)
    mine_late = _scattered(late_h, "scatter_late_wait", res["ffn_w_out"][0])
    theirs_late = _exchange(mine_late, "swap2", "swap_late")
    rows = lambda a: jnp.swapaxes(a, 1, 2).reshape(1, -1, LANES)
    back = lambda a: jnp.swapaxes(a.reshape(1, EVEN_IN // N_CHIPS, d), 1, 2)
    res["even_w_in"] = [back(o) for o in _adamw([mine_late[0], theirs_late[0]], rows(even_w_in), rows(m_even_w_in),
                                                rows(v_even_w_in), "adamw_even_w_in")]
    res["even_w_out"] = _adamw([mine_late[1], theirs_late[1]], even_w_out, m_even_w_out, v_even_w_out,
                               "adamw_even_w_out")
    (packed,), (gathered,) = _split_wait(small_h, "gather_small_wait", theirs_late[0])
    gathered = lax.dynamic_update_index_in_dim(gathered, packed, 4 * lax.axis_index("x") + 2 * lax.axis_index("y")
                                               + lax.axis_index("c"), 0)

    base = n_rep + pad_rep
    own_rows = jnp.concatenate([
        lax.dynamic_slice_in_dim(gathered, base + 2 * chip, 2, axis=1),
        lax.dynamic_slice_in_dim(gathered, base + SUBLANES + 2 * chip, 2, axis=1),
        lax.dynamic_slice_in_dim(gathered, base + 2 * SUBLANES + CONV_K * chip, CONV_K, axis=1),
        jnp.zeros((N_DEV, 1, LANES), F32)], axis=1)
    small_parts = jnp.concatenate([gathered[:, :base], own_rows], axis=1)[:, None]

    def pack_small(get):
        rows = [_to_rows(get(tw)) for tw in rep_w] + [jnp.zeros((pad_rep, LANES), F32)]
        rows += [get(sh).reshape(-1, LANES) for sh in ((odd_v_ln_g, m_odd_v_ln_g, v_odd_v_ln_g),
                                                       (odd_v_ln_b, m_odd_v_ln_b, v_odd_v_ln_b),
                                                       (even_conv_w, m_even_conv_w, v_even_conv_w))]
        return jnp.concatenate(rows + [jnp.zeros((1, LANES), F32)], axis=0)[None]

    small_out = _adamw([small_parts], pack_small(lambda tw: tw[0]), pack_small(lambda tw: tw[1]),
                       pack_small(lambda tw: tw[2]), "adamw_small")

    def unpack_small(rows3):
        rows = rows3[0]
        out, off = {}, 0
        for nm, (w, _, _), r in zip(rep_names, rep_w, rep_rows):
            out[nm] = rows[off:off + r.shape[0]].reshape(-1)[:w.size].reshape(w.shape)
            off += r.shape[0]
        off += pad_rep
        out["odd_v_ln_g"] = rows[off:off + 2].reshape(odd_v_ln_g.shape)
        out["odd_v_ln_b"] = rows[off + 2:off + 4].reshape(odd_v_ln_b.shape)
        out["even_conv_w"] = rows[off + 4:off + 4 + CONV_K].reshape(even_conv_w.shape)
        return out

    small = [unpack_small(o) for o in small_out]
    order = ["even_w_in", "even_b_f", "even_conv_w", "even_w_out", "odd_w_in", "odd_v_ln_g", "odd_v_ln_b", "odd_w_s",
             "odd_b_s", "odd_w_out", "mix_ln_g", "mix_ln_b", "ffn_w_in", "ffn_w_out", "ffn_ln_g", "ffn_ln_b"]
    outs = [loss, grad_x[None]]
    for kind in range(4):
        for nm in order:
            outs.append(res[nm][kind] if nm in res else small[kind][nm])
    return tuple(outs)
```

```python
import math

import jax
import jax.numpy as jnp
from jax import lax
from jax.experimental import pallas as pl
from jax.experimental.pallas import tpu as pltpu

F32 = jnp.float32
BF16 = jnp.bfloat16

D_MODEL = 1024
FOX_HEADS = 8
HEAD_DIM = 64
FOX_WIDTH = FOX_HEADS * HEAD_DIM
CONV_WIDTH = 512
CONV_K = 3
QKV = 3 * FOX_WIDTH
BCH = 3 * CONV_WIDTH
EVEN_IN = QKV + FOX_HEADS + BCH
EVEN_IN_PAD = QKV + BCH + 128
GMLP_BLOCK = 128
GMLP_GROUPS = 8
CHUNK = 64
FFN_HIDDEN = 2816
HALF_HIDDEN = FFN_HIDDEN // 2
ALPHA = 4.0 ** 0.25
LN_EPS = 1e-5
ADAM_LR = 0.001
ADAM_B1 = 0.9
ADAM_B2 = 0.999
ADAM_EPS = 1e-08
ADAM_WD = 0.01
ADAM_STEP = 10
N_CHIPS = 4
N_DEV = 8
LANES = 128
SUBLANES = 8
ROW_TILE = 512
FFN_FUSED_ROW_TILE = 256
REDUCE_TILE = 2048
ATT_BLOCK = 512
ATT_FWD_HEADS = 8
ATT_BWD_HEADS = 8
ADAMW_BLOCK_BYTES = 2 ** 20
VMEM_LIMIT = 56 * 2 ** 20
ATT_BWD_VMEM_LIMIT = 60 * 2 ** 20
NEG = -1e30
MESH = pl.DeviceIdType.MESH
HIGHEST = lax.Precision.HIGHEST
Q_C, Q_ONE, Q_LSE = 64, 67, 70
K_ONE, K_C, K_ONE2 = 64, 67, 70
V_ONE = 64
DO_DELTA = 65
NT = (((1,), (1,)), ((), ()))
TN = (((0,), (0,)), ((), ()))


def _cp(limit=VMEM_LIMIT):
    return pltpu.CompilerParams(vmem_limit_bytes=limit)


def _resident(shape):
    zeros = (0,) * len(shape)
    return pl.BlockSpec(shape, lambda *_: zeros, pipeline_mode=pl.Buffered(1))


def _sds(shape, dtype):
    return jax.ShapeDtypeStruct(tuple(shape), dtype)


_MASKS = {
    "gather4": [(1, 0, 0), (0, 1, 0), (1, 1, 0)],
    "scatter4": [(1, 0, 0), (0, 1, 0), (1, 1, 0)],
    "swap2": [(0, 0, 1)],
    "gather8": [(0, 0, 1), (0, 1, 0), (0, 1, 1), (1, 0, 0), (1, 0, 1), (1, 1, 0), (1, 1, 1)],
}


def _exchange(arrs, mode, name):
    n = len(arrs)
    masks = _MASKS[mode]
    npeer = len(masks)
    lead = {"gather4": N_CHIPS, "gather8": N_DEV}.get(mode)
    out_shapes = [_sds(((lead,) if lead else ()) + a.shape, a.dtype) for a in arrs]

    def body(*refs):
        ins, outs = refs[:n], refs[n:2 * n]
        send_sems, recv_sems, loc_sems = refs[2 * n:]
        x, y, c = lax.axis_index("x"), lax.axis_index("y"), lax.axis_index("c")
        chip, dev = 2 * x + y, 4 * x + 2 * y + c
        sends, recvs, locs = [], [], []
        for k in range(n):
            if mode == "gather4":
                locs.append(pltpu.make_async_copy(ins[k], outs[k].at[chip], loc_sems.at[k]))
            elif mode == "scatter4":
                locs.append(pltpu.make_async_copy(ins[k].at[chip], outs[k].at[chip], loc_sems.at[k]))
            elif mode == "gather8":
                locs.append(pltpu.make_async_copy(ins[k], outs[k].at[dev], loc_sems.at[k]))
        for cp in locs:
            cp.start()
        for k in range(n):
            for j, (dx, dy, dc) in enumerate(masks):
                px = 1 - x if dx else x
                py = 1 - y if dy else y
                pc = 1 - c if dc else c
                pchip, pdev = 2 * px + py, 4 * px + 2 * py + pc
                if mode == "gather4":
                    src, dst, land = ins[k], outs[k].at[chip], outs[k].at[pchip]
                elif mode == "scatter4":
                    src, dst, land = ins[k].at[pchip], outs[k].at[chip], outs[k].at[pchip]
                elif mode == "swap2":
                    src, dst, land = ins[k], outs[k], outs[k]
                else:
                    src, dst, land = ins[k], outs[k].at[dev], outs[k].at[pdev]
                s = k * npeer + j
                kw = dict(send_sem=send_sems.at[s], recv_sem=recv_sems.at[s], device_id=(px, py, pc),
                          device_id_type=MESH)
                cp = pltpu.make_async_remote_copy(src_ref=src, dst_ref=dst, **kw)
                cp.start()
                sends.append(cp)
                recvs.append(pltpu.make_async_remote_copy(src_ref=src, dst_ref=land, **kw))
        for cp in recvs:
            cp.wait_recv()
        for cp in sends:
            cp.wait_send()
        for cp in locs:
            cp.wait()

    any_spec = pl.BlockSpec(memory_space=pl.ANY)
    outs = pl.pallas_call(
        body,
        out_shape=out_shapes,
        in_specs=[any_spec] * n,
        out_specs=[any_spec] * n,
        scratch_shapes=[pltpu.SemaphoreType.DMA((n * npeer,)), pltpu.SemaphoreType.DMA((n * npeer,)),
                        pltpu.SemaphoreType.DMA((max(n, 1),))],
        name=name,
    )(*arrs)
    return list(outs)


_HBM_SPEC = pl.BlockSpec(memory_space=pltpu.HBM)
_SEM_SPEC = pl.BlockSpec(memory_space=pltpu.SEMAPHORE)
_ANY_SPEC = pl.BlockSpec(memory_space=pl.ANY)
_EFFECT = pltpu.SideEffectType.DATAFLOW_SIDE_EFFECTING


def _split_copies(mode, ins, lands, send_sems, recv_sems):
    x, y, c = lax.axis_index("x"), lax.axis_index("y"), lax.axis_index("c")
    chip, dev = 2 * x + y, 4 * x + 2 * y + c
    masks = _MASKS[mode]
    out = []
    for k in range(len(ins)):
        for j, (dx, dy, dc) in enumerate(masks):
            px = 1 - x if dx else x
            py = 1 - y if dy else y
            pc = 1 - c if dc else c
            pchip, pdev = 2 * px + py, 4 * px + 2 * py + pc
            if mode == "gather4":
                src, dst, land = ins[k], lands[k].at[chip], lands[k].at[pchip]
            elif mode == "scatter4":
                src, dst, land = ins[k].at[pchip], lands[k].at[chip], lands[k].at[pchip]
            elif mode == "swap2":
                src, dst, land = ins[k], lands[k], lands[k]
            else:
                src, dst, land = ins[k], lands[k].at[dev], lands[k].at[pdev]
            s = k * len(masks) + j
            kw = dict(send_sem=send_sems.at[s], recv_sem=recv_sems.at[s], device_id=(px, py, pc), device_id_type=MESH)
            out.append((pltpu.make_async_remote_copy(src_ref=src, dst_ref=dst, **kw),
                        pltpu.make_async_remote_copy(src_ref=src, dst_ref=land, **kw)))
    return out


def _split_start(arrs, mode, name, after=None):
    n = len(arrs)
    nsem = n * len(_MASKS[mode])
    lead = {"gather4": (N_CHIPS,), "gather8": (N_DEV,)}.get(mode, ())
    land_shapes = [lead + a.shape for a in arrs]

    def body(*refs):
        ins, lands = refs[:n], refs[n:2 * n]
        outs = refs[2 * n + (after is not None):]
        for start, _ in _split_copies(mode, ins, lands, outs[0], outs[1]):
            start.start()
        outs[-1][...] = jnp.zeros(outs[-1].shape, F32)

    srcs = [pltpu.with_memory_space_constraint(a, pltpu.HBM) for a in arrs]
    empties = [pltpu.with_memory_space_constraint(lax.empty(s, a.dtype), pltpu.HBM) for s, a in zip(land_shapes, arrs)]
    res = pl.pallas_call(
        body, name=name,
        out_shape=(pltpu.SemaphoreType.DMA((nsem,)), pltpu.SemaphoreType.DMA((nsem,)),
                   *[pltpu.HBM(a.shape, a.dtype) for a in arrs],
                   *[pltpu.HBM(s, a.dtype) for s, a in zip(land_shapes, arrs)],
                   _sds((SUBLANES, LANES), F32)),
        in_specs=[_HBM_SPEC] * (2 * n) + ([_ANY_SPEC] if after is not None else []),
        out_specs=(_SEM_SPEC, _SEM_SPEC, *[_HBM_SPEC] * (2 * n), pl.BlockSpec(memory_space=pltpu.VMEM)),
        input_output_aliases={k: 2 + k for k in range(2 * n)},
        compiler_params=pltpu.CompilerParams(has_side_effects=_EFFECT),
    )(*srcs, *empties, *([after] if after is not None else []))
    return dict(mode=mode, n=n, sems=res[:2], bufs=res[2:2 + 2 * n]), res[-1]


def _split_wait(handle, name, after):
    n, mode = handle["n"], handle["mode"]

    def body(*refs):
        ins, lands = refs[:n], refs[n:2 * n]
        send_sems, recv_sems = refs[2 * n], refs[2 * n + 1]
        for _, arrival in _split_copies(mode, ins, lands, send_sems, recv_sems):
            arrival.wait_send()
            arrival.wait_recv()

    bufs = handle["bufs"]
    res = pl.pallas_call(
        body, name=name,
        out_shape=tuple(pltpu.HBM(b.shape, b.dtype) for b in bufs),
        in_specs=[_HBM_SPEC] * (2 * n) + [_SEM_SPEC, _SEM_SPEC, _ANY_SPEC],
        out_specs=tuple([_HBM_SPEC] * (2 * n)),
        input_output_aliases={k: k for k in range(2 * n)},
        compiler_params=pltpu.CompilerParams(has_side_effects=_EFFECT),
    )(*bufs, *handle["sems"], after)
    return list(res[:n]), list(res[n:])


def _with_own(landed, own):
    chip = 2 * lax.axis_index("x") + lax.axis_index("y")
    return lax.dynamic_update_index_in_dim(landed, own, chip, 0)


def _gathered(handle, name, after):
    sent, landed = _split_wait(handle, name, after)
    return [_with_own(g, own) for g, own in zip(landed, sent)]


def _scattered(handle, name, after):
    chip = 2 * lax.axis_index("x") + lax.axis_index("y")
    sent, landed = _split_wait(handle, name, after)
    return [_with_own(r, lax.dynamic_index_in_dim(g, chip, 0, keepdims=False)) for r, g in zip(landed, sent)]


def _sigmoid(x):
    return 0.5 * jnp.tanh(0.5 * x) + 0.5


def _log_sigmoid(x):
    e = jnp.exp(-jnp.abs(x))
    log1p = jnp.where(e < 1e-2, e * (1.0 - e * (0.5 - e * (1.0 / 3.0))), jnp.log(1.0 + e))
    return jnp.minimum(x, 0.0) - log1p


def _ln_fwd(z):
    mu = jnp.mean(z, axis=-1, keepdims=True)
    zc = z - mu
    var = jnp.mean(zc * zc, axis=-1, keepdims=True)
    rstd = lax.rsqrt(var + LN_EPS)
    return zc * rstd, rstd


def _ln_bwd(dy, xhat, rstd, g):
    dxh = dy * g
    m1 = jnp.mean(dxh, axis=-1, keepdims=True)
    m2 = jnp.mean(dxh * xhat, axis=-1, keepdims=True)
    dz = rstd * (dxh - m1 - xhat * m2)
    return dz, jnp.sum(dy * xhat, axis=0, keepdims=True), jnp.sum(dy, axis=0, keepdims=True)


def _shift_down(z, halo):
    r = lax.broadcasted_iota(jnp.int32, z.shape, 0)
    z1 = jnp.where(r == 0, halo[7:8, :], pltpu.roll(z, 1, 0))
    z2 = jnp.where(r == 0, halo[6:7, :], jnp.where(r == 1, halo[7:8, :], pltpu.roll(z, 2, 0)))
    return z1, z2


def _shift_up(z, halo):
    n = z.shape[0]
    r = lax.broadcasted_iota(jnp.int32, z.shape, 0)
    z1 = jnp.where(r == n - 1, halo[0:1, :], pltpu.roll(z, n - 1, 0))
    z2 = jnp.where(r == n - 1, halo[1:2, :], jnp.where(r == n - 2, halo[0:1, :], pltpu.roll(z, n - 2, 0)))
    return z1, z2


def _triangle_ones(prefix):
    r = lax.broadcasted_iota(jnp.int32, (LANES, LANES), 0)
    c = lax.broadcasted_iota(jnp.int32, (LANES, LANES), 1)
    return ((r <= c) if prefix else (r >= c)).astype(F32)


def _same_head_chunks(rows, earlier):
    r = lax.broadcasted_iota(jnp.int32, (rows, rows), 0)
    c = lax.broadcasted_iota(jnp.int32, (rows, rows), 1)
    same = r % FOX_HEADS == c % FOX_HEADS
    return jnp.logical_and(same, (c < r) if earlier else (c > r)).astype(F32)


def _accumulate(ref, first, value):
    @pl.when(first)
    def _():
        ref[...] = value

    @pl.when(jnp.logical_not(first))
    def _():
        ref[...] += value


def _proj(x, wt, splits, name):
    t, k = x.shape
    tm = min(ROW_TILE, t)
    w = wt

    def body(x_ref, w_ref, *outs):
        a = x_ref[...].astype(BF16)
        for (lo, hi, dt), o in zip(splits, outs):
            o[...] = lax.dot_general(a, w_ref[lo:hi, :], NT, preferred_element_type=F32).astype(dt)

    return pl.pallas_call(
        body, grid=(t // tm,),
        in_specs=[pl.BlockSpec((tm, k), lambda i: (i, 0)), _resident(w.shape)],
        out_specs=[pl.BlockSpec((tm, hi - lo), lambda i: (i, 0)) for lo, hi, _ in splits],
        out_shape=[_sds((t, hi - lo), dt) for lo, hi, dt in splits],
        compiler_params=_cp(), name=name)(x, w)


def _fgate_fwd(fl3, b_f):
    nc = fl3.shape[0]
    rows = nc * FOX_HEADS

    def body(f_ref, b_ref, c_ref):
        within = jnp.dot(_log_sigmoid(f_ref[...] + b_ref[...]), _triangle_ones(True), precision=HIGHEST,
                         preferred_element_type=F32)
        totals = jnp.broadcast_to(within[:, LANES - 1:LANES], within.shape)
        c_ref[...] = within + jnp.dot(_same_head_chunks(rows, earlier=True), totals, precision=HIGHEST,
                                      preferred_element_type=F32)

    c2 = pl.pallas_call(body, out_shape=_sds((rows, LANES), F32), name="fgate_fwd")(
        fl3.reshape(rows, LANES), jnp.tile(b_f, (nc, 1)))
    return c2.reshape(fl3.shape)


def _fgate_bwd(dc3, fl3, b_f):
    nc = fl3.shape[0]
    rows = nc * FOX_HEADS

    def body(dc_ref, f_ref, b_ref, df_ref, db_ref):
        within = jnp.dot(dc_ref[...], _triangle_ones(False), precision=HIGHEST, preferred_element_type=F32)
        totals = jnp.broadcast_to(within[:, 0:1], within.shape)
        dlf = within + jnp.dot(_same_head_chunks(rows, earlier=False), totals, precision=HIGHEST,
                               preferred_element_type=F32)
        df = dlf * (1.0 - _sigmoid(f_ref[...] + b_ref[...]))
        df_ref[...] = df
        head = lax.broadcasted_iota(jnp.int32, (FOX_HEADS, rows), 0)
        row = lax.broadcasted_iota(jnp.int32, (FOX_HEADS, rows), 1)
        of_head = (row % FOX_HEADS == head).astype(F32)
        per_row = jnp.broadcast_to(jnp.sum(df, axis=1, keepdims=True), df.shape)
        db_ref[...] = jnp.dot(of_head, per_row, precision=HIGHEST, preferred_element_type=F32)[:, 0:1]

    df2, db = pl.pallas_call(body, out_shape=[_sds((rows, LANES), F32), _sds((FOX_HEADS, 1), F32)], name="fgate_bwd")(
        dc3.reshape(rows, LANES), fl3.reshape(rows, LANES), jnp.tile(b_f, (nc, 1)))
    return df2.reshape(fl3.shape), db


def _split3(c):
    hi = c.astype(BF16).astype(F32)
    mid = (c - hi).astype(BF16).astype(F32)
    lo = (c - hi - mid).astype(BF16).astype(F32)
    return hi, mid, lo


PIECE_ONE = 3 * FOX_HEADS


def _piece_rows(values):
    hi, mid, lo = _split3(values)
    lane = lax.broadcasted_iota(jnp.int32, values.shape, 1)
    row = hi + pltpu.roll(mid, FOX_HEADS, 1) + pltpu.roll(lo, 2 * FOX_HEADS, 1) + jnp.where(lane == PIECE_ONE, 1.0, 0.0)
    return row.astype(BF16)


def _piece_selector(start, sign, ones=()):
    sel = [[0.0] * FOX_WIDTH for _ in range(LANES)]
    for h in range(FOX_HEADS):
        for n in range(3):
            sel[n * FOX_HEADS + h][h * HEAD_DIM + start - HEAD_DIM + n] = sign
        for lane in ones:
            sel[PIECE_ONE][h * HEAD_DIM + lane - HEAD_DIM] = 1.0
    return jnp.asarray(sel, BF16)


def _attn_pack(qkv, c_pad):
    t = qkv.shape[0]
    tm = min(ROW_TILE, t)
    hd = HEAD_DIM
    sel_q = _piece_selector(Q_C, 1.0, range(Q_ONE, Q_ONE + 3))
    sel_k = _piece_selector(K_C, -1.0, [*range(K_ONE, K_ONE + 3), *range(K_ONE2, K_ONE2 + 3)])
    sel_v = _piece_selector(HEAD_DIM, 0.0, range(V_ONE, V_ONE + 4))

    def body(x_ref, c_ref, sq_ref, sk_ref, sv_ref, qp_ref, kp_ref, vp_ref, kt_ref, vt_ref):
        pieces = _piece_rows(c_ref[...])
        q_extra = jnp.dot(pieces, sq_ref[...], preferred_element_type=F32).astype(BF16)
        k_extra = jnp.dot(pieces, sk_ref[...], preferred_element_type=F32).astype(BF16)
        v_extra = jnp.dot(pieces, sv_ref[...], preferred_element_type=F32).astype(BF16)
        for h in range(FOX_HEADS):
            hs = slice(h * hd, (h + 1) * hd)
            qp_ref[h, :, :hd] = (x_ref[:, hs].astype(F32) * (hd ** -0.5)).astype(BF16)
            qp_ref[h, :, hd:] = q_extra[:, hs]
            kp_ref[h, :, :hd] = x_ref[:, FOX_WIDTH + h * hd:FOX_WIDTH + (h + 1) * hd]
            kp_ref[h, :, hd:] = k_extra[:, hs]
            vp_ref[h, :, :hd] = x_ref[:, 2 * FOX_WIDTH + h * hd:2 * FOX_WIDTH + (h + 1) * hd]
            vp_ref[h, :, hd:] = v_extra[:, hs]
            kt_ref[h] = kp_ref[h].T
            vt_ref[h] = vp_ref[h].T

    row3 = pl.BlockSpec((FOX_HEADS, tm, LANES), lambda i: (0, i, 0))
    col3 = pl.BlockSpec((FOX_HEADS, LANES, tm), lambda i: (0, 0, i))
    sel = _resident(sel_q.shape)
    return pl.pallas_call(
        body, grid=(t // tm,),
        in_specs=[pl.BlockSpec((tm, QKV), lambda i: (i, 0)), pl.BlockSpec((tm, LANES), lambda i: (i, 0)), sel, sel, sel],
        out_specs=[row3, row3, row3, col3, col3],
        out_shape=[_sds((FOX_HEADS, t, LANES), BF16)] * 3 + [_sds((FOX_HEADS, LANES, t), BF16)] * 2,
        compiler_params=_cp(), name="attn_pack")(qkv, c_pad, sel_q, sel_k, sel_v)


def _triangle(nq, key_major):
    if key_major:
        pairs = [(i, j) for j in range(nq) for i in range(j, nq)]
    else:
        pairs = [(i, j) for i in range(nq) for j in range(i + 1)]
    return jnp.asarray([p[0] for p in pairs], jnp.int32), jnp.asarray([p[1] for p in pairs], jnp.int32)


def _attn_fwd(qp, kp, vt):
    t = qp.shape[1]
    bq = min(ATT_BLOCK, t)
    nq = t // bq
    nh = ATT_FWD_HEADS
    i_tab, j_tab = _triangle(nq, key_major=False)

    def body(it_ref, jt_ref, q_ref, k_ref, vt_ref, o_ref, lse_ref, m_sc, acc_sc):
        s = pl.program_id(1)
        i, j = it_ref[s], jt_ref[s]

        @pl.when(j == 0)
        def _():
            m_sc[...] = jnp.full(m_sc.shape, NEG, F32)
            acc_sc[...] = jnp.zeros(acc_sc.shape, F32)

        def sweep(masked):
            scores = lambda h: lax.dot_general(k_ref[h], q_ref[h], NT, preferred_element_type=F32)

            def accumulate(h, pt, rescale):
                acc_sc[h] = rescale * acc_sc[h] + jnp.dot(vt_ref[h], pt, preferred_element_type=F32)

            ahead, behind = scores(0), None
            for h in range(nh):
                st = ahead
                if h + 1 < nh:
                    ahead = scores(h + 1)
                if behind is not None:
                    accumulate(*behind)
                if masked:
                    key = lax.broadcasted_iota(jnp.int32, (bq, bq), 0)
                    qry = lax.broadcasted_iota(jnp.int32, (bq, bq), 1)
                    st = jnp.where(key <= qry, st, NEG)
                m_prev = m_sc[h]
                m_new = jnp.maximum(m_prev, jnp.max(st, axis=0, keepdims=True))
                behind = (h, jnp.exp(st - m_new).astype(BF16), jnp.exp(m_prev - m_new))
                m_sc[h] = m_new
            accumulate(*behind)

        @pl.when(j < i)
        def _():
            sweep(False)

        @pl.when(j == i)
        def _():
            sweep(True)
            for h in range(nh):
                acc = acc_sc[h]
                denom = acc[V_ONE:V_ONE + 1, :]
                o_ref[:, h * HEAD_DIM:(h + 1) * HEAD_DIM] = (acc[:HEAD_DIM, :] / denom).T.astype(BF16)
                lse_ref[h] = m_sc[h] + jnp.log(denom)

    grid_spec = pltpu.PrefetchScalarGridSpec(
        num_scalar_prefetch=2, grid=(FOX_HEADS // nh, i_tab.shape[0]),
        in_specs=[pl.BlockSpec((nh, bq, LANES), lambda hp, s, it, jt: (hp, it[s], 0)),
                  pl.BlockSpec((nh, bq, LANES), lambda hp, s, it, jt: (hp, jt[s], 0)),
                  pl.BlockSpec((nh, LANES, bq), lambda hp, s, it, jt: (hp, 0, jt[s]))],
        out_specs=[pl.BlockSpec((bq, nh * HEAD_DIM), lambda hp, s, it, jt: (it[s], hp)),
                   pl.BlockSpec((nh, 1, bq), lambda hp, s, it, jt: (hp, 0, it[s]))],
        scratch_shapes=[pltpu.VMEM((nh, 1, bq), F32), pltpu.VMEM((nh, LANES, bq), F32)])
    return pl.pallas_call(body, grid_spec=grid_spec,
                          out_shape=[_sds((t, FOX_WIDTH), BF16), _sds((FOX_HEADS, 1, t), F32)],
                          compiler_params=_cp(), name="attn_fwd")(i_tab, j_tab, qp, kp, vt)


def _even_out(attn, bch, conv_w, w_out, x, g, b):
    t, d = x.shape
    tm = min(ROW_TILE, t)
    halo_blocks = tm // SUBLANES
    cw = CONV_WIDTH

    def body(a_ref, cur_ref, prev_ref, cw_ref, wo_ref, x_ref, g_ref, b_ref, conv_ref, yb_ref, xh_ref, rs_ref):
        i = pl.program_id(0)
        z = cur_ref[:, cw:2 * cw] * cur_ref[:, 2 * cw:]
        zp = jnp.where(i == 0, 0.0, prev_ref[:, cw:2 * cw] * prev_ref[:, 2 * cw:])
        z1, z2 = _shift_down(z, zp)
        conv = (cur_ref[:, :cw] * (cw_ref[0:1, :] * z2 + cw_ref[1:2, :] * z1 + cw_ref[2:3, :] * z)).astype(BF16)
        conv_ref[...] = conv
        pre = (ALPHA * x_ref[...] + jnp.dot(a_ref[...], wo_ref[:FOX_WIDTH, :], preferred_element_type=F32)
               + jnp.dot(conv, wo_ref[FOX_WIDTH:, :], preferred_element_type=F32))
        xhat, rstd = _ln_fwd(pre)
        yb_ref[...] = (xhat * g_ref[...] + b_ref[...]).astype(BF16)
        xh_ref[...] = xhat
        rs_ref[...] = rstd

    row = lambda i: (i, 0)
    full, half = pl.BlockSpec((tm, d), row), pl.BlockSpec((tm, cw), row)
    return pl.pallas_call(
        body, grid=(t // tm,),
        in_specs=[half, pl.BlockSpec((tm, BCH), row),
                  pl.BlockSpec((SUBLANES, BCH), lambda i: (jnp.maximum(i * halo_blocks - 1, 0), 0)),
                  _resident(conv_w.shape), _resident(w_out.shape), full, _resident(g.shape), _resident(b.shape)],
        out_specs=[half, full, full, pl.BlockSpec((tm, 1), row)],
        out_shape=[_sds((t, cw), BF16), _sds((t, d), BF16), _sds((t, d), F32), _sds((t, 1), F32)],
        compiler_params=_cp(), name="even_out")(attn, bch, bch, conv_w, w_out, x, g, b)


def _gmlp_fwd(x, w_in, vg, vb, wm, bs_col, w_out, res_ln, g, b):
    t, d = x.shape
    tm = min(ROW_TILE, t)
    gb = GMLP_BLOCK
    rxh, rg, rb = res_ln

    def body(x_ref, w_ref, vg_ref, vb_ref, wm_ref, bs_ref, wo_ref, rxh_ref, rg_ref, rb_ref, g_ref, b_ref,
             sv_ref, rs_ref, o_ref, yb_ref, xh_ref, rsy_ref, a_sc):
        xb = x_ref[...].astype(BF16)
        nc = w_ref.shape[2]
        for j in range(w_ref.shape[0]):
            a_sc[:, j * nc:(j + 1) * nc] = jnp.dot(xb, w_ref[j], preferred_element_type=F32)
        halves = []
        for half in range(2):
            a = a_sc[:, half * d:(half + 1) * d]
            cdf = 0.5 * (1.0 + lax.erf(a * (2.0 ** -0.5)))
            halves.append(a * cdf)
            slope = cdf + a * (jnp.exp(-0.5 * a * a) * (1.0 / math.sqrt(2.0 * math.pi)))
            sv_ref[:, (2 * half + 1) * d:(2 * half + 2) * d] = slope.astype(BF16)
        u = halves[0]
        vhat, rstd = _ln_fwd(halves[1])
        sv_ref[:, :d] = u.astype(BF16)
        sv_ref[:, 2 * d:3 * d] = vhat.astype(BF16)
        rs_ref[...] = rstd
        vln = (vhat * vg_ref[...] + vb_ref[...]).astype(BF16)
        for blk in range(tm // gb):
            rs = slice(blk * gb, (blk + 1) * gb)
            for gi in range(GMLP_GROUPS):
                cs = slice(gi * gb, (gi + 1) * gb)
                s = jnp.dot(wm_ref[gi], vln[rs, cs], preferred_element_type=F32) + bs_ref[:, gi:gi + 1]
                o_ref[rs, cs] = (u[rs, cs] * s).astype(BF16)
        z = ALPHA * (rxh_ref[...] * rg_ref[...] + rb_ref[...]) + jnp.dot(o_ref[...], wo_ref[...], preferred_element_type=F32)
        xhat, rstd_y = _ln_fwd(z)
        yb_ref[...] = (xhat * g_ref[...] + b_ref[...]).astype(BF16)
        xh_ref[...] = xhat
        rsy_ref[...] = rstd_y

    row = lambda i: (i, 0)
    full, col, vec = pl.BlockSpec((tm, d), row), pl.BlockSpec((tm, 1), row), _resident(g.shape)
    return pl.pallas_call(
        body, grid=(t // tm,),
        in_specs=[full, _resident(w_in.shape), _resident(vg.shape), _resident(vb.shape),
                  _resident(wm.shape), _resident(bs_col.shape), _resident(w_out.shape), full, vec, vec, vec, vec],
        out_specs=[pl.BlockSpec((tm, 4 * d), row), col, full, full, full, col],
        out_shape=[_sds((t, 4 * d), BF16), _sds((t, 1), F32), _sds((t, d), BF16), _sds((t, d), BF16), _sds((t, d), F32),
                   _sds((t, 1), F32)],
        scratch_shapes=[pltpu.VMEM((tm, 2 * d), F32)],
        compiler_params=_cp(), name="gmlp_fwd")(x, w_in, vg, vb, wm, bs_col, w_out, rxh, rg, rb, g, b)


def _mm_back(pairs, wt, res, after, name):
    t = pairs[0][0].shape[0]
    k = wt.shape[1]
    tm = min(ROW_TILE, t)
    n = len(pairs)

    def body(after_ref, *refs):
        a_refs, w_ref, res_ref, o_ref = refs[:n], refs[n], refs[n + 1], refs[n + 2]
        dx = ALPHA * res_ref[...]
        for a_ref, (_, lo, hi) in zip(a_refs, pairs):
            dx = dx + jnp.dot(a_ref[...].astype(BF16), w_ref[lo:hi, :], preferred_element_type=F32)
        o_ref[...] = dx

    row = lambda i: (i, 0)
    return pl.pallas_call(
        body, grid=(t // tm,),
        in_specs=[_ANY_SPEC] + [pl.BlockSpec((tm, a.shape[1]), row) for a, _, _ in pairs]
        + [_resident(wt.shape), pl.BlockSpec((tm, k), row)],
        out_specs=pl.BlockSpec((tm, k), row), out_shape=_sds((t, k), F32),
        compiler_params=_cp(), name=name)(after, *[a for a, _, _ in pairs], wt, res)


def _mm_tn(a, b, name, *, tn, tk=None, tt=None, stack_cols=False, out_dtype=BF16, after=None):
    t, k = a.shape
    n = b.shape[1]
    tk = k if tk is None else tk
    tt = min(REDUCE_TILE if tt is None else tt, t)
    nt = t // tt

    def body(a_ref, b_ref, *rest):
        o_ref, acc_ref = rest[after is not None:]
        s = pl.program_id(2)
        part = lax.dot_general(a_ref[...].astype(BF16), b_ref[...].astype(BF16), TN, preferred_element_type=F32)
        _accumulate(acc_ref, s == 0, part)

        @pl.when(s == nt - 1)
        def _():
            o_ref[...] = acc_ref[...].astype(out_dtype).reshape(o_ref.shape)

    if stack_cols:
        assert tk == k
        out_spec = pl.BlockSpec((1, k, tn), lambda kk, j, s: (j, 0, 0))
        out_shape = _sds((n // tn, k, tn), out_dtype)
    else:
        out_spec = pl.BlockSpec((tk, tn), lambda kk, j, s: (kk, j))
        out_shape = _sds((k, n), out_dtype)
    return pl.pallas_call(
        body, grid=(k // tk, n // tn, nt),
        in_specs=[pl.BlockSpec((tt, tk), lambda kk, j, s: (s, kk)), pl.BlockSpec((tt, tn), lambda kk, j, s: (s, j))]
        + ([_ANY_SPEC] if after is not None else []),
        out_specs=out_spec, out_shape=out_shape,
        scratch_shapes=[pltpu.VMEM((tk, tn), F32)],
        compiler_params=_cp(), name=name)(a, b, *([after] if after is not None else []))


def _ffn_bwd_rows(dz, wo, gu, wi, ln_below, name):
    t, d = dz.shape
    tm = min(FFN_FUSED_ROW_TILE, t)
    hh = HALF_HIDDEN
    xhat, rstd, g = ln_below

    def body(dz_ref, wo_ref, gu_ref, wi_ref, xh_ref, rs_ref, g_ref, dgu_ref, dzb_ref, dg_ref, db_ref):
        first = pl.program_id(0) == 0
        a = dz_ref[...].astype(BF16)
        for c in range(2):
            gs, us = slice(c * hh, (c + 1) * hh), slice(FFN_HIDDEN + c * hh, FFN_HIDDEN + (c + 1) * hh)
            dh = lax.dot_general(a, wo_ref[gs, :], NT, preferred_element_type=F32)
            dgu_ref[:, gs] = (dh * gu_ref[:, gs].astype(F32)).astype(BF16)
            dgu_ref[:, us] = (dh * gu_ref[:, us].astype(F32)).astype(BF16)
        dx = ALPHA * dz_ref[...]
        for j in range(wi_ref.shape[0]):
            dx = dx + lax.dot_general(dgu_ref[:, j * hh:(j + 1) * hh], wi_ref[j], NT, preferred_element_type=F32)
        dzb, dg, db = _ln_bwd(dx, xh_ref[...], rs_ref[...], g_ref[...])
        dzb_ref[...] = dzb
        _accumulate(dg_ref, first, dg)
        _accumulate(db_ref, first, db)

    row = lambda i: (i, 0)
    wide, full = pl.BlockSpec((tm, 2 * FFN_HIDDEN), row), pl.BlockSpec((tm, d), row)
    vec = pl.BlockSpec((1, d), lambda i: (0, 0))
    return pl.pallas_call(
        body, grid=(t // tm,),
        in_specs=[full, _resident(wo.shape), wide, _resident(wi.shape), full, pl.BlockSpec((tm, 1), row),
                  _resident(g.shape)],
        out_specs=[wide, full, vec, vec],
        out_shape=[_sds((t, 2 * FFN_HIDDEN), BF16), _sds((t, d), F32), _sds((1, d), F32), _sds((1, d), F32)],
        compiler_params=_cp(), name=name)(dz, wo, gu, wi, xhat, rstd, g)


def _gmlp_bwd(dz, w_out, saved, rstd_v, vg, vb, wm, bs_col, w_in, ln_below):
    t, d = dz.shape
    d2 = 2 * d
    tm = min(ROW_TILE, t)
    gb = GMLP_BLOCK
    xhat_below, rstd_below, g_below = ln_below

    def body(dz_ref, wo_ref, sv_ref, rs_ref, vg_ref, vb_ref, wm_ref, bs_ref, wi_ref, xh_ref, rsb_ref, gb_ref,
             da_ref, dws_ref, dbs_ref, dvg_ref, dvb_ref, dzb_ref, dg_ref, db_ref, dvln_sc):
        first = pl.program_id(0) == 0
        u = sv_ref[:, :d].astype(F32)
        vhat = sv_ref[:, 2 * d:3 * d].astype(F32)
        rstd = rs_ref[...]
        vln = (vhat * vg_ref[...] + vb_ref[...]).astype(BF16)
        dgate = lax.dot_general(dz_ref[...].astype(BF16), wo_ref[...], NT, preferred_element_type=F32)

        @pl.when(first)
        def _():
            dws_ref[...] = jnp.zeros(dws_ref.shape, F32)
            dbs_ref[...] = jnp.zeros(dbs_ref.shape, F32)

        for blk in range(tm // gb):
            rs = slice(blk * gb, (blk + 1) * gb)
            for gi in range(GMLP_GROUPS):
                cs = slice(gi * gb, (gi + 1) * gb)
                vblk = vln[rs, cs]
                s = jnp.dot(wm_ref[gi], vblk, preferred_element_type=F32) + bs_ref[:, gi:gi + 1]
                dgb = dgate[rs, cs]
                da_ref[rs, cs] = (dgb * s * sv_ref[rs, d + gi * gb:d + (gi + 1) * gb].astype(F32)).astype(BF16)
                ds = dgb * u[rs, cs]
                dsb = ds.astype(BF16)
                dws_ref[gi] += lax.dot_general(dsb, vblk, NT, preferred_element_type=F32)
                dbs_ref[:, gi:gi + 1] += jnp.sum(ds, axis=1, keepdims=True)
                dvln_sc[rs, cs] = lax.dot_general(wm_ref[gi], dsb, TN, preferred_element_type=F32)
        dv, dvg, dvb = _ln_bwd(dvln_sc[...], vhat, rstd, vg_ref[...])
        da_ref[:, d:] = (dv * sv_ref[:, 3 * d:].astype(F32)).astype(BF16)
        _accumulate(dvg_ref, first, dvg)
        _accumulate(dvb_ref, first, dvb)
        dx = ALPHA * dz_ref[...]
        nc = wi_ref.shape[2]
        for j in range(wi_ref.shape[0]):
            dx = dx + lax.dot_general(da_ref[:, j * nc:(j + 1) * nc], wi_ref[j], NT, preferred_element_type=F32)
        dzb, dg, db = _ln_bwd(dx, xh_ref[...], rsb_ref[...], gb_ref[...])
        dzb_ref[...] = dzb
        _accumulate(dg_ref, first, dg)
        _accumulate(db_ref, first, db)

    row = lambda i: (i, 0)
    full, col = pl.BlockSpec((tm, d), row), pl.BlockSpec((tm, 1), row)
    vec = pl.BlockSpec((1, d), lambda i: (0, 0))
    return pl.pallas_call(
        body, grid=(t // tm,),
        in_specs=[full, _resident(w_out.shape), pl.BlockSpec((tm, 4 * d), row), col,
                  _resident(vg.shape), _resident(vb.shape), _resident(wm.shape), _resident(bs_col.shape),
                  _resident(w_in.shape), full, col, _resident(g_below.shape)],
        out_specs=[pl.BlockSpec((tm, d2), row), pl.BlockSpec(wm.shape, lambda i: (0, 0, 0)),
                   pl.BlockSpec(bs_col.shape, lambda i: (0, 0)), vec, vec, full, vec, vec],
        out_shape=[_sds((t, d2), BF16), _sds(wm.shape, F32), _sds(bs_col.shape, F32), _sds((1, d), F32), _sds((1, d), F32),
                   _sds((t, d), F32), _sds((1, d), F32), _sds((1, d), F32)],
        scratch_shapes=[pltpu.VMEM((tm, d), F32)],
        compiler_params=_cp(), name="gmlp_bwd")(dz, w_out, saved, rstd_v, vg, vb, wm, bs_col, w_in, xhat_below,
                                                rstd_below, g_below)


def _conv_bwd(bch, dconv, conv_w):
    t = bch.shape[0]
    tm = min(ROW_TILE, t)
    nb = t // tm
    halo_blocks = tm // SUBLANES
    cw = CONV_WIDTH

    def body(cur_ref, prev_ref, next_ref, dc_ref, dn_ref, w_ref, o_ref, dw_ref):
        i = pl.program_id(0)
        bgate, cgate, hval = cur_ref[:, :cw], cur_ref[:, cw:2 * cw], cur_ref[:, 2 * cw:]
        z = cgate * hval
        zp = jnp.where(i == 0, 0.0, prev_ref[:, cw:2 * cw] * prev_ref[:, 2 * cw:])
        z1, z2 = _shift_down(z, zp)
        w0, w1, w2 = w_ref[0:1, :], w_ref[1:2, :], w_ref[2:3, :]
        dconv = dc_ref[...]
        o_ref[:, :cw] = (dconv * (w0 * z2 + w1 * z1 + w2 * z)).astype(BF16)
        dy = dconv * bgate
        dyn = jnp.where(i == nb - 1, 0.0, dn_ref[...] * next_ref[:, :cw])
        dy1, dy2 = _shift_up(dy, dyn)
        dz = w2 * dy + w1 * dy1 + w0 * dy2
        o_ref[:, cw:2 * cw] = (dz * hval).astype(BF16)
        o_ref[:, 2 * cw:] = (dz * cgate).astype(BF16)

        @pl.when(i == 0)
        def _():
            dw_ref[...] = jnp.zeros(dw_ref.shape, F32)

        for tap, zs in enumerate((z2, z1, z)):
            dw_ref[tap:tap + 1, :] += jnp.sum(dy * zs, axis=0, keepdims=True)

    last_halo = t // SUBLANES - 1
    return pl.pallas_call(
        body, grid=(nb,),
        in_specs=[pl.BlockSpec((tm, BCH), lambda i: (i, 0)),
                  pl.BlockSpec((SUBLANES, BCH), lambda i: (jnp.maximum(i * halo_blocks - 1, 0), 0)),
                  pl.BlockSpec((SUBLANES, BCH), lambda i: (jnp.minimum((i + 1) * halo_blocks, last_halo), 0)),
                  pl.BlockSpec((tm, cw), lambda i: (i, 0)),
                  pl.BlockSpec((SUBLANES, cw), lambda i: (jnp.minimum((i + 1) * halo_blocks, last_halo), 0)),
                  _resident(conv_w.shape)],
        out_specs=[pl.BlockSpec((tm, BCH), lambda i: (i, 0)), pl.BlockSpec((SUBLANES, cw), lambda i: (0, 0))],
        out_shape=[_sds((t, BCH), BF16), _sds((SUBLANES, cw), F32)],
        compiler_params=_cp(), name="conv_bwd")(bch, bch, bch, dconv, dconv, conv_w)


def _attn_bwd_prep(dz, w_out, o, qp, lse_pad, after):
    t = o.shape[0]
    tm = min(ROW_TILE, t)
    hd = HEAD_DIM
    sel_lse = _piece_selector(Q_LSE, -1.0)
    sel_delta = _piece_selector(DO_DELTA, -1.0)
    head_of = jnp.asarray([[1.0 if col == row // hd else 0.0 for col in range(LANES)] for row in range(FOX_WIDTH)], F32)

    def body(after_ref, dz_ref, wo_ref, o_ref, qp_ref, lse_ref, sl_ref, sd_ref, seg_ref, qb_ref, dob_ref, dconv_ref):
        dzb = dz_ref[...].astype(BF16)
        do = lax.dot_general(dzb, wo_ref[:FOX_WIDTH, :], NT, preferred_element_type=F32)
        dconv_ref[...] = lax.dot_general(dzb, wo_ref[FOX_WIDTH:, :], NT, preferred_element_type=F32)
        delta = jnp.dot(o_ref[...].astype(F32) * do, seg_ref[...], precision=HIGHEST, preferred_element_type=F32)
        lse_extra = jnp.dot(_piece_rows(lse_ref[...]), sl_ref[...], preferred_element_type=F32)
        do_extra = jnp.dot(_piece_rows(delta), sd_ref[...], preferred_element_type=F32).astype(BF16)
        for h in range(FOX_HEADS):
            hs = slice(h * hd, (h + 1) * hd)
            dob_ref[h, :, :hd] = do[:, hs].astype(BF16)
            dob_ref[h, :, hd:] = do_extra[:, hs]
            qb_ref[h, :, :hd] = qp_ref[h, :, :hd]
            qb_ref[h, :, hd:] = (qp_ref[h, :, hd:].astype(F32) + lse_extra[:, hs]).astype(BF16)

    row = lambda i: (i, 0)
    row3 = pl.BlockSpec((FOX_HEADS, tm, LANES), lambda i: (0, i, 0))
    half = pl.BlockSpec((tm, FOX_WIDTH), row)
    return pl.pallas_call(
        body, grid=(t // tm,),
        in_specs=[_ANY_SPEC, pl.BlockSpec((tm, dz.shape[1]), row), _resident(w_out.shape), half, row3,
                  pl.BlockSpec((tm, LANES), row), _resident(sel_lse.shape), _resident(sel_delta.shape),
                  _resident(head_of.shape)],
        out_specs=[row3, row3, half],
        out_shape=[_sds((FOX_HEADS, t, LANES), BF16)] * 2 + [_sds((t, FOX_WIDTH), F32)],
        compiler_params=_cp(), name="attn_bwd_prep")(after, dz, w_out, o, qp, lse_pad, sel_lse, sel_delta, head_of)


def _attn_bwd(qb, kp, vp, dob, kt):
    t = qb.shape[1]
    bq = min(ATT_BLOCK, t)
    nq = t // bq
    i_tab, j_tab = _triangle(nq, key_major=True)

    def body(it_ref, jt_ref, q_ref, k_ref, v_ref, do_ref, kt_ref, dqt_ref, dk_ref, dv_ref, dk_sc, dv_sc):
        s = pl.program_id(1)
        i, j = it_ref[s], jt_ref[s]

        @pl.when(s == 0)
        def _():
            dqt_ref[...] = jnp.zeros(dqt_ref.shape, F32)

        @pl.when(i == j)
        def _():
            dk_sc[...] = jnp.zeros(dk_sc.shape, F32)
            dv_sc[...] = jnp.zeros(dv_sc.shape, F32)

        cols = pl.ds(pl.multiple_of(i * bq, bq), bq)

        def sweep(masked):
            def scores(h):
                return (lax.dot_general(k_ref[h], q_ref[h], NT, preferred_element_type=F32),
                        lax.dot_general(v_ref[h], do_ref[h], NT, preferred_element_type=F32))

            def accumulate(h, ptb, dstb):
                dv_sc[h] += jnp.dot(ptb, do_ref[h], preferred_element_type=F32)
                dk_sc[h] += jnp.dot(dstb, q_ref[h], preferred_element_type=F32)
                dqt_ref[h, :, cols] += jnp.dot(kt_ref[h], dstb, preferred_element_type=F32)

            ahead, behind = scores(0), None
            for h in range(ATT_BWD_HEADS):
                st, dpt = ahead
                if h + 1 < ATT_BWD_HEADS:
                    ahead = scores(h + 1)
                if behind is not None:
                    accumulate(*behind)
                if masked:
                    key = lax.broadcasted_iota(jnp.int32, (bq, bq), 0)
                    qry = lax.broadcasted_iota(jnp.int32, (bq, bq), 1)
                    st = jnp.where(key <= qry, st, NEG)
                pt = jnp.exp(st)
                behind = (h, pt.astype(BF16), (pt * dpt).astype(BF16))
            accumulate(*behind)

        @pl.when(i == j)
        def _():
            sweep(True)

        @pl.when(i > j)
        def _():
            sweep(False)

        @pl.when(i == nq - 1)
        def _():
            dk_ref[...] = dk_sc[...]
            dv_ref[...] = dv_sc[...].astype(BF16)

    nh = ATT_BWD_HEADS
    qblk = pl.BlockSpec((nh, bq, LANES), lambda hp, s, it, jt: (hp, it[s], 0))
    kblk = pl.BlockSpec((nh, bq, LANES), lambda hp, s, it, jt: (hp, jt[s], 0))
    grid_spec = pltpu.PrefetchScalarGridSpec(
        num_scalar_prefetch=2, grid=(FOX_HEADS // nh, i_tab.shape[0]),
        in_specs=[qblk, kblk, kblk, qblk, pl.BlockSpec((nh, LANES, bq), lambda hp, s, it, jt: (hp, 0, jt[s]))],
        out_specs=[pl.BlockSpec((nh, LANES, t), lambda hp, s, it, jt: (hp, 0, 0), pipeline_mode=pl.Buffered(1)),
                   kblk, kblk],
        scratch_shapes=[pltpu.VMEM((nh, bq, LANES), F32), pltpu.VMEM((nh, bq, LANES), F32)])
    return pl.pallas_call(body, grid_spec=grid_spec,
                          out_shape=[_sds((FOX_HEADS, LANES, t), F32), _sds((FOX_HEADS, t, LANES), F32),
                                     _sds((FOX_HEADS, t, LANES), BF16)],
                          compiler_params=_cp(ATT_BWD_VMEM_LIMIT), name="attn_bwd")(i_tab, j_tab, qb, kp, vp, dob, kt)


def _attn_unpack(dqt, dkp, dvp):
    t = dkp.shape[1]
    tm = min(ROW_TILE, t)
    hd = HEAD_DIM

    def body(dqt_ref, dk_ref, dv_ref, o_ref, dc_ref):
        for h in range(FOX_HEADS):
            dq = dqt_ref[h].T
            o_ref[:, h * hd:(h + 1) * hd] = (dq[:, :hd] * (hd ** -0.5)).astype(BF16)
            o_ref[:, FOX_WIDTH + h * hd:FOX_WIDTH + (h + 1) * hd] = dk_ref[h, :, :hd].astype(BF16)
            o_ref[:, 2 * FOX_WIDTH + h * hd:2 * FOX_WIDTH + (h + 1) * hd] = dv_ref[h, :, :hd]
            dc_ref[:, h:h + 1] = dq[:, K_ONE:K_ONE + 1] - dk_ref[h, :, Q_ONE:Q_ONE + 1]

    row3 = pl.BlockSpec((FOX_HEADS, tm, LANES), lambda i: (0, i, 0))
    return pl.pallas_call(
        body, grid=(t // tm,),
        in_specs=[pl.BlockSpec((FOX_HEADS, LANES, tm), lambda i: (0, 0, i)), row3, row3],
        out_specs=[pl.BlockSpec((tm, QKV), lambda i: (i, 0)), pl.BlockSpec((tm, FOX_HEADS), lambda i: (i, 0))],
        out_shape=[_sds((t, QKV), BF16), _sds((t, FOX_HEADS), F32)],
        compiler_params=_cp(), name="attn_unpack")(dqt, dkp, dvp)


def _adamw(parts, w, m, v, name, layer=None, into=None):
    nl, r, c = w.shape
    fits = [cand for cand in [*range(SUBLANES, r, SUBLANES), r] if r % cand == 0 and cand * c * 4 <= ADAMW_BLOCK_BYTES]
    tr = max(fits) if fits else r
    npart = len(parts)
    bc1 = 1.0 - ADAM_B1 ** ADAM_STEP
    bc2 = 1.0 - ADAM_B2 ** ADAM_STEP

    def body(*refs):
        p_refs = refs[:npart]
        w_ref, m_ref, v_ref = refs[npart:npart + 3]
        g_ref, d_ref, nm_ref, nv_ref = refs[-4:]
        sums = []
        for p_ref in p_refs:
            acc = p_ref[0, 0].astype(F32)
            for s in range(1, p_ref.shape[0]):
                acc = acc + p_ref[s, 0].astype(F32)
            sums.append(acc)
        g = sums[0]
        for extra in sums[1:]:
            g = g + extra
        nm = ADAM_B1 * m_ref[0] + (1.0 - ADAM_B1) * g
        nv = ADAM_B2 * v_ref[0] + (1.0 - ADAM_B2) * (g * g)
        m_hat = nm / bc1
        v_hat = nv / bc2
        g_ref[0] = g
        d_ref[0] = -ADAM_LR * (m_hat / (jnp.sqrt(v_hat) + ADAM_EPS) + ADAM_WD * w_ref[0])
        nm_ref[0] = nm
        nv_ref[0] = nv

    first = 0 if layer is None else layer
    blk = pl.BlockSpec((1, tr, c), lambda l, i: (first + l, i, 0))
    extra = [] if into is None else list(into)
    return pl.pallas_call(
        body, grid=(nl if layer is None else 1, r // tr),
        in_specs=[pl.BlockSpec((p.shape[0], 1, tr, c), lambda l, i: (0, l, i, 0)) for p in parts] + [blk, blk, blk]
        + [_ANY_SPEC] * len(extra),
        out_specs=[blk] * 4, out_shape=[_sds(w.shape, F32)] * 4,
        input_output_aliases={npart + 3 + k: k for k in range(len(extra))},
        compiler_params=_cp(), name=name)(*parts, w, m, v, *extra)


def _to_rows(a):
    flat = a.reshape(-1)
    pad = (-flat.shape[0]) % LANES
    if pad:
        flat = jnp.concatenate([flat, jnp.zeros((pad,), flat.dtype)])
    return flat.reshape(-1, LANES)


def _ffn_fwd(xin_ln, xin_b, wi, wo, g, b, layer, target=None):
    t, d = xin_b.shape
    tm = min(FFN_FUSED_ROW_TILE, t)
    hh = HALF_HIDDEN
    rxh, rg, rb = xin_ln

    def body(x_ref, wi_ref, wo_ref, rxh_ref, rg_ref, rb_ref, g_ref, b_ref, *rest):
        gu_ref, h_ref = rest[target is not None:][:2]
        a = x_ref[...]
        for c in range(2):
            gs, us = slice(c * hh, (c + 1) * hh), slice(FFN_HIDDEN + c * hh, FFN_HIDDEN + (c + 1) * hh)
            gate = jnp.dot(a, wi_ref[c], preferred_element_type=F32)
            up = jnp.dot(a, wi_ref[2 + c], preferred_element_type=F32)
            sig = _sigmoid(gate)
            silu = gate * sig
            gu_ref[:, gs] = (up * sig * (1.0 + gate * (1.0 - sig))).astype(BF16)
            gu_ref[:, us] = silu.astype(BF16)
            h_ref[:, gs] = (silu * up).astype(BF16)
        z = ALPHA * (rxh_ref[...] * rg_ref[...] + rb_ref[...]) + jnp.dot(h_ref[...], wo_ref[...], preferred_element_type=F32)
        xhat, rstd = _ln_fwd(z)
        if target is None:
            yb_ref, xh_ref, rs_ref = rest[2:]
            yb_ref[...] = (xhat * g_ref[...] + b_ref[...]).astype(BF16)
            xh_ref[...] = xhat
            rs_ref[...] = rstd
            return
        sq_ref, dz_ref, dg_ref, db_ref = rest[3:]
        first = pl.program_id(0) == 0
        err = xhat * g_ref[...] + b_ref[...] - rest[0][...]
        dz, dg, db = _ln_bwd(err * (1.0 / d), xhat, rstd, g_ref[...])
        dz_ref[...] = dz
        _accumulate(sq_ref, first, jnp.sum(err * err, axis=0, keepdims=True))
        _accumulate(dg_ref, first, dg)
        _accumulate(db_ref, first, db)

    row = lambda i: (i, 0)
    full = pl.BlockSpec((tm, d), row)
    vec = _resident(g.shape)
    acc = pl.BlockSpec((1, d), lambda i: (0, 0))
    in_specs = [full, _resident(wi.shape), _resident(wo.shape), full, vec, vec, vec, vec]
    out_specs = [pl.BlockSpec((tm, 2 * FFN_HIDDEN), row), pl.BlockSpec((tm, FFN_HIDDEN), row)]
    out_shape = [_sds((t, 2 * FFN_HIDDEN), BF16), _sds((t, FFN_HIDDEN), BF16)]
    args = [xin_b, wi, wo, rxh, rg, rb, g, b]
    if target is None:
        out_specs += [full, full, pl.BlockSpec((tm, 1), row)]
        out_shape += [_sds((t, d), BF16), _sds((t, d), F32), _sds((t, 1), F32)]
    else:
        in_specs.append(full)
        args.append(target)
        out_specs += [acc, full, acc, acc]
        out_shape += [_sds((1, d), F32), _sds((t, d), F32), _sds((1, d), F32), _sds((1, d), F32)]
    gu, h, *tail = pl.pallas_call(body, grid=(t // tm,), in_specs=in_specs, out_specs=out_specs, out_shape=out_shape,
                                  compiler_params=_cp(), name=f"ffn_fwd_rows_{layer}")(*args)
    if target is None:
        y_b, xhat, rstd = tail
        return y_b, (xin_b, gu, h, xhat, rstd)
    return tail, (xin_b, gu, h)


def _ffn_bwd(dz, saved, wi, wo, ln_below, layer):
    xin_b, gu, h = saved[:3]
    dgu, *below = _ffn_bwd_rows(dz, wo, gu, wi, ln_below, f"ffn_bwd_rows_{layer}")
    g_out = _mm_tn(h, dz, f"ffn_dw_out_{layer}", tn=D_MODEL, tk=HALF_HIDDEN)
    g_in = _mm_tn(xin_b, dgu, f"ffn_dw_in_{layer}", tn=HALF_HIDDEN, stack_cols=True)
    return below, g_in, g_out.reshape(N_CHIPS, FFN_HIDDEN // N_CHIPS, D_MODEL)


def kernel(x, even_w_in, even_b_f, even_conv_w, even_w_out, odd_w_in, odd_v_ln_g, odd_v_ln_b, odd_w_s, odd_b_s, odd_w_out, mix_ln_g, mix_ln_b, ffn_w_in, ffn_w_out, ffn_ln_g, ffn_ln_b, loss_target, m_even_w_in, m_even_b_f, m_even_conv_w, m_even_w_out, m_odd_w_in, m_odd_v_ln_g, m_odd_v_ln_b, m_odd_w_s, m_odd_b_s, m_odd_w_out, m_mix_ln_g, m_mix_ln_b, m_ffn_w_in, m_ffn_w_out, m_ffn_ln_g, m_ffn_ln_b, v_even_w_in, v_even_b_f, v_even_conv_w, v_even_w_out, v_odd_w_in, v_odd_v_ln_g, v_odd_v_ln_b, v_odd_w_s, v_odd_b_s, v_odd_w_out, v_mix_ln_g, v_mix_ln_b, v_ffn_w_in, v_ffn_w_out, v_ffn_ln_g, v_ffn_ln_b):
    t = x.shape[1]
    d = D_MODEL
    chip = 2 * lax.axis_index("x") + lax.axis_index("y")
    x2d = x[0]
    target = loss_target[0]

    small_shard = jnp.concatenate([odd_v_ln_g.reshape(2, LANES), odd_v_ln_b.reshape(2, LANES),
                                   even_conv_w.reshape(CONV_K, LANES), jnp.zeros((1, LANES), F32)], axis=0)
    first = [jnp.swapaxes(even_w_in[0], 0, 1).astype(BF16)]
    second = [even_w_out[0].astype(BF16), small_shard]
    later = [odd_w_in[0].astype(BF16), odd_w_out[0].astype(BF16), ffn_w_in[0].astype(BF16), ffn_w_in[1].astype(BF16),
             ffn_w_out[0].astype(BF16), ffn_w_out[1].astype(BF16)]
    first_h, first_tok = _split_start(first, "gather4", "gather_first_start")
    second_h, second_tok = _split_start(second, "gather4", "gather_second_start", after=first_tok)
    later_h, later_tok = _split_start(later, "gather4", "gather_later_start", after=second_tok)
    (g_ewi,) = _gathered(first_h, "gather_first_wait", later_tok)
    ewi = g_ewi.reshape(EVEN_IN, d)
    w_even_in = jnp.concatenate([ewi[:QKV], ewi[QKV + FOX_HEADS:],
                                 jnp.pad(ewi[QKV:QKV + FOX_HEADS], ((0, LANES - FOX_HEADS), (0, 0)))], axis=0)
    chunk_id = jnp.arange(GMLP_BLOCK) // CHUNK
    gmask = chunk_id[None, :] <= chunk_id[:, None]
    w_spatial = jnp.where(gmask[None], odd_w_s[0], 0.0).astype(BF16)
    bs_col = odd_b_s[0].T
    b_f_col = even_b_f.reshape(FOX_HEADS, 1)
    ln = lambda p, l: p[l:l + 1]

    qkv, bch, fl = _proj(x2d, w_even_in, [(0, QKV, BF16), (QKV, QKV + BCH, F32), (QKV + BCH, EVEN_IN_PAD, F32)], "even_proj")
    fl3 = fl[:, :FOX_HEADS].T.reshape(FOX_HEADS, t // LANES, LANES).transpose(1, 0, 2)
    c3 = _fgate_fwd(fl3, b_f_col)
    c_rows = c3.transpose(1, 0, 2).reshape(FOX_HEADS, t)
    head_lanes = lambda rows: jnp.pad(rows.T, ((0, 0), (0, LANES - FOX_HEADS)))
    qp, kp, vp, kt, vt = _attn_pack(qkv, head_lanes(c_rows))
    attn, lse = _attn_fwd(qp, kp, vt)
    g_ewo, g_small = _gathered(second_h, "gather_second_wait", attn)
    w_even_out = g_ewo.reshape(d, d)
    v_ln_g = g_small[:, 0:2].reshape(1, d)
    v_ln_b = g_small[:, 2:4].reshape(1, d)
    conv_w = g_small[:, 4:7].transpose(1, 0, 2).reshape(CONV_K, CONV_WIDTH)
    conv, x1_b, xh1, rs1 = _even_out(attn, bch, conv_w, w_even_out, x2d, ln(mix_ln_g, 0), ln(mix_ln_b, 0))
    w_odd_in, g_owo, w_fi0, w_fi1, g_fo0, g_fo1 = _gathered(later_h, "gather_later_wait", x1_b)
    w_odd_out = g_owo.reshape(d, d)
    w_ffn_in = [w_fi0, w_fi1]
    w_ffn_out = [g_fo0.reshape(FFN_HIDDEN, d), g_fo1.reshape(FFN_HIDDEN, d)]
    x2_b, ffn0 = _ffn_fwd((xh1, ln(mix_ln_g, 0), ln(mix_ln_b, 0)), x1_b, w_ffn_in[0], w_ffn_out[0],
                          ln(ffn_ln_g, 0), ln(ffn_ln_b, 0), 0)

    sv_odd, rs_odd, gated, x3_b, xh3, rs3 = _gmlp_fwd(
        x2_b, w_odd_in, v_ln_g, v_ln_b, w_spatial, bs_col, w_odd_out, (ffn0[3], ln(ffn_ln_g, 0), ln(ffn_ln_b, 0)),
        ln(mix_ln_g, 1), ln(mix_ln_b, 1))
    (sq, dz4, d_fg1, d_fb1), ffn1 = _ffn_fwd((xh3, ln(mix_ln_g, 1), ln(mix_ln_b, 1)), x3_b, w_ffn_in[1], w_ffn_out[1],
                                             ln(ffn_ln_g, 1), ln(ffn_ln_b, 1), 1, target=target)

    loss = lax.psum(0.5 / d * jnp.sum(sq), ("x", "y", "c"))
    (dz3, d_mg1, d_mb1), gi_f1, go_f1 = _ffn_bwd(dz4, ffn1, w_ffn_in[1], w_ffn_out[1], (xh3, rs3, ln(mix_ln_g, 1)), 1)

    go_odd = _mm_tn(gated, dz3, "odd_dw_out", tn=d).reshape(N_CHIPS, 1, d // N_CHIPS, d)
    da_odd, dws, dbs_col, d_vg, d_vb, dz2, d_fg0, d_fb0 = _gmlp_bwd(
        dz3, w_odd_out, sv_odd, rs_odd, v_ln_g, v_ln_b, w_spatial, bs_col, w_odd_in,
        (ffn0[3], ffn0[4], ln(ffn_ln_g, 0)))
    gi_odd = _mm_tn(x2_b, da_odd, "odd_dw_in", tn=d // 2, stack_cols=True)[:, None]
    (dz1, d_mg0, d_mb0), gi_f0, go_f0 = _ffn_bwd(dz2, ffn0, w_ffn_in[0], w_ffn_out[0], (xh1, rs1, ln(mix_ln_g, 0)), 0)

    sent_early = [gi_odd, go_odd, gi_f0[:, None], gi_f1[:, None], go_f0[:, None], go_f1[:, None]]
    early_h, early_tok = _split_start(sent_early, "scatter4", "scatter_early_start")
    qb, dob, dconv = _attn_bwd_prep(dz1, w_even_out, attn, qp, head_lanes(lse.reshape(FOX_HEADS, t)), early_tok)
    go_even = jnp.concatenate([_mm_tn(attn, dz1, "even_dw_out_attn", tn=d), _mm_tn(conv, dz1, "even_dw_out_conv", tn=d)],
                              axis=0).reshape(N_CHIPS, 1, d // N_CHIPS, d)
    dbch, dconv_w8 = _conv_bwd(bch, dconv, conv_w)
    dqkv, dc_col = _attn_unpack(*_attn_bwd(qb, kp, vp, dob, kt))
    dc3 = dc_col.T.reshape(FOX_HEADS, t // LANES, LANES).transpose(1, 0, 2)
    dfl3, d_bf = _fgate_bwd(dc3, fl3, b_f_col)
    dfl = jnp.concatenate([dfl3.transpose(1, 0, 2).reshape(FOX_HEADS, t).T.astype(BF16),
                           jnp.zeros((t, LANES - FOX_HEADS), BF16)], axis=1)

    dws_masked = jnp.where(gmask[None], dws, 0.0)
    rep_names = ["odd_w_s", "odd_b_s", "mix_ln_g", "mix_ln_b", "ffn_ln_g", "ffn_ln_b", "even_b_f"]
    rep_grads = [dws_masked, dbs_col.T, jnp.concatenate([d_mg0, d_mg1]), jnp.concatenate([d_mb0, d_mb1]),
                 jnp.concatenate([d_fg0, d_fg1]), jnp.concatenate([d_fb0, d_fb1]), d_bf.reshape(1, FOX_HEADS)]
    rep_w = [(odd_w_s, m_odd_w_s, v_odd_w_s), (odd_b_s, m_odd_b_s, v_odd_b_s), (mix_ln_g, m_mix_ln_g, v_mix_ln_g),
             (mix_ln_b, m_mix_ln_b, v_mix_ln_b), (ffn_ln_g, m_ffn_ln_g, v_ffn_ln_g), (ffn_ln_b, m_ffn_ln_b, v_ffn_ln_b),
             (even_b_f, m_even_b_f, v_even_b_f)]
    rep_rows = [_to_rows(gr) for gr in rep_grads]
    n_rep = sum(r.shape[0] for r in rep_rows)
    pad_rep = (-n_rep) % SUBLANES
    dconv_w = dconv_w8[:CONV_K].reshape(CONV_K, N_CHIPS, LANES).transpose(1, 0, 2).reshape(N_CHIPS * CONV_K, LANES)
    packed = jnp.concatenate(rep_rows + [jnp.zeros((pad_rep, LANES), F32), d_vg.reshape(SUBLANES, LANES),
                                         d_vb.reshape(SUBLANES, LANES), dconv_w, jnp.zeros((4, LANES), F32)], axis=0)
    small_h, small_tok = _split_start([packed], "gather8", "gather_small_start")

    swap_h, swap_tok = _split_start(_scattered(early_h, "scatter_early_wait", small_tok), "swap2", "swap_early_start")
    dw_qkv = _mm_tn(dqkv, x2d, "even_dw_qkv", tn=d, tk=QKV // 2, after=swap_tok)
    dw_bch = _mm_tn(dbch, x2d, "even_dw_bch", tn=d, tk=BCH // 2)
    dw_f = _mm_tn(dfl, x2d, "even_dw_f", tn=d)
    gi_even = jnp.concatenate([dw_qkv, dw_f[:FOX_HEADS], dw_bch], axis=0).reshape(N_CHIPS, 1, -1, LANES)
    sent_late = [gi_even, go_even]
    late_h, late_tok = _split_start(sent_late, "scatter4", "scatter_late_start")
    grad_x = _mm_back([(dqkv, 0, QKV), (dbch, QKV, QKV + BCH), (dfl, QKV + BCH, EVEN_IN_PAD)], w_even_in, dz1,
                      late_tok, "even_dx")
    mine, theirs = _split_wait(swap_h, "swap_early_wait", grad_x)
    res = {}
    res["odd_w_in"] = _adamw([mine[0], theirs[0]], odd_w_in, m_odd_w_in, v_odd_w_in, "adamw_odd_w_in")
    res["odd_w_out"] = _adamw([mine[1], theirs[1]], odd_w_out, m_odd_w_out, v_odd_w_out, "adamw_odd_w_out")
    for nm, at, (w, m, v) in (("ffn_w_in", 2, (ffn_w_in, m_ffn_w_in, v_ffn_w_in)),
                              ("ffn_w_out", 4, (ffn_w_out, m_ffn_w_out, v_ffn_w_out))):
        upper = _adamw([mine[at + 1], theirs[at + 1]], w, m, v, f"adamw_{nm}_1", layer=1)
        res[nm] = _adamw([mine[at], theirs[at]], w, m, v, f"adamw_{nm}_0", layer=0, into=upper)
    mine_late = _scattered(late_h, "scatter_late_wait", res["ffn_w_out"][0])
    theirs_late = _exchange(mine_late, "swap2", "swap_late")
    rows = lambda a: jnp.swapaxes(a, 1, 2).reshape(1, -1, LANES)
    back = lambda a: jnp.swapaxes(a.reshape(1, EVEN_IN // N_CHIPS, d), 1, 2)
    res["even_w_in"] = [back(o) for o in _adamw([mine_late[0], theirs_late[0]], rows(even_w_in), rows(m_even_w_in),
                                                rows(v_even_w_in), "adamw_even_w_in")]
    res["even_w_out"] = _adamw([mine_late[1], theirs_late[1]], even_w_out, m_even_w_out, v_even_w_out,
                               "adamw_even_w_out")
    (packed,), (gathered,) = _split_wait(small_h, "gather_small_wait", theirs_late[0])
    gathered = lax.dynamic_update_index_in_dim(gathered, packed, 4 * lax.axis_index("x") + 2 * lax.axis_index("y")
                                               + lax.axis_index("c"), 0)

    base = n_rep + pad_rep
    own_rows = jnp.concatenate([
        lax.dynamic_slice_in_dim(gathered, base + 2 * chip, 2, axis=1),
        lax.dynamic_slice_in_dim(gathered, base + SUBLANES + 2 * chip, 2, axis=1),
        lax.dynamic_slice_in_dim(gathered, base + 2 * SUBLANES + CONV_K * chip, CONV_K, axis=1),
        jnp.zeros((N_DEV, 1, LANES), F32)], axis=1)
    small_parts = jnp.concatenate([gathered[:, :base], own_rows], axis=1)[:, None]

    def pack_small(get):
        rows = [_to_rows(get(tw)) for tw in rep_w] + [jnp.zeros((pad_rep, LANES), F32)]
        rows += [get(sh).reshape(-1, LANES) for sh in ((odd_v_ln_g, m_odd_v_ln_g, v_odd_v_ln_g),
                                                       (odd_v_ln_b, m_odd_v_ln_b, v_odd_v_ln_b),
                                                       (even_conv_w, m_even_conv_w, v_even_conv_w))]
        return jnp.concatenate(rows + [jnp.zeros((1, LANES), F32)], axis=0)[None]

    small_out = _adamw([small_parts], pack_small(lambda tw: tw[0]), pack_small(lambda tw: tw[1]),
                       pack_small(lambda tw: tw[2]), "adamw_small")

    def unpack_small(rows3):
        rows = rows3[0]
        out, off = {}, 0
        for nm, (w, _, _), r in zip(rep_names, rep_w, rep_rows):
            out[nm] = rows[off:off + r.shape[0]].reshape(-1)[:w.size].reshape(w.shape)
            off += r.shape[0]
        off += pad_rep
        out["odd_v_ln_g"] = rows[off:off + 2].reshape(odd_v_ln_g.shape)
        out["odd_v_ln_b"] = rows[off + 2:off + 4].reshape(odd_v_ln_b.shape)
        out["even_conv_w"] = rows[off + 4:off + 4 + CONV_K].reshape(even_conv_w.shape)
        return out

    small = [unpack_small(o) for o in small_out]
    order = ["even_w_in", "even_b_f", "even_conv_w", "even_w_out", "odd_w_in", "odd_v_ln_g", "odd_v_ln_b", "odd_w_s",
             "odd_b_s", "odd_w_out", "mix_ln_g", "mix_ln_b", "ffn_w_in", "ffn_w_out", "ffn_ln_g", "ffn_ln_b"]
    outs = [loss, grad_x[None]]
    for kind in range(4):
        for nm in order:
            outs.append(res[nm][kind] if nm in res else small[kind][nm])
    return tuple(outs)
```

```python
import math

import jax
import jax.numpy as jnp
from jax import lax
from jax.experimental import pallas as pl
from jax.experimental.pallas import tpu as pltpu

F32 = jnp.float32
BF16 = jnp.bfloat16

D_MODEL = 1024
FOX_HEADS = 8
HEAD_DIM = 64
FOX_WIDTH = FOX_HEADS * HEAD_DIM
CONV_WIDTH = 512
CONV_K = 3
QKV = 3 * FOX_WIDTH
BCH = 3 * CONV_WIDTH
EVEN_IN = QKV + FOX_HEADS + BCH
EVEN_IN_PAD = QKV + BCH + 128
GMLP_BLOCK = 128
GMLP_GROUPS = 8
CHUNK = 64
FFN_HIDDEN = 2816
HALF_HIDDEN = FFN_HIDDEN // 2
ALPHA = 4.0 ** 0.25
LN_EPS = 1e-5
ADAM_LR = 0.001
ADAM_B1 = 0.9
ADAM_B2 = 0.999
ADAM_EPS = 1e-08
ADAM_WD = 0.01
ADAM_STEP = 10
N_CHIPS = 4
N_DEV = 8
LANES = 128
SUBLANES = 8
ROW_TILE = 512
FFN_FUSED_ROW_TILE = 256
REDUCE_TILE = 2048
ATT_BLOCK = 512
ATT_FWD_HEADS = 8
ATT_BWD_HEADS = 8
ADAMW_BLOCK_BYTES = 2 ** 20
VMEM_LIMIT = 56 * 2 ** 20
ATT_BWD_VMEM_LIMIT = 60 * 2 ** 20
NEG = -1e30
MESH = pl.DeviceIdType.MESH
HIGHEST = lax.Precision.HIGHEST
Q_C, Q_ONE, Q_LSE = 64, 67, 70
K_ONE, K_C, K_ONE2 = 64, 67, 70
V_ONE = 64
DO_DELTA = 65
NT = (((1,), (1,)), ((), ()))
TN = (((0,), (0,)), ((), ()))


def _cp(limit=VMEM_LIMIT):
    return pltpu.CompilerParams(vmem_limit_bytes=limit)


def _resident(shape):
    zeros = (0,) * len(shape)
    return pl.BlockSpec(shape, lambda *_: zeros, pipeline_mode=pl.Buffered(1))


def _sds(shape, dtype):
    return jax.ShapeDtypeStruct(tuple(shape), dtype)


_MASKS = {
    "gather4": [(1, 0, 0), (0, 1, 0), (1, 1, 0)],
    "scatter4": [(1, 0, 0), (0, 1, 0), (1, 1, 0)],
    "swap2": [(0, 0, 1)],
    "gather8": [(0, 0, 1), (0, 1, 0), (0, 1, 1), (1, 0, 0), (1, 0, 1), (1, 1, 0), (1, 1, 1)],
}


def _exchange(arrs, mode, name):
    n = len(arrs)
    masks = _MASKS[mode]
    npeer = len(masks)
    lead = {"gather4": N_CHIPS, "gather8": N_DEV}.get(mode)
    out_shapes = [_sds(((lead,) if lead else ()) + a.shape, a.dtype) for a in arrs]

    def body(*refs):
        ins, outs = refs[:n], refs[n:2 * n]
        send_sems, recv_sems, loc_sems = refs[2 * n:]
        x, y, c = lax.axis_index("x"), lax.axis_index("y"), lax.axis_index("c")
        chip, dev = 2 * x + y, 4 * x + 2 * y + c
        sends, recvs, locs = [], [], []
        for k in range(n):
            if mode == "gather4":
                locs.append(pltpu.make_async_copy(ins[k], outs[k].at[chip], loc_sems.at[k]))
            elif mode == "scatter4":
                locs.append(pltpu.make_async_copy(ins[k].at[chip], outs[k].at[chip], loc_sems.at[k]))
            elif mode == "gather8":
                locs.append(pltpu.make_async_copy(ins[k], outs[k].at[dev], loc_sems.at[k]))
        for cp in locs:
            cp.start()
        for k in range(n):
            for j, (dx, dy, dc) in enumerate(masks):
                px = 1 - x if dx else x
                py = 1 - y if dy else y
                pc = 1 - c if dc else c
                pchip, pdev = 2 * px + py, 4 * px + 2 * py + pc
                if mode == "gather4":
                    src, dst, land = ins[k], outs[k].at[chip], outs[k].at[pchip]
                elif mode == "scatter4":
                    src, dst, land = ins[k].at[pchip], outs[k].at[chip], outs[k].at[pchip]
                elif mode == "swap2":
                    src, dst, land = ins[k], outs[k], outs[k]
                else:
                    src, dst, land = ins[k], outs[k].at[dev], outs[k].at[pdev]
                s = k * npeer + j
                kw = dict(send_sem=send_sems.at[s], recv_sem=recv_sems.at[s], device_id=(px, py, pc),
                          device_id_type=MESH)
                cp = pltpu.make_async_remote_copy(src_ref=src, dst_ref=dst, **kw)
                cp.start()
                sends.append(cp)
                recvs.append(pltpu.make_async_remote_copy(src_ref=src, dst_ref=land, **kw))
        for cp in recvs:
            cp.wait_recv()
        for cp in sends:
            cp.wait_send()
        for cp in locs:
            cp.wait()

    any_spec = pl.BlockSpec(memory_space=pl.ANY)
    outs = pl.pallas_call(
        body,
        out_shape=out_shapes,
        in_specs=[any_spec] * n,
        out_specs=[any_spec] * n,
        scratch_shapes=[pltpu.SemaphoreType.DMA((n * npeer,)), pltpu.SemaphoreType.DMA((n * npeer,)),
                        pltpu.SemaphoreType.DMA((max(n, 1),))],
        name=name,
    )(*arrs)
    return list(outs)


_HBM_SPEC = pl.BlockSpec(memory_space=pltpu.HBM)
_SEM_SPEC = pl.BlockSpec(memory_space=pltpu.SEMAPHORE)
_ANY_SPEC = pl.BlockSpec(memory_space=pl.ANY)
_EFFECT = pltpu.SideEffectType.DATAFLOW_SIDE_EFFECTING


def _split_copies(mode, ins, lands, send_sems, recv_sems):
    x, y, c = lax.axis_index("x"), lax.axis_index("y"), lax.axis_index("c")
    chip, dev = 2 * x + y, 4 * x + 2 * y + c
    masks = _MASKS[mode]
    out = []
    for k in range(len(ins)):
        for j, (dx, dy, dc) in enumerate(masks):
            px = 1 - x if dx else x
            py = 1 - y if dy else y
            pc = 1 - c if dc else c
            pchip, pdev = 2 * px + py, 4 * px + 2 * py + pc
            if mode == "gather4":
                src, dst, land = ins[k], lands[k].at[chip], lands[k].at[pchip]
            elif mode == "scatter4":
                src, dst, land = ins[k].at[pchip], lands[k].at[chip], lands[k].at[pchip]
            elif mode == "swap2":
                src, dst, land = ins[k], lands[k], lands[k]
            else:
                src, dst, land = ins[k], lands[k].at[dev], lands[k].at[pdev]
            s = k * len(masks) + j
            kw = dict(send_sem=send_sems.at[s], recv_sem=recv_sems.at[s], device_id=(px, py, pc), device_id_type=MESH)
            out.append((pltpu.make_async_remote_copy(src_ref=src, dst_ref=dst, **kw),
                        pltpu.make_async_remote_copy(src_ref=src, dst_ref=land, **kw)))
    return out


def _split_start(arrs, mode, name, after=None):
    n = len(arrs)
    nsem = n * len(_MASKS[mode])
    lead = {"gather4": (N_CHIPS,), "gather8": (N_DEV,)}.get(mode, ())
    land_shapes = [lead + a.shape for a in arrs]

    def body(*refs):
        ins, lands = refs[:n], refs[n:2 * n]
        outs = refs[2 * n + (after is not None):]
        for start, _ in _split_copies(mode, ins, lands, outs[0], outs[1]):
            start.start()
        outs[-1][...] = jnp.zeros(outs[-1].shape, F32)

    srcs = [pltpu.with_memory_space_constraint(a, pltpu.HBM) for a in arrs]
    empties = [pltpu.with_memory_space_constraint(lax.empty(s, a.dtype), pltpu.HBM) for s, a in zip(land_shapes, arrs)]
    res = pl.pallas_call(
        body, name=name,
        out_shape=(pltpu.SemaphoreType.DMA((nsem,)), pltpu.SemaphoreType.DMA((nsem,)),
                   *[pltpu.HBM(a.shape, a.dtype) for a in arrs],
                   *[pltpu.HBM(s, a.dtype) for s, a in zip(land_shapes, arrs)],
                   _sds((SUBLANES, LANES), F32)),
        in_specs=[_HBM_SPEC] * (2 * n) + ([_ANY_SPEC] if after is not None else []),
        out_specs=(_SEM_SPEC, _SEM_SPEC, *[_HBM_SPEC] * (2 * n), pl.BlockSpec(memory_space=pltpu.VMEM)),
        input_output_aliases={k: 2 + k for k in range(2 * n)},
        compiler_params=pltpu.CompilerParams(has_side_effects=_EFFECT),
    )(*srcs, *empties, *([after] if after is not None else []))
    return dict(mode=mode, n=n, sems=res[:2], bufs=res[2:2 + 2 * n]), res[-1]


def _split_wait(handle, name, after):
    n, mode = handle["n"], handle["mode"]

    def body(*refs):
        ins, lands = refs[:n], refs[n:2 * n]
        send_sems, recv_sems = refs[2 * n], refs[2 * n + 1]
        for _, arrival in _split_copies(mode, ins, lands, send_sems, recv_sems):
            arrival.wait_send()
            arrival.wait_recv()

    bufs = handle["bufs"]
    res = pl.pallas_call(
        body, name=name,
        out_shape=tuple(pltpu.HBM(b.shape, b.dtype) for b in bufs),
        in_specs=[_HBM_SPEC] * (2 * n) + [_SEM_SPEC, _SEM_SPEC, _ANY_SPEC],
        out_specs=tuple([_HBM_SPEC] * (2 * n)),
        input_output_aliases={k: k for k in range(2 * n)},
        compiler_params=pltpu.CompilerParams(has_side_effects=_EFFECT),
    )(*bufs, *handle["sems"], after)
    return list(res[:n]), list(res[n:])


def _with_own(landed, own):
    chip = 2 * lax.axis_index("x") + lax.axis_index("y")
    return lax.dynamic_update_index_in_dim(landed, own, chip, 0)


def _gathered(handle, name, after):
    sent, landed = _split_wait(handle, name, after)
    return [_with_own(g, own) for g, own in zip(landed, sent)]


def _scattered(handle, name, after):
    chip = 2 * lax.axis_index("x") + lax.axis_index("y")
    sent, landed = _split_wait(handle, name, after)
    return [_with_own(r, lax.dynamic_index_in_dim(g, chip, 0, keepdims=False)) for r, g in zip(landed, sent)]


def _sigmoid(x):
    return 0.5 * jnp.tanh(0.5 * x) + 0.5


def _log_sigmoid(x):
    e = jnp.exp(-jnp.abs(x))
    log1p = jnp.where(e < 1e-2, e * (1.0 - e * (0.5 - e * (1.0 / 3.0))), jnp.log(1.0 + e))
    return jnp.minimum(x, 0.0) - log1p


def _ln_fwd(z):
    mu = jnp.mean(z, axis=-1, keepdims=True)
    zc = z - mu
    var = jnp.mean(zc * zc, axis=-1, keepdims=True)
    rstd = lax.rsqrt(var + LN_EPS)
    return zc * rstd, rstd


def _ln_bwd(dy, xhat, rstd, g):
    dxh = dy * g
    m1 = jnp.mean(dxh, axis=-1, keepdims=True)
    m2 = jnp.mean(dxh * xhat, axis=-1, keepdims=True)
    dz = rstd * (dxh - m1 - xhat * m2)
    return dz, jnp.sum(dy * xhat, axis=0, keepdims=True), jnp.sum(dy, axis=0, keepdims=True)


def _shift_down(z, halo):
    r = lax.broadcasted_iota(jnp.int32, z.shape, 0)
    z1 = jnp.where(r == 0, halo[7:8, :], pltpu.roll(z, 1, 0))
    z2 = jnp.where(r == 0, halo[6:7, :], jnp.where(r == 1, halo[7:8, :], pltpu.roll(z, 2, 0)))
    return z1, z2


def _shift_up(z, halo):
    n = z.shape[0]
    r = lax.broadcasted_iota(jnp.int32, z.shape, 0)
    z1 = jnp.where(r == n - 1, halo[0:1, :], pltpu.roll(z, n - 1, 0))
    z2 = jnp.where(r == n - 1, halo[1:2, :], jnp.where(r == n - 2, halo[0:1, :], pltpu.roll(z, n - 2, 0)))
    return z1, z2


def _triangle_ones(prefix):
    r = lax.broadcasted_iota(jnp.int32, (LANES, LANES), 0)
    c = lax.broadcasted_iota(jnp.int32, (LANES, LANES), 1)
    return ((r <= c) if prefix else (r >= c)).astype(F32)


def _same_head_chunks(rows, earlier):
    r = lax.broadcasted_iota(jnp.int32, (rows, rows), 0)
    c = lax.broadcasted_iota(jnp.int32, (rows, rows), 1)
    same = r % FOX_HEADS == c % FOX_HEADS
    return jnp.logical_and(same, (c < r) if earlier else (c > r)).astype(F32)


def _accumulate(ref, first, value):
    @pl.when(first)
    def _():
        ref[...] = value

    @pl.when(jnp.logical_not(first))
    def _():
        ref[...] += value


def _proj(x, wt, splits, name):
    t, k = x.shape
    tm = min(ROW_TILE, t)
    w = wt

    def body(x_ref, w_ref, *outs):
        a = x_ref[...].astype(BF16)
        for (lo, hi, dt), o in zip(splits, outs):
            o[...] = lax.dot_general(a, w_ref[lo:hi, :], NT, preferred_element_type=F32).astype(dt)

    return pl.pallas_call(
        body, grid=(t // tm,),
        in_specs=[pl.BlockSpec((tm, k), lambda i: (i, 0)), _resident(w.shape)],
        out_specs=[pl.BlockSpec((tm, hi - lo), lambda i: (i, 0)) for lo, hi, _ in splits],
        out_shape=[_sds((t, hi - lo), dt) for lo, hi, dt in splits],
        compiler_params=_cp(), name=name)(x, w)


def _fgate_fwd(fl3, b_f):
    nc = fl3.shape[0]
    rows = nc * FOX_HEADS

    def body(f_ref, b_ref, c_ref):
        within = jnp.dot(_log_sigmoid(f_ref[...] + b_ref[...]), _triangle_ones(True), precision=HIGHEST,
                         preferred_element_type=F32)
        totals = jnp.broadcast_to(within[:, LANES - 1:LANES], within.shape)
        c_ref[...] = within + jnp.dot(_same_head_chunks(rows, earlier=True), totals, precision=HIGHEST,
                                      preferred_element_type=F32)

    c2 = pl.pallas_call(body, out_shape=_sds((rows, LANES), F32), name="fgate_fwd")(
        fl3.reshape(rows, LANES), jnp.tile(b_f, (nc, 1)))
    return c2.reshape(fl3.shape)


def _fgate_bwd(dc3, fl3, b_f):
    nc = fl3.shape[0]
    rows = nc * FOX_HEADS

    def body(dc_ref, f_ref, b_ref, df_ref, db_ref):
        within = jnp.dot(dc_ref[...], _triangle_ones(False), precision=HIGHEST, preferred_element_type=F32)
        totals = jnp.broadcast_to(within[:, 0:1], within.shape)
        dlf = within + jnp.dot(_same_head_chunks(rows, earlier=False), totals, precision=HIGHEST,
                               preferred_element_type=F32)
        df = dlf * (1.0 - _sigmoid(f_ref[...] + b_ref[...]))
        df_ref[...] = df
        head = lax.broadcasted_iota(jnp.int32, (FOX_HEADS, rows), 0)
        row = lax.broadcasted_iota(jnp.int32, (FOX_HEADS, rows), 1)
        of_head = (row % FOX_HEADS == head).astype(F32)
        per_row = jnp.broadcast_to(jnp.sum(df, axis=1, keepdims=True), df.shape)
        db_ref[...] = jnp.dot(of_head, per_row, precision=HIGHEST, preferred_element_type=F32)[:, 0:1]

    df2, db = pl.pallas_call(body, out_shape=[_sds((rows, LANES), F32), _sds((FOX_HEADS, 1), F32)], name="fgate_bwd")(
        dc3.reshape(rows, LANES), fl3.reshape(rows, LANES), jnp.tile(b_f, (nc, 1)))
    return df2.reshape(fl3.shape), db


def _split3(c):
    hi = c.astype(BF16).astype(F32)
    mid = (c - hi).astype(BF16).astype(F32)
    lo = (c - hi - mid).astype(BF16).astype(F32)
    return hi, mid, lo


PIECE_ONE = 3 * FOX_HEADS


def _piece_rows(values):
    hi, mid, lo = _split3(values)
    lane = lax.broadcasted_iota(jnp.int32, values.shape, 1)
    row = hi + pltpu.roll(mid, FOX_HEADS, 1) + pltpu.roll(lo, 2 * FOX_HEADS, 1) + jnp.where(lane == PIECE_ONE, 1.0, 0.0)
    return row.astype(BF16)


def _piece_selector(start, sign, ones=()):
    sel = [[0.0] * FOX_WIDTH for _ in range(LANES)]
    for h in range(FOX_HEADS):
        for n in range(3):
            sel[n * FOX_HEADS + h][h * HEAD_DIM + start - HEAD_DIM + n] = sign
        for lane in ones:
            sel[PIECE_ONE][h * HEAD_DIM + lane - HEAD_DIM] = 1.0
    return jnp.asarray(sel, BF16)


def _attn_pack(qkv, c_pad):
    t = qkv.shape[0]
    tm = min(ROW_TILE, t)
    hd = HEAD_DIM
    sel_q = _piece_selector(Q_C, 1.0, range(Q_ONE, Q_ONE + 3))
    sel_k = _piece_selector(K_C, -1.0, [*range(K_ONE, K_ONE + 3), *range(K_ONE2, K_ONE2 + 3)])
    sel_v = _piece_selector(HEAD_DIM, 0.0, range(V_ONE, V_ONE + 4))

    def body(x_ref, c_ref, sq_ref, sk_ref, sv_ref, qp_ref, kp_ref, vp_ref, kt_ref, vt_ref):
        pieces = _piece_rows(c_ref[...])
        q_extra = jnp.dot(pieces, sq_ref[...], preferred_element_type=F32).astype(BF16)
        k_extra = jnp.dot(pieces, sk_ref[...], preferred_element_type=F32).astype(BF16)
        v_extra = jnp.dot(pieces, sv_ref[...], preferred_element_type=F32).astype(BF16)
        for h in range(FOX_HEADS):
            hs = slice(h * hd, (h + 1) * hd)
            qp_ref[h, :, :hd] = (x_ref[:, hs].astype(F32) * (hd ** -0.5)).astype(BF16)
            qp_ref[h, :, hd:] = q_extra[:, hs]
            kp_ref[h, :, :hd] = x_ref[:, FOX_WIDTH + h * hd:FOX_WIDTH + (h + 1) * hd]
            kp_ref[h, :, hd:] = k_extra[:, hs]
            vp_ref[h, :, :hd] = x_ref[:, 2 * FOX_WIDTH + h * hd:2 * FOX_WIDTH + (h + 1) * hd]
            vp_ref[h, :, hd:] = v_extra[:, hs]
            kt_ref[h] = kp_ref[h].T
            vt_ref[h] = vp_ref[h].T

    row3 = pl.BlockSpec((FOX_HEADS, tm, LANES), lambda i: (0, i, 0))
    col3 = pl.BlockSpec((FOX_HEADS, LANES, tm), lambda i: (0, 0, i))
    sel = _resident(sel_q.shape)
    return pl.pallas_call(
        body, grid=(t // tm,),
        in_specs=[pl.BlockSpec((tm, QKV), lambda i: (i, 0)), pl.BlockSpec((tm, LANES), lambda i: (i, 0)), sel, sel, sel],
        out_specs=[row3, row3, row3, col3, col3],
        out_shape=[_sds((FOX_HEADS, t, LANES), BF16)] * 3 + [_sds((FOX_HEADS, LANES, t), BF16)] * 2,
        compiler_params=_cp(), name="attn_pack")(qkv, c_pad, sel_q, sel_k, sel_v)


def _triangle(nq, key_major):
    if key_major:
        pairs = [(i, j) for j in range(nq) for i in range(j, nq)]
    else:
        pairs = [(i, j) for i in range(nq) for j in range(i + 1)]
    return jnp.asarray([p[0] for p in pairs], jnp.int32), jnp.asarray([p[1] for p in pairs], jnp.int32)


def _attn_fwd(qp, kp, vt):
    t = qp.shape[1]
    bq = min(ATT_BLOCK, t)
    nq = t // bq
    nh = ATT_FWD_HEADS
    i_tab, j_tab = _triangle(nq, key_major=False)

    def body(it_ref, jt_ref, q_ref, k_ref, vt_ref, o_ref, lse_ref, m_sc, acc_sc):
        s = pl.program_id(1)
        i, j = it_ref[s], jt_ref[s]

        @pl.when(j == 0)
        def _():
            m_sc[...] = jnp.full(m_sc.shape, NEG, F32)
            acc_sc[...] = jnp.zeros(acc_sc.shape, F32)

        def sweep(masked):
            scores = lambda h: lax.dot_general(k_ref[h], q_ref[h], NT, preferred_element_type=F32)

            def accumulate(h, pt, rescale):
                acc_sc[h] = rescale * acc_sc[h] + jnp.dot(vt_ref[h], pt, preferred_element_type=F32)

            ahead, behind = scores(0), None
            for h in range(nh):
                st = ahead
                if h + 1 < nh:
                    ahead = scores(h + 1)
                if behind is not None:
                    accumulate(*behind)
                if masked:
                    key = lax.broadcasted_iota(jnp.int32, (bq, bq), 0)
                    qry = lax.broadcasted_iota(jnp.int32, (bq, bq), 1)
                    st = jnp.where(key <= qry, st, NEG)
                m_prev = m_sc[h]
                m_new = jnp.maximum(m_prev, jnp.max(st, axis=0, keepdims=True))
                behind = (h, jnp.exp(st - m_new).astype(BF16), jnp.exp(m_prev - m_new))
                m_sc[h] = m_new
            accumulate(*behind)

        @pl.when(j < i)
        def _():
            sweep(False)

        @pl.when(j == i)
        def _():
            sweep(True)
            for h in range(nh):
                acc = acc_sc[h]
                denom = acc[V_ONE:V_ONE + 1, :]
                o_ref[:, h * HEAD_DIM:(h + 1) * HEAD_DIM] = (acc[:HEAD_DIM, :] / denom).T.astype(BF16)
                lse_ref[h] = m_sc[h] + jnp.log(denom)

    grid_spec = pltpu.PrefetchScalarGridSpec(
        num_scalar_prefetch=2, grid=(FOX_HEADS // nh, i_tab.shape[0]),
        in_specs=[pl.BlockSpec((nh, bq, LANES), lambda hp, s, it, jt: (hp, it[s], 0)),
                  pl.BlockSpec((nh, bq, LANES), lambda hp, s, it, jt: (hp, jt[s], 0)),
                  pl.BlockSpec((nh, LANES, bq), lambda hp, s, it, jt: (hp, 0, jt[s]))],
        out_specs=[pl.BlockSpec((bq, nh * HEAD_DIM), lambda hp, s, it, jt: (it[s], hp)),
                   pl.BlockSpec((nh, 1, bq), lambda hp, s, it, jt: (hp, 0, it[s]))],
        scratch_shapes=[pltpu.VMEM((nh, 1, bq), F32), pltpu.VMEM((nh, LANES, bq), F32)])
    return pl.pallas_call(body, grid_spec=grid_spec,
                          out_shape=[_sds((t, FOX_WIDTH), BF16), _sds((FOX_HEADS, 1, t), F32)],
                          compiler_params=_cp(), name="attn_fwd")(i_tab, j_tab, qp, kp, vt)


def _even_out(attn, bch, conv_w, w_out, x, g, b):
    t, d = x.shape
    tm = min(ROW_TILE, t)
    halo_blocks = tm // SUBLANES
    cw = CONV_WIDTH

    def body(a_ref, cur_ref, prev_ref, cw_ref, wo_ref, x_ref, g_ref, b_ref, conv_ref, yb_ref, xh_ref, rs_ref):
        i = pl.program_id(0)
        z = cur_ref[:, cw:2 * cw] * cur_ref[:, 2 * cw:]
        zp = jnp.where(i == 0, 0.0, prev_ref[:, cw:2 * cw] * prev_ref[:, 2 * cw:])
        z1, z2 = _shift_down(z, zp)
        conv = (cur_ref[:, :cw] * (cw_ref[0:1, :] * z2 + cw_ref[1:2, :] * z1 + cw_ref[2:3, :] * z)).astype(BF16)
        conv_ref[...] = conv
        pre = (ALPHA * x_ref[...] + jnp.dot(a_ref[...], wo_ref[:FOX_WIDTH, :], preferred_element_type=F32)
               + jnp.dot(conv, wo_ref[FOX_WIDTH:, :], preferred_element_type=F32))
        xhat, rstd = _ln_fwd(pre)
        yb_ref[...] = (xhat * g_ref[...] + b_ref[...]).astype(BF16)
        xh_ref[...] = xhat
        rs_ref[...] = rstd

    row = lambda i: (i, 0)
    full, half = pl.BlockSpec((tm, d), row), pl.BlockSpec((tm, cw), row)
    return pl.pallas_call(
        body, grid=(t // tm,),
        in_specs=[half, pl.BlockSpec((tm, BCH), row),
                  pl.BlockSpec((SUBLANES, BCH), lambda i: (jnp.maximum(i * halo_blocks - 1, 0), 0)),
                  _resident(conv_w.shape), _resident(w_out.shape), full, _resident(g.shape), _resident(b.shape)],
        out_specs=[half, full, full, pl.BlockSpec((tm, 1), row)],
        out_shape=[_sds((t, cw), BF16), _sds((t, d), BF16), _sds((t, d), F32), _sds((t, 1), F32)],
        compiler_params=_cp(), name="even_out")(attn, bch, bch, conv_w, w_out, x, g, b)


def _gmlp_fwd(x, w_in, vg, vb, wm, bs_col, w_out, res_ln, g, b):
    t, d = x.shape
    tm = min(ROW_TILE, t)
    gb = GMLP_BLOCK
    rxh, rg, rb = res_ln

    def body(x_ref, w_ref, vg_ref, vb_ref, wm_ref, bs_ref, wo_ref, rxh_ref, rg_ref, rb_ref, g_ref, b_ref,
             sv_ref, rs_ref, o_ref, yb_ref, xh_ref, rsy_ref, a_sc):
        xb = x_ref[...].astype(BF16)
        nc = w_ref.shape[2]
        for j in range(w_ref.shape[0]):
            a_sc[:, j * nc:(j + 1) * nc] = jnp.dot(xb, w_ref[j], preferred_element_type=F32)
        halves = []
        for half in range(2):
            a = a_sc[:, half * d:(half + 1) * d]
            cdf = 0.5 * (1.0 + lax.erf(a * (2.0 ** -0.5)))
            halves.append(a * cdf)
            slope = cdf + a * (jnp.exp(-0.5 * a * a) * (1.0 / math.sqrt(2.0 * math.pi)))
            sv_ref[:, (2 * half + 1) * d:(2 * half + 2) * d] = slope.astype(BF16)
        u = halves[0]
        vhat, rstd = _ln_fwd(halves[1])
        sv_ref[:, :d] = u.astype(BF16)
        sv_ref[:, 2 * d:3 * d] = vhat.astype(BF16)
        rs_ref[...] = rstd
        vln = (vhat * vg_ref[...] + vb_ref[...]).astype(BF16)
        for blk in range(tm // gb):
            rs = slice(blk * gb, (blk + 1) * gb)
            for gi in range(GMLP_GROUPS):
                cs = slice(gi * gb, (gi + 1) * gb)
                s = jnp.dot(wm_ref[gi], vln[rs, cs], preferred_element_type=F32) + bs_ref[:, gi:gi + 1]
                o_ref[rs, cs] = (u[rs, cs] * s).astype(BF16)
        z = ALPHA * (rxh_ref[...] * rg_ref[...] + rb_ref[...]) + jnp.dot(o_ref[...], wo_ref[...], preferred_element_type=F32)
        xhat, rstd_y = _ln_fwd(z)
        yb_ref[...] = (xhat * g_ref[...] + b_ref[...]).astype(BF16)
        xh_ref[...] = xhat
        rsy_ref[...] = rstd_y

    row = lambda i: (i, 0)
    full, col, vec = pl.BlockSpec((tm, d), row), pl.BlockSpec((tm, 1), row), _resident(g.shape)
    return pl.pallas_call(
        body, grid=(t // tm,),
        in_specs=[full, _resident(w_in.shape), _resident(vg.shape), _resident(vb.shape),
                  _resident(wm.shape), _resident(bs_col.shape), _resident(w_out.shape), full, vec, vec, vec, vec],
        out_specs=[pl.BlockSpec((tm, 4 * d), row), col, full, full, full, col],
        out_shape=[_sds((t, 4 * d), BF16), _sds((t, 1), F32), _sds((t, d), BF16), _sds((t, d), BF16), _sds((t, d), F32),
                   _sds((t, 1), F32)],
        scratch_shapes=[pltpu.VMEM((tm, 2 * d), F32)],
        compiler_params=_cp(), name="gmlp_fwd")(x, w_in, vg, vb, wm, bs_col, w_out, rxh, rg, rb, g, b)


def _mm_back(pairs, wt, res, after, name):
    t = pairs[0][0].shape[0]
    k = wt.shape[1]
    tm = min(ROW_TILE, t)
    n = len(pairs)

    def body(after_ref, *refs):
        a_refs, w_ref, res_ref, o_ref = refs[:n], refs[n], refs[n + 1], refs[n + 2]
        dx = ALPHA * res_ref[...]
        for a_ref, (_, lo, hi) in zip(a_refs, pairs):
            dx = dx + jnp.dot(a_ref[...].astype(BF16), w_ref[lo:hi, :], preferred_element_type=F32)
        o_ref[...] = dx

    row = lambda i: (i, 0)
    return pl.pallas_call(
        body, grid=(t // tm,),
        in_specs=[_ANY_SPEC] + [pl.BlockSpec((tm, a.shape[1]), row) for a, _, _ in pairs]
        + [_resident(wt.shape), pl.BlockSpec((tm, k), row)],
        out_specs=pl.BlockSpec((tm, k), row), out_shape=_sds((t, k), F32),
        compiler_params=_cp(), name=name)(after, *[a for a, _, _ in pairs], wt, res)


def _mm_tn(a, b, name, *, tn, tk=None, tt=None, stack_cols=False, out_dtype=BF16, after=None):
    t, k = a.shape
    n = b.shape[1]
    tk = k if tk is None else tk
    tt = min(REDUCE_TILE if tt is None else tt, t)
    nt = t // tt

    def body(a_ref, b_ref, *rest):
        o_ref, acc_ref = rest[after is not None:]
        s = pl.program_id(2)
        part = lax.dot_general(a_ref[...].astype(BF16), b_ref[...].astype(BF16), TN, preferred_element_type=F32)
        _accumulate(acc_ref, s == 0, part)

        @pl.when(s == nt - 1)
        def _():
            o_ref[...] = acc_ref[...].astype(out_dtype).reshape(o_ref.shape)

    if stack_cols:
        assert tk == k
        out_spec = pl.BlockSpec((1, k, tn), lambda kk, j, s: (j, 0, 0))
        out_shape = _sds((n // tn, k, tn), out_dtype)
    else:
        out_spec = pl.BlockSpec((tk, tn), lambda kk, j, s: (kk, j))
        out_shape = _sds((k, n), out_dtype)
    return pl.pallas_call(
        body, grid=(k // tk, n // tn, nt),
        in_specs=[pl.BlockSpec((tt, tk), lambda kk, j, s: (s, kk)), pl.BlockSpec((tt, tn), lambda kk, j, s: (s, j))]
        + ([_ANY_SPEC] if after is not None else []),
        out_specs=out_spec, out_shape=out_shape,
        scratch_shapes=[pltpu.VMEM((tk, tn), F32)],
        compiler_params=_cp(), name=name)(a, b, *([after] if after is not None else []))


def _mm_tn_pair(a1, a2, b, name):
    t, k = a1.shape
    n = b.shape[1]
    tt = min(REDUCE_TILE, t)
    nt = t // tt

    def body(a1_ref, a2_ref, b_ref, o_ref, acc_ref):
        s = pl.program_id(0)
        bb = b_ref[...].astype(BF16)
        part = jnp.concatenate([lax.dot_general(a_ref[...].astype(BF16), bb, TN, preferred_element_type=F32)
                                for a_ref in (a1_ref, a2_ref)], axis=0)
        _accumulate(acc_ref, s == 0, part)

        @pl.when(s == nt - 1)
        def _():
            o_ref[...] = acc_ref[...].astype(BF16)

    rows = pl.BlockSpec((tt, k), lambda s: (s, 0))
    return pl.pallas_call(
        body, grid=(nt,),
        in_specs=[rows, rows, pl.BlockSpec((tt, n), lambda s: (s, 0))],
        out_specs=pl.BlockSpec((2 * k, n), lambda s: (0, 0)), out_shape=_sds((2 * k, n), BF16),
        scratch_shapes=[pltpu.VMEM((2 * k, n), F32)],
        compiler_params=_cp(), name=name)(a1, a2, b)


def _ffn_bwd_rows(dz, wo, gu, wi, ln_below, name):
    t, d = dz.shape
    tm = min(FFN_FUSED_ROW_TILE, t)
    hh = HALF_HIDDEN
    xhat, rstd, g = ln_below

    def body(dz_ref, wo_ref, gu_ref, wi_ref, xh_ref, rs_ref, g_ref, dgu_ref, dzb_ref, dg_ref, db_ref):
        first = pl.program_id(0) == 0
        a = dz_ref[...].astype(BF16)
        for c in range(2):
            gs, us = slice(c * hh, (c + 1) * hh), slice(FFN_HIDDEN + c * hh, FFN_HIDDEN + (c + 1) * hh)
            dh = lax.dot_general(a, wo_ref[gs, :], NT, preferred_element_type=F32)
            dgu_ref[:, gs] = (dh * gu_ref[:, gs].astype(F32)).astype(BF16)
            dgu_ref[:, us] = (dh * gu_ref[:, us].astype(F32)).astype(BF16)
        dx = ALPHA * dz_ref[...]
        for j in range(wi_ref.shape[0]):
            dx = dx + lax.dot_general(dgu_ref[:, j * hh:(j + 1) * hh], wi_ref[j], NT, preferred_element_type=F32)
        dzb, dg, db = _ln_bwd(dx, xh_ref[...], rs_ref[...], g_ref[...])
        dzb_ref[...] = dzb
        _accumulate(dg_ref, first, dg)
        _accumulate(db_ref, first, db)

    row = lambda i: (i, 0)
    wide, full = pl.BlockSpec((tm, 2 * FFN_HIDDEN), row), pl.BlockSpec((tm, d), row)
    vec = pl.BlockSpec((1, d), lambda i: (0, 0))
    return pl.pallas_call(
        body, grid=(t // tm,),
        in_specs=[full, _resident(wo.shape), wide, _resident(wi.shape), full, pl.BlockSpec((tm, 1), row),
                  _resident(g.shape)],
        out_specs=[wide, full, vec, vec],
        out_shape=[_sds((t, 2 * FFN_HIDDEN), BF16), _sds((t, d), F32), _sds((1, d), F32), _sds((1, d), F32)],
        compiler_params=_cp(), name=name)(dz, wo, gu, wi, xhat, rstd, g)


def _gmlp_bwd(dz, w_out, saved, rstd_v, vg, vb, wm, bs_col, w_in, ln_below):
    t, d = dz.shape
    d2 = 2 * d
    tm = min(ROW_TILE, t)
    gb = GMLP_BLOCK
    xhat_below, rstd_below, g_below = ln_below

    def body(dz_ref, wo_ref, sv_ref, rs_ref, vg_ref, vb_ref, wm_ref, bs_ref, wi_ref, xh_ref, rsb_ref, gb_ref,
             da_ref, dws_ref, dbs_ref, dvg_ref, dvb_ref, dzb_ref, dg_ref, db_ref, dvln_sc):
        first = pl.program_id(0) == 0
        u = sv_ref[:, :d].astype(F32)
        vhat = sv_ref[:, 2 * d:3 * d].astype(F32)
        rstd = rs_ref[...]
        vln = (vhat * vg_ref[...] + vb_ref[...]).astype(BF16)
        dgate = lax.dot_general(dz_ref[...].astype(BF16), wo_ref[...], NT, preferred_element_type=F32)

        @pl.when(first)
        def _():
            dws_ref[...] = jnp.zeros(dws_ref.shape, F32)
            dbs_ref[...] = jnp.zeros(dbs_ref.shape, F32)

        for blk in range(tm // gb):
            rs = slice(blk * gb, (blk + 1) * gb)
            for gi in range(GMLP_GROUPS):
                cs = slice(gi * gb, (gi + 1) * gb)
                vblk = vln[rs, cs]
                s = jnp.dot(wm_ref[gi], vblk, preferred_element_type=F32) + bs_ref[:, gi:gi + 1]
                dgb = dgate[rs, cs]
                da_ref[rs, cs] = (dgb * s * sv_ref[rs, d + gi * gb:d + (gi + 1) * gb].astype(F32)).astype(BF16)
                ds = dgb * u[rs, cs]
                dsb = ds.astype(BF16)
                dws_ref[gi] += lax.dot_general(dsb, vblk, NT, preferred_element_type=F32)
                dbs_ref[:, gi:gi + 1] += jnp.sum(ds, axis=1, keepdims=True)
                dvln_sc[rs, cs] = lax.dot_general(wm_ref[gi], dsb, TN, preferred_element_type=F32)
        dv, dvg, dvb = _ln_bwd(dvln_sc[...], vhat, rstd, vg_ref[...])
        da_ref[:, d:] = (dv * sv_ref[:, 3 * d:].astype(F32)).astype(BF16)
        _accumulate(dvg_ref, first, dvg)
        _accumulate(dvb_ref, first, dvb)
        dx = ALPHA * dz_ref[...]
        nc = wi_ref.shape[2]
        for j in range(wi_ref.shape[0]):
            dx = dx + lax.dot_general(da_ref[:, j * nc:(j + 1) * nc], wi_ref[j], NT, preferred_element_type=F32)
        dzb, dg, db = _ln_bwd(dx, xh_ref[...], rsb_ref[...], gb_ref[...])
        dzb_ref[...] = dzb
        _accumulate(dg_ref, first, dg)
        _accumulate(db_ref, first, db)

    row = lambda i: (i, 0)
    full, col = pl.BlockSpec((tm, d), row), pl.BlockSpec((tm, 1), row)
    vec = pl.BlockSpec((1, d), lambda i: (0, 0))
    return pl.pallas_call(
        body, grid=(t // tm,),
        in_specs=[full, _resident(w_out.shape), pl.BlockSpec((tm, 4 * d), row), col,
                  _resident(vg.shape), _resident(vb.shape), _resident(wm.shape), _resident(bs_col.shape),
                  _resident(w_in.shape), full, col, _resident(g_below.shape)],
        out_specs=[pl.BlockSpec((tm, d2), row), pl.BlockSpec(wm.shape, lambda i: (0, 0, 0)),
                   pl.BlockSpec(bs_col.shape, lambda i: (0, 0)), vec, vec, full, vec, vec],
        out_shape=[_sds((t, d2), BF16), _sds(wm.shape, F32), _sds(bs_col.shape, F32), _sds((1, d), F32), _sds((1, d), F32),
                   _sds((t, d), F32), _sds((1, d), F32), _sds((1, d), F32)],
        scratch_shapes=[pltpu.VMEM((tm, d), F32)],
        compiler_params=_cp(), name="gmlp_bwd")(dz, w_out, saved, rstd_v, vg, vb, wm, bs_col, w_in, xhat_below,
                                                rstd_below, g_below)


def _conv_bwd(bch, dconv, conv_w):
    t = bch.shape[0]
    tm = min(ROW_TILE, t)
    nb = t // tm
    halo_blocks = tm // SUBLANES
    cw = CONV_WIDTH

    def body(cur_ref, prev_ref, next_ref, dc_ref, dn_ref, w_ref, o_ref, dw_ref):
        i = pl.program_id(0)
        bgate, cgate, hval = cur_ref[:, :cw], cur_ref[:, cw:2 * cw], cur_ref[:, 2 * cw:]
        z = cgate * hval
        zp = jnp.where(i == 0, 0.0, prev_ref[:, cw:2 * cw] * prev_ref[:, 2 * cw:])
        z1, z2 = _shift_down(z, zp)
        w0, w1, w2 = w_ref[0:1, :], w_ref[1:2, :], w_ref[2:3, :]
        dconv = dc_ref[...]
        o_ref[:, :cw] = (dconv * (w0 * z2 + w1 * z1 + w2 * z)).astype(BF16)
        dy = dconv * bgate
        dyn = jnp.where(i == nb - 1, 0.0, dn_ref[...] * next_ref[:, :cw])
        dy1, dy2 = _shift_up(dy, dyn)
        dz = w2 * dy + w1 * dy1 + w0 * dy2
        o_ref[:, cw:2 * cw] = (dz * hval).astype(BF16)
        o_ref[:, 2 * cw:] = (dz * cgate).astype(BF16)

        @pl.when(i == 0)
        def _():
            dw_ref[...] = jnp.zeros(dw_ref.shape, F32)

        for tap, zs in enumerate((z2, z1, z)):
            dw_ref[tap:tap + 1, :] += jnp.sum(dy * zs, axis=0, keepdims=True)

    last_halo = t // SUBLANES - 1
    return pl.pallas_call(
        body, grid=(nb,),
        in_specs=[pl.BlockSpec((tm, BCH), lambda i: (i, 0)),
                  pl.BlockSpec((SUBLANES, BCH), lambda i: (jnp.maximum(i * halo_blocks - 1, 0), 0)),
                  pl.BlockSpec((SUBLANES, BCH), lambda i: (jnp.minimum((i + 1) * halo_blocks, last_halo), 0)),
                  pl.BlockSpec((tm, cw), lambda i: (i, 0)),
                  pl.BlockSpec((SUBLANES, cw), lambda i: (jnp.minimum((i + 1) * halo_blocks, last_halo), 0)),
                  _resident(conv_w.shape)],
        out_specs=[pl.BlockSpec((tm, BCH), lambda i: (i, 0)), pl.BlockSpec((SUBLANES, cw), lambda i: (0, 0))],
        out_shape=[_sds((t, BCH), BF16), _sds((SUBLANES, cw), F32)],
        compiler_params=_cp(), name="conv_bwd")(bch, bch, bch, dconv, dconv, conv_w)


def _attn_bwd_prep(dz, w_out, o, qp, lse_pad, after):
    t = o.shape[0]
    tm = min(ROW_TILE, t)
    hd = HEAD_DIM
    sel_lse = _piece_selector(Q_LSE, -1.0)
    sel_delta = _piece_selector(DO_DELTA, -1.0)
    head_of = jnp.asarray([[1.0 if col == row // hd else 0.0 for col in range(LANES)] for row in range(FOX_WIDTH)], F32)

    def body(after_ref, dz_ref, wo_ref, o_ref, qp_ref, lse_ref, sl_ref, sd_ref, seg_ref, qb_ref, dob_ref, dconv_ref):
        dzb = dz_ref[...].astype(BF16)
        do = lax.dot_general(dzb, wo_ref[:FOX_WIDTH, :], NT, preferred_element_type=F32)
        dconv_ref[...] = lax.dot_general(dzb, wo_ref[FOX_WIDTH:, :], NT, preferred_element_type=F32)
        delta = jnp.dot(o_ref[...].astype(F32) * do, seg_ref[...], precision=HIGHEST, preferred_element_type=F32)
        lse_extra = jnp.dot(_piece_rows(lse_ref[...]), sl_ref[...], preferred_element_type=F32)
        do_extra = jnp.dot(_piece_rows(delta), sd_ref[...], preferred_element_type=F32).astype(BF16)
        for h in range(FOX_HEADS):
            hs = slice(h * hd, (h + 1) * hd)
            dob_ref[h, :, :hd] = do[:, hs].astype(BF16)
            dob_ref[h, :, hd:] = do_extra[:, hs]
            qb_ref[h, :, :hd] = qp_ref[h, :, :hd]
            qb_ref[h, :, hd:] = (qp_ref[h, :, hd:].astype(F32) + lse_extra[:, hs]).astype(BF16)

    row = lambda i: (i, 0)
    row3 = pl.BlockSpec((FOX_HEADS, tm, LANES), lambda i: (0, i, 0))
    half = pl.BlockSpec((tm, FOX_WIDTH), row)
    return pl.pallas_call(
        body, grid=(t // tm,),
        in_specs=[_ANY_SPEC, pl.BlockSpec((tm, dz.shape[1]), row), _resident(w_out.shape), half, row3,
                  pl.BlockSpec((tm, LANES), row), _resident(sel_lse.shape), _resident(sel_delta.shape),
                  _resident(head_of.shape)],
        out_specs=[row3, row3, half],
        out_shape=[_sds((FOX_HEADS, t, LANES), BF16)] * 2 + [_sds((t, FOX_WIDTH), F32)],
        compiler_params=_cp(), name="attn_bwd_prep")(after, dz, w_out, o, qp, lse_pad, sel_lse, sel_delta, head_of)


def _attn_bwd(qb, kp, vp, dob, kt):
    t = qb.shape[1]
    bq = min(ATT_BLOCK, t)
    nq = t // bq
    i_tab, j_tab = _triangle(nq, key_major=True)

    def body(it_ref, jt_ref, q_ref, k_ref, v_ref, do_ref, kt_ref, dqt_ref, dk_ref, dv_ref, dk_sc, dv_sc):
        s = pl.program_id(1)
        i, j = it_ref[s], jt_ref[s]

        @pl.when(s == 0)
        def _():
            dqt_ref[...] = jnp.zeros(dqt_ref.shape, F32)

        @pl.when(i == j)
        def _():
            dk_sc[...] = jnp.zeros(dk_sc.shape, F32)
            dv_sc[...] = jnp.zeros(dv_sc.shape, F32)

        cols = pl.ds(pl.multiple_of(i * bq, bq), bq)

        def sweep(masked):
            def scores(h):
                return (lax.dot_general(k_ref[h], q_ref[h], NT, preferred_element_type=F32),
                        lax.dot_general(v_ref[h], do_ref[h], NT, preferred_element_type=F32))

            def accumulate(h, ptb, dstb):
                dv_sc[h] += jnp.dot(ptb, do_ref[h], preferred_element_type=F32)
                dk_sc[h] += jnp.dot(dstb, q_ref[h], preferred_element_type=F32)
                dqt_ref[h, :, cols] += jnp.dot(kt_ref[h], dstb, preferred_element_type=F32)

            ahead, behind = scores(0), None
            for h in range(ATT_BWD_HEADS):
                st, dpt = ahead
                if h + 1 < ATT_BWD_HEADS:
                    ahead = scores(h + 1)
                if behind is not None:
                    accumulate(*behind)
                if masked:
                    key = lax.broadcasted_iota(jnp.int32, (bq, bq), 0)
                    qry = lax.broadcasted_iota(jnp.int32, (bq, bq), 1)
                    st = jnp.where(key <= qry, st, NEG)
                pt = jnp.exp(st)
                behind = (h, pt.astype(BF16), (pt * dpt).astype(BF16))
            accumulate(*behind)

        @pl.when(i == j)
        def _():
            sweep(True)

        @pl.when(i > j)
        def _():
            sweep(False)

        @pl.when(i == nq - 1)
        def _():
            dk_ref[...] = dk_sc[...]
            dv_ref[...] = dv_sc[...].astype(BF16)

    nh = ATT_BWD_HEADS
    qblk = pl.BlockSpec((nh, bq, LANES), lambda hp, s, it, jt: (hp, it[s], 0))
    kblk = pl.BlockSpec((nh, bq, LANES), lambda hp, s, it, jt: (hp, jt[s], 0))
    grid_spec = pltpu.PrefetchScalarGridSpec(
        num_scalar_prefetch=2, grid=(FOX_HEADS // nh, i_tab.shape[0]),
        in_specs=[qblk, kblk, kblk, qblk, pl.BlockSpec((nh, LANES, bq), lambda hp, s, it, jt: (hp, 0, jt[s]))],
        out_specs=[pl.BlockSpec((nh, LANES, t), lambda hp, s, it, jt: (hp, 0, 0), pipeline_mode=pl.Buffered(1)),
                   kblk, kblk],
        scratch_shapes=[pltpu.VMEM((nh, bq, LANES), F32), pltpu.VMEM((nh, bq, LANES), F32)])
    return pl.pallas_call(body, grid_spec=grid_spec,
                          out_shape=[_sds((FOX_HEADS, LANES, t), F32), _sds((FOX_HEADS, t, LANES), F32),
                                     _sds((FOX_HEADS, t, LANES), BF16)],
                          compiler_params=_cp(ATT_BWD_VMEM_LIMIT), name="attn_bwd")(i_tab, j_tab, qb, kp, vp, dob, kt)


def _attn_unpack(dqt, dkp, dvp):
    t = dkp.shape[1]
    tm = min(ROW_TILE, t)
    hd = HEAD_DIM

    def body(dqt_ref, dk_ref, dv_ref, o_ref, dc_ref):
        for h in range(FOX_HEADS):
            dq = dqt_ref[h].T
            o_ref[:, h * hd:(h + 1) * hd] = (dq[:, :hd] * (hd ** -0.5)).astype(BF16)
            o_ref[:, FOX_WIDTH + h * hd:FOX_WIDTH + (h + 1) * hd] = dk_ref[h, :, :hd].astype(BF16)
            o_ref[:, 2 * FOX_WIDTH + h * hd:2 * FOX_WIDTH + (h + 1) * hd] = dv_ref[h, :, :hd]
            dc_ref[:, h:h + 1] = dq[:, K_ONE:K_ONE + 1] - dk_ref[h, :, Q_ONE:Q_ONE + 1]

    row3 = pl.BlockSpec((FOX_HEADS, tm, LANES), lambda i: (0, i, 0))
    return pl.pallas_call(
        body, grid=(t // tm,),
        in_specs=[pl.BlockSpec((FOX_HEADS, LANES, tm), lambda i: (0, 0, i)), row3, row3],
        out_specs=[pl.BlockSpec((tm, QKV), lambda i: (i, 0)), pl.BlockSpec((tm, FOX_HEADS), lambda i: (i, 0))],
        out_shape=[_sds((t, QKV), BF16), _sds((t, FOX_HEADS), F32)],
        compiler_params=_cp(), name="attn_unpack")(dqt, dkp, dvp)


def _adamw(parts, w, m, v, name, layer=None, into=None):
    nl, r, c = w.shape
    fits = [cand for cand in [*range(SUBLANES, r, SUBLANES), r] if r % cand == 0 and cand * c * 4 <= ADAMW_BLOCK_BYTES]
    tr = max(fits) if fits else r
    npart = len(parts)
    bc1 = 1.0 - ADAM_B1 ** ADAM_STEP
    bc2 = 1.0 - ADAM_B2 ** ADAM_STEP

    def body(*refs):
        p_refs = refs[:npart]
        w_ref, m_ref, v_ref = refs[npart:npart + 3]
        g_ref, d_ref, nm_ref, nv_ref = refs[-4:]
        sums = []
        for p_ref in p_refs:
            acc = p_ref[0, 0].astype(F32)
            for s in range(1, p_ref.shape[0]):
                acc = acc + p_ref[s, 0].astype(F32)
            sums.append(acc)
        g = sums[0]
        for extra in sums[1:]:
            g = g + extra
        nm = ADAM_B1 * m_ref[0] + (1.0 - ADAM_B1) * g
        nv = ADAM_B2 * v_ref[0] + (1.0 - ADAM_B2) * (g * g)
        m_hat = nm / bc1
        v_hat = nv / bc2
        g_ref[0] = g
        d_ref[0] = -ADAM_LR * (m_hat / (jnp.sqrt(v_hat) + ADAM_EPS) + ADAM_WD * w_ref[0])
        nm_ref[0] = nm
        nv_ref[0] = nv

    first = 0 if layer is None else layer
    blk = pl.BlockSpec((1, tr, c), lambda l, i: (first + l, i, 0))
    extra = [] if into is None else list(into)
    return pl.pallas_call(
        body, grid=(nl if layer is None else 1, r // tr),
        in_specs=[pl.BlockSpec((p.shape[0], 1, tr, c), lambda l, i: (0, l, i, 0)) for p in parts] + [blk, blk, blk]
        + [_ANY_SPEC] * len(extra),
        out_specs=[blk] * 4, out_shape=[_sds(w.shape, F32)] * 4,
        input_output_aliases={npart + 3 + k: k for k in range(len(extra))},
        compiler_params=_cp(), name=name)(*parts, w, m, v, *extra)


def _to_rows(a):
    flat = a.reshape(-1)
    pad = (-flat.shape[0]) % LANES
    if pad:
        flat = jnp.concatenate([flat, jnp.zeros((pad,), flat.dtype)])
    return flat.reshape(-1, LANES)


def _ffn_fwd(xin_ln, xin_b, wi, wo, g, b, layer, target=None):
    t, d = xin_b.shape
    tm = min(FFN_FUSED_ROW_TILE, t)
    hh = HALF_HIDDEN
    rxh, rg, rb = xin_ln

    def body(x_ref, wi_ref, wo_ref, rxh_ref, rg_ref, rb_ref, g_ref, b_ref, *rest):
        gu_ref, h_ref = rest[target is not None:][:2]
        a = x_ref[...]
        for c in range(2):
            gs, us = slice(c * hh, (c + 1) * hh), slice(FFN_HIDDEN + c * hh, FFN_HIDDEN + (c + 1) * hh)
            gate = jnp.dot(a, wi_ref[c], preferred_element_type=F32)
            up = jnp.dot(a, wi_ref[2 + c], preferred_element_type=F32)
            sig = _sigmoid(gate)
            silu = gate * sig
            gu_ref[:, gs] = (up * sig * (1.0 + gate * (1.0 - sig))).astype(BF16)
            gu_ref[:, us] = silu.astype(BF16)
            h_ref[:, gs] = (silu * up).astype(BF16)
        z = ALPHA * (rxh_ref[...] * rg_ref[...] + rb_ref[...]) + jnp.dot(h_ref[...], wo_ref[...], preferred_element_type=F32)
        xhat, rstd = _ln_fwd(z)
        if target is None:
            yb_ref, xh_ref, rs_ref = rest[2:]
            yb_ref[...] = (xhat * g_ref[...] + b_ref[...]).astype(BF16)
            xh_ref[...] = xhat
            rs_ref[...] = rstd
            return
        sq_ref, dz_ref, dg_ref, db_ref = rest[3:]
        first = pl.program_id(0) == 0
        err = xhat * g_ref[...] + b_ref[...] - rest[0][...]
        dz, dg, db = _ln_bwd(err * (1.0 / d), xhat, rstd, g_ref[...])
        dz_ref[...] = dz
        _accumulate(sq_ref, first, jnp.sum(err * err, axis=0, keepdims=True))
        _accumulate(dg_ref, first, dg)
        _accumulate(db_ref, first, db)

    row = lambda i: (i, 0)
    full = pl.BlockSpec((tm, d), row)
    vec = _resident(g.shape)
    acc = pl.BlockSpec((1, d), lambda i: (0, 0))
    in_specs = [full, _resident(wi.shape), _resident(wo.shape), full, vec, vec, vec, vec]
    out_specs = [pl.BlockSpec((tm, 2 * FFN_HIDDEN), row), pl.BlockSpec((tm, FFN_HIDDEN), row)]
    out_shape = [_sds((t, 2 * FFN_HIDDEN), BF16), _sds((t, FFN_HIDDEN), BF16)]
    args = [xin_b, wi, wo, rxh, rg, rb, g, b]
    if target is None:
        out_specs += [full, full, pl.BlockSpec((tm, 1), row)]
        out_shape += [_sds((t, d), BF16), _sds((t, d), F32), _sds((t, 1), F32)]
    else:
        in_specs.append(full)
        args.append(target)
        out_specs += [acc, full, acc, acc]
        out_shape += [_sds((1, d), F32), _sds((t, d), F32), _sds((1, d), F32), _sds((1, d), F32)]
    gu, h, *tail = pl.pallas_call(body, grid=(t // tm,), in_specs=in_specs, out_specs=out_specs, out_shape=out_shape,
                                  compiler_params=_cp(), name=f"ffn_fwd_rows_{layer}")(*args)
    if target is None:
        y_b, xhat, rstd = tail
        return y_b, (xin_b, gu, h, xhat, rstd)
    return tail, (xin_b, gu, h)


def _ffn_bwd(dz, saved, wi, wo, ln_below, layer):
    xin_b, gu, h = saved[:3]
    dgu, *below = _ffn_bwd_rows(dz, wo, gu, wi, ln_below, f"ffn_bwd_rows_{layer}")
    g_out = _mm_tn(h, dz, f"ffn_dw_out_{layer}", tn=D_MODEL, tk=HALF_HIDDEN)
    g_in = _mm_tn(xin_b, dgu, f"ffn_dw_in_{layer}", tn=HALF_HIDDEN, stack_cols=True)
    return below, g_in, g_out.reshape(N_CHIPS, FFN_HIDDEN // N_CHIPS, D_MODEL)


def kernel(x, even_w_in, even_b_f, even_conv_w, even_w_out, odd_w_in, odd_v_ln_g, odd_v_ln_b, odd_w_s, odd_b_s, odd_w_out, mix_ln_g, mix_ln_b, ffn_w_in, ffn_w_out, ffn_ln_g, ffn_ln_b, loss_target, m_even_w_in, m_even_b_f, m_even_conv_w, m_even_w_out, m_odd_w_in, m_odd_v_ln_g, m_odd_v_ln_b, m_odd_w_s, m_odd_b_s, m_odd_w_out, m_mix_ln_g, m_mix_ln_b, m_ffn_w_in, m_ffn_w_out, m_ffn_ln_g, m_ffn_ln_b, v_even_w_in, v_even_b_f, v_even_conv_w, v_even_w_out, v_odd_w_in, v_odd_v_ln_g, v_odd_v_ln_b, v_odd_w_s, v_odd_b_s, v_odd_w_out, v_mix_ln_g, v_mix_ln_b, v_ffn_w_in, v_ffn_w_out, v_ffn_ln_g, v_ffn_ln_b):
    t = x.shape[1]
    d = D_MODEL
    chip = 2 * lax.axis_index("x") + lax.axis_index("y")
    x2d = x[0]
    target = loss_target[0]

    small_shard = jnp.concatenate([odd_v_ln_g.reshape(2, LANES), odd_v_ln_b.reshape(2, LANES),
                                   even_conv_w.reshape(CONV_K, LANES), jnp.zeros((1, LANES), F32)], axis=0)
    first = [jnp.swapaxes(even_w_in[0], 0, 1).astype(BF16)]
    second = [even_w_out[0].astype(BF16), small_shard]
    later = [odd_w_in[0].astype(BF16), odd_w_out[0].astype(BF16), ffn_w_in[0].astype(BF16), ffn_w_in[1].astype(BF16),
             ffn_w_out[0].astype(BF16), ffn_w_out[1].astype(BF16)]
    first_h, first_tok = _split_start(first, "gather4", "gather_first_start")
    second_h, second_tok = _split_start(second, "gather4", "gather_second_start", after=first_tok)
    later_h, later_tok = _split_start(later, "gather4", "gather_later_start", after=second_tok)
    (g_ewi,) = _gathered(first_h, "gather_first_wait", later_tok)
    ewi = g_ewi.reshape(EVEN_IN, d)
    w_even_in = jnp.concatenate([ewi[:QKV], ewi[QKV + FOX_HEADS:],
                                 jnp.pad(ewi[QKV:QKV + FOX_HEADS], ((0, LANES - FOX_HEADS), (0, 0)))], axis=0)
    chunk_id = jnp.arange(GMLP_BLOCK) // CHUNK
    gmask = chunk_id[None, :] <= chunk_id[:, None]
    w_spatial = jnp.where(gmask[None], odd_w_s[0], 0.0).astype(BF16)
    bs_col = odd_b_s[0].T
    b_f_col = even_b_f.reshape(FOX_HEADS, 1)
    ln = lambda p, l: p[l:l + 1]

    qkv, bch, fl = _proj(x2d, w_even_in, [(0, QKV, BF16), (QKV, QKV + BCH, F32), (QKV + BCH, EVEN_IN_PAD, F32)], "even_proj")
    fl3 = fl[:, :FOX_HEADS].T.reshape(FOX_HEADS, t // LANES, LANES).transpose(1, 0, 2)
    c3 = _fgate_fwd(fl3, b_f_col)
    c_rows = c3.transpose(1, 0, 2).reshape(FOX_HEADS, t)
    head_lanes = lambda rows: jnp.pad(rows.T, ((0, 0), (0, LANES - FOX_HEADS)))
    qp, kp, vp, kt, vt = _attn_pack(qkv, head_lanes(c_rows))
    attn, lse = _attn_fwd(qp, kp, vt)
    g_ewo, g_small = _gathered(second_h, "gather_second_wait", attn)
    w_even_out = g_ewo.reshape(d, d)
    v_ln_g = g_small[:, 0:2].reshape(1, d)
    v_ln_b = g_small[:, 2:4].reshape(1, d)
    conv_w = g_small[:, 4:7].transpose(1, 0, 2).reshape(CONV_K, CONV_WIDTH)
    conv, x1_b, xh1, rs1 = _even_out(attn, bch, conv_w, w_even_out, x2d, ln(mix_ln_g, 0), ln(mix_ln_b, 0))
    w_odd_in, g_owo, w_fi0, w_fi1, g_fo0, g_fo1 = _gathered(later_h, "gather_later_wait", x1_b)
    w_odd_out = g_owo.reshape(d, d)
    w_ffn_in = [w_fi0, w_fi1]
    w_ffn_out = [g_fo0.reshape(FFN_HIDDEN, d), g_fo1.reshape(FFN_HIDDEN, d)]
    x2_b, ffn0 = _ffn_fwd((xh1, ln(mix_ln_g, 0), ln(mix_ln_b, 0)), x1_b, w_ffn_in[0], w_ffn_out[0],
                          ln(ffn_ln_g, 0), ln(ffn_ln_b, 0), 0)

    sv_odd, rs_odd, gated, x3_b, xh3, rs3 = _gmlp_fwd(
        x2_b, w_odd_in, v_ln_g, v_ln_b, w_spatial, bs_col, w_odd_out, (ffn0[3], ln(ffn_ln_g, 0), ln(ffn_ln_b, 0)),
        ln(mix_ln_g, 1), ln(mix_ln_b, 1))
    (sq, dz4, d_fg1, d_fb1), ffn1 = _ffn_fwd((xh3, ln(mix_ln_g, 1), ln(mix_ln_b, 1)), x3_b, w_ffn_in[1], w_ffn_out[1],
                                             ln(ffn_ln_g, 1), ln(ffn_ln_b, 1), 1, target=target)

    loss = lax.psum(0.5 / d * jnp.sum(sq), ("x", "y", "c"))
    (dz3, d_mg1, d_mb1), gi_f1, go_f1 = _ffn_bwd(dz4, ffn1, w_ffn_in[1], w_ffn_out[1], (xh3, rs3, ln(mix_ln_g, 1)), 1)

    go_odd = _mm_tn(gated, dz3, "odd_dw_out", tn=d).reshape(N_CHIPS, 1, d // N_CHIPS, d)
    da_odd, dws, dbs_col, d_vg, d_vb, dz2, d_fg0, d_fb0 = _gmlp_bwd(
        dz3, w_odd_out, sv_odd, rs_odd, v_ln_g, v_ln_b, w_spatial, bs_col, w_odd_in,
        (ffn0[3], ffn0[4], ln(ffn_ln_g, 0)))
    gi_odd = _mm_tn(x2_b, da_odd, "odd_dw_in", tn=d // 2, stack_cols=True)[:, None]
    (dz1, d_mg0, d_mb0), gi_f0, go_f0 = _ffn_bwd(dz2, ffn0, w_ffn_in[0], w_ffn_out[0], (xh1, rs1, ln(mix_ln_g, 0)), 0)

    sent_early = [gi_odd, go_odd, gi_f0[:, None], gi_f1[:, None], go_f0[:, None], go_f1[:, None]]
    early_h, early_tok = _split_start(sent_early, "scatter4", "scatter_early_start")
    qb, dob, dconv = _attn_bwd_prep(dz1, w_even_out, attn, qp, head_lanes(lse.reshape(FOX_HEADS, t)), early_tok)
    go_even = _mm_tn_pair(attn, conv, dz1, "even_dw_out").reshape(N_CHIPS, 1, d // N_CHIPS, d)
    dbch, dconv_w8 = _conv_bwd(bch, dconv, conv_w)
    dqkv, dc_col = _attn_unpack(*_attn_bwd(qb, kp, vp, dob, kt))
    dc3 = dc_col.T.reshape(FOX_HEADS, t // LANES, LANES).transpose(1, 0, 2)
    dfl3, d_bf = _fgate_bwd(dc3, fl3, b_f_col)
    dfl = jnp.concatenate([dfl3.transpose(1, 0, 2).reshape(FOX_HEADS, t).T.astype(BF16),
                           jnp.zeros((t, LANES - FOX_HEADS), BF16)], axis=1)

    dws_masked = jnp.where(gmask[None], dws, 0.0)
    rep_names = ["odd_w_s", "odd_b_s", "mix_ln_g", "mix_ln_b", "ffn_ln_g", "ffn_ln_b", "even_b_f"]
    rep_grads = [dws_masked, dbs_col.T, jnp.concatenate([d_mg0, d_mg1]), jnp.concatenate([d_mb0, d_mb1]),
                 jnp.concatenate([d_fg0, d_fg1]), jnp.concatenate([d_fb0, d_fb1]), d_bf.reshape(1, FOX_HEADS)]
    rep_w = [(odd_w_s, m_odd_w_s, v_odd_w_s), (odd_b_s, m_odd_b_s, v_odd_b_s), (mix_ln_g, m_mix_ln_g, v_mix_ln_g),
             (mix_ln_b, m_mix_ln_b, v_mix_ln_b), (ffn_ln_g, m_ffn_ln_g, v_ffn_ln_g), (ffn_ln_b, m_ffn_ln_b, v_ffn_ln_b),
             (even_b_f, m_even_b_f, v_even_b_f)]
    rep_rows = [_to_rows(gr) for gr in rep_grads]
    n_rep = sum(r.shape[0] for r in rep_rows)
    pad_rep = (-n_rep) % SUBLANES
    dconv_w = dconv_w8[:CONV_K].reshape(CONV_K, N_CHIPS, LANES).transpose(1, 0, 2).reshape(N_CHIPS * CONV_K, LANES)
    packed = jnp.concatenate(rep_rows + [jnp.zeros((pad_rep, LANES), F32), d_vg.reshape(SUBLANES, LANES),
                                         d_vb.reshape(SUBLANES, LANES), dconv_w, jnp.zeros((4, LANES), F32)], axis=0)
    small_h, small_tok = _split_start([packed], "gather8", "gather_small_start")

    swap_h, swap_tok = _split_start(_scattered(early_h, "scatter_early_wait", small_tok), "swap2", "swap_early_start")
    dw_qkv = _mm_tn(dqkv, x2d, "even_dw_qkv", tn=d, tk=QKV // 2, after=swap_tok)
    dw_bch = _mm_tn(dbch, x2d, "even_dw_bch", tn=d, tk=BCH // 2)
    dw_f = _mm_tn(dfl, x2d, "even_dw_f", tn=d)
    gi_even = jnp.concatenate([dw_qkv, dw_f[:FOX_HEADS], dw_bch], axis=0).reshape(N_CHIPS, 1, -1, LANES)
    sent_late = [gi_even, go_even]
    late_h, late_tok = _split_start(sent_late, "scatter4", "scatter_late_start")
    grad_x = _mm_back([(dqkv, 0, QKV), (dbch, QKV, QKV + BCH), (dfl, QKV + BCH, EVEN_IN_PAD)], w_even_in, dz1,
                      late_tok, "even_dx")
    mine, theirs = _split_wait(swap_h, "swap_early_wait", grad_x)
    res = {}
    res["odd_w_in"] = _adamw([mine[0], theirs[0]], odd_w_in, m_odd_w_in, v_odd_w_in, "adamw_odd_w_in")
    res["odd_w_out"] = _adamw([mine[1], theirs[1]], odd_w_out, m_odd_w_out, v_odd_w_out, "adamw_odd_w_out")
    for nm, at, (w, m, v) in (("ffn_w_in", 2, (ffn_w_in, m_ffn_w_in, v_ffn_w_in)),
                              ("ffn_w_out", 4, (ffn_w_out, m_ffn_w_out, v_ffn_w_out))):
        upper = _adamw([mine[at + 1], theirs[at + 1]], w, m, v, f"adamw_{nm}_1", layer=1)
        res[nm] = _adamw([mine[at], theirs[at]], w, m, v, f"adamw_{nm}_0", layer=0, into=upper)
    mine_late = _scattered(late_h, "scatter_late_wait", res["ffn_w_out"][0])
    theirs_late = _exchange(mine_late, "swap2", "swap_late")
    rows = lambda a: jnp.swapaxes(a, 1, 2).reshape(1, -1, LANES)
    back = lambda a: jnp.swapaxes(a.reshape(1, EVEN_IN // N_CHIPS, d), 1, 2)
    res["even_w_in"] = [back(o) for o in _adamw([mine_late[0], theirs_late[0]], rows(even_w_in), rows(m_even_w_in),
                                                rows(v_even_w_in), "adamw_even_w_in")]
    res["even_w_out"] = _adamw([mine_late[1], theirs_late[1]], even_w_out, m_even_w_out, v_even_w_out,
                               "adamw_even_w_out")
    (packed,), (gathered,) = _split_wait(small_h, "gather_small_wait", theirs_late[0])
    gathered = lax.dynamic_update_index_in_dim(gathered, packed, 4 * lax.axis_index("x") + 2 * lax.axis_index("y")
                                               + lax.axis_index("c"), 0)

    base = n_rep + pad_rep
    own_rows = jnp.concatenate([
        lax.dynamic_slice_in_dim(gathered, base + 2 * chip, 2, axis=1),
        lax.dynamic_slice_in_dim(gathered, base + SUBLANES + 2 * chip, 2, axis=1),
        lax.dynamic_slice_in_dim(gathered, base + 2 * SUBLANES + CONV_K * chip, CONV_K, axis=1),
        jnp.zeros((N_DEV, 1, LANES), F32)], axis=1)
    small_parts = jnp.concatenate([gathered[:, :base], own_rows], axis=1)[:, None]

    def pack_small(get):
        rows = [_to_rows(get(tw)) for tw in rep_w] + [jnp.zeros((pad_rep, LANES), F32)]
        rows += [get(sh).reshape(-1, LANES) for sh in ((odd_v_ln_g, m_odd_v_ln_g, v_odd_v_ln_g),
                                                       (odd_v_ln_b, m_odd_v_ln_b, v_odd_v_ln_b),
                                                       (even_conv_w, m_even_conv_w, v_even_conv_w))]
        return jnp.concatenate(rows + [jnp.zeros((1, LANES), F32)], axis=0)[None]

    small_out = _adamw([small_parts], pack_small(lambda tw: tw[0]), pack_small(lambda tw: tw[1]),
                       pack_small(lambda tw: tw[2]), "adamw_small")

    def unpack_small(rows3):
        rows = rows3[0]
        out, off = {}, 0
        for nm, (w, _, _), r in zip(rep_names, rep_w, rep_rows):
            out[nm] = rows[off:off + r.shape[0]].reshape(-1)[:w.size].reshape(w.shape)
            off += r.shape[0]
        off += pad_rep
        out["odd_v_ln_g"] = rows[off:off + 2].reshape(odd_v_ln_g.shape)
        out["odd_v_ln_b"] = rows[off + 2:off + 4].reshape(odd_v_ln_b.shape)
        out["even_conv_w"] = rows[off + 4:off + 4 + CONV_K].reshape(even_conv_w.shape)
        return out

    small = [unpack_small(o) for o in small_out]
    order = ["even_w_in", "even_b_f", "even_conv_w", "even_w_out", "odd_w_in", "odd_v_ln_g", "odd_v_ln_b", "odd_w_s",
             "odd_b_s", "odd_w_out", "mix_ln_g", "mix_ln_b", "ffn_w_in", "ffn_w_out", "ffn_ln_g", "ffn_ln_b"]
    outs = [loss, grad_x[None]]
    for kind in range(4):
        for nm in order:
            outs.append(res[nm][kind] if nm in res else small[kind][nm])
    return tuple(outs)
```

```python
import math

import jax
import jax.numpy as jnp
from jax import lax
from jax.experimental import pallas as pl
from jax.experimental.pallas import tpu as pltpu

F32 = jnp.float32
BF16 = jnp.bfloat16

D_MODEL = 1024
FOX_HEADS = 8
HEAD_DIM = 64
FOX_WIDTH = FOX_HEADS * HEAD_DIM
CONV_WIDTH = 512
CONV_K = 3
QKV = 3 * FOX_WIDTH
BCH = 3 * CONV_WIDTH
EVEN_IN = QKV + FOX_HEADS + BCH
EVEN_IN_PAD = QKV + BCH + 128
GMLP_BLOCK = 128
GMLP_GROUPS = 8
CHUNK = 64
FFN_HIDDEN = 2816
HALF_HIDDEN = FFN_HIDDEN // 2
ALPHA = 4.0 ** 0.25
LN_EPS = 1e-5
ADAM_LR = 0.001
ADAM_B1 = 0.9
ADAM_B2 = 0.999
ADAM_EPS = 1e-08
ADAM_WD = 0.01
ADAM_STEP = 10
N_CHIPS = 4
N_DEV = 8
LANES = 128
SUBLANES = 8
ROW_TILE = 512
FFN_FUSED_ROW_TILE = 256
REDUCE_TILE = 2048
ATT_BLOCK = 512
ATT_FWD_HEADS = 8
ATT_BWD_HEADS = 8
ADAMW_BLOCK_BYTES = 2 ** 20
VMEM_LIMIT = 56 * 2 ** 20
ATT_BWD_VMEM_LIMIT = 60 * 2 ** 20
NEG = -1e30
MESH = pl.DeviceIdType.MESH
HIGHEST = lax.Precision.HIGHEST
Q_C, Q_ONE, Q_LSE = 64, 67, 70
K_ONE, K_C, K_ONE2 = 64, 67, 70
V_ONE = 64
DO_DELTA = 65
NT = (((1,), (1,)), ((), ()))
TN = (((0,), (0,)), ((), ()))


def _cp(limit=VMEM_LIMIT):
    return pltpu.CompilerParams(vmem_limit_bytes=limit)


def _resident(shape):
    zeros = (0,) * len(shape)
    return pl.BlockSpec(shape, lambda *_: zeros, pipeline_mode=pl.Buffered(1))


def _sds(shape, dtype):
    return jax.ShapeDtypeStruct(tuple(shape), dtype)


_MASKS = {
    "gather4": [(1, 0, 0), (0, 1, 0), (1, 1, 0)],
    "scatter4": [(1, 0, 0), (0, 1, 0), (1, 1, 0)],
    "swap2": [(0, 0, 1)],
    "gather8": [(0, 0, 1), (0, 1, 0), (0, 1, 1), (1, 0, 0), (1, 0, 1), (1, 1, 0), (1, 1, 1)],
}


def _exchange(arrs, mode, name):
    n = len(arrs)
    masks = _MASKS[mode]
    npeer = len(masks)
    lead = {"gather4": N_CHIPS, "gather8": N_DEV}.get(mode)
    out_shapes = [_sds(((lead,) if lead else ()) + a.shape, a.dtype) for a in arrs]

    def body(*refs):
        ins, outs = refs[:n], refs[n:2 * n]
        send_sems, recv_sems, loc_sems = refs[2 * n:]
        x, y, c = lax.axis_index("x"), lax.axis_index("y"), lax.axis_index("c")
        chip, dev = 2 * x + y, 4 * x + 2 * y + c
        sends, recvs, locs = [], [], []
        for k in range(n):
            if mode == "gather4":
                locs.append(pltpu.make_async_copy(ins[k], outs[k].at[chip], loc_sems.at[k]))
            elif mode == "scatter4":
                locs.append(pltpu.make_async_copy(ins[k].at[chip], outs[k].at[chip], loc_sems.at[k]))
            elif mode == "gather8":
                locs.append(pltpu.make_async_copy(ins[k], outs[k].at[dev], loc_sems.at[k]))
        for cp in locs:
            cp.start()
        for k in range(n):
            for j, (dx, dy, dc) in enumerate(masks):
                px = 1 - x if dx else x
                py = 1 - y if dy else y
                pc = 1 - c if dc else c
                pchip, pdev = 2 * px + py, 4 * px + 2 * py + pc
                if mode == "gather4":
                    src, dst, land = ins[k], outs[k].at[chip], outs[k].at[pchip]
                elif mode == "scatter4":
                    src, dst, land = ins[k].at[pchip], outs[k].at[chip], outs[k].at[pchip]
                elif mode == "swap2":
                    src, dst, land = ins[k], outs[k], outs[k]
                else:
                    src, dst, land = ins[k], outs[k].at[dev], outs[k].at[pdev]
                s = k * npeer + j
                kw = dict(send_sem=send_sems.at[s], recv_sem=recv_sems.at[s], device_id=(px, py, pc),
                          device_id_type=MESH)
                cp = pltpu.make_async_remote_copy(src_ref=src, dst_ref=dst, **kw)
                cp.start()
                sends.append(cp)
                recvs.append(pltpu.make_async_remote_copy(src_ref=src, dst_ref=land, **kw))
        for cp in recvs:
            cp.wait_recv()
        for cp in sends:
            cp.wait_send()
        for cp in locs:
            cp.wait()

    any_spec = pl.BlockSpec(memory_space=pl.ANY)
    outs = pl.pallas_call(
        body,
        out_shape=out_shapes,
        in_specs=[any_spec] * n,
        out_specs=[any_spec] * n,
        scratch_shapes=[pltpu.SemaphoreType.DMA((n * npeer,)), pltpu.SemaphoreType.DMA((n * npeer,)),
                        pltpu.SemaphoreType.DMA((max(n, 1),))],
        name=name,
    )(*arrs)
    return list(outs)


_HBM_SPEC = pl.BlockSpec(memory_space=pltpu.HBM)
_SEM_SPEC = pl.BlockSpec(memory_space=pltpu.SEMAPHORE)
_ANY_SPEC = pl.BlockSpec(memory_space=pl.ANY)
_EFFECT = pltpu.SideEffectType.DATAFLOW_SIDE_EFFECTING


def _split_copies(mode, ins, lands, send_sems, recv_sems):
    x, y, c = lax.axis_index("x"), lax.axis_index("y"), lax.axis_index("c")
    chip, dev = 2 * x + y, 4 * x + 2 * y + c
    masks = _MASKS[mode]
    out = []
    for k in range(len(ins)):
        for j, (dx, dy, dc) in enumerate(masks):
            px = 1 - x if dx else x
            py = 1 - y if dy else y
            pc = 1 - c if dc else c
            pchip, pdev = 2 * px + py, 4 * px + 2 * py + pc
            if mode == "gather4":
                src, dst, land = ins[k], lands[k].at[chip], lands[k].at[pchip]
            elif mode == "scatter4":
                src, dst, land = ins[k].at[pchip], lands[k].at[chip], lands[k].at[pchip]
            elif mode == "swap2":
                src, dst, land = ins[k], lands[k], lands[k]
            else:
                src, dst, land = ins[k], lands[k].at[dev], lands[k].at[pdev]
            s = k * len(masks) + j
            kw = dict(send_sem=send_sems.at[s], recv_sem=recv_sems.at[s], device_id=(px, py, pc), device_id_type=MESH)
            out.append((pltpu.make_async_remote_copy(src_ref=src, dst_ref=dst, **kw),
                        pltpu.make_async_remote_copy(src_ref=src, dst_ref=land, **kw)))
    return out


def _split_start(arrs, mode, name, after=None):
    n = len(arrs)
    nsem = n * len(_MASKS[mode])
    lead = {"gather4": (N_CHIPS,), "gather8": (N_DEV,)}.get(mode, ())
    land_shapes = [lead + a.shape for a in arrs]

    def body(*refs):
        ins, lands = refs[:n], refs[n:2 * n]
        outs = refs[2 * n + (after is not None):]
        for start, _ in _split_copies(mode, ins, lands, outs[0], outs[1]):
            start.start()
        outs[-1][...] = jnp.zeros(outs[-1].shape, F32)

    srcs = [pltpu.with_memory_space_constraint(a, pltpu.HBM) for a in arrs]
    empties = [pltpu.with_memory_space_constraint(lax.empty(s, a.dtype), pltpu.HBM) for s, a in zip(land_shapes, arrs)]
    res = pl.pallas_call(
        body, name=name,
        out_shape=(pltpu.SemaphoreType.DMA((nsem,)), pltpu.SemaphoreType.DMA((nsem,)),
                   *[pltpu.HBM(a.shape, a.dtype) for a in arrs],
                   *[pltpu.HBM(s, a.dtype) for s, a in zip(land_shapes, arrs)],
                   _sds((SUBLANES, LANES), F32)),
        in_specs=[_HBM_SPEC] * (2 * n) + ([_ANY_SPEC] if after is not None else []),
        out_specs=(_SEM_SPEC, _SEM_SPEC, *[_HBM_SPEC] * (2 * n), pl.BlockSpec(memory_space=pltpu.VMEM)),
        input_output_aliases={k: 2 + k for k in range(2 * n)},
        compiler_params=pltpu.CompilerParams(has_side_effects=_EFFECT),
    )(*srcs, *empties, *([after] if after is not None else []))
    return dict(mode=mode, n=n, sems=res[:2], bufs=res[2:2 + 2 * n]), res[-1]


def _split_wait(handle, name, after):
    n, mode = handle["n"], handle["mode"]

    def body(*refs):
        ins, lands = refs[:n], refs[n:2 * n]
        send_sems, recv_sems = refs[2 * n], refs[2 * n + 1]
        for _, arrival in _split_copies(mode, ins, lands, send_sems, recv_sems):
            arrival.wait_send()
            arrival.wait_recv()

    bufs = handle["bufs"]
    res = pl.pallas_call(
        body, name=name,
        out_shape=tuple(pltpu.HBM(b.shape, b.dtype) for b in bufs),
        in_specs=[_HBM_SPEC] * (2 * n) + [_SEM_SPEC, _SEM_SPEC, _ANY_SPEC],
        out_specs=tuple([_HBM_SPEC] * (2 * n)),
        input_output_aliases={k: k for k in range(2 * n)},
        compiler_params=pltpu.CompilerParams(has_side_effects=_EFFECT),
    )(*bufs, *handle["sems"], after)
    return list(res[:n]), list(res[n:])


def _with_own(landed, own):
    chip = 2 * lax.axis_index("x") + lax.axis_index("y")
    return lax.dynamic_update_index_in_dim(landed, own, chip, 0)


def _gathered(handle, name, after):
    sent, landed = _split_wait(handle, name, after)
    return [_with_own(g, own) for g, own in zip(landed, sent)]


def _scattered(handle, name, after):
    chip = 2 * lax.axis_index("x") + lax.axis_index("y")
    sent, landed = _split_wait(handle, name, after)
    return [_with_own(r, lax.dynamic_index_in_dim(g, chip, 0, keepdims=False)) for r, g in zip(landed, sent)]


def _sigmoid(x):
    return 0.5 * jnp.tanh(0.5 * x) + 0.5


def _log_sigmoid(x):
    e = jnp.exp(-jnp.abs(x))
    log1p = jnp.where(e < 1e-2, e * (1.0 - e * (0.5 - e * (1.0 / 3.0))), jnp.log(1.0 + e))
    return jnp.minimum(x, 0.0) - log1p


def _ln_fwd(z):
    mu = jnp.mean(z, axis=-1, keepdims=True)
    zc = z - mu
    var = jnp.mean(zc * zc, axis=-1, keepdims=True)
    rstd = lax.rsqrt(var + LN_EPS)
    return zc * rstd, rstd


def _ln_bwd(dy, xhat, rstd, g):
    dxh = dy * g
    m1 = jnp.mean(dxh, axis=-1, keepdims=True)
    m2 = jnp.mean(dxh * xhat, axis=-1, keepdims=True)
    dz = rstd * (dxh - m1 - xhat * m2)
    return dz, jnp.sum(dy * xhat, axis=0, keepdims=True), jnp.sum(dy, axis=0, keepdims=True)


def _shift_down(z, halo):
    r = lax.broadcasted_iota(jnp.int32, z.shape, 0)
    z1 = jnp.where(r == 0, halo[7:8, :], pltpu.roll(z, 1, 0))
    z2 = jnp.where(r == 0, halo[6:7, :], jnp.where(r == 1, halo[7:8, :], pltpu.roll(z, 2, 0)))
    return z1, z2


def _shift_up(z, halo):
    n = z.shape[0]
    r = lax.broadcasted_iota(jnp.int32, z.shape, 0)
    z1 = jnp.where(r == n - 1, halo[0:1, :], pltpu.roll(z, n - 1, 0))
    z2 = jnp.where(r == n - 1, halo[1:2, :], jnp.where(r == n - 2, halo[0:1, :], pltpu.roll(z, n - 2, 0)))
    return z1, z2


def _triangle_ones(prefix):
    r = lax.broadcasted_iota(jnp.int32, (LANES, LANES), 0)
    c = lax.broadcasted_iota(jnp.int32, (LANES, LANES), 1)
    return ((r <= c) if prefix else (r >= c)).astype(F32)


def _same_head_chunks(rows, earlier):
    r = lax.broadcasted_iota(jnp.int32, (rows, rows), 0)
    c = lax.broadcasted_iota(jnp.int32, (rows, rows), 1)
    same = r % FOX_HEADS == c % FOX_HEADS
    return jnp.logical_and(same, (c < r) if earlier else (c > r)).astype(F32)


def _accumulate(ref, first, value):
    @pl.when(first)
    def _():
        ref[...] = value

    @pl.when(jnp.logical_not(first))
    def _():
        ref[...] += value


def _proj(x, wt, splits, name):
    t, k = x.shape
    tm = min(ROW_TILE, t)

    def body(x_ref, w_ref, *outs):
        a = x_ref[...].astype(BF16)
        for (lo, hi, dt), o in zip(splits, outs):
            o[...] = lax.dot_general(a, w_ref[lo:hi, :], NT, preferred_element_type=F32).astype(dt)
        outs[-1][...] = a

    row = lambda i: (i, 0)
    return pl.pallas_call(
        body, grid=(t // tm,),
        in_specs=[pl.BlockSpec((tm, k), row), _resident(wt.shape)],
        out_specs=[pl.BlockSpec((tm, hi - lo), row) for lo, hi, _ in splits] + [pl.BlockSpec((tm, k), row)],
        out_shape=[_sds((t, hi - lo), dt) for lo, hi, dt in splits] + [_sds((t, k), BF16)],
        compiler_params=_cp(), name=name)(x, wt)


def _fgate_fwd(fl3, b_f):
    nc = fl3.shape[0]
    rows = nc * FOX_HEADS

    def body(f_ref, b_ref, c_ref):
        within = jnp.dot(_log_sigmoid(f_ref[...] + b_ref[...]), _triangle_ones(True), precision=HIGHEST,
                         preferred_element_type=F32)
        totals = jnp.broadcast_to(within[:, LANES - 1:LANES], within.shape)
        c_ref[...] = within + jnp.dot(_same_head_chunks(rows, earlier=True), totals, precision=HIGHEST,
                                      preferred_element_type=F32)

    c2 = pl.pallas_call(body, out_shape=_sds((rows, LANES), F32), name="fgate_fwd")(
        fl3.reshape(rows, LANES), jnp.tile(b_f, (nc, 1)))
    return c2.reshape(fl3.shape)


def _fgate_bwd(dc3, fl3, b_f):
    nc = fl3.shape[0]
    rows = nc * FOX_HEADS

    def body(dc_ref, f_ref, b_ref, df_ref, db_ref):
        within = jnp.dot(dc_ref[...], _triangle_ones(False), precision=HIGHEST, preferred_element_type=F32)
        totals = jnp.broadcast_to(within[:, 0:1], within.shape)
        dlf = within + jnp.dot(_same_head_chunks(rows, earlier=False), totals, precision=HIGHEST,
                               preferred_element_type=F32)
        df = dlf * (1.0 - _sigmoid(f_ref[...] + b_ref[...]))
        df_ref[...] = df
        head = lax.broadcasted_iota(jnp.int32, (FOX_HEADS, rows), 0)
        row = lax.broadcasted_iota(jnp.int32, (FOX_HEADS, rows), 1)
        of_head = (row % FOX_HEADS == head).astype(F32)
        per_row = jnp.broadcast_to(jnp.sum(df, axis=1, keepdims=True), df.shape)
        db_ref[...] = jnp.dot(of_head, per_row, precision=HIGHEST, preferred_element_type=F32)[:, 0:1]

    df2, db = pl.pallas_call(body, out_shape=[_sds((rows, LANES), F32), _sds((FOX_HEADS, 1), F32)], name="fgate_bwd")(
        dc3.reshape(rows, LANES), fl3.reshape(rows, LANES), jnp.tile(b_f, (nc, 1)))
    return df2.reshape(fl3.shape), db


def _split3(c):
    hi = c.astype(BF16).astype(F32)
    mid = (c - hi).astype(BF16).astype(F32)
    lo = (c - hi - mid).astype(BF16).astype(F32)
    return hi, mid, lo


PIECE_ONE = 3 * FOX_HEADS


def _piece_rows(values):
    hi, mid, lo = _split3(values)
    lane = lax.broadcasted_iota(jnp.int32, values.shape, 1)
    row = hi + pltpu.roll(mid, FOX_HEADS, 1) + pltpu.roll(lo, 2 * FOX_HEADS, 1) + jnp.where(lane == PIECE_ONE, 1.0, 0.0)
    return row.astype(BF16)


def _piece_selector(start, sign, ones=()):
    sel = [[0.0] * FOX_WIDTH for _ in range(LANES)]
    for h in range(FOX_HEADS):
        for n in range(3):
            sel[n * FOX_HEADS + h][h * HEAD_DIM + start - HEAD_DIM + n] = sign
        for lane in ones:
            sel[PIECE_ONE][h * HEAD_DIM + lane - HEAD_DIM] = 1.0
    return jnp.asarray(sel, BF16)


def _attn_pack(qkv, c_pad):
    t = qkv.shape[0]
    tm = min(ROW_TILE, t)
    hd = HEAD_DIM
    sel_q = _piece_selector(Q_C, 1.0, range(Q_ONE, Q_ONE + 3))
    sel_k = _piece_selector(K_C, -1.0, [*range(K_ONE, K_ONE + 3), *range(K_ONE2, K_ONE2 + 3)])
    sel_v = _piece_selector(HEAD_DIM, 0.0, range(V_ONE, V_ONE + 4))

    def body(x_ref, c_ref, sq_ref, sk_ref, sv_ref, qp_ref, kp_ref, vp_ref, kt_ref, vt_ref):
        pieces = _piece_rows(c_ref[...])
        q_extra = jnp.dot(pieces, sq_ref[...], preferred_element_type=F32).astype(BF16)
        k_extra = jnp.dot(pieces, sk_ref[...], preferred_element_type=F32).astype(BF16)
        v_extra = jnp.dot(pieces, sv_ref[...], preferred_element_type=F32).astype(BF16)
        for h in range(FOX_HEADS):
            hs = slice(h * hd, (h + 1) * hd)
            qp_ref[h, :, :hd] = (x_ref[:, hs].astype(F32) * (hd ** -0.5)).astype(BF16)
            qp_ref[h, :, hd:] = q_extra[:, hs]
            kp_ref[h, :, :hd] = x_ref[:, FOX_WIDTH + h * hd:FOX_WIDTH + (h + 1) * hd]
            kp_ref[h, :, hd:] = k_extra[:, hs]
            vp_ref[h, :, :hd] = x_ref[:, 2 * FOX_WIDTH + h * hd:2 * FOX_WIDTH + (h + 1) * hd]
            vp_ref[h, :, hd:] = v_extra[:, hs]
            kt_ref[h] = kp_ref[h].T
            vt_ref[h] = vp_ref[h].T

    row3 = pl.BlockSpec((FOX_HEADS, tm, LANES), lambda i: (0, i, 0))
    col3 = pl.BlockSpec((FOX_HEADS, LANES, tm), lambda i: (0, 0, i))
    sel = _resident(sel_q.shape)
    return pl.pallas_call(
        body, grid=(t // tm,),
        in_specs=[pl.BlockSpec((tm, QKV), lambda i: (i, 0)), pl.BlockSpec((tm, LANES), lambda i: (i, 0)), sel, sel, sel],
        out_specs=[row3, row3, row3, col3, col3],
        out_shape=[_sds((FOX_HEADS, t, LANES), BF16)] * 3 + [_sds((FOX_HEADS, LANES, t), BF16)] * 2,
        compiler_params=_cp(), name="attn_pack")(qkv, c_pad, sel_q, sel_k, sel_v)


def _triangle(nq, key_major):
    if key_major:
        pairs = [(i, j) for j in range(nq) for i in range(j, nq)]
    else:
        pairs = [(i, j) for i in range(nq) for j in range(i + 1)]
    return jnp.asarray([p[0] for p in pairs], jnp.int32), jnp.asarray([p[1] for p in pairs], jnp.int32)


def _attn_fwd(qp, kp, vt):
    t = qp.shape[1]
    bq = min(ATT_BLOCK, t)
    nq = t // bq
    nh = ATT_FWD_HEADS
    i_tab, j_tab = _triangle(nq, key_major=False)

    def body(it_ref, jt_ref, q_ref, k_ref, vt_ref, o_ref, lse_ref, m_sc, acc_sc):
        s = pl.program_id(1)
        i, j = it_ref[s], jt_ref[s]

        @pl.when(j == 0)
        def _():
            m_sc[...] = jnp.full(m_sc.shape, NEG, F32)
            acc_sc[...] = jnp.zeros(acc_sc.shape, F32)

        def sweep(masked):
            scores = lambda h: lax.dot_general(k_ref[h], q_ref[h], NT, preferred_element_type=F32)

            def accumulate(h, pt, rescale):
                acc_sc[h] = rescale * acc_sc[h] + jnp.dot(vt_ref[h], pt, preferred_element_type=F32)

            ahead, behind = scores(0), None
            for h in range(nh):
                st = ahead
                if h + 1 < nh:
                    ahead = scores(h + 1)
                if behind is not None:
                    accumulate(*behind)
                if masked:
                    key = lax.broadcasted_iota(jnp.int32, (bq, bq), 0)
                    qry = lax.broadcasted_iota(jnp.int32, (bq, bq), 1)
                    st = jnp.where(key <= qry, st, NEG)
                m_prev = m_sc[h]
                m_new = jnp.maximum(m_prev, jnp.max(st, axis=0, keepdims=True))
                behind = (h, jnp.exp(st - m_new).astype(BF16), jnp.exp(m_prev - m_new))
                m_sc[h] = m_new
            accumulate(*behind)

        @pl.when(j < i)
        def _():
            sweep(False)

        @pl.when(j == i)
        def _():
            sweep(True)
            for h in range(nh):
                acc = acc_sc[h]
                denom = acc[V_ONE:V_ONE + 1, :]
                o_ref[:, h * HEAD_DIM:(h + 1) * HEAD_DIM] = (acc[:HEAD_DIM, :] / denom).T.astype(BF16)
                lse_ref[h] = m_sc[h] + jnp.log(denom)

    grid_spec = pltpu.PrefetchScalarGridSpec(
        num_scalar_prefetch=2, grid=(FOX_HEADS // nh, i_tab.shape[0]),
        in_specs=[pl.BlockSpec((nh, bq, LANES), lambda hp, s, it, jt: (hp, it[s], 0)),
                  pl.BlockSpec((nh, bq, LANES), lambda hp, s, it, jt: (hp, jt[s], 0)),
                  pl.BlockSpec((nh, LANES, bq), lambda hp, s, it, jt: (hp, 0, jt[s]))],
        out_specs=[pl.BlockSpec((bq, nh * HEAD_DIM), lambda hp, s, it, jt: (it[s], hp)),
                   pl.BlockSpec((nh, 1, bq), lambda hp, s, it, jt: (hp, 0, it[s]))],
        scratch_shapes=[pltpu.VMEM((nh, 1, bq), F32), pltpu.VMEM((nh, LANES, bq), F32)])
    return pl.pallas_call(body, grid_spec=grid_spec,
                          out_shape=[_sds((t, FOX_WIDTH), BF16), _sds((FOX_HEADS, 1, t), F32)],
                          compiler_params=_cp(), name="attn_fwd")(i_tab, j_tab, qp, kp, vt)


def _even_out(attn, bch, conv_w, w_out, x, g, b):
    t, d = x.shape
    tm = min(ROW_TILE, t)
    halo_blocks = tm // SUBLANES
    cw = CONV_WIDTH

    def body(a_ref, cur_ref, prev_ref, cw_ref, wo_ref, x_ref, g_ref, b_ref, conv_ref, yb_ref, xh_ref, rs_ref):
        i = pl.program_id(0)
        z = cur_ref[:, cw:2 * cw] * cur_ref[:, 2 * cw:]
        zp = jnp.where(i == 0, 0.0, prev_ref[:, cw:2 * cw] * prev_ref[:, 2 * cw:])
        z1, z2 = _shift_down(z, zp)
        conv = (cur_ref[:, :cw] * (cw_ref[0:1, :] * z2 + cw_ref[1:2, :] * z1 + cw_ref[2:3, :] * z)).astype(BF16)
        conv_ref[...] = conv
        pre = (ALPHA * x_ref[...] + jnp.dot(a_ref[...], wo_ref[:FOX_WIDTH, :], preferred_element_type=F32)
               + jnp.dot(conv, wo_ref[FOX_WIDTH:, :], preferred_element_type=F32))
        xhat, rstd = _ln_fwd(pre)
        yb_ref[...] = (xhat * g_ref[...] + b_ref[...]).astype(BF16)
        xh_ref[...] = xhat
        rs_ref[...] = rstd

    row = lambda i: (i, 0)
    full, half = pl.BlockSpec((tm, d), row), pl.BlockSpec((tm, cw), row)
    return pl.pallas_call(
        body, grid=(t // tm,),
        in_specs=[half, pl.BlockSpec((tm, BCH), row),
                  pl.BlockSpec((SUBLANES, BCH), lambda i: (jnp.maximum(i * halo_blocks - 1, 0), 0)),
                  _resident(conv_w.shape), _resident(w_out.shape), full, _resident(g.shape), _resident(b.shape)],
        out_specs=[half, full, full, pl.BlockSpec((tm, 1), row)],
        out_shape=[_sds((t, cw), BF16), _sds((t, d), BF16), _sds((t, d), F32), _sds((t, 1), F32)],
        compiler_params=_cp(), name="even_out")(attn, bch, bch, conv_w, w_out, x, g, b)


def _gmlp_fwd(x, w_in, vg, vb, wm, bs_col, w_out, res_ln, g, b):
    t, d = x.shape
    tm = min(ROW_TILE, t)
    gb = GMLP_BLOCK
    rxh, rg, rb = res_ln

    def body(x_ref, w_ref, vg_ref, vb_ref, wm_ref, bs_ref, wo_ref, rxh_ref, rg_ref, rb_ref, g_ref, b_ref,
             sv_ref, rs_ref, o_ref, yb_ref, xh_ref, rsy_ref, a_sc):
        xb = x_ref[...].astype(BF16)
        nc = w_ref.shape[2]
        for j in range(w_ref.shape[0]):
            a_sc[:, j * nc:(j + 1) * nc] = jnp.dot(xb, w_ref[j], preferred_element_type=F32)
        halves = []
        for half in range(2):
            a = a_sc[:, half * d:(half + 1) * d]
            cdf = 0.5 * (1.0 + lax.erf(a * (2.0 ** -0.5)))
            halves.append(a * cdf)
            slope = cdf + a * (jnp.exp(-0.5 * a * a) * (1.0 / math.sqrt(2.0 * math.pi)))
            sv_ref[:, (2 * half + 1) * d:(2 * half + 2) * d] = slope.astype(BF16)
        u = halves[0]
        vhat, rstd = _ln_fwd(halves[1])
        sv_ref[:, :d] = u.astype(BF16)
        sv_ref[:, 2 * d:3 * d] = vhat.astype(BF16)
        rs_ref[...] = rstd
        vln = (vhat * vg_ref[...] + vb_ref[...]).astype(BF16)
        for blk in range(tm // gb):
            rs = slice(blk * gb, (blk + 1) * gb)
            for gi in range(GMLP_GROUPS):
                cs = slice(gi * gb, (gi + 1) * gb)
                s = jnp.dot(wm_ref[gi], vln[rs, cs], preferred_element_type=F32) + bs_ref[:, gi:gi + 1]
                o_ref[rs, cs] = (u[rs, cs] * s).astype(BF16)
        z = ALPHA * (rxh_ref[...] * rg_ref[...] + rb_ref[...]) + jnp.dot(o_ref[...], wo_ref[...], preferred_element_type=F32)
        xhat, rstd_y = _ln_fwd(z)
        yb_ref[...] = (xhat * g_ref[...] + b_ref[...]).astype(BF16)
        xh_ref[...] = xhat
        rsy_ref[...] = rstd_y

    row = lambda i: (i, 0)
    full, col, vec = pl.BlockSpec((tm, d), row), pl.BlockSpec((tm, 1), row), _resident(g.shape)
    return pl.pallas_call(
        body, grid=(t // tm,),
        in_specs=[full, _resident(w_in.shape), _resident(vg.shape), _resident(vb.shape),
                  _resident(wm.shape), _resident(bs_col.shape), _resident(w_out.shape), full, vec, vec, vec, vec],
        out_specs=[pl.BlockSpec((tm, 4 * d), row), col, full, full, full, col],
        out_shape=[_sds((t, 4 * d), BF16), _sds((t, 1), F32), _sds((t, d), BF16), _sds((t, d), BF16), _sds((t, d), F32),
                   _sds((t, 1), F32)],
        scratch_shapes=[pltpu.VMEM((tm, 2 * d), F32)],
        compiler_params=_cp(), name="gmlp_fwd")(x, w_in, vg, vb, wm, bs_col, w_out, rxh, rg, rb, g, b)


def _mm_back(pairs, wt, res, after, name):
    t = pairs[0][0].shape[0]
    k = wt.shape[1]
    tm = min(ROW_TILE, t)
    n = len(pairs)

    def body(after_ref, *refs):
        a_refs, w_ref, res_ref, o_ref = refs[:n], refs[n], refs[n + 1], refs[n + 2]
        dx = ALPHA * res_ref[...]
        for a_ref, (_, lo, hi) in zip(a_refs, pairs):
            dx = dx + jnp.dot(a_ref[...].astype(BF16), w_ref[lo:hi, :], preferred_element_type=F32)
        o_ref[...] = dx

    row = lambda i: (i, 0)
    return pl.pallas_call(
        body, grid=(t // tm,),
        in_specs=[_ANY_SPEC] + [pl.BlockSpec((tm, a.shape[1]), row) for a, _, _ in pairs]
        + [_resident(wt.shape), pl.BlockSpec((tm, k), row)],
        out_specs=pl.BlockSpec((tm, k), row), out_shape=_sds((t, k), F32),
        compiler_params=_cp(), name=name)(after, *[a for a, _, _ in pairs], wt, res)


def _mm_tn(a, b, name, *, tn, tk=None, tt=None, stack_cols=False, out_dtype=BF16, after=None):
    t, k = a.shape
    n = b.shape[1]
    tk = k if tk is None else tk
    tt = min(REDUCE_TILE if tt is None else tt, t)
    nt = t // tt

    def body(a_ref, b_ref, *rest):
        o_ref, acc_ref = rest[after is not None:]
        s = pl.program_id(2)
        part = lax.dot_general(a_ref[...].astype(BF16), b_ref[...].astype(BF16), TN, preferred_element_type=F32)
        _accumulate(acc_ref, s == 0, part)

        @pl.when(s == nt - 1)
        def _():
            o_ref[...] = acc_ref[...].astype(out_dtype).reshape(o_ref.shape)

    if stack_cols:
        assert tk == k
        out_spec = pl.BlockSpec((1, k, tn), lambda kk, j, s: (j, 0, 0))
        out_shape = _sds((n // tn, k, tn), out_dtype)
    else:
        out_spec = pl.BlockSpec((tk, tn), lambda kk, j, s: (kk, j))
        out_shape = _sds((k, n), out_dtype)
    return pl.pallas_call(
        body, grid=(k // tk, n // tn, nt),
        in_specs=[pl.BlockSpec((tt, tk), lambda kk, j, s: (s, kk)), pl.BlockSpec((tt, tn), lambda kk, j, s: (s, j))]
        + ([_ANY_SPEC] if after is not None else []),
        out_specs=out_spec, out_shape=out_shape,
        scratch_shapes=[pltpu.VMEM((tk, tn), F32)],
        compiler_params=_cp(), name=name)(a, b, *([after] if after is not None else []))


def _mm_tn_pair(a1, a2, b, name):
    t, k = a1.shape
    n = b.shape[1]
    tt = min(REDUCE_TILE, t)
    nt = t // tt

    def body(a1_ref, a2_ref, b_ref, o_ref, acc_ref):
        s = pl.program_id(0)
        bb = b_ref[...].astype(BF16)
        part = jnp.concatenate([lax.dot_general(a_ref[...].astype(BF16), bb, TN, preferred_element_type=F32)
                                for a_ref in (a1_ref, a2_ref)], axis=0)
        _accumulate(acc_ref, s == 0, part)

        @pl.when(s == nt - 1)
        def _():
            o_ref[...] = acc_ref[...].astype(BF16)

    rows = pl.BlockSpec((tt, k), lambda s: (s, 0))
    return pl.pallas_call(
        body, grid=(nt,),
        in_specs=[rows, rows, pl.BlockSpec((tt, n), lambda s: (s, 0))],
        out_specs=pl.BlockSpec((2 * k, n), lambda s: (0, 0)), out_shape=_sds((2 * k, n), BF16),
        scratch_shapes=[pltpu.VMEM((2 * k, n), F32)],
        compiler_params=_cp(), name=name)(a1, a2, b)


def _ffn_bwd_rows(dz, wo, gu, wi, ln_below, name):
    t, d = dz.shape
    tm = min(FFN_FUSED_ROW_TILE, t)
    hh = HALF_HIDDEN
    xhat, rstd, g = ln_below

    def body(dz_ref, wo_ref, gu_ref, wi_ref, xh_ref, rs_ref, g_ref, dgu_ref, dzb_ref, dg_ref, db_ref):
        first = pl.program_id(0) == 0
        a = dz_ref[...].astype(BF16)
        for c in range(2):
            gs, us = slice(c * hh, (c + 1) * hh), slice(FFN_HIDDEN + c * hh, FFN_HIDDEN + (c + 1) * hh)
            dh = lax.dot_general(a, wo_ref[gs, :], NT, preferred_element_type=F32)
            dgu_ref[:, gs] = (dh * gu_ref[:, gs].astype(F32)).astype(BF16)
            dgu_ref[:, us] = (dh * gu_ref[:, us].astype(F32)).astype(BF16)
        dx = ALPHA * dz_ref[...]
        for j in range(wi_ref.shape[0]):
            dx = dx + lax.dot_general(dgu_ref[:, j * hh:(j + 1) * hh], wi_ref[j], NT, preferred_element_type=F32)
        dzb, dg, db = _ln_bwd(dx, xh_ref[...], rs_ref[...], g_ref[...])
        dzb_ref[...] = dzb
        _accumulate(dg_ref, first, dg)
        _accumulate(db_ref, first, db)

    row = lambda i: (i, 0)
    wide, full = pl.BlockSpec((tm, 2 * FFN_HIDDEN), row), pl.BlockSpec((tm, d), row)
    vec = pl.BlockSpec((1, d), lambda i: (0, 0))
    return pl.pallas_call(
        body, grid=(t // tm,),
        in_specs=[full, _resident(wo.shape), wide, _resident(wi.shape), full, pl.BlockSpec((tm, 1), row),
                  _resident(g.shape)],
        out_specs=[wide, full, vec, vec],
        out_shape=[_sds((t, 2 * FFN_HIDDEN), BF16), _sds((t, d), F32), _sds((1, d), F32), _sds((1, d), F32)],
        compiler_params=_cp(), name=name)(dz, wo, gu, wi, xhat, rstd, g)


def _gmlp_bwd(dz, w_out, saved, rstd_v, vg, vb, wm, bs_col, w_in, ln_below):
    t, d = dz.shape
    d2 = 2 * d
    tm = min(ROW_TILE, t)
    gb = GMLP_BLOCK
    xhat_below, rstd_below, g_below = ln_below

    def body(dz_ref, wo_ref, sv_ref, rs_ref, vg_ref, vb_ref, wm_ref, bs_ref, wi_ref, xh_ref, rsb_ref, gb_ref,
             da_ref, dws_ref, dbs_ref, dvg_ref, dvb_ref, dzb_ref, dg_ref, db_ref, dvln_sc):
        first = pl.program_id(0) == 0
        u = sv_ref[:, :d].astype(F32)
        vhat = sv_ref[:, 2 * d:3 * d].astype(F32)
        rstd = rs_ref[...]
        vln = (vhat * vg_ref[...] + vb_ref[...]).astype(BF16)
        dgate = lax.dot_general(dz_ref[...].astype(BF16), wo_ref[...], NT, preferred_element_type=F32)

        @pl.when(first)
        def _():
            dws_ref[...] = jnp.zeros(dws_ref.shape, F32)
            dbs_ref[...] = jnp.zeros(dbs_ref.shape, F32)

        for blk in range(tm // gb):
            rs = slice(blk * gb, (blk + 1) * gb)
            for gi in range(GMLP_GROUPS):
                cs = slice(gi * gb, (gi + 1) * gb)
                vblk = vln[rs, cs]
                s = jnp.dot(wm_ref[gi], vblk, preferred_element_type=F32) + bs_ref[:, gi:gi + 1]
                dgb = dgate[rs, cs]
                da_ref[rs, cs] = (dgb * s * sv_ref[rs, d + gi * gb:d + (gi + 1) * gb].astype(F32)).astype(BF16)
                ds = dgb * u[rs, cs]
                dsb = ds.astype(BF16)
                dws_ref[gi] += lax.dot_general(dsb, vblk, NT, preferred_element_type=F32)
                dbs_ref[:, gi:gi + 1] += jnp.sum(ds, axis=1, keepdims=True)
                dvln_sc[rs, cs] = lax.dot_general(wm_ref[gi], dsb, TN, preferred_element_type=F32)
        dv, dvg, dvb = _ln_bwd(dvln_sc[...], vhat, rstd, vg_ref[...])
        da_ref[:, d:] = (dv * sv_ref[:, 3 * d:].astype(F32)).astype(BF16)
        _accumulate(dvg_ref, first, dvg)
        _accumulate(dvb_ref, first, dvb)
        dx = ALPHA * dz_ref[...]
        nc = wi_ref.shape[2]
        for j in range(wi_ref.shape[0]):
            dx = dx + lax.dot_general(da_ref[:, j * nc:(j + 1) * nc], wi_ref[j], NT, preferred_element_type=F32)
        dzb, dg, db = _ln_bwd(dx, xh_ref[...], rsb_ref[...], gb_ref[...])
        dzb_ref[...] = dzb
        _accumulate(dg_ref, first, dg)
        _accumulate(db_ref, first, db)

    row = lambda i: (i, 0)
    full, col = pl.BlockSpec((tm, d), row), pl.BlockSpec((tm, 1), row)
    vec = pl.BlockSpec((1, d), lambda i: (0, 0))
    return pl.pallas_call(
        body, grid=(t // tm,),
        in_specs=[full, _resident(w_out.shape), pl.BlockSpec((tm, 4 * d), row), col,
                  _resident(vg.shape), _resident(vb.shape), _resident(wm.shape), _resident(bs_col.shape),
                  _resident(w_in.shape), full, col, _resident(g_below.shape)],
        out_specs=[pl.BlockSpec((tm, d2), row), pl.BlockSpec(wm.shape, lambda i: (0, 0, 0)),
                   pl.BlockSpec(bs_col.shape, lambda i: (0, 0)), vec, vec, full, vec, vec],
        out_shape=[_sds((t, d2), BF16), _sds(wm.shape, F32), _sds(bs_col.shape, F32), _sds((1, d), F32), _sds((1, d), F32),
                   _sds((t, d), F32), _sds((1, d), F32), _sds((1, d), F32)],
        scratch_shapes=[pltpu.VMEM((tm, d), F32)],
        compiler_params=_cp(), name="gmlp_bwd")(dz, w_out, saved, rstd_v, vg, vb, wm, bs_col, w_in, xhat_below,
                                                rstd_below, g_below)


def _conv_bwd(bch, dconv, conv_w):
    t = bch.shape[0]
    tm = min(ROW_TILE, t)
    nb = t // tm
    halo_blocks = tm // SUBLANES
    cw = CONV_WIDTH

    def body(cur_ref, prev_ref, next_ref, dc_ref, dn_ref, w_ref, o_ref, dw_ref):
        i = pl.program_id(0)
        bgate, cgate, hval = cur_ref[:, :cw], cur_ref[:, cw:2 * cw], cur_ref[:, 2 * cw:]
        z = cgate * hval
        zp = jnp.where(i == 0, 0.0, prev_ref[:, cw:2 * cw] * prev_ref[:, 2 * cw:])
        z1, z2 = _shift_down(z, zp)
        w0, w1, w2 = w_ref[0:1, :], w_ref[1:2, :], w_ref[2:3, :]
        dconv = dc_ref[...]
        o_ref[:, :cw] = (dconv * (w0 * z2 + w1 * z1 + w2 * z)).astype(BF16)
        dy = dconv * bgate
        dyn = jnp.where(i == nb - 1, 0.0, dn_ref[...] * next_ref[:, :cw])
        dy1, dy2 = _shift_up(dy, dyn)
        dz = w2 * dy + w1 * dy1 + w0 * dy2
        o_ref[:, cw:2 * cw] = (dz * hval).astype(BF16)
        o_ref[:, 2 * cw:] = (dz * cgate).astype(BF16)

        @pl.when(i == 0)
        def _():
            dw_ref[...] = jnp.zeros(dw_ref.shape, F32)

        for tap, zs in enumerate((z2, z1, z)):
            dw_ref[tap:tap + 1, :] += jnp.sum(dy * zs, axis=0, keepdims=True)

    last_halo = t // SUBLANES - 1
    return pl.pallas_call(
        body, grid=(nb,),
        in_specs=[pl.BlockSpec((tm, BCH), lambda i: (i, 0)),
                  pl.BlockSpec((SUBLANES, BCH), lambda i: (jnp.maximum(i * halo_blocks - 1, 0), 0)),
                  pl.BlockSpec((SUBLANES, BCH), lambda i: (jnp.minimum((i + 1) * halo_blocks, last_halo), 0)),
                  pl.BlockSpec((tm, cw), lambda i: (i, 0)),
                  pl.BlockSpec((SUBLANES, cw), lambda i: (jnp.minimum((i + 1) * halo_blocks, last_halo), 0)),
                  _resident(conv_w.shape)],
        out_specs=[pl.BlockSpec((tm, BCH), lambda i: (i, 0)), pl.BlockSpec((SUBLANES, cw), lambda i: (0, 0))],
        out_shape=[_sds((t, BCH), BF16), _sds((SUBLANES, cw), F32)],
        compiler_params=_cp(), name="conv_bwd")(bch, bch, bch, dconv, dconv, conv_w)


def _attn_bwd_prep(dz, w_out, o, qp, lse_pad, after):
    t = o.shape[0]
    tm = min(ROW_TILE, t)
    hd = HEAD_DIM
    sel_lse = _piece_selector(Q_LSE, -1.0)
    sel_delta = _piece_selector(DO_DELTA, -1.0)
    head_of = jnp.asarray([[1.0 if col == row // hd else 0.0 for col in range(LANES)] for row in range(FOX_WIDTH)], F32)

    def body(after_ref, dz_ref, wo_ref, o_ref, qp_ref, lse_ref, sl_ref, sd_ref, seg_ref, qb_ref, dob_ref, dconv_ref):
        dzb = dz_ref[...].astype(BF16)
        do = lax.dot_general(dzb, wo_ref[:FOX_WIDTH, :], NT, preferred_element_type=F32)
        dconv_ref[...] = lax.dot_general(dzb, wo_ref[FOX_WIDTH:, :], NT, preferred_element_type=F32)
        delta = jnp.dot(o_ref[...].astype(F32) * do, seg_ref[...], precision=HIGHEST, preferred_element_type=F32)
        lse_extra = jnp.dot(_piece_rows(lse_ref[...]), sl_ref[...], preferred_element_type=F32)
        do_extra = jnp.dot(_piece_rows(delta), sd_ref[...], preferred_element_type=F32).astype(BF16)
        for h in range(FOX_HEADS):
            hs = slice(h * hd, (h + 1) * hd)
            dob_ref[h, :, :hd] = do[:, hs].astype(BF16)
            dob_ref[h, :, hd:] = do_extra[:, hs]
            qb_ref[h, :, :hd] = qp_ref[h, :, :hd]
            qb_ref[h, :, hd:] = (qp_ref[h, :, hd:].astype(F32) + lse_extra[:, hs]).astype(BF16)

    row = lambda i: (i, 0)
    row3 = pl.BlockSpec((FOX_HEADS, tm, LANES), lambda i: (0, i, 0))
    half = pl.BlockSpec((tm, FOX_WIDTH), row)
    return pl.pallas_call(
        body, grid=(t // tm,),
        in_specs=[_ANY_SPEC, pl.BlockSpec((tm, dz.shape[1]), row), _resident(w_out.shape), half, row3,
                  pl.BlockSpec((tm, LANES), row), _resident(sel_lse.shape), _resident(sel_delta.shape),
                  _resident(head_of.shape)],
        out_specs=[row3, row3, half],
        out_shape=[_sds((FOX_HEADS, t, LANES), BF16)] * 2 + [_sds((t, FOX_WIDTH), F32)],
        compiler_params=_cp(), name="attn_bwd_prep")(after, dz, w_out, o, qp, lse_pad, sel_lse, sel_delta, head_of)


def _attn_bwd(qb, kp, vp, dob, kt):
    t = qb.shape[1]
    bq = min(ATT_BLOCK, t)
    nq = t // bq
    i_tab, j_tab = _triangle(nq, key_major=True)

    def body(it_ref, jt_ref, q_ref, k_ref, v_ref, do_ref, kt_ref, dqt_ref, dk_ref, dv_ref, dk_sc, dv_sc):
        s = pl.program_id(1)
        i, j = it_ref[s], jt_ref[s]

        @pl.when(s == 0)
        def _():
            dqt_ref[...] = jnp.zeros(dqt_ref.shape, F32)

        @pl.when(i == j)
        def _():
            dk_sc[...] = jnp.zeros(dk_sc.shape, F32)
            dv_sc[...] = jnp.zeros(dv_sc.shape, F32)

        cols = pl.ds(pl.multiple_of(i * bq, bq), bq)

        def sweep(masked):
            def scores(h):
                return (lax.dot_general(k_ref[h], q_ref[h], NT, preferred_element_type=F32),
                        lax.dot_general(v_ref[h], do_ref[h], NT, preferred_element_type=F32))

            def accumulate(h, ptb, dstb):
                dv_sc[h] += jnp.dot(ptb, do_ref[h], preferred_element_type=F32)
                dk_sc[h] += jnp.dot(dstb, q_ref[h], preferred_element_type=F32)
                dqt_ref[h, :, cols] += jnp.dot(kt_ref[h], dstb, preferred_element_type=F32)

            ahead, behind = scores(0), None
            for h in range(ATT_BWD_HEADS):
                st, dpt = ahead
                if h + 1 < ATT_BWD_HEADS:
                    ahead = scores(h + 1)
                if behind is not None:
                    accumulate(*behind)
                if masked:
                    key = lax.broadcasted_iota(jnp.int32, (bq, bq), 0)
                    qry = lax.broadcasted_iota(jnp.int32, (bq, bq), 1)
                    st = jnp.where(key <= qry, st, NEG)
                pt = jnp.exp(st)
                behind = (h, pt.astype(BF16), (pt * dpt).astype(BF16))
            accumulate(*behind)

        @pl.when(i == j)
        def _():
            sweep(True)

        @pl.when(i > j)
        def _():
            sweep(False)

        @pl.when(i == nq - 1)
        def _():
            dk_ref[...] = dk_sc[...]
            dv_ref[...] = dv_sc[...].astype(BF16)

    nh = ATT_BWD_HEADS
    qblk = pl.BlockSpec((nh, bq, LANES), lambda hp, s, it, jt: (hp, it[s], 0))
    kblk = pl.BlockSpec((nh, bq, LANES), lambda hp, s, it, jt: (hp, jt[s], 0))
    grid_spec = pltpu.PrefetchScalarGridSpec(
        num_scalar_prefetch=2, grid=(FOX_HEADS // nh, i_tab.shape[0]),
        in_specs=[qblk, kblk, kblk, qblk, pl.BlockSpec((nh, LANES, bq), lambda hp, s, it, jt: (hp, 0, jt[s]))],
        out_specs=[pl.BlockSpec((nh, LANES, t), lambda hp, s, it, jt: (hp, 0, 0), pipeline_mode=pl.Buffered(1)),
                   kblk, kblk],
        scratch_shapes=[pltpu.VMEM((nh, bq, LANES), F32), pltpu.VMEM((nh, bq, LANES), F32)])
    return pl.pallas_call(body, grid_spec=grid_spec,
                          out_shape=[_sds((FOX_HEADS, LANES, t), F32), _sds((FOX_HEADS, t, LANES), F32),
                                     _sds((FOX_HEADS, t, LANES), BF16)],
                          compiler_params=_cp(ATT_BWD_VMEM_LIMIT), name="attn_bwd")(i_tab, j_tab, qb, kp, vp, dob, kt)


def _attn_unpack(dqt, dkp, dvp):
    t = dkp.shape[1]
    tm = min(ROW_TILE, t)
    hd = HEAD_DIM

    def body(dqt_ref, dk_ref, dv_ref, o_ref, dc_ref):
        for h in range(FOX_HEADS):
            dq = dqt_ref[h].T
            o_ref[:, h * hd:(h + 1) * hd] = (dq[:, :hd] * (hd ** -0.5)).astype(BF16)
            o_ref[:, FOX_WIDTH + h * hd:FOX_WIDTH + (h + 1) * hd] = dk_ref[h, :, :hd].astype(BF16)
            o_ref[:, 2 * FOX_WIDTH + h * hd:2 * FOX_WIDTH + (h + 1) * hd] = dv_ref[h, :, :hd]
            dc_ref[:, h:h + 1] = dq[:, K_ONE:K_ONE + 1] - dk_ref[h, :, Q_ONE:Q_ONE + 1]

    row3 = pl.BlockSpec((FOX_HEADS, tm, LANES), lambda i: (0, i, 0))
    return pl.pallas_call(
        body, grid=(t // tm,),
        in_specs=[pl.BlockSpec((FOX_HEADS, LANES, tm), lambda i: (0, 0, i)), row3, row3],
        out_specs=[pl.BlockSpec((tm, QKV), lambda i: (i, 0)), pl.BlockSpec((tm, FOX_HEADS), lambda i: (i, 0))],
        out_shape=[_sds((t, QKV), BF16), _sds((t, FOX_HEADS), F32)],
        compiler_params=_cp(), name="attn_unpack")(dqt, dkp, dvp)


def _adamw(parts, w, m, v, name, layer=None, into=None):
    nl, r, c = w.shape
    fits = [cand for cand in [*range(SUBLANES, r, SUBLANES), r] if r % cand == 0 and cand * c * 4 <= ADAMW_BLOCK_BYTES]
    tr = max(fits) if fits else r
    npart = len(parts)
    bc1 = 1.0 - ADAM_B1 ** ADAM_STEP
    bc2 = 1.0 - ADAM_B2 ** ADAM_STEP

    def body(*refs):
        p_refs = refs[:npart]
        w_ref, m_ref, v_ref = refs[npart:npart + 3]
        g_ref, d_ref, nm_ref, nv_ref = refs[-4:]
        sums = []
        for p_ref in p_refs:
            acc = p_ref[0, 0].astype(F32)
            for s in range(1, p_ref.shape[0]):
                acc = acc + p_ref[s, 0].astype(F32)
            sums.append(acc)
        g = sums[0]
        for extra in sums[1:]:
            g = g + extra
        nm = ADAM_B1 * m_ref[0] + (1.0 - ADAM_B1) * g
        nv = ADAM_B2 * v_ref[0] + (1.0 - ADAM_B2) * (g * g)
        m_hat = nm / bc1
        v_hat = nv / bc2
        g_ref[0] = g
        d_ref[0] = -ADAM_LR * (m_hat / (jnp.sqrt(v_hat) + ADAM_EPS) + ADAM_WD * w_ref[0])
        nm_ref[0] = nm
        nv_ref[0] = nv

    first = 0 if layer is None else layer
    blk = pl.BlockSpec((1, tr, c), lambda l, i: (first + l, i, 0))
    extra = [] if into is None else list(into)
    return pl.pallas_call(
        body, grid=(nl if layer is None else 1, r // tr),
        in_specs=[pl.BlockSpec((p.shape[0], 1, tr, c), lambda l, i: (0, l, i, 0)) for p in parts] + [blk, blk, blk]
        + [_ANY_SPEC] * len(extra),
        out_specs=[blk] * 4, out_shape=[_sds(w.shape, F32)] * 4,
        input_output_aliases={npart + 3 + k: k for k in range(len(extra))},
        compiler_params=_cp(), name=name)(*parts, w, m, v, *extra)


def _to_rows(a):
    flat = a.reshape(-1)
    pad = (-flat.shape[0]) % LANES
    if pad:
        flat = jnp.concatenate([flat, jnp.zeros((pad,), flat.dtype)])
    return flat.reshape(-1, LANES)


def _ffn_fwd(xin_ln, xin_b, wi, wo, g, b, layer, target=None):
    t, d = xin_b.shape
    tm = min(FFN_FUSED_ROW_TILE, t)
    hh = HALF_HIDDEN
    rxh, rg, rb = xin_ln

    def body(x_ref, wi_ref, wo_ref, rxh_ref, rg_ref, rb_ref, g_ref, b_ref, *rest):
        gu_ref, h_ref = rest[target is not None:][:2]
        a = x_ref[...]
        for c in range(2):
            gs, us = slice(c * hh, (c + 1) * hh), slice(FFN_HIDDEN + c * hh, FFN_HIDDEN + (c + 1) * hh)
            gate = jnp.dot(a, wi_ref[c], preferred_element_type=F32)
            up = jnp.dot(a, wi_ref[2 + c], preferred_element_type=F32)
            sig = _sigmoid(gate)
            silu = gate * sig
            gu_ref[:, gs] = (up * sig * (1.0 + gate * (1.0 - sig))).astype(BF16)
            gu_ref[:, us] = silu.astype(BF16)
            h_ref[:, gs] = (silu * up).astype(BF16)
        z = ALPHA * (rxh_ref[...] * rg_ref[...] + rb_ref[...]) + jnp.dot(h_ref[...], wo_ref[...], preferred_element_type=F32)
        xhat, rstd = _ln_fwd(z)
        if target is None:
            yb_ref, xh_ref, rs_ref = rest[2:]
            yb_ref[...] = (xhat * g_ref[...] + b_ref[...]).astype(BF16)
            xh_ref[...] = xhat
            rs_ref[...] = rstd
            return
        sq_ref, dz_ref, dg_ref, db_ref = rest[3:]
        first = pl.program_id(0) == 0
        err = xhat * g_ref[...] + b_ref[...] - rest[0][...]
        dz, dg, db = _ln_bwd(err * (1.0 / d), xhat, rstd, g_ref[...])
        dz_ref[...] = dz
        _accumulate(sq_ref, first, jnp.sum(err * err, axis=0, keepdims=True))
        _accumulate(dg_ref, first, dg)
        _accumulate(db_ref, first, db)

    row = lambda i: (i, 0)
    full = pl.BlockSpec((tm, d), row)
    vec = _resident(g.shape)
    acc = pl.BlockSpec((1, d), lambda i: (0, 0))
    in_specs = [full, _resident(wi.shape), _resident(wo.shape), full, vec, vec, vec, vec]
    out_specs = [pl.BlockSpec((tm, 2 * FFN_HIDDEN), row), pl.BlockSpec((tm, FFN_HIDDEN), row)]
    out_shape = [_sds((t, 2 * FFN_HIDDEN), BF16), _sds((t, FFN_HIDDEN), BF16)]
    args = [xin_b, wi, wo, rxh, rg, rb, g, b]
    if target is None:
        out_specs += [full, full, pl.BlockSpec((tm, 1), row)]
        out_shape += [_sds((t, d), BF16), _sds((t, d), F32), _sds((t, 1), F32)]
    else:
        in_specs.append(full)
        args.append(target)
        out_specs += [acc, full, acc, acc]
        out_shape += [_sds((1, d), F32), _sds((t, d), F32), _sds((1, d), F32), _sds((1, d), F32)]
    gu, h, *tail = pl.pallas_call(body, grid=(t // tm,), in_specs=in_specs, out_specs=out_specs, out_shape=out_shape,
                                  compiler_params=_cp(), name=f"ffn_fwd_rows_{layer}")(*args)
    if target is None:
        y_b, xhat, rstd = tail
        return y_b, (xin_b, gu, h, xhat, rstd)
    return tail, (xin_b, gu, h)


def _ffn_bwd(dz, saved, wi, wo, ln_below, layer):
    xin_b, gu, h = saved[:3]
    dgu, *below = _ffn_bwd_rows(dz, wo, gu, wi, ln_below, f"ffn_bwd_rows_{layer}")
    g_out = _mm_tn(h, dz, f"ffn_dw_out_{layer}", tn=D_MODEL, tk=HALF_HIDDEN)
    g_in = _mm_tn(xin_b, dgu, f"ffn_dw_in_{layer}", tn=HALF_HIDDEN, stack_cols=True)
    return below, g_in, g_out.reshape(N_CHIPS, FFN_HIDDEN // N_CHIPS, D_MODEL)


def kernel(x, even_w_in, even_b_f, even_conv_w, even_w_out, odd_w_in, odd_v_ln_g, odd_v_ln_b, odd_w_s, odd_b_s, odd_w_out, mix_ln_g, mix_ln_b, ffn_w_in, ffn_w_out, ffn_ln_g, ffn_ln_b, loss_target, m_even_w_in, m_even_b_f, m_even_conv_w, m_even_w_out, m_odd_w_in, m_odd_v_ln_g, m_odd_v_ln_b, m_odd_w_s, m_odd_b_s, m_odd_w_out, m_mix_ln_g, m_mix_ln_b, m_ffn_w_in, m_ffn_w_out, m_ffn_ln_g, m_ffn_ln_b, v_even_w_in, v_even_b_f, v_even_conv_w, v_even_w_out, v_odd_w_in, v_odd_v_ln_g, v_odd_v_ln_b, v_odd_w_s, v_odd_b_s, v_odd_w_out, v_mix_ln_g, v_mix_ln_b, v_ffn_w_in, v_ffn_w_out, v_ffn_ln_g, v_ffn_ln_b):
    t = x.shape[1]
    d = D_MODEL
    chip = 2 * lax.axis_index("x") + lax.axis_index("y")
    x2d = x[0]
    target = loss_target[0]

    small_shard = jnp.concatenate([odd_v_ln_g.reshape(2, LANES), odd_v_ln_b.reshape(2, LANES),
                                   even_conv_w.reshape(CONV_K, LANES), jnp.zeros((1, LANES), F32)], axis=0)
    first = [jnp.swapaxes(even_w_in[0], 0, 1).astype(BF16)]
    second = [even_w_out[0].astype(BF16), small_shard]
    later = [odd_w_in[0].astype(BF16), odd_w_out[0].astype(BF16), ffn_w_in[0].astype(BF16), ffn_w_in[1].astype(BF16),
             ffn_w_out[0].astype(BF16), ffn_w_out[1].astype(BF16)]
    first_h, first_tok = _split_start(first, "gather4", "gather_first_start")
    second_h, second_tok = _split_start(second, "gather4", "gather_second_start", after=first_tok)
    later_h, later_tok = _split_start(later, "gather4", "gather_later_start", after=second_tok)
    (g_ewi,) = _gathered(first_h, "gather_first_wait", later_tok)
    ewi = g_ewi.reshape(EVEN_IN, d)
    w_even_in = jnp.concatenate([ewi[:QKV], ewi[QKV + FOX_HEADS:],
                                 jnp.pad(ewi[QKV:QKV + FOX_HEADS], ((0, LANES - FOX_HEADS), (0, 0)))], axis=0)
    chunk_id = jnp.arange(GMLP_BLOCK) // CHUNK
    gmask = chunk_id[None, :] <= chunk_id[:, None]
    w_spatial = jnp.where(gmask[None], odd_w_s[0], 0.0).astype(BF16)
    bs_col = odd_b_s[0].T
    b_f_col = even_b_f.reshape(FOX_HEADS, 1)
    ln = lambda p, l: p[l:l + 1]

    qkv, bch, fl, x2d_b = _proj(x2d, w_even_in, [(0, QKV, BF16), (QKV, QKV + BCH, F32), (QKV + BCH, EVEN_IN_PAD, F32)], "even_proj")
    fl3 = fl[:, :FOX_HEADS].T.reshape(FOX_HEADS, t // LANES, LANES).transpose(1, 0, 2)
    c3 = _fgate_fwd(fl3, b_f_col)
    c_rows = c3.transpose(1, 0, 2).reshape(FOX_HEADS, t)
    head_lanes = lambda rows: jnp.pad(rows.T, ((0, 0), (0, LANES - FOX_HEADS)))
    qp, kp, vp, kt, vt = _attn_pack(qkv, head_lanes(c_rows))
    attn, lse = _attn_fwd(qp, kp, vt)
    g_ewo, g_small = _gathered(second_h, "gather_second_wait", attn)
    w_even_out = g_ewo.reshape(d, d)
    v_ln_g = g_small[:, 0:2].reshape(1, d)
    v_ln_b = g_small[:, 2:4].reshape(1, d)
    conv_w = g_small[:, 4:7].transpose(1, 0, 2).reshape(CONV_K, CONV_WIDTH)
    conv, x1_b, xh1, rs1 = _even_out(attn, bch, conv_w, w_even_out, x2d, ln(mix_ln_g, 0), ln(mix_ln_b, 0))
    w_odd_in, g_owo, w_fi0, w_fi1, g_fo0, g_fo1 = _gathered(later_h, "gather_later_wait", x1_b)
    w_odd_out = g_owo.reshape(d, d)
    w_ffn_in = [w_fi0, w_fi1]
    w_ffn_out = [g_fo0.reshape(FFN_HIDDEN, d), g_fo1.reshape(FFN_HIDDEN, d)]
    x2_b, ffn0 = _ffn_fwd((xh1, ln(mix_ln_g, 0), ln(mix_ln_b, 0)), x1_b, w_ffn_in[0], w_ffn_out[0],
                          ln(ffn_ln_g, 0), ln(ffn_ln_b, 0), 0)

    sv_odd, rs_odd, gated, x3_b, xh3, rs3 = _gmlp_fwd(
        x2_b, w_odd_in, v_ln_g, v_ln_b, w_spatial, bs_col, w_odd_out, (ffn0[3], ln(ffn_ln_g, 0), ln(ffn_ln_b, 0)),
        ln(mix_ln_g, 1), ln(mix_ln_b, 1))
    (sq, dz4, d_fg1, d_fb1), ffn1 = _ffn_fwd((xh3, ln(mix_ln_g, 1), ln(mix_ln_b, 1)), x3_b, w_ffn_in[1], w_ffn_out[1],
                                             ln(ffn_ln_g, 1), ln(ffn_ln_b, 1), 1, target=target)

    loss = lax.psum(0.5 / d * jnp.sum(sq), ("x", "y", "c"))
    (dz3, d_mg1, d_mb1), gi_f1, go_f1 = _ffn_bwd(dz4, ffn1, w_ffn_in[1], w_ffn_out[1], (xh3, rs3, ln(mix_ln_g, 1)), 1)

    go_odd = _mm_tn(gated, dz3, "odd_dw_out", tn=d).reshape(N_CHIPS, 1, d // N_CHIPS, d)
    da_odd, dws, dbs_col, d_vg, d_vb, dz2, d_fg0, d_fb0 = _gmlp_bwd(
        dz3, w_odd_out, sv_odd, rs_odd, v_ln_g, v_ln_b, w_spatial, bs_col, w_odd_in,
        (ffn0[3], ffn0[4], ln(ffn_ln_g, 0)))
    gi_odd = _mm_tn(x2_b, da_odd, "odd_dw_in", tn=d // 2, stack_cols=True)[:, None]
    (dz1, d_mg0, d_mb0), gi_f0, go_f0 = _ffn_bwd(dz2, ffn0, w_ffn_in[0], w_ffn_out[0], (xh1, rs1, ln(mix_ln_g, 0)), 0)

    sent_early = [gi_odd, go_odd, gi_f0[:, None], gi_f1[:, None], go_f0[:, None], go_f1[:, None]]
    early_h, early_tok = _split_start(sent_early, "scatter4", "scatter_early_start")
    qb, dob, dconv = _attn_bwd_prep(dz1, w_even_out, attn, qp, head_lanes(lse.reshape(FOX_HEADS, t)), early_tok)
    go_even = _mm_tn_pair(attn, conv, dz1, "even_dw_out").reshape(N_CHIPS, 1, d // N_CHIPS, d)
    dbch, dconv_w8 = _conv_bwd(bch, dconv, conv_w)
    dqkv, dc_col = _attn_unpack(*_attn_bwd(qb, kp, vp, dob, kt))
    dc3 = dc_col.T.reshape(FOX_HEADS, t // LANES, LANES).transpose(1, 0, 2)
    dfl3, d_bf = _fgate_bwd(dc3, fl3, b_f_col)
    dfl = jnp.concatenate([dfl3.transpose(1, 0, 2).reshape(FOX_HEADS, t).T.astype(BF16),
                           jnp.zeros((t, LANES - FOX_HEADS), BF16)], axis=1)

    dws_masked = jnp.where(gmask[None], dws, 0.0)
    rep_names = ["odd_w_s", "odd_b_s", "mix_ln_g", "mix_ln_b", "ffn_ln_g", "ffn_ln_b", "even_b_f"]
    rep_grads = [dws_masked, dbs_col.T, jnp.concatenate([d_mg0, d_mg1]), jnp.concatenate([d_mb0, d_mb1]),
                 jnp.concatenate([d_fg0, d_fg1]), jnp.concatenate([d_fb0, d_fb1]), d_bf.reshape(1, FOX_HEADS)]
    rep_w = [(odd_w_s, m_odd_w_s, v_odd_w_s), (odd_b_s, m_odd_b_s, v_odd_b_s), (mix_ln_g, m_mix_ln_g, v_mix_ln_g),
             (mix_ln_b, m_mix_ln_b, v_mix_ln_b), (ffn_ln_g, m_ffn_ln_g, v_ffn_ln_g), (ffn_ln_b, m_ffn_ln_b, v_ffn_ln_b),
             (even_b_f, m_even_b_f, v_even_b_f)]
    rep_rows = [_to_rows(gr) for gr in rep_grads]
    n_rep = sum(r.shape[0] for r in rep_rows)
    pad_rep = (-n_rep) % SUBLANES
    dconv_w = dconv_w8[:CONV_K].reshape(CONV_K, N_CHIPS, LANES).transpose(1, 0, 2).reshape(N_CHIPS * CONV_K, LANES)
    packed = jnp.concatenate(rep_rows + [jnp.zeros((pad_rep, LANES), F32), d_vg.reshape(SUBLANES, LANES),
                                         d_vb.reshape(SUBLANES, LANES), dconv_w, jnp.zeros((4, LANES), F32)], axis=0)
    small_h, small_tok = _split_start([packed], "gather8", "gather_small_start")

    swap_h, swap_tok = _split_start(_scattered(early_h, "scatter_early_wait", small_tok), "swap2", "swap_early_start")
    dw_qkv = _mm_tn(dqkv, x2d_b, "even_dw_qkv", tn=d, tk=QKV // 2, after=swap_tok)
    dw_bch = _mm_tn(dbch, x2d_b, "even_dw_bch", tn=d, tk=BCH // 2)
    dw_f = _mm_tn(dfl, x2d_b, "even_dw_f", tn=d)
    gi_even = jnp.concatenate([dw_qkv, dw_f[:FOX_HEADS], dw_bch], axis=0).reshape(N_CHIPS, 1, -1, LANES)
    sent_late = [gi_even, go_even]
    late_h, late_tok = _split_start(sent_late, "scatter4", "scatter_late_start")
    grad_x = _mm_back([(dqkv, 0, QKV), (dbch, QKV, QKV + BCH), (dfl, QKV + BCH, EVEN_IN_PAD)], w_even_in, dz1,
                      late_tok, "even_dx")
    mine, theirs = _split_wait(swap_h, "swap_early_wait", grad_x)
    res = {}
    res["odd_w_in"] = _adamw([mine[0], theirs[0]], odd_w_in, m_odd_w_in, v_odd_w_in, "adamw_odd_w_in")
    res["odd_w_out"] = _adamw([mine[1], theirs[1]], odd_w_out, m_odd_w_out, v_odd_w_out, "adamw_odd_w_out")
    for nm, at, (w, m, v) in (("ffn_w_in", 2, (ffn_w_in, m_ffn_w_in, v_ffn_w_in)),
                              ("ffn_w_out", 4, (ffn_w_out, m_ffn_w_out, v_ffn_w_out))):
        upper = _adamw([mine[at + 1], theirs[at + 1]], w, m, v, f"adamw_{nm}_1", layer=1)
        res[nm] = _adamw([mine[at], theirs[at]], w, m, v, f"adamw_{nm}_0", layer=0, into=upper)
    mine_late = _scattered(late_h, "scatter_late_wait", res["ffn_w_out"][0])
    theirs_late = _exchange(mine_late, "swap2", "swap_late")
    rows = lambda a: jnp.swapaxes(a, 1, 2).reshape(1, -1, LANES)
    back = lambda a: jnp.swapaxes(a.reshape(1, EVEN_IN // N_CHIPS, d), 1, 2)
    res["even_w_in"] = [back(o) for o in _adamw([mine_late[0], theirs_late[0]], rows(even_w_in), rows(m_even_w_in),
                                                rows(v_even_w_in), "adamw_even_w_in")]
    res["even_w_out"] = _adamw([mine_late[1], theirs_late[1]], even_w_out, m_even_w_out, v_even_w_out,
                               "adamw_even_w_out")
    (packed,), (gathered,) = _split_wait(small_h, "gather_small_wait", theirs_late[0])
    gathered = lax.dynamic_update_index_in_dim(gathered, packed, 4 * lax.axis_index("x") + 2 * lax.axis_index("y")
                                               + lax.axis_index("c"), 0)

    base = n_rep + pad_rep
    own_rows = jnp.concatenate([
        lax.dynamic_slice_in_dim(gathered, base + 2 * chip, 2, axis=1),
        lax.dynamic_slice_in_dim(gathered, base + SUBLANES + 2 * chip, 2, axis=1),
        lax.dynamic_slice_in_dim(gathered, base + 2 * SUBLANES + CONV_K * chip, CONV_K, axis=1),
        jnp.zeros((N_DEV, 1, LANES), F32)], axis=1)
    small_parts = jnp.concatenate([gathered[:, :base], own_rows], axis=1)[:, None]

    def pack_small(get):
        rows = [_to_rows(get(tw)) for tw in rep_w] + [jnp.zeros((pad_rep, LANES), F32)]
        rows += [get(sh).reshape(-1, LANES) for sh in ((odd_v_ln_g, m_odd_v_ln_g, v_odd_v_ln_g),
                                                       (odd_v_ln_b, m_odd_v_ln_b, v_odd_v_ln_b),
                                                       (even_conv_w, m_even_conv_w, v_even_conv_w))]
        return jnp.concatenate(rows + [jnp.zeros((1, LANES), F32)], axis=0)[None]

    small_out = _adamw([small_parts], pack_small(lambda tw: tw[0]), pack_small(lambda tw: tw[1]),
                       pack_small(lambda tw: tw[2]), "adamw_small")

    def unpack_small(rows3):
        rows = rows3[0]
        out, off = {}, 0
        for nm, (w, _, _), r in zip(rep_names, rep_w, rep_rows):
            out[nm] = rows[off:off + r.shape[0]].reshape(-1)[:w.size].reshape(w.shape)
            off += r.shape[0]
        off += pad_rep
        out["odd_v_ln_g"] = rows[off:off + 2].reshape(odd_v_ln_g.shape)
        out["odd_v_ln_b"] = rows[off + 2:off + 4].reshape(odd_v_ln_b.shape)
        out["even_conv_w"] = rows[off + 4:off + 4 + CONV_K].reshape(even_conv_w.shape)
        return out

    small = [unpack_small(o) for o in small_out]
    order = ["even_w_in", "even_b_f", "even_conv_w", "even_w_out", "odd_w_in", "odd_v_ln_g", "odd_v_ln_b", "odd_w_s",
             "odd_b_s", "odd_w_out", "mix_ln_g", "mix_ln_b", "ffn_w_in", "ffn_w_out", "ffn_ln_g", "ffn_ln_b"]
    outs = [loss, grad_x[None]]
    for kind in range(4):
        for nm in order:
            outs.append(res[nm][kind] if nm in res else small[kind][nm])
    return tuple(outs)
```

```python
import math

import jax
import jax.numpy as jnp
from jax import lax
from jax.experimental import pallas as pl
from jax.experimental.pallas import tpu as pltpu

F32 = jnp.float32
BF16 = jnp.bfloat16

D_MODEL = 1024
FOX_HEADS = 8
HEAD_DIM = 64
FOX_WIDTH = FOX_HEADS * HEAD_DIM
CONV_WIDTH = 512
CONV_K = 3
QKV = 3 * FOX_WIDTH
BCH = 3 * CONV_WIDTH
EVEN_IN = QKV + FOX_HEADS + BCH
EVEN_IN_PAD = QKV + BCH + 128
GMLP_BLOCK = 128
GMLP_GROUPS = 8
CHUNK = 64
FFN_HIDDEN = 2816
HALF_HIDDEN = FFN_HIDDEN // 2
ALPHA = 4.0 ** 0.25
LN_EPS = 1e-5
ADAM_LR = 0.001
ADAM_B1 = 0.9
ADAM_B2 = 0.999
ADAM_EPS = 1e-08
ADAM_WD = 0.01
ADAM_STEP = 10
N_CHIPS = 4
N_DEV = 8
LANES = 128
SUBLANES = 8
ROW_TILE = 512
LAYOUT_ROW_TILE = 1024
FFN_FUSED_ROW_TILE = 256
REDUCE_TILE = 2048
ATT_BLOCK = 512
ATT_FWD_HEADS = 8
ATT_BWD_HEADS = 8
ADAMW_BLOCK_BYTES = 2 ** 20
VMEM_LIMIT = 56 * 2 ** 20
ATT_BWD_VMEM_LIMIT = 60 * 2 ** 20
NEG = -1e30
MESH = pl.DeviceIdType.MESH
HIGHEST = lax.Precision.HIGHEST
Q_C, Q_ONE, Q_LSE = 64, 67, 70
K_ONE, K_C, K_ONE2 = 64, 67, 70
V_ONE = 64
DO_DELTA = 65
NT = (((1,), (1,)), ((), ()))
TN = (((0,), (0,)), ((), ()))


def _cp(limit=VMEM_LIMIT):
    return pltpu.CompilerParams(vmem_limit_bytes=limit)


def _resident(shape):
    zeros = (0,) * len(shape)
    return pl.BlockSpec(shape, lambda *_: zeros, pipeline_mode=pl.Buffered(1))


def _sds(shape, dtype):
    return jax.ShapeDtypeStruct(tuple(shape), dtype)


_MASKS = {
    "gather4": [(1, 0, 0), (0, 1, 0), (1, 1, 0)],
    "scatter4": [(1, 0, 0), (0, 1, 0), (1, 1, 0)],
    "swap2": [(0, 0, 1)],
    "gather8": [(0, 0, 1), (0, 1, 0), (0, 1, 1), (1, 0, 0), (1, 0, 1), (1, 1, 0), (1, 1, 1)],
}


def _exchange(arrs, mode, name):
    n = len(arrs)
    masks = _MASKS[mode]
    npeer = len(masks)
    lead = {"gather4": N_CHIPS, "gather8": N_DEV}.get(mode)
    out_shapes = [_sds(((lead,) if lead else ()) + a.shape, a.dtype) for a in arrs]

    def body(*refs):
        ins, outs = refs[:n], refs[n:2 * n]
        send_sems, recv_sems, loc_sems = refs[2 * n:]
        x, y, c = lax.axis_index("x"), lax.axis_index("y"), lax.axis_index("c")
        chip, dev = 2 * x + y, 4 * x + 2 * y + c
        sends, recvs, locs = [], [], []
        for k in range(n):
            if mode == "gather4":
                locs.append(pltpu.make_async_copy(ins[k], outs[k].at[chip], loc_sems.at[k]))
            elif mode == "scatter4":
                locs.append(pltpu.make_async_copy(ins[k].at[chip], outs[k].at[chip], loc_sems.at[k]))
            elif mode == "gather8":
                locs.append(pltpu.make_async_copy(ins[k], outs[k].at[dev], loc_sems.at[k]))
        for cp in locs:
            cp.start()
        for k in range(n):
            for j, (dx, dy, dc) in enumerate(masks):
                px = 1 - x if dx else x
                py = 1 - y if dy else y
                pc = 1 - c if dc else c
                pchip, pdev = 2 * px + py, 4 * px + 2 * py + pc
                if mode == "gather4":
                    src, dst, land = ins[k], outs[k].at[chip], outs[k].at[pchip]
                elif mode == "scatter4":
                    src, dst, land = ins[k].at[pchip], outs[k].at[chip], outs[k].at[pchip]
                elif mode == "swap2":
                    src, dst, land = ins[k], outs[k], outs[k]
                else:
                    src, dst, land = ins[k], outs[k].at[dev], outs[k].at[pdev]
                s = k * npeer + j
                kw = dict(send_sem=send_sems.at[s], recv_sem=recv_sems.at[s], device_id=(px, py, pc),
                          device_id_type=MESH)
                cp = pltpu.make_async_remote_copy(src_ref=src, dst_ref=dst, **kw)
                cp.start()
                sends.append(cp)
                recvs.append(pltpu.make_async_remote_copy(src_ref=src, dst_ref=land, **kw))
        for cp in recvs:
            cp.wait_recv()
        for cp in sends:
            cp.wait_send()
        for cp in locs:
            cp.wait()

    any_spec = pl.BlockSpec(memory_space=pl.ANY)
    outs = pl.pallas_call(
        body,
        out_shape=out_shapes,
        in_specs=[any_spec] * n,
        out_specs=[any_spec] * n,
        scratch_shapes=[pltpu.SemaphoreType.DMA((n * npeer,)), pltpu.SemaphoreType.DMA((n * npeer,)),
                        pltpu.SemaphoreType.DMA((max(n, 1),))],
        name=name,
    )(*arrs)
    return list(outs)


_HBM_SPEC = pl.BlockSpec(memory_space=pltpu.HBM)
_SEM_SPEC = pl.BlockSpec(memory_space=pltpu.SEMAPHORE)
_ANY_SPEC = pl.BlockSpec(memory_space=pl.ANY)
_EFFECT = pltpu.SideEffectType.DATAFLOW_SIDE_EFFECTING


def _split_copies(mode, ins, lands, send_sems, recv_sems):
    x, y, c = lax.axis_index("x"), lax.axis_index("y"), lax.axis_index("c")
    chip, dev = 2 * x + y, 4 * x + 2 * y + c
    masks = _MASKS[mode]
    out = []
    for k in range(len(ins)):
        for j, (dx, dy, dc) in enumerate(masks):
            px = 1 - x if dx else x
            py = 1 - y if dy else y
            pc = 1 - c if dc else c
            pchip, pdev = 2 * px + py, 4 * px + 2 * py + pc
            if mode == "gather4":
                src, dst, land = ins[k], lands[k].at[chip], lands[k].at[pchip]
            elif mode == "scatter4":
                src, dst, land = ins[k].at[pchip], lands[k].at[chip], lands[k].at[pchip]
            elif mode == "swap2":
                src, dst, land = ins[k], lands[k], lands[k]
            else:
                src, dst, land = ins[k], lands[k].at[dev], lands[k].at[pdev]
            s = k * len(masks) + j
            kw = dict(send_sem=send_sems.at[s], recv_sem=recv_sems.at[s], device_id=(px, py, pc), device_id_type=MESH)
            out.append((pltpu.make_async_remote_copy(src_ref=src, dst_ref=dst, **kw),
                        pltpu.make_async_remote_copy(src_ref=src, dst_ref=land, **kw)))
    return out


def _split_start(arrs, mode, name, after=None):
    n = len(arrs)
    nsem = n * len(_MASKS[mode])
    lead = {"gather4": (N_CHIPS,), "gather8": (N_DEV,)}.get(mode, ())
    land_shapes = [lead + a.shape for a in arrs]

    def body(*refs):
        ins, lands = refs[:n], refs[n:2 * n]
        outs = refs[2 * n + (after is not None):]
        for start, _ in _split_copies(mode, ins, lands, outs[0], outs[1]):
            start.start()
        outs[-1][...] = jnp.zeros(outs[-1].shape, F32)

    srcs = [pltpu.with_memory_space_constraint(a, pltpu.HBM) for a in arrs]
    empties = [pltpu.with_memory_space_constraint(lax.empty(s, a.dtype), pltpu.HBM) for s, a in zip(land_shapes, arrs)]
    res = pl.pallas_call(
        body, name=name,
        out_shape=(pltpu.SemaphoreType.DMA((nsem,)), pltpu.SemaphoreType.DMA((nsem,)),
                   *[pltpu.HBM(a.shape, a.dtype) for a in arrs],
                   *[pltpu.HBM(s, a.dtype) for s, a in zip(land_shapes, arrs)],
                   _sds((SUBLANES, LANES), F32)),
        in_specs=[_HBM_SPEC] * (2 * n) + ([_ANY_SPEC] if after is not None else []),
        out_specs=(_SEM_SPEC, _SEM_SPEC, *[_HBM_SPEC] * (2 * n), pl.BlockSpec(memory_space=pltpu.VMEM)),
        input_output_aliases={k: 2 + k for k in range(2 * n)},
        compiler_params=pltpu.CompilerParams(has_side_effects=_EFFECT),
    )(*srcs, *empties, *([after] if after is not None else []))
    return dict(mode=mode, n=n, sems=res[:2], bufs=res[2:2 + 2 * n]), res[-1]


def _split_wait(handle, name, after):
    n, mode = handle["n"], handle["mode"]

    def body(*refs):
        ins, lands = refs[:n], refs[n:2 * n]
        send_sems, recv_sems = refs[2 * n], refs[2 * n + 1]
        for _, arrival in _split_copies(mode, ins, lands, send_sems, recv_sems):
            arrival.wait_send()
            arrival.wait_recv()

    bufs = handle["bufs"]
    res = pl.pallas_call(
        body, name=name,
        out_shape=tuple(pltpu.HBM(b.shape, b.dtype) for b in bufs),
        in_specs=[_HBM_SPEC] * (2 * n) + [_SEM_SPEC, _SEM_SPEC, _ANY_SPEC],
        out_specs=tuple([_HBM_SPEC] * (2 * n)),
        input_output_aliases={k: k for k in range(2 * n)},
        compiler_params=pltpu.CompilerParams(has_side_effects=_EFFECT),
    )(*bufs, *handle["sems"], after)
    return list(res[:n]), list(res[n:])


def _with_own(landed, own):
    chip = 2 * lax.axis_index("x") + lax.axis_index("y")
    return lax.dynamic_update_index_in_dim(landed, own, chip, 0)


def _gathered(handle, name, after):
    sent, landed = _split_wait(handle, name, after)
    return [_with_own(g, own) for g, own in zip(landed, sent)]


def _scattered(handle, name, after):
    chip = 2 * lax.axis_index("x") + lax.axis_index("y")
    sent, landed = _split_wait(handle, name, after)
    return [_with_own(r, lax.dynamic_index_in_dim(g, chip, 0, keepdims=False)) for r, g in zip(landed, sent)]


def _sigmoid(x):
    return 0.5 * jnp.tanh(0.5 * x) + 0.5


def _log_sigmoid(x):
    e = jnp.exp(-jnp.abs(x))
    log1p = jnp.where(e < 1e-2, e * (1.0 - e * (0.5 - e * (1.0 / 3.0))), jnp.log(1.0 + e))
    return jnp.minimum(x, 0.0) - log1p


def _ln_fwd(z):
    mu = jnp.mean(z, axis=-1, keepdims=True)
    zc = z - mu
    var = jnp.mean(zc * zc, axis=-1, keepdims=True)
    rstd = lax.rsqrt(var + LN_EPS)
    return zc * rstd, rstd


def _ln_bwd(dy, xhat, rstd, g):
    dxh = dy * g
    m1 = jnp.mean(dxh, axis=-1, keepdims=True)
    m2 = jnp.mean(dxh * xhat, axis=-1, keepdims=True)
    dz = rstd * (dxh - m1 - xhat * m2)
    return dz, jnp.sum(dy * xhat, axis=0, keepdims=True), jnp.sum(dy, axis=0, keepdims=True)


def _shift_down(z, halo):
    r = lax.broadcasted_iota(jnp.int32, z.shape, 0)
    z1 = jnp.where(r == 0, halo[7:8, :], pltpu.roll(z, 1, 0))
    z2 = jnp.where(r == 0, halo[6:7, :], jnp.where(r == 1, halo[7:8, :], pltpu.roll(z, 2, 0)))
    return z1, z2


def _shift_up(z, halo):
    n = z.shape[0]
    r = lax.broadcasted_iota(jnp.int32, z.shape, 0)
    z1 = jnp.where(r == n - 1, halo[0:1, :], pltpu.roll(z, n - 1, 0))
    z2 = jnp.where(r == n - 1, halo[1:2, :], jnp.where(r == n - 2, halo[0:1, :], pltpu.roll(z, n - 2, 0)))
    return z1, z2


def _triangle_ones(prefix):
    r = lax.broadcasted_iota(jnp.int32, (LANES, LANES), 0)
    c = lax.broadcasted_iota(jnp.int32, (LANES, LANES), 1)
    return ((r <= c) if prefix else (r >= c)).astype(F32)


def _same_head_chunks(rows, earlier):
    r = lax.broadcasted_iota(jnp.int32, (rows, rows), 0)
    c = lax.broadcasted_iota(jnp.int32, (rows, rows), 1)
    same = r % FOX_HEADS == c % FOX_HEADS
    return jnp.logical_and(same, (c < r) if earlier else (c > r)).astype(F32)


def _accumulate(ref, first, value):
    @pl.when(first)
    def _():
        ref[...] = value

    @pl.when(jnp.logical_not(first))
    def _():
        ref[...] += value


def _proj(x, wt, splits, name):
    t, k = x.shape
    tm = min(ROW_TILE, t)

    def body(x_ref, w_ref, *outs):
        a = x_ref[...].astype(BF16)
        for (lo, hi, dt), o in zip(splits, outs):
            o[...] = lax.dot_general(a, w_ref[lo:hi, :], NT, preferred_element_type=F32).astype(dt)
        outs[-1][...] = a

    row = lambda i: (i, 0)
    return pl.pallas_call(
        body, grid=(t // tm,),
        in_specs=[pl.BlockSpec((tm, k), row), _resident(wt.shape)],
        out_specs=[pl.BlockSpec((tm, hi - lo), row) for lo, hi, _ in splits] + [pl.BlockSpec((tm, k), row)],
        out_shape=[_sds((t, hi - lo), dt) for lo, hi, dt in splits] + [_sds((t, k), BF16)],
        compiler_params=_cp(), name=name)(x, wt)


def _fgate_fwd(fl3, b_f):
    nc = fl3.shape[0]
    rows = nc * FOX_HEADS

    def body(f_ref, b_ref, c_ref):
        within = jnp.dot(_log_sigmoid(f_ref[...] + b_ref[...]), _triangle_ones(True), precision=HIGHEST,
                         preferred_element_type=F32)
        totals = jnp.broadcast_to(within[:, LANES - 1:LANES], within.shape)
        c_ref[...] = within + jnp.dot(_same_head_chunks(rows, earlier=True), totals, precision=HIGHEST,
                                      preferred_element_type=F32)

    c2 = pl.pallas_call(body, out_shape=_sds((rows, LANES), F32), name="fgate_fwd")(
        fl3.reshape(rows, LANES), jnp.tile(b_f, (nc, 1)))
    return c2.reshape(fl3.shape)


def _fgate_bwd(dc3, fl3, b_f):
    nc = fl3.shape[0]
    rows = nc * FOX_HEADS

    def body(dc_ref, f_ref, b_ref, df_ref, db_ref):
        within = jnp.dot(dc_ref[...], _triangle_ones(False), precision=HIGHEST, preferred_element_type=F32)
        totals = jnp.broadcast_to(within[:, 0:1], within.shape)
        dlf = within + jnp.dot(_same_head_chunks(rows, earlier=False), totals, precision=HIGHEST,
                               preferred_element_type=F32)
        df = dlf * (1.0 - _sigmoid(f_ref[...] + b_ref[...]))
        df_ref[...] = df
        head = lax.broadcasted_iota(jnp.int32, (FOX_HEADS, rows), 0)
        row = lax.broadcasted_iota(jnp.int32, (FOX_HEADS, rows), 1)
        of_head = (row % FOX_HEADS == head).astype(F32)
        per_row = jnp.broadcast_to(jnp.sum(df, axis=1, keepdims=True), df.shape)
        db_ref[...] = jnp.dot(of_head, per_row, precision=HIGHEST, preferred_element_type=F32)[:, 0:1]

    df2, db = pl.pallas_call(body, out_shape=[_sds((rows, LANES), F32), _sds((FOX_HEADS, 1), F32)], name="fgate_bwd")(
        dc3.reshape(rows, LANES), fl3.reshape(rows, LANES), jnp.tile(b_f, (nc, 1)))
    return df2.reshape(fl3.shape), db


def _split3(c):
    hi = c.astype(BF16).astype(F32)
    mid = (c - hi).astype(BF16).astype(F32)
    lo = (c - hi - mid).astype(BF16).astype(F32)
    return hi, mid, lo


PIECE_ONE = 3 * FOX_HEADS


def _piece_rows(values):
    hi, mid, lo = _split3(values)
    lane = lax.broadcasted_iota(jnp.int32, values.shape, 1)
    row = hi + pltpu.roll(mid, FOX_HEADS, 1) + pltpu.roll(lo, 2 * FOX_HEADS, 1) + jnp.where(lane == PIECE_ONE, 1.0, 0.0)
    return row.astype(BF16)


def _piece_selector(start, sign, ones=()):
    sel = [[0.0] * FOX_WIDTH for _ in range(LANES)]
    for h in range(FOX_HEADS):
        for n in range(3):
            sel[n * FOX_HEADS + h][h * HEAD_DIM + start - HEAD_DIM + n] = sign
        for lane in ones:
            sel[PIECE_ONE][h * HEAD_DIM + lane - HEAD_DIM] = 1.0
    return jnp.asarray(sel, BF16)


def _attn_pack(qkv, c_pad):
    t = qkv.shape[0]
    tm = min(LAYOUT_ROW_TILE, t)
    hd = HEAD_DIM
    sel_q = _piece_selector(Q_C, 1.0, range(Q_ONE, Q_ONE + 3))
    sel_k = _piece_selector(K_C, -1.0, [*range(K_ONE, K_ONE + 3), *range(K_ONE2, K_ONE2 + 3)])
    sel_v = _piece_selector(HEAD_DIM, 0.0, range(V_ONE, V_ONE + 4))

    def body(x_ref, c_ref, sq_ref, sk_ref, sv_ref, qp_ref, kp_ref, vp_ref, kt_ref, vt_ref):
        pieces = _piece_rows(c_ref[...])
        q_extra = jnp.dot(pieces, sq_ref[...], preferred_element_type=F32).astype(BF16)
        k_extra = jnp.dot(pieces, sk_ref[...], preferred_element_type=F32).astype(BF16)
        v_extra = jnp.dot(pieces, sv_ref[...], preferred_element_type=F32).astype(BF16)
        for h in range(FOX_HEADS):
            hs = slice(h * hd, (h + 1) * hd)
            qp_ref[h, :, :hd] = (x_ref[:, hs].astype(F32) * (hd ** -0.5)).astype(BF16)
            qp_ref[h, :, hd:] = q_extra[:, hs]
            kp_ref[h, :, :hd] = x_ref[:, FOX_WIDTH + h * hd:FOX_WIDTH + (h + 1) * hd]
            kp_ref[h, :, hd:] = k_extra[:, hs]
            vp_ref[h, :, :hd] = x_ref[:, 2 * FOX_WIDTH + h * hd:2 * FOX_WIDTH + (h + 1) * hd]
            vp_ref[h, :, hd:] = v_extra[:, hs]
            kt_ref[h] = kp_ref[h].T
            vt_ref[h] = vp_ref[h].T

    row3 = pl.BlockSpec((FOX_HEADS, tm, LANES), lambda i: (0, i, 0))
    col3 = pl.BlockSpec((FOX_HEADS, LANES, tm), lambda i: (0, 0, i))
    sel = _resident(sel_q.shape)
    return pl.pallas_call(
        body, grid=(t // tm,),
        in_specs=[pl.BlockSpec((tm, QKV), lambda i: (i, 0)), pl.BlockSpec((tm, LANES), lambda i: (i, 0)), sel, sel, sel],
        out_specs=[row3, row3, row3, col3, col3],
        out_shape=[_sds((FOX_HEADS, t, LANES), BF16)] * 3 + [_sds((FOX_HEADS, LANES, t), BF16)] * 2,
        compiler_params=_cp(), name="attn_pack")(qkv, c_pad, sel_q, sel_k, sel_v)


def _triangle(nq, key_major):
    if key_major:
        pairs = [(i, j) for j in range(nq) for i in range(j, nq)]
    else:
        pairs = [(i, j) for i in range(nq) for j in range(i + 1)]
    return jnp.asarray([p[0] for p in pairs], jnp.int32), jnp.asarray([p[1] for p in pairs], jnp.int32)


def _attn_fwd(qp, kp, vt):
    t = qp.shape[1]
    bq = min(ATT_BLOCK, t)
    nq = t // bq
    nh = ATT_FWD_HEADS
    i_tab, j_tab = _triangle(nq, key_major=False)

    def body(it_ref, jt_ref, q_ref, k_ref, vt_ref, o_ref, lse_ref, m_sc, acc_sc):
        s = pl.program_id(1)
        i, j = it_ref[s], jt_ref[s]

        @pl.when(j == 0)
        def _():
            m_sc[...] = jnp.full(m_sc.shape, NEG, F32)
            acc_sc[...] = jnp.zeros(acc_sc.shape, F32)

        def sweep(masked):
            scores = lambda h: lax.dot_general(k_ref[h], q_ref[h], NT, preferred_element_type=F32)

            def accumulate(h, pt, rescale):
                acc_sc[h] = rescale * acc_sc[h] + jnp.dot(vt_ref[h], pt, preferred_element_type=F32)

            ahead, behind = scores(0), None
            for h in range(nh):
                st = ahead
                if h + 1 < nh:
                    ahead = scores(h + 1)
                if behind is not None:
                    accumulate(*behind)
                if masked:
                    key = lax.broadcasted_iota(jnp.int32, (bq, bq), 0)
                    qry = lax.broadcasted_iota(jnp.int32, (bq, bq), 1)
                    st = jnp.where(key <= qry, st, NEG)
                m_prev = m_sc[h]
                m_new = jnp.maximum(m_prev, jnp.max(st, axis=0, keepdims=True))
                behind = (h, jnp.exp(st - m_new).astype(BF16), jnp.exp(m_prev - m_new))
                m_sc[h] = m_new
            accumulate(*behind)

        @pl.when(j < i)
        def _():
            sweep(False)

        @pl.when(j == i)
        def _():
            sweep(True)
            for h in range(nh):
                acc = acc_sc[h]
                denom = acc[V_ONE:V_ONE + 1, :]
                o_ref[:, h * HEAD_DIM:(h + 1) * HEAD_DIM] = (acc[:HEAD_DIM, :] / denom).T.astype(BF16)
                lse_ref[h] = m_sc[h] + jnp.log(denom)

    grid_spec = pltpu.PrefetchScalarGridSpec(
        num_scalar_prefetch=2, grid=(FOX_HEADS // nh, i_tab.shape[0]),
        in_specs=[pl.BlockSpec((nh, bq, LANES), lambda hp, s, it, jt: (hp, it[s], 0)),
                  pl.BlockSpec((nh, bq, LANES), lambda hp, s, it, jt: (hp, jt[s], 0)),
                  pl.BlockSpec((nh, LANES, bq), lambda hp, s, it, jt: (hp, 0, jt[s]))],
        out_specs=[pl.BlockSpec((bq, nh * HEAD_DIM), lambda hp, s, it, jt: (it[s], hp)),
                   pl.BlockSpec((nh, 1, bq), lambda hp, s, it, jt: (hp, 0, it[s]))],
        scratch_shapes=[pltpu.VMEM((nh, 1, bq), F32), pltpu.VMEM((nh, LANES, bq), F32)])
    return pl.pallas_call(body, grid_spec=grid_spec,
                          out_shape=[_sds((t, FOX_WIDTH), BF16), _sds((FOX_HEADS, 1, t), F32)],
                          compiler_params=_cp(), name="attn_fwd")(i_tab, j_tab, qp, kp, vt)


def _even_out(attn, bch, conv_w, w_out, x, g, b):
    t, d = x.shape
    tm = min(ROW_TILE, t)
    halo_blocks = tm // SUBLANES
    cw = CONV_WIDTH

    def body(a_ref, cur_ref, prev_ref, cw_ref, wo_ref, x_ref, g_ref, b_ref, conv_ref, yb_ref, xh_ref, rs_ref):
        i = pl.program_id(0)
        z = cur_ref[:, cw:2 * cw] * cur_ref[:, 2 * cw:]
        zp = jnp.where(i == 0, 0.0, prev_ref[:, cw:2 * cw] * prev_ref[:, 2 * cw:])
        z1, z2 = _shift_down(z, zp)
        conv = (cur_ref[:, :cw] * (cw_ref[0:1, :] * z2 + cw_ref[1:2, :] * z1 + cw_ref[2:3, :] * z)).astype(BF16)
        conv_ref[...] = conv
        pre = (ALPHA * x_ref[...] + jnp.dot(a_ref[...], wo_ref[:FOX_WIDTH, :], preferred_element_type=F32)
               + jnp.dot(conv, wo_ref[FOX_WIDTH:, :], preferred_element_type=F32))
        xhat, rstd = _ln_fwd(pre)
        yb_ref[...] = (xhat * g_ref[...] + b_ref[...]).astype(BF16)
        xh_ref[...] = xhat
        rs_ref[...] = rstd

    row = lambda i: (i, 0)
    full, half = pl.BlockSpec((tm, d), row), pl.BlockSpec((tm, cw), row)
    return pl.pallas_call(
        body, grid=(t // tm,),
        in_specs=[half, pl.BlockSpec((tm, BCH), row),
                  pl.BlockSpec((SUBLANES, BCH), lambda i: (jnp.maximum(i * halo_blocks - 1, 0), 0)),
                  _resident(conv_w.shape), _resident(w_out.shape), full, _resident(g.shape), _resident(b.shape)],
        out_specs=[half, full, full, pl.BlockSpec((tm, 1), row)],
        out_shape=[_sds((t, cw), BF16), _sds((t, d), BF16), _sds((t, d), F32), _sds((t, 1), F32)],
        compiler_params=_cp(), name="even_out")(attn, bch, bch, conv_w, w_out, x, g, b)


def _gmlp_fwd(x, w_in, vg, vb, wm, bs_col, w_out, res_ln, g, b):
    t, d = x.shape
    tm = min(ROW_TILE, t)
    gb = GMLP_BLOCK
    rxh, rg, rb = res_ln

    def body(x_ref, w_ref, vg_ref, vb_ref, wm_ref, bs_ref, wo_ref, rxh_ref, rg_ref, rb_ref, g_ref, b_ref,
             sv_ref, rs_ref, o_ref, yb_ref, xh_ref, rsy_ref, a_sc):
        xb = x_ref[...].astype(BF16)
        nc = w_ref.shape[2]
        for j in range(w_ref.shape[0]):
            a_sc[:, j * nc:(j + 1) * nc] = jnp.dot(xb, w_ref[j], preferred_element_type=F32)
        halves = []
        for half in range(2):
            a = a_sc[:, half * d:(half + 1) * d]
            cdf = 0.5 * (1.0 + lax.erf(a * (2.0 ** -0.5)))
            halves.append(a * cdf)
            slope = cdf + a * (jnp.exp(-0.5 * a * a) * (1.0 / math.sqrt(2.0 * math.pi)))
            sv_ref[:, (2 * half + 1) * d:(2 * half + 2) * d] = slope.astype(BF16)
        u = halves[0]
        vhat, rstd = _ln_fwd(halves[1])
        sv_ref[:, :d] = u.astype(BF16)
        sv_ref[:, 2 * d:3 * d] = vhat.astype(BF16)
        rs_ref[...] = rstd
        vln = (vhat * vg_ref[...] + vb_ref[...]).astype(BF16)
        for blk in range(tm // gb):
            rs = slice(blk * gb, (blk + 1) * gb)
            for gi in range(GMLP_GROUPS):
                cs = slice(gi * gb, (gi + 1) * gb)
                s = jnp.dot(wm_ref[gi], vln[rs, cs], preferred_element_type=F32) + bs_ref[:, gi:gi + 1]
                o_ref[rs, cs] = (u[rs, cs] * s).astype(BF16)
        z = ALPHA * (rxh_ref[...] * rg_ref[...] + rb_ref[...]) + jnp.dot(o_ref[...], wo_ref[...], preferred_element_type=F32)
        xhat, rstd_y = _ln_fwd(z)
        yb_ref[...] = (xhat * g_ref[...] + b_ref[...]).astype(BF16)
        xh_ref[...] = xhat
        rsy_ref[...] = rstd_y

    row = lambda i: (i, 0)
    full, col, vec = pl.BlockSpec((tm, d), row), pl.BlockSpec((tm, 1), row), _resident(g.shape)
    return pl.pallas_call(
        body, grid=(t // tm,),
        in_specs=[full, _resident(w_in.shape), _resident(vg.shape), _resident(vb.shape),
                  _resident(wm.shape), _resident(bs_col.shape), _resident(w_out.shape), full, vec, vec, vec, vec],
        out_specs=[pl.BlockSpec((tm, 4 * d), row), col, full, full, full, col],
        out_shape=[_sds((t, 4 * d), BF16), _sds((t, 1), F32), _sds((t, d), BF16), _sds((t, d), BF16), _sds((t, d), F32),
                   _sds((t, 1), F32)],
        scratch_shapes=[pltpu.VMEM((tm, 2 * d), F32)],
        compiler_params=_cp(), name="gmlp_fwd")(x, w_in, vg, vb, wm, bs_col, w_out, rxh, rg, rb, g, b)


def _mm_back(pairs, wt, res, after, name):
    t = pairs[0][0].shape[0]
    k = wt.shape[1]
    tm = min(ROW_TILE, t)
    n = len(pairs)

    def body(after_ref, *refs):
        a_refs, w_ref, res_ref, o_ref = refs[:n], refs[n], refs[n + 1], refs[n + 2]
        dx = ALPHA * res_ref[...]
        for a_ref, (_, lo, hi) in zip(a_refs, pairs):
            dx = dx + jnp.dot(a_ref[...].astype(BF16), w_ref[lo:hi, :], preferred_element_type=F32)
        o_ref[...] = dx

    row = lambda i: (i, 0)
    return pl.pallas_call(
        body, grid=(t // tm,),
        in_specs=[_ANY_SPEC] + [pl.BlockSpec((tm, a.shape[1]), row) for a, _, _ in pairs]
        + [_resident(wt.shape), pl.BlockSpec((tm, k), row)],
        out_specs=pl.BlockSpec((tm, k), row), out_shape=_sds((t, k), F32),
        compiler_params=_cp(), name=name)(after, *[a for a, _, _ in pairs], wt, res)


def _mm_tn(a, b, name, *, tn, tk=None, tt=None, stack_cols=False, out_dtype=BF16, after=None):
    t, k = a.shape
    n = b.shape[1]
    tk = k if tk is None else tk
    tt = min(REDUCE_TILE if tt is None else tt, t)
    nt = t // tt

    def body(a_ref, b_ref, *rest):
        o_ref, acc_ref = rest[after is not None:]
        s = pl.program_id(2)
        part = lax.dot_general(a_ref[...].astype(BF16), b_ref[...].astype(BF16), TN, preferred_element_type=F32)
        _accumulate(acc_ref, s == 0, part)

        @pl.when(s == nt - 1)
        def _():
            o_ref[...] = acc_ref[...].astype(out_dtype).reshape(o_ref.shape)

    if stack_cols:
        assert tk == k
        out_spec = pl.BlockSpec((1, k, tn), lambda kk, j, s: (j, 0, 0))
        out_shape = _sds((n // tn, k, tn), out_dtype)
    else:
        out_spec = pl.BlockSpec((tk, tn), lambda kk, j, s: (kk, j))
        out_shape = _sds((k, n), out_dtype)
    return pl.pallas_call(
        body, grid=(k // tk, n // tn, nt),
        in_specs=[pl.BlockSpec((tt, tk), lambda kk, j, s: (s, kk)), pl.BlockSpec((tt, tn), lambda kk, j, s: (s, j))]
        + ([_ANY_SPEC] if after is not None else []),
        out_specs=out_spec, out_shape=out_shape,
        scratch_shapes=[pltpu.VMEM((tk, tn), F32)],
        compiler_params=_cp(), name=name)(a, b, *([after] if after is not None else []))


def _mm_tn_pair(a1, a2, b, name):
    t, k = a1.shape
    n = b.shape[1]
    tt = min(REDUCE_TILE, t)
    nt = t // tt

    def body(a1_ref, a2_ref, b_ref, o_ref, acc_ref):
        s = pl.program_id(0)
        bb = b_ref[...].astype(BF16)
        part = jnp.concatenate([lax.dot_general(a_ref[...].astype(BF16), bb, TN, preferred_element_type=F32)
                                for a_ref in (a1_ref, a2_ref)], axis=0)
        _accumulate(acc_ref, s == 0, part)

        @pl.when(s == nt - 1)
        def _():
            o_ref[...] = acc_ref[...].astype(BF16)

    rows = pl.BlockSpec((tt, k), lambda s: (s, 0))
    return pl.pallas_call(
        body, grid=(nt,),
        in_specs=[rows, rows, pl.BlockSpec((tt, n), lambda s: (s, 0))],
        out_specs=pl.BlockSpec((2 * k, n), lambda s: (0, 0)), out_shape=_sds((2 * k, n), BF16),
        scratch_shapes=[pltpu.VMEM((2 * k, n), F32)],
        compiler_params=_cp(), name=name)(a1, a2, b)


def _ffn_bwd_rows(dz, wo, gu, wi, ln_below, name):
    t, d = dz.shape
    tm = min(FFN_FUSED_ROW_TILE, t)
    hh = HALF_HIDDEN
    xhat, rstd, g = ln_below

    def body(dz_ref, wo_ref, gu_ref, wi_ref, xh_ref, rs_ref, g_ref, dgu_ref, dzb_ref, dg_ref, db_ref):
        first = pl.program_id(0) == 0
        a = dz_ref[...].astype(BF16)
        for c in range(2):
            gs, us = slice(c * hh, (c + 1) * hh), slice(FFN_HIDDEN + c * hh, FFN_HIDDEN + (c + 1) * hh)
            dh = lax.dot_general(a, wo_ref[gs, :], NT, preferred_element_type=F32)
            dgu_ref[:, gs] = (dh * gu_ref[:, gs].astype(F32)).astype(BF16)
            dgu_ref[:, us] = (dh * gu_ref[:, us].astype(F32)).astype(BF16)
        dx = ALPHA * dz_ref[...]
        for j in range(wi_ref.shape[0]):
            dx = dx + lax.dot_general(dgu_ref[:, j * hh:(j + 1) * hh], wi_ref[j], NT, preferred_element_type=F32)
        dzb, dg, db = _ln_bwd(dx, xh_ref[...], rs_ref[...], g_ref[...])
        dzb_ref[...] = dzb
        _accumulate(dg_ref, first, dg)
        _accumulate(db_ref, first, db)

    row = lambda i: (i, 0)
    wide, full = pl.BlockSpec((tm, 2 * FFN_HIDDEN), row), pl.BlockSpec((tm, d), row)
    vec = pl.BlockSpec((1, d), lambda i: (0, 0))
    return pl.pallas_call(
        body, grid=(t // tm,),
        in_specs=[full, _resident(wo.shape), wide, _resident(wi.shape), full, pl.BlockSpec((tm, 1), row),
                  _resident(g.shape)],
        out_specs=[wide, full, vec, vec],
        out_shape=[_sds((t, 2 * FFN_HIDDEN), BF16), _sds((t, d), F32), _sds((1, d), F32), _sds((1, d), F32)],
        compiler_params=_cp(), name=name)(dz, wo, gu, wi, xhat, rstd, g)


def _gmlp_bwd(dz, w_out, saved, rstd_v, vg, vb, wm, bs_col, w_in, ln_below):
    t, d = dz.shape
    d2 = 2 * d
    tm = min(ROW_TILE, t)
    gb = GMLP_BLOCK
    xhat_below, rstd_below, g_below = ln_below

    def body(dz_ref, wo_ref, sv_ref, rs_ref, vg_ref, vb_ref, wm_ref, bs_ref, wi_ref, xh_ref, rsb_ref, gb_ref,
             da_ref, dws_ref, dbs_ref, dvg_ref, dvb_ref, dzb_ref, dg_ref, db_ref, dvln_sc):
        first = pl.program_id(0) == 0
        u = sv_ref[:, :d].astype(F32)
        vhat = sv_ref[:, 2 * d:3 * d].astype(F32)
        rstd = rs_ref[...]
        vln = (vhat * vg_ref[...] + vb_ref[...]).astype(BF16)
        dgate = lax.dot_general(dz_ref[...].astype(BF16), wo_ref[...], NT, preferred_element_type=F32)

        @pl.when(first)
        def _():
            dws_ref[...] = jnp.zeros(dws_ref.shape, F32)
            dbs_ref[...] = jnp.zeros(dbs_ref.shape, F32)

        for blk in range(tm // gb):
            rs = slice(blk * gb, (blk + 1) * gb)
            for gi in range(GMLP_GROUPS):
                cs = slice(gi * gb, (gi + 1) * gb)
                vblk = vln[rs, cs]
                s = jnp.dot(wm_ref[gi], vblk, preferred_element_type=F32) + bs_ref[:, gi:gi + 1]
                dgb = dgate[rs, cs]
                da_ref[rs, cs] = (dgb * s * sv_ref[rs, d + gi * gb:d + (gi + 1) * gb].astype(F32)).astype(BF16)
                ds = dgb * u[rs, cs]
                dsb = ds.astype(BF16)
                dws_ref[gi] += lax.dot_general(dsb, vblk, NT, preferred_element_type=F32)
                dbs_ref[:, gi:gi + 1] += jnp.sum(ds, axis=1, keepdims=True)
                dvln_sc[rs, cs] = lax.dot_general(wm_ref[gi], dsb, TN, preferred_element_type=F32)
        dv, dvg, dvb = _ln_bwd(dvln_sc[...], vhat, rstd, vg_ref[...])
        da_ref[:, d:] = (dv * sv_ref[:, 3 * d:].astype(F32)).astype(BF16)
        _accumulate(dvg_ref, first, dvg)
        _accumulate(dvb_ref, first, dvb)
        dx = ALPHA * dz_ref[...]
        nc = wi_ref.shape[2]
        for j in range(wi_ref.shape[0]):
            dx = dx + lax.dot_general(da_ref[:, j * nc:(j + 1) * nc], wi_ref[j], NT, preferred_element_type=F32)
        dzb, dg, db = _ln_bwd(dx, xh_ref[...], rsb_ref[...], gb_ref[...])
        dzb_ref[...] = dzb
        _accumulate(dg_ref, first, dg)
        _accumulate(db_ref, first, db)

    row = lambda i: (i, 0)
    full, col = pl.BlockSpec((tm, d), row), pl.BlockSpec((tm, 1), row)
    vec = pl.BlockSpec((1, d), lambda i: (0, 0))
    return pl.pallas_call(
        body, grid=(t // tm,),
        in_specs=[full, _resident(w_out.shape), pl.BlockSpec((tm, 4 * d), row), col,
                  _resident(vg.shape), _resident(vb.shape), _resident(wm.shape), _resident(bs_col.shape),
                  _resident(w_in.shape), full, col, _resident(g_below.shape)],
        out_specs=[pl.BlockSpec((tm, d2), row), pl.BlockSpec(wm.shape, lambda i: (0, 0, 0)),
                   pl.BlockSpec(bs_col.shape, lambda i: (0, 0)), vec, vec, full, vec, vec],
        out_shape=[_sds((t, d2), BF16), _sds(wm.shape, F32), _sds(bs_col.shape, F32), _sds((1, d), F32), _sds((1, d), F32),
                   _sds((t, d), F32), _sds((1, d), F32), _sds((1, d), F32)],
        scratch_shapes=[pltpu.VMEM((tm, d), F32)],
        compiler_params=_cp(), name="gmlp_bwd")(dz, w_out, saved, rstd_v, vg, vb, wm, bs_col, w_in, xhat_below,
                                                rstd_below, g_below)


def _conv_bwd(bch, dconv, conv_w):
    t = bch.shape[0]
    tm = min(ROW_TILE, t)
    nb = t // tm
    halo_blocks = tm // SUBLANES
    cw = CONV_WIDTH

    def body(cur_ref, prev_ref, next_ref, dc_ref, dn_ref, w_ref, o_ref, dw_ref):
        i = pl.program_id(0)
        bgate, cgate, hval = cur_ref[:, :cw], cur_ref[:, cw:2 * cw], cur_ref[:, 2 * cw:]
        z = cgate * hval
        zp = jnp.where(i == 0, 0.0, prev_ref[:, cw:2 * cw] * prev_ref[:, 2 * cw:])
        z1, z2 = _shift_down(z, zp)
        w0, w1, w2 = w_ref[0:1, :], w_ref[1:2, :], w_ref[2:3, :]
        dconv = dc_ref[...]
        o_ref[:, :cw] = (dconv * (w0 * z2 + w1 * z1 + w2 * z)).astype(BF16)
        dy = dconv * bgate
        dyn = jnp.where(i == nb - 1, 0.0, dn_ref[...] * next_ref[:, :cw])
        dy1, dy2 = _shift_up(dy, dyn)
        dz = w2 * dy + w1 * dy1 + w0 * dy2
        o_ref[:, cw:2 * cw] = (dz * hval).astype(BF16)
        o_ref[:, 2 * cw:] = (dz * cgate).astype(BF16)

        @pl.when(i == 0)
        def _():
            dw_ref[...] = jnp.zeros(dw_ref.shape, F32)

        for tap, zs in enumerate((z2, z1, z)):
            dw_ref[tap:tap + 1, :] += jnp.sum(dy * zs, axis=0, keepdims=True)

    last_halo = t // SUBLANES - 1
    return pl.pallas_call(
        body, grid=(nb,),
        in_specs=[pl.BlockSpec((tm, BCH), lambda i: (i, 0)),
                  pl.BlockSpec((SUBLANES, BCH), lambda i: (jnp.maximum(i * halo_blocks - 1, 0), 0)),
                  pl.BlockSpec((SUBLANES, BCH), lambda i: (jnp.minimum((i + 1) * halo_blocks, last_halo), 0)),
                  pl.BlockSpec((tm, cw), lambda i: (i, 0)),
                  pl.BlockSpec((SUBLANES, cw), lambda i: (jnp.minimum((i + 1) * halo_blocks, last_halo), 0)),
                  _resident(conv_w.shape)],
        out_specs=[pl.BlockSpec((tm, BCH), lambda i: (i, 0)), pl.BlockSpec((SUBLANES, cw), lambda i: (0, 0))],
        out_shape=[_sds((t, BCH), BF16), _sds((SUBLANES, cw), F32)],
        compiler_params=_cp(), name="conv_bwd")(bch, bch, bch, dconv, dconv, conv_w)


def _attn_bwd_prep(dz, w_out, o, qp, lse_pad, after):
    t = o.shape[0]
    tm = min(ROW_TILE, t)
    hd = HEAD_DIM
    sel_lse = _piece_selector(Q_LSE, -1.0)
    sel_delta = _piece_selector(DO_DELTA, -1.0)
    head_of = jnp.asarray([[1.0 if col == row // hd else 0.0 for col in range(LANES)] for row in range(FOX_WIDTH)], F32)

    def body(after_ref, dz_ref, wo_ref, o_ref, qp_ref, lse_ref, sl_ref, sd_ref, seg_ref, qb_ref, dob_ref, dconv_ref):
        dzb = dz_ref[...].astype(BF16)
        do = lax.dot_general(dzb, wo_ref[:FOX_WIDTH, :], NT, preferred_element_type=F32)
        dconv_ref[...] = lax.dot_general(dzb, wo_ref[FOX_WIDTH:, :], NT, preferred_element_type=F32)
        delta = jnp.dot(o_ref[...].astype(F32) * do, seg_ref[...], precision=HIGHEST, preferred_element_type=F32)
        lse_extra = jnp.dot(_piece_rows(lse_ref[...]), sl_ref[...], preferred_element_type=F32)
        do_extra = jnp.dot(_piece_rows(delta), sd_ref[...], preferred_element_type=F32).astype(BF16)
        for h in range(FOX_HEADS):
            hs = slice(h * hd, (h + 1) * hd)
            dob_ref[h, :, :hd] = do[:, hs].astype(BF16)
            dob_ref[h, :, hd:] = do_extra[:, hs]
            qb_ref[h, :, :hd] = qp_ref[h, :, :hd]
            qb_ref[h, :, hd:] = (qp_ref[h, :, hd:].astype(F32) + lse_extra[:, hs]).astype(BF16)

    row = lambda i: (i, 0)
    row3 = pl.BlockSpec((FOX_HEADS, tm, LANES), lambda i: (0, i, 0))
    half = pl.BlockSpec((tm, FOX_WIDTH), row)
    return pl.pallas_call(
        body, grid=(t // tm,),
        in_specs=[_ANY_SPEC, pl.BlockSpec((tm, dz.shape[1]), row), _resident(w_out.shape), half, row3,
                  pl.BlockSpec((tm, LANES), row), _resident(sel_lse.shape), _resident(sel_delta.shape),
                  _resident(head_of.shape)],
        out_specs=[row3, row3, half],
        out_shape=[_sds((FOX_HEADS, t, LANES), BF16)] * 2 + [_sds((t, FOX_WIDTH), F32)],
        compiler_params=_cp(), name="attn_bwd_prep")(after, dz, w_out, o, qp, lse_pad, sel_lse, sel_delta, head_of)


def _attn_bwd(qb, kp, vp, dob, kt):
    t = qb.shape[1]
    bq = min(ATT_BLOCK, t)
    nq = t // bq
    i_tab, j_tab = _triangle(nq, key_major=True)

    def body(it_ref, jt_ref, q_ref, k_ref, v_ref, do_ref, kt_ref, dqt_ref, dk_ref, dv_ref, dk_sc, dv_sc):
        s = pl.program_id(1)
        i, j = it_ref[s], jt_ref[s]

        @pl.when(s == 0)
        def _():
            dqt_ref[...] = jnp.zeros(dqt_ref.shape, F32)

        @pl.when(i == j)
        def _():
            dk_sc[...] = jnp.zeros(dk_sc.shape, F32)
            dv_sc[...] = jnp.zeros(dv_sc.shape, F32)

        cols = pl.ds(pl.multiple_of(i * bq, bq), bq)

        def sweep(masked):
            def scores(h):
                return (lax.dot_general(k_ref[h], q_ref[h], NT, preferred_element_type=F32),
                        lax.dot_general(v_ref[h], do_ref[h], NT, preferred_element_type=F32))

            def accumulate(h, ptb, dstb):
                dv_sc[h] += jnp.dot(ptb, do_ref[h], preferred_element_type=F32)
                dk_sc[h] += jnp.dot(dstb, q_ref[h], preferred_element_type=F32)
                dqt_ref[h, :, cols] += jnp.dot(kt_ref[h], dstb, preferred_element_type=F32)

            ahead, behind = scores(0), None
            for h in range(ATT_BWD_HEADS):
                st, dpt = ahead
                if h + 1 < ATT_BWD_HEADS:
                    ahead = scores(h + 1)
                if behind is not None:
                    accumulate(*behind)
                if masked:
                    key = lax.broadcasted_iota(jnp.int32, (bq, bq), 0)
                    qry = lax.broadcasted_iota(jnp.int32, (bq, bq), 1)
                    st = jnp.where(key <= qry, st, NEG)
                pt = jnp.exp(st)
                behind = (h, pt.astype(BF16), (pt * dpt).astype(BF16))
            accumulate(*behind)

        @pl.when(i == j)
        def _():
            sweep(True)

        @pl.when(i > j)
        def _():
            sweep(False)

        @pl.when(i == nq - 1)
        def _():
            dk_ref[...] = dk_sc[...]
            dv_ref[...] = dv_sc[...].astype(BF16)

    nh = ATT_BWD_HEADS
    qblk = pl.BlockSpec((nh, bq, LANES), lambda hp, s, it, jt: (hp, it[s], 0))
    kblk = pl.BlockSpec((nh, bq, LANES), lambda hp, s, it, jt: (hp, jt[s], 0))
    grid_spec = pltpu.PrefetchScalarGridSpec(
        num_scalar_prefetch=2, grid=(FOX_HEADS // nh, i_tab.shape[0]),
        in_specs=[qblk, kblk, kblk, qblk, pl.BlockSpec((nh, LANES, bq), lambda hp, s, it, jt: (hp, 0, jt[s]))],
        out_specs=[pl.BlockSpec((nh, LANES, t), lambda hp, s, it, jt: (hp, 0, 0), pipeline_mode=pl.Buffered(1)),
                   kblk, kblk],
        scratch_shapes=[pltpu.VMEM((nh, bq, LANES), F32), pltpu.VMEM((nh, bq, LANES), F32)])
    return pl.pallas_call(body, grid_spec=grid_spec,
                          out_shape=[_sds((FOX_HEADS, LANES, t), F32), _sds((FOX_HEADS, t, LANES), F32),
                                     _sds((FOX_HEADS, t, LANES), BF16)],
                          compiler_params=_cp(ATT_BWD_VMEM_LIMIT), name="attn_bwd")(i_tab, j_tab, qb, kp, vp, dob, kt)


def _attn_unpack(dqt, dkp, dvp):
    t = dkp.shape[1]
    tm = min(LAYOUT_ROW_TILE, t)
    hd = HEAD_DIM

    def body(dqt_ref, dk_ref, dv_ref, o_ref, dc_ref):
        for h in range(FOX_HEADS):
            dq = dqt_ref[h].T
            o_ref[:, h * hd:(h + 1) * hd] = (dq[:, :hd] * (hd ** -0.5)).astype(BF16)
            o_ref[:, FOX_WIDTH + h * hd:FOX_WIDTH + (h + 1) * hd] = dk_ref[h, :, :hd].astype(BF16)
            o_ref[:, 2 * FOX_WIDTH + h * hd:2 * FOX_WIDTH + (h + 1) * hd] = dv_ref[h, :, :hd]
            dc_ref[:, h:h + 1] = dq[:, K_ONE:K_ONE + 1] - dk_ref[h, :, Q_ONE:Q_ONE + 1]

    row3 = pl.BlockSpec((FOX_HEADS, tm, LANES), lambda i: (0, i, 0))
    return pl.pallas_call(
        body, grid=(t // tm,),
        in_specs=[pl.BlockSpec((FOX_HEADS, LANES, tm), lambda i: (0, 0, i)), row3, row3],
        out_specs=[pl.BlockSpec((tm, QKV), lambda i: (i, 0)), pl.BlockSpec((tm, FOX_HEADS), lambda i: (i, 0))],
        out_shape=[_sds((t, QKV), BF16), _sds((t, FOX_HEADS), F32)],
        compiler_params=_cp(), name="attn_unpack")(dqt, dkp, dvp)


def _adamw(parts, w, m, v, name, layer=None, into=None):
    nl, r, c = w.shape
    fits = [cand for cand in [*range(SUBLANES, r, SUBLANES), r] if r % cand == 0 and cand * c * 4 <= ADAMW_BLOCK_BYTES]
    tr = max(fits) if fits else r
    npart = len(parts)
    bc1 = 1.0 - ADAM_B1 ** ADAM_STEP
    bc2 = 1.0 - ADAM_B2 ** ADAM_STEP

    def body(*refs):
        p_refs = refs[:npart]
        w_ref, m_ref, v_ref = refs[npart:npart + 3]
        g_ref, d_ref, nm_ref, nv_ref = refs[-4:]
        sums = []
        for p_ref in p_refs:
            acc = p_ref[0, 0].astype(F32)
            for s in range(1, p_ref.shape[0]):
                acc = acc + p_ref[s, 0].astype(F32)
            sums.append(acc)
        g = sums[0]
        for extra in sums[1:]:
            g = g + extra
        nm = ADAM_B1 * m_ref[0] + (1.0 - ADAM_B1) * g
        nv = ADAM_B2 * v_ref[0] + (1.0 - ADAM_B2) * (g * g)
        m_hat = nm / bc1
        v_hat = nv / bc2
        g_ref[0] = g
        d_ref[0] = -ADAM_LR * (m_hat / (jnp.sqrt(v_hat) + ADAM_EPS) + ADAM_WD * w_ref[0])
        nm_ref[0] = nm
        nv_ref[0] = nv

    first = 0 if layer is None else layer
    blk = pl.BlockSpec((1, tr, c), lambda l, i: (first + l, i, 0))
    extra = [] if into is None else list(into)
    return pl.pallas_call(
        body, grid=(nl if layer is None else 1, r // tr),
        in_specs=[pl.BlockSpec((p.shape[0], 1, tr, c), lambda l, i: (0, l, i, 0)) for p in parts] + [blk, blk, blk]
        + [_ANY_SPEC] * len(extra),
        out_specs=[blk] * 4, out_shape=[_sds(w.shape, F32)] * 4,
        input_output_aliases={npart + 3 + k: k for k in range(len(extra))},
        compiler_params=_cp(), name=name)(*parts, w, m, v, *extra)


def _to_rows(a):
    flat = a.reshape(-1)
    pad = (-flat.shape[0]) % LANES
    if pad:
        flat = jnp.concatenate([flat, jnp.zeros((pad,), flat.dtype)])
    return flat.reshape(-1, LANES)


def _ffn_fwd(xin_ln, xin_b, wi, wo, g, b, layer, target=None):
    t, d = xin_b.shape
    tm = min(FFN_FUSED_ROW_TILE, t)
    hh = HALF_HIDDEN
    rxh, rg, rb = xin_ln

    def body(x_ref, wi_ref, wo_ref, rxh_ref, rg_ref, rb_ref, g_ref, b_ref, *rest):
        gu_ref, h_ref = rest[target is not None:][:2]
        a = x_ref[...]
        for c in range(2):
            gs, us = slice(c * hh, (c + 1) * hh), slice(FFN_HIDDEN + c * hh, FFN_HIDDEN + (c + 1) * hh)
            gate = jnp.dot(a, wi_ref[c], preferred_element_type=F32)
            up = jnp.dot(a, wi_ref[2 + c], preferred_element_type=F32)
            sig = _sigmoid(gate)
            silu = gate * sig
            gu_ref[:, gs] = (up * sig * (1.0 + gate * (1.0 - sig))).astype(BF16)
            gu_ref[:, us] = silu.astype(BF16)
            h_ref[:, gs] = (silu * up).astype(BF16)
        z = ALPHA * (rxh_ref[...] * rg_ref[...] + rb_ref[...]) + jnp.dot(h_ref[...], wo_ref[...], preferred_element_type=F32)
        xhat, rstd = _ln_fwd(z)
        if target is None:
            yb_ref, xh_ref, rs_ref = rest[2:]
            yb_ref[...] = (xhat * g_ref[...] + b_ref[...]).astype(BF16)
            xh_ref[...] = xhat
            rs_ref[...] = rstd
            return
        sq_ref, dz_ref, dg_ref, db_ref = rest[3:]
        first = pl.program_id(0) == 0
        err = xhat * g_ref[...] + b_ref[...] - rest[0][...]
        dz, dg, db = _ln_bwd(err * (1.0 / d), xhat, rstd, g_ref[...])
        dz_ref[...] = dz
        _accumulate(sq_ref, first, jnp.sum(err * err, axis=0, keepdims=True))
        _accumulate(dg_ref, first, dg)
        _accumulate(db_ref, first, db)

    row = lambda i: (i, 0)
    full = pl.BlockSpec((tm, d), row)
    vec = _resident(g.shape)
    acc = pl.BlockSpec((1, d), lambda i: (0, 0))
    in_specs = [full, _resident(wi.shape), _resident(wo.shape), full, vec, vec, vec, vec]
    out_specs = [pl.BlockSpec((tm, 2 * FFN_HIDDEN), row), pl.BlockSpec((tm, FFN_HIDDEN), row)]
    out_shape = [_sds((t, 2 * FFN_HIDDEN), BF16), _sds((t, FFN_HIDDEN), BF16)]
    args = [xin_b, wi, wo, rxh, rg, rb, g, b]
    if target is None:
        out_specs += [full, full, pl.BlockSpec((tm, 1), row)]
        out_shape += [_sds((t, d), BF16), _sds((t, d), F32), _sds((t, 1), F32)]
    else:
        in_specs.append(full)
        args.append(target)
        out_specs += [acc, full, acc, acc]
        out_shape += [_sds((1, d), F32), _sds((t, d), F32), _sds((1, d), F32), _sds((1, d), F32)]
    gu, h, *tail = pl.pallas_call(body, grid=(t // tm,), in_specs=in_specs, out_specs=out_specs, out_shape=out_shape,
                                  compiler_params=_cp(), name=f"ffn_fwd_rows_{layer}")(*args)
    if target is None:
        y_b, xhat, rstd = tail
        return y_b, (xin_b, gu, h, xhat, rstd)
    return tail, (xin_b, gu, h)


def _ffn_bwd(dz, saved, wi, wo, ln_below, layer):
    xin_b, gu, h = saved[:3]
    dgu, *below = _ffn_bwd_rows(dz, wo, gu, wi, ln_below, f"ffn_bwd_rows_{layer}")
    g_out = _mm_tn(h, dz, f"ffn_dw_out_{layer}", tn=D_MODEL, tk=HALF_HIDDEN)
    g_in = _mm_tn(xin_b, dgu, f"ffn_dw_in_{layer}", tn=HALF_HIDDEN, stack_cols=True)
    return below, g_in, g_out.reshape(N_CHIPS, FFN_HIDDEN // N_CHIPS, D_MODEL)


def kernel(x, even_w_in, even_b_f, even_conv_w, even_w_out, odd_w_in, odd_v_ln_g, odd_v_ln_b, odd_w_s, odd_b_s, odd_w_out, mix_ln_g, mix_ln_b, ffn_w_in, ffn_w_out, ffn_ln_g, ffn_ln_b, loss_target, m_even_w_in, m_even_b_f, m_even_conv_w, m_even_w_out, m_odd_w_in, m_odd_v_ln_g, m_odd_v_ln_b, m_odd_w_s, m_odd_b_s, m_odd_w_out, m_mix_ln_g, m_mix_ln_b, m_ffn_w_in, m_ffn_w_out, m_ffn_ln_g, m_ffn_ln_b, v_even_w_in, v_even_b_f, v_even_conv_w, v_even_w_out, v_odd_w_in, v_odd_v_ln_g, v_odd_v_ln_b, v_odd_w_s, v_odd_b_s, v_odd_w_out, v_mix_ln_g, v_mix_ln_b, v_ffn_w_in, v_ffn_w_out, v_ffn_ln_g, v_ffn_ln_b):
    t = x.shape[1]
    d = D_MODEL
    chip = 2 * lax.axis_index("x") + lax.axis_index("y")
    x2d = x[0]
    target = loss_target[0]

    small_shard = jnp.concatenate([odd_v_ln_g.reshape(2, LANES), odd_v_ln_b.reshape(2, LANES),
                                   even_conv_w.reshape(CONV_K, LANES), jnp.zeros((1, LANES), F32)], axis=0)
    first = [jnp.swapaxes(even_w_in[0], 0, 1).astype(BF16)]
    second = [even_w_out[0].astype(BF16), small_shard]
    later = [odd_w_in[0].astype(BF16), odd_w_out[0].astype(BF16), ffn_w_in[0].astype(BF16), ffn_w_in[1].astype(BF16),
             ffn_w_out[0].astype(BF16), ffn_w_out[1].astype(BF16)]
    first_h, first_tok = _split_start(first, "gather4", "gather_first_start")
    second_h, second_tok = _split_start(second, "gather4", "gather_second_start", after=first_tok)
    later_h, later_tok = _split_start(later, "gather4", "gather_later_start", after=second_tok)
    (g_ewi,) = _gathered(first_h, "gather_first_wait", later_tok)
    ewi = g_ewi.reshape(EVEN_IN, d)
    w_even_in = jnp.concatenate([ewi[:QKV], ewi[QKV + FOX_HEADS:],
                                 jnp.pad(ewi[QKV:QKV + FOX_HEADS], ((0, LANES - FOX_HEADS), (0, 0)))], axis=0)
    chunk_id = jnp.arange(GMLP_BLOCK) // CHUNK
    gmask = chunk_id[None, :] <= chunk_id[:, None]
    w_spatial = jnp.where(gmask[None], odd_w_s[0], 0.0).astype(BF16)
    bs_col = odd_b_s[0].T
    b_f_col = even_b_f.reshape(FOX_HEADS, 1)
    ln = lambda p, l: p[l:l + 1]

    qkv, bch, fl, x2d_b = _proj(x2d, w_even_in, [(0, QKV, BF16), (QKV, QKV + BCH, F32), (QKV + BCH, EVEN_IN_PAD, F32)], "even_proj")
    fl3 = fl[:, :FOX_HEADS].T.reshape(FOX_HEADS, t // LANES, LANES).transpose(1, 0, 2)
    c3 = _fgate_fwd(fl3, b_f_col)
    c_rows = c3.transpose(1, 0, 2).reshape(FOX_HEADS, t)
    head_lanes = lambda rows: jnp.pad(rows.T, ((0, 0), (0, LANES - FOX_HEADS)))
    qp, kp, vp, kt, vt = _attn_pack(qkv, head_lanes(c_rows))
    attn, lse = _attn_fwd(qp, kp, vt)
    g_ewo, g_small = _gathered(second_h, "gather_second_wait", attn)
    w_even_out = g_ewo.reshape(d, d)
    v_ln_g = g_small[:, 0:2].reshape(1, d)
    v_ln_b = g_small[:, 2:4].reshape(1, d)
    conv_w = g_small[:, 4:7].transpose(1, 0, 2).reshape(CONV_K, CONV_WIDTH)
    conv, x1_b, xh1, rs1 = _even_out(attn, bch, conv_w, w_even_out, x2d, ln(mix_ln_g, 0), ln(mix_ln_b, 0))
    w_odd_in, g_owo, w_fi0, w_fi1, g_fo0, g_fo1 = _gathered(later_h, "gather_later_wait", x1_b)
    w_odd_out = g_owo.reshape(d, d)
    w_ffn_in = [w_fi0, w_fi1]
    w_ffn_out = [g_fo0.reshape(FFN_HIDDEN, d), g_fo1.reshape(FFN_HIDDEN, d)]
    x2_b, ffn0 = _ffn_fwd((xh1, ln(mix_ln_g, 0), ln(mix_ln_b, 0)), x1_b, w_ffn_in[0], w_ffn_out[0],
                          ln(ffn_ln_g, 0), ln(ffn_ln_b, 0), 0)

    sv_odd, rs_odd, gated, x3_b, xh3, rs3 = _gmlp_fwd(
        x2_b, w_odd_in, v_ln_g, v_ln_b, w_spatial, bs_col, w_odd_out, (ffn0[3], ln(ffn_ln_g, 0), ln(ffn_ln_b, 0)),
        ln(mix_ln_g, 1), ln(mix_ln_b, 1))
    (sq, dz4, d_fg1, d_fb1), ffn1 = _ffn_fwd((xh3, ln(mix_ln_g, 1), ln(mix_ln_b, 1)), x3_b, w_ffn_in[1], w_ffn_out[1],
                                             ln(ffn_ln_g, 1), ln(ffn_ln_b, 1), 1, target=target)

    loss = lax.psum(0.5 / d * jnp.sum(sq), ("x", "y", "c"))
    (dz3, d_mg1, d_mb1), gi_f1, go_f1 = _ffn_bwd(dz4, ffn1, w_ffn_in[1], w_ffn_out[1], (xh3, rs3, ln(mix_ln_g, 1)), 1)

    go_odd = _mm_tn(gated, dz3, "odd_dw_out", tn=d).reshape(N_CHIPS, 1, d // N_CHIPS, d)
    da_odd, dws, dbs_col, d_vg, d_vb, dz2, d_fg0, d_fb0 = _gmlp_bwd(
        dz3, w_odd_out, sv_odd, rs_odd, v_ln_g, v_ln_b, w_spatial, bs_col, w_odd_in,
        (ffn0[3], ffn0[4], ln(ffn_ln_g, 0)))
    gi_odd = _mm_tn(x2_b, da_odd, "odd_dw_in", tn=d // 2, stack_cols=True)[:, None]
    (dz1, d_mg0, d_mb0), gi_f0, go_f0 = _ffn_bwd(dz2, ffn0, w_ffn_in[0], w_ffn_out[0], (xh1, rs1, ln(mix_ln_g, 0)), 0)

    sent_early = [gi_odd, go_odd, gi_f0[:, None], gi_f1[:, None], go_f0[:, None], go_f1[:, None]]
    early_h, early_tok = _split_start(sent_early, "scatter4", "scatter_early_start")
    qb, dob, dconv = _attn_bwd_prep(dz1, w_even_out, attn, qp, head_lanes(lse.reshape(FOX_HEADS, t)), early_tok)
    go_even = _mm_tn_pair(attn, conv, dz1, "even_dw_out").reshape(N_CHIPS, 1, d // N_CHIPS, d)
    dbch, dconv_w8 = _conv_bwd(bch, dconv, conv_w)
    dqkv, dc_col = _attn_unpack(*_attn_bwd(qb, kp, vp, dob, kt))
    dc3 = dc_col.T.reshape(FOX_HEADS, t // LANES, LANES).transpose(1, 0, 2)
    dfl3, d_bf = _fgate_bwd(dc3, fl3, b_f_col)
    dfl = jnp.concatenate([dfl3.transpose(1, 0, 2).reshape(FOX_HEADS, t).T.astype(BF16),
                           jnp.zeros((t, LANES - FOX_HEADS), BF16)], axis=1)

    dws_masked = jnp.where(gmask[None], dws, 0.0)
    rep_names = ["odd_w_s", "odd_b_s", "mix_ln_g", "mix_ln_b", "ffn_ln_g", "ffn_ln_b", "even_b_f"]
    rep_grads = [dws_masked, dbs_col.T, jnp.concatenate([d_mg0, d_mg1]), jnp.concatenate([d_mb0, d_mb1]),
                 jnp.concatenate([d_fg0, d_fg1]), jnp.concatenate([d_fb0, d_fb1]), d_bf.reshape(1, FOX_HEADS)]
    rep_w = [(odd_w_s, m_odd_w_s, v_odd_w_s), (odd_b_s, m_odd_b_s, v_odd_b_s), (mix_ln_g, m_mix_ln_g, v_mix_ln_g),
             (mix_ln_b, m_mix_ln_b, v_mix_ln_b), (ffn_ln_g, m_ffn_ln_g, v_ffn_ln_g), (ffn_ln_b, m_ffn_ln_b, v_ffn_ln_b),
             (even_b_f, m_even_b_f, v_even_b_f)]
    rep_rows = [_to_rows(gr) for gr in rep_grads]
    n_rep = sum(r.shape[0] for r in rep_rows)
    pad_rep = (-n_rep) % SUBLANES
    dconv_w = dconv_w8[:CONV_K].reshape(CONV_K, N_CHIPS, LANES).transpose(1, 0, 2).reshape(N_CHIPS * CONV_K, LANES)
    packed = jnp.concatenate(rep_rows + [jnp.zeros((pad_rep, LANES), F32), d_vg.reshape(SUBLANES, LANES),
                                         d_vb.reshape(SUBLANES, LANES), dconv_w, jnp.zeros((4, LANES), F32)], axis=0)
    small_h, small_tok = _split_start([packed], "gather8", "gather_small_start")

    swap_h, swap_tok = _split_start(_scattered(early_h, "scatter_early_wait", small_tok), "swap2", "swap_early_start")
    dw_qkv = _mm_tn(dqkv, x2d_b, "even_dw_qkv", tn=d, tk=QKV // 2, after=swap_tok)
    dw_bch = _mm_tn(dbch, x2d_b, "even_dw_bch", tn=d, tk=BCH // 2)
    dw_f = _mm_tn(dfl, x2d_b, "even_dw_f", tn=d)
    gi_even = jnp.concatenate([dw_qkv, dw_f[:FOX_HEADS], dw_bch], axis=0).reshape(N_CHIPS, 1, -1, LANES)
    sent_late = [gi_even, go_even]
    late_h, late_tok = _split_start(sent_late, "scatter4", "scatter_late_start")
    grad_x = _mm_back([(dqkv, 0, QKV), (dbch, QKV, QKV + BCH), (dfl, QKV + BCH, EVEN_IN_PAD)], w_even_in, dz1,
                      late_tok, "even_dx")
    mine, theirs = _split_wait(swap_h, "swap_early_wait", grad_x)
    res = {}
    res["odd_w_in"] = _adamw([mine[0], theirs[0]], odd_w_in, m_odd_w_in, v_odd_w_in, "adamw_odd_w_in")
    res["odd_w_out"] = _adamw([mine[1], theirs[1]], odd_w_out, m_odd_w_out, v_odd_w_out, "adamw_odd_w_out")
    for nm, at, (w, m, v) in (("ffn_w_in", 2, (ffn_w_in, m_ffn_w_in, v_ffn_w_in)),
                              ("ffn_w_out", 4, (ffn_w_out, m_ffn_w_out, v_ffn_w_out))):
        upper = _adamw([mine[at + 1], theirs[at + 1]], w, m, v, f"adamw_{nm}_1", layer=1)
        res[nm] = _adamw([mine[at], theirs[at]], w, m, v, f"adamw_{nm}_0", layer=0, into=upper)
    mine_late = _scattered(late_h, "scatter_late_wait", res["ffn_w_out"][0])
    theirs_late = _exchange(mine_late, "swap2", "swap_late")
    rows = lambda a: jnp.swapaxes(a, 1, 2).reshape(1, -1, LANES)
    back = lambda a: jnp.swapaxes(a.reshape(1, EVEN_IN // N_CHIPS, d), 1, 2)
    res["even_w_in"] = [back(o) for o in _adamw([mine_late[0], theirs_late[0]], rows(even_w_in), rows(m_even_w_in),
                                                rows(v_even_w_in), "adamw_even_w_in")]
    res["even_w_out"] = _adamw([mine_late[1], theirs_late[1]], even_w_out, m_even_w_out, v_even_w_out,
                               "adamw_even_w_out")
    (packed,), (gathered,) = _split_wait(small_h, "gather_small_wait", theirs_late[0])
    gathered = lax.dynamic_update_index_in_dim(gathered, packed, 4 * lax.axis_index("x") + 2 * lax.axis_index("y")
                                               + lax.axis_index("c"), 0)

    base = n_rep + pad_rep
    own_rows = jnp.concatenate([
        lax.dynamic_slice_in_dim(gathered, base + 2 * chip, 2, axis=1),
        lax.dynamic_slice_in_dim(gathered, base + SUBLANES + 2 * chip, 2, axis=1),
        lax.dynamic_slice_in_dim(gathered, base + 2 * SUBLANES + CONV_K * chip, CONV_K, axis=1),
        jnp.zeros((N_DEV, 1, LANES), F32)], axis=1)
    small_parts = jnp.concatenate([gathered[:, :base], own_rows], axis=1)[:, None]

    def pack_small(get):
        rows = [_to_rows(get(tw)) for tw in rep_w] + [jnp.zeros((pad_rep, LANES), F32)]
        rows += [get(sh).reshape(-1, LANES) for sh in ((odd_v_ln_g, m_odd_v_ln_g, v_odd_v_ln_g),
                                                       (odd_v_ln_b, m_odd_v_ln_b, v_odd_v_ln_b),
                                                       (even_conv_w, m_even_conv_w, v_even_conv_w))]
        return jnp.concatenate(rows + [jnp.zeros((1, LANES), F32)], axis=0)[None]

    small_out = _adamw([small_parts], pack_small(lambda tw: tw[0]), pack_small(lambda tw: tw[1]),
                       pack_small(lambda tw: tw[2]), "adamw_small")

    def unpack_small(rows3):
        rows = rows3[0]
        out, off = {}, 0
        for nm, (w, _, _), r in zip(rep_names, rep_w, rep_rows):
            out[nm] = rows[off:off + r.shape[0]].reshape(-1)[:w.size].reshape(w.shape)
            off += r.shape[0]
        off += pad_rep
        out["odd_v_ln_g"] = rows[off:off + 2].reshape(odd_v_ln_g.shape)
        out["odd_v_ln_b"] = rows[off + 2:off + 4].reshape(odd_v_ln_b.shape)
        out["even_conv_w"] = rows[off + 4:off + 4 + CONV_K].reshape(even_conv_w.shape)
        return out

    small = [unpack_small(o) for o in small_out]
    order = ["even_w_in", "even_b_f", "even_conv_w", "even_w_out", "odd_w_in", "odd_v_ln_g", "odd_v_ln_b", "odd_w_s",
             "odd_b_s", "odd_w_out", "mix_ln_g", "mix_ln_b", "ffn_w_in", "ffn_w_out", "ffn_ln_g", "ffn_ln_b"]
    outs = [loss, grad_x[None]]
    for kind in range(4):
        for nm in order:
            outs.append(res[nm][kind] if nm in res else small[kind][nm])
    return tuple(outs)
```

```python
import math

import jax
import jax.numpy as jnp
from jax import lax
from jax.experimental import pallas as pl
from jax.experimental.pallas import tpu as pltpu

F32 = jnp.float32
BF16 = jnp.bfloat16

D_MODEL = 1024
FOX_HEADS = 8
HEAD_DIM = 64
FOX_WIDTH = FOX_HEADS * HEAD_DIM
CONV_WIDTH = 512
CONV_K = 3
QKV = 3 * FOX_WIDTH
BCH = 3 * CONV_WIDTH
EVEN_IN = QKV + FOX_HEADS + BCH
EVEN_IN_PAD = QKV + BCH + 128
GMLP_BLOCK = 128
GMLP_GROUPS = 8
CHUNK = 64
FFN_HIDDEN = 2816
HALF_HIDDEN = FFN_HIDDEN // 2
ALPHA = 4.0 ** 0.25
LN_EPS = 1e-5
ADAM_LR = 0.001
ADAM_B1 = 0.9
ADAM_B2 = 0.999
ADAM_EPS = 1e-08
ADAM_WD = 0.01
ADAM_STEP = 10
N_CHIPS = 4
N_DEV = 8
LANES = 128
SUBLANES = 8
ROW_TILE = 512
LAYOUT_ROW_TILE = 1024
FFN_FUSED_ROW_TILE = 512
REDUCE_TILE = 2048
ATT_BLOCK = 512
ATT_FWD_HEADS = 8
ATT_BWD_HEADS = 8
ADAMW_BLOCK_BYTES = 2 ** 20
VMEM_LIMIT = 56 * 2 ** 20
ATT_BWD_VMEM_LIMIT = 60 * 2 ** 20
NEG = -1e30
MESH = pl.DeviceIdType.MESH
HIGHEST = lax.Precision.HIGHEST
Q_C, Q_ONE, Q_LSE = 64, 67, 70
K_ONE, K_C, K_ONE2 = 64, 67, 70
V_ONE = 64
DO_DELTA = 65
NT = (((1,), (1,)), ((), ()))
TN = (((0,), (0,)), ((), ()))


def _cp(limit=VMEM_LIMIT):
    return pltpu.CompilerParams(vmem_limit_bytes=limit)


def _resident(shape):
    zeros = (0,) * len(shape)
    return pl.BlockSpec(shape, lambda *_: zeros, pipeline_mode=pl.Buffered(1))


def _sds(shape, dtype):
    return jax.ShapeDtypeStruct(tuple(shape), dtype)


_MASKS = {
    "gather4": [(1, 0, 0), (0, 1, 0), (1, 1, 0)],
    "scatter4": [(1, 0, 0), (0, 1, 0), (1, 1, 0)],
    "swap2": [(0, 0, 1)],
    "gather8": [(0, 0, 1), (0, 1, 0), (0, 1, 1), (1, 0, 0), (1, 0, 1), (1, 1, 0), (1, 1, 1)],
}


def _exchange(arrs, mode, name):
    n = len(arrs)
    masks = _MASKS[mode]
    npeer = len(masks)
    lead = {"gather4": N_CHIPS, "gather8": N_DEV}.get(mode)
    out_shapes = [_sds(((lead,) if lead else ()) + a.shape, a.dtype) for a in arrs]

    def body(*refs):
        ins, outs = refs[:n], refs[n:2 * n]
        send_sems, recv_sems, loc_sems = refs[2 * n:]
        x, y, c = lax.axis_index("x"), lax.axis_index("y"), lax.axis_index("c")
        chip, dev = 2 * x + y, 4 * x + 2 * y + c
        sends, recvs, locs = [], [], []
        for k in range(n):
            if mode == "gather4":
                locs.append(pltpu.make_async_copy(ins[k], outs[k].at[chip], loc_sems.at[k]))
            elif mode == "scatter4":
                locs.append(pltpu.make_async_copy(ins[k].at[chip], outs[k].at[chip], loc_sems.at[k]))
            elif mode == "gather8":
                locs.append(pltpu.make_async_copy(ins[k], outs[k].at[dev], loc_sems.at[k]))
        for cp in locs:
            cp.start()
        for k in range(n):
            for j, (dx, dy, dc) in enumerate(masks):
                px = 1 - x if dx else x
                py = 1 - y if dy else y
                pc = 1 - c if dc else c
                pchip, pdev = 2 * px + py, 4 * px + 2 * py + pc
                if mode == "gather4":
                    src, dst, land = ins[k], outs[k].at[chip], outs[k].at[pchip]
                elif mode == "scatter4":
                    src, dst, land = ins[k].at[pchip], outs[k].at[chip], outs[k].at[pchip]
                elif mode == "swap2":
                    src, dst, land = ins[k], outs[k], outs[k]
                else:
                    src, dst, land = ins[k], outs[k].at[dev], outs[k].at[pdev]
                s = k * npeer + j
                kw = dict(send_sem=send_sems.at[s], recv_sem=recv_sems.at[s], device_id=(px, py, pc),
                          device_id_type=MESH)
                cp = pltpu.make_async_remote_copy(src_ref=src, dst_ref=dst, **kw)
                cp.start()
                sends.append(cp)
                recvs.append(pltpu.make_async_remote_copy(src_ref=src, dst_ref=land, **kw))
        for cp in recvs:
            cp.wait_recv()
        for cp in sends:
            cp.wait_send()
        for cp in locs:
            cp.wait()

    any_spec = pl.BlockSpec(memory_space=pl.ANY)
    outs = pl.pallas_call(
        body,
        out_shape=out_shapes,
        in_specs=[any_spec] * n,
        out_specs=[any_spec] * n,
        scratch_shapes=[pltpu.SemaphoreType.DMA((n * npeer,)), pltpu.SemaphoreType.DMA((n * npeer,)),
                        pltpu.SemaphoreType.DMA((max(n, 1),))],
        name=name,
    )(*arrs)
    return list(outs)


_HBM_SPEC = pl.BlockSpec(memory_space=pltpu.HBM)
_SEM_SPEC = pl.BlockSpec(memory_space=pltpu.SEMAPHORE)
_ANY_SPEC = pl.BlockSpec(memory_space=pl.ANY)
_EFFECT = pltpu.SideEffectType.DATAFLOW_SIDE_EFFECTING


def _split_copies(mode, ins, lands, send_sems, recv_sems):
    x, y, c = lax.axis_index("x"), lax.axis_index("y"), lax.axis_index("c")
    chip, dev = 2 * x + y, 4 * x + 2 * y + c
    masks = _MASKS[mode]
    out = []
    for k in range(len(ins)):
        for j, (dx, dy, dc) in enumerate(masks):
            px = 1 - x if dx else x
            py = 1 - y if dy else y
            pc = 1 - c if dc else c
            pchip, pdev = 2 * px + py, 4 * px + 2 * py + pc
            if mode == "gather4":
                src, dst, land = ins[k], lands[k].at[chip], lands[k].at[pchip]
            elif mode == "scatter4":
                src, dst, land = ins[k].at[pchip], lands[k].at[chip], lands[k].at[pchip]
            elif mode == "swap2":
                src, dst, land = ins[k], lands[k], lands[k]
            else:
                src, dst, land = ins[k], lands[k].at[dev], lands[k].at[pdev]
            s = k * len(masks) + j
            kw = dict(send_sem=send_sems.at[s], recv_sem=recv_sems.at[s], device_id=(px, py, pc), device_id_type=MESH)
            out.append((pltpu.make_async_remote_copy(src_ref=src, dst_ref=dst, **kw),
                        pltpu.make_async_remote_copy(src_ref=src, dst_ref=land, **kw)))
    return out


def _split_start(arrs, mode, name, after=None):
    n = len(arrs)
    nsem = n * len(_MASKS[mode])
    lead = {"gather4": (N_CHIPS,), "gather8": (N_DEV,)}.get(mode, ())
    land_shapes = [lead + a.shape for a in arrs]

    def body(*refs):
        ins, lands = refs[:n], refs[n:2 * n]
        outs = refs[2 * n + (after is not None):]
        for start, _ in _split_copies(mode, ins, lands, outs[0], outs[1]):
            start.start()
        outs[-1][...] = jnp.zeros(outs[-1].shape, F32)

    srcs = [pltpu.with_memory_space_constraint(a, pltpu.HBM) for a in arrs]
    empties = [pltpu.with_memory_space_constraint(lax.empty(s, a.dtype), pltpu.HBM) for s, a in zip(land_shapes, arrs)]
    res = pl.pallas_call(
        body, name=name,
        out_shape=(pltpu.SemaphoreType.DMA((nsem,)), pltpu.SemaphoreType.DMA((nsem,)),
                   *[pltpu.HBM(a.shape, a.dtype) for a in arrs],
                   *[pltpu.HBM(s, a.dtype) for s, a in zip(land_shapes, arrs)],
                   _sds((SUBLANES, LANES), F32)),
        in_specs=[_HBM_SPEC] * (2 * n) + ([_ANY_SPEC] if after is not None else []),
        out_specs=(_SEM_SPEC, _SEM_SPEC, *[_HBM_SPEC] * (2 * n), pl.BlockSpec(memory_space=pltpu.VMEM)),
        input_output_aliases={k: 2 + k for k in range(2 * n)},
        compiler_params=pltpu.CompilerParams(has_side_effects=_EFFECT),
    )(*srcs, *empties, *([after] if after is not None else []))
    return dict(mode=mode, n=n, sems=res[:2], bufs=res[2:2 + 2 * n]), res[-1]


def _split_wait(handle, name, after):
    n, mode = handle["n"], handle["mode"]

    def body(*refs):
        ins, lands = refs[:n], refs[n:2 * n]
        send_sems, recv_sems = refs[2 * n], refs[2 * n + 1]
        for _, arrival in _split_copies(mode, ins, lands, send_sems, recv_sems):
            arrival.wait_send()
            arrival.wait_recv()

    bufs = handle["bufs"]
    res = pl.pallas_call(
        body, name=name,
        out_shape=tuple(pltpu.HBM(b.shape, b.dtype) for b in bufs),
        in_specs=[_HBM_SPEC] * (2 * n) + [_SEM_SPEC, _SEM_SPEC, _ANY_SPEC],
        out_specs=tuple([_HBM_SPEC] * (2 * n)),
        input_output_aliases={k: k for k in range(2 * n)},
        compiler_params=pltpu.CompilerParams(has_side_effects=_EFFECT),
    )(*bufs, *handle["sems"], after)
    return list(res[:n]), list(res[n:])


def _with_own(landed, own):
    chip = 2 * lax.axis_index("x") + lax.axis_index("y")
    return lax.dynamic_update_index_in_dim(landed, own, chip, 0)


def _gathered(handle, name, after):
    sent, landed = _split_wait(handle, name, after)
    return [_with_own(g, own) for g, own in zip(landed, sent)]


def _scattered(handle, name, after):
    chip = 2 * lax.axis_index("x") + lax.axis_index("y")
    sent, landed = _split_wait(handle, name, after)
    return [_with_own(r, lax.dynamic_index_in_dim(g, chip, 0, keepdims=False)) for r, g in zip(landed, sent)]


def _sigmoid(x):
    return 0.5 * jnp.tanh(0.5 * x) + 0.5


def _log_sigmoid(x):
    e = jnp.exp(-jnp.abs(x))
    log1p = jnp.where(e < 1e-2, e * (1.0 - e * (0.5 - e * (1.0 / 3.0))), jnp.log(1.0 + e))
    return jnp.minimum(x, 0.0) - log1p


def _ln_fwd(z):
    mu = jnp.mean(z, axis=-1, keepdims=True)
    zc = z - mu
    var = jnp.mean(zc * zc, axis=-1, keepdims=True)
    rstd = lax.rsqrt(var + LN_EPS)
    return zc * rstd, rstd


def _ln_bwd(dy, xhat, rstd, g):
    dxh = dy * g
    m1 = jnp.mean(dxh, axis=-1, keepdims=True)
    m2 = jnp.mean(dxh * xhat, axis=-1, keepdims=True)
    dz = rstd * (dxh - m1 - xhat * m2)
    return dz, jnp.sum(dy * xhat, axis=0, keepdims=True), jnp.sum(dy, axis=0, keepdims=True)


def _shift_down(z, halo):
    r = lax.broadcasted_iota(jnp.int32, z.shape, 0)
    z1 = jnp.where(r == 0, halo[7:8, :], pltpu.roll(z, 1, 0))
    z2 = jnp.where(r == 0, halo[6:7, :], jnp.where(r == 1, halo[7:8, :], pltpu.roll(z, 2, 0)))
    return z1, z2


def _shift_up(z, halo):
    n = z.shape[0]
    r = lax.broadcasted_iota(jnp.int32, z.shape, 0)
    z1 = jnp.where(r == n - 1, halo[0:1, :], pltpu.roll(z, n - 1, 0))
    z2 = jnp.where(r == n - 1, halo[1:2, :], jnp.where(r == n - 2, halo[0:1, :], pltpu.roll(z, n - 2, 0)))
    return z1, z2


def _triangle_ones(prefix):
    r = lax.broadcasted_iota(jnp.int32, (LANES, LANES), 0)
    c = lax.broadcasted_iota(jnp.int32, (LANES, LANES), 1)
    return ((r <= c) if prefix else (r >= c)).astype(F32)


def _same_head_chunks(rows, earlier):
    r = lax.broadcasted_iota(jnp.int32, (rows, rows), 0)
    c = lax.broadcasted_iota(jnp.int32, (rows, rows), 1)
    same = r % FOX_HEADS == c % FOX_HEADS
    return jnp.logical_and(same, (c < r) if earlier else (c > r)).astype(F32)


def _accumulate(ref, first, value):
    @pl.when(first)
    def _():
        ref[...] = value

    @pl.when(jnp.logical_not(first))
    def _():
        ref[...] += value


def _proj(x, wt, splits, name):
    t, k = x.shape
    tm = min(ROW_TILE, t)

    def body(x_ref, w_ref, *outs):
        a = x_ref[...].astype(BF16)
        for (lo, hi, dt), o in zip(splits, outs):
            o[...] = lax.dot_general(a, w_ref[lo:hi, :], NT, preferred_element_type=F32).astype(dt)
        outs[-1][...] = a

    row = lambda i: (i, 0)
    return pl.pallas_call(
        body, grid=(t // tm,),
        in_specs=[pl.BlockSpec((tm, k), row), _resident(wt.shape)],
        out_specs=[pl.BlockSpec((tm, hi - lo), row) for lo, hi, _ in splits] + [pl.BlockSpec((tm, k), row)],
        out_shape=[_sds((t, hi - lo), dt) for lo, hi, dt in splits] + [_sds((t, k), BF16)],
        compiler_params=_cp(), name=name)(x, wt)


def _fgate_fwd(fl3, b_f):
    nc = fl3.shape[0]
    rows = nc * FOX_HEADS

    def body(f_ref, b_ref, c_ref):
        within = jnp.dot(_log_sigmoid(f_ref[...] + b_ref[...]), _triangle_ones(True), precision=HIGHEST,
                         preferred_element_type=F32)
        totals = jnp.broadcast_to(within[:, LANES - 1:LANES], within.shape)
        c_ref[...] = within + jnp.dot(_same_head_chunks(rows, earlier=True), totals, precision=HIGHEST,
                                      preferred_element_type=F32)

    c2 = pl.pallas_call(body, out_shape=_sds((rows, LANES), F32), name="fgate_fwd")(
        fl3.reshape(rows, LANES), jnp.tile(b_f, (nc, 1)))
    return c2.reshape(fl3.shape)


def _fgate_bwd(dc3, fl3, b_f):
    nc = fl3.shape[0]
    rows = nc * FOX_HEADS

    def body(dc_ref, f_ref, b_ref, df_ref, db_ref):
        within = jnp.dot(dc_ref[...], _triangle_ones(False), precision=HIGHEST, preferred_element_type=F32)
        totals = jnp.broadcast_to(within[:, 0:1], within.shape)
        dlf = within + jnp.dot(_same_head_chunks(rows, earlier=False), totals, precision=HIGHEST,
                               preferred_element_type=F32)
        df = dlf * (1.0 - _sigmoid(f_ref[...] + b_ref[...]))
        df_ref[...] = df
        head = lax.broadcasted_iota(jnp.int32, (FOX_HEADS, rows), 0)
        row = lax.broadcasted_iota(jnp.int32, (FOX_HEADS, rows), 1)
        of_head = (row % FOX_HEADS == head).astype(F32)
        per_row = jnp.broadcast_to(jnp.sum(df, axis=1, keepdims=True), df.shape)
        db_ref[...] = jnp.dot(of_head, per_row, precision=HIGHEST, preferred_element_type=F32)[:, 0:1]

    df2, db = pl.pallas_call(body, out_shape=[_sds((rows, LANES), F32), _sds((FOX_HEADS, 1), F32)], name="fgate_bwd")(
        dc3.reshape(rows, LANES), fl3.reshape(rows, LANES), jnp.tile(b_f, (nc, 1)))
    return df2.reshape(fl3.shape), db


def _split3(c):
    hi = c.astype(BF16).astype(F32)
    mid = (c - hi).astype(BF16).astype(F32)
    lo = (c - hi - mid).astype(BF16).astype(F32)
    return hi, mid, lo


PIECE_ONE = 3 * FOX_HEADS


def _piece_rows(values):
    hi, mid, lo = _split3(values)
    lane = lax.broadcasted_iota(jnp.int32, values.shape, 1)
    row = hi + pltpu.roll(mid, FOX_HEADS, 1) + pltpu.roll(lo, 2 * FOX_HEADS, 1) + jnp.where(lane == PIECE_ONE, 1.0, 0.0)
    return row.astype(BF16)


def _piece_selector(start, sign, ones=()):
    sel = [[0.0] * FOX_WIDTH for _ in range(LANES)]
    for h in range(FOX_HEADS):
        for n in range(3):
            sel[n * FOX_HEADS + h][h * HEAD_DIM + start - HEAD_DIM + n] = sign
        for lane in ones:
            sel[PIECE_ONE][h * HEAD_DIM + lane - HEAD_DIM] = 1.0
    return jnp.asarray(sel, BF16)


def _attn_pack(qkv, c_pad):
    t = qkv.shape[0]
    tm = min(LAYOUT_ROW_TILE, t)
    hd = HEAD_DIM
    sel_q = _piece_selector(Q_C, 1.0, range(Q_ONE, Q_ONE + 3))
    sel_k = _piece_selector(K_C, -1.0, [*range(K_ONE, K_ONE + 3), *range(K_ONE2, K_ONE2 + 3)])
    sel_v = _piece_selector(HEAD_DIM, 0.0, range(V_ONE, V_ONE + 4))

    def body(x_ref, c_ref, sq_ref, sk_ref, sv_ref, qp_ref, kp_ref, vp_ref, kt_ref, vt_ref):
        pieces = _piece_rows(c_ref[...])
        q_extra = jnp.dot(pieces, sq_ref[...], preferred_element_type=F32).astype(BF16)
        k_extra = jnp.dot(pieces, sk_ref[...], preferred_element_type=F32).astype(BF16)
        v_extra = jnp.dot(pieces, sv_ref[...], preferred_element_type=F32).astype(BF16)
        for h in range(FOX_HEADS):
            hs = slice(h * hd, (h + 1) * hd)
            qp_ref[h, :, :hd] = (x_ref[:, hs].astype(F32) * (hd ** -0.5)).astype(BF16)
            qp_ref[h, :, hd:] = q_extra[:, hs]
            kp_ref[h, :, :hd] = x_ref[:, FOX_WIDTH + h * hd:FOX_WIDTH + (h + 1) * hd]
            kp_ref[h, :, hd:] = k_extra[:, hs]
            vp_ref[h, :, :hd] = x_ref[:, 2 * FOX_WIDTH + h * hd:2 * FOX_WIDTH + (h + 1) * hd]
            vp_ref[h, :, hd:] = v_extra[:, hs]
            kt_ref[h] = kp_ref[h].T
            vt_ref[h] = vp_ref[h].T

    row3 = pl.BlockSpec((FOX_HEADS, tm, LANES), lambda i: (0, i, 0))
    col3 = pl.BlockSpec((FOX_HEADS, LANES, tm), lambda i: (0, 0, i))
    sel = _resident(sel_q.shape)
    return pl.pallas_call(
        body, grid=(t // tm,),
        in_specs=[pl.BlockSpec((tm, QKV), lambda i: (i, 0)), pl.BlockSpec((tm, LANES), lambda i: (i, 0)), sel, sel, sel],
        out_specs=[row3, row3, row3, col3, col3],
        out_shape=[_sds((FOX_HEADS, t, LANES), BF16)] * 3 + [_sds((FOX_HEADS, LANES, t), BF16)] * 2,
        compiler_params=_cp(), name="attn_pack")(qkv, c_pad, sel_q, sel_k, sel_v)


def _triangle(nq, key_major):
    if key_major:
        pairs = [(i, j) for j in range(nq) for i in range(j, nq)]
    else:
        pairs = [(i, j) for i in range(nq) for j in range(i + 1)]
    return jnp.asarray([p[0] for p in pairs], jnp.int32), jnp.asarray([p[1] for p in pairs], jnp.int32)


def _attn_fwd(qp, kp, vt):
    t = qp.shape[1]
    bq = min(ATT_BLOCK, t)
    nq = t // bq
    nh = ATT_FWD_HEADS
    i_tab, j_tab = _triangle(nq, key_major=False)

    def body(it_ref, jt_ref, q_ref, k_ref, vt_ref, o_ref, lse_ref, m_sc, acc_sc):
        s = pl.program_id(1)
        i, j = it_ref[s], jt_ref[s]

        @pl.when(j == 0)
        def _():
            m_sc[...] = jnp.full(m_sc.shape, NEG, F32)
            acc_sc[...] = jnp.zeros(acc_sc.shape, F32)

        def sweep(masked):
            scores = lambda h: lax.dot_general(k_ref[h], q_ref[h], NT, preferred_element_type=F32)

            def accumulate(h, pt, rescale):
                acc_sc[h] = rescale * acc_sc[h] + jnp.dot(vt_ref[h], pt, preferred_element_type=F32)

            ahead, behind = scores(0), None
            for h in range(nh):
                st = ahead
                if h + 1 < nh:
                    ahead = scores(h + 1)
                if behind is not None:
                    accumulate(*behind)
                if masked:
                    key = lax.broadcasted_iota(jnp.int32, (bq, bq), 0)
                    qry = lax.broadcasted_iota(jnp.int32, (bq, bq), 1)
                    st = jnp.where(key <= qry, st, NEG)
                m_prev = m_sc[h]
                m_new = jnp.maximum(m_prev, jnp.max(st, axis=0, keepdims=True))
                behind = (h, jnp.exp(st - m_new).astype(BF16), jnp.exp(m_prev - m_new))
                m_sc[h] = m_new
            accumulate(*behind)

        @pl.when(j < i)
        def _():
            sweep(False)

        @pl.when(j == i)
        def _():
            sweep(True)
            for h in range(nh):
                acc = acc_sc[h]
                denom = acc[V_ONE:V_ONE + 1, :]
                o_ref[:, h * HEAD_DIM:(h + 1) * HEAD_DIM] = (acc[:HEAD_DIM, :] / denom).T.astype(BF16)
                lse_ref[h] = m_sc[h] + jnp.log(denom)

    grid_spec = pltpu.PrefetchScalarGridSpec(
        num_scalar_prefetch=2, grid=(FOX_HEADS // nh, i_tab.shape[0]),
        in_specs=[pl.BlockSpec((nh, bq, LANES), lambda hp, s, it, jt: (hp, it[s], 0)),
                  pl.BlockSpec((nh, bq, LANES), lambda hp, s, it, jt: (hp, jt[s], 0)),
                  pl.BlockSpec((nh, LANES, bq), lambda hp, s, it, jt: (hp, 0, jt[s]))],
        out_specs=[pl.BlockSpec((bq, nh * HEAD_DIM), lambda hp, s, it, jt: (it[s], hp)),
                   pl.BlockSpec((nh, 1, bq), lambda hp, s, it, jt: (hp, 0, it[s]))],
        scratch_shapes=[pltpu.VMEM((nh, 1, bq), F32), pltpu.VMEM((nh, LANES, bq), F32)])
    return pl.pallas_call(body, grid_spec=grid_spec,
                          out_shape=[_sds((t, FOX_WIDTH), BF16), _sds((FOX_HEADS, 1, t), F32)],
                          compiler_params=_cp(), name="attn_fwd")(i_tab, j_tab, qp, kp, vt)


def _even_out(attn, bch, conv_w, w_out, x, g, b):
    t, d = x.shape
    tm = min(ROW_TILE, t)
    halo_blocks = tm // SUBLANES
    cw = CONV_WIDTH

    def body(a_ref, cur_ref, prev_ref, cw_ref, wo_ref, x_ref, g_ref, b_ref, conv_ref, yb_ref, xh_ref, rs_ref):
        i = pl.program_id(0)
        z = cur_ref[:, cw:2 * cw] * cur_ref[:, 2 * cw:]
        zp = jnp.where(i == 0, 0.0, prev_ref[:, cw:2 * cw] * prev_ref[:, 2 * cw:])
        z1, z2 = _shift_down(z, zp)
        conv = (cur_ref[:, :cw] * (cw_ref[0:1, :] * z2 + cw_ref[1:2, :] * z1 + cw_ref[2:3, :] * z)).astype(BF16)
        conv_ref[...] = conv
        pre = (ALPHA * x_ref[...] + jnp.dot(a_ref[...], wo_ref[:FOX_WIDTH, :], preferred_element_type=F32)
               + jnp.dot(conv, wo_ref[FOX_WIDTH:, :], preferred_element_type=F32))
        xhat, rstd = _ln_fwd(pre)
        yb_ref[...] = (xhat * g_ref[...] + b_ref[...]).astype(BF16)
        xh_ref[...] = xhat
        rs_ref[...] = rstd

    row = lambda i: (i, 0)
    full, half = pl.BlockSpec((tm, d), row), pl.BlockSpec((tm, cw), row)
    return pl.pallas_call(
        body, grid=(t // tm,),
        in_specs=[half, pl.BlockSpec((tm, BCH), row),
                  pl.BlockSpec((SUBLANES, BCH), lambda i: (jnp.maximum(i * halo_blocks - 1, 0), 0)),
                  _resident(conv_w.shape), _resident(w_out.shape), full, _resident(g.shape), _resident(b.shape)],
        out_specs=[half, full, full, pl.BlockSpec((tm, 1), row)],
        out_shape=[_sds((t, cw), BF16), _sds((t, d), BF16), _sds((t, d), F32), _sds((t, 1), F32)],
        compiler_params=_cp(), name="even_out")(attn, bch, bch, conv_w, w_out, x, g, b)


def _gmlp_fwd(x, w_in, vg, vb, wm, bs_col, w_out, res_ln, g, b):
    t, d = x.shape
    tm = min(ROW_TILE, t)
    gb = GMLP_BLOCK
    rxh, rg, rb = res_ln

    def body(x_ref, w_ref, vg_ref, vb_ref, wm_ref, bs_ref, wo_ref, rxh_ref, rg_ref, rb_ref, g_ref, b_ref,
             sv_ref, rs_ref, o_ref, yb_ref, xh_ref, rsy_ref, a_sc):
        xb = x_ref[...].astype(BF16)
        nc = w_ref.shape[2]
        for j in range(w_ref.shape[0]):
            a_sc[:, j * nc:(j + 1) * nc] = jnp.dot(xb, w_ref[j], preferred_element_type=F32)
        halves = []
        for half in range(2):
            a = a_sc[:, half * d:(half + 1) * d]
            cdf = 0.5 * (1.0 + lax.erf(a * (2.0 ** -0.5)))
            halves.append(a * cdf)
            slope = cdf + a * (jnp.exp(-0.5 * a * a) * (1.0 / math.sqrt(2.0 * math.pi)))
            sv_ref[:, (2 * half + 1) * d:(2 * half + 2) * d] = slope.astype(BF16)
        u = halves[0]
        vhat, rstd = _ln_fwd(halves[1])
        sv_ref[:, :d] = u.astype(BF16)
        sv_ref[:, 2 * d:3 * d] = vhat.astype(BF16)
        rs_ref[...] = rstd
        vln = (vhat * vg_ref[...] + vb_ref[...]).astype(BF16)
        for blk in range(tm // gb):
            rs = slice(blk * gb, (blk + 1) * gb)
            for gi in range(GMLP_GROUPS):
                cs = slice(gi * gb, (gi + 1) * gb)
                s = jnp.dot(wm_ref[gi], vln[rs, cs], preferred_element_type=F32) + bs_ref[:, gi:gi + 1]
                o_ref[rs, cs] = (u[rs, cs] * s).astype(BF16)
        z = ALPHA * (rxh_ref[...] * rg_ref[...] + rb_ref[...]) + jnp.dot(o_ref[...], wo_ref[...], preferred_element_type=F32)
        xhat, rstd_y = _ln_fwd(z)
        yb_ref[...] = (xhat * g_ref[...] + b_ref[...]).astype(BF16)
        xh_ref[...] = xhat
        rsy_ref[...] = rstd_y

    row = lambda i: (i, 0)
    full, col, vec = pl.BlockSpec((tm, d), row), pl.BlockSpec((tm, 1), row), _resident(g.shape)
    return pl.pallas_call(
        body, grid=(t // tm,),
        in_specs=[full, _resident(w_in.shape), _resident(vg.shape), _resident(vb.shape),
                  _resident(wm.shape), _resident(bs_col.shape), _resident(w_out.shape), full, vec, vec, vec, vec],
        out_specs=[pl.BlockSpec((tm, 4 * d), row), col, full, full, full, col],
        out_shape=[_sds((t, 4 * d), BF16), _sds((t, 1), F32), _sds((t, d), BF16), _sds((t, d), BF16), _sds((t, d), F32),
                   _sds((t, 1), F32)],
        scratch_shapes=[pltpu.VMEM((tm, 2 * d), F32)],
        compiler_params=_cp(), name="gmlp_fwd")(x, w_in, vg, vb, wm, bs_col, w_out, rxh, rg, rb, g, b)


def _mm_back(pairs, wt, res, after, name):
    t = pairs[0][0].shape[0]
    k = wt.shape[1]
    tm = min(ROW_TILE, t)
    n = len(pairs)

    def body(after_ref, *refs):
        a_refs, w_ref, res_ref, o_ref = refs[:n], refs[n], refs[n + 1], refs[n + 2]
        dx = ALPHA * res_ref[...]
        for a_ref, (_, lo, hi) in zip(a_refs, pairs):
            dx = dx + jnp.dot(a_ref[...].astype(BF16), w_ref[lo:hi, :], preferred_element_type=F32)
        o_ref[...] = dx

    row = lambda i: (i, 0)
    return pl.pallas_call(
        body, grid=(t // tm,),
        in_specs=[_ANY_SPEC] + [pl.BlockSpec((tm, a.shape[1]), row) for a, _, _ in pairs]
        + [_resident(wt.shape), pl.BlockSpec((tm, k), row)],
        out_specs=pl.BlockSpec((tm, k), row), out_shape=_sds((t, k), F32),
        compiler_params=_cp(), name=name)(after, *[a for a, _, _ in pairs], wt, res)


def _mm_tn(a, b, name, *, tn, tk=None, tt=None, stack_cols=False, out_dtype=BF16, after=None):
    t, k = a.shape
    n = b.shape[1]
    tk = k if tk is None else tk
    tt = min(REDUCE_TILE if tt is None else tt, t)
    nt = t // tt

    def body(a_ref, b_ref, *rest):
        o_ref, acc_ref = rest[after is not None:]
        s = pl.program_id(2)
        part = lax.dot_general(a_ref[...].astype(BF16), b_ref[...].astype(BF16), TN, preferred_element_type=F32)
        _accumulate(acc_ref, s == 0, part)

        @pl.when(s == nt - 1)
        def _():
            o_ref[...] = acc_ref[...].astype(out_dtype).reshape(o_ref.shape)

    if stack_cols:
        assert tk == k
        out_spec = pl.BlockSpec((1, k, tn), lambda kk, j, s: (j, 0, 0))
        out_shape = _sds((n // tn, k, tn), out_dtype)
    else:
        out_spec = pl.BlockSpec((tk, tn), lambda kk, j, s: (kk, j))
        out_shape = _sds((k, n), out_dtype)
    return pl.pallas_call(
        body, grid=(k // tk, n // tn, nt),
        in_specs=[pl.BlockSpec((tt, tk), lambda kk, j, s: (s, kk)), pl.BlockSpec((tt, tn), lambda kk, j, s: (s, j))]
        + ([_ANY_SPEC] if after is not None else []),
        out_specs=out_spec, out_shape=out_shape,
        scratch_shapes=[pltpu.VMEM((tk, tn), F32)],
        compiler_params=_cp(), name=name)(a, b, *([after] if after is not None else []))


def _mm_tn_pair(a1, a2, b, name):
    t, k = a1.shape
    n = b.shape[1]
    tt = min(REDUCE_TILE, t)
    nt = t // tt

    def body(a1_ref, a2_ref, b_ref, o_ref, acc_ref):
        s = pl.program_id(0)
        bb = b_ref[...].astype(BF16)
        part = jnp.concatenate([lax.dot_general(a_ref[...].astype(BF16), bb, TN, preferred_element_type=F32)
                                for a_ref in (a1_ref, a2_ref)], axis=0)
        _accumulate(acc_ref, s == 0, part)

        @pl.when(s == nt - 1)
        def _():
            o_ref[...] = acc_ref[...].astype(BF16)

    rows = pl.BlockSpec((tt, k), lambda s: (s, 0))
    return pl.pallas_call(
        body, grid=(nt,),
        in_specs=[rows, rows, pl.BlockSpec((tt, n), lambda s: (s, 0))],
        out_specs=pl.BlockSpec((2 * k, n), lambda s: (0, 0)), out_shape=_sds((2 * k, n), BF16),
        scratch_shapes=[pltpu.VMEM((2 * k, n), F32)],
        compiler_params=_cp(), name=name)(a1, a2, b)


def _ffn_bwd_rows(dz, wo, gu, wi, ln_below, name):
    t, d = dz.shape
    tm = min(FFN_FUSED_ROW_TILE, t)
    hh = HALF_HIDDEN
    xhat, rstd, g = ln_below

    def body(dz_ref, wo_ref, gu_ref, wi_ref, xh_ref, rs_ref, g_ref, dgu_ref, dzb_ref, dg_ref, db_ref):
        first = pl.program_id(0) == 0
        a = dz_ref[...].astype(BF16)
        for c in range(2):
            gs, us = slice(c * hh, (c + 1) * hh), slice(FFN_HIDDEN + c * hh, FFN_HIDDEN + (c + 1) * hh)
            dh = lax.dot_general(a, wo_ref[gs, :], NT, preferred_element_type=F32)
            dgu_ref[:, gs] = (dh * gu_ref[:, gs].astype(F32)).astype(BF16)
            dgu_ref[:, us] = (dh * gu_ref[:, us].astype(F32)).astype(BF16)
        dx = ALPHA * dz_ref[...]
        for j in range(wi_ref.shape[0]):
            dx = dx + lax.dot_general(dgu_ref[:, j * hh:(j + 1) * hh], wi_ref[j], NT, preferred_element_type=F32)
        dzb, dg, db = _ln_bwd(dx, xh_ref[...], rs_ref[...], g_ref[...])
        dzb_ref[...] = dzb
        _accumulate(dg_ref, first, dg)
        _accumulate(db_ref, first, db)

    row = lambda i: (i, 0)
    wide, full = pl.BlockSpec((tm, 2 * FFN_HIDDEN), row), pl.BlockSpec((tm, d), row)
    vec = pl.BlockSpec((1, d), lambda i: (0, 0))
    return pl.pallas_call(
        body, grid=(t // tm,),
        in_specs=[full, _resident(wo.shape), wide, _resident(wi.shape), full, pl.BlockSpec((tm, 1), row),
                  _resident(g.shape)],
        out_specs=[wide, full, vec, vec],
        out_shape=[_sds((t, 2 * FFN_HIDDEN), BF16), _sds((t, d), F32), _sds((1, d), F32), _sds((1, d), F32)],
        compiler_params=_cp(ATT_BWD_VMEM_LIMIT), name=name)(dz, wo, gu, wi, xhat, rstd, g)


def _gmlp_bwd(dz, w_out, saved, rstd_v, vg, vb, wm, bs_col, w_in, ln_below):
    t, d = dz.shape
    d2 = 2 * d
    tm = min(ROW_TILE, t)
    gb = GMLP_BLOCK
    xhat_below, rstd_below, g_below = ln_below

    def body(dz_ref, wo_ref, sv_ref, rs_ref, vg_ref, vb_ref, wm_ref, bs_ref, wi_ref, xh_ref, rsb_ref, gb_ref,
             da_ref, dws_ref, dbs_ref, dvg_ref, dvb_ref, dzb_ref, dg_ref, db_ref, dvln_sc):
        first = pl.program_id(0) == 0
        u = sv_ref[:, :d].astype(F32)
        vhat = sv_ref[:, 2 * d:3 * d].astype(F32)
        rstd = rs_ref[...]
        vln = (vhat * vg_ref[...] + vb_ref[...]).astype(BF16)
        dgate = lax.dot_general(dz_ref[...].astype(BF16), wo_ref[...], NT, preferred_element_type=F32)

        @pl.when(first)
        def _():
            dws_ref[...] = jnp.zeros(dws_ref.shape, F32)
            dbs_ref[...] = jnp.zeros(dbs_ref.shape, F32)

        for blk in range(tm // gb):
            rs = slice(blk * gb, (blk + 1) * gb)
            for gi in range(GMLP_GROUPS):
                cs = slice(gi * gb, (gi + 1) * gb)
                vblk = vln[rs, cs]
                s = jnp.dot(wm_ref[gi], vblk, preferred_element_type=F32) + bs_ref[:, gi:gi + 1]
                dgb = dgate[rs, cs]
                da_ref[rs, cs] = (dgb * s * sv_ref[rs, d + gi * gb:d + (gi + 1) * gb].astype(F32)).astype(BF16)
                ds = dgb * u[rs, cs]
                dsb = ds.astype(BF16)
                dws_ref[gi] += lax.dot_general(dsb, vblk, NT, preferred_element_type=F32)
                dbs_ref[:, gi:gi + 1] += jnp.sum(ds, axis=1, keepdims=True)
                dvln_sc[rs, cs] = lax.dot_general(wm_ref[gi], dsb, TN, preferred_element_type=F32)
        dv, dvg, dvb = _ln_bwd(dvln_sc[...], vhat, rstd, vg_ref[...])
        da_ref[:, d:] = (dv * sv_ref[:, 3 * d:].astype(F32)).astype(BF16)
        _accumulate(dvg_ref, first, dvg)
        _accumulate(dvb_ref, first, dvb)
        dx = ALPHA * dz_ref[...]
        nc = wi_ref.shape[2]
        for j in range(wi_ref.shape[0]):
            dx = dx + lax.dot_general(da_ref[:, j * nc:(j + 1) * nc], wi_ref[j], NT, preferred_element_type=F32)
        dzb, dg, db = _ln_bwd(dx, xh_ref[...], rsb_ref[...], gb_ref[...])
        dzb_ref[...] = dzb
        _accumulate(dg_ref, first, dg)
        _accumulate(db_ref, first, db)

    row = lambda i: (i, 0)
    full, col = pl.BlockSpec((tm, d), row), pl.BlockSpec((tm, 1), row)
    vec = pl.BlockSpec((1, d), lambda i: (0, 0))
    return pl.pallas_call(
        body, grid=(t // tm,),
        in_specs=[full, _resident(w_out.shape), pl.BlockSpec((tm, 4 * d), row), col,
                  _resident(vg.shape), _resident(vb.shape), _resident(wm.shape), _resident(bs_col.shape),
                  _resident(w_in.shape), full, col, _resident(g_below.shape)],
        out_specs=[pl.BlockSpec((tm, d2), row), pl.BlockSpec(wm.shape, lambda i: (0, 0, 0)),
                   pl.BlockSpec(bs_col.shape, lambda i: (0, 0)), vec, vec, full, vec, vec],
        out_shape=[_sds((t, d2), BF16), _sds(wm.shape, F32), _sds(bs_col.shape, F32), _sds((1, d), F32), _sds((1, d), F32),
                   _sds((t, d), F32), _sds((1, d), F32), _sds((1, d), F32)],
        scratch_shapes=[pltpu.VMEM((tm, d), F32)],
        compiler_params=_cp(), name="gmlp_bwd")(dz, w_out, saved, rstd_v, vg, vb, wm, bs_col, w_in, xhat_below,
                                                rstd_below, g_below)


def _conv_bwd(bch, dconv, conv_w):
    t = bch.shape[0]
    tm = min(ROW_TILE, t)
    nb = t // tm
    halo_blocks = tm // SUBLANES
    cw = CONV_WIDTH

    def body(cur_ref, prev_ref, next_ref, dc_ref, dn_ref, w_ref, o_ref, dw_ref):
        i = pl.program_id(0)
        bgate, cgate, hval = cur_ref[:, :cw], cur_ref[:, cw:2 * cw], cur_ref[:, 2 * cw:]
        z = cgate * hval
        zp = jnp.where(i == 0, 0.0, prev_ref[:, cw:2 * cw] * prev_ref[:, 2 * cw:])
        z1, z2 = _shift_down(z, zp)
        w0, w1, w2 = w_ref[0:1, :], w_ref[1:2, :], w_ref[2:3, :]
        dconv = dc_ref[...]
        o_ref[:, :cw] = (dconv * (w0 * z2 + w1 * z1 + w2 * z)).astype(BF16)
        dy = dconv * bgate
        dyn = jnp.where(i == nb - 1, 0.0, dn_ref[...] * next_ref[:, :cw])
        dy1, dy2 = _shift_up(dy, dyn)
        dz = w2 * dy + w1 * dy1 + w0 * dy2
        o_ref[:, cw:2 * cw] = (dz * hval).astype(BF16)
        o_ref[:, 2 * cw:] = (dz * cgate).astype(BF16)

        @pl.when(i == 0)
        def _():
            dw_ref[...] = jnp.zeros(dw_ref.shape, F32)

        for tap, zs in enumerate((z2, z1, z)):
            dw_ref[tap:tap + 1, :] += jnp.sum(dy * zs, axis=0, keepdims=True)

    last_halo = t // SUBLANES - 1
    return pl.pallas_call(
        body, grid=(nb,),
        in_specs=[pl.BlockSpec((tm, BCH), lambda i: (i, 0)),
                  pl.BlockSpec((SUBLANES, BCH), lambda i: (jnp.maximum(i * halo_blocks - 1, 0), 0)),
                  pl.BlockSpec((SUBLANES, BCH), lambda i: (jnp.minimum((i + 1) * halo_blocks, last_halo), 0)),
                  pl.BlockSpec((tm, cw), lambda i: (i, 0)),
                  pl.BlockSpec((SUBLANES, cw), lambda i: (jnp.minimum((i + 1) * halo_blocks, last_halo), 0)),
                  _resident(conv_w.shape)],
        out_specs=[pl.BlockSpec((tm, BCH), lambda i: (i, 0)), pl.BlockSpec((SUBLANES, cw), lambda i: (0, 0))],
        out_shape=[_sds((t, BCH), BF16), _sds((SUBLANES, cw), F32)],
        compiler_params=_cp(), name="conv_bwd")(bch, bch, bch, dconv, dconv, conv_w)


def _attn_bwd_prep(dz, w_out, o, qp, lse_pad, after):
    t = o.shape[0]
    tm = min(ROW_TILE, t)
    hd = HEAD_DIM
    sel_lse = _piece_selector(Q_LSE, -1.0)
    sel_delta = _piece_selector(DO_DELTA, -1.0)
    head_of = jnp.asarray([[1.0 if col == row // hd else 0.0 for col in range(LANES)] for row in range(FOX_WIDTH)], F32)

    def body(after_ref, dz_ref, wo_ref, o_ref, qp_ref, lse_ref, sl_ref, sd_ref, seg_ref, qb_ref, dob_ref, dconv_ref):
        dzb = dz_ref[...].astype(BF16)
        do = lax.dot_general(dzb, wo_ref[:FOX_WIDTH, :], NT, preferred_element_type=F32)
        dconv_ref[...] = lax.dot_general(dzb, wo_ref[FOX_WIDTH:, :], NT, preferred_element_type=F32)
        delta = jnp.dot(o_ref[...].astype(F32) * do, seg_ref[...], precision=HIGHEST, preferred_element_type=F32)
        lse_extra = jnp.dot(_piece_rows(lse_ref[...]), sl_ref[...], preferred_element_type=F32)
        do_extra = jnp.dot(_piece_rows(delta), sd_ref[...], preferred_element_type=F32).astype(BF16)
        for h in range(FOX_HEADS):
            hs = slice(h * hd, (h + 1) * hd)
            dob_ref[h, :, :hd] = do[:, hs].astype(BF16)
            dob_ref[h, :, hd:] = do_extra[:, hs]
            qb_ref[h, :, :hd] = qp_ref[h, :, :hd]
            qb_ref[h, :, hd:] = (qp_ref[h, :, hd:].astype(F32) + lse_extra[:, hs]).astype(BF16)

    row = lambda i: (i, 0)
    row3 = pl.BlockSpec((FOX_HEADS, tm, LANES), lambda i: (0, i, 0))
    half = pl.BlockSpec((tm, FOX_WIDTH), row)
    return pl.pallas_call(
        body, grid=(t // tm,),
        in_specs=[_ANY_SPEC, pl.BlockSpec((tm, dz.shape[1]), row), _resident(w_out.shape), half, row3,
                  pl.BlockSpec((tm, LANES), row), _resident(sel_lse.shape), _resident(sel_delta.shape),
                  _resident(head_of.shape)],
        out_specs=[row3, row3, half],
        out_shape=[_sds((FOX_HEADS, t, LANES), BF16)] * 2 + [_sds((t, FOX_WIDTH), F32)],
        compiler_params=_cp(), name="attn_bwd_prep")(after, dz, w_out, o, qp, lse_pad, sel_lse, sel_delta, head_of)


def _attn_bwd(qb, kp, vp, dob, kt):
    t = qb.shape[1]
    bq = min(ATT_BLOCK, t)
    nq = t // bq
    i_tab, j_tab = _triangle(nq, key_major=True)

    def body(it_ref, jt_ref, q_ref, k_ref, v_ref, do_ref, kt_ref, dqt_ref, dk_ref, dv_ref, dk_sc, dv_sc):
        s = pl.program_id(1)
        i, j = it_ref[s], jt_ref[s]

        @pl.when(s == 0)
        def _():
            dqt_ref[...] = jnp.zeros(dqt_ref.shape, F32)

        @pl.when(i == j)
        def _():
            dk_sc[...] = jnp.zeros(dk_sc.shape, F32)
            dv_sc[...] = jnp.zeros(dv_sc.shape, F32)

        cols = pl.ds(pl.multiple_of(i * bq, bq), bq)

        def sweep(masked):
            def scores(h):
                return (lax.dot_general(k_ref[h], q_ref[h], NT, preferred_element_type=F32),
                        lax.dot_general(v_ref[h], do_ref[h], NT, preferred_element_type=F32))

            def accumulate(h, ptb, dstb):
                dv_sc[h] += jnp.dot(ptb, do_ref[h], preferred_element_type=F32)
                dk_sc[h] += jnp.dot(dstb, q_ref[h], preferred_element_type=F32)
                dqt_ref[h, :, cols] += jnp.dot(kt_ref[h], dstb, preferred_element_type=F32)

            ahead, behind = scores(0), None
            for h in range(ATT_BWD_HEADS):
                st, dpt = ahead
                if h + 1 < ATT_BWD_HEADS:
                    ahead = scores(h + 1)
                if behind is not None:
                    accumulate(*behind)
                if masked:
                    key = lax.broadcasted_iota(jnp.int32, (bq, bq), 0)
                    qry = lax.broadcasted_iota(jnp.int32, (bq, bq), 1)
                    st = jnp.where(key <= qry, st, NEG)
                pt = jnp.exp(st)
                behind = (h, pt.astype(BF16), (pt * dpt).astype(BF16))
            accumulate(*behind)

        @pl.when(i == j)
        def _():
            sweep(True)

        @pl.when(i > j)
        def _():
            sweep(False)

        @pl.when(i == nq - 1)
        def _():
            dk_ref[...] = dk_sc[...]
            dv_ref[...] = dv_sc[...].astype(BF16)

    nh = ATT_BWD_HEADS
    qblk = pl.BlockSpec((nh, bq, LANES), lambda hp, s, it, jt: (hp, it[s], 0))
    kblk = pl.BlockSpec((nh, bq, LANES), lambda hp, s, it, jt: (hp, jt[s], 0))
    grid_spec = pltpu.PrefetchScalarGridSpec(
        num_scalar_prefetch=2, grid=(FOX_HEADS // nh, i_tab.shape[0]),
        in_specs=[qblk, kblk, kblk, qblk, pl.BlockSpec((nh, LANES, bq), lambda hp, s, it, jt: (hp, 0, jt[s]))],
        out_specs=[pl.BlockSpec((nh, LANES, t), lambda hp, s, it, jt: (hp, 0, 0), pipeline_mode=pl.Buffered(1)),
                   kblk, kblk],
        scratch_shapes=[pltpu.VMEM((nh, bq, LANES), F32), pltpu.VMEM((nh, bq, LANES), F32)])
    return pl.pallas_call(body, grid_spec=grid_spec,
                          out_shape=[_sds((FOX_HEADS, LANES, t), F32), _sds((FOX_HEADS, t, LANES), F32),
                                     _sds((FOX_HEADS, t, LANES), BF16)],
                          compiler_params=_cp(ATT_BWD_VMEM_LIMIT), name="attn_bwd")(i_tab, j_tab, qb, kp, vp, dob, kt)


def _attn_unpack(dqt, dkp, dvp):
    t = dkp.shape[1]
    tm = min(LAYOUT_ROW_TILE, t)
    hd = HEAD_DIM

    def body(dqt_ref, dk_ref, dv_ref, o_ref, dc_ref):
        for h in range(FOX_HEADS):
            dq = dqt_ref[h].T
            o_ref[:, h * hd:(h + 1) * hd] = (dq[:, :hd] * (hd ** -0.5)).astype(BF16)
            o_ref[:, FOX_WIDTH + h * hd:FOX_WIDTH + (h + 1) * hd] = dk_ref[h, :, :hd].astype(BF16)
            o_ref[:, 2 * FOX_WIDTH + h * hd:2 * FOX_WIDTH + (h + 1) * hd] = dv_ref[h, :, :hd]
            dc_ref[:, h:h + 1] = dq[:, K_ONE:K_ONE + 1] - dk_ref[h, :, Q_ONE:Q_ONE + 1]

    row3 = pl.BlockSpec((FOX_HEADS, tm, LANES), lambda i: (0, i, 0))
    return pl.pallas_call(
        body, grid=(t // tm,),
        in_specs=[pl.BlockSpec((FOX_HEADS, LANES, tm), lambda i: (0, 0, i)), row3, row3],
        out_specs=[pl.BlockSpec((tm, QKV), lambda i: (i, 0)), pl.BlockSpec((tm, FOX_HEADS), lambda i: (i, 0))],
        out_shape=[_sds((t, QKV), BF16), _sds((t, FOX_HEADS), F32)],
        compiler_params=_cp(), name="attn_unpack")(dqt, dkp, dvp)


def _adamw(parts, w, m, v, name, layer=None, into=None):
    nl, r, c = w.shape
    fits = [cand for cand in [*range(SUBLANES, r, SUBLANES), r] if r % cand == 0 and cand * c * 4 <= ADAMW_BLOCK_BYTES]
    tr = max(fits) if fits else r
    npart = len(parts)
    bc1 = 1.0 - ADAM_B1 ** ADAM_STEP
    bc2 = 1.0 - ADAM_B2 ** ADAM_STEP

    def body(*refs):
        p_refs = refs[:npart]
        w_ref, m_ref, v_ref = refs[npart:npart + 3]
        g_ref, d_ref, nm_ref, nv_ref = refs[-4:]
        sums = []
        for p_ref in p_refs:
            acc = p_ref[0, 0].astype(F32)
            for s in range(1, p_ref.shape[0]):
                acc = acc + p_ref[s, 0].astype(F32)
            sums.append(acc)
        g = sums[0]
        for extra in sums[1:]:
            g = g + extra
        nm = ADAM_B1 * m_ref[0] + (1.0 - ADAM_B1) * g
        nv = ADAM_B2 * v_ref[0] + (1.0 - ADAM_B2) * (g * g)
        m_hat = nm / bc1
        v_hat = nv / bc2
        g_ref[0] = g
        d_ref[0] = -ADAM_LR * (m_hat / (jnp.sqrt(v_hat) + ADAM_EPS) + ADAM_WD * w_ref[0])
        nm_ref[0] = nm
        nv_ref[0] = nv

    first = 0 if layer is None else layer
    blk = pl.BlockSpec((1, tr, c), lambda l, i: (first + l, i, 0))
    extra = [] if into is None else list(into)
    return pl.pallas_call(
        body, grid=(nl if layer is None else 1, r // tr),
        in_specs=[pl.BlockSpec((p.shape[0], 1, tr, c), lambda l, i: (0, l, i, 0)) for p in parts] + [blk, blk, blk]
        + [_ANY_SPEC] * len(extra),
        out_specs=[blk] * 4, out_shape=[_sds(w.shape, F32)] * 4,
        input_output_aliases={npart + 3 + k: k for k in range(len(extra))},
        compiler_params=_cp(), name=name)(*parts, w, m, v, *extra)


def _to_rows(a):
    flat = a.reshape(-1)
    pad = (-flat.shape[0]) % LANES
    if pad:
        flat = jnp.concatenate([flat, jnp.zeros((pad,), flat.dtype)])
    return flat.reshape(-1, LANES)


def _ffn_fwd(xin_ln, xin_b, wi, wo, g, b, layer, target=None):
    t, d = xin_b.shape
    tm = min(FFN_FUSED_ROW_TILE, t)
    hh = HALF_HIDDEN
    rxh, rg, rb = xin_ln

    def body(x_ref, wi_ref, wo_ref, rxh_ref, rg_ref, rb_ref, g_ref, b_ref, *rest):
        gu_ref, h_ref = rest[target is not None:][:2]
        a = x_ref[...]
        for c in range(2):
            gs, us = slice(c * hh, (c + 1) * hh), slice(FFN_HIDDEN + c * hh, FFN_HIDDEN + (c + 1) * hh)
            gate = jnp.dot(a, wi_ref[c], preferred_element_type=F32)
            up = jnp.dot(a, wi_ref[2 + c], preferred_element_type=F32)
            sig = _sigmoid(gate)
            silu = gate * sig
            gu_ref[:, gs] = (up * sig * (1.0 + gate * (1.0 - sig))).astype(BF16)
            gu_ref[:, us] = silu.astype(BF16)
            h_ref[:, gs] = (silu * up).astype(BF16)
        z = ALPHA * (rxh_ref[...] * rg_ref[...] + rb_ref[...]) + jnp.dot(h_ref[...], wo_ref[...], preferred_element_type=F32)
        xhat, rstd = _ln_fwd(z)
        if target is None:
            yb_ref, xh_ref, rs_ref = rest[2:]
            yb_ref[...] = (xhat * g_ref[...] + b_ref[...]).astype(BF16)
            xh_ref[...] = xhat
            rs_ref[...] = rstd
            return
        sq_ref, dz_ref, dg_ref, db_ref = rest[3:]
        first = pl.program_id(0) == 0
        err = xhat * g_ref[...] + b_ref[...] - rest[0][...]
        dz, dg, db = _ln_bwd(err * (1.0 / d), xhat, rstd, g_ref[...])
        dz_ref[...] = dz
        _accumulate(sq_ref, first, jnp.sum(err * err, axis=0, keepdims=True))
        _accumulate(dg_ref, first, dg)
        _accumulate(db_ref, first, db)

    row = lambda i: (i, 0)
    full = pl.BlockSpec((tm, d), row)
    vec = _resident(g.shape)
    acc = pl.BlockSpec((1, d), lambda i: (0, 0))
    in_specs = [full, _resident(wi.shape), _resident(wo.shape), full, vec, vec, vec, vec]
    out_specs = [pl.BlockSpec((tm, 2 * FFN_HIDDEN), row), pl.BlockSpec((tm, FFN_HIDDEN), row)]
    out_shape = [_sds((t, 2 * FFN_HIDDEN), BF16), _sds((t, FFN_HIDDEN), BF16)]
    args = [xin_b, wi, wo, rxh, rg, rb, g, b]
    if target is None:
        out_specs += [full, full, pl.BlockSpec((tm, 1), row)]
        out_shape += [_sds((t, d), BF16), _sds((t, d), F32), _sds((t, 1), F32)]
    else:
        in_specs.append(full)
        args.append(target)
        out_specs += [acc, full, acc, acc]
        out_shape += [_sds((1, d), F32), _sds((t, d), F32), _sds((1, d), F32), _sds((1, d), F32)]
    gu, h, *tail = pl.pallas_call(body, grid=(t // tm,), in_specs=in_specs, out_specs=out_specs, out_shape=out_shape,
                                  compiler_params=_cp(ATT_BWD_VMEM_LIMIT), name=f"ffn_fwd_rows_{layer}")(*args)
    if target is None:
        y_b, xhat, rstd = tail
        return y_b, (xin_b, gu, h, xhat, rstd)
    return tail, (xin_b, gu, h)


def _ffn_bwd(dz, saved, wi, wo, ln_below, layer):
    xin_b, gu, h = saved[:3]
    dgu, *below = _ffn_bwd_rows(dz, wo, gu, wi, ln_below, f"ffn_bwd_rows_{layer}")
    g_out = _mm_tn(h, dz, f"ffn_dw_out_{layer}", tn=D_MODEL, tk=HALF_HIDDEN)
    g_in = _mm_tn(xin_b, dgu, f"ffn_dw_in_{layer}", tn=HALF_HIDDEN, stack_cols=True)
    return below, g_in, g_out.reshape(N_CHIPS, FFN_HIDDEN // N_CHIPS, D_MODEL)


def kernel(x, even_w_in, even_b_f, even_conv_w, even_w_out, odd_w_in, odd_v_ln_g, odd_v_ln_b, odd_w_s, odd_b_s, odd_w_out, mix_ln_g, mix_ln_b, ffn_w_in, ffn_w_out, ffn_ln_g, ffn_ln_b, loss_target, m_even_w_in, m_even_b_f, m_even_conv_w, m_even_w_out, m_odd_w_in, m_odd_v_ln_g, m_odd_v_ln_b, m_odd_w_s, m_odd_b_s, m_odd_w_out, m_mix_ln_g, m_mix_ln_b, m_ffn_w_in, m_ffn_w_out, m_ffn_ln_g, m_ffn_ln_b, v_even_w_in, v_even_b_f, v_even_conv_w, v_even_w_out, v_odd_w_in, v_odd_v_ln_g, v_odd_v_ln_b, v_odd_w_s, v_odd_b_s, v_odd_w_out, v_mix_ln_g, v_mix_ln_b, v_ffn_w_in, v_ffn_w_out, v_ffn_ln_g, v_ffn_ln_b):
    t = x.shape[1]
    d = D_MODEL
    chip = 2 * lax.axis_index("x") + lax.axis_index("y")
    x2d = x[0]
    target = loss_target[0]

    small_shard = jnp.concatenate([odd_v_ln_g.reshape(2, LANES), odd_v_ln_b.reshape(2, LANES),
                                   even_conv_w.reshape(CONV_K, LANES), jnp.zeros((1, LANES), F32)], axis=0)
    first = [jnp.swapaxes(even_w_in[0], 0, 1).astype(BF16)]
    second = [even_w_out[0].astype(BF16), small_shard]
    later = [odd_w_in[0].astype(BF16), odd_w_out[0].astype(BF16), ffn_w_in[0].astype(BF16), ffn_w_in[1].astype(BF16),
             ffn_w_out[0].astype(BF16), ffn_w_out[1].astype(BF16)]
    first_h, first_tok = _split_start(first, "gather4", "gather_first_start")
    second_h, second_tok = _split_start(second, "gather4", "gather_second_start", after=first_tok)
    later_h, later_tok = _split_start(later, "gather4", "gather_later_start", after=second_tok)
    (g_ewi,) = _gathered(first_h, "gather_first_wait", later_tok)
    ewi = g_ewi.reshape(EVEN_IN, d)
    w_even_in = jnp.concatenate([ewi[:QKV], ewi[QKV + FOX_HEADS:],
                                 jnp.pad(ewi[QKV:QKV + FOX_HEADS], ((0, LANES - FOX_HEADS), (0, 0)))], axis=0)
    chunk_id = jnp.arange(GMLP_BLOCK) // CHUNK
    gmask = chunk_id[None, :] <= chunk_id[:, None]
    w_spatial = jnp.where(gmask[None], odd_w_s[0], 0.0).astype(BF16)
    bs_col = odd_b_s[0].T
    b_f_col = even_b_f.reshape(FOX_HEADS, 1)
    ln = lambda p, l: p[l:l + 1]

    qkv, bch, fl, x2d_b = _proj(x2d, w_even_in, [(0, QKV, BF16), (QKV, QKV + BCH, F32), (QKV + BCH, EVEN_IN_PAD, F32)], "even_proj")
    fl3 = fl[:, :FOX_HEADS].T.reshape(FOX_HEADS, t // LANES, LANES).transpose(1, 0, 2)
    c3 = _fgate_fwd(fl3, b_f_col)
    c_rows = c3.transpose(1, 0, 2).reshape(FOX_HEADS, t)
    head_lanes = lambda rows: jnp.pad(rows.T, ((0, 0), (0, LANES - FOX_HEADS)))
    qp, kp, vp, kt, vt = _attn_pack(qkv, head_lanes(c_rows))
    attn, lse = _attn_fwd(qp, kp, vt)
    g_ewo, g_small = _gathered(second_h, "gather_second_wait", attn)
    w_even_out = g_ewo.reshape(d, d)
    v_ln_g = g_small[:, 0:2].reshape(1, d)
    v_ln_b = g_small[:, 2:4].reshape(1, d)
    conv_w = g_small[:, 4:7].transpose(1, 0, 2).reshape(CONV_K, CONV_WIDTH)
    conv, x1_b, xh1, rs1 = _even_out(attn, bch, conv_w, w_even_out, x2d, ln(mix_ln_g, 0), ln(mix_ln_b, 0))
    w_odd_in, g_owo, w_fi0, w_fi1, g_fo0, g_fo1 = _gathered(later_h, "gather_later_wait", x1_b)
    w_odd_out = g_owo.reshape(d, d)
    w_ffn_in = [w_fi0, w_fi1]
    w_ffn_out = [g_fo0.reshape(FFN_HIDDEN, d), g_fo1.reshape(FFN_HIDDEN, d)]
    x2_b, ffn0 = _ffn_fwd((xh1, ln(mix_ln_g, 0), ln(mix_ln_b, 0)), x1_b, w_ffn_in[0], w_ffn_out[0],
                          ln(ffn_ln_g, 0), ln(ffn_ln_b, 0), 0)

    sv_odd, rs_odd, gated, x3_b, xh3, rs3 = _gmlp_fwd(
        x2_b, w_odd_in, v_ln_g, v_ln_b, w_spatial, bs_col, w_odd_out, (ffn0[3], ln(ffn_ln_g, 0), ln(ffn_ln_b, 0)),
        ln(mix_ln_g, 1), ln(mix_ln_b, 1))
    (sq, dz4, d_fg1, d_fb1), ffn1 = _ffn_fwd((xh3, ln(mix_ln_g, 1), ln(mix_ln_b, 1)), x3_b, w_ffn_in[1], w_ffn_out[1],
                                             ln(ffn_ln_g, 1), ln(ffn_ln_b, 1), 1, target=target)

    loss = lax.psum(0.5 / d * jnp.sum(sq), ("x", "y", "c"))
    (dz3, d_mg1, d_mb1), gi_f1, go_f1 = _ffn_bwd(dz4, ffn1, w_ffn_in[1], w_ffn_out[1], (xh3, rs3, ln(mix_ln_g, 1)), 1)

    go_odd = _mm_tn(gated, dz3, "odd_dw_out", tn=d).reshape(N_CHIPS, 1, d // N_CHIPS, d)
    da_odd, dws, dbs_col, d_vg, d_vb, dz2, d_fg0, d_fb0 = _gmlp_bwd(
        dz3, w_odd_out, sv_odd, rs_odd, v_ln_g, v_ln_b, w_spatial, bs_col, w_odd_in,
        (ffn0[3], ffn0[4], ln(ffn_ln_g, 0)))
    gi_odd = _mm_tn(x2_b, da_odd, "odd_dw_in", tn=d // 2, stack_cols=True)[:, None]
    (dz1, d_mg0, d_mb0), gi_f0, go_f0 = _ffn_bwd(dz2, ffn0, w_ffn_in[0], w_ffn_out[0], (xh1, rs1, ln(mix_ln_g, 0)), 0)

    sent_early = [gi_odd, go_odd, gi_f0[:, None], gi_f1[:, None], go_f0[:, None], go_f1[:, None]]
    early_h, early_tok = _split_start(sent_early, "scatter4", "scatter_early_start")
    qb, dob, dconv = _attn_bwd_prep(dz1, w_even_out, attn, qp, head_lanes(lse.reshape(FOX_HEADS, t)), early_tok)
    go_even = _mm_tn_pair(attn, conv, dz1, "even_dw_out").reshape(N_CHIPS, 1, d // N_CHIPS, d)
    dbch, dconv_w8 = _conv_bwd(bch, dconv, conv_w)
    dqkv, dc_col = _attn_unpack(*_attn_bwd(qb, kp, vp, dob, kt))
    dc3 = dc_col.T.reshape(FOX_HEADS, t // LANES, LANES).transpose(1, 0, 2)
    dfl3, d_bf = _fgate_bwd(dc3, fl3, b_f_col)
    dfl = jnp.concatenate([dfl3.transpose(1, 0, 2).reshape(FOX_HEADS, t).T.astype(BF16),
                           jnp.zeros((t, LANES - FOX_HEADS), BF16)], axis=1)

    dws_masked = jnp.where(gmask[None], dws, 0.0)
    rep_names = ["odd_w_s", "odd_b_s", "mix_ln_g", "mix_ln_b", "ffn_ln_g", "ffn_ln_b", "even_b_f"]
    rep_grads = [dws_masked, dbs_col.T, jnp.concatenate([d_mg0, d_mg1]), jnp.concatenate([d_mb0, d_mb1]),
                 jnp.concatenate([d_fg0, d_fg1]), jnp.concatenate([d_fb0, d_fb1]), d_bf.reshape(1, FOX_HEADS)]
    rep_w = [(odd_w_s, m_odd_w_s, v_odd_w_s), (odd_b_s, m_odd_b_s, v_odd_b_s), (mix_ln_g, m_mix_ln_g, v_mix_ln_g),
             (mix_ln_b, m_mix_ln_b, v_mix_ln_b), (ffn_ln_g, m_ffn_ln_g, v_ffn_ln_g), (ffn_ln_b, m_ffn_ln_b, v_ffn_ln_b),
             (even_b_f, m_even_b_f, v_even_b_f)]
    rep_rows = [_to_rows(gr) for gr in rep_grads]
    n_rep = sum(r.shape[0] for r in rep_rows)
    pad_rep = (-n_rep) % SUBLANES
    dconv_w = dconv_w8[:CONV_K].reshape(CONV_K, N_CHIPS, LANES).transpose(1, 0, 2).reshape(N_CHIPS * CONV_K, LANES)
    packed = jnp.concatenate(rep_rows + [jnp.zeros((pad_rep, LANES), F32), d_vg.reshape(SUBLANES, LANES),
                                         d_vb.reshape(SUBLANES, LANES), dconv_w, jnp.zeros((4, LANES), F32)], axis=0)
    small_h, small_tok = _split_start([packed], "gather8", "gather_small_start")

    swap_h, swap_tok = _split_start(_scattered(early_h, "scatter_early_wait", small_tok), "swap2", "swap_early_start")
    dw_qkv = _mm_tn(dqkv, x2d_b, "even_dw_qkv", tn=d, tk=QKV // 2, after=swap_tok)
    dw_bch = _mm_tn(dbch, x2d_b, "even_dw_bch", tn=d, tk=BCH // 2)
    dw_f = _mm_tn(dfl, x2d_b, "even_dw_f", tn=d)
    gi_even = jnp.concatenate([dw_qkv, dw_f[:FOX_HEADS], dw_bch], axis=0).reshape(N_CHIPS, 1, -1, LANES)
    sent_late = [gi_even, go_even]
    late_h, late_tok = _split_start(sent_late, "scatter4", "scatter_late_start")
    grad_x = _mm_back([(dqkv, 0, QKV), (dbch, QKV, QKV + BCH), (dfl, QKV + BCH, EVEN_IN_PAD)], w_even_in, dz1,
                      late_tok, "even_dx")
    mine, theirs = _split_wait(swap_h, "swap_early_wait", grad_x)
    res = {}
    res["odd_w_in"] = _adamw([mine[0], theirs[0]], odd_w_in, m_odd_w_in, v_odd_w_in, "adamw_odd_w_in")
    res["odd_w_out"] = _adamw([mine[1], theirs[1]], odd_w_out, m_odd_w_out, v_odd_w_out, "adamw_odd_w_out")
    for nm, at, (w, m, v) in (("ffn_w_in", 2, (ffn_w_in, m_ffn_w_in, v_ffn_w_in)),
                              ("ffn_w_out", 4, (ffn_w_out, m_ffn_w_out, v_ffn_w_out))):
        upper = _adamw([mine[at + 1], theirs[at + 1]], w, m, v, f"adamw_{nm}_1", layer=1)
        res[nm] = _adamw([mine[at], theirs[at]], w, m, v, f"adamw_{nm}_0", layer=0, into=upper)
    mine_late = _scattered(late_h, "scatter_late_wait", res["ffn_w_out"][0])
    theirs_late = _exchange(mine_late, "swap2", "swap_late")
    rows = lambda a: jnp.swapaxes(a, 1, 2).reshape(1, -1, LANES)
    back = lambda a: jnp.swapaxes(a.reshape(1, EVEN_IN // N_CHIPS, d), 1, 2)
    res["even_w_in"] = [back(o) for o in _adamw([mine_late[0], theirs_late[0]], rows(even_w_in), rows(m_even_w_in),
                                                rows(v_even_w_in), "adamw_even_w_in")]
    res["even_w_out"] = _adamw([mine_late[1], theirs_late[1]], even_w_out, m_even_w_out, v_even_w_out,
                               "adamw_even_w_out")
    (packed,), (gathered,) = _split_wait(small_h, "gather_small_wait", theirs_late[0])
    gathered = lax.dynamic_update_index_in_dim(gathered, packed, 4 * lax.axis_index("x") + 2 * lax.axis_index("y")
                                               + lax.axis_index("c"), 0)

    base = n_rep + pad_rep
    own_rows = jnp.concatenate([
        lax.dynamic_slice_in_dim(gathered, base + 2 * chip, 2, axis=1),
        lax.dynamic_slice_in_dim(gathered, base + SUBLANES + 2 * chip, 2, axis=1),
        lax.dynamic_slice_in_dim(gathered, base + 2 * SUBLANES + CONV_K * chip, CONV_K, axis=1),
        jnp.zeros((N_DEV, 1, LANES), F32)], axis=1)
    small_parts = jnp.concatenate([gathered[:, :base], own_rows], axis=1)[:, None]

    def pack_small(get):
        rows = [_to_rows(get(tw)) for tw in rep_w] + [jnp.zeros((pad_rep, LANES), F32)]
        rows += [get(sh).reshape(-1, LANES) for sh in ((odd_v_ln_g, m_odd_v_ln_g, v_odd_v_ln_g),
                                                       (odd_v_ln_b, m_odd_v_ln_b, v_odd_v_ln_b),
                                                       (even_conv_w, m_even_conv_w, v_even_conv_w))]
        return jnp.concatenate(rows + [jnp.zeros((1, LANES), F32)], axis=0)[None]

    small_out = _adamw([small_parts], pack_small(lambda tw: tw[0]), pack_small(lambda tw: tw[1]),
                       pack_small(lambda tw: tw[2]), "adamw_small")

    def unpack_small(rows3):
        rows = rows3[0]
        out, off = {}, 0
        for nm, (w, _, _), r in zip(rep_names, rep_w, rep_rows):
            out[nm] = rows[off:off + r.shape[0]].reshape(-1)[:w.size].reshape(w.shape)
            off += r.shape[0]
        off += pad_rep
        out["odd_v_ln_g"] = rows[off:off + 2].reshape(odd_v_ln_g.shape)
        out["odd_v_ln_b"] = rows[off + 2:off + 4].reshape(odd_v_ln_b.shape)
        out["even_conv_w"] = rows[off + 4:off + 4 + CONV_K].reshape(even_conv_w.shape)
        return out

    small = [unpack_small(o) for o in small_out]
    order = ["even_w_in", "even_b_f", "even_conv_w", "even_w_out", "odd_w_in", "odd_v_ln_g", "odd_v_ln_b", "odd_w_s",
             "odd_b_s", "odd_w_out", "mix_ln_g", "mix_ln_b", "ffn_w_in", "ffn_w_out", "ffn_ln_g", "ffn_ln_b"]
    outs = [loss, grad_x[None]]
    for kind in range(4):
        for nm in order:
            outs.append(res[nm][kind] if nm in res else small[kind][nm])
    return tuple(outs)
```

```python
import math

import jax
import jax.numpy as jnp
from jax import lax
from jax.experimental import pallas as pl
from jax.experimental.pallas import tpu as pltpu

F32 = jnp.float32
BF16 = jnp.bfloat16

D_MODEL = 1024
FOX_HEADS = 8
HEAD_DIM = 64
FOX_WIDTH = FOX_HEADS * HEAD_DIM
CONV_WIDTH = 512
CONV_K = 3
QKV = 3 * FOX_WIDTH
BCH = 3 * CONV_WIDTH
EVEN_IN = QKV + FOX_HEADS + BCH
EVEN_IN_PAD = QKV + BCH + 128
GMLP_BLOCK = 128
GMLP_GROUPS = 8
CHUNK = 64
FFN_HIDDEN = 2816
HALF_HIDDEN = FFN_HIDDEN // 2
ALPHA = 4.0 ** 0.25
LN_EPS = 1e-5
ADAM_LR = 0.001
ADAM_B1 = 0.9
ADAM_B2 = 0.999
ADAM_EPS = 1e-08
ADAM_WD = 0.01
ADAM_STEP = 10
N_CHIPS = 4
N_DEV = 8
LANES = 128
SUBLANES = 8
ROW_TILE = 512
LAYOUT_ROW_TILE = 1024
FFN_FUSED_ROW_TILE = 512
REDUCE_TILE = 2048
ATT_BLOCK = 512
ATT_FWD_HEADS = 8
ATT_BWD_HEADS = 8
ADAMW_BLOCK_BYTES = 2 ** 20
VMEM_LIMIT = 56 * 2 ** 20
ATT_BWD_VMEM_LIMIT = 60 * 2 ** 20
NEG = -1e30
MESH = pl.DeviceIdType.MESH
HIGHEST = lax.Precision.HIGHEST
Q_C, Q_ONE, Q_LSE = 64, 67, 70
K_ONE, K_C, K_ONE2 = 64, 67, 70
V_ONE = 64
DO_DELTA = 65
NT = (((1,), (1,)), ((), ()))
TN = (((0,), (0,)), ((), ()))


def _cp(limit=VMEM_LIMIT):
    return pltpu.CompilerParams(vmem_limit_bytes=limit)


def _resident(shape):
    zeros = (0,) * len(shape)
    return pl.BlockSpec(shape, lambda *_: zeros, pipeline_mode=pl.Buffered(1))


def _sds(shape, dtype):
    return jax.ShapeDtypeStruct(tuple(shape), dtype)


_MASKS = {
    "gather4": [(1, 0, 0), (0, 1, 0), (1, 1, 0)],
    "scatter4": [(1, 0, 0), (0, 1, 0), (1, 1, 0)],
    "swap2": [(0, 0, 1)],
    "gather8": [(0, 0, 1), (0, 1, 0), (0, 1, 1), (1, 0, 0), (1, 0, 1), (1, 1, 0), (1, 1, 1)],
}


def _exchange(arrs, mode, name):
    n = len(arrs)
    masks = _MASKS[mode]
    npeer = len(masks)
    lead = {"gather4": N_CHIPS, "gather8": N_DEV}.get(mode)
    out_shapes = [_sds(((lead,) if lead else ()) + a.shape, a.dtype) for a in arrs]

    def body(*refs):
        ins, outs = refs[:n], refs[n:2 * n]
        send_sems, recv_sems, loc_sems = refs[2 * n:]
        x, y, c = lax.axis_index("x"), lax.axis_index("y"), lax.axis_index("c")
        chip, dev = 2 * x + y, 4 * x + 2 * y + c
        sends, recvs, locs = [], [], []
        for k in range(n):
            if mode == "gather4":
                locs.append(pltpu.make_async_copy(ins[k], outs[k].at[chip], loc_sems.at[k]))
            elif mode == "scatter4":
                locs.append(pltpu.make_async_copy(ins[k].at[chip], outs[k].at[chip], loc_sems.at[k]))
            elif mode == "gather8":
                locs.append(pltpu.make_async_copy(ins[k], outs[k].at[dev], loc_sems.at[k]))
        for cp in locs:
            cp.start()
        for k in range(n):
            for j, (dx, dy, dc) in enumerate(masks):
                px = 1 - x if dx else x
                py = 1 - y if dy else y
                pc = 1 - c if dc else c
                pchip, pdev = 2 * px + py, 4 * px + 2 * py + pc
                if mode == "gather4":
                    src, dst, land = ins[k], outs[k].at[chip], outs[k].at[pchip]
                elif mode == "scatter4":
                    src, dst, land = ins[k].at[pchip], outs[k].at[chip], outs[k].at[pchip]
                elif mode == "swap2":
                    src, dst, land = ins[k], outs[k], outs[k]
                else:
                    src, dst, land = ins[k], outs[k].at[dev], outs[k].at[pdev]
                s = k * npeer + j
                kw = dict(send_sem=send_sems.at[s], recv_sem=recv_sems.at[s], device_id=(px, py, pc),
                          device_id_type=MESH)
                cp = pltpu.make_async_remote_copy(src_ref=src, dst_ref=dst, **kw)
                cp.start()
                sends.append(cp)
                recvs.append(pltpu.make_async_remote_copy(src_ref=src, dst_ref=land, **kw))
        for cp in recvs:
            cp.wait_recv()
        for cp in sends:
            cp.wait_send()
        for cp in locs:
            cp.wait()

    any_spec = pl.BlockSpec(memory_space=pl.ANY)
    outs = pl.pallas_call(
        body,
        out_shape=out_shapes,
        in_specs=[any_spec] * n,
        out_specs=[any_spec] * n,
        scratch_shapes=[pltpu.SemaphoreType.DMA((n * npeer,)), pltpu.SemaphoreType.DMA((n * npeer,)),
                        pltpu.SemaphoreType.DMA((max(n, 1),))],
        name=name,
    )(*arrs)
    return list(outs)


_HBM_SPEC = pl.BlockSpec(memory_space=pltpu.HBM)
_SEM_SPEC = pl.BlockSpec(memory_space=pltpu.SEMAPHORE)
_ANY_SPEC = pl.BlockSpec(memory_space=pl.ANY)
_EFFECT = pltpu.SideEffectType.DATAFLOW_SIDE_EFFECTING


def _split_copies(mode, ins, lands, send_sems, recv_sems):
    x, y, c = lax.axis_index("x"), lax.axis_index("y"), lax.axis_index("c")
    chip, dev = 2 * x + y, 4 * x + 2 * y + c
    masks = _MASKS[mode]
    out = []
    for k in range(len(ins)):
        for j, (dx, dy, dc) in enumerate(masks):
            px = 1 - x if dx else x
            py = 1 - y if dy else y
            pc = 1 - c if dc else c
            pchip, pdev = 2 * px + py, 4 * px + 2 * py + pc
            if mode == "gather4":
                src, dst, land = ins[k], lands[k].at[chip], lands[k].at[pchip]
            elif mode == "scatter4":
                src, dst, land = ins[k].at[pchip], lands[k].at[chip], lands[k].at[pchip]
            elif mode == "swap2":
                src, dst, land = ins[k], lands[k], lands[k]
            else:
                src, dst, land = ins[k], lands[k].at[dev], lands[k].at[pdev]
            s = k * len(masks) + j
            kw = dict(send_sem=send_sems.at[s], recv_sem=recv_sems.at[s], device_id=(px, py, pc), device_id_type=MESH)
            out.append((pltpu.make_async_remote_copy(src_ref=src, dst_ref=dst, **kw),
                        pltpu.make_async_remote_copy(src_ref=src, dst_ref=land, **kw)))
    return out


def _split_start(arrs, mode, name, after=None):
    n = len(arrs)
    nsem = n * len(_MASKS[mode])
    lead = {"gather4": (N_CHIPS,), "gather8": (N_DEV,)}.get(mode, ())
    land_shapes = [lead + a.shape for a in arrs]

    def body(*refs):
        ins, lands = refs[:n], refs[n:2 * n]
        outs = refs[2 * n + (after is not None):]
        for start, _ in _split_copies(mode, ins, lands, outs[0], outs[1]):
            start.start()
        outs[-1][...] = jnp.zeros(outs[-1].shape, F32)

    srcs = [pltpu.with_memory_space_constraint(a, pltpu.HBM) for a in arrs]
    empties = [pltpu.with_memory_space_constraint(lax.empty(s, a.dtype), pltpu.HBM) for s, a in zip(land_shapes, arrs)]
    res = pl.pallas_call(
        body, name=name,
        out_shape=(pltpu.SemaphoreType.DMA((nsem,)), pltpu.SemaphoreType.DMA((nsem,)),
                   *[pltpu.HBM(a.shape, a.dtype) for a in arrs],
                   *[pltpu.HBM(s, a.dtype) for s, a in zip(land_shapes, arrs)],
                   _sds((SUBLANES, LANES), F32)),
        in_specs=[_HBM_SPEC] * (2 * n) + ([_ANY_SPEC] if after is not None else []),
        out_specs=(_SEM_SPEC, _SEM_SPEC, *[_HBM_SPEC] * (2 * n), pl.BlockSpec(memory_space=pltpu.VMEM)),
        input_output_aliases={k: 2 + k for k in range(2 * n)},
        compiler_params=pltpu.CompilerParams(has_side_effects=_EFFECT),
    )(*srcs, *empties, *([after] if after is not None else []))
    return dict(mode=mode, n=n, sems=res[:2], bufs=res[2:2 + 2 * n]), res[-1]


def _split_wait(handle, name, after):
    n, mode = handle["n"], handle["mode"]

    def body(*refs):
        ins, lands = refs[:n], refs[n:2 * n]
        send_sems, recv_sems = refs[2 * n], refs[2 * n + 1]
        for _, arrival in _split_copies(mode, ins, lands, send_sems, recv_sems):
            arrival.wait_send()
            arrival.wait_recv()

    bufs = handle["bufs"]
    res = pl.pallas_call(
        body, name=name,
        out_shape=tuple(pltpu.HBM(b.shape, b.dtype) for b in bufs),
        in_specs=[_HBM_SPEC] * (2 * n) + [_SEM_SPEC, _SEM_SPEC, _ANY_SPEC],
        out_specs=tuple([_HBM_SPEC] * (2 * n)),
        input_output_aliases={k: k for k in range(2 * n)},
        compiler_params=pltpu.CompilerParams(has_side_effects=_EFFECT),
    )(*bufs, *handle["sems"], after)
    return list(res[:n]), list(res[n:])


def _with_own(landed, own):
    chip = 2 * lax.axis_index("x") + lax.axis_index("y")
    return lax.dynamic_update_index_in_dim(landed, own, chip, 0)


def _gathered(handle, name, after):
    sent, landed = _split_wait(handle, name, after)
    return [_with_own(g, own) for g, own in zip(landed, sent)]


def _scattered(handle, name, after):
    chip = 2 * lax.axis_index("x") + lax.axis_index("y")
    sent, landed = _split_wait(handle, name, after)
    return [_with_own(r, lax.dynamic_index_in_dim(g, chip, 0, keepdims=False)) for r, g in zip(landed, sent)]


def _sigmoid(x):
    return 0.5 * jnp.tanh(0.5 * x) + 0.5


def _log_sigmoid(x):
    e = jnp.exp(-jnp.abs(x))
    log1p = jnp.where(e < 1e-2, e * (1.0 - e * (0.5 - e * (1.0 / 3.0))), jnp.log(1.0 + e))
    return jnp.minimum(x, 0.0) - log1p


def _ln_fwd(z):
    mu = jnp.mean(z, axis=-1, keepdims=True)
    zc = z - mu
    var = jnp.mean(zc * zc, axis=-1, keepdims=True)
    rstd = lax.rsqrt(var + LN_EPS)
    return zc * rstd, rstd


def _ln_bwd(dy, xhat, rstd, g):
    dxh = dy * g
    m1 = jnp.mean(dxh, axis=-1, keepdims=True)
    m2 = jnp.mean(dxh * xhat, axis=-1, keepdims=True)
    dz = rstd * (dxh - m1 - xhat * m2)
    return dz, jnp.sum(dy * xhat, axis=0, keepdims=True), jnp.sum(dy, axis=0, keepdims=True)


def _shift_down(z, halo):
    r = lax.broadcasted_iota(jnp.int32, z.shape, 0)
    z1 = jnp.where(r == 0, halo[7:8, :], pltpu.roll(z, 1, 0))
    z2 = jnp.where(r == 0, halo[6:7, :], jnp.where(r == 1, halo[7:8, :], pltpu.roll(z, 2, 0)))
    return z1, z2


def _shift_up(z, halo):
    n = z.shape[0]
    r = lax.broadcasted_iota(jnp.int32, z.shape, 0)
    z1 = jnp.where(r == n - 1, halo[0:1, :], pltpu.roll(z, n - 1, 0))
    z2 = jnp.where(r == n - 1, halo[1:2, :], jnp.where(r == n - 2, halo[0:1, :], pltpu.roll(z, n - 2, 0)))
    return z1, z2


def _triangle_ones(prefix):
    r = lax.broadcasted_iota(jnp.int32, (LANES, LANES), 0)
    c = lax.broadcasted_iota(jnp.int32, (LANES, LANES), 1)
    return ((r <= c) if prefix else (r >= c)).astype(F32)


def _same_head_chunks(rows, earlier):
    r = lax.broadcasted_iota(jnp.int32, (rows, rows), 0)
    c = lax.broadcasted_iota(jnp.int32, (rows, rows), 1)
    same = r % FOX_HEADS == c % FOX_HEADS
    return jnp.logical_and(same, (c < r) if earlier else (c > r)).astype(F32)


def _accumulate(ref, first, value):
    @pl.when(first)
    def _():
        ref[...] = value

    @pl.when(jnp.logical_not(first))
    def _():
        ref[...] += value


def _proj(x, wt, splits, name):
    t, k = x.shape
    tm = min(ROW_TILE, t)

    def body(x_ref, w_ref, *outs):
        a = x_ref[...].astype(BF16)
        for (lo, hi, dt), o in zip(splits, outs):
            o[...] = lax.dot_general(a, w_ref[lo:hi, :], NT, preferred_element_type=F32).astype(dt)
        outs[-1][...] = a

    row = lambda i: (i, 0)
    return pl.pallas_call(
        body, grid=(t // tm,),
        in_specs=[pl.BlockSpec((tm, k), row), _resident(wt.shape)],
        out_specs=[pl.BlockSpec((tm, hi - lo), row) for lo, hi, _ in splits] + [pl.BlockSpec((tm, k), row)],
        out_shape=[_sds((t, hi - lo), dt) for lo, hi, dt in splits] + [_sds((t, k), BF16)],
        compiler_params=_cp(), name=name)(x, wt)


def _fgate_fwd(fl3, b_f):
    nc = fl3.shape[0]
    rows = nc * FOX_HEADS

    def body(f_ref, b_ref, c_ref):
        within = jnp.dot(_log_sigmoid(f_ref[...] + b_ref[...]), _triangle_ones(True), precision=HIGHEST,
                         preferred_element_type=F32)
        totals = jnp.broadcast_to(within[:, LANES - 1:LANES], within.shape)
        c_ref[...] = within + jnp.dot(_same_head_chunks(rows, earlier=True), totals, precision=HIGHEST,
                                      preferred_element_type=F32)

    c2 = pl.pallas_call(body, out_shape=_sds((rows, LANES), F32), name="fgate_fwd")(
        fl3.reshape(rows, LANES), jnp.tile(b_f, (nc, 1)))
    return c2.reshape(fl3.shape)


def _fgate_bwd(dc3, fl3, b_f):
    nc = fl3.shape[0]
    rows = nc * FOX_HEADS

    def body(dc_ref, f_ref, b_ref, df_ref, db_ref):
        within = jnp.dot(dc_ref[...], _triangle_ones(False), precision=HIGHEST, preferred_element_type=F32)
        totals = jnp.broadcast_to(within[:, 0:1], within.shape)
        dlf = within + jnp.dot(_same_head_chunks(rows, earlier=False), totals, precision=HIGHEST,
                               preferred_element_type=F32)
        df = dlf * (1.0 - _sigmoid(f_ref[...] + b_ref[...]))
        df_ref[...] = df
        head = lax.broadcasted_iota(jnp.int32, (FOX_HEADS, rows), 0)
        row = lax.broadcasted_iota(jnp.int32, (FOX_HEADS, rows), 1)
        of_head = (row % FOX_HEADS == head).astype(F32)
        per_row = jnp.broadcast_to(jnp.sum(df, axis=1, keepdims=True), df.shape)
        db_ref[...] = jnp.dot(of_head, per_row, precision=HIGHEST, preferred_element_type=F32)[:, 0:1]

    df2, db = pl.pallas_call(body, out_shape=[_sds((rows, LANES), F32), _sds((FOX_HEADS, 1), F32)], name="fgate_bwd")(
        dc3.reshape(rows, LANES), fl3.reshape(rows, LANES), jnp.tile(b_f, (nc, 1)))
    return df2.reshape(fl3.shape), db


def _split3(c):
    hi = c.astype(BF16).astype(F32)
    mid = (c - hi).astype(BF16).astype(F32)
    lo = (c - hi - mid).astype(BF16).astype(F32)
    return hi, mid, lo


PIECE_ONE = 3 * FOX_HEADS


def _piece_rows(values):
    hi, mid, lo = _split3(values)
    lane = lax.broadcasted_iota(jnp.int32, values.shape, 1)
    row = hi + pltpu.roll(mid, FOX_HEADS, 1) + pltpu.roll(lo, 2 * FOX_HEADS, 1) + jnp.where(lane == PIECE_ONE, 1.0, 0.0)
    return row.astype(BF16)


def _piece_selector(start, sign, ones=()):
    sel = [[0.0] * FOX_WIDTH for _ in range(LANES)]
    for h in range(FOX_HEADS):
        for n in range(3):
            sel[n * FOX_HEADS + h][h * HEAD_DIM + start - HEAD_DIM + n] = sign
        for lane in ones:
            sel[PIECE_ONE][h * HEAD_DIM + lane - HEAD_DIM] = 1.0
    return jnp.asarray(sel, BF16)


def _attn_pack(qkv, c_pad):
    t = qkv.shape[0]
    tm = min(LAYOUT_ROW_TILE, t)
    hd = HEAD_DIM
    sel_q = _piece_selector(Q_C, 1.0, range(Q_ONE, Q_ONE + 3))
    sel_k = _piece_selector(K_C, -1.0, [*range(K_ONE, K_ONE + 3), *range(K_ONE2, K_ONE2 + 3)])
    sel_v = _piece_selector(HEAD_DIM, 0.0, range(V_ONE, V_ONE + 4))

    def body(x_ref, c_ref, sq_ref, sk_ref, sv_ref, qp_ref, kp_ref, vp_ref, kt_ref, vt_ref):
        pieces = _piece_rows(c_ref[...])
        q_extra = jnp.dot(pieces, sq_ref[...], preferred_element_type=F32).astype(BF16)
        k_extra = jnp.dot(pieces, sk_ref[...], preferred_element_type=F32).astype(BF16)
        v_extra = jnp.dot(pieces, sv_ref[...], preferred_element_type=F32).astype(BF16)
        for h in range(FOX_HEADS):
            hs = slice(h * hd, (h + 1) * hd)
            qp_ref[h, :, :hd] = (x_ref[:, hs].astype(F32) * (hd ** -0.5)).astype(BF16)
            qp_ref[h, :, hd:] = q_extra[:, hs]
            kp_ref[h, :, :hd] = x_ref[:, FOX_WIDTH + h * hd:FOX_WIDTH + (h + 1) * hd]
            kp_ref[h, :, hd:] = k_extra[:, hs]
            vp_ref[h, :, :hd] = x_ref[:, 2 * FOX_WIDTH + h * hd:2 * FOX_WIDTH + (h + 1) * hd]
            vp_ref[h, :, hd:] = v_extra[:, hs]
            kt_ref[h] = kp_ref[h].T
            vt_ref[h] = vp_ref[h].T

    row3 = pl.BlockSpec((FOX_HEADS, tm, LANES), lambda i: (0, i, 0))
    col3 = pl.BlockSpec((FOX_HEADS, LANES, tm), lambda i: (0, 0, i))
    sel = _resident(sel_q.shape)
    return pl.pallas_call(
        body, grid=(t // tm,),
        in_specs=[pl.BlockSpec((tm, QKV), lambda i: (i, 0)), pl.BlockSpec((tm, LANES), lambda i: (i, 0)), sel, sel, sel],
        out_specs=[row3, row3, row3, col3, col3],
        out_shape=[_sds((FOX_HEADS, t, LANES), BF16)] * 3 + [_sds((FOX_HEADS, LANES, t), BF16)] * 2,
        compiler_params=_cp(), name="attn_pack")(qkv, c_pad, sel_q, sel_k, sel_v)


def _triangle(nq, key_major):
    if key_major:
        pairs = [(i, j) for j in range(nq) for i in range(j, nq)]
    else:
        pairs = [(i, j) for i in range(nq) for j in range(i + 1)]
    return jnp.asarray([p[0] for p in pairs], jnp.int32), jnp.asarray([p[1] for p in pairs], jnp.int32)


def _attn_fwd(qp, kp, vt):
    t = qp.shape[1]
    bq = min(ATT_BLOCK, t)
    nq = t // bq
    nh = ATT_FWD_HEADS
    i_tab, j_tab = _triangle(nq, key_major=False)

    def body(it_ref, jt_ref, q_ref, k_ref, vt_ref, o_ref, lse_ref, m_sc, acc_sc):
        s = pl.program_id(1)
        i, j = it_ref[s], jt_ref[s]

        @pl.when(j == 0)
        def _():
            m_sc[...] = jnp.full(m_sc.shape, NEG, F32)
            acc_sc[...] = jnp.zeros(acc_sc.shape, F32)

        def sweep(masked):
            scores = lambda h: lax.dot_general(k_ref[h], q_ref[h], NT, preferred_element_type=F32)

            def accumulate(h, pt, rescale):
                acc_sc[h] = rescale * acc_sc[h] + jnp.dot(vt_ref[h], pt, preferred_element_type=F32)

            ahead, behind = scores(0), None
            for h in range(nh):
                st = ahead
                if h + 1 < nh:
                    ahead = scores(h + 1)
                if behind is not None:
                    accumulate(*behind)
                if masked:
                    key = lax.broadcasted_iota(jnp.int32, (bq, bq), 0)
                    qry = lax.broadcasted_iota(jnp.int32, (bq, bq), 1)
                    st = jnp.where(key <= qry, st, NEG)
                m_prev = m_sc[h]
                m_new = jnp.maximum(m_prev, jnp.max(st, axis=0, keepdims=True))
                behind = (h, jnp.exp(st - m_new).astype(BF16), jnp.exp(m_prev - m_new))
                m_sc[h] = m_new
            accumulate(*behind)

        @pl.when(j < i)
        def _():
            sweep(False)

        @pl.when(j == i)
        def _():
            sweep(True)
            for h in range(nh):
                acc = acc_sc[h]
                denom = acc[V_ONE:V_ONE + 1, :]
                o_ref[:, h * HEAD_DIM:(h + 1) * HEAD_DIM] = (acc[:HEAD_DIM, :] / denom).T.astype(BF16)
                lse_ref[h] = m_sc[h] + jnp.log(denom)

    grid_spec = pltpu.PrefetchScalarGridSpec(
        num_scalar_prefetch=2, grid=(FOX_HEADS // nh, i_tab.shape[0]),
        in_specs=[pl.BlockSpec((nh, bq, LANES), lambda hp, s, it, jt: (hp, it[s], 0)),
                  pl.BlockSpec((nh, bq, LANES), lambda hp, s, it, jt: (hp, jt[s], 0)),
                  pl.BlockSpec((nh, LANES, bq), lambda hp, s, it, jt: (hp, 0, jt[s]))],
        out_specs=[pl.BlockSpec((bq, nh * HEAD_DIM), lambda hp, s, it, jt: (it[s], hp)),
                   pl.BlockSpec((nh, 1, bq), lambda hp, s, it, jt: (hp, 0, it[s]))],
        scratch_shapes=[pltpu.VMEM((nh, 1, bq), F32), pltpu.VMEM((nh, LANES, bq), F32)])
    return pl.pallas_call(body, grid_spec=grid_spec,
                          out_shape=[_sds((t, FOX_WIDTH), BF16), _sds((FOX_HEADS, 1, t), F32)],
                          compiler_params=_cp(), name="attn_fwd")(i_tab, j_tab, qp, kp, vt)


def _even_out(attn, bch, conv_w, w_out, x, g, b):
    t, d = x.shape
    tm = min(ROW_TILE, t)
    halo_blocks = tm // SUBLANES
    cw = CONV_WIDTH

    def body(a_ref, cur_ref, prev_ref, cw_ref, wo_ref, x_ref, g_ref, b_ref, conv_ref, yb_ref, xh_ref, rs_ref):
        i = pl.program_id(0)
        z = cur_ref[:, cw:2 * cw] * cur_ref[:, 2 * cw:]
        zp = jnp.where(i == 0, 0.0, prev_ref[:, cw:2 * cw] * prev_ref[:, 2 * cw:])
        z1, z2 = _shift_down(z, zp)
        conv = (cur_ref[:, :cw] * (cw_ref[0:1, :] * z2 + cw_ref[1:2, :] * z1 + cw_ref[2:3, :] * z)).astype(BF16)
        conv_ref[...] = conv
        pre = (ALPHA * x_ref[...] + jnp.dot(a_ref[...], wo_ref[:FOX_WIDTH, :], preferred_element_type=F32)
               + jnp.dot(conv, wo_ref[FOX_WIDTH:, :], preferred_element_type=F32))
        xhat, rstd = _ln_fwd(pre)
        yb_ref[...] = (xhat * g_ref[...] + b_ref[...]).astype(BF16)
        xh_ref[...] = xhat
        rs_ref[...] = rstd

    row = lambda i: (i, 0)
    full, half = pl.BlockSpec((tm, d), row), pl.BlockSpec((tm, cw), row)
    return pl.pallas_call(
        body, grid=(t // tm,),
        in_specs=[half, pl.BlockSpec((tm, BCH), row),
                  pl.BlockSpec((SUBLANES, BCH), lambda i: (jnp.maximum(i * halo_blocks - 1, 0), 0)),
                  _resident(conv_w.shape), _resident(w_out.shape), full, _resident(g.shape), _resident(b.shape)],
        out_specs=[half, full, full, pl.BlockSpec((tm, 1), row)],
        out_shape=[_sds((t, cw), BF16), _sds((t, d), BF16), _sds((t, d), F32), _sds((t, 1), F32)],
        compiler_params=_cp(), name="even_out")(attn, bch, bch, conv_w, w_out, x, g, b)


def _gmlp_fwd(x, w_in, vg, vb, wm, bs_col, w_out, res_ln, g, b):
    t, d = x.shape
    tm = min(ROW_TILE, t)
    gb = GMLP_BLOCK
    rxh, rg, rb = res_ln

    def body(x_ref, w_ref, vg_ref, vb_ref, wm_ref, bs_ref, wo_ref, rxh_ref, rg_ref, rb_ref, g_ref, b_ref,
             sv_ref, rs_ref, o_ref, yb_ref, xh_ref, rsy_ref, a_sc):
        xb = x_ref[...].astype(BF16)
        nc = w_ref.shape[2]
        for j in range(w_ref.shape[0]):
            a_sc[:, j * nc:(j + 1) * nc] = jnp.dot(xb, w_ref[j], preferred_element_type=F32)
        halves = []
        for half in range(2):
            a = a_sc[:, half * d:(half + 1) * d]
            cdf = 0.5 * (1.0 + lax.erf(a * (2.0 ** -0.5)))
            halves.append(a * cdf)
            slope = cdf + a * (jnp.exp(-0.5 * a * a) * (1.0 / math.sqrt(2.0 * math.pi)))
            sv_ref[:, (2 * half + 1) * d:(2 * half + 2) * d] = slope.astype(BF16)
        u = halves[0]
        vhat, rstd = _ln_fwd(halves[1])
        sv_ref[:, :d] = u.astype(BF16)
        sv_ref[:, 2 * d:3 * d] = vhat.astype(BF16)
        rs_ref[...] = rstd
        vln = (vhat * vg_ref[...] + vb_ref[...]).astype(BF16)
        for blk in range(tm // gb):
            rs = slice(blk * gb, (blk + 1) * gb)
            for gi in range(GMLP_GROUPS):
                cs = slice(gi * gb, (gi + 1) * gb)
                s = jnp.dot(wm_ref[gi], vln[rs, cs], preferred_element_type=F32) + bs_ref[:, gi:gi + 1]
                o_ref[rs, cs] = (u[rs, cs] * s).astype(BF16)
        z = ALPHA * (rxh_ref[...] * rg_ref[...] + rb_ref[...]) + jnp.dot(o_ref[...], wo_ref[...], preferred_element_type=F32)
        xhat, rstd_y = _ln_fwd(z)
        yb_ref[...] = (xhat * g_ref[...] + b_ref[...]).astype(BF16)
        xh_ref[...] = xhat
        rsy_ref[...] = rstd_y

    row = lambda i: (i, 0)
    full, col, vec = pl.BlockSpec((tm, d), row), pl.BlockSpec((tm, 1), row), _resident(g.shape)
    return pl.pallas_call(
        body, grid=(t // tm,),
        in_specs=[full, _resident(w_in.shape), _resident(vg.shape), _resident(vb.shape),
                  _resident(wm.shape), _resident(bs_col.shape), _resident(w_out.shape), full, vec, vec, vec, vec],
        out_specs=[pl.BlockSpec((tm, 4 * d), row), col, full, full, full, col],
        out_shape=[_sds((t, 4 * d), BF16), _sds((t, 1), F32), _sds((t, d), BF16), _sds((t, d), BF16), _sds((t, d), F32),
                   _sds((t, 1), F32)],
        scratch_shapes=[pltpu.VMEM((tm, 2 * d), F32)],
        compiler_params=_cp(), name="gmlp_fwd")(x, w_in, vg, vb, wm, bs_col, w_out, rxh, rg, rb, g, b)


def _mm_back(pairs, wt, res, after, name):
    t = pairs[0][0].shape[0]
    k = wt.shape[1]
    tm = min(ROW_TILE, t)
    n = len(pairs)

    def body(after_ref, *refs):
        a_refs, w_ref, res_ref, o_ref = refs[:n], refs[n], refs[n + 1], refs[n + 2]
        dx = ALPHA * res_ref[...]
        for a_ref, (_, lo, hi) in zip(a_refs, pairs):
            dx = dx + jnp.dot(a_ref[...].astype(BF16), w_ref[lo:hi, :], preferred_element_type=F32)
        o_ref[...] = dx

    row = lambda i: (i, 0)
    return pl.pallas_call(
        body, grid=(t // tm,),
        in_specs=[_ANY_SPEC] + [pl.BlockSpec((tm, a.shape[1]), row) for a, _, _ in pairs]
        + [_resident(wt.shape), pl.BlockSpec((tm, k), row)],
        out_specs=pl.BlockSpec((tm, k), row), out_shape=_sds((t, k), F32),
        compiler_params=_cp(), name=name)(after, *[a for a, _, _ in pairs], wt, res)


def _mm_tn(a, b, name, *, tn, tk=None, tt=None, stack_cols=False, out_dtype=BF16, after=None):
    t, k = a.shape
    n = b.shape[1]
    tk = k if tk is None else tk
    tt = min(REDUCE_TILE if tt is None else tt, t)
    nt = t // tt

    def body(a_ref, b_ref, *rest):
        o_ref, acc_ref = rest[after is not None:]
        s = pl.program_id(2)
        part = lax.dot_general(a_ref[...].astype(BF16), b_ref[...].astype(BF16), TN, preferred_element_type=F32)
        _accumulate(acc_ref, s == 0, part)

        @pl.when(s == nt - 1)
        def _():
            o_ref[...] = acc_ref[...].astype(out_dtype).reshape(o_ref.shape)

    if stack_cols:
        assert tk == k
        out_spec = pl.BlockSpec((1, k, tn), lambda kk, j, s: (j, 0, 0))
        out_shape = _sds((n // tn, k, tn), out_dtype)
    else:
        out_spec = pl.BlockSpec((tk, tn), lambda kk, j, s: (kk, j))
        out_shape = _sds((k, n), out_dtype)
    return pl.pallas_call(
        body, grid=(k // tk, n // tn, nt),
        in_specs=[pl.BlockSpec((tt, tk), lambda kk, j, s: (s, kk)), pl.BlockSpec((tt, tn), lambda kk, j, s: (s, j))]
        + ([_ANY_SPEC] if after is not None else []),
        out_specs=out_spec, out_shape=out_shape,
        scratch_shapes=[pltpu.VMEM((tk, tn), F32)],
        compiler_params=_cp(), name=name)(a, b, *([after] if after is not None else []))


def _mm_tn_pair(a1, a2, b, name):
    t, k = a1.shape
    n = b.shape[1]
    tt = min(REDUCE_TILE, t)
    nt = t // tt

    def body(a1_ref, a2_ref, b_ref, o_ref, acc_ref):
        s = pl.program_id(0)
        bb = b_ref[...].astype(BF16)
        part = jnp.concatenate([lax.dot_general(a_ref[...].astype(BF16), bb, TN, preferred_element_type=F32)
                                for a_ref in (a1_ref, a2_ref)], axis=0)
        _accumulate(acc_ref, s == 0, part)

        @pl.when(s == nt - 1)
        def _():
            o_ref[...] = acc_ref[...].astype(BF16)

    rows = pl.BlockSpec((tt, k), lambda s: (s, 0))
    return pl.pallas_call(
        body, grid=(nt,),
        in_specs=[rows, rows, pl.BlockSpec((tt, n), lambda s: (s, 0))],
        out_specs=pl.BlockSpec((2 * k, n), lambda s: (0, 0)), out_shape=_sds((2 * k, n), BF16),
        scratch_shapes=[pltpu.VMEM((2 * k, n), F32)],
        compiler_params=_cp(), name=name)(a1, a2, b)


def _ffn_bwd_rows(dz, wo, gu, wi, ln_below, name):
    t, d = dz.shape
    tm = min(FFN_FUSED_ROW_TILE, t)
    hh = HALF_HIDDEN
    xhat, rstd, g = ln_below

    def body(dz_ref, wo_ref, gu_ref, wi_ref, xh_ref, rs_ref, g_ref, dgu_ref, dzb_ref, dg_ref, db_ref):
        first = pl.program_id(0) == 0
        a = dz_ref[...].astype(BF16)
        for c in range(2):
            gs, us = slice(c * hh, (c + 1) * hh), slice(FFN_HIDDEN + c * hh, FFN_HIDDEN + (c + 1) * hh)
            dh = lax.dot_general(a, wo_ref[gs, :], NT, preferred_element_type=F32)
            dgu_ref[:, gs] = (dh * gu_ref[:, gs].astype(F32)).astype(BF16)
            dgu_ref[:, us] = (dh * gu_ref[:, us].astype(F32)).astype(BF16)
        dx = ALPHA * dz_ref[...]
        for j in range(wi_ref.shape[0]):
            dx = dx + lax.dot_general(dgu_ref[:, j * hh:(j + 1) * hh], wi_ref[j], NT, preferred_element_type=F32)
        dzb, dg, db = _ln_bwd(dx, xh_ref[...], rs_ref[...], g_ref[...])
        dzb_ref[...] = dzb
        _accumulate(dg_ref, first, dg)
        _accumulate(db_ref, first, db)

    row = lambda i: (i, 0)
    wide, full = pl.BlockSpec((tm, 2 * FFN_HIDDEN), row), pl.BlockSpec((tm, d), row)
    vec = pl.BlockSpec((1, d), lambda i: (0, 0))
    return pl.pallas_call(
        body, grid=(t // tm,),
        in_specs=[full, _resident(wo.shape), wide, _resident(wi.shape), full, pl.BlockSpec((tm, 1), row),
                  _resident(g.shape)],
        out_specs=[wide, full, vec, vec],
        out_shape=[_sds((t, 2 * FFN_HIDDEN), BF16), _sds((t, d), F32), _sds((1, d), F32), _sds((1, d), F32)],
        compiler_params=_cp(ATT_BWD_VMEM_LIMIT), name=name)(dz, wo, gu, wi, xhat, rstd, g)


def _gmlp_bwd(dz, w_out, saved, rstd_v, vg, vb, wm, bs_col, w_in, ln_below):
    t, d = dz.shape
    d2 = 2 * d
    tm = min(ROW_TILE, t)
    gb = GMLP_BLOCK
    xhat_below, rstd_below, g_below = ln_below

    def body(dz_ref, wo_ref, sv_ref, rs_ref, vg_ref, vb_ref, wm_ref, bs_ref, wi_ref, xh_ref, rsb_ref, gb_ref,
             da_ref, dws_ref, dbs_ref, dvg_ref, dvb_ref, dzb_ref, dg_ref, db_ref, dvln_sc):
        first = pl.program_id(0) == 0
        u = sv_ref[:, :d].astype(F32)
        vhat = sv_ref[:, 2 * d:3 * d].astype(F32)
        rstd = rs_ref[...]
        vln = (vhat * vg_ref[...] + vb_ref[...]).astype(BF16)
        dgate = lax.dot_general(dz_ref[...].astype(BF16), wo_ref[...], NT, preferred_element_type=F32)

        @pl.when(first)
        def _():
            dws_ref[...] = jnp.zeros(dws_ref.shape, F32)
            dbs_ref[...] = jnp.zeros(dbs_ref.shape, F32)

        for blk in range(tm // gb):
            rs = slice(blk * gb, (blk + 1) * gb)
            for gi in range(GMLP_GROUPS):
                cs = slice(gi * gb, (gi + 1) * gb)
                vblk = vln[rs, cs]
                s = jnp.dot(wm_ref[gi], vblk, preferred_element_type=F32) + bs_ref[:, gi:gi + 1]
                dgb = dgate[rs, cs]
                da_ref[rs, cs] = (dgb * s * sv_ref[rs, d + gi * gb:d + (gi + 1) * gb].astype(F32)).astype(BF16)
                ds = dgb * u[rs, cs]
                dsb = ds.astype(BF16)
                dws_ref[gi] += lax.dot_general(dsb, vblk, NT, preferred_element_type=F32)
                dbs_ref[:, gi:gi + 1] += jnp.sum(ds, axis=1, keepdims=True)
                dvln_sc[rs, cs] = lax.dot_general(wm_ref[gi], dsb, TN, preferred_element_type=F32)
        dv, dvg, dvb = _ln_bwd(dvln_sc[...], vhat, rstd, vg_ref[...])
        da_ref[:, d:] = (dv * sv_ref[:, 3 * d:].astype(F32)).astype(BF16)
        _accumulate(dvg_ref, first, dvg)
        _accumulate(dvb_ref, first, dvb)
        dx = ALPHA * dz_ref[...]
        nc = wi_ref.shape[2]
        for j in range(wi_ref.shape[0]):
            dx = dx + lax.dot_general(da_ref[:, j * nc:(j + 1) * nc], wi_ref[j], NT, preferred_element_type=F32)
        dzb, dg, db = _ln_bwd(dx, xh_ref[...], rsb_ref[...], gb_ref[...])
        dzb_ref[...] = dzb
        _accumulate(dg_ref, first, dg)
        _accumulate(db_ref, first, db)

    row = lambda i: (i, 0)
    full, col = pl.BlockSpec((tm, d), row), pl.BlockSpec((tm, 1), row)
    vec = pl.BlockSpec((1, d), lambda i: (0, 0))
    return pl.pallas_call(
        body, grid=(t // tm,),
        in_specs=[full, _resident(w_out.shape), pl.BlockSpec((tm, 4 * d), row), col,
                  _resident(vg.shape), _resident(vb.shape), _resident(wm.shape), _resident(bs_col.shape),
                  _resident(w_in.shape), full, col, _resident(g_below.shape)],
        out_specs=[pl.BlockSpec((tm, d2), row), pl.BlockSpec(wm.shape, lambda i: (0, 0, 0)),
                   pl.BlockSpec(bs_col.shape, lambda i: (0, 0)), vec, vec, full, vec, vec],
        out_shape=[_sds((t, d2), BF16), _sds(wm.shape, F32), _sds(bs_col.shape, F32), _sds((1, d), F32), _sds((1, d), F32),
                   _sds((t, d), F32), _sds((1, d), F32), _sds((1, d), F32)],
        scratch_shapes=[pltpu.VMEM((tm, d), F32)],
        compiler_params=_cp(), name="gmlp_bwd")(dz, w_out, saved, rstd_v, vg, vb, wm, bs_col, w_in, xhat_below,
                                                rstd_below, g_below)


def _conv_bwd(bch, dconv, conv_w):
    t = bch.shape[0]
    tm = min(ROW_TILE, t)
    nb = t // tm
    halo_blocks = tm // SUBLANES
    cw = CONV_WIDTH

    def body(cur_ref, prev_ref, next_ref, dc_ref, dn_ref, w_ref, o_ref, dw_ref):
        i = pl.program_id(0)
        bgate, cgate, hval = cur_ref[:, :cw], cur_ref[:, cw:2 * cw], cur_ref[:, 2 * cw:]
        z = cgate * hval
        zp = jnp.where(i == 0, 0.0, prev_ref[:, cw:2 * cw] * prev_ref[:, 2 * cw:])
        z1, z2 = _shift_down(z, zp)
        w0, w1, w2 = w_ref[0:1, :], w_ref[1:2, :], w_ref[2:3, :]
        dconv = dc_ref[...]
        o_ref[:, :cw] = (dconv * (w0 * z2 + w1 * z1 + w2 * z)).astype(BF16)
        dy = dconv * bgate
        dyn = jnp.where(i == nb - 1, 0.0, dn_ref[...] * next_ref[:, :cw])
        dy1, dy2 = _shift_up(dy, dyn)
        dz = w2 * dy + w1 * dy1 + w0 * dy2
        o_ref[:, cw:2 * cw] = (dz * hval).astype(BF16)
        o_ref[:, 2 * cw:] = (dz * cgate).astype(BF16)

        @pl.when(i == 0)
        def _():
            dw_ref[...] = jnp.zeros(dw_ref.shape, F32)

        for tap, zs in enumerate((z2, z1, z)):
            dw_ref[tap:tap + 1, :] += jnp.sum(dy * zs, axis=0, keepdims=True)

    last_halo = t // SUBLANES - 1
    return pl.pallas_call(
        body, grid=(nb,),
        in_specs=[pl.BlockSpec((tm, BCH), lambda i: (i, 0)),
                  pl.BlockSpec((SUBLANES, BCH), lambda i: (jnp.maximum(i * halo_blocks - 1, 0), 0)),
                  pl.BlockSpec((SUBLANES, BCH), lambda i: (jnp.minimum((i + 1) * halo_blocks, last_halo), 0)),
                  pl.BlockSpec((tm, cw), lambda i: (i, 0)),
                  pl.BlockSpec((SUBLANES, cw), lambda i: (jnp.minimum((i + 1) * halo_blocks, last_halo), 0)),
                  _resident(conv_w.shape)],
        out_specs=[pl.BlockSpec((tm, BCH), lambda i: (i, 0)), pl.BlockSpec((SUBLANES, cw), lambda i: (0, 0))],
        out_shape=[_sds((t, BCH), BF16), _sds((SUBLANES, cw), F32)],
        compiler_params=_cp(), name="conv_bwd")(bch, bch, bch, dconv, dconv, conv_w)


def _attn_bwd_prep(dz, w_out, o, qp, lse_pad, after):
    t = o.shape[0]
    tm = min(ROW_TILE, t)
    hd = HEAD_DIM
    sel_lse = _piece_selector(Q_LSE, -1.0)
    sel_delta = _piece_selector(DO_DELTA, -1.0)
    head_of = jnp.asarray([[1.0 if col == row // hd else 0.0 for col in range(LANES)] for row in range(FOX_WIDTH)], F32)

    def body(after_ref, dz_ref, wo_ref, o_ref, qp_ref, lse_ref, sl_ref, sd_ref, seg_ref, qb_ref, dob_ref, dconv_ref):
        dzb = dz_ref[...].astype(BF16)
        do = lax.dot_general(dzb, wo_ref[:FOX_WIDTH, :], NT, preferred_element_type=F32)
        dconv_ref[...] = lax.dot_general(dzb, wo_ref[FOX_WIDTH:, :], NT, preferred_element_type=F32)
        delta = jnp.dot(o_ref[...].astype(F32) * do, seg_ref[...], precision=HIGHEST, preferred_element_type=F32)
        lse_extra = jnp.dot(_piece_rows(lse_ref[...]), sl_ref[...], preferred_element_type=F32)
        do_extra = jnp.dot(_piece_rows(delta), sd_ref[...], preferred_element_type=F32).astype(BF16)
        for h in range(FOX_HEADS):
            hs = slice(h * hd, (h + 1) * hd)
            dob_ref[h, :, :hd] = do[:, hs].astype(BF16)
            dob_ref[h, :, hd:] = do_extra[:, hs]
            qb_ref[h, :, :hd] = qp_ref[h, :, :hd]
            qb_ref[h, :, hd:] = (qp_ref[h, :, hd:].astype(F32) + lse_extra[:, hs]).astype(BF16)

    row = lambda i: (i, 0)
    row3 = pl.BlockSpec((FOX_HEADS, tm, LANES), lambda i: (0, i, 0))
    half = pl.BlockSpec((tm, FOX_WIDTH), row)
    return pl.pallas_call(
        body, grid=(t // tm,),
        in_specs=[_ANY_SPEC, pl.BlockSpec((tm, dz.shape[1]), row), _resident(w_out.shape), half, row3,
                  pl.BlockSpec((tm, LANES), row), _resident(sel_lse.shape), _resident(sel_delta.shape),
                  _resident(head_of.shape)],
        out_specs=[row3, row3, half],
        out_shape=[_sds((FOX_HEADS, t, LANES), BF16)] * 2 + [_sds((t, FOX_WIDTH), F32)],
        compiler_params=_cp(), name="attn_bwd_prep")(after, dz, w_out, o, qp, lse_pad, sel_lse, sel_delta, head_of)


def _attn_bwd(qb, kp, vp, dob, kt):
    t = qb.shape[1]
    bq = min(ATT_BLOCK, t)
    nq = t // bq
    i_tab, j_tab = _triangle(nq, key_major=True)

    def body(it_ref, jt_ref, q_ref, k_ref, v_ref, do_ref, kt_ref, dqt_ref, dk_ref, dv_ref, dk_sc, dv_sc):
        s = pl.program_id(1)
        i, j = it_ref[s], jt_ref[s]

        @pl.when(s == 0)
        def _():
            dqt_ref[...] = jnp.zeros(dqt_ref.shape, F32)

        @pl.when(i == j)
        def _():
            dk_sc[...] = jnp.zeros(dk_sc.shape, F32)
            dv_sc[...] = jnp.zeros(dv_sc.shape, F32)

        hq = bq // 2

        def sweep(masked):
            def span(half):
                return slice(half * hq, (half + 1) * hq), (slice(0, hq) if masked and half == 0 else slice(0, bq))

            def scores(h, half):
                qs, ks = span(half)
                return (lax.dot_general(k_ref[h, ks, :], q_ref[h, qs, :], NT, preferred_element_type=F32),
                        lax.dot_general(v_ref[h, ks, :], do_ref[h, qs, :], NT, preferred_element_type=F32))

            def accumulate(h, half, ptb, dstb):
                qs, ks = span(half)
                cols = pl.ds(pl.multiple_of(i * bq + half * hq, hq), hq)
                dv_sc[h, ks, :] += jnp.dot(ptb, do_ref[h, qs, :], preferred_element_type=F32)
                dk_sc[h, ks, :] += jnp.dot(dstb, q_ref[h, qs, :], preferred_element_type=F32)
                dqt_ref[h, :, cols] += jnp.dot(kt_ref[h, :, ks], dstb, preferred_element_type=F32)

            units = [(h, half) for h in range(ATT_BWD_HEADS) for half in range(2)]
            ahead, behind = scores(*units[0]), None
            for n, (h, half) in enumerate(units):
                st, dpt = ahead
                if n + 1 < len(units):
                    ahead = scores(*units[n + 1])
                if behind is not None:
                    accumulate(*behind)
                if masked:
                    key = lax.broadcasted_iota(jnp.int32, st.shape, 0)
                    qry = lax.broadcasted_iota(jnp.int32, st.shape, 1) + half * hq
                    st = jnp.where(key <= qry, st, NEG)
                pt = jnp.exp(st)
                behind = (h, half, pt.astype(BF16), (pt * dpt).astype(BF16))
            accumulate(*behind)

        @pl.when(i == j)
        def _():
            sweep(True)

        @pl.when(i > j)
        def _():
            sweep(False)

        @pl.when(i == nq - 1)
        def _():
            dk_ref[...] = dk_sc[...]
            dv_ref[...] = dv_sc[...].astype(BF16)

    nh = ATT_BWD_HEADS
    qblk = pl.BlockSpec((nh, bq, LANES), lambda hp, s, it, jt: (hp, it[s], 0))
    kblk = pl.BlockSpec((nh, bq, LANES), lambda hp, s, it, jt: (hp, jt[s], 0))
    grid_spec = pltpu.PrefetchScalarGridSpec(
        num_scalar_prefetch=2, grid=(FOX_HEADS // nh, i_tab.shape[0]),
        in_specs=[qblk, kblk, kblk, qblk, pl.BlockSpec((nh, LANES, bq), lambda hp, s, it, jt: (hp, 0, jt[s]))],
        out_specs=[pl.BlockSpec((nh, LANES, t), lambda hp, s, it, jt: (hp, 0, 0), pipeline_mode=pl.Buffered(1)),
                   kblk, kblk],
        scratch_shapes=[pltpu.VMEM((nh, bq, LANES), F32), pltpu.VMEM((nh, bq, LANES), F32)])
    return pl.pallas_call(body, grid_spec=grid_spec,
                          out_shape=[_sds((FOX_HEADS, LANES, t), F32), _sds((FOX_HEADS, t, LANES), F32),
                                     _sds((FOX_HEADS, t, LANES), BF16)],
                          compiler_params=_cp(ATT_BWD_VMEM_LIMIT), name="attn_bwd")(i_tab, j_tab, qb, kp, vp, dob, kt)


def _attn_unpack(dqt, dkp, dvp):
    t = dkp.shape[1]
    tm = min(LAYOUT_ROW_TILE, t)
    hd = HEAD_DIM

    def body(dqt_ref, dk_ref, dv_ref, o_ref, dc_ref):
        for h in range(FOX_HEADS):
            dq = dqt_ref[h].T
            o_ref[:, h * hd:(h + 1) * hd] = (dq[:, :hd] * (hd ** -0.5)).astype(BF16)
            o_ref[:, FOX_WIDTH + h * hd:FOX_WIDTH + (h + 1) * hd] = dk_ref[h, :, :hd].astype(BF16)
            o_ref[:, 2 * FOX_WIDTH + h * hd:2 * FOX_WIDTH + (h + 1) * hd] = dv_ref[h, :, :hd]
            dc_ref[:, h:h + 1] = dq[:, K_ONE:K_ONE + 1] - dk_ref[h, :, Q_ONE:Q_ONE + 1]

    row3 = pl.BlockSpec((FOX_HEADS, tm, LANES), lambda i: (0, i, 0))
    return pl.pallas_call(
        body, grid=(t // tm,),
        in_specs=[pl.BlockSpec((FOX_HEADS, LANES, tm), lambda i: (0, 0, i)), row3, row3],
        out_specs=[pl.BlockSpec((tm, QKV), lambda i: (i, 0)), pl.BlockSpec((tm, FOX_HEADS), lambda i: (i, 0))],
        out_shape=[_sds((t, QKV), BF16), _sds((t, FOX_HEADS), F32)],
        compiler_params=_cp(), name="attn_unpack")(dqt, dkp, dvp)


def _adamw(parts, w, m, v, name, layer=None, into=None):
    nl, r, c = w.shape
    fits = [cand for cand in [*range(SUBLANES, r, SUBLANES), r] if r % cand == 0 and cand * c * 4 <= ADAMW_BLOCK_BYTES]
    tr = max(fits) if fits else r
    npart = len(parts)
    bc1 = 1.0 - ADAM_B1 ** ADAM_STEP
    bc2 = 1.0 - ADAM_B2 ** ADAM_STEP

    def body(*refs):
        p_refs = refs[:npart]
        w_ref, m_ref, v_ref = refs[npart:npart + 3]
        g_ref, d_ref, nm_ref, nv_ref = refs[-4:]
        sums = []
        for p_ref in p_refs:
            acc = p_ref[0, 0].astype(F32)
            for s in range(1, p_ref.shape[0]):
                acc = acc + p_ref[s, 0].astype(F32)
            sums.append(acc)
        g = sums[0]
        for extra in sums[1:]:
            g = g + extra
        nm = ADAM_B1 * m_ref[0] + (1.0 - ADAM_B1) * g
        nv = ADAM_B2 * v_ref[0] + (1.0 - ADAM_B2) * (g * g)
        m_hat = nm / bc1
        v_hat = nv / bc2
        g_ref[0] = g
        d_ref[0] = -ADAM_LR * (m_hat / (jnp.sqrt(v_hat) + ADAM_EPS) + ADAM_WD * w_ref[0])
        nm_ref[0] = nm
        nv_ref[0] = nv

    first = 0 if layer is None else layer
    blk = pl.BlockSpec((1, tr, c), lambda l, i: (first + l, i, 0))
    extra = [] if into is None else list(into)
    return pl.pallas_call(
        body, grid=(nl if layer is None else 1, r // tr),
        in_specs=[pl.BlockSpec((p.shape[0], 1, tr, c), lambda l, i: (0, l, i, 0)) for p in parts] + [blk, blk, blk]
        + [_ANY_SPEC] * len(extra),
        out_specs=[blk] * 4, out_shape=[_sds(w.shape, F32)] * 4,
        input_output_aliases={npart + 3 + k: k for k in range(len(extra))},
        compiler_params=_cp(), name=name)(*parts, w, m, v, *extra)


def _to_rows(a):
    flat = a.reshape(-1)
    pad = (-flat.shape[0]) % LANES
    if pad:
        flat = jnp.concatenate([flat, jnp.zeros((pad,), flat.dtype)])
    return flat.reshape(-1, LANES)


def _ffn_fwd(xin_ln, xin_b, wi, wo, g, b, layer, target=None):
    t, d = xin_b.shape
    tm = min(FFN_FUSED_ROW_TILE, t)
    hh = HALF_HIDDEN
    rxh, rg, rb = xin_ln

    def body(x_ref, wi_ref, wo_ref, rxh_ref, rg_ref, rb_ref, g_ref, b_ref, *rest):
        gu_ref, h_ref = rest[target is not None:][:2]
        a = x_ref[...]
        for c in range(2):
            gs, us = slice(c * hh, (c + 1) * hh), slice(FFN_HIDDEN + c * hh, FFN_HIDDEN + (c + 1) * hh)
            gate = jnp.dot(a, wi_ref[c], preferred_element_type=F32)
            up = jnp.dot(a, wi_ref[2 + c], preferred_element_type=F32)
            sig = _sigmoid(gate)
            silu = gate * sig
            gu_ref[:, gs] = (up * sig * (1.0 + gate * (1.0 - sig))).astype(BF16)
            gu_ref[:, us] = silu.astype(BF16)
            h_ref[:, gs] = (silu * up).astype(BF16)
        z = ALPHA * (rxh_ref[...] * rg_ref[...] + rb_ref[...]) + jnp.dot(h_ref[...], wo_ref[...], preferred_element_type=F32)
        xhat, rstd = _ln_fwd(z)
        if target is None:
            yb_ref, xh_ref, rs_ref = rest[2:]
            yb_ref[...] = (xhat * g_ref[...] + b_ref[...]).astype(BF16)
            xh_ref[...] = xhat
            rs_ref[...] = rstd
            return
        sq_ref, dz_ref, dg_ref, db_ref = rest[3:]
        first = pl.program_id(0) == 0
        err = xhat * g_ref[...] + b_ref[...] - rest[0][...]
        dz, dg, db = _ln_bwd(err * (1.0 / d), xhat, rstd, g_ref[...])
        dz_ref[...] = dz
        _accumulate(sq_ref, first, jnp.sum(err * err, axis=0, keepdims=True))
        _accumulate(dg_ref, first, dg)
        _accumulate(db_ref, first, db)

    row = lambda i: (i, 0)
    full = pl.BlockSpec((tm, d), row)
    vec = _resident(g.shape)
    acc = pl.BlockSpec((1, d), lambda i: (0, 0))
    in_specs = [full, _resident(wi.shape), _resident(wo.shape), full, vec, vec, vec, vec]
    out_specs = [pl.BlockSpec((tm, 2 * FFN_HIDDEN), row), pl.BlockSpec((tm, FFN_HIDDEN), row)]
    out_shape = [_sds((t, 2 * FFN_HIDDEN), BF16), _sds((t, FFN_HIDDEN), BF16)]
    args = [xin_b, wi, wo, rxh, rg, rb, g, b]
    if target is None:
        out_specs += [full, full, pl.BlockSpec((tm, 1), row)]
        out_shape += [_sds((t, d), BF16), _sds((t, d), F32), _sds((t, 1), F32)]
    else:
        in_specs.append(full)
        args.append(target)
        out_specs += [acc, full, acc, acc]
        out_shape += [_sds((1, d), F32), _sds((t, d), F32), _sds((1, d), F32), _sds((1, d), F32)]
    gu, h, *tail = pl.pallas_call(body, grid=(t // tm,), in_specs=in_specs, out_specs=out_specs, out_shape=out_shape,
                                  compiler_params=_cp(ATT_BWD_VMEM_LIMIT), name=f"ffn_fwd_rows_{layer}")(*args)
    if target is None:
        y_b, xhat, rstd = tail
        return y_b, (xin_b, gu, h, xhat, rstd)
    return tail, (xin_b, gu, h)


def _ffn_bwd(dz, saved, wi, wo, ln_below, layer):
    xin_b, gu, h = saved[:3]
    dgu, *below = _ffn_bwd_rows(dz, wo, gu, wi, ln_below, f"ffn_bwd_rows_{layer}")
    g_out = _mm_tn(h, dz, f"ffn_dw_out_{layer}", tn=D_MODEL, tk=HALF_HIDDEN)
    g_in = _mm_tn(xin_b, dgu, f"ffn_dw_in_{layer}", tn=HALF_HIDDEN, stack_cols=True)
    return below, g_in, g_out.reshape(N_CHIPS, FFN_HIDDEN // N_CHIPS, D_MODEL)


def kernel(x, even_w_in, even_b_f, even_conv_w, even_w_out, odd_w_in, odd_v_ln_g, odd_v_ln_b, odd_w_s, odd_b_s, odd_w_out, mix_ln_g, mix_ln_b, ffn_w_in, ffn_w_out, ffn_ln_g, ffn_ln_b, loss_target, m_even_w_in, m_even_b_f, m_even_conv_w, m_even_w_out, m_odd_w_in, m_odd_v_ln_g, m_odd_v_ln_b, m_odd_w_s, m_odd_b_s, m_odd_w_out, m_mix_ln_g, m_mix_ln_b, m_ffn_w_in, m_ffn_w_out, m_ffn_ln_g, m_ffn_ln_b, v_even_w_in, v_even_b_f, v_even_conv_w, v_even_w_out, v_odd_w_in, v_odd_v_ln_g, v_odd_v_ln_b, v_odd_w_s, v_odd_b_s, v_odd_w_out, v_mix_ln_g, v_mix_ln_b, v_ffn_w_in, v_ffn_w_out, v_ffn_ln_g, v_ffn_ln_b):
    t = x.shape[1]
    d = D_MODEL
    chip = 2 * lax.axis_index("x") + lax.axis_index("y")
    x2d = x[0]
    target = loss_target[0]

    small_shard = jnp.concatenate([odd_v_ln_g.reshape(2, LANES), odd_v_ln_b.reshape(2, LANES),
                                   even_conv_w.reshape(CONV_K, LANES), jnp.zeros((1, LANES), F32)], axis=0)
    first = [jnp.swapaxes(even_w_in[0], 0, 1).astype(BF16)]
    second = [even_w_out[0].astype(BF16), small_shard]
    later = [odd_w_in[0].astype(BF16), odd_w_out[0].astype(BF16), ffn_w_in[0].astype(BF16), ffn_w_in[1].astype(BF16),
             ffn_w_out[0].astype(BF16), ffn_w_out[1].astype(BF16)]
    first_h, first_tok = _split_start(first, "gather4", "gather_first_start")
    second_h, second_tok = _split_start(second, "gather4", "gather_second_start", after=first_tok)
    later_h, later_tok = _split_start(later, "gather4", "gather_later_start", after=second_tok)
    (g_ewi,) = _gathered(first_h, "gather_first_wait", later_tok)
    ewi = g_ewi.reshape(EVEN_IN, d)
    w_even_in = jnp.concatenate([ewi[:QKV], ewi[QKV + FOX_HEADS:],
                                 jnp.pad(ewi[QKV:QKV + FOX_HEADS], ((0, LANES - FOX_HEADS), (0, 0)))], axis=0)
    chunk_id = jnp.arange(GMLP_BLOCK) // CHUNK
    gmask = chunk_id[None, :] <= chunk_id[:, None]
    w_spatial = jnp.where(gmask[None], odd_w_s[0], 0.0).astype(BF16)
    bs_col = odd_b_s[0].T
    b_f_col = even_b_f.reshape(FOX_HEADS, 1)
    ln = lambda p, l: p[l:l + 1]

    qkv, bch, fl, x2d_b = _proj(x2d, w_even_in, [(0, QKV, BF16), (QKV, QKV + BCH, F32), (QKV + BCH, EVEN_IN_PAD, F32)], "even_proj")
    fl3 = fl[:, :FOX_HEADS].T.reshape(FOX_HEADS, t // LANES, LANES).transpose(1, 0, 2)
    c3 = _fgate_fwd(fl3, b_f_col)
    c_rows = c3.transpose(1, 0, 2).reshape(FOX_HEADS, t)
    head_lanes = lambda rows: jnp.pad(rows.T, ((0, 0), (0, LANES - FOX_HEADS)))
    qp, kp, vp, kt, vt = _attn_pack(qkv, head_lanes(c_rows))
    attn, lse = _attn_fwd(qp, kp, vt)
    g_ewo, g_small = _gathered(second_h, "gather_second_wait", attn)
    w_even_out = g_ewo.reshape(d, d)
    v_ln_g = g_small[:, 0:2].reshape(1, d)
    v_ln_b = g_small[:, 2:4].reshape(1, d)
    conv_w = g_small[:, 4:7].transpose(1, 0, 2).reshape(CONV_K, CONV_WIDTH)
    conv, x1_b, xh1, rs1 = _even_out(attn, bch, conv_w, w_even_out, x2d, ln(mix_ln_g, 0), ln(mix_ln_b, 0))
    w_odd_in, g_owo, w_fi0, w_fi1, g_fo0, g_fo1 = _gathered(later_h, "gather_later_wait", x1_b)
    w_odd_out = g_owo.reshape(d, d)
    w_ffn_in = [w_fi0, w_fi1]
    w_ffn_out = [g_fo0.reshape(FFN_HIDDEN, d), g_fo1.reshape(FFN_HIDDEN, d)]
    x2_b, ffn0 = _ffn_fwd((xh1, ln(mix_ln_g, 0), ln(mix_ln_b, 0)), x1_b, w_ffn_in[0], w_ffn_out[0],
                          ln(ffn_ln_g, 0), ln(ffn_ln_b, 0), 0)

    sv_odd, rs_odd, gated, x3_b, xh3, rs3 = _gmlp_fwd(
        x2_b, w_odd_in, v_ln_g, v_ln_b, w_spatial, bs_col, w_odd_out, (ffn0[3], ln(ffn_ln_g, 0), ln(ffn_ln_b, 0)),
        ln(mix_ln_g, 1), ln(mix_ln_b, 1))
    (sq, dz4, d_fg1, d_fb1), ffn1 = _ffn_fwd((xh3, ln(mix_ln_g, 1), ln(mix_ln_b, 1)), x3_b, w_ffn_in[1], w_ffn_out[1],
                                             ln(ffn_ln_g, 1), ln(ffn_ln_b, 1), 1, target=target)

    loss = lax.psum(0.5 / d * jnp.sum(sq), ("x", "y", "c"))
    (dz3, d_mg1, d_mb1), gi_f1, go_f1 = _ffn_bwd(dz4, ffn1, w_ffn_in[1], w_ffn_out[1], (xh3, rs3, ln(mix_ln_g, 1)), 1)

    go_odd = _mm_tn(gated, dz3, "odd_dw_out", tn=d).reshape(N_CHIPS, 1, d // N_CHIPS, d)
    da_odd, dws, dbs_col, d_vg, d_vb, dz2, d_fg0, d_fb0 = _gmlp_bwd(
        dz3, w_odd_out, sv_odd, rs_odd, v_ln_g, v_ln_b, w_spatial, bs_col, w_odd_in,
        (ffn0[3], ffn0[4], ln(ffn_ln_g, 0)))
    gi_odd = _mm_tn(x2_b, da_odd, "odd_dw_in", tn=d // 2, stack_cols=True)[:, None]
    (dz1, d_mg0, d_mb0), gi_f0, go_f0 = _ffn_bwd(dz2, ffn0, w_ffn_in[0], w_ffn_out[0], (xh1, rs1, ln(mix_ln_g, 0)), 0)

    sent_early = [gi_odd, go_odd, gi_f0[:, None], gi_f1[:, None], go_f0[:, None], go_f1[:, None]]
    early_h, early_tok = _split_start(sent_early, "scatter4", "scatter_early_start")
    qb, dob, dconv = _attn_bwd_prep(dz1, w_even_out, attn, qp, head_lanes(lse.reshape(FOX_HEADS, t)), early_tok)
    go_even = _mm_tn_pair(attn, conv, dz1, "even_dw_out").reshape(N_CHIPS, 1, d // N_CHIPS, d)
    dbch, dconv_w8 = _conv_bwd(bch, dconv, conv_w)
    dqkv, dc_col = _attn_unpack(*_attn_bwd(qb, kp, vp, dob, kt))
    dc3 = dc_col.T.reshape(FOX_HEADS, t // LANES, LANES).transpose(1, 0, 2)
    dfl3, d_bf = _fgate_bwd(dc3, fl3, b_f_col)
    dfl = jnp.concatenate([dfl3.transpose(1, 0, 2).reshape(FOX_HEADS, t).T.astype(BF16),
                           jnp.zeros((t, LANES - FOX_HEADS), BF16)], axis=1)

    dws_masked = jnp.where(gmask[None], dws, 0.0)
    rep_names = ["odd_w_s", "odd_b_s", "mix_ln_g", "mix_ln_b", "ffn_ln_g", "ffn_ln_b", "even_b_f"]
    rep_grads = [dws_masked, dbs_col.T, jnp.concatenate([d_mg0, d_mg1]), jnp.concatenate([d_mb0, d_mb1]),
                 jnp.concatenate([d_fg0, d_fg1]), jnp.concatenate([d_fb0, d_fb1]), d_bf.reshape(1, FOX_HEADS)]
    rep_w = [(odd_w_s, m_odd_w_s, v_odd_w_s), (odd_b_s, m_odd_b_s, v_odd_b_s), (mix_ln_g, m_mix_ln_g, v_mix_ln_g),
             (mix_ln_b, m_mix_ln_b, v_mix_ln_b), (ffn_ln_g, m_ffn_ln_g, v_ffn_ln_g), (ffn_ln_b, m_ffn_ln_b, v_ffn_ln_b),
             (even_b_f, m_even_b_f, v_even_b_f)]
    rep_rows = [_to_rows(gr) for gr in rep_grads]
    n_rep = sum(r.shape[0] for r in rep_rows)
    pad_rep = (-n_rep) % SUBLANES
    dconv_w = dconv_w8[:CONV_K].reshape(CONV_K, N_CHIPS, LANES).transpose(1, 0, 2).reshape(N_CHIPS * CONV_K, LANES)
    packed = jnp.concatenate(rep_rows + [jnp.zeros((pad_rep, LANES), F32), d_vg.reshape(SUBLANES, LANES),
                                         d_vb.reshape(SUBLANES, LANES), dconv_w, jnp.zeros((4, LANES), F32)], axis=0)
    small_h, small_tok = _split_start([packed], "gather8", "gather_small_start")

    swap_h, swap_tok = _split_start(_scattered(early_h, "scatter_early_wait", small_tok), "swap2", "swap_early_start")
    dw_qkv = _mm_tn(dqkv, x2d_b, "even_dw_qkv", tn=d, tk=QKV // 2, after=swap_tok)
    dw_bch = _mm_tn(dbch, x2d_b, "even_dw_bch", tn=d, tk=BCH // 2)
    dw_f = _mm_tn(dfl, x2d_b, "even_dw_f", tn=d)
    gi_even = jnp.concatenate([dw_qkv, dw_f[:FOX_HEADS], dw_bch], axis=0).reshape(N_CHIPS, 1, -1, LANES)
    sent_late = [gi_even, go_even]
    late_h, late_tok = _split_start(sent_late, "scatter4", "scatter_late_start")
    grad_x = _mm_back([(dqkv, 0, QKV), (dbch, QKV, QKV + BCH), (dfl, QKV + BCH, EVEN_IN_PAD)], w_even_in, dz1,
                      late_tok, "even_dx")
    mine, theirs = _split_wait(swap_h, "swap_early_wait", grad_x)
    res = {}
    res["odd_w_in"] = _adamw([mine[0], theirs[0]], odd_w_in, m_odd_w_in, v_odd_w_in, "adamw_odd_w_in")
    res["odd_w_out"] = _adamw([mine[1], theirs[1]], odd_w_out, m_odd_w_out, v_odd_w_out, "adamw_odd_w_out")
    for nm, at, (w, m, v) in (("ffn_w_in", 2, (ffn_w_in, m_ffn_w_in, v_ffn_w_in)),
                              ("ffn_w_out", 4, (ffn_w_out, m_ffn_w_out, v_ffn_w_out))):
        upper = _adamw([mine[at + 1], theirs[at + 1]], w, m, v, f"adamw_{nm}_1", layer=1)
        res[nm] = _adamw([mine[at], theirs[at]], w, m, v, f"adamw_{nm}_0", layer=0, into=upper)
    mine_late = _scattered(late_h, "scatter_late_wait", res["ffn_w_out"][0])
    theirs_late = _exchange(mine_late, "swap2", "swap_late")
    rows = lambda a: jnp.swapaxes(a, 1, 2).reshape(1, -1, LANES)
    back = lambda a: jnp.swapaxes(a.reshape(1, EVEN_IN // N_CHIPS, d), 1, 2)
    res["even_w_in"] = [back(o) for o in _adamw([mine_late[0], theirs_late[0]], rows(even_w_in), rows(m_even_w_in),
                                                rows(v_even_w_in), "adamw_even_w_in")]
    res["even_w_out"] = _adamw([mine_late[1], theirs_late[1]], even_w_out, m_even_w_out, v_even_w_out,
                               "adamw_even_w_out")
    (packed,), (gathered,) = _split_wait(small_h, "gather_small_wait", theirs_late[0])
    gathered = lax.dynamic_update_index_in_dim(gathered, packed, 4 * lax.axis_index("x") + 2 * lax.axis_index("y")
                                               + lax.axis_index("c"), 0)

    base = n_rep + pad_rep
    own_rows = jnp.concatenate([
        lax.dynamic_slice_in_dim(gathered, base + 2 * chip, 2, axis=1),
        lax.dynamic_slice_in_dim(gathered, base + SUBLANES + 2 * chip, 2, axis=1),
        lax.dynamic_slice_in_dim(gathered, base + 2 * SUBLANES + CONV_K * chip, CONV_K, axis=1),
        jnp.zeros((N_DEV, 1, LANES), F32)], axis=1)
    small_parts = jnp.concatenate([gathered[:, :base], own_rows], axis=1)[:, None]

    def pack_small(get):
        rows = [_to_rows(get(tw)) for tw in rep_w] + [jnp.zeros((pad_rep, LANES), F32)]
        rows += [get(sh).reshape(-1, LANES) for sh in ((odd_v_ln_g, m_odd_v_ln_g, v_odd_v_ln_g),
                                                       (odd_v_ln_b, m_odd_v_ln_b, v_odd_v_ln_b),
                                                       (even_conv_w, m_even_conv_w, v_even_conv_w))]
        return jnp.concatenate(rows + [jnp.zeros((1, LANES), F32)], axis=0)[None]

    small_out = _adamw([small_parts], pack_small(lambda tw: tw[0]), pack_small(lambda tw: tw[1]),
                       pack_small(lambda tw: tw[2]), "adamw_small")

    def unpack_small(rows3):
        rows = rows3[0]
        out, off = {}, 0
        for nm, (w, _, _), r in zip(rep_names, rep_w, rep_rows):
            out[nm] = rows[off:off + r.shape[0]].reshape(-1)[:w.size].reshape(w.shape)
            off += r.shape[0]
        off += pad_rep
        out["odd_v_ln_g"] = rows[off:off + 2].reshape(odd_v_ln_g.shape)
        out["odd_v_ln_b"] = rows[off + 2:off + 4].reshape(odd_v_ln_b.shape)
        out["even_conv_w"] = rows[off + 4:off + 4 + CONV_K].reshape(even_conv_w.shape)
        return out

    small = [unpack_small(o) for o in small_out]
    order = ["even_w_in", "even_b_f", "even_conv_w", "even_w_out", "odd_w_in", "odd_v_ln_g", "odd_v_ln_b", "odd_w_s",
             "odd_b_s", "odd_w_out", "mix_ln_g", "mix_ln_b", "ffn_w_in", "ffn_w_out", "ffn_ln_g", "ffn_ln_b"]
    outs = [loss, grad_x[None]]
    for kind in range(4):
        for nm in order:
            outs.append(res[nm][kind] if nm in res else small[kind][nm])
    return tuple(outs)
```

```python
import math

import jax
import jax.numpy as jnp
from jax import lax
from jax.experimental import pallas as pl
from jax.experimental.pallas import tpu as pltpu

F32 = jnp.float32
BF16 = jnp.bfloat16

D_MODEL = 1024
FOX_HEADS = 8
HEAD_DIM = 64
FOX_WIDTH = FOX_HEADS * HEAD_DIM
CONV_WIDTH = 512
CONV_K = 3
QKV = 3 * FOX_WIDTH
BCH = 3 * CONV_WIDTH
EVEN_IN = QKV + FOX_HEADS + BCH
EVEN_IN_PAD = QKV + BCH + 128
GMLP_BLOCK = 128
GMLP_GROUPS = 8
CHUNK = 64
FFN_HIDDEN = 2816
HALF_HIDDEN = FFN_HIDDEN // 2
ALPHA = 4.0 ** 0.25
LN_EPS = 1e-5
ADAM_LR = 0.001
ADAM_B1 = 0.9
ADAM_B2 = 0.999
ADAM_EPS = 1e-08
ADAM_WD = 0.01
ADAM_STEP = 10
N_CHIPS = 4
N_DEV = 8
LANES = 128
SUBLANES = 8
ROW_TILE = 512
LAYOUT_ROW_TILE = 1024
FFN_FUSED_ROW_TILE = 512
REDUCE_TILE = 2048
ATT_BLOCK = 512
ATT_FWD_HEADS = 8
ATT_BWD_HEADS = 8
ADAMW_BLOCK_BYTES = 2 ** 20
VMEM_LIMIT = 56 * 2 ** 20
ATT_BWD_VMEM_LIMIT = 60 * 2 ** 20
NEG = -1e30
MESH = pl.DeviceIdType.MESH
HIGHEST = lax.Precision.HIGHEST
Q_C, Q_ONE, Q_LSE = 64, 67, 70
K_ONE, K_C, K_ONE2 = 64, 67, 70
V_ONE = 64
DO_DELTA = 65
NT = (((1,), (1,)), ((), ()))
TN = (((0,), (0,)), ((), ()))


def _cp(limit=VMEM_LIMIT):
    return pltpu.CompilerParams(vmem_limit_bytes=limit)


def _resident(shape):
    zeros = (0,) * len(shape)
    return pl.BlockSpec(shape, lambda *_: zeros, pipeline_mode=pl.Buffered(1))


def _sds(shape, dtype):
    return jax.ShapeDtypeStruct(tuple(shape), dtype)


_MASKS = {
    "gather4": [(1, 0, 0), (0, 1, 0), (1, 1, 0)],
    "scatter4": [(1, 0, 0), (0, 1, 0), (1, 1, 0)],
    "swap2": [(0, 0, 1)],
    "gather8": [(0, 0, 1), (0, 1, 0), (0, 1, 1), (1, 0, 0), (1, 0, 1), (1, 1, 0), (1, 1, 1)],
}


def _exchange(arrs, mode, name):
    n = len(arrs)
    masks = _MASKS[mode]
    npeer = len(masks)
    lead = {"gather4": N_CHIPS, "gather8": N_DEV}.get(mode)
    out_shapes = [_sds(((lead,) if lead else ()) + a.shape, a.dtype) for a in arrs]

    def body(*refs):
        ins, outs = refs[:n], refs[n:2 * n]
        send_sems, recv_sems, loc_sems = refs[2 * n:]
        x, y, c = lax.axis_index("x"), lax.axis_index("y"), lax.axis_index("c")
        chip, dev = 2 * x + y, 4 * x + 2 * y + c
        sends, recvs, locs = [], [], []
        for k in range(n):
            if mode == "gather4":
                locs.append(pltpu.make_async_copy(ins[k], outs[k].at[chip], loc_sems.at[k]))
            elif mode == "scatter4":
                locs.append(pltpu.make_async_copy(ins[k].at[chip], outs[k].at[chip], loc_sems.at[k]))
            elif mode == "gather8":
                locs.append(pltpu.make_async_copy(ins[k], outs[k].at[dev], loc_sems.at[k]))
        for cp in locs:
            cp.start()
        for k in range(n):
            for j, (dx, dy, dc) in enumerate(masks):
                px = 1 - x if dx else x
                py = 1 - y if dy else y
                pc = 1 - c if dc else c
                pchip, pdev = 2 * px + py, 4 * px + 2 * py + pc
                if mode == "gather4":
                    src, dst, land = ins[k], outs[k].at[chip], outs[k].at[pchip]
                elif mode == "scatter4":
                    src, dst, land = ins[k].at[pchip], outs[k].at[chip], outs[k].at[pchip]
                elif mode == "swap2":
                    src, dst, land = ins[k], outs[k], outs[k]
                else:
                    src, dst, land = ins[k], outs[k].at[dev], outs[k].at[pdev]
                s = k * npeer + j
                kw = dict(send_sem=send_sems.at[s], recv_sem=recv_sems.at[s], device_id=(px, py, pc),
                          device_id_type=MESH)
                cp = pltpu.make_async_remote_copy(src_ref=src, dst_ref=dst, **kw)
                cp.start()
                sends.append(cp)
                recvs.append(pltpu.make_async_remote_copy(src_ref=src, dst_ref=land, **kw))
        for cp in recvs:
            cp.wait_recv()
        for cp in sends:
            cp.wait_send()
        for cp in locs:
            cp.wait()

    any_spec = pl.BlockSpec(memory_space=pl.ANY)
    outs = pl.pallas_call(
        body,
        out_shape=out_shapes,
        in_specs=[any_spec] * n,
        out_specs=[any_spec] * n,
        scratch_shapes=[pltpu.SemaphoreType.DMA((n * npeer,)), pltpu.SemaphoreType.DMA((n * npeer,)),
                        pltpu.SemaphoreType.DMA((max(n, 1),))],
        name=name,
    )(*arrs)
    return list(outs)


_HBM_SPEC = pl.BlockSpec(memory_space=pltpu.HBM)
_SEM_SPEC = pl.BlockSpec(memory_space=pltpu.SEMAPHORE)
_ANY_SPEC = pl.BlockSpec(memory_space=pl.ANY)
_EFFECT = pltpu.SideEffectType.DATAFLOW_SIDE_EFFECTING


def _split_copies(mode, ins, lands, send_sems, recv_sems):
    x, y, c = lax.axis_index("x"), lax.axis_index("y"), lax.axis_index("c")
    chip, dev = 2 * x + y, 4 * x + 2 * y + c
    masks = _MASKS[mode]
    out = []
    for k in range(len(ins)):
        for j, (dx, dy, dc) in enumerate(masks):
            px = 1 - x if dx else x
            py = 1 - y if dy else y
            pc = 1 - c if dc else c
            pchip, pdev = 2 * px + py, 4 * px + 2 * py + pc
            if mode == "gather4":
                src, dst, land = ins[k], lands[k].at[chip], lands[k].at[pchip]
            elif mode == "scatter4":
                src, dst, land = ins[k].at[pchip], lands[k].at[chip], lands[k].at[pchip]
            elif mode == "swap2":
                src, dst, land = ins[k], lands[k], lands[k]
            else:
                src, dst, land = ins[k], lands[k].at[dev], lands[k].at[pdev]
            s = k * len(masks) + j
            kw = dict(send_sem=send_sems.at[s], recv_sem=recv_sems.at[s], device_id=(px, py, pc), device_id_type=MESH)
            out.append((pltpu.make_async_remote_copy(src_ref=src, dst_ref=dst, **kw),
                        pltpu.make_async_remote_copy(src_ref=src, dst_ref=land, **kw)))
    return out


def _split_start(arrs, mode, name, after=None):
    n = len(arrs)
    nsem = n * len(_MASKS[mode])
    lead = {"gather4": (N_CHIPS,), "gather8": (N_DEV,)}.get(mode, ())
    land_shapes = [lead + a.shape for a in arrs]

    def body(*refs):
        ins, lands = refs[:n], refs[n:2 * n]
        outs = refs[2 * n + (after is not None):]
        for start, _ in _split_copies(mode, ins, lands, outs[0], outs[1]):
            start.start()
        outs[-1][...] = jnp.zeros(outs[-1].shape, F32)

    srcs = [pltpu.with_memory_space_constraint(a, pltpu.HBM) for a in arrs]
    empties = [pltpu.with_memory_space_constraint(lax.empty(s, a.dtype), pltpu.HBM) for s, a in zip(land_shapes, arrs)]
    res = pl.pallas_call(
        body, name=name,
        out_shape=(pltpu.SemaphoreType.DMA((nsem,)), pltpu.SemaphoreType.DMA((nsem,)),
                   *[pltpu.HBM(a.shape, a.dtype) for a in arrs],
                   *[pltpu.HBM(s, a.dtype) for s, a in zip(land_shapes, arrs)],
                   _sds((SUBLANES, LANES), F32)),
        in_specs=[_HBM_SPEC] * (2 * n) + ([_ANY_SPEC] if after is not None else []),
        out_specs=(_SEM_SPEC, _SEM_SPEC, *[_HBM_SPEC] * (2 * n), pl.BlockSpec(memory_space=pltpu.VMEM)),
        input_output_aliases={k: 2 + k for k in range(2 * n)},
        compiler_params=pltpu.CompilerParams(has_side_effects=_EFFECT),
    )(*srcs, *empties, *([after] if after is not None else []))
    return dict(mode=mode, n=n, sems=res[:2], bufs=res[2:2 + 2 * n]), res[-1]


def _split_wait(handle, name, after):
    n, mode = handle["n"], handle["mode"]

    def body(*refs):
        ins, lands = refs[:n], refs[n:2 * n]
        send_sems, recv_sems = refs[2 * n], refs[2 * n + 1]
        for _, arrival in _split_copies(mode, ins, lands, send_sems, recv_sems):
            arrival.wait_send()
            arrival.wait_recv()

    bufs = handle["bufs"]
    res = pl.pallas_call(
        body, name=name,
        out_shape=tuple(pltpu.HBM(b.shape, b.dtype) for b in bufs),
        in_specs=[_HBM_SPEC] * (2 * n) + [_SEM_SPEC, _SEM_SPEC, _ANY_SPEC],
        out_specs=tuple([_HBM_SPEC] * (2 * n)),
        input_output_aliases={k: k for k in range(2 * n)},
        compiler_params=pltpu.CompilerParams(has_side_effects=_EFFECT),
    )(*bufs, *handle["sems"], after)
    return list(res[:n]), list(res[n:])


def _with_own(landed, own):
    chip = 2 * lax.axis_index("x") + lax.axis_index("y")
    return lax.dynamic_update_index_in_dim(landed, own, chip, 0)


def _gathered(handle, name, after):
    sent, landed = _split_wait(handle, name, after)
    return [_with_own(g, own) for g, own in zip(landed, sent)]


def _scattered(handle, name, after):
    chip = 2 * lax.axis_index("x") + lax.axis_index("y")
    sent, landed = _split_wait(handle, name, after)
    return [_with_own(r, lax.dynamic_index_in_dim(g, chip, 0, keepdims=False)) for r, g in zip(landed, sent)]


def _sigmoid(x):
    return 0.5 * jnp.tanh(0.5 * x) + 0.5


def _log_sigmoid(x):
    e = jnp.exp(-jnp.abs(x))
    log1p = jnp.where(e < 1e-2, e * (1.0 - e * (0.5 - e * (1.0 / 3.0))), jnp.log(1.0 + e))
    return jnp.minimum(x, 0.0) - log1p


def _ln_fwd(z):
    mu = jnp.mean(z, axis=-1, keepdims=True)
    zc = z - mu
    var = jnp.mean(zc * zc, axis=-1, keepdims=True)
    rstd = lax.rsqrt(var + LN_EPS)
    return zc * rstd, rstd


def _ln_bwd(dy, xhat, rstd, g):
    dxh = dy * g
    m1 = jnp.mean(dxh, axis=-1, keepdims=True)
    m2 = jnp.mean(dxh * xhat, axis=-1, keepdims=True)
    dz = rstd * (dxh - m1 - xhat * m2)
    return dz, jnp.sum(dy * xhat, axis=0, keepdims=True), jnp.sum(dy, axis=0, keepdims=True)


def _shift_down(z, halo):
    r = lax.broadcasted_iota(jnp.int32, z.shape, 0)
    z1 = jnp.where(r == 0, halo[7:8, :], pltpu.roll(z, 1, 0))
    z2 = jnp.where(r == 0, halo[6:7, :], jnp.where(r == 1, halo[7:8, :], pltpu.roll(z, 2, 0)))
    return z1, z2


def _shift_up(z, halo):
    n = z.shape[0]
    r = lax.broadcasted_iota(jnp.int32, z.shape, 0)
    z1 = jnp.where(r == n - 1, halo[0:1, :], pltpu.roll(z, n - 1, 0))
    z2 = jnp.where(r == n - 1, halo[1:2, :], jnp.where(r == n - 2, halo[0:1, :], pltpu.roll(z, n - 2, 0)))
    return z1, z2


def _triangle_ones(prefix):
    r = lax.broadcasted_iota(jnp.int32, (LANES, LANES), 0)
    c = lax.broadcasted_iota(jnp.int32, (LANES, LANES), 1)
    return ((r <= c) if prefix else (r >= c)).astype(F32)


def _same_head_chunks(rows, earlier):
    r = lax.broadcasted_iota(jnp.int32, (rows, rows), 0)
    c = lax.broadcasted_iota(jnp.int32, (rows, rows), 1)
    same = r % FOX_HEADS == c % FOX_HEADS
    return jnp.logical_and(same, (c < r) if earlier else (c > r)).astype(F32)


def _accumulate(ref, first, value):
    @pl.when(first)
    def _():
        ref[...] = value

    @pl.when(jnp.logical_not(first))
    def _():
        ref[...] += value


def _proj(x, wt, splits, name):
    t, k = x.shape
    tm = min(ROW_TILE, t)

    def body(x_ref, w_ref, *outs):
        a = x_ref[...].astype(BF16)
        for (lo, hi, dt), o in zip(splits, outs):
            o[...] = lax.dot_general(a, w_ref[lo:hi, :], NT, preferred_element_type=F32).astype(dt)
        outs[-1][...] = a

    row = lambda i: (i, 0)
    return pl.pallas_call(
        body, grid=(t // tm,),
        in_specs=[pl.BlockSpec((tm, k), row), _resident(wt.shape)],
        out_specs=[pl.BlockSpec((tm, hi - lo), row) for lo, hi, _ in splits] + [pl.BlockSpec((tm, k), row)],
        out_shape=[_sds((t, hi - lo), dt) for lo, hi, dt in splits] + [_sds((t, k), BF16)],
        compiler_params=_cp(), name=name)(x, wt)


def _fgate_fwd(fl3, b_f):
    nc = fl3.shape[0]
    rows = nc * FOX_HEADS

    def body(f_ref, b_ref, c_ref):
        within = jnp.dot(_log_sigmoid(f_ref[...] + b_ref[...]), _triangle_ones(True), precision=HIGHEST,
                         preferred_element_type=F32)
        totals = jnp.broadcast_to(within[:, LANES - 1:LANES], within.shape)
        c_ref[...] = within + jnp.dot(_same_head_chunks(rows, earlier=True), totals, precision=HIGHEST,
                                      preferred_element_type=F32)

    c2 = pl.pallas_call(body, out_shape=_sds((rows, LANES), F32), name="fgate_fwd")(
        fl3.reshape(rows, LANES), jnp.tile(b_f, (nc, 1)))
    return c2.reshape(fl3.shape)


def _fgate_bwd(dc3, fl3, b_f):
    nc = fl3.shape[0]
    rows = nc * FOX_HEADS

    def body(dc_ref, f_ref, b_ref, df_ref, db_ref):
        within = jnp.dot(dc_ref[...], _triangle_ones(False), precision=HIGHEST, preferred_element_type=F32)
        totals = jnp.broadcast_to(within[:, 0:1], within.shape)
        dlf = within + jnp.dot(_same_head_chunks(rows, earlier=False), totals, precision=HIGHEST,
                               preferred_element_type=F32)
        df = dlf * (1.0 - _sigmoid(f_ref[...] + b_ref[...]))
        df_ref[...] = df
        head = lax.broadcasted_iota(jnp.int32, (FOX_HEADS, rows), 0)
        row = lax.broadcasted_iota(jnp.int32, (FOX_HEADS, rows), 1)
        of_head = (row % FOX_HEADS == head).astype(F32)
        per_row = jnp.broadcast_to(jnp.sum(df, axis=1, keepdims=True), df.shape)
        db_ref[...] = jnp.dot(of_head, per_row, precision=HIGHEST, preferred_element_type=F32)[:, 0:1]

    df2, db = pl.pallas_call(body, out_shape=[_sds((rows, LANES), F32), _sds((FOX_HEADS, 1), F32)], name="fgate_bwd")(
        dc3.reshape(rows, LANES), fl3.reshape(rows, LANES), jnp.tile(b_f, (nc, 1)))
    return df2.reshape(fl3.shape), db


def _split3(c):
    hi = c.astype(BF16).astype(F32)
    mid = (c - hi).astype(BF16).astype(F32)
    lo = (c - hi - mid).astype(BF16).astype(F32)
    return hi, mid, lo


PIECE_ONE = 3 * FOX_HEADS


def _piece_rows(values):
    hi, mid, lo = _split3(values)
    lane = lax.broadcasted_iota(jnp.int32, values.shape, 1)
    row = hi + pltpu.roll(mid, FOX_HEADS, 1) + pltpu.roll(lo, 2 * FOX_HEADS, 1) + jnp.where(lane == PIECE_ONE, 1.0, 0.0)
    return row.astype(BF16)


def _piece_selector(start, sign, ones=()):
    sel = [[0.0] * FOX_WIDTH for _ in range(LANES)]
    for h in range(FOX_HEADS):
        for n in range(3):
            sel[n * FOX_HEADS + h][h * HEAD_DIM + start - HEAD_DIM + n] = sign
        for lane in ones:
            sel[PIECE_ONE][h * HEAD_DIM + lane - HEAD_DIM] = 1.0
    return jnp.asarray(sel, BF16)


def _attn_pack(qkv, c_pad):
    t = qkv.shape[0]
    tm = min(LAYOUT_ROW_TILE, t)
    hd = HEAD_DIM
    sel_q = _piece_selector(Q_C, 1.0, range(Q_ONE, Q_ONE + 3))
    sel_k = _piece_selector(K_C, -1.0, [*range(K_ONE, K_ONE + 3), *range(K_ONE2, K_ONE2 + 3)])
    sel_v = _piece_selector(HEAD_DIM, 0.0, range(V_ONE, V_ONE + 4))

    def body(x_ref, c_ref, sq_ref, sk_ref, sv_ref, qp_ref, kp_ref, vp_ref, kt_ref, vt_ref):
        pieces = _piece_rows(c_ref[...])
        q_extra = jnp.dot(pieces, sq_ref[...], preferred_element_type=F32).astype(BF16)
        k_extra = jnp.dot(pieces, sk_ref[...], preferred_element_type=F32).astype(BF16)
        v_extra = jnp.dot(pieces, sv_ref[...], preferred_element_type=F32).astype(BF16)
        for h in range(FOX_HEADS):
            hs = slice(h * hd, (h + 1) * hd)
            qp_ref[h, :, :hd] = (x_ref[:, hs].astype(F32) * (hd ** -0.5)).astype(BF16)
            qp_ref[h, :, hd:] = q_extra[:, hs]
            kp_ref[h, :, :hd] = x_ref[:, FOX_WIDTH + h * hd:FOX_WIDTH + (h + 1) * hd]
            kp_ref[h, :, hd:] = k_extra[:, hs]
            vp_ref[h, :, :hd] = x_ref[:, 2 * FOX_WIDTH + h * hd:2 * FOX_WIDTH + (h + 1) * hd]
            vp_ref[h, :, hd:] = v_extra[:, hs]
            kt_ref[h] = kp_ref[h].T
            vt_ref[h] = vp_ref[h].T

    row3 = pl.BlockSpec((FOX_HEADS, tm, LANES), lambda i: (0, i, 0))
    col3 = pl.BlockSpec((FOX_HEADS, LANES, tm), lambda i: (0, 0, i))
    sel = _resident(sel_q.shape)
    return pl.pallas_call(
        body, grid=(t // tm,),
        in_specs=[pl.BlockSpec((tm, QKV), lambda i: (i, 0)), pl.BlockSpec((tm, LANES), lambda i: (i, 0)), sel, sel, sel],
        out_specs=[row3, row3, row3, col3, col3],
        out_shape=[_sds((FOX_HEADS, t, LANES), BF16)] * 3 + [_sds((FOX_HEADS, LANES, t), BF16)] * 2,
        compiler_params=_cp(), name="attn_pack")(qkv, c_pad, sel_q, sel_k, sel_v)


def _triangle(nq, key_major):
    if key_major:
        pairs = [(i, j) for j in range(nq) for i in range(j, nq)]
    else:
        pairs = [(i, j) for i in range(nq) for j in range(i + 1)]
    return jnp.asarray([p[0] for p in pairs], jnp.int32), jnp.asarray([p[1] for p in pairs], jnp.int32)


def _attn_fwd(qp, kp, vt):
    t = qp.shape[1]
    bq = min(ATT_BLOCK, t)
    nq = t // bq
    nh = ATT_FWD_HEADS
    i_tab, j_tab = _triangle(nq, key_major=False)

    def body(it_ref, jt_ref, q_ref, k_ref, vt_ref, o_ref, lse_ref, m_sc, acc_sc):
        s = pl.program_id(1)
        i, j = it_ref[s], jt_ref[s]

        @pl.when(j == 0)
        def _():
            m_sc[...] = jnp.full(m_sc.shape, NEG, F32)
            acc_sc[...] = jnp.zeros(acc_sc.shape, F32)

        hq = bq // 2

        def sweep(masked):
            def span(half):
                return slice(half * hq, (half + 1) * hq), (slice(0, hq) if masked and half == 0 else slice(0, bq))

            def scores(h, half):
                qs, ks = span(half)
                return lax.dot_general(k_ref[h, ks, :], q_ref[h, qs, :], NT, preferred_element_type=F32)

            def accumulate(h, half, pt, rescale):
                qs, ks = span(half)
                acc_sc[h, :, qs] = rescale * acc_sc[h, :, qs] + jnp.dot(vt_ref[h, :, ks], pt, preferred_element_type=F32)

            units = [(h, half) for h in range(nh) for half in range(2)]
            ahead, behind = scores(*units[0]), None
            for n, (h, half) in enumerate(units):
                st = ahead
                if n + 1 < len(units):
                    ahead = scores(*units[n + 1])
                if behind is not None:
                    accumulate(*behind)
                if masked:
                    key = lax.broadcasted_iota(jnp.int32, st.shape, 0)
                    qry = lax.broadcasted_iota(jnp.int32, st.shape, 1) + half * hq
                    st = jnp.where(key <= qry, st, NEG)
                qs = span(half)[0]
                m_prev = m_sc[h, :, qs]
                m_new = jnp.maximum(m_prev, jnp.max(st, axis=0, keepdims=True))
                behind = (h, half, jnp.exp(st - m_new).astype(BF16), jnp.exp(m_prev - m_new))
                m_sc[h, :, qs] = m_new
            accumulate(*behind)

        @pl.when(j < i)
        def _():
            sweep(False)

        @pl.when(j == i)
        def _():
            sweep(True)
            for h in range(nh):
                acc = acc_sc[h]
                denom = acc[V_ONE:V_ONE + 1, :]
                o_ref[:, h * HEAD_DIM:(h + 1) * HEAD_DIM] = (acc[:HEAD_DIM, :] / denom).T.astype(BF16)
                lse_ref[h] = m_sc[h] + jnp.log(denom)

    grid_spec = pltpu.PrefetchScalarGridSpec(
        num_scalar_prefetch=2, grid=(FOX_HEADS // nh, i_tab.shape[0]),
        in_specs=[pl.BlockSpec((nh, bq, LANES), lambda hp, s, it, jt: (hp, it[s], 0)),
                  pl.BlockSpec((nh, bq, LANES), lambda hp, s, it, jt: (hp, jt[s], 0)),
                  pl.BlockSpec((nh, LANES, bq), lambda hp, s, it, jt: (hp, 0, jt[s]))],
        out_specs=[pl.BlockSpec((bq, nh * HEAD_DIM), lambda hp, s, it, jt: (it[s], hp)),
                   pl.BlockSpec((nh, 1, bq), lambda hp, s, it, jt: (hp, 0, it[s]))],
        scratch_shapes=[pltpu.VMEM((nh, 1, bq), F32), pltpu.VMEM((nh, LANES, bq), F32)])
    return pl.pallas_call(body, grid_spec=grid_spec,
                          out_shape=[_sds((t, FOX_WIDTH), BF16), _sds((FOX_HEADS, 1, t), F32)],
                          compiler_params=_cp(), name="attn_fwd")(i_tab, j_tab, qp, kp, vt)


def _even_out(attn, bch, conv_w, w_out, x, g, b):
    t, d = x.shape
    tm = min(ROW_TILE, t)
    halo_blocks = tm // SUBLANES
    cw = CONV_WIDTH

    def body(a_ref, cur_ref, prev_ref, cw_ref, wo_ref, x_ref, g_ref, b_ref, conv_ref, yb_ref, xh_ref, rs_ref):
        i = pl.program_id(0)
        z = cur_ref[:, cw:2 * cw] * cur_ref[:, 2 * cw:]
        zp = jnp.where(i == 0, 0.0, prev_ref[:, cw:2 * cw] * prev_ref[:, 2 * cw:])
        z1, z2 = _shift_down(z, zp)
        conv = (cur_ref[:, :cw] * (cw_ref[0:1, :] * z2 + cw_ref[1:2, :] * z1 + cw_ref[2:3, :] * z)).astype(BF16)
        conv_ref[...] = conv
        pre = (ALPHA * x_ref[...] + jnp.dot(a_ref[...], wo_ref[:FOX_WIDTH, :], preferred_element_type=F32)
               + jnp.dot(conv, wo_ref[FOX_WIDTH:, :], preferred_element_type=F32))
        xhat, rstd = _ln_fwd(pre)
        yb_ref[...] = (xhat * g_ref[...] + b_ref[...]).astype(BF16)
        xh_ref[...] = xhat
        rs_ref[...] = rstd

    row = lambda i: (i, 0)
    full, half = pl.BlockSpec((tm, d), row), pl.BlockSpec((tm, cw), row)
    return pl.pallas_call(
        body, grid=(t // tm,),
        in_specs=[half, pl.BlockSpec((tm, BCH), row),
                  pl.BlockSpec((SUBLANES, BCH), lambda i: (jnp.maximum(i * halo_blocks - 1, 0), 0)),
                  _resident(conv_w.shape), _resident(w_out.shape), full, _resident(g.shape), _resident(b.shape)],
        out_specs=[half, full, full, pl.BlockSpec((tm, 1), row)],
        out_shape=[_sds((t, cw), BF16), _sds((t, d), BF16), _sds((t, d), F32), _sds((t, 1), F32)],
        compiler_params=_cp(), name="even_out")(attn, bch, bch, conv_w, w_out, x, g, b)


def _gmlp_fwd(x, w_in, vg, vb, wm, bs_col, w_out, res_ln, g, b):
    t, d = x.shape
    tm = min(ROW_TILE, t)
    gb = GMLP_BLOCK
    rxh, rg, rb = res_ln

    def body(x_ref, w_ref, vg_ref, vb_ref, wm_ref, bs_ref, wo_ref, rxh_ref, rg_ref, rb_ref, g_ref, b_ref,
             sv_ref, rs_ref, o_ref, yb_ref, xh_ref, rsy_ref, a_sc):
        xb = x_ref[...].astype(BF16)
        nc = w_ref.shape[2]
        for j in range(w_ref.shape[0]):
            a_sc[:, j * nc:(j + 1) * nc] = jnp.dot(xb, w_ref[j], preferred_element_type=F32)
        halves = []
        for half in range(2):
            a = a_sc[:, half * d:(half + 1) * d]
            cdf = 0.5 * (1.0 + lax.erf(a * (2.0 ** -0.5)))
            halves.append(a * cdf)
            slope = cdf + a * (jnp.exp(-0.5 * a * a) * (1.0 / math.sqrt(2.0 * math.pi)))
            sv_ref[:, (2 * half + 1) * d:(2 * half + 2) * d] = slope.astype(BF16)
        u = halves[0]
        vhat, rstd = _ln_fwd(halves[1])
        sv_ref[:, :d] = u.astype(BF16)
        sv_ref[:, 2 * d:3 * d] = vhat.astype(BF16)
        rs_ref[...] = rstd
        vln = (vhat * vg_ref[...] + vb_ref[...]).astype(BF16)
        for blk in range(tm // gb):
            rs = slice(blk * gb, (blk + 1) * gb)
            for gi in range(GMLP_GROUPS):
                cs = slice(gi * gb, (gi + 1) * gb)
                s = jnp.dot(wm_ref[gi], vln[rs, cs], preferred_element_type=F32) + bs_ref[:, gi:gi + 1]
                o_ref[rs, cs] = (u[rs, cs] * s).astype(BF16)
        z = ALPHA * (rxh_ref[...] * rg_ref[...] + rb_ref[...]) + jnp.dot(o_ref[...], wo_ref[...], preferred_element_type=F32)
        xhat, rstd_y = _ln_fwd(z)
        yb_ref[...] = (xhat * g_ref[...] + b_ref[...]).astype(BF16)
        xh_ref[...] = xhat
        rsy_ref[...] = rstd_y

    row = lambda i: (i, 0)
    full, col, vec = pl.BlockSpec((tm, d), row), pl.BlockSpec((tm, 1), row), _resident(g.shape)
    return pl.pallas_call(
        body, grid=(t // tm,),
        in_specs=[full, _resident(w_in.shape), _resident(vg.shape), _resident(vb.shape),
                  _resident(wm.shape), _resident(bs_col.shape), _resident(w_out.shape), full, vec, vec, vec, vec],
        out_specs=[pl.BlockSpec((tm, 4 * d), row), col, full, full, full, col],
        out_shape=[_sds((t, 4 * d), BF16), _sds((t, 1), F32), _sds((t, d), BF16), _sds((t, d), BF16), _sds((t, d), F32),
                   _sds((t, 1), F32)],
        scratch_shapes=[pltpu.VMEM((tm, 2 * d), F32)],
        compiler_params=_cp(), name="gmlp_fwd")(x, w_in, vg, vb, wm, bs_col, w_out, rxh, rg, rb, g, b)


def _mm_back(pairs, wt, res, after, name):
    t = pairs[0][0].shape[0]
    k = wt.shape[1]
    tm = min(ROW_TILE, t)
    n = len(pairs)

    def body(after_ref, *refs):
        a_refs, w_ref, res_ref, o_ref = refs[:n], refs[n], refs[n + 1], refs[n + 2]
        dx = ALPHA * res_ref[...]
        for a_ref, (_, lo, hi) in zip(a_refs, pairs):
            dx = dx + jnp.dot(a_ref[...].astype(BF16), w_ref[lo:hi, :], preferred_element_type=F32)
        o_ref[...] = dx

    row = lambda i: (i, 0)
    return pl.pallas_call(
        body, grid=(t // tm,),
        in_specs=[_ANY_SPEC] + [pl.BlockSpec((tm, a.shape[1]), row) for a, _, _ in pairs]
        + [_resident(wt.shape), pl.BlockSpec((tm, k), row)],
        out_specs=pl.BlockSpec((tm, k), row), out_shape=_sds((t, k), F32),
        compiler_params=_cp(), name=name)(after, *[a for a, _, _ in pairs], wt, res)


def _mm_tn(a, b, name, *, tn, tk=None, tt=None, stack_cols=False, out_dtype=BF16, after=None):
    t, k = a.shape
    n = b.shape[1]
    tk = k if tk is None else tk
    tt = min(REDUCE_TILE if tt is None else tt, t)
    nt = t // tt

    def body(a_ref, b_ref, *rest):
        o_ref, acc_ref = rest[after is not None:]
        s = pl.program_id(2)
        part = lax.dot_general(a_ref[...].astype(BF16), b_ref[...].astype(BF16), TN, preferred_element_type=F32)
        _accumulate(acc_ref, s == 0, part)

        @pl.when(s == nt - 1)
        def _():
            o_ref[...] = acc_ref[...].astype(out_dtype).reshape(o_ref.shape)

    if stack_cols:
        assert tk == k
        out_spec = pl.BlockSpec((1, k, tn), lambda kk, j, s: (j, 0, 0))
        out_shape = _sds((n // tn, k, tn), out_dtype)
    else:
        out_spec = pl.BlockSpec((tk, tn), lambda kk, j, s: (kk, j))
        out_shape = _sds((k, n), out_dtype)
    return pl.pallas_call(
        body, grid=(k // tk, n // tn, nt),
        in_specs=[pl.BlockSpec((tt, tk), lambda kk, j, s: (s, kk)), pl.BlockSpec((tt, tn), lambda kk, j, s: (s, j))]
        + ([_ANY_SPEC] if after is not None else []),
        out_specs=out_spec, out_shape=out_shape,
        scratch_shapes=[pltpu.VMEM((tk, tn), F32)],
        compiler_params=_cp(), name=name)(a, b, *([after] if after is not None else []))


def _mm_tn_pair(a1, a2, b, name):
    t, k = a1.shape
    n = b.shape[1]
    tt = min(REDUCE_TILE, t)
    nt = t // tt

    def body(a1_ref, a2_ref, b_ref, o_ref, acc_ref):
        s = pl.program_id(0)
        bb = b_ref[...].astype(BF16)
        part = jnp.concatenate([lax.dot_general(a_ref[...].astype(BF16), bb, TN, preferred_element_type=F32)
                                for a_ref in (a1_ref, a2_ref)], axis=0)
        _accumulate(acc_ref, s == 0, part)

        @pl.when(s == nt - 1)
        def _():
            o_ref[...] = acc_ref[...].astype(BF16)

    rows = pl.BlockSpec((tt, k), lambda s: (s, 0))
    return pl.pallas_call(
        body, grid=(nt,),
        in_specs=[rows, rows, pl.BlockSpec((tt, n), lambda s: (s, 0))],
        out_specs=pl.BlockSpec((2 * k, n), lambda s: (0, 0)), out_shape=_sds((2 * k, n), BF16),
        scratch_shapes=[pltpu.VMEM((2 * k, n), F32)],
        compiler_params=_cp(), name=name)(a1, a2, b)


def _ffn_bwd_rows(dz, wo, gu, wi, ln_below, name):
    t, d = dz.shape
    tm = min(FFN_FUSED_ROW_TILE, t)
    hh = HALF_HIDDEN
    xhat, rstd, g = ln_below

    def body(dz_ref, wo_ref, gu_ref, wi_ref, xh_ref, rs_ref, g_ref, dgu_ref, dzb_ref, dg_ref, db_ref):
        first = pl.program_id(0) == 0
        a = dz_ref[...].astype(BF16)
        for c in range(2):
            gs, us = slice(c * hh, (c + 1) * hh), slice(FFN_HIDDEN + c * hh, FFN_HIDDEN + (c + 1) * hh)
            dh = lax.dot_general(a, wo_ref[gs, :], NT, preferred_element_type=F32)
            dgu_ref[:, gs] = (dh * gu_ref[:, gs].astype(F32)).astype(BF16)
            dgu_ref[:, us] = (dh * gu_ref[:, us].astype(F32)).astype(BF16)
        dx = ALPHA * dz_ref[...]
        for j in range(wi_ref.shape[0]):
            dx = dx + lax.dot_general(dgu_ref[:, j * hh:(j + 1) * hh], wi_ref[j], NT, preferred_element_type=F32)
        dzb, dg, db = _ln_bwd(dx, xh_ref[...], rs_ref[...], g_ref[...])
        dzb_ref[...] = dzb
        _accumulate(dg_ref, first, dg)
        _accumulate(db_ref, first, db)

    row = lambda i: (i, 0)
    wide, full = pl.BlockSpec((tm, 2 * FFN_HIDDEN), row), pl.BlockSpec((tm, d), row)
    vec = pl.BlockSpec((1, d), lambda i: (0, 0))
    return pl.pallas_call(
        body, grid=(t // tm,),
        in_specs=[full, _resident(wo.shape), wide, _resident(wi.shape), full, pl.BlockSpec((tm, 1), row),
                  _resident(g.shape)],
        out_specs=[wide, full, vec, vec],
        out_shape=[_sds((t, 2 * FFN_HIDDEN), BF16), _sds((t, d), F32), _sds((1, d), F32), _sds((1, d), F32)],
        compiler_params=_cp(ATT_BWD_VMEM_LIMIT), name=name)(dz, wo, gu, wi, xhat, rstd, g)


def _gmlp_bwd(dz, w_out, saved, rstd_v, vg, vb, wm, bs_col, w_in, ln_below):
    t, d = dz.shape
    d2 = 2 * d
    tm = min(ROW_TILE, t)
    gb = GMLP_BLOCK
    xhat_below, rstd_below, g_below = ln_below

    def body(dz_ref, wo_ref, sv_ref, rs_ref, vg_ref, vb_ref, wm_ref, bs_ref, wi_ref, xh_ref, rsb_ref, gb_ref,
             da_ref, dws_ref, dbs_ref, dvg_ref, dvb_ref, dzb_ref, dg_ref, db_ref, dvln_sc):
        first = pl.program_id(0) == 0
        u = sv_ref[:, :d].astype(F32)
        vhat = sv_ref[:, 2 * d:3 * d].astype(F32)
        rstd = rs_ref[...]
        vln = (vhat * vg_ref[...] + vb_ref[...]).astype(BF16)
        dgate = lax.dot_general(dz_ref[...].astype(BF16), wo_ref[...], NT, preferred_element_type=F32)

        @pl.when(first)
        def _():
            dws_ref[...] = jnp.zeros(dws_ref.shape, F32)
            dbs_ref[...] = jnp.zeros(dbs_ref.shape, F32)

        for blk in range(tm // gb):
            rs = slice(blk * gb, (blk + 1) * gb)
            for gi in range(GMLP_GROUPS):
                cs = slice(gi * gb, (gi + 1) * gb)
                vblk = vln[rs, cs]
                s = jnp.dot(wm_ref[gi], vblk, preferred_element_type=F32) + bs_ref[:, gi:gi + 1]
                dgb = dgate[rs, cs]
                da_ref[rs, cs] = (dgb * s * sv_ref[rs, d + gi * gb:d + (gi + 1) * gb].astype(F32)).astype(BF16)
                ds = dgb * u[rs, cs]
                dsb = ds.astype(BF16)
                dws_ref[gi] += lax.dot_general(dsb, vblk, NT, preferred_element_type=F32)
                dbs_ref[:, gi:gi + 1] += jnp.sum(ds, axis=1, keepdims=True)
                dvln_sc[rs, cs] = lax.dot_general(wm_ref[gi], dsb, TN, preferred_element_type=F32)
        dv, dvg, dvb = _ln_bwd(dvln_sc[...], vhat, rstd, vg_ref[...])
        da_ref[:, d:] = (dv * sv_ref[:, 3 * d:].astype(F32)).astype(BF16)
        _accumulate(dvg_ref, first, dvg)
        _accumulate(dvb_ref, first, dvb)
        dx = ALPHA * dz_ref[...]
        nc = wi_ref.shape[2]
        for j in range(wi_ref.shape[0]):
            dx = dx + lax.dot_general(da_ref[:, j * nc:(j + 1) * nc], wi_ref[j], NT, preferred_element_type=F32)
        dzb, dg, db = _ln_bwd(dx, xh_ref[...], rsb_ref[...], gb_ref[...])
        dzb_ref[...] = dzb
        _accumulate(dg_ref, first, dg)
        _accumulate(db_ref, first, db)

    row = lambda i: (i, 0)
    full, col = pl.BlockSpec((tm, d), row), pl.BlockSpec((tm, 1), row)
    vec = pl.BlockSpec((1, d), lambda i: (0, 0))
    return pl.pallas_call(
        body, grid=(t // tm,),
        in_specs=[full, _resident(w_out.shape), pl.BlockSpec((tm, 4 * d), row), col,
                  _resident(vg.shape), _resident(vb.shape), _resident(wm.shape), _resident(bs_col.shape),
                  _resident(w_in.shape), full, col, _resident(g_below.shape)],
        out_specs=[pl.BlockSpec((tm, d2), row), pl.BlockSpec(wm.shape, lambda i: (0, 0, 0)),
                   pl.BlockSpec(bs_col.shape, lambda i: (0, 0)), vec, vec, full, vec, vec],
        out_shape=[_sds((t, d2), BF16), _sds(wm.shape, F32), _sds(bs_col.shape, F32), _sds((1, d), F32), _sds((1, d), F32),
                   _sds((t, d), F32), _sds((1, d), F32), _sds((1, d), F32)],
        scratch_shapes=[pltpu.VMEM((tm, d), F32)],
        compiler_params=_cp(), name="gmlp_bwd")(dz, w_out, saved, rstd_v, vg, vb, wm, bs_col, w_in, xhat_below,
                                                rstd_below, g_below)


def _conv_bwd(bch, dconv, conv_w):
    t = bch.shape[0]
    tm = min(ROW_TILE, t)
    nb = t // tm
    halo_blocks = tm // SUBLANES
    cw = CONV_WIDTH

    def body(cur_ref, prev_ref, next_ref, dc_ref, dn_ref, w_ref, o_ref, dw_ref):
        i = pl.program_id(0)
        bgate, cgate, hval = cur_ref[:, :cw], cur_ref[:, cw:2 * cw], cur_ref[:, 2 * cw:]
        z = cgate * hval
        zp = jnp.where(i == 0, 0.0, prev_ref[:, cw:2 * cw] * prev_ref[:, 2 * cw:])
        z1, z2 = _shift_down(z, zp)
        w0, w1, w2 = w_ref[0:1, :], w_ref[1:2, :], w_ref[2:3, :]
        dconv = dc_ref[...]
        o_ref[:, :cw] = (dconv * (w0 * z2 + w1 * z1 + w2 * z)).astype(BF16)
        dy = dconv * bgate
        dyn = jnp.where(i == nb - 1, 0.0, dn_ref[...] * next_ref[:, :cw])
        dy1, dy2 = _shift_up(dy, dyn)
        dz = w2 * dy + w1 * dy1 + w0 * dy2
        o_ref[:, cw:2 * cw] = (dz * hval).astype(BF16)
        o_ref[:, 2 * cw:] = (dz * cgate).astype(BF16)

        @pl.when(i == 0)
        def _():
            dw_ref[...] = jnp.zeros(dw_ref.shape, F32)

        for tap, zs in enumerate((z2, z1, z)):
            dw_ref[tap:tap + 1, :] += jnp.sum(dy * zs, axis=0, keepdims=True)

    last_halo = t // SUBLANES - 1
    return pl.pallas_call(
        body, grid=(nb,),
        in_specs=[pl.BlockSpec((tm, BCH), lambda i: (i, 0)),
                  pl.BlockSpec((SUBLANES, BCH), lambda i: (jnp.maximum(i * halo_blocks - 1, 0), 0)),
                  pl.BlockSpec((SUBLANES, BCH), lambda i: (jnp.minimum((i + 1) * halo_blocks, last_halo), 0)),
                  pl.BlockSpec((tm, cw), lambda i: (i, 0)),
                  pl.BlockSpec((SUBLANES, cw), lambda i: (jnp.minimum((i + 1) * halo_blocks, last_halo), 0)),
                  _resident(conv_w.shape)],
        out_specs=[pl.BlockSpec((tm, BCH), lambda i: (i, 0)), pl.BlockSpec((SUBLANES, cw), lambda i: (0, 0))],
        out_shape=[_sds((t, BCH), BF16), _sds((SUBLANES, cw), F32)],
        compiler_params=_cp(), name="conv_bwd")(bch, bch, bch, dconv, dconv, conv_w)


def _attn_bwd_prep(dz, w_out, o, qp, lse_pad, after):
    t = o.shape[0]
    tm = min(ROW_TILE, t)
    hd = HEAD_DIM
    sel_lse = _piece_selector(Q_LSE, -1.0)
    sel_delta = _piece_selector(DO_DELTA, -1.0)
    head_of = jnp.asarray([[1.0 if col == row // hd else 0.0 for col in range(LANES)] for row in range(FOX_WIDTH)], F32)

    def body(after_ref, dz_ref, wo_ref, o_ref, qp_ref, lse_ref, sl_ref, sd_ref, seg_ref, qb_ref, dob_ref, dconv_ref):
        dzb = dz_ref[...].astype(BF16)
        do = lax.dot_general(dzb, wo_ref[:FOX_WIDTH, :], NT, preferred_element_type=F32)
        dconv_ref[...] = lax.dot_general(dzb, wo_ref[FOX_WIDTH:, :], NT, preferred_element_type=F32)
        delta = jnp.dot(o_ref[...].astype(F32) * do, seg_ref[...], precision=HIGHEST, preferred_element_type=F32)
        lse_extra = jnp.dot(_piece_rows(lse_ref[...]), sl_ref[...], preferred_element_type=F32)
        do_extra = jnp.dot(_piece_rows(delta), sd_ref[...], preferred_element_type=F32).astype(BF16)
        for h in range(FOX_HEADS):
            hs = slice(h * hd, (h + 1) * hd)
            dob_ref[h, :, :hd] = do[:, hs].astype(BF16)
            dob_ref[h, :, hd:] = do_extra[:, hs]
            qb_ref[h, :, :hd] = qp_ref[h, :, :hd]
            qb_ref[h, :, hd:] = (qp_ref[h, :, hd:].astype(F32) + lse_extra[:, hs]).astype(BF16)

    row = lambda i: (i, 0)
    row3 = pl.BlockSpec((FOX_HEADS, tm, LANES), lambda i: (0, i, 0))
    half = pl.BlockSpec((tm, FOX_WIDTH), row)
    return pl.pallas_call(
        body, grid=(t // tm,),
        in_specs=[_ANY_SPEC, pl.BlockSpec((tm, dz.shape[1]), row), _resident(w_out.shape), half, row3,
                  pl.BlockSpec((tm, LANES), row), _resident(sel_lse.shape), _resident(sel_delta.shape),
                  _resident(head_of.shape)],
        out_specs=[row3, row3, half],
        out_shape=[_sds((FOX_HEADS, t, LANES), BF16)] * 2 + [_sds((t, FOX_WIDTH), F32)],
        compiler_params=_cp(), name="attn_bwd_prep")(after, dz, w_out, o, qp, lse_pad, sel_lse, sel_delta, head_of)


def _attn_bwd(qb, kp, vp, dob, kt):
    t = qb.shape[1]
    bq = min(ATT_BLOCK, t)
    nq = t // bq
    i_tab, j_tab = _triangle(nq, key_major=True)

    def body(it_ref, jt_ref, q_ref, k_ref, v_ref, do_ref, kt_ref, dqt_ref, dk_ref, dv_ref, dk_sc, dv_sc):
        s = pl.program_id(1)
        i, j = it_ref[s], jt_ref[s]

        @pl.when(s == 0)
        def _():
            dqt_ref[...] = jnp.zeros(dqt_ref.shape, F32)

        @pl.when(i == j)
        def _():
            dk_sc[...] = jnp.zeros(dk_sc.shape, F32)
            dv_sc[...] = jnp.zeros(dv_sc.shape, F32)

        hq = bq // 2

        def sweep(masked):
            def span(half):
                return slice(half * hq, (half + 1) * hq), (slice(0, hq) if masked and half == 0 else slice(0, bq))

            def scores(h, half):
                qs, ks = span(half)
                return (lax.dot_general(k_ref[h, ks, :], q_ref[h, qs, :], NT, preferred_element_type=F32),
                        lax.dot_general(v_ref[h, ks, :], do_ref[h, qs, :], NT, preferred_element_type=F32))

            def accumulate(h, half, ptb, dstb):
                qs, ks = span(half)
                cols = pl.ds(pl.multiple_of(i * bq + half * hq, hq), hq)
                dv_sc[h, ks, :] += jnp.dot(ptb, do_ref[h, qs, :], preferred_element_type=F32)
                dk_sc[h, ks, :] += jnp.dot(dstb, q_ref[h, qs, :], preferred_element_type=F32)
                dqt_ref[h, :, cols] += jnp.dot(kt_ref[h, :, ks], dstb, preferred_element_type=F32)

            units = [(h, half) for h in range(ATT_BWD_HEADS) for half in range(2)]
            ahead, behind = scores(*units[0]), None
            for n, (h, half) in enumerate(units):
                st, dpt = ahead
                if n + 1 < len(units):
                    ahead = scores(*units[n + 1])
                if behind is not None:
                    accumulate(*behind)
                if masked:
                    key = lax.broadcasted_iota(jnp.int32, st.shape, 0)
                    qry = lax.broadcasted_iota(jnp.int32, st.shape, 1) + half * hq
                    st = jnp.where(key <= qry, st, NEG)
                pt = jnp.exp(st)
                behind = (h, half, pt.astype(BF16), (pt * dpt).astype(BF16))
            accumulate(*behind)

        @pl.when(i == j)
        def _():
            sweep(True)

        @pl.when(i > j)
        def _():
            sweep(False)

        @pl.when(i == nq - 1)
        def _():
            dk_ref[...] = dk_sc[...]
            dv_ref[...] = dv_sc[...].astype(BF16)

    nh = ATT_BWD_HEADS
    qblk = pl.BlockSpec((nh, bq, LANES), lambda hp, s, it, jt: (hp, it[s], 0))
    kblk = pl.BlockSpec((nh, bq, LANES), lambda hp, s, it, jt: (hp, jt[s], 0))
    grid_spec = pltpu.PrefetchScalarGridSpec(
        num_scalar_prefetch=2, grid=(FOX_HEADS // nh, i_tab.shape[0]),
        in_specs=[qblk, kblk, kblk, qblk, pl.BlockSpec((nh, LANES, bq), lambda hp, s, it, jt: (hp, 0, jt[s]))],
        out_specs=[pl.BlockSpec((nh, LANES, t), lambda hp, s, it, jt: (hp, 0, 0), pipeline_mode=pl.Buffered(1)),
                   kblk, kblk],
        scratch_shapes=[pltpu.VMEM((nh, bq, LANES), F32), pltpu.VMEM((nh, bq, LANES), F32)])
    return pl.pallas_call(body, grid_spec=grid_spec,
                          out_shape=[_sds((FOX_HEADS, LANES, t), F32), _sds((FOX_HEADS, t, LANES), F32),
                                     _sds((FOX_HEADS, t, LANES), BF16)],
                          compiler_params=_cp(ATT_BWD_VMEM_LIMIT), name="attn_bwd")(i_tab, j_tab, qb, kp, vp, dob, kt)


def _attn_unpack(dqt, dkp, dvp):
    t = dkp.shape[1]
    tm = min(LAYOUT_ROW_TILE, t)
    hd = HEAD_DIM

    def body(dqt_ref, dk_ref, dv_ref, o_ref, dc_ref):
        for h in range(FOX_HEADS):
            dq = dqt_ref[h].T
            o_ref[:, h * hd:(h + 1) * hd] = (dq[:, :hd] * (hd ** -0.5)).astype(BF16)
            o_ref[:, FOX_WIDTH + h * hd:FOX_WIDTH + (h + 1) * hd] = dk_ref[h, :, :hd].astype(BF16)
            o_ref[:, 2 * FOX_WIDTH + h * hd:2 * FOX_WIDTH + (h + 1) * hd] = dv_ref[h, :, :hd]
            dc_ref[:, h:h + 1] = dq[:, K_ONE:K_ONE + 1] - dk_ref[h, :, Q_ONE:Q_ONE + 1]

    row3 = pl.BlockSpec((FOX_HEADS, tm, LANES), lambda i: (0, i, 0))
    return pl.pallas_call(
        body, grid=(t // tm,),
        in_specs=[pl.BlockSpec((FOX_HEADS, LANES, tm), lambda i: (0, 0, i)), row3, row3],
        out_specs=[pl.BlockSpec((tm, QKV), lambda i: (i, 0)), pl.BlockSpec((tm, FOX_HEADS), lambda i: (i, 0))],
        out_shape=[_sds((t, QKV), BF16), _sds((t, FOX_HEADS), F32)],
        compiler_params=_cp(), name="attn_unpack")(dqt, dkp, dvp)


def _adamw(parts, w, m, v, name, layer=None, into=None):
    nl, r, c = w.shape
    fits = [cand for cand in [*range(SUBLANES, r, SUBLANES), r] if r % cand == 0 and cand * c * 4 <= ADAMW_BLOCK_BYTES]
    tr = max(fits) if fits else r
    npart = len(parts)
    bc1 = 1.0 - ADAM_B1 ** ADAM_STEP
    bc2 = 1.0 - ADAM_B2 ** ADAM_STEP

    def body(*refs):
        p_refs = refs[:npart]
        w_ref, m_ref, v_ref = refs[npart:npart + 3]
        g_ref, d_ref, nm_ref, nv_ref = refs[-4:]
        sums = []
        for p_ref in p_refs:
            acc = p_ref[0, 0].astype(F32)
            for s in range(1, p_ref.shape[0]):
                acc = acc + p_ref[s, 0].astype(F32)
            sums.append(acc)
        g = sums[0]
        for extra in sums[1:]:
            g = g + extra
        nm = ADAM_B1 * m_ref[0] + (1.0 - ADAM_B1) * g
        nv = ADAM_B2 * v_ref[0] + (1.0 - ADAM_B2) * (g * g)
        m_hat = nm / bc1
        v_hat = nv / bc2
        g_ref[0] = g
        d_ref[0] = -ADAM_LR * (m_hat / (jnp.sqrt(v_hat) + ADAM_EPS) + ADAM_WD * w_ref[0])
        nm_ref[0] = nm
        nv_ref[0] = nv

    first = 0 if layer is None else layer
    blk = pl.BlockSpec((1, tr, c), lambda l, i: (first + l, i, 0))
    extra = [] if into is None else list(into)
    return pl.pallas_call(
        body, grid=(nl if layer is None else 1, r // tr),
        in_specs=[pl.BlockSpec((p.shape[0], 1, tr, c), lambda l, i: (0, l, i, 0)) for p in parts] + [blk, blk, blk]
        + [_ANY_SPEC] * len(extra),
        out_specs=[blk] * 4, out_shape=[_sds(w.shape, F32)] * 4,
        input_output_aliases={npart + 3 + k: k for k in range(len(extra))},
        compiler_params=_cp(), name=name)(*parts, w, m, v, *extra)


def _to_rows(a):
    flat = a.reshape(-1)
    pad = (-flat.shape[0]) % LANES
    if pad:
        flat = jnp.concatenate([flat, jnp.zeros((pad,), flat.dtype)])
    return flat.reshape(-1, LANES)


def _ffn_fwd(xin_ln, xin_b, wi, wo, g, b, layer, target=None):
    t, d = xin_b.shape
    tm = min(FFN_FUSED_ROW_TILE, t)
    hh = HALF_HIDDEN
    rxh, rg, rb = xin_ln

    def body(x_ref, wi_ref, wo_ref, rxh_ref, rg_ref, rb_ref, g_ref, b_ref, *rest):
        gu_ref, h_ref = rest[target is not None:][:2]
        a = x_ref[...]
        for c in range(2):
            gs, us = slice(c * hh, (c + 1) * hh), slice(FFN_HIDDEN + c * hh, FFN_HIDDEN + (c + 1) * hh)
            gate = jnp.dot(a, wi_ref[c], preferred_element_type=F32)
            up = jnp.dot(a, wi_ref[2 + c], preferred_element_type=F32)
            sig = _sigmoid(gate)
            silu = gate * sig
            gu_ref[:, gs] = (up * sig * (1.0 + gate * (1.0 - sig))).astype(BF16)
            gu_ref[:, us] = silu.astype(BF16)
            h_ref[:, gs] = (silu * up).astype(BF16)
        z = ALPHA * (rxh_ref[...] * rg_ref[...] + rb_ref[...]) + jnp.dot(h_ref[...], wo_ref[...], preferred_element_type=F32)
        xhat, rstd = _ln_fwd(z)
        if target is None:
            yb_ref, xh_ref, rs_ref = rest[2:]
            yb_ref[...] = (xhat * g_ref[...] + b_ref[...]).astype(BF16)
            xh_ref[...] = xhat
            rs_ref[...] = rstd
            return
        sq_ref, dz_ref, dg_ref, db_ref = rest[3:]
        first = pl.program_id(0) == 0
        err = xhat * g_ref[...] + b_ref[...] - rest[0][...]
        dz, dg, db = _ln_bwd(err * (1.0 / d), xhat, rstd, g_ref[...])
        dz_ref[...] = dz
        _accumulate(sq_ref, first, jnp.sum(err * err, axis=0, keepdims=True))
        _accumulate(dg_ref, first, dg)
        _accumulate(db_ref, first, db)

    row = lambda i: (i, 0)
    full = pl.BlockSpec((tm, d), row)
    vec = _resident(g.shape)
    acc = pl.BlockSpec((1, d), lambda i: (0, 0))
    in_specs = [full, _resident(wi.shape), _resident(wo.shape), full, vec, vec, vec, vec]
    out_specs = [pl.BlockSpec((tm, 2 * FFN_HIDDEN), row), pl.BlockSpec((tm, FFN_HIDDEN), row)]
    out_shape = [_sds((t, 2 * FFN_HIDDEN), BF16), _sds((t, FFN_HIDDEN), BF16)]
    args = [xin_b, wi, wo, rxh, rg, rb, g, b]
    if target is None:
        out_specs += [full, full, pl.BlockSpec((tm, 1), row)]
        out_shape += [_sds((t, d), BF16), _sds((t, d), F32), _sds((t, 1), F32)]
    else:
        in_specs.append(full)
        args.append(target)
        out_specs += [acc, full, acc, acc]
        out_shape += [_sds((1, d), F32), _sds((t, d), F32), _sds((1, d), F32), _sds((1, d), F32)]
    gu, h, *tail = pl.pallas_call(body, grid=(t // tm,), in_specs=in_specs, out_specs=out_specs, out_shape=out_shape,
                                  compiler_params=_cp(ATT_BWD_VMEM_LIMIT), name=f"ffn_fwd_rows_{layer}")(*args)
    if target is None:
        y_b, xhat, rstd = tail
        return y_b, (xin_b, gu, h, xhat, rstd)
    return tail, (xin_b, gu, h)


def _ffn_bwd(dz, saved, wi, wo, ln_below, layer):
    xin_b, gu, h = saved[:3]
    dgu, *below = _ffn_bwd_rows(dz, wo, gu, wi, ln_below, f"ffn_bwd_rows_{layer}")
    g_out = _mm_tn(h, dz, f"ffn_dw_out_{layer}", tn=D_MODEL, tk=HALF_HIDDEN)
    g_in = _mm_tn(xin_b, dgu, f"ffn_dw_in_{layer}", tn=HALF_HIDDEN, stack_cols=True)
    return below, g_in, g_out.reshape(N_CHIPS, FFN_HIDDEN // N_CHIPS, D_MODEL)


def kernel(x, even_w_in, even_b_f, even_conv_w, even_w_out, odd_w_in, odd_v_ln_g, odd_v_ln_b, odd_w_s, odd_b_s, odd_w_out, mix_ln_g, mix_ln_b, ffn_w_in, ffn_w_out, ffn_ln_g, ffn_ln_b, loss_target, m_even_w_in, m_even_b_f, m_even_conv_w, m_even_w_out, m_odd_w_in, m_odd_v_ln_g, m_odd_v_ln_b, m_odd_w_s, m_odd_b_s, m_odd_w_out, m_mix_ln_g, m_mix_ln_b, m_ffn_w_in, m_ffn_w_out, m_ffn_ln_g, m_ffn_ln_b, v_even_w_in, v_even_b_f, v_even_conv_w, v_even_w_out, v_odd_w_in, v_odd_v_ln_g, v_odd_v_ln_b, v_odd_w_s, v_odd_b_s, v_odd_w_out, v_mix_ln_g, v_mix_ln_b, v_ffn_w_in, v_ffn_w_out, v_ffn_ln_g, v_ffn_ln_b):
    t = x.shape[1]
    d = D_MODEL
    chip = 2 * lax.axis_index("x") + lax.axis_index("y")
    x2d = x[0]
    target = loss_target[0]

    small_shard = jnp.concatenate([odd_v_ln_g.reshape(2, LANES), odd_v_ln_b.reshape(2, LANES),
                                   even_conv_w.reshape(CONV_K, LANES), jnp.zeros((1, LANES), F32)], axis=0)
    first = [jnp.swapaxes(even_w_in[0], 0, 1).astype(BF16)]
    second = [even_w_out[0].astype(BF16), small_shard]
    later = [odd_w_in[0].astype(BF16), odd_w_out[0].astype(BF16), ffn_w_in[0].astype(BF16), ffn_w_in[1].astype(BF16),
             ffn_w_out[0].astype(BF16), ffn_w_out[1].astype(BF16)]
    first_h, first_tok = _split_start(first, "gather4", "gather_first_start")
    second_h, second_tok = _split_start(second, "gather4", "gather_second_start", after=first_tok)
    later_h, later_tok = _split_start(later, "gather4", "gather_later_start", after=second_tok)
    (g_ewi,) = _gathered(first_h, "gather_first_wait", later_tok)
    ewi = g_ewi.reshape(EVEN_IN, d)
    w_even_in = jnp.concatenate([ewi[:QKV], ewi[QKV + FOX_HEADS:],
                                 jnp.pad(ewi[QKV:QKV + FOX_HEADS], ((0, LANES - FOX_HEADS), (0, 0)))], axis=0)
    chunk_id = jnp.arange(GMLP_BLOCK) // CHUNK
    gmask = chunk_id[None, :] <= chunk_id[:, None]
    w_spatial = jnp.where(gmask[None], odd_w_s[0], 0.0).astype(BF16)
    bs_col = odd_b_s[0].T
    b_f_col = even_b_f.reshape(FOX_HEADS, 1)
    ln = lambda p, l: p[l:l + 1]

    qkv, bch, fl, x2d_b = _proj(x2d, w_even_in, [(0, QKV, BF16), (QKV, QKV + BCH, F32), (QKV + BCH, EVEN_IN_PAD, F32)], "even_proj")
    fl3 = fl[:, :FOX_HEADS].T.reshape(FOX_HEADS, t // LANES, LANES).transpose(1, 0, 2)
    c3 = _fgate_fwd(fl3, b_f_col)
    c_rows = c3.transpose(1, 0, 2).reshape(FOX_HEADS, t)
    head_lanes = lambda rows: jnp.pad(rows.T, ((0, 0), (0, LANES - FOX_HEADS)))
    qp, kp, vp, kt, vt = _attn_pack(qkv, head_lanes(c_rows))
    attn, lse = _attn_fwd(qp, kp, vt)
    g_ewo, g_small = _gathered(second_h, "gather_second_wait", attn)
    w_even_out = g_ewo.reshape(d, d)
    v_ln_g = g_small[:, 0:2].reshape(1, d)
    v_ln_b = g_small[:, 2:4].reshape(1, d)
    conv_w = g_small[:, 4:7].transpose(1, 0, 2).reshape(CONV_K, CONV_WIDTH)
    conv, x1_b, xh1, rs1 = _even_out(attn, bch, conv_w, w_even_out, x2d, ln(mix_ln_g, 0), ln(mix_ln_b, 0))
    w_odd_in, g_owo, w_fi0, w_fi1, g_fo0, g_fo1 = _gathered(later_h, "gather_later_wait", x1_b)
    w_odd_out = g_owo.reshape(d, d)
    w_ffn_in = [w_fi0, w_fi1]
    w_ffn_out = [g_fo0.reshape(FFN_HIDDEN, d), g_fo1.reshape(FFN_HIDDEN, d)]
    x2_b, ffn0 = _ffn_fwd((xh1, ln(mix_ln_g, 0), ln(mix_ln_b, 0)), x1_b, w_ffn_in[0], w_ffn_out[0],
                          ln(ffn_ln_g, 0), ln(ffn_ln_b, 0), 0)

    sv_odd, rs_odd, gated, x3_b, xh3, rs3 = _gmlp_fwd(
        x2_b, w_odd_in, v_ln_g, v_ln_b, w_spatial, bs_col, w_odd_out, (ffn0[3], ln(ffn_ln_g, 0), ln(ffn_ln_b, 0)),
        ln(mix_ln_g, 1), ln(mix_ln_b, 1))
    (sq, dz4, d_fg1, d_fb1), ffn1 = _ffn_fwd((xh3, ln(mix_ln_g, 1), ln(mix_ln_b, 1)), x3_b, w_ffn_in[1], w_ffn_out[1],
                                             ln(ffn_ln_g, 1), ln(ffn_ln_b, 1), 1, target=target)

    loss = lax.psum(0.5 / d * jnp.sum(sq), ("x", "y", "c"))
    (dz3, d_mg1, d_mb1), gi_f1, go_f1 = _ffn_bwd(dz4, ffn1, w_ffn_in[1], w_ffn_out[1], (xh3, rs3, ln(mix_ln_g, 1)), 1)

    go_odd = _mm_tn(gated, dz3, "odd_dw_out", tn=d).reshape(N_CHIPS, 1, d // N_CHIPS, d)
    da_odd, dws, dbs_col, d_vg, d_vb, dz2, d_fg0, d_fb0 = _gmlp_bwd(
        dz3, w_odd_out, sv_odd, rs_odd, v_ln_g, v_ln_b, w_spatial, bs_col, w_odd_in,
        (ffn0[3], ffn0[4], ln(ffn_ln_g, 0)))
    gi_odd = _mm_tn(x2_b, da_odd, "odd_dw_in", tn=d // 2, stack_cols=True)[:, None]
    (dz1, d_mg0, d_mb0), gi_f0, go_f0 = _ffn_bwd(dz2, ffn0, w_ffn_in[0], w_ffn_out[0], (xh1, rs1, ln(mix_ln_g, 0)), 0)

    sent_early = [gi_odd, go_odd, gi_f0[:, None], gi_f1[:, None], go_f0[:, None], go_f1[:, None]]
    early_h, early_tok = _split_start(sent_early, "scatter4", "scatter_early_start")
    qb, dob, dconv = _attn_bwd_prep(dz1, w_even_out, attn, qp, head_lanes(lse.reshape(FOX_HEADS, t)), early_tok)
    go_even = _mm_tn_pair(attn, conv, dz1, "even_dw_out").reshape(N_CHIPS, 1, d // N_CHIPS, d)
    dbch, dconv_w8 = _conv_bwd(bch, dconv, conv_w)
    dqkv, dc_col = _attn_unpack(*_attn_bwd(qb, kp, vp, dob, kt))
    dc3 = dc_col.T.reshape(FOX_HEADS, t // LANES, LANES).transpose(1, 0, 2)
    dfl3, d_bf = _fgate_bwd(dc3, fl3, b_f_col)
    dfl = jnp.concatenate([dfl3.transpose(1, 0, 2).reshape(FOX_HEADS, t).T.astype(BF16),
                           jnp.zeros((t, LANES - FOX_HEADS), BF16)], axis=1)

    dws_masked = jnp.where(gmask[None], dws, 0.0)
    rep_names = ["odd_w_s", "odd_b_s", "mix_ln_g", "mix_ln_b", "ffn_ln_g", "ffn_ln_b", "even_b_f"]
    rep_grads = [dws_masked, dbs_col.T, jnp.concatenate([d_mg0, d_mg1]), jnp.concatenate([d_mb0, d_mb1]),
                 jnp.concatenate([d_fg0, d_fg1]), jnp.concatenate([d_fb0, d_fb1]), d_bf.reshape(1, FOX_HEADS)]
    rep_w = [(odd_w_s, m_odd_w_s, v_odd_w_s), (odd_b_s, m_odd_b_s, v_odd_b_s), (mix_ln_g, m_mix_ln_g, v_mix_ln_g),
             (mix_ln_b, m_mix_ln_b, v_mix_ln_b), (ffn_ln_g, m_ffn_ln_g, v_ffn_ln_g), (ffn_ln_b, m_ffn_ln_b, v_ffn_ln_b),
             (even_b_f, m_even_b_f, v_even_b_f)]
    rep_rows = [_to_rows(gr) for gr in rep_grads]
    n_rep = sum(r.shape[0] for r in rep_rows)
    pad_rep = (-n_rep) % SUBLANES
    dconv_w = dconv_w8[:CONV_K].reshape(CONV_K, N_CHIPS, LANES).transpose(1, 0, 2).reshape(N_CHIPS * CONV_K, LANES)
    packed = jnp.concatenate(rep_rows + [jnp.zeros((pad_rep, LANES), F32), d_vg.reshape(SUBLANES, LANES),
                                         d_vb.reshape(SUBLANES, LANES), dconv_w, jnp.zeros((4, LANES), F32)], axis=0)
    small_h, small_tok = _split_start([packed], "gather8", "gather_small_start")

    swap_h, swap_tok = _split_start(_scattered(early_h, "scatter_early_wait", small_tok), "swap2", "swap_early_start")
    dw_qkv = _mm_tn(dqkv, x2d_b, "even_dw_qkv", tn=d, tk=QKV // 2, after=swap_tok)
    dw_bch = _mm_tn(dbch, x2d_b, "even_dw_bch", tn=d, tk=BCH // 2)
    dw_f = _mm_tn(dfl, x2d_b, "even_dw_f", tn=d)
    gi_even = jnp.concatenate([dw_qkv, dw_f[:FOX_HEADS], dw_bch], axis=0).reshape(N_CHIPS, 1, -1, LANES)
    sent_late = [gi_even, go_even]
    late_h, late_tok = _split_start(sent_late, "scatter4", "scatter_late_start")
    grad_x = _mm_back([(dqkv, 0, QKV), (dbch, QKV, QKV + BCH), (dfl, QKV + BCH, EVEN_IN_PAD)], w_even_in, dz1,
                      late_tok, "even_dx")
    mine, theirs = _split_wait(swap_h, "swap_early_wait", grad_x)
    res = {}
    res["odd_w_in"] = _adamw([mine[0], theirs[0]], odd_w_in, m_odd_w_in, v_odd_w_in, "adamw_odd_w_in")
    res["odd_w_out"] = _adamw([mine[1], theirs[1]], odd_w_out, m_odd_w_out, v_odd_w_out, "adamw_odd_w_out")
    for nm, at, (w, m, v) in (("ffn_w_in", 2, (ffn_w_in, m_ffn_w_in, v_ffn_w_in)),
                              ("ffn_w_out", 4, (ffn_w_out, m_ffn_w_out, v_ffn_w_out))):
        upper = _adamw([mine[at + 1], theirs[at + 1]], w, m, v, f"adamw_{nm}_1", layer=1)
        res[nm] = _adamw([mine[at], theirs[at]], w, m, v, f"adamw_{nm}_0", layer=0, into=upper)
    mine_late = _scattered(late_h, "scatter_late_wait", res["ffn_w_out"][0])
    theirs_late = _exchange(mine_late, "swap2", "swap_late")
    rows = lambda a: jnp.swapaxes(a, 1, 2).reshape(1, -1, LANES)
    back = lambda a: jnp.swapaxes(a.reshape(1, EVEN_IN // N_CHIPS, d), 1, 2)
    res["even_w_in"] = [back(o) for o in _adamw([mine_late[0], theirs_late[0]], rows(even_w_in), rows(m_even_w_in),
                                                rows(v_even_w_in), "adamw_even_w_in")]
    res["even_w_out"] = _adamw([mine_late[1], theirs_late[1]], even_w_out, m_even_w_out, v_even_w_out,
                               "adamw_even_w_out")
    (packed,), (gathered,) = _split_wait(small_h, "gather_small_wait", theirs_late[0])
    gathered = lax.dynamic_update_index_in_dim(gathered, packed, 4 * lax.axis_index("x") + 2 * lax.axis_index("y")
                                               + lax.axis_index("c"), 0)

    base = n_rep + pad_rep
    own_rows = jnp.concatenate([
        lax.dynamic_slice_in_dim(gathered, base + 2 * chip, 2, axis=1),
        lax.dynamic_slice_in_dim(gathered, base + SUBLANES + 2 * chip, 2, axis=1),
        lax.dynamic_slice_in_dim(gathered, base + 2 * SUBLANES + CONV_K * chip, CONV_K, axis=1),
        jnp.zeros((N_DEV, 1, LANES), F32)], axis=1)
    small_parts = jnp.concatenate([gathered[:, :base], own_rows], axis=1)[:, None]

    def pack_small(get):
        rows = [_to_rows(get(tw)) for tw in rep_w] + [jnp.zeros((pad_rep, LANES), F32)]
        rows += [get(sh).reshape(-1, LANES) for sh in ((odd_v_ln_g, m_odd_v_ln_g, v_odd_v_ln_g),
                                                       (odd_v_ln_b, m_odd_v_ln_b, v_odd_v_ln_b),
                                                       (even_conv_w, m_even_conv_w, v_even_conv_w))]
        return jnp.concatenate(rows + [jnp.zeros((1, LANES), F32)], axis=0)[None]

    small_out = _adamw([small_parts], pack_small(lambda tw: tw[0]), pack_small(lambda tw: tw[1]),
                       pack_small(lambda tw: tw[2]), "adamw_small")

    def unpack_small(rows3):
        rows = rows3[0]
        out, off = {}, 0
        for nm, (w, _, _), r in zip(rep_names, rep_w, rep_rows):
            out[nm] = rows[off:off + r.shape[0]].reshape(-1)[:w.size].reshape(w.shape)
            off += r.shape[0]
        off += pad_rep
        out["odd_v_ln_g"] = rows[off:off + 2].reshape(odd_v_ln_g.shape)
        out["odd_v_ln_b"] = rows[off + 2:off + 4].reshape(odd_v_ln_b.shape)
        out["even_conv_w"] = rows[off + 4:off + 4 + CONV_K].reshape(even_conv_w.shape)
        return out

    small = [unpack_small(o) for o in small_out]
    order = ["even_w_in", "even_b_f", "even_conv_w", "even_w_out", "odd_w_in", "odd_v_ln_g", "odd_v_ln_b", "odd_w_s",
             "odd_b_s", "odd_w_out", "mix_ln_g", "mix_ln_b", "ffn_w_in", "ffn_w_out", "ffn_ln_g", "ffn_ln_b"]
    outs = [loss, grad_x[None]]
    for kind in range(4):
        for nm in order:
            outs.append(res[nm][kind] if nm in res else small[kind][nm])
    return tuple(outs)
```
